```python
import math
import jax, jax.numpy as jnp
from jax import lax
import numpy as np

D_MODEL = 1024
BATCH = 8
SEQ = 4096
DEPTH = 2

N_A_LAYERS = DEPTH // 2
N_B_LAYERS = DEPTH - N_A_LAYERS
SSM_EXPAND = 2
D_INNER = SSM_EXPAND * D_MODEL
SSM_HEAD_DIM = 64
SSM_HEADS = D_INNER // SSM_HEAD_DIM
SSM_GROUPS = 4
SSM_STATE = 128
SSM_CONV = 4
SSM_CHUNK = 128
GN = SSM_GROUPS * SSM_STATE
CONV_DIM = D_INNER + 2 * GN
IN_PROJ_DIM = D_INNER + CONV_DIM + SSM_HEADS
SB_HEADS = 16
SB_HEAD_DIM = D_MODEL // SB_HEADS
SB_BLOCK = 128
D_FF = 2816
FFN_CONV = 3
EPS = 1e-6

kernel_name = 'yoco_mamba2_stickbreaking_convffn'


def rmsnorm(x, w):
    xf = x.astype(jnp.float32)
    y = xf * lax.rsqrt(jnp.mean(xf * xf, axis=-1, keepdims=True) + EPS)
    return (y * w.astype(jnp.float32)).astype(x.dtype)


def causal_dwconv(x, w, b):
    k_w, s = w.shape[0], x.shape[1]
    xp = jnp.pad(x, ((0, 0), (k_w - 1, 0), (0, 0)))
    out = b
    for j in range(k_w):
        out = out + xp[:, j:j + s, :] * w[j]
    return out


def segsum(a):
    t = a.shape[-1]
    cs = jnp.cumsum(a, axis=-1)
    diff = cs[..., :, None] - cs[..., None, :]
    mask = jnp.tril(jnp.ones((t, t), dtype=bool))
    return jnp.where(mask, diff, -jnp.inf)


def ssd_chunked(xs, dt, a, bm, cm):
    bsz, s, h, p = xs.shape
    g, n = bm.shape[2], bm.shape[3]
    r = h // g
    c, l = s // SSM_CHUNK, SSM_CHUNK
    xd = (xs * dt[..., None]).reshape(bsz, c, l, g, r, p)
    ad = (dt * a).reshape(bsz, c, l, g, r).transpose(0, 3, 4, 1, 2)
    bc = bm.reshape(bsz, c, l, g, n)
    cc = cm.reshape(bsz, c, l, g, n)
    a_cs = jnp.cumsum(ad, axis=-1)
    lmat = jnp.exp(segsum(ad))
    cb = jnp.einsum('bclgn,bcsgn->bcgls', cc, bc)
    y_diag = jnp.einsum('bcgls,bgrcls,bcsgrp->bclgrp', cb, lmat, xd)
    decay_states = jnp.exp(a_cs[..., -1:] - a_cs)
    states = jnp.einsum('bclgn,bgrcl,bclgrp->bcgrpn', bc, decay_states, xd)
    chunk_decay = jnp.exp(a_cs[..., -1])

    def step(carry, inp):
        st, dc = inp
        return carry * dc[..., None, None] + st, carry

    init = jnp.zeros((bsz, g, r, p, n), dtype=states.dtype)
    _, prev = lax.scan(step, init, (jnp.moveaxis(states, 1, 0), jnp.moveaxis(chunk_decay, -1, 0)))
    prev = jnp.moveaxis(prev, 0, 1)
    y_off = jnp.einsum('bclgn,bcgrpn,bgrcl->bclgrp', cc, prev, jnp.exp(a_cs))
    return (y_diag + y_off).reshape(bsz, s, h, p)


def mamba2_mixer(u, w_in, conv_w, conv_b, dt_bias, a_log, d_skip, gate_norm_w, w_out):
    bsz, s, _ = u.shape
    f32 = jnp.float32
    zxbcdt = u @ w_in
    z = zxbcdt[..., :D_INNER]
    xbc = zxbcdt[..., D_INNER:D_INNER + CONV_DIM]
    dt = zxbcdt[..., D_INNER + CONV_DIM:]
    xbc = jax.nn.silu(causal_dwconv(xbc, conv_w, conv_b))
    xs = xbc[..., :D_INNER].reshape(bsz, s, SSM_HEADS, SSM_HEAD_DIM).astype(f32)
    bm = xbc[..., D_INNER:D_INNER + GN].reshape(bsz, s, SSM_GROUPS, SSM_STATE).astype(f32)
    cm = xbc[..., D_INNER + GN:].reshape(bsz, s, SSM_GROUPS, SSM_STATE).astype(f32)
    dt = jax.nn.softplus(dt.astype(f32) + dt_bias.astype(f32))
    a = -jnp.exp(a_log.astype(f32))
    y = ssd_chunked(xs, dt, a, bm, cm)
    y = y + d_skip.astype(f32)[:, None] * xs
    y = y.reshape(bsz, s, D_INNER) * jax.nn.silu(z.astype(f32))
    gsz = D_INNER // SSM_GROUPS
    y = rmsnorm(y.reshape(bsz, s, SSM_GROUPS, gsz), gate_norm_w.reshape(SSM_GROUPS, gsz))
    return y.reshape(bsz, s, D_INNER).astype(u.dtype) @ w_out


def stick_breaking_attention(q, k, v):
    bsz, s, h, d = q.shape
    nblk = s // SB_BLOCK
    scale = 1.0 / math.sqrt(d)
    qb = q.reshape(bsz, nblk, SB_BLOCK, h, d).transpose(1, 0, 3, 2, 4)
    kt = k.transpose(0, 2, 1, 3).astype(jnp.float32)
    vt = v.transpose(0, 2, 1, 3).astype(jnp.float32)
    starts = jnp.arange(nblk, dtype=jnp.int32) * SB_BLOCK
    kpos = jnp.arange(s, dtype=jnp.int32)

    def block(args):
        qblk, i0 = args
        qpos = i0 + jnp.arange(SB_BLOCK, dtype=jnp.int32)
        mask = kpos[None, :] < qpos[:, None]
        zl = jnp.einsum('bhqd,bhkd->bhqk', qblk.astype(jnp.float32), kt) * scale
        log_1m = jnp.where(mask, -jax.nn.softplus(zl), 0.0)
        later = lax.cumsum(log_1m, axis=3, reverse=True) - log_1m
        att = jnp.where(mask, jnp.exp(jax.nn.log_sigmoid(zl) + later), 0.0)
        return jnp.einsum('bhqk,bhkd->bhqd', att, vt)

    o = lax.map(block, (qb, starts))
    return o.transpose(1, 0, 3, 2, 4).reshape(bsz, s, h * d).astype(q.dtype)


def conv_ffn(u, w_up, conv_w, conv_b, w_down):
    hid = causal_dwconv(u @ w_up, conv_w, conv_b)
    gate, val = hid[..., :D_FF], hid[..., D_FF:]
    return (jax.nn.silu(gate) * val) @ w_down


def _fwd_setup_inputs(seed: int = 0) -> dict:
    key = jax.random.key(seed)
    ks = jax.random.split(key, 24)
    f32 = jnp.float32

    def nrm(k, shape, scale):
        return jax.random.normal(k, shape, f32) * scale

    def gain(k, shape):
        return 1.0 + 0.02 * jax.random.normal(k, shape, f32)

    na, nb = N_A_LAYERS, N_B_LAYERS
    dt0 = jnp.exp(jax.random.uniform(ks[5], (na, SSM_HEADS), f32, math.log(1e-3), math.log(1e-1)))
    dt_bias = dt0 + jnp.log(-jnp.expm1(-dt0))
    a_log = jnp.log(jax.random.uniform(ks[6], (na, SSM_HEADS), f32, 1.0, 16.0))
    return {
        'x': jax.random.normal(ks[0], (BATCH, SEQ, D_MODEL), f32),
        'ssm_norm_w': gain(ks[1], (na, D_MODEL)),
        'ssm_in_w': nrm(ks[2], (na, D_MODEL, IN_PROJ_DIM), D_MODEL ** -0.5),
        'ssm_conv_w': nrm(ks[3], (na, SSM_CONV, CONV_DIM), SSM_CONV ** -0.5),
        'ssm_conv_b': nrm(ks[4], (na, CONV_DIM), 0.02),
        'ssm_dt_bias': dt_bias,
        'ssm_a_log': a_log,
        'ssm_d': 1.0 + 0.1 * jax.random.normal(ks[7], (na, SSM_HEADS), f32),
        'ssm_gate_norm_w': gain(ks[8], (na, D_INNER)),
        'ssm_out_w': nrm(ks[9], (na, D_INNER, D_MODEL), D_INNER ** -0.5),
        'kv_norm_w': gain(ks[10], (D_MODEL,)),
        'w_k': nrm(ks[11], (D_MODEL, SB_HEADS * SB_HEAD_DIM), D_MODEL ** -0.5),
        'w_v': nrm(ks[12], (D_MODEL, SB_HEADS * SB_HEAD_DIM), D_MODEL ** -0.5),
        'attn_norm_w': gain(ks[13], (nb, D_MODEL)),
        'w_q': nrm(ks[14], (nb, D_MODEL, SB_HEADS * SB_HEAD_DIM), D_MODEL ** -0.5),
        'w_o': nrm(ks[15], (nb, SB_HEADS * SB_HEAD_DIM, D_MODEL), D_MODEL ** -0.5),
        'ffn_norm_w': gain(ks[16], (DEPTH, D_MODEL)),
        'ffn_up_w': nrm(ks[17], (DEPTH, D_MODEL, 2 * D_FF), D_MODEL ** -0.5),
        'ffn_conv_w': nrm(ks[18], (DEPTH, FFN_CONV, 2 * D_FF), FFN_CONV ** -0.5),
        'ffn_conv_b': nrm(ks[19], (DEPTH, 2 * D_FF), 0.02),
        'ffn_down_w': nrm(ks[20], (DEPTH, D_FF, D_MODEL), D_FF ** -0.5),
        'final_norm_w': gain(ks[21], (D_MODEL,)),
    }


def _fwd_reference(x, ssm_norm_w, ssm_in_w, ssm_conv_w, ssm_conv_b, ssm_dt_bias, ssm_a_log, ssm_d,
              ssm_gate_norm_w, ssm_out_w, kv_norm_w, w_k, w_v, attn_norm_w, w_q, w_o,
              ffn_norm_w, ffn_up_w, ffn_conv_w, ffn_conv_b, ffn_down_w, final_norm_w):
    bsz, s, _ = x.shape
    h = x
    k_shared = None
    v_shared = None
    for layer in range(DEPTH):
        if layer < N_A_LAYERS:
            i = layer
            h = h + mamba2_mixer(rmsnorm(h, ssm_norm_w[i]), ssm_in_w[i], ssm_conv_w[i], ssm_conv_b[i],
                                 ssm_dt_bias[i], ssm_a_log[i], ssm_d[i], ssm_gate_norm_w[i], ssm_out_w[i])
        else:
            i = layer - N_A_LAYERS
            if k_shared is None:
                hk = rmsnorm(h, kv_norm_w)
                k_shared = (hk @ w_k).reshape(bsz, s, SB_HEADS, SB_HEAD_DIM)
                v_shared = (hk @ w_v).reshape(bsz, s, SB_HEADS, SB_HEAD_DIM)
            q = (rmsnorm(h, attn_norm_w[i]) @ w_q[i]).reshape(bsz, s, SB_HEADS, SB_HEAD_DIM)
            h = h + stick_breaking_attention(q, k_shared, v_shared) @ w_o[i]
        h = h + conv_ffn(rmsnorm(h, ffn_norm_w[layer]), ffn_up_w[layer], ffn_conv_w[layer],
                         ffn_conv_b[layer], ffn_down_w[layer])
    return rmsnorm(h, final_norm_w)


import jax as _jax
import jax.numpy as _jnp

TWIN_FORMAT = 'train_step'
FWD_PARAMS = ['x', 'ssm_norm_w', 'ssm_in_w', 'ssm_conv_w', 'ssm_conv_b', 'ssm_dt_bias', 'ssm_a_log', 'ssm_d', 'ssm_gate_norm_w', 'ssm_out_w', 'kv_norm_w', 'w_k', 'w_v', 'attn_norm_w', 'w_q', 'w_o', 'ffn_norm_w', 'ffn_up_w', 'ffn_conv_w', 'ffn_conv_b', 'ffn_down_w', 'final_norm_w']
TWIN_WEIGHTS = ['ssm_norm_w', 'ssm_in_w', 'ssm_conv_w', 'ssm_conv_b', 'ssm_dt_bias', 'ssm_a_log', 'ssm_d', 'ssm_gate_norm_w', 'ssm_out_w', 'kv_norm_w', 'w_k', 'w_v', 'attn_norm_w', 'w_q', 'w_o', 'ffn_norm_w', 'ffn_up_w', 'ffn_conv_w', 'ffn_conv_b', 'ffn_down_w', 'final_norm_w']
TWIN_DIFF_INPUT = 'x'
TWIN_INPUTS = ['x', 'ssm_norm_w', 'ssm_in_w', 'ssm_conv_w', 'ssm_conv_b', 'ssm_dt_bias', 'ssm_a_log', 'ssm_d', 'ssm_gate_norm_w', 'ssm_out_w', 'kv_norm_w', 'w_k', 'w_v', 'attn_norm_w', 'w_q', 'w_o', 'ffn_norm_w', 'ffn_up_w', 'ffn_conv_w', 'ffn_conv_b', 'ffn_down_w', 'final_norm_w', 'loss_target', 'm_ssm_norm_w', 'm_ssm_in_w', 'm_ssm_conv_w', 'm_ssm_conv_b', 'm_ssm_dt_bias', 'm_ssm_a_log', 'm_ssm_d', 'm_ssm_gate_norm_w', 'm_ssm_out_w', 'm_kv_norm_w', 'm_w_k', 'm_w_v', 'm_attn_norm_w', 'm_w_q', 'm_w_o', 'm_ffn_norm_w', 'm_ffn_up_w', 'm_ffn_conv_w', 'm_ffn_conv_b', 'm_ffn_down_w', 'm_final_norm_w', 'v_ssm_norm_w', 'v_ssm_in_w', 'v_ssm_conv_w', 'v_ssm_conv_b', 'v_ssm_dt_bias', 'v_ssm_a_log', 'v_ssm_d', 'v_ssm_gate_norm_w', 'v_ssm_out_w', 'v_kv_norm_w', 'v_w_k', 'v_w_v', 'v_attn_norm_w', 'v_w_q', 'v_w_o', 'v_ffn_norm_w', 'v_ffn_up_w', 'v_ffn_conv_w', 'v_ffn_conv_b', 'v_ffn_down_w', 'v_final_norm_w']
TWIN_OUTPUTS = ['loss', 'grad_x', 'grad_ssm_norm_w', 'grad_ssm_in_w', 'grad_ssm_conv_w', 'grad_ssm_conv_b', 'grad_ssm_dt_bias', 'grad_ssm_a_log', 'grad_ssm_d', 'grad_ssm_gate_norm_w', 'grad_ssm_out_w', 'grad_kv_norm_w', 'grad_w_k', 'grad_w_v', 'grad_attn_norm_w', 'grad_w_q', 'grad_w_o', 'grad_ffn_norm_w', 'grad_ffn_up_w', 'grad_ffn_conv_w', 'grad_ffn_conv_b', 'grad_ffn_down_w', 'grad_final_norm_w', 'delta_ssm_norm_w', 'delta_ssm_in_w', 'delta_ssm_conv_w', 'delta_ssm_conv_b', 'delta_ssm_dt_bias', 'delta_ssm_a_log', 'delta_ssm_d', 'delta_ssm_gate_norm_w', 'delta_ssm_out_w', 'delta_kv_norm_w', 'delta_w_k', 'delta_w_v', 'delta_attn_norm_w', 'delta_w_q', 'delta_w_o', 'delta_ffn_norm_w', 'delta_ffn_up_w', 'delta_ffn_conv_w', 'delta_ffn_conv_b', 'delta_ffn_down_w', 'delta_final_norm_w', 'new_m_ssm_norm_w', 'new_m_ssm_in_w', 'new_m_ssm_conv_w', 'new_m_ssm_conv_b', 'new_m_ssm_dt_bias', 'new_m_ssm_a_log', 'new_m_ssm_d', 'new_m_ssm_gate_norm_w', 'new_m_ssm_out_w', 'new_m_kv_norm_w', 'new_m_w_k', 'new_m_w_v', 'new_m_attn_norm_w', 'new_m_w_q', 'new_m_w_o', 'new_m_ffn_norm_w', 'new_m_ffn_up_w', 'new_m_ffn_conv_w', 'new_m_ffn_conv_b', 'new_m_ffn_down_w', 'new_m_final_norm_w', 'new_v_ssm_norm_w', 'new_v_ssm_in_w', 'new_v_ssm_conv_w', 'new_v_ssm_conv_b', 'new_v_ssm_dt_bias', 'new_v_ssm_a_log', 'new_v_ssm_d', 'new_v_ssm_gate_norm_w', 'new_v_ssm_out_w', 'new_v_kv_norm_w', 'new_v_w_k', 'new_v_w_v', 'new_v_attn_norm_w', 'new_v_w_q', 'new_v_w_o', 'new_v_ffn_norm_w', 'new_v_ffn_up_w', 'new_v_ffn_conv_w', 'new_v_ffn_conv_b', 'new_v_ffn_down_w', 'new_v_final_norm_w']
TWIN_LEAF_KINDS = {'loss': 'loss', 'grad_x': 'grad_x', 'grad_ssm_norm_w': 'grad_w', 'grad_ssm_in_w': 'grad_w', 'grad_ssm_conv_w': 'grad_w', 'grad_ssm_conv_b': 'grad_w', 'grad_ssm_dt_bias': 'grad_w', 'grad_ssm_a_log': 'grad_w', 'grad_ssm_d': 'grad_w', 'grad_ssm_gate_norm_w': 'grad_w', 'grad_ssm_out_w': 'grad_w', 'grad_kv_norm_w': 'grad_w', 'grad_w_k': 'grad_w', 'grad_w_v': 'grad_w', 'grad_attn_norm_w': 'grad_w', 'grad_w_q': 'grad_w', 'grad_w_o': 'grad_w', 'grad_ffn_norm_w': 'grad_w', 'grad_ffn_up_w': 'grad_w', 'grad_ffn_conv_w': 'grad_w', 'grad_ffn_conv_b': 'grad_w', 'grad_ffn_down_w': 'grad_w', 'grad_final_norm_w': 'grad_w', 'delta_ssm_norm_w': 'delta_w', 'delta_ssm_in_w': 'delta_w', 'delta_ssm_conv_w': 'delta_w', 'delta_ssm_conv_b': 'delta_w', 'delta_ssm_dt_bias': 'delta_w', 'delta_ssm_a_log': 'delta_w', 'delta_ssm_d': 'delta_w', 'delta_ssm_gate_norm_w': 'delta_w', 'delta_ssm_out_w': 'delta_w', 'delta_kv_norm_w': 'delta_w', 'delta_w_k': 'delta_w', 'delta_w_v': 'delta_w', 'delta_attn_norm_w': 'delta_w', 'delta_w_q': 'delta_w', 'delta_w_o': 'delta_w', 'delta_ffn_norm_w': 'delta_w', 'delta_ffn_up_w': 'delta_w', 'delta_ffn_conv_w': 'delta_w', 'delta_ffn_conv_b': 'delta_w', 'delta_ffn_down_w': 'delta_w', 'delta_final_norm_w': 'delta_w', 'new_m_ssm_norm_w': 'new_m', 'new_m_ssm_in_w': 'new_m', 'new_m_ssm_conv_w': 'new_m', 'new_m_ssm_conv_b': 'new_m', 'new_m_ssm_dt_bias': 'new_m', 'new_m_ssm_a_log': 'new_m', 'new_m_ssm_d': 'new_m', 'new_m_ssm_gate_norm_w': 'new_m', 'new_m_ssm_out_w': 'new_m', 'new_m_kv_norm_w': 'new_m', 'new_m_w_k': 'new_m', 'new_m_w_v': 'new_m', 'new_m_attn_norm_w': 'new_m', 'new_m_w_q': 'new_m', 'new_m_w_o': 'new_m', 'new_m_ffn_norm_w': 'new_m', 'new_m_ffn_up_w': 'new_m', 'new_m_ffn_conv_w': 'new_m', 'new_m_ffn_conv_b': 'new_m', 'new_m_ffn_down_w': 'new_m', 'new_m_final_norm_w': 'new_m', 'new_v_ssm_norm_w': 'new_v', 'new_v_ssm_in_w': 'new_v', 'new_v_ssm_conv_w': 'new_v', 'new_v_ssm_conv_b': 'new_v', 'new_v_ssm_dt_bias': 'new_v', 'new_v_ssm_a_log': 'new_v', 'new_v_ssm_d': 'new_v', 'new_v_ssm_gate_norm_w': 'new_v', 'new_v_ssm_out_w': 'new_v', 'new_v_kv_norm_w': 'new_v', 'new_v_w_k': 'new_v', 'new_v_w_v': 'new_v', 'new_v_attn_norm_w': 'new_v', 'new_v_w_q': 'new_v', 'new_v_w_o': 'new_v', 'new_v_ffn_norm_w': 'new_v', 'new_v_ffn_up_w': 'new_v', 'new_v_ffn_conv_w': 'new_v', 'new_v_ffn_conv_b': 'new_v', 'new_v_ffn_down_w': 'new_v', 'new_v_final_norm_w': 'new_v'}


def _forward(args):
    return _fwd_reference(*[args[k] for k in FWD_PARAMS])


def _output_shape():
    out = _jax.eval_shape(lambda: _forward(_fwd_setup_inputs(0)))
    return out.shape, out.dtype

N_MICROBATCH = 1
ADAM_LR = 0.001
ADAM_B1 = 0.9
ADAM_B2 = 0.999
ADAM_EPS = 1e-08
ADAM_WD = 0.01
ADAM_STEP = 10
PER_EXAMPLE_BATCH_AXIS = {'x': 0, 'loss_target': 0}
SHARED_INPUTS = []
_WEIGHT_DTYPES = {'ssm_norm_w': _jnp.float32, 'ssm_in_w': _jnp.float32, 'ssm_conv_w': _jnp.float32, 'ssm_conv_b': _jnp.float32, 'ssm_dt_bias': _jnp.float32, 'ssm_a_log': _jnp.float32, 'ssm_d': _jnp.float32, 'ssm_gate_norm_w': _jnp.float32, 'ssm_out_w': _jnp.float32, 'kv_norm_w': _jnp.float32, 'w_k': _jnp.float32, 'w_v': _jnp.float32, 'attn_norm_w': _jnp.float32, 'w_q': _jnp.float32, 'w_o': _jnp.float32, 'ffn_norm_w': _jnp.float32, 'ffn_up_w': _jnp.float32, 'ffn_conv_w': _jnp.float32, 'ffn_conv_b': _jnp.float32, 'ffn_down_w': _jnp.float32, 'final_norm_w': _jnp.float32}
MOMENT_SCALE = {'ssm_norm_w': 2.248148e-01, 'ssm_in_w': 1.000202e-01, 'ssm_conv_w': 9.206803e-02, 'ssm_conv_b': 1.225318e-01, 'ssm_dt_bias': 2.710052e-01, 'ssm_a_log': 4.447519e-01, 'ssm_d': 4.980662e-01, 'ssm_gate_norm_w': 1.038009e-01, 'ssm_out_w': 1.482609e-01, 'kv_norm_w': 8.529283e-02, 'w_k': 3.346443e-02, 'w_v': 7.376899e-02, 'attn_norm_w': 3.509310e-02, 'w_q': 3.344200e-02, 'w_o': 7.373714e-02, 'ffn_norm_w': 9.876697e-02, 'ffn_up_w': 4.171500e-02, 'ffn_conv_w': 4.206791e-02, 'ffn_conv_b': 4.140397e-02, 'ffn_down_w': 6.839148e-02, 'final_norm_w': 3.204760e+01}


def _to_microbatches(a, axis):
    t = _jnp.moveaxis(a, axis, 0)
    t = t.reshape((N_MICROBATCH, t.shape[0] // N_MICROBATCH) + t.shape[1:])
    return _jnp.moveaxis(t, 1, axis + 1)


def setup_inputs(seed: int = 0) -> dict:
    inp = _fwd_setup_inputs(seed)
    key = _jax.random.fold_in(_jax.random.key(seed), 7919)
    shape, _ = _output_shape()
    out = dict(inp)
    out["loss_target"] = _jax.random.normal(_jax.random.fold_in(key, 0), shape, _jnp.float32)
    for i, name in enumerate(TWIN_WEIGHTS):
        w = inp[name].astype(_jnp.float32)
        if MOMENT_SCALE is None:
            s = _jnp.sqrt(_jnp.mean(_jnp.square(w)) + 1e-30)
        else:
            s = MOMENT_SCALE[name]
        km, kv = _jax.random.split(_jax.random.fold_in(key, i + 1))
        out[name] = w
        out["m_" + name] = s * _jax.random.normal(km, w.shape, _jnp.float32)
        out["v_" + name] = (s * s) * _jax.random.uniform(kv, w.shape, _jnp.float32, 0.5, 1.5)
    if N_MICROBATCH > 1:
        for name, axis in PER_EXAMPLE_BATCH_AXIS.items():
            out[name] = _to_microbatches(out[name], axis)
    return {'x': out['x'], 'ssm_norm_w': out['ssm_norm_w'], 'ssm_in_w': out['ssm_in_w'], 'ssm_conv_w': out['ssm_conv_w'], 'ssm_conv_b': out['ssm_conv_b'], 'ssm_dt_bias': out['ssm_dt_bias'], 'ssm_a_log': out['ssm_a_log'], 'ssm_d': out['ssm_d'], 'ssm_gate_norm_w': out['ssm_gate_norm_w'], 'ssm_out_w': out['ssm_out_w'], 'kv_norm_w': out['kv_norm_w'], 'w_k': out['w_k'], 'w_v': out['w_v'], 'attn_norm_w': out['attn_norm_w'], 'w_q': out['w_q'], 'w_o': out['w_o'], 'ffn_norm_w': out['ffn_norm_w'], 'ffn_up_w': out['ffn_up_w'], 'ffn_conv_w': out['ffn_conv_w'], 'ffn_conv_b': out['ffn_conv_b'], 'ffn_down_w': out['ffn_down_w'], 'final_norm_w': out['final_norm_w'], 'loss_target': out['loss_target'], 'm_ssm_norm_w': out['m_ssm_norm_w'], 'm_ssm_in_w': out['m_ssm_in_w'], 'm_ssm_conv_w': out['m_ssm_conv_w'], 'm_ssm_conv_b': out['m_ssm_conv_b'], 'm_ssm_dt_bias': out['m_ssm_dt_bias'], 'm_ssm_a_log': out['m_ssm_a_log'], 'm_ssm_d': out['m_ssm_d'], 'm_ssm_gate_norm_w': out['m_ssm_gate_norm_w'], 'm_ssm_out_w': out['m_ssm_out_w'], 'm_kv_norm_w': out['m_kv_norm_w'], 'm_w_k': out['m_w_k'], 'm_w_v': out['m_w_v'], 'm_attn_norm_w': out['m_attn_norm_w'], 'm_w_q': out['m_w_q'], 'm_w_o': out['m_w_o'], 'm_ffn_norm_w': out['m_ffn_norm_w'], 'm_ffn_up_w': out['m_ffn_up_w'], 'm_ffn_conv_w': out['m_ffn_conv_w'], 'm_ffn_conv_b': out['m_ffn_conv_b'], 'm_ffn_down_w': out['m_ffn_down_w'], 'm_final_norm_w': out['m_final_norm_w'], 'v_ssm_norm_w': out['v_ssm_norm_w'], 'v_ssm_in_w': out['v_ssm_in_w'], 'v_ssm_conv_w': out['v_ssm_conv_w'], 'v_ssm_conv_b': out['v_ssm_conv_b'], 'v_ssm_dt_bias': out['v_ssm_dt_bias'], 'v_ssm_a_log': out['v_ssm_a_log'], 'v_ssm_d': out['v_ssm_d'], 'v_ssm_gate_norm_w': out['v_ssm_gate_norm_w'], 'v_ssm_out_w': out['v_ssm_out_w'], 'v_kv_norm_w': out['v_kv_norm_w'], 'v_w_k': out['v_w_k'], 'v_w_v': out['v_w_v'], 'v_attn_norm_w': out['v_attn_norm_w'], 'v_w_q': out['v_w_q'], 'v_w_o': out['v_w_o'], 'v_ffn_norm_w': out['v_ffn_norm_w'], 'v_ffn_up_w': out['v_ffn_up_w'], 'v_ffn_conv_w': out['v_ffn_conv_w'], 'v_ffn_conv_b': out['v_ffn_conv_b'], 'v_ffn_down_w': out['v_ffn_down_w'], 'v_final_norm_w': out['v_final_norm_w']}


def _loss(weights, diff, rest, loss_target):
    with _jax.named_scope("forward"):
        args = {**rest, TWIN_DIFF_INPUT: diff, **{k: w.astype(_WEIGHT_DTYPES[k]) for k, w in weights.items()}}
        y = _forward(args)
    with _jax.named_scope("loss_head"):
        err = _jnp.square(y.astype(_jnp.float32) - loss_target)
        return 0.5 * _jnp.sum(_jnp.mean(err, axis=-1)) if err.ndim else 0.5 * err


def _adamw(w, g, m, v):
    m = ADAM_B1 * m + (1.0 - ADAM_B1) * g
    v = ADAM_B2 * v + (1.0 - ADAM_B2) * _jnp.square(g)
    m_hat = m / (1.0 - ADAM_B1 ** ADAM_STEP)
    v_hat = v / (1.0 - ADAM_B2 ** ADAM_STEP)
    delta = -ADAM_LR * (m_hat / (_jnp.sqrt(v_hat) + ADAM_EPS) + ADAM_WD * w)
    return delta, m, v


def reference(x, ssm_norm_w, ssm_in_w, ssm_conv_w, ssm_conv_b, ssm_dt_bias, ssm_a_log, ssm_d, ssm_gate_norm_w, ssm_out_w, kv_norm_w, w_k, w_v, attn_norm_w, w_q, w_o, ffn_norm_w, ffn_up_w, ffn_conv_w, ffn_conv_b, ffn_down_w, final_norm_w, loss_target, m_ssm_norm_w, m_ssm_in_w, m_ssm_conv_w, m_ssm_conv_b, m_ssm_dt_bias, m_ssm_a_log, m_ssm_d, m_ssm_gate_norm_w, m_ssm_out_w, m_kv_norm_w, m_w_k, m_w_v, m_attn_norm_w, m_w_q, m_w_o, m_ffn_norm_w, m_ffn_up_w, m_ffn_conv_w, m_ffn_conv_b, m_ffn_down_w, m_final_norm_w, v_ssm_norm_w, v_ssm_in_w, v_ssm_conv_w, v_ssm_conv_b, v_ssm_dt_bias, v_ssm_a_log, v_ssm_d, v_ssm_gate_norm_w, v_ssm_out_w, v_kv_norm_w, v_w_k, v_w_v, v_attn_norm_w, v_w_q, v_w_o, v_ffn_norm_w, v_ffn_up_w, v_ffn_conv_w, v_ffn_conv_b, v_ffn_down_w, v_final_norm_w):
    given = dict(x=x, ssm_norm_w=ssm_norm_w, ssm_in_w=ssm_in_w, ssm_conv_w=ssm_conv_w, ssm_conv_b=ssm_conv_b, ssm_dt_bias=ssm_dt_bias, ssm_a_log=ssm_a_log, ssm_d=ssm_d, ssm_gate_norm_w=ssm_gate_norm_w, ssm_out_w=ssm_out_w, kv_norm_w=kv_norm_w, w_k=w_k, w_v=w_v, attn_norm_w=attn_norm_w, w_q=w_q, w_o=w_o, ffn_norm_w=ffn_norm_w, ffn_up_w=ffn_up_w, ffn_conv_w=ffn_conv_w, ffn_conv_b=ffn_conv_b, ffn_down_w=ffn_down_w, final_norm_w=final_norm_w, loss_target=loss_target, m_ssm_norm_w=m_ssm_norm_w, m_ssm_in_w=m_ssm_in_w, m_ssm_conv_w=m_ssm_conv_w, m_ssm_conv_b=m_ssm_conv_b, m_ssm_dt_bias=m_ssm_dt_bias, m_ssm_a_log=m_ssm_a_log, m_ssm_d=m_ssm_d, m_ssm_gate_norm_w=m_ssm_gate_norm_w, m_ssm_out_w=m_ssm_out_w, m_kv_norm_w=m_kv_norm_w, m_w_k=m_w_k, m_w_v=m_w_v, m_attn_norm_w=m_attn_norm_w, m_w_q=m_w_q, m_w_o=m_w_o, m_ffn_norm_w=m_ffn_norm_w, m_ffn_up_w=m_ffn_up_w, m_ffn_conv_w=m_ffn_conv_w, m_ffn_conv_b=m_ffn_conv_b, m_ffn_down_w=m_ffn_down_w, m_final_norm_w=m_final_norm_w, v_ssm_norm_w=v_ssm_norm_w, v_ssm_in_w=v_ssm_in_w, v_ssm_conv_w=v_ssm_conv_w, v_ssm_conv_b=v_ssm_conv_b, v_ssm_dt_bias=v_ssm_dt_bias, v_ssm_a_log=v_ssm_a_log, v_ssm_d=v_ssm_d, v_ssm_gate_norm_w=v_ssm_gate_norm_w, v_ssm_out_w=v_ssm_out_w, v_kv_norm_w=v_kv_norm_w, v_w_k=v_w_k, v_w_v=v_w_v, v_attn_norm_w=v_attn_norm_w, v_w_q=v_w_q, v_w_o=v_w_o, v_ffn_norm_w=v_ffn_norm_w, v_ffn_up_w=v_ffn_up_w, v_ffn_conv_w=v_ffn_conv_w, v_ffn_conv_b=v_ffn_conv_b, v_ffn_down_w=v_ffn_down_w, v_final_norm_w=v_final_norm_w)
    weights = {n: given[n] for n in TWIN_WEIGHTS}
    shared = {n: given[n] for n in SHARED_INPUTS}
    per_example = {n: given[n] for n in ['x']}
    grad_fn = _jax.value_and_grad(_loss, argnums=(0, 1))

    def one_microbatch(ex, loss_target):
        ex = dict(ex)
        diff = ex.pop(TWIN_DIFF_INPUT)
        return grad_fn(weights, diff, {**shared, **ex}, loss_target)

    if N_MICROBATCH == 1:
        loss, (grad_w, grad_x) = one_microbatch(per_example, given["loss_target"])
    else:
        def body(carry, xs):
            loss_sum, grad_sum = carry
            l_k, (gw_k, gx_k) = one_microbatch(xs[0], xs[1])
            with _jax.named_scope("update"):
                return (loss_sum + l_k, _jax.tree.map(_jnp.add, grad_sum, gw_k)), gx_k

        init = (_jnp.zeros((), _jnp.float32), _jax.tree.map(_jnp.zeros_like, weights))
        (loss, grad_w), grad_x = _jax.lax.scan(body, init, (per_example, given["loss_target"]))
    with _jax.named_scope("update"):
        delta_w, new_m, new_v = {}, {}, {}
        for n in TWIN_WEIGHTS:
            delta_w[n], new_m[n], new_v[n] = _adamw(weights[n], grad_w[n], given["m_" + n], given["v_" + n])
    return (loss, grad_x, *[grad_w[n] for n in TWIN_WEIGHTS], *[delta_w[n] for n in TWIN_WEIGHTS],
            *[new_m[n] for n in TWIN_WEIGHTS], *[new_v[n] for n in TWIN_WEIGHTS])
```

```python
import math

import jax
import jax.numpy as jnp
from jax import lax
from jax.experimental import pallas as pl
from jax.experimental.pallas import tpu as pltpu

D_MODEL = 1024
D_INNER = 2048
SSM_HEAD_DIM = 64
SSM_HEADS = 32
SSM_GROUPS = 4
SSM_STATE = 128
SSM_CONV = 4
SSM_CHUNK = 128
GN = SSM_GROUPS * SSM_STATE
CONV_DIM = D_INNER + 2 * GN
SB_HEADS = 16
SB_HEAD_DIM = 64
D_FF = 2816
FFN_CONV = 3
EPS = 1e-6
ADAM_LR = 0.001
ADAM_B1 = 0.9
ADAM_B2 = 0.999
ADAM_EPS = 1e-08
ADAM_WD = 0.01
ADAM_STEP = 10

LANES = 128
SUBLANES = 8
VMEM_LIMIT = 48 * 1024 * 1024
ADAM_BLOCK_BYTES = 1 << 20
F32 = jnp.float32
BF16 = jnp.bfloat16
MESH = pl.DeviceIdType.MESH


def _cparams(sem=None):
    return pltpu.CompilerParams(dimension_semantics=sem, vmem_limit_bytes=VMEM_LIMIT)


def _tile(n, cands):
    for c in cands:
        if n % c == 0:
            return c
    return n


def _nt(a, b):
    return lax.dot_general(a, b, (((1,), (1,)), ((), ())), preferred_element_type=F32)


def _tn(a, b):
    return lax.dot_general(a, b, (((0,), (0,)), ((), ())), preferred_element_type=F32)


def _nn(a, b):
    return jnp.dot(a, b, preferred_element_type=F32)


def _split3(x):
    h1 = x.astype(BF16)
    r1 = x - h1.astype(F32)
    h2 = r1.astype(BF16)
    h3 = (r1 - h2.astype(F32)).astype(BF16)
    return h1, h2, h3


def _ones_dot(ones, x, *, ones_left):
    o16 = ones.astype(BF16)
    acc = None
    for piece in _split3(x):
        term = _nn(o16, piece) if ones_left else _nn(piece, o16)
        acc = term if acc is None else acc + term
    return acc


def _softplus(x):
    return jnp.maximum(x, 0.0) + jnp.log(1.0 + jnp.exp(-jnp.abs(x)))


def _sigmoid(x):
    e = jnp.exp(-jnp.abs(x))
    r = 1.0 / (1.0 + e)
    return jnp.where(x >= 0, r, e * r)


def _matmul(a, b, *, ta=False, tb=False, add=None, out_dtype=F32, name):
    m, k = (a.shape[1], a.shape[0]) if ta else a.shape
    n = b.shape[0] if tb else b.shape[1]
    assert (b.shape[1] if tb else b.shape[0]) == k
    tm = _tile(m, (512, 256, 128))
    tn = _tile(n, (512, 256, 128))
    tk = _tile(k, (512, 256, 128))
    nk = k // tk

    def body(*refs):
        if add is None:
            a_ref, b_ref, o_ref, acc_ref = refs
            add_ref = None
        else:
            a_ref, b_ref, add_ref, o_ref, acc_ref = refs
        kk = pl.program_id(2)

        @pl.when(kk == 0)
        def _():
            acc_ref[...] = jnp.zeros_like(acc_ref)

        av = a_ref[...].astype(BF16)
        bv = b_ref[...].astype(BF16)
        dn = (((0 if ta else 1,), (1 if tb else 0,)), ((), ()))
        acc_ref[...] += lax.dot_general(av, bv, dn, preferred_element_type=F32)

        @pl.when(kk == nk - 1)
        def _():
            r = acc_ref[...]
            if add_ref is not None:
                r = r + add_ref[...].astype(F32)
            o_ref[...] = r.astype(o_ref.dtype)

    a_spec = pl.BlockSpec((tk, tm), lambda i, j, kk: (kk, i)) if ta else pl.BlockSpec((tm, tk), lambda i, j, kk: (i, kk))
    b_spec = pl.BlockSpec((tn, tk), lambda i, j, kk: (j, kk)) if tb else pl.BlockSpec((tk, tn), lambda i, j, kk: (kk, j))
    in_specs = [a_spec, b_spec]
    args = [a, b]
    if add is not None:
        in_specs.append(pl.BlockSpec((tm, tn), lambda i, j, kk: (i, j)))
        args.append(add)
    return pl.pallas_call(
        body,
        grid=(m // tm, n // tn, nk),
        in_specs=in_specs,
        out_specs=pl.BlockSpec((tm, tn), lambda i, j, kk: (i, j)),
        out_shape=jax.ShapeDtypeStruct((m, n), out_dtype),
        scratch_shapes=[pltpu.VMEM((tm, tn), F32)],
        compiler_params=_cparams(("parallel", "parallel", "arbitrary")),
        name=name,
    )(*args)


def _rmsnorm_fwd(x, w, *, name):
    t, d = x.shape
    tb = _tile(t, (512, 256, 128))

    def body(x_ref, w_ref, o_ref):
        xv = x_ref[...]
        r = lax.rsqrt(jnp.mean(xv * xv, axis=-1, keepdims=True) + EPS)
        o_ref[...] = (xv * r * w_ref[...]).astype(o_ref.dtype)

    return pl.pallas_call(
        body,
        grid=(t // tb,),
        in_specs=[pl.BlockSpec((tb, d), lambda i: (i, 0)), pl.BlockSpec((1, d), lambda i: (0, 0))],
        out_specs=pl.BlockSpec((tb, d), lambda i: (i, 0)),
        out_shape=jax.ShapeDtypeStruct((t, d), BF16),
        compiler_params=_cparams(("parallel",)),
        name=name,
    )(x, w.reshape(1, d))


def _rmsnorm_bwd(x, dys, dres, *, name):
    t, d = x.shape
    tb = _tile(t, (256, 128))
    nn = len(dys)
    has_res = dres is not None

    def body(*refs):
        x_ref = refs[0]
        dy_refs = refs[1:1 + nn]
        w_refs = refs[1 + nn:1 + 2 * nn]
        pos = 1 + 2 * nn
        res_ref = refs[pos] if has_res else None
        pos += 1 if has_res else 0
        dx_ref = refs[pos]
        dw_refs = refs[pos + 1:pos + 1 + nn]
        i = pl.program_id(0)
        xv = x_ref[...]
        r = lax.rsqrt(jnp.mean(xv * xv, axis=-1, keepdims=True) + EPS)
        xn = xv * r
        dx = res_ref[...] if has_res else jnp.zeros_like(xv)
        for q in range(nn):
            dy = dy_refs[q][...].astype(F32)
            g = dy * w_refs[q][...]
            dx = dx + r * (g - xn * jnp.mean(g * xn, axis=-1, keepdims=True))
            dwp = jnp.sum(dy * xn, axis=0, keepdims=True)

            @pl.when(i == 0)
            def _(q=q, dwp=dwp):
                dw_refs[q][...] = dwp

            @pl.when(i > 0)
            def _(q=q, dwp=dwp):
                dw_refs[q][...] += dwp
        dx_ref[...] = dx

    row = pl.BlockSpec((tb, d), lambda i: (i, 0))
    vec = pl.BlockSpec((1, d), lambda i: (0, 0))
    in_specs = [row] + [row] * nn + [vec] * nn + ([row] if has_res else [])
    args = [x] + [p[0] for p in dys] + [p[1].reshape(1, d) for p in dys] + ([dres] if has_res else [])
    outs = pl.pallas_call(
        body,
        grid=(t // tb,),
        in_specs=in_specs,
        out_specs=[row] + [vec] * nn,
        out_shape=[jax.ShapeDtypeStruct((t, d), F32)] + [jax.ShapeDtypeStruct((1, d), F32)] * nn,
        compiler_params=_cparams(("arbitrary",)),
        name=name,
    )(*args)
    return outs[0], list(outs[1:])


def _loss_head(x, w, target, *, name):
    t, d = x.shape
    tb = _tile(t, (256, 128))

    def body(x_ref, w_ref, t_ref, loss_ref, dx_ref, dw_ref):
        i = pl.program_id(0)
        xv = x_ref[...]
        wv = w_ref[...]
        r = lax.rsqrt(jnp.mean(xv * xv, axis=-1, keepdims=True) + EPS)
        xn = xv * r
        e = xn * wv - t_ref[...]
        lp = 0.5 * jnp.sum(jnp.mean(e * e, axis=-1, keepdims=True), axis=0, keepdims=True)
        dy = e * (1.0 / d)
        g = dy * wv
        dx_ref[...] = r * (g - xn * jnp.mean(g * xn, axis=-1, keepdims=True))
        dwp = jnp.sum(dy * xn, axis=0, keepdims=True)
        lpv = jnp.broadcast_to(lp, (1, LANES)) * (1.0 / LANES)

        @pl.when(i == 0)
        def _():
            dw_ref[...] = dwp
            loss_ref[...] = lpv

        @pl.when(i > 0)
        def _():
            dw_ref[...] += dwp
            loss_ref[...] += lpv

    row = pl.BlockSpec((tb, d), lambda i: (i, 0))
    vec = pl.BlockSpec((1, d), lambda i: (0, 0))
    return pl.pallas_call(
        body,
        grid=(t // tb,),
        in_specs=[row, vec, row],
        out_specs=[pl.BlockSpec((1, LANES), lambda i: (0, 0)), row, vec],
        out_shape=[jax.ShapeDtypeStruct((1, LANES), F32), jax.ShapeDtypeStruct((t, d), F32),
                   jax.ShapeDtypeStruct((1, d), F32)],
        compiler_params=_cparams(("arbitrary",)),
        name=name,
    )(x, w.reshape(1, d), target)


ROW_CHUNK = 512
PAD = SUBLANES


def _shifted(pad_ref, r0, rows, back):
    return pad_ref[pl.ds(PAD + r0 - back, rows), :]


def _conv_taps(pad_ref, w_ref, r0, rows, kw):
    acc = None
    for j in range(kw):
        term = _shifted(pad_ref, r0, rows, kw - 1 - j) * w_ref[j:j + 1, :]
        acc = term if acc is None else acc + term
    return acc


def _fill_pad(pad_ref, x_ref, t):
    pad_ref[0:PAD, :] = jnp.zeros((PAD, pad_ref.shape[1]), F32)
    pad_ref[pl.ds(PAD + t, PAD), :] = jnp.zeros((PAD, pad_ref.shape[1]), F32)
    pad_ref[pl.ds(PAD, t), :] = x_ref[...].astype(F32)


def _conv_silu_fwd(x, w, b, *, x_off=0, name):
    t = x.shape[0]
    kw, c = w.shape
    cw = _tile(math.gcd(c, x_off) if x_off else c, (256, 128))
    ob = x_off // cw
    rc = _tile(t, (ROW_CHUNK,))

    def body(x_ref, w_ref, b_ref, o_ref, pad_ref):
        _fill_pad(pad_ref, x_ref, t)
        for r0 in range(0, t, rc):
            pre = _conv_taps(pad_ref, w_ref, r0, rc, kw) + b_ref[...]
            o_ref[pl.ds(r0, rc), :] = pre * _sigmoid(pre)

    strip = pl.BlockSpec((t, cw), lambda i: (0, i))
    return pl.pallas_call(
        body,
        grid=(c // cw,),
        in_specs=[pl.BlockSpec((t, cw), lambda i: (0, i + ob)), pl.BlockSpec((kw, cw), lambda i: (0, i)),
                  pl.BlockSpec((1, cw), lambda i: (0, i))],
        out_specs=strip,
        out_shape=jax.ShapeDtypeStruct((t, c), F32),
        scratch_shapes=[pltpu.VMEM((t + 2 * PAD, cw), F32)],
        compiler_params=_cparams(("parallel",)),
        name=name,
    )(x, w, b.reshape(1, c))


def _conv_bwd_core(dpre_pad_ref, x_pad_ref, w_ref, dx_ref, dw_ref, db_ref, t, rc, kw):
    cw = dx_ref.shape[1]
    dws = [jnp.zeros((1, cw), F32) for _ in range(kw)]
    dbs = jnp.zeros((1, cw), F32)
    for r0 in range(0, t, rc):
        dpre = dpre_pad_ref[pl.ds(PAD + r0, rc), :]
        dx = None
        for j in range(kw):
            s = kw - 1 - j
            term = dpre_pad_ref[pl.ds(PAD + r0 + s, rc), :] * w_ref[j:j + 1, :]
            dx = term if dx is None else dx + term
            dws[j] = dws[j] + jnp.sum(dpre * _shifted(x_pad_ref, r0, rc, s), axis=0, keepdims=True)
        dbs = dbs + jnp.sum(dpre, axis=0, keepdims=True)
        dx_ref[pl.ds(r0, rc), :] = dx
    for j in range(kw):
        dw_ref[j:j + 1, :] = dws[j]
    db_ref[...] = dbs


def _conv_silu_bwd(x, w, b, dact, *, x_off=0, name):
    t = x.shape[0]
    kw, c = w.shape
    cw = _tile(math.gcd(c, x_off) if x_off else c, (256, 128))
    ob = x_off // cw
    rc = _tile(t, (ROW_CHUNK,))

    def body(x_ref, w_ref, b_ref, da_ref, dx_ref, dw_ref, db_ref, xpad_ref, dpad_ref):
        _fill_pad(xpad_ref, x_ref, t)
        dpad_ref[0:PAD, :] = jnp.zeros((PAD, cw), F32)
        dpad_ref[pl.ds(PAD + t, PAD), :] = jnp.zeros((PAD, cw), F32)
        for r0 in range(0, t, rc):
            pre = _conv_taps(xpad_ref, w_ref, r0, rc, kw) + b_ref[...]
            sg = _sigmoid(pre)
            dpad_ref[pl.ds(PAD + r0, rc), :] = da_ref[pl.ds(r0, rc), :] * (sg * (1.0 + pre * (1.0 - sg)))
        _conv_bwd_core(dpad_ref, xpad_ref, w_ref, dx_ref, dw_ref, db_ref, t, rc, kw)

    strip = pl.BlockSpec((t, cw), lambda i: (0, i))
    wspec = pl.BlockSpec((kw, cw), lambda i: (0, i))
    bspec = pl.BlockSpec((1, cw), lambda i: (0, i))
    return pl.pallas_call(
        body,
        grid=(c // cw,),
        in_specs=[pl.BlockSpec((t, cw), lambda i: (0, i + ob)), wspec, bspec, strip],
        out_specs=[strip, wspec, bspec],
        out_shape=[jax.ShapeDtypeStruct((t, c), F32), jax.ShapeDtypeStruct((kw, c), F32),
                   jax.ShapeDtypeStruct((1, c), F32)],
        scratch_shapes=[pltpu.VMEM((t + 2 * PAD, cw), F32), pltpu.VMEM((t + 2 * PAD, cw), F32)],
        compiler_params=_cparams(("parallel",)),
        name=name,
    )(x, w, b.reshape(1, c), dact)


def _conv_glu_fwd(hid, w, b, *, name):
    t, c2 = hid.shape
    f = c2 // 2
    kw = w.shape[0]
    cw = _tile(f, (256, 128))
    nf = f // cw
    rc = _tile(t, (ROW_CHUNK,))

    def body(g_ref, v_ref, wg_ref, wv_ref, bg_ref, bv_ref, o_ref, gpad_ref, vpad_ref):
        _fill_pad(gpad_ref, g_ref, t)
        _fill_pad(vpad_ref, v_ref, t)
        for r0 in range(0, t, rc):
            gate = _conv_taps(gpad_ref, wg_ref, r0, rc, kw) + bg_ref[...]
            val = _conv_taps(vpad_ref, wv_ref, r0, rc, kw) + bv_ref[...]
            o_ref[pl.ds(r0, rc), :] = (gate * _sigmoid(gate) * val).astype(o_ref.dtype)

    gs = pl.BlockSpec((t, cw), lambda i: (0, i))
    vs = pl.BlockSpec((t, cw), lambda i: (0, i + nf))
    b2 = b.reshape(1, c2)
    return pl.pallas_call(
        body,
        grid=(nf,),
        in_specs=[gs, vs, pl.BlockSpec((kw, cw), lambda i: (0, i)), pl.BlockSpec((kw, cw), lambda i: (0, i + nf)),
                  pl.BlockSpec((1, cw), lambda i: (0, i)), pl.BlockSpec((1, cw), lambda i: (0, i + nf))],
        out_specs=gs,
        out_shape=jax.ShapeDtypeStruct((t, f), BF16),
        scratch_shapes=[pltpu.VMEM((t + 2 * PAD, cw), F32), pltpu.VMEM((t + 2 * PAD, cw), F32)],
        compiler_params=_cparams(("parallel",)),
        name=name,
    )(hid, hid, w, w, b2, b2)


def _conv_glu_bwd(hid, w, b, dact, *, name):
    t, c2 = hid.shape
    f = c2 // 2
    kw = w.shape[0]
    cw = _tile(f, (128,))
    nf = f // cw
    rc = _tile(t, (ROW_CHUNK,))

    def body(g_ref, v_ref, wg_ref, wv_ref, bg_ref, bv_ref, da_ref,
             dg_ref, dv_ref, dwg_ref, dwv_ref, dbg_ref, dbv_ref,
             gpad_ref, vpad_ref, dgpad_ref, dvpad_ref):
        _fill_pad(gpad_ref, g_ref, t)
        _fill_pad(vpad_ref, v_ref, t)
        for ref in (dgpad_ref, dvpad_ref):
            ref[0:PAD, :] = jnp.zeros((PAD, cw), F32)
            ref[pl.ds(PAD + t, PAD), :] = jnp.zeros((PAD, cw), F32)
        for r0 in range(0, t, rc):
            gate = _conv_taps(gpad_ref, wg_ref, r0, rc, kw) + bg_ref[...]
            val = _conv_taps(vpad_ref, wv_ref, r0, rc, kw) + bv_ref[...]
            sg = _sigmoid(gate)
            da = da_ref[pl.ds(r0, rc), :].astype(F32)
            dgpad_ref[pl.ds(PAD + r0, rc), :] = da * val * (sg * (1.0 + gate * (1.0 - sg)))
            dvpad_ref[pl.ds(PAD + r0, rc), :] = da * (gate * sg)
        _conv_bwd_core(dgpad_ref, gpad_ref, wg_ref, dg_ref, dwg_ref, dbg_ref, t, rc, kw)
        _conv_bwd_core(dvpad_ref, vpad_ref, wv_ref, dv_ref, dwv_ref, dbv_ref, t, rc, kw)

    gs = pl.BlockSpec((t, cw), lambda i: (0, i))
    vs = pl.BlockSpec((t, cw), lambda i: (0, i + nf))
    wg = pl.BlockSpec((kw, cw), lambda i: (0, i))
    wv = pl.BlockSpec((kw, cw), lambda i: (0, i + nf))
    bg = pl.BlockSpec((1, cw), lambda i: (0, i))
    bv = pl.BlockSpec((1, cw), lambda i: (0, i + nf))
    b2 = b.reshape(1, c2)
    pad = pltpu.VMEM((t + 2 * PAD, cw), F32)
    return pl.pallas_call(
        body,
        grid=(nf,),
        in_specs=[gs, vs, wg, wv, bg, bv, gs],
        out_specs=[gs, gs, wg, wg, bg, bg],
        out_shape=[jax.ShapeDtypeStruct((t, f), F32), jax.ShapeDtypeStruct((t, f), F32),
                   jax.ShapeDtypeStruct((kw, f), F32), jax.ShapeDtypeStruct((kw, f), F32),
                   jax.ShapeDtypeStruct((1, f), F32), jax.ShapeDtypeStruct((1, f), F32)],
        scratch_shapes=[pad, pad, pad, pad],
        compiler_params=_cparams(("parallel",)),
        name=name,
    )(hid, hid, w, w, b2, b2, dact)


def _gate_norm_fwd(y, zx, w, *, name):
    t, di = y.shape
    gsz = di // SSM_GROUPS
    tb = _tile(t, (256, 128))

    def body(y_ref, z_ref, w_ref, o_ref):
        for g in range(SSM_GROUPS):
            sl = slice(g * gsz, (g + 1) * gsz)
            zv = z_ref[:, sl]
            gv = y_ref[:, sl] * (zv * _sigmoid(zv))
            r = lax.rsqrt(jnp.mean(gv * gv, axis=-1, keepdims=True) + EPS)
            o_ref[:, sl] = (gv * r * w_ref[:, sl]).astype(o_ref.dtype)

    row = pl.BlockSpec((tb, di), lambda i: (i, 0))
    return pl.pallas_call(
        body,
        grid=(t // tb,),
        in_specs=[row, row, pl.BlockSpec((1, di), lambda i: (0, 0))],
        out_specs=row,
        out_shape=jax.ShapeDtypeStruct((t, di), BF16),
        compiler_params=_cparams(("parallel",)),
        name=name,
    )(y, zx, w.reshape(1, di))


def _gate_norm_bwd(y, zx, w, dyn, *, name):
    t, di = y.shape
    gsz = di // SSM_GROUPS
    tb = _tile(t, (256, 128))

    def body(y_ref, z_ref, w_ref, d_ref, dy_ref, dz_ref, dw_ref):
        i = pl.program_id(0)
        for g in range(SSM_GROUPS):
            sl = slice(g * gsz, (g + 1) * gsz)
            zv = z_ref[:, sl]
            yv = y_ref[:, sl]
            sg = _sigmoid(zv)
            sz = zv * sg
            gv = yv * sz
            r = lax.rsqrt(jnp.mean(gv * gv, axis=-1, keepdims=True) + EPS)
            gn = gv * r
            dn = d_ref[:, sl].astype(F32)
            q = dn * w_ref[:, sl]
            dg = r * (q - gn * jnp.mean(q * gn, axis=-1, keepdims=True))
            dy_ref[:, sl] = dg * sz
            dz_ref[:, sl] = dg * yv * (sg * (1.0 + zv * (1.0 - sg)))
            dwp = jnp.sum(dn * gn, axis=0, keepdims=True)

            @pl.when(i == 0)
            def _(sl=sl, dwp=dwp):
                dw_ref[:, sl] = dwp

            @pl.when(i > 0)
            def _(sl=sl, dwp=dwp):
                dw_ref[:, sl] += dwp

    row = pl.BlockSpec((tb, di), lambda i: (i, 0))
    vec = pl.BlockSpec((1, di), lambda i: (0, 0))
    return pl.pallas_call(
        body,
        grid=(t // tb,),
        in_specs=[row, row, vec, row],
        out_specs=[row, row, vec],
        out_shape=[jax.ShapeDtypeStruct((t, di), F32), jax.ShapeDtypeStruct((t, di), F32),
                   jax.ShapeDtypeStruct((1, di), F32)],
        compiler_params=_cparams(("arbitrary",)),
        name=name,
    )(y, zx, w.reshape(1, di), dyn)


def _adamw(w, g, m, v, *, name):
    shape = w.shape
    cols = shape[-1]
    rows = w.size // cols
    w2, g2, m2, v2 = (a.reshape(rows, cols) for a in (w, g, m, v))
    tr = rows
    if rows * cols * 4 > ADAM_BLOCK_BYTES:
        tr = _tile(rows, tuple(r for r in (512, 256, 128, 64, 32, 16, 8) if r * cols * 4 <= ADAM_BLOCK_BYTES))
    c1 = 1.0 - ADAM_B1 ** ADAM_STEP
    c2 = 1.0 - ADAM_B2 ** ADAM_STEP

    def body(w_ref, g_ref, m_ref, v_ref, d_ref, nm_ref, nv_ref):
        gv = g_ref[...]
        nm = ADAM_B1 * m_ref[...] + (1.0 - ADAM_B1) * gv
        nv = ADAM_B2 * v_ref[...] + (1.0 - ADAM_B2) * (gv * gv)
        d_ref[...] = -ADAM_LR * ((nm / c1) / (jnp.sqrt(nv / c2) + ADAM_EPS) + ADAM_WD * w_ref[...])
        nm_ref[...] = nm
        nv_ref[...] = nv

    blk = pl.BlockSpec((tr, cols), lambda i: (i, 0))
    outs = pl.pallas_call(
        body,
        grid=(rows // tr,),
        in_specs=[blk] * 4,
        out_specs=[blk] * 3,
        out_shape=[jax.ShapeDtypeStruct((rows, cols), F32)] * 3,
        compiler_params=_cparams(("parallel",)),
        name=name,
    )(w2, g2, m2, v2)
    return tuple(o.reshape(shape) for o in outs)


def _ssd_scalars(dtc_ref, dtr_ref, hpc_ref, hpr_ref, ln):
    bias_c, alog_c = hpc_ref[0, 0:1, :], hpc_ref[0, 1:2, :]
    bias_r, alog_r = hpr_ref[0, :, 0:1], hpr_ref[0, :, 1:2]
    a_c, a_r = -jnp.exp(alog_c), -jnp.exp(alog_r)
    raw_c = dtc_ref[0] + bias_c
    dt_c = _softplus(raw_c)
    dt_r = _softplus(dtr_ref[0] + bias_r)
    row = lax.broadcasted_iota(jnp.int32, (ln, ln), 0)
    col = lax.broadcasted_iota(jnp.int32, (ln, ln), 1)
    lower = (col <= row).astype(F32)
    upper = (row <= col).astype(F32)
    acs_c = _ones_dot(lower, dt_c * a_c, ones_left=True)
    acs_r = _ones_dot(upper, dt_r * a_r, ones_left=False)
    return raw_c, dt_c, a_c, acs_c, acs_r, row, col


def _ssd_specs(t, di, g_n, n_st, rp, ln, r_h, rev):
    nc = t // ln
    cidx = (lambda c: nc - 1 - c) if rev else (lambda c: c)
    xs = pl.BlockSpec((ln, rp), lambda g, c: (cidx(c), g))
    bm = pl.BlockSpec((ln, n_st), lambda g, c: (cidx(c), di // n_st + g))
    cm = pl.BlockSpec((ln, n_st), lambda g, c: (cidx(c), di // n_st + g_n + g))
    dtc = pl.BlockSpec((1, ln, r_h), lambda g, c: (g, cidx(c), 0))
    dtr = pl.BlockSpec((1, r_h, ln), lambda g, c: (g, 0, cidx(c)))
    hpc = pl.BlockSpec((1, 3, r_h), lambda g, c: (g, 0, 0))
    hpr = pl.BlockSpec((1, r_h, 3), lambda g, c: (g, 0, 0))
    prev = pl.BlockSpec((1, rp, n_st), lambda g, c: (cidx(c), g, 0))
    return xs, bm, cm, dtc, dtr, hpc, hpr, prev


def _ssd_fwd(xbc, dtc, dtr, hpc, hpr, *, name):
    t = xbc.shape[0]
    di, g_n, n_st, p_h, ln = D_INNER, SSM_GROUPS, SSM_STATE, SSM_HEAD_DIM, SSM_CHUNK
    r_h = SSM_HEADS // g_n
    rp = r_h * p_h
    nc = t // ln

    def body(xs_ref, b_ref, c_ref, dtc_ref, dtr_ref, hpc_ref, hpr_ref, y_ref, prev_ref, st_ref):
        @pl.when(pl.program_id(1) == 0)
        def _():
            st_ref[...] = jnp.zeros_like(st_ref)

        _, dt_c, _, acs_c, acs_r, row, col = _ssd_scalars(dtc_ref, dtr_ref, hpc_ref, hpr_ref, ln)
        bm = b_ref[...]
        cm = c_ref[...]
        cm16 = cm.astype(BF16)
        cb = _nt(cm16, bm.astype(BF16))
        causal = row >= col
        for r in range(r_h):
            sl = slice(r * p_h, (r + 1) * p_h)
            xs = xs_ref[:, sl]
            acs = acs_c[:, r:r + 1]
            last = acs_c[ln - 1:ln, r:r + 1]
            lm = jnp.where(causal, jnp.exp(acs - acs_r[r:r + 1, :]), 0.0)
            xd = (xs * dt_c[:, r:r + 1]).astype(BF16)
            prev = st_ref[sl, :]
            y = _nn((cb * lm).astype(BF16), xd)
            y = y + _nt(cm16, prev.astype(BF16)) * jnp.exp(acs)
            y_ref[:, sl] = y + hpc_ref[0, 2:3, r:r + 1] * xs
            prev_ref[0, sl, :] = prev
            bd = (bm * jnp.exp(last - acs)).astype(BF16)
            st_ref[sl, :] = prev * jnp.exp(last) + _tn(xd, bd)

    xs, bm, cm, dtcs, dtrs, hpcs, hprs, prev = _ssd_specs(t, di, g_n, n_st, rp, ln, r_h, False)
    return pl.pallas_call(
        body,
        grid=(g_n, nc),
        in_specs=[xs, bm, cm, dtcs, dtrs, hpcs, hprs],
        out_specs=[xs, prev],
        out_shape=[jax.ShapeDtypeStruct((t, di), F32), jax.ShapeDtypeStruct((nc, g_n * rp, n_st), F32)],
        scratch_shapes=[pltpu.VMEM((rp, n_st), F32)],
        compiler_params=_cparams(("parallel", "arbitrary")),
        name=name,
    )(xbc, xbc, xbc, dtc, dtr, hpc, hpr)


def _ssd_bwd(xbc, dtc, dtr, hpc, hpr, prev, dy, *, name):
    t = xbc.shape[0]
    di, g_n, n_st, p_h, ln = D_INNER, SSM_GROUPS, SSM_STATE, SSM_HEAD_DIM, SSM_CHUNK
    r_h = SSM_HEADS // g_n
    rp = r_h * p_h
    nc = t // ln

    def body(xs_ref, b_ref, c_ref, dtc_ref, dtr_ref, hpc_ref, hpr_ref, prev_ref, dy_ref,
             dxs_ref, db_ref, dc_ref, ddt_ref, hg_ref, ds_ref):
        step = pl.program_id(1)

        @pl.when(step == 0)
        def _():
            ds_ref[...] = jnp.zeros_like(ds_ref)

        raw_c, dt_c, a_c, acs_c, acs_r, row, col = _ssd_scalars(dtc_ref, dtr_ref, hpc_ref, hpr_ref, ln)
        bm = b_ref[...]
        cm = c_ref[...]
        bm16, cm16 = bm.astype(BF16), cm.astype(BF16)
        cb = _nt(cm16, bm16)
        cbt = _nt(bm16, cm16)
        lane_r = lax.broadcasted_iota(jnp.int32, (ln, r_h), 1)
        sub_r = lax.broadcasted_iota(jnp.int32, (ln, r_h), 0)
        dacs_all = jnp.zeros((ln, r_h), F32)
        ddtx_all = jnp.zeros((ln, r_h), F32)
        dd_all = jnp.zeros((ln, r_h), F32)
        dcb = jnp.zeros((ln, ln), F32)
        dcbt = jnp.zeros((ln, ln), F32)
        dc_acc = jnp.zeros((ln, n_st), F32)
        db_acc = jnp.zeros((ln, n_st), F32)
        for r in range(r_h):
            sl = slice(r * p_h, (r + 1) * p_h)
            xs = xs_ref[:, sl]
            dyv = dy_ref[:, sl]
            dy16 = dyv.astype(BF16)
            dtv = dt_c[:, r:r + 1]
            acs = acs_c[:, r:r + 1]
            acsr = acs_r[r:r + 1, :]
            last = acs_c[ln - 1:ln, r:r + 1]
            xd = xs * dtv
            xd16 = xd.astype(BF16)
            lm = jnp.where(row >= col, jnp.exp(acs - acsr), 0.0)
            lmt = jnp.where(col >= row, jnp.exp(acsr - acs), 0.0)
            m_ls = cb * lm
            m_sl = cbt * lmt
            dm = _nt(dy16, xd16)
            dmt = _nt(xd16, dy16)
            dxd = _nn(m_sl.astype(BF16), dy16)
            dacs = jnp.sum(dm * m_ls, axis=1, keepdims=True) - jnp.sum(dmt * m_sl, axis=1, keepdims=True)
            dcb = dcb + dm * lm
            dcbt = dcbt + dmt * lmt
            prev = prev_ref[0, sl, :]
            prev16 = prev.astype(BF16)
            e = jnp.exp(acs)
            y_off = _nt(cm16, prev16) * e
            dacs = dacs + jnp.sum(dyv * y_off, axis=1, keepdims=True)
            dyo16 = (dyv * e).astype(BF16)
            dc_acc = dc_acc + _nn(dyo16, prev16)
            dprev = _tn(dyo16, cm16)
            ds = ds_ref[sl, :]
            ds16 = ds.astype(BF16)
            decay = jnp.exp(last - acs)
            bd16 = (bm * decay).astype(BF16)
            dbd = _nn(xd16, ds16)
            dxd = dxd + _nt(bd16, ds16)
            db_acc = db_acc + dbd * decay
            tdec = jnp.sum(dbd * bm, axis=1, keepdims=True) * decay
            dacs = dacs - tdec
            cd = jnp.exp(last)
            dlast = jnp.sum(tdec, axis=0, keepdims=True) + jnp.sum(jnp.sum(prev * ds, axis=1, keepdims=True), axis=0, keepdims=True) * cd
            ds_ref[sl, :] = dprev + cd * ds
            dskip = hpc_ref[0, 2:3, r:r + 1]
            dxs_ref[:, sl] = dxd * dtv + dskip * dyv
            ddtx = jnp.sum(dxd * xs, axis=1, keepdims=True)
            ddv = jnp.sum(dyv * xs, axis=1, keepdims=True)
            dacs = dacs + jnp.where(sub_r[:, 0:1] == ln - 1, dlast, 0.0)
            dacs_all = jnp.where(lane_r == r, dacs, dacs_all)
            ddtx_all = jnp.where(lane_r == r, ddtx, ddtx_all)
            dd_all = jnp.where(lane_r == r, ddv, dd_all)
        dc_ref[...] = dc_acc + _nn(dcb.astype(BF16), bm16)
        db_ref[...] = db_acc + _nn(dcbt.astype(BF16), cm16)
        upper = (row <= col).astype(F32)
        dad = _ones_dot(upper, dacs_all, ones_left=True)
        ddt = dad * a_c + ddtx_all
        ddt_raw = ddt * _sigmoid(raw_c)
        ddt_ref[0] = ddt_raw
        d_bias = jnp.sum(ddt_raw, axis=0, keepdims=True)
        d_alog = jnp.sum(dad * dt_c, axis=0, keepdims=True) * a_c
        d_d = jnp.sum(dd_all, axis=0, keepdims=True)
        hg = jnp.concatenate([d_bias, d_alog, d_d], axis=0)

        @pl.when(step == 0)
        def _():
            hg_ref[0] = hg

        @pl.when(step > 0)
        def _():
            hg_ref[0] += hg

    xs, bms, cms, dtcs, dtrs, hpcs, hprs, prevs = _ssd_specs(t, di, g_n, n_st, rp, ln, r_h, True)
    bout = pl.BlockSpec((ln, n_st), lambda g, c: (nc - 1 - c, g))
    return pl.pallas_call(
        body,
        grid=(g_n, nc),
        in_specs=[xs, bms, cms, dtcs, dtrs, hpcs, hprs, prevs, xs],
        out_specs=[xs, bout, bout, dtcs, hpcs],
        out_shape=[jax.ShapeDtypeStruct((t, di), F32), jax.ShapeDtypeStruct((t, g_n * n_st), F32),
                   jax.ShapeDtypeStruct((t, g_n * n_st), F32), jax.ShapeDtypeStruct((g_n, t, r_h), F32),
                   jax.ShapeDtypeStruct((g_n, 3, r_h), F32)],
        scratch_shapes=[pltpu.VMEM((rp, n_st), F32)],
        compiler_params=_cparams(("parallel", "arbitrary")),
        name=name,
    )(xbc, xbc, xbc, dtc, dtr, hpc, hpr, prev, dy)


SB_BLOCK = 128


def _sb_logits(qv, kv, i, j, row, col, scale):
    z = _nt(qv, kv) * scale
    mask = (j * SB_BLOCK + col) < (i * SB_BLOCK + row)
    sp = _softplus(z)
    lg = jnp.where(mask, -sp, 0.0)
    return z, mask, sp, lg


def _sb_iota():
    row = lax.broadcasted_iota(jnp.int32, (SB_BLOCK, SB_BLOCK), 0)
    col = lax.broadcasted_iota(jnp.int32, (SB_BLOCK, SB_BLOCK), 1)
    return row, col


def _sb_fwd(q, k, v, *, name):
    h, t, d = q.shape
    nq = t // SB_BLOCK
    scale = 1.0 / math.sqrt(d)

    def body(q_ref, k_ref, v_ref, o_ref, lt_ref):
        i = pl.program_id(1)
        qv = q_ref[0]
        row, col = _sb_iota()
        later = (row > col).astype(F32)

        def step(s, carry):
            acc, cl = carry
            j = i - s
            rows = pl.ds(pl.multiple_of(j * SB_BLOCK, SB_BLOCK), SB_BLOCK)
            z, mask, sp, lg = _sb_logits(qv, k_ref[0, rows, :], i, j, row, col, scale)
            c = _ones_dot(later, lg, ones_left=False) + cl
            att = jnp.where(mask, jnp.exp(z - sp + c), 0.0)
            acc = acc + _nn(att.astype(BF16), v_ref[0, rows, :])
            return acc, cl + jnp.sum(lg, axis=1, keepdims=True)

        acc, cl = lax.fori_loop(0, i + 1, step, (jnp.zeros((SB_BLOCK, d), F32), jnp.zeros((SB_BLOCK, 1), F32)))
        o_ref[0] = acc
        lt_ref[0] = cl

    qs = pl.BlockSpec((1, SB_BLOCK, d), lambda hh, i: (hh, i, 0))
    ls = pl.BlockSpec((1, SB_BLOCK, 1), lambda hh, i: (hh, i, 0))
    ks = pl.BlockSpec((1, t, d), lambda hh, i: (hh, 0, 0))
    return pl.pallas_call(
        body,
        grid=(h, nq),
        in_specs=[qs, ks, ks],
        out_specs=[qs, ls],
        out_shape=[jax.ShapeDtypeStruct((h, t, d), F32), jax.ShapeDtypeStruct((h, t, 1), F32)],
        compiler_params=_cparams(("parallel", "arbitrary")),
        name=name,
    )(q, k, v)


def _sb_bwd(q, k, v, lt, do, *, name):
    h, t, d = q.shape
    nq = t // SB_BLOCK
    scale = 1.0 / math.sqrt(d)

    def body(q_ref, k_ref, v_ref, lt_ref, do_ref, dq_ref, dk_ref, dv_ref):
        i = pl.program_id(1)

        @pl.when(i == 0)
        def _():
            dk_ref[...] = jnp.zeros_like(dk_ref)
            dv_ref[...] = jnp.zeros_like(dv_ref)

        qv = q_ref[0]
        do16 = do_ref[0].astype(BF16)
        ltot = lt_ref[0]
        row, col = _sb_iota()
        upto = (row <= col).astype(F32)
        before = (row < col).astype(F32)

        def step(j, carry):
            dq, pl_sum, pg_sum = carry
            rows = pl.ds(pl.multiple_of(j * SB_BLOCK, SB_BLOCK), SB_BLOCK)
            kv = k_ref[0, rows, :]
            vv = v_ref[0, rows, :]
            z, mask, sp, lg = _sb_logits(qv, kv, i, j, row, col, scale)
            c = ltot - (_ones_dot(upto, lg, ones_left=False) + pl_sum)
            att = jnp.where(mask, jnp.exp(z - sp + c), 0.0)
            g = att * _nt(do16, vv)
            dlg = _ones_dot(before, g, ones_left=False) + pg_sum
            sig = _sigmoid(z)
            dz16 = (jnp.where(mask, g * (1.0 - sig) - dlg * sig, 0.0) * scale).astype(BF16)
            dq = dq + _nn(dz16, kv)
            dk_ref[0, rows, :] += _tn(dz16, qv)
            dv_ref[0, rows, :] += _tn(att.astype(BF16), do16)
            return dq, pl_sum + jnp.sum(lg, axis=1, keepdims=True), pg_sum + jnp.sum(g, axis=1, keepdims=True)

        zero = jnp.zeros((SB_BLOCK, 1), F32)
        dq, _, _ = lax.fori_loop(0, i + 1, step, (jnp.zeros((SB_BLOCK, d), F32), zero, zero))
        dq_ref[0] = dq

    qs = pl.BlockSpec((1, SB_BLOCK, d), lambda hh, i: (hh, i, 0))
    ls = pl.BlockSpec((1, SB_BLOCK, 1), lambda hh, i: (hh, i, 0))
    ks = pl.BlockSpec((1, t, d), lambda hh, i: (hh, 0, 0))
    full = jax.ShapeDtypeStruct((h, t, d), F32)
    return pl.pallas_call(
        body,
        grid=(h, nq),
        in_specs=[qs, ks, ks, ls, qs],
        out_specs=[qs, ks, ks],
        out_shape=[full, full, full],
        compiler_params=_cparams(("parallel", "arbitrary")),
        name=name,
    )(q, k, v, lt, do)


def _sum_leading(x, *, name):
    n, rows, cols = x.shape
    tr = _tile(rows, tuple(r for r in (2048, 1024, 512, 256, 128, 64, 32, 16, 8) if r * cols * 4 <= ADAM_BLOCK_BYTES))

    def body(x_ref, o_ref):
        acc = x_ref[0]
        for q in range(1, n):
            acc = acc + x_ref[q]
        o_ref[...] = acc

    return pl.pallas_call(
        body,
        grid=(rows // tr,),
        in_specs=[pl.BlockSpec((n, tr, cols), lambda i: (0, i, 0))],
        out_specs=pl.BlockSpec((tr, cols), lambda i: (i, 0)),
        out_shape=jax.ShapeDtypeStruct((rows, cols), x.dtype),
        compiler_params=_cparams(("parallel",)),
        name=name,
    )(x)


def _add2(a, b, *, name):
    n, rows, cols = a.shape
    tr = _tile(rows, tuple(r for r in (2048, 1024, 512, 256, 128, 64, 32, 16, 8) if r * cols * 4 <= ADAM_BLOCK_BYTES))

    def body(a_ref, b_ref, o_ref):
        o_ref[...] = a_ref[...] + b_ref[...]

    blk = pl.BlockSpec((1, tr, cols), lambda q, i: (q, i, 0))
    return pl.pallas_call(
        body,
        grid=(n, rows // tr),
        in_specs=[blk, blk],
        out_specs=blk,
        out_shape=jax.ShapeDtypeStruct(a.shape, a.dtype),
        compiler_params=_cparams(("parallel", "parallel")),
        name=name,
    )(a, b)


ANY = pl.BlockSpec(memory_space=pl.ANY)


def _other_chips(x, y):
    return [(1 - x, y), (x, 1 - y), (1 - x, 1 - y)]


def _gather_chips(shard, *, name):
    def body(x_ref, o_ref, send_sems, recv_sems, local_sem):
        x, y, c = lax.axis_index("x"), lax.axis_index("y"), lax.axis_index("c")
        me = 2 * x + y
        mine = pltpu.make_async_copy(x_ref, o_ref.at[me], local_sem)
        mine.start()
        chips = _other_chips(x, y)
        sends = [pltpu.make_async_remote_copy(src_ref=x_ref, dst_ref=o_ref.at[me], send_sem=send_sems.at[q],
                                              recv_sem=recv_sems.at[q], device_id=(px, py, c), device_id_type=MESH)
                 for q, (px, py) in enumerate(chips)]
        for cp in sends:
            cp.start()
        for q, (px, py) in enumerate(chips):
            pltpu.make_async_remote_copy(src_ref=x_ref, dst_ref=o_ref.at[2 * px + py], send_sem=send_sems.at[q],
                                         recv_sem=recv_sems.at[q], device_id=(px, py, c), device_id_type=MESH).wait_recv()
        for cp in sends:
            cp.wait_send()
        mine.wait()

    return pl.pallas_call(
        body,
        in_specs=[ANY],
        out_specs=ANY,
        out_shape=jax.ShapeDtypeStruct((4,) + shard.shape, shard.dtype),
        scratch_shapes=[pltpu.SemaphoreType.DMA((3,)), pltpu.SemaphoreType.DMA((3,)), pltpu.SemaphoreType.DMA],
        compiler_params=pltpu.CompilerParams(has_side_effects=True),
        name=name,
    )(shard)


def _scatter_chips(parts, *, name):
    def body(p_ref, o_ref, send_sems, recv_sems, local_sem):
        x, y, c = lax.axis_index("x"), lax.axis_index("y"), lax.axis_index("c")
        me = 2 * x + y
        mine = pltpu.make_async_copy(p_ref.at[me], o_ref.at[me], local_sem)
        mine.start()
        chips = _other_chips(x, y)
        sends = [pltpu.make_async_remote_copy(src_ref=p_ref.at[2 * px + py], dst_ref=o_ref.at[me], send_sem=send_sems.at[q],
                                              recv_sem=recv_sems.at[q], device_id=(px, py, c), device_id_type=MESH)
                 for q, (px, py) in enumerate(chips)]
        for cp in sends:
            cp.start()
        for q, (px, py) in enumerate(chips):
            pltpu.make_async_remote_copy(src_ref=p_ref.at[me], dst_ref=o_ref.at[2 * px + py], send_sem=send_sems.at[q],
                                         recv_sem=recv_sems.at[q], device_id=(px, py, c), device_id_type=MESH).wait_recv()
        for cp in sends:
            cp.wait_send()
        mine.wait()

    return pl.pallas_call(
        body,
        in_specs=[ANY],
        out_specs=ANY,
        out_shape=jax.ShapeDtypeStruct(parts.shape, parts.dtype),
        scratch_shapes=[pltpu.SemaphoreType.DMA((3,)), pltpu.SemaphoreType.DMA((3,)), pltpu.SemaphoreType.DMA],
        compiler_params=pltpu.CompilerParams(has_side_effects=True),
        name=name,
    )(parts)


def _swap_sibling(v, *, name):
    def body(v_ref, o_ref, send_sem, recv_sem):
        x, y, c = lax.axis_index("x"), lax.axis_index("y"), lax.axis_index("c")
        cp = pltpu.make_async_remote_copy(src_ref=v_ref, dst_ref=o_ref, send_sem=send_sem, recv_sem=recv_sem,
                                          device_id=(x, y, 1 - c), device_id_type=MESH)
        cp.start()
        cp.wait()

    return pl.pallas_call(
        body,
        in_specs=[ANY],
        out_specs=ANY,
        out_shape=jax.ShapeDtypeStruct(v.shape, v.dtype),
        scratch_shapes=[pltpu.SemaphoreType.DMA, pltpu.SemaphoreType.DMA],
        compiler_params=pltpu.CompilerParams(has_side_effects=True),
        name=name,
    )(v)


def _gather_all(v, *, name):
    def body(v_ref, o_ref, send_sems, recv_sems, local_sem):
        x, y, c = lax.axis_index("x"), lax.axis_index("y"), lax.axis_index("c")
        me = 4 * x + 2 * y + c
        mine = pltpu.make_async_copy(v_ref, o_ref.at[me], local_sem)
        mine.start()
        peers = [(x ^ (q >> 2 & 1), y ^ (q >> 1 & 1), c ^ (q & 1)) for q in range(1, 8)]
        sends = [pltpu.make_async_remote_copy(src_ref=v_ref, dst_ref=o_ref.at[me], send_sem=send_sems.at[q],
                                              recv_sem=recv_sems.at[q], device_id=peer, device_id_type=MESH)
                 for q, peer in enumerate(peers)]
        for cp in sends:
            cp.start()
        for q, (px, py, pc) in enumerate(peers):
            pltpu.make_async_remote_copy(src_ref=v_ref, dst_ref=o_ref.at[4 * px + 2 * py + pc], send_sem=send_sems.at[q],
                                         recv_sem=recv_sems.at[q], device_id=(px, py, pc), device_id_type=MESH).wait_recv()
        for cp in sends:
            cp.wait_send()
        mine.wait()

    return pl.pallas_call(
        body,
        in_specs=[ANY],
        out_specs=ANY,
        out_shape=jax.ShapeDtypeStruct((8,) + v.shape, v.dtype),
        scratch_shapes=[pltpu.SemaphoreType.DMA((7,)), pltpu.SemaphoreType.DMA((7,)), pltpu.SemaphoreType.DMA],
        compiler_params=pltpu.CompilerParams(has_side_effects=True),
        name=name,
    )(v)


WEIGHTS = ['ssm_norm_w', 'ssm_in_w', 'ssm_conv_w', 'ssm_conv_b', 'ssm_dt_bias', 'ssm_a_log', 'ssm_d',
           'ssm_gate_norm_w', 'ssm_out_w', 'kv_norm_w', 'w_k', 'w_v', 'attn_norm_w', 'w_q', 'w_o',
           'ffn_norm_w', 'ffn_up_w', 'ffn_conv_w', 'ffn_conv_b', 'ffn_down_w', 'final_norm_w']
SHARD_AXIS = {'ssm_norm_w': 1, 'ssm_in_w': 2, 'ssm_conv_w': 2, 'ssm_conv_b': 1, 'ssm_gate_norm_w': 1,
              'ssm_out_w': 1, 'w_k': 0, 'w_v': 0, 'w_q': 1, 'w_o': 1, 'ffn_up_w': 2, 'ffn_conv_w': 2,
              'ffn_down_w': 1}
BIG = ['ssm_in_w', 'ssm_out_w', 'w_k', 'w_v', 'w_q', 'w_o', 'ffn_up_w', 'ffn_down_w']
SMALL = [n for n in WEIGHTS if n in SHARD_AXIS and n not in BIG]
REPLICATED = [n for n in WEIGHTS if n not in SHARD_AXIS]
N_CHIPS = 4


def _pack(arrs, dtype, row_mult):
    flat = jnp.concatenate([a.astype(dtype) for a in arrs], axis=-1)
    n = flat.shape[-1]
    rows = -(-n // LANES)
    rows = -(-rows // row_mult) * row_mult
    flat = jnp.pad(flat, [(0, 0)] * (flat.ndim - 1) + [(0, rows * LANES - n)])
    return flat.reshape(flat.shape[:-1] + (rows, LANES))


def _unpack(buf, shapes):
    lead = buf.shape[:-2]
    flat = buf.reshape(lead + (-1,))
    out, off = [], 0
    for shp in shapes:
        n = math.prod(shp)
        out.append(flat[..., off:off + n].reshape(lead + tuple(shp)))
        off += n
    return out


def _to_shards(full, axis):
    return jnp.stack(jnp.split(full, N_CHIPS, axis=axis), axis=0)


def _from_shards(stacked, axis):
    return jnp.concatenate([stacked[j] for j in range(N_CHIPS)], axis=axis)


def _heads(a, h):
    t = a.shape[0]
    return a.reshape(t, h, a.shape[1] // h).transpose(1, 0, 2)


def _unheads(a):
    h, t, d = a.shape
    return a.transpose(1, 0, 2).reshape(t, h * d)


def _ffn_fwd(h, norm_w, w_up, conv_w, conv_b, w_down, tag):
    u = _rmsnorm_fwd(h, norm_w, name=f"ffn{tag}_norm")
    hid = _matmul(u, w_up, name=f"ffn{tag}_up")
    act = _conv_glu_fwd(hid, conv_w, conv_b, name=f"ffn{tag}_glu")
    out = _matmul(act, w_down, add=h, name=f"ffn{tag}_down")
    return out, (u, hid, act)


def _ffn_bwd(h, saved, dout, norm_w, w_up, conv_w, conv_b, w_down, tag):
    u, hid, act = saved
    f = w_down.shape[0]
    dact = _matmul(dout, w_down, tb=True, name=f"ffn{tag}_down_dx")
    dw_down = _matmul(act, dout, ta=True, name=f"ffn{tag}_down_dw")
    dg, dv, dwg, dwv, dbg, dbv = _conv_glu_bwd(hid, conv_w, conv_b, dact, name=f"ffn{tag}_glu_bwd")
    du = _matmul(dg, w_up[:, :f], tb=True, name=f"ffn{tag}_up_dx_g")
    du = _matmul(dv, w_up[:, f:], tb=True, add=du, name=f"ffn{tag}_up_dx_v")
    dw_up = jnp.concatenate([_matmul(u, dg, ta=True, name=f"ffn{tag}_up_dw_g"),
                             _matmul(u, dv, ta=True, name=f"ffn{tag}_up_dw_v")], axis=1)
    dh, (dnorm,) = _rmsnorm_bwd(h, [(du, norm_w)], dout, name=f"ffn{tag}_norm_bwd")
    return dh, dict(norm=dnorm[0], up=dw_up, conv_w=jnp.concatenate([dwg, dwv], axis=1),
                    conv_b=jnp.concatenate([dbg, dbv], axis=1)[0], down=dw_down)


def _step(x, target, w):
    t = x.shape[0]
    g_n, heads = SSM_GROUPS, SSM_HEADS
    r_h = heads // g_n
    di = D_INNER
    zx_cols = di + CONV_DIM
    w_in = w['ssm_in_w'][0]
    w_zx = w_in[:, :zx_cols]
    w_dt = jnp.pad(w_in[:, zx_cols:], ((0, 0), (0, LANES - heads)))
    conv_w, conv_b = w['ssm_conv_w'][0], w['ssm_conv_b'][0]
    hp = jnp.stack([w['ssm_dt_bias'][0], w['ssm_a_log'][0], w['ssm_d'][0]], axis=0).reshape(3, g_n, r_h)
    hpc, hpr = hp.transpose(1, 0, 2), hp.transpose(1, 2, 0)
    w_out = w['ssm_out_w'][0]
    w_q, w_o = w['w_q'][0], w['w_o'][0]

    h0 = x
    u0 = _rmsnorm_fwd(h0, w['ssm_norm_w'][0], name="ssm_norm")
    zx = _matmul(u0, w_zx, name="ssm_in_zx")
    dt_raw = _matmul(u0, w_dt, name="ssm_in_dt")[:, :heads]
    dtg = dt_raw.reshape(t, g_n, r_h)
    dtc, dtr = dtg.transpose(1, 0, 2), dtg.transpose(1, 2, 0)
    xbc = _conv_silu_fwd(zx, conv_w, conv_b, x_off=di, name="ssm_conv")
    y, prev = _ssd_fwd(xbc, dtc, dtr, hpc, hpr, name="ssd_fwd")
    yn = _gate_norm_fwd(y, zx, w['ssm_gate_norm_w'][0], name="ssm_gate_norm")
    h1 = _matmul(yn, w_out, add=h0, name="ssm_out")
    h2, ffn0 = _ffn_fwd(h1, w['ffn_norm_w'][0], w['ffn_up_w'][0], w['ffn_conv_w'][0], w['ffn_conv_b'][0],
                        w['ffn_down_w'][0], 0)
    hk = _rmsnorm_fwd(h2, w['kv_norm_w'], name="kv_norm")
    qn = _rmsnorm_fwd(h2, w['attn_norm_w'][0], name="attn_norm")
    k2 = _matmul(hk, w['w_k'], out_dtype=BF16, name="attn_k")
    v2 = _matmul(hk, w['w_v'], out_dtype=BF16, name="attn_v")
    q2 = _matmul(qn, w_q, out_dtype=BF16, name="attn_q")
    qh, kh, vh = _heads(q2, SB_HEADS), _heads(k2, SB_HEADS), _heads(v2, SB_HEADS)
    oh, lt = _sb_fwd(qh, kh, vh, name="sb_fwd")
    o2 = _unheads(oh)
    h3 = _matmul(o2, w_o, add=h2, name="attn_o")
    h4, ffn1 = _ffn_fwd(h3, w['ffn_norm_w'][1], w['ffn_up_w'][1], w['ffn_conv_w'][1], w['ffn_conv_b'][1],
                        w['ffn_down_w'][1], 1)
    loss_p, dh4, d_final = _loss_head(h4, w['final_norm_w'], target, name="loss_head")

    dh3, g1 = _ffn_bwd(h3, ffn1, dh4, w['ffn_norm_w'][1], w['ffn_up_w'][1], w['ffn_conv_w'][1],
                       w['ffn_conv_b'][1], w['ffn_down_w'][1], 1)
    do2 = _matmul(dh3, w_o, tb=True, name="attn_o_dx")
    dw_o = _matmul(o2, dh3, ta=True, name="attn_o_dw")
    dqh, dkh, dvh = _sb_bwd(qh, kh, vh, lt, _heads(do2, SB_HEADS), name="sb_bwd")
    dq2, dk2, dv2 = _unheads(dqh), _unheads(dkh), _unheads(dvh)
    dqn = _matmul(dq2, w_q, tb=True, name="attn_q_dx")
    dw_q = _matmul(qn, dq2, ta=True, name="attn_q_dw")
    dhk = _matmul(dk2, w['w_k'], tb=True, name="attn_k_dx")
    dhk = _matmul(dv2, w['w_v'], tb=True, add=dhk, name="attn_v_dx")
    dw_k = _matmul(hk, dk2, ta=True, name="attn_k_dw")
    dw_v = _matmul(hk, dv2, ta=True, name="attn_v_dw")
    dh2, (d_attn_norm, d_kv_norm) = _rmsnorm_bwd(h2, [(dqn, w['attn_norm_w'][0]), (dhk, w['kv_norm_w'])], dh3,
                                                 name="attn_norms_bwd")
    dh1, g0 = _ffn_bwd(h1, ffn0, dh2, w['ffn_norm_w'][0], w['ffn_up_w'][0], w['ffn_conv_w'][0],
                       w['ffn_conv_b'][0], w['ffn_down_w'][0], 0)
    dyn = _matmul(dh1, w_out, tb=True, name="ssm_out_dx")
    dw_out = _matmul(yn, dh1, ta=True, name="ssm_out_dw")
    dy, dz, d_gate = _gate_norm_bwd(y, zx, w['ssm_gate_norm_w'][0], dyn, name="ssm_gate_norm_bwd")
    dxs, dbm, dcm, ddt_g, hg = _ssd_bwd(xbc, dtc, dtr, hpc, hpr, prev, dy, name="ssd_bwd")
    dxbc = jnp.concatenate([dxs, dbm, dcm], axis=1)
    dxbc_pre, d_conv_w, d_conv_b = _conv_silu_bwd(zx, conv_w, conv_b, dxbc, x_off=di, name="ssm_conv_bwd")
    dzx = jnp.concatenate([dz, dxbc_pre], axis=1)
    ddt = jnp.pad(ddt_g.transpose(1, 0, 2).reshape(t, heads), ((0, 0), (0, LANES - heads)))
    du0 = _matmul(dzx, w_zx, tb=True, name="ssm_in_zx_dx")
    du0 = _matmul(ddt, w_dt, tb=True, add=du0, name="ssm_in_dt_dx")
    dw_in = jnp.concatenate([_matmul(u0, dzx, ta=True, name="ssm_in_zx_dw"),
                             _matmul(u0, ddt, ta=True, name="ssm_in_dt_dw")[:, :heads]], axis=1)
    dx, (d_ssm_norm,) = _rmsnorm_bwd(h0, [(du0, w['ssm_norm_w'][0])], dh1, name="ssm_norm_bwd")

    hgr = hg.transpose(1, 0, 2).reshape(3, heads)
    grads = {
        'ssm_norm_w': d_ssm_norm, 'ssm_in_w': dw_in[None], 'ssm_conv_w': d_conv_w[None], 'ssm_conv_b': d_conv_b,
        'ssm_dt_bias': hgr[0:1], 'ssm_a_log': hgr[1:2], 'ssm_d': hgr[2:3], 'ssm_gate_norm_w': d_gate,
        'ssm_out_w': dw_out[None], 'kv_norm_w': d_kv_norm[0], 'w_k': dw_k, 'w_v': dw_v, 'attn_norm_w': d_attn_norm,
        'w_q': dw_q[None], 'w_o': dw_o[None], 'ffn_norm_w': jnp.stack([g0['norm'], g1['norm']]),
        'ffn_up_w': jnp.stack([g0['up'], g1['up']]), 'ffn_conv_w': jnp.stack([g0['conv_w'], g1['conv_w']]),
        'ffn_conv_b': jnp.stack([g0['conv_b'], g1['conv_b']]), 'ffn_down_w': jnp.stack([g0['down'], g1['down']]),
        'final_norm_w': d_final[0],
    }
    return loss_p, dx, grads


def kernel(x, ssm_norm_w, ssm_in_w, ssm_conv_w, ssm_conv_b, ssm_dt_bias, ssm_a_log, ssm_d, ssm_gate_norm_w, ssm_out_w, kv_norm_w, w_k, w_v, attn_norm_w, w_q, w_o, ffn_norm_w, ffn_up_w, ffn_conv_w, ffn_conv_b, ffn_down_w, final_norm_w, loss_target, m_ssm_norm_w, m_ssm_in_w, m_ssm_conv_w, m_ssm_conv_b, m_ssm_dt_bias, m_ssm_a_log, m_ssm_d, m_ssm_gate_norm_w, m_ssm_out_w, m_kv_norm_w, m_w_k, m_w_v, m_attn_norm_w, m_w_q, m_w_o, m_ffn_norm_w, m_ffn_up_w, m_ffn_conv_w, m_ffn_conv_b, m_ffn_down_w, m_final_norm_w, v_ssm_norm_w, v_ssm_in_w, v_ssm_conv_w, v_ssm_conv_b, v_ssm_dt_bias, v_ssm_a_log, v_ssm_d, v_ssm_gate_norm_w, v_ssm_out_w, v_kv_norm_w, v_w_k, v_w_v, v_attn_norm_w, v_w_q, v_w_o, v_ffn_norm_w, v_ffn_up_w, v_ffn_conv_w, v_ffn_conv_b, v_ffn_down_w, v_final_norm_w):
    args = (ssm_norm_w, ssm_in_w, ssm_conv_w, ssm_conv_b, ssm_dt_bias, ssm_a_log, ssm_d, ssm_gate_norm_w, ssm_out_w, kv_norm_w, w_k, w_v, attn_norm_w, w_q, w_o, ffn_norm_w, ffn_up_w, ffn_conv_w, ffn_conv_b, ffn_down_w, final_norm_w)
    moms = (m_ssm_norm_w, m_ssm_in_w, m_ssm_conv_w, m_ssm_conv_b, m_ssm_dt_bias, m_ssm_a_log, m_ssm_d, m_ssm_gate_norm_w, m_ssm_out_w, m_kv_norm_w, m_w_k, m_w_v, m_attn_norm_w, m_w_q, m_w_o, m_ffn_norm_w, m_ffn_up_w, m_ffn_conv_w, m_ffn_conv_b, m_ffn_down_w, m_final_norm_w)
    vels = (v_ssm_norm_w, v_ssm_in_w, v_ssm_conv_w, v_ssm_conv_b, v_ssm_dt_bias, v_ssm_a_log, v_ssm_d, v_ssm_gate_norm_w, v_ssm_out_w, v_kv_norm_w, v_w_k, v_w_v, v_attn_norm_w, v_w_q, v_w_o, v_ffn_norm_w, v_ffn_up_w, v_ffn_conv_w, v_ffn_conv_b, v_ffn_down_w, v_final_norm_w)
    local = dict(zip(WEIGHTS, args))
    m_in = dict(zip(WEIGHTS, moms))
    v_in = dict(zip(WEIGHTS, vels))
    c = lax.axis_index("c")

    big16 = _gather_chips(_pack([local[n].reshape(-1) for n in BIG], BF16, 16), name="gather_big")
    small32 = _gather_chips(_pack([local[n].reshape(-1) for n in SMALL], F32, 8), name="gather_small")
    full = {n: local[n] for n in REPLICATED}
    for names, buf in ((BIG, big16), (SMALL, small32)):
        for n, st in zip(names, _unpack(buf, [local[n].shape for n in names])):
            full[n] = _from_shards(st, SHARD_AXIS[n])

    loss_p, dx, grads = _step(x[0], loss_target[0], full)

    sharded = BIG + SMALL
    g4 = _pack([_to_shards(grads[n], SHARD_AXIS[n]).reshape(N_CHIPS, -1) for n in sharded], F32, 16)
    rows = g4.shape[1]
    g4h = g4.reshape(N_CHIPS, 2, rows // 2, LANES)
    mine = lax.dynamic_index_in_dim(g4h, c, axis=1, keepdims=False)
    theirs = lax.dynamic_index_in_dim(g4h, 1 - c, axis=1, keepdims=False)
    pair = _add2(mine, _swap_sibling(theirs, name="rs_pair_swap"), name="rs_pair_add")
    half = _sum_leading(_scatter_chips(pair, name="rs_chip_scatter"), name="rs_chip_sum")
    other = _swap_sibling(half, name="rs_half_swap")
    shard = jnp.where(c == 0, jnp.concatenate([half, other], axis=0), jnp.concatenate([other, half], axis=0))
    gshard = dict(zip(sharded, _unpack(shard, [local[n].shape for n in sharded])))

    rep = _pack([loss_p.reshape(-1)] + [grads[n].reshape(-1) for n in REPLICATED], F32, 8)
    tot = _sum_leading(_gather_all(rep, name="ar_gather"), name="ar_sum")
    parts = _unpack(tot, [(LANES,)] + [local[n].shape for n in REPLICATED])
    loss = jnp.sum(parts[0])
    gshard.update(dict(zip(REPLICATED, parts[1:])))

    deltas, new_m, new_v = [], [], []
    for n in WEIGHTS:
        d, nm, nv = _adamw(local[n], gshard[n], m_in[n], v_in[n], name=f"adamw_{n}")
        deltas.append(d)
        new_m.append(nm)
        new_v.append(nv)
    return (loss, dx[None], *[gshard[n] for n in WEIGHTS], *deltas, *new_m, *new_v)
```

```python
import math

import jax
import jax.numpy as jnp
from jax import lax
from jax.experimental import pallas as pl
from jax.experimental.pallas import tpu as pltpu

D_MODEL = 1024
D_INNER = 2048
SSM_HEAD_DIM = 64
SSM_HEADS = 32
SSM_GROUPS = 4
SSM_STATE = 128
SSM_CONV = 4
SSM_CHUNK = 128
GN = SSM_GROUPS * SSM_STATE
CONV_DIM = D_INNER + 2 * GN
SB_HEADS = 16
SB_HEAD_DIM = 64
D_FF = 2816
FFN_CONV = 3
EPS = 1e-6
ADAM_LR = 0.001
ADAM_B1 = 0.9
ADAM_B2 = 0.999
ADAM_EPS = 1e-08
ADAM_WD = 0.01
ADAM_STEP = 10

LANES = 128
SUBLANES = 8
VMEM_LIMIT = 48 * 1024 * 1024
ADAM_BLOCK_BYTES = 1 << 20
F32 = jnp.float32
BF16 = jnp.bfloat16
MESH = pl.DeviceIdType.MESH


def _cparams(sem=None):
    return pltpu.CompilerParams(dimension_semantics=sem, vmem_limit_bytes=VMEM_LIMIT)


def _tile(n, cands):
    for c in cands:
        if n % c == 0:
            return c
    return n


def _nt(a, b):
    return lax.dot_general(a, b, (((1,), (1,)), ((), ())), preferred_element_type=F32)


def _tn(a, b):
    return lax.dot_general(a, b, (((0,), (0,)), ((), ())), preferred_element_type=F32)


def _nn(a, b):
    return jnp.dot(a, b, preferred_element_type=F32)


def _split3(x):
    h1 = x.astype(BF16)
    r1 = x - h1.astype(F32)
    h2 = r1.astype(BF16)
    h3 = (r1 - h2.astype(F32)).astype(BF16)
    return h1, h2, h3


def _ones_dot(ones, x, *, ones_left):
    o16 = ones.astype(BF16)
    acc = None
    for piece in _split3(x):
        term = _nn(o16, piece) if ones_left else _nn(piece, o16)
        acc = term if acc is None else acc + term
    return acc


def _softplus(x):
    return jnp.maximum(x, 0.0) + jnp.log(1.0 + jnp.exp(-jnp.abs(x)))


def _sigmoid(x):
    e = jnp.exp(-jnp.abs(x))
    r = 1.0 / (1.0 + e)
    return jnp.where(x >= 0, r, e * r)


MM_TILE_MAX = 1408
MM_VMEM_BUDGET = 40 * 1024 * 1024


def _divisors(n, cap):
    out = [d for d in range(min(cap, n) // LANES * LANES, 0, -LANES) if n % d == 0]
    return out or [n]


def _mm_tiles(m, n, k, a_bytes, b_bytes, o_bytes, add_bytes):
    best = None
    for tm in _divisors(m, MM_TILE_MAX):
        for tn in _divisors(n, MM_TILE_MAX):
            for tk in _divisors(k, MM_TILE_MAX):
                vmem = 2 * (tm * tk * a_bytes + tk * tn * b_bytes + tm * tn * (o_bytes + add_bytes)) + tm * tn * 4
                if vmem > MM_VMEM_BUDGET:
                    continue
                score = (tm * tn * tk, tm * tn)
                if best is None or score > best[0]:
                    best = (score, (tm, tn, tk))
    return best[1]
def _matmul(a, b, *, ta=False, tb=False, add=None, out_dtype=F32, name):
    m, k = (a.shape[1], a.shape[0]) if ta else a.shape
    n = b.shape[0] if tb else b.shape[1]
    assert (b.shape[1] if tb else b.shape[0]) == k
    tm, tn, tk = _mm_tiles(m, n, k, a.dtype.itemsize, b.dtype.itemsize, jnp.dtype(out_dtype).itemsize,
                           0 if add is None else add.dtype.itemsize)
    nk = k // tk

    def body(*refs):
        if add is None:
            a_ref, b_ref, o_ref, acc_ref = refs
            add_ref = None
        else:
            a_ref, b_ref, add_ref, o_ref, acc_ref = refs
        kk = pl.program_id(2)

        @pl.when(kk == 0)
        def _():
            acc_ref[...] = jnp.zeros_like(acc_ref)

        av = a_ref[...].astype(BF16)
        bv = b_ref[...].astype(BF16)
        dn = (((0 if ta else 1,), (1 if tb else 0,)), ((), ()))
        acc_ref[...] += lax.dot_general(av, bv, dn, preferred_element_type=F32)

        @pl.when(kk == nk - 1)
        def _():
            r = acc_ref[...]
            if add_ref is not None:
                r = r + add_ref[...].astype(F32)
            o_ref[...] = r.astype(o_ref.dtype)

    a_spec = pl.BlockSpec((tk, tm), lambda i, j, kk: (kk, i)) if ta else pl.BlockSpec((tm, tk), lambda i, j, kk: (i, kk))
    b_spec = pl.BlockSpec((tn, tk), lambda i, j, kk: (j, kk)) if tb else pl.BlockSpec((tk, tn), lambda i, j, kk: (kk, j))
    in_specs = [a_spec, b_spec]
    args = [a, b]
    if add is not None:
        in_specs.append(pl.BlockSpec((tm, tn), lambda i, j, kk: (i, j)))
        args.append(add)
    return pl.pallas_call(
        body,
        grid=(m // tm, n // tn, nk),
        in_specs=in_specs,
        out_specs=pl.BlockSpec((tm, tn), lambda i, j, kk: (i, j)),
        out_shape=jax.ShapeDtypeStruct((m, n), out_dtype),
        scratch_shapes=[pltpu.VMEM((tm, tn), F32)],
        compiler_params=_cparams(("parallel", "parallel", "arbitrary")),
        name=name,
    )(*args)


def _rmsnorm_fwd(x, w, *, name):
    t, d = x.shape
    tb = _tile(t, (512, 256, 128))

    def body(x_ref, w_ref, o_ref):
        xv = x_ref[...]
        r = lax.rsqrt(jnp.mean(xv * xv, axis=-1, keepdims=True) + EPS)
        o_ref[...] = (xv * r * w_ref[...]).astype(o_ref.dtype)

    return pl.pallas_call(
        body,
        grid=(t // tb,),
        in_specs=[pl.BlockSpec((tb, d), lambda i: (i, 0)), pl.BlockSpec((1, d), lambda i: (0, 0))],
        out_specs=pl.BlockSpec((tb, d), lambda i: (i, 0)),
        out_shape=jax.ShapeDtypeStruct((t, d), BF16),
        compiler_params=_cparams(("parallel",)),
        name=name,
    )(x, w.reshape(1, d))


def _rmsnorm_bwd(x, dys, dres, *, name):
    t, d = x.shape
    tb = _tile(t, (256, 128))
    nn = len(dys)
    has_res = dres is not None

    def body(*refs):
        x_ref = refs[0]
        dy_refs = refs[1:1 + nn]
        w_refs = refs[1 + nn:1 + 2 * nn]
        pos = 1 + 2 * nn
        res_ref = refs[pos] if has_res else None
        pos += 1 if has_res else 0
        dx_ref = refs[pos]
        dw_refs = refs[pos + 1:pos + 1 + nn]
        i = pl.program_id(0)
        xv = x_ref[...]
        r = lax.rsqrt(jnp.mean(xv * xv, axis=-1, keepdims=True) + EPS)
        xn = xv * r
        dx = res_ref[...] if has_res else jnp.zeros_like(xv)
        for q in range(nn):
            dy = dy_refs[q][...].astype(F32)
            g = dy * w_refs[q][...]
            dx = dx + r * (g - xn * jnp.mean(g * xn, axis=-1, keepdims=True))
            dwp = jnp.sum(dy * xn, axis=0, keepdims=True)

            @pl.when(i == 0)
            def _(q=q, dwp=dwp):
                dw_refs[q][...] = dwp

            @pl.when(i > 0)
            def _(q=q, dwp=dwp):
                dw_refs[q][...] += dwp
        dx_ref[...] = dx

    row = pl.BlockSpec((tb, d), lambda i: (i, 0))
    vec = pl.BlockSpec((1, d), lambda i: (0, 0))
    in_specs = [row] + [row] * nn + [vec] * nn + ([row] if has_res else [])
    args = [x] + [p[0] for p in dys] + [p[1].reshape(1, d) for p in dys] + ([dres] if has_res else [])
    outs = pl.pallas_call(
        body,
        grid=(t // tb,),
        in_specs=in_specs,
        out_specs=[row] + [vec] * nn,
        out_shape=[jax.ShapeDtypeStruct((t, d), F32)] + [jax.ShapeDtypeStruct((1, d), F32)] * nn,
        compiler_params=_cparams(("arbitrary",)),
        name=name,
    )(*args)
    return outs[0], list(outs[1:])


def _loss_head(x, w, target, *, name):
    t, d = x.shape
    tb = _tile(t, (256, 128))

    def body(x_ref, w_ref, t_ref, loss_ref, dx_ref, dw_ref):
        i = pl.program_id(0)
        xv = x_ref[...]
        wv = w_ref[...]
        r = lax.rsqrt(jnp.mean(xv * xv, axis=-1, keepdims=True) + EPS)
        xn = xv * r
        e = xn * wv - t_ref[...]
        lp = 0.5 * jnp.sum(jnp.mean(e * e, axis=-1, keepdims=True), axis=0, keepdims=True)
        dy = e * (1.0 / d)
        g = dy * wv
        dx_ref[...] = r * (g - xn * jnp.mean(g * xn, axis=-1, keepdims=True))
        dwp = jnp.sum(dy * xn, axis=0, keepdims=True)
        lpv = jnp.broadcast_to(lp, (1, LANES)) * (1.0 / LANES)

        @pl.when(i == 0)
        def _():
            dw_ref[...] = dwp
            loss_ref[...] = lpv

        @pl.when(i > 0)
        def _():
            dw_ref[...] += dwp
            loss_ref[...] += lpv

    row = pl.BlockSpec((tb, d), lambda i: (i, 0))
    vec = pl.BlockSpec((1, d), lambda i: (0, 0))
    return pl.pallas_call(
        body,
        grid=(t // tb,),
        in_specs=[row, vec, row],
        out_specs=[pl.BlockSpec((1, LANES), lambda i: (0, 0)), row, vec],
        out_shape=[jax.ShapeDtypeStruct((1, LANES), F32), jax.ShapeDtypeStruct((t, d), F32),
                   jax.ShapeDtypeStruct((1, d), F32)],
        compiler_params=_cparams(("arbitrary",)),
        name=name,
    )(x, w.reshape(1, d), target)


ROW_CHUNK = 512
PAD = SUBLANES


def _shifted(pad_ref, r0, rows, back):
    return pad_ref[pl.ds(PAD + r0 - back, rows), :]


def _conv_taps(pad_ref, w_ref, r0, rows, kw):
    acc = None
    for j in range(kw):
        term = _shifted(pad_ref, r0, rows, kw - 1 - j) * w_ref[j:j + 1, :]
        acc = term if acc is None else acc + term
    return acc


def _fill_pad(pad_ref, x_ref, t):
    pad_ref[0:PAD, :] = jnp.zeros((PAD, pad_ref.shape[1]), F32)
    pad_ref[pl.ds(PAD + t, PAD), :] = jnp.zeros((PAD, pad_ref.shape[1]), F32)
    pad_ref[pl.ds(PAD, t), :] = x_ref[...].astype(F32)


def _conv_silu_fwd(x, w, b, *, x_off=0, name):
    t = x.shape[0]
    kw, c = w.shape
    cw = _tile(math.gcd(c, x_off) if x_off else c, (256, 128))
    ob = x_off // cw
    rc = _tile(t, (ROW_CHUNK,))

    def body(x_ref, w_ref, b_ref, o_ref, pad_ref):
        _fill_pad(pad_ref, x_ref, t)
        for r0 in range(0, t, rc):
            pre = _conv_taps(pad_ref, w_ref, r0, rc, kw) + b_ref[...]
            o_ref[pl.ds(r0, rc), :] = pre * _sigmoid(pre)

    strip = pl.BlockSpec((t, cw), lambda i: (0, i))
    return pl.pallas_call(
        body,
        grid=(c // cw,),
        in_specs=[pl.BlockSpec((t, cw), lambda i: (0, i + ob)), pl.BlockSpec((kw, cw), lambda i: (0, i)),
                  pl.BlockSpec((1, cw), lambda i: (0, i))],
        out_specs=strip,
        out_shape=jax.ShapeDtypeStruct((t, c), F32),
        scratch_shapes=[pltpu.VMEM((t + 2 * PAD, cw), F32)],
        compiler_params=_cparams(("parallel",)),
        name=name,
    )(x, w, b.reshape(1, c))


def _conv_bwd_core(dpre_pad_ref, x_pad_ref, w_ref, dx_ref, dw_ref, db_ref, t, rc, kw):
    cw = dx_ref.shape[1]
    dws = [jnp.zeros((1, cw), F32) for _ in range(kw)]
    dbs = jnp.zeros((1, cw), F32)
    for r0 in range(0, t, rc):
        dpre = dpre_pad_ref[pl.ds(PAD + r0, rc), :]
        dx = None
        for j in range(kw):
            s = kw - 1 - j
            term = dpre_pad_ref[pl.ds(PAD + r0 + s, rc), :] * w_ref[j:j + 1, :]
            dx = term if dx is None else dx + term
            dws[j] = dws[j] + jnp.sum(dpre * _shifted(x_pad_ref, r0, rc, s), axis=0, keepdims=True)
        dbs = dbs + jnp.sum(dpre, axis=0, keepdims=True)
        dx_ref[pl.ds(r0, rc), :] = dx
    for j in range(kw):
        dw_ref[j:j + 1, :] = dws[j]
    db_ref[...] = dbs


def _conv_silu_bwd(x, w, b, dact, *, x_off=0, name):
    t = x.shape[0]
    kw, c = w.shape
    cw = _tile(math.gcd(c, x_off) if x_off else c, (256, 128))
    ob = x_off // cw
    rc = _tile(t, (ROW_CHUNK,))

    def body(x_ref, w_ref, b_ref, da_ref, dx_ref, dw_ref, db_ref, xpad_ref, dpad_ref):
        _fill_pad(xpad_ref, x_ref, t)
        dpad_ref[0:PAD, :] = jnp.zeros((PAD, cw), F32)
        dpad_ref[pl.ds(PAD + t, PAD), :] = jnp.zeros((PAD, cw), F32)
        for r0 in range(0, t, rc):
            pre = _conv_taps(xpad_ref, w_ref, r0, rc, kw) + b_ref[...]
            sg = _sigmoid(pre)
            dpad_ref[pl.ds(PAD + r0, rc), :] = da_ref[pl.ds(r0, rc), :] * (sg * (1.0 + pre * (1.0 - sg)))
        _conv_bwd_core(dpad_ref, xpad_ref, w_ref, dx_ref, dw_ref, db_ref, t, rc, kw)

    strip = pl.BlockSpec((t, cw), lambda i: (0, i))
    wspec = pl.BlockSpec((kw, cw), lambda i: (0, i))
    bspec = pl.BlockSpec((1, cw), lambda i: (0, i))
    return pl.pallas_call(
        body,
        grid=(c // cw,),
        in_specs=[pl.BlockSpec((t, cw), lambda i: (0, i + ob)), wspec, bspec, strip],
        out_specs=[strip, wspec, bspec],
        out_shape=[jax.ShapeDtypeStruct((t, c), F32), jax.ShapeDtypeStruct((kw, c), F32),
                   jax.ShapeDtypeStruct((1, c), F32)],
        scratch_shapes=[pltpu.VMEM((t + 2 * PAD, cw), F32), pltpu.VMEM((t + 2 * PAD, cw), F32)],
        compiler_params=_cparams(("parallel",)),
        name=name,
    )(x, w, b.reshape(1, c), dact)


def _conv_glu_fwd(hid, w, b, *, name):
    t, c2 = hid.shape
    f = c2 // 2
    kw = w.shape[0]
    cw = _tile(f, (256, 128))
    nf = f // cw
    rc = _tile(t, (ROW_CHUNK,))

    def body(g_ref, v_ref, wg_ref, wv_ref, bg_ref, bv_ref, o_ref, gpad_ref, vpad_ref):
        _fill_pad(gpad_ref, g_ref, t)
        _fill_pad(vpad_ref, v_ref, t)
        for r0 in range(0, t, rc):
            gate = _conv_taps(gpad_ref, wg_ref, r0, rc, kw) + bg_ref[...]
            val = _conv_taps(vpad_ref, wv_ref, r0, rc, kw) + bv_ref[...]
            o_ref[pl.ds(r0, rc), :] = (gate * _sigmoid(gate) * val).astype(o_ref.dtype)

    gs = pl.BlockSpec((t, cw), lambda i: (0, i))
    vs = pl.BlockSpec((t, cw), lambda i: (0, i + nf))
    b2 = b.reshape(1, c2)
    return pl.pallas_call(
        body,
        grid=(nf,),
        in_specs=[gs, vs, pl.BlockSpec((kw, cw), lambda i: (0, i)), pl.BlockSpec((kw, cw), lambda i: (0, i + nf)),
                  pl.BlockSpec((1, cw), lambda i: (0, i)), pl.BlockSpec((1, cw), lambda i: (0, i + nf))],
        out_specs=gs,
        out_shape=jax.ShapeDtypeStruct((t, f), BF16),
        scratch_shapes=[pltpu.VMEM((t + 2 * PAD, cw), F32), pltpu.VMEM((t + 2 * PAD, cw), F32)],
        compiler_params=_cparams(("parallel",)),
        name=name,
    )(hid, hid, w, w, b2, b2)


def _conv_glu_bwd(hid, w, b, dact, *, name):
    t, c2 = hid.shape
    f = c2 // 2
    kw = w.shape[0]
    cw = _tile(f, (128,))
    nf = f // cw
    rc = _tile(t, (ROW_CHUNK,))

    def body(g_ref, v_ref, wg_ref, wv_ref, bg_ref, bv_ref, da_ref,
             dg_ref, dv_ref, dwg_ref, dwv_ref, dbg_ref, dbv_ref,
             gpad_ref, vpad_ref, dgpad_ref, dvpad_ref):
        _fill_pad(gpad_ref, g_ref, t)
        _fill_pad(vpad_ref, v_ref, t)
        for ref in (dgpad_ref, dvpad_ref):
            ref[0:PAD, :] = jnp.zeros((PAD, cw), F32)
            ref[pl.ds(PAD + t, PAD), :] = jnp.zeros((PAD, cw), F32)
        for r0 in range(0, t, rc):
            gate = _conv_taps(gpad_ref, wg_ref, r0, rc, kw) + bg_ref[...]
            val = _conv_taps(vpad_ref, wv_ref, r0, rc, kw) + bv_ref[...]
            sg = _sigmoid(gate)
            da = da_ref[pl.ds(r0, rc), :].astype(F32)
            dgpad_ref[pl.ds(PAD + r0, rc), :] = da * val * (sg * (1.0 + gate * (1.0 - sg)))
            dvpad_ref[pl.ds(PAD + r0, rc), :] = da * (gate * sg)
        _conv_bwd_core(dgpad_ref, gpad_ref, wg_ref, dg_ref, dwg_ref, dbg_ref, t, rc, kw)
        _conv_bwd_core(dvpad_ref, vpad_ref, wv_ref, dv_ref, dwv_ref, dbv_ref, t, rc, kw)

    gs = pl.BlockSpec((t, cw), lambda i: (0, i))
    vs = pl.BlockSpec((t, cw), lambda i: (0, i + nf))
    wg = pl.BlockSpec((kw, cw), lambda i: (0, i))
    wv = pl.BlockSpec((kw, cw), lambda i: (0, i + nf))
    bg = pl.BlockSpec((1, cw), lambda i: (0, i))
    bv = pl.BlockSpec((1, cw), lambda i: (0, i + nf))
    b2 = b.reshape(1, c2)
    pad = pltpu.VMEM((t + 2 * PAD, cw), F32)
    return pl.pallas_call(
        body,
        grid=(nf,),
        in_specs=[gs, vs, wg, wv, bg, bv, gs],
        out_specs=[gs, gs, wg, wg, bg, bg],
        out_shape=[jax.ShapeDtypeStruct((t, f), F32), jax.ShapeDtypeStruct((t, f), F32),
                   jax.ShapeDtypeStruct((kw, f), F32), jax.ShapeDtypeStruct((kw, f), F32),
                   jax.ShapeDtypeStruct((1, f), F32), jax.ShapeDtypeStruct((1, f), F32)],
        scratch_shapes=[pad, pad, pad, pad],
        compiler_params=_cparams(("parallel",)),
        name=name,
    )(hid, hid, w, w, b2, b2, dact)


def _gate_norm_fwd(y, zx, w, *, name):
    t, di = y.shape
    gsz = di // SSM_GROUPS
    tb = _tile(t, (256, 128))

    def body(y_ref, z_ref, w_ref, o_ref):
        for g in range(SSM_GROUPS):
            sl = slice(g * gsz, (g + 1) * gsz)
            zv = z_ref[:, sl]
            gv = y_ref[:, sl] * (zv * _sigmoid(zv))
            r = lax.rsqrt(jnp.mean(gv * gv, axis=-1, keepdims=True) + EPS)
            o_ref[:, sl] = (gv * r * w_ref[:, sl]).astype(o_ref.dtype)

    row = pl.BlockSpec((tb, di), lambda i: (i, 0))
    return pl.pallas_call(
        body,
        grid=(t // tb,),
        in_specs=[row, row, pl.BlockSpec((1, di), lambda i: (0, 0))],
        out_specs=row,
        out_shape=jax.ShapeDtypeStruct((t, di), BF16),
        compiler_params=_cparams(("parallel",)),
        name=name,
    )(y, zx, w.reshape(1, di))


def _gate_norm_bwd(y, zx, w, dyn, *, name):
    t, di = y.shape
    gsz = di // SSM_GROUPS
    tb = _tile(t, (256, 128))

    def body(y_ref, z_ref, w_ref, d_ref, dy_ref, dz_ref, dw_ref):
        i = pl.program_id(0)
        for g in range(SSM_GROUPS):
            sl = slice(g * gsz, (g + 1) * gsz)
            zv = z_ref[:, sl]
            yv = y_ref[:, sl]
            sg = _sigmoid(zv)
            sz = zv * sg
            gv = yv * sz
            r = lax.rsqrt(jnp.mean(gv * gv, axis=-1, keepdims=True) + EPS)
            gn = gv * r
            dn = d_ref[:, sl].astype(F32)
            q = dn * w_ref[:, sl]
            dg = r * (q - gn * jnp.mean(q * gn, axis=-1, keepdims=True))
            dy_ref[:, sl] = dg * sz
            dz_ref[:, sl] = dg * yv * (sg * (1.0 + zv * (1.0 - sg)))
            dwp = jnp.sum(dn * gn, axis=0, keepdims=True)

            @pl.when(i == 0)
            def _(sl=sl, dwp=dwp):
                dw_ref[:, sl] = dwp

            @pl.when(i > 0)
            def _(sl=sl, dwp=dwp):
                dw_ref[:, sl] += dwp

    row = pl.BlockSpec((tb, di), lambda i: (i, 0))
    vec = pl.BlockSpec((1, di), lambda i: (0, 0))
    return pl.pallas_call(
        body,
        grid=(t // tb,),
        in_specs=[row, row, vec, row],
        out_specs=[row, row, vec],
        out_shape=[jax.ShapeDtypeStruct((t, di), F32), jax.ShapeDtypeStruct((t, di), F32),
                   jax.ShapeDtypeStruct((1, di), F32)],
        compiler_params=_cparams(("arbitrary",)),
        name=name,
    )(y, zx, w.reshape(1, di), dyn)


def _adamw(w, g, m, v, *, name):
    shape = w.shape
    cols = shape[-1]
    rows = w.size // cols
    w2, g2, m2, v2 = (a.reshape(rows, cols) for a in (w, g, m, v))
    tr = rows
    if rows * cols * 4 > ADAM_BLOCK_BYTES:
        tr = _tile(rows, tuple(r for r in (512, 256, 128, 64, 32, 16, 8) if r * cols * 4 <= ADAM_BLOCK_BYTES))
    c1 = 1.0 - ADAM_B1 ** ADAM_STEP
    c2 = 1.0 - ADAM_B2 ** ADAM_STEP

    def body(w_ref, g_ref, m_ref, v_ref, d_ref, nm_ref, nv_ref):
        gv = g_ref[...]
        nm = ADAM_B1 * m_ref[...] + (1.0 - ADAM_B1) * gv
        nv = ADAM_B2 * v_ref[...] + (1.0 - ADAM_B2) * (gv * gv)
        d_ref[...] = -ADAM_LR * ((nm / c1) / (jnp.sqrt(nv / c2) + ADAM_EPS) + ADAM_WD * w_ref[...])
        nm_ref[...] = nm
        nv_ref[...] = nv

    blk = pl.BlockSpec((tr, cols), lambda i: (i, 0))
    outs = pl.pallas_call(
        body,
        grid=(rows // tr,),
        in_specs=[blk] * 4,
        out_specs=[blk] * 3,
        out_shape=[jax.ShapeDtypeStruct((rows, cols), F32)] * 3,
        compiler_params=_cparams(("parallel",)),
        name=name,
    )(w2, g2, m2, v2)
    return tuple(o.reshape(shape) for o in outs)


def _ssd_scalars(dtc_ref, dtr_ref, hpc_ref, hpr_ref, ln):
    bias_c, alog_c = hpc_ref[0, 0:1, :], hpc_ref[0, 1:2, :]
    bias_r, alog_r = hpr_ref[0, :, 0:1], hpr_ref[0, :, 1:2]
    a_c, a_r = -jnp.exp(alog_c), -jnp.exp(alog_r)
    raw_c = dtc_ref[0] + bias_c
    dt_c = _softplus(raw_c)
    dt_r = _softplus(dtr_ref[0] + bias_r)
    row = lax.broadcasted_iota(jnp.int32, (ln, ln), 0)
    col = lax.broadcasted_iota(jnp.int32, (ln, ln), 1)
    lower = (col <= row).astype(F32)
    upper = (row <= col).astype(F32)
    acs_c = _ones_dot(lower, dt_c * a_c, ones_left=True)
    acs_r = _ones_dot(upper, dt_r * a_r, ones_left=False)
    return raw_c, dt_c, a_c, acs_c, acs_r, row, col


def _ssd_specs(t, di, g_n, n_st, rp, ln, r_h, rev):
    nc = t // ln
    cidx = (lambda c: nc - 1 - c) if rev else (lambda c: c)
    xs = pl.BlockSpec((ln, rp), lambda g, c: (cidx(c), g))
    bm = pl.BlockSpec((ln, n_st), lambda g, c: (cidx(c), di // n_st + g))
    cm = pl.BlockSpec((ln, n_st), lambda g, c: (cidx(c), di // n_st + g_n + g))
    dtc = pl.BlockSpec((1, ln, r_h), lambda g, c: (g, cidx(c), 0))
    dtr = pl.BlockSpec((1, r_h, ln), lambda g, c: (g, 0, cidx(c)))
    hpc = pl.BlockSpec((1, 3, r_h), lambda g, c: (g, 0, 0))
    hpr = pl.BlockSpec((1, r_h, 3), lambda g, c: (g, 0, 0))
    prev = pl.BlockSpec((1, rp, n_st), lambda g, c: (cidx(c), g, 0))
    return xs, bm, cm, dtc, dtr, hpc, hpr, prev


def _ssd_fwd(xbc, dtc, dtr, hpc, hpr, *, name):
    t = xbc.shape[0]
    di, g_n, n_st, p_h, ln = D_INNER, SSM_GROUPS, SSM_STATE, SSM_HEAD_DIM, SSM_CHUNK
    r_h = SSM_HEADS // g_n
    rp = r_h * p_h
    nc = t // ln

    def body(xs_ref, b_ref, c_ref, dtc_ref, dtr_ref, hpc_ref, hpr_ref, y_ref, prev_ref, st_ref):
        @pl.when(pl.program_id(1) == 0)
        def _():
            st_ref[...] = jnp.zeros_like(st_ref)

        _, dt_c, _, acs_c, acs_r, row, col = _ssd_scalars(dtc_ref, dtr_ref, hpc_ref, hpr_ref, ln)
        bm = b_ref[...]
        cm = c_ref[...]
        cm16 = cm.astype(BF16)
        cb = _nt(cm16, bm.astype(BF16))
        causal = row >= col
        for r in range(r_h):
            sl = slice(r * p_h, (r + 1) * p_h)
            xs = xs_ref[:, sl]
            acs = acs_c[:, r:r + 1]
            last = acs_c[ln - 1:ln, r:r + 1]
            lm = jnp.where(causal, jnp.exp(acs - acs_r[r:r + 1, :]), 0.0)
            xd = (xs * dt_c[:, r:r + 1]).astype(BF16)
            prev = st_ref[sl, :]
            y = _nn((cb * lm).astype(BF16), xd)
            y = y + _nt(cm16, prev.astype(BF16)) * jnp.exp(acs)
            y_ref[:, sl] = y + hpc_ref[0, 2:3, r:r + 1] * xs
            prev_ref[0, sl, :] = prev
            bd = (bm * jnp.exp(last - acs)).astype(BF16)
            st_ref[sl, :] = prev * jnp.exp(last) + _tn(xd, bd)

    xs, bm, cm, dtcs, dtrs, hpcs, hprs, prev = _ssd_specs(t, di, g_n, n_st, rp, ln, r_h, False)
    return pl.pallas_call(
        body,
        grid=(g_n, nc),
        in_specs=[xs, bm, cm, dtcs, dtrs, hpcs, hprs],
        out_specs=[xs, prev],
        out_shape=[jax.ShapeDtypeStruct((t, di), F32), jax.ShapeDtypeStruct((nc, g_n * rp, n_st), F32)],
        scratch_shapes=[pltpu.VMEM((rp, n_st), F32)],
        compiler_params=_cparams(("parallel", "arbitrary")),
        name=name,
    )(xbc, xbc, xbc, dtc, dtr, hpc, hpr)


def _ssd_bwd(xbc, dtc, dtr, hpc, hpr, prev, dy, *, name):
    t = xbc.shape[0]
    di, g_n, n_st, p_h, ln = D_INNER, SSM_GROUPS, SSM_STATE, SSM_HEAD_DIM, SSM_CHUNK
    r_h = SSM_HEADS // g_n
    rp = r_h * p_h
    nc = t // ln

    def body(xs_ref, b_ref, c_ref, dtc_ref, dtr_ref, hpc_ref, hpr_ref, prev_ref, dy_ref,
             dxs_ref, db_ref, dc_ref, ddt_ref, hg_ref, ds_ref):
        step = pl.program_id(1)

        @pl.when(step == 0)
        def _():
            ds_ref[...] = jnp.zeros_like(ds_ref)

        raw_c, dt_c, a_c, acs_c, acs_r, row, col = _ssd_scalars(dtc_ref, dtr_ref, hpc_ref, hpr_ref, ln)
        bm = b_ref[...]
        cm = c_ref[...]
        bm16, cm16 = bm.astype(BF16), cm.astype(BF16)
        cb = _nt(cm16, bm16)
        cbt = _nt(bm16, cm16)
        lane_r = lax.broadcasted_iota(jnp.int32, (ln, r_h), 1)
        sub_r = lax.broadcasted_iota(jnp.int32, (ln, r_h), 0)
        dacs_all = jnp.zeros((ln, r_h), F32)
        ddtx_all = jnp.zeros((ln, r_h), F32)
        dd_all = jnp.zeros((ln, r_h), F32)
        dcb = jnp.zeros((ln, ln), F32)
        dcbt = jnp.zeros((ln, ln), F32)
        dc_acc = jnp.zeros((ln, n_st), F32)
        db_acc = jnp.zeros((ln, n_st), F32)
        for r in range(r_h):
            sl = slice(r * p_h, (r + 1) * p_h)
            xs = xs_ref[:, sl]
            dyv = dy_ref[:, sl]
            dy16 = dyv.astype(BF16)
            dtv = dt_c[:, r:r + 1]
            acs = acs_c[:, r:r + 1]
            acsr = acs_r[r:r + 1, :]
            last = acs_c[ln - 1:ln, r:r + 1]
            xd = xs * dtv
            xd16 = xd.astype(BF16)
            lm = jnp.where(row >= col, jnp.exp(acs - acsr), 0.0)
            lmt = jnp.where(col >= row, jnp.exp(acsr - acs), 0.0)
            m_ls = cb * lm
            m_sl = cbt * lmt
            dm = _nt(dy16, xd16)
            dmt = _nt(xd16, dy16)
            dxd = _nn(m_sl.astype(BF16), dy16)
            dacs = jnp.sum(dm * m_ls, axis=1, keepdims=True) - jnp.sum(dmt * m_sl, axis=1, keepdims=True)
            dcb = dcb + dm * lm
            dcbt = dcbt + dmt * lmt
            prev = prev_ref[0, sl, :]
            prev16 = prev.astype(BF16)
            e = jnp.exp(acs)
            y_off = _nt(cm16, prev16) * e
            dacs = dacs + jnp.sum(dyv * y_off, axis=1, keepdims=True)
            dyo16 = (dyv * e).astype(BF16)
            dc_acc = dc_acc + _nn(dyo16, prev16)
            dprev = _tn(dyo16, cm16)
            ds = ds_ref[sl, :]
            ds16 = ds.astype(BF16)
            decay = jnp.exp(last - acs)
            bd16 = (bm * decay).astype(BF16)
            dbd = _nn(xd16, ds16)
            dxd = dxd + _nt(bd16, ds16)
            db_acc = db_acc + dbd * decay
            tdec = jnp.sum(dbd * bm, axis=1, keepdims=True) * decay
            dacs = dacs - tdec
            cd = jnp.exp(last)
            dlast = jnp.sum(tdec, axis=0, keepdims=True) + jnp.sum(jnp.sum(prev * ds, axis=1, keepdims=True), axis=0, keepdims=True) * cd
            ds_ref[sl, :] = dprev + cd * ds
            dskip = hpc_ref[0, 2:3, r:r + 1]
            dxs_ref[:, sl] = dxd * dtv + dskip * dyv
            ddtx = jnp.sum(dxd * xs, axis=1, keepdims=True)
            ddv = jnp.sum(dyv * xs, axis=1, keepdims=True)
            dacs = dacs + jnp.where(sub_r[:, 0:1] == ln - 1, dlast, 0.0)
            dacs_all = jnp.where(lane_r == r, dacs, dacs_all)
            ddtx_all = jnp.where(lane_r == r, ddtx, ddtx_all)
            dd_all = jnp.where(lane_r == r, ddv, dd_all)
        dc_ref[...] = dc_acc + _nn(dcb.astype(BF16), bm16)
        db_ref[...] = db_acc + _nn(dcbt.astype(BF16), cm16)
        upper = (row <= col).astype(F32)
        dad = _ones_dot(upper, dacs_all, ones_left=True)
        ddt = dad * a_c + ddtx_all
        ddt_raw = ddt * _sigmoid(raw_c)
        ddt_ref[0] = ddt_raw
        d_bias = jnp.sum(ddt_raw, axis=0, keepdims=True)
        d_alog = jnp.sum(dad * dt_c, axis=0, keepdims=True) * a_c
        d_d = jnp.sum(dd_all, axis=0, keepdims=True)
        hg = jnp.concatenate([d_bias, d_alog, d_d], axis=0)

        @pl.when(step == 0)
        def _():
            hg_ref[0] = hg

        @pl.when(step > 0)
        def _():
            hg_ref[0] += hg

    xs, bms, cms, dtcs, dtrs, hpcs, hprs, prevs = _ssd_specs(t, di, g_n, n_st, rp, ln, r_h, True)
    bout = pl.BlockSpec((ln, n_st), lambda g, c: (nc - 1 - c, g))
    return pl.pallas_call(
        body,
        grid=(g_n, nc),
        in_specs=[xs, bms, cms, dtcs, dtrs, hpcs, hprs, prevs, xs],
        out_specs=[xs, bout, bout, dtcs, hpcs],
        out_shape=[jax.ShapeDtypeStruct((t, di), F32), jax.ShapeDtypeStruct((t, g_n * n_st), F32),
                   jax.ShapeDtypeStruct((t, g_n * n_st), F32), jax.ShapeDtypeStruct((g_n, t, r_h), F32),
                   jax.ShapeDtypeStruct((g_n, 3, r_h), F32)],
        scratch_shapes=[pltpu.VMEM((rp, n_st), F32)],
        compiler_params=_cparams(("parallel", "arbitrary")),
        name=name,
    )(xbc, xbc, xbc, dtc, dtr, hpc, hpr, prev, dy)


SB_KEYS = 128
SB_QUERIES = (512, 256, 128)


def _sb_logits(qv, kv, q0, j, row, col, scale):
    z = _nt(qv, kv) * scale
    mask = (j * SB_KEYS + col) < (q0 + row)
    sp = _softplus(z)
    lg = jnp.where(mask, -sp, 0.0)
    return z, mask, sp, lg


def _sb_iota(tq):
    row = lax.broadcasted_iota(jnp.int32, (tq, SB_KEYS), 0)
    col = lax.broadcasted_iota(jnp.int32, (tq, SB_KEYS), 1)
    krow = lax.broadcasted_iota(jnp.int32, (SB_KEYS, SB_KEYS), 0)
    kcol = lax.broadcasted_iota(jnp.int32, (SB_KEYS, SB_KEYS), 1)
    return row, col, krow, kcol


def _sb_fwd(q, k, v, *, name):
    h, t, d = q.shape
    tq = _tile(t, SB_QUERIES)
    nq = t // tq
    kpq = tq // SB_KEYS
    scale = 1.0 / math.sqrt(d)

    def body(q_ref, k_ref, v_ref, o_ref, lt_ref):
        i = pl.program_id(1)
        qv = q_ref[0]
        row, col, krow, kcol = _sb_iota(tq)
        later = (krow > kcol).astype(F32)
        nkb = (i + 1) * kpq

        def step(s, carry):
            acc, cl = carry
            j = nkb - 1 - s
            rows = pl.ds(pl.multiple_of(j * SB_KEYS, SB_KEYS), SB_KEYS)
            z, mask, sp, lg = _sb_logits(qv, k_ref[0, rows, :], i * tq, j, row, col, scale)
            c = _ones_dot(later, lg, ones_left=False) + cl
            att = jnp.where(mask, jnp.exp(z - sp + c), 0.0)
            acc = acc + _nn(att.astype(BF16), v_ref[0, rows, :])
            return acc, cl + jnp.sum(lg, axis=1, keepdims=True)

        acc, cl = lax.fori_loop(0, nkb, step, (jnp.zeros((tq, d), F32), jnp.zeros((tq, 1), F32)))
        o_ref[0] = acc
        lt_ref[0] = cl

    qs = pl.BlockSpec((1, tq, d), lambda hh, i: (hh, i, 0))
    ls = pl.BlockSpec((1, tq, 1), lambda hh, i: (hh, i, 0))
    ks = pl.BlockSpec((1, t, d), lambda hh, i: (hh, 0, 0))
    return pl.pallas_call(
        body,
        grid=(h, nq),
        in_specs=[qs, ks, ks],
        out_specs=[qs, ls],
        out_shape=[jax.ShapeDtypeStruct((h, t, d), F32), jax.ShapeDtypeStruct((h, t, 1), F32)],
        compiler_params=_cparams(("parallel", "arbitrary")),
        name=name,
    )(q, k, v)


def _sb_bwd(q, k, v, lt, do, *, name):
    h, t, d = q.shape
    tq = _tile(t, SB_QUERIES)
    nq = t // tq
    kpq = tq // SB_KEYS
    scale = 1.0 / math.sqrt(d)

    def body(q_ref, k_ref, v_ref, lt_ref, do_ref, dq_ref, dk_ref, dv_ref):
        i = pl.program_id(1)

        @pl.when(i == 0)
        def _():
            dk_ref[...] = jnp.zeros_like(dk_ref)
            dv_ref[...] = jnp.zeros_like(dv_ref)

        qv = q_ref[0]
        do16 = do_ref[0].astype(BF16)
        ltot = lt_ref[0]
        row, col, krow, kcol = _sb_iota(tq)
        upto = (krow <= kcol).astype(F32)
        before = (krow < kcol).astype(F32)

        def step(j, carry):
            dq, pl_sum, pg_sum = carry
            rows = pl.ds(pl.multiple_of(j * SB_KEYS, SB_KEYS), SB_KEYS)
            kv = k_ref[0, rows, :]
            vv = v_ref[0, rows, :]
            z, mask, sp, lg = _sb_logits(qv, kv, i * tq, j, row, col, scale)
            c = ltot - (_ones_dot(upto, lg, ones_left=False) + pl_sum)
            att = jnp.where(mask, jnp.exp(z - sp + c), 0.0)
            g = att * _nt(do16, vv)
            dlg = _ones_dot(before, g, ones_left=False) + pg_sum
            sig = _sigmoid(z)
            dz16 = (jnp.where(mask, g * (1.0 - sig) - dlg * sig, 0.0) * scale).astype(BF16)
            dq = dq + _nn(dz16, kv)
            dk_ref[0, rows, :] += _tn(dz16, qv)
            dv_ref[0, rows, :] += _tn(att.astype(BF16), do16)
            return dq, pl_sum + jnp.sum(lg, axis=1, keepdims=True), pg_sum + jnp.sum(g, axis=1, keepdims=True)

        zero = jnp.zeros((tq, 1), F32)
        dq, _, _ = lax.fori_loop(0, (i + 1) * kpq, step, (jnp.zeros((tq, d), F32), zero, zero))
        dq_ref[0] = dq

    qs = pl.BlockSpec((1, tq, d), lambda hh, i: (hh, i, 0))
    ls = pl.BlockSpec((1, tq, 1), lambda hh, i: (hh, i, 0))
    ks = pl.BlockSpec((1, t, d), lambda hh, i: (hh, 0, 0))
    full = jax.ShapeDtypeStruct((h, t, d), F32)
    return pl.pallas_call(
        body,
        grid=(h, nq),
        in_specs=[qs, ks, ks, ls, qs],
        out_specs=[qs, ks, ks],
        out_shape=[full, full, full],
        compiler_params=_cparams(("parallel", "arbitrary")),
        name=name,
    )(q, k, v, lt, do)


def _sum_leading(x, *, name):
    n, rows, cols = x.shape
    tr = _tile(rows, tuple(r for r in (2048, 1024, 512, 256, 128, 64, 32, 16, 8) if r * cols * 4 <= ADAM_BLOCK_BYTES))

    def body(x_ref, o_ref):
        acc = x_ref[0]
        for q in range(1, n):
            acc = acc + x_ref[q]
        o_ref[...] = acc

    return pl.pallas_call(
        body,
        grid=(rows // tr,),
        in_specs=[pl.BlockSpec((n, tr, cols), lambda i: (0, i, 0))],
        out_specs=pl.BlockSpec((tr, cols), lambda i: (i, 0)),
        out_shape=jax.ShapeDtypeStruct((rows, cols), x.dtype),
        compiler_params=_cparams(("parallel",)),
        name=name,
    )(x)


def _add2(a, b, *, name):
    n, rows, cols = a.shape
    tr = _tile(rows, tuple(r for r in (2048, 1024, 512, 256, 128, 64, 32, 16, 8) if r * cols * 4 <= ADAM_BLOCK_BYTES))

    def body(a_ref, b_ref, o_ref):
        o_ref[...] = a_ref[...] + b_ref[...]

    blk = pl.BlockSpec((1, tr, cols), lambda q, i: (q, i, 0))
    return pl.pallas_call(
        body,
        grid=(n, rows // tr),
        in_specs=[blk, blk],
        out_specs=blk,
        out_shape=jax.ShapeDtypeStruct(a.shape, a.dtype),
        compiler_params=_cparams(("parallel", "parallel")),
        name=name,
    )(a, b)


ANY = pl.BlockSpec(memory_space=pl.ANY)


def _other_chips(x, y):
    return [(1 - x, y), (x, 1 - y), (1 - x, 1 - y)]


def _gather_chips(shard, *, name):
    def body(x_ref, o_ref, send_sems, recv_sems, local_sem):
        x, y, c = lax.axis_index("x"), lax.axis_index("y"), lax.axis_index("c")
        me = 2 * x + y
        mine = pltpu.make_async_copy(x_ref, o_ref.at[me], local_sem)
        mine.start()
        chips = _other_chips(x, y)
        sends = [pltpu.make_async_remote_copy(src_ref=x_ref, dst_ref=o_ref.at[me], send_sem=send_sems.at[q],
                                              recv_sem=recv_sems.at[q], device_id=(px, py, c), device_id_type=MESH)
                 for q, (px, py) in enumerate(chips)]
        for cp in sends:
            cp.start()
        for q, (px, py) in enumerate(chips):
            pltpu.make_async_remote_copy(src_ref=x_ref, dst_ref=o_ref.at[2 * px + py], send_sem=send_sems.at[q],
                                         recv_sem=recv_sems.at[q], device_id=(px, py, c), device_id_type=MESH).wait_recv()
        for cp in sends:
            cp.wait_send()
        mine.wait()

    return pl.pallas_call(
        body,
        in_specs=[ANY],
        out_specs=ANY,
        out_shape=jax.ShapeDtypeStruct((4,) + shard.shape, shard.dtype),
        scratch_shapes=[pltpu.SemaphoreType.DMA((3,)), pltpu.SemaphoreType.DMA((3,)), pltpu.SemaphoreType.DMA],
        compiler_params=pltpu.CompilerParams(has_side_effects=True),
        name=name,
    )(shard)


def _scatter_chips(parts, *, name):
    def body(p_ref, o_ref, send_sems, recv_sems, local_sem):
        x, y, c = lax.axis_index("x"), lax.axis_index("y"), lax.axis_index("c")
        me = 2 * x + y
        mine = pltpu.make_async_copy(p_ref.at[me], o_ref.at[me], local_sem)
        mine.start()
        chips = _other_chips(x, y)
        sends = [pltpu.make_async_remote_copy(src_ref=p_ref.at[2 * px + py], dst_ref=o_ref.at[me], send_sem=send_sems.at[q],
                                              recv_sem=recv_sems.at[q], device_id=(px, py, c), device_id_type=MESH)
                 for q, (px, py) in enumerate(chips)]
        for cp in sends:
            cp.start()
        for q, (px, py) in enumerate(chips):
            pltpu.make_async_remote_copy(src_ref=p_ref.at[me], dst_ref=o_ref.at[2 * px + py], send_sem=send_sems.at[q],
                                         recv_sem=recv_sems.at[q], device_id=(px, py, c), device_id_type=MESH).wait_recv()
        for cp in sends:
            cp.wait_send()
        mine.wait()

    return pl.pallas_call(
        body,
        in_specs=[ANY],
        out_specs=ANY,
        out_shape=jax.ShapeDtypeStruct(parts.shape, parts.dtype),
        scratch_shapes=[pltpu.SemaphoreType.DMA((3,)), pltpu.SemaphoreType.DMA((3,)), pltpu.SemaphoreType.DMA],
        compiler_params=pltpu.CompilerParams(has_side_effects=True),
        name=name,
    )(parts)


def _swap_sibling(v, *, name):
    def body(v_ref, o_ref, send_sem, recv_sem):
        x, y, c = lax.axis_index("x"), lax.axis_index("y"), lax.axis_index("c")
        cp = pltpu.make_async_remote_copy(src_ref=v_ref, dst_ref=o_ref, send_sem=send_sem, recv_sem=recv_sem,
                                          device_id=(x, y, 1 - c), device_id_type=MESH)
        cp.start()
        cp.wait()

    return pl.pallas_call(
        body,
        in_specs=[ANY],
        out_specs=ANY,
        out_shape=jax.ShapeDtypeStruct(v.shape, v.dtype),
        scratch_shapes=[pltpu.SemaphoreType.DMA, pltpu.SemaphoreType.DMA],
        compiler_params=pltpu.CompilerParams(has_side_effects=True),
        name=name,
    )(v)


def _gather_all(v, *, name):
    def body(v_ref, o_ref, send_sems, recv_sems, local_sem):
        x, y, c = lax.axis_index("x"), lax.axis_index("y"), lax.axis_index("c")
        me = 4 * x + 2 * y + c
        mine = pltpu.make_async_copy(v_ref, o_ref.at[me], local_sem)
        mine.start()
        peers = [(x ^ (q >> 2 & 1), y ^ (q >> 1 & 1), c ^ (q & 1)) for q in range(1, 8)]
        sends = [pltpu.make_async_remote_copy(src_ref=v_ref, dst_ref=o_ref.at[me], send_sem=send_sems.at[q],
                                              recv_sem=recv_sems.at[q], device_id=peer, device_id_type=MESH)
                 for q, peer in enumerate(peers)]
        for cp in sends:
            cp.start()
        for q, (px, py, pc) in enumerate(peers):
            pltpu.make_async_remote_copy(src_ref=v_ref, dst_ref=o_ref.at[4 * px + 2 * py + pc], send_sem=send_sems.at[q],
                                         recv_sem=recv_sems.at[q], device_id=(px, py, pc), device_id_type=MESH).wait_recv()
        for cp in sends:
            cp.wait_send()
        mine.wait()

    return pl.pallas_call(
        body,
        in_specs=[ANY],
        out_specs=ANY,
        out_shape=jax.ShapeDtypeStruct((8,) + v.shape, v.dtype),
        scratch_shapes=[pltpu.SemaphoreType.DMA((7,)), pltpu.SemaphoreType.DMA((7,)), pltpu.SemaphoreType.DMA],
        compiler_params=pltpu.CompilerParams(has_side_effects=True),
        name=name,
    )(v)


WEIGHTS = ['ssm_norm_w', 'ssm_in_w', 'ssm_conv_w', 'ssm_conv_b', 'ssm_dt_bias', 'ssm_a_log', 'ssm_d',
           'ssm_gate_norm_w', 'ssm_out_w', 'kv_norm_w', 'w_k', 'w_v', 'attn_norm_w', 'w_q', 'w_o',
           'ffn_norm_w', 'ffn_up_w', 'ffn_conv_w', 'ffn_conv_b', 'ffn_down_w', 'final_norm_w']
SHARD_AXIS = {'ssm_norm_w': 1, 'ssm_in_w': 2, 'ssm_conv_w': 2, 'ssm_conv_b': 1, 'ssm_gate_norm_w': 1,
              'ssm_out_w': 1, 'w_k': 0, 'w_v': 0, 'w_q': 1, 'w_o': 1, 'ffn_up_w': 2, 'ffn_conv_w': 2,
              'ffn_down_w': 1}
BIG = ['ssm_in_w', 'ssm_out_w', 'w_k', 'w_v', 'w_q', 'w_o', 'ffn_up_w', 'ffn_down_w']
SMALL = [n for n in WEIGHTS if n in SHARD_AXIS and n not in BIG]
REPLICATED = [n for n in WEIGHTS if n not in SHARD_AXIS]
N_CHIPS = 4
RS_ROW_MULT = 4096


def _pack(arrs, dtype, row_mult):
    flat = jnp.concatenate([a.astype(dtype) for a in arrs], axis=-1)
    n = flat.shape[-1]
    rows = -(-n // LANES)
    rows = -(-rows // row_mult) * row_mult
    flat = jnp.pad(flat, [(0, 0)] * (flat.ndim - 1) + [(0, rows * LANES - n)])
    return flat.reshape(flat.shape[:-1] + (rows, LANES))


def _unpack(buf, shapes):
    lead = buf.shape[:-2]
    flat = buf.reshape(lead + (-1,))
    out, off = [], 0
    for shp in shapes:
        n = math.prod(shp)
        out.append(flat[..., off:off + n].reshape(lead + tuple(shp)))
        off += n
    return out


def _to_shards(full, axis):
    return jnp.stack(jnp.split(full, N_CHIPS, axis=axis), axis=0)


def _from_shards(stacked, axis):
    return jnp.concatenate([stacked[j] for j in range(N_CHIPS)], axis=axis)


def _heads(a, h):
    t = a.shape[0]
    return a.reshape(t, h, a.shape[1] // h).transpose(1, 0, 2)


def _unheads(a):
    h, t, d = a.shape
    return a.transpose(1, 0, 2).reshape(t, h * d)


def _ffn_fwd(h, norm_w, w_up, conv_w, conv_b, w_down, tag):
    u = _rmsnorm_fwd(h, norm_w, name=f"ffn{tag}_norm")
    hid = _matmul(u, w_up, name=f"ffn{tag}_up")
    act = _conv_glu_fwd(hid, conv_w, conv_b, name=f"ffn{tag}_glu")
    out = _matmul(act, w_down, add=h, name=f"ffn{tag}_down")
    return out, (u, hid, act)


def _ffn_bwd(h, saved, dout, norm_w, w_up, conv_w, conv_b, w_down, tag):
    u, hid, act = saved
    f = w_down.shape[0]
    dact = _matmul(dout, w_down, tb=True, name=f"ffn{tag}_down_dx")
    dw_down = _matmul(act, dout, ta=True, name=f"ffn{tag}_down_dw")
    dg, dv, dwg, dwv, dbg, dbv = _conv_glu_bwd(hid, conv_w, conv_b, dact, name=f"ffn{tag}_glu_bwd")
    du = _matmul(dg, w_up[:, :f], tb=True, name=f"ffn{tag}_up_dx_g")
    du = _matmul(dv, w_up[:, f:], tb=True, add=du, name=f"ffn{tag}_up_dx_v")
    dw_up = jnp.concatenate([_matmul(u, dg, ta=True, name=f"ffn{tag}_up_dw_g"),
                             _matmul(u, dv, ta=True, name=f"ffn{tag}_up_dw_v")], axis=1)
    dh, (dnorm,) = _rmsnorm_bwd(h, [(du, norm_w)], dout, name=f"ffn{tag}_norm_bwd")
    return dh, dict(norm=dnorm[0], up=dw_up, conv_w=jnp.concatenate([dwg, dwv], axis=1),
                    conv_b=jnp.concatenate([dbg, dbv], axis=1)[0], down=dw_down)


def _step(x, target, w):
    t = x.shape[0]
    g_n, heads = SSM_GROUPS, SSM_HEADS
    r_h = heads // g_n
    di = D_INNER
    zx_cols = di + CONV_DIM
    w_in = w['ssm_in_w'][0]
    w_zx = w_in[:, :zx_cols]
    w_dt = jnp.pad(w_in[:, zx_cols:], ((0, 0), (0, LANES - heads)))
    conv_w, conv_b = w['ssm_conv_w'][0], w['ssm_conv_b'][0]
    hp = jnp.stack([w['ssm_dt_bias'][0], w['ssm_a_log'][0], w['ssm_d'][0]], axis=0).reshape(3, g_n, r_h)
    hpc, hpr = hp.transpose(1, 0, 2), hp.transpose(1, 2, 0)
    w_out = w['ssm_out_w'][0]
    w_q, w_o = w['w_q'][0], w['w_o'][0]

    h0 = x
    u0 = _rmsnorm_fwd(h0, w['ssm_norm_w'][0], name="ssm_norm")
    zx = _matmul(u0, w_zx, name="ssm_in_zx")
    dt_raw = _matmul(u0, w_dt, name="ssm_in_dt")[:, :heads]
    dtg = dt_raw.reshape(t, g_n, r_h)
    dtc, dtr = dtg.transpose(1, 0, 2), dtg.transpose(1, 2, 0)
    xbc = _conv_silu_fwd(zx, conv_w, conv_b, x_off=di, name="ssm_conv")
    y, prev = _ssd_fwd(xbc, dtc, dtr, hpc, hpr, name="ssd_fwd")
    yn = _gate_norm_fwd(y, zx, w['ssm_gate_norm_w'][0], name="ssm_gate_norm")
    h1 = _matmul(yn, w_out, add=h0, name="ssm_out")
    h2, ffn0 = _ffn_fwd(h1, w['ffn_norm_w'][0], w['ffn_up_w'][0], w['ffn_conv_w'][0], w['ffn_conv_b'][0],
                        w['ffn_down_w'][0], 0)
    hk = _rmsnorm_fwd(h2, w['kv_norm_w'], name="kv_norm")
    qn = _rmsnorm_fwd(h2, w['attn_norm_w'][0], name="attn_norm")
    k2 = _matmul(hk, w['w_k'], out_dtype=BF16, name="attn_k")
    v2 = _matmul(hk, w['w_v'], out_dtype=BF16, name="attn_v")
    q2 = _matmul(qn, w_q, out_dtype=BF16, name="attn_q")
    qh, kh, vh = _heads(q2, SB_HEADS), _heads(k2, SB_HEADS), _heads(v2, SB_HEADS)
    oh, lt = _sb_fwd(qh, kh, vh, name="sb_fwd")
    o2 = _unheads(oh)
    h3 = _matmul(o2, w_o, add=h2, name="attn_o")
    h4, ffn1 = _ffn_fwd(h3, w['ffn_norm_w'][1], w['ffn_up_w'][1], w['ffn_conv_w'][1], w['ffn_conv_b'][1],
                        w['ffn_down_w'][1], 1)
    loss_p, dh4, d_final = _loss_head(h4, w['final_norm_w'], target, name="loss_head")

    dh3, g1 = _ffn_bwd(h3, ffn1, dh4, w['ffn_norm_w'][1], w['ffn_up_w'][1], w['ffn_conv_w'][1],
                       w['ffn_conv_b'][1], w['ffn_down_w'][1], 1)
    do2 = _matmul(dh3, w_o, tb=True, name="attn_o_dx")
    dw_o = _matmul(o2, dh3, ta=True, name="attn_o_dw")
    dqh, dkh, dvh = _sb_bwd(qh, kh, vh, lt, _heads(do2, SB_HEADS), name="sb_bwd")
    dq2, dk2, dv2 = _unheads(dqh), _unheads(dkh), _unheads(dvh)
    dqn = _matmul(dq2, w_q, tb=True, name="attn_q_dx")
    dw_q = _matmul(qn, dq2, ta=True, name="attn_q_dw")
    dhk = _matmul(dk2, w['w_k'], tb=True, name="attn_k_dx")
    dhk = _matmul(dv2, w['w_v'], tb=True, add=dhk, name="attn_v_dx")
    dw_k = _matmul(hk, dk2, ta=True, name="attn_k_dw")
    dw_v = _matmul(hk, dv2, ta=True, name="attn_v_dw")
    dh2, (d_attn_norm, d_kv_norm) = _rmsnorm_bwd(h2, [(dqn, w['attn_norm_w'][0]), (dhk, w['kv_norm_w'])], dh3,
                                                 name="attn_norms_bwd")
    dh1, g0 = _ffn_bwd(h1, ffn0, dh2, w['ffn_norm_w'][0], w['ffn_up_w'][0], w['ffn_conv_w'][0],
                       w['ffn_conv_b'][0], w['ffn_down_w'][0], 0)
    dyn = _matmul(dh1, w_out, tb=True, name="ssm_out_dx")
    dw_out = _matmul(yn, dh1, ta=True, name="ssm_out_dw")
    dy, dz, d_gate = _gate_norm_bwd(y, zx, w['ssm_gate_norm_w'][0], dyn, name="ssm_gate_norm_bwd")
    dxs, dbm, dcm, ddt_g, hg = _ssd_bwd(xbc, dtc, dtr, hpc, hpr, prev, dy, name="ssd_bwd")
    dxbc = jnp.concatenate([dxs, dbm, dcm], axis=1)
    dxbc_pre, d_conv_w, d_conv_b = _conv_silu_bwd(zx, conv_w, conv_b, dxbc, x_off=di, name="ssm_conv_bwd")
    dzx = jnp.concatenate([dz, dxbc_pre], axis=1)
    ddt = jnp.pad(ddt_g.transpose(1, 0, 2).reshape(t, heads), ((0, 0), (0, LANES - heads)))
    du0 = _matmul(dzx, w_zx, tb=True, name="ssm_in_zx_dx")
    du0 = _matmul(ddt, w_dt, tb=True, add=du0, name="ssm_in_dt_dx")
    dw_in = jnp.concatenate([_matmul(u0, dzx, ta=True, name="ssm_in_zx_dw"),
                             _matmul(u0, ddt, ta=True, name="ssm_in_dt_dw")[:, :heads]], axis=1)
    dx, (d_ssm_norm,) = _rmsnorm_bwd(h0, [(du0, w['ssm_norm_w'][0])], dh1, name="ssm_norm_bwd")

    hgr = hg.transpose(1, 0, 2).reshape(3, heads)
    grads = {
        'ssm_norm_w': d_ssm_norm, 'ssm_in_w': dw_in[None], 'ssm_conv_w': d_conv_w[None], 'ssm_conv_b': d_conv_b,
        'ssm_dt_bias': hgr[0:1], 'ssm_a_log': hgr[1:2], 'ssm_d': hgr[2:3], 'ssm_gate_norm_w': d_gate,
        'ssm_out_w': dw_out[None], 'kv_norm_w': d_kv_norm[0], 'w_k': dw_k, 'w_v': dw_v, 'attn_norm_w': d_attn_norm,
        'w_q': dw_q[None], 'w_o': dw_o[None], 'ffn_norm_w': jnp.stack([g0['norm'], g1['norm']]),
        'ffn_up_w': jnp.stack([g0['up'], g1['up']]), 'ffn_conv_w': jnp.stack([g0['conv_w'], g1['conv_w']]),
        'ffn_conv_b': jnp.stack([g0['conv_b'], g1['conv_b']]), 'ffn_down_w': jnp.stack([g0['down'], g1['down']]),
        'final_norm_w': d_final[0],
    }
    return loss_p, dx, grads


def kernel(x, ssm_norm_w, ssm_in_w, ssm_conv_w, ssm_conv_b, ssm_dt_bias, ssm_a_log, ssm_d, ssm_gate_norm_w, ssm_out_w, kv_norm_w, w_k, w_v, attn_norm_w, w_q, w_o, ffn_norm_w, ffn_up_w, ffn_conv_w, ffn_conv_b, ffn_down_w, final_norm_w, loss_target, m_ssm_norm_w, m_ssm_in_w, m_ssm_conv_w, m_ssm_conv_b, m_ssm_dt_bias, m_ssm_a_log, m_ssm_d, m_ssm_gate_norm_w, m_ssm_out_w, m_kv_norm_w, m_w_k, m_w_v, m_attn_norm_w, m_w_q, m_w_o, m_ffn_norm_w, m_ffn_up_w, m_ffn_conv_w, m_ffn_conv_b, m_ffn_down_w, m_final_norm_w, v_ssm_norm_w, v_ssm_in_w, v_ssm_conv_w, v_ssm_conv_b, v_ssm_dt_bias, v_ssm_a_log, v_ssm_d, v_ssm_gate_norm_w, v_ssm_out_w, v_kv_norm_w, v_w_k, v_w_v, v_attn_norm_w, v_w_q, v_w_o, v_ffn_norm_w, v_ffn_up_w, v_ffn_conv_w, v_ffn_conv_b, v_ffn_down_w, v_final_norm_w):
    args = (ssm_norm_w, ssm_in_w, ssm_conv_w, ssm_conv_b, ssm_dt_bias, ssm_a_log, ssm_d, ssm_gate_norm_w, ssm_out_w, kv_norm_w, w_k, w_v, attn_norm_w, w_q, w_o, ffn_norm_w, ffn_up_w, ffn_conv_w, ffn_conv_b, ffn_down_w, final_norm_w)
    moms = (m_ssm_norm_w, m_ssm_in_w, m_ssm_conv_w, m_ssm_conv_b, m_ssm_dt_bias, m_ssm_a_log, m_ssm_d, m_ssm_gate_norm_w, m_ssm_out_w, m_kv_norm_w, m_w_k, m_w_v, m_attn_norm_w, m_w_q, m_w_o, m_ffn_norm_w, m_ffn_up_w, m_ffn_conv_w, m_ffn_conv_b, m_ffn_down_w, m_final_norm_w)
    vels = (v_ssm_norm_w, v_ssm_in_w, v_ssm_conv_w, v_ssm_conv_b, v_ssm_dt_bias, v_ssm_a_log, v_ssm_d, v_ssm_gate_norm_w, v_ssm_out_w, v_kv_norm_w, v_w_k, v_w_v, v_attn_norm_w, v_w_q, v_w_o, v_ffn_norm_w, v_ffn_up_w, v_ffn_conv_w, v_ffn_conv_b, v_ffn_down_w, v_final_norm_w)
    local = dict(zip(WEIGHTS, args))
    m_in = dict(zip(WEIGHTS, moms))
    v_in = dict(zip(WEIGHTS, vels))
    c = lax.axis_index("c")

    big16 = _gather_chips(_pack([local[n].reshape(-1) for n in BIG], BF16, 16), name="gather_big")
    small32 = _gather_chips(_pack([local[n].reshape(-1) for n in SMALL], F32, 8), name="gather_small")
    full = {n: local[n] for n in REPLICATED}
    for names, buf in ((BIG, big16), (SMALL, small32)):
        for n, st in zip(names, _unpack(buf, [local[n].shape for n in names])):
            full[n] = _from_shards(st, SHARD_AXIS[n])

    loss_p, dx, grads = _step(x[0], loss_target[0], full)

    sharded = BIG + SMALL
    g4 = _pack([_to_shards(grads[n], SHARD_AXIS[n]).reshape(N_CHIPS, -1) for n in sharded], F32, RS_ROW_MULT)
    rows = g4.shape[1]
    g4h = g4.reshape(N_CHIPS, 2, rows // 2, LANES)
    mine = lax.dynamic_index_in_dim(g4h, c, axis=1, keepdims=False)
    theirs = lax.dynamic_index_in_dim(g4h, 1 - c, axis=1, keepdims=False)
    pair = _add2(mine, _swap_sibling(theirs, name="rs_pair_swap"), name="rs_pair_add")
    half = _sum_leading(_scatter_chips(pair, name="rs_chip_scatter"), name="rs_chip_sum")
    other = _swap_sibling(half, name="rs_half_swap")
    shard = jnp.where(c == 0, jnp.concatenate([half, other], axis=0), jnp.concatenate([other, half], axis=0))
    gshard = dict(zip(sharded, _unpack(shard, [local[n].shape for n in sharded])))

    rep = _pack([loss_p.reshape(-1)] + [grads[n].reshape(-1) for n in REPLICATED], F32, 8)
    tot = _sum_leading(_gather_all(rep, name="ar_gather"), name="ar_sum")
    parts = _unpack(tot, [(LANES,)] + [local[n].shape for n in REPLICATED])
    loss = jnp.sum(parts[0])
    gshard.update(dict(zip(REPLICATED, parts[1:])))

    deltas, new_m, new_v = [], [], []
    for n in WEIGHTS:
        d, nm, nv = _adamw(local[n], gshard[n], m_in[n], v_in[n], name=f"adamw_{n}")
        deltas.append(d)
        new_m.append(nm)
        new_v.append(nv)
    return (loss, dx[None], *[gshard[n] for n in WEIGHTS], *deltas, *new_m, *new_v)
```

```python
import math

import jax
import jax.numpy as jnp
from jax import lax
from jax.experimental import pallas as pl
from jax.experimental.pallas import tpu as pltpu

D_MODEL = 1024
D_INNER = 2048
SSM_HEAD_DIM = 64
SSM_HEADS = 32
SSM_GROUPS = 4
SSM_STATE = 128
SSM_CONV = 4
SSM_CHUNK = 128
GN = SSM_GROUPS * SSM_STATE
CONV_DIM = D_INNER + 2 * GN
SB_HEADS = 16
SB_HEAD_DIM = 64
D_FF = 2816
FFN_CONV = 3
EPS = 1e-6
ADAM_LR = 0.001
ADAM_B1 = 0.9
ADAM_B2 = 0.999
ADAM_EPS = 1e-08
ADAM_WD = 0.01
ADAM_STEP = 10

LANES = 128
SUBLANES = 8
VMEM_LIMIT = 48 * 1024 * 1024
ADAM_BLOCK_BYTES = 1 << 20
F32 = jnp.float32
BF16 = jnp.bfloat16
MESH = pl.DeviceIdType.MESH


def _cparams(sem=None):
    return pltpu.CompilerParams(dimension_semantics=sem, vmem_limit_bytes=VMEM_LIMIT)


def _tile(n, cands):
    for c in cands:
        if n % c == 0:
            return c
    return n


def _nt(a, b):
    return lax.dot_general(a, b, (((1,), (1,)), ((), ())), preferred_element_type=F32)


def _tn(a, b):
    return lax.dot_general(a, b, (((0,), (0,)), ((), ())), preferred_element_type=F32)


def _nn(a, b):
    return jnp.dot(a, b, preferred_element_type=F32)


def _split(x, pieces):
    out = []
    for _ in range(pieces - 1):
        h = x.astype(BF16)
        out.append(h)
        x = x - h.astype(F32)
    out.append(x.astype(BF16))
    return out


def _ones_dot(ones, x, *, ones_left, pieces=3):
    o16 = ones.astype(BF16)
    acc = None
    for piece in _split(x, pieces):
        term = _nn(o16, piece) if ones_left else _nn(piece, o16)
        acc = term if acc is None else acc + term
    return acc


def _softplus(x):
    return jnp.maximum(x, 0.0) + jnp.log(1.0 + jnp.exp(-jnp.abs(x)))


def _sigmoid(x):
    e = jnp.exp(-jnp.abs(x))
    r = 1.0 / (1.0 + e)
    return jnp.where(x >= 0, r, e * r)


MM_TILE_MAX = 1408
MM_VMEM_BUDGET = 40 * 1024 * 1024


def _divisors(n, cap):
    out = [d for d in range(min(cap, n) // LANES * LANES, 0, -LANES) if n % d == 0]
    return out or [n]


def _mm_tiles(m, n, k, a_bytes, b_bytes, o_bytes, add_bytes):
    best = None
    for tm in _divisors(m, MM_TILE_MAX):
        for tn in _divisors(n, MM_TILE_MAX):
            for tk in _divisors(k, MM_TILE_MAX):
                vmem = 2 * (tm * tk * a_bytes + tk * tn * b_bytes + tm * tn * (o_bytes + add_bytes)) + tm * tn * 4
                if vmem > MM_VMEM_BUDGET:
                    continue
                score = (tm * tn * tk, tm * tn)
                if best is None or score > best[0]:
                    best = (score, (tm, tn, tk))
    return best[1]
def _matmul(a, b, *, ta=False, tb=False, add=None, out_dtype=F32, name):
    m, k = (a.shape[1], a.shape[0]) if ta else a.shape
    n = b.shape[0] if tb else b.shape[1]
    assert (b.shape[1] if tb else b.shape[0]) == k
    tm, tn, tk = _mm_tiles(m, n, k, a.dtype.itemsize, b.dtype.itemsize, jnp.dtype(out_dtype).itemsize,
                           0 if add is None else add.dtype.itemsize)
    nk = k // tk

    def body(*refs):
        if add is None:
            a_ref, b_ref, o_ref, acc_ref = refs
            add_ref = None
        else:
            a_ref, b_ref, add_ref, o_ref, acc_ref = refs
        kk = pl.program_id(2)

        @pl.when(kk == 0)
        def _():
            acc_ref[...] = jnp.zeros_like(acc_ref)

        av = a_ref[...].astype(BF16)
        bv = b_ref[...].astype(BF16)
        dn = (((0 if ta else 1,), (1 if tb else 0,)), ((), ()))
        acc_ref[...] += lax.dot_general(av, bv, dn, preferred_element_type=F32)

        @pl.when(kk == nk - 1)
        def _():
            r = acc_ref[...]
            if add_ref is not None:
                r = r + add_ref[...].astype(F32)
            o_ref[...] = r.astype(o_ref.dtype)

    a_spec = pl.BlockSpec((tk, tm), lambda i, j, kk: (kk, i)) if ta else pl.BlockSpec((tm, tk), lambda i, j, kk: (i, kk))
    b_spec = pl.BlockSpec((tn, tk), lambda i, j, kk: (j, kk)) if tb else pl.BlockSpec((tk, tn), lambda i, j, kk: (kk, j))
    in_specs = [a_spec, b_spec]
    args = [a, b]
    if add is not None:
        in_specs.append(pl.BlockSpec((tm, tn), lambda i, j, kk: (i, j)))
        args.append(add)
    return pl.pallas_call(
        body,
        grid=(m // tm, n // tn, nk),
        in_specs=in_specs,
        out_specs=pl.BlockSpec((tm, tn), lambda i, j, kk: (i, j)),
        out_shape=jax.ShapeDtypeStruct((m, n), out_dtype),
        scratch_shapes=[pltpu.VMEM((tm, tn), F32)],
        compiler_params=_cparams(("parallel", "parallel", "arbitrary")),
        name=name,
    )(*args)


def _rmsnorm_fwd(x, w, *, name):
    t, d = x.shape
    tb = _tile(t, (512, 256, 128))

    def body(x_ref, w_ref, o_ref):
        xv = x_ref[...]
        r = lax.rsqrt(jnp.mean(xv * xv, axis=-1, keepdims=True) + EPS)
        o_ref[...] = (xv * r * w_ref[...]).astype(o_ref.dtype)

    return pl.pallas_call(
        body,
        grid=(t // tb,),
        in_specs=[pl.BlockSpec((tb, d), lambda i: (i, 0)), pl.BlockSpec((1, d), lambda i: (0, 0))],
        out_specs=pl.BlockSpec((tb, d), lambda i: (i, 0)),
        out_shape=jax.ShapeDtypeStruct((t, d), BF16),
        compiler_params=_cparams(("parallel",)),
        name=name,
    )(x, w.reshape(1, d))


def _rmsnorm_bwd(x, dys, dres, *, name):
    t, d = x.shape
    tb = _tile(t, (256, 128))
    nn = len(dys)
    has_res = dres is not None

    def body(*refs):
        x_ref = refs[0]
        dy_refs = refs[1:1 + nn]
        w_refs = refs[1 + nn:1 + 2 * nn]
        pos = 1 + 2 * nn
        res_ref = refs[pos] if has_res else None
        pos += 1 if has_res else 0
        dx_ref = refs[pos]
        dw_refs = refs[pos + 1:pos + 1 + nn]
        i = pl.program_id(0)
        xv = x_ref[...]
        r = lax.rsqrt(jnp.mean(xv * xv, axis=-1, keepdims=True) + EPS)
        xn = xv * r
        dx = res_ref[...] if has_res else jnp.zeros_like(xv)
        for q in range(nn):
            dy = dy_refs[q][...].astype(F32)
            g = dy * w_refs[q][...]
            dx = dx + r * (g - xn * jnp.mean(g * xn, axis=-1, keepdims=True))
            dwp = jnp.sum(dy * xn, axis=0, keepdims=True)

            @pl.when(i == 0)
            def _(q=q, dwp=dwp):
                dw_refs[q][...] = dwp

            @pl.when(i > 0)
            def _(q=q, dwp=dwp):
                dw_refs[q][...] += dwp
        dx_ref[...] = dx

    row = pl.BlockSpec((tb, d), lambda i: (i, 0))
    vec = pl.BlockSpec((1, d), lambda i: (0, 0))
    in_specs = [row] + [row] * nn + [vec] * nn + ([row] if has_res else [])
    args = [x] + [p[0] for p in dys] + [p[1].reshape(1, d) for p in dys] + ([dres] if has_res else [])
    outs = pl.pallas_call(
        body,
        grid=(t // tb,),
        in_specs=in_specs,
        out_specs=[row] + [vec] * nn,
        out_shape=[jax.ShapeDtypeStruct((t, d), F32)] + [jax.ShapeDtypeStruct((1, d), F32)] * nn,
        compiler_params=_cparams(("arbitrary",)),
        name=name,
    )(*args)
    return outs[0], list(outs[1:])


def _loss_head(x, w, target, *, name):
    t, d = x.shape
    tb = _tile(t, (256, 128))

    def body(x_ref, w_ref, t_ref, loss_ref, dx_ref, dw_ref):
        i = pl.program_id(0)
        xv = x_ref[...]
        wv = w_ref[...]
        r = lax.rsqrt(jnp.mean(xv * xv, axis=-1, keepdims=True) + EPS)
        xn = xv * r
        e = xn * wv - t_ref[...]
        lp = 0.5 * jnp.sum(jnp.mean(e * e, axis=-1, keepdims=True), axis=0, keepdims=True)
        dy = e * (1.0 / d)
        g = dy * wv
        dx_ref[...] = r * (g - xn * jnp.mean(g * xn, axis=-1, keepdims=True))
        dwp = jnp.sum(dy * xn, axis=0, keepdims=True)
        lpv = jnp.broadcast_to(lp, (1, LANES)) * (1.0 / LANES)

        @pl.when(i == 0)
        def _():
            dw_ref[...] = dwp
            loss_ref[...] = lpv

        @pl.when(i > 0)
        def _():
            dw_ref[...] += dwp
            loss_ref[...] += lpv

    row = pl.BlockSpec((tb, d), lambda i: (i, 0))
    vec = pl.BlockSpec((1, d), lambda i: (0, 0))
    return pl.pallas_call(
        body,
        grid=(t // tb,),
        in_specs=[row, vec, row],
        out_specs=[pl.BlockSpec((1, LANES), lambda i: (0, 0)), row, vec],
        out_shape=[jax.ShapeDtypeStruct((1, LANES), F32), jax.ShapeDtypeStruct((t, d), F32),
                   jax.ShapeDtypeStruct((1, d), F32)],
        compiler_params=_cparams(("arbitrary",)),
        name=name,
    )(x, w.reshape(1, d), target)


ROW_CHUNK = 512
PAD = SUBLANES


def _shifted(pad_ref, r0, rows, back):
    return pad_ref[pl.ds(PAD + r0 - back, rows), :]


def _conv_taps(pad_ref, w_ref, r0, rows, kw):
    acc = None
    for j in range(kw):
        term = _shifted(pad_ref, r0, rows, kw - 1 - j) * w_ref[j:j + 1, :]
        acc = term if acc is None else acc + term
    return acc


def _fill_pad(pad_ref, x_ref, t):
    pad_ref[0:PAD, :] = jnp.zeros((PAD, pad_ref.shape[1]), F32)
    pad_ref[pl.ds(PAD + t, PAD), :] = jnp.zeros((PAD, pad_ref.shape[1]), F32)
    pad_ref[pl.ds(PAD, t), :] = x_ref[...].astype(F32)


def _conv_silu_fwd(x, w, b, *, x_off=0, name):
    t = x.shape[0]
    kw, c = w.shape
    cw = _tile(math.gcd(c, x_off) if x_off else c, (256, 128))
    ob = x_off // cw
    rc = _tile(t, (ROW_CHUNK,))

    def body(x_ref, w_ref, b_ref, o_ref, pad_ref):
        _fill_pad(pad_ref, x_ref, t)
        for r0 in range(0, t, rc):
            pre = _conv_taps(pad_ref, w_ref, r0, rc, kw) + b_ref[...]
            o_ref[pl.ds(r0, rc), :] = pre * _sigmoid(pre)

    strip = pl.BlockSpec((t, cw), lambda i: (0, i))
    return pl.pallas_call(
        body,
        grid=(c // cw,),
        in_specs=[pl.BlockSpec((t, cw), lambda i: (0, i + ob)), pl.BlockSpec((kw, cw), lambda i: (0, i)),
                  pl.BlockSpec((1, cw), lambda i: (0, i))],
        out_specs=strip,
        out_shape=jax.ShapeDtypeStruct((t, c), F32),
        scratch_shapes=[pltpu.VMEM((t + 2 * PAD, cw), F32)],
        compiler_params=_cparams(("parallel",)),
        name=name,
    )(x, w, b.reshape(1, c))


def _conv_bwd_core(dpre_pad_ref, x_pad_ref, w_ref, dx_ref, dw_ref, db_ref, t, rc, kw):
    cw = dx_ref.shape[1]
    dws = [jnp.zeros((1, cw), F32) for _ in range(kw)]
    dbs = jnp.zeros((1, cw), F32)
    for r0 in range(0, t, rc):
        dpre = dpre_pad_ref[pl.ds(PAD + r0, rc), :]
        dx = None
        for j in range(kw):
            s = kw - 1 - j
            term = dpre_pad_ref[pl.ds(PAD + r0 + s, rc), :] * w_ref[j:j + 1, :]
            dx = term if dx is None else dx + term
            dws[j] = dws[j] + jnp.sum(dpre * _shifted(x_pad_ref, r0, rc, s), axis=0, keepdims=True)
        dbs = dbs + jnp.sum(dpre, axis=0, keepdims=True)
        dx_ref[pl.ds(r0, rc), :] = dx
    for j in range(kw):
        dw_ref[j:j + 1, :] = dws[j]
    db_ref[...] = dbs


def _conv_silu_bwd(x, w, b, dact, *, x_off=0, name):
    t = x.shape[0]
    kw, c = w.shape
    cw = _tile(math.gcd(c, x_off) if x_off else c, (256, 128))
    ob = x_off // cw
    rc = _tile(t, (ROW_CHUNK,))

    def body(x_ref, w_ref, b_ref, da_ref, dx_ref, dw_ref, db_ref, xpad_ref, dpad_ref):
        _fill_pad(xpad_ref, x_ref, t)
        dpad_ref[0:PAD, :] = jnp.zeros((PAD, cw), F32)
        dpad_ref[pl.ds(PAD + t, PAD), :] = jnp.zeros((PAD, cw), F32)
        for r0 in range(0, t, rc):
            pre = _conv_taps(xpad_ref, w_ref, r0, rc, kw) + b_ref[...]
            sg = _sigmoid(pre)
            dpad_ref[pl.ds(PAD + r0, rc), :] = da_ref[pl.ds(r0, rc), :] * (sg * (1.0 + pre * (1.0 - sg)))
        _conv_bwd_core(dpad_ref, xpad_ref, w_ref, dx_ref, dw_ref, db_ref, t, rc, kw)

    strip = pl.BlockSpec((t, cw), lambda i: (0, i))
    wspec = pl.BlockSpec((kw, cw), lambda i: (0, i))
    bspec = pl.BlockSpec((1, cw), lambda i: (0, i))
    return pl.pallas_call(
        body,
        grid=(c // cw,),
        in_specs=[pl.BlockSpec((t, cw), lambda i: (0, i + ob)), wspec, bspec, strip],
        out_specs=[strip, wspec, bspec],
        out_shape=[jax.ShapeDtypeStruct((t, c), F32), jax.ShapeDtypeStruct((kw, c), F32),
                   jax.ShapeDtypeStruct((1, c), F32)],
        scratch_shapes=[pltpu.VMEM((t + 2 * PAD, cw), F32), pltpu.VMEM((t + 2 * PAD, cw), F32)],
        compiler_params=_cparams(("parallel",)),
        name=name,
    )(x, w, b.reshape(1, c), dact)


def _conv_glu_fwd(hid, w, b, *, name):
    t, c2 = hid.shape
    f = c2 // 2
    kw = w.shape[0]
    cw = _tile(f, (256, 128))
    nf = f // cw
    rc = _tile(t, (ROW_CHUNK,))

    def body(g_ref, v_ref, wg_ref, wv_ref, bg_ref, bv_ref, o_ref, gpad_ref, vpad_ref):
        _fill_pad(gpad_ref, g_ref, t)
        _fill_pad(vpad_ref, v_ref, t)
        for r0 in range(0, t, rc):
            gate = _conv_taps(gpad_ref, wg_ref, r0, rc, kw) + bg_ref[...]
            val = _conv_taps(vpad_ref, wv_ref, r0, rc, kw) + bv_ref[...]
            o_ref[pl.ds(r0, rc), :] = (gate * _sigmoid(gate) * val).astype(o_ref.dtype)

    gs = pl.BlockSpec((t, cw), lambda i: (0, i))
    vs = pl.BlockSpec((t, cw), lambda i: (0, i + nf))
    b2 = b.reshape(1, c2)
    return pl.pallas_call(
        body,
        grid=(nf,),
        in_specs=[gs, vs, pl.BlockSpec((kw, cw), lambda i: (0, i)), pl.BlockSpec((kw, cw), lambda i: (0, i + nf)),
                  pl.BlockSpec((1, cw), lambda i: (0, i)), pl.BlockSpec((1, cw), lambda i: (0, i + nf))],
        out_specs=gs,
        out_shape=jax.ShapeDtypeStruct((t, f), BF16),
        scratch_shapes=[pltpu.VMEM((t + 2 * PAD, cw), F32), pltpu.VMEM((t + 2 * PAD, cw), F32)],
        compiler_params=_cparams(("parallel",)),
        name=name,
    )(hid, hid, w, w, b2, b2)


def _conv_glu_bwd(hid, w, b, dact, *, name):
    t, c2 = hid.shape
    f = c2 // 2
    kw = w.shape[0]
    cw = _tile(f, (128,))
    nf = f // cw
    rc = _tile(t, (ROW_CHUNK,))

    def body(g_ref, v_ref, wg_ref, wv_ref, bg_ref, bv_ref, da_ref,
             dg_ref, dv_ref, dwg_ref, dwv_ref, dbg_ref, dbv_ref,
             gpad_ref, vpad_ref, dgpad_ref, dvpad_ref):
        _fill_pad(gpad_ref, g_ref, t)
        _fill_pad(vpad_ref, v_ref, t)
        for ref in (dgpad_ref, dvpad_ref):
            ref[0:PAD, :] = jnp.zeros((PAD, cw), F32)
            ref[pl.ds(PAD + t, PAD), :] = jnp.zeros((PAD, cw), F32)
        for r0 in range(0, t, rc):
            gate = _conv_taps(gpad_ref, wg_ref, r0, rc, kw) + bg_ref[...]
            val = _conv_taps(vpad_ref, wv_ref, r0, rc, kw) + bv_ref[...]
            sg = _sigmoid(gate)
            da = da_ref[pl.ds(r0, rc), :].astype(F32)
            dgpad_ref[pl.ds(PAD + r0, rc), :] = da * val * (sg * (1.0 + gate * (1.0 - sg)))
            dvpad_ref[pl.ds(PAD + r0, rc), :] = da * (gate * sg)
        _conv_bwd_core(dgpad_ref, gpad_ref, wg_ref, dg_ref, dwg_ref, dbg_ref, t, rc, kw)
        _conv_bwd_core(dvpad_ref, vpad_ref, wv_ref, dv_ref, dwv_ref, dbv_ref, t, rc, kw)

    gs = pl.BlockSpec((t, cw), lambda i: (0, i))
    vs = pl.BlockSpec((t, cw), lambda i: (0, i + nf))
    wg = pl.BlockSpec((kw, cw), lambda i: (0, i))
    wv = pl.BlockSpec((kw, cw), lambda i: (0, i + nf))
    bg = pl.BlockSpec((1, cw), lambda i: (0, i))
    bv = pl.BlockSpec((1, cw), lambda i: (0, i + nf))
    b2 = b.reshape(1, c2)
    pad = pltpu.VMEM((t + 2 * PAD, cw), F32)
    return pl.pallas_call(
        body,
        grid=(nf,),
        in_specs=[gs, vs, wg, wv, bg, bv, gs],
        out_specs=[gs, gs, wg, wg, bg, bg],
        out_shape=[jax.ShapeDtypeStruct((t, f), F32), jax.ShapeDtypeStruct((t, f), F32),
                   jax.ShapeDtypeStruct((kw, f), F32), jax.ShapeDtypeStruct((kw, f), F32),
                   jax.ShapeDtypeStruct((1, f), F32), jax.ShapeDtypeStruct((1, f), F32)],
        scratch_shapes=[pad, pad, pad, pad],
        compiler_params=_cparams(("parallel",)),
        name=name,
    )(hid, hid, w, w, b2, b2, dact)


def _gate_norm_fwd(y, zx, w, *, name):
    t, di = y.shape
    gsz = di // SSM_GROUPS
    tb = _tile(t, (256, 128))

    def body(y_ref, z_ref, w_ref, o_ref):
        for g in range(SSM_GROUPS):
            sl = slice(g * gsz, (g + 1) * gsz)
            zv = z_ref[:, sl]
            gv = y_ref[:, sl] * (zv * _sigmoid(zv))
            r = lax.rsqrt(jnp.mean(gv * gv, axis=-1, keepdims=True) + EPS)
            o_ref[:, sl] = (gv * r * w_ref[:, sl]).astype(o_ref.dtype)

    row = pl.BlockSpec((tb, di), lambda i: (i, 0))
    return pl.pallas_call(
        body,
        grid=(t // tb,),
        in_specs=[row, row, pl.BlockSpec((1, di), lambda i: (0, 0))],
        out_specs=row,
        out_shape=jax.ShapeDtypeStruct((t, di), BF16),
        compiler_params=_cparams(("parallel",)),
        name=name,
    )(y, zx, w.reshape(1, di))


def _gate_norm_bwd(y, zx, w, dyn, *, name):
    t, di = y.shape
    gsz = di // SSM_GROUPS
    tb = _tile(t, (256, 128))

    def body(y_ref, z_ref, w_ref, d_ref, dy_ref, dz_ref, dw_ref):
        i = pl.program_id(0)
        for g in range(SSM_GROUPS):
            sl = slice(g * gsz, (g + 1) * gsz)
            zv = z_ref[:, sl]
            yv = y_ref[:, sl]
            sg = _sigmoid(zv)
            sz = zv * sg
            gv = yv * sz
            r = lax.rsqrt(jnp.mean(gv * gv, axis=-1, keepdims=True) + EPS)
            gn = gv * r
            dn = d_ref[:, sl].astype(F32)
            q = dn * w_ref[:, sl]
            dg = r * (q - gn * jnp.mean(q * gn, axis=-1, keepdims=True))
            dy_ref[:, sl] = dg * sz
            dz_ref[:, sl] = dg * yv * (sg * (1.0 + zv * (1.0 - sg)))
            dwp = jnp.sum(dn * gn, axis=0, keepdims=True)

            @pl.when(i == 0)
            def _(sl=sl, dwp=dwp):
                dw_ref[:, sl] = dwp

            @pl.when(i > 0)
            def _(sl=sl, dwp=dwp):
                dw_ref[:, sl] += dwp

    row = pl.BlockSpec((tb, di), lambda i: (i, 0))
    vec = pl.BlockSpec((1, di), lambda i: (0, 0))
    return pl.pallas_call(
        body,
        grid=(t // tb,),
        in_specs=[row, row, vec, row],
        out_specs=[row, row, vec],
        out_shape=[jax.ShapeDtypeStruct((t, di), F32), jax.ShapeDtypeStruct((t, di), F32),
                   jax.ShapeDtypeStruct((1, di), F32)],
        compiler_params=_cparams(("arbitrary",)),
        name=name,
    )(y, zx, w.reshape(1, di), dyn)


def _adamw(w, g, m, v, *, name):
    shape = w.shape
    cols = shape[-1]
    rows = w.size // cols
    w2, g2, m2, v2 = (a.reshape(rows, cols) for a in (w, g, m, v))
    tr = rows
    if rows * cols * 4 > ADAM_BLOCK_BYTES:
        tr = _tile(rows, tuple(r for r in (512, 256, 128, 64, 32, 16, 8) if r * cols * 4 <= ADAM_BLOCK_BYTES))
    c1 = 1.0 - ADAM_B1 ** ADAM_STEP
    c2 = 1.0 - ADAM_B2 ** ADAM_STEP

    def body(w_ref, g_ref, m_ref, v_ref, d_ref, nm_ref, nv_ref):
        gv = g_ref[...]
        nm = ADAM_B1 * m_ref[...] + (1.0 - ADAM_B1) * gv
        nv = ADAM_B2 * v_ref[...] + (1.0 - ADAM_B2) * (gv * gv)
        d_ref[...] = -ADAM_LR * ((nm / c1) / (jnp.sqrt(nv / c2) + ADAM_EPS) + ADAM_WD * w_ref[...])
        nm_ref[...] = nm
        nv_ref[...] = nv

    blk = pl.BlockSpec((tr, cols), lambda i: (i, 0))
    outs = pl.pallas_call(
        body,
        grid=(rows // tr,),
        in_specs=[blk] * 4,
        out_specs=[blk] * 3,
        out_shape=[jax.ShapeDtypeStruct((rows, cols), F32)] * 3,
        compiler_params=_cparams(("parallel",)),
        name=name,
    )(w2, g2, m2, v2)
    return tuple(o.reshape(shape) for o in outs)


def _ssd_scalars(dtc_ref, dtr_ref, hpc_ref, hpr_ref, ln):
    bias_c, alog_c = hpc_ref[0, 0:1, :], hpc_ref[0, 1:2, :]
    bias_r, alog_r = hpr_ref[0, :, 0:1], hpr_ref[0, :, 1:2]
    a_c, a_r = -jnp.exp(alog_c), -jnp.exp(alog_r)
    raw_c = dtc_ref[0] + bias_c
    dt_c = _softplus(raw_c)
    dt_r = _softplus(dtr_ref[0] + bias_r)
    row = lax.broadcasted_iota(jnp.int32, (ln, ln), 0)
    col = lax.broadcasted_iota(jnp.int32, (ln, ln), 1)
    lower = (col <= row).astype(F32)
    upper = (row <= col).astype(F32)
    acs_c = _ones_dot(lower, dt_c * a_c, ones_left=True)
    acs_r = _ones_dot(upper, dt_r * a_r, ones_left=False)
    return raw_c, dt_c, a_c, acs_c, acs_r, row, col


def _ssd_specs(t, di, g_n, n_st, rp, ln, r_h, rev):
    nc = t // ln
    cidx = (lambda c: nc - 1 - c) if rev else (lambda c: c)
    xs = pl.BlockSpec((ln, rp), lambda g, c: (cidx(c), g))
    bm = pl.BlockSpec((ln, n_st), lambda g, c: (cidx(c), di // n_st + g))
    cm = pl.BlockSpec((ln, n_st), lambda g, c: (cidx(c), di // n_st + g_n + g))
    dtc = pl.BlockSpec((1, ln, r_h), lambda g, c: (g, cidx(c), 0))
    dtr = pl.BlockSpec((1, r_h, ln), lambda g, c: (g, 0, cidx(c)))
    hpc = pl.BlockSpec((1, 3, r_h), lambda g, c: (g, 0, 0))
    hpr = pl.BlockSpec((1, r_h, 3), lambda g, c: (g, 0, 0))
    prev = pl.BlockSpec((1, rp, n_st), lambda g, c: (cidx(c), g, 0))
    return xs, bm, cm, dtc, dtr, hpc, hpr, prev


def _ssd_fwd(xbc, dtc, dtr, hpc, hpr, *, name):
    t = xbc.shape[0]
    di, g_n, n_st, p_h, ln = D_INNER, SSM_GROUPS, SSM_STATE, SSM_HEAD_DIM, SSM_CHUNK
    r_h = SSM_HEADS // g_n
    rp = r_h * p_h
    nc = t // ln

    def body(xs_ref, b_ref, c_ref, dtc_ref, dtr_ref, hpc_ref, hpr_ref, y_ref, prev_ref, st_ref):
        @pl.when(pl.program_id(1) == 0)
        def _():
            st_ref[...] = jnp.zeros_like(st_ref)

        _, dt_c, _, acs_c, acs_r, row, col = _ssd_scalars(dtc_ref, dtr_ref, hpc_ref, hpr_ref, ln)
        bm = b_ref[...]
        cm = c_ref[...]
        cm16 = cm.astype(BF16)
        cb = _nt(cm16, bm.astype(BF16))
        causal = row >= col
        for r in range(r_h):
            sl = slice(r * p_h, (r + 1) * p_h)
            xs = xs_ref[:, sl]
            acs = acs_c[:, r:r + 1]
            last = acs_c[ln - 1:ln, r:r + 1]
            lm = jnp.where(causal, jnp.exp(acs - acs_r[r:r + 1, :]), 0.0)
            xd = (xs * dt_c[:, r:r + 1]).astype(BF16)
            prev = st_ref[sl, :]
            y = _nn((cb * lm).astype(BF16), xd)
            y = y + _nt(cm16, prev.astype(BF16)) * jnp.exp(acs)
            y_ref[:, sl] = y + hpc_ref[0, 2:3, r:r + 1] * xs
            prev_ref[0, sl, :] = prev
            bd = (bm * jnp.exp(last - acs)).astype(BF16)
            st_ref[sl, :] = prev * jnp.exp(last) + _tn(xd, bd)

    xs, bm, cm, dtcs, dtrs, hpcs, hprs, prev = _ssd_specs(t, di, g_n, n_st, rp, ln, r_h, False)
    return pl.pallas_call(
        body,
        grid=(g_n, nc),
        in_specs=[xs, bm, cm, dtcs, dtrs, hpcs, hprs],
        out_specs=[xs, prev],
        out_shape=[jax.ShapeDtypeStruct((t, di), F32), jax.ShapeDtypeStruct((nc, g_n * rp, n_st), F32)],
        scratch_shapes=[pltpu.VMEM((rp, n_st), F32)],
        compiler_params=_cparams(("parallel", "arbitrary")),
        name=name,
    )(xbc, xbc, xbc, dtc, dtr, hpc, hpr)


def _ssd_bwd(xbc, dtc, dtr, hpc, hpr, prev, dy, *, name):
    t = xbc.shape[0]
    di, g_n, n_st, p_h, ln = D_INNER, SSM_GROUPS, SSM_STATE, SSM_HEAD_DIM, SSM_CHUNK
    r_h = SSM_HEADS // g_n
    rp = r_h * p_h
    nc = t // ln

    def body(xs_ref, b_ref, c_ref, dtc_ref, dtr_ref, hpc_ref, hpr_ref, prev_ref, dy_ref,
             dxs_ref, db_ref, dc_ref, ddt_ref, hg_ref, ds_ref):
        step = pl.program_id(1)

        @pl.when(step == 0)
        def _():
            ds_ref[...] = jnp.zeros_like(ds_ref)

        raw_c, dt_c, a_c, acs_c, acs_r, row, col = _ssd_scalars(dtc_ref, dtr_ref, hpc_ref, hpr_ref, ln)
        bm = b_ref[...]
        cm = c_ref[...]
        bm16, cm16 = bm.astype(BF16), cm.astype(BF16)
        cb = _nt(cm16, bm16)
        cbt = _nt(bm16, cm16)
        lane_r = lax.broadcasted_iota(jnp.int32, (ln, r_h), 1)
        sub_r = lax.broadcasted_iota(jnp.int32, (ln, r_h), 0)
        dacs_all = jnp.zeros((ln, r_h), F32)
        ddtx_all = jnp.zeros((ln, r_h), F32)
        dd_all = jnp.zeros((ln, r_h), F32)
        dcb = jnp.zeros((ln, ln), F32)
        dcbt = jnp.zeros((ln, ln), F32)
        dc_acc = jnp.zeros((ln, n_st), F32)
        db_acc = jnp.zeros((ln, n_st), F32)
        for r in range(r_h):
            sl = slice(r * p_h, (r + 1) * p_h)
            xs = xs_ref[:, sl]
            dyv = dy_ref[:, sl]
            dy16 = dyv.astype(BF16)
            dtv = dt_c[:, r:r + 1]
            acs = acs_c[:, r:r + 1]
            acsr = acs_r[r:r + 1, :]
            last = acs_c[ln - 1:ln, r:r + 1]
            xd = xs * dtv
            xd16 = xd.astype(BF16)
            lm = jnp.where(row >= col, jnp.exp(acs - acsr), 0.0)
            lmt = jnp.where(col >= row, jnp.exp(acsr - acs), 0.0)
            m_ls = cb * lm
            m_sl = cbt * lmt
            dm = _nt(dy16, xd16)
            dmt = _nt(xd16, dy16)
            dxd = _nn(m_sl.astype(BF16), dy16)
            dacs = jnp.sum(dm * m_ls, axis=1, keepdims=True) - jnp.sum(dmt * m_sl, axis=1, keepdims=True)
            dcb = dcb + dm * lm
            dcbt = dcbt + dmt * lmt
            prev = prev_ref[0, sl, :]
            prev16 = prev.astype(BF16)
            e = jnp.exp(acs)
            y_off = _nt(cm16, prev16) * e
            dacs = dacs + jnp.sum(dyv * y_off, axis=1, keepdims=True)
            dyo16 = (dyv * e).astype(BF16)
            dc_acc = dc_acc + _nn(dyo16, prev16)
            dprev = _tn(dyo16, cm16)
            ds = ds_ref[sl, :]
            ds16 = ds.astype(BF16)
            decay = jnp.exp(last - acs)
            bd16 = (bm * decay).astype(BF16)
            dbd = _nn(xd16, ds16)
            dxd = dxd + _nt(bd16, ds16)
            db_acc = db_acc + dbd * decay
            tdec = jnp.sum(dbd * bm, axis=1, keepdims=True) * decay
            dacs = dacs - tdec
            cd = jnp.exp(last)
            dlast = jnp.sum(tdec, axis=0, keepdims=True) + jnp.sum(jnp.sum(prev * ds, axis=1, keepdims=True), axis=0, keepdims=True) * cd
            ds_ref[sl, :] = dprev + cd * ds
            dskip = hpc_ref[0, 2:3, r:r + 1]
            dxs_ref[:, sl] = dxd * dtv + dskip * dyv
            ddtx = jnp.sum(dxd * xs, axis=1, keepdims=True)
            ddv = jnp.sum(dyv * xs, axis=1, keepdims=True)
            dacs = dacs + jnp.where(sub_r[:, 0:1] == ln - 1, dlast, 0.0)
            dacs_all = jnp.where(lane_r == r, dacs, dacs_all)
            ddtx_all = jnp.where(lane_r == r, ddtx, ddtx_all)
            dd_all = jnp.where(lane_r == r, ddv, dd_all)
        dc_ref[...] = dc_acc + _nn(dcb.astype(BF16), bm16)
        db_ref[...] = db_acc + _nn(dcbt.astype(BF16), cm16)
        upper = (row <= col).astype(F32)
        dad = _ones_dot(upper, dacs_all, ones_left=True)
        ddt = dad * a_c + ddtx_all
        ddt_raw = ddt * _sigmoid(raw_c)
        ddt_ref[0] = ddt_raw
        d_bias = jnp.sum(ddt_raw, axis=0, keepdims=True)
        d_alog = jnp.sum(dad * dt_c, axis=0, keepdims=True) * a_c
        d_d = jnp.sum(dd_all, axis=0, keepdims=True)
        hg = jnp.concatenate([d_bias, d_alog, d_d], axis=0)

        @pl.when(step == 0)
        def _():
            hg_ref[0] = hg

        @pl.when(step > 0)
        def _():
            hg_ref[0] += hg

    xs, bms, cms, dtcs, dtrs, hpcs, hprs, prevs = _ssd_specs(t, di, g_n, n_st, rp, ln, r_h, True)
    bout = pl.BlockSpec((ln, n_st), lambda g, c: (nc - 1 - c, g))
    return pl.pallas_call(
        body,
        grid=(g_n, nc),
        in_specs=[xs, bms, cms, dtcs, dtrs, hpcs, hprs, prevs, xs],
        out_specs=[xs, bout, bout, dtcs, hpcs],
        out_shape=[jax.ShapeDtypeStruct((t, di), F32), jax.ShapeDtypeStruct((t, g_n * n_st), F32),
                   jax.ShapeDtypeStruct((t, g_n * n_st), F32), jax.ShapeDtypeStruct((g_n, t, r_h), F32),
                   jax.ShapeDtypeStruct((g_n, 3, r_h), F32)],
        scratch_shapes=[pltpu.VMEM((rp, n_st), F32)],
        compiler_params=_cparams(("parallel", "arbitrary")),
        name=name,
    )(xbc, xbc, xbc, dtc, dtr, hpc, hpr, prev, dy)


SB_KEYS = 128
SB_QUERIES = (512, 256, 128)
SB_PIECES = 2


def _sb_logits(qs, kv, valid):
    z = _nt(qs, kv)
    sp = _softplus(z)
    lg = -sp if valid is None else jnp.where(valid, -sp, 0.0)
    return z - sp, lg


def _sb_iota(tq):
    diff = lax.broadcasted_iota(jnp.int32, (tq, SB_KEYS), 1) - lax.broadcasted_iota(jnp.int32, (tq, SB_KEYS), 0)
    krow = lax.broadcasted_iota(jnp.int32, (SB_KEYS, SB_KEYS), 0)
    kcol = lax.broadcasted_iota(jnp.int32, (SB_KEYS, SB_KEYS), 1)
    return diff, krow, kcol


def _sb_scale(d):
    scale = 1.0 / math.sqrt(d)
    assert math.frexp(scale)[0] == 0.5, "the scale is folded into bf16 queries: it must be a power of two"
    return scale


def _key_rows(j):
    return pl.ds(pl.multiple_of(j * SB_KEYS, SB_KEYS), SB_KEYS)


def _pairs(tiles, per_tile, one, carry):
    if per_tile % 2:
        return lax.fori_loop(0, tiles * per_tile, one, carry)
    return lax.fori_loop(0, tiles * (per_tile // 2), lambda s, cr: one(2 * s + 1, one(2 * s, cr)), carry)


def _sb_fwd(q, k, v, *, name):
    h, t, d = q.shape
    tq = _tile(t, SB_QUERIES)
    nq = t // tq
    kpq = tq // SB_KEYS
    scale = _sb_scale(d)

    def body(q_ref, k_ref, v_ref, o_ref, lt_ref):
        i = pl.program_id(1)
        qs = (q_ref[0].astype(F32) * scale).astype(BF16)
        diff, krow, kcol = _sb_iota(tq)
        later = (krow > kcol).astype(F32)

        def block(j, carry, valid):
            acc, cl = carry
            rows = _key_rows(j)
            ls, lg = _sb_logits(qs, k_ref[0, rows, :], valid)
            cs = _ones_dot(later, lg, ones_left=False, pieces=SB_PIECES)
            att = jnp.exp(ls + (cs + cl))
            if valid is not None:
                att = jnp.where(valid, att, 0.0)
            acc = acc + _nn(att.astype(BF16), v_ref[0, rows, :])
            return acc, cl + (cs[:, 0:1] + lg[:, 0:1])

        carry = (jnp.zeros((tq, d), F32), jnp.zeros((tq, 1), F32))
        for m in range(kpq - 1, -1, -1):
            carry = block(i * kpq + m, carry, diff < -m * SB_KEYS)
        nb = i * kpq
        acc, cl = _pairs(i, kpq, lambda s, cr: block(nb - 1 - s, cr, None), carry)
        o_ref[0] = acc
        lt_ref[0] = cl

    qs = pl.BlockSpec((1, tq, d), lambda hh, i: (hh, i, 0))
    ls = pl.BlockSpec((1, tq, 1), lambda hh, i: (hh, i, 0))
    ks = pl.BlockSpec((1, t, d), lambda hh, i: (hh, 0, 0))
    return pl.pallas_call(
        body,
        grid=(h, nq),
        in_specs=[qs, ks, ks],
        out_specs=[qs, ls],
        out_shape=[jax.ShapeDtypeStruct((h, t, d), F32), jax.ShapeDtypeStruct((h, t, 1), F32)],
        compiler_params=_cparams(("parallel", "arbitrary")),
        name=name,
    )(q, k, v)


def _sb_bwd(q, k, v, lt, do, *, name):
    h, t, d = q.shape
    tq = _tile(t, SB_QUERIES)
    nq = t // tq
    kpq = tq // SB_KEYS
    scale = _sb_scale(d)
    last = SB_KEYS - 1

    def body(q_ref, k_ref, v_ref, lt_ref, do_ref, dq_ref, dk_ref, dv_ref):
        i = pl.program_id(1)

        @pl.when(i == 0)
        def _():
            dk_ref[...] = jnp.zeros_like(dk_ref)
            dv_ref[...] = jnp.zeros_like(dv_ref)

        qs = (q_ref[0].astype(F32) * scale).astype(BF16)
        do16 = do_ref[0].astype(BF16)
        ltot = lt_ref[0]
        diff, krow, kcol = _sb_iota(tq)
        upto = (krow <= kcol).astype(F32)
        before = (krow < kcol).astype(F32)

        def block(j, carry, valid):
            dq, pl_sum, pg_sum = carry
            rows = _key_rows(j)
            kv = k_ref[0, rows, :]
            vv = v_ref[0, rows, :]
            ls, lg = _sb_logits(qs, kv, valid)
            pre = _ones_dot(upto, lg, ones_left=False, pieces=SB_PIECES)
            att = jnp.exp(ls + (ltot - (pre + pl_sum)))
            if valid is not None:
                att = jnp.where(valid, att, 0.0)
            g = att * _nt(do16, vv)
            gpre = _ones_dot(before, g, ones_left=False, pieces=SB_PIECES)
            sig = jnp.exp(ls)
            dz16 = (g - sig * (g + (gpre + pg_sum))).astype(BF16)
            if valid is not None:
                dz16 = jnp.where(valid, dz16, jnp.zeros_like(dz16))
            dq = dq + _nn(dz16, kv)
            dk_ref[0, rows, :] += _tn(dz16, qs)
            dv_ref[0, rows, :] += _tn(att.astype(BF16), do16)
            return dq, pl_sum + pre[:, last:], pg_sum + (gpre[:, last:] + g[:, last:])

        zero = jnp.zeros((tq, 1), F32)
        nb = i * kpq
        carry = _pairs(i, kpq, lambda j, cr: block(j, cr, None), (jnp.zeros((tq, d), F32), zero, zero))
        for m in range(kpq):
            carry = block(nb + m, carry, diff < -m * SB_KEYS)
        dq_ref[0] = carry[0] * scale

    qs = pl.BlockSpec((1, tq, d), lambda hh, i: (hh, i, 0))
    ls = pl.BlockSpec((1, tq, 1), lambda hh, i: (hh, i, 0))
    ks = pl.BlockSpec((1, t, d), lambda hh, i: (hh, 0, 0))
    full = jax.ShapeDtypeStruct((h, t, d), F32)
    return pl.pallas_call(
        body,
        grid=(h, nq),
        in_specs=[qs, ks, ks, ls, qs],
        out_specs=[qs, ks, ks],
        out_shape=[full, full, full],
        compiler_params=_cparams(("parallel", "arbitrary")),
        name=name,
    )(q, k, v, lt, do)


def _row_tile(rows, cols):
    return _tile(rows, tuple(r for r in (2048, 1024, 512, 256, 128, 64, 32, 16, 8) if r * cols * 4 <= ADAM_BLOCK_BYTES))


def _sum_leading(x, *, name):
    n, rows, cols = x.shape
    tr = _row_tile(rows, cols)

    def body(x_ref, o_ref):
        acc = x_ref[0].astype(F32)
        for q in range(1, n):
            acc = acc + x_ref[q].astype(F32)
        o_ref[...] = acc

    return pl.pallas_call(
        body,
        grid=(rows // tr,),
        in_specs=[pl.BlockSpec((n, tr, cols), lambda i: (0, i, 0))],
        out_specs=pl.BlockSpec((tr, cols), lambda i: (i, 0)),
        out_shape=jax.ShapeDtypeStruct((rows, cols), F32),
        compiler_params=_cparams(("parallel",)),
        name=name,
    )(x)


def _pair_add(g4h, recv, c, *, out_dtype, name):
    n, _, rows, cols = g4h.shape
    tr = _row_tile(rows, cols)

    def body(c_ref, g_ref, r_ref, o_ref):
        o_ref[...] = (g_ref[...] + r_ref[...]).astype(o_ref.dtype)

    blk = pl.BlockSpec((1, tr, cols), lambda q, i, c_ref: (q, i, 0))
    return pl.pallas_call(
        body,
        grid_spec=pltpu.PrefetchScalarGridSpec(
            num_scalar_prefetch=1,
            grid=(n, rows // tr),
            in_specs=[pl.BlockSpec((1, None, tr, cols), lambda q, i, c_ref: (q, c_ref[0], i, 0)), blk],
            out_specs=blk),
        out_shape=jax.ShapeDtypeStruct((n, rows, cols), out_dtype),
        compiler_params=_cparams(("parallel", "parallel")),
        name=name,
    )(c.reshape(1).astype(jnp.int32), g4h, recv)


ANY = pl.BlockSpec(memory_space=pl.ANY)


def _other_chips(x, y):
    return [(1 - x, y), (x, 1 - y), (1 - x, 1 - y)]


def _gather_chips(shard, *, name):
    def body(x_ref, o_ref, send_sems, recv_sems, local_sem):
        x, y, c = lax.axis_index("x"), lax.axis_index("y"), lax.axis_index("c")
        me = 2 * x + y
        mine = pltpu.make_async_copy(x_ref, o_ref.at[me], local_sem)
        mine.start()
        chips = _other_chips(x, y)
        sends = [pltpu.make_async_remote_copy(src_ref=x_ref, dst_ref=o_ref.at[me], send_sem=send_sems.at[q],
                                              recv_sem=recv_sems.at[q], device_id=(px, py, c), device_id_type=MESH)
                 for q, (px, py) in enumerate(chips)]
        for cp in sends:
            cp.start()
        for q, (px, py) in enumerate(chips):
            pltpu.make_async_remote_copy(src_ref=x_ref, dst_ref=o_ref.at[2 * px + py], send_sem=send_sems.at[q],
                                         recv_sem=recv_sems.at[q], device_id=(px, py, c), device_id_type=MESH).wait_recv()
        for cp in sends:
            cp.wait_send()
        mine.wait()

    return pl.pallas_call(
        body,
        in_specs=[ANY],
        out_specs=ANY,
        out_shape=jax.ShapeDtypeStruct((4,) + shard.shape, shard.dtype),
        scratch_shapes=[pltpu.SemaphoreType.DMA((3,)), pltpu.SemaphoreType.DMA((3,)), pltpu.SemaphoreType.DMA],
        compiler_params=pltpu.CompilerParams(has_side_effects=True),
        name=name,
    )(shard)


def _scatter_chips(parts, *, name):
    def body(p_ref, o_ref, send_sems, recv_sems, local_sem):
        x, y, c = lax.axis_index("x"), lax.axis_index("y"), lax.axis_index("c")
        me = 2 * x + y
        mine = pltpu.make_async_copy(p_ref.at[me], o_ref.at[me], local_sem)
        mine.start()
        chips = _other_chips(x, y)
        sends = [pltpu.make_async_remote_copy(src_ref=p_ref.at[2 * px + py], dst_ref=o_ref.at[me], send_sem=send_sems.at[q],
                                              recv_sem=recv_sems.at[q], device_id=(px, py, c), device_id_type=MESH)
                 for q, (px, py) in enumerate(chips)]
        for cp in sends:
            cp.start()
        for q, (px, py) in enumerate(chips):
            pltpu.make_async_remote_copy(src_ref=p_ref.at[me], dst_ref=o_ref.at[2 * px + py], send_sem=send_sems.at[q],
                                         recv_sem=recv_sems.at[q], device_id=(px, py, c), device_id_type=MESH).wait_recv()
        for cp in sends:
            cp.wait_send()
        mine.wait()

    return pl.pallas_call(
        body,
        in_specs=[ANY],
        out_specs=ANY,
        out_shape=jax.ShapeDtypeStruct(parts.shape, parts.dtype),
        scratch_shapes=[pltpu.SemaphoreType.DMA((3,)), pltpu.SemaphoreType.DMA((3,)), pltpu.SemaphoreType.DMA],
        compiler_params=pltpu.CompilerParams(has_side_effects=True),
        name=name,
    )(parts)


def _gather_chips_halves(shard, *, name):
    def body(x_ref, o_ref, send_sems, recv_sems, local_sem):
        x, y, c = lax.axis_index("x"), lax.axis_index("y"), lax.axis_index("c")
        me, sibling = 2 * x + y, (x, y, 1 - c)
        mine = pltpu.make_async_copy(x_ref, o_ref.at[me], local_sem)
        mine.start()
        chips = _other_chips(x, y)

        def copy(q, src, dst, to):
            return pltpu.make_async_remote_copy(src_ref=src, dst_ref=dst, send_sem=send_sems.at[q],
                                                recv_sem=recv_sems.at[q], device_id=to, device_id_type=MESH)

        sends = [copy(q, x_ref.at[c], o_ref.at[me, c], (px, py, c)) for q, (px, py) in enumerate(chips)]
        for cp in sends:
            cp.start()
        passed = []
        for q, (px, py) in enumerate(chips):
            slot = o_ref.at[2 * px + py, c]
            copy(q, x_ref.at[c], slot, (px, py, c)).wait_recv()
            passed.append(copy(3 + q, slot, slot, sibling))
            passed[-1].start()
        for q, (px, py) in enumerate(chips):
            copy(3 + q, x_ref.at[1 - c], o_ref.at[2 * px + py, 1 - c], sibling).wait_recv()
        for cp in sends + passed:
            cp.wait_send()
        mine.wait()

    return pl.pallas_call(
        body,
        in_specs=[ANY],
        out_specs=ANY,
        out_shape=jax.ShapeDtypeStruct((N_CHIPS,) + shard.shape, shard.dtype),
        scratch_shapes=[pltpu.SemaphoreType.DMA((6,)), pltpu.SemaphoreType.DMA((6,)), pltpu.SemaphoreType.DMA],
        compiler_params=pltpu.CompilerParams(has_side_effects=True),
        name=name,
    )(shard)


def _swap_other_half(g4h, *, name):
    n = g4h.shape[0]

    def body(g_ref, o_ref, send_sem, recv_sem):
        x, y, c = lax.axis_index("x"), lax.axis_index("y"), lax.axis_index("c")
        cp = pltpu.make_async_remote_copy(src_ref=g_ref.at[pl.ds(0, n), 1 - c], dst_ref=o_ref, send_sem=send_sem,
                                          recv_sem=recv_sem, device_id=(x, y, 1 - c), device_id_type=MESH)
        cp.start()
        cp.wait()

    return pl.pallas_call(
        body,
        in_specs=[ANY],
        out_specs=ANY,
        out_shape=jax.ShapeDtypeStruct((n,) + g4h.shape[2:], g4h.dtype),
        scratch_shapes=[pltpu.SemaphoreType.DMA, pltpu.SemaphoreType.DMA],
        compiler_params=pltpu.CompilerParams(has_side_effects=True),
        name=name,
    )(g4h)


def _join_halves(half, *, name):
    def body(h_ref, o_ref, send_sem, recv_sem, local_sem):
        x, y, c = lax.axis_index("x"), lax.axis_index("y"), lax.axis_index("c")
        mine = pltpu.make_async_copy(h_ref, o_ref.at[c], local_sem)
        mine.start()
        cp = pltpu.make_async_remote_copy(src_ref=h_ref, dst_ref=o_ref.at[c], send_sem=send_sem, recv_sem=recv_sem,
                                          device_id=(x, y, 1 - c), device_id_type=MESH)
        cp.start()
        pltpu.make_async_remote_copy(src_ref=h_ref, dst_ref=o_ref.at[1 - c], send_sem=send_sem, recv_sem=recv_sem,
                                     device_id=(x, y, 1 - c), device_id_type=MESH).wait_recv()
        cp.wait_send()
        mine.wait()

    return pl.pallas_call(
        body,
        in_specs=[ANY],
        out_specs=ANY,
        out_shape=jax.ShapeDtypeStruct((2,) + half.shape, half.dtype),
        scratch_shapes=[pltpu.SemaphoreType.DMA, pltpu.SemaphoreType.DMA, pltpu.SemaphoreType.DMA],
        compiler_params=pltpu.CompilerParams(has_side_effects=True),
        name=name,
    )(half)


def _gather_all(v, *, name):
    def body(v_ref, o_ref, send_sems, recv_sems, local_sem):
        x, y, c = lax.axis_index("x"), lax.axis_index("y"), lax.axis_index("c")
        me = 4 * x + 2 * y + c
        mine = pltpu.make_async_copy(v_ref, o_ref.at[me], local_sem)
        mine.start()
        peers = [(x ^ (q >> 2 & 1), y ^ (q >> 1 & 1), c ^ (q & 1)) for q in range(1, 8)]
        sends = [pltpu.make_async_remote_copy(src_ref=v_ref, dst_ref=o_ref.at[me], send_sem=send_sems.at[q],
                                              recv_sem=recv_sems.at[q], device_id=peer, device_id_type=MESH)
                 for q, peer in enumerate(peers)]
        for cp in sends:
            cp.start()
        for q, (px, py, pc) in enumerate(peers):
            pltpu.make_async_remote_copy(src_ref=v_ref, dst_ref=o_ref.at[4 * px + 2 * py + pc], send_sem=send_sems.at[q],
                                         recv_sem=recv_sems.at[q], device_id=(px, py, pc), device_id_type=MESH).wait_recv()
        for cp in sends:
            cp.wait_send()
        mine.wait()

    return pl.pallas_call(
        body,
        in_specs=[ANY],
        out_specs=ANY,
        out_shape=jax.ShapeDtypeStruct((8,) + v.shape, v.dtype),
        scratch_shapes=[pltpu.SemaphoreType.DMA((7,)), pltpu.SemaphoreType.DMA((7,)), pltpu.SemaphoreType.DMA],
        compiler_params=pltpu.CompilerParams(has_side_effects=True),
        name=name,
    )(v)


WEIGHTS = ['ssm_norm_w', 'ssm_in_w', 'ssm_conv_w', 'ssm_conv_b', 'ssm_dt_bias', 'ssm_a_log', 'ssm_d',
           'ssm_gate_norm_w', 'ssm_out_w', 'kv_norm_w', 'w_k', 'w_v', 'attn_norm_w', 'w_q', 'w_o',
           'ffn_norm_w', 'ffn_up_w', 'ffn_conv_w', 'ffn_conv_b', 'ffn_down_w', 'final_norm_w']
SHARD_AXIS = {'ssm_norm_w': 1, 'ssm_in_w': 2, 'ssm_conv_w': 2, 'ssm_conv_b': 1, 'ssm_gate_norm_w': 1,
              'ssm_out_w': 1, 'w_k': 0, 'w_v': 0, 'w_q': 1, 'w_o': 1, 'ffn_up_w': 2, 'ffn_conv_w': 2,
              'ffn_down_w': 1}
BIG = ['ssm_in_w', 'ssm_out_w', 'w_k', 'w_v', 'w_q', 'w_o', 'ffn_up_w', 'ffn_down_w']
SMALL = [n for n in WEIGHTS if n in SHARD_AXIS and n not in BIG]
REPLICATED = [n for n in WEIGHTS if n not in SHARD_AXIS]
N_CHIPS = 4
RS_ROW_MULT = 4096


def _pack(arrs, dtype, row_mult):
    flat = jnp.concatenate([a.astype(dtype) for a in arrs], axis=-1)
    n = flat.shape[-1]
    rows = -(-n // LANES)
    rows = -(-rows // row_mult) * row_mult
    flat = jnp.pad(flat, [(0, 0)] * (flat.ndim - 1) + [(0, rows * LANES - n)])
    return flat.reshape(flat.shape[:-1] + (rows, LANES))


def _unpack(buf, shapes):
    lead = buf.shape[:-2]
    flat = buf.reshape(lead + (-1,))
    out, off = [], 0
    for shp in shapes:
        n = math.prod(shp)
        out.append(flat[..., off:off + n].reshape(lead + tuple(shp)))
        off += n
    return out


def _to_shards(full, axis):
    return jnp.stack(jnp.split(full, N_CHIPS, axis=axis), axis=0)


def _from_shards(stacked, axis):
    return jnp.concatenate([stacked[j] for j in range(N_CHIPS)], axis=axis)


def _heads(a, h):
    t = a.shape[0]
    return a.reshape(t, h, a.shape[1] // h).transpose(1, 0, 2)


def _unheads(a):
    h, t, d = a.shape
    return a.transpose(1, 0, 2).reshape(t, h * d)


def _ffn_fwd(h, norm_w, w_up, conv_w, conv_b, w_down, tag):
    u = _rmsnorm_fwd(h, norm_w, name=f"ffn{tag}_norm")
    hid = _matmul(u, w_up, name=f"ffn{tag}_up")
    act = _conv_glu_fwd(hid, conv_w, conv_b, name=f"ffn{tag}_glu")
    out = _matmul(act, w_down, add=h, name=f"ffn{tag}_down")
    return out, (u, hid, act)


def _ffn_bwd(h, saved, dout, norm_w, w_up, conv_w, conv_b, w_down, tag):
    u, hid, act = saved
    f = w_down.shape[0]
    dact = _matmul(dout, w_down, tb=True, name=f"ffn{tag}_down_dx")
    dw_down = _matmul(act, dout, ta=True, name=f"ffn{tag}_down_dw")
    dg, dv, dwg, dwv, dbg, dbv = _conv_glu_bwd(hid, conv_w, conv_b, dact, name=f"ffn{tag}_glu_bwd")
    du = _matmul(dg, w_up[:, :f], tb=True, name=f"ffn{tag}_up_dx_g")
    du = _matmul(dv, w_up[:, f:], tb=True, add=du, name=f"ffn{tag}_up_dx_v")
    dw_up = jnp.concatenate([_matmul(u, dg, ta=True, name=f"ffn{tag}_up_dw_g"),
                             _matmul(u, dv, ta=True, name=f"ffn{tag}_up_dw_v")], axis=1)
    dh, (dnorm,) = _rmsnorm_bwd(h, [(du, norm_w)], dout, name=f"ffn{tag}_norm_bwd")
    return dh, dict(norm=dnorm[0], up=dw_up, conv_w=jnp.concatenate([dwg, dwv], axis=1),
                    conv_b=jnp.concatenate([dbg, dbv], axis=1)[0], down=dw_down)


def _step(x, target, w):
    t = x.shape[0]
    g_n, heads = SSM_GROUPS, SSM_HEADS
    r_h = heads // g_n
    di = D_INNER
    zx_cols = di + CONV_DIM
    w_in = w['ssm_in_w'][0]
    w_zx = w_in[:, :zx_cols]
    w_dt = jnp.pad(w_in[:, zx_cols:], ((0, 0), (0, LANES - heads)))
    conv_w, conv_b = w['ssm_conv_w'][0], w['ssm_conv_b'][0]
    hp = jnp.stack([w['ssm_dt_bias'][0], w['ssm_a_log'][0], w['ssm_d'][0]], axis=0).reshape(3, g_n, r_h)
    hpc, hpr = hp.transpose(1, 0, 2), hp.transpose(1, 2, 0)
    w_out = w['ssm_out_w'][0]
    w_q, w_o = w['w_q'][0], w['w_o'][0]

    h0 = x
    u0 = _rmsnorm_fwd(h0, w['ssm_norm_w'][0], name="ssm_norm")
    zx = _matmul(u0, w_zx, name="ssm_in_zx")
    dt_raw = _matmul(u0, w_dt, name="ssm_in_dt")[:, :heads]
    dtg = dt_raw.reshape(t, g_n, r_h)
    dtc, dtr = dtg.transpose(1, 0, 2), dtg.transpose(1, 2, 0)
    xbc = _conv_silu_fwd(zx, conv_w, conv_b, x_off=di, name="ssm_conv")
    y, prev = _ssd_fwd(xbc, dtc, dtr, hpc, hpr, name="ssd_fwd")
    yn = _gate_norm_fwd(y, zx, w['ssm_gate_norm_w'][0], name="ssm_gate_norm")
    h1 = _matmul(yn, w_out, add=h0, name="ssm_out")
    h2, ffn0 = _ffn_fwd(h1, w['ffn_norm_w'][0], w['ffn_up_w'][0], w['ffn_conv_w'][0], w['ffn_conv_b'][0],
                        w['ffn_down_w'][0], 0)
    hk = _rmsnorm_fwd(h2, w['kv_norm_w'], name="kv_norm")
    qn = _rmsnorm_fwd(h2, w['attn_norm_w'][0], name="attn_norm")
    k2 = _matmul(hk, w['w_k'], out_dtype=BF16, name="attn_k")
    v2 = _matmul(hk, w['w_v'], out_dtype=BF16, name="attn_v")
    q2 = _matmul(qn, w_q, out_dtype=BF16, name="attn_q")
    qh, kh, vh = _heads(q2, SB_HEADS), _heads(k2, SB_HEADS), _heads(v2, SB_HEADS)
    oh, lt = _sb_fwd(qh, kh, vh, name="sb_fwd")
    o2 = _unheads(oh)
    h3 = _matmul(o2, w_o, add=h2, name="attn_o")
    h4, ffn1 = _ffn_fwd(h3, w['ffn_norm_w'][1], w['ffn_up_w'][1], w['ffn_conv_w'][1], w['ffn_conv_b'][1],
                        w['ffn_down_w'][1], 1)
    loss_p, dh4, d_final = _loss_head(h4, w['final_norm_w'], target, name="loss_head")

    dh3, g1 = _ffn_bwd(h3, ffn1, dh4, w['ffn_norm_w'][1], w['ffn_up_w'][1], w['ffn_conv_w'][1],
                       w['ffn_conv_b'][1], w['ffn_down_w'][1], 1)
    do2 = _matmul(dh3, w_o, tb=True, name="attn_o_dx")
    dw_o = _matmul(o2, dh3, ta=True, name="attn_o_dw")
    dqh, dkh, dvh = _sb_bwd(qh, kh, vh, lt, _heads(do2, SB_HEADS), name="sb_bwd")
    dq2, dk2, dv2 = _unheads(dqh), _unheads(dkh), _unheads(dvh)
    dqn = _matmul(dq2, w_q, tb=True, name="attn_q_dx")
    dw_q = _matmul(qn, dq2, ta=True, name="attn_q_dw")
    dhk = _matmul(dk2, w['w_k'], tb=True, name="attn_k_dx")
    dhk = _matmul(dv2, w['w_v'], tb=True, add=dhk, name="attn_v_dx")
    dw_k = _matmul(hk, dk2, ta=True, name="attn_k_dw")
    dw_v = _matmul(hk, dv2, ta=True, name="attn_v_dw")
    dh2, (d_attn_norm, d_kv_norm) = _rmsnorm_bwd(h2, [(dqn, w['attn_norm_w'][0]), (dhk, w['kv_norm_w'])], dh3,
                                                 name="attn_norms_bwd")
    dh1, g0 = _ffn_bwd(h1, ffn0, dh2, w['ffn_norm_w'][0], w['ffn_up_w'][0], w['ffn_conv_w'][0],
                       w['ffn_conv_b'][0], w['ffn_down_w'][0], 0)
    dyn = _matmul(dh1, w_out, tb=True, name="ssm_out_dx")
    dw_out = _matmul(yn, dh1, ta=True, name="ssm_out_dw")
    dy, dz, d_gate = _gate_norm_bwd(y, zx, w['ssm_gate_norm_w'][0], dyn, name="ssm_gate_norm_bwd")
    dxs, dbm, dcm, ddt_g, hg = _ssd_bwd(xbc, dtc, dtr, hpc, hpr, prev, dy, name="ssd_bwd")
    dxbc = jnp.concatenate([dxs, dbm, dcm], axis=1)
    dxbc_pre, d_conv_w, d_conv_b = _conv_silu_bwd(zx, conv_w, conv_b, dxbc, x_off=di, name="ssm_conv_bwd")
    dzx = jnp.concatenate([dz, dxbc_pre], axis=1)
    ddt = jnp.pad(ddt_g.transpose(1, 0, 2).reshape(t, heads), ((0, 0), (0, LANES - heads)))
    du0 = _matmul(dzx, w_zx, tb=True, name="ssm_in_zx_dx")
    du0 = _matmul(ddt, w_dt, tb=True, add=du0, name="ssm_in_dt_dx")
    dw_in = jnp.concatenate([_matmul(u0, dzx, ta=True, name="ssm_in_zx_dw"),
                             _matmul(u0, ddt, ta=True, name="ssm_in_dt_dw")[:, :heads]], axis=1)
    dx, (d_ssm_norm,) = _rmsnorm_bwd(h0, [(du0, w['ssm_norm_w'][0])], dh1, name="ssm_norm_bwd")

    hgr = hg.transpose(1, 0, 2).reshape(3, heads)
    grads = {
        'ssm_norm_w': d_ssm_norm, 'ssm_in_w': dw_in[None], 'ssm_conv_w': d_conv_w[None], 'ssm_conv_b': d_conv_b,
        'ssm_dt_bias': hgr[0:1], 'ssm_a_log': hgr[1:2], 'ssm_d': hgr[2:3], 'ssm_gate_norm_w': d_gate,
        'ssm_out_w': dw_out[None], 'kv_norm_w': d_kv_norm[0], 'w_k': dw_k, 'w_v': dw_v, 'attn_norm_w': d_attn_norm,
        'w_q': dw_q[None], 'w_o': dw_o[None], 'ffn_norm_w': jnp.stack([g0['norm'], g1['norm']]),
        'ffn_up_w': jnp.stack([g0['up'], g1['up']]), 'ffn_conv_w': jnp.stack([g0['conv_w'], g1['conv_w']]),
        'ffn_conv_b': jnp.stack([g0['conv_b'], g1['conv_b']]), 'ffn_down_w': jnp.stack([g0['down'], g1['down']]),
        'final_norm_w': d_final[0],
    }
    return loss_p, dx, grads


def kernel(x, ssm_norm_w, ssm_in_w, ssm_conv_w, ssm_conv_b, ssm_dt_bias, ssm_a_log, ssm_d, ssm_gate_norm_w, ssm_out_w, kv_norm_w, w_k, w_v, attn_norm_w, w_q, w_o, ffn_norm_w, ffn_up_w, ffn_conv_w, ffn_conv_b, ffn_down_w, final_norm_w, loss_target, m_ssm_norm_w, m_ssm_in_w, m_ssm_conv_w, m_ssm_conv_b, m_ssm_dt_bias, m_ssm_a_log, m_ssm_d, m_ssm_gate_norm_w, m_ssm_out_w, m_kv_norm_w, m_w_k, m_w_v, m_attn_norm_w, m_w_q, m_w_o, m_ffn_norm_w, m_ffn_up_w, m_ffn_conv_w, m_ffn_conv_b, m_ffn_down_w, m_final_norm_w, v_ssm_norm_w, v_ssm_in_w, v_ssm_conv_w, v_ssm_conv_b, v_ssm_dt_bias, v_ssm_a_log, v_ssm_d, v_ssm_gate_norm_w, v_ssm_out_w, v_kv_norm_w, v_w_k, v_w_v, v_attn_norm_w, v_w_q, v_w_o, v_ffn_norm_w, v_ffn_up_w, v_ffn_conv_w, v_ffn_conv_b, v_ffn_down_w, v_final_norm_w):
    args = (ssm_norm_w, ssm_in_w, ssm_conv_w, ssm_conv_b, ssm_dt_bias, ssm_a_log, ssm_d, ssm_gate_norm_w, ssm_out_w, kv_norm_w, w_k, w_v, attn_norm_w, w_q, w_o, ffn_norm_w, ffn_up_w, ffn_conv_w, ffn_conv_b, ffn_down_w, final_norm_w)
    moms = (m_ssm_norm_w, m_ssm_in_w, m_ssm_conv_w, m_ssm_conv_b, m_ssm_dt_bias, m_ssm_a_log, m_ssm_d, m_ssm_gate_norm_w, m_ssm_out_w, m_kv_norm_w, m_w_k, m_w_v, m_attn_norm_w, m_w_q, m_w_o, m_ffn_norm_w, m_ffn_up_w, m_ffn_conv_w, m_ffn_conv_b, m_ffn_down_w, m_final_norm_w)
    vels = (v_ssm_norm_w, v_ssm_in_w, v_ssm_conv_w, v_ssm_conv_b, v_ssm_dt_bias, v_ssm_a_log, v_ssm_d, v_ssm_gate_norm_w, v_ssm_out_w, v_kv_norm_w, v_w_k, v_w_v, v_attn_norm_w, v_w_q, v_w_o, v_ffn_norm_w, v_ffn_up_w, v_ffn_conv_w, v_ffn_conv_b, v_ffn_down_w, v_final_norm_w)
    local = dict(zip(WEIGHTS, args))
    m_in = dict(zip(WEIGHTS, moms))
    v_in = dict(zip(WEIGHTS, vels))
    c = lax.axis_index("c")

    big_shard = _pack([local[n].reshape(-1) for n in BIG], BF16, 32)
    big16 = _gather_chips_halves(big_shard.reshape(2, -1, LANES), name="gather_big").reshape((N_CHIPS,) + big_shard.shape)
    small32 = _gather_chips(_pack([local[n].reshape(-1) for n in SMALL], F32, 8), name="gather_small")
    full = {n: local[n] for n in REPLICATED}
    for names, buf in ((BIG, big16), (SMALL, small32)):
        for n, st in zip(names, _unpack(buf, [local[n].shape for n in names])):
            full[n] = _from_shards(st, SHARD_AXIS[n])

    loss_p, dx, grads = _step(x[0], loss_target[0], full)

    sharded = BIG + SMALL
    g4 = _pack([_to_shards(grads[n], SHARD_AXIS[n]).reshape(N_CHIPS, -1) for n in sharded], F32, RS_ROW_MULT)
    rows = g4.shape[1]
    g4h = g4.reshape(N_CHIPS, 2, rows // 2, LANES)
    pair = _pair_add(g4h, _swap_other_half(g4h, name="rs_pair_swap"), c, out_dtype=BF16, name="rs_pair_add")
    half = _sum_leading(_scatter_chips(pair, name="rs_chip_scatter"), name="rs_chip_sum")
    shard = _join_halves(half, name="rs_half_join").reshape(rows, LANES)
    gshard = dict(zip(sharded, _unpack(shard, [local[n].shape for n in sharded])))

    rep = _pack([loss_p.reshape(-1)] + [grads[n].reshape(-1) for n in REPLICATED], F32, 8)
    tot = _sum_leading(_gather_all(rep, name="ar_gather"), name="ar_sum")
    parts = _unpack(tot, [(LANES,)] + [local[n].shape for n in REPLICATED])
    loss = jnp.sum(parts[0])
    gshard.update(dict(zip(REPLICATED, parts[1:])))

    deltas, new_m, new_v = [], [], []
    for n in WEIGHTS:
        d, nm, nv = _adamw(local[n], gshard[n], m_in[n], v_in[n], name=f"adamw_{n}")
        deltas.append(d)
        new_m.append(nm)
        new_v.append(nv)
    return (loss, dx[None], *[gshard[n] for n in WEIGHTS], *deltas, *new_m, *new_v)
```

```python
import math

import jax
import jax.numpy as jnp
from jax import lax
from jax.experimental import pallas as pl
from jax.experimental.pallas import tpu as pltpu

D_MODEL = 1024
D_INNER = 2048
SSM_HEAD_DIM = 64
SSM_HEADS = 32
SSM_GROUPS = 4
SSM_STATE = 128
SSM_CONV = 4
SSM_CHUNK = 128
GN = SSM_GROUPS * SSM_STATE
CONV_DIM = D_INNER + 2 * GN
SB_HEADS = 16
SB_HEAD_DIM = 64
D_FF = 2816
FFN_CONV = 3
EPS = 1e-6
ADAM_LR = 0.001
ADAM_B1 = 0.9
ADAM_B2 = 0.999
ADAM_EPS = 1e-08
ADAM_WD = 0.01
ADAM_STEP = 10

LANES = 128
SUBLANES = 8
VMEM_LIMIT = 48 * 1024 * 1024
ADAM_BLOCK_BYTES = 1 << 20
F32 = jnp.float32
BF16 = jnp.bfloat16
MESH = pl.DeviceIdType.MESH


def _cparams(sem=None):
    return pltpu.CompilerParams(dimension_semantics=sem, vmem_limit_bytes=VMEM_LIMIT)


def _tile(n, cands):
    for c in cands:
        if n % c == 0:
            return c
    return n


def _nt(a, b):
    return lax.dot_general(a, b, (((1,), (1,)), ((), ())), preferred_element_type=F32)


def _tn(a, b):
    return lax.dot_general(a, b, (((0,), (0,)), ((), ())), preferred_element_type=F32)


def _nn(a, b):
    return jnp.dot(a, b, preferred_element_type=F32)


def _split(x, pieces):
    out = []
    for _ in range(pieces - 1):
        h = x.astype(BF16)
        out.append(h)
        x = x - h.astype(F32)
    out.append(x.astype(BF16))
    return out


def _ones_dot(ones, x, *, ones_left, pieces=3):
    o16 = ones.astype(BF16)
    acc = None
    for piece in _split(x, pieces):
        term = _nn(o16, piece) if ones_left else _nn(piece, o16)
        acc = term if acc is None else acc + term
    return acc


def _softplus(x):
    return jnp.maximum(x, 0.0) + jnp.log(1.0 + jnp.exp(-jnp.abs(x)))


def _sigmoid(x):
    e = jnp.exp(-jnp.abs(x))
    r = 1.0 / (1.0 + e)
    return jnp.where(x >= 0, r, e * r)


MM_TILE_MAX = 1408
MM_VMEM_BUDGET = 40 * 1024 * 1024


def _divisors(n, cap):
    out = [d for d in range(min(cap, n) // LANES * LANES, 0, -LANES) if n % d == 0]
    return out or [n]


def _mm_tiles(m, n, k, a_bytes, b_bytes, o_bytes, add_bytes):
    best = None
    for tm in _divisors(m, MM_TILE_MAX):
        for tn in _divisors(n, MM_TILE_MAX):
            for tk in _divisors(k, MM_TILE_MAX):
                vmem = 2 * (tm * tk * a_bytes + tk * tn * b_bytes + tm * tn * (o_bytes + add_bytes)) + tm * tn * 4
                if vmem > MM_VMEM_BUDGET:
                    continue
                score = (tm * tn * tk, tm * tn)
                if best is None or score > best[0]:
                    best = (score, (tm, tn, tk))
    return best[1]
def _matmul(a, b, *, ta=False, tb=False, add=None, out_dtype=F32, name):
    m, k = (a.shape[1], a.shape[0]) if ta else a.shape
    n = b.shape[0] if tb else b.shape[1]
    assert (b.shape[1] if tb else b.shape[0]) == k
    tm, tn, tk = _mm_tiles(m, n, k, a.dtype.itemsize, b.dtype.itemsize, jnp.dtype(out_dtype).itemsize,
                           0 if add is None else add.dtype.itemsize)
    nk = k // tk

    def body(*refs):
        if add is None:
            a_ref, b_ref, o_ref, acc_ref = refs
            add_ref = None
        else:
            a_ref, b_ref, add_ref, o_ref, acc_ref = refs
        kk = pl.program_id(2)

        @pl.when(kk == 0)
        def _():
            acc_ref[...] = jnp.zeros_like(acc_ref)

        av = a_ref[...].astype(BF16)
        bv = b_ref[...].astype(BF16)
        dn = (((0 if ta else 1,), (1 if tb else 0,)), ((), ()))
        acc_ref[...] += lax.dot_general(av, bv, dn, preferred_element_type=F32)

        @pl.when(kk == nk - 1)
        def _():
            r = acc_ref[...]
            if add_ref is not None:
                r = r + add_ref[...].astype(F32)
            o_ref[...] = r.astype(o_ref.dtype)

    a_spec = pl.BlockSpec((tk, tm), lambda i, j, kk: (kk, i)) if ta else pl.BlockSpec((tm, tk), lambda i, j, kk: (i, kk))
    b_spec = pl.BlockSpec((tn, tk), lambda i, j, kk: (j, kk)) if tb else pl.BlockSpec((tk, tn), lambda i, j, kk: (kk, j))
    in_specs = [a_spec, b_spec]
    args = [a, b]
    if add is not None:
        in_specs.append(pl.BlockSpec((tm, tn), lambda i, j, kk: (i, j)))
        args.append(add)
    return pl.pallas_call(
        body,
        grid=(m // tm, n // tn, nk),
        in_specs=in_specs,
        out_specs=pl.BlockSpec((tm, tn), lambda i, j, kk: (i, j)),
        out_shape=jax.ShapeDtypeStruct((m, n), out_dtype),
        scratch_shapes=[pltpu.VMEM((tm, tn), F32)],
        compiler_params=_cparams(("parallel", "parallel", "arbitrary")),
        name=name,
    )(*args)


def _rmsnorm_fwd(x, w, *, name):
    t, d = x.shape
    tb = _tile(t, (512, 256, 128))

    def body(x_ref, w_ref, o_ref):
        xv = x_ref[...]
        r = lax.rsqrt(jnp.mean(xv * xv, axis=-1, keepdims=True) + EPS)
        o_ref[...] = (xv * r * w_ref[...]).astype(o_ref.dtype)

    return pl.pallas_call(
        body,
        grid=(t // tb,),
        in_specs=[pl.BlockSpec((tb, d), lambda i: (i, 0)), pl.BlockSpec((1, d), lambda i: (0, 0))],
        out_specs=pl.BlockSpec((tb, d), lambda i: (i, 0)),
        out_shape=jax.ShapeDtypeStruct((t, d), BF16),
        compiler_params=_cparams(("parallel",)),
        name=name,
    )(x, w.reshape(1, d))


def _rmsnorm_bwd(x, dys, dres, *, name):
    t, d = x.shape
    tb = _tile(t, (256, 128))
    nn = len(dys)
    has_res = dres is not None

    def body(*refs):
        x_ref = refs[0]
        dy_refs = refs[1:1 + nn]
        w_refs = refs[1 + nn:1 + 2 * nn]
        pos = 1 + 2 * nn
        res_ref = refs[pos] if has_res else None
        pos += 1 if has_res else 0
        dx_ref = refs[pos]
        dw_refs = refs[pos + 1:pos + 1 + nn]
        i = pl.program_id(0)
        xv = x_ref[...]
        r = lax.rsqrt(jnp.mean(xv * xv, axis=-1, keepdims=True) + EPS)
        xn = xv * r
        dx = res_ref[...] if has_res else jnp.zeros_like(xv)
        for q in range(nn):
            dy = dy_refs[q][...].astype(F32)
            g = dy * w_refs[q][...]
            dx = dx + r * (g - xn * jnp.mean(g * xn, axis=-1, keepdims=True))
            dwp = jnp.sum(dy * xn, axis=0, keepdims=True)

            @pl.when(i == 0)
            def _(q=q, dwp=dwp):
                dw_refs[q][...] = dwp

            @pl.when(i > 0)
            def _(q=q, dwp=dwp):
                dw_refs[q][...] += dwp
        dx_ref[...] = dx

    row = pl.BlockSpec((tb, d), lambda i: (i, 0))
    vec = pl.BlockSpec((1, d), lambda i: (0, 0))
    in_specs = [row] + [row] * nn + [vec] * nn + ([row] if has_res else [])
    args = [x] + [p[0] for p in dys] + [p[1].reshape(1, d) for p in dys] + ([dres] if has_res else [])
    outs = pl.pallas_call(
        body,
        grid=(t // tb,),
        in_specs=in_specs,
        out_specs=[row] + [vec] * nn,
        out_shape=[jax.ShapeDtypeStruct((t, d), F32)] + [jax.ShapeDtypeStruct((1, d), F32)] * nn,
        compiler_params=_cparams(("arbitrary",)),
        name=name,
    )(*args)
    return outs[0], list(outs[1:])


def _loss_head(x, w, target, *, name):
    t, d = x.shape
    tb = _tile(t, (256, 128))

    def body(x_ref, w_ref, t_ref, loss_ref, dx_ref, dw_ref):
        i = pl.program_id(0)
        xv = x_ref[...]
        wv = w_ref[...]
        r = lax.rsqrt(jnp.mean(xv * xv, axis=-1, keepdims=True) + EPS)
        xn = xv * r
        e = xn * wv - t_ref[...]
        lp = 0.5 * jnp.sum(jnp.mean(e * e, axis=-1, keepdims=True), axis=0, keepdims=True)
        dy = e * (1.0 / d)
        g = dy * wv
        dx_ref[...] = r * (g - xn * jnp.mean(g * xn, axis=-1, keepdims=True))
        dwp = jnp.sum(dy * xn, axis=0, keepdims=True)
        lpv = jnp.broadcast_to(lp, (1, LANES)) * (1.0 / LANES)

        @pl.when(i == 0)
        def _():
            dw_ref[...] = dwp
            loss_ref[...] = lpv

        @pl.when(i > 0)
        def _():
            dw_ref[...] += dwp
            loss_ref[...] += lpv

    row = pl.BlockSpec((tb, d), lambda i: (i, 0))
    vec = pl.BlockSpec((1, d), lambda i: (0, 0))
    return pl.pallas_call(
        body,
        grid=(t // tb,),
        in_specs=[row, vec, row],
        out_specs=[pl.BlockSpec((1, LANES), lambda i: (0, 0)), row, vec],
        out_shape=[jax.ShapeDtypeStruct((1, LANES), F32), jax.ShapeDtypeStruct((t, d), F32),
                   jax.ShapeDtypeStruct((1, d), F32)],
        compiler_params=_cparams(("arbitrary",)),
        name=name,
    )(x, w.reshape(1, d), target)


ROW_CHUNK = 512
PAD = SUBLANES


def _shifted(pad_ref, r0, rows, back):
    return pad_ref[pl.ds(PAD + r0 - back, rows), :]


def _conv_taps(pad_ref, w_ref, r0, rows, kw):
    acc = None
    for j in range(kw):
        term = _shifted(pad_ref, r0, rows, kw - 1 - j) * w_ref[j:j + 1, :]
        acc = term if acc is None else acc + term
    return acc


def _fill_pad(pad_ref, x_ref, t):
    pad_ref[0:PAD, :] = jnp.zeros((PAD, pad_ref.shape[1]), F32)
    pad_ref[pl.ds(PAD + t, PAD), :] = jnp.zeros((PAD, pad_ref.shape[1]), F32)
    pad_ref[pl.ds(PAD, t), :] = x_ref[...].astype(F32)


def _conv_silu_fwd(x, w, b, *, x_off=0, name):
    t = x.shape[0]
    kw, c = w.shape
    cw = _tile(math.gcd(c, x_off) if x_off else c, (256, 128))
    ob = x_off // cw
    rc = _tile(t, (ROW_CHUNK,))

    def body(x_ref, w_ref, b_ref, o_ref, pad_ref):
        _fill_pad(pad_ref, x_ref, t)
        for r0 in range(0, t, rc):
            pre = _conv_taps(pad_ref, w_ref, r0, rc, kw) + b_ref[...]
            o_ref[pl.ds(r0, rc), :] = pre * _sigmoid(pre)

    strip = pl.BlockSpec((t, cw), lambda i: (0, i))
    return pl.pallas_call(
        body,
        grid=(c // cw,),
        in_specs=[pl.BlockSpec((t, cw), lambda i: (0, i + ob)), pl.BlockSpec((kw, cw), lambda i: (0, i)),
                  pl.BlockSpec((1, cw), lambda i: (0, i))],
        out_specs=strip,
        out_shape=jax.ShapeDtypeStruct((t, c), F32),
        scratch_shapes=[pltpu.VMEM((t + 2 * PAD, cw), F32)],
        compiler_params=_cparams(("parallel",)),
        name=name,
    )(x, w, b.reshape(1, c))


def _conv_bwd_core(dpre_pad_ref, x_pad_ref, w_ref, dx_ref, dw_ref, db_ref, t, rc, kw):
    cw = dx_ref.shape[1]
    dws = [jnp.zeros((1, cw), F32) for _ in range(kw)]
    dbs = jnp.zeros((1, cw), F32)
    for r0 in range(0, t, rc):
        dpre = dpre_pad_ref[pl.ds(PAD + r0, rc), :]
        dx = None
        for j in range(kw):
            s = kw - 1 - j
            term = dpre_pad_ref[pl.ds(PAD + r0 + s, rc), :] * w_ref[j:j + 1, :]
            dx = term if dx is None else dx + term
            dws[j] = dws[j] + jnp.sum(dpre * _shifted(x_pad_ref, r0, rc, s), axis=0, keepdims=True)
        dbs = dbs + jnp.sum(dpre, axis=0, keepdims=True)
        dx_ref[pl.ds(r0, rc), :] = dx
    for j in range(kw):
        dw_ref[j:j + 1, :] = dws[j]
    db_ref[...] = dbs


def _conv_silu_bwd(x, w, b, dact, *, x_off=0, name):
    t = x.shape[0]
    kw, c = w.shape
    cw = _tile(math.gcd(c, x_off) if x_off else c, (256, 128))
    ob = x_off // cw
    rc = _tile(t, (ROW_CHUNK,))

    def body(x_ref, w_ref, b_ref, da_ref, dx_ref, dw_ref, db_ref, xpad_ref, dpad_ref):
        _fill_pad(xpad_ref, x_ref, t)
        dpad_ref[0:PAD, :] = jnp.zeros((PAD, cw), F32)
        dpad_ref[pl.ds(PAD + t, PAD), :] = jnp.zeros((PAD, cw), F32)
        for r0 in range(0, t, rc):
            pre = _conv_taps(xpad_ref, w_ref, r0, rc, kw) + b_ref[...]
            sg = _sigmoid(pre)
            dpad_ref[pl.ds(PAD + r0, rc), :] = da_ref[pl.ds(r0, rc), :] * (sg * (1.0 + pre * (1.0 - sg)))
        _conv_bwd_core(dpad_ref, xpad_ref, w_ref, dx_ref, dw_ref, db_ref, t, rc, kw)

    strip = pl.BlockSpec((t, cw), lambda i: (0, i))
    wspec = pl.BlockSpec((kw, cw), lambda i: (0, i))
    bspec = pl.BlockSpec((1, cw), lambda i: (0, i))
    return pl.pallas_call(
        body,
        grid=(c // cw,),
        in_specs=[pl.BlockSpec((t, cw), lambda i: (0, i + ob)), wspec, bspec, strip],
        out_specs=[strip, wspec, bspec],
        out_shape=[jax.ShapeDtypeStruct((t, c), F32), jax.ShapeDtypeStruct((kw, c), F32),
                   jax.ShapeDtypeStruct((1, c), F32)],
        scratch_shapes=[pltpu.VMEM((t + 2 * PAD, cw), F32), pltpu.VMEM((t + 2 * PAD, cw), F32)],
        compiler_params=_cparams(("parallel",)),
        name=name,
    )(x, w, b.reshape(1, c), dact)


def _conv_glu_fwd(hid, w, b, *, name):
    t, c2 = hid.shape
    f = c2 // 2
    kw = w.shape[0]
    cw = _tile(f, (256, 128))
    nf = f // cw
    rc = _tile(t, (ROW_CHUNK,))

    def body(g_ref, v_ref, wg_ref, wv_ref, bg_ref, bv_ref, o_ref, gpad_ref, vpad_ref):
        _fill_pad(gpad_ref, g_ref, t)
        _fill_pad(vpad_ref, v_ref, t)
        for r0 in range(0, t, rc):
            gate = _conv_taps(gpad_ref, wg_ref, r0, rc, kw) + bg_ref[...]
            val = _conv_taps(vpad_ref, wv_ref, r0, rc, kw) + bv_ref[...]
            o_ref[pl.ds(r0, rc), :] = (gate * _sigmoid(gate) * val).astype(o_ref.dtype)

    gs = pl.BlockSpec((t, cw), lambda i: (0, i))
    vs = pl.BlockSpec((t, cw), lambda i: (0, i + nf))
    b2 = b.reshape(1, c2)
    return pl.pallas_call(
        body,
        grid=(nf,),
        in_specs=[gs, vs, pl.BlockSpec((kw, cw), lambda i: (0, i)), pl.BlockSpec((kw, cw), lambda i: (0, i + nf)),
                  pl.BlockSpec((1, cw), lambda i: (0, i)), pl.BlockSpec((1, cw), lambda i: (0, i + nf))],
        out_specs=gs,
        out_shape=jax.ShapeDtypeStruct((t, f), BF16),
        scratch_shapes=[pltpu.VMEM((t + 2 * PAD, cw), F32), pltpu.VMEM((t + 2 * PAD, cw), F32)],
        compiler_params=_cparams(("parallel",)),
        name=name,
    )(hid, hid, w, w, b2, b2)


def _conv_glu_bwd(hid, w, b, dact, *, name):
    t, c2 = hid.shape
    f = c2 // 2
    kw = w.shape[0]
    cw = _tile(f, (128,))
    nf = f // cw
    rc = _tile(t, (ROW_CHUNK,))

    def body(g_ref, v_ref, wg_ref, wv_ref, bg_ref, bv_ref, da_ref,
             dg_ref, dv_ref, dwg_ref, dwv_ref, dbg_ref, dbv_ref,
             gpad_ref, vpad_ref, dgpad_ref, dvpad_ref):
        _fill_pad(gpad_ref, g_ref, t)
        _fill_pad(vpad_ref, v_ref, t)
        for ref in (dgpad_ref, dvpad_ref):
            ref[0:PAD, :] = jnp.zeros((PAD, cw), F32)
            ref[pl.ds(PAD + t, PAD), :] = jnp.zeros((PAD, cw), F32)
        for r0 in range(0, t, rc):
            gate = _conv_taps(gpad_ref, wg_ref, r0, rc, kw) + bg_ref[...]
            val = _conv_taps(vpad_ref, wv_ref, r0, rc, kw) + bv_ref[...]
            sg = _sigmoid(gate)
            da = da_ref[pl.ds(r0, rc), :].astype(F32)
            dgpad_ref[pl.ds(PAD + r0, rc), :] = da * val * (sg * (1.0 + gate * (1.0 - sg)))
            dvpad_ref[pl.ds(PAD + r0, rc), :] = da * (gate * sg)
        _conv_bwd_core(dgpad_ref, gpad_ref, wg_ref, dg_ref, dwg_ref, dbg_ref, t, rc, kw)
        _conv_bwd_core(dvpad_ref, vpad_ref, wv_ref, dv_ref, dwv_ref, dbv_ref, t, rc, kw)

    gs = pl.BlockSpec((t, cw), lambda i: (0, i))
    vs = pl.BlockSpec((t, cw), lambda i: (0, i + nf))
    wg = pl.BlockSpec((kw, cw), lambda i: (0, i))
    wv = pl.BlockSpec((kw, cw), lambda i: (0, i + nf))
    bg = pl.BlockSpec((1, cw), lambda i: (0, i))
    bv = pl.BlockSpec((1, cw), lambda i: (0, i + nf))
    b2 = b.reshape(1, c2)
    pad = pltpu.VMEM((t + 2 * PAD, cw), F32)
    return pl.pallas_call(
        body,
        grid=(nf,),
        in_specs=[gs, vs, wg, wv, bg, bv, gs],
        out_specs=[gs, gs, wg, wg, bg, bg],
        out_shape=[jax.ShapeDtypeStruct((t, f), F32), jax.ShapeDtypeStruct((t, f), F32),
                   jax.ShapeDtypeStruct((kw, f), F32), jax.ShapeDtypeStruct((kw, f), F32),
                   jax.ShapeDtypeStruct((1, f), F32), jax.ShapeDtypeStruct((1, f), F32)],
        scratch_shapes=[pad, pad, pad, pad],
        compiler_params=_cparams(("parallel",)),
        name=name,
    )(hid, hid, w, w, b2, b2, dact)


def _gate_norm_fwd(y, zx, w, *, name):
    t, di = y.shape
    gsz = di // SSM_GROUPS
    tb = _tile(t, (256, 128))

    def body(y_ref, z_ref, w_ref, o_ref):
        for g in range(SSM_GROUPS):
            sl = slice(g * gsz, (g + 1) * gsz)
            zv = z_ref[:, sl]
            gv = y_ref[:, sl] * (zv * _sigmoid(zv))
            r = lax.rsqrt(jnp.mean(gv * gv, axis=-1, keepdims=True) + EPS)
            o_ref[:, sl] = (gv * r * w_ref[:, sl]).astype(o_ref.dtype)

    row = pl.BlockSpec((tb, di), lambda i: (i, 0))
    return pl.pallas_call(
        body,
        grid=(t // tb,),
        in_specs=[row, row, pl.BlockSpec((1, di), lambda i: (0, 0))],
        out_specs=row,
        out_shape=jax.ShapeDtypeStruct((t, di), BF16),
        compiler_params=_cparams(("parallel",)),
        name=name,
    )(y, zx, w.reshape(1, di))


def _gate_norm_bwd(y, zx, w, dyn, *, name):
    t, di = y.shape
    gsz = di // SSM_GROUPS
    tb = _tile(t, (256, 128))

    def body(y_ref, z_ref, w_ref, d_ref, dy_ref, dz_ref, dw_ref):
        i = pl.program_id(0)
        for g in range(SSM_GROUPS):
            sl = slice(g * gsz, (g + 1) * gsz)
            zv = z_ref[:, sl]
            yv = y_ref[:, sl]
            sg = _sigmoid(zv)
            sz = zv * sg
            gv = yv * sz
            r = lax.rsqrt(jnp.mean(gv * gv, axis=-1, keepdims=True) + EPS)
            gn = gv * r
            dn = d_ref[:, sl].astype(F32)
            q = dn * w_ref[:, sl]
            dg = r * (q - gn * jnp.mean(q * gn, axis=-1, keepdims=True))
            dy_ref[:, sl] = dg * sz
            dz_ref[:, sl] = dg * yv * (sg * (1.0 + zv * (1.0 - sg)))
            dwp = jnp.sum(dn * gn, axis=0, keepdims=True)

            @pl.when(i == 0)
            def _(sl=sl, dwp=dwp):
                dw_ref[:, sl] = dwp

            @pl.when(i > 0)
            def _(sl=sl, dwp=dwp):
                dw_ref[:, sl] += dwp

    row = pl.BlockSpec((tb, di), lambda i: (i, 0))
    vec = pl.BlockSpec((1, di), lambda i: (0, 0))
    return pl.pallas_call(
        body,
        grid=(t // tb,),
        in_specs=[row, row, vec, row],
        out_specs=[row, row, vec],
        out_shape=[jax.ShapeDtypeStruct((t, di), F32), jax.ShapeDtypeStruct((t, di), F32),
                   jax.ShapeDtypeStruct((1, di), F32)],
        compiler_params=_cparams(("arbitrary",)),
        name=name,
    )(y, zx, w.reshape(1, di), dyn)


def _adamw(w, g, m, v, *, name):
    shape = w.shape
    cols = shape[-1]
    rows = w.size // cols
    w2, g2, m2, v2 = (a.reshape(rows, cols) for a in (w, g, m, v))
    tr = rows
    if rows * cols * 4 > ADAM_BLOCK_BYTES:
        tr = _tile(rows, tuple(r for r in (512, 256, 128, 64, 32, 16, 8) if r * cols * 4 <= ADAM_BLOCK_BYTES))
    c1 = 1.0 - ADAM_B1 ** ADAM_STEP
    c2 = 1.0 - ADAM_B2 ** ADAM_STEP

    def body(w_ref, g_ref, m_ref, v_ref, d_ref, nm_ref, nv_ref):
        gv = g_ref[...]
        nm = ADAM_B1 * m_ref[...] + (1.0 - ADAM_B1) * gv
        nv = ADAM_B2 * v_ref[...] + (1.0 - ADAM_B2) * (gv * gv)
        d_ref[...] = -ADAM_LR * ((nm / c1) / (jnp.sqrt(nv / c2) + ADAM_EPS) + ADAM_WD * w_ref[...])
        nm_ref[...] = nm
        nv_ref[...] = nv

    blk = pl.BlockSpec((tr, cols), lambda i: (i, 0))
    outs = pl.pallas_call(
        body,
        grid=(rows // tr,),
        in_specs=[blk] * 4,
        out_specs=[blk] * 3,
        out_shape=[jax.ShapeDtypeStruct((rows, cols), F32)] * 3,
        compiler_params=_cparams(("parallel",)),
        name=name,
    )(w2, g2, m2, v2)
    return tuple(o.reshape(shape) for o in outs)


def _ssd_scalars(dtc_ref, dtr_ref, hpc_ref, hpr_ref, ln):
    bias_c, alog_c = hpc_ref[0, 0:1, :], hpc_ref[0, 1:2, :]
    bias_r, alog_r = hpr_ref[0, :, 0:1], hpr_ref[0, :, 1:2]
    a_c, a_r = -jnp.exp(alog_c), -jnp.exp(alog_r)
    raw_c = dtc_ref[0] + bias_c
    dt_c = _softplus(raw_c)
    dt_r = _softplus(dtr_ref[0] + bias_r)
    row = lax.broadcasted_iota(jnp.int32, (ln, ln), 0)
    col = lax.broadcasted_iota(jnp.int32, (ln, ln), 1)
    lower = (col <= row).astype(F32)
    upper = (row <= col).astype(F32)
    acs_c = _ones_dot(lower, dt_c * a_c, ones_left=True)
    acs_r = _ones_dot(upper, dt_r * a_r, ones_left=False)
    return raw_c, dt_c, a_c, acs_c, acs_r, row, col


def _ssd_specs(t, di, g_n, n_st, rp, ln, r_h, rev):
    nc = t // ln
    cidx = (lambda c: nc - 1 - c) if rev else (lambda c: c)
    xs = pl.BlockSpec((ln, rp), lambda g, c: (cidx(c), g))
    bm = pl.BlockSpec((ln, n_st), lambda g, c: (cidx(c), di // n_st + g))
    cm = pl.BlockSpec((ln, n_st), lambda g, c: (cidx(c), di // n_st + g_n + g))
    dtc = pl.BlockSpec((1, ln, r_h), lambda g, c: (g, cidx(c), 0))
    dtr = pl.BlockSpec((1, r_h, ln), lambda g, c: (g, 0, cidx(c)))
    hpc = pl.BlockSpec((1, 3, r_h), lambda g, c: (g, 0, 0))
    hpr = pl.BlockSpec((1, r_h, 3), lambda g, c: (g, 0, 0))
    prev = pl.BlockSpec((1, rp, n_st), lambda g, c: (cidx(c), g, 0))
    return xs, bm, cm, dtc, dtr, hpc, hpr, prev


def _ssd_fwd(xbc, dtc, dtr, hpc, hpr, *, name):
    t = xbc.shape[0]
    di, g_n, n_st, p_h, ln = D_INNER, SSM_GROUPS, SSM_STATE, SSM_HEAD_DIM, SSM_CHUNK
    r_h = SSM_HEADS // g_n
    rp = r_h * p_h
    nc = t // ln

    def body(xs_ref, b_ref, c_ref, dtc_ref, dtr_ref, hpc_ref, hpr_ref, y_ref, prev_ref, st_ref):
        @pl.when(pl.program_id(1) == 0)
        def _():
            st_ref[...] = jnp.zeros_like(st_ref)

        _, dt_c, _, acs_c, acs_r, row, col = _ssd_scalars(dtc_ref, dtr_ref, hpc_ref, hpr_ref, ln)
        bm = b_ref[...]
        cm = c_ref[...]
        cm16 = cm.astype(BF16)
        cb = _nt(cm16, bm.astype(BF16))
        causal = row >= col
        for r in range(r_h):
            sl = slice(r * p_h, (r + 1) * p_h)
            xs = xs_ref[:, sl]
            acs = acs_c[:, r:r + 1]
            last = acs_c[ln - 1:ln, r:r + 1]
            lm = jnp.where(causal, jnp.exp(acs - acs_r[r:r + 1, :]), 0.0)
            xd = (xs * dt_c[:, r:r + 1]).astype(BF16)
            prev = st_ref[sl, :]
            y = _nn((cb * lm).astype(BF16), xd)
            y = y + _nt(cm16, prev.astype(BF16)) * jnp.exp(acs)
            y_ref[:, sl] = y + hpc_ref[0, 2:3, r:r + 1] * xs
            prev_ref[0, sl, :] = prev
            bd = (bm * jnp.exp(last - acs)).astype(BF16)
            st_ref[sl, :] = prev * jnp.exp(last) + _tn(xd, bd)

    xs, bm, cm, dtcs, dtrs, hpcs, hprs, prev = _ssd_specs(t, di, g_n, n_st, rp, ln, r_h, False)
    return pl.pallas_call(
        body,
        grid=(g_n, nc),
        in_specs=[xs, bm, cm, dtcs, dtrs, hpcs, hprs],
        out_specs=[xs, prev],
        out_shape=[jax.ShapeDtypeStruct((t, di), F32), jax.ShapeDtypeStruct((nc, g_n * rp, n_st), F32)],
        scratch_shapes=[pltpu.VMEM((rp, n_st), F32)],
        compiler_params=_cparams(("parallel", "arbitrary")),
        name=name,
    )(xbc, xbc, xbc, dtc, dtr, hpc, hpr)


def _ssd_bwd(xbc, dtc, dtr, hpc, hpr, prev, dy, *, name):
    t = xbc.shape[0]
    di, g_n, n_st, p_h, ln = D_INNER, SSM_GROUPS, SSM_STATE, SSM_HEAD_DIM, SSM_CHUNK
    r_h = SSM_HEADS // g_n
    rp = r_h * p_h
    nc = t // ln

    def body(xs_ref, b_ref, c_ref, dtc_ref, dtr_ref, hpc_ref, hpr_ref, prev_ref, dy_ref,
             dxs_ref, db_ref, dc_ref, ddt_ref, hg_ref, ds_ref):
        step = pl.program_id(1)

        @pl.when(step == 0)
        def _():
            ds_ref[...] = jnp.zeros_like(ds_ref)

        raw_c, dt_c, a_c, acs_c, acs_r, row, col = _ssd_scalars(dtc_ref, dtr_ref, hpc_ref, hpr_ref, ln)
        bm = b_ref[...]
        cm = c_ref[...]
        bm16, cm16 = bm.astype(BF16), cm.astype(BF16)
        cb = _nt(cm16, bm16)
        cbt = _nt(bm16, cm16)
        lane_r = lax.broadcasted_iota(jnp.int32, (ln, r_h), 1)
        sub_r = lax.broadcasted_iota(jnp.int32, (ln, r_h), 0)
        dacs_all = jnp.zeros((ln, r_h), F32)
        ddtx_all = jnp.zeros((ln, r_h), F32)
        dd_all = jnp.zeros((ln, r_h), F32)
        dcb = jnp.zeros((ln, ln), F32)
        dcbt = jnp.zeros((ln, ln), F32)
        dc_acc = jnp.zeros((ln, n_st), F32)
        db_acc = jnp.zeros((ln, n_st), F32)
        for r in range(r_h):
            sl = slice(r * p_h, (r + 1) * p_h)
            xs = xs_ref[:, sl]
            dyv = dy_ref[:, sl]
            dy16 = dyv.astype(BF16)
            dtv = dt_c[:, r:r + 1]
            acs = acs_c[:, r:r + 1]
            acsr = acs_r[r:r + 1, :]
            last = acs_c[ln - 1:ln, r:r + 1]
            xd = xs * dtv
            xd16 = xd.astype(BF16)
            lm = jnp.where(row >= col, jnp.exp(acs - acsr), 0.0)
            lmt = jnp.where(col >= row, jnp.exp(acsr - acs), 0.0)
            m_ls = cb * lm
            m_sl = cbt * lmt
            dm = _nt(dy16, xd16)
            dmt = _nt(xd16, dy16)
            dxd = _nn(m_sl.astype(BF16), dy16)
            dacs = jnp.sum(dm * m_ls, axis=1, keepdims=True) - jnp.sum(dmt * m_sl, axis=1, keepdims=True)
            dcb = dcb + dm * lm
            dcbt = dcbt + dmt * lmt
            prev = prev_ref[0, sl, :]
            prev16 = prev.astype(BF16)
            e = jnp.exp(acs)
            y_off = _nt(cm16, prev16) * e
            dacs = dacs + jnp.sum(dyv * y_off, axis=1, keepdims=True)
            dyo16 = (dyv * e).astype(BF16)
            dc_acc = dc_acc + _nn(dyo16, prev16)
            dprev = _tn(dyo16, cm16)
            ds = ds_ref[sl, :]
            ds16 = ds.astype(BF16)
            decay = jnp.exp(last - acs)
            bd16 = (bm * decay).astype(BF16)
            dbd = _nn(xd16, ds16)
            dxd = dxd + _nt(bd16, ds16)
            db_acc = db_acc + dbd * decay
            tdec = jnp.sum(dbd * bm, axis=1, keepdims=True) * decay
            dacs = dacs - tdec
            cd = jnp.exp(last)
            dlast = jnp.sum(tdec, axis=0, keepdims=True) + jnp.sum(jnp.sum(prev * ds, axis=1, keepdims=True), axis=0, keepdims=True) * cd
            ds_ref[sl, :] = dprev + cd * ds
            dskip = hpc_ref[0, 2:3, r:r + 1]
            dxs_ref[:, sl] = dxd * dtv + dskip * dyv
            ddtx = jnp.sum(dxd * xs, axis=1, keepdims=True)
            ddv = jnp.sum(dyv * xs, axis=1, keepdims=True)
            dacs = dacs + jnp.where(sub_r[:, 0:1] == ln - 1, dlast, 0.0)
            dacs_all = jnp.where(lane_r == r, dacs, dacs_all)
            ddtx_all = jnp.where(lane_r == r, ddtx, ddtx_all)
            dd_all = jnp.where(lane_r == r, ddv, dd_all)
        dc_ref[...] = dc_acc + _nn(dcb.astype(BF16), bm16)
        db_ref[...] = db_acc + _nn(dcbt.astype(BF16), cm16)
        upper = (row <= col).astype(F32)
        dad = _ones_dot(upper, dacs_all, ones_left=True)
        ddt = dad * a_c + ddtx_all
        ddt_raw = ddt * _sigmoid(raw_c)
        ddt_ref[0] = ddt_raw
        d_bias = jnp.sum(ddt_raw, axis=0, keepdims=True)
        d_alog = jnp.sum(dad * dt_c, axis=0, keepdims=True) * a_c
        d_d = jnp.sum(dd_all, axis=0, keepdims=True)
        hg = jnp.concatenate([d_bias, d_alog, d_d], axis=0)

        @pl.when(step == 0)
        def _():
            hg_ref[0] = hg

        @pl.when(step > 0)
        def _():
            hg_ref[0] += hg

    xs, bms, cms, dtcs, dtrs, hpcs, hprs, prevs = _ssd_specs(t, di, g_n, n_st, rp, ln, r_h, True)
    bout = pl.BlockSpec((ln, n_st), lambda g, c: (nc - 1 - c, g))
    return pl.pallas_call(
        body,
        grid=(g_n, nc),
        in_specs=[xs, bms, cms, dtcs, dtrs, hpcs, hprs, prevs, xs],
        out_specs=[xs, bout, bout, dtcs, hpcs],
        out_shape=[jax.ShapeDtypeStruct((t, di), F32), jax.ShapeDtypeStruct((t, g_n * n_st), F32),
                   jax.ShapeDtypeStruct((t, g_n * n_st), F32), jax.ShapeDtypeStruct((g_n, t, r_h), F32),
                   jax.ShapeDtypeStruct((g_n, 3, r_h), F32)],
        scratch_shapes=[pltpu.VMEM((rp, n_st), F32)],
        compiler_params=_cparams(("parallel", "arbitrary")),
        name=name,
    )(xbc, xbc, xbc, dtc, dtr, hpc, hpr, prev, dy)


SB_KEYS = 128
SB_QUERIES = (512, 256, 128)
SB_PIECES = 2


def _sb_logits(qs, kv, valid):
    z = _nt(qs, kv)
    sp = _softplus(z)
    lg = -sp if valid is None else jnp.where(valid, -sp, 0.0)
    return z - sp, lg


def _sb_iota(tq):
    diff = lax.broadcasted_iota(jnp.int32, (tq, SB_KEYS), 1) - lax.broadcasted_iota(jnp.int32, (tq, SB_KEYS), 0)
    krow = lax.broadcasted_iota(jnp.int32, (SB_KEYS, SB_KEYS), 0)
    kcol = lax.broadcasted_iota(jnp.int32, (SB_KEYS, SB_KEYS), 1)
    return diff, krow, kcol


def _sb_scale(d):
    scale = 1.0 / math.sqrt(d)
    assert math.frexp(scale)[0] == 0.5, "the scale is folded into bf16 queries: it must be a power of two"
    return scale


def _key_rows(j):
    return pl.ds(pl.multiple_of(j * SB_KEYS, SB_KEYS), SB_KEYS)


def _pairs(tiles, per_tile, one, carry):
    if per_tile % 2:
        return lax.fori_loop(0, tiles * per_tile, one, carry)
    return lax.fori_loop(0, tiles * (per_tile // 2), lambda s, cr: one(2 * s + 1, one(2 * s, cr)), carry)


def _sb_fwd(q, k, v, *, name):
    h, t, d = q.shape
    tq = _tile(t, SB_QUERIES)
    nq = t // tq
    kpq = tq // SB_KEYS
    scale = _sb_scale(d)

    def body(q_ref, k_ref, v_ref, o_ref, lt_ref):
        i = pl.program_id(1)
        qs = (q_ref[0].astype(F32) * scale).astype(BF16)
        diff, krow, kcol = _sb_iota(tq)
        later = (krow > kcol).astype(F32)

        def block(j, carry, valid):
            acc, cl = carry
            rows = _key_rows(j)
            ls, lg = _sb_logits(qs, k_ref[0, rows, :], valid)
            cs = _ones_dot(later, lg, ones_left=False, pieces=SB_PIECES)
            att = jnp.exp(ls + (cs + cl))
            if valid is not None:
                att = jnp.where(valid, att, 0.0)
            acc = acc + _nn(att.astype(BF16), v_ref[0, rows, :])
            return acc, cl + (cs[:, 0:1] + lg[:, 0:1])

        carry = (jnp.zeros((tq, d), F32), jnp.zeros((tq, 1), F32))
        for m in range(kpq - 1, -1, -1):
            carry = block(i * kpq + m, carry, diff < -m * SB_KEYS)
        nb = i * kpq
        acc, cl = _pairs(i, kpq, lambda s, cr: block(nb - 1 - s, cr, None), carry)
        o_ref[0] = acc
        lt_ref[0] = cl

    qs = pl.BlockSpec((1, tq, d), lambda hh, i: (hh, i, 0))
    ls = pl.BlockSpec((1, tq, 1), lambda hh, i: (hh, i, 0))
    ks = pl.BlockSpec((1, t, d), lambda hh, i: (hh, 0, 0))
    return pl.pallas_call(
        body,
        grid=(h, nq),
        in_specs=[qs, ks, ks],
        out_specs=[qs, ls],
        out_shape=[jax.ShapeDtypeStruct((h, t, d), F32), jax.ShapeDtypeStruct((h, t, 1), F32)],
        compiler_params=_cparams(("parallel", "arbitrary")),
        name=name,
    )(q, k, v)


def _sb_bwd(q, k, v, lt, do, *, name):
    h, t, d = q.shape
    tq = _tile(t, SB_QUERIES)
    nq = t // tq
    kpq = tq // SB_KEYS
    scale = _sb_scale(d)
    last = SB_KEYS - 1

    def body(q_ref, k_ref, v_ref, lt_ref, do_ref, dq_ref, dk_ref, dv_ref):
        i = pl.program_id(1)

        @pl.when(i == 0)
        def _():
            dk_ref[...] = jnp.zeros_like(dk_ref)
            dv_ref[...] = jnp.zeros_like(dv_ref)

        qs = (q_ref[0].astype(F32) * scale).astype(BF16)
        do16 = do_ref[0].astype(BF16)
        ltot = lt_ref[0]
        diff, krow, kcol = _sb_iota(tq)
        upto = (krow <= kcol).astype(F32)
        before = (krow < kcol).astype(F32)

        def block(j, carry, valid):
            dq, pl_sum, pg_sum = carry
            rows = _key_rows(j)
            kv = k_ref[0, rows, :]
            vv = v_ref[0, rows, :]
            ls, lg = _sb_logits(qs, kv, valid)
            pre = _ones_dot(upto, lg, ones_left=False, pieces=SB_PIECES)
            att = jnp.exp(ls + (ltot - (pre + pl_sum)))
            if valid is not None:
                att = jnp.where(valid, att, 0.0)
            g = att * _nt(do16, vv)
            gpre = _ones_dot(before, g, ones_left=False, pieces=SB_PIECES)
            sig = jnp.exp(ls)
            dz16 = (g - sig * (g + (gpre + pg_sum))).astype(BF16)
            if valid is not None:
                dz16 = jnp.where(valid, dz16, jnp.zeros_like(dz16))
            dq = dq + _nn(dz16, kv)
            dk_ref[0, rows, :] += _tn(dz16, qs)
            dv_ref[0, rows, :] += _tn(att.astype(BF16), do16)
            return dq, pl_sum + pre[:, last:], pg_sum + (gpre[:, last:] + g[:, last:])

        zero = jnp.zeros((tq, 1), F32)
        nb = i * kpq
        carry = _pairs(i, kpq, lambda j, cr: block(j, cr, None), (jnp.zeros((tq, d), F32), zero, zero))
        for m in range(kpq):
            carry = block(nb + m, carry, diff < -m * SB_KEYS)
        dq_ref[0] = carry[0] * scale

    qs = pl.BlockSpec((1, tq, d), lambda hh, i: (hh, i, 0))
    ls = pl.BlockSpec((1, tq, 1), lambda hh, i: (hh, i, 0))
    ks = pl.BlockSpec((1, t, d), lambda hh, i: (hh, 0, 0))
    full = jax.ShapeDtypeStruct((h, t, d), F32)
    return pl.pallas_call(
        body,
        grid=(h, nq),
        in_specs=[qs, ks, ks, ls, qs],
        out_specs=[qs, ks, ks],
        out_shape=[full, full, full],
        compiler_params=_cparams(("parallel", "arbitrary")),
        name=name,
    )(q, k, v, lt, do)


def _row_tile(rows, cols):
    return _tile(rows, tuple(r for r in (2048, 1024, 512, 256, 128, 64, 32, 16, 8) if r * cols * 4 <= ADAM_BLOCK_BYTES))


def _sum_leading(x, *, name):
    n, rows, cols = x.shape
    tr = _row_tile(rows, cols)

    def body(x_ref, o_ref):
        acc = x_ref[0].astype(F32)
        for q in range(1, n):
            acc = acc + x_ref[q].astype(F32)
        o_ref[...] = acc

    return pl.pallas_call(
        body,
        grid=(rows // tr,),
        in_specs=[pl.BlockSpec((n, tr, cols), lambda i: (0, i, 0))],
        out_specs=pl.BlockSpec((tr, cols), lambda i: (i, 0)),
        out_shape=jax.ShapeDtypeStruct((rows, cols), F32),
        compiler_params=_cparams(("parallel",)),
        name=name,
    )(x)


def _pair_add(g4h, recv, c, *, out_dtype, name):
    n, _, rows, cols = g4h.shape
    tr = _row_tile(rows, cols)

    def body(c_ref, g_ref, r_ref, o_ref):
        o_ref[...] = (g_ref[...] + r_ref[...]).astype(o_ref.dtype)

    blk = pl.BlockSpec((1, tr, cols), lambda q, i, c_ref: (q, i, 0))
    return pl.pallas_call(
        body,
        grid_spec=pltpu.PrefetchScalarGridSpec(
            num_scalar_prefetch=1,
            grid=(n, rows // tr),
            in_specs=[pl.BlockSpec((1, None, tr, cols), lambda q, i, c_ref: (q, c_ref[0], i, 0)), blk],
            out_specs=blk),
        out_shape=jax.ShapeDtypeStruct((n, rows, cols), out_dtype),
        compiler_params=_cparams(("parallel", "parallel")),
        name=name,
    )(c.reshape(1).astype(jnp.int32), g4h, recv)


ANY = pl.BlockSpec(memory_space=pl.ANY)


def _other_chips(x, y):
    return [(1 - x, y), (x, 1 - y), (1 - x, 1 - y)]


def _gather_chips(shard, *, name):
    def body(x_ref, o_ref, send_sems, recv_sems, local_sem):
        x, y, c = lax.axis_index("x"), lax.axis_index("y"), lax.axis_index("c")
        me = 2 * x + y
        mine = pltpu.make_async_copy(x_ref, o_ref.at[me], local_sem)
        mine.start()
        chips = _other_chips(x, y)
        sends = [pltpu.make_async_remote_copy(src_ref=x_ref, dst_ref=o_ref.at[me], send_sem=send_sems.at[q],
                                              recv_sem=recv_sems.at[q], device_id=(px, py, c), device_id_type=MESH)
                 for q, (px, py) in enumerate(chips)]
        for cp in sends:
            cp.start()
        for q, (px, py) in enumerate(chips):
            pltpu.make_async_remote_copy(src_ref=x_ref, dst_ref=o_ref.at[2 * px + py], send_sem=send_sems.at[q],
                                         recv_sem=recv_sems.at[q], device_id=(px, py, c), device_id_type=MESH).wait_recv()
        for cp in sends:
            cp.wait_send()
        mine.wait()

    return pl.pallas_call(
        body,
        in_specs=[ANY],
        out_specs=ANY,
        out_shape=jax.ShapeDtypeStruct((4,) + shard.shape, shard.dtype),
        scratch_shapes=[pltpu.SemaphoreType.DMA((3,)), pltpu.SemaphoreType.DMA((3,)), pltpu.SemaphoreType.DMA],
        compiler_params=pltpu.CompilerParams(has_side_effects=True),
        name=name,
    )(shard)


def _scatter_chips(parts, *, name):
    def body(p_ref, o_ref, send_sems, recv_sems):
        x, y, c = lax.axis_index("x"), lax.axis_index("y"), lax.axis_index("c")
        me = 2 * x + y
        chips = _other_chips(x, y)
        sends = [pltpu.make_async_remote_copy(src_ref=p_ref.at[2 * px + py], dst_ref=o_ref.at[me], send_sem=send_sems.at[q],
                                              recv_sem=recv_sems.at[q], device_id=(px, py, c), device_id_type=MESH)
                 for q, (px, py) in enumerate(chips)]
        for cp in sends:
            cp.start()
        for q, (px, py) in enumerate(chips):
            pltpu.make_async_remote_copy(src_ref=p_ref.at[me], dst_ref=o_ref.at[2 * px + py], send_sem=send_sems.at[q],
                                         recv_sem=recv_sems.at[q], device_id=(px, py, c), device_id_type=MESH).wait_recv()
        for cp in sends:
            cp.wait_send()

    return pl.pallas_call(
        body,
        in_specs=[ANY],
        out_specs=ANY,
        out_shape=jax.ShapeDtypeStruct(parts.shape, parts.dtype),
        scratch_shapes=[pltpu.SemaphoreType.DMA((3,)), pltpu.SemaphoreType.DMA((3,))],
        compiler_params=pltpu.CompilerParams(has_side_effects=True),
        name=name,
    )(parts)


def _gather_chips_halves(shard, *, name):
    def body(x_ref, o_ref, send_sems, recv_sems):
        x, y, c = lax.axis_index("x"), lax.axis_index("y"), lax.axis_index("c")
        me, sibling = 2 * x + y, (x, y, 1 - c)
        chips = _other_chips(x, y)

        def copy(q, src, dst, to):
            return pltpu.make_async_remote_copy(src_ref=src, dst_ref=dst, send_sem=send_sems.at[q],
                                                recv_sem=recv_sems.at[q], device_id=to, device_id_type=MESH)

        sends = [copy(q, x_ref.at[c], o_ref.at[me, c], (px, py, c)) for q, (px, py) in enumerate(chips)]
        for cp in sends:
            cp.start()
        passed = []
        for q, (px, py) in enumerate(chips):
            slot = o_ref.at[2 * px + py, c]
            copy(q, x_ref.at[c], slot, (px, py, c)).wait_recv()
            passed.append(copy(3 + q, slot, slot, sibling))
            passed[-1].start()
        for q, (px, py) in enumerate(chips):
            copy(3 + q, x_ref.at[1 - c], o_ref.at[2 * px + py, 1 - c], sibling).wait_recv()
        for cp in sends + passed:
            cp.wait_send()

    return pl.pallas_call(
        body,
        in_specs=[ANY],
        out_specs=ANY,
        out_shape=jax.ShapeDtypeStruct((N_CHIPS,) + shard.shape, shard.dtype),
        scratch_shapes=[pltpu.SemaphoreType.DMA((6,)), pltpu.SemaphoreType.DMA((6,))],
        compiler_params=pltpu.CompilerParams(has_side_effects=True),
        name=name,
    )(shard)


def _swap_other_half(g4h, *, name):
    n = g4h.shape[0]

    def body(g_ref, o_ref, send_sem, recv_sem):
        x, y, c = lax.axis_index("x"), lax.axis_index("y"), lax.axis_index("c")
        cp = pltpu.make_async_remote_copy(src_ref=g_ref.at[pl.ds(0, n), 1 - c], dst_ref=o_ref, send_sem=send_sem,
                                          recv_sem=recv_sem, device_id=(x, y, 1 - c), device_id_type=MESH)
        cp.start()
        cp.wait()

    return pl.pallas_call(
        body,
        in_specs=[ANY],
        out_specs=ANY,
        out_shape=jax.ShapeDtypeStruct((n,) + g4h.shape[2:], g4h.dtype),
        scratch_shapes=[pltpu.SemaphoreType.DMA, pltpu.SemaphoreType.DMA],
        compiler_params=pltpu.CompilerParams(has_side_effects=True),
        name=name,
    )(g4h)


def _join_halves(half, *, name):
    def body(h_ref, o_ref, send_sem, recv_sem):
        x, y, c = lax.axis_index("x"), lax.axis_index("y"), lax.axis_index("c")
        cp = pltpu.make_async_remote_copy(src_ref=h_ref, dst_ref=o_ref.at[c], send_sem=send_sem, recv_sem=recv_sem,
                                          device_id=(x, y, 1 - c), device_id_type=MESH)
        cp.start()
        pltpu.make_async_remote_copy(src_ref=h_ref, dst_ref=o_ref.at[1 - c], send_sem=send_sem, recv_sem=recv_sem,
                                     device_id=(x, y, 1 - c), device_id_type=MESH).wait_recv()
        cp.wait_send()

    return pl.pallas_call(
        body,
        in_specs=[ANY],
        out_specs=ANY,
        out_shape=jax.ShapeDtypeStruct((2,) + half.shape, half.dtype),
        scratch_shapes=[pltpu.SemaphoreType.DMA, pltpu.SemaphoreType.DMA],
        compiler_params=pltpu.CompilerParams(has_side_effects=True),
        name=name,
    )(half)


def _gather_all(v, *, name):
    def body(v_ref, o_ref, send_sems, recv_sems, local_sem):
        x, y, c = lax.axis_index("x"), lax.axis_index("y"), lax.axis_index("c")
        me = 4 * x + 2 * y + c
        mine = pltpu.make_async_copy(v_ref, o_ref.at[me], local_sem)
        mine.start()
        peers = [(x ^ (q >> 2 & 1), y ^ (q >> 1 & 1), c ^ (q & 1)) for q in range(1, 8)]
        sends = [pltpu.make_async_remote_copy(src_ref=v_ref, dst_ref=o_ref.at[me], send_sem=send_sems.at[q],
                                              recv_sem=recv_sems.at[q], device_id=peer, device_id_type=MESH)
                 for q, peer in enumerate(peers)]
        for cp in sends:
            cp.start()
        for q, (px, py, pc) in enumerate(peers):
            pltpu.make_async_remote_copy(src_ref=v_ref, dst_ref=o_ref.at[4 * px + 2 * py + pc], send_sem=send_sems.at[q],
                                         recv_sem=recv_sems.at[q], device_id=(px, py, pc), device_id_type=MESH).wait_recv()
        for cp in sends:
            cp.wait_send()
        mine.wait()

    return pl.pallas_call(
        body,
        in_specs=[ANY],
        out_specs=ANY,
        out_shape=jax.ShapeDtypeStruct((8,) + v.shape, v.dtype),
        scratch_shapes=[pltpu.SemaphoreType.DMA((7,)), pltpu.SemaphoreType.DMA((7,)), pltpu.SemaphoreType.DMA],
        compiler_params=pltpu.CompilerParams(has_side_effects=True),
        name=name,
    )(v)


WEIGHTS = ['ssm_norm_w', 'ssm_in_w', 'ssm_conv_w', 'ssm_conv_b', 'ssm_dt_bias', 'ssm_a_log', 'ssm_d',
           'ssm_gate_norm_w', 'ssm_out_w', 'kv_norm_w', 'w_k', 'w_v', 'attn_norm_w', 'w_q', 'w_o',
           'ffn_norm_w', 'ffn_up_w', 'ffn_conv_w', 'ffn_conv_b', 'ffn_down_w', 'final_norm_w']
SHARD_AXIS = {'ssm_norm_w': 1, 'ssm_in_w': 2, 'ssm_conv_w': 2, 'ssm_conv_b': 1, 'ssm_gate_norm_w': 1,
              'ssm_out_w': 1, 'w_k': 0, 'w_v': 0, 'w_q': 1, 'w_o': 1, 'ffn_up_w': 2, 'ffn_conv_w': 2,
              'ffn_down_w': 1}
BIG = ['ssm_in_w', 'ssm_out_w', 'w_k', 'w_v', 'w_q', 'w_o', 'ffn_up_w', 'ffn_down_w']
SMALL = [n for n in WEIGHTS if n in SHARD_AXIS and n not in BIG]
REPLICATED = [n for n in WEIGHTS if n not in SHARD_AXIS]
N_CHIPS = 4
RS_ROW_MULT = 4096


PACK_ROWS = 16


def _piece_rows(n):
    return -(-n // (PACK_ROWS * LANES)) * PACK_ROWS


def _pack(arrs, dtype, row_mult):
    lead = arrs[0].shape[:-1]
    pieces, total = [], 0
    for a in arrs:
        n = a.shape[-1]
        rows = _piece_rows(n)
        a = a.astype(dtype)
        if rows * LANES != n:
            a = jnp.pad(a, [(0, 0)] * len(lead) + [(0, rows * LANES - n)])
        pieces.append(a.reshape(lead + (rows, LANES)))
        total += rows
    extra = -total % row_mult
    if extra:
        pieces.append(jnp.zeros(lead + (extra, LANES), dtype))
    return jnp.concatenate(pieces, axis=len(lead))


def _unpack(buf, shapes):
    lead = buf.shape[:-2]
    out, off = [], 0
    for shp in shapes:
        n = math.prod(shp)
        rows = _piece_rows(n)
        piece = lax.slice_in_dim(buf, off, off + rows, axis=len(lead)).reshape(lead + (rows * LANES,))
        out.append(piece[..., :n].reshape(lead + tuple(shp)))
        off += rows
    return out


def _set_slot(buf, piece, index):
    return lax.dynamic_update_slice_in_dim(buf, piece[None], index, axis=0)


def _to_shards(full, axis):
    return jnp.stack(jnp.split(full, N_CHIPS, axis=axis), axis=0)


def _from_shards(stacked, axis):
    return jnp.concatenate([stacked[j] for j in range(N_CHIPS)], axis=axis)


def _heads(a, h):
    t = a.shape[0]
    return a.reshape(t, h, a.shape[1] // h).transpose(1, 0, 2)


def _unheads(a):
    h, t, d = a.shape
    return a.transpose(1, 0, 2).reshape(t, h * d)


def _ffn_fwd(h, norm_w, w_up, conv_w, conv_b, w_down, tag):
    u = _rmsnorm_fwd(h, norm_w, name=f"ffn{tag}_norm")
    hid = _matmul(u, w_up, name=f"ffn{tag}_up")
    act = _conv_glu_fwd(hid, conv_w, conv_b, name=f"ffn{tag}_glu")
    out = _matmul(act, w_down, add=h, name=f"ffn{tag}_down")
    return out, (u, hid, act)


def _ffn_bwd(h, saved, dout, norm_w, w_up, conv_w, conv_b, w_down, tag):
    u, hid, act = saved
    f = w_down.shape[0]
    dact = _matmul(dout, w_down, tb=True, name=f"ffn{tag}_down_dx")
    dw_down = _matmul(act, dout, ta=True, name=f"ffn{tag}_down_dw")
    dg, dv, dwg, dwv, dbg, dbv = _conv_glu_bwd(hid, conv_w, conv_b, dact, name=f"ffn{tag}_glu_bwd")
    du = _matmul(dg, w_up[:, :f], tb=True, name=f"ffn{tag}_up_dx_g")
    du = _matmul(dv, w_up[:, f:], tb=True, add=du, name=f"ffn{tag}_up_dx_v")
    dw_up = jnp.concatenate([_matmul(u, dg, ta=True, name=f"ffn{tag}_up_dw_g"),
                             _matmul(u, dv, ta=True, name=f"ffn{tag}_up_dw_v")], axis=1)
    dh, (dnorm,) = _rmsnorm_bwd(h, [(du, norm_w)], dout, name=f"ffn{tag}_norm_bwd")
    return dh, dict(norm=dnorm[0], up=dw_up, conv_w=jnp.concatenate([dwg, dwv], axis=1),
                    conv_b=jnp.concatenate([dbg, dbv], axis=1)[0], down=dw_down)


def _step(x, target, w):
    t = x.shape[0]
    g_n, heads = SSM_GROUPS, SSM_HEADS
    r_h = heads // g_n
    di = D_INNER
    zx_cols = di + CONV_DIM
    w_in = w['ssm_in_w'][0]
    w_zx = w_in[:, :zx_cols]
    w_dt = jnp.pad(w_in[:, zx_cols:], ((0, 0), (0, LANES - heads)))
    conv_w, conv_b = w['ssm_conv_w'][0], w['ssm_conv_b'][0]
    hp = jnp.stack([w['ssm_dt_bias'][0], w['ssm_a_log'][0], w['ssm_d'][0]], axis=0).reshape(3, g_n, r_h)
    hpc, hpr = hp.transpose(1, 0, 2), hp.transpose(1, 2, 0)
    w_out = w['ssm_out_w'][0]
    w_q, w_o = w['w_q'][0], w['w_o'][0]

    h0 = x
    u0 = _rmsnorm_fwd(h0, w['ssm_norm_w'][0], name="ssm_norm")
    zx = _matmul(u0, w_zx, name="ssm_in_zx")
    dt_raw = _matmul(u0, w_dt, name="ssm_in_dt")[:, :heads]
    dtg = dt_raw.reshape(t, g_n, r_h)
    dtc, dtr = dtg.transpose(1, 0, 2), dtg.transpose(1, 2, 0)
    xbc = _conv_silu_fwd(zx, conv_w, conv_b, x_off=di, name="ssm_conv")
    y, prev = _ssd_fwd(xbc, dtc, dtr, hpc, hpr, name="ssd_fwd")
    yn = _gate_norm_fwd(y, zx, w['ssm_gate_norm_w'][0], name="ssm_gate_norm")
    h1 = _matmul(yn, w_out, add=h0, name="ssm_out")
    h2, ffn0 = _ffn_fwd(h1, w['ffn_norm_w'][0], w['ffn_up_w'][0], w['ffn_conv_w'][0], w['ffn_conv_b'][0],
                        w['ffn_down_w'][0], 0)
    hk = _rmsnorm_fwd(h2, w['kv_norm_w'], name="kv_norm")
    qn = _rmsnorm_fwd(h2, w['attn_norm_w'][0], name="attn_norm")
    k2 = _matmul(hk, w['w_k'], out_dtype=BF16, name="attn_k")
    v2 = _matmul(hk, w['w_v'], out_dtype=BF16, name="attn_v")
    q2 = _matmul(qn, w_q, out_dtype=BF16, name="attn_q")
    qh, kh, vh = _heads(q2, SB_HEADS), _heads(k2, SB_HEADS), _heads(v2, SB_HEADS)
    oh, lt = _sb_fwd(qh, kh, vh, name="sb_fwd")
    o2 = _unheads(oh)
    h3 = _matmul(o2, w_o, add=h2, name="attn_o")
    h4, ffn1 = _ffn_fwd(h3, w['ffn_norm_w'][1], w['ffn_up_w'][1], w['ffn_conv_w'][1], w['ffn_conv_b'][1],
                        w['ffn_down_w'][1], 1)
    loss_p, dh4, d_final = _loss_head(h4, w['final_norm_w'], target, name="loss_head")

    dh3, g1 = _ffn_bwd(h3, ffn1, dh4, w['ffn_norm_w'][1], w['ffn_up_w'][1], w['ffn_conv_w'][1],
                       w['ffn_conv_b'][1], w['ffn_down_w'][1], 1)
    do2 = _matmul(dh3, w_o, tb=True, name="attn_o_dx")
    dw_o = _matmul(o2, dh3, ta=True, name="attn_o_dw")
    dqh, dkh, dvh = _sb_bwd(qh, kh, vh, lt, _heads(do2, SB_HEADS), name="sb_bwd")
    dq2, dk2, dv2 = _unheads(dqh), _unheads(dkh), _unheads(dvh)
    dqn = _matmul(dq2, w_q, tb=True, name="attn_q_dx")
    dw_q = _matmul(qn, dq2, ta=True, name="attn_q_dw")
    dhk = _matmul(dk2, w['w_k'], tb=True, name="attn_k_dx")
    dhk = _matmul(dv2, w['w_v'], tb=True, add=dhk, name="attn_v_dx")
    dw_k = _matmul(hk, dk2, ta=True, name="attn_k_dw")
    dw_v = _matmul(hk, dv2, ta=True, name="attn_v_dw")
    dh2, (d_attn_norm, d_kv_norm) = _rmsnorm_bwd(h2, [(dqn, w['attn_norm_w'][0]), (dhk, w['kv_norm_w'])], dh3,
                                                 name="attn_norms_bwd")
    dh1, g0 = _ffn_bwd(h1, ffn0, dh2, w['ffn_norm_w'][0], w['ffn_up_w'][0], w['ffn_conv_w'][0],
                       w['ffn_conv_b'][0], w['ffn_down_w'][0], 0)
    dyn = _matmul(dh1, w_out, tb=True, name="ssm_out_dx")
    dw_out = _matmul(yn, dh1, ta=True, name="ssm_out_dw")
    dy, dz, d_gate = _gate_norm_bwd(y, zx, w['ssm_gate_norm_w'][0], dyn, name="ssm_gate_norm_bwd")
    dxs, dbm, dcm, ddt_g, hg = _ssd_bwd(xbc, dtc, dtr, hpc, hpr, prev, dy, name="ssd_bwd")
    dxbc = jnp.concatenate([dxs, dbm, dcm], axis=1)
    dxbc_pre, d_conv_w, d_conv_b = _conv_silu_bwd(zx, conv_w, conv_b, dxbc, x_off=di, name="ssm_conv_bwd")
    dzx = jnp.concatenate([dz, dxbc_pre], axis=1)
    ddt = jnp.pad(ddt_g.transpose(1, 0, 2).reshape(t, heads), ((0, 0), (0, LANES - heads)))
    du0 = _matmul(dzx, w_zx, tb=True, name="ssm_in_zx_dx")
    du0 = _matmul(ddt, w_dt, tb=True, add=du0, name="ssm_in_dt_dx")
    dw_in = jnp.concatenate([_matmul(u0, dzx, ta=True, name="ssm_in_zx_dw"),
                             _matmul(u0, ddt, ta=True, name="ssm_in_dt_dw")[:, :heads]], axis=1)
    dx, (d_ssm_norm,) = _rmsnorm_bwd(h0, [(du0, w['ssm_norm_w'][0])], dh1, name="ssm_norm_bwd")

    hgr = hg.transpose(1, 0, 2).reshape(3, heads)
    grads = {
        'ssm_norm_w': d_ssm_norm, 'ssm_in_w': dw_in[None], 'ssm_conv_w': d_conv_w[None], 'ssm_conv_b': d_conv_b,
        'ssm_dt_bias': hgr[0:1], 'ssm_a_log': hgr[1:2], 'ssm_d': hgr[2:3], 'ssm_gate_norm_w': d_gate,
        'ssm_out_w': dw_out[None], 'kv_norm_w': d_kv_norm[0], 'w_k': dw_k, 'w_v': dw_v, 'attn_norm_w': d_attn_norm,
        'w_q': dw_q[None], 'w_o': dw_o[None], 'ffn_norm_w': jnp.stack([g0['norm'], g1['norm']]),
        'ffn_up_w': jnp.stack([g0['up'], g1['up']]), 'ffn_conv_w': jnp.stack([g0['conv_w'], g1['conv_w']]),
        'ffn_conv_b': jnp.stack([g0['conv_b'], g1['conv_b']]), 'ffn_down_w': jnp.stack([g0['down'], g1['down']]),
        'final_norm_w': d_final[0],
    }
    return loss_p, dx, grads


def kernel(x, ssm_norm_w, ssm_in_w, ssm_conv_w, ssm_conv_b, ssm_dt_bias, ssm_a_log, ssm_d, ssm_gate_norm_w, ssm_out_w, kv_norm_w, w_k, w_v, attn_norm_w, w_q, w_o, ffn_norm_w, ffn_up_w, ffn_conv_w, ffn_conv_b, ffn_down_w, final_norm_w, loss_target, m_ssm_norm_w, m_ssm_in_w, m_ssm_conv_w, m_ssm_conv_b, m_ssm_dt_bias, m_ssm_a_log, m_ssm_d, m_ssm_gate_norm_w, m_ssm_out_w, m_kv_norm_w, m_w_k, m_w_v, m_attn_norm_w, m_w_q, m_w_o, m_ffn_norm_w, m_ffn_up_w, m_ffn_conv_w, m_ffn_conv_b, m_ffn_down_w, m_final_norm_w, v_ssm_norm_w, v_ssm_in_w, v_ssm_conv_w, v_ssm_conv_b, v_ssm_dt_bias, v_ssm_a_log, v_ssm_d, v_ssm_gate_norm_w, v_ssm_out_w, v_kv_norm_w, v_w_k, v_w_v, v_attn_norm_w, v_w_q, v_w_o, v_ffn_norm_w, v_ffn_up_w, v_ffn_conv_w, v_ffn_conv_b, v_ffn_down_w, v_final_norm_w):
    args = (ssm_norm_w, ssm_in_w, ssm_conv_w, ssm_conv_b, ssm_dt_bias, ssm_a_log, ssm_d, ssm_gate_norm_w, ssm_out_w, kv_norm_w, w_k, w_v, attn_norm_w, w_q, w_o, ffn_norm_w, ffn_up_w, ffn_conv_w, ffn_conv_b, ffn_down_w, final_norm_w)
    moms = (m_ssm_norm_w, m_ssm_in_w, m_ssm_conv_w, m_ssm_conv_b, m_ssm_dt_bias, m_ssm_a_log, m_ssm_d, m_ssm_gate_norm_w, m_ssm_out_w, m_kv_norm_w, m_w_k, m_w_v, m_attn_norm_w, m_w_q, m_w_o, m_ffn_norm_w, m_ffn_up_w, m_ffn_conv_w, m_ffn_conv_b, m_ffn_down_w, m_final_norm_w)
    vels = (v_ssm_norm_w, v_ssm_in_w, v_ssm_conv_w, v_ssm_conv_b, v_ssm_dt_bias, v_ssm_a_log, v_ssm_d, v_ssm_gate_norm_w, v_ssm_out_w, v_kv_norm_w, v_w_k, v_w_v, v_attn_norm_w, v_w_q, v_w_o, v_ffn_norm_w, v_ffn_up_w, v_ffn_conv_w, v_ffn_conv_b, v_ffn_down_w, v_final_norm_w)
    local = dict(zip(WEIGHTS, args))
    m_in = dict(zip(WEIGHTS, moms))
    v_in = dict(zip(WEIGHTS, vels))
    c = lax.axis_index("c")
    chip = 2 * lax.axis_index("x") + lax.axis_index("y")

    big_shard = _pack([local[n].reshape(-1) for n in BIG], BF16, 32)
    big16 = _gather_chips_halves(big_shard.reshape(2, -1, LANES), name="gather_big").reshape((N_CHIPS,) + big_shard.shape)
    big16 = _set_slot(big16, big_shard, chip)
    small32 = _gather_chips(_pack([local[n].reshape(-1) for n in SMALL], F32, 8), name="gather_small")
    full = {n: local[n] for n in REPLICATED}
    for names, buf in ((BIG, big16), (SMALL, small32)):
        for n, st in zip(names, _unpack(buf, [local[n].shape for n in names])):
            full[n] = _from_shards(st, SHARD_AXIS[n])

    loss_p, dx, grads = _step(x[0], loss_target[0], full)

    sharded = BIG + SMALL
    g4 = _pack([_to_shards(grads[n], SHARD_AXIS[n]).reshape(N_CHIPS, -1) for n in sharded], F32, RS_ROW_MULT)
    rows = g4.shape[1]
    g4h = g4.reshape(N_CHIPS, 2, rows // 2, LANES)
    pair = _pair_add(g4h, _swap_other_half(g4h, name="rs_pair_swap"), c, out_dtype=BF16, name="rs_pair_add")
    mine = lax.dynamic_index_in_dim(pair, chip, axis=0, keepdims=False)
    half = _sum_leading(_set_slot(_scatter_chips(pair, name="rs_chip_scatter"), mine, chip), name="rs_chip_sum")
    shard = _set_slot(_join_halves(half, name="rs_half_join"), half, c).reshape(rows, LANES)
    gshard = dict(zip(sharded, _unpack(shard, [local[n].shape for n in sharded])))

    rep = _pack([loss_p.reshape(-1)] + [grads[n].reshape(-1) for n in REPLICATED], F32, 8)
    tot = _sum_leading(_gather_all(rep, name="ar_gather"), name="ar_sum")
    parts = _unpack(tot, [(LANES,)] + [local[n].shape for n in REPLICATED])
    loss = jnp.sum(parts[0])
    gshard.update(dict(zip(REPLICATED, parts[1:])))

    deltas, new_m, new_v = [], [], []
    for n in WEIGHTS:
        d, nm, nv = _adamw(local[n], gshard[n], m_in[n], v_in[n], name=f"adamw_{n}")
        deltas.append(d)
        new_m.append(nm)
        new_v.append(nv)
    return (loss, dx[None], *[gshard[n] for n in WEIGHTS], *deltas, *new_m, *new_v)
```

```python
import math

import jax
import jax.numpy as jnp
from jax import lax
from jax.experimental import pallas as pl
from jax.experimental.pallas import tpu as pltpu

D_MODEL = 1024
D_INNER = 2048
SSM_HEAD_DIM = 64
SSM_HEADS = 32
SSM_GROUPS = 4
SSM_STATE = 128
SSM_CONV = 4
SSM_CHUNK = 128
GN = SSM_GROUPS * SSM_STATE
CONV_DIM = D_INNER + 2 * GN
SB_HEADS = 16
SB_HEAD_DIM = 64
D_FF = 2816
FFN_CONV = 3
EPS = 1e-6
ADAM_LR = 0.001
ADAM_B1 = 0.9
ADAM_B2 = 0.999
ADAM_EPS = 1e-08
ADAM_WD = 0.01
ADAM_STEP = 10

LANES = 128
SUBLANES = 8
VMEM_LIMIT = 48 * 1024 * 1024
ADAM_BLOCK_BYTES = 1 << 20
F32 = jnp.float32
BF16 = jnp.bfloat16
MESH = pl.DeviceIdType.MESH


def _cparams(sem=None):
    return pltpu.CompilerParams(dimension_semantics=sem, vmem_limit_bytes=VMEM_LIMIT)


def _tile(n, cands):
    for c in cands:
        if n % c == 0:
            return c
    return n


def _nt(a, b):
    return lax.dot_general(a, b, (((1,), (1,)), ((), ())), preferred_element_type=F32)


def _tn(a, b):
    return lax.dot_general(a, b, (((0,), (0,)), ((), ())), preferred_element_type=F32)


def _nn(a, b):
    return jnp.dot(a, b, preferred_element_type=F32)


def _split(x, pieces):
    out = []
    for _ in range(pieces - 1):
        h = x.astype(BF16)
        out.append(h)
        x = x - h.astype(F32)
    out.append(x.astype(BF16))
    return out


def _ones_dot(ones, x, *, ones_left, pieces=3):
    o16 = ones.astype(BF16)
    acc = None
    for piece in _split(x, pieces):
        term = _nn(o16, piece) if ones_left else _nn(piece, o16)
        acc = term if acc is None else acc + term
    return acc


def _row_sums(x):
    return _ones_dot(jnp.ones((x.shape[1], LANES), F32), x, ones_left=False)


def _softplus(x):
    return jnp.maximum(x, 0.0) + jnp.log(1.0 + jnp.exp(-jnp.abs(x)))


def _sigmoid(x):
    e = jnp.exp(-jnp.abs(x))
    r = 1.0 / (1.0 + e)
    return jnp.where(x >= 0, r, e * r)


MM_TILE_MAX = 1408
MM_VMEM_BUDGET = 40 * 1024 * 1024


def _divisors(n, cap):
    out = [d for d in range(min(cap, n) // LANES * LANES, 0, -LANES) if n % d == 0]
    return out or [n]


def _mm_tiles(m, n, k, a_bytes, b_bytes, o_bytes, add_bytes):
    best = None
    for tm in _divisors(m, MM_TILE_MAX):
        for tn in _divisors(n, MM_TILE_MAX):
            for tk in _divisors(k, MM_TILE_MAX):
                vmem = 2 * (tm * tk * a_bytes + tk * tn * b_bytes + tm * tn * (o_bytes + add_bytes)) + tm * tn * 4
                if vmem > MM_VMEM_BUDGET:
                    continue
                score = (tm * tn * tk, tm * tn)
                if best is None or score > best[0]:
                    best = (score, (tm, tn, tk))
    return best[1]
def _matmul(a, b, *, ta=False, tb=False, add=None, out_dtype=F32, name):
    m, k = (a.shape[1], a.shape[0]) if ta else a.shape
    n = b.shape[0] if tb else b.shape[1]
    assert (b.shape[1] if tb else b.shape[0]) == k
    tm, tn, tk = _mm_tiles(m, n, k, a.dtype.itemsize, b.dtype.itemsize, jnp.dtype(out_dtype).itemsize,
                           0 if add is None else add.dtype.itemsize)
    nk = k // tk

    def body(*refs):
        if add is None:
            a_ref, b_ref, o_ref, acc_ref = refs
            add_ref = None
        else:
            a_ref, b_ref, add_ref, o_ref, acc_ref = refs
        kk = pl.program_id(2)

        @pl.when(kk == 0)
        def _():
            acc_ref[...] = jnp.zeros_like(acc_ref)

        av = a_ref[...].astype(BF16)
        bv = b_ref[...].astype(BF16)
        dn = (((0 if ta else 1,), (1 if tb else 0,)), ((), ()))
        acc_ref[...] += lax.dot_general(av, bv, dn, preferred_element_type=F32)

        @pl.when(kk == nk - 1)
        def _():
            r = acc_ref[...]
            if add_ref is not None:
                r = r + add_ref[...].astype(F32)
            o_ref[...] = r.astype(o_ref.dtype)

    a_spec = pl.BlockSpec((tk, tm), lambda i, j, kk: (kk, i)) if ta else pl.BlockSpec((tm, tk), lambda i, j, kk: (i, kk))
    b_spec = pl.BlockSpec((tn, tk), lambda i, j, kk: (j, kk)) if tb else pl.BlockSpec((tk, tn), lambda i, j, kk: (kk, j))
    in_specs = [a_spec, b_spec]
    args = [a, b]
    if add is not None:
        in_specs.append(pl.BlockSpec((tm, tn), lambda i, j, kk: (i, j)))
        args.append(add)
    return pl.pallas_call(
        body,
        grid=(m // tm, n // tn, nk),
        in_specs=in_specs,
        out_specs=pl.BlockSpec((tm, tn), lambda i, j, kk: (i, j)),
        out_shape=jax.ShapeDtypeStruct((m, n), out_dtype),
        scratch_shapes=[pltpu.VMEM((tm, tn), F32)],
        compiler_params=_cparams(("parallel", "parallel", "arbitrary")),
        name=name,
    )(*args)


def _rmsnorm_fwd(x, w, *, name):
    t, d = x.shape
    tb = _tile(t, (512, 256, 128))

    def body(x_ref, w_ref, o_ref):
        xv = x_ref[...]
        r = lax.rsqrt(jnp.mean(xv * xv, axis=-1, keepdims=True) + EPS)
        o_ref[...] = (xv * r * w_ref[...]).astype(o_ref.dtype)

    return pl.pallas_call(
        body,
        grid=(t // tb,),
        in_specs=[pl.BlockSpec((tb, d), lambda i: (i, 0)), pl.BlockSpec((1, d), lambda i: (0, 0))],
        out_specs=pl.BlockSpec((tb, d), lambda i: (i, 0)),
        out_shape=jax.ShapeDtypeStruct((t, d), BF16),
        compiler_params=_cparams(("parallel",)),
        name=name,
    )(x, w.reshape(1, d))


def _rmsnorm_bwd(x, dys, dres, *, name):
    t, d = x.shape
    tb = _tile(t, (256, 128))
    nn = len(dys)
    has_res = dres is not None

    def body(*refs):
        x_ref = refs[0]
        dy_refs = refs[1:1 + nn]
        w_refs = refs[1 + nn:1 + 2 * nn]
        pos = 1 + 2 * nn
        res_ref = refs[pos] if has_res else None
        pos += 1 if has_res else 0
        dx_ref = refs[pos]
        dw_refs = refs[pos + 1:pos + 1 + nn]
        i = pl.program_id(0)
        xv = x_ref[...]
        r = lax.rsqrt(jnp.mean(xv * xv, axis=-1, keepdims=True) + EPS)
        xn = xv * r
        dx = res_ref[...] if has_res else jnp.zeros_like(xv)
        for q in range(nn):
            dy = dy_refs[q][...].astype(F32)
            g = dy * w_refs[q][...]
            dx = dx + r * (g - xn * jnp.mean(g * xn, axis=-1, keepdims=True))
            dwp = jnp.sum(dy * xn, axis=0, keepdims=True)

            @pl.when(i == 0)
            def _(q=q, dwp=dwp):
                dw_refs[q][...] = dwp

            @pl.when(i > 0)
            def _(q=q, dwp=dwp):
                dw_refs[q][...] += dwp
        dx_ref[...] = dx

    row = pl.BlockSpec((tb, d), lambda i: (i, 0))
    vec = pl.BlockSpec((1, d), lambda i: (0, 0))
    in_specs = [row] + [row] * nn + [vec] * nn + ([row] if has_res else [])
    args = [x] + [p[0] for p in dys] + [p[1].reshape(1, d) for p in dys] + ([dres] if has_res else [])
    outs = pl.pallas_call(
        body,
        grid=(t // tb,),
        in_specs=in_specs,
        out_specs=[row] + [vec] * nn,
        out_shape=[jax.ShapeDtypeStruct((t, d), F32)] + [jax.ShapeDtypeStruct((1, d), F32)] * nn,
        compiler_params=_cparams(("arbitrary",)),
        name=name,
    )(*args)
    return outs[0], list(outs[1:])


def _loss_head(x, w, target, *, name):
    t, d = x.shape
    tb = _tile(t, (256, 128))

    def body(x_ref, w_ref, t_ref, loss_ref, dx_ref, dw_ref):
        i = pl.program_id(0)
        xv = x_ref[...]
        wv = w_ref[...]
        r = lax.rsqrt(jnp.mean(xv * xv, axis=-1, keepdims=True) + EPS)
        xn = xv * r
        e = xn * wv - t_ref[...]
        lp = 0.5 * jnp.sum(jnp.mean(e * e, axis=-1, keepdims=True), axis=0, keepdims=True)
        dy = e * (1.0 / d)
        g = dy * wv
        dx_ref[...] = r * (g - xn * jnp.mean(g * xn, axis=-1, keepdims=True))
        dwp = jnp.sum(dy * xn, axis=0, keepdims=True)
        lpv = jnp.broadcast_to(lp, (1, LANES)) * (1.0 / LANES)

        @pl.when(i == 0)
        def _():
            dw_ref[...] = dwp
            loss_ref[...] = lpv

        @pl.when(i > 0)
        def _():
            dw_ref[...] += dwp
            loss_ref[...] += lpv

    row = pl.BlockSpec((tb, d), lambda i: (i, 0))
    vec = pl.BlockSpec((1, d), lambda i: (0, 0))
    return pl.pallas_call(
        body,
        grid=(t // tb,),
        in_specs=[row, vec, row],
        out_specs=[pl.BlockSpec((1, LANES), lambda i: (0, 0)), row, vec],
        out_shape=[jax.ShapeDtypeStruct((1, LANES), F32), jax.ShapeDtypeStruct((t, d), F32),
                   jax.ShapeDtypeStruct((1, d), F32)],
        compiler_params=_cparams(("arbitrary",)),
        name=name,
    )(x, w.reshape(1, d), target)


ROW_CHUNK = 64
PAD = SUBLANES


def _shifted(pad_ref, r0, rows, back):
    return pad_ref[pl.ds(PAD + r0 - back, rows), :]


def _conv_taps(pad_ref, w_ref, r0, rows, kw):
    acc = None
    for j in range(kw):
        term = _shifted(pad_ref, r0, rows, kw - 1 - j) * w_ref[j:j + 1, :]
        acc = term if acc is None else acc + term
    return acc


def _fill_pad(pad_ref, x_ref, t):
    pad_ref[0:PAD, :] = jnp.zeros((PAD, pad_ref.shape[1]), F32)
    pad_ref[pl.ds(PAD + t, PAD), :] = jnp.zeros((PAD, pad_ref.shape[1]), F32)
    pad_ref[pl.ds(PAD, t), :] = x_ref[...].astype(F32)


def _conv_silu_fwd(x, w, b, *, x_off=0, name):
    t = x.shape[0]
    kw, c = w.shape
    cw = _tile(math.gcd(c, x_off) if x_off else c, (256, 128))
    ob = x_off // cw
    rc = _tile(t, (ROW_CHUNK,))

    def body(x_ref, w_ref, b_ref, o_ref, pad_ref):
        _fill_pad(pad_ref, x_ref, t)
        for r0 in range(0, t, rc):
            pre = _conv_taps(pad_ref, w_ref, r0, rc, kw) + b_ref[...]
            o_ref[pl.ds(r0, rc), :] = pre * _sigmoid(pre)

    strip = pl.BlockSpec((t, cw), lambda i: (0, i))
    return pl.pallas_call(
        body,
        grid=(c // cw,),
        in_specs=[pl.BlockSpec((t, cw), lambda i: (0, i + ob)), pl.BlockSpec((kw, cw), lambda i: (0, i)),
                  pl.BlockSpec((1, cw), lambda i: (0, i))],
        out_specs=strip,
        out_shape=jax.ShapeDtypeStruct((t, c), F32),
        scratch_shapes=[pltpu.VMEM((t + 2 * PAD, cw), F32)],
        compiler_params=_cparams(("parallel",)),
        name=name,
    )(x, w, b.reshape(1, c))


def _conv_bwd_core(dpre_pad_ref, x_pad_ref, w_ref, dx_ref, dw_ref, db_ref, t, rc, kw):
    cw = dx_ref.shape[1]

    def fold(a):
        return jnp.sum(a.reshape(rc // SUBLANES, SUBLANES, cw), axis=0) if rc % SUBLANES == 0 else jnp.sum(a, axis=0, keepdims=True)

    dws = [None] * kw
    dbs = None
    for r0 in range(0, t, rc):
        dpre = dpre_pad_ref[pl.ds(PAD + r0, rc), :]
        dx = None
        for j in range(kw):
            s = kw - 1 - j
            term = dpre_pad_ref[pl.ds(PAD + r0 + s, rc), :] * w_ref[j:j + 1, :]
            dx = term if dx is None else dx + term
            part = fold(dpre * _shifted(x_pad_ref, r0, rc, s))
            dws[j] = part if dws[j] is None else dws[j] + part
        part = fold(dpre)
        dbs = part if dbs is None else dbs + part
        dx_ref[pl.ds(r0, rc), :] = dx
    for j in range(kw):
        dw_ref[j:j + 1, :] = jnp.sum(dws[j], axis=0, keepdims=True)
    db_ref[...] = jnp.sum(dbs, axis=0, keepdims=True)


def _conv_silu_bwd(x, w, b, dact, *, x_off=0, name):
    t = x.shape[0]
    kw, c = w.shape
    cw = _tile(math.gcd(c, x_off) if x_off else c, (256, 128))
    ob = x_off // cw
    rc = _tile(t, (ROW_CHUNK,))

    def body(x_ref, w_ref, b_ref, da_ref, dx_ref, dw_ref, db_ref, xpad_ref, dpad_ref):
        _fill_pad(xpad_ref, x_ref, t)
        dpad_ref[0:PAD, :] = jnp.zeros((PAD, cw), F32)
        dpad_ref[pl.ds(PAD + t, PAD), :] = jnp.zeros((PAD, cw), F32)
        for r0 in range(0, t, rc):
            pre = _conv_taps(xpad_ref, w_ref, r0, rc, kw) + b_ref[...]
            sg = _sigmoid(pre)
            dpad_ref[pl.ds(PAD + r0, rc), :] = da_ref[pl.ds(r0, rc), :] * (sg * (1.0 + pre * (1.0 - sg)))
        _conv_bwd_core(dpad_ref, xpad_ref, w_ref, dx_ref, dw_ref, db_ref, t, rc, kw)

    strip = pl.BlockSpec((t, cw), lambda i: (0, i))
    wspec = pl.BlockSpec((kw, cw), lambda i: (0, i))
    bspec = pl.BlockSpec((1, cw), lambda i: (0, i))
    return pl.pallas_call(
        body,
        grid=(c // cw,),
        in_specs=[pl.BlockSpec((t, cw), lambda i: (0, i + ob)), wspec, bspec, strip],
        out_specs=[strip, wspec, bspec],
        out_shape=[jax.ShapeDtypeStruct((t, c), F32), jax.ShapeDtypeStruct((kw, c), F32),
                   jax.ShapeDtypeStruct((1, c), F32)],
        scratch_shapes=[pltpu.VMEM((t + 2 * PAD, cw), F32), pltpu.VMEM((t + 2 * PAD, cw), F32)],
        compiler_params=_cparams(("parallel",)),
        name=name,
    )(x, w, b.reshape(1, c), dact)


def _conv_glu_fwd(hid, w, b, *, name):
    t, c2 = hid.shape
    f = c2 // 2
    kw = w.shape[0]
    cw = _tile(f, (256, 128))
    nf = f // cw
    rc = _tile(t, (ROW_CHUNK,))

    def body(g_ref, v_ref, wg_ref, wv_ref, bg_ref, bv_ref, o_ref, gpad_ref, vpad_ref):
        _fill_pad(gpad_ref, g_ref, t)
        _fill_pad(vpad_ref, v_ref, t)
        for r0 in range(0, t, rc):
            gate = _conv_taps(gpad_ref, wg_ref, r0, rc, kw) + bg_ref[...]
            val = _conv_taps(vpad_ref, wv_ref, r0, rc, kw) + bv_ref[...]
            o_ref[pl.ds(r0, rc), :] = (gate * _sigmoid(gate) * val).astype(o_ref.dtype)

    gs = pl.BlockSpec((t, cw), lambda i: (0, i))
    vs = pl.BlockSpec((t, cw), lambda i: (0, i + nf))
    b2 = b.reshape(1, c2)
    return pl.pallas_call(
        body,
        grid=(nf,),
        in_specs=[gs, vs, pl.BlockSpec((kw, cw), lambda i: (0, i)), pl.BlockSpec((kw, cw), lambda i: (0, i + nf)),
                  pl.BlockSpec((1, cw), lambda i: (0, i)), pl.BlockSpec((1, cw), lambda i: (0, i + nf))],
        out_specs=gs,
        out_shape=jax.ShapeDtypeStruct((t, f), BF16),
        scratch_shapes=[pltpu.VMEM((t + 2 * PAD, cw), F32), pltpu.VMEM((t + 2 * PAD, cw), F32)],
        compiler_params=_cparams(("parallel",)),
        name=name,
    )(hid, hid, w, w, b2, b2)


def _conv_glu_bwd(hid, w, b, dact, *, name):
    t, c2 = hid.shape
    f = c2 // 2
    kw = w.shape[0]
    cw = _tile(f, (128,))
    nf = f // cw
    rc = _tile(t, (ROW_CHUNK,))

    def body(g_ref, v_ref, wg_ref, wv_ref, bg_ref, bv_ref, da_ref,
             dg_ref, dv_ref, dwg_ref, dwv_ref, dbg_ref, dbv_ref,
             gpad_ref, vpad_ref, dgpad_ref, dvpad_ref):
        _fill_pad(gpad_ref, g_ref, t)
        _fill_pad(vpad_ref, v_ref, t)
        for ref in (dgpad_ref, dvpad_ref):
            ref[0:PAD, :] = jnp.zeros((PAD, cw), F32)
            ref[pl.ds(PAD + t, PAD), :] = jnp.zeros((PAD, cw), F32)
        for r0 in range(0, t, rc):
            gate = _conv_taps(gpad_ref, wg_ref, r0, rc, kw) + bg_ref[...]
            val = _conv_taps(vpad_ref, wv_ref, r0, rc, kw) + bv_ref[...]
            sg = _sigmoid(gate)
            da = da_ref[pl.ds(r0, rc), :].astype(F32)
            dgpad_ref[pl.ds(PAD + r0, rc), :] = da * val * (sg * (1.0 + gate * (1.0 - sg)))
            dvpad_ref[pl.ds(PAD + r0, rc), :] = da * (gate * sg)
        _conv_bwd_core(dgpad_ref, gpad_ref, wg_ref, dg_ref, dwg_ref, dbg_ref, t, rc, kw)
        _conv_bwd_core(dvpad_ref, vpad_ref, wv_ref, dv_ref, dwv_ref, dbv_ref, t, rc, kw)

    gs = pl.BlockSpec((t, cw), lambda i: (0, i))
    vs = pl.BlockSpec((t, cw), lambda i: (0, i + nf))
    wg = pl.BlockSpec((kw, cw), lambda i: (0, i))
    wv = pl.BlockSpec((kw, cw), lambda i: (0, i + nf))
    bg = pl.BlockSpec((1, cw), lambda i: (0, i))
    bv = pl.BlockSpec((1, cw), lambda i: (0, i + nf))
    b2 = b.reshape(1, c2)
    pad = pltpu.VMEM((t + 2 * PAD, cw), F32)
    return pl.pallas_call(
        body,
        grid=(nf,),
        in_specs=[gs, vs, wg, wv, bg, bv, gs],
        out_specs=[gs, gs, wg, wg, bg, bg],
        out_shape=[jax.ShapeDtypeStruct((t, f), F32), jax.ShapeDtypeStruct((t, f), F32),
                   jax.ShapeDtypeStruct((kw, f), F32), jax.ShapeDtypeStruct((kw, f), F32),
                   jax.ShapeDtypeStruct((1, f), F32), jax.ShapeDtypeStruct((1, f), F32)],
        scratch_shapes=[pad, pad, pad, pad],
        compiler_params=_cparams(("parallel",)),
        name=name,
    )(hid, hid, w, w, b2, b2, dact)


def _gate_norm_fwd(y, zx, w, *, name):
    t, di = y.shape
    gsz = di // SSM_GROUPS
    tb = _tile(t, (256, 128))

    def body(y_ref, z_ref, w_ref, o_ref):
        for g in range(SSM_GROUPS):
            sl = slice(g * gsz, (g + 1) * gsz)
            zv = z_ref[:, sl]
            gv = y_ref[:, sl] * (zv * _sigmoid(zv))
            r = lax.rsqrt(jnp.mean(gv * gv, axis=-1, keepdims=True) + EPS)
            o_ref[:, sl] = (gv * r * w_ref[:, sl]).astype(o_ref.dtype)

    row = pl.BlockSpec((tb, di), lambda i: (i, 0))
    return pl.pallas_call(
        body,
        grid=(t // tb,),
        in_specs=[row, row, pl.BlockSpec((1, di), lambda i: (0, 0))],
        out_specs=row,
        out_shape=jax.ShapeDtypeStruct((t, di), BF16),
        compiler_params=_cparams(("parallel",)),
        name=name,
    )(y, zx, w.reshape(1, di))


def _gate_norm_bwd(y, zx, w, dyn, *, name):
    t, di = y.shape
    gsz = di // SSM_GROUPS
    tb = _tile(t, (256, 128))

    def body(y_ref, z_ref, w_ref, d_ref, dy_ref, dz_ref, dw_ref):
        i = pl.program_id(0)
        for g in range(SSM_GROUPS):
            sl = slice(g * gsz, (g + 1) * gsz)
            zv = z_ref[:, sl]
            yv = y_ref[:, sl]
            sg = _sigmoid(zv)
            sz = zv * sg
            gv = yv * sz
            r = lax.rsqrt(jnp.mean(gv * gv, axis=-1, keepdims=True) + EPS)
            gn = gv * r
            dn = d_ref[:, sl].astype(F32)
            q = dn * w_ref[:, sl]
            dg = r * (q - gn * jnp.mean(q * gn, axis=-1, keepdims=True))
            dy_ref[:, sl] = dg * sz
            dz_ref[:, sl] = dg * yv * (sg * (1.0 + zv * (1.0 - sg)))
            dwp = jnp.sum(dn * gn, axis=0, keepdims=True)

            @pl.when(i == 0)
            def _(sl=sl, dwp=dwp):
                dw_ref[:, sl] = dwp

            @pl.when(i > 0)
            def _(sl=sl, dwp=dwp):
                dw_ref[:, sl] += dwp

    row = pl.BlockSpec((tb, di), lambda i: (i, 0))
    vec = pl.BlockSpec((1, di), lambda i: (0, 0))
    return pl.pallas_call(
        body,
        grid=(t // tb,),
        in_specs=[row, row, vec, row],
        out_specs=[row, row, vec],
        out_shape=[jax.ShapeDtypeStruct((t, di), F32), jax.ShapeDtypeStruct((t, di), F32),
                   jax.ShapeDtypeStruct((1, di), F32)],
        compiler_params=_cparams(("arbitrary",)),
        name=name,
    )(y, zx, w.reshape(1, di), dyn)


def _adamw(w, g, m, v, *, name):
    shape = w.shape
    cols = shape[-1]
    rows = w.size // cols
    w2, g2, m2, v2 = (a.reshape(rows, cols) for a in (w, g, m, v))
    tr = rows
    if rows * cols * 4 > ADAM_BLOCK_BYTES:
        tr = _tile(rows, tuple(r for r in (512, 256, 128, 64, 32, 16, 8) if r * cols * 4 <= ADAM_BLOCK_BYTES))
    c1 = 1.0 - ADAM_B1 ** ADAM_STEP
    c2 = 1.0 - ADAM_B2 ** ADAM_STEP

    def body(w_ref, g_ref, m_ref, v_ref, d_ref, nm_ref, nv_ref):
        gv = g_ref[...]
        nm = ADAM_B1 * m_ref[...] + (1.0 - ADAM_B1) * gv
        nv = ADAM_B2 * v_ref[...] + (1.0 - ADAM_B2) * (gv * gv)
        d_ref[...] = -ADAM_LR * ((nm / c1) / (jnp.sqrt(nv / c2) + ADAM_EPS) + ADAM_WD * w_ref[...])
        nm_ref[...] = nm
        nv_ref[...] = nv

    blk = pl.BlockSpec((tr, cols), lambda i: (i, 0))
    outs = pl.pallas_call(
        body,
        grid=(rows // tr,),
        in_specs=[blk] * 4,
        out_specs=[blk] * 3,
        out_shape=[jax.ShapeDtypeStruct((rows, cols), F32)] * 3,
        compiler_params=_cparams(("parallel",)),
        name=name,
    )(w2, g2, m2, v2)
    return tuple(o.reshape(shape) for o in outs)


def _ssd_scalars(dtc_ref, dtr_ref, hpc_ref, hpr_ref, ln):
    assert SSM_CHUNK == SSM_STATE == LANES, "the SSD kernels mix chunk, state and lane-wide tiles freely"
    bias_c, alog_c = hpc_ref[0, 0:1, :], hpc_ref[0, 1:2, :]
    bias_r, alog_r = hpr_ref[0, :, 0:1], hpr_ref[0, :, 1:2]
    a_c, a_r = -jnp.exp(alog_c), -jnp.exp(alog_r)
    raw_c = dtc_ref[0] + bias_c
    dt_c = _softplus(raw_c)
    dt_r = _softplus(dtr_ref[0] + bias_r)
    row = lax.broadcasted_iota(jnp.int32, (ln, ln), 0)
    col = lax.broadcasted_iota(jnp.int32, (ln, ln), 1)
    lower = (col <= row).astype(F32)
    upper = (row <= col).astype(F32)
    acs_c = _ones_dot(lower, dt_c * a_c, ones_left=True)
    acs_r = _ones_dot(upper, dt_r * a_r, ones_left=False)
    return raw_c, dt_c, a_c, acs_c, acs_r, row, col


def _ssd_specs(t, di, g_n, n_st, rp, ln, r_h, rev):
    nc = t // ln
    cidx = (lambda c: nc - 1 - c) if rev else (lambda c: c)
    xs = pl.BlockSpec((ln, rp), lambda g, c: (cidx(c), g))
    bm = pl.BlockSpec((ln, n_st), lambda g, c: (cidx(c), di // n_st + g))
    cm = pl.BlockSpec((ln, n_st), lambda g, c: (cidx(c), di // n_st + g_n + g))
    dtc = pl.BlockSpec((1, ln, r_h), lambda g, c: (g, cidx(c), 0))
    dtr = pl.BlockSpec((1, r_h, ln), lambda g, c: (g, 0, cidx(c)))
    hpc = pl.BlockSpec((1, 3, r_h), lambda g, c: (g, 0, 0))
    hpr = pl.BlockSpec((1, r_h, 3), lambda g, c: (g, 0, 0))
    prev = pl.BlockSpec((1, rp, n_st), lambda g, c: (cidx(c), g, 0))
    return xs, bm, cm, dtc, dtr, hpc, hpr, prev


def _ssd_fwd(xbc, dtc, dtr, hpc, hpr, *, name):
    t = xbc.shape[0]
    di, g_n, n_st, p_h, ln = D_INNER, SSM_GROUPS, SSM_STATE, SSM_HEAD_DIM, SSM_CHUNK
    r_h = SSM_HEADS // g_n
    rp = r_h * p_h
    nc = t // ln

    def body(xs_ref, b_ref, c_ref, dtc_ref, dtr_ref, hpc_ref, hpr_ref, y_ref, prev_ref, st_ref):
        @pl.when(pl.program_id(1) == 0)
        def _():
            st_ref[...] = jnp.zeros_like(st_ref)

        _, dt_c, _, acs_c, acs_r, row, col = _ssd_scalars(dtc_ref, dtr_ref, hpc_ref, hpr_ref, ln)
        bm = b_ref[...]
        cm = c_ref[...]
        cm16 = cm.astype(BF16)
        cb = _nt(cm16, bm.astype(BF16))
        causal = row >= col
        for r in range(r_h):
            sl = slice(r * p_h, (r + 1) * p_h)
            xs = xs_ref[:, sl]
            acs = jnp.broadcast_to(acs_c[:, r:r + 1], (ln, ln))
            last = acs[ln - 1:ln, :]
            lm = jnp.where(causal, jnp.exp(acs - acs_r[r:r + 1, :]), 0.0)
            xd = (xs * jnp.broadcast_to(dt_c[:, r:r + 1], (ln, p_h))).astype(BF16)
            prev = st_ref[sl, :]
            y = _nn((cb * lm).astype(BF16), xd)
            y = y + _nt(cm16, prev.astype(BF16)) * jnp.exp(acs[:, :p_h])
            y_ref[:, sl] = y + hpc_ref[0, 2:3, r:r + 1] * xs
            prev_ref[0, sl, :] = prev
            bd = (bm * jnp.exp(last - acs[:, :n_st])).astype(BF16)
            st_ref[sl, :] = prev * jnp.exp(last[:, :n_st]) + _tn(xd, bd)

    xs, bm, cm, dtcs, dtrs, hpcs, hprs, prev = _ssd_specs(t, di, g_n, n_st, rp, ln, r_h, False)
    return pl.pallas_call(
        body,
        grid=(g_n, nc),
        in_specs=[xs, bm, cm, dtcs, dtrs, hpcs, hprs],
        out_specs=[xs, prev],
        out_shape=[jax.ShapeDtypeStruct((t, di), F32), jax.ShapeDtypeStruct((nc, g_n * rp, n_st), F32)],
        scratch_shapes=[pltpu.VMEM((rp, n_st), F32)],
        compiler_params=_cparams(("parallel", "arbitrary")),
        name=name,
    )(xbc, xbc, xbc, dtc, dtr, hpc, hpr)


def _ssd_bwd(xbc, dtc, dtr, hpc, hpr, prev, dy, *, name):
    t = xbc.shape[0]
    di, g_n, n_st, p_h, ln = D_INNER, SSM_GROUPS, SSM_STATE, SSM_HEAD_DIM, SSM_CHUNK
    r_h = SSM_HEADS // g_n
    rp = r_h * p_h
    nc = t // ln

    def body(xs_ref, b_ref, c_ref, dtc_ref, dtr_ref, hpc_ref, hpr_ref, prev_ref, dy_ref,
             dxs_ref, db_ref, dc_ref, ddt_ref, hg_ref, ds_ref):
        step = pl.program_id(1)

        @pl.when(step == 0)
        def _():
            ds_ref[...] = jnp.zeros_like(ds_ref)

        raw_c, dt_c, a_c, acs_c, acs_r, row, col = _ssd_scalars(dtc_ref, dtr_ref, hpc_ref, hpr_ref, ln)
        bm = b_ref[...]
        cm = c_ref[...]
        bm16, cm16 = bm.astype(BF16), cm.astype(BF16)
        cb = _nt(cm16, bm16)
        cbt = _nt(bm16, cm16)
        lane_r = lax.broadcasted_iota(jnp.int32, (ln, r_h), 1)
        dacs_all = jnp.zeros((ln, r_h), F32)
        ddtx_all = jnp.zeros((ln, r_h), F32)
        dd_all = jnp.zeros((ln, r_h), F32)
        dcb = jnp.zeros((ln, ln), F32)
        dcbt = jnp.zeros((ln, ln), F32)
        dc_acc = jnp.zeros((ln, n_st), F32)
        db_acc = jnp.zeros((ln, n_st), F32)
        for r in range(r_h):
            sl = slice(r * p_h, (r + 1) * p_h)
            xs = xs_ref[:, sl]
            dyv = dy_ref[:, sl]
            dy16 = dyv.astype(BF16)
            acs = jnp.broadcast_to(acs_c[:, r:r + 1], (ln, ln))
            dtv = jnp.broadcast_to(dt_c[:, r:r + 1], (ln, p_h))
            acsr = acs_r[r:r + 1, :]
            last = acs[ln - 1:ln, :]
            xd = xs * dtv
            xd16 = xd.astype(BF16)
            lm = jnp.where(row >= col, jnp.exp(acs - acsr), 0.0)
            lmt = jnp.where(col >= row, jnp.exp(acsr - acs), 0.0)
            m_ls = cb * lm
            m_sl = cbt * lmt
            dm = _nt(dy16, xd16)
            dmt = _nt(xd16, dy16)
            dxd = _nn(m_sl.astype(BF16), dy16)
            dacs = _row_sums(dm * m_ls - dmt * m_sl)
            dcb = dcb + dm * lm
            dcbt = dcbt + dmt * lmt
            prev = prev_ref[0, sl, :]
            prev16 = prev.astype(BF16)
            e = jnp.exp(acs[:, :p_h])
            y_off = _nt(cm16, prev16) * e
            dacs = dacs + _row_sums(dyv * y_off)
            dyo16 = (dyv * e).astype(BF16)
            dc_acc = dc_acc + _nn(dyo16, prev16)
            dprev = _tn(dyo16, cm16)
            ds = ds_ref[sl, :]
            ds16 = ds.astype(BF16)
            decay = jnp.exp(last - acs)[:, :n_st]
            bd16 = (bm * decay).astype(BF16)
            dbd = _nn(xd16, ds16)
            dxd = dxd + _nt(bd16, ds16)
            db_acc = db_acc + dbd * decay
            tdec = _row_sums(dbd * bm) * decay
            cd = jnp.exp(last)
            dlast = (jnp.sum(tdec, axis=0, keepdims=True)
                     + jnp.sum(_row_sums(prev * ds), axis=0, keepdims=True) * cd)
            ds_ref[sl, :] = dprev + cd[:, :n_st] * ds
            dskip = hpc_ref[0, 2:3, r:r + 1]
            dxs_ref[:, sl] = dxd * dtv + dskip * dyv
            dacs = dacs - tdec + jnp.where(row == ln - 1, dlast, 0.0)
            dacs_all = jnp.where(lane_r == r, dacs[:, :r_h], dacs_all)
            ddtx_all = jnp.where(lane_r == r, _row_sums(dxd * xs)[:, :r_h], ddtx_all)
            dd_all = jnp.where(lane_r == r, _row_sums(dyv * xs)[:, :r_h], dd_all)
        dc_ref[...] = dc_acc + _nn(dcb.astype(BF16), bm16)
        db_ref[...] = db_acc + _nn(dcbt.astype(BF16), cm16)
        upper = (row <= col).astype(F32)
        dad = _ones_dot(upper, dacs_all, ones_left=True)
        ddt = dad * a_c + ddtx_all
        ddt_raw = ddt * _sigmoid(raw_c)
        ddt_ref[0] = ddt_raw
        d_bias = jnp.sum(ddt_raw, axis=0, keepdims=True)
        d_alog = jnp.sum(dad * dt_c, axis=0, keepdims=True) * a_c
        d_d = jnp.sum(dd_all, axis=0, keepdims=True)
        hg = jnp.concatenate([d_bias, d_alog, d_d], axis=0)

        @pl.when(step == 0)
        def _():
            hg_ref[0] = hg

        @pl.when(step > 0)
        def _():
            hg_ref[0] += hg

    xs, bms, cms, dtcs, dtrs, hpcs, hprs, prevs = _ssd_specs(t, di, g_n, n_st, rp, ln, r_h, True)
    bout = pl.BlockSpec((ln, n_st), lambda g, c: (nc - 1 - c, g))
    return pl.pallas_call(
        body,
        grid=(g_n, nc),
        in_specs=[xs, bms, cms, dtcs, dtrs, hpcs, hprs, prevs, xs],
        out_specs=[xs, bout, bout, dtcs, hpcs],
        out_shape=[jax.ShapeDtypeStruct((t, di), F32), jax.ShapeDtypeStruct((t, g_n * n_st), F32),
                   jax.ShapeDtypeStruct((t, g_n * n_st), F32), jax.ShapeDtypeStruct((g_n, t, r_h), F32),
                   jax.ShapeDtypeStruct((g_n, 3, r_h), F32)],
        scratch_shapes=[pltpu.VMEM((rp, n_st), F32)],
        compiler_params=_cparams(("parallel", "arbitrary")),
        name=name,
    )(xbc, xbc, xbc, dtc, dtr, hpc, hpr, prev, dy)


SB_KEYS = 256
SB_QUERIES = (1024, 512, 256)
SB_PIECES = 2


def _sb_logits(qs, kv, valid):
    z = _nt(qs, kv)
    sp = _softplus(z)
    lg = -sp if valid is None else jnp.where(valid, -sp, 0.0)
    return z - sp, lg


def _sb_iota(tq):
    diff = lax.broadcasted_iota(jnp.int32, (tq, SB_KEYS), 1) - lax.broadcasted_iota(jnp.int32, (tq, SB_KEYS), 0)
    krow = lax.broadcasted_iota(jnp.int32, (SB_KEYS, SB_KEYS), 0)
    kcol = lax.broadcasted_iota(jnp.int32, (SB_KEYS, SB_KEYS), 1)
    return diff, krow, kcol


def _sb_scale(d):
    scale = 1.0 / math.sqrt(d)
    assert math.frexp(scale)[0] == 0.5, "the scale is folded into bf16 queries: it must be a power of two"
    return scale


def _key_rows(j):
    return pl.ds(pl.multiple_of(j * SB_KEYS, SB_KEYS), SB_KEYS)


def _pairs(tiles, per_tile, one, carry):
    if per_tile % 2:
        return lax.fori_loop(0, tiles * per_tile, one, carry)
    return lax.fori_loop(0, tiles * (per_tile // 2), lambda s, cr: one(2 * s + 1, one(2 * s, cr)), carry)


def _sb_fwd(q, k, v, *, name):
    h, t, d = q.shape
    tq = _tile(t, SB_QUERIES)
    nq = t // tq
    kpq = tq // SB_KEYS
    scale = _sb_scale(d)

    def body(q_ref, k_ref, v_ref, o_ref, lt_ref):
        i = pl.program_id(1)
        qs = (q_ref[0].astype(F32) * scale).astype(BF16)
        diff, krow, kcol = _sb_iota(tq)
        later = (krow > kcol).astype(F32)

        def block(j, carry, valid):
            acc, cl = carry
            rows = _key_rows(j)
            ls, lg = _sb_logits(qs, k_ref[0, rows, :], valid)
            cs = _ones_dot(later, lg, ones_left=False, pieces=SB_PIECES)
            att = jnp.exp(ls + (cs + cl))
            if valid is not None:
                att = jnp.where(valid, att, 0.0)
            acc = acc + _nn(att.astype(BF16), v_ref[0, rows, :])
            return acc, cl + (cs[:, 0:1] + lg[:, 0:1])

        carry = (jnp.zeros((tq, d), F32), jnp.zeros((tq, 1), F32))
        for m in range(kpq - 1, -1, -1):
            carry = block(i * kpq + m, carry, diff < -m * SB_KEYS)
        nb = i * kpq
        acc, cl = _pairs(i, kpq, lambda s, cr: block(nb - 1 - s, cr, None), carry)
        o_ref[0] = acc
        lt_ref[0] = cl

    qs = pl.BlockSpec((1, tq, d), lambda hh, i: (hh, i, 0))
    ls = pl.BlockSpec((1, tq, 1), lambda hh, i: (hh, i, 0))
    ks = pl.BlockSpec((1, t, d), lambda hh, i: (hh, 0, 0))
    return pl.pallas_call(
        body,
        grid=(h, nq),
        in_specs=[qs, ks, ks],
        out_specs=[qs, ls],
        out_shape=[jax.ShapeDtypeStruct((h, t, d), F32), jax.ShapeDtypeStruct((h, t, 1), F32)],
        compiler_params=_cparams(("parallel", "arbitrary")),
        name=name,
    )(q, k, v)


def _sb_bwd(q, k, v, lt, do, *, name):
    h, t, d = q.shape
    tq = _tile(t, SB_QUERIES)
    nq = t // tq
    kpq = tq // SB_KEYS
    scale = _sb_scale(d)
    last = SB_KEYS - 1

    def body(q_ref, k_ref, v_ref, lt_ref, do_ref, dq_ref, dk_ref, dv_ref):
        i = pl.program_id(1)

        @pl.when(i == 0)
        def _():
            dk_ref[...] = jnp.zeros_like(dk_ref)
            dv_ref[...] = jnp.zeros_like(dv_ref)

        qs = (q_ref[0].astype(F32) * scale).astype(BF16)
        do16 = do_ref[0].astype(BF16)
        ltot = lt_ref[0]
        diff, krow, kcol = _sb_iota(tq)
        upto = (krow <= kcol).astype(F32)
        before = (krow < kcol).astype(F32)

        def block(j, carry, valid):
            dq, pl_sum, pg_sum = carry
            rows = _key_rows(j)
            kv = k_ref[0, rows, :]
            vv = v_ref[0, rows, :]
            ls, lg = _sb_logits(qs, kv, valid)
            pre = _ones_dot(upto, lg, ones_left=False, pieces=SB_PIECES)
            att = jnp.exp(ls + (ltot - (pre + pl_sum)))
            if valid is not None:
                att = jnp.where(valid, att, 0.0)
            g = att * _nt(do16, vv)
            gpre = _ones_dot(before, g, ones_left=False, pieces=SB_PIECES)
            sig = jnp.exp(ls)
            dz16 = (g - sig * (g + (gpre + pg_sum))).astype(BF16)
            if valid is not None:
                dz16 = jnp.where(valid, dz16, jnp.zeros_like(dz16))
            dq = dq + _nn(dz16, kv)
            dk_ref[0, rows, :] += _tn(dz16, qs)
            dv_ref[0, rows, :] += _tn(att.astype(BF16), do16)
            return dq, pl_sum + pre[:, last:], pg_sum + (gpre[:, last:] + g[:, last:])

        zero = jnp.zeros((tq, 1), F32)
        nb = i * kpq
        carry = _pairs(i, kpq, lambda j, cr: block(j, cr, None), (jnp.zeros((tq, d), F32), zero, zero))
        for m in range(kpq):
            carry = block(nb + m, carry, diff < -m * SB_KEYS)
        dq_ref[0] = carry[0] * scale

    qs = pl.BlockSpec((1, tq, d), lambda hh, i: (hh, i, 0))
    ls = pl.BlockSpec((1, tq, 1), lambda hh, i: (hh, i, 0))
    ks = pl.BlockSpec((1, t, d), lambda hh, i: (hh, 0, 0))
    full = jax.ShapeDtypeStruct((h, t, d), F32)
    return pl.pallas_call(
        body,
        grid=(h, nq),
        in_specs=[qs, ks, ks, ls, qs],
        out_specs=[qs, ks, ks],
        out_shape=[full, full, full],
        compiler_params=_cparams(("parallel", "arbitrary")),
        name=name,
    )(q, k, v, lt, do)


def _row_tile(rows, cols):
    return _tile(rows, tuple(r for r in (2048, 1024, 512, 256, 128, 64, 32, 16, 8) if r * cols * 4 <= ADAM_BLOCK_BYTES))


def _sum_leading(x, *, name):
    n, rows, cols = x.shape
    tr = _row_tile(rows, cols)

    def body(x_ref, o_ref):
        acc = x_ref[0].astype(F32)
        for q in range(1, n):
            acc = acc + x_ref[q].astype(F32)
        o_ref[...] = acc

    return pl.pallas_call(
        body,
        grid=(rows // tr,),
        in_specs=[pl.BlockSpec((n, tr, cols), lambda i: (0, i, 0))],
        out_specs=pl.BlockSpec((tr, cols), lambda i: (i, 0)),
        out_shape=jax.ShapeDtypeStruct((rows, cols), F32),
        compiler_params=_cparams(("parallel",)),
        name=name,
    )(x)


def _pair_add(g4h, recv, c, *, out_dtype, name):
    n, _, rows, cols = g4h.shape
    tr = _row_tile(rows, cols)

    def body(c_ref, g_ref, r_ref, o_ref):
        o_ref[...] = (g_ref[...] + r_ref[...]).astype(o_ref.dtype)

    blk = pl.BlockSpec((1, tr, cols), lambda q, i, c_ref: (q, i, 0))
    return pl.pallas_call(
        body,
        grid_spec=pltpu.PrefetchScalarGridSpec(
            num_scalar_prefetch=1,
            grid=(n, rows // tr),
            in_specs=[pl.BlockSpec((1, None, tr, cols), lambda q, i, c_ref: (q, c_ref[0], i, 0)), blk],
            out_specs=blk),
        out_shape=jax.ShapeDtypeStruct((n, rows, cols), out_dtype),
        compiler_params=_cparams(("parallel", "parallel")),
        name=name,
    )(c.reshape(1).astype(jnp.int32), g4h, recv)


ANY = pl.BlockSpec(memory_space=pl.ANY)


def _other_chips(x, y):
    return [(1 - x, y), (x, 1 - y), (1 - x, 1 - y)]


def _gather_chips(shard, *, name):
    def body(x_ref, o_ref, send_sems, recv_sems, local_sem):
        x, y, c = lax.axis_index("x"), lax.axis_index("y"), lax.axis_index("c")
        me = 2 * x + y
        mine = pltpu.make_async_copy(x_ref, o_ref.at[me], local_sem)
        mine.start()
        chips = _other_chips(x, y)
        sends = [pltpu.make_async_remote_copy(src_ref=x_ref, dst_ref=o_ref.at[me], send_sem=send_sems.at[q],
                                              recv_sem=recv_sems.at[q], device_id=(px, py, c), device_id_type=MESH)
                 for q, (px, py) in enumerate(chips)]
        for cp in sends:
            cp.start()
        for q, (px, py) in enumerate(chips):
            pltpu.make_async_remote_copy(src_ref=x_ref, dst_ref=o_ref.at[2 * px + py], send_sem=send_sems.at[q],
                                         recv_sem=recv_sems.at[q], device_id=(px, py, c), device_id_type=MESH).wait_recv()
        for cp in sends:
            cp.wait_send()
        mine.wait()

    return pl.pallas_call(
        body,
        in_specs=[ANY],
        out_specs=ANY,
        out_shape=jax.ShapeDtypeStruct((4,) + shard.shape, shard.dtype),
        scratch_shapes=[pltpu.SemaphoreType.DMA((3,)), pltpu.SemaphoreType.DMA((3,)), pltpu.SemaphoreType.DMA],
        compiler_params=pltpu.CompilerParams(has_side_effects=True),
        name=name,
    )(shard)


def _scatter_chips(parts, *, name):
    def body(p_ref, o_ref, send_sems, recv_sems):
        x, y, c = lax.axis_index("x"), lax.axis_index("y"), lax.axis_index("c")
        me = 2 * x + y
        chips = _other_chips(x, y)
        sends = [pltpu.make_async_remote_copy(src_ref=p_ref.at[2 * px + py], dst_ref=o_ref.at[me], send_sem=send_sems.at[q],
                                              recv_sem=recv_sems.at[q], device_id=(px, py, c), device_id_type=MESH)
                 for q, (px, py) in enumerate(chips)]
        for cp in sends:
            cp.start()
        for q, (px, py) in enumerate(chips):
            pltpu.make_async_remote_copy(src_ref=p_ref.at[me], dst_ref=o_ref.at[2 * px + py], send_sem=send_sems.at[q],
                                         recv_sem=recv_sems.at[q], device_id=(px, py, c), device_id_type=MESH).wait_recv()
        for cp in sends:
            cp.wait_send()

    return pl.pallas_call(
        body,
        in_specs=[ANY],
        out_specs=ANY,
        out_shape=jax.ShapeDtypeStruct(parts.shape, parts.dtype),
        scratch_shapes=[pltpu.SemaphoreType.DMA((3,)), pltpu.SemaphoreType.DMA((3,))],
        compiler_params=pltpu.CompilerParams(has_side_effects=True),
        name=name,
    )(parts)


def _gather_chips_halves(shard, *, name):
    def body(x_ref, o_ref, send_sems, recv_sems):
        x, y, c = lax.axis_index("x"), lax.axis_index("y"), lax.axis_index("c")
        me, sibling = 2 * x + y, (x, y, 1 - c)
        chips = _other_chips(x, y)

        def copy(q, src, dst, to):
            return pltpu.make_async_remote_copy(src_ref=src, dst_ref=dst, send_sem=send_sems.at[q],
                                                recv_sem=recv_sems.at[q], device_id=to, device_id_type=MESH)

        sends = [copy(q, x_ref.at[c], o_ref.at[me, c], (px, py, c)) for q, (px, py) in enumerate(chips)]
        for cp in sends:
            cp.start()
        passed = []
        for q, (px, py) in enumerate(chips):
            slot = o_ref.at[2 * px + py, c]
            copy(q, x_ref.at[c], slot, (px, py, c)).wait_recv()
            passed.append(copy(3 + q, slot, slot, sibling))
            passed[-1].start()
        for q, (px, py) in enumerate(chips):
            copy(3 + q, x_ref.at[1 - c], o_ref.at[2 * px + py, 1 - c], sibling).wait_recv()
        for cp in sends + passed:
            cp.wait_send()

    return pl.pallas_call(
        body,
        in_specs=[ANY],
        out_specs=ANY,
        out_shape=jax.ShapeDtypeStruct((N_CHIPS,) + shard.shape, shard.dtype),
        scratch_shapes=[pltpu.SemaphoreType.DMA((6,)), pltpu.SemaphoreType.DMA((6,))],
        compiler_params=pltpu.CompilerParams(has_side_effects=True),
        name=name,
    )(shard)


def _swap_other_half(g4h, *, name):
    n = g4h.shape[0]

    def body(g_ref, o_ref, send_sem, recv_sem):
        x, y, c = lax.axis_index("x"), lax.axis_index("y"), lax.axis_index("c")
        cp = pltpu.make_async_remote_copy(src_ref=g_ref.at[pl.ds(0, n), 1 - c], dst_ref=o_ref, send_sem=send_sem,
                                          recv_sem=recv_sem, device_id=(x, y, 1 - c), device_id_type=MESH)
        cp.start()
        cp.wait()

    return pl.pallas_call(
        body,
        in_specs=[ANY],
        out_specs=ANY,
        out_shape=jax.ShapeDtypeStruct((n,) + g4h.shape[2:], g4h.dtype),
        scratch_shapes=[pltpu.SemaphoreType.DMA, pltpu.SemaphoreType.DMA],
        compiler_params=pltpu.CompilerParams(has_side_effects=True),
        name=name,
    )(g4h)


def _join_halves(half, *, name):
    def body(h_ref, o_ref, send_sem, recv_sem):
        x, y, c = lax.axis_index("x"), lax.axis_index("y"), lax.axis_index("c")
        cp = pltpu.make_async_remote_copy(src_ref=h_ref, dst_ref=o_ref.at[c], send_sem=send_sem, recv_sem=recv_sem,
                                          device_id=(x, y, 1 - c), device_id_type=MESH)
        cp.start()
        pltpu.make_async_remote_copy(src_ref=h_ref, dst_ref=o_ref.at[1 - c], send_sem=send_sem, recv_sem=recv_sem,
                                     device_id=(x, y, 1 - c), device_id_type=MESH).wait_recv()
        cp.wait_send()

    return pl.pallas_call(
        body,
        in_specs=[ANY],
        out_specs=ANY,
        out_shape=jax.ShapeDtypeStruct((2,) + half.shape, half.dtype),
        scratch_shapes=[pltpu.SemaphoreType.DMA, pltpu.SemaphoreType.DMA],
        compiler_params=pltpu.CompilerParams(has_side_effects=True),
        name=name,
    )(half)


def _gather_all(v, *, name):
    def body(v_ref, o_ref, send_sems, recv_sems, local_sem):
        x, y, c = lax.axis_index("x"), lax.axis_index("y"), lax.axis_index("c")
        me = 4 * x + 2 * y + c
        mine = pltpu.make_async_copy(v_ref, o_ref.at[me], local_sem)
        mine.start()
        peers = [(x ^ (q >> 2 & 1), y ^ (q >> 1 & 1), c ^ (q & 1)) for q in range(1, 8)]
        sends = [pltpu.make_async_remote_copy(src_ref=v_ref, dst_ref=o_ref.at[me], send_sem=send_sems.at[q],
                                              recv_sem=recv_sems.at[q], device_id=peer, device_id_type=MESH)
                 for q, peer in enumerate(peers)]
        for cp in sends:
            cp.start()
        for q, (px, py, pc) in enumerate(peers):
            pltpu.make_async_remote_copy(src_ref=v_ref, dst_ref=o_ref.at[4 * px + 2 * py + pc], send_sem=send_sems.at[q],
                                         recv_sem=recv_sems.at[q], device_id=(px, py, pc), device_id_type=MESH).wait_recv()
        for cp in sends:
            cp.wait_send()
        mine.wait()

    return pl.pallas_call(
        body,
        in_specs=[ANY],
        out_specs=ANY,
        out_shape=jax.ShapeDtypeStruct((8,) + v.shape, v.dtype),
        scratch_shapes=[pltpu.SemaphoreType.DMA((7,)), pltpu.SemaphoreType.DMA((7,)), pltpu.SemaphoreType.DMA],
        compiler_params=pltpu.CompilerParams(has_side_effects=True),
        name=name,
    )(v)


WEIGHTS = ['ssm_norm_w', 'ssm_in_w', 'ssm_conv_w', 'ssm_conv_b', 'ssm_dt_bias', 'ssm_a_log', 'ssm_d',
           'ssm_gate_norm_w', 'ssm_out_w', 'kv_norm_w', 'w_k', 'w_v', 'attn_norm_w', 'w_q', 'w_o',
           'ffn_norm_w', 'ffn_up_w', 'ffn_conv_w', 'ffn_conv_b', 'ffn_down_w', 'final_norm_w']
SHARD_AXIS = {'ssm_norm_w': 1, 'ssm_in_w': 2, 'ssm_conv_w': 2, 'ssm_conv_b': 1, 'ssm_gate_norm_w': 1,
              'ssm_out_w': 1, 'w_k': 0, 'w_v': 0, 'w_q': 1, 'w_o': 1, 'ffn_up_w': 2, 'ffn_conv_w': 2,
              'ffn_down_w': 1}
BIG = ['ssm_in_w', 'ssm_out_w', 'w_k', 'w_v', 'w_q', 'w_o', 'ffn_up_w', 'ffn_down_w']
SMALL = [n for n in WEIGHTS if n in SHARD_AXIS and n not in BIG]
REPLICATED = [n for n in WEIGHTS if n not in SHARD_AXIS]
N_CHIPS = 4
RS_ROW_MULT = 4096


PACK_ROWS = 16


def _piece_rows(n):
    return -(-n // (PACK_ROWS * LANES)) * PACK_ROWS


def _pack(arrs, dtype, row_mult):
    lead = arrs[0].shape[:-1]
    pieces, total = [], 0
    for a in arrs:
        n = a.shape[-1]
        rows = _piece_rows(n)
        a = a.astype(dtype)
        if rows * LANES != n:
            a = jnp.pad(a, [(0, 0)] * len(lead) + [(0, rows * LANES - n)])
        pieces.append(a.reshape(lead + (rows, LANES)))
        total += rows
    extra = -total % row_mult
    if extra:
        pieces.append(jnp.zeros(lead + (extra, LANES), dtype))
    return jnp.concatenate(pieces, axis=len(lead))


def _unpack(buf, shapes):
    lead = buf.shape[:-2]
    out, off = [], 0
    for shp in shapes:
        n = math.prod(shp)
        rows = _piece_rows(n)
        piece = lax.slice_in_dim(buf, off, off + rows, axis=len(lead)).reshape(lead + (rows * LANES,))
        out.append(piece[..., :n].reshape(lead + tuple(shp)))
        off += rows
    return out


def _set_slot(buf, piece, index):
    return lax.dynamic_update_slice_in_dim(buf, piece[None], index, axis=0)


def _to_shards(full, axis):
    return jnp.stack(jnp.split(full, N_CHIPS, axis=axis), axis=0)


def _from_shards(stacked, axis):
    return jnp.concatenate([stacked[j] for j in range(N_CHIPS)], axis=axis)


def _heads(a, h):
    t = a.shape[0]
    return a.reshape(t, h, a.shape[1] // h).transpose(1, 0, 2)


def _unheads(a):
    h, t, d = a.shape
    return a.transpose(1, 0, 2).reshape(t, h * d)


def _ffn_fwd(h, norm_w, w_up, conv_w, conv_b, w_down, tag):
    u = _rmsnorm_fwd(h, norm_w, name=f"ffn{tag}_norm")
    hid = _matmul(u, w_up, name=f"ffn{tag}_up")
    act = _conv_glu_fwd(hid, conv_w, conv_b, name=f"ffn{tag}_glu")
    out = _matmul(act, w_down, add=h, name=f"ffn{tag}_down")
    return out, (u, hid, act)


def _ffn_bwd(h, saved, dout, norm_w, w_up, conv_w, conv_b, w_down, tag):
    u, hid, act = saved
    f = w_down.shape[0]
    dact = _matmul(dout, w_down, tb=True, name=f"ffn{tag}_down_dx")
    dw_down = _matmul(act, dout, ta=True, name=f"ffn{tag}_down_dw")
    dg, dv, dwg, dwv, dbg, dbv = _conv_glu_bwd(hid, conv_w, conv_b, dact, name=f"ffn{tag}_glu_bwd")
    du = _matmul(dg, w_up[:, :f], tb=True, name=f"ffn{tag}_up_dx_g")
    du = _matmul(dv, w_up[:, f:], tb=True, add=du, name=f"ffn{tag}_up_dx_v")
    dw_up = jnp.concatenate([_matmul(u, dg, ta=True, name=f"ffn{tag}_up_dw_g"),
                             _matmul(u, dv, ta=True, name=f"ffn{tag}_up_dw_v")], axis=1)
    dh, (dnorm,) = _rmsnorm_bwd(h, [(du, norm_w)], dout, name=f"ffn{tag}_norm_bwd")
    return dh, dict(norm=dnorm[0], up=dw_up, conv_w=jnp.concatenate([dwg, dwv], axis=1),
                    conv_b=jnp.concatenate([dbg, dbv], axis=1)[0], down=dw_down)


def _step(x, target, w):
    t = x.shape[0]
    g_n, heads = SSM_GROUPS, SSM_HEADS
    r_h = heads // g_n
    di = D_INNER
    zx_cols = di + CONV_DIM
    w_in = w['ssm_in_w'][0]
    w_zx = w_in[:, :zx_cols]
    w_dt = jnp.pad(w_in[:, zx_cols:], ((0, 0), (0, LANES - heads)))
    conv_w, conv_b = w['ssm_conv_w'][0], w['ssm_conv_b'][0]
    hp = jnp.stack([w['ssm_dt_bias'][0], w['ssm_a_log'][0], w['ssm_d'][0]], axis=0).reshape(3, g_n, r_h)
    hpc, hpr = hp.transpose(1, 0, 2), hp.transpose(1, 2, 0)
    w_out = w['ssm_out_w'][0]
    w_q, w_o = w['w_q'][0], w['w_o'][0]

    h0 = x
    u0 = _rmsnorm_fwd(h0, w['ssm_norm_w'][0], name="ssm_norm")
    zx = _matmul(u0, w_zx, name="ssm_in_zx")
    dt_raw = _matmul(u0, w_dt, name="ssm_in_dt")[:, :heads]
    dtg = dt_raw.reshape(t, g_n, r_h)
    dtc, dtr = dtg.transpose(1, 0, 2), dtg.transpose(1, 2, 0)
    xbc = _conv_silu_fwd(zx, conv_w, conv_b, x_off=di, name="ssm_conv")
    y, prev = _ssd_fwd(xbc, dtc, dtr, hpc, hpr, name="ssd_fwd")
    yn = _gate_norm_fwd(y, zx, w['ssm_gate_norm_w'][0], name="ssm_gate_norm")
    h1 = _matmul(yn, w_out, add=h0, name="ssm_out")
    h2, ffn0 = _ffn_fwd(h1, w['ffn_norm_w'][0], w['ffn_up_w'][0], w['ffn_conv_w'][0], w['ffn_conv_b'][0],
                        w['ffn_down_w'][0], 0)
    hk = _rmsnorm_fwd(h2, w['kv_norm_w'], name="kv_norm")
    qn = _rmsnorm_fwd(h2, w['attn_norm_w'][0], name="attn_norm")
    k2 = _matmul(hk, w['w_k'], out_dtype=BF16, name="attn_k")
    v2 = _matmul(hk, w['w_v'], out_dtype=BF16, name="attn_v")
    q2 = _matmul(qn, w_q, out_dtype=BF16, name="attn_q")
    qh, kh, vh = _heads(q2, SB_HEADS), _heads(k2, SB_HEADS), _heads(v2, SB_HEADS)
    oh, lt = _sb_fwd(qh, kh, vh, name="sb_fwd")
    o2 = _unheads(oh)
    h3 = _matmul(o2, w_o, add=h2, name="attn_o")
    h4, ffn1 = _ffn_fwd(h3, w['ffn_norm_w'][1], w['ffn_up_w'][1], w['ffn_conv_w'][1], w['ffn_conv_b'][1],
                        w['ffn_down_w'][1], 1)
    loss_p, dh4, d_final = _loss_head(h4, w['final_norm_w'], target, name="loss_head")

    dh3, g1 = _ffn_bwd(h3, ffn1, dh4, w['ffn_norm_w'][1], w['ffn_up_w'][1], w['ffn_conv_w'][1],
                       w['ffn_conv_b'][1], w['ffn_down_w'][1], 1)
    do2 = _matmul(dh3, w_o, tb=True, name="attn_o_dx")
    dw_o = _matmul(o2, dh3, ta=True, name="attn_o_dw")
    dqh, dkh, dvh = _sb_bwd(qh, kh, vh, lt, _heads(do2, SB_HEADS), name="sb_bwd")
    dq2, dk2, dv2 = _unheads(dqh), _unheads(dkh), _unheads(dvh)
    dqn = _matmul(dq2, w_q, tb=True, name="attn_q_dx")
    dw_q = _matmul(qn, dq2, ta=True, name="attn_q_dw")
    dhk = _matmul(dk2, w['w_k'], tb=True, name="attn_k_dx")
    dhk = _matmul(dv2, w['w_v'], tb=True, add=dhk, name="attn_v_dx")
    dw_k = _matmul(hk, dk2, ta=True, name="attn_k_dw")
    dw_v = _matmul(hk, dv2, ta=True, name="attn_v_dw")
    dh2, (d_attn_norm, d_kv_norm) = _rmsnorm_bwd(h2, [(dqn, w['attn_norm_w'][0]), (dhk, w['kv_norm_w'])], dh3,
                                                 name="attn_norms_bwd")
    dh1, g0 = _ffn_bwd(h1, ffn0, dh2, w['ffn_norm_w'][0], w['ffn_up_w'][0], w['ffn_conv_w'][0],
                       w['ffn_conv_b'][0], w['ffn_down_w'][0], 0)
    dyn = _matmul(dh1, w_out, tb=True, name="ssm_out_dx")
    dw_out = _matmul(yn, dh1, ta=True, name="ssm_out_dw")
    dy, dz, d_gate = _gate_norm_bwd(y, zx, w['ssm_gate_norm_w'][0], dyn, name="ssm_gate_norm_bwd")
    dxs, dbm, dcm, ddt_g, hg = _ssd_bwd(xbc, dtc, dtr, hpc, hpr, prev, dy, name="ssd_bwd")
    dxbc = jnp.concatenate([dxs, dbm, dcm], axis=1)
    dxbc_pre, d_conv_w, d_conv_b = _conv_silu_bwd(zx, conv_w, conv_b, dxbc, x_off=di, name="ssm_conv_bwd")
    dzx = jnp.concatenate([dz, dxbc_pre], axis=1)
    ddt = jnp.pad(ddt_g.transpose(1, 0, 2).reshape(t, heads), ((0, 0), (0, LANES - heads)))
    du0 = _matmul(dzx, w_zx, tb=True, name="ssm_in_zx_dx")
    du0 = _matmul(ddt, w_dt, tb=True, add=du0, name="ssm_in_dt_dx")
    dw_in = jnp.concatenate([_matmul(u0, dzx, ta=True, name="ssm_in_zx_dw"),
                             _matmul(u0, ddt, ta=True, name="ssm_in_dt_dw")[:, :heads]], axis=1)
    dx, (d_ssm_norm,) = _rmsnorm_bwd(h0, [(du0, w['ssm_norm_w'][0])], dh1, name="ssm_norm_bwd")

    hgr = hg.transpose(1, 0, 2).reshape(3, heads)
    grads = {
        'ssm_norm_w': d_ssm_norm, 'ssm_in_w': dw_in[None], 'ssm_conv_w': d_conv_w[None], 'ssm_conv_b': d_conv_b,
        'ssm_dt_bias': hgr[0:1], 'ssm_a_log': hgr[1:2], 'ssm_d': hgr[2:3], 'ssm_gate_norm_w': d_gate,
        'ssm_out_w': dw_out[None], 'kv_norm_w': d_kv_norm[0], 'w_k': dw_k, 'w_v': dw_v, 'attn_norm_w': d_attn_norm,
        'w_q': dw_q[None], 'w_o': dw_o[None], 'ffn_norm_w': jnp.stack([g0['norm'], g1['norm']]),
        'ffn_up_w': jnp.stack([g0['up'], g1['up']]), 'ffn_conv_w': jnp.stack([g0['conv_w'], g1['conv_w']]),
        'ffn_conv_b': jnp.stack([g0['conv_b'], g1['conv_b']]), 'ffn_down_w': jnp.stack([g0['down'], g1['down']]),
        'final_norm_w': d_final[0],
    }
    return loss_p, dx, grads


def kernel(x, ssm_norm_w, ssm_in_w, ssm_conv_w, ssm_conv_b, ssm_dt_bias, ssm_a_log, ssm_d, ssm_gate_norm_w, ssm_out_w, kv_norm_w, w_k, w_v, attn_norm_w, w_q, w_o, ffn_norm_w, ffn_up_w, ffn_conv_w, ffn_conv_b, ffn_down_w, final_norm_w, loss_target, m_ssm_norm_w, m_ssm_in_w, m_ssm_conv_w, m_ssm_conv_b, m_ssm_dt_bias, m_ssm_a_log, m_ssm_d, m_ssm_gate_norm_w, m_ssm_out_w, m_kv_norm_w, m_w_k, m_w_v, m_attn_norm_w, m_w_q, m_w_o, m_ffn_norm_w, m_ffn_up_w, m_ffn_conv_w, m_ffn_conv_b, m_ffn_down_w, m_final_norm_w, v_ssm_norm_w, v_ssm_in_w, v_ssm_conv_w, v_ssm_conv_b, v_ssm_dt_bias, v_ssm_a_log, v_ssm_d, v_ssm_gate_norm_w, v_ssm_out_w, v_kv_norm_w, v_w_k, v_w_v, v_attn_norm_w, v_w_q, v_w_o, v_ffn_norm_w, v_ffn_up_w, v_ffn_conv_w, v_ffn_conv_b, v_ffn_down_w, v_final_norm_w):
    args = (ssm_norm_w, ssm_in_w, ssm_conv_w, ssm_conv_b, ssm_dt_bias, ssm_a_log, ssm_d, ssm_gate_norm_w, ssm_out_w, kv_norm_w, w_k, w_v, attn_norm_w, w_q, w_o, ffn_norm_w, ffn_up_w, ffn_conv_w, ffn_conv_b, ffn_down_w, final_norm_w)
    moms = (m_ssm_norm_w, m_ssm_in_w, m_ssm_conv_w, m_ssm_conv_b, m_ssm_dt_bias, m_ssm_a_log, m_ssm_d, m_ssm_gate_norm_w, m_ssm_out_w, m_kv_norm_w, m_w_k, m_w_v, m_attn_norm_w, m_w_q, m_w_o, m_ffn_norm_w, m_ffn_up_w, m_ffn_conv_w, m_ffn_conv_b, m_ffn_down_w, m_final_norm_w)
    vels = (v_ssm_norm_w, v_ssm_in_w, v_ssm_conv_w, v_ssm_conv_b, v_ssm_dt_bias, v_ssm_a_log, v_ssm_d, v_ssm_gate_norm_w, v_ssm_out_w, v_kv_norm_w, v_w_k, v_w_v, v_attn_norm_w, v_w_q, v_w_o, v_ffn_norm_w, v_ffn_up_w, v_ffn_conv_w, v_ffn_conv_b, v_ffn_down_w, v_final_norm_w)
    local = dict(zip(WEIGHTS, args))
    m_in = dict(zip(WEIGHTS, moms))
    v_in = dict(zip(WEIGHTS, vels))
    c = lax.axis_index("c")
    chip = 2 * lax.axis_index("x") + lax.axis_index("y")

    big_shard = _pack([local[n].reshape(-1) for n in BIG], BF16, 32)
    big16 = _gather_chips_halves(big_shard.reshape(2, -1, LANES), name="gather_big").reshape((N_CHIPS,) + big_shard.shape)
    big16 = _set_slot(big16, big_shard, chip)
    small32 = _gather_chips(_pack([local[n].reshape(-1) for n in SMALL], F32, 8), name="gather_small")
    full = {n: local[n] for n in REPLICATED}
    for names, buf in ((BIG, big16), (SMALL, small32)):
        for n, st in zip(names, _unpack(buf, [local[n].shape for n in names])):
            full[n] = _from_shards(st, SHARD_AXIS[n])

    loss_p, dx, grads = _step(x[0], loss_target[0], full)

    sharded = BIG + SMALL
    g4 = _pack([_to_shards(grads[n], SHARD_AXIS[n]).reshape(N_CHIPS, -1) for n in sharded], F32, RS_ROW_MULT)
    rows = g4.shape[1]
    g4h = g4.reshape(N_CHIPS, 2, rows // 2, LANES)
    pair = _pair_add(g4h, _swap_other_half(g4h, name="rs_pair_swap"), c, out_dtype=BF16, name="rs_pair_add")
    mine = lax.dynamic_index_in_dim(pair, chip, axis=0, keepdims=False)
    half = _sum_leading(_set_slot(_scatter_chips(pair, name="rs_chip_scatter"), mine, chip), name="rs_chip_sum")
    shard = _set_slot(_join_halves(half, name="rs_half_join"), half, c).reshape(rows, LANES)
    gshard = dict(zip(sharded, _unpack(shard, [local[n].shape for n in sharded])))

    rep = _pack([loss_p.reshape(-1)] + [grads[n].reshape(-1) for n in REPLICATED], F32, 8)
    tot = _sum_leading(_gather_all(rep, name="ar_gather"), name="ar_sum")
    parts = _unpack(tot, [(LANES,)] + [local[n].shape for n in REPLICATED])
    loss = jnp.sum(parts[0])
    gshard.update(dict(zip(REPLICATED, parts[1:])))

    deltas, new_m, new_v = [], [], []
    for n in WEIGHTS:
        d, nm, nv = _adamw(local[n], gshard[n], m_in[n], v_in[n], name=f"adamw_{n}")
        deltas.append(d)
        new_m.append(nm)
        new_v.append(nv)
    return (loss, dx[None], *[gshard[n] for n in WEIGHTS], *deltas, *new_m, *new_v)
```

```python
import math

import jax
import jax.numpy as jnp
from jax import lax
from jax.experimental import pallas as pl
from jax.experimental.pallas import tpu as pltpu

D_MODEL = 1024
D_INNER = 2048
SSM_HEAD_DIM = 64
SSM_HEADS = 32
SSM_GROUPS = 4
SSM_STATE = 128
SSM_CONV = 4
SSM_CHUNK = 128
GN = SSM_GROUPS * SSM_STATE
CONV_DIM = D_INNER + 2 * GN
SB_HEADS = 16
SB_HEAD_DIM = 64
D_FF = 2816
FFN_CONV = 3
EPS = 1e-6
ADAM_LR = 0.001
ADAM_B1 = 0.9
ADAM_B2 = 0.999
ADAM_EPS = 1e-08
ADAM_WD = 0.01
ADAM_STEP = 10

LANES = 128
SUBLANES = 8
VMEM_LIMIT = 48 * 1024 * 1024
ADAM_BLOCK_BYTES = 1 << 20
F32 = jnp.float32
BF16 = jnp.bfloat16
MESH = pl.DeviceIdType.MESH


def _cparams(sem=None):
    return pltpu.CompilerParams(dimension_semantics=sem, vmem_limit_bytes=VMEM_LIMIT)


def _tile(n, cands):
    for c in cands:
        if n % c == 0:
            return c
    return n


def _nt(a, b):
    return lax.dot_general(a, b, (((1,), (1,)), ((), ())), preferred_element_type=F32)


def _tn(a, b):
    return lax.dot_general(a, b, (((0,), (0,)), ((), ())), preferred_element_type=F32)


def _nn(a, b):
    return jnp.dot(a, b, preferred_element_type=F32)


def _split(x, pieces):
    out = []
    for _ in range(pieces - 1):
        h = x.astype(BF16)
        out.append(h)
        x = x - h.astype(F32)
    out.append(x.astype(BF16))
    return out


def _ones_dot(ones, x, *, ones_left, pieces=3):
    o16 = ones.astype(BF16)
    acc = None
    for piece in _split(x, pieces):
        term = _nn(o16, piece) if ones_left else _nn(piece, o16)
        acc = term if acc is None else acc + term
    return acc


def _row_sums(x):
    return _ones_dot(jnp.ones((x.shape[1], LANES), F32), x, ones_left=False)


def _softplus(x):
    return jnp.maximum(x, 0.0) + jnp.log(1.0 + jnp.exp(-jnp.abs(x)))


def _sigmoid(x):
    e = jnp.exp(-jnp.abs(x))
    r = 1.0 / (1.0 + e)
    return jnp.where(x >= 0, r, e * r)


MM_TILE_MAX = 1408
MM_VMEM_BUDGET = 40 * 1024 * 1024


def _divisors(n, cap):
    out = [d for d in range(min(cap, n) // LANES * LANES, 0, -LANES) if n % d == 0]
    return out or [n]


def _mm_tiles(m, n, k, a_bytes, b_bytes, o_bytes, add_bytes):
    best = None
    for tm in _divisors(m, MM_TILE_MAX):
        for tn in _divisors(n, MM_TILE_MAX):
            for tk in _divisors(k, MM_TILE_MAX):
                vmem = 2 * (tm * tk * a_bytes + tk * tn * b_bytes + tm * tn * (o_bytes + add_bytes)) + tm * tn * 4
                if vmem > MM_VMEM_BUDGET:
                    continue
                score = (tm * tn * tk, tm * tn)
                if best is None or score > best[0]:
                    best = (score, (tm, tn, tk))
    return best[1]


def _matmul(a, b, *, ta=False, tb=False, add=None, out_dtype=F32, out_parts=1, name):
    a_parts = a.shape[0] if a.ndim == 3 else 1
    b_parts = b.shape[0] if b.ndim == 3 else 1
    assert not (ta and a_parts > 1)
    a2, b2 = a.shape[-2:], b.shape[-2:]
    m, k = (a2[1], a2[0]) if ta else (a2[0], a2[1] * a_parts)
    n, kb = (b2[0], b2[1] * b_parts) if tb else (b2[1] * b_parts, b2[0])
    assert kb == k, (a.shape, b.shape)
    n_unit = math.gcd(n // out_parts, n if tb else b2[1])
    k_unit = math.gcd(k // a_parts, b2[1] if tb else k)
    tm, tn, tk = _mm_tiles(m, n_unit, k_unit, a.dtype.itemsize, b.dtype.itemsize, jnp.dtype(out_dtype).itemsize,
                           0 if add is None else add.dtype.itemsize)
    nk = k // tk
    ka, kbp = (k // a_parts) // tk, (k // b_parts) // tk
    nb, no = (n // b_parts) // tn, (n // out_parts) // tn

    def body(*refs):
        if add is None:
            a_ref, b_ref, o_ref, acc_ref = refs
            add_ref = None
        else:
            a_ref, b_ref, add_ref, o_ref, acc_ref = refs
        kk = pl.program_id(2)

        @pl.when(kk == 0)
        def _():
            acc_ref[...] = jnp.zeros_like(acc_ref)

        av = a_ref[...].astype(BF16)
        bv = b_ref[...].astype(BF16)
        dn = (((0 if ta else 1,), (1 if tb else 0,)), ((), ()))
        acc_ref[...] += lax.dot_general(av, bv, dn, preferred_element_type=F32)

        @pl.when(kk == nk - 1)
        def _():
            r = acc_ref[...]
            if add_ref is not None:
                r = r + add_ref[...].astype(F32)
            o_ref[...] = r.astype(o_ref.dtype)

    if ta:
        a_spec = pl.BlockSpec((tk, tm), lambda i, j, kk: (kk, i))
    elif a_parts > 1:
        a_spec = pl.BlockSpec((None, tm, tk), lambda i, j, kk: (kk // ka, i, kk % ka))
    else:
        a_spec = pl.BlockSpec((tm, tk), lambda i, j, kk: (i, kk))
    if b_parts == 1:
        b_spec = pl.BlockSpec((tn, tk), lambda i, j, kk: (j, kk)) if tb else pl.BlockSpec((tk, tn), lambda i, j, kk: (kk, j))
    elif tb:
        b_spec = pl.BlockSpec((None, tn, tk), lambda i, j, kk: (kk // kbp, j, kk % kbp))
    else:
        b_spec = pl.BlockSpec((None, tk, tn), lambda i, j, kk: (j // nb, kk, j % nb))
    if out_parts > 1:
        o_spec = pl.BlockSpec((None, tm, tn), lambda i, j, kk: (j // no, i, j % no))
        o_shape = jax.ShapeDtypeStruct((out_parts, m, n // out_parts), out_dtype)
    else:
        o_spec = pl.BlockSpec((tm, tn), lambda i, j, kk: (i, j))
        o_shape = jax.ShapeDtypeStruct((m, n), out_dtype)
    in_specs = [a_spec, b_spec]
    args = [a, b]
    if add is not None:
        in_specs.append(pl.BlockSpec((tm, tn), lambda i, j, kk: (i, j)))
        args.append(add)
    return pl.pallas_call(
        body,
        grid=(m // tm, n // tn, nk),
        in_specs=in_specs,
        out_specs=o_spec,
        out_shape=o_shape,
        scratch_shapes=[pltpu.VMEM((tm, tn), F32)],
        compiler_params=_cparams(("parallel", "parallel", "arbitrary")),
        name=name,
    )(*args)


def _rmsnorm_fwd(x, w, *, name):
    t, d = x.shape
    tb = _tile(t, (512, 256, 128))

    def body(x_ref, w_ref, o_ref):
        xv = x_ref[...]
        r = lax.rsqrt(jnp.mean(xv * xv, axis=-1, keepdims=True) + EPS)
        o_ref[...] = (xv * r * w_ref[...]).astype(o_ref.dtype)

    return pl.pallas_call(
        body,
        grid=(t // tb,),
        in_specs=[pl.BlockSpec((tb, d), lambda i: (i, 0)), pl.BlockSpec((1, d), lambda i: (0, 0))],
        out_specs=pl.BlockSpec((tb, d), lambda i: (i, 0)),
        out_shape=jax.ShapeDtypeStruct((t, d), BF16),
        compiler_params=_cparams(("parallel",)),
        name=name,
    )(x, w.reshape(1, d))


def _rmsnorm_bwd(x, dys, dres, *, name):
    t, d = x.shape
    tb = _tile(t, (256, 128))
    nn = len(dys)
    has_res = dres is not None

    def body(*refs):
        x_ref = refs[0]
        dy_refs = refs[1:1 + nn]
        w_refs = refs[1 + nn:1 + 2 * nn]
        pos = 1 + 2 * nn
        res_ref = refs[pos] if has_res else None
        pos += 1 if has_res else 0
        dx_ref = refs[pos]
        dw_refs = refs[pos + 1:pos + 1 + nn]
        i = pl.program_id(0)
        xv = x_ref[...]
        r = lax.rsqrt(jnp.mean(xv * xv, axis=-1, keepdims=True) + EPS)
        xn = xv * r
        dx = res_ref[...] if has_res else jnp.zeros_like(xv)
        for q in range(nn):
            dy = dy_refs[q][...].astype(F32)
            g = dy * w_refs[q][...]
            dx = dx + r * (g - xn * jnp.mean(g * xn, axis=-1, keepdims=True))
            dwp = jnp.sum(dy * xn, axis=0, keepdims=True)

            @pl.when(i == 0)
            def _(q=q, dwp=dwp):
                dw_refs[q][...] = dwp

            @pl.when(i > 0)
            def _(q=q, dwp=dwp):
                dw_refs[q][...] += dwp
        dx_ref[...] = dx

    row = pl.BlockSpec((tb, d), lambda i: (i, 0))
    vec = pl.BlockSpec((1, d), lambda i: (0, 0))
    in_specs = [row] + [row] * nn + [vec] * nn + ([row] if has_res else [])
    args = [x] + [p[0] for p in dys] + [p[1].reshape(1, d) for p in dys] + ([dres] if has_res else [])
    outs = pl.pallas_call(
        body,
        grid=(t // tb,),
        in_specs=in_specs,
        out_specs=[row] + [vec] * nn,
        out_shape=[jax.ShapeDtypeStruct((t, d), F32)] + [jax.ShapeDtypeStruct((1, d), F32)] * nn,
        compiler_params=_cparams(("arbitrary",)),
        name=name,
    )(*args)
    return outs[0], list(outs[1:])


def _loss_head(x, w, target, *, name):
    t, d = x.shape
    tb = _tile(t, (256, 128))

    def body(x_ref, w_ref, t_ref, loss_ref, dx_ref, dw_ref):
        i = pl.program_id(0)
        xv = x_ref[...]
        wv = w_ref[...]
        r = lax.rsqrt(jnp.mean(xv * xv, axis=-1, keepdims=True) + EPS)
        xn = xv * r
        e = xn * wv - t_ref[...]
        lp = 0.5 * jnp.sum(jnp.mean(e * e, axis=-1, keepdims=True), axis=0, keepdims=True)
        dy = e * (1.0 / d)
        g = dy * wv
        dx_ref[...] = r * (g - xn * jnp.mean(g * xn, axis=-1, keepdims=True))
        dwp = jnp.sum(dy * xn, axis=0, keepdims=True)
        lpv = jnp.broadcast_to(lp, (1, LANES)) * (1.0 / LANES)

        @pl.when(i == 0)
        def _():
            dw_ref[...] = dwp
            loss_ref[...] = lpv

        @pl.when(i > 0)
        def _():
            dw_ref[...] += dwp
            loss_ref[...] += lpv

    row = pl.BlockSpec((tb, d), lambda i: (i, 0))
    vec = pl.BlockSpec((1, d), lambda i: (0, 0))
    return pl.pallas_call(
        body,
        grid=(t // tb,),
        in_specs=[row, vec, row],
        out_specs=[pl.BlockSpec((1, LANES), lambda i: (0, 0)), row, vec],
        out_shape=[jax.ShapeDtypeStruct((1, LANES), F32), jax.ShapeDtypeStruct((t, d), F32),
                   jax.ShapeDtypeStruct((1, d), F32)],
        compiler_params=_cparams(("arbitrary",)),
        name=name,
    )(x, w.reshape(1, d), target)


ROW_CHUNK = 64
PAD = SUBLANES


def _shifted(pad_ref, r0, rows, back):
    return pad_ref[pl.ds(PAD + r0 - back, rows), :]


def _conv_taps(pad_ref, w_ref, r0, rows, kw):
    acc = None
    for j in range(kw):
        term = _shifted(pad_ref, r0, rows, kw - 1 - j) * w_ref[j:j + 1, :]
        acc = term if acc is None else acc + term
    return acc


def _fill_pad(pad_ref, x_ref, t):
    pad_ref[0:PAD, :] = jnp.zeros((PAD, pad_ref.shape[1]), F32)
    pad_ref[pl.ds(PAD + t, PAD), :] = jnp.zeros((PAD, pad_ref.shape[1]), F32)
    pad_ref[pl.ds(PAD, t), :] = x_ref[...].astype(F32)


def _conv_silu_fwd(x, w, b, *, x_off=0, name):
    t = x.shape[0]
    kw, c = w.shape
    cw = _tile(math.gcd(c, x_off) if x_off else c, (256, 128))
    ob = x_off // cw
    rc = _tile(t, (ROW_CHUNK,))

    def body(x_ref, w_ref, b_ref, o_ref, pad_ref):
        _fill_pad(pad_ref, x_ref, t)
        for r0 in range(0, t, rc):
            pre = _conv_taps(pad_ref, w_ref, r0, rc, kw) + b_ref[...]
            o_ref[pl.ds(r0, rc), :] = pre * _sigmoid(pre)

    strip = pl.BlockSpec((t, cw), lambda i: (0, i))
    return pl.pallas_call(
        body,
        grid=(c // cw,),
        in_specs=[pl.BlockSpec((t, cw), lambda i: (0, i + ob)), pl.BlockSpec((kw, cw), lambda i: (0, i)),
                  pl.BlockSpec((1, cw), lambda i: (0, i))],
        out_specs=strip,
        out_shape=jax.ShapeDtypeStruct((t, c), F32),
        scratch_shapes=[pltpu.VMEM((t + 2 * PAD, cw), F32)],
        compiler_params=_cparams(("parallel",)),
        name=name,
    )(x, w, b.reshape(1, c))


def _conv_bwd_core(dpre_pad_ref, x_pad_ref, w_ref, dx_ref, dw_ref, db_ref, t, rc, kw):
    cw = dx_ref.shape[1]

    def fold(a):
        return jnp.sum(a.reshape(rc // SUBLANES, SUBLANES, cw), axis=0) if rc % SUBLANES == 0 else jnp.sum(a, axis=0, keepdims=True)

    dws = [None] * kw
    dbs = None
    for r0 in range(0, t, rc):
        dpre = dpre_pad_ref[pl.ds(PAD + r0, rc), :]
        dx = None
        for j in range(kw):
            s = kw - 1 - j
            term = dpre_pad_ref[pl.ds(PAD + r0 + s, rc), :] * w_ref[j:j + 1, :]
            dx = term if dx is None else dx + term
            part = fold(dpre * _shifted(x_pad_ref, r0, rc, s))
            dws[j] = part if dws[j] is None else dws[j] + part
        part = fold(dpre)
        dbs = part if dbs is None else dbs + part
        dx_ref[pl.ds(r0, rc), :] = dx
    for j in range(kw):
        dw_ref[j:j + 1, :] = jnp.sum(dws[j], axis=0, keepdims=True)
    db_ref[...] = jnp.sum(dbs, axis=0, keepdims=True)


def _conv_silu_bwd(x, w, b, dact, *, x_off=0, name):
    t = x.shape[0]
    kw, c = w.shape
    cw = _tile(math.gcd(c, x_off) if x_off else c, (256, 128))
    ob = x_off // cw
    rc = _tile(t, (ROW_CHUNK,))

    def body(x_ref, w_ref, b_ref, da_ref, dx_ref, dw_ref, db_ref, xpad_ref, dpad_ref):
        _fill_pad(xpad_ref, x_ref, t)
        dpad_ref[0:PAD, :] = jnp.zeros((PAD, cw), F32)
        dpad_ref[pl.ds(PAD + t, PAD), :] = jnp.zeros((PAD, cw), F32)
        for r0 in range(0, t, rc):
            pre = _conv_taps(xpad_ref, w_ref, r0, rc, kw) + b_ref[...]
            sg = _sigmoid(pre)
            dpad_ref[pl.ds(PAD + r0, rc), :] = da_ref[pl.ds(r0, rc), :] * (sg * (1.0 + pre * (1.0 - sg)))
        _conv_bwd_core(dpad_ref, xpad_ref, w_ref, dx_ref, dw_ref, db_ref, t, rc, kw)

    strip = pl.BlockSpec((t, cw), lambda i: (0, i))
    wspec = pl.BlockSpec((kw, cw), lambda i: (0, i))
    bspec = pl.BlockSpec((1, cw), lambda i: (0, i))
    return pl.pallas_call(
        body,
        grid=(c // cw,),
        in_specs=[pl.BlockSpec((t, cw), lambda i: (0, i + ob)), wspec, bspec, strip],
        out_specs=[strip, wspec, bspec],
        out_shape=[jax.ShapeDtypeStruct((t, c), F32), jax.ShapeDtypeStruct((kw, c), F32),
                   jax.ShapeDtypeStruct((1, c), F32)],
        scratch_shapes=[pltpu.VMEM((t + 2 * PAD, cw), F32), pltpu.VMEM((t + 2 * PAD, cw), F32)],
        compiler_params=_cparams(("parallel",)),
        name=name,
    )(x, w, b.reshape(1, c), dact)


def _conv_glu_fwd(hid, w, b, *, name):
    t, c2 = hid.shape
    f = c2 // 2
    kw = w.shape[0]
    cw = _tile(f, (256, 128))
    nf = f // cw
    rc = _tile(t, (ROW_CHUNK,))

    def body(g_ref, v_ref, wg_ref, wv_ref, bg_ref, bv_ref, o_ref, gpad_ref, vpad_ref):
        _fill_pad(gpad_ref, g_ref, t)
        _fill_pad(vpad_ref, v_ref, t)
        for r0 in range(0, t, rc):
            gate = _conv_taps(gpad_ref, wg_ref, r0, rc, kw) + bg_ref[...]
            val = _conv_taps(vpad_ref, wv_ref, r0, rc, kw) + bv_ref[...]
            o_ref[pl.ds(r0, rc), :] = (gate * _sigmoid(gate) * val).astype(o_ref.dtype)

    gs = pl.BlockSpec((t, cw), lambda i: (0, i))
    vs = pl.BlockSpec((t, cw), lambda i: (0, i + nf))
    b2 = b.reshape(1, c2)
    return pl.pallas_call(
        body,
        grid=(nf,),
        in_specs=[gs, vs, pl.BlockSpec((kw, cw), lambda i: (0, i)), pl.BlockSpec((kw, cw), lambda i: (0, i + nf)),
                  pl.BlockSpec((1, cw), lambda i: (0, i)), pl.BlockSpec((1, cw), lambda i: (0, i + nf))],
        out_specs=gs,
        out_shape=jax.ShapeDtypeStruct((t, f), BF16),
        scratch_shapes=[pltpu.VMEM((t + 2 * PAD, cw), F32), pltpu.VMEM((t + 2 * PAD, cw), F32)],
        compiler_params=_cparams(("parallel",)),
        name=name,
    )(hid, hid, w, w, b2, b2)


def _conv_glu_bwd(hid, w, b, dact, *, name):
    t, c2 = hid.shape
    f = c2 // 2
    kw = w.shape[0]
    cw = _tile(f, (128,))
    nf = f // cw
    rc = _tile(t, (ROW_CHUNK,))

    def body(g_ref, v_ref, wg_ref, wv_ref, bg_ref, bv_ref, da_ref,
             dgv_ref, dwg_ref, dwv_ref, dbg_ref, dbv_ref,
             gpad_ref, vpad_ref, dgpad_ref, dvpad_ref):
        _fill_pad(gpad_ref, g_ref, t)
        _fill_pad(vpad_ref, v_ref, t)
        for ref in (dgpad_ref, dvpad_ref):
            ref[0:PAD, :] = jnp.zeros((PAD, cw), F32)
            ref[pl.ds(PAD + t, PAD), :] = jnp.zeros((PAD, cw), F32)
        for r0 in range(0, t, rc):
            gate = _conv_taps(gpad_ref, wg_ref, r0, rc, kw) + bg_ref[...]
            val = _conv_taps(vpad_ref, wv_ref, r0, rc, kw) + bv_ref[...]
            sg = _sigmoid(gate)
            da = da_ref[pl.ds(r0, rc), :].astype(F32)
            dgpad_ref[pl.ds(PAD + r0, rc), :] = da * val * (sg * (1.0 + gate * (1.0 - sg)))
            dvpad_ref[pl.ds(PAD + r0, rc), :] = da * (gate * sg)
        _conv_bwd_core(dgpad_ref, gpad_ref, wg_ref, dgv_ref.at[0], dwg_ref, dbg_ref, t, rc, kw)
        _conv_bwd_core(dvpad_ref, vpad_ref, wv_ref, dgv_ref.at[1], dwv_ref, dbv_ref, t, rc, kw)

    gs = pl.BlockSpec((t, cw), lambda i: (0, i))
    vs = pl.BlockSpec((t, cw), lambda i: (0, i + nf))
    wg = pl.BlockSpec((kw, cw), lambda i: (0, i))
    wv = pl.BlockSpec((kw, cw), lambda i: (0, i + nf))
    bg = pl.BlockSpec((1, cw), lambda i: (0, i))
    bv = pl.BlockSpec((1, cw), lambda i: (0, i + nf))
    b2 = b.reshape(1, c2)
    pad = pltpu.VMEM((t + 2 * PAD, cw), F32)
    return pl.pallas_call(
        body,
        grid=(nf,),
        in_specs=[gs, vs, wg, wv, bg, bv, gs],
        out_specs=[pl.BlockSpec((2, t, cw), lambda i: (0, 0, i)), wg, wg, bg, bg],
        out_shape=[jax.ShapeDtypeStruct((2, t, f), F32),
                   jax.ShapeDtypeStruct((kw, f), F32), jax.ShapeDtypeStruct((kw, f), F32),
                   jax.ShapeDtypeStruct((1, f), F32), jax.ShapeDtypeStruct((1, f), F32)],
        scratch_shapes=[pad, pad, pad, pad],
        compiler_params=_cparams(("parallel",)),
        name=name,
    )(hid, hid, w, w, b2, b2, dact)


def _gate_norm_fwd(y, zx, w, *, name):
    t, di = y.shape
    gsz = di // SSM_GROUPS
    tb = _tile(t, (256, 128))

    def body(y_ref, z_ref, w_ref, o_ref):
        for g in range(SSM_GROUPS):
            sl = slice(g * gsz, (g + 1) * gsz)
            zv = z_ref[:, sl]
            gv = y_ref[:, sl] * (zv * _sigmoid(zv))
            r = lax.rsqrt(jnp.mean(gv * gv, axis=-1, keepdims=True) + EPS)
            o_ref[:, sl] = (gv * r * w_ref[:, sl]).astype(o_ref.dtype)

    row = pl.BlockSpec((tb, di), lambda i: (i, 0))
    return pl.pallas_call(
        body,
        grid=(t // tb,),
        in_specs=[row, row, pl.BlockSpec((1, di), lambda i: (0, 0))],
        out_specs=row,
        out_shape=jax.ShapeDtypeStruct((t, di), BF16),
        compiler_params=_cparams(("parallel",)),
        name=name,
    )(y, zx, w.reshape(1, di))


def _gate_norm_bwd(y, zx, w, dyn, *, name):
    t, di = y.shape
    gsz = di // SSM_GROUPS
    tb = _tile(t, (256, 128))

    def body(y_ref, z_ref, w_ref, d_ref, dy_ref, dz_ref, dw_ref):
        i = pl.program_id(0)
        for g in range(SSM_GROUPS):
            sl = slice(g * gsz, (g + 1) * gsz)
            zv = z_ref[:, sl]
            yv = y_ref[:, sl]
            sg = _sigmoid(zv)
            sz = zv * sg
            gv = yv * sz
            r = lax.rsqrt(jnp.mean(gv * gv, axis=-1, keepdims=True) + EPS)
            gn = gv * r
            dn = d_ref[:, sl].astype(F32)
            q = dn * w_ref[:, sl]
            dg = r * (q - gn * jnp.mean(q * gn, axis=-1, keepdims=True))
            dy_ref[:, sl] = dg * sz
            dz_ref[:, sl] = dg * yv * (sg * (1.0 + zv * (1.0 - sg)))
            dwp = jnp.sum(dn * gn, axis=0, keepdims=True)

            @pl.when(i == 0)
            def _(sl=sl, dwp=dwp):
                dw_ref[:, sl] = dwp

            @pl.when(i > 0)
            def _(sl=sl, dwp=dwp):
                dw_ref[:, sl] += dwp

    row = pl.BlockSpec((tb, di), lambda i: (i, 0))
    vec = pl.BlockSpec((1, di), lambda i: (0, 0))
    return pl.pallas_call(
        body,
        grid=(t // tb,),
        in_specs=[row, row, vec, row],
        out_specs=[row, row, vec],
        out_shape=[jax.ShapeDtypeStruct((t, di), F32), jax.ShapeDtypeStruct((t, di), F32),
                   jax.ShapeDtypeStruct((1, di), F32)],
        compiler_params=_cparams(("arbitrary",)),
        name=name,
    )(y, zx, w.reshape(1, di), dyn)


def _adamw(w, g, m, v, *, name):
    shape = w.shape
    cols = shape[-1]
    rows = w.size // cols
    w2, g2, m2, v2 = (a.reshape(rows, cols) for a in (w, g, m, v))
    tr = rows
    if rows * cols * 4 > ADAM_BLOCK_BYTES:
        tr = _tile(rows, tuple(r for r in (512, 256, 128, 64, 32, 16, 8) if r * cols * 4 <= ADAM_BLOCK_BYTES))
    c1 = 1.0 - ADAM_B1 ** ADAM_STEP
    c2 = 1.0 - ADAM_B2 ** ADAM_STEP

    def body(w_ref, g_ref, m_ref, v_ref, d_ref, nm_ref, nv_ref):
        gv = g_ref[...]
        nm = ADAM_B1 * m_ref[...] + (1.0 - ADAM_B1) * gv
        nv = ADAM_B2 * v_ref[...] + (1.0 - ADAM_B2) * (gv * gv)
        d_ref[...] = -ADAM_LR * ((nm / c1) / (jnp.sqrt(nv / c2) + ADAM_EPS) + ADAM_WD * w_ref[...])
        nm_ref[...] = nm
        nv_ref[...] = nv

    blk = pl.BlockSpec((tr, cols), lambda i: (i, 0))
    outs = pl.pallas_call(
        body,
        grid=(rows // tr,),
        in_specs=[blk] * 4,
        out_specs=[blk] * 3,
        out_shape=[jax.ShapeDtypeStruct((rows, cols), F32)] * 3,
        compiler_params=_cparams(("parallel",)),
        name=name,
    )(w2, g2, m2, v2)
    return tuple(o.reshape(shape) for o in outs)


def _adamw_layers(w, gs, m, v, *, name):
    n_l, rows, cols = w.shape
    assert len(gs) == n_l
    tr = _tile(rows, tuple(r for r in (512, 256, 128, 64, 32, 16, 8) if r * cols * 4 <= ADAM_BLOCK_BYTES))
    c1 = 1.0 - ADAM_B1 ** ADAM_STEP
    c2 = 1.0 - ADAM_B2 ** ADAM_STEP

    def body(*refs):
        w_ref, m_ref, v_ref = refs[:3]
        g_refs = refs[3:3 + n_l]
        g_ref, d_ref, nm_ref, nv_ref = refs[3 + n_l:]
        layer = pl.program_id(0)
        gv = g_refs[0][...]
        for q in range(1, n_l):
            gv = jnp.where(layer == q, g_refs[q][...], gv)
        nm = ADAM_B1 * m_ref[...] + (1.0 - ADAM_B1) * gv
        nv = ADAM_B2 * v_ref[...] + (1.0 - ADAM_B2) * (gv * gv)
        g_ref[...] = gv
        d_ref[...] = -ADAM_LR * ((nm / c1) / (jnp.sqrt(nv / c2) + ADAM_EPS) + ADAM_WD * w_ref[...])
        nm_ref[...] = nm
        nv_ref[...] = nv

    stacked = pl.BlockSpec((None, tr, cols), lambda l, i: (l, i, 0))
    single = pl.BlockSpec((tr, cols), lambda l, i: (i, 0))
    return pl.pallas_call(
        body,
        grid=(n_l, rows // tr),
        in_specs=[stacked] * 3 + [single] * n_l,
        out_specs=[stacked] * 4,
        out_shape=[jax.ShapeDtypeStruct(w.shape, F32)] * 4,
        compiler_params=_cparams(("parallel", "parallel")),
        name=name,
    )(w, m, v, *gs)


def _ssd_scalars(dtc_ref, dtr_ref, hpc_ref, hpr_ref, ln):
    assert SSM_CHUNK == SSM_STATE == LANES, "the SSD kernels mix chunk, state and lane-wide tiles freely"
    bias_c, alog_c = hpc_ref[0, 0:1, :], hpc_ref[0, 1:2, :]
    bias_r, alog_r = hpr_ref[0, :, 0:1], hpr_ref[0, :, 1:2]
    a_c, a_r = -jnp.exp(alog_c), -jnp.exp(alog_r)
    raw_c = dtc_ref[0] + bias_c
    dt_c = _softplus(raw_c)
    dt_r = _softplus(dtr_ref[0] + bias_r)
    row = lax.broadcasted_iota(jnp.int32, (ln, ln), 0)
    col = lax.broadcasted_iota(jnp.int32, (ln, ln), 1)
    lower = (col <= row).astype(F32)
    upper = (row <= col).astype(F32)
    acs_c = _ones_dot(lower, dt_c * a_c, ones_left=True)
    acs_r = _ones_dot(upper, dt_r * a_r, ones_left=False)
    return raw_c, dt_c, a_c, acs_c, acs_r, row, col


def _ssd_specs(t, di, g_n, n_st, rp, ln, r_h, rev):
    nc = t // ln
    cidx = (lambda c: nc - 1 - c) if rev else (lambda c: c)
    xs = pl.BlockSpec((ln, rp), lambda g, c: (cidx(c), g))
    bm = pl.BlockSpec((ln, n_st), lambda g, c: (cidx(c), di // n_st + g))
    cm = pl.BlockSpec((ln, n_st), lambda g, c: (cidx(c), di // n_st + g_n + g))
    dtc = pl.BlockSpec((1, ln, r_h), lambda g, c: (g, cidx(c), 0))
    dtr = pl.BlockSpec((1, r_h, ln), lambda g, c: (g, 0, cidx(c)))
    hpc = pl.BlockSpec((1, 3, r_h), lambda g, c: (g, 0, 0))
    hpr = pl.BlockSpec((1, r_h, 3), lambda g, c: (g, 0, 0))
    prev = pl.BlockSpec((1, rp, n_st), lambda g, c: (cidx(c), g, 0))
    return xs, bm, cm, dtc, dtr, hpc, hpr, prev


def _ssd_fwd(xbc, dtc, dtr, hpc, hpr, *, name):
    t = xbc.shape[0]
    di, g_n, n_st, p_h, ln = D_INNER, SSM_GROUPS, SSM_STATE, SSM_HEAD_DIM, SSM_CHUNK
    r_h = SSM_HEADS // g_n
    rp = r_h * p_h
    nc = t // ln

    def body(xs_ref, b_ref, c_ref, dtc_ref, dtr_ref, hpc_ref, hpr_ref, y_ref, prev_ref, st_ref):
        @pl.when(pl.program_id(1) == 0)
        def _():
            st_ref[...] = jnp.zeros_like(st_ref)

        _, dt_c, _, acs_c, acs_r, row, col = _ssd_scalars(dtc_ref, dtr_ref, hpc_ref, hpr_ref, ln)
        bm = b_ref[...]
        cm = c_ref[...]
        cm16 = cm.astype(BF16)
        cb = _nt(cm16, bm.astype(BF16))
        causal = row >= col
        for r in range(r_h):
            sl = slice(r * p_h, (r + 1) * p_h)
            xs = xs_ref[:, sl]
            acs = jnp.broadcast_to(acs_c[:, r:r + 1], (ln, ln))
            last = acs[ln - 1:ln, :]
            lm = jnp.where(causal, jnp.exp(acs - acs_r[r:r + 1, :]), 0.0)
            xd = (xs * jnp.broadcast_to(dt_c[:, r:r + 1], (ln, p_h))).astype(BF16)
            prev = st_ref[sl, :]
            y = _nn((cb * lm).astype(BF16), xd)
            y = y + _nt(cm16, prev.astype(BF16)) * jnp.exp(acs[:, :p_h])
            y_ref[:, sl] = y + hpc_ref[0, 2:3, r:r + 1] * xs
            prev_ref[0, sl, :] = prev
            bd = (bm * jnp.exp(last - acs[:, :n_st])).astype(BF16)
            st_ref[sl, :] = prev * jnp.exp(last[:, :n_st]) + _tn(xd, bd)

    xs, bm, cm, dtcs, dtrs, hpcs, hprs, prev = _ssd_specs(t, di, g_n, n_st, rp, ln, r_h, False)
    return pl.pallas_call(
        body,
        grid=(g_n, nc),
        in_specs=[xs, bm, cm, dtcs, dtrs, hpcs, hprs],
        out_specs=[xs, prev],
        out_shape=[jax.ShapeDtypeStruct((t, di), F32), jax.ShapeDtypeStruct((nc, g_n * rp, n_st), F32)],
        scratch_shapes=[pltpu.VMEM((rp, n_st), F32)],
        compiler_params=_cparams(("parallel", "arbitrary")),
        name=name,
    )(xbc, xbc, xbc, dtc, dtr, hpc, hpr)


def _ssd_bwd(xbc, dtc, dtr, hpc, hpr, prev, dy, *, name):
    t = xbc.shape[0]
    di, g_n, n_st, p_h, ln = D_INNER, SSM_GROUPS, SSM_STATE, SSM_HEAD_DIM, SSM_CHUNK
    r_h = SSM_HEADS // g_n
    rp = r_h * p_h
    nc = t // ln

    def body(xs_ref, b_ref, c_ref, dtc_ref, dtr_ref, hpc_ref, hpr_ref, prev_ref, dy_ref,
             dxs_ref, db_ref, dc_ref, ddt_ref, hg_ref, ds_ref):
        step = pl.program_id(1)

        @pl.when(step == 0)
        def _():
            ds_ref[...] = jnp.zeros_like(ds_ref)

        raw_c, dt_c, a_c, acs_c, acs_r, row, col = _ssd_scalars(dtc_ref, dtr_ref, hpc_ref, hpr_ref, ln)
        bm = b_ref[...]
        cm = c_ref[...]
        bm16, cm16 = bm.astype(BF16), cm.astype(BF16)
        cb = _nt(cm16, bm16)
        cbt = _nt(bm16, cm16)
        lane_r = lax.broadcasted_iota(jnp.int32, (ln, r_h), 1)
        dacs_all = jnp.zeros((ln, r_h), F32)
        ddtx_all = jnp.zeros((ln, r_h), F32)
        dd_all = jnp.zeros((ln, r_h), F32)
        dcb = jnp.zeros((ln, ln), F32)
        dcbt = jnp.zeros((ln, ln), F32)
        dc_acc = jnp.zeros((ln, n_st), F32)
        db_acc = jnp.zeros((ln, n_st), F32)
        for r in range(r_h):
            sl = slice(r * p_h, (r + 1) * p_h)
            xs = xs_ref[:, sl]
            dyv = dy_ref[:, sl]
            dy16 = dyv.astype(BF16)
            acs = jnp.broadcast_to(acs_c[:, r:r + 1], (ln, ln))
            dtv = jnp.broadcast_to(dt_c[:, r:r + 1], (ln, p_h))
            acsr = acs_r[r:r + 1, :]
            last = acs[ln - 1:ln, :]
            xd = xs * dtv
            xd16 = xd.astype(BF16)
            lm = jnp.where(row >= col, jnp.exp(acs - acsr), 0.0)
            lmt = jnp.where(col >= row, jnp.exp(acsr - acs), 0.0)
            m_ls = cb * lm
            m_sl = cbt * lmt
            dm = _nt(dy16, xd16)
            dmt = _nt(xd16, dy16)
            dxd = _nn(m_sl.astype(BF16), dy16)
            dacs = _row_sums(dm * m_ls - dmt * m_sl)
            dcb = dcb + dm * lm
            dcbt = dcbt + dmt * lmt
            prev = prev_ref[0, sl, :]
            prev16 = prev.astype(BF16)
            e = jnp.exp(acs[:, :p_h])
            y_off = _nt(cm16, prev16) * e
            dacs = dacs + _row_sums(dyv * y_off)
            dyo16 = (dyv * e).astype(BF16)
            dc_acc = dc_acc + _nn(dyo16, prev16)
            dprev = _tn(dyo16, cm16)
            ds = ds_ref[sl, :]
            ds16 = ds.astype(BF16)
            decay = jnp.exp(last - acs)[:, :n_st]
            bd16 = (bm * decay).astype(BF16)
            dbd = _nn(xd16, ds16)
            dxd = dxd + _nt(bd16, ds16)
            db_acc = db_acc + dbd * decay
            tdec = _row_sums(dbd * bm) * decay
            cd = jnp.exp(last)
            dlast = (jnp.sum(tdec, axis=0, keepdims=True)
                     + jnp.sum(_row_sums(prev * ds), axis=0, keepdims=True) * cd)
            ds_ref[sl, :] = dprev + cd[:, :n_st] * ds
            dskip = hpc_ref[0, 2:3, r:r + 1]
            dxs_ref[:, sl] = dxd * dtv + dskip * dyv
            dacs = dacs - tdec + jnp.where(row == ln - 1, dlast, 0.0)
            dacs_all = jnp.where(lane_r == r, dacs[:, :r_h], dacs_all)
            ddtx_all = jnp.where(lane_r == r, _row_sums(dxd * xs)[:, :r_h], ddtx_all)
            dd_all = jnp.where(lane_r == r, _row_sums(dyv * xs)[:, :r_h], dd_all)
        dc_ref[...] = dc_acc + _nn(dcb.astype(BF16), bm16)
        db_ref[...] = db_acc + _nn(dcbt.astype(BF16), cm16)
        upper = (row <= col).astype(F32)
        dad = _ones_dot(upper, dacs_all, ones_left=True)
        ddt = dad * a_c + ddtx_all
        ddt_raw = ddt * _sigmoid(raw_c)
        ddt_ref[0] = ddt_raw
        d_bias = jnp.sum(ddt_raw, axis=0, keepdims=True)
        d_alog = jnp.sum(dad * dt_c, axis=0, keepdims=True) * a_c
        d_d = jnp.sum(dd_all, axis=0, keepdims=True)
        hg = jnp.concatenate([d_bias, d_alog, d_d], axis=0)

        @pl.when(step == 0)
        def _():
            hg_ref[0] = hg

        @pl.when(step > 0)
        def _():
            hg_ref[0] += hg

    xs, bms, cms, dtcs, dtrs, hpcs, hprs, prevs = _ssd_specs(t, di, g_n, n_st, rp, ln, r_h, True)
    bout = pl.BlockSpec((ln, n_st), lambda g, c: (nc - 1 - c, g))
    return pl.pallas_call(
        body,
        grid=(g_n, nc),
        in_specs=[xs, bms, cms, dtcs, dtrs, hpcs, hprs, prevs, xs],
        out_specs=[xs, bout, bout, dtcs, hpcs],
        out_shape=[jax.ShapeDtypeStruct((t, di), F32), jax.ShapeDtypeStruct((t, g_n * n_st), F32),
                   jax.ShapeDtypeStruct((t, g_n * n_st), F32), jax.ShapeDtypeStruct((g_n, t, r_h), F32),
                   jax.ShapeDtypeStruct((g_n, 3, r_h), F32)],
        scratch_shapes=[pltpu.VMEM((rp, n_st), F32)],
        compiler_params=_cparams(("parallel", "arbitrary")),
        name=name,
    )(xbc, xbc, xbc, dtc, dtr, hpc, hpr, prev, dy)


SB_KEYS = 256
SB_QUERIES = (1024, 512, 256)
SB_PIECES = 2


def _sb_logits(qs, kv, valid):
    z = _nt(qs, kv)
    sp = _softplus(z)
    lg = -sp if valid is None else jnp.where(valid, -sp, 0.0)
    return z - sp, lg


def _sb_iota(tq):
    diff = lax.broadcasted_iota(jnp.int32, (tq, SB_KEYS), 1) - lax.broadcasted_iota(jnp.int32, (tq, SB_KEYS), 0)
    krow = lax.broadcasted_iota(jnp.int32, (SB_KEYS, SB_KEYS), 0)
    kcol = lax.broadcasted_iota(jnp.int32, (SB_KEYS, SB_KEYS), 1)
    return diff, krow, kcol


def _sb_scale(d):
    scale = 1.0 / math.sqrt(d)
    assert math.frexp(scale)[0] == 0.5, "the scale is folded into bf16 queries: it must be a power of two"
    return scale


def _key_rows(j):
    return pl.ds(pl.multiple_of(j * SB_KEYS, SB_KEYS), SB_KEYS)


def _pairs(tiles, per_tile, one, carry):
    if per_tile % 2:
        return lax.fori_loop(0, tiles * per_tile, one, carry)
    return lax.fori_loop(0, tiles * (per_tile // 2), lambda s, cr: one(2 * s + 1, one(2 * s, cr)), carry)


def _sb_fwd(q, k, v, *, name):
    h, t, d = q.shape
    tq = _tile(t, SB_QUERIES)
    nq = t // tq
    kpq = tq // SB_KEYS
    scale = _sb_scale(d)

    def body(q_ref, k_ref, v_ref, o_ref, lt_ref):
        i = pl.program_id(1)
        qs = (q_ref[0].astype(F32) * scale).astype(BF16)
        diff, krow, kcol = _sb_iota(tq)
        later = (krow > kcol).astype(F32)

        def block(j, carry, valid):
            acc, cl = carry
            rows = _key_rows(j)
            ls, lg = _sb_logits(qs, k_ref[0, rows, :], valid)
            cs = _ones_dot(later, lg, ones_left=False, pieces=SB_PIECES)
            att = jnp.exp(ls + (cs + cl))
            if valid is not None:
                att = jnp.where(valid, att, 0.0)
            acc = acc + _nn(att.astype(BF16), v_ref[0, rows, :])
            return acc, cl + (cs[:, 0:1] + lg[:, 0:1])

        carry = (jnp.zeros((tq, d), F32), jnp.zeros((tq, 1), F32))
        for m in range(kpq - 1, -1, -1):
            carry = block(i * kpq + m, carry, diff < -m * SB_KEYS)
        nb = i * kpq
        acc, cl = _pairs(i, kpq, lambda s, cr: block(nb - 1 - s, cr, None), carry)
        o_ref[0] = acc
        lt_ref[0] = cl

    qs = pl.BlockSpec((1, tq, d), lambda hh, i: (hh, i, 0))
    ls = pl.BlockSpec((1, tq, 1), lambda hh, i: (hh, i, 0))
    ks = pl.BlockSpec((1, t, d), lambda hh, i: (hh, 0, 0))
    return pl.pallas_call(
        body,
        grid=(h, nq),
        in_specs=[qs, ks, ks],
        out_specs=[qs, ls],
        out_shape=[jax.ShapeDtypeStruct((h, t, d), F32), jax.ShapeDtypeStruct((h, t, 1), F32)],
        compiler_params=_cparams(("parallel", "arbitrary")),
        name=name,
    )(q, k, v)


def _sb_bwd(q, k, v, lt, do, *, name):
    h, t, d = q.shape
    tq = _tile(t, SB_QUERIES)
    nq = t // tq
    kpq = tq // SB_KEYS
    scale = _sb_scale(d)
    last = SB_KEYS - 1

    def body(q_ref, k_ref, v_ref, lt_ref, do_ref, dq_ref, dk_ref, dv_ref):
        i = pl.program_id(1)

        @pl.when(i == 0)
        def _():
            dk_ref[...] = jnp.zeros_like(dk_ref)
            dv_ref[...] = jnp.zeros_like(dv_ref)

        qs = (q_ref[0].astype(F32) * scale).astype(BF16)
        do16 = do_ref[0].astype(BF16)
        ltot = lt_ref[0]
        diff, krow, kcol = _sb_iota(tq)
        upto = (krow <= kcol).astype(F32)
        before = (krow < kcol).astype(F32)

        def block(j, carry, valid):
            dq, pl_sum, pg_sum = carry
            rows = _key_rows(j)
            kv = k_ref[0, rows, :]
            vv = v_ref[0, rows, :]
            ls, lg = _sb_logits(qs, kv, valid)
            pre = _ones_dot(upto, lg, ones_left=False, pieces=SB_PIECES)
            att = jnp.exp(ls + (ltot - (pre + pl_sum)))
            if valid is not None:
                att = jnp.where(valid, att, 0.0)
            g = att * _nt(do16, vv)
            gpre = _ones_dot(before, g, ones_left=False, pieces=SB_PIECES)
            sig = jnp.exp(ls)
            dz16 = (g - sig * (g + (gpre + pg_sum))).astype(BF16)
            if valid is not None:
                dz16 = jnp.where(valid, dz16, jnp.zeros_like(dz16))
            dq = dq + _nn(dz16, kv)
            dk_ref[0, rows, :] += _tn(dz16, qs)
            dv_ref[0, rows, :] += _tn(att.astype(BF16), do16)
            return dq, pl_sum + pre[:, last:], pg_sum + (gpre[:, last:] + g[:, last:])

        zero = jnp.zeros((tq, 1), F32)
        nb = i * kpq
        carry = _pairs(i, kpq, lambda j, cr: block(j, cr, None), (jnp.zeros((tq, d), F32), zero, zero))
        for m in range(kpq):
            carry = block(nb + m, carry, diff < -m * SB_KEYS)
        dq_ref[0] = carry[0] * scale

    qs = pl.BlockSpec((1, tq, d), lambda hh, i: (hh, i, 0))
    ls = pl.BlockSpec((1, tq, 1), lambda hh, i: (hh, i, 0))
    ks = pl.BlockSpec((1, t, d), lambda hh, i: (hh, 0, 0))
    full = jax.ShapeDtypeStruct((h, t, d), F32)
    return pl.pallas_call(
        body,
        grid=(h, nq),
        in_specs=[qs, ks, ks, ls, qs],
        out_specs=[qs, ks, ks],
        out_shape=[full, full, full],
        compiler_params=_cparams(("parallel", "arbitrary")),
        name=name,
    )(q, k, v, lt, do)


def _row_tile(rows, cols):
    return _tile(rows, tuple(r for r in (2048, 1024, 512, 256, 128, 64, 32, 16, 8) if r * cols * 4 <= ADAM_BLOCK_BYTES))


def _sum_leading(x, *, name):
    n, rows, cols = x.shape
    tr = _row_tile(rows, cols)

    def body(x_ref, o_ref):
        acc = x_ref[0].astype(F32)
        for q in range(1, n):
            acc = acc + x_ref[q].astype(F32)
        o_ref[...] = acc

    return pl.pallas_call(
        body,
        grid=(rows // tr,),
        in_specs=[pl.BlockSpec((n, tr, cols), lambda i: (0, i, 0))],
        out_specs=pl.BlockSpec((tr, cols), lambda i: (i, 0)),
        out_shape=jax.ShapeDtypeStruct((rows, cols), F32),
        compiler_params=_cparams(("parallel",)),
        name=name,
    )(x)


def _pair_add(g4h, recv, c, *, out_dtype, name):
    n, _, rows, cols = g4h.shape
    tr = _row_tile(rows, cols)

    def body(c_ref, g_ref, r_ref, o_ref):
        o_ref[...] = (g_ref[...] + r_ref[...]).astype(o_ref.dtype)

    blk = pl.BlockSpec((1, tr, cols), lambda q, i, c_ref: (q, i, 0))
    return pl.pallas_call(
        body,
        grid_spec=pltpu.PrefetchScalarGridSpec(
            num_scalar_prefetch=1,
            grid=(n, rows // tr),
            in_specs=[pl.BlockSpec((1, None, tr, cols), lambda q, i, c_ref: (q, c_ref[0], i, 0)), blk],
            out_specs=blk),
        out_shape=jax.ShapeDtypeStruct((n, rows, cols), out_dtype),
        compiler_params=_cparams(("parallel", "parallel")),
        name=name,
    )(c.reshape(1).astype(jnp.int32), g4h, recv)


ANY = pl.BlockSpec(memory_space=pl.ANY)


def _other_chips(x, y):
    return [(1 - x, y), (x, 1 - y), (1 - x, 1 - y)]


def _gather_chips(shard, *, name):
    def body(x_ref, o_ref, send_sems, recv_sems, local_sem):
        x, y, c = lax.axis_index("x"), lax.axis_index("y"), lax.axis_index("c")
        me = 2 * x + y
        mine = pltpu.make_async_copy(x_ref, o_ref.at[me], local_sem)
        mine.start()
        chips = _other_chips(x, y)
        sends = [pltpu.make_async_remote_copy(src_ref=x_ref, dst_ref=o_ref.at[me], send_sem=send_sems.at[q],
                                              recv_sem=recv_sems.at[q], device_id=(px, py, c), device_id_type=MESH)
                 for q, (px, py) in enumerate(chips)]
        for cp in sends:
            cp.start()
        for q, (px, py) in enumerate(chips):
            pltpu.make_async_remote_copy(src_ref=x_ref, dst_ref=o_ref.at[2 * px + py], send_sem=send_sems.at[q],
                                         recv_sem=recv_sems.at[q], device_id=(px, py, c), device_id_type=MESH).wait_recv()
        for cp in sends:
            cp.wait_send()
        mine.wait()

    return pl.pallas_call(
        body,
        in_specs=[ANY],
        out_specs=ANY,
        out_shape=jax.ShapeDtypeStruct((4,) + shard.shape, shard.dtype),
        scratch_shapes=[pltpu.SemaphoreType.DMA((3,)), pltpu.SemaphoreType.DMA((3,)), pltpu.SemaphoreType.DMA],
        compiler_params=pltpu.CompilerParams(has_side_effects=True),
        name=name,
    )(shard)


def _comm_call(body, ins, out_shapes, n_sems, name):
    n = len(ins)

    def wrapped(*refs):
        body(refs[:n], refs[n:n + len(out_shapes)], refs[-2], refs[-1])

    return pl.pallas_call(
        wrapped,
        in_specs=[ANY] * n,
        out_specs=[ANY] * len(out_shapes),
        out_shape=out_shapes,
        scratch_shapes=[pltpu.SemaphoreType.DMA((n_sems,)), pltpu.SemaphoreType.DMA((n_sems,))],
        compiler_params=pltpu.CompilerParams(has_side_effects=True),
        name=name,
    )(*ins)


def _remote(send_sems, recv_sems, q, src, dst, to):
    return pltpu.make_async_remote_copy(src_ref=src, dst_ref=dst, send_sem=send_sems.at[q], recv_sem=recv_sems.at[q],
                                        device_id=to, device_id_type=MESH)


def _scatter_chips(parts, *, name):
    def body(ins, outs, send_sems, recv_sems):
        x, y, c = lax.axis_index("x"), lax.axis_index("y"), lax.axis_index("c")
        me = 2 * x + y
        chips = _other_chips(x, y)
        sends = [_remote(send_sems, recv_sems, 3 * i + q, p.at[2 * px + py], o.at[me], (px, py, c))
                 for i, (p, o) in enumerate(zip(ins, outs)) for q, (px, py) in enumerate(chips)]
        for cp in sends:
            cp.start()
        for i, (p, o) in enumerate(zip(ins, outs)):
            for q, (px, py) in enumerate(chips):
                _remote(send_sems, recv_sems, 3 * i + q, p.at[me], o.at[2 * px + py], (px, py, c)).wait_recv()
        for cp in sends:
            cp.wait_send()

    return _comm_call(body, parts, [jax.ShapeDtypeStruct(p.shape, p.dtype) for p in parts], 3 * len(parts), name)


def _gather_chips_halves(shards, *, name):
    def body(ins, outs, send_sems, recv_sems):
        x, y, c = lax.axis_index("x"), lax.axis_index("y"), lax.axis_index("c")
        me, sibling = 2 * x + y, (x, y, 1 - c)
        chips = _other_chips(x, y)
        copy = lambda q, src, dst, to: _remote(send_sems, recv_sems, q, src, dst, to)
        sends = [copy(6 * i + q, s.at[c], o.at[me, c], (px, py, c))
                 for i, (s, o) in enumerate(zip(ins, outs)) for q, (px, py) in enumerate(chips)]
        for cp in sends:
            cp.start()
        passed = []
        for i, (s, o) in enumerate(zip(ins, outs)):
            for q, (px, py) in enumerate(chips):
                slot = o.at[2 * px + py, c]
                copy(6 * i + q, s.at[c], slot, (px, py, c)).wait_recv()
                passed.append(copy(6 * i + 3 + q, slot, slot, sibling))
                passed[-1].start()
        for i, (s, o) in enumerate(zip(ins, outs)):
            for q, (px, py) in enumerate(chips):
                copy(6 * i + 3 + q, s.at[1 - c], o.at[2 * px + py, 1 - c], sibling).wait_recv()
        for cp in sends + passed:
            cp.wait_send()

    return _comm_call(body, shards, [jax.ShapeDtypeStruct((N_CHIPS,) + s.shape, s.dtype) for s in shards],
                      6 * len(shards), name)


def _swap_other_half(gs, *, name):
    def body(ins, outs, send_sems, recv_sems):
        x, y, c = lax.axis_index("x"), lax.axis_index("y"), lax.axis_index("c")
        copies = [_remote(send_sems, recv_sems, i, g.at[pl.ds(0, g.shape[0]), 1 - c], o, (x, y, 1 - c))
                  for i, (g, o) in enumerate(zip(ins, outs))]
        for cp in copies:
            cp.start()
        for cp in copies:
            cp.wait()

    return _comm_call(body, gs, [jax.ShapeDtypeStruct((g.shape[0],) + g.shape[2:], g.dtype) for g in gs], len(gs), name)


def _join_halves(halves, *, name):
    def body(ins, outs, send_sems, recv_sems):
        x, y, c = lax.axis_index("x"), lax.axis_index("y"), lax.axis_index("c")
        sibling = (x, y, 1 - c)
        sends = [_remote(send_sems, recv_sems, i, h, o.at[c], sibling) for i, (h, o) in enumerate(zip(ins, outs))]
        for cp in sends:
            cp.start()
        for i, (h, o) in enumerate(zip(ins, outs)):
            _remote(send_sems, recv_sems, i, h, o.at[1 - c], sibling).wait_recv()
        for cp in sends:
            cp.wait_send()

    return _comm_call(body, halves, [jax.ShapeDtypeStruct((2,) + h.shape, h.dtype) for h in halves], len(halves), name)


def _gather_all(v, *, name):
    def body(v_ref, o_ref, send_sems, recv_sems, local_sem):
        x, y, c = lax.axis_index("x"), lax.axis_index("y"), lax.axis_index("c")
        me = 4 * x + 2 * y + c
        mine = pltpu.make_async_copy(v_ref, o_ref.at[me], local_sem)
        mine.start()
        peers = [(x ^ (q >> 2 & 1), y ^ (q >> 1 & 1), c ^ (q & 1)) for q in range(1, 8)]
        sends = [pltpu.make_async_remote_copy(src_ref=v_ref, dst_ref=o_ref.at[me], send_sem=send_sems.at[q],
                                              recv_sem=recv_sems.at[q], device_id=peer, device_id_type=MESH)
                 for q, peer in enumerate(peers)]
        for cp in sends:
            cp.start()
        for q, (px, py, pc) in enumerate(peers):
            pltpu.make_async_remote_copy(src_ref=v_ref, dst_ref=o_ref.at[4 * px + 2 * py + pc], send_sem=send_sems.at[q],
                                         recv_sem=recv_sems.at[q], device_id=(px, py, pc), device_id_type=MESH).wait_recv()
        for cp in sends:
            cp.wait_send()
        mine.wait()

    return pl.pallas_call(
        body,
        in_specs=[ANY],
        out_specs=ANY,
        out_shape=jax.ShapeDtypeStruct((8,) + v.shape, v.dtype),
        scratch_shapes=[pltpu.SemaphoreType.DMA((7,)), pltpu.SemaphoreType.DMA((7,)), pltpu.SemaphoreType.DMA],
        compiler_params=pltpu.CompilerParams(has_side_effects=True),
        name=name,
    )(v)


WEIGHTS = ['ssm_norm_w', 'ssm_in_w', 'ssm_conv_w', 'ssm_conv_b', 'ssm_dt_bias', 'ssm_a_log', 'ssm_d',
           'ssm_gate_norm_w', 'ssm_out_w', 'kv_norm_w', 'w_k', 'w_v', 'attn_norm_w', 'w_q', 'w_o',
           'ffn_norm_w', 'ffn_up_w', 'ffn_conv_w', 'ffn_conv_b', 'ffn_down_w', 'final_norm_w']
SHARD_AXIS = {'ssm_norm_w': 1, 'ssm_in_w': 2, 'ssm_conv_w': 2, 'ssm_conv_b': 1, 'ssm_gate_norm_w': 1,
              'ssm_out_w': 1, 'w_k': 0, 'w_v': 0, 'w_q': 1, 'w_o': 1, 'ffn_up_w': 2, 'ffn_conv_w': 2,
              'ffn_down_w': 1}
BIG = ['ssm_in_w', 'ssm_out_w', 'w_k', 'w_v', 'w_q', 'w_o', 'ffn_up_w', 'ffn_down_w']
SMALL = [n for n in WEIGHTS if n in SHARD_AXIS and n not in BIG]
REPLICATED = [n for n in WEIGHTS if n not in SHARD_AXIS]
STACKED = ['ffn_up_w', 'ffn_down_w']
N_CHIPS = 4


PACK_ROWS = 16


def _piece_rows(n):
    return -(-n // (PACK_ROWS * LANES)) * PACK_ROWS


def _pack(arrs, dtype, row_mult):
    lead = arrs[0].shape[:-1]
    pieces, total = [], 0
    for a in arrs:
        n = a.shape[-1]
        rows = _piece_rows(n)
        a = a.astype(dtype)
        if rows * LANES != n:
            a = jnp.pad(a, [(0, 0)] * len(lead) + [(0, rows * LANES - n)])
        pieces.append(a.reshape(lead + (rows, LANES)))
        total += rows
    extra = -total % row_mult
    if extra:
        pieces.append(jnp.zeros(lead + (extra, LANES), dtype))
    return jnp.concatenate(pieces, axis=len(lead))


def _unpack(buf, shapes):
    lead = buf.shape[:-2]
    out, off = [], 0
    for shp in shapes:
        n = math.prod(shp)
        rows = _piece_rows(n)
        piece = lax.slice_in_dim(buf, off, off + rows, axis=len(lead)).reshape(lead + (rows * LANES,))
        out.append(piece[..., :n].reshape(lead + tuple(shp)))
        off += rows
    return out


def _set_slot(buf, piece, index):
    return lax.dynamic_update_slice_in_dim(buf, piece[None], index, axis=0)


def _from_shards(stacked, axis):
    return jnp.concatenate([stacked[j] for j in range(N_CHIPS)], axis=axis)


def _heads(a, h):
    t = a.shape[0]
    return a.reshape(t, h, a.shape[1] // h).transpose(1, 0, 2)


def _unheads(a):
    h, t, d = a.shape
    return a.transpose(1, 0, 2).reshape(t, h * d)


def _ffn_fwd(h, norm_w, w_up, conv_w, conv_b, w_down, tag):
    u = _rmsnorm_fwd(h, norm_w, name=f"ffn{tag}_norm")
    hid = _matmul(u, w_up, name=f"ffn{tag}_up")
    act = _conv_glu_fwd(hid, conv_w, conv_b, name=f"ffn{tag}_glu")
    out = _matmul(act, w_down, add=h, name=f"ffn{tag}_down")
    return out, (u, hid, act)


def _ffn_bwd(h, saved, dout, norm_w, w_up, conv_w, conv_b, w_down, tag):
    u, hid, act = saved
    dact = _matmul(dout, w_down, tb=True, name=f"ffn{tag}_down_dx")
    dw_down = _matmul(act, dout, ta=True, name=f"ffn{tag}_down_dw")
    dhid, dwg, dwv, dbg, dbv = _conv_glu_bwd(hid, conv_w, conv_b, dact, name=f"ffn{tag}_glu_bwd")
    du = _matmul(dhid, w_up, tb=True, name=f"ffn{tag}_up_dx")
    dw_up = _matmul(u, dhid, ta=True, out_parts=N_CHIPS, name=f"ffn{tag}_up_dw")
    dh, (dnorm,) = _rmsnorm_bwd(h, [(du, norm_w)], dout, name=f"ffn{tag}_norm_bwd")
    return dh, dict(norm=dnorm[0], up=dw_up, conv_w=jnp.concatenate([dwg, dwv], axis=1),
                    conv_b=jnp.concatenate([dbg, dbv], axis=1)[0], down=dw_down)


def _step(x, target, w):
    t = x.shape[0]
    g_n, heads = SSM_GROUPS, SSM_HEADS
    r_h = heads // g_n
    di = D_INNER
    zx_cols = di + CONV_DIM
    w_in = w['ssm_in_w'][0]
    w_zx = w_in[:, :zx_cols]
    w_dt = jnp.pad(w_in[:, zx_cols:], ((0, 0), (0, LANES - heads)))
    conv_w, conv_b = w['ssm_conv_w'][0], w['ssm_conv_b'][0]
    hp = jnp.stack([w['ssm_dt_bias'][0], w['ssm_a_log'][0], w['ssm_d'][0]], axis=0).reshape(3, g_n, r_h)
    hpc, hpr = hp.transpose(1, 0, 2), hp.transpose(1, 2, 0)
    w_out = w['ssm_out_w'][0]
    w_q, w_o = w['w_q'][0], w['w_o'][0]

    h0 = x
    u0 = _rmsnorm_fwd(h0, w['ssm_norm_w'][0], name="ssm_norm")
    zx = _matmul(u0, w_zx, name="ssm_in_zx")
    dt_raw = _matmul(u0, w_dt, name="ssm_in_dt")[:, :heads]
    dtg = dt_raw.reshape(t, g_n, r_h)
    dtc, dtr = dtg.transpose(1, 0, 2), dtg.transpose(1, 2, 0)
    xbc = _conv_silu_fwd(zx, conv_w, conv_b, x_off=di, name="ssm_conv")
    y, prev = _ssd_fwd(xbc, dtc, dtr, hpc, hpr, name="ssd_fwd")
    yn = _gate_norm_fwd(y, zx, w['ssm_gate_norm_w'][0], name="ssm_gate_norm")
    h1 = _matmul(yn, w_out, add=h0, name="ssm_out")
    h2, ffn0 = _ffn_fwd(h1, w['ffn_norm_w'][0], w['ffn_up_w'][0], w['ffn_conv_w'][0], w['ffn_conv_b'][0],
                        w['ffn_down_w'][0], 0)
    hk = _rmsnorm_fwd(h2, w['kv_norm_w'], name="kv_norm")
    qn = _rmsnorm_fwd(h2, w['attn_norm_w'][0], name="attn_norm")
    k2 = _matmul(hk, w['w_k'], out_dtype=BF16, name="attn_k")
    v2 = _matmul(hk, w['w_v'], out_dtype=BF16, name="attn_v")
    q2 = _matmul(qn, w_q, out_dtype=BF16, name="attn_q")
    qh, kh, vh = _heads(q2, SB_HEADS), _heads(k2, SB_HEADS), _heads(v2, SB_HEADS)
    oh, lt = _sb_fwd(qh, kh, vh, name="sb_fwd")
    o2 = _unheads(oh)
    h3 = _matmul(o2, w_o, add=h2, name="attn_o")
    h4, ffn1 = _ffn_fwd(h3, w['ffn_norm_w'][1], w['ffn_up_w'][1], w['ffn_conv_w'][1], w['ffn_conv_b'][1],
                        w['ffn_down_w'][1], 1)
    loss_p, dh4, d_final = _loss_head(h4, w['final_norm_w'], target, name="loss_head")

    dh3, g1 = _ffn_bwd(h3, ffn1, dh4, w['ffn_norm_w'][1], w['ffn_up_w'][1], w['ffn_conv_w'][1],
                       w['ffn_conv_b'][1], w['ffn_down_w'][1], 1)
    do2 = _matmul(dh3, w_o, tb=True, name="attn_o_dx")
    dw_o = _matmul(o2, dh3, ta=True, name="attn_o_dw")
    dqh, dkh, dvh = _sb_bwd(qh, kh, vh, lt, _heads(do2, SB_HEADS), name="sb_bwd")
    dq2, dk2, dv2 = _unheads(dqh), _unheads(dkh), _unheads(dvh)
    dqn = _matmul(dq2, w_q, tb=True, name="attn_q_dx")
    dw_q = _matmul(qn, dq2, ta=True, name="attn_q_dw")
    dhk = _matmul(dk2, w['w_k'], tb=True, name="attn_k_dx")
    dhk = _matmul(dv2, w['w_v'], tb=True, add=dhk, name="attn_v_dx")
    dw_k = _matmul(hk, dk2, ta=True, name="attn_k_dw")
    dw_v = _matmul(hk, dv2, ta=True, name="attn_v_dw")
    dh2, (d_attn_norm, d_kv_norm) = _rmsnorm_bwd(h2, [(dqn, w['attn_norm_w'][0]), (dhk, w['kv_norm_w'])], dh3,
                                                 name="attn_norms_bwd")
    dh1, g0 = _ffn_bwd(h1, ffn0, dh2, w['ffn_norm_w'][0], w['ffn_up_w'][0], w['ffn_conv_w'][0],
                       w['ffn_conv_b'][0], w['ffn_down_w'][0], 0)
    dyn = _matmul(dh1, w_out, tb=True, name="ssm_out_dx")
    dw_out = _matmul(yn, dh1, ta=True, name="ssm_out_dw")
    dy, dz, d_gate = _gate_norm_bwd(y, zx, w['ssm_gate_norm_w'][0], dyn, name="ssm_gate_norm_bwd")
    dxs, dbm, dcm, ddt_g, hg = _ssd_bwd(xbc, dtc, dtr, hpc, hpr, prev, dy, name="ssd_bwd")
    dxbc = jnp.concatenate([dxs, dbm, dcm], axis=1)
    dxbc_pre, d_conv_w, d_conv_b = _conv_silu_bwd(zx, conv_w, conv_b, dxbc, x_off=di, name="ssm_conv_bwd")
    dzx = jnp.concatenate([dz, dxbc_pre], axis=1)
    ddt = jnp.pad(ddt_g.transpose(1, 0, 2).reshape(t, heads), ((0, 0), (0, LANES - heads)))
    du0 = _matmul(dzx, w_zx, tb=True, name="ssm_in_zx_dx")
    du0 = _matmul(ddt, w_dt, tb=True, add=du0, name="ssm_in_dt_dx")
    dw_in = jnp.concatenate([_matmul(u0, dzx, ta=True, name="ssm_in_zx_dw"),
                             _matmul(u0, ddt, ta=True, name="ssm_in_dt_dw")[:, :heads]], axis=1)
    dx, (d_ssm_norm,) = _rmsnorm_bwd(h0, [(du0, w['ssm_norm_w'][0])], dh1, name="ssm_norm_bwd")

    hgr = hg.transpose(1, 0, 2).reshape(3, heads)
    grads = {
        'ssm_norm_w': d_ssm_norm, 'ssm_in_w': dw_in, 'ssm_conv_w': d_conv_w[None], 'ssm_conv_b': d_conv_b,
        'ssm_dt_bias': hgr[0:1], 'ssm_a_log': hgr[1:2], 'ssm_d': hgr[2:3], 'ssm_gate_norm_w': d_gate,
        'ssm_out_w': dw_out, 'kv_norm_w': d_kv_norm[0], 'w_k': dw_k, 'w_v': dw_v, 'attn_norm_w': d_attn_norm,
        'w_q': dw_q, 'w_o': dw_o, 'ffn_norm_w': jnp.stack([g0['norm'], g1['norm']]),
        'ffn_up_w': [g0['up'], g1['up']], 'ffn_conv_w': jnp.stack([g0['conv_w'], g1['conv_w']]),
        'ffn_conv_b': jnp.stack([g0['conv_b'], g1['conv_b']]), 'ffn_down_w': [g0['down'], g1['down']],
        'final_norm_w': d_final[0],
    }
    return loss_p, dx, grads


def kernel(x, ssm_norm_w, ssm_in_w, ssm_conv_w, ssm_conv_b, ssm_dt_bias, ssm_a_log, ssm_d, ssm_gate_norm_w, ssm_out_w, kv_norm_w, w_k, w_v, attn_norm_w, w_q, w_o, ffn_norm_w, ffn_up_w, ffn_conv_w, ffn_conv_b, ffn_down_w, final_norm_w, loss_target, m_ssm_norm_w, m_ssm_in_w, m_ssm_conv_w, m_ssm_conv_b, m_ssm_dt_bias, m_ssm_a_log, m_ssm_d, m_ssm_gate_norm_w, m_ssm_out_w, m_kv_norm_w, m_w_k, m_w_v, m_attn_norm_w, m_w_q, m_w_o, m_ffn_norm_w, m_ffn_up_w, m_ffn_conv_w, m_ffn_conv_b, m_ffn_down_w, m_final_norm_w, v_ssm_norm_w, v_ssm_in_w, v_ssm_conv_w, v_ssm_conv_b, v_ssm_dt_bias, v_ssm_a_log, v_ssm_d, v_ssm_gate_norm_w, v_ssm_out_w, v_kv_norm_w, v_w_k, v_w_v, v_attn_norm_w, v_w_q, v_w_o, v_ffn_norm_w, v_ffn_up_w, v_ffn_conv_w, v_ffn_conv_b, v_ffn_down_w, v_final_norm_w):
    args = (ssm_norm_w, ssm_in_w, ssm_conv_w, ssm_conv_b, ssm_dt_bias, ssm_a_log, ssm_d, ssm_gate_norm_w, ssm_out_w, kv_norm_w, w_k, w_v, attn_norm_w, w_q, w_o, ffn_norm_w, ffn_up_w, ffn_conv_w, ffn_conv_b, ffn_down_w, final_norm_w)
    moms = (m_ssm_norm_w, m_ssm_in_w, m_ssm_conv_w, m_ssm_conv_b, m_ssm_dt_bias, m_ssm_a_log, m_ssm_d, m_ssm_gate_norm_w, m_ssm_out_w, m_kv_norm_w, m_w_k, m_w_v, m_attn_norm_w, m_w_q, m_w_o, m_ffn_norm_w, m_ffn_up_w, m_ffn_conv_w, m_ffn_conv_b, m_ffn_down_w, m_final_norm_w)
    vels = (v_ssm_norm_w, v_ssm_in_w, v_ssm_conv_w, v_ssm_conv_b, v_ssm_dt_bias, v_ssm_a_log, v_ssm_d, v_ssm_gate_norm_w, v_ssm_out_w, v_kv_norm_w, v_w_k, v_w_v, v_attn_norm_w, v_w_q, v_w_o, v_ffn_norm_w, v_ffn_up_w, v_ffn_conv_w, v_ffn_conv_b, v_ffn_down_w, v_final_norm_w)
    local = dict(zip(WEIGHTS, args))
    m_in = dict(zip(WEIGHTS, moms))
    v_in = dict(zip(WEIGHTS, vels))
    c = lax.axis_index("c")
    chip = 2 * lax.axis_index("x") + lax.axis_index("y")

    pieces = []
    for n in BIG:
        blk = local[n]
        if n in STACKED:
            pieces += [(n, l, blk[l]) for l in range(blk.shape[0])]
        else:
            pieces.append((n, None, blk.reshape(blk.shape[-2:])))
    shards16 = [p.astype(BF16).reshape(2, p.shape[0] // 2, p.shape[1]) for _, _, p in pieces]
    gathered = _gather_chips_halves(shards16, name="gather_big")
    full = {n: local[n] for n in REPLICATED}
    for n in STACKED:
        full[n] = []
    for (n, l, p), s16, g in zip(pieces, shards16, gathered):
        by_chip = _set_slot(g, s16, chip).reshape((N_CHIPS,) + p.shape)
        if n == 'ssm_in_w':
            w_full = by_chip.transpose(1, 0, 2).reshape(p.shape[0], N_CHIPS * p.shape[1])
        elif n == 'ffn_up_w':
            w_full = by_chip
        else:
            w_full = by_chip.reshape(N_CHIPS * p.shape[0], p.shape[1])
        if n in STACKED:
            full[n].append(w_full)
        else:
            full[n] = w_full.reshape(local[n].shape[:-2] + w_full.shape)
    small32 = _gather_chips(_pack([local[n].reshape(-1) for n in SMALL], F32, 8), name="gather_small")
    for n, st in zip(SMALL, _unpack(small32, [local[n].shape for n in SMALL])):
        full[n] = _from_shards(st, SHARD_AXIS[n])

    loss_p, dx, grads = _step(x[0], loss_target[0], full)

    gs = []
    for n, l, p in pieces:
        g = grads[n] if l is None else grads[n][l]
        if n == 'ssm_in_w':
            g = g.reshape(p.shape[0], N_CHIPS, p.shape[1]).transpose(1, 0, 2)
        gs.append(g.reshape(N_CHIPS, 2, p.shape[0] // 2, p.shape[1]))
    recv = _swap_other_half(gs, name="rs_pair_swap")
    pairs = [_pair_add(g, r, c, out_dtype=BF16, name=f"rs_pair_add_{i}") for i, (g, r) in enumerate(zip(gs, recv))]
    scattered = _scatter_chips(pairs, name="rs_chip_scatter")
    halves = [_sum_leading(_set_slot(s, lax.dynamic_index_in_dim(pr, chip, axis=0, keepdims=False), chip),
                           name=f"rs_chip_sum_{i}") for i, (s, pr) in enumerate(zip(scattered, pairs))]
    joined = _join_halves(halves, name="rs_half_join")
    gshard = {n: [] for n in STACKED}
    for (n, l, p), h, j in zip(pieces, halves, joined):
        g = _set_slot(j, h, c).reshape(p.shape)
        if n in STACKED:
            gshard[n].append(g)
        else:
            gshard[n] = g.reshape(local[n].shape)

    small = SMALL + REPLICATED
    rep = _pack([loss_p.reshape(-1)] + [grads[n].reshape(-1) for n in small], F32, 8)
    tot = _sum_leading(_gather_all(rep, name="ar_gather"), name="ar_sum")
    parts = _unpack(tot, [(LANES,)] + [grads[n].shape for n in small])
    loss = jnp.sum(parts[0])
    for n, g in zip(small, parts[1:]):
        if n in SHARD_AXIS:
            size = local[n].shape[SHARD_AXIS[n]]
            g = lax.dynamic_slice_in_dim(g, chip * size, size, axis=SHARD_AXIS[n])
        gshard[n] = g

    grads_out, deltas, new_m, new_v = [], [], [], []
    for n in WEIGHTS:
        if n in STACKED:
            g, d, nm, nv = _adamw_layers(local[n], gshard[n], m_in[n], v_in[n], name=f"adamw_{n}")
        else:
            g = gshard[n]
            d, nm, nv = _adamw(local[n], g, m_in[n], v_in[n], name=f"adamw_{n}")
        grads_out.append(g)
        deltas.append(d)
        new_m.append(nm)
        new_v.append(nv)
    return (loss, dx[None], *grads_out, *deltas, *new_m, *new_v)
```

```python
import math

import jax
import jax.numpy as jnp
from jax import lax
from jax.experimental import pallas as pl
from jax.experimental.pallas import tpu as pltpu

D_MODEL = 1024
D_INNER = 2048
SSM_HEAD_DIM = 64
SSM_HEADS = 32
SSM_GROUPS = 4
SSM_STATE = 128
SSM_CONV = 4
SSM_CHUNK = 128
GN = SSM_GROUPS * SSM_STATE
CONV_DIM = D_INNER + 2 * GN
SB_HEADS = 16
SB_HEAD_DIM = 64
D_FF = 2816
FFN_CONV = 3
EPS = 1e-6
ADAM_LR = 0.001
ADAM_B1 = 0.9
ADAM_B2 = 0.999
ADAM_EPS = 1e-08
ADAM_WD = 0.01
ADAM_STEP = 10

LANES = 128
SUBLANES = 8
VMEM_LIMIT = 48 * 1024 * 1024
ADAM_BLOCK_BYTES = 1 << 20
F32 = jnp.float32
BF16 = jnp.bfloat16
MESH = pl.DeviceIdType.MESH


def _cparams(sem=None):
    return pltpu.CompilerParams(dimension_semantics=sem, vmem_limit_bytes=VMEM_LIMIT)


def _tile(n, cands):
    for c in cands:
        if n % c == 0:
            return c
    return n


def _nt(a, b):
    return lax.dot_general(a, b, (((1,), (1,)), ((), ())), preferred_element_type=F32)


def _tn(a, b):
    return lax.dot_general(a, b, (((0,), (0,)), ((), ())), preferred_element_type=F32)


def _nn(a, b):
    return jnp.dot(a, b, preferred_element_type=F32)


def _split(x, pieces):
    out = []
    for _ in range(pieces - 1):
        h = x.astype(BF16)
        out.append(h)
        x = x - h.astype(F32)
    out.append(x.astype(BF16))
    return out


def _ones_dot(ones, x, *, ones_left, pieces=3):
    o16 = ones.astype(BF16)
    acc = None
    for piece in _split(x, pieces):
        term = _nn(o16, piece) if ones_left else _nn(piece, o16)
        acc = term if acc is None else acc + term
    return acc


def _row_sums(x, pieces=3):
    return _ones_dot(jnp.ones((x.shape[1], LANES), F32), x, ones_left=False, pieces=pieces)


def _softplus(x):
    return jnp.maximum(x, 0.0) + jnp.log(1.0 + jnp.exp(-jnp.abs(x)))


def _sigmoid(x):
    e = jnp.exp(-jnp.abs(x))
    r = 1.0 / (1.0 + e)
    return jnp.where(x >= 0, r, e * r)


MM_TILE_MAX = 1408
MM_VMEM_BUDGET = 40 * 1024 * 1024


def _divisors(n, cap):
    out = [d for d in range(min(cap, n) // LANES * LANES, 0, -LANES) if n % d == 0]
    return out or [n]


def _mm_tiles(m, n, k, a_bytes, b_bytes, o_bytes, add_bytes):
    best = None
    for tm in _divisors(m, MM_TILE_MAX):
        for tn in _divisors(n, MM_TILE_MAX):
            for tk in _divisors(k, MM_TILE_MAX):
                vmem = 2 * (tm * tk * a_bytes + tk * tn * b_bytes + tm * tn * (o_bytes + add_bytes)) + tm * tn * 4
                if vmem > MM_VMEM_BUDGET:
                    continue
                score = (tm * tn * tk, tm * tn)
                if best is None or score > best[0]:
                    best = (score, (tm, tn, tk))
    return best[1]


def _matmul(a, b, *, ta=False, tb=False, add=None, out_dtype=F32, out_parts=1, name):
    a_parts = a.shape[0] if a.ndim == 3 else 1
    b_parts = b.shape[0] if b.ndim == 3 else 1
    assert not (ta and a_parts > 1)
    a2, b2 = a.shape[-2:], b.shape[-2:]
    m, k = (a2[1], a2[0]) if ta else (a2[0], a2[1] * a_parts)
    n, kb = (b2[0], b2[1] * b_parts) if tb else (b2[1] * b_parts, b2[0])
    assert kb == k, (a.shape, b.shape)
    n_unit = math.gcd(n // out_parts, n if tb else b2[1])
    k_unit = math.gcd(k // a_parts, b2[1] if tb else k)
    tm, tn, tk = _mm_tiles(m, n_unit, k_unit, a.dtype.itemsize, b.dtype.itemsize, jnp.dtype(out_dtype).itemsize,
                           0 if add is None else add.dtype.itemsize)
    nk = k // tk
    ka, kbp = (k // a_parts) // tk, (k // b_parts) // tk
    nb, no = (n // b_parts) // tn, (n // out_parts) // tn

    def body(*refs):
        if add is None:
            a_ref, b_ref, o_ref = refs[:3]
            add_ref = None
        else:
            a_ref, b_ref, add_ref, o_ref = refs[:4]
        kk = pl.program_id(2)
        dn = (((0 if ta else 1,), (1 if tb else 0,)), ((), ()))
        prod = lax.dot_general(a_ref[...].astype(BF16), b_ref[...].astype(BF16), dn, preferred_element_type=F32)

        def finish(r):
            if add_ref is not None:
                r = r + add_ref[...].astype(F32)
            o_ref[...] = r.astype(o_ref.dtype)

        if nk == 1:
            finish(prod)
            return
        acc_ref = refs[-1]

        @pl.when(kk == 0)
        def _():
            acc_ref[...] = prod

        @pl.when(jnp.logical_and(kk > 0, kk < nk - 1))
        def _():
            acc_ref[...] += prod

        @pl.when(kk == nk - 1)
        def _():
            finish(acc_ref[...] + prod)

    if ta:
        a_spec = pl.BlockSpec((tk, tm), lambda i, j, kk: (kk, i))
    elif a_parts > 1:
        a_spec = pl.BlockSpec((None, tm, tk), lambda i, j, kk: (kk // ka, i, kk % ka))
    else:
        a_spec = pl.BlockSpec((tm, tk), lambda i, j, kk: (i, kk))
    if b_parts == 1:
        b_spec = pl.BlockSpec((tn, tk), lambda i, j, kk: (j, kk)) if tb else pl.BlockSpec((tk, tn), lambda i, j, kk: (kk, j))
    elif tb:
        b_spec = pl.BlockSpec((None, tn, tk), lambda i, j, kk: (kk // kbp, j, kk % kbp))
    else:
        b_spec = pl.BlockSpec((None, tk, tn), lambda i, j, kk: (j // nb, kk, j % nb))
    if out_parts > 1:
        o_spec = pl.BlockSpec((None, tm, tn), lambda i, j, kk: (j // no, i, j % no))
        o_shape = jax.ShapeDtypeStruct((out_parts, m, n // out_parts), out_dtype)
    else:
        o_spec = pl.BlockSpec((tm, tn), lambda i, j, kk: (i, j))
        o_shape = jax.ShapeDtypeStruct((m, n), out_dtype)
    in_specs = [a_spec, b_spec]
    args = [a, b]
    if add is not None:
        in_specs.append(pl.BlockSpec((tm, tn), lambda i, j, kk: (i, j)))
        args.append(add)
    return pl.pallas_call(
        body,
        grid=(m // tm, n // tn, nk),
        in_specs=in_specs,
        out_specs=o_spec,
        out_shape=o_shape,
        scratch_shapes=[pltpu.VMEM((tm, tn), F32)] if nk > 1 else [],
        compiler_params=_cparams(("parallel", "parallel", "arbitrary")),
        name=name,
    )(*args)


def _rmsnorm_fwd(x, w, *, name):
    t, d = x.shape
    tb = _tile(t, (512, 256, 128))

    def body(x_ref, w_ref, o_ref):
        xv = x_ref[...]
        r = lax.rsqrt(jnp.mean(xv * xv, axis=-1, keepdims=True) + EPS)
        o_ref[...] = (xv * r * w_ref[...]).astype(o_ref.dtype)

    return pl.pallas_call(
        body,
        grid=(t // tb,),
        in_specs=[pl.BlockSpec((tb, d), lambda i: (i, 0)), pl.BlockSpec((1, d), lambda i: (0, 0))],
        out_specs=pl.BlockSpec((tb, d), lambda i: (i, 0)),
        out_shape=jax.ShapeDtypeStruct((t, d), BF16),
        compiler_params=_cparams(("parallel",)),
        name=name,
    )(x, w.reshape(1, d))


def _rmsnorm_bwd(x, dys, dres, *, name):
    t, d = x.shape
    tb = _tile(t, (256, 128))
    nn = len(dys)
    has_res = dres is not None

    def body(*refs):
        x_ref = refs[0]
        dy_refs = refs[1:1 + nn]
        w_refs = refs[1 + nn:1 + 2 * nn]
        pos = 1 + 2 * nn
        res_ref = refs[pos] if has_res else None
        pos += 1 if has_res else 0
        dx_ref = refs[pos]
        dw_refs = refs[pos + 1:pos + 1 + nn]
        i = pl.program_id(0)
        xv = x_ref[...]
        r = lax.rsqrt(jnp.mean(xv * xv, axis=-1, keepdims=True) + EPS)
        xn = xv * r
        dx = res_ref[...] if has_res else jnp.zeros_like(xv)
        for q in range(nn):
            dy = dy_refs[q][...].astype(F32)
            g = dy * w_refs[q][...]
            dx = dx + r * (g - xn * jnp.mean(g * xn, axis=-1, keepdims=True))
            dwp = jnp.sum(dy * xn, axis=0, keepdims=True)

            @pl.when(i == 0)
            def _(q=q, dwp=dwp):
                dw_refs[q][...] = dwp

            @pl.when(i > 0)
            def _(q=q, dwp=dwp):
                dw_refs[q][...] += dwp
        dx_ref[...] = dx

    row = pl.BlockSpec((tb, d), lambda i: (i, 0))
    vec = pl.BlockSpec((1, d), lambda i: (0, 0))
    in_specs = [row] + [row] * nn + [vec] * nn + ([row] if has_res else [])
    args = [x] + [p[0] for p in dys] + [p[1].reshape(1, d) for p in dys] + ([dres] if has_res else [])
    outs = pl.pallas_call(
        body,
        grid=(t // tb,),
        in_specs=in_specs,
        out_specs=[row] + [vec] * nn,
        out_shape=[jax.ShapeDtypeStruct((t, d), F32)] + [jax.ShapeDtypeStruct((1, d), F32)] * nn,
        compiler_params=_cparams(("arbitrary",)),
        name=name,
    )(*args)
    return outs[0], list(outs[1:])


def _loss_head(x, w, target, *, name):
    t, d = x.shape
    tb = _tile(t, (256, 128))

    def body(x_ref, w_ref, t_ref, loss_ref, dx_ref, dw_ref):
        i = pl.program_id(0)
        xv = x_ref[...]
        wv = w_ref[...]
        r = lax.rsqrt(jnp.mean(xv * xv, axis=-1, keepdims=True) + EPS)
        xn = xv * r
        e = xn * wv - t_ref[...]
        lp = 0.5 * jnp.sum(jnp.mean(e * e, axis=-1, keepdims=True), axis=0, keepdims=True)
        dy = e * (1.0 / d)
        g = dy * wv
        dx_ref[...] = r * (g - xn * jnp.mean(g * xn, axis=-1, keepdims=True))
        dwp = jnp.sum(dy * xn, axis=0, keepdims=True)
        lpv = jnp.broadcast_to(lp, (1, LANES)) * (1.0 / LANES)

        @pl.when(i == 0)
        def _():
            dw_ref[...] = dwp
            loss_ref[...] = lpv

        @pl.when(i > 0)
        def _():
            dw_ref[...] += dwp
            loss_ref[...] += lpv

    row = pl.BlockSpec((tb, d), lambda i: (i, 0))
    vec = pl.BlockSpec((1, d), lambda i: (0, 0))
    return pl.pallas_call(
        body,
        grid=(t // tb,),
        in_specs=[row, vec, row],
        out_specs=[pl.BlockSpec((1, LANES), lambda i: (0, 0)), row, vec],
        out_shape=[jax.ShapeDtypeStruct((1, LANES), F32), jax.ShapeDtypeStruct((t, d), F32),
                   jax.ShapeDtypeStruct((1, d), F32)],
        compiler_params=_cparams(("arbitrary",)),
        name=name,
    )(x, w.reshape(1, d), target)


ROW_CHUNK = 64
PAD = SUBLANES


def _shifted(pad_ref, r0, rows, back):
    return pad_ref[pl.ds(PAD + r0 - back, rows), :]


def _conv_taps(pad_ref, w_ref, r0, rows, kw):
    acc = None
    for j in range(kw):
        term = _shifted(pad_ref, r0, rows, kw - 1 - j) * w_ref[j:j + 1, :]
        acc = term if acc is None else acc + term
    return acc


def _fill_pad(pad_ref, x_ref, t):
    pad_ref[0:PAD, :] = jnp.zeros((PAD, pad_ref.shape[1]), F32)
    pad_ref[pl.ds(PAD + t, PAD), :] = jnp.zeros((PAD, pad_ref.shape[1]), F32)
    pad_ref[pl.ds(PAD, t), :] = x_ref[...].astype(F32)


def _conv_silu_fwd(x, w, b, *, x_off=0, name):
    t = x.shape[0]
    kw, c = w.shape
    cw = _tile(math.gcd(c, x_off) if x_off else c, (256, 128))
    ob = x_off // cw
    rc = _tile(t, (ROW_CHUNK,))

    def body(x_ref, w_ref, b_ref, o_ref, pad_ref):
        _fill_pad(pad_ref, x_ref, t)
        for r0 in range(0, t, rc):
            pre = _conv_taps(pad_ref, w_ref, r0, rc, kw) + b_ref[...]
            o_ref[pl.ds(r0, rc), :] = pre * _sigmoid(pre)

    strip = pl.BlockSpec((t, cw), lambda i: (0, i))
    return pl.pallas_call(
        body,
        grid=(c // cw,),
        in_specs=[pl.BlockSpec((t, cw), lambda i: (0, i + ob)), pl.BlockSpec((kw, cw), lambda i: (0, i)),
                  pl.BlockSpec((1, cw), lambda i: (0, i))],
        out_specs=strip,
        out_shape=jax.ShapeDtypeStruct((t, c), F32),
        scratch_shapes=[pltpu.VMEM((t + 2 * PAD, cw), F32)],
        compiler_params=_cparams(("parallel",)),
        name=name,
    )(x, w, b.reshape(1, c))


def _conv_bwd_core(dpre_pad_ref, x_pad_ref, w_ref, dx_ref, dw_ref, db_ref, t, rc, kw):
    cw = dx_ref.shape[1]

    def fold(a):
        return jnp.sum(a.reshape(rc // SUBLANES, SUBLANES, cw), axis=0) if rc % SUBLANES == 0 else jnp.sum(a, axis=0, keepdims=True)

    dws = [None] * kw
    dbs = None
    for r0 in range(0, t, rc):
        dpre = dpre_pad_ref[pl.ds(PAD + r0, rc), :]
        dx = None
        for j in range(kw):
            s = kw - 1 - j
            term = dpre_pad_ref[pl.ds(PAD + r0 + s, rc), :] * w_ref[j:j + 1, :]
            dx = term if dx is None else dx + term
            part = fold(dpre * _shifted(x_pad_ref, r0, rc, s))
            dws[j] = part if dws[j] is None else dws[j] + part
        part = fold(dpre)
        dbs = part if dbs is None else dbs + part
        dx_ref[pl.ds(r0, rc), :] = dx
    for j in range(kw):
        dw_ref[j:j + 1, :] = jnp.sum(dws[j], axis=0, keepdims=True)
    db_ref[...] = jnp.sum(dbs, axis=0, keepdims=True)


def _conv_silu_bwd(x, w, b, dact, *, x_off=0, name):
    t = x.shape[0]
    kw, c = w.shape
    cw = _tile(math.gcd(c, x_off) if x_off else c, (256, 128))
    ob = x_off // cw
    rc = _tile(t, (ROW_CHUNK,))

    def body(x_ref, w_ref, b_ref, da_ref, dx_ref, dw_ref, db_ref, xpad_ref, dpad_ref):
        _fill_pad(xpad_ref, x_ref, t)
        dpad_ref[0:PAD, :] = jnp.zeros((PAD, cw), F32)
        dpad_ref[pl.ds(PAD + t, PAD), :] = jnp.zeros((PAD, cw), F32)
        for r0 in range(0, t, rc):
            pre = _conv_taps(xpad_ref, w_ref, r0, rc, kw) + b_ref[...]
            sg = _sigmoid(pre)
            dpad_ref[pl.ds(PAD + r0, rc), :] = da_ref[pl.ds(r0, rc), :] * (sg * (1.0 + pre * (1.0 - sg)))
        _conv_bwd_core(dpad_ref, xpad_ref, w_ref, dx_ref, dw_ref, db_ref, t, rc, kw)

    strip = pl.BlockSpec((t, cw), lambda i: (0, i))
    wspec = pl.BlockSpec((kw, cw), lambda i: (0, i))
    bspec = pl.BlockSpec((1, cw), lambda i: (0, i))
    return pl.pallas_call(
        body,
        grid=(c // cw,),
        in_specs=[pl.BlockSpec((t, cw), lambda i: (0, i + ob)), wspec, bspec, strip],
        out_specs=[strip, wspec, bspec],
        out_shape=[jax.ShapeDtypeStruct((t, c), F32), jax.ShapeDtypeStruct((kw, c), F32),
                   jax.ShapeDtypeStruct((1, c), F32)],
        scratch_shapes=[pltpu.VMEM((t + 2 * PAD, cw), F32), pltpu.VMEM((t + 2 * PAD, cw), F32)],
        compiler_params=_cparams(("parallel",)),
        name=name,
    )(x, w, b.reshape(1, c), dact)


def _conv_glu_fwd(hid, w, b, *, name):
    t, c2 = hid.shape
    f = c2 // 2
    kw = w.shape[0]
    cw = _tile(f, (256, 128))
    nf = f // cw
    rc = _tile(t, (ROW_CHUNK,))

    def body(g_ref, v_ref, wg_ref, wv_ref, bg_ref, bv_ref, o_ref, gpad_ref, vpad_ref):
        _fill_pad(gpad_ref, g_ref, t)
        _fill_pad(vpad_ref, v_ref, t)
        for r0 in range(0, t, rc):
            gate = _conv_taps(gpad_ref, wg_ref, r0, rc, kw) + bg_ref[...]
            val = _conv_taps(vpad_ref, wv_ref, r0, rc, kw) + bv_ref[...]
            o_ref[pl.ds(r0, rc), :] = (gate * _sigmoid(gate) * val).astype(o_ref.dtype)

    gs = pl.BlockSpec((t, cw), lambda i: (0, i))
    vs = pl.BlockSpec((t, cw), lambda i: (0, i + nf))
    b2 = b.reshape(1, c2)
    return pl.pallas_call(
        body,
        grid=(nf,),
        in_specs=[gs, vs, pl.BlockSpec((kw, cw), lambda i: (0, i)), pl.BlockSpec((kw, cw), lambda i: (0, i + nf)),
                  pl.BlockSpec((1, cw), lambda i: (0, i)), pl.BlockSpec((1, cw), lambda i: (0, i + nf))],
        out_specs=gs,
        out_shape=jax.ShapeDtypeStruct((t, f), BF16),
        scratch_shapes=[pltpu.VMEM((t + 2 * PAD, cw), F32), pltpu.VMEM((t + 2 * PAD, cw), F32)],
        compiler_params=_cparams(("parallel",)),
        name=name,
    )(hid, hid, w, w, b2, b2)


def _conv_glu_bwd(hid, w, b, dact, *, name):
    t, c2 = hid.shape
    f = c2 // 2
    kw = w.shape[0]
    cw = _tile(f, (128,))
    nf = f // cw
    rc = _tile(t, (ROW_CHUNK,))

    def body(g_ref, v_ref, wg_ref, wv_ref, bg_ref, bv_ref, da_ref,
             dgv_ref, dwg_ref, dwv_ref, dbg_ref, dbv_ref,
             gpad_ref, vpad_ref, dgpad_ref, dvpad_ref):
        _fill_pad(gpad_ref, g_ref, t)
        _fill_pad(vpad_ref, v_ref, t)
        for ref in (dgpad_ref, dvpad_ref):
            ref[0:PAD, :] = jnp.zeros((PAD, cw), F32)
            ref[pl.ds(PAD + t, PAD), :] = jnp.zeros((PAD, cw), F32)
        for r0 in range(0, t, rc):
            gate = _conv_taps(gpad_ref, wg_ref, r0, rc, kw) + bg_ref[...]
            val = _conv_taps(vpad_ref, wv_ref, r0, rc, kw) + bv_ref[...]
            sg = _sigmoid(gate)
            da = da_ref[pl.ds(r0, rc), :].astype(F32)
            dgpad_ref[pl.ds(PAD + r0, rc), :] = da * val * (sg * (1.0 + gate * (1.0 - sg)))
            dvpad_ref[pl.ds(PAD + r0, rc), :] = da * (gate * sg)
        _conv_bwd_core(dgpad_ref, gpad_ref, wg_ref, dgv_ref.at[0], dwg_ref, dbg_ref, t, rc, kw)
        _conv_bwd_core(dvpad_ref, vpad_ref, wv_ref, dgv_ref.at[1], dwv_ref, dbv_ref, t, rc, kw)

    gs = pl.BlockSpec((t, cw), lambda i: (0, i))
    vs = pl.BlockSpec((t, cw), lambda i: (0, i + nf))
    wg = pl.BlockSpec((kw, cw), lambda i: (0, i))
    wv = pl.BlockSpec((kw, cw), lambda i: (0, i + nf))
    bg = pl.BlockSpec((1, cw), lambda i: (0, i))
    bv = pl.BlockSpec((1, cw), lambda i: (0, i + nf))
    b2 = b.reshape(1, c2)
    pad = pltpu.VMEM((t + 2 * PAD, cw), F32)
    return pl.pallas_call(
        body,
        grid=(nf,),
        in_specs=[gs, vs, wg, wv, bg, bv, gs],
        out_specs=[pl.BlockSpec((2, t, cw), lambda i: (0, 0, i)), wg, wg, bg, bg],
        out_shape=[jax.ShapeDtypeStruct((2, t, f), F32),
                   jax.ShapeDtypeStruct((kw, f), F32), jax.ShapeDtypeStruct((kw, f), F32),
                   jax.ShapeDtypeStruct((1, f), F32), jax.ShapeDtypeStruct((1, f), F32)],
        scratch_shapes=[pad, pad, pad, pad],
        compiler_params=_cparams(("parallel",)),
        name=name,
    )(hid, hid, w, w, b2, b2, dact)


def _gate_norm_fwd(y, zx, w, *, name):
    t, di = y.shape
    gsz = di // SSM_GROUPS
    tb = _tile(t, (256, 128))

    def body(y_ref, z_ref, w_ref, o_ref):
        for g in range(SSM_GROUPS):
            sl = slice(g * gsz, (g + 1) * gsz)
            zv = z_ref[:, sl]
            gv = y_ref[:, sl] * (zv * _sigmoid(zv))
            r = lax.rsqrt(jnp.mean(gv * gv, axis=-1, keepdims=True) + EPS)
            o_ref[:, sl] = (gv * r * w_ref[:, sl]).astype(o_ref.dtype)

    row = pl.BlockSpec((tb, di), lambda i: (i, 0))
    return pl.pallas_call(
        body,
        grid=(t // tb,),
        in_specs=[row, row, pl.BlockSpec((1, di), lambda i: (0, 0))],
        out_specs=row,
        out_shape=jax.ShapeDtypeStruct((t, di), BF16),
        compiler_params=_cparams(("parallel",)),
        name=name,
    )(y, zx, w.reshape(1, di))


def _gate_norm_bwd(y, zx, w, dyn, *, name):
    t, di = y.shape
    gsz = di // SSM_GROUPS
    tb = _tile(t, (256, 128))

    def body(y_ref, z_ref, w_ref, d_ref, dy_ref, dz_ref, dw_ref):
        i = pl.program_id(0)
        for g in range(SSM_GROUPS):
            sl = slice(g * gsz, (g + 1) * gsz)
            zv = z_ref[:, sl]
            yv = y_ref[:, sl]
            sg = _sigmoid(zv)
            sz = zv * sg
            gv = yv * sz
            r = lax.rsqrt(jnp.mean(gv * gv, axis=-1, keepdims=True) + EPS)
            gn = gv * r
            dn = d_ref[:, sl].astype(F32)
            q = dn * w_ref[:, sl]
            dg = r * (q - gn * jnp.mean(q * gn, axis=-1, keepdims=True))
            dy_ref[:, sl] = dg * sz
            dz_ref[:, sl] = dg * yv * (sg * (1.0 + zv * (1.0 - sg)))
            dwp = jnp.sum(dn * gn, axis=0, keepdims=True)

            @pl.when(i == 0)
            def _(sl=sl, dwp=dwp):
                dw_ref[:, sl] = dwp

            @pl.when(i > 0)
            def _(sl=sl, dwp=dwp):
                dw_ref[:, sl] += dwp

    row = pl.BlockSpec((tb, di), lambda i: (i, 0))
    vec = pl.BlockSpec((1, di), lambda i: (0, 0))
    return pl.pallas_call(
        body,
        grid=(t // tb,),
        in_specs=[row, row, vec, row],
        out_specs=[row, row, vec],
        out_shape=[jax.ShapeDtypeStruct((t, di), F32), jax.ShapeDtypeStruct((t, di), F32),
                   jax.ShapeDtypeStruct((1, di), F32)],
        compiler_params=_cparams(("arbitrary",)),
        name=name,
    )(y, zx, w.reshape(1, di), dyn)


def _adamw(w, g, m, v, *, name):
    shape = w.shape
    cols = shape[-1]
    rows = w.size // cols
    w2, g2, m2, v2 = (a.reshape(rows, cols) for a in (w, g, m, v))
    tr = rows
    if rows * cols * 4 > ADAM_BLOCK_BYTES:
        tr = _tile(rows, tuple(r for r in (512, 256, 128, 64, 32, 16, 8) if r * cols * 4 <= ADAM_BLOCK_BYTES))
    c1 = 1.0 - ADAM_B1 ** ADAM_STEP
    c2 = 1.0 - ADAM_B2 ** ADAM_STEP

    def body(w_ref, g_ref, m_ref, v_ref, d_ref, nm_ref, nv_ref):
        gv = g_ref[...]
        nm = ADAM_B1 * m_ref[...] + (1.0 - ADAM_B1) * gv
        nv = ADAM_B2 * v_ref[...] + (1.0 - ADAM_B2) * (gv * gv)
        d_ref[...] = -ADAM_LR * ((nm / c1) / (jnp.sqrt(nv / c2) + ADAM_EPS) + ADAM_WD * w_ref[...])
        nm_ref[...] = nm
        nv_ref[...] = nv

    blk = pl.BlockSpec((tr, cols), lambda i: (i, 0))
    outs = pl.pallas_call(
        body,
        grid=(rows // tr,),
        in_specs=[blk] * 4,
        out_specs=[blk] * 3,
        out_shape=[jax.ShapeDtypeStruct((rows, cols), F32)] * 3,
        compiler_params=_cparams(("parallel",)),
        name=name,
    )(w2, g2, m2, v2)
    return tuple(o.reshape(shape) for o in outs)


def _adamw_layers(w, gs, m, v, *, name):
    n_l, rows, cols = w.shape
    assert len(gs) == n_l
    tr = _tile(rows, tuple(r for r in (512, 256, 128, 64, 32, 16, 8) if r * cols * 4 <= ADAM_BLOCK_BYTES))
    c1 = 1.0 - ADAM_B1 ** ADAM_STEP
    c2 = 1.0 - ADAM_B2 ** ADAM_STEP

    def body(*refs):
        w_ref, m_ref, v_ref = refs[:3]
        g_refs = refs[3:3 + n_l]
        g_ref, d_ref, nm_ref, nv_ref = refs[3 + n_l:]
        layer = pl.program_id(0)
        gv = g_refs[0][...]
        for q in range(1, n_l):
            gv = jnp.where(layer == q, g_refs[q][...], gv)
        nm = ADAM_B1 * m_ref[...] + (1.0 - ADAM_B1) * gv
        nv = ADAM_B2 * v_ref[...] + (1.0 - ADAM_B2) * (gv * gv)
        g_ref[...] = gv
        d_ref[...] = -ADAM_LR * ((nm / c1) / (jnp.sqrt(nv / c2) + ADAM_EPS) + ADAM_WD * w_ref[...])
        nm_ref[...] = nm
        nv_ref[...] = nv

    stacked = pl.BlockSpec((None, tr, cols), lambda l, i: (l, i, 0))
    single = pl.BlockSpec((tr, cols), lambda l, i: (i, 0))
    return pl.pallas_call(
        body,
        grid=(n_l, rows // tr),
        in_specs=[stacked] * 3 + [single] * n_l,
        out_specs=[stacked] * 4,
        out_shape=[jax.ShapeDtypeStruct(w.shape, F32)] * 4,
        compiler_params=_cparams(("parallel", "parallel")),
        name=name,
    )(w, m, v, *gs)


def _ssd_scalars(dtc_ref, dtr_ref, hpc_ref, hpr_ref, ln):
    assert SSM_CHUNK == SSM_STATE == LANES, "the SSD kernels mix chunk, state and lane-wide tiles freely"
    bias_c, alog_c = hpc_ref[0, 0:1, :], hpc_ref[0, 1:2, :]
    bias_r, alog_r = hpr_ref[0, :, 0:1], hpr_ref[0, :, 1:2]
    a_c, a_r = -jnp.exp(alog_c), -jnp.exp(alog_r)
    raw_c = dtc_ref[0] + bias_c
    dt_c = _softplus(raw_c)
    dt_r = _softplus(dtr_ref[0] + bias_r)
    row = lax.broadcasted_iota(jnp.int32, (ln, ln), 0)
    col = lax.broadcasted_iota(jnp.int32, (ln, ln), 1)
    lower = (col <= row).astype(F32)
    upper = (row <= col).astype(F32)
    acs_c = _ones_dot(lower, dt_c * a_c, ones_left=True)
    acs_r = _ones_dot(upper, dt_r * a_r, ones_left=False)
    return raw_c, dt_c, a_c, acs_c, acs_r, row, col


def _ssd_specs(t, di, g_n, n_st, rp, ln, r_h, rev):
    nc = t // ln
    cidx = (lambda c: nc - 1 - c) if rev else (lambda c: c)
    xs = pl.BlockSpec((ln, rp), lambda g, c: (cidx(c), g))
    bm = pl.BlockSpec((ln, n_st), lambda g, c: (cidx(c), di // n_st + g))
    cm = pl.BlockSpec((ln, n_st), lambda g, c: (cidx(c), di // n_st + g_n + g))
    dtc = pl.BlockSpec((1, ln, r_h), lambda g, c: (g, cidx(c), 0))
    dtr = pl.BlockSpec((1, r_h, ln), lambda g, c: (g, 0, cidx(c)))
    hpc = pl.BlockSpec((1, 3, r_h), lambda g, c: (g, 0, 0))
    hpr = pl.BlockSpec((1, r_h, 3), lambda g, c: (g, 0, 0))
    prev = pl.BlockSpec((1, rp, n_st), lambda g, c: (cidx(c), g, 0))
    return xs, bm, cm, dtc, dtr, hpc, hpr, prev


def _ssd_fwd(xbc, dtc, dtr, hpc, hpr, *, name):
    t = xbc.shape[0]
    di, g_n, n_st, p_h, ln = D_INNER, SSM_GROUPS, SSM_STATE, SSM_HEAD_DIM, SSM_CHUNK
    r_h = SSM_HEADS // g_n
    rp = r_h * p_h
    nc = t // ln

    def body(xs_ref, b_ref, c_ref, dtc_ref, dtr_ref, hpc_ref, hpr_ref, y_ref, prev_ref, st_ref):
        @pl.when(pl.program_id(1) == 0)
        def _():
            st_ref[...] = jnp.zeros_like(st_ref)

        _, dt_c, _, acs_c, acs_r, row, col = _ssd_scalars(dtc_ref, dtr_ref, hpc_ref, hpr_ref, ln)
        bm = b_ref[...]
        cm = c_ref[...]
        cm16 = cm.astype(BF16)
        cb = _nt(cm16, bm.astype(BF16))
        causal = row >= col
        for r in range(r_h):
            sl = slice(r * p_h, (r + 1) * p_h)
            xs = xs_ref[:, sl]
            acs = jnp.broadcast_to(acs_c[:, r:r + 1], (ln, ln))
            last = acs[ln - 1:ln, :]
            lm = jnp.where(causal, jnp.exp(acs - acs_r[r:r + 1, :]), 0.0)
            xd = (xs * jnp.broadcast_to(dt_c[:, r:r + 1], (ln, p_h))).astype(BF16)
            prev = st_ref[sl, :]
            y = _nn((cb * lm).astype(BF16), xd)
            y = y + _nt(cm16, prev.astype(BF16)) * jnp.exp(acs[:, :p_h])
            y_ref[:, sl] = y + hpc_ref[0, 2:3, r:r + 1] * xs
            prev_ref[0, sl, :] = prev
            bd = (bm * jnp.exp(last - acs[:, :n_st])).astype(BF16)
            st_ref[sl, :] = prev * jnp.exp(last[:, :n_st]) + _tn(xd, bd)

    xs, bm, cm, dtcs, dtrs, hpcs, hprs, prev = _ssd_specs(t, di, g_n, n_st, rp, ln, r_h, False)
    return pl.pallas_call(
        body,
        grid=(g_n, nc),
        in_specs=[xs, bm, cm, dtcs, dtrs, hpcs, hprs],
        out_specs=[xs, prev],
        out_shape=[jax.ShapeDtypeStruct((t, di), F32), jax.ShapeDtypeStruct((nc, g_n * rp, n_st), F32)],
        scratch_shapes=[pltpu.VMEM((rp, n_st), F32)],
        compiler_params=_cparams(("parallel", "arbitrary")),
        name=name,
    )(xbc, xbc, xbc, dtc, dtr, hpc, hpr)


def _ssd_bwd(xbc, dtc, dtr, hpc, hpr, prev, dy, *, name):
    t = xbc.shape[0]
    di, g_n, n_st, p_h, ln = D_INNER, SSM_GROUPS, SSM_STATE, SSM_HEAD_DIM, SSM_CHUNK
    r_h = SSM_HEADS // g_n
    rp = r_h * p_h
    nc = t // ln

    def body(xs_ref, b_ref, c_ref, dtc_ref, dtr_ref, hpc_ref, hpr_ref, prev_ref, dy_ref,
             dxs_ref, db_ref, dc_ref, ddt_ref, hg_ref, ds_ref):
        step = pl.program_id(1)

        @pl.when(step == 0)
        def _():
            ds_ref[...] = jnp.zeros_like(ds_ref)

        raw_c, dt_c, a_c, acs_c, acs_r, row, col = _ssd_scalars(dtc_ref, dtr_ref, hpc_ref, hpr_ref, ln)
        bm = b_ref[...]
        cm = c_ref[...]
        bm16, cm16 = bm.astype(BF16), cm.astype(BF16)
        cb = _nt(cm16, bm16)
        cbt = _nt(bm16, cm16)
        lane_r = lax.broadcasted_iota(jnp.int32, (ln, r_h), 1)
        dacs_all = jnp.zeros((ln, r_h), F32)
        ddtx_all = jnp.zeros((ln, r_h), F32)
        dd_all = jnp.zeros((ln, r_h), F32)
        dcb = jnp.zeros((ln, ln), F32)
        dcbt = jnp.zeros((ln, ln), F32)
        dc_acc = jnp.zeros((ln, n_st), F32)
        db_acc = jnp.zeros((ln, n_st), F32)
        for r in range(r_h):
            sl = slice(r * p_h, (r + 1) * p_h)
            xs = xs_ref[:, sl]
            dyv = dy_ref[:, sl]
            dy16 = dyv.astype(BF16)
            acs = jnp.broadcast_to(acs_c[:, r:r + 1], (ln, ln))
            dtv = jnp.broadcast_to(dt_c[:, r:r + 1], (ln, p_h))
            acsr = acs_r[r:r + 1, :]
            last = acs[ln - 1:ln, :]
            xd = xs * dtv
            xd16 = xd.astype(BF16)
            lm = jnp.where(row >= col, jnp.exp(acs - acsr), 0.0)
            lmt = jnp.where(col >= row, jnp.exp(acsr - acs), 0.0)
            m_ls = cb * lm
            m_sl = cbt * lmt
            dm = _nt(dy16, xd16)
            dmt = _nt(xd16, dy16)
            dxd = _nn(m_sl.astype(BF16), dy16)
            dacs = _row_sums(dm * m_ls - dmt * m_sl)
            dcb = dcb + dm * lm
            dcbt = dcbt + dmt * lmt
            prev = prev_ref[0, sl, :]
            prev16 = prev.astype(BF16)
            e = jnp.exp(acs[:, :p_h])
            y_off = _nt(cm16, prev16) * e
            dacs = dacs + _row_sums(dyv * y_off)
            dyo16 = (dyv * e).astype(BF16)
            dc_acc = dc_acc + _nn(dyo16, prev16)
            dprev = _tn(dyo16, cm16)
            ds = ds_ref[sl, :]
            ds16 = ds.astype(BF16)
            decay = jnp.exp(last - acs)[:, :n_st]
            bd16 = (bm * decay).astype(BF16)
            dbd = _nn(xd16, ds16)
            dxd = dxd + _nt(bd16, ds16)
            db_acc = db_acc + dbd * decay
            tdec = _row_sums(dbd * bm, 2) * decay
            cd = jnp.exp(last)
            dlast = (jnp.sum(tdec, axis=0, keepdims=True)
                     + jnp.sum(_row_sums(prev * ds, 2), axis=0, keepdims=True) * cd)
            ds_ref[sl, :] = dprev + cd[:, :n_st] * ds
            dskip = hpc_ref[0, 2:3, r:r + 1]
            dxs_ref[:, sl] = dxd * dtv + dskip * dyv
            dacs = dacs - tdec + jnp.where(row == ln - 1, dlast, 0.0)
            dacs_all = jnp.where(lane_r == r, dacs[:, :r_h], dacs_all)
            ddtx_all = jnp.where(lane_r == r, _row_sums(dxd * xs, 2)[:, :r_h], ddtx_all)
            dd_all = jnp.where(lane_r == r, _row_sums(dyv * xs, 2)[:, :r_h], dd_all)
        dc_ref[...] = dc_acc + _nn(dcb.astype(BF16), bm16)
        db_ref[...] = db_acc + _nn(dcbt.astype(BF16), cm16)
        upper = (row <= col).astype(F32)
        dad = _ones_dot(upper, dacs_all, ones_left=True)
        ddt = dad * a_c + ddtx_all
        ddt_raw = ddt * _sigmoid(raw_c)
        ddt_ref[0] = ddt_raw
        d_bias = jnp.sum(ddt_raw, axis=0, keepdims=True)
        d_alog = jnp.sum(dad * dt_c, axis=0, keepdims=True) * a_c
        d_d = jnp.sum(dd_all, axis=0, keepdims=True)
        hg = jnp.concatenate([d_bias, d_alog, d_d], axis=0)

        @pl.when(step == 0)
        def _():
            hg_ref[0] = hg

        @pl.when(step > 0)
        def _():
            hg_ref[0] += hg

    xs, bms, cms, dtcs, dtrs, hpcs, hprs, prevs = _ssd_specs(t, di, g_n, n_st, rp, ln, r_h, True)
    bout = pl.BlockSpec((ln, n_st), lambda g, c: (nc - 1 - c, g))
    return pl.pallas_call(
        body,
        grid=(g_n, nc),
        in_specs=[xs, bms, cms, dtcs, dtrs, hpcs, hprs, prevs, xs],
        out_specs=[xs, bout, bout, dtcs, hpcs],
        out_shape=[jax.ShapeDtypeStruct((t, di), F32), jax.ShapeDtypeStruct((t, g_n * n_st), F32),
                   jax.ShapeDtypeStruct((t, g_n * n_st), F32), jax.ShapeDtypeStruct((g_n, t, r_h), F32),
                   jax.ShapeDtypeStruct((g_n, 3, r_h), F32)],
        scratch_shapes=[pltpu.VMEM((rp, n_st), F32)],
        compiler_params=_cparams(("parallel", "arbitrary")),
        name=name,
    )(xbc, xbc, xbc, dtc, dtr, hpc, hpr, prev, dy)


SB_KEYS = 256
SB_QUERIES = (1024, 512, 256)
SB_PIECES = 2


def _sb_logits(qs, kv, valid):
    z = _nt(qs, kv)
    nz = -z
    lg = jnp.minimum(nz, 0.0) - jnp.log(1.0 + jnp.exp(jnp.minimum(z, nz)))
    return z + lg, (lg if valid is None else jnp.where(valid, lg, 0.0))


def _sb_iota(tq):
    diff = lax.broadcasted_iota(jnp.int32, (tq, SB_KEYS), 1) - lax.broadcasted_iota(jnp.int32, (tq, SB_KEYS), 0)
    krow = lax.broadcasted_iota(jnp.int32, (SB_KEYS, SB_KEYS), 0)
    kcol = lax.broadcasted_iota(jnp.int32, (SB_KEYS, SB_KEYS), 1)
    return diff, krow, kcol


def _sb_scale(d):
    scale = 1.0 / math.sqrt(d)
    assert math.frexp(scale)[0] == 0.5, "the scale is folded into bf16 queries: it must be a power of two"
    return scale


def _key_rows(j):
    return pl.ds(pl.multiple_of(j * SB_KEYS, SB_KEYS), SB_KEYS)


def _pairs(tiles, per_tile, one, carry):
    if per_tile % 2:
        return lax.fori_loop(0, tiles * per_tile, one, carry)
    return lax.fori_loop(0, tiles * (per_tile // 2), lambda s, cr: one(2 * s + 1, one(2 * s, cr)), carry)


def _sb_fwd(q, k, v, *, name):
    h, t, d = q.shape
    tq = _tile(t, SB_QUERIES)
    nq = t // tq
    kpq = tq // SB_KEYS
    scale = _sb_scale(d)

    def body(q_ref, k_ref, v_ref, o_ref, lt_ref):
        i = pl.program_id(1)
        qs = (q_ref[0].astype(F32) * scale).astype(BF16)
        diff, krow, kcol = _sb_iota(tq)
        later = (krow > kcol).astype(F32)

        def block(j, carry, valid):
            acc, cl = carry
            rows = _key_rows(j)
            ls, lg = _sb_logits(qs, k_ref[0, rows, :], valid)
            cs = _ones_dot(later, lg, ones_left=False, pieces=SB_PIECES)
            att = jnp.exp(ls + (cs + cl))
            if valid is not None:
                att = jnp.where(valid, att, 0.0)
            acc = acc + _nn(att.astype(BF16), v_ref[0, rows, :])
            return acc, cl + (cs[:, 0:1] + lg[:, 0:1])

        carry = (jnp.zeros((tq, d), F32), jnp.zeros((tq, 1), F32))
        for m in range(kpq - 1, -1, -1):
            carry = block(i * kpq + m, carry, diff < -m * SB_KEYS)
        nb = i * kpq
        acc, cl = _pairs(i, kpq, lambda s, cr: block(nb - 1 - s, cr, None), carry)
        o_ref[0] = acc
        lt_ref[0] = cl

    qs = pl.BlockSpec((1, tq, d), lambda hh, i: (hh, i, 0))
    ls = pl.BlockSpec((1, tq, 1), lambda hh, i: (hh, i, 0))
    ks = pl.BlockSpec((1, t, d), lambda hh, i: (hh, 0, 0))
    return pl.pallas_call(
        body,
        grid=(h, nq),
        in_specs=[qs, ks, ks],
        out_specs=[qs, ls],
        out_shape=[jax.ShapeDtypeStruct((h, t, d), F32), jax.ShapeDtypeStruct((h, t, 1), F32)],
        compiler_params=_cparams(("parallel", "arbitrary")),
        name=name,
    )(q, k, v)


def _sb_bwd(q, k, v, lt, do, *, name):
    h, t, d = q.shape
    tq = _tile(t, SB_QUERIES)
    nq = t // tq
    kpq = tq // SB_KEYS
    scale = _sb_scale(d)
    last = SB_KEYS - 1

    def body(q_ref, k_ref, v_ref, lt_ref, do_ref, dq_ref, dk_ref, dv_ref):
        i = pl.program_id(1)

        @pl.when(i == 0)
        def _():
            dk_ref[...] = jnp.zeros_like(dk_ref)
            dv_ref[...] = jnp.zeros_like(dv_ref)

        qs = (q_ref[0].astype(F32) * scale).astype(BF16)
        do16 = do_ref[0].astype(BF16)
        ltot = lt_ref[0]
        diff, krow, kcol = _sb_iota(tq)
        upto = (krow <= kcol).astype(F32)
        before = (krow < kcol).astype(F32)

        def block(j, carry, valid):
            dq, pl_sum, pg_sum = carry
            rows = _key_rows(j)
            kv = k_ref[0, rows, :]
            vv = v_ref[0, rows, :]
            ls, lg = _sb_logits(qs, kv, valid)
            pre = _ones_dot(upto, lg, ones_left=False, pieces=SB_PIECES)
            att = jnp.exp(ls + (ltot - (pre + pl_sum)))
            if valid is not None:
                att = jnp.where(valid, att, 0.0)
            g = att * _nt(do16, vv)
            gpre = _ones_dot(before, g, ones_left=False, pieces=SB_PIECES)
            sig = jnp.exp(ls)
            dz16 = (g - sig * (g + (gpre + pg_sum))).astype(BF16)
            if valid is not None:
                dz16 = jnp.where(valid, dz16, jnp.zeros_like(dz16))
            dq = dq + _nn(dz16, kv)
            dk_ref[0, rows, :] += _tn(dz16, qs)
            dv_ref[0, rows, :] += _tn(att.astype(BF16), do16)
            return dq, pl_sum + pre[:, last:], pg_sum + (gpre[:, last:] + g[:, last:])

        zero = jnp.zeros((tq, 1), F32)
        nb = i * kpq
        carry = _pairs(i, kpq, lambda j, cr: block(j, cr, None), (jnp.zeros((tq, d), F32), zero, zero))
        for m in range(kpq):
            carry = block(nb + m, carry, diff < -m * SB_KEYS)
        dq_ref[0] = carry[0] * scale

    qs = pl.BlockSpec((1, tq, d), lambda hh, i: (hh, i, 0))
    ls = pl.BlockSpec((1, tq, 1), lambda hh, i: (hh, i, 0))
    ks = pl.BlockSpec((1, t, d), lambda hh, i: (hh, 0, 0))
    full = jax.ShapeDtypeStruct((h, t, d), F32)
    return pl.pallas_call(
        body,
        grid=(h, nq),
        in_specs=[qs, ks, ks, ls, qs],
        out_specs=[qs, ks, ks],
        out_shape=[full, full, full],
        compiler_params=_cparams(("parallel", "arbitrary")),
        name=name,
    )(q, k, v, lt, do)


def _row_tile(rows, cols):
    return _tile(rows, tuple(r for r in (2048, 1024, 512, 256, 128, 64, 32, 16, 8) if r * cols * 4 <= ADAM_BLOCK_BYTES))


def _sum_leading(x, *, name):
    n, rows, cols = x.shape
    tr = _row_tile(rows, cols)

    def body(x_ref, o_ref):
        acc = x_ref[0].astype(F32)
        for q in range(1, n):
            acc = acc + x_ref[q].astype(F32)
        o_ref[...] = acc

    return pl.pallas_call(
        body,
        grid=(rows // tr,),
        in_specs=[pl.BlockSpec((n, tr, cols), lambda i: (0, i, 0))],
        out_specs=pl.BlockSpec((tr, cols), lambda i: (i, 0)),
        out_shape=jax.ShapeDtypeStruct((rows, cols), F32),
        compiler_params=_cparams(("parallel",)),
        name=name,
    )(x)


def _pair_add(g4h, recv, c, *, out_dtype, name):
    n, _, rows, cols = g4h.shape
    tr = _row_tile(rows, cols)

    def body(c_ref, g_ref, r_ref, o_ref):
        o_ref[...] = (g_ref[...] + r_ref[...]).astype(o_ref.dtype)

    blk = pl.BlockSpec((1, tr, cols), lambda q, i, c_ref: (q, i, 0))
    return pl.pallas_call(
        body,
        grid_spec=pltpu.PrefetchScalarGridSpec(
            num_scalar_prefetch=1,
            grid=(n, rows // tr),
            in_specs=[pl.BlockSpec((1, None, tr, cols), lambda q, i, c_ref: (q, c_ref[0], i, 0)), blk],
            out_specs=blk),
        out_shape=jax.ShapeDtypeStruct((n, rows, cols), out_dtype),
        compiler_params=_cparams(("parallel", "parallel")),
        name=name,
    )(c.reshape(1).astype(jnp.int32), g4h, recv)


ANY = pl.BlockSpec(memory_space=pl.ANY)


def _other_chips(x, y):
    return [(1 - x, y), (x, 1 - y), (1 - x, 1 - y)]


def _gather_chips(shard, *, name):
    def body(x_ref, o_ref, send_sems, recv_sems, local_sem):
        x, y, c = lax.axis_index("x"), lax.axis_index("y"), lax.axis_index("c")
        me = 2 * x + y
        mine = pltpu.make_async_copy(x_ref, o_ref.at[me], local_sem)
        mine.start()
        chips = _other_chips(x, y)
        sends = [pltpu.make_async_remote_copy(src_ref=x_ref, dst_ref=o_ref.at[me], send_sem=send_sems.at[q],
                                              recv_sem=recv_sems.at[q], device_id=(px, py, c), device_id_type=MESH)
                 for q, (px, py) in enumerate(chips)]
        for cp in sends:
            cp.start()
        for q, (px, py) in enumerate(chips):
            pltpu.make_async_remote_copy(src_ref=x_ref, dst_ref=o_ref.at[2 * px + py], send_sem=send_sems.at[q],
                                         recv_sem=recv_sems.at[q], device_id=(px, py, c), device_id_type=MESH).wait_recv()
        for cp in sends:
            cp.wait_send()
        mine.wait()

    return pl.pallas_call(
        body,
        in_specs=[ANY],
        out_specs=ANY,
        out_shape=jax.ShapeDtypeStruct((4,) + shard.shape, shard.dtype),
        scratch_shapes=[pltpu.SemaphoreType.DMA((3,)), pltpu.SemaphoreType.DMA((3,)), pltpu.SemaphoreType.DMA],
        compiler_params=pltpu.CompilerParams(has_side_effects=True),
        name=name,
    )(shard)


def _comm_call(body, ins, out_shapes, n_sems, name):
    n = len(ins)

    def wrapped(*refs):
        body(refs[:n], refs[n:n + len(out_shapes)], refs[-2], refs[-1])

    return pl.pallas_call(
        wrapped,
        in_specs=[ANY] * n,
        out_specs=[ANY] * len(out_shapes),
        out_shape=out_shapes,
        scratch_shapes=[pltpu.SemaphoreType.DMA((n_sems,)), pltpu.SemaphoreType.DMA((n_sems,))],
        compiler_params=pltpu.CompilerParams(has_side_effects=True),
        name=name,
    )(*ins)


def _remote(send_sems, recv_sems, q, src, dst, to):
    return pltpu.make_async_remote_copy(src_ref=src, dst_ref=dst, send_sem=send_sems.at[q], recv_sem=recv_sems.at[q],
                                        device_id=to, device_id_type=MESH)


def _scatter_chips(parts, *, name):
    def body(ins, outs, send_sems, recv_sems):
        x, y, c = lax.axis_index("x"), lax.axis_index("y"), lax.axis_index("c")
        me = 2 * x + y
        chips = _other_chips(x, y)
        sends = [_remote(send_sems, recv_sems, 3 * i + q, p.at[2 * px + py], o.at[me], (px, py, c))
                 for i, (p, o) in enumerate(zip(ins, outs)) for q, (px, py) in enumerate(chips)]
        for cp in sends:
            cp.start()
        for i, (p, o) in enumerate(zip(ins, outs)):
            for q, (px, py) in enumerate(chips):
                _remote(send_sems, recv_sems, 3 * i + q, p.at[me], o.at[2 * px + py], (px, py, c)).wait_recv()
        for cp in sends:
            cp.wait_send()

    return _comm_call(body, parts, [jax.ShapeDtypeStruct(p.shape, p.dtype) for p in parts], 3 * len(parts), name)


def _gather_chips_halves(shards, *, name):
    def body(ins, outs, send_sems, recv_sems):
        x, y, c = lax.axis_index("x"), lax.axis_index("y"), lax.axis_index("c")
        me, sibling = 2 * x + y, (x, y, 1 - c)
        chips = _other_chips(x, y)
        copy = lambda q, src, dst, to: _remote(send_sems, recv_sems, q, src, dst, to)
        sends = [copy(6 * i + q, s.at[c], o.at[me, c], (px, py, c))
                 for i, (s, o) in enumerate(zip(ins, outs)) for q, (px, py) in enumerate(chips)]
        for cp in sends:
            cp.start()
        passed = []
        for i, (s, o) in enumerate(zip(ins, outs)):
            for q, (px, py) in enumerate(chips):
                slot = o.at[2 * px + py, c]
                copy(6 * i + q, s.at[c], slot, (px, py, c)).wait_recv()
                passed.append(copy(6 * i + 3 + q, slot, slot, sibling))
                passed[-1].start()
        for i, (s, o) in enumerate(zip(ins, outs)):
            for q, (px, py) in enumerate(chips):
                copy(6 * i + 3 + q, s.at[1 - c], o.at[2 * px + py, 1 - c], sibling).wait_recv()
        for cp in sends + passed:
            cp.wait_send()

    return _comm_call(body, shards, [jax.ShapeDtypeStruct((N_CHIPS,) + s.shape, s.dtype) for s in shards],
                      6 * len(shards), name)


def _swap_other_half(gs, *, name):
    def body(ins, outs, send_sems, recv_sems):
        x, y, c = lax.axis_index("x"), lax.axis_index("y"), lax.axis_index("c")
        copies = [_remote(send_sems, recv_sems, i, g.at[pl.ds(0, g.shape[0]), 1 - c], o, (x, y, 1 - c))
                  for i, (g, o) in enumerate(zip(ins, outs))]
        for cp in copies:
            cp.start()
        for cp in copies:
            cp.wait()

    return _comm_call(body, gs, [jax.ShapeDtypeStruct((g.shape[0],) + g.shape[2:], g.dtype) for g in gs], len(gs), name)


def _join_halves(halves, *, name):
    def body(ins, outs, send_sems, recv_sems):
        x, y, c = lax.axis_index("x"), lax.axis_index("y"), lax.axis_index("c")
        sibling = (x, y, 1 - c)
        sends = [_remote(send_sems, recv_sems, i, h, o.at[c], sibling) for i, (h, o) in enumerate(zip(ins, outs))]
        for cp in sends:
            cp.start()
        for i, (h, o) in enumerate(zip(ins, outs)):
            _remote(send_sems, recv_sems, i, h, o.at[1 - c], sibling).wait_recv()
        for cp in sends:
            cp.wait_send()

    return _comm_call(body, halves, [jax.ShapeDtypeStruct((2,) + h.shape, h.dtype) for h in halves], len(halves), name)


def _gather_all(v, *, name):
    def body(v_ref, o_ref, send_sems, recv_sems, local_sem):
        x, y, c = lax.axis_index("x"), lax.axis_index("y"), lax.axis_index("c")
        me = 4 * x + 2 * y + c
        mine = pltpu.make_async_copy(v_ref, o_ref.at[me], local_sem)
        mine.start()
        peers = [(x ^ (q >> 2 & 1), y ^ (q >> 1 & 1), c ^ (q & 1)) for q in range(1, 8)]
        sends = [pltpu.make_async_remote_copy(src_ref=v_ref, dst_ref=o_ref.at[me], send_sem=send_sems.at[q],
                                              recv_sem=recv_sems.at[q], device_id=peer, device_id_type=MESH)
                 for q, peer in enumerate(peers)]
        for cp in sends:
            cp.start()
        for q, (px, py, pc) in enumerate(peers):
            pltpu.make_async_remote_copy(src_ref=v_ref, dst_ref=o_ref.at[4 * px + 2 * py + pc], send_sem=send_sems.at[q],
                                         recv_sem=recv_sems.at[q], device_id=(px, py, pc), device_id_type=MESH).wait_recv()
        for cp in sends:
            cp.wait_send()
        mine.wait()

    return pl.pallas_call(
        body,
        in_specs=[ANY],
        out_specs=ANY,
        out_shape=jax.ShapeDtypeStruct((8,) + v.shape, v.dtype),
        scratch_shapes=[pltpu.SemaphoreType.DMA((7,)), pltpu.SemaphoreType.DMA((7,)), pltpu.SemaphoreType.DMA],
        compiler_params=pltpu.CompilerParams(has_side_effects=True),
        name=name,
    )(v)


WEIGHTS = ['ssm_norm_w', 'ssm_in_w', 'ssm_conv_w', 'ssm_conv_b', 'ssm_dt_bias', 'ssm_a_log', 'ssm_d',
           'ssm_gate_norm_w', 'ssm_out_w', 'kv_norm_w', 'w_k', 'w_v', 'attn_norm_w', 'w_q', 'w_o',
           'ffn_norm_w', 'ffn_up_w', 'ffn_conv_w', 'ffn_conv_b', 'ffn_down_w', 'final_norm_w']
SHARD_AXIS = {'ssm_norm_w': 1, 'ssm_in_w': 2, 'ssm_conv_w': 2, 'ssm_conv_b': 1, 'ssm_gate_norm_w': 1,
              'ssm_out_w': 1, 'w_k': 0, 'w_v': 0, 'w_q': 1, 'w_o': 1, 'ffn_up_w': 2, 'ffn_conv_w': 2,
              'ffn_down_w': 1}
BIG = ['ssm_in_w', 'ssm_out_w', 'w_k', 'w_v', 'w_q', 'w_o', 'ffn_up_w', 'ffn_down_w']
SMALL = [n for n in WEIGHTS if n in SHARD_AXIS and n not in BIG]
REPLICATED = [n for n in WEIGHTS if n not in SHARD_AXIS]
STACKED = ['ffn_up_w', 'ffn_down_w']
N_CHIPS = 4


PACK_ROWS = 16


def _piece_rows(n):
    return -(-n // (PACK_ROWS * LANES)) * PACK_ROWS


def _pack(arrs, dtype, row_mult):
    lead = arrs[0].shape[:-1]
    pieces, total = [], 0
    for a in arrs:
        n = a.shape[-1]
        rows = _piece_rows(n)
        a = a.astype(dtype)
        if rows * LANES != n:
            a = jnp.pad(a, [(0, 0)] * len(lead) + [(0, rows * LANES - n)])
        pieces.append(a.reshape(lead + (rows, LANES)))
        total += rows
    extra = -total % row_mult
    if extra:
        pieces.append(jnp.zeros(lead + (extra, LANES), dtype))
    return jnp.concatenate(pieces, axis=len(lead))


def _unpack(buf, shapes):
    lead = buf.shape[:-2]
    out, off = [], 0
    for shp in shapes:
        n = math.prod(shp)
        rows = _piece_rows(n)
        piece = lax.slice_in_dim(buf, off, off + rows, axis=len(lead)).reshape(lead + (rows * LANES,))
        out.append(piece[..., :n].reshape(lead + tuple(shp)))
        off += rows
    return out


def _set_slot(buf, piece, index):
    return lax.dynamic_update_slice_in_dim(buf, piece[None], index, axis=0)


def _from_shards(stacked, axis):
    return jnp.concatenate([stacked[j] for j in range(N_CHIPS)], axis=axis)


def _heads(a, h):
    t = a.shape[0]
    return a.reshape(t, h, a.shape[1] // h).transpose(1, 0, 2)


def _unheads(a):
    h, t, d = a.shape
    return a.transpose(1, 0, 2).reshape(t, h * d)


def _ffn_fwd(h, norm_w, w_up, conv_w, conv_b, w_down, tag):
    u = _rmsnorm_fwd(h, norm_w, name=f"ffn{tag}_norm")
    hid = _matmul(u, w_up, name=f"ffn{tag}_up")
    act = _conv_glu_fwd(hid, conv_w, conv_b, name=f"ffn{tag}_glu")
    out = _matmul(act, w_down, add=h, name=f"ffn{tag}_down")
    return out, (u, hid, act)


def _ffn_bwd(h, saved, dout, norm_w, w_up, conv_w, conv_b, w_down, tag):
    u, hid, act = saved
    dact = _matmul(dout, w_down, tb=True, name=f"ffn{tag}_down_dx")
    dw_down = _matmul(act, dout, ta=True, name=f"ffn{tag}_down_dw")
    dhid, dwg, dwv, dbg, dbv = _conv_glu_bwd(hid, conv_w, conv_b, dact, name=f"ffn{tag}_glu_bwd")
    du = _matmul(dhid, w_up, tb=True, name=f"ffn{tag}_up_dx")
    dw_up = _matmul(u, dhid, ta=True, out_parts=N_CHIPS, name=f"ffn{tag}_up_dw")
    dh, (dnorm,) = _rmsnorm_bwd(h, [(du, norm_w)], dout, name=f"ffn{tag}_norm_bwd")
    return dh, dict(norm=dnorm[0], up=dw_up, conv_w=jnp.concatenate([dwg, dwv], axis=1),
                    conv_b=jnp.concatenate([dbg, dbv], axis=1)[0], down=dw_down)


def _step(x, target, w):
    t = x.shape[0]
    g_n, heads = SSM_GROUPS, SSM_HEADS
    r_h = heads // g_n
    di = D_INNER
    zx_cols = di + CONV_DIM
    w_in = w['ssm_in_w'][0]
    w_zx = w_in[:, :zx_cols]
    w_dt = jnp.pad(w_in[:, zx_cols:], ((0, 0), (0, LANES - heads)))
    conv_w, conv_b = w['ssm_conv_w'][0], w['ssm_conv_b'][0]
    hp = jnp.stack([w['ssm_dt_bias'][0], w['ssm_a_log'][0], w['ssm_d'][0]], axis=0).reshape(3, g_n, r_h)
    hpc, hpr = hp.transpose(1, 0, 2), hp.transpose(1, 2, 0)
    w_out = w['ssm_out_w'][0]
    w_q, w_o = w['w_q'][0], w['w_o'][0]

    h0 = x
    u0 = _rmsnorm_fwd(h0, w['ssm_norm_w'][0], name="ssm_norm")
    zx = _matmul(u0, w_zx, name="ssm_in_zx")
    dt_raw = _matmul(u0, w_dt, name="ssm_in_dt")[:, :heads]
    dtg = dt_raw.reshape(t, g_n, r_h)
    dtc, dtr = dtg.transpose(1, 0, 2), dtg.transpose(1, 2, 0)
    xbc = _conv_silu_fwd(zx, conv_w, conv_b, x_off=di, name="ssm_conv")
    y, prev = _ssd_fwd(xbc, dtc, dtr, hpc, hpr, name="ssd_fwd")
    yn = _gate_norm_fwd(y, zx, w['ssm_gate_norm_w'][0], name="ssm_gate_norm")
    h1 = _matmul(yn, w_out, add=h0, name="ssm_out")
    h2, ffn0 = _ffn_fwd(h1, w['ffn_norm_w'][0], w['ffn_up_w'][0], w['ffn_conv_w'][0], w['ffn_conv_b'][0],
                        w['ffn_down_w'][0], 0)
    hk = _rmsnorm_fwd(h2, w['kv_norm_w'], name="kv_norm")
    qn = _rmsnorm_fwd(h2, w['attn_norm_w'][0], name="attn_norm")
    k2 = _matmul(hk, w['w_k'], out_dtype=BF16, name="attn_k")
    v2 = _matmul(hk, w['w_v'], out_dtype=BF16, name="attn_v")
    q2 = _matmul(qn, w_q, out_dtype=BF16, name="attn_q")
    qh, kh, vh = _heads(q2, SB_HEADS), _heads(k2, SB_HEADS), _heads(v2, SB_HEADS)
    oh, lt = _sb_fwd(qh, kh, vh, name="sb_fwd")
    o2 = _unheads(oh)
    h3 = _matmul(o2, w_o, add=h2, name="attn_o")
    h4, ffn1 = _ffn_fwd(h3, w['ffn_norm_w'][1], w['ffn_up_w'][1], w['ffn_conv_w'][1], w['ffn_conv_b'][1],
                        w['ffn_down_w'][1], 1)
    loss_p, dh4, d_final = _loss_head(h4, w['final_norm_w'], target, name="loss_head")

    dh3, g1 = _ffn_bwd(h3, ffn1, dh4, w['ffn_norm_w'][1], w['ffn_up_w'][1], w['ffn_conv_w'][1],
                       w['ffn_conv_b'][1], w['ffn_down_w'][1], 1)
    do2 = _matmul(dh3, w_o, tb=True, name="attn_o_dx")
    dw_o = _matmul(o2, dh3, ta=True, name="attn_o_dw")
    dqh, dkh, dvh = _sb_bwd(qh, kh, vh, lt, _heads(do2, SB_HEADS), name="sb_bwd")
    dq2, dk2, dv2 = _unheads(dqh), _unheads(dkh), _unheads(dvh)
    dqn = _matmul(dq2, w_q, tb=True, name="attn_q_dx")
    dw_q = _matmul(qn, dq2, ta=True, name="attn_q_dw")
    dhk = _matmul(dk2, w['w_k'], tb=True, name="attn_k_dx")
    dhk = _matmul(dv2, w['w_v'], tb=True, add=dhk, name="attn_v_dx")
    dw_k = _matmul(hk, dk2, ta=True, name="attn_k_dw")
    dw_v = _matmul(hk, dv2, ta=True, name="attn_v_dw")
    dh2, (d_attn_norm, d_kv_norm) = _rmsnorm_bwd(h2, [(dqn, w['attn_norm_w'][0]), (dhk, w['kv_norm_w'])], dh3,
                                                 name="attn_norms_bwd")
    dh1, g0 = _ffn_bwd(h1, ffn0, dh2, w['ffn_norm_w'][0], w['ffn_up_w'][0], w['ffn_conv_w'][0],
                       w['ffn_conv_b'][0], w['ffn_down_w'][0], 0)
    dyn = _matmul(dh1, w_out, tb=True, name="ssm_out_dx")
    dw_out = _matmul(yn, dh1, ta=True, name="ssm_out_dw")
    dy, dz, d_gate = _gate_norm_bwd(y, zx, w['ssm_gate_norm_w'][0], dyn, name="ssm_gate_norm_bwd")
    dxs, dbm, dcm, ddt_g, hg = _ssd_bwd(xbc, dtc, dtr, hpc, hpr, prev, dy, name="ssd_bwd")
    dxbc = jnp.concatenate([dxs, dbm, dcm], axis=1)
    dxbc_pre, d_conv_w, d_conv_b = _conv_silu_bwd(zx, conv_w, conv_b, dxbc, x_off=di, name="ssm_conv_bwd")
    dzx = jnp.concatenate([dz, dxbc_pre], axis=1)
    ddt = jnp.pad(ddt_g.transpose(1, 0, 2).reshape(t, heads), ((0, 0), (0, LANES - heads)))
    du0 = _matmul(dzx, w_zx, tb=True, name="ssm_in_zx_dx")
    du0 = _matmul(ddt, w_dt, tb=True, add=du0, name="ssm_in_dt_dx")
    dw_in = jnp.concatenate([_matmul(u0, dzx, ta=True, name="ssm_in_zx_dw"),
                             _matmul(u0, ddt, ta=True, name="ssm_in_dt_dw")[:, :heads]], axis=1)
    dx, (d_ssm_norm,) = _rmsnorm_bwd(h0, [(du0, w['ssm_norm_w'][0])], dh1, name="ssm_norm_bwd")

    hgr = hg.transpose(1, 0, 2).reshape(3, heads)
    grads = {
        'ssm_norm_w': d_ssm_norm, 'ssm_in_w': dw_in, 'ssm_conv_w': d_conv_w[None], 'ssm_conv_b': d_conv_b,
        'ssm_dt_bias': hgr[0:1], 'ssm_a_log': hgr[1:2], 'ssm_d': hgr[2:3], 'ssm_gate_norm_w': d_gate,
        'ssm_out_w': dw_out, 'kv_norm_w': d_kv_norm[0], 'w_k': dw_k, 'w_v': dw_v, 'attn_norm_w': d_attn_norm,
        'w_q': dw_q, 'w_o': dw_o, 'ffn_norm_w': jnp.stack([g0['norm'], g1['norm']]),
        'ffn_up_w': [g0['up'], g1['up']], 'ffn_conv_w': jnp.stack([g0['conv_w'], g1['conv_w']]),
        'ffn_conv_b': jnp.stack([g0['conv_b'], g1['conv_b']]), 'ffn_down_w': [g0['down'], g1['down']],
        'final_norm_w': d_final[0],
    }
    return loss_p, dx, grads


def kernel(x, ssm_norm_w, ssm_in_w, ssm_conv_w, ssm_conv_b, ssm_dt_bias, ssm_a_log, ssm_d, ssm_gate_norm_w, ssm_out_w, kv_norm_w, w_k, w_v, attn_norm_w, w_q, w_o, ffn_norm_w, ffn_up_w, ffn_conv_w, ffn_conv_b, ffn_down_w, final_norm_w, loss_target, m_ssm_norm_w, m_ssm_in_w, m_ssm_conv_w, m_ssm_conv_b, m_ssm_dt_bias, m_ssm_a_log, m_ssm_d, m_ssm_gate_norm_w, m_ssm_out_w, m_kv_norm_w, m_w_k, m_w_v, m_attn_norm_w, m_w_q, m_w_o, m_ffn_norm_w, m_ffn_up_w, m_ffn_conv_w, m_ffn_conv_b, m_ffn_down_w, m_final_norm_w, v_ssm_norm_w, v_ssm_in_w, v_ssm_conv_w, v_ssm_conv_b, v_ssm_dt_bias, v_ssm_a_log, v_ssm_d, v_ssm_gate_norm_w, v_ssm_out_w, v_kv_norm_w, v_w_k, v_w_v, v_attn_norm_w, v_w_q, v_w_o, v_ffn_norm_w, v_ffn_up_w, v_ffn_conv_w, v_ffn_conv_b, v_ffn_down_w, v_final_norm_w):
    args = (ssm_norm_w, ssm_in_w, ssm_conv_w, ssm_conv_b, ssm_dt_bias, ssm_a_log, ssm_d, ssm_gate_norm_w, ssm_out_w, kv_norm_w, w_k, w_v, attn_norm_w, w_q, w_o, ffn_norm_w, ffn_up_w, ffn_conv_w, ffn_conv_b, ffn_down_w, final_norm_w)
    moms = (m_ssm_norm_w, m_ssm_in_w, m_ssm_conv_w, m_ssm_conv_b, m_ssm_dt_bias, m_ssm_a_log, m_ssm_d, m_ssm_gate_norm_w, m_ssm_out_w, m_kv_norm_w, m_w_k, m_w_v, m_attn_norm_w, m_w_q, m_w_o, m_ffn_norm_w, m_ffn_up_w, m_ffn_conv_w, m_ffn_conv_b, m_ffn_down_w, m_final_norm_w)
    vels = (v_ssm_norm_w, v_ssm_in_w, v_ssm_conv_w, v_ssm_conv_b, v_ssm_dt_bias, v_ssm_a_log, v_ssm_d, v_ssm_gate_norm_w, v_ssm_out_w, v_kv_norm_w, v_w_k, v_w_v, v_attn_norm_w, v_w_q, v_w_o, v_ffn_norm_w, v_ffn_up_w, v_ffn_conv_w, v_ffn_conv_b, v_ffn_down_w, v_final_norm_w)
    local = dict(zip(WEIGHTS, args))
    m_in = dict(zip(WEIGHTS, moms))
    v_in = dict(zip(WEIGHTS, vels))
    c = lax.axis_index("c")
    chip = 2 * lax.axis_index("x") + lax.axis_index("y")

    pieces = []
    for n in BIG:
        blk = local[n]
        if n in STACKED:
            pieces += [(n, l, blk[l]) for l in range(blk.shape[0])]
        else:
            pieces.append((n, None, blk.reshape(blk.shape[-2:])))
    shards16 = [p.astype(BF16).reshape(2, p.shape[0] // 2, p.shape[1]) for _, _, p in pieces]
    gathered = _gather_chips_halves(shards16, name="gather_big")
    full = {n: local[n] for n in REPLICATED}
    for n in STACKED:
        full[n] = []
    for (n, l, p), s16, g in zip(pieces, shards16, gathered):
        by_chip = _set_slot(g, s16, chip).reshape((N_CHIPS,) + p.shape)
        if n == 'ssm_in_w':
            w_full = by_chip.transpose(1, 0, 2).reshape(p.shape[0], N_CHIPS * p.shape[1])
        elif n == 'ffn_up_w':
            w_full = by_chip
        else:
            w_full = by_chip.reshape(N_CHIPS * p.shape[0], p.shape[1])
        if n in STACKED:
            full[n].append(w_full)
        else:
            full[n] = w_full.reshape(local[n].shape[:-2] + w_full.shape)
    small32 = _gather_chips(_pack([local[n].reshape(-1) for n in SMALL], F32, 8), name="gather_small")
    for n, st in zip(SMALL, _unpack(small32, [local[n].shape for n in SMALL])):
        full[n] = _from_shards(st, SHARD_AXIS[n])

    loss_p, dx, grads = _step(x[0], loss_target[0], full)

    gs = []
    for n, l, p in pieces:
        g = grads[n] if l is None else grads[n][l]
        if n == 'ssm_in_w':
            g = g.reshape(p.shape[0], N_CHIPS, p.shape[1]).transpose(1, 0, 2)
        gs.append(g.reshape(N_CHIPS, 2, p.shape[0] // 2, p.shape[1]))
    recv = _swap_other_half(gs, name="rs_pair_swap")
    pairs = [_pair_add(g, r, c, out_dtype=BF16, name=f"rs_pair_add_{i}") for i, (g, r) in enumerate(zip(gs, recv))]
    scattered = _scatter_chips(pairs, name="rs_chip_scatter")
    halves = [_sum_leading(_set_slot(s, lax.dynamic_index_in_dim(pr, chip, axis=0, keepdims=False), chip),
                           name=f"rs_chip_sum_{i}") for i, (s, pr) in enumerate(zip(scattered, pairs))]
    joined = _join_halves(halves, name="rs_half_join")
    gshard = {n: [] for n in STACKED}
    for (n, l, p), h, j in zip(pieces, halves, joined):
        g = _set_slot(j, h, c).reshape(p.shape)
        if n in STACKED:
            gshard[n].append(g)
        else:
            gshard[n] = g.reshape(local[n].shape)

    small = SMALL + REPLICATED
    rep = _pack([loss_p.reshape(-1)] + [grads[n].reshape(-1) for n in small], F32, 8)
    tot = _sum_leading(_gather_all(rep, name="ar_gather"), name="ar_sum")
    parts = _unpack(tot, [(LANES,)] + [grads[n].shape for n in small])
    loss = jnp.sum(parts[0])
    for n, g in zip(small, parts[1:]):
        if n in SHARD_AXIS:
            size = local[n].shape[SHARD_AXIS[n]]
            g = lax.dynamic_slice_in_dim(g, chip * size, size, axis=SHARD_AXIS[n])
        gshard[n] = g

    grads_out, deltas, new_m, new_v = [], [], [], []
    for n in WEIGHTS:
        if n in STACKED:
            g, d, nm, nv = _adamw_layers(local[n], gshard[n], m_in[n], v_in[n], name=f"adamw_{n}")
        else:
            g = gshard[n]
            d, nm, nv = _adamw(local[n], g, m_in[n], v_in[n], name=f"adamw_{n}")
        grads_out.append(g)
        deltas.append(d)
        new_m.append(nm)
        new_v.append(nv)
    return (loss, dx[None], *grads_out, *deltas, *new_m, *new_v)
```

```python
import math

import jax
import jax.numpy as jnp
from jax import lax
from jax.experimental import pallas as pl
from jax.experimental.pallas import tpu as pltpu

D_MODEL = 1024
D_INNER = 2048
SSM_HEAD_DIM = 64
SSM_HEADS = 32
SSM_GROUPS = 4
SSM_STATE = 128
SSM_CONV = 4
SSM_CHUNK = 128
GN = SSM_GROUPS * SSM_STATE
CONV_DIM = D_INNER + 2 * GN
SB_HEADS = 16
SB_HEAD_DIM = 64
D_FF = 2816
FFN_CONV = 3
EPS = 1e-6
ADAM_LR = 0.001
ADAM_B1 = 0.9
ADAM_B2 = 0.999
ADAM_EPS = 1e-08
ADAM_WD = 0.01
ADAM_STEP = 10

LANES = 128
SUBLANES = 8
VMEM_LIMIT = 48 * 1024 * 1024
ADAM_BLOCK_BYTES = 1 << 20
F32 = jnp.float32
BF16 = jnp.bfloat16
MESH = pl.DeviceIdType.MESH


def _cparams(sem=None):
    return pltpu.CompilerParams(dimension_semantics=sem, vmem_limit_bytes=VMEM_LIMIT)


def _tile(n, cands):
    for c in cands:
        if n % c == 0:
            return c
    return n


def _nt(a, b):
    return lax.dot_general(a, b, (((1,), (1,)), ((), ())), preferred_element_type=F32)


def _tn(a, b):
    return lax.dot_general(a, b, (((0,), (0,)), ((), ())), preferred_element_type=F32)


def _nn(a, b):
    return jnp.dot(a, b, preferred_element_type=F32)


def _split(x, pieces):
    out = []
    for _ in range(pieces - 1):
        h = x.astype(BF16)
        out.append(h)
        x = x - h.astype(F32)
    out.append(x.astype(BF16))
    return out


def _ones_dot(ones, x, *, ones_left, pieces=3):
    o16 = ones.astype(BF16)
    acc = None
    for piece in _split(x, pieces):
        term = _nn(o16, piece) if ones_left else _nn(piece, o16)
        acc = term if acc is None else acc + term
    return acc


def _row_sums(x, pieces=3):
    return _ones_dot(jnp.ones((x.shape[1], LANES), F32), x, ones_left=False, pieces=pieces)


def _softplus(x):
    return jnp.maximum(x, 0.0) + jnp.log(1.0 + jnp.exp(-jnp.abs(x)))


def _sigmoid(x):
    e = jnp.exp(-jnp.abs(x))
    r = 1.0 / (1.0 + e)
    return jnp.where(x >= 0, r, e * r)


MM_TILE_MAX = 1408
MM_VMEM_BUDGET = 40 * 1024 * 1024


def _divisors(n, cap):
    out = [d for d in range(min(cap, n) // LANES * LANES, 0, -LANES) if n % d == 0]
    return out or [n]


def _mm_tiles(m, n, k, a_bytes, b_bytes, o_bytes, add_bytes):
    best = None
    for tm in _divisors(m, MM_TILE_MAX):
        for tn in _divisors(n, MM_TILE_MAX):
            for tk in _divisors(k, MM_TILE_MAX):
                vmem = 2 * (tm * tk * a_bytes + tk * tn * b_bytes + tm * tn * (o_bytes + add_bytes)) + tm * tn * 4
                if vmem > MM_VMEM_BUDGET:
                    continue
                score = (tm * tn * tk, tm * tn)
                if best is None or score > best[0]:
                    best = (score, (tm, tn, tk))
    return best[1]


def _matmul(a, b, *, ta=False, tb=False, add=None, out_dtype=F32, out_parts=1, name):
    a_parts = a.shape[0] if a.ndim == 3 else 1
    b_parts = b.shape[0] if b.ndim == 3 else 1
    assert not (ta and a_parts > 1)
    a2, b2 = a.shape[-2:], b.shape[-2:]
    m, k = (a2[1], a2[0]) if ta else (a2[0], a2[1] * a_parts)
    n, kb = (b2[0], b2[1] * b_parts) if tb else (b2[1] * b_parts, b2[0])
    assert kb == k, (a.shape, b.shape)
    n_unit = math.gcd(n // out_parts, n if tb else b2[1])
    k_unit = math.gcd(k // a_parts, b2[1] if tb else k)
    tm, tn, tk = _mm_tiles(m, n_unit, k_unit, a.dtype.itemsize, b.dtype.itemsize, jnp.dtype(out_dtype).itemsize,
                           0 if add is None else add.dtype.itemsize)
    nk = k // tk
    ka, kbp = (k // a_parts) // tk, (k // b_parts) // tk
    nb, no = (n // b_parts) // tn, (n // out_parts) // tn

    def body(*refs):
        if add is None:
            a_ref, b_ref, o_ref = refs[:3]
            add_ref = None
        else:
            a_ref, b_ref, add_ref, o_ref = refs[:4]
        kk = pl.program_id(2)
        dn = (((0 if ta else 1,), (1 if tb else 0,)), ((), ()))
        prod = lax.dot_general(a_ref[...].astype(BF16), b_ref[...].astype(BF16), dn, preferred_element_type=F32)

        def finish(r):
            if add_ref is not None:
                r = r + add_ref[...].astype(F32)
            o_ref[...] = r.astype(o_ref.dtype)

        if nk == 1:
            finish(prod)
            return
        acc_ref = refs[-1]

        @pl.when(kk == 0)
        def _():
            acc_ref[...] = prod

        @pl.when(jnp.logical_and(kk > 0, kk < nk - 1))
        def _():
            acc_ref[...] += prod

        @pl.when(kk == nk - 1)
        def _():
            finish(acc_ref[...] + prod)

    if ta:
        a_spec = pl.BlockSpec((tk, tm), lambda i, j, kk: (kk, i))
    elif a_parts > 1:
        a_spec = pl.BlockSpec((None, tm, tk), lambda i, j, kk: (kk // ka, i, kk % ka))
    else:
        a_spec = pl.BlockSpec((tm, tk), lambda i, j, kk: (i, kk))
    if b_parts == 1:
        b_spec = pl.BlockSpec((tn, tk), lambda i, j, kk: (j, kk)) if tb else pl.BlockSpec((tk, tn), lambda i, j, kk: (kk, j))
    elif tb:
        b_spec = pl.BlockSpec((None, tn, tk), lambda i, j, kk: (kk // kbp, j, kk % kbp))
    else:
        b_spec = pl.BlockSpec((None, tk, tn), lambda i, j, kk: (j // nb, kk, j % nb))
    if out_parts > 1:
        o_spec = pl.BlockSpec((None, tm, tn), lambda i, j, kk: (j // no, i, j % no))
        o_shape = jax.ShapeDtypeStruct((out_parts, m, n // out_parts), out_dtype)
    else:
        o_spec = pl.BlockSpec((tm, tn), lambda i, j, kk: (i, j))
        o_shape = jax.ShapeDtypeStruct((m, n), out_dtype)
    in_specs = [a_spec, b_spec]
    args = [a, b]
    if add is not None:
        in_specs.append(pl.BlockSpec((tm, tn), lambda i, j, kk: (i, j)))
        args.append(add)
    return pl.pallas_call(
        body,
        grid=(m // tm, n // tn, nk),
        in_specs=in_specs,
        out_specs=o_spec,
        out_shape=o_shape,
        scratch_shapes=[pltpu.VMEM((tm, tn), F32)] if nk > 1 else [],
        compiler_params=_cparams(("parallel", "parallel", "arbitrary")),
        name=name,
    )(*args)


def _rmsnorm_fwd(x, w, *, name):
    t, d = x.shape
    tb = _tile(t, (512, 256, 128))

    def body(x_ref, w_ref, o_ref):
        xv = x_ref[...]
        r = lax.rsqrt(jnp.mean(xv * xv, axis=-1, keepdims=True) + EPS)
        o_ref[...] = (xv * r * w_ref[...]).astype(o_ref.dtype)

    return pl.pallas_call(
        body,
        grid=(t // tb,),
        in_specs=[pl.BlockSpec((tb, d), lambda i: (i, 0)), pl.BlockSpec((1, d), lambda i: (0, 0))],
        out_specs=pl.BlockSpec((tb, d), lambda i: (i, 0)),
        out_shape=jax.ShapeDtypeStruct((t, d), BF16),
        compiler_params=_cparams(("parallel",)),
        name=name,
    )(x, w.reshape(1, d))


def _rmsnorm_bwd(x, dys, dres, *, name):
    t, d = x.shape
    tb = _tile(t, (256, 128))
    nn = len(dys)
    has_res = dres is not None

    def body(*refs):
        x_ref = refs[0]
        dy_refs = refs[1:1 + nn]
        w_refs = refs[1 + nn:1 + 2 * nn]
        pos = 1 + 2 * nn
        res_ref = refs[pos] if has_res else None
        pos += 1 if has_res else 0
        dx_ref = refs[pos]
        dw_refs = refs[pos + 1:pos + 1 + nn]
        i = pl.program_id(0)
        xv = x_ref[...]
        r = lax.rsqrt(jnp.mean(xv * xv, axis=-1, keepdims=True) + EPS)
        xn = xv * r
        dx = res_ref[...] if has_res else jnp.zeros_like(xv)
        for q in range(nn):
            dy = dy_refs[q][...].astype(F32)
            g = dy * w_refs[q][...]
            dx = dx + r * (g - xn * jnp.mean(g * xn, axis=-1, keepdims=True))
            dwp = jnp.sum(dy * xn, axis=0, keepdims=True)

            @pl.when(i == 0)
            def _(q=q, dwp=dwp):
                dw_refs[q][...] = dwp

            @pl.when(i > 0)
            def _(q=q, dwp=dwp):
                dw_refs[q][...] += dwp
        dx_ref[...] = dx

    row = pl.BlockSpec((tb, d), lambda i: (i, 0))
    vec = pl.BlockSpec((1, d), lambda i: (0, 0))
    in_specs = [row] + [row] * nn + [vec] * nn + ([row] if has_res else [])
    args = [x] + [p[0] for p in dys] + [p[1].reshape(1, d) for p in dys] + ([dres] if has_res else [])
    outs = pl.pallas_call(
        body,
        grid=(t // tb,),
        in_specs=in_specs,
        out_specs=[row] + [vec] * nn,
        out_shape=[jax.ShapeDtypeStruct((t, d), F32)] + [jax.ShapeDtypeStruct((1, d), F32)] * nn,
        compiler_params=_cparams(("arbitrary",)),
        name=name,
    )(*args)
    return outs[0], list(outs[1:])


def _loss_head(x, w, target, *, name):
    t, d = x.shape
    tb = _tile(t, (256, 128))

    def body(x_ref, w_ref, t_ref, loss_ref, dx_ref, dw_ref):
        i = pl.program_id(0)
        xv = x_ref[...]
        wv = w_ref[...]
        r = lax.rsqrt(jnp.mean(xv * xv, axis=-1, keepdims=True) + EPS)
        xn = xv * r
        e = xn * wv - t_ref[...]
        lp = 0.5 * jnp.sum(jnp.mean(e * e, axis=-1, keepdims=True), axis=0, keepdims=True)
        dy = e * (1.0 / d)
        g = dy * wv
        dx_ref[...] = r * (g - xn * jnp.mean(g * xn, axis=-1, keepdims=True))
        dwp = jnp.sum(dy * xn, axis=0, keepdims=True)
        lpv = jnp.broadcast_to(lp, (1, LANES)) * (1.0 / LANES)

        @pl.when(i == 0)
        def _():
            dw_ref[...] = dwp
            loss_ref[...] = lpv

        @pl.when(i > 0)
        def _():
            dw_ref[...] += dwp
            loss_ref[...] += lpv

    row = pl.BlockSpec((tb, d), lambda i: (i, 0))
    vec = pl.BlockSpec((1, d), lambda i: (0, 0))
    return pl.pallas_call(
        body,
        grid=(t // tb,),
        in_specs=[row, vec, row],
        out_specs=[pl.BlockSpec((1, LANES), lambda i: (0, 0)), row, vec],
        out_shape=[jax.ShapeDtypeStruct((1, LANES), F32), jax.ShapeDtypeStruct((t, d), F32),
                   jax.ShapeDtypeStruct((1, d), F32)],
        compiler_params=_cparams(("arbitrary",)),
        name=name,
    )(x, w.reshape(1, d), target)


ROW_CHUNK = 64
PAD = SUBLANES


def _shifted(pad_ref, r0, rows, back):
    return pad_ref[pl.ds(PAD + r0 - back, rows), :]


def _conv_taps(pad_ref, w_ref, r0, rows, kw):
    acc = None
    for j in range(kw):
        term = _shifted(pad_ref, r0, rows, kw - 1 - j) * w_ref[j:j + 1, :]
        acc = term if acc is None else acc + term
    return acc


def _fill_pad(pad_ref, x_ref, t):
    pad_ref[0:PAD, :] = jnp.zeros((PAD, pad_ref.shape[1]), F32)
    pad_ref[pl.ds(PAD + t, PAD), :] = jnp.zeros((PAD, pad_ref.shape[1]), F32)
    pad_ref[pl.ds(PAD, t), :] = x_ref[...].astype(F32)


def _conv_silu_fwd(x, w, b, *, x_off=0, name):
    t = x.shape[0]
    kw, c = w.shape
    cw = _tile(math.gcd(c, x_off) if x_off else c, (256, 128))
    ob = x_off // cw
    rc = _tile(t, (ROW_CHUNK,))

    def body(x_ref, w_ref, b_ref, o_ref, pad_ref):
        _fill_pad(pad_ref, x_ref, t)
        for r0 in range(0, t, rc):
            pre = _conv_taps(pad_ref, w_ref, r0, rc, kw) + b_ref[...]
            o_ref[pl.ds(r0, rc), :] = pre * _sigmoid(pre)

    strip = pl.BlockSpec((t, cw), lambda i: (0, i))
    return pl.pallas_call(
        body,
        grid=(c // cw,),
        in_specs=[pl.BlockSpec((t, cw), lambda i: (0, i + ob)), pl.BlockSpec((kw, cw), lambda i: (0, i)),
                  pl.BlockSpec((1, cw), lambda i: (0, i))],
        out_specs=strip,
        out_shape=jax.ShapeDtypeStruct((t, c), F32),
        scratch_shapes=[pltpu.VMEM((t + 2 * PAD, cw), F32)],
        compiler_params=_cparams(("parallel",)),
        name=name,
    )(x, w, b.reshape(1, c))


def _conv_bwd_core(dpre_pad_ref, x_pad_ref, w_ref, dx_ref, dw_ref, db_ref, t, rc, kw):
    cw = dx_ref.shape[1]

    def fold(a):
        return jnp.sum(a.reshape(rc // SUBLANES, SUBLANES, cw), axis=0) if rc % SUBLANES == 0 else jnp.sum(a, axis=0, keepdims=True)

    dws = [None] * kw
    dbs = None
    for r0 in range(0, t, rc):
        dpre = dpre_pad_ref[pl.ds(PAD + r0, rc), :]
        dx = None
        for j in range(kw):
            s = kw - 1 - j
            term = dpre_pad_ref[pl.ds(PAD + r0 + s, rc), :] * w_ref[j:j + 1, :]
            dx = term if dx is None else dx + term
            part = fold(dpre * _shifted(x_pad_ref, r0, rc, s))
            dws[j] = part if dws[j] is None else dws[j] + part
        part = fold(dpre)
        dbs = part if dbs is None else dbs + part
        dx_ref[pl.ds(r0, rc), :] = dx
    for j in range(kw):
        dw_ref[j:j + 1, :] = jnp.sum(dws[j], axis=0, keepdims=True)
    db_ref[...] = jnp.sum(dbs, axis=0, keepdims=True)


def _conv_silu_bwd(x, w, b, dact, *, x_off=0, name):
    t = x.shape[0]
    kw, c = w.shape
    cw = _tile(math.gcd(c, x_off) if x_off else c, (256, 128))
    ob = x_off // cw
    rc = _tile(t, (ROW_CHUNK,))

    def body(x_ref, w_ref, b_ref, da_ref, dx_ref, dw_ref, db_ref, xpad_ref, dpad_ref):
        _fill_pad(xpad_ref, x_ref, t)
        dpad_ref[0:PAD, :] = jnp.zeros((PAD, cw), F32)
        dpad_ref[pl.ds(PAD + t, PAD), :] = jnp.zeros((PAD, cw), F32)
        for r0 in range(0, t, rc):
            pre = _conv_taps(xpad_ref, w_ref, r0, rc, kw) + b_ref[...]
            sg = _sigmoid(pre)
            dpad_ref[pl.ds(PAD + r0, rc), :] = da_ref[pl.ds(r0, rc), :] * (sg * (1.0 + pre * (1.0 - sg)))
        _conv_bwd_core(dpad_ref, xpad_ref, w_ref, dx_ref, dw_ref, db_ref, t, rc, kw)

    strip = pl.BlockSpec((t, cw), lambda i: (0, i))
    wspec = pl.BlockSpec((kw, cw), lambda i: (0, i))
    bspec = pl.BlockSpec((1, cw), lambda i: (0, i))
    return pl.pallas_call(
        body,
        grid=(c // cw,),
        in_specs=[pl.BlockSpec((t, cw), lambda i: (0, i + ob)), wspec, bspec, strip],
        out_specs=[strip, wspec, bspec],
        out_shape=[jax.ShapeDtypeStruct((t, c), F32), jax.ShapeDtypeStruct((kw, c), F32),
                   jax.ShapeDtypeStruct((1, c), F32)],
        scratch_shapes=[pltpu.VMEM((t + 2 * PAD, cw), F32), pltpu.VMEM((t + 2 * PAD, cw), F32)],
        compiler_params=_cparams(("parallel",)),
        name=name,
    )(x, w, b.reshape(1, c), dact)


def _conv_glu_fwd(hid, w, b, *, name):
    t, c2 = hid.shape
    f = c2 // 2
    kw = w.shape[0]
    cw = _tile(f, (256, 128))
    nf = f // cw
    rc = _tile(t, (ROW_CHUNK,))

    def body(g_ref, v_ref, wg_ref, wv_ref, bg_ref, bv_ref, o_ref, gpad_ref, vpad_ref):
        _fill_pad(gpad_ref, g_ref, t)
        _fill_pad(vpad_ref, v_ref, t)
        for r0 in range(0, t, rc):
            gate = _conv_taps(gpad_ref, wg_ref, r0, rc, kw) + bg_ref[...]
            val = _conv_taps(vpad_ref, wv_ref, r0, rc, kw) + bv_ref[...]
            o_ref[pl.ds(r0, rc), :] = (gate * _sigmoid(gate) * val).astype(o_ref.dtype)

    gs = pl.BlockSpec((t, cw), lambda i: (0, i))
    vs = pl.BlockSpec((t, cw), lambda i: (0, i + nf))
    b2 = b.reshape(1, c2)
    return pl.pallas_call(
        body,
        grid=(nf,),
        in_specs=[gs, vs, pl.BlockSpec((kw, cw), lambda i: (0, i)), pl.BlockSpec((kw, cw), lambda i: (0, i + nf)),
                  pl.BlockSpec((1, cw), lambda i: (0, i)), pl.BlockSpec((1, cw), lambda i: (0, i + nf))],
        out_specs=gs,
        out_shape=jax.ShapeDtypeStruct((t, f), BF16),
        scratch_shapes=[pltpu.VMEM((t + 2 * PAD, cw), F32), pltpu.VMEM((t + 2 * PAD, cw), F32)],
        compiler_params=_cparams(("parallel",)),
        name=name,
    )(hid, hid, w, w, b2, b2)


def _conv_glu_bwd(hid, w, b, dact, *, name):
    t, c2 = hid.shape
    f = c2 // 2
    kw = w.shape[0]
    cw = _tile(f, (128,))
    nf = f // cw
    rc = _tile(t, (ROW_CHUNK,))

    def body(g_ref, v_ref, wg_ref, wv_ref, bg_ref, bv_ref, da_ref,
             dgv_ref, dwg_ref, dwv_ref, dbg_ref, dbv_ref,
             gpad_ref, vpad_ref, dgpad_ref, dvpad_ref):
        _fill_pad(gpad_ref, g_ref, t)
        _fill_pad(vpad_ref, v_ref, t)
        for ref in (dgpad_ref, dvpad_ref):
            ref[0:PAD, :] = jnp.zeros((PAD, cw), F32)
            ref[pl.ds(PAD + t, PAD), :] = jnp.zeros((PAD, cw), F32)
        for r0 in range(0, t, rc):
            gate = _conv_taps(gpad_ref, wg_ref, r0, rc, kw) + bg_ref[...]
            val = _conv_taps(vpad_ref, wv_ref, r0, rc, kw) + bv_ref[...]
            sg = _sigmoid(gate)
            da = da_ref[pl.ds(r0, rc), :].astype(F32)
            dgpad_ref[pl.ds(PAD + r0, rc), :] = da * val * (sg * (1.0 + gate * (1.0 - sg)))
            dvpad_ref[pl.ds(PAD + r0, rc), :] = da * (gate * sg)
        _conv_bwd_core(dgpad_ref, gpad_ref, wg_ref, dgv_ref.at[0], dwg_ref, dbg_ref, t, rc, kw)
        _conv_bwd_core(dvpad_ref, vpad_ref, wv_ref, dgv_ref.at[1], dwv_ref, dbv_ref, t, rc, kw)

    gs = pl.BlockSpec((t, cw), lambda i: (0, i))
    vs = pl.BlockSpec((t, cw), lambda i: (0, i + nf))
    wg = pl.BlockSpec((kw, cw), lambda i: (0, i))
    wv = pl.BlockSpec((kw, cw), lambda i: (0, i + nf))
    bg = pl.BlockSpec((1, cw), lambda i: (0, i))
    bv = pl.BlockSpec((1, cw), lambda i: (0, i + nf))
    b2 = b.reshape(1, c2)
    pad = pltpu.VMEM((t + 2 * PAD, cw), F32)
    return pl.pallas_call(
        body,
        grid=(nf,),
        in_specs=[gs, vs, wg, wv, bg, bv, gs],
        out_specs=[pl.BlockSpec((2, t, cw), lambda i: (0, 0, i)), wg, wg, bg, bg],
        out_shape=[jax.ShapeDtypeStruct((2, t, f), F32),
                   jax.ShapeDtypeStruct((kw, f), F32), jax.ShapeDtypeStruct((kw, f), F32),
                   jax.ShapeDtypeStruct((1, f), F32), jax.ShapeDtypeStruct((1, f), F32)],
        scratch_shapes=[pad, pad, pad, pad],
        compiler_params=_cparams(("parallel",)),
        name=name,
    )(hid, hid, w, w, b2, b2, dact)


def _gate_norm_fwd(y, zx, w, *, name):
    t, di = y.shape
    gsz = di // SSM_GROUPS
    tb = _tile(t, (256, 128))

    def body(y_ref, z_ref, w_ref, o_ref):
        for g in range(SSM_GROUPS):
            sl = slice(g * gsz, (g + 1) * gsz)
            zv = z_ref[:, sl]
            gv = y_ref[:, sl] * (zv * _sigmoid(zv))
            r = lax.rsqrt(jnp.mean(gv * gv, axis=-1, keepdims=True) + EPS)
            o_ref[:, sl] = (gv * r * w_ref[:, sl]).astype(o_ref.dtype)

    row = pl.BlockSpec((tb, di), lambda i: (i, 0))
    return pl.pallas_call(
        body,
        grid=(t // tb,),
        in_specs=[row, row, pl.BlockSpec((1, di), lambda i: (0, 0))],
        out_specs=row,
        out_shape=jax.ShapeDtypeStruct((t, di), BF16),
        compiler_params=_cparams(("parallel",)),
        name=name,
    )(y, zx, w.reshape(1, di))


def _gate_norm_bwd(y, zx, w, dyn, *, name):
    t, di = y.shape
    gsz = di // SSM_GROUPS
    tb = _tile(t, (256, 128))

    def body(y_ref, z_ref, w_ref, d_ref, dy_ref, dz_ref, dw_ref):
        i = pl.program_id(0)
        for g in range(SSM_GROUPS):
            sl = slice(g * gsz, (g + 1) * gsz)
            zv = z_ref[:, sl]
            yv = y_ref[:, sl]
            sg = _sigmoid(zv)
            sz = zv * sg
            gv = yv * sz
            r = lax.rsqrt(jnp.mean(gv * gv, axis=-1, keepdims=True) + EPS)
            gn = gv * r
            dn = d_ref[:, sl].astype(F32)
            q = dn * w_ref[:, sl]
            dg = r * (q - gn * jnp.mean(q * gn, axis=-1, keepdims=True))
            dy_ref[:, sl] = dg * sz
            dz_ref[:, sl] = dg * yv * (sg * (1.0 + zv * (1.0 - sg)))
            dwp = jnp.sum(dn * gn, axis=0, keepdims=True)

            @pl.when(i == 0)
            def _(sl=sl, dwp=dwp):
                dw_ref[:, sl] = dwp

            @pl.when(i > 0)
            def _(sl=sl, dwp=dwp):
                dw_ref[:, sl] += dwp

    row = pl.BlockSpec((tb, di), lambda i: (i, 0))
    vec = pl.BlockSpec((1, di), lambda i: (0, 0))
    return pl.pallas_call(
        body,
        grid=(t // tb,),
        in_specs=[row, row, vec, row],
        out_specs=[row, row, vec],
        out_shape=[jax.ShapeDtypeStruct((t, di), F32), jax.ShapeDtypeStruct((t, di), F32),
                   jax.ShapeDtypeStruct((1, di), F32)],
        compiler_params=_cparams(("arbitrary",)),
        name=name,
    )(y, zx, w.reshape(1, di), dyn)


def _adamw(w, g, m, v, *, name):
    shape = w.shape
    cols = shape[-1]
    rows = w.size // cols
    w2, g2, m2, v2 = (a.reshape(rows, cols) for a in (w, g, m, v))
    tr = rows
    if rows * cols * 4 > ADAM_BLOCK_BYTES:
        tr = _tile(rows, tuple(r for r in (512, 256, 128, 64, 32, 16, 8) if r * cols * 4 <= ADAM_BLOCK_BYTES))
    c1 = 1.0 - ADAM_B1 ** ADAM_STEP
    c2 = 1.0 - ADAM_B2 ** ADAM_STEP

    def body(w_ref, g_ref, m_ref, v_ref, d_ref, nm_ref, nv_ref):
        gv = g_ref[...]
        nm = ADAM_B1 * m_ref[...] + (1.0 - ADAM_B1) * gv
        nv = ADAM_B2 * v_ref[...] + (1.0 - ADAM_B2) * (gv * gv)
        d_ref[...] = -ADAM_LR * ((nm / c1) / (jnp.sqrt(nv / c2) + ADAM_EPS) + ADAM_WD * w_ref[...])
        nm_ref[...] = nm
        nv_ref[...] = nv

    blk = pl.BlockSpec((tr, cols), lambda i: (i, 0))
    outs = pl.pallas_call(
        body,
        grid=(rows // tr,),
        in_specs=[blk] * 4,
        out_specs=[blk] * 3,
        out_shape=[jax.ShapeDtypeStruct((rows, cols), F32)] * 3,
        compiler_params=_cparams(("parallel",)),
        name=name,
    )(w2, g2, m2, v2)
    return tuple(o.reshape(shape) for o in outs)


def _adamw_layers(w, gs, m, v, *, name):
    n_l, rows, cols = w.shape
    assert len(gs) == n_l
    tr = _tile(rows, tuple(r for r in (512, 256, 128, 64, 32, 16, 8) if r * cols * 4 <= ADAM_BLOCK_BYTES))
    c1 = 1.0 - ADAM_B1 ** ADAM_STEP
    c2 = 1.0 - ADAM_B2 ** ADAM_STEP

    def body(*refs):
        w_ref, m_ref, v_ref = refs[:3]
        g_refs = refs[3:3 + n_l]
        g_ref, d_ref, nm_ref, nv_ref = refs[3 + n_l:]
        layer = pl.program_id(0)
        gv = g_refs[0][...]
        for q in range(1, n_l):
            gv = jnp.where(layer == q, g_refs[q][...], gv)
        nm = ADAM_B1 * m_ref[...] + (1.0 - ADAM_B1) * gv
        nv = ADAM_B2 * v_ref[...] + (1.0 - ADAM_B2) * (gv * gv)
        g_ref[...] = gv
        d_ref[...] = -ADAM_LR * ((nm / c1) / (jnp.sqrt(nv / c2) + ADAM_EPS) + ADAM_WD * w_ref[...])
        nm_ref[...] = nm
        nv_ref[...] = nv

    stacked = pl.BlockSpec((None, tr, cols), lambda l, i: (l, i, 0))
    single = pl.BlockSpec((tr, cols), lambda l, i: (i, 0))
    return pl.pallas_call(
        body,
        grid=(n_l, rows // tr),
        in_specs=[stacked] * 3 + [single] * n_l,
        out_specs=[stacked] * 4,
        out_shape=[jax.ShapeDtypeStruct(w.shape, F32)] * 4,
        compiler_params=_cparams(("parallel", "parallel")),
        name=name,
    )(w, m, v, *gs)


def _ssd_scalars(dtc_ref, dtr_ref, hpc_ref, hpr_ref, ln):
    assert SSM_CHUNK == SSM_STATE == LANES, "the SSD kernels mix chunk, state and lane-wide tiles freely"
    bias_c, alog_c = hpc_ref[0, 0:1, :], hpc_ref[0, 1:2, :]
    bias_r, alog_r = hpr_ref[0, :, 0:1], hpr_ref[0, :, 1:2]
    a_c, a_r = -jnp.exp(alog_c), -jnp.exp(alog_r)
    raw_c = dtc_ref[0] + bias_c
    dt_c = _softplus(raw_c)
    dt_r = _softplus(dtr_ref[0] + bias_r)
    row = lax.broadcasted_iota(jnp.int32, (ln, ln), 0)
    col = lax.broadcasted_iota(jnp.int32, (ln, ln), 1)
    lower = (col <= row).astype(F32)
    upper = (row <= col).astype(F32)
    acs_c = _ones_dot(lower, dt_c * a_c, ones_left=True)
    acs_r = _ones_dot(upper, dt_r * a_r, ones_left=False)
    return raw_c, dt_c, a_c, acs_c, acs_r, row, col


def _ssd_specs(t, di, g_n, n_st, rp, ln, r_h, rev):
    nc = t // ln
    cidx = (lambda c: nc - 1 - c) if rev else (lambda c: c)
    xs = pl.BlockSpec((ln, rp), lambda g, c: (cidx(c), g))
    bm = pl.BlockSpec((ln, n_st), lambda g, c: (cidx(c), di // n_st + g))
    cm = pl.BlockSpec((ln, n_st), lambda g, c: (cidx(c), di // n_st + g_n + g))
    dtc = pl.BlockSpec((1, ln, r_h), lambda g, c: (g, cidx(c), 0))
    dtr = pl.BlockSpec((1, r_h, ln), lambda g, c: (g, 0, cidx(c)))
    hpc = pl.BlockSpec((1, 3, r_h), lambda g, c: (g, 0, 0))
    hpr = pl.BlockSpec((1, r_h, 3), lambda g, c: (g, 0, 0))
    prev = pl.BlockSpec((1, rp, n_st), lambda g, c: (cidx(c), g, 0))
    return xs, bm, cm, dtc, dtr, hpc, hpr, prev


def _ssd_fwd(xbc, dtc, dtr, hpc, hpr, *, name):
    t = xbc.shape[0]
    di, g_n, n_st, p_h, ln = D_INNER, SSM_GROUPS, SSM_STATE, SSM_HEAD_DIM, SSM_CHUNK
    r_h = SSM_HEADS // g_n
    rp = r_h * p_h
    nc = t // ln

    def body(xs_ref, b_ref, c_ref, dtc_ref, dtr_ref, hpc_ref, hpr_ref, y_ref, prev_ref, st_ref):
        @pl.when(pl.program_id(1) == 0)
        def _():
            st_ref[...] = jnp.zeros_like(st_ref)

        _, dt_c, _, acs_c, acs_r, row, col = _ssd_scalars(dtc_ref, dtr_ref, hpc_ref, hpr_ref, ln)
        bm = b_ref[...]
        cm = c_ref[...]
        cm16 = cm.astype(BF16)
        cb = _nt(cm16, bm.astype(BF16))
        causal = row >= col
        for r in range(r_h):
            sl = slice(r * p_h, (r + 1) * p_h)
            xs = xs_ref[:, sl]
            acs = jnp.broadcast_to(acs_c[:, r:r + 1], (ln, ln))
            last = acs[ln - 1:ln, :]
            lm = jnp.where(causal, jnp.exp(acs - acs_r[r:r + 1, :]), 0.0)
            xd = (xs * jnp.broadcast_to(dt_c[:, r:r + 1], (ln, p_h))).astype(BF16)
            prev = st_ref[sl, :]
            y = _nn((cb * lm).astype(BF16), xd)
            y = y + _nt(cm16, prev.astype(BF16)) * jnp.exp(acs[:, :p_h])
            y_ref[:, sl] = y + hpc_ref[0, 2:3, r:r + 1] * xs
            prev_ref[0, sl, :] = prev
            bd = (bm * jnp.exp(last - acs[:, :n_st])).astype(BF16)
            st_ref[sl, :] = prev * jnp.exp(last[:, :n_st]) + _tn(xd, bd)

    xs, bm, cm, dtcs, dtrs, hpcs, hprs, prev = _ssd_specs(t, di, g_n, n_st, rp, ln, r_h, False)
    return pl.pallas_call(
        body,
        grid=(g_n, nc),
        in_specs=[xs, bm, cm, dtcs, dtrs, hpcs, hprs],
        out_specs=[xs, prev],
        out_shape=[jax.ShapeDtypeStruct((t, di), F32), jax.ShapeDtypeStruct((nc, g_n * rp, n_st), F32)],
        scratch_shapes=[pltpu.VMEM((rp, n_st), F32)],
        compiler_params=_cparams(("parallel", "arbitrary")),
        name=name,
    )(xbc, xbc, xbc, dtc, dtr, hpc, hpr)


def _ssd_bwd(xbc, dtc, dtr, hpc, hpr, prev, dy, *, name):
    t = xbc.shape[0]
    di, g_n, n_st, p_h, ln = D_INNER, SSM_GROUPS, SSM_STATE, SSM_HEAD_DIM, SSM_CHUNK
    r_h = SSM_HEADS // g_n
    rp = r_h * p_h
    nc = t // ln

    def body(xs_ref, b_ref, c_ref, dtc_ref, dtr_ref, hpc_ref, hpr_ref, prev_ref, dy_ref,
             dxs_ref, db_ref, dc_ref, ddt_ref, hg_ref, ds_ref):
        step = pl.program_id(1)

        @pl.when(step == 0)
        def _():
            ds_ref[...] = jnp.zeros_like(ds_ref)

        raw_c, dt_c, a_c, acs_c, acs_r, row, col = _ssd_scalars(dtc_ref, dtr_ref, hpc_ref, hpr_ref, ln)
        bm = b_ref[...]
        cm = c_ref[...]
        bm16, cm16 = bm.astype(BF16), cm.astype(BF16)
        cb = _nt(cm16, bm16)
        cbt = _nt(bm16, cm16)
        lane_r = lax.broadcasted_iota(jnp.int32, (ln, r_h), 1)
        dacs_all = jnp.zeros((ln, r_h), F32)
        ddtx_all = jnp.zeros((ln, r_h), F32)
        dd_all = jnp.zeros((ln, r_h), F32)
        dcb = jnp.zeros((ln, ln), F32)
        dcbt = jnp.zeros((ln, ln), F32)
        dc_acc = jnp.zeros((ln, n_st), F32)
        db_acc = jnp.zeros((ln, n_st), F32)
        for r in range(r_h):
            sl = slice(r * p_h, (r + 1) * p_h)
            xs = xs_ref[:, sl]
            dyv = dy_ref[:, sl]
            dy16 = dyv.astype(BF16)
            acs = jnp.broadcast_to(acs_c[:, r:r + 1], (ln, ln))
            dtv = jnp.broadcast_to(dt_c[:, r:r + 1], (ln, p_h))
            acsr = acs_r[r:r + 1, :]
            last = acs[ln - 1:ln, :]
            xd = xs * dtv
            xd16 = xd.astype(BF16)
            lm = jnp.where(row >= col, jnp.exp(acs - acsr), 0.0)
            lmt = jnp.where(col >= row, jnp.exp(acsr - acs), 0.0)
            m_ls = cb * lm
            m_sl = cbt * lmt
            dm = _nt(dy16, xd16)
            dmt = _nt(xd16, dy16)
            dxd = _nn(m_sl.astype(BF16), dy16)
            dacs = _row_sums(dm * m_ls - dmt * m_sl)
            dcb = dcb + dm * lm
            dcbt = dcbt + dmt * lmt
            prev = prev_ref[0, sl, :]
            prev16 = prev.astype(BF16)
            e = jnp.exp(acs[:, :p_h])
            y_off = _nt(cm16, prev16) * e
            dacs = dacs + _row_sums(dyv * y_off)
            dyo16 = (dyv * e).astype(BF16)
            dc_acc = dc_acc + _nn(dyo16, prev16)
            dprev = _tn(dyo16, cm16)
            ds = ds_ref[sl, :]
            ds16 = ds.astype(BF16)
            decay = jnp.exp(last - acs)[:, :n_st]
            bd16 = (bm * decay).astype(BF16)
            dbd = _nn(xd16, ds16)
            dxd = dxd + _nt(bd16, ds16)
            db_acc = db_acc + dbd * decay
            tdec = _row_sums(dbd * bm, 2) * decay
            cd = jnp.exp(last)
            dlast = (jnp.sum(tdec, axis=0, keepdims=True)
                     + jnp.sum(_row_sums(prev * ds, 2), axis=0, keepdims=True) * cd)
            ds_ref[sl, :] = dprev + cd[:, :n_st] * ds
            dskip = hpc_ref[0, 2:3, r:r + 1]
            dxs_ref[:, sl] = dxd * dtv + dskip * dyv
            dacs = dacs - tdec + jnp.where(row == ln - 1, dlast, 0.0)
            dacs_all = jnp.where(lane_r == r, dacs[:, :r_h], dacs_all)
            ddtx_all = jnp.where(lane_r == r, _row_sums(dxd * xs, 2)[:, :r_h], ddtx_all)
            dd_all = jnp.where(lane_r == r, _row_sums(dyv * xs, 2)[:, :r_h], dd_all)
        dc_ref[...] = dc_acc + _nn(dcb.astype(BF16), bm16)
        db_ref[...] = db_acc + _nn(dcbt.astype(BF16), cm16)
        upper = (row <= col).astype(F32)
        dad = _ones_dot(upper, dacs_all, ones_left=True)
        ddt = dad * a_c + ddtx_all
        ddt_raw = ddt * _sigmoid(raw_c)
        ddt_ref[0] = ddt_raw
        d_bias = jnp.sum(ddt_raw, axis=0, keepdims=True)
        d_alog = jnp.sum(dad * dt_c, axis=0, keepdims=True) * a_c
        d_d = jnp.sum(dd_all, axis=0, keepdims=True)
        hg = jnp.concatenate([d_bias, d_alog, d_d], axis=0)

        @pl.when(step == 0)
        def _():
            hg_ref[0] = hg

        @pl.when(step > 0)
        def _():
            hg_ref[0] += hg

    xs, bms, cms, dtcs, dtrs, hpcs, hprs, prevs = _ssd_specs(t, di, g_n, n_st, rp, ln, r_h, True)
    bout = pl.BlockSpec((ln, n_st), lambda g, c: (nc - 1 - c, g))
    return pl.pallas_call(
        body,
        grid=(g_n, nc),
        in_specs=[xs, bms, cms, dtcs, dtrs, hpcs, hprs, prevs, xs],
        out_specs=[xs, bout, bout, dtcs, hpcs],
        out_shape=[jax.ShapeDtypeStruct((t, di), F32), jax.ShapeDtypeStruct((t, g_n * n_st), F32),
                   jax.ShapeDtypeStruct((t, g_n * n_st), F32), jax.ShapeDtypeStruct((g_n, t, r_h), F32),
                   jax.ShapeDtypeStruct((g_n, 3, r_h), F32)],
        scratch_shapes=[pltpu.VMEM((rp, n_st), F32)],
        compiler_params=_cparams(("parallel", "arbitrary")),
        name=name,
    )(xbc, xbc, xbc, dtc, dtr, hpc, hpr, prev, dy)


SB_KEYS = 256
SB_QUERIES = (512, 256)
SB_CUTOFF = 110.0
SB_PIECES = 2


def _sb_logits(qs, kv, valid):
    z = _nt(qs, kv)
    nz = -z
    lg = jnp.minimum(nz, 0.0) - jnp.log(1.0 + jnp.exp(jnp.minimum(z, nz)))
    return z + lg, (lg if valid is None else jnp.where(valid, lg, 0.0))


def _sb_iota(tq):
    diff = lax.broadcasted_iota(jnp.int32, (tq, SB_KEYS), 1) - lax.broadcasted_iota(jnp.int32, (tq, SB_KEYS), 0)
    krow = lax.broadcasted_iota(jnp.int32, (SB_KEYS, SB_KEYS), 0)
    kcol = lax.broadcasted_iota(jnp.int32, (SB_KEYS, SB_KEYS), 1)
    return diff, krow, kcol


def _sb_scale(d):
    scale = 1.0 / math.sqrt(d)
    assert math.frexp(scale)[0] == 0.5, "the scale is folded into bf16 queries: it must be a power of two"
    return scale


def _key_rows(j):
    return pl.ds(pl.multiple_of(j * SB_KEYS, SB_KEYS), SB_KEYS)


def _sb_fwd(q, k, v, *, name):
    h, t, d = q.shape
    tq = _tile(t, SB_QUERIES)
    nq = t // tq
    kpq = tq // SB_KEYS
    scale = _sb_scale(d)

    def body(q_ref, k_ref, v_ref, o_ref, lt_ref, first_ref):
        i = pl.program_id(1)
        qs = (q_ref[0].astype(F32) * scale).astype(BF16)
        diff, krow, kcol = _sb_iota(tq)
        later = (krow > kcol).astype(F32)

        def block(j, carry, valid):
            acc, cl = carry
            rows = _key_rows(j)
            ls, lg = _sb_logits(qs, k_ref[0, rows, :], valid)
            cs = _ones_dot(later, lg, ones_left=False, pieces=SB_PIECES)
            att = jnp.exp(ls + (cs + cl))
            if valid is not None:
                att = jnp.where(valid, att, 0.0)
            acc = acc + _nn(att.astype(BF16), v_ref[0, rows, :])
            return acc, cl + (cs[:, 0:1] + lg[:, 0:1])

        carry = (jnp.zeros((tq, d), F32), jnp.zeros((tq, 1), F32))
        for m in range(kpq - 1, -1, -1):
            carry = block(i * kpq + m, carry, diff < -m * SB_KEYS)
        nb = i * kpq

        def more(st):
            s, _, cl = st
            return jnp.logical_and(s < nb, jnp.max(cl) > -SB_CUTOFF)

        def step(st):
            s, acc, cl = st
            acc, cl = block(nb - 1 - s, (acc, cl), None)
            return s + 1, acc, cl

        walked, acc, cl = lax.while_loop(more, step, (jnp.int32(0),) + carry)
        o_ref[0] = acc
        lt_ref[0] = cl
        first_ref[pl.program_id(0), i] = nb - walked

    qs = pl.BlockSpec((1, tq, d), lambda hh, i: (hh, i, 0))
    ls = pl.BlockSpec((1, tq, 1), lambda hh, i: (hh, i, 0))
    ks = pl.BlockSpec((1, t, d), lambda hh, i: (hh, 0, 0))
    return pl.pallas_call(
        body,
        grid=(h, nq),
        in_specs=[qs, ks, ks],
        out_specs=[qs, ls, pl.BlockSpec(memory_space=pltpu.SMEM)],
        out_shape=[jax.ShapeDtypeStruct((h, t, d), F32), jax.ShapeDtypeStruct((h, t, 1), F32),
                   jax.ShapeDtypeStruct((h, nq), jnp.int32)],
        compiler_params=_cparams(("arbitrary", "arbitrary")),
        name=name,
    )(q, k, v)


def _sb_bwd(q, k, v, lt, first, do, *, name):
    h, t, d = q.shape
    tq = _tile(t, SB_QUERIES)
    nq = t // tq
    kpq = tq // SB_KEYS
    scale = _sb_scale(d)
    last = SB_KEYS - 1

    def body(q_ref, k_ref, v_ref, lt_ref, first_ref, do_ref, dq_ref, dk_ref, dv_ref):
        i = pl.program_id(1)

        @pl.when(i == 0)
        def _():
            dk_ref[...] = jnp.zeros_like(dk_ref)
            dv_ref[...] = jnp.zeros_like(dv_ref)

        qs = (q_ref[0].astype(F32) * scale).astype(BF16)
        do16 = do_ref[0].astype(BF16)
        ltot = lt_ref[0]
        diff, krow, kcol = _sb_iota(tq)
        upto = (krow <= kcol).astype(F32)
        before = (krow < kcol).astype(F32)

        def block(j, carry, valid):
            dq, pl_sum, pg_sum = carry
            rows = _key_rows(j)
            kv = k_ref[0, rows, :]
            vv = v_ref[0, rows, :]
            ls, lg = _sb_logits(qs, kv, valid)
            pre = _ones_dot(upto, lg, ones_left=False, pieces=SB_PIECES)
            att = jnp.exp(ls + (ltot - (pre + pl_sum)))
            if valid is not None:
                att = jnp.where(valid, att, 0.0)
            g = att * _nt(do16, vv)
            gpre = _ones_dot(before, g, ones_left=False, pieces=SB_PIECES)
            sig = jnp.exp(ls)
            dz16 = (g - sig * (g + (gpre + pg_sum))).astype(BF16)
            if valid is not None:
                dz16 = jnp.where(valid, dz16, jnp.zeros_like(dz16))
            dq = dq + _nn(dz16, kv)
            dk_ref[0, rows, :] += _tn(dz16, qs)
            dv_ref[0, rows, :] += _tn(att.astype(BF16), do16)
            return dq, pl_sum + pre[:, last:], pg_sum + (gpre[:, last:] + g[:, last:])

        zero = jnp.zeros((tq, 1), F32)
        nb = i * kpq
        start = jnp.clip(first_ref[pl.program_id(0), i], 0, nb)
        carry = lax.fori_loop(start, nb, lambda j, cr: block(j, cr, None), (jnp.zeros((tq, d), F32), zero, zero))
        for m in range(kpq):
            carry = block(nb + m, carry, diff < -m * SB_KEYS)
        dq_ref[0] = carry[0] * scale

    qs = pl.BlockSpec((1, tq, d), lambda hh, i: (hh, i, 0))
    ls = pl.BlockSpec((1, tq, 1), lambda hh, i: (hh, i, 0))
    ks = pl.BlockSpec((1, t, d), lambda hh, i: (hh, 0, 0))
    full = jax.ShapeDtypeStruct((h, t, d), F32)
    return pl.pallas_call(
        body,
        grid=(h, nq),
        in_specs=[qs, ks, ks, ls, pl.BlockSpec(memory_space=pltpu.SMEM), qs],
        out_specs=[qs, ks, ks],
        out_shape=[full, full, full],
        compiler_params=_cparams(("arbitrary", "arbitrary")),
        name=name,
    )(q, k, v, lt, first, do)


def _row_tile(rows, cols):
    return _tile(rows, tuple(r for r in (2048, 1024, 512, 256, 128, 64, 32, 16, 8) if r * cols * 4 <= ADAM_BLOCK_BYTES))


def _sum_leading(x, *, name):
    n, rows, cols = x.shape
    tr = _row_tile(rows, cols)

    def body(x_ref, o_ref):
        acc = x_ref[0].astype(F32)
        for q in range(1, n):
            acc = acc + x_ref[q].astype(F32)
        o_ref[...] = acc

    return pl.pallas_call(
        body,
        grid=(rows // tr,),
        in_specs=[pl.BlockSpec((n, tr, cols), lambda i: (0, i, 0))],
        out_specs=pl.BlockSpec((tr, cols), lambda i: (i, 0)),
        out_shape=jax.ShapeDtypeStruct((rows, cols), F32),
        compiler_params=_cparams(("parallel",)),
        name=name,
    )(x)


def _pair_add(g4h, recv, c, *, out_dtype, name):
    n, _, rows, cols = g4h.shape
    tr = _row_tile(rows, cols)

    def body(c_ref, g_ref, r_ref, o_ref):
        o_ref[...] = (g_ref[...] + r_ref[...]).astype(o_ref.dtype)

    blk = pl.BlockSpec((1, tr, cols), lambda q, i, c_ref: (q, i, 0))
    return pl.pallas_call(
        body,
        grid_spec=pltpu.PrefetchScalarGridSpec(
            num_scalar_prefetch=1,
            grid=(n, rows // tr),
            in_specs=[pl.BlockSpec((1, None, tr, cols), lambda q, i, c_ref: (q, c_ref[0], i, 0)), blk],
            out_specs=blk),
        out_shape=jax.ShapeDtypeStruct((n, rows, cols), out_dtype),
        compiler_params=_cparams(("parallel", "parallel")),
        name=name,
    )(c.reshape(1).astype(jnp.int32), g4h, recv)


ANY = pl.BlockSpec(memory_space=pl.ANY)


def _other_chips(x, y):
    return [(1 - x, y), (x, 1 - y), (1 - x, 1 - y)]


def _gather_chips(shard, *, name):
    def body(x_ref, o_ref, send_sems, recv_sems, local_sem):
        x, y, c = lax.axis_index("x"), lax.axis_index("y"), lax.axis_index("c")
        me = 2 * x + y
        mine = pltpu.make_async_copy(x_ref, o_ref.at[me], local_sem)
        mine.start()
        chips = _other_chips(x, y)
        sends = [pltpu.make_async_remote_copy(src_ref=x_ref, dst_ref=o_ref.at[me], send_sem=send_sems.at[q],
                                              recv_sem=recv_sems.at[q], device_id=(px, py, c), device_id_type=MESH)
                 for q, (px, py) in enumerate(chips)]
        for cp in sends:
            cp.start()
        for q, (px, py) in enumerate(chips):
            pltpu.make_async_remote_copy(src_ref=x_ref, dst_ref=o_ref.at[2 * px + py], send_sem=send_sems.at[q],
                                         recv_sem=recv_sems.at[q], device_id=(px, py, c), device_id_type=MESH).wait_recv()
        for cp in sends:
            cp.wait_send()
        mine.wait()

    return pl.pallas_call(
        body,
        in_specs=[ANY],
        out_specs=ANY,
        out_shape=jax.ShapeDtypeStruct((4,) + shard.shape, shard.dtype),
        scratch_shapes=[pltpu.SemaphoreType.DMA((3,)), pltpu.SemaphoreType.DMA((3,)), pltpu.SemaphoreType.DMA],
        compiler_params=pltpu.CompilerParams(has_side_effects=True),
        name=name,
    )(shard)


def _comm_call(body, ins, out_shapes, n_sems, name):
    n = len(ins)

    def wrapped(*refs):
        body(refs[:n], refs[n:n + len(out_shapes)], refs[-2], refs[-1])

    return pl.pallas_call(
        wrapped,
        in_specs=[ANY] * n,
        out_specs=[ANY] * len(out_shapes),
        out_shape=out_shapes,
        scratch_shapes=[pltpu.SemaphoreType.DMA((n_sems,)), pltpu.SemaphoreType.DMA((n_sems,))],
        compiler_params=pltpu.CompilerParams(has_side_effects=True),
        name=name,
    )(*ins)


def _remote(send_sems, recv_sems, q, src, dst, to):
    return pltpu.make_async_remote_copy(src_ref=src, dst_ref=dst, send_sem=send_sems.at[q], recv_sem=recv_sems.at[q],
                                        device_id=to, device_id_type=MESH)


def _scatter_chips(parts, *, name):
    def body(ins, outs, send_sems, recv_sems):
        x, y, c = lax.axis_index("x"), lax.axis_index("y"), lax.axis_index("c")
        me = 2 * x + y
        chips = _other_chips(x, y)
        sends = [_remote(send_sems, recv_sems, 3 * i + q, p.at[2 * px + py], o.at[me], (px, py, c))
                 for i, (p, o) in enumerate(zip(ins, outs)) for q, (px, py) in enumerate(chips)]
        for cp in sends:
            cp.start()
        for i, (p, o) in enumerate(zip(ins, outs)):
            for q, (px, py) in enumerate(chips):
                _remote(send_sems, recv_sems, 3 * i + q, p.at[me], o.at[2 * px + py], (px, py, c)).wait_recv()
        for cp in sends:
            cp.wait_send()

    return _comm_call(body, parts, [jax.ShapeDtypeStruct(p.shape, p.dtype) for p in parts], 3 * len(parts), name)


def _gather_chips_halves(shards, *, name):
    def body(ins, outs, send_sems, recv_sems):
        x, y, c = lax.axis_index("x"), lax.axis_index("y"), lax.axis_index("c")
        me, sibling = 2 * x + y, (x, y, 1 - c)
        chips = _other_chips(x, y)
        copy = lambda q, src, dst, to: _remote(send_sems, recv_sems, q, src, dst, to)
        sends = [copy(6 * i + q, s.at[c], o.at[me, c], (px, py, c))
                 for i, (s, o) in enumerate(zip(ins, outs)) for q, (px, py) in enumerate(chips)]
        for cp in sends:
            cp.start()
        passed = []
        for i, (s, o) in enumerate(zip(ins, outs)):
            for q, (px, py) in enumerate(chips):
                slot = o.at[2 * px + py, c]
                copy(6 * i + q, s.at[c], slot, (px, py, c)).wait_recv()
                passed.append(copy(6 * i + 3 + q, slot, slot, sibling))
                passed[-1].start()
        for i, (s, o) in enumerate(zip(ins, outs)):
            for q, (px, py) in enumerate(chips):
                copy(6 * i + 3 + q, s.at[1 - c], o.at[2 * px + py, 1 - c], sibling).wait_recv()
        for cp in sends + passed:
            cp.wait_send()

    return _comm_call(body, shards, [jax.ShapeDtypeStruct((N_CHIPS,) + s.shape, s.dtype) for s in shards],
                      6 * len(shards), name)


def _swap_other_half(gs, *, name):
    def body(ins, outs, send_sems, recv_sems):
        x, y, c = lax.axis_index("x"), lax.axis_index("y"), lax.axis_index("c")
        copies = [_remote(send_sems, recv_sems, i, g.at[pl.ds(0, g.shape[0]), 1 - c], o, (x, y, 1 - c))
                  for i, (g, o) in enumerate(zip(ins, outs))]
        for cp in copies:
            cp.start()
        for cp in copies:
            cp.wait()

    return _comm_call(body, gs, [jax.ShapeDtypeStruct((g.shape[0],) + g.shape[2:], g.dtype) for g in gs], len(gs), name)


def _join_halves(halves, *, name):
    def body(ins, outs, send_sems, recv_sems):
        x, y, c = lax.axis_index("x"), lax.axis_index("y"), lax.axis_index("c")
        sibling = (x, y, 1 - c)
        sends = [_remote(send_sems, recv_sems, i, h, o.at[c], sibling) for i, (h, o) in enumerate(zip(ins, outs))]
        for cp in sends:
            cp.start()
        for i, (h, o) in enumerate(zip(ins, outs)):
            _remote(send_sems, recv_sems, i, h, o.at[1 - c], sibling).wait_recv()
        for cp in sends:
            cp.wait_send()

    return _comm_call(body, halves, [jax.ShapeDtypeStruct((2,) + h.shape, h.dtype) for h in halves], len(halves), name)


def _gather_all(v, *, name):
    def body(v_ref, o_ref, send_sems, recv_sems, local_sem):
        x, y, c = lax.axis_index("x"), lax.axis_index("y"), lax.axis_index("c")
        me = 4 * x + 2 * y + c
        mine = pltpu.make_async_copy(v_ref, o_ref.at[me], local_sem)
        mine.start()
        peers = [(x ^ (q >> 2 & 1), y ^ (q >> 1 & 1), c ^ (q & 1)) for q in range(1, 8)]
        sends = [pltpu.make_async_remote_copy(src_ref=v_ref, dst_ref=o_ref.at[me], send_sem=send_sems.at[q],
                                              recv_sem=recv_sems.at[q], device_id=peer, device_id_type=MESH)
                 for q, peer in enumerate(peers)]
        for cp in sends:
            cp.start()
        for q, (px, py, pc) in enumerate(peers):
            pltpu.make_async_remote_copy(src_ref=v_ref, dst_ref=o_ref.at[4 * px + 2 * py + pc], send_sem=send_sems.at[q],
                                         recv_sem=recv_sems.at[q], device_id=(px, py, pc), device_id_type=MESH).wait_recv()
        for cp in sends:
            cp.wait_send()
        mine.wait()

    return pl.pallas_call(
        body,
        in_specs=[ANY],
        out_specs=ANY,
        out_shape=jax.ShapeDtypeStruct((8,) + v.shape, v.dtype),
        scratch_shapes=[pltpu.SemaphoreType.DMA((7,)), pltpu.SemaphoreType.DMA((7,)), pltpu.SemaphoreType.DMA],
        compiler_params=pltpu.CompilerParams(has_side_effects=True),
        name=name,
    )(v)


WEIGHTS = ['ssm_norm_w', 'ssm_in_w', 'ssm_conv_w', 'ssm_conv_b', 'ssm_dt_bias', 'ssm_a_log', 'ssm_d',
           'ssm_gate_norm_w', 'ssm_out_w', 'kv_norm_w', 'w_k', 'w_v', 'attn_norm_w', 'w_q', 'w_o',
           'ffn_norm_w', 'ffn_up_w', 'ffn_conv_w', 'ffn_conv_b', 'ffn_down_w', 'final_norm_w']
SHARD_AXIS = {'ssm_norm_w': 1, 'ssm_in_w': 2, 'ssm_conv_w': 2, 'ssm_conv_b': 1, 'ssm_gate_norm_w': 1,
              'ssm_out_w': 1, 'w_k': 0, 'w_v': 0, 'w_q': 1, 'w_o': 1, 'ffn_up_w': 2, 'ffn_conv_w': 2,
              'ffn_down_w': 1}
BIG = ['ssm_in_w', 'ssm_out_w', 'w_k', 'w_v', 'w_q', 'w_o', 'ffn_up_w', 'ffn_down_w']
SMALL = [n for n in WEIGHTS if n in SHARD_AXIS and n not in BIG]
REPLICATED = [n for n in WEIGHTS if n not in SHARD_AXIS]
STACKED = ['ffn_up_w', 'ffn_down_w']
N_CHIPS = 4


PACK_ROWS = 16


def _piece_rows(n):
    return -(-n // (PACK_ROWS * LANES)) * PACK_ROWS


def _pack(arrs, dtype, row_mult):
    lead = arrs[0].shape[:-1]
    pieces, total = [], 0
    for a in arrs:
        n = a.shape[-1]
        rows = _piece_rows(n)
        a = a.astype(dtype)
        if rows * LANES != n:
            a = jnp.pad(a, [(0, 0)] * len(lead) + [(0, rows * LANES - n)])
        pieces.append(a.reshape(lead + (rows, LANES)))
        total += rows
    extra = -total % row_mult
    if extra:
        pieces.append(jnp.zeros(lead + (extra, LANES), dtype))
    return jnp.concatenate(pieces, axis=len(lead))


def _unpack(buf, shapes):
    lead = buf.shape[:-2]
    out, off = [], 0
    for shp in shapes:
        n = math.prod(shp)
        rows = _piece_rows(n)
        piece = lax.slice_in_dim(buf, off, off + rows, axis=len(lead)).reshape(lead + (rows * LANES,))
        out.append(piece[..., :n].reshape(lead + tuple(shp)))
        off += rows
    return out


def _set_slot(buf, piece, index):
    return lax.dynamic_update_slice_in_dim(buf, piece[None], index, axis=0)


def _from_shards(stacked, axis):
    return jnp.concatenate([stacked[j] for j in range(N_CHIPS)], axis=axis)


def _heads(a, h):
    t = a.shape[0]
    return a.reshape(t, h, a.shape[1] // h).transpose(1, 0, 2)


def _unheads(a):
    h, t, d = a.shape
    return a.transpose(1, 0, 2).reshape(t, h * d)


def _ffn_fwd(h, norm_w, w_up, conv_w, conv_b, w_down, tag):
    u = _rmsnorm_fwd(h, norm_w, name=f"ffn{tag}_norm")
    hid = _matmul(u, w_up, name=f"ffn{tag}_up")
    act = _conv_glu_fwd(hid, conv_w, conv_b, name=f"ffn{tag}_glu")
    out = _matmul(act, w_down, add=h, name=f"ffn{tag}_down")
    return out, (u, hid, act)


def _ffn_bwd(h, saved, dout, norm_w, w_up, conv_w, conv_b, w_down, tag):
    u, hid, act = saved
    dact = _matmul(dout, w_down, tb=True, name=f"ffn{tag}_down_dx")
    dw_down = _matmul(act, dout, ta=True, name=f"ffn{tag}_down_dw")
    dhid, dwg, dwv, dbg, dbv = _conv_glu_bwd(hid, conv_w, conv_b, dact, name=f"ffn{tag}_glu_bwd")
    du = _matmul(dhid, w_up, tb=True, name=f"ffn{tag}_up_dx")
    dw_up = _matmul(u, dhid, ta=True, out_parts=N_CHIPS, name=f"ffn{tag}_up_dw")
    dh, (dnorm,) = _rmsnorm_bwd(h, [(du, norm_w)], dout, name=f"ffn{tag}_norm_bwd")
    return dh, dict(norm=dnorm[0], up=dw_up, conv_w=jnp.concatenate([dwg, dwv], axis=1),
                    conv_b=jnp.concatenate([dbg, dbv], axis=1)[0], down=dw_down)


def _step(x, target, w):
    t = x.shape[0]
    g_n, heads = SSM_GROUPS, SSM_HEADS
    r_h = heads // g_n
    di = D_INNER
    zx_cols = di + CONV_DIM
    w_in = w['ssm_in_w'][0]
    w_zx = w_in[:, :zx_cols]
    w_dt = jnp.pad(w_in[:, zx_cols:], ((0, 0), (0, LANES - heads)))
    conv_w, conv_b = w['ssm_conv_w'][0], w['ssm_conv_b'][0]
    hp = jnp.stack([w['ssm_dt_bias'][0], w['ssm_a_log'][0], w['ssm_d'][0]], axis=0).reshape(3, g_n, r_h)
    hpc, hpr = hp.transpose(1, 0, 2), hp.transpose(1, 2, 0)
    w_out = w['ssm_out_w'][0]
    w_q, w_o = w['w_q'][0], w['w_o'][0]

    h0 = x
    u0 = _rmsnorm_fwd(h0, w['ssm_norm_w'][0], name="ssm_norm")
    zx = _matmul(u0, w_zx, name="ssm_in_zx")
    dt_raw = _matmul(u0, w_dt, name="ssm_in_dt")[:, :heads]
    dtg = dt_raw.reshape(t, g_n, r_h)
    dtc, dtr = dtg.transpose(1, 0, 2), dtg.transpose(1, 2, 0)
    xbc = _conv_silu_fwd(zx, conv_w, conv_b, x_off=di, name="ssm_conv")
    y, prev = _ssd_fwd(xbc, dtc, dtr, hpc, hpr, name="ssd_fwd")
    yn = _gate_norm_fwd(y, zx, w['ssm_gate_norm_w'][0], name="ssm_gate_norm")
    h1 = _matmul(yn, w_out, add=h0, name="ssm_out")
    h2, ffn0 = _ffn_fwd(h1, w['ffn_norm_w'][0], w['ffn_up_w'][0], w['ffn_conv_w'][0], w['ffn_conv_b'][0],
                        w['ffn_down_w'][0], 0)
    hk = _rmsnorm_fwd(h2, w['kv_norm_w'], name="kv_norm")
    qn = _rmsnorm_fwd(h2, w['attn_norm_w'][0], name="attn_norm")
    k2 = _matmul(hk, w['w_k'], out_dtype=BF16, name="attn_k")
    v2 = _matmul(hk, w['w_v'], out_dtype=BF16, name="attn_v")
    q2 = _matmul(qn, w_q, out_dtype=BF16, name="attn_q")
    qh, kh, vh = _heads(q2, SB_HEADS), _heads(k2, SB_HEADS), _heads(v2, SB_HEADS)
    oh, lt, first = _sb_fwd(qh, kh, vh, name="sb_fwd")
    o2 = _unheads(oh)
    h3 = _matmul(o2, w_o, add=h2, name="attn_o")
    h4, ffn1 = _ffn_fwd(h3, w['ffn_norm_w'][1], w['ffn_up_w'][1], w['ffn_conv_w'][1], w['ffn_conv_b'][1],
                        w['ffn_down_w'][1], 1)
    loss_p, dh4, d_final = _loss_head(h4, w['final_norm_w'], target, name="loss_head")

    dh3, g1 = _ffn_bwd(h3, ffn1, dh4, w['ffn_norm_w'][1], w['ffn_up_w'][1], w['ffn_conv_w'][1],
                       w['ffn_conv_b'][1], w['ffn_down_w'][1], 1)
    do2 = _matmul(dh3, w_o, tb=True, name="attn_o_dx")
    dw_o = _matmul(o2, dh3, ta=True, name="attn_o_dw")
    dqh, dkh, dvh = _sb_bwd(qh, kh, vh, lt, first, _heads(do2, SB_HEADS), name="sb_bwd")
    dq2, dk2, dv2 = _unheads(dqh), _unheads(dkh), _unheads(dvh)
    dqn = _matmul(dq2, w_q, tb=True, name="attn_q_dx")
    dw_q = _matmul(qn, dq2, ta=True, name="attn_q_dw")
    dhk = _matmul(dk2, w['w_k'], tb=True, name="attn_k_dx")
    dhk = _matmul(dv2, w['w_v'], tb=True, add=dhk, name="attn_v_dx")
    dw_k = _matmul(hk, dk2, ta=True, name="attn_k_dw")
    dw_v = _matmul(hk, dv2, ta=True, name="attn_v_dw")
    dh2, (d_attn_norm, d_kv_norm) = _rmsnorm_bwd(h2, [(dqn, w['attn_norm_w'][0]), (dhk, w['kv_norm_w'])], dh3,
                                                 name="attn_norms_bwd")
    dh1, g0 = _ffn_bwd(h1, ffn0, dh2, w['ffn_norm_w'][0], w['ffn_up_w'][0], w['ffn_conv_w'][0],
                       w['ffn_conv_b'][0], w['ffn_down_w'][0], 0)
    dyn = _matmul(dh1, w_out, tb=True, name="ssm_out_dx")
    dw_out = _matmul(yn, dh1, ta=True, name="ssm_out_dw")
    dy, dz, d_gate = _gate_norm_bwd(y, zx, w['ssm_gate_norm_w'][0], dyn, name="ssm_gate_norm_bwd")
    dxs, dbm, dcm, ddt_g, hg = _ssd_bwd(xbc, dtc, dtr, hpc, hpr, prev, dy, name="ssd_bwd")
    dxbc = jnp.concatenate([dxs, dbm, dcm], axis=1)
    dxbc_pre, d_conv_w, d_conv_b = _conv_silu_bwd(zx, conv_w, conv_b, dxbc, x_off=di, name="ssm_conv_bwd")
    dzx = jnp.concatenate([dz, dxbc_pre], axis=1)
    ddt = jnp.pad(ddt_g.transpose(1, 0, 2).reshape(t, heads), ((0, 0), (0, LANES - heads)))
    du0 = _matmul(dzx, w_zx, tb=True, name="ssm_in_zx_dx")
    du0 = _matmul(ddt, w_dt, tb=True, add=du0, name="ssm_in_dt_dx")
    dw_in = jnp.concatenate([_matmul(u0, dzx, ta=True, name="ssm_in_zx_dw"),
                             _matmul(u0, ddt, ta=True, name="ssm_in_dt_dw")[:, :heads]], axis=1)
    dx, (d_ssm_norm,) = _rmsnorm_bwd(h0, [(du0, w['ssm_norm_w'][0])], dh1, name="ssm_norm_bwd")

    hgr = hg.transpose(1, 0, 2).reshape(3, heads)
    grads = {
        'ssm_norm_w': d_ssm_norm, 'ssm_in_w': dw_in, 'ssm_conv_w': d_conv_w[None], 'ssm_conv_b': d_conv_b,
        'ssm_dt_bias': hgr[0:1], 'ssm_a_log': hgr[1:2], 'ssm_d': hgr[2:3], 'ssm_gate_norm_w': d_gate,
        'ssm_out_w': dw_out, 'kv_norm_w': d_kv_norm[0], 'w_k': dw_k, 'w_v': dw_v, 'attn_norm_w': d_attn_norm,
        'w_q': dw_q, 'w_o': dw_o, 'ffn_norm_w': jnp.stack([g0['norm'], g1['norm']]),
        'ffn_up_w': [g0['up'], g1['up']], 'ffn_conv_w': jnp.stack([g0['conv_w'], g1['conv_w']]),
        'ffn_conv_b': jnp.stack([g0['conv_b'], g1['conv_b']]), 'ffn_down_w': [g0['down'], g1['down']],
        'final_norm_w': d_final[0],
    }
    return loss_p, dx, grads


def kernel(x, ssm_norm_w, ssm_in_w, ssm_conv_w, ssm_conv_b, ssm_dt_bias, ssm_a_log, ssm_d, ssm_gate_norm_w, ssm_out_w, kv_norm_w, w_k, w_v, attn_norm_w, w_q, w_o, ffn_norm_w, ffn_up_w, ffn_conv_w, ffn_conv_b, ffn_down_w, final_norm_w, loss_target, m_ssm_norm_w, m_ssm_in_w, m_ssm_conv_w, m_ssm_conv_b, m_ssm_dt_bias, m_ssm_a_log, m_ssm_d, m_ssm_gate_norm_w, m_ssm_out_w, m_kv_norm_w, m_w_k, m_w_v, m_attn_norm_w, m_w_q, m_w_o, m_ffn_norm_w, m_ffn_up_w, m_ffn_conv_w, m_ffn_conv_b, m_ffn_down_w, m_final_norm_w, v_ssm_norm_w, v_ssm_in_w, v_ssm_conv_w, v_ssm_conv_b, v_ssm_dt_bias, v_ssm_a_log, v_ssm_d, v_ssm_gate_norm_w, v_ssm_out_w, v_kv_norm_w, v_w_k, v_w_v, v_attn_norm_w, v_w_q, v_w_o, v_ffn_norm_w, v_ffn_up_w, v_ffn_conv_w, v_ffn_conv_b, v_ffn_down_w, v_final_norm_w):
    args = (ssm_norm_w, ssm_in_w, ssm_conv_w, ssm_conv_b, ssm_dt_bias, ssm_a_log, ssm_d, ssm_gate_norm_w, ssm_out_w, kv_norm_w, w_k, w_v, attn_norm_w, w_q, w_o, ffn_norm_w, ffn_up_w, ffn_conv_w, ffn_conv_b, ffn_down_w, final_norm_w)
    moms = (m_ssm_norm_w, m_ssm_in_w, m_ssm_conv_w, m_ssm_conv_b, m_ssm_dt_bias, m_ssm_a_log, m_ssm_d, m_ssm_gate_norm_w, m_ssm_out_w, m_kv_norm_w, m_w_k, m_w_v, m_attn_norm_w, m_w_q, m_w_o, m_ffn_norm_w, m_ffn_up_w, m_ffn_conv_w, m_ffn_conv_b, m_ffn_down_w, m_final_norm_w)
    vels = (v_ssm_norm_w, v_ssm_in_w, v_ssm_conv_w, v_ssm_conv_b, v_ssm_dt_bias, v_ssm_a_log, v_ssm_d, v_ssm_gate_norm_w, v_ssm_out_w, v_kv_norm_w, v_w_k, v_w_v, v_attn_norm_w, v_w_q, v_w_o, v_ffn_norm_w, v_ffn_up_w, v_ffn_conv_w, v_ffn_conv_b, v_ffn_down_w, v_final_norm_w)
    local = dict(zip(WEIGHTS, args))
    m_in = dict(zip(WEIGHTS, moms))
    v_in = dict(zip(WEIGHTS, vels))
    c = lax.axis_index("c")
    chip = 2 * lax.axis_index("x") + lax.axis_index("y")

    pieces = []
    for n in BIG:
        blk = local[n]
        if n in STACKED:
            pieces += [(n, l, blk[l]) for l in range(blk.shape[0])]
        else:
            pieces.append((n, None, blk.reshape(blk.shape[-2:])))
    shards16 = [p.astype(BF16).reshape(2, p.shape[0] // 2, p.shape[1]) for _, _, p in pieces]
    gathered = _gather_chips_halves(shards16, name="gather_big")
    full = {n: local[n] for n in REPLICATED}
    for n in STACKED:
        full[n] = []
    for (n, l, p), s16, g in zip(pieces, shards16, gathered):
        by_chip = _set_slot(g, s16, chip).reshape((N_CHIPS,) + p.shape)
        if n == 'ssm_in_w':
            w_full = by_chip.transpose(1, 0, 2).reshape(p.shape[0], N_CHIPS * p.shape[1])
        elif n == 'ffn_up_w':
            w_full = by_chip
        else:
            w_full = by_chip.reshape(N_CHIPS * p.shape[0], p.shape[1])
        if n in STACKED:
            full[n].append(w_full)
        else:
            full[n] = w_full.reshape(local[n].shape[:-2] + w_full.shape)
    small32 = _gather_chips(_pack([local[n].reshape(-1) for n in SMALL], F32, 8), name="gather_small")
    for n, st in zip(SMALL, _unpack(small32, [local[n].shape for n in SMALL])):
        full[n] = _from_shards(st, SHARD_AXIS[n])

    loss_p, dx, grads = _step(x[0], loss_target[0], full)

    gs = []
    for n, l, p in pieces:
        g = grads[n] if l is None else grads[n][l]
        if n == 'ssm_in_w':
            g = g.reshape(p.shape[0], N_CHIPS, p.shape[1]).transpose(1, 0, 2)
        gs.append(g.reshape(N_CHIPS, 2, p.shape[0] // 2, p.shape[1]))
    recv = _swap_other_half(gs, name="rs_pair_swap")
    pairs = [_pair_add(g, r, c, out_dtype=BF16, name=f"rs_pair_add_{i}") for i, (g, r) in enumerate(zip(gs, recv))]
    scattered = _scatter_chips(pairs, name="rs_chip_scatter")
    halves = [_sum_leading(_set_slot(s, lax.dynamic_index_in_dim(pr, chip, axis=0, keepdims=False), chip),
                           name=f"rs_chip_sum_{i}") for i, (s, pr) in enumerate(zip(scattered, pairs))]
    joined = _join_halves(halves, name="rs_half_join")
    gshard = {n: [] for n in STACKED}
    for (n, l, p), h, j in zip(pieces, halves, joined):
        g = _set_slot(j, h, c).reshape(p.shape)
        if n in STACKED:
            gshard[n].append(g)
        else:
            gshard[n] = g.reshape(local[n].shape)

    small = SMALL + REPLICATED
    rep = _pack([loss_p.reshape(-1)] + [grads[n].reshape(-1) for n in small], F32, 8)
    tot = _sum_leading(_gather_all(rep, name="ar_gather"), name="ar_sum")
    parts = _unpack(tot, [(LANES,)] + [grads[n].shape for n in small])
    loss = jnp.sum(parts[0])
    for n, g in zip(small, parts[1:]):
        if n in SHARD_AXIS:
            size = local[n].shape[SHARD_AXIS[n]]
            g = lax.dynamic_slice_in_dim(g, chip * size, size, axis=SHARD_AXIS[n])
        gshard[n] = g

    grads_out, deltas, new_m, new_v = [], [], [], []
    for n in WEIGHTS:
        if n in STACKED:
            g, d, nm, nv = _adamw_layers(local[n], gshard[n], m_in[n], v_in[n], name=f"adamw_{n}")
        else:
            g = gshard[n]
            d, nm, nv = _adamw(local[n], g, m_in[n], v_in[n], name=f"adamw_{n}")
        grads_out.append(g)
        deltas.append(d)
        new_m.append(nm)
        new_v.append(nv)
    return (loss, dx[None], *grads_out, *deltas, *new_m, *new_v)
```

```python
import functools
import math

import jax
import jax.numpy as jnp
from jax import lax
from jax.experimental import pallas as pl
from jax.experimental.pallas import tpu as pltpu

D_MODEL = 1024
D_INNER = 2048
SSM_HEAD_DIM = 64
SSM_HEADS = 32
SSM_GROUPS = 4
SSM_STATE = 128
SSM_CONV = 4
SSM_CHUNK = 128
GN = SSM_GROUPS * SSM_STATE
CONV_DIM = D_INNER + 2 * GN
SB_HEADS = 16
SB_HEAD_DIM = 64
D_FF = 2816
FFN_CONV = 3
EPS = 1e-6
ADAM_LR = 0.001
ADAM_B1 = 0.9
ADAM_B2 = 0.999
ADAM_EPS = 1e-08
ADAM_WD = 0.01
ADAM_STEP = 10

LANES = 128
SUBLANES = 8
VMEM_LIMIT = 48 * 1024 * 1024
ADAM_BLOCK_BYTES = 1 << 20
F32 = jnp.float32
BF16 = jnp.bfloat16
MESH = pl.DeviceIdType.MESH


def _cparams(sem=None):
    return pltpu.CompilerParams(dimension_semantics=sem, vmem_limit_bytes=VMEM_LIMIT)


class _SideJob:
    def __init__(self, ins, out_shapes, n_sems, start, finish):
        self.ins, self.out_shapes, self.n_sems, self.start, self.finish = ins, out_shapes, n_sems, start, finish


def _call(body, *, grid, in_specs, out_specs, out_shape, scratch_shapes=(), sem, name, args, side=None):
    in_specs, out_specs, out_shape, scratch_shapes = list(in_specs), list(out_specs), list(out_shape), list(scratch_shapes)
    n_in, n_out = len(in_specs), len(out_specs)
    if side is None:
        outs = pl.pallas_call(body, grid=grid, in_specs=in_specs, out_specs=out_specs, out_shape=out_shape,
                              scratch_shapes=scratch_shapes, compiler_params=_cparams(sem), name=name)(*args)
        return list(outs), []
    k_in, k_out = len(side.ins), len(side.out_shapes)

    def wrapped(*refs):
        ins, s_ins = refs[:n_in], refs[n_in:n_in + k_in]
        o0 = n_in + k_in
        outs, s_outs = refs[o0:o0 + n_out], refs[o0 + n_out:o0 + n_out + k_out]
        scratch, send_sems, recv_sems = refs[o0 + n_out + k_out:-2], refs[-2], refs[-1]
        ids = [pl.program_id(a) for a in range(len(grid))]
        first = functools.reduce(jnp.logical_and, [p == 0 for p in ids])
        last = functools.reduce(jnp.logical_and, [p == g - 1 for p, g in zip(ids, grid)])

        @pl.when(first)
        def _():
            side.start(s_ins, s_outs, send_sems, recv_sems)

        body(*ins, *outs, *scratch)

        @pl.when(last)
        def _():
            side.finish(s_ins, s_outs, send_sems, recv_sems)

    outs = pl.pallas_call(
        wrapped, grid=grid, in_specs=in_specs + [ANY] * k_in, out_specs=out_specs + [ANY] * k_out,
        out_shape=out_shape + list(side.out_shapes),
        scratch_shapes=scratch_shapes + [pltpu.SemaphoreType.DMA((side.n_sems,)), pltpu.SemaphoreType.DMA((side.n_sems,))],
        compiler_params=_cparams(tuple("arbitrary" for _ in grid)), name=name)(*args, *side.ins)
    return list(outs[:n_out]), list(outs[n_out:])


def _tile(n, cands):
    for c in cands:
        if n % c == 0:
            return c
    return n


def _nt(a, b):
    return lax.dot_general(a, b, (((1,), (1,)), ((), ())), preferred_element_type=F32)


def _tn(a, b):
    return lax.dot_general(a, b, (((0,), (0,)), ((), ())), preferred_element_type=F32)


def _nn(a, b):
    return jnp.dot(a, b, preferred_element_type=F32)


def _split(x, pieces):
    out = []
    for _ in range(pieces - 1):
        h = x.astype(BF16)
        out.append(h)
        x = x - h.astype(F32)
    out.append(x.astype(BF16))
    return out


def _ones_dot(ones, x, *, ones_left, pieces=3):
    o16 = ones.astype(BF16)
    acc = None
    for piece in _split(x, pieces):
        term = _nn(o16, piece) if ones_left else _nn(piece, o16)
        acc = term if acc is None else acc + term
    return acc


def _row_sums(x, pieces=3):
    return _ones_dot(jnp.ones((x.shape[1], LANES), F32), x, ones_left=False, pieces=pieces)


def _softplus(x):
    return jnp.maximum(x, 0.0) + jnp.log(1.0 + jnp.exp(-jnp.abs(x)))


def _sigmoid(x):
    e = jnp.exp(-jnp.abs(x))
    r = 1.0 / (1.0 + e)
    return jnp.where(x >= 0, r, e * r)


MM_TILE_MAX = 1408
MM_VMEM_BUDGET = 40 * 1024 * 1024


def _divisors(n, cap):
    out = [d for d in range(min(cap, n) // LANES * LANES, 0, -LANES) if n % d == 0]
    return out or [n]


def _mm_tiles(m, n, k, a_bytes, b_bytes, o_bytes, add_bytes):
    best = None
    for tm in _divisors(m, MM_TILE_MAX):
        for tn in _divisors(n, MM_TILE_MAX):
            for tk in _divisors(k, MM_TILE_MAX):
                vmem = 2 * (tm * tk * a_bytes + tk * tn * b_bytes + tm * tn * (o_bytes + add_bytes)) + tm * tn * 4
                if vmem > MM_VMEM_BUDGET:
                    continue
                score = (tm * tn * tk, tm * tn)
                if best is None or score > best[0]:
                    best = (score, (tm, tn, tk))
    return best[1]


def _matmul(a, b, *, ta=False, tb=False, add=None, out_dtype=F32, out_parts=1, name):
    a_parts = a.shape[0] if a.ndim == 3 else 1
    b_parts = b.shape[0] if b.ndim == 3 else 1
    assert not (ta and a_parts > 1)
    a2, b2 = a.shape[-2:], b.shape[-2:]
    m, k = (a2[1], a2[0]) if ta else (a2[0], a2[1] * a_parts)
    n, kb = (b2[0], b2[1] * b_parts) if tb else (b2[1] * b_parts, b2[0])
    assert kb == k, (a.shape, b.shape)
    n_unit = math.gcd(n // out_parts, n if tb else b2[1])
    k_unit = math.gcd(k // a_parts, b2[1] if tb else k)
    tm, tn, tk = _mm_tiles(m, n_unit, k_unit, a.dtype.itemsize, b.dtype.itemsize, jnp.dtype(out_dtype).itemsize,
                           0 if add is None else add.dtype.itemsize)
    nk = k // tk
    ka, kbp = (k // a_parts) // tk, (k // b_parts) // tk
    nb, no = (n // b_parts) // tn, (n // out_parts) // tn

    def body(*refs):
        if add is None:
            a_ref, b_ref, o_ref = refs[:3]
            add_ref = None
        else:
            a_ref, b_ref, add_ref, o_ref = refs[:4]
        kk = pl.program_id(2)
        dn = (((0 if ta else 1,), (1 if tb else 0,)), ((), ()))
        prod = lax.dot_general(a_ref[...].astype(BF16), b_ref[...].astype(BF16), dn, preferred_element_type=F32)

        def finish(r):
            if add_ref is not None:
                r = r + add_ref[...].astype(F32)
            o_ref[...] = r.astype(o_ref.dtype)

        if nk == 1:
            finish(prod)
            return
        acc_ref = refs[-1]

        @pl.when(kk == 0)
        def _():
            acc_ref[...] = prod

        @pl.when(jnp.logical_and(kk > 0, kk < nk - 1))
        def _():
            acc_ref[...] += prod

        @pl.when(kk == nk - 1)
        def _():
            finish(acc_ref[...] + prod)

    if ta:
        a_spec = pl.BlockSpec((tk, tm), lambda i, j, kk: (kk, i))
    elif a_parts > 1:
        a_spec = pl.BlockSpec((None, tm, tk), lambda i, j, kk: (kk // ka, i, kk % ka))
    else:
        a_spec = pl.BlockSpec((tm, tk), lambda i, j, kk: (i, kk))
    if b_parts == 1:
        b_spec = pl.BlockSpec((tn, tk), lambda i, j, kk: (j, kk)) if tb else pl.BlockSpec((tk, tn), lambda i, j, kk: (kk, j))
    elif tb:
        b_spec = pl.BlockSpec((None, tn, tk), lambda i, j, kk: (kk // kbp, j, kk % kbp))
    else:
        b_spec = pl.BlockSpec((None, tk, tn), lambda i, j, kk: (j // nb, kk, j % nb))
    if out_parts > 1:
        o_spec = pl.BlockSpec((None, tm, tn), lambda i, j, kk: (j // no, i, j % no))
        o_shape = jax.ShapeDtypeStruct((out_parts, m, n // out_parts), out_dtype)
    else:
        o_spec = pl.BlockSpec((tm, tn), lambda i, j, kk: (i, j))
        o_shape = jax.ShapeDtypeStruct((m, n), out_dtype)
    in_specs = [a_spec, b_spec]
    args = [a, b]
    if add is not None:
        in_specs.append(pl.BlockSpec((tm, tn), lambda i, j, kk: (i, j)))
        args.append(add)
    return pl.pallas_call(
        body,
        grid=(m // tm, n // tn, nk),
        in_specs=in_specs,
        out_specs=o_spec,
        out_shape=o_shape,
        scratch_shapes=[pltpu.VMEM((tm, tn), F32)] if nk > 1 else [],
        compiler_params=_cparams(("parallel", "parallel", "arbitrary")),
        name=name,
    )(*args)


def _rmsnorm_fwd(x, w, *, name):
    t, d = x.shape
    tb = _tile(t, (512, 256, 128))

    def body(x_ref, w_ref, o_ref):
        xv = x_ref[...]
        r = lax.rsqrt(jnp.mean(xv * xv, axis=-1, keepdims=True) + EPS)
        o_ref[...] = (xv * r * w_ref[...]).astype(o_ref.dtype)

    return pl.pallas_call(
        body,
        grid=(t // tb,),
        in_specs=[pl.BlockSpec((tb, d), lambda i: (i, 0)), pl.BlockSpec((1, d), lambda i: (0, 0))],
        out_specs=pl.BlockSpec((tb, d), lambda i: (i, 0)),
        out_shape=jax.ShapeDtypeStruct((t, d), BF16),
        compiler_params=_cparams(("parallel",)),
        name=name,
    )(x, w.reshape(1, d))


def _rmsnorm_bwd(x, dys, dres, *, name):
    t, d = x.shape
    tb = _tile(t, (256, 128))
    nn = len(dys)
    has_res = dres is not None

    def body(*refs):
        x_ref = refs[0]
        dy_refs = refs[1:1 + nn]
        w_refs = refs[1 + nn:1 + 2 * nn]
        pos = 1 + 2 * nn
        res_ref = refs[pos] if has_res else None
        pos += 1 if has_res else 0
        dx_ref = refs[pos]
        dw_refs = refs[pos + 1:pos + 1 + nn]
        i = pl.program_id(0)
        xv = x_ref[...]
        r = lax.rsqrt(jnp.mean(xv * xv, axis=-1, keepdims=True) + EPS)
        xn = xv * r
        dx = res_ref[...] if has_res else jnp.zeros_like(xv)
        for q in range(nn):
            dy = dy_refs[q][...].astype(F32)
            g = dy * w_refs[q][...]
            dx = dx + r * (g - xn * jnp.mean(g * xn, axis=-1, keepdims=True))
            dwp = jnp.sum(dy * xn, axis=0, keepdims=True)

            @pl.when(i == 0)
            def _(q=q, dwp=dwp):
                dw_refs[q][...] = dwp

            @pl.when(i > 0)
            def _(q=q, dwp=dwp):
                dw_refs[q][...] += dwp
        dx_ref[...] = dx

    row = pl.BlockSpec((tb, d), lambda i: (i, 0))
    vec = pl.BlockSpec((1, d), lambda i: (0, 0))
    in_specs = [row] + [row] * nn + [vec] * nn + ([row] if has_res else [])
    args = [x] + [p[0] for p in dys] + [p[1].reshape(1, d) for p in dys] + ([dres] if has_res else [])
    outs = pl.pallas_call(
        body,
        grid=(t // tb,),
        in_specs=in_specs,
        out_specs=[row] + [vec] * nn,
        out_shape=[jax.ShapeDtypeStruct((t, d), F32)] + [jax.ShapeDtypeStruct((1, d), F32)] * nn,
        compiler_params=_cparams(("arbitrary",)),
        name=name,
    )(*args)
    return outs[0], list(outs[1:])


def _loss_head(x, w, target, *, name):
    t, d = x.shape
    tb = _tile(t, (256, 128))

    def body(x_ref, w_ref, t_ref, loss_ref, dx_ref, dw_ref):
        i = pl.program_id(0)
        xv = x_ref[...]
        wv = w_ref[...]
        r = lax.rsqrt(jnp.mean(xv * xv, axis=-1, keepdims=True) + EPS)
        xn = xv * r
        e = xn * wv - t_ref[...]
        lp = 0.5 * jnp.sum(jnp.mean(e * e, axis=-1, keepdims=True), axis=0, keepdims=True)
        dy = e * (1.0 / d)
        g = dy * wv
        dx_ref[...] = r * (g - xn * jnp.mean(g * xn, axis=-1, keepdims=True))
        dwp = jnp.sum(dy * xn, axis=0, keepdims=True)
        lpv = jnp.broadcast_to(lp, (1, LANES)) * (1.0 / LANES)

        @pl.when(i == 0)
        def _():
            dw_ref[...] = dwp
            loss_ref[...] = lpv

        @pl.when(i > 0)
        def _():
            dw_ref[...] += dwp
            loss_ref[...] += lpv

    row = pl.BlockSpec((tb, d), lambda i: (i, 0))
    vec = pl.BlockSpec((1, d), lambda i: (0, 0))
    return pl.pallas_call(
        body,
        grid=(t // tb,),
        in_specs=[row, vec, row],
        out_specs=[pl.BlockSpec((1, LANES), lambda i: (0, 0)), row, vec],
        out_shape=[jax.ShapeDtypeStruct((1, LANES), F32), jax.ShapeDtypeStruct((t, d), F32),
                   jax.ShapeDtypeStruct((1, d), F32)],
        compiler_params=_cparams(("arbitrary",)),
        name=name,
    )(x, w.reshape(1, d), target)


ROW_CHUNK = 64
PAD = SUBLANES


def _shifted(pad_ref, r0, rows, back):
    return pad_ref[pl.ds(PAD + r0 - back, rows), :]


def _conv_taps(pad_ref, w_ref, r0, rows, kw):
    acc = None
    for j in range(kw):
        term = _shifted(pad_ref, r0, rows, kw - 1 - j) * w_ref[j:j + 1, :]
        acc = term if acc is None else acc + term
    return acc


def _fill_pad(pad_ref, x_ref, t):
    pad_ref[0:PAD, :] = jnp.zeros((PAD, pad_ref.shape[1]), F32)
    pad_ref[pl.ds(PAD + t, PAD), :] = jnp.zeros((PAD, pad_ref.shape[1]), F32)
    pad_ref[pl.ds(PAD, t), :] = x_ref[...].astype(F32)


def _conv_silu_fwd(x, w, b, *, x_off=0, name):
    t = x.shape[0]
    kw, c = w.shape
    cw = _tile(math.gcd(c, x_off) if x_off else c, (256, 128))
    ob = x_off // cw
    rc = _tile(t, (ROW_CHUNK,))

    def body(x_ref, w_ref, b_ref, o_ref, pad_ref):
        _fill_pad(pad_ref, x_ref, t)
        for r0 in range(0, t, rc):
            pre = _conv_taps(pad_ref, w_ref, r0, rc, kw) + b_ref[...]
            o_ref[pl.ds(r0, rc), :] = pre * _sigmoid(pre)

    strip = pl.BlockSpec((t, cw), lambda i: (0, i))
    return pl.pallas_call(
        body,
        grid=(c // cw,),
        in_specs=[pl.BlockSpec((t, cw), lambda i: (0, i + ob)), pl.BlockSpec((kw, cw), lambda i: (0, i)),
                  pl.BlockSpec((1, cw), lambda i: (0, i))],
        out_specs=strip,
        out_shape=jax.ShapeDtypeStruct((t, c), F32),
        scratch_shapes=[pltpu.VMEM((t + 2 * PAD, cw), F32)],
        compiler_params=_cparams(("parallel",)),
        name=name,
    )(x, w, b.reshape(1, c))


def _conv_bwd_core(dpre_pad_ref, x_pad_ref, w_ref, dx_ref, dw_ref, db_ref, t, rc, kw):
    cw = dx_ref.shape[1]

    def fold(a):
        return jnp.sum(a.reshape(rc // SUBLANES, SUBLANES, cw), axis=0) if rc % SUBLANES == 0 else jnp.sum(a, axis=0, keepdims=True)

    dws = [None] * kw
    dbs = None
    for r0 in range(0, t, rc):
        dpre = dpre_pad_ref[pl.ds(PAD + r0, rc), :]
        dx = None
        for j in range(kw):
            s = kw - 1 - j
            term = dpre_pad_ref[pl.ds(PAD + r0 + s, rc), :] * w_ref[j:j + 1, :]
            dx = term if dx is None else dx + term
            part = fold(dpre * _shifted(x_pad_ref, r0, rc, s))
            dws[j] = part if dws[j] is None else dws[j] + part
        part = fold(dpre)
        dbs = part if dbs is None else dbs + part
        dx_ref[pl.ds(r0, rc), :] = dx
    for j in range(kw):
        dw_ref[j:j + 1, :] = jnp.sum(dws[j], axis=0, keepdims=True)
    db_ref[...] = jnp.sum(dbs, axis=0, keepdims=True)


def _conv_silu_bwd(x, w, b, dact, *, x_off=0, name):
    t = x.shape[0]
    kw, c = w.shape
    cw = _tile(math.gcd(c, x_off) if x_off else c, (256, 128))
    ob = x_off // cw
    rc = _tile(t, (ROW_CHUNK,))

    def body(x_ref, w_ref, b_ref, da_ref, dx_ref, dw_ref, db_ref, xpad_ref, dpad_ref):
        _fill_pad(xpad_ref, x_ref, t)
        dpad_ref[0:PAD, :] = jnp.zeros((PAD, cw), F32)
        dpad_ref[pl.ds(PAD + t, PAD), :] = jnp.zeros((PAD, cw), F32)
        for r0 in range(0, t, rc):
            pre = _conv_taps(xpad_ref, w_ref, r0, rc, kw) + b_ref[...]
            sg = _sigmoid(pre)
            dpad_ref[pl.ds(PAD + r0, rc), :] = da_ref[pl.ds(r0, rc), :] * (sg * (1.0 + pre * (1.0 - sg)))
        _conv_bwd_core(dpad_ref, xpad_ref, w_ref, dx_ref, dw_ref, db_ref, t, rc, kw)

    strip = pl.BlockSpec((t, cw), lambda i: (0, i))
    wspec = pl.BlockSpec((kw, cw), lambda i: (0, i))
    bspec = pl.BlockSpec((1, cw), lambda i: (0, i))
    return pl.pallas_call(
        body,
        grid=(c // cw,),
        in_specs=[pl.BlockSpec((t, cw), lambda i: (0, i + ob)), wspec, bspec, strip],
        out_specs=[strip, wspec, bspec],
        out_shape=[jax.ShapeDtypeStruct((t, c), F32), jax.ShapeDtypeStruct((kw, c), F32),
                   jax.ShapeDtypeStruct((1, c), F32)],
        scratch_shapes=[pltpu.VMEM((t + 2 * PAD, cw), F32), pltpu.VMEM((t + 2 * PAD, cw), F32)],
        compiler_params=_cparams(("parallel",)),
        name=name,
    )(x, w, b.reshape(1, c), dact)


def _conv_glu_fwd(hid, w, b, *, side=None, name):
    t, c2 = hid.shape
    f = c2 // 2
    kw = w.shape[0]
    cw = _tile(f, (256, 128))
    nf = f // cw
    rc = _tile(t, (ROW_CHUNK,))

    def body(g_ref, v_ref, wg_ref, wv_ref, bg_ref, bv_ref, o_ref, gpad_ref, vpad_ref):
        _fill_pad(gpad_ref, g_ref, t)
        _fill_pad(vpad_ref, v_ref, t)
        for r0 in range(0, t, rc):
            gate = _conv_taps(gpad_ref, wg_ref, r0, rc, kw) + bg_ref[...]
            val = _conv_taps(vpad_ref, wv_ref, r0, rc, kw) + bv_ref[...]
            o_ref[pl.ds(r0, rc), :] = (gate * _sigmoid(gate) * val).astype(o_ref.dtype)

    gs = pl.BlockSpec((t, cw), lambda i: (0, i))
    vs = pl.BlockSpec((t, cw), lambda i: (0, i + nf))
    b2 = b.reshape(1, c2)
    (act,), side_outs = _call(
        body,
        grid=(nf,),
        in_specs=[gs, vs, pl.BlockSpec((kw, cw), lambda i: (0, i)), pl.BlockSpec((kw, cw), lambda i: (0, i + nf)),
                  pl.BlockSpec((1, cw), lambda i: (0, i)), pl.BlockSpec((1, cw), lambda i: (0, i + nf))],
        out_specs=[gs],
        out_shape=[jax.ShapeDtypeStruct((t, f), BF16)],
        scratch_shapes=[pltpu.VMEM((t + 2 * PAD, cw), F32), pltpu.VMEM((t + 2 * PAD, cw), F32)],
        sem=("parallel",),
        name=name,
        args=(hid, hid, w, w, b2, b2),
        side=side,
    )
    return act, side_outs


def _conv_glu_bwd(hid, w, b, dact, *, name):
    t, c2 = hid.shape
    f = c2 // 2
    kw = w.shape[0]
    cw = _tile(f, (128,))
    nf = f // cw
    rc = _tile(t, (ROW_CHUNK,))

    def body(g_ref, v_ref, wg_ref, wv_ref, bg_ref, bv_ref, da_ref,
             dgv_ref, dwg_ref, dwv_ref, dbg_ref, dbv_ref,
             gpad_ref, vpad_ref, dgpad_ref, dvpad_ref):
        _fill_pad(gpad_ref, g_ref, t)
        _fill_pad(vpad_ref, v_ref, t)
        for ref in (dgpad_ref, dvpad_ref):
            ref[0:PAD, :] = jnp.zeros((PAD, cw), F32)
            ref[pl.ds(PAD + t, PAD), :] = jnp.zeros((PAD, cw), F32)
        for r0 in range(0, t, rc):
            gate = _conv_taps(gpad_ref, wg_ref, r0, rc, kw) + bg_ref[...]
            val = _conv_taps(vpad_ref, wv_ref, r0, rc, kw) + bv_ref[...]
            sg = _sigmoid(gate)
            da = da_ref[pl.ds(r0, rc), :].astype(F32)
            dgpad_ref[pl.ds(PAD + r0, rc), :] = da * val * (sg * (1.0 + gate * (1.0 - sg)))
            dvpad_ref[pl.ds(PAD + r0, rc), :] = da * (gate * sg)
        _conv_bwd_core(dgpad_ref, gpad_ref, wg_ref, dgv_ref.at[0], dwg_ref, dbg_ref, t, rc, kw)
        _conv_bwd_core(dvpad_ref, vpad_ref, wv_ref, dgv_ref.at[1], dwv_ref, dbv_ref, t, rc, kw)

    gs = pl.BlockSpec((t, cw), lambda i: (0, i))
    vs = pl.BlockSpec((t, cw), lambda i: (0, i + nf))
    wg = pl.BlockSpec((kw, cw), lambda i: (0, i))
    wv = pl.BlockSpec((kw, cw), lambda i: (0, i + nf))
    bg = pl.BlockSpec((1, cw), lambda i: (0, i))
    bv = pl.BlockSpec((1, cw), lambda i: (0, i + nf))
    b2 = b.reshape(1, c2)
    pad = pltpu.VMEM((t + 2 * PAD, cw), F32)
    return pl.pallas_call(
        body,
        grid=(nf,),
        in_specs=[gs, vs, wg, wv, bg, bv, gs],
        out_specs=[pl.BlockSpec((2, t, cw), lambda i: (0, 0, i)), wg, wg, bg, bg],
        out_shape=[jax.ShapeDtypeStruct((2, t, f), F32),
                   jax.ShapeDtypeStruct((kw, f), F32), jax.ShapeDtypeStruct((kw, f), F32),
                   jax.ShapeDtypeStruct((1, f), F32), jax.ShapeDtypeStruct((1, f), F32)],
        scratch_shapes=[pad, pad, pad, pad],
        compiler_params=_cparams(("parallel",)),
        name=name,
    )(hid, hid, w, w, b2, b2, dact)


def _gate_norm_fwd(y, zx, w, *, name):
    t, di = y.shape
    gsz = di // SSM_GROUPS
    tb = _tile(t, (256, 128))

    def body(y_ref, z_ref, w_ref, o_ref):
        for g in range(SSM_GROUPS):
            sl = slice(g * gsz, (g + 1) * gsz)
            zv = z_ref[:, sl]
            gv = y_ref[:, sl] * (zv * _sigmoid(zv))
            r = lax.rsqrt(jnp.mean(gv * gv, axis=-1, keepdims=True) + EPS)
            o_ref[:, sl] = (gv * r * w_ref[:, sl]).astype(o_ref.dtype)

    row = pl.BlockSpec((tb, di), lambda i: (i, 0))
    return pl.pallas_call(
        body,
        grid=(t // tb,),
        in_specs=[row, row, pl.BlockSpec((1, di), lambda i: (0, 0))],
        out_specs=row,
        out_shape=jax.ShapeDtypeStruct((t, di), BF16),
        compiler_params=_cparams(("parallel",)),
        name=name,
    )(y, zx, w.reshape(1, di))


def _gate_norm_bwd(y, zx, w, dyn, *, name):
    t, di = y.shape
    gsz = di // SSM_GROUPS
    tb = _tile(t, (256, 128))

    def body(y_ref, z_ref, w_ref, d_ref, dy_ref, dz_ref, dw_ref):
        i = pl.program_id(0)
        for g in range(SSM_GROUPS):
            sl = slice(g * gsz, (g + 1) * gsz)
            zv = z_ref[:, sl]
            yv = y_ref[:, sl]
            sg = _sigmoid(zv)
            sz = zv * sg
            gv = yv * sz
            r = lax.rsqrt(jnp.mean(gv * gv, axis=-1, keepdims=True) + EPS)
            gn = gv * r
            dn = d_ref[:, sl].astype(F32)
            q = dn * w_ref[:, sl]
            dg = r * (q - gn * jnp.mean(q * gn, axis=-1, keepdims=True))
            dy_ref[:, sl] = dg * sz
            dz_ref[:, sl] = dg * yv * (sg * (1.0 + zv * (1.0 - sg)))
            dwp = jnp.sum(dn * gn, axis=0, keepdims=True)

            @pl.when(i == 0)
            def _(sl=sl, dwp=dwp):
                dw_ref[:, sl] = dwp

            @pl.when(i > 0)
            def _(sl=sl, dwp=dwp):
                dw_ref[:, sl] += dwp

    row = pl.BlockSpec((tb, di), lambda i: (i, 0))
    vec = pl.BlockSpec((1, di), lambda i: (0, 0))
    return pl.pallas_call(
        body,
        grid=(t // tb,),
        in_specs=[row, row, vec, row],
        out_specs=[row, row, vec],
        out_shape=[jax.ShapeDtypeStruct((t, di), F32), jax.ShapeDtypeStruct((t, di), F32),
                   jax.ShapeDtypeStruct((1, di), F32)],
        compiler_params=_cparams(("arbitrary",)),
        name=name,
    )(y, zx, w.reshape(1, di), dyn)


def _adamw(w, g, m, v, *, name):
    shape = w.shape
    cols = shape[-1]
    rows = w.size // cols
    w2, g2, m2, v2 = (a.reshape(rows, cols) for a in (w, g, m, v))
    tr = rows
    if rows * cols * 4 > ADAM_BLOCK_BYTES:
        tr = _tile(rows, tuple(r for r in (512, 256, 128, 64, 32, 16, 8) if r * cols * 4 <= ADAM_BLOCK_BYTES))
    c1 = 1.0 - ADAM_B1 ** ADAM_STEP
    c2 = 1.0 - ADAM_B2 ** ADAM_STEP

    def body(w_ref, g_ref, m_ref, v_ref, d_ref, nm_ref, nv_ref):
        gv = g_ref[...]
        nm = ADAM_B1 * m_ref[...] + (1.0 - ADAM_B1) * gv
        nv = ADAM_B2 * v_ref[...] + (1.0 - ADAM_B2) * (gv * gv)
        d_ref[...] = -ADAM_LR * ((nm / c1) / (jnp.sqrt(nv / c2) + ADAM_EPS) + ADAM_WD * w_ref[...])
        nm_ref[...] = nm
        nv_ref[...] = nv

    blk = pl.BlockSpec((tr, cols), lambda i: (i, 0))
    outs = pl.pallas_call(
        body,
        grid=(rows // tr,),
        in_specs=[blk] * 4,
        out_specs=[blk] * 3,
        out_shape=[jax.ShapeDtypeStruct((rows, cols), F32)] * 3,
        compiler_params=_cparams(("parallel",)),
        name=name,
    )(w2, g2, m2, v2)
    return tuple(o.reshape(shape) for o in outs)


def _adamw_layers(w, gs, m, v, *, name):
    n_l, rows, cols = w.shape
    assert len(gs) == n_l
    tr = _tile(rows, tuple(r for r in (512, 256, 128, 64, 32, 16, 8) if r * cols * 4 <= ADAM_BLOCK_BYTES))
    c1 = 1.0 - ADAM_B1 ** ADAM_STEP
    c2 = 1.0 - ADAM_B2 ** ADAM_STEP

    def body(*refs):
        w_ref, m_ref, v_ref = refs[:3]
        g_refs = refs[3:3 + n_l]
        g_ref, d_ref, nm_ref, nv_ref = refs[3 + n_l:]
        layer = pl.program_id(0)
        gv = g_refs[0][...]
        for q in range(1, n_l):
            gv = jnp.where(layer == q, g_refs[q][...], gv)
        nm = ADAM_B1 * m_ref[...] + (1.0 - ADAM_B1) * gv
        nv = ADAM_B2 * v_ref[...] + (1.0 - ADAM_B2) * (gv * gv)
        g_ref[...] = gv
        d_ref[...] = -ADAM_LR * ((nm / c1) / (jnp.sqrt(nv / c2) + ADAM_EPS) + ADAM_WD * w_ref[...])
        nm_ref[...] = nm
        nv_ref[...] = nv

    stacked = pl.BlockSpec((None, tr, cols), lambda l, i: (l, i, 0))
    single = pl.BlockSpec((tr, cols), lambda l, i: (i, 0))
    return pl.pallas_call(
        body,
        grid=(n_l, rows // tr),
        in_specs=[stacked] * 3 + [single] * n_l,
        out_specs=[stacked] * 4,
        out_shape=[jax.ShapeDtypeStruct(w.shape, F32)] * 4,
        compiler_params=_cparams(("parallel", "parallel")),
        name=name,
    )(w, m, v, *gs)


def _ssd_scalars(dtc_ref, dtr_ref, hpc_ref, hpr_ref, ln):
    assert SSM_CHUNK == SSM_STATE == LANES, "the SSD kernels mix chunk, state and lane-wide tiles freely"
    bias_c, alog_c = hpc_ref[0, 0:1, :], hpc_ref[0, 1:2, :]
    bias_r, alog_r = hpr_ref[0, :, 0:1], hpr_ref[0, :, 1:2]
    a_c, a_r = -jnp.exp(alog_c), -jnp.exp(alog_r)
    raw_c = dtc_ref[0] + bias_c
    dt_c = _softplus(raw_c)
    dt_r = _softplus(dtr_ref[0] + bias_r)
    row = lax.broadcasted_iota(jnp.int32, (ln, ln), 0)
    col = lax.broadcasted_iota(jnp.int32, (ln, ln), 1)
    lower = (col <= row).astype(F32)
    upper = (row <= col).astype(F32)
    acs_c = _ones_dot(lower, dt_c * a_c, ones_left=True)
    acs_r = _ones_dot(upper, dt_r * a_r, ones_left=False)
    return raw_c, dt_c, a_c, acs_c, acs_r, row, col


def _ssd_specs(t, di, g_n, n_st, rp, ln, r_h, rev):
    nc = t // ln
    cidx = (lambda c: nc - 1 - c) if rev else (lambda c: c)
    xs = pl.BlockSpec((ln, rp), lambda g, c: (cidx(c), g))
    bm = pl.BlockSpec((ln, n_st), lambda g, c: (cidx(c), di // n_st + g))
    cm = pl.BlockSpec((ln, n_st), lambda g, c: (cidx(c), di // n_st + g_n + g))
    dtc = pl.BlockSpec((1, ln, r_h), lambda g, c: (g, cidx(c), 0))
    dtr = pl.BlockSpec((1, r_h, ln), lambda g, c: (g, 0, cidx(c)))
    hpc = pl.BlockSpec((1, 3, r_h), lambda g, c: (g, 0, 0))
    hpr = pl.BlockSpec((1, r_h, 3), lambda g, c: (g, 0, 0))
    prev = pl.BlockSpec((1, rp, n_st), lambda g, c: (cidx(c), g, 0))
    return xs, bm, cm, dtc, dtr, hpc, hpr, prev


def _ssd_fwd(xbc, dtc, dtr, hpc, hpr, *, side=None, name):
    t = xbc.shape[0]
    di, g_n, n_st, p_h, ln = D_INNER, SSM_GROUPS, SSM_STATE, SSM_HEAD_DIM, SSM_CHUNK
    r_h = SSM_HEADS // g_n
    rp = r_h * p_h
    nc = t // ln

    def body(xs_ref, b_ref, c_ref, dtc_ref, dtr_ref, hpc_ref, hpr_ref, y_ref, prev_ref, st_ref):
        @pl.when(pl.program_id(1) == 0)
        def _():
            st_ref[...] = jnp.zeros_like(st_ref)

        _, dt_c, _, acs_c, acs_r, row, col = _ssd_scalars(dtc_ref, dtr_ref, hpc_ref, hpr_ref, ln)
        bm = b_ref[...]
        cm = c_ref[...]
        cm16 = cm.astype(BF16)
        cb = _nt(cm16, bm.astype(BF16))
        causal = row >= col
        for r in range(r_h):
            sl = slice(r * p_h, (r + 1) * p_h)
            xs = xs_ref[:, sl]
            acs = jnp.broadcast_to(acs_c[:, r:r + 1], (ln, ln))
            last = acs[ln - 1:ln, :]
            lm = jnp.where(causal, jnp.exp(acs - acs_r[r:r + 1, :]), 0.0)
            xd = (xs * jnp.broadcast_to(dt_c[:, r:r + 1], (ln, p_h))).astype(BF16)
            prev = st_ref[sl, :]
            y = _nn((cb * lm).astype(BF16), xd)
            y = y + _nt(cm16, prev.astype(BF16)) * jnp.exp(acs[:, :p_h])
            y_ref[:, sl] = y + hpc_ref[0, 2:3, r:r + 1] * xs
            prev_ref[0, sl, :] = prev
            bd = (bm * jnp.exp(last - acs[:, :n_st])).astype(BF16)
            st_ref[sl, :] = prev * jnp.exp(last[:, :n_st]) + _tn(xd, bd)

    xs, bm, cm, dtcs, dtrs, hpcs, hprs, prev = _ssd_specs(t, di, g_n, n_st, rp, ln, r_h, False)
    (y, prev_out), side_outs = _call(
        body,
        grid=(g_n, nc),
        in_specs=[xs, bm, cm, dtcs, dtrs, hpcs, hprs],
        out_specs=[xs, prev],
        out_shape=[jax.ShapeDtypeStruct((t, di), F32), jax.ShapeDtypeStruct((nc, g_n * rp, n_st), F32)],
        scratch_shapes=[pltpu.VMEM((rp, n_st), F32)],
        sem=("parallel", "arbitrary"),
        name=name,
        args=(xbc, xbc, xbc, dtc, dtr, hpc, hpr),
        side=side,
    )
    return y, prev_out, side_outs


def _ssd_bwd(xbc, dtc, dtr, hpc, hpr, prev, dy, *, side=None, name):
    t = xbc.shape[0]
    di, g_n, n_st, p_h, ln = D_INNER, SSM_GROUPS, SSM_STATE, SSM_HEAD_DIM, SSM_CHUNK
    r_h = SSM_HEADS // g_n
    rp = r_h * p_h
    nc = t // ln

    def body(xs_ref, b_ref, c_ref, dtc_ref, dtr_ref, hpc_ref, hpr_ref, prev_ref, dy_ref,
             dxs_ref, db_ref, dc_ref, ddt_ref, hg_ref, ds_ref):
        step = pl.program_id(1)

        @pl.when(step == 0)
        def _():
            ds_ref[...] = jnp.zeros_like(ds_ref)

        raw_c, dt_c, a_c, acs_c, acs_r, row, col = _ssd_scalars(dtc_ref, dtr_ref, hpc_ref, hpr_ref, ln)
        bm = b_ref[...]
        cm = c_ref[...]
        bm16, cm16 = bm.astype(BF16), cm.astype(BF16)
        cb = _nt(cm16, bm16)
        cbt = _nt(bm16, cm16)
        lane_r = lax.broadcasted_iota(jnp.int32, (ln, r_h), 1)
        dacs_all = jnp.zeros((ln, r_h), F32)
        ddtx_all = jnp.zeros((ln, r_h), F32)
        dd_all = jnp.zeros((ln, r_h), F32)
        dcb = jnp.zeros((ln, ln), F32)
        dcbt = jnp.zeros((ln, ln), F32)
        dc_acc = jnp.zeros((ln, n_st), F32)
        db_acc = jnp.zeros((ln, n_st), F32)
        for r in range(r_h):
            sl = slice(r * p_h, (r + 1) * p_h)
            xs = xs_ref[:, sl]
            dyv = dy_ref[:, sl]
            dy16 = dyv.astype(BF16)
            acs = jnp.broadcast_to(acs_c[:, r:r + 1], (ln, ln))
            dtv = jnp.broadcast_to(dt_c[:, r:r + 1], (ln, p_h))
            acsr = acs_r[r:r + 1, :]
            last = acs[ln - 1:ln, :]
            xd = xs * dtv
            xd16 = xd.astype(BF16)
            lm = jnp.where(row >= col, jnp.exp(acs - acsr), 0.0)
            lmt = jnp.where(col >= row, jnp.exp(acsr - acs), 0.0)
            m_ls = cb * lm
            m_sl = cbt * lmt
            dm = _nt(dy16, xd16)
            dmt = _nt(xd16, dy16)
            dxd = _nn(m_sl.astype(BF16), dy16)
            dacs = _row_sums(dm * m_ls - dmt * m_sl)
            dcb = dcb + dm * lm
            dcbt = dcbt + dmt * lmt
            prev = prev_ref[0, sl, :]
            prev16 = prev.astype(BF16)
            e = jnp.exp(acs[:, :p_h])
            y_off = _nt(cm16, prev16) * e
            dacs = dacs + _row_sums(dyv * y_off)
            dyo16 = (dyv * e).astype(BF16)
            dc_acc = dc_acc + _nn(dyo16, prev16)
            dprev = _tn(dyo16, cm16)
            ds = ds_ref[sl, :]
            ds16 = ds.astype(BF16)
            decay = jnp.exp(last - acs)[:, :n_st]
            bd16 = (bm * decay).astype(BF16)
            dbd = _nn(xd16, ds16)
            dxd = dxd + _nt(bd16, ds16)
            db_acc = db_acc + dbd * decay
            tdec = _row_sums(dbd * bm, 2) * decay
            cd = jnp.exp(last)
            dlast = (jnp.sum(tdec, axis=0, keepdims=True)
                     + jnp.sum(_row_sums(prev * ds, 2), axis=0, keepdims=True) * cd)
            ds_ref[sl, :] = dprev + cd[:, :n_st] * ds
            dskip = hpc_ref[0, 2:3, r:r + 1]
            dxs_ref[:, sl] = dxd * dtv + dskip * dyv
            dacs = dacs - tdec + jnp.where(row == ln - 1, dlast, 0.0)
            dacs_all = jnp.where(lane_r == r, dacs[:, :r_h], dacs_all)
            ddtx_all = jnp.where(lane_r == r, _row_sums(dxd * xs, 2)[:, :r_h], ddtx_all)
            dd_all = jnp.where(lane_r == r, _row_sums(dyv * xs, 2)[:, :r_h], dd_all)
        dc_ref[...] = dc_acc + _nn(dcb.astype(BF16), bm16)
        db_ref[...] = db_acc + _nn(dcbt.astype(BF16), cm16)
        upper = (row <= col).astype(F32)
        dad = _ones_dot(upper, dacs_all, ones_left=True)
        ddt = dad * a_c + ddtx_all
        ddt_raw = ddt * _sigmoid(raw_c)
        ddt_ref[0] = ddt_raw
        d_bias = jnp.sum(ddt_raw, axis=0, keepdims=True)
        d_alog = jnp.sum(dad * dt_c, axis=0, keepdims=True) * a_c
        d_d = jnp.sum(dd_all, axis=0, keepdims=True)
        hg = jnp.concatenate([d_bias, d_alog, d_d], axis=0)

        @pl.when(step == 0)
        def _():
            hg_ref[0] = hg

        @pl.when(step > 0)
        def _():
            hg_ref[0] += hg

    xs, bms, cms, dtcs, dtrs, hpcs, hprs, prevs = _ssd_specs(t, di, g_n, n_st, rp, ln, r_h, True)
    bout = pl.BlockSpec((ln, n_st), lambda g, c: (nc - 1 - c, g))
    outs, side_outs = _call(
        body,
        grid=(g_n, nc),
        in_specs=[xs, bms, cms, dtcs, dtrs, hpcs, hprs, prevs, xs],
        out_specs=[xs, bout, bout, dtcs, hpcs],
        out_shape=[jax.ShapeDtypeStruct((t, di), F32), jax.ShapeDtypeStruct((t, g_n * n_st), F32),
                   jax.ShapeDtypeStruct((t, g_n * n_st), F32), jax.ShapeDtypeStruct((g_n, t, r_h), F32),
                   jax.ShapeDtypeStruct((g_n, 3, r_h), F32)],
        scratch_shapes=[pltpu.VMEM((rp, n_st), F32)],
        sem=("parallel", "arbitrary"),
        name=name,
        args=(xbc, xbc, xbc, dtc, dtr, hpc, hpr, prev, dy),
        side=side,
    )
    return (*outs, side_outs)


SB_KEYS = 256
SB_QUERIES = (512, 256)
SB_CUTOFF = 110.0
SB_PIECES = 2


def _sb_logits(qs, kv, valid):
    z = _nt(qs, kv)
    nz = -z
    lg = jnp.minimum(nz, 0.0) - jnp.log(1.0 + jnp.exp(jnp.minimum(z, nz)))
    return z + lg, (lg if valid is None else jnp.where(valid, lg, 0.0))


def _sb_iota(tq):
    diff = lax.broadcasted_iota(jnp.int32, (tq, SB_KEYS), 1) - lax.broadcasted_iota(jnp.int32, (tq, SB_KEYS), 0)
    krow = lax.broadcasted_iota(jnp.int32, (SB_KEYS, SB_KEYS), 0)
    kcol = lax.broadcasted_iota(jnp.int32, (SB_KEYS, SB_KEYS), 1)
    return diff, krow, kcol


def _sb_scale(d):
    scale = 1.0 / math.sqrt(d)
    assert math.frexp(scale)[0] == 0.5, "the scale is folded into bf16 queries: it must be a power of two"
    return scale


def _key_rows(j):
    return pl.ds(pl.multiple_of(j * SB_KEYS, SB_KEYS), SB_KEYS)


def _sb_fwd(q, k, v, *, side=None, name):
    h, t, d = q.shape
    tq = _tile(t, SB_QUERIES)
    nq = t // tq
    kpq = tq // SB_KEYS
    scale = _sb_scale(d)

    def body(q_ref, k_ref, v_ref, o_ref, lt_ref, first_ref):
        i = pl.program_id(1)
        qs = (q_ref[0].astype(F32) * scale).astype(BF16)
        diff, krow, kcol = _sb_iota(tq)
        later = (krow > kcol).astype(F32)

        def block(j, carry, valid):
            acc, cl = carry
            rows = _key_rows(j)
            ls, lg = _sb_logits(qs, k_ref[0, rows, :], valid)
            cs = _ones_dot(later, lg, ones_left=False, pieces=SB_PIECES)
            att = jnp.exp(ls + (cs + cl))
            if valid is not None:
                att = jnp.where(valid, att, 0.0)
            acc = acc + _nn(att.astype(BF16), v_ref[0, rows, :])
            return acc, cl + (cs[:, 0:1] + lg[:, 0:1])

        carry = (jnp.zeros((tq, d), F32), jnp.zeros((tq, 1), F32))
        for m in range(kpq - 1, -1, -1):
            carry = block(i * kpq + m, carry, diff < -m * SB_KEYS)
        nb = i * kpq

        def more(st):
            s, _, cl = st
            return jnp.logical_and(s < nb, jnp.max(cl) > -SB_CUTOFF)

        def step(st):
            s, acc, cl = st
            acc, cl = block(nb - 1 - s, (acc, cl), None)
            return s + 1, acc, cl

        walked, acc, cl = lax.while_loop(more, step, (jnp.int32(0),) + carry)
        o_ref[0] = acc
        lt_ref[0] = cl
        first_ref[pl.program_id(0), i] = nb - walked

    qs = pl.BlockSpec((1, tq, d), lambda hh, i: (hh, i, 0))
    ls = pl.BlockSpec((1, tq, 1), lambda hh, i: (hh, i, 0))
    ks = pl.BlockSpec((1, t, d), lambda hh, i: (hh, 0, 0))
    outs, side_outs = _call(
        body,
        grid=(h, nq),
        in_specs=[qs, ks, ks],
        out_specs=[qs, ls, pl.BlockSpec(memory_space=pltpu.SMEM)],
        out_shape=[jax.ShapeDtypeStruct((h, t, d), F32), jax.ShapeDtypeStruct((h, t, 1), F32),
                   jax.ShapeDtypeStruct((h, nq), jnp.int32)],
        sem=("arbitrary", "arbitrary"),
        name=name,
        args=(q, k, v),
        side=side,
    )
    return (*outs, side_outs)


def _sb_bwd(q, k, v, lt, first, do, *, name):
    h, t, d = q.shape
    tq = _tile(t, SB_QUERIES)
    nq = t // tq
    kpq = tq // SB_KEYS
    scale = _sb_scale(d)
    last = SB_KEYS - 1

    def body(q_ref, k_ref, v_ref, lt_ref, first_ref, do_ref, dq_ref, dk_ref, dv_ref):
        i = pl.program_id(1)

        @pl.when(i == 0)
        def _():
            dk_ref[...] = jnp.zeros_like(dk_ref)
            dv_ref[...] = jnp.zeros_like(dv_ref)

        qs = (q_ref[0].astype(F32) * scale).astype(BF16)
        do16 = do_ref[0].astype(BF16)
        ltot = lt_ref[0]
        diff, krow, kcol = _sb_iota(tq)
        upto = (krow <= kcol).astype(F32)
        before = (krow < kcol).astype(F32)

        def block(j, carry, valid):
            dq, pl_sum, pg_sum = carry
            rows = _key_rows(j)
            kv = k_ref[0, rows, :]
            vv = v_ref[0, rows, :]
            ls, lg = _sb_logits(qs, kv, valid)
            pre = _ones_dot(upto, lg, ones_left=False, pieces=SB_PIECES)
            att = jnp.exp(ls + (ltot - (pre + pl_sum)))
            if valid is not None:
                att = jnp.where(valid, att, 0.0)
            g = att * _nt(do16, vv)
            gpre = _ones_dot(before, g, ones_left=False, pieces=SB_PIECES)
            sig = jnp.exp(ls)
            dz16 = (g - sig * (g + (gpre + pg_sum))).astype(BF16)
            if valid is not None:
                dz16 = jnp.where(valid, dz16, jnp.zeros_like(dz16))
            dq = dq + _nn(dz16, kv)
            dk_ref[0, rows, :] += _tn(dz16, qs)
            dv_ref[0, rows, :] += _tn(att.astype(BF16), do16)
            return dq, pl_sum + pre[:, last:], pg_sum + (gpre[:, last:] + g[:, last:])

        zero = jnp.zeros((tq, 1), F32)
        nb = i * kpq
        start = jnp.clip(first_ref[pl.program_id(0), i], 0, nb)
        carry = lax.fori_loop(start, nb, lambda j, cr: block(j, cr, None), (jnp.zeros((tq, d), F32), zero, zero))
        for m in range(kpq):
            carry = block(nb + m, carry, diff < -m * SB_KEYS)
        dq_ref[0] = carry[0] * scale

    qs = pl.BlockSpec((1, tq, d), lambda hh, i: (hh, i, 0))
    ls = pl.BlockSpec((1, tq, 1), lambda hh, i: (hh, i, 0))
    ks = pl.BlockSpec((1, t, d), lambda hh, i: (hh, 0, 0))
    full = jax.ShapeDtypeStruct((h, t, d), F32)
    return pl.pallas_call(
        body,
        grid=(h, nq),
        in_specs=[qs, ks, ks, ls, pl.BlockSpec(memory_space=pltpu.SMEM), qs],
        out_specs=[qs, ks, ks],
        out_shape=[full, full, full],
        compiler_params=_cparams(("arbitrary", "arbitrary")),
        name=name,
    )(q, k, v, lt, first, do)


def _row_tile(rows, cols):
    return _tile(rows, tuple(r for r in (2048, 1024, 512, 256, 128, 64, 32, 16, 8) if r * cols * 4 <= ADAM_BLOCK_BYTES))


def _sum_leading(x, *, name):
    n, rows, cols = x.shape
    tr = _row_tile(rows, cols)

    def body(x_ref, o_ref):
        acc = x_ref[0].astype(F32)
        for q in range(1, n):
            acc = acc + x_ref[q].astype(F32)
        o_ref[...] = acc

    return pl.pallas_call(
        body,
        grid=(rows // tr,),
        in_specs=[pl.BlockSpec((n, tr, cols), lambda i: (0, i, 0))],
        out_specs=pl.BlockSpec((tr, cols), lambda i: (i, 0)),
        out_shape=jax.ShapeDtypeStruct((rows, cols), F32),
        compiler_params=_cparams(("parallel",)),
        name=name,
    )(x)


def _pair_add(g4h, recv, c, *, out_dtype, name):
    n, _, rows, cols = g4h.shape
    tr = _row_tile(rows, cols)

    def body(c_ref, g_ref, r_ref, o_ref):
        o_ref[...] = (g_ref[...] + r_ref[...]).astype(o_ref.dtype)

    blk = pl.BlockSpec((1, tr, cols), lambda q, i, c_ref: (q, i, 0))
    return pl.pallas_call(
        body,
        grid_spec=pltpu.PrefetchScalarGridSpec(
            num_scalar_prefetch=1,
            grid=(n, rows // tr),
            in_specs=[pl.BlockSpec((1, None, tr, cols), lambda q, i, c_ref: (q, c_ref[0], i, 0)), blk],
            out_specs=blk),
        out_shape=jax.ShapeDtypeStruct((n, rows, cols), out_dtype),
        compiler_params=_cparams(("parallel", "parallel")),
        name=name,
    )(c.reshape(1).astype(jnp.int32), g4h, recv)


ANY = pl.BlockSpec(memory_space=pl.ANY)


def _other_chips(x, y):
    return [(1 - x, y), (x, 1 - y), (1 - x, 1 - y)]


def _gather_chips(shard, *, name):
    def body(x_ref, o_ref, send_sems, recv_sems, local_sem):
        x, y, c = lax.axis_index("x"), lax.axis_index("y"), lax.axis_index("c")
        me = 2 * x + y
        mine = pltpu.make_async_copy(x_ref, o_ref.at[me], local_sem)
        mine.start()
        chips = _other_chips(x, y)
        sends = [pltpu.make_async_remote_copy(src_ref=x_ref, dst_ref=o_ref.at[me], send_sem=send_sems.at[q],
                                              recv_sem=recv_sems.at[q], device_id=(px, py, c), device_id_type=MESH)
                 for q, (px, py) in enumerate(chips)]
        for cp in sends:
            cp.start()
        for q, (px, py) in enumerate(chips):
            pltpu.make_async_remote_copy(src_ref=x_ref, dst_ref=o_ref.at[2 * px + py], send_sem=send_sems.at[q],
                                         recv_sem=recv_sems.at[q], device_id=(px, py, c), device_id_type=MESH).wait_recv()
        for cp in sends:
            cp.wait_send()
        mine.wait()

    return pl.pallas_call(
        body,
        in_specs=[ANY],
        out_specs=ANY,
        out_shape=jax.ShapeDtypeStruct((4,) + shard.shape, shard.dtype),
        scratch_shapes=[pltpu.SemaphoreType.DMA((3,)), pltpu.SemaphoreType.DMA((3,)), pltpu.SemaphoreType.DMA],
        compiler_params=pltpu.CompilerParams(has_side_effects=True),
        name=name,
    )(shard)


def _comm_call(body, ins, out_shapes, n_sems, name):
    n = len(ins)

    def wrapped(*refs):
        body(refs[:n], refs[n:n + len(out_shapes)], refs[-2], refs[-1])

    return pl.pallas_call(
        wrapped,
        in_specs=[ANY] * n,
        out_specs=[ANY] * len(out_shapes),
        out_shape=out_shapes,
        scratch_shapes=[pltpu.SemaphoreType.DMA((n_sems,)), pltpu.SemaphoreType.DMA((n_sems,))],
        compiler_params=pltpu.CompilerParams(has_side_effects=True),
        name=name,
    )(*ins)


def _remote(send_sems, recv_sems, q, src, dst, to):
    return pltpu.make_async_remote_copy(src_ref=src, dst_ref=dst, send_sem=send_sems.at[q], recv_sem=recv_sems.at[q],
                                        device_id=to, device_id_type=MESH)


def _scatter_chips(parts, *, name):
    return _run_job(_scatter_job(parts), name)


def _scatter_job(parts):
    def sends(ins, outs, send_sems, recv_sems):
        x, y, c = lax.axis_index("x"), lax.axis_index("y"), lax.axis_index("c")
        return [_remote(send_sems, recv_sems, 3 * i + q, p.at[2 * px + py], o.at[2 * x + y], (px, py, c))
                for i, (p, o) in enumerate(zip(ins, outs)) for q, (px, py) in enumerate(_other_chips(x, y))]

    def start(ins, outs, send_sems, recv_sems):
        for cp in sends(ins, outs, send_sems, recv_sems):
            cp.start()

    def finish(ins, outs, send_sems, recv_sems):
        x, y, c = lax.axis_index("x"), lax.axis_index("y"), lax.axis_index("c")
        for i, (p, o) in enumerate(zip(ins, outs)):
            for q, (px, py) in enumerate(_other_chips(x, y)):
                _remote(send_sems, recv_sems, 3 * i + q, p.at[2 * x + y], o.at[2 * px + py], (px, py, c)).wait_recv()
        for cp in sends(ins, outs, send_sems, recv_sems):
            cp.wait_send()

    return _SideJob(parts, [jax.ShapeDtypeStruct(p.shape, p.dtype) for p in parts], 3 * len(parts), start, finish)


def _run_job(job, name):
    return _comm_call(lambda *refs: (job.start(*refs), job.finish(*refs)), job.ins, job.out_shapes, job.n_sems, name)


def _gather_job(shards):
    def sends(ins, outs, send_sems, recv_sems):
        x, y, c = lax.axis_index("x"), lax.axis_index("y"), lax.axis_index("c")
        return [_remote(send_sems, recv_sems, 6 * i + q, s.at[c], o.at[2 * x + y, c], (px, py, c))
                for i, (s, o) in enumerate(zip(ins, outs)) for q, (px, py) in enumerate(_other_chips(x, y))]

    def start(ins, outs, send_sems, recv_sems):
        for cp in sends(ins, outs, send_sems, recv_sems):
            cp.start()

    def finish(ins, outs, send_sems, recv_sems):
        x, y, c = lax.axis_index("x"), lax.axis_index("y"), lax.axis_index("c")
        sibling = (x, y, 1 - c)
        chips = _other_chips(x, y)
        copy = lambda q, src, dst, to: _remote(send_sems, recv_sems, q, src, dst, to)
        passed = []
        for i, (s, o) in enumerate(zip(ins, outs)):
            for q, (px, py) in enumerate(chips):
                slot = o.at[2 * px + py, c]
                copy(6 * i + q, s.at[c], slot, (px, py, c)).wait_recv()
                passed.append(copy(6 * i + 3 + q, slot, slot, sibling))
                passed[-1].start()
        for i, (s, o) in enumerate(zip(ins, outs)):
            for q, (px, py) in enumerate(chips):
                copy(6 * i + 3 + q, s.at[1 - c], o.at[2 * px + py, 1 - c], sibling).wait_recv()
        for cp in sends(ins, outs, send_sems, recv_sems) + passed:
            cp.wait_send()

    return _SideJob(shards, [jax.ShapeDtypeStruct((N_CHIPS,) + s.shape, s.dtype) for s in shards], 6 * len(shards),
                    start, finish)


def _swap_other_half(gs, *, name):
    def body(ins, outs, send_sems, recv_sems):
        x, y, c = lax.axis_index("x"), lax.axis_index("y"), lax.axis_index("c")
        copies = [_remote(send_sems, recv_sems, i, g.at[pl.ds(0, g.shape[0]), 1 - c], o, (x, y, 1 - c))
                  for i, (g, o) in enumerate(zip(ins, outs))]
        for cp in copies:
            cp.start()
        for cp in copies:
            cp.wait()

    return _comm_call(body, gs, [jax.ShapeDtypeStruct((g.shape[0],) + g.shape[2:], g.dtype) for g in gs], len(gs), name)


def _join_halves(halves, *, name):
    def body(ins, outs, send_sems, recv_sems):
        x, y, c = lax.axis_index("x"), lax.axis_index("y"), lax.axis_index("c")
        sibling = (x, y, 1 - c)
        sends = [_remote(send_sems, recv_sems, i, h, o.at[c], sibling) for i, (h, o) in enumerate(zip(ins, outs))]
        for cp in sends:
            cp.start()
        for i, (h, o) in enumerate(zip(ins, outs)):
            _remote(send_sems, recv_sems, i, h, o.at[1 - c], sibling).wait_recv()
        for cp in sends:
            cp.wait_send()

    return _comm_call(body, halves, [jax.ShapeDtypeStruct((2,) + h.shape, h.dtype) for h in halves], len(halves), name)


def _gather_all(v, *, name):
    def body(v_ref, o_ref, send_sems, recv_sems, local_sem):
        x, y, c = lax.axis_index("x"), lax.axis_index("y"), lax.axis_index("c")
        me = 4 * x + 2 * y + c
        mine = pltpu.make_async_copy(v_ref, o_ref.at[me], local_sem)
        mine.start()
        peers = [(x ^ (q >> 2 & 1), y ^ (q >> 1 & 1), c ^ (q & 1)) for q in range(1, 8)]
        sends = [pltpu.make_async_remote_copy(src_ref=v_ref, dst_ref=o_ref.at[me], send_sem=send_sems.at[q],
                                              recv_sem=recv_sems.at[q], device_id=peer, device_id_type=MESH)
                 for q, peer in enumerate(peers)]
        for cp in sends:
            cp.start()
        for q, (px, py, pc) in enumerate(peers):
            pltpu.make_async_remote_copy(src_ref=v_ref, dst_ref=o_ref.at[4 * px + 2 * py + pc], send_sem=send_sems.at[q],
                                         recv_sem=recv_sems.at[q], device_id=(px, py, pc), device_id_type=MESH).wait_recv()
        for cp in sends:
            cp.wait_send()
        mine.wait()

    return pl.pallas_call(
        body,
        in_specs=[ANY],
        out_specs=ANY,
        out_shape=jax.ShapeDtypeStruct((8,) + v.shape, v.dtype),
        scratch_shapes=[pltpu.SemaphoreType.DMA((7,)), pltpu.SemaphoreType.DMA((7,)), pltpu.SemaphoreType.DMA],
        compiler_params=pltpu.CompilerParams(has_side_effects=True),
        name=name,
    )(v)


WEIGHTS = ['ssm_norm_w', 'ssm_in_w', 'ssm_conv_w', 'ssm_conv_b', 'ssm_dt_bias', 'ssm_a_log', 'ssm_d',
           'ssm_gate_norm_w', 'ssm_out_w', 'kv_norm_w', 'w_k', 'w_v', 'attn_norm_w', 'w_q', 'w_o',
           'ffn_norm_w', 'ffn_up_w', 'ffn_conv_w', 'ffn_conv_b', 'ffn_down_w', 'final_norm_w']
SHARD_AXIS = {'ssm_norm_w': 1, 'ssm_in_w': 2, 'ssm_conv_w': 2, 'ssm_conv_b': 1, 'ssm_gate_norm_w': 1,
              'ssm_out_w': 1, 'w_k': 0, 'w_v': 0, 'w_q': 1, 'w_o': 1, 'ffn_up_w': 2, 'ffn_conv_w': 2,
              'ffn_down_w': 1}
BIG = ['ssm_in_w', 'ssm_out_w', 'w_k', 'w_v', 'w_q', 'w_o', 'ffn_up_w', 'ffn_down_w']
SMALL = [n for n in WEIGHTS if n in SHARD_AXIS and n not in BIG]
REPLICATED = [n for n in WEIGHTS if n not in SHARD_AXIS]
STACKED = ['ffn_up_w', 'ffn_down_w']
N_CHIPS = 4


PACK_ROWS = 16


def _piece_rows(n):
    return -(-n // (PACK_ROWS * LANES)) * PACK_ROWS


def _pack(arrs, dtype, row_mult):
    lead = arrs[0].shape[:-1]
    pieces, total = [], 0
    for a in arrs:
        n = a.shape[-1]
        rows = _piece_rows(n)
        a = a.astype(dtype)
        if rows * LANES != n:
            a = jnp.pad(a, [(0, 0)] * len(lead) + [(0, rows * LANES - n)])
        pieces.append(a.reshape(lead + (rows, LANES)))
        total += rows
    extra = -total % row_mult
    if extra:
        pieces.append(jnp.zeros(lead + (extra, LANES), dtype))
    return jnp.concatenate(pieces, axis=len(lead))


def _unpack(buf, shapes):
    lead = buf.shape[:-2]
    out, off = [], 0
    for shp in shapes:
        n = math.prod(shp)
        rows = _piece_rows(n)
        piece = lax.slice_in_dim(buf, off, off + rows, axis=len(lead)).reshape(lead + (rows * LANES,))
        out.append(piece[..., :n].reshape(lead + tuple(shp)))
        off += rows
    return out


def _set_slot(buf, piece, index):
    return lax.dynamic_update_slice_in_dim(buf, piece[None], index, axis=0)


def _from_shards(stacked, axis):
    return jnp.concatenate([stacked[j] for j in range(N_CHIPS)], axis=axis)


def _heads(a, h):
    t = a.shape[0]
    return a.reshape(t, h, a.shape[1] // h).transpose(1, 0, 2)


def _unheads(a):
    h, t, d = a.shape
    return a.transpose(1, 0, 2).reshape(t, h * d)


def _ffn_fwd(h, norm_w, w_up, conv_w, conv_b, w_down, tag, side=None):
    u = _rmsnorm_fwd(h, norm_w, name=f"ffn{tag}_norm")
    hid = _matmul(u, w_up, name=f"ffn{tag}_up")
    act, side_outs = _conv_glu_fwd(hid, conv_w, conv_b, side=side, name=f"ffn{tag}_glu")
    out = _matmul(act, w_down, add=h, name=f"ffn{tag}_down")
    return out, (u, hid, act), side_outs


def _ffn_bwd(h, saved, dout, norm_w, w_up, conv_w, conv_b, w_down, tag):
    u, hid, act = saved
    dact = _matmul(dout, w_down, tb=True, name=f"ffn{tag}_down_dx")
    dw_down = _matmul(act, dout, ta=True, name=f"ffn{tag}_down_dw")
    dhid, dwg, dwv, dbg, dbv = _conv_glu_bwd(hid, conv_w, conv_b, dact, name=f"ffn{tag}_glu_bwd")
    du = _matmul(dhid, w_up, tb=True, name=f"ffn{tag}_up_dx")
    dw_up = _matmul(u, dhid, ta=True, out_parts=N_CHIPS, name=f"ffn{tag}_up_dw")
    dh, (dnorm,) = _rmsnorm_bwd(h, [(du, norm_w)], dout, name=f"ffn{tag}_norm_bwd")
    return dh, dict(norm=dnorm[0], up=dw_up, conv_w=jnp.concatenate([dwg, dwv], axis=1),
                    conv_b=jnp.concatenate([dbg, dbv], axis=1)[0], down=dw_down)


class _Pieces:
    def __init__(self, local):
        self.c = lax.axis_index("c")
        self.chip = 2 * lax.axis_index("x") + lax.axis_index("y")
        self.shape, self.s16 = {}, {}
        for n in BIG:
            blk = local[n]
            layers = [(n, l, blk[l]) for l in range(blk.shape[0])] if n in STACKED else [(n, None, blk.reshape(blk.shape[-2:]))]
            for name, l, p in layers:
                self.shape[name, l] = p.shape
                self.s16[name, l] = p.astype(BF16).reshape(2, p.shape[0] // 2, p.shape[1])

    def gather_job(self, keys):
        return _gather_job([self.s16[k] for k in keys])

    def weights(self, keys, gathered):
        out = []
        for k, g in zip(keys, gathered):
            r, cc = self.shape[k]
            by_chip = _set_slot(g, self.s16[k], self.chip).reshape(N_CHIPS, r, cc)
            if k[0] == 'ssm_in_w':
                by_chip = by_chip.transpose(1, 0, 2).reshape(r, N_CHIPS * cc)
            elif k[0] != 'ffn_up_w':
                by_chip = by_chip.reshape(N_CHIPS * r, cc)
            out.append(by_chip)
        return out

    def pair_sums(self, keys, grads, tag):
        gs = []
        for k, g in zip(keys, grads):
            r, cc = self.shape[k]
            if k[0] == 'ssm_in_w':
                g = g.reshape(r, N_CHIPS, cc).transpose(1, 0, 2)
            gs.append(g.reshape(N_CHIPS, 2, r // 2, cc))
        recv = _swap_other_half(gs, name=f"rs_pair_swap_{tag}")
        return [_pair_add(g, rv, self.c, out_dtype=BF16, name=f"rs_pair_add_{tag}{i}") for i, (g, rv) in enumerate(zip(gs, recv))]

    def chip_sums(self, pairs, scattered, tag):
        return [_sum_leading(_set_slot(s, lax.dynamic_index_in_dim(p, self.chip, axis=0, keepdims=False), self.chip),
                             name=f"rs_chip_sum_{tag}{i}") for i, (s, p) in enumerate(zip(scattered, pairs))]

    def shards(self, keys, halves):
        joined = _join_halves(halves, name="rs_half_join")
        return {k: _set_slot(j, h, self.c).reshape(self.shape[k]) for k, h, j in zip(keys, halves, joined)}


def _step(x, target, w, pieces):
    t = x.shape[0]
    g_n, heads = SSM_GROUPS, SSM_HEADS
    r_h = heads // g_n
    di = D_INNER
    zx_cols = di + CONV_DIM
    k_ssm = [('ssm_in_w', None), ('ssm_out_w', None)]
    k_ffn0 = [('ffn_up_w', 0), ('ffn_down_w', 0)]
    k_qkv = [('w_k', None), ('w_v', None), ('w_q', None)]
    k_late = [('w_o', None), ('ffn_up_w', 1), ('ffn_down_w', 1)]
    w_in, w_out = pieces.weights(k_ssm, _run_job(pieces.gather_job(k_ssm), "gather_ssm"))
    w_zx = w_in[:, :zx_cols]
    w_dt = jnp.pad(w_in[:, zx_cols:], ((0, 0), (0, LANES - heads)))
    conv_w, conv_b = w['ssm_conv_w'][0], w['ssm_conv_b'][0]
    hp = jnp.stack([w['ssm_dt_bias'][0], w['ssm_a_log'][0], w['ssm_d'][0]], axis=0).reshape(3, g_n, r_h)
    hpc, hpr = hp.transpose(1, 0, 2), hp.transpose(1, 2, 0)

    h0 = x
    u0 = _rmsnorm_fwd(h0, w['ssm_norm_w'][0], name="ssm_norm")
    zx = _matmul(u0, w_zx, name="ssm_in_zx")
    dt_raw = _matmul(u0, w_dt, name="ssm_in_dt")[:, :heads]
    dtg = dt_raw.reshape(t, g_n, r_h)
    dtc, dtr = dtg.transpose(1, 0, 2), dtg.transpose(1, 2, 0)
    xbc = _conv_silu_fwd(zx, conv_w, conv_b, x_off=di, name="ssm_conv")
    y, prev, got = _ssd_fwd(xbc, dtc, dtr, hpc, hpr, side=pieces.gather_job(k_ffn0), name="ssd_fwd")
    w_up0, w_down0 = pieces.weights(k_ffn0, got)
    yn = _gate_norm_fwd(y, zx, w['ssm_gate_norm_w'][0], name="ssm_gate_norm")
    h1 = _matmul(yn, w_out, add=h0, name="ssm_out")
    h2, ffn0, got = _ffn_fwd(h1, w['ffn_norm_w'][0], w_up0, w['ffn_conv_w'][0], w['ffn_conv_b'][0], w_down0, 0,
                             side=pieces.gather_job(k_qkv))
    w_k, w_v, w_q = pieces.weights(k_qkv, got)
    hk = _rmsnorm_fwd(h2, w['kv_norm_w'], name="kv_norm")
    qn = _rmsnorm_fwd(h2, w['attn_norm_w'][0], name="attn_norm")
    k2 = _matmul(hk, w_k, out_dtype=BF16, name="attn_k")
    v2 = _matmul(hk, w_v, out_dtype=BF16, name="attn_v")
    q2 = _matmul(qn, w_q, out_dtype=BF16, name="attn_q")
    qh, kh, vh = _heads(q2, SB_HEADS), _heads(k2, SB_HEADS), _heads(v2, SB_HEADS)
    oh, lt, first, got = _sb_fwd(qh, kh, vh, side=pieces.gather_job(k_late), name="sb_fwd")
    w_o, w_up1, w_down1 = pieces.weights(k_late, got)
    o2 = _unheads(oh)
    h3 = _matmul(o2, w_o, add=h2, name="attn_o")
    h4, ffn1, _ = _ffn_fwd(h3, w['ffn_norm_w'][1], w_up1, w['ffn_conv_w'][1], w['ffn_conv_b'][1], w_down1, 1)
    loss_p, dh4, d_final = _loss_head(h4, w['final_norm_w'], target, name="loss_head")

    dh3, g1 = _ffn_bwd(h3, ffn1, dh4, w['ffn_norm_w'][1], w_up1, w['ffn_conv_w'][1], w['ffn_conv_b'][1], w_down1, 1)
    do2 = _matmul(dh3, w_o, tb=True, name="attn_o_dx")
    dw_o = _matmul(o2, dh3, ta=True, name="attn_o_dw")
    dqh, dkh, dvh = _sb_bwd(qh, kh, vh, lt, first, _heads(do2, SB_HEADS), name="sb_bwd")
    dq2, dk2, dv2 = _unheads(dqh), _unheads(dkh), _unheads(dvh)
    dqn = _matmul(dq2, w_q, tb=True, name="attn_q_dx")
    dw_q = _matmul(qn, dq2, ta=True, name="attn_q_dw")
    dhk = _matmul(dk2, w_k, tb=True, name="attn_k_dx")
    dhk = _matmul(dv2, w_v, tb=True, add=dhk, name="attn_v_dx")
    dw_k = _matmul(hk, dk2, ta=True, name="attn_k_dw")
    dw_v = _matmul(hk, dv2, ta=True, name="attn_v_dw")
    dh2, (d_attn_norm, d_kv_norm) = _rmsnorm_bwd(h2, [(dqn, w['attn_norm_w'][0]), (dhk, w['kv_norm_w'])], dh3,
                                                 name="attn_norms_bwd")
    dh1, g0 = _ffn_bwd(h1, ffn0, dh2, w['ffn_norm_w'][0], w_up0, w['ffn_conv_w'][0], w['ffn_conv_b'][0], w_down0, 0)
    k_done = k_qkv + k_late + k_ffn0
    pairs_done = pieces.pair_sums(k_done, [dw_k, dw_v, dw_q, dw_o, g1['up'], g1['down'], g0['up'], g0['down']], "a")
    dyn = _matmul(dh1, w_out, tb=True, name="ssm_out_dx")
    dw_out = _matmul(yn, dh1, ta=True, name="ssm_out_dw")
    dy, dz, d_gate = _gate_norm_bwd(y, zx, w['ssm_gate_norm_w'][0], dyn, name="ssm_gate_norm_bwd")
    dxs, dbm, dcm, ddt_g, hg, scattered_done = _ssd_bwd(xbc, dtc, dtr, hpc, hpr, prev, dy,
                                                        side=_scatter_job(pairs_done), name="ssd_bwd")
    dxbc = jnp.concatenate([dxs, dbm, dcm], axis=1)
    dxbc_pre, d_conv_w, d_conv_b = _conv_silu_bwd(zx, conv_w, conv_b, dxbc, x_off=di, name="ssm_conv_bwd")
    dzx = jnp.concatenate([dz, dxbc_pre], axis=1)
    ddt = jnp.pad(ddt_g.transpose(1, 0, 2).reshape(t, heads), ((0, 0), (0, LANES - heads)))
    du0 = _matmul(dzx, w_zx, tb=True, name="ssm_in_zx_dx")
    du0 = _matmul(ddt, w_dt, tb=True, add=du0, name="ssm_in_dt_dx")
    dw_in = jnp.concatenate([_matmul(u0, dzx, ta=True, name="ssm_in_zx_dw"),
                             _matmul(u0, ddt, ta=True, name="ssm_in_dt_dw")[:, :heads]], axis=1)
    dx, (d_ssm_norm,) = _rmsnorm_bwd(h0, [(du0, w['ssm_norm_w'][0])], dh1, name="ssm_norm_bwd")

    pairs_ssm = pieces.pair_sums(k_ssm, [dw_in, dw_out], "b")
    halves = (pieces.chip_sums(pairs_done, scattered_done, "a")
              + pieces.chip_sums(pairs_ssm, _scatter_chips(pairs_ssm, name="rs_chip_scatter_b"), "b"))
    big_grads = pieces.shards(k_done + k_ssm, halves)

    hgr = hg.transpose(1, 0, 2).reshape(3, heads)
    grads = {
        'ssm_norm_w': d_ssm_norm, 'ssm_conv_w': d_conv_w[None], 'ssm_conv_b': d_conv_b,
        'ssm_dt_bias': hgr[0:1], 'ssm_a_log': hgr[1:2], 'ssm_d': hgr[2:3], 'ssm_gate_norm_w': d_gate,
        'kv_norm_w': d_kv_norm[0], 'attn_norm_w': d_attn_norm, 'ffn_norm_w': jnp.stack([g0['norm'], g1['norm']]),
        'ffn_conv_w': jnp.stack([g0['conv_w'], g1['conv_w']]), 'ffn_conv_b': jnp.stack([g0['conv_b'], g1['conv_b']]),
        'final_norm_w': d_final[0],
    }
    return loss_p, dx, grads, big_grads


def kernel(x, ssm_norm_w, ssm_in_w, ssm_conv_w, ssm_conv_b, ssm_dt_bias, ssm_a_log, ssm_d, ssm_gate_norm_w, ssm_out_w, kv_norm_w, w_k, w_v, attn_norm_w, w_q, w_o, ffn_norm_w, ffn_up_w, ffn_conv_w, ffn_conv_b, ffn_down_w, final_norm_w, loss_target, m_ssm_norm_w, m_ssm_in_w, m_ssm_conv_w, m_ssm_conv_b, m_ssm_dt_bias, m_ssm_a_log, m_ssm_d, m_ssm_gate_norm_w, m_ssm_out_w, m_kv_norm_w, m_w_k, m_w_v, m_attn_norm_w, m_w_q, m_w_o, m_ffn_norm_w, m_ffn_up_w, m_ffn_conv_w, m_ffn_conv_b, m_ffn_down_w, m_final_norm_w, v_ssm_norm_w, v_ssm_in_w, v_ssm_conv_w, v_ssm_conv_b, v_ssm_dt_bias, v_ssm_a_log, v_ssm_d, v_ssm_gate_norm_w, v_ssm_out_w, v_kv_norm_w, v_w_k, v_w_v, v_attn_norm_w, v_w_q, v_w_o, v_ffn_norm_w, v_ffn_up_w, v_ffn_conv_w, v_ffn_conv_b, v_ffn_down_w, v_final_norm_w):
    args = (ssm_norm_w, ssm_in_w, ssm_conv_w, ssm_conv_b, ssm_dt_bias, ssm_a_log, ssm_d, ssm_gate_norm_w, ssm_out_w, kv_norm_w, w_k, w_v, attn_norm_w, w_q, w_o, ffn_norm_w, ffn_up_w, ffn_conv_w, ffn_conv_b, ffn_down_w, final_norm_w)
    moms = (m_ssm_norm_w, m_ssm_in_w, m_ssm_conv_w, m_ssm_conv_b, m_ssm_dt_bias, m_ssm_a_log, m_ssm_d, m_ssm_gate_norm_w, m_ssm_out_w, m_kv_norm_w, m_w_k, m_w_v, m_attn_norm_w, m_w_q, m_w_o, m_ffn_norm_w, m_ffn_up_w, m_ffn_conv_w, m_ffn_conv_b, m_ffn_down_w, m_final_norm_w)
    vels = (v_ssm_norm_w, v_ssm_in_w, v_ssm_conv_w, v_ssm_conv_b, v_ssm_dt_bias, v_ssm_a_log, v_ssm_d, v_ssm_gate_norm_w, v_ssm_out_w, v_kv_norm_w, v_w_k, v_w_v, v_attn_norm_w, v_w_q, v_w_o, v_ffn_norm_w, v_ffn_up_w, v_ffn_conv_w, v_ffn_conv_b, v_ffn_down_w, v_final_norm_w)
    local = dict(zip(WEIGHTS, args))
    m_in = dict(zip(WEIGHTS, moms))
    v_in = dict(zip(WEIGHTS, vels))
    chip = 2 * lax.axis_index("x") + lax.axis_index("y")

    full = {n: local[n] for n in REPLICATED}
    small32 = _gather_chips(_pack([local[n].reshape(-1) for n in SMALL], F32, 8), name="gather_small")
    for n, st in zip(SMALL, _unpack(small32, [local[n].shape for n in SMALL])):
        full[n] = _from_shards(st, SHARD_AXIS[n])

    pieces = _Pieces(local)
    loss_p, dx, grads, big_grads = _step(x[0], loss_target[0], full, pieces)
    gshard = {}
    for n in BIG:
        if n in STACKED:
            gshard[n] = [big_grads[n, l] for l in range(local[n].shape[0])]
        else:
            gshard[n] = big_grads[n, None].reshape(local[n].shape)

    small = SMALL + REPLICATED
    rep = _pack([loss_p.reshape(-1)] + [grads[n].reshape(-1) for n in small], F32, 8)
    tot = _sum_leading(_gather_all(rep, name="ar_gather"), name="ar_sum")
    parts = _unpack(tot, [(LANES,)] + [grads[n].shape for n in small])
    loss = jnp.sum(parts[0])
    for n, g in zip(small, parts[1:]):
        if n in SHARD_AXIS:
            size = local[n].shape[SHARD_AXIS[n]]
            g = lax.dynamic_slice_in_dim(g, chip * size, size, axis=SHARD_AXIS[n])
        gshard[n] = g

    grads_out, deltas, new_m, new_v = [], [], [], []
    for n in WEIGHTS:
        if n in STACKED:
            g, d, nm, nv = _adamw_layers(local[n], gshard[n], m_in[n], v_in[n], name=f"adamw_{n}")
        else:
            g = gshard[n]
            d, nm, nv = _adamw(local[n], g, m_in[n], v_in[n], name=f"adamw_{n}")
        grads_out.append(g)
        deltas.append(d)
        new_m.append(nm)
        new_v.append(nv)
    return (loss, dx[None], *grads_out, *deltas, *new_m, *new_v)
```

```python
import functools
import math

import jax
import jax.numpy as jnp
from jax import lax
from jax.experimental import pallas as pl
from jax.experimental.pallas import tpu as pltpu

D_MODEL = 1024
D_INNER = 2048
SSM_HEAD_DIM = 64
SSM_HEADS = 32
SSM_GROUPS = 4
SSM_STATE = 128
SSM_CONV = 4
SSM_CHUNK = 128
GN = SSM_GROUPS * SSM_STATE
CONV_DIM = D_INNER + 2 * GN
SB_HEADS = 16
SB_HEAD_DIM = 64
D_FF = 2816
FFN_CONV = 3
EPS = 1e-6
ADAM_LR = 0.001
ADAM_B1 = 0.9
ADAM_B2 = 0.999
ADAM_EPS = 1e-08
ADAM_WD = 0.01
ADAM_STEP = 10

LANES = 128
SUBLANES = 8
VMEM_LIMIT = 48 * 1024 * 1024
ADAM_BLOCK_BYTES = 1 << 20
F32 = jnp.float32
BF16 = jnp.bfloat16
MESH = pl.DeviceIdType.MESH


def _cparams(sem=None):
    return pltpu.CompilerParams(dimension_semantics=sem, vmem_limit_bytes=VMEM_LIMIT)


class _SideJob:
    def __init__(self, ins, out_shapes, n_sems, start, finish):
        self.ins, self.out_shapes, self.n_sems, self.start, self.finish = ins, out_shapes, n_sems, start, finish


def _call(body, *, grid, in_specs, out_specs, out_shape, scratch_shapes=(), sem, name, args, side=None):
    in_specs, out_specs, out_shape, scratch_shapes = list(in_specs), list(out_specs), list(out_shape), list(scratch_shapes)
    n_in, n_out = len(in_specs), len(out_specs)
    if side is None:
        outs = pl.pallas_call(body, grid=grid, in_specs=in_specs, out_specs=out_specs, out_shape=out_shape,
                              scratch_shapes=scratch_shapes, compiler_params=_cparams(sem), name=name)(*args)
        return list(outs), []
    k_in, k_out = len(side.ins), len(side.out_shapes)

    def wrapped(*refs):
        ins, s_ins = refs[:n_in], refs[n_in:n_in + k_in]
        o0 = n_in + k_in
        outs, s_outs = refs[o0:o0 + n_out], refs[o0 + n_out:o0 + n_out + k_out]
        scratch, send_sems, recv_sems = refs[o0 + n_out + k_out:-2], refs[-2], refs[-1]
        ids = [pl.program_id(a) for a in range(len(grid))]
        first = functools.reduce(jnp.logical_and, [p == 0 for p in ids])
        last = functools.reduce(jnp.logical_and, [p == g - 1 for p, g in zip(ids, grid)])

        @pl.when(first)
        def _():
            side.start(s_ins, s_outs, send_sems, recv_sems)

        body(*ins, *outs, *scratch)

        @pl.when(last)
        def _():
            side.finish(s_ins, s_outs, send_sems, recv_sems)

    outs = pl.pallas_call(
        wrapped, grid=grid, in_specs=in_specs + [ANY] * k_in, out_specs=out_specs + [ANY] * k_out,
        out_shape=out_shape + list(side.out_shapes),
        scratch_shapes=scratch_shapes + [pltpu.SemaphoreType.DMA((side.n_sems,)), pltpu.SemaphoreType.DMA((side.n_sems,))],
        compiler_params=_cparams(tuple("arbitrary" for _ in grid)), name=name)(*args, *side.ins)
    return list(outs[:n_out]), list(outs[n_out:])


def _tile(n, cands):
    for c in cands:
        if n % c == 0:
            return c
    return n


def _nt(a, b):
    return lax.dot_general(a, b, (((1,), (1,)), ((), ())), preferred_element_type=F32)


def _tn(a, b):
    return lax.dot_general(a, b, (((0,), (0,)), ((), ())), preferred_element_type=F32)


def _nn(a, b):
    return jnp.dot(a, b, preferred_element_type=F32)


def _split(x, pieces):
    out = []
    for _ in range(pieces - 1):
        h = x.astype(BF16)
        out.append(h)
        x = x - h.astype(F32)
    out.append(x.astype(BF16))
    return out


def _ones_dot(ones, x, *, ones_left, pieces=3):
    o16 = ones.astype(BF16)
    acc = None
    for piece in _split(x, pieces):
        term = _nn(o16, piece) if ones_left else _nn(piece, o16)
        acc = term if acc is None else acc + term
    return acc


def _row_sums(x, pieces=3):
    return _ones_dot(jnp.ones((x.shape[1], LANES), F32), x, ones_left=False, pieces=pieces)


def _softplus(x):
    return jnp.maximum(x, 0.0) + jnp.log(1.0 + jnp.exp(-jnp.abs(x)))


def _sigmoid(x):
    return 0.5 * jnp.tanh(0.5 * x) + 0.5


MM_TILE_MAX = 1408
MM_VMEM_BUDGET = 40 * 1024 * 1024


def _divisors(n, cap):
    out = [d for d in range(min(cap, n) // LANES * LANES, 0, -LANES) if n % d == 0]
    return out or [n]


def _mm_tiles(m, n, k, a_bytes, b_bytes, o_bytes, add_bytes):
    best = None
    for tm in _divisors(m, MM_TILE_MAX):
        for tn in _divisors(n, MM_TILE_MAX):
            for tk in _divisors(k, MM_TILE_MAX):
                vmem = 2 * (tm * tk * a_bytes + tk * tn * b_bytes + tm * tn * (o_bytes + add_bytes)) + tm * tn * 4
                if vmem > MM_VMEM_BUDGET:
                    continue
                score = (tm * tn * tk, tm * tn)
                if best is None or score > best[0]:
                    best = (score, (tm, tn, tk))
    return best[1]


def _matmul(a, b, *, ta=False, tb=False, add=None, out_dtype=F32, out_parts=1, name):
    a_parts = a.shape[0] if a.ndim == 3 else 1
    b_parts = b.shape[0] if b.ndim == 3 else 1
    assert not (ta and a_parts > 1)
    a2, b2 = a.shape[-2:], b.shape[-2:]
    m, k = (a2[1], a2[0]) if ta else (a2[0], a2[1] * a_parts)
    n, kb = (b2[0], b2[1] * b_parts) if tb else (b2[1] * b_parts, b2[0])
    assert kb == k, (a.shape, b.shape)
    n_unit = math.gcd(n // out_parts, n if tb else b2[1])
    k_unit = math.gcd(k // a_parts, b2[1] if tb else k)
    tm, tn, tk = _mm_tiles(m, n_unit, k_unit, a.dtype.itemsize, b.dtype.itemsize, jnp.dtype(out_dtype).itemsize,
                           0 if add is None else add.dtype.itemsize)
    nk = k // tk
    ka, kbp = (k // a_parts) // tk, (k // b_parts) // tk
    nb, no = (n // b_parts) // tn, (n // out_parts) // tn

    def body(*refs):
        if add is None:
            a_ref, b_ref, o_ref = refs[:3]
            add_ref = None
        else:
            a_ref, b_ref, add_ref, o_ref = refs[:4]
        kk = pl.program_id(2)
        dn = (((0 if ta else 1,), (1 if tb else 0,)), ((), ()))
        prod = lax.dot_general(a_ref[...].astype(BF16), b_ref[...].astype(BF16), dn, preferred_element_type=F32)

        def finish(r):
            if add_ref is not None:
                r = r + add_ref[...].astype(F32)
            o_ref[...] = r.astype(o_ref.dtype)

        if nk == 1:
            finish(prod)
            return
        acc_ref = refs[-1]

        @pl.when(kk == 0)
        def _():
            acc_ref[...] = prod

        @pl.when(jnp.logical_and(kk > 0, kk < nk - 1))
        def _():
            acc_ref[...] += prod

        @pl.when(kk == nk - 1)
        def _():
            finish(acc_ref[...] + prod)

    if ta:
        a_spec = pl.BlockSpec((tk, tm), lambda i, j, kk: (kk, i))
    elif a_parts > 1:
        a_spec = pl.BlockSpec((None, tm, tk), lambda i, j, kk: (kk // ka, i, kk % ka))
    else:
        a_spec = pl.BlockSpec((tm, tk), lambda i, j, kk: (i, kk))
    if b_parts == 1:
        b_spec = pl.BlockSpec((tn, tk), lambda i, j, kk: (j, kk)) if tb else pl.BlockSpec((tk, tn), lambda i, j, kk: (kk, j))
    elif tb:
        b_spec = pl.BlockSpec((None, tn, tk), lambda i, j, kk: (kk // kbp, j, kk % kbp))
    else:
        b_spec = pl.BlockSpec((None, tk, tn), lambda i, j, kk: (j // nb, kk, j % nb))
    if out_parts > 1:
        o_spec = pl.BlockSpec((None, tm, tn), lambda i, j, kk: (j // no, i, j % no))
        o_shape = jax.ShapeDtypeStruct((out_parts, m, n // out_parts), out_dtype)
    else:
        o_spec = pl.BlockSpec((tm, tn), lambda i, j, kk: (i, j))
        o_shape = jax.ShapeDtypeStruct((m, n), out_dtype)
    in_specs = [a_spec, b_spec]
    args = [a, b]
    if add is not None:
        in_specs.append(pl.BlockSpec((tm, tn), lambda i, j, kk: (i, j)))
        args.append(add)
    return pl.pallas_call(
        body,
        grid=(m // tm, n // tn, nk),
        in_specs=in_specs,
        out_specs=o_spec,
        out_shape=o_shape,
        scratch_shapes=[pltpu.VMEM((tm, tn), F32)] if nk > 1 else [],
        compiler_params=_cparams(("parallel", "parallel", "arbitrary")),
        name=name,
    )(*args)


def _rmsnorm_fwd(x, w, *, name):
    t, d = x.shape
    tb = _tile(t, (512, 256, 128))

    def body(x_ref, w_ref, o_ref):
        xv = x_ref[...]
        r = lax.rsqrt(jnp.mean(xv * xv, axis=-1, keepdims=True) + EPS)
        o_ref[...] = (xv * r * w_ref[...]).astype(o_ref.dtype)

    return pl.pallas_call(
        body,
        grid=(t // tb,),
        in_specs=[pl.BlockSpec((tb, d), lambda i: (i, 0)), pl.BlockSpec((1, d), lambda i: (0, 0))],
        out_specs=pl.BlockSpec((tb, d), lambda i: (i, 0)),
        out_shape=jax.ShapeDtypeStruct((t, d), BF16),
        compiler_params=_cparams(("parallel",)),
        name=name,
    )(x, w.reshape(1, d))


def _rmsnorm_bwd(x, dys, dres, *, name):
    t, d = x.shape
    tb = _tile(t, (256, 128))
    nn = len(dys)
    has_res = dres is not None

    def body(*refs):
        x_ref = refs[0]
        dy_refs = refs[1:1 + nn]
        w_refs = refs[1 + nn:1 + 2 * nn]
        pos = 1 + 2 * nn
        res_ref = refs[pos] if has_res else None
        pos += 1 if has_res else 0
        dx_ref = refs[pos]
        dw_refs = refs[pos + 1:pos + 1 + nn]
        i = pl.program_id(0)
        xv = x_ref[...]
        r = lax.rsqrt(jnp.mean(xv * xv, axis=-1, keepdims=True) + EPS)
        xn = xv * r
        dx = res_ref[...] if has_res else jnp.zeros_like(xv)
        for q in range(nn):
            dy = dy_refs[q][...].astype(F32)
            g = dy * w_refs[q][...]
            dx = dx + r * (g - xn * jnp.mean(g * xn, axis=-1, keepdims=True))
            dwp = jnp.sum(dy * xn, axis=0, keepdims=True)

            @pl.when(i == 0)
            def _(q=q, dwp=dwp):
                dw_refs[q][...] = dwp

            @pl.when(i > 0)
            def _(q=q, dwp=dwp):
                dw_refs[q][...] += dwp
        dx_ref[...] = dx

    row = pl.BlockSpec((tb, d), lambda i: (i, 0))
    vec = pl.BlockSpec((1, d), lambda i: (0, 0))
    in_specs = [row] + [row] * nn + [vec] * nn + ([row] if has_res else [])
    args = [x] + [p[0] for p in dys] + [p[1].reshape(1, d) for p in dys] + ([dres] if has_res else [])
    outs = pl.pallas_call(
        body,
        grid=(t // tb,),
        in_specs=in_specs,
        out_specs=[row] + [vec] * nn,
        out_shape=[jax.ShapeDtypeStruct((t, d), F32)] + [jax.ShapeDtypeStruct((1, d), F32)] * nn,
        compiler_params=_cparams(("arbitrary",)),
        name=name,
    )(*args)
    return outs[0], list(outs[1:])


def _loss_head(x, w, target, *, name):
    t, d = x.shape
    tb = _tile(t, (256, 128))

    def body(x_ref, w_ref, t_ref, loss_ref, dx_ref, dw_ref):
        i = pl.program_id(0)
        xv = x_ref[...]
        wv = w_ref[...]
        r = lax.rsqrt(jnp.mean(xv * xv, axis=-1, keepdims=True) + EPS)
        xn = xv * r
        e = xn * wv - t_ref[...]
        lp = 0.5 * jnp.sum(jnp.mean(e * e, axis=-1, keepdims=True), axis=0, keepdims=True)
        dy = e * (1.0 / d)
        g = dy * wv
        dx_ref[...] = r * (g - xn * jnp.mean(g * xn, axis=-1, keepdims=True))
        dwp = jnp.sum(dy * xn, axis=0, keepdims=True)
        lpv = jnp.broadcast_to(lp, (1, LANES)) * (1.0 / LANES)

        @pl.when(i == 0)
        def _():
            dw_ref[...] = dwp
            loss_ref[...] = lpv

        @pl.when(i > 0)
        def _():
            dw_ref[...] += dwp
            loss_ref[...] += lpv

    row = pl.BlockSpec((tb, d), lambda i: (i, 0))
    vec = pl.BlockSpec((1, d), lambda i: (0, 0))
    return pl.pallas_call(
        body,
        grid=(t // tb,),
        in_specs=[row, vec, row],
        out_specs=[pl.BlockSpec((1, LANES), lambda i: (0, 0)), row, vec],
        out_shape=[jax.ShapeDtypeStruct((1, LANES), F32), jax.ShapeDtypeStruct((t, d), F32),
                   jax.ShapeDtypeStruct((1, d), F32)],
        compiler_params=_cparams(("arbitrary",)),
        name=name,
    )(x, w.reshape(1, d), target)


ROW_CHUNK = 64
PAD = SUBLANES


def _shifted(pad_ref, r0, rows, back):
    return pad_ref[pl.ds(PAD + r0 - back, rows), :]


def _conv_taps(pad_ref, w_ref, r0, rows, kw):
    acc = None
    for j in range(kw):
        term = _shifted(pad_ref, r0, rows, kw - 1 - j) * w_ref[j:j + 1, :]
        acc = term if acc is None else acc + term
    return acc


def _fill_pad(pad_ref, x_ref, t):
    pad_ref[0:PAD, :] = jnp.zeros((PAD, pad_ref.shape[1]), F32)
    pad_ref[pl.ds(PAD + t, PAD), :] = jnp.zeros((PAD, pad_ref.shape[1]), F32)
    pad_ref[pl.ds(PAD, t), :] = x_ref[...].astype(F32)


def _conv_silu_fwd(x, w, b, *, x_off=0, name):
    t = x.shape[0]
    kw, c = w.shape
    cw = _tile(math.gcd(c, x_off) if x_off else c, (256, 128))
    ob = x_off // cw
    rc = _tile(t, (ROW_CHUNK,))

    def body(x_ref, w_ref, b_ref, o_ref, pad_ref):
        _fill_pad(pad_ref, x_ref, t)
        for r0 in range(0, t, rc):
            pre = _conv_taps(pad_ref, w_ref, r0, rc, kw) + b_ref[...]
            o_ref[pl.ds(r0, rc), :] = pre * _sigmoid(pre)

    strip = pl.BlockSpec((t, cw), lambda i: (0, i))
    return pl.pallas_call(
        body,
        grid=(c // cw,),
        in_specs=[pl.BlockSpec((t, cw), lambda i: (0, i + ob)), pl.BlockSpec((kw, cw), lambda i: (0, i)),
                  pl.BlockSpec((1, cw), lambda i: (0, i))],
        out_specs=strip,
        out_shape=jax.ShapeDtypeStruct((t, c), F32),
        scratch_shapes=[pltpu.VMEM((t + 2 * PAD, cw), F32)],
        compiler_params=_cparams(("parallel",)),
        name=name,
    )(x, w, b.reshape(1, c))


def _conv_bwd_core(dpre_pad_ref, x_pad_ref, w_ref, dx_ref, dw_ref, db_ref, t, rc, kw):
    cw = dx_ref.shape[1]

    def fold(a):
        return jnp.sum(a.reshape(rc // SUBLANES, SUBLANES, cw), axis=0) if rc % SUBLANES == 0 else jnp.sum(a, axis=0, keepdims=True)

    dws = [None] * kw
    dbs = None
    for r0 in range(0, t, rc):
        dpre = dpre_pad_ref[pl.ds(PAD + r0, rc), :]
        dx = None
        for j in range(kw):
            s = kw - 1 - j
            term = dpre_pad_ref[pl.ds(PAD + r0 + s, rc), :] * w_ref[j:j + 1, :]
            dx = term if dx is None else dx + term
            part = fold(dpre * _shifted(x_pad_ref, r0, rc, s))
            dws[j] = part if dws[j] is None else dws[j] + part
        part = fold(dpre)
        dbs = part if dbs is None else dbs + part
        dx_ref[pl.ds(r0, rc), :] = dx
    for j in range(kw):
        dw_ref[j:j + 1, :] = jnp.sum(dws[j], axis=0, keepdims=True)
    db_ref[...] = jnp.sum(dbs, axis=0, keepdims=True)


def _conv_silu_bwd(x, w, b, dact, *, x_off=0, name):
    t = x.shape[0]
    kw, c = w.shape
    cw = _tile(math.gcd(c, x_off) if x_off else c, (256, 128))
    ob = x_off // cw
    rc = _tile(t, (ROW_CHUNK,))

    def body(x_ref, w_ref, b_ref, da_ref, dx_ref, dw_ref, db_ref, xpad_ref, dpad_ref):
        _fill_pad(xpad_ref, x_ref, t)
        dpad_ref[0:PAD, :] = jnp.zeros((PAD, cw), F32)
        dpad_ref[pl.ds(PAD + t, PAD), :] = jnp.zeros((PAD, cw), F32)
        for r0 in range(0, t, rc):
            pre = _conv_taps(xpad_ref, w_ref, r0, rc, kw) + b_ref[...]
            sg = _sigmoid(pre)
            dpad_ref[pl.ds(PAD + r0, rc), :] = da_ref[pl.ds(r0, rc), :] * (sg * (1.0 + pre * (1.0 - sg)))
        _conv_bwd_core(dpad_ref, xpad_ref, w_ref, dx_ref, dw_ref, db_ref, t, rc, kw)

    strip = pl.BlockSpec((t, cw), lambda i: (0, i))
    wspec = pl.BlockSpec((kw, cw), lambda i: (0, i))
    bspec = pl.BlockSpec((1, cw), lambda i: (0, i))
    return pl.pallas_call(
        body,
        grid=(c // cw,),
        in_specs=[pl.BlockSpec((t, cw), lambda i: (0, i + ob)), wspec, bspec, strip],
        out_specs=[strip, wspec, bspec],
        out_shape=[jax.ShapeDtypeStruct((t, c), F32), jax.ShapeDtypeStruct((kw, c), F32),
                   jax.ShapeDtypeStruct((1, c), F32)],
        scratch_shapes=[pltpu.VMEM((t + 2 * PAD, cw), F32), pltpu.VMEM((t + 2 * PAD, cw), F32)],
        compiler_params=_cparams(("parallel",)),
        name=name,
    )(x, w, b.reshape(1, c), dact)


def _conv_glu_fwd(hid, w, b, *, side=None, name):
    t, c2 = hid.shape
    f = c2 // 2
    kw = w.shape[0]
    cw = _tile(f, (256, 128))
    nf = f // cw
    rc = _tile(t, (ROW_CHUNK,))

    def body(g_ref, v_ref, wg_ref, wv_ref, bg_ref, bv_ref, o_ref, gpad_ref, vpad_ref):
        _fill_pad(gpad_ref, g_ref, t)
        _fill_pad(vpad_ref, v_ref, t)
        for r0 in range(0, t, rc):
            gate = _conv_taps(gpad_ref, wg_ref, r0, rc, kw) + bg_ref[...]
            val = _conv_taps(vpad_ref, wv_ref, r0, rc, kw) + bv_ref[...]
            o_ref[pl.ds(r0, rc), :] = (gate * _sigmoid(gate) * val).astype(o_ref.dtype)

    gs = pl.BlockSpec((t, cw), lambda i: (0, i))
    vs = pl.BlockSpec((t, cw), lambda i: (0, i + nf))
    b2 = b.reshape(1, c2)
    (act,), side_outs = _call(
        body,
        grid=(nf,),
        in_specs=[gs, vs, pl.BlockSpec((kw, cw), lambda i: (0, i)), pl.BlockSpec((kw, cw), lambda i: (0, i + nf)),
                  pl.BlockSpec((1, cw), lambda i: (0, i)), pl.BlockSpec((1, cw), lambda i: (0, i + nf))],
        out_specs=[gs],
        out_shape=[jax.ShapeDtypeStruct((t, f), BF16)],
        scratch_shapes=[pltpu.VMEM((t + 2 * PAD, cw), F32), pltpu.VMEM((t + 2 * PAD, cw), F32)],
        sem=("parallel",),
        name=name,
        args=(hid, hid, w, w, b2, b2),
        side=side,
    )
    return act, side_outs


def _conv_glu_bwd(hid, w, b, dact, *, name):
    t, c2 = hid.shape
    f = c2 // 2
    kw = w.shape[0]
    cw = _tile(f, (128,))
    nf = f // cw
    rc = _tile(t, (ROW_CHUNK,))

    def body(g_ref, v_ref, wg_ref, wv_ref, bg_ref, bv_ref, da_ref,
             dgv_ref, dwg_ref, dwv_ref, dbg_ref, dbv_ref,
             gpad_ref, vpad_ref, dgpad_ref, dvpad_ref):
        _fill_pad(gpad_ref, g_ref, t)
        _fill_pad(vpad_ref, v_ref, t)
        for ref in (dgpad_ref, dvpad_ref):
            ref[0:PAD, :] = jnp.zeros((PAD, cw), F32)
            ref[pl.ds(PAD + t, PAD), :] = jnp.zeros((PAD, cw), F32)
        for r0 in range(0, t, rc):
            gate = _conv_taps(gpad_ref, wg_ref, r0, rc, kw) + bg_ref[...]
            val = _conv_taps(vpad_ref, wv_ref, r0, rc, kw) + bv_ref[...]
            sg = _sigmoid(gate)
            da = da_ref[pl.ds(r0, rc), :].astype(F32)
            dgpad_ref[pl.ds(PAD + r0, rc), :] = da * val * (sg * (1.0 + gate * (1.0 - sg)))
            dvpad_ref[pl.ds(PAD + r0, rc), :] = da * (gate * sg)
        _conv_bwd_core(dgpad_ref, gpad_ref, wg_ref, dgv_ref.at[0], dwg_ref, dbg_ref, t, rc, kw)
        _conv_bwd_core(dvpad_ref, vpad_ref, wv_ref, dgv_ref.at[1], dwv_ref, dbv_ref, t, rc, kw)

    gs = pl.BlockSpec((t, cw), lambda i: (0, i))
    vs = pl.BlockSpec((t, cw), lambda i: (0, i + nf))
    wg = pl.BlockSpec((kw, cw), lambda i: (0, i))
    wv = pl.BlockSpec((kw, cw), lambda i: (0, i + nf))
    bg = pl.BlockSpec((1, cw), lambda i: (0, i))
    bv = pl.BlockSpec((1, cw), lambda i: (0, i + nf))
    b2 = b.reshape(1, c2)
    pad = pltpu.VMEM((t + 2 * PAD, cw), F32)
    return pl.pallas_call(
        body,
        grid=(nf,),
        in_specs=[gs, vs, wg, wv, bg, bv, gs],
        out_specs=[pl.BlockSpec((2, t, cw), lambda i: (0, 0, i)), wg, wg, bg, bg],
        out_shape=[jax.ShapeDtypeStruct((2, t, f), F32),
                   jax.ShapeDtypeStruct((kw, f), F32), jax.ShapeDtypeStruct((kw, f), F32),
                   jax.ShapeDtypeStruct((1, f), F32), jax.ShapeDtypeStruct((1, f), F32)],
        scratch_shapes=[pad, pad, pad, pad],
        compiler_params=_cparams(("parallel",)),
        name=name,
    )(hid, hid, w, w, b2, b2, dact)


def _gate_norm_fwd(y, zx, w, *, name):
    t, di = y.shape
    gsz = di // SSM_GROUPS
    tb = _tile(t, (256, 128))

    def body(y_ref, z_ref, w_ref, o_ref):
        for g in range(SSM_GROUPS):
            sl = slice(g * gsz, (g + 1) * gsz)
            zv = z_ref[:, sl]
            gv = y_ref[:, sl] * (zv * _sigmoid(zv))
            r = lax.rsqrt(jnp.mean(gv * gv, axis=-1, keepdims=True) + EPS)
            o_ref[:, sl] = (gv * r * w_ref[:, sl]).astype(o_ref.dtype)

    row = pl.BlockSpec((tb, di), lambda i: (i, 0))
    return pl.pallas_call(
        body,
        grid=(t // tb,),
        in_specs=[row, row, pl.BlockSpec((1, di), lambda i: (0, 0))],
        out_specs=row,
        out_shape=jax.ShapeDtypeStruct((t, di), BF16),
        compiler_params=_cparams(("parallel",)),
        name=name,
    )(y, zx, w.reshape(1, di))


def _gate_norm_bwd(y, zx, w, dyn, *, name):
    t, di = y.shape
    gsz = di // SSM_GROUPS
    tb = _tile(t, (256, 128))

    def body(y_ref, z_ref, w_ref, d_ref, dy_ref, dz_ref, dw_ref):
        i = pl.program_id(0)
        for g in range(SSM_GROUPS):
            sl = slice(g * gsz, (g + 1) * gsz)
            zv = z_ref[:, sl]
            yv = y_ref[:, sl]
            sg = _sigmoid(zv)
            sz = zv * sg
            gv = yv * sz
            r = lax.rsqrt(jnp.mean(gv * gv, axis=-1, keepdims=True) + EPS)
            gn = gv * r
            dn = d_ref[:, sl].astype(F32)
            q = dn * w_ref[:, sl]
            dg = r * (q - gn * jnp.mean(q * gn, axis=-1, keepdims=True))
            dy_ref[:, sl] = dg * sz
            dz_ref[:, sl] = dg * yv * (sg * (1.0 + zv * (1.0 - sg)))
            dwp = jnp.sum(dn * gn, axis=0, keepdims=True)

            @pl.when(i == 0)
            def _(sl=sl, dwp=dwp):
                dw_ref[:, sl] = dwp

            @pl.when(i > 0)
            def _(sl=sl, dwp=dwp):
                dw_ref[:, sl] += dwp

    row = pl.BlockSpec((tb, di), lambda i: (i, 0))
    vec = pl.BlockSpec((1, di), lambda i: (0, 0))
    return pl.pallas_call(
        body,
        grid=(t // tb,),
        in_specs=[row, row, vec, row],
        out_specs=[row, row, vec],
        out_shape=[jax.ShapeDtypeStruct((t, di), F32), jax.ShapeDtypeStruct((t, di), F32),
                   jax.ShapeDtypeStruct((1, di), F32)],
        compiler_params=_cparams(("arbitrary",)),
        name=name,
    )(y, zx, w.reshape(1, di), dyn)


def _adamw(w, g, m, v, *, name):
    shape = w.shape
    cols = shape[-1]
    rows = w.size // cols
    w2, g2, m2, v2 = (a.reshape(rows, cols) for a in (w, g, m, v))
    tr = rows
    if rows * cols * 4 > ADAM_BLOCK_BYTES:
        tr = _tile(rows, tuple(r for r in (512, 256, 128, 64, 32, 16, 8) if r * cols * 4 <= ADAM_BLOCK_BYTES))
    c1 = 1.0 - ADAM_B1 ** ADAM_STEP
    c2 = 1.0 - ADAM_B2 ** ADAM_STEP

    def body(w_ref, g_ref, m_ref, v_ref, d_ref, nm_ref, nv_ref):
        gv = g_ref[...]
        nm = ADAM_B1 * m_ref[...] + (1.0 - ADAM_B1) * gv
        nv = ADAM_B2 * v_ref[...] + (1.0 - ADAM_B2) * (gv * gv)
        d_ref[...] = -ADAM_LR * ((nm / c1) / (jnp.sqrt(nv / c2) + ADAM_EPS) + ADAM_WD * w_ref[...])
        nm_ref[...] = nm
        nv_ref[...] = nv

    blk = pl.BlockSpec((tr, cols), lambda i: (i, 0))
    outs = pl.pallas_call(
        body,
        grid=(rows // tr,),
        in_specs=[blk] * 4,
        out_specs=[blk] * 3,
        out_shape=[jax.ShapeDtypeStruct((rows, cols), F32)] * 3,
        compiler_params=_cparams(("parallel",)),
        name=name,
    )(w2, g2, m2, v2)
    return tuple(o.reshape(shape) for o in outs)


def _adamw_layers(w, gs, m, v, *, name):
    n_l, rows, cols = w.shape
    assert len(gs) == n_l
    tr = _tile(rows, tuple(r for r in (512, 256, 128, 64, 32, 16, 8) if r * cols * 4 <= ADAM_BLOCK_BYTES))
    c1 = 1.0 - ADAM_B1 ** ADAM_STEP
    c2 = 1.0 - ADAM_B2 ** ADAM_STEP

    def body(*refs):
        w_ref, m_ref, v_ref = refs[:3]
        g_refs = refs[3:3 + n_l]
        g_ref, d_ref, nm_ref, nv_ref = refs[3 + n_l:]
        layer = pl.program_id(0)
        gv = g_refs[0][...]
        for q in range(1, n_l):
            gv = jnp.where(layer == q, g_refs[q][...], gv)
        nm = ADAM_B1 * m_ref[...] + (1.0 - ADAM_B1) * gv
        nv = ADAM_B2 * v_ref[...] + (1.0 - ADAM_B2) * (gv * gv)
        g_ref[...] = gv
        d_ref[...] = -ADAM_LR * ((nm / c1) / (jnp.sqrt(nv / c2) + ADAM_EPS) + ADAM_WD * w_ref[...])
        nm_ref[...] = nm
        nv_ref[...] = nv

    stacked = pl.BlockSpec((None, tr, cols), lambda l, i: (l, i, 0))
    single = pl.BlockSpec((tr, cols), lambda l, i: (i, 0))
    return pl.pallas_call(
        body,
        grid=(n_l, rows // tr),
        in_specs=[stacked] * 3 + [single] * n_l,
        out_specs=[stacked] * 4,
        out_shape=[jax.ShapeDtypeStruct(w.shape, F32)] * 4,
        compiler_params=_cparams(("parallel", "parallel")),
        name=name,
    )(w, m, v, *gs)


def _ssd_scalars(dtc_ref, dtr_ref, hpc_ref, hpr_ref, ln):
    assert SSM_CHUNK == SSM_STATE == LANES, "the SSD kernels mix chunk, state and lane-wide tiles freely"
    bias_c, alog_c = hpc_ref[0, 0:1, :], hpc_ref[0, 1:2, :]
    bias_r, alog_r = hpr_ref[0, :, 0:1], hpr_ref[0, :, 1:2]
    a_c, a_r = -jnp.exp(alog_c), -jnp.exp(alog_r)
    raw_c = dtc_ref[0] + bias_c
    dt_c = _softplus(raw_c)
    dt_r = _softplus(dtr_ref[0] + bias_r)
    row = lax.broadcasted_iota(jnp.int32, (ln, ln), 0)
    col = lax.broadcasted_iota(jnp.int32, (ln, ln), 1)
    lower = (col <= row).astype(F32)
    upper = (row <= col).astype(F32)
    acs_c = _ones_dot(lower, dt_c * a_c, ones_left=True)
    acs_r = _ones_dot(upper, dt_r * a_r, ones_left=False)
    return raw_c, dt_c, a_c, acs_c, acs_r, row, col


def _ssd_specs(t, di, g_n, n_st, rp, ln, r_h, rev):
    nc = t // ln
    cidx = (lambda c: nc - 1 - c) if rev else (lambda c: c)
    xs = pl.BlockSpec((ln, rp), lambda g, c: (cidx(c), g))
    bm = pl.BlockSpec((ln, n_st), lambda g, c: (cidx(c), di // n_st + g))
    cm = pl.BlockSpec((ln, n_st), lambda g, c: (cidx(c), di // n_st + g_n + g))
    dtc = pl.BlockSpec((1, ln, r_h), lambda g, c: (g, cidx(c), 0))
    dtr = pl.BlockSpec((1, r_h, ln), lambda g, c: (g, 0, cidx(c)))
    hpc = pl.BlockSpec((1, 3, r_h), lambda g, c: (g, 0, 0))
    hpr = pl.BlockSpec((1, r_h, 3), lambda g, c: (g, 0, 0))
    prev = pl.BlockSpec((1, rp, n_st), lambda g, c: (cidx(c), g, 0))
    return xs, bm, cm, dtc, dtr, hpc, hpr, prev


def _ssd_fwd(xbc, dtc, dtr, hpc, hpr, *, side=None, name):
    t = xbc.shape[0]
    di, g_n, n_st, p_h, ln = D_INNER, SSM_GROUPS, SSM_STATE, SSM_HEAD_DIM, SSM_CHUNK
    r_h = SSM_HEADS // g_n
    rp = r_h * p_h
    nc = t // ln

    def body(xs_ref, b_ref, c_ref, dtc_ref, dtr_ref, hpc_ref, hpr_ref, y_ref, prev_ref, st_ref):
        @pl.when(pl.program_id(1) == 0)
        def _():
            st_ref[...] = jnp.zeros_like(st_ref)

        _, dt_c, _, acs_c, acs_r, row, col = _ssd_scalars(dtc_ref, dtr_ref, hpc_ref, hpr_ref, ln)
        bm = b_ref[...]
        cm = c_ref[...]
        cm16 = cm.astype(BF16)
        cb = _nt(cm16, bm.astype(BF16))
        causal = row >= col
        for r in range(r_h):
            sl = slice(r * p_h, (r + 1) * p_h)
            xs = xs_ref[:, sl]
            acs = jnp.broadcast_to(acs_c[:, r:r + 1], (ln, ln))
            last = acs[ln - 1:ln, :]
            lm = jnp.where(causal, jnp.exp(acs - acs_r[r:r + 1, :]), 0.0)
            xd = (xs * jnp.broadcast_to(dt_c[:, r:r + 1], (ln, p_h))).astype(BF16)
            prev = st_ref[sl, :]
            y = _nn((cb * lm).astype(BF16), xd)
            y = y + _nt(cm16, prev.astype(BF16)) * jnp.exp(acs[:, :p_h])
            y_ref[:, sl] = y + hpc_ref[0, 2:3, r:r + 1] * xs
            prev_ref[0, sl, :] = prev
            bd = (bm * jnp.exp(last - acs[:, :n_st])).astype(BF16)
            st_ref[sl, :] = prev * jnp.exp(last[:, :n_st]) + _tn(xd, bd)

    xs, bm, cm, dtcs, dtrs, hpcs, hprs, prev = _ssd_specs(t, di, g_n, n_st, rp, ln, r_h, False)
    (y, prev_out), side_outs = _call(
        body,
        grid=(g_n, nc),
        in_specs=[xs, bm, cm, dtcs, dtrs, hpcs, hprs],
        out_specs=[xs, prev],
        out_shape=[jax.ShapeDtypeStruct((t, di), F32), jax.ShapeDtypeStruct((nc, g_n * rp, n_st), F32)],
        scratch_shapes=[pltpu.VMEM((rp, n_st), F32)],
        sem=("parallel", "arbitrary"),
        name=name,
        args=(xbc, xbc, xbc, dtc, dtr, hpc, hpr),
        side=side,
    )
    return y, prev_out, side_outs


def _ssd_bwd(xbc, dtc, dtr, hpc, hpr, prev, dy, *, side=None, name):
    t = xbc.shape[0]
    di, g_n, n_st, p_h, ln = D_INNER, SSM_GROUPS, SSM_STATE, SSM_HEAD_DIM, SSM_CHUNK
    r_h = SSM_HEADS // g_n
    rp = r_h * p_h
    nc = t // ln

    def body(xs_ref, b_ref, c_ref, dtc_ref, dtr_ref, hpc_ref, hpr_ref, prev_ref, dy_ref,
             dxs_ref, db_ref, dc_ref, ddt_ref, hg_ref, ds_ref):
        step = pl.program_id(1)

        @pl.when(step == 0)
        def _():
            ds_ref[...] = jnp.zeros_like(ds_ref)

        raw_c, dt_c, a_c, acs_c, acs_r, row, col = _ssd_scalars(dtc_ref, dtr_ref, hpc_ref, hpr_ref, ln)
        bm = b_ref[...]
        cm = c_ref[...]
        bm16, cm16 = bm.astype(BF16), cm.astype(BF16)
        cb = _nt(cm16, bm16)
        cbt = _nt(bm16, cm16)
        lane_r = lax.broadcasted_iota(jnp.int32, (ln, r_h), 1)
        dacs_all = jnp.zeros((ln, r_h), F32)
        ddtx_all = jnp.zeros((ln, r_h), F32)
        dd_all = jnp.zeros((ln, r_h), F32)
        dcb = jnp.zeros((ln, ln), F32)
        dcbt = jnp.zeros((ln, ln), F32)
        dc_acc = jnp.zeros((ln, n_st), F32)
        db_acc = jnp.zeros((ln, n_st), F32)
        for r in range(r_h):
            sl = slice(r * p_h, (r + 1) * p_h)
            xs = xs_ref[:, sl]
            dyv = dy_ref[:, sl]
            dy16 = dyv.astype(BF16)
            acs = jnp.broadcast_to(acs_c[:, r:r + 1], (ln, ln))
            dtv = jnp.broadcast_to(dt_c[:, r:r + 1], (ln, p_h))
            acsr = acs_r[r:r + 1, :]
            last = acs[ln - 1:ln, :]
            xd = xs * dtv
            xd16 = xd.astype(BF16)
            lm = jnp.where(row >= col, jnp.exp(acs - acsr), 0.0)
            lmt = jnp.where(col >= row, jnp.exp(acsr - acs), 0.0)
            m_ls = cb * lm
            m_sl = cbt * lmt
            dm = _nt(dy16, xd16)
            dmt = _nt(xd16, dy16)
            dxd = _nn(m_sl.astype(BF16), dy16)
            dacs = _row_sums(dm * m_ls - dmt * m_sl)
            dcb = dcb + dm * lm
            dcbt = dcbt + dmt * lmt
            prev = prev_ref[0, sl, :]
            prev16 = prev.astype(BF16)
            e = jnp.exp(acs[:, :p_h])
            y_off = _nt(cm16, prev16) * e
            dacs = dacs + _row_sums(dyv * y_off)
            dyo16 = (dyv * e).astype(BF16)
            dc_acc = dc_acc + _nn(dyo16, prev16)
            dprev = _tn(dyo16, cm16)
            ds = ds_ref[sl, :]
            ds16 = ds.astype(BF16)
            decay = jnp.exp(last - acs)[:, :n_st]
            bd16 = (bm * decay).astype(BF16)
            dbd = _nn(xd16, ds16)
            dxd = dxd + _nt(bd16, ds16)
            db_acc = db_acc + dbd * decay
            tdec = _row_sums(dbd * bm, 2) * decay
            cd = jnp.exp(last)
            dlast = (jnp.sum(tdec, axis=0, keepdims=True)
                     + jnp.sum(_row_sums(prev * ds, 2), axis=0, keepdims=True) * cd)
            ds_ref[sl, :] = dprev + cd[:, :n_st] * ds
            dskip = hpc_ref[0, 2:3, r:r + 1]
            dxs_ref[:, sl] = dxd * dtv + dskip * dyv
            dacs = dacs - tdec + jnp.where(row == ln - 1, dlast, 0.0)
            dacs_all = jnp.where(lane_r == r, dacs[:, :r_h], dacs_all)
            ddtx_all = jnp.where(lane_r == r, _row_sums(dxd * xs, 2)[:, :r_h], ddtx_all)
            dd_all = jnp.where(lane_r == r, _row_sums(dyv * xs, 2)[:, :r_h], dd_all)
        dc_ref[...] = dc_acc + _nn(dcb.astype(BF16), bm16)
        db_ref[...] = db_acc + _nn(dcbt.astype(BF16), cm16)
        upper = (row <= col).astype(F32)
        dad = _ones_dot(upper, dacs_all, ones_left=True)
        ddt = dad * a_c + ddtx_all
        ddt_raw = ddt * _sigmoid(raw_c)
        ddt_ref[0] = ddt_raw
        d_bias = jnp.sum(ddt_raw, axis=0, keepdims=True)
        d_alog = jnp.sum(dad * dt_c, axis=0, keepdims=True) * a_c
        d_d = jnp.sum(dd_all, axis=0, keepdims=True)
        hg = jnp.concatenate([d_bias, d_alog, d_d], axis=0)

        @pl.when(step == 0)
        def _():
            hg_ref[0] = hg

        @pl.when(step > 0)
        def _():
            hg_ref[0] += hg

    xs, bms, cms, dtcs, dtrs, hpcs, hprs, prevs = _ssd_specs(t, di, g_n, n_st, rp, ln, r_h, True)
    bout = pl.BlockSpec((ln, n_st), lambda g, c: (nc - 1 - c, g))
    outs, side_outs = _call(
        body,
        grid=(g_n, nc),
        in_specs=[xs, bms, cms, dtcs, dtrs, hpcs, hprs, prevs, xs],
        out_specs=[xs, bout, bout, dtcs, hpcs],
        out_shape=[jax.ShapeDtypeStruct((t, di), F32), jax.ShapeDtypeStruct((t, g_n * n_st), F32),
                   jax.ShapeDtypeStruct((t, g_n * n_st), F32), jax.ShapeDtypeStruct((g_n, t, r_h), F32),
                   jax.ShapeDtypeStruct((g_n, 3, r_h), F32)],
        scratch_shapes=[pltpu.VMEM((rp, n_st), F32)],
        sem=("parallel", "arbitrary"),
        name=name,
        args=(xbc, xbc, xbc, dtc, dtr, hpc, hpr, prev, dy),
        side=side,
    )
    return (*outs, side_outs)


SB_KEYS = 256
SB_QUERIES = (256,)
SB_CUTOFF = 110.0
SB_PIECES = 2


def _sb_logits(qs, kv, valid):
    z = _nt(qs, kv)
    nz = -z
    lg = jnp.minimum(nz, 0.0) - jnp.log(1.0 + jnp.exp(jnp.minimum(z, nz)))
    return z + lg, (lg if valid is None else jnp.where(valid, lg, 0.0))


def _sb_iota(tq):
    diff = lax.broadcasted_iota(jnp.int32, (tq, SB_KEYS), 1) - lax.broadcasted_iota(jnp.int32, (tq, SB_KEYS), 0)
    krow = lax.broadcasted_iota(jnp.int32, (SB_KEYS, SB_KEYS), 0)
    kcol = lax.broadcasted_iota(jnp.int32, (SB_KEYS, SB_KEYS), 1)
    return diff, krow, kcol


def _sb_scale(d):
    scale = 1.0 / math.sqrt(d)
    assert math.frexp(scale)[0] == 0.5, "the scale is folded into bf16 queries: it must be a power of two"
    return scale


def _key_rows(j):
    return pl.ds(pl.multiple_of(j * SB_KEYS, SB_KEYS), SB_KEYS)


def _sb_fwd(q, k, v, *, side=None, name):
    h, t, d = q.shape
    tq = _tile(t, SB_QUERIES)
    nq = t // tq
    kpq = tq // SB_KEYS
    scale = _sb_scale(d)

    def body(q_ref, k_ref, v_ref, o_ref, lt_ref, first_ref):
        i = pl.program_id(1)
        qs = (q_ref[0].astype(F32) * scale).astype(BF16)
        diff, krow, kcol = _sb_iota(tq)
        later = (krow > kcol).astype(F32)

        def block(j, carry, valid):
            acc, cl = carry
            rows = _key_rows(j)
            ls, lg = _sb_logits(qs, k_ref[0, rows, :], valid)
            cs = _ones_dot(later, lg, ones_left=False, pieces=SB_PIECES)
            att = jnp.exp(ls + (cs + cl))
            if valid is not None:
                att = jnp.where(valid, att, 0.0)
            acc = acc + _nn(att.astype(BF16), v_ref[0, rows, :])
            return acc, cl + (cs[:, 0:1] + lg[:, 0:1])

        carry = (jnp.zeros((tq, d), F32), jnp.zeros((tq, 1), F32))
        for m in range(kpq - 1, -1, -1):
            carry = block(i * kpq + m, carry, diff < -m * SB_KEYS)
        nb = i * kpq

        def more(st):
            s, _, cl = st
            return jnp.logical_and(s < nb, jnp.max(cl) > -SB_CUTOFF)

        def step(st):
            s, acc, cl = st
            acc, cl = block(nb - 1 - s, (acc, cl), None)
            return s + 1, acc, cl

        walked, acc, cl = lax.while_loop(more, step, (jnp.int32(0),) + carry)
        o_ref[0] = acc
        lt_ref[0] = cl
        first_ref[pl.program_id(0), i] = nb - walked

    qs = pl.BlockSpec((1, tq, d), lambda hh, i: (hh, i, 0))
    ls = pl.BlockSpec((1, tq, 1), lambda hh, i: (hh, i, 0))
    ks = pl.BlockSpec((1, t, d), lambda hh, i: (hh, 0, 0))
    outs, side_outs = _call(
        body,
        grid=(h, nq),
        in_specs=[qs, ks, ks],
        out_specs=[qs, ls, pl.BlockSpec(memory_space=pltpu.SMEM)],
        out_shape=[jax.ShapeDtypeStruct((h, t, d), F32), jax.ShapeDtypeStruct((h, t, 1), F32),
                   jax.ShapeDtypeStruct((h, nq), jnp.int32)],
        sem=("arbitrary", "arbitrary"),
        name=name,
        args=(q, k, v),
        side=side,
    )
    return (*outs, side_outs)


def _sb_bwd(q, k, v, lt, first, do, *, name):
    h, t, d = q.shape
    tq = _tile(t, SB_QUERIES)
    nq = t // tq
    kpq = tq // SB_KEYS
    scale = _sb_scale(d)
    last = SB_KEYS - 1

    def body(q_ref, k_ref, v_ref, lt_ref, first_ref, do_ref, dq_ref, dk_ref, dv_ref):
        i = pl.program_id(1)

        @pl.when(i == 0)
        def _():
            dk_ref[...] = jnp.zeros_like(dk_ref)
            dv_ref[...] = jnp.zeros_like(dv_ref)

        qs = (q_ref[0].astype(F32) * scale).astype(BF16)
        do16 = do_ref[0].astype(BF16)
        ltot = lt_ref[0]
        diff, krow, kcol = _sb_iota(tq)
        upto = (krow <= kcol).astype(F32)
        before = (krow < kcol).astype(F32)

        def block(j, carry, valid):
            dq, pl_sum, pg_sum = carry
            rows = _key_rows(j)
            kv = k_ref[0, rows, :]
            vv = v_ref[0, rows, :]
            ls, lg = _sb_logits(qs, kv, valid)
            pre = _ones_dot(upto, lg, ones_left=False, pieces=SB_PIECES)
            att = jnp.exp(ls + (ltot - (pre + pl_sum)))
            if valid is not None:
                att = jnp.where(valid, att, 0.0)
            g = att * _nt(do16, vv)
            gpre = _ones_dot(before, g, ones_left=False, pieces=SB_PIECES)
            sig = jnp.exp(ls)
            dz16 = (g - sig * (g + (gpre + pg_sum))).astype(BF16)
            if valid is not None:
                dz16 = jnp.where(valid, dz16, jnp.zeros_like(dz16))
            dq = dq + _nn(dz16, kv)
            dk_ref[0, rows, :] += _tn(dz16, qs)
            dv_ref[0, rows, :] += _tn(att.astype(BF16), do16)
            return dq, pl_sum + pre[:, last:], pg_sum + (gpre[:, last:] + g[:, last:])

        zero = jnp.zeros((tq, 1), F32)
        nb = i * kpq
        start = jnp.clip(first_ref[pl.program_id(0), i], 0, nb)
        carry = lax.fori_loop(start, nb, lambda j, cr: block(j, cr, None), (jnp.zeros((tq, d), F32), zero, zero))
        for m in range(kpq):
            carry = block(nb + m, carry, diff < -m * SB_KEYS)
        dq_ref[0] = carry[0] * scale

    qs = pl.BlockSpec((1, tq, d), lambda hh, i: (hh, i, 0))
    ls = pl.BlockSpec((1, tq, 1), lambda hh, i: (hh, i, 0))
    ks = pl.BlockSpec((1, t, d), lambda hh, i: (hh, 0, 0))
    full = jax.ShapeDtypeStruct((h, t, d), F32)
    return pl.pallas_call(
        body,
        grid=(h, nq),
        in_specs=[qs, ks, ks, ls, pl.BlockSpec(memory_space=pltpu.SMEM), qs],
        out_specs=[qs, ks, ks],
        out_shape=[full, full, full],
        compiler_params=_cparams(("arbitrary", "arbitrary")),
        name=name,
    )(q, k, v, lt, first, do)


def _row_tile(rows, cols):
    return _tile(rows, tuple(r for r in (2048, 1024, 512, 256, 128, 64, 32, 16, 8) if r * cols * 4 <= ADAM_BLOCK_BYTES))


def _sum_leading(x, *, name):
    n, rows, cols = x.shape
    tr = _row_tile(rows, cols)

    def body(x_ref, o_ref):
        acc = x_ref[0].astype(F32)
        for q in range(1, n):
            acc = acc + x_ref[q].astype(F32)
        o_ref[...] = acc

    return pl.pallas_call(
        body,
        grid=(rows // tr,),
        in_specs=[pl.BlockSpec((n, tr, cols), lambda i: (0, i, 0))],
        out_specs=pl.BlockSpec((tr, cols), lambda i: (i, 0)),
        out_shape=jax.ShapeDtypeStruct((rows, cols), F32),
        compiler_params=_cparams(("parallel",)),
        name=name,
    )(x)


def _pair_add(g4h, recv, c, *, out_dtype, name):
    n, _, rows, cols = g4h.shape
    tr = _row_tile(rows, cols)

    def body(c_ref, g_ref, r_ref, o_ref):
        o_ref[...] = (g_ref[...] + r_ref[...]).astype(o_ref.dtype)

    blk = pl.BlockSpec((1, tr, cols), lambda q, i, c_ref: (q, i, 0))
    return pl.pallas_call(
        body,
        grid_spec=pltpu.PrefetchScalarGridSpec(
            num_scalar_prefetch=1,
            grid=(n, rows // tr),
            in_specs=[pl.BlockSpec((1, None, tr, cols), lambda q, i, c_ref: (q, c_ref[0], i, 0)), blk],
            out_specs=blk),
        out_shape=jax.ShapeDtypeStruct((n, rows, cols), out_dtype),
        compiler_params=_cparams(("parallel", "parallel")),
        name=name,
    )(c.reshape(1).astype(jnp.int32), g4h, recv)


ANY = pl.BlockSpec(memory_space=pl.ANY)


def _other_chips(x, y):
    return [(1 - x, y), (x, 1 - y), (1 - x, 1 - y)]


def _gather_chips(shard, *, name):
    def body(x_ref, o_ref, send_sems, recv_sems, local_sem):
        x, y, c = lax.axis_index("x"), lax.axis_index("y"), lax.axis_index("c")
        me = 2 * x + y
        mine = pltpu.make_async_copy(x_ref, o_ref.at[me], local_sem)
        mine.start()
        chips = _other_chips(x, y)
        sends = [pltpu.make_async_remote_copy(src_ref=x_ref, dst_ref=o_ref.at[me], send_sem=send_sems.at[q],
                                              recv_sem=recv_sems.at[q], device_id=(px, py, c), device_id_type=MESH)
                 for q, (px, py) in enumerate(chips)]
        for cp in sends:
            cp.start()
        for q, (px, py) in enumerate(chips):
            pltpu.make_async_remote_copy(src_ref=x_ref, dst_ref=o_ref.at[2 * px + py], send_sem=send_sems.at[q],
                                         recv_sem=recv_sems.at[q], device_id=(px, py, c), device_id_type=MESH).wait_recv()
        for cp in sends:
            cp.wait_send()
        mine.wait()

    return pl.pallas_call(
        body,
        in_specs=[ANY],
        out_specs=ANY,
        out_shape=jax.ShapeDtypeStruct((4,) + shard.shape, shard.dtype),
        scratch_shapes=[pltpu.SemaphoreType.DMA((3,)), pltpu.SemaphoreType.DMA((3,)), pltpu.SemaphoreType.DMA],
        compiler_params=pltpu.CompilerParams(has_side_effects=True),
        name=name,
    )(shard)


def _comm_call(body, ins, out_shapes, n_sems, name):
    n = len(ins)

    def wrapped(*refs):
        body(refs[:n], refs[n:n + len(out_shapes)], refs[-2], refs[-1])

    return pl.pallas_call(
        wrapped,
        in_specs=[ANY] * n,
        out_specs=[ANY] * len(out_shapes),
        out_shape=out_shapes,
        scratch_shapes=[pltpu.SemaphoreType.DMA((n_sems,)), pltpu.SemaphoreType.DMA((n_sems,))],
        compiler_params=pltpu.CompilerParams(has_side_effects=True),
        name=name,
    )(*ins)


def _remote(send_sems, recv_sems, q, src, dst, to):
    return pltpu.make_async_remote_copy(src_ref=src, dst_ref=dst, send_sem=send_sems.at[q], recv_sem=recv_sems.at[q],
                                        device_id=to, device_id_type=MESH)


def _scatter_chips(parts, *, name):
    return _run_job(_scatter_job(parts), name)


def _scatter_job(parts):
    def sends(ins, outs, send_sems, recv_sems):
        x, y, c = lax.axis_index("x"), lax.axis_index("y"), lax.axis_index("c")
        return [_remote(send_sems, recv_sems, 3 * i + q, p.at[2 * px + py], o.at[2 * x + y], (px, py, c))
                for i, (p, o) in enumerate(zip(ins, outs)) for q, (px, py) in enumerate(_other_chips(x, y))]

    def start(ins, outs, send_sems, recv_sems):
        for cp in sends(ins, outs, send_sems, recv_sems):
            cp.start()

    def finish(ins, outs, send_sems, recv_sems):
        x, y, c = lax.axis_index("x"), lax.axis_index("y"), lax.axis_index("c")
        for i, (p, o) in enumerate(zip(ins, outs)):
            for q, (px, py) in enumerate(_other_chips(x, y)):
                _remote(send_sems, recv_sems, 3 * i + q, p.at[2 * x + y], o.at[2 * px + py], (px, py, c)).wait_recv()
        for cp in sends(ins, outs, send_sems, recv_sems):
            cp.wait_send()

    return _SideJob(parts, [jax.ShapeDtypeStruct(p.shape, p.dtype) for p in parts], 3 * len(parts), start, finish)


def _run_job(job, name):
    return _comm_call(lambda *refs: (job.start(*refs), job.finish(*refs)), job.ins, job.out_shapes, job.n_sems, name)


def _gather_job(shards):
    def sends(ins, outs, send_sems, recv_sems):
        x, y, c = lax.axis_index("x"), lax.axis_index("y"), lax.axis_index("c")
        return [_remote(send_sems, recv_sems, 6 * i + q, s.at[c], o.at[2 * x + y, c], (px, py, c))
                for i, (s, o) in enumerate(zip(ins, outs)) for q, (px, py) in enumerate(_other_chips(x, y))]

    def start(ins, outs, send_sems, recv_sems):
        for cp in sends(ins, outs, send_sems, recv_sems):
            cp.start()

    def finish(ins, outs, send_sems, recv_sems):
        x, y, c = lax.axis_index("x"), lax.axis_index("y"), lax.axis_index("c")
        sibling = (x, y, 1 - c)
        chips = _other_chips(x, y)
        copy = lambda q, src, dst, to: _remote(send_sems, recv_sems, q, src, dst, to)
        passed = []
        for i, (s, o) in enumerate(zip(ins, outs)):
            for q, (px, py) in enumerate(chips):
                slot = o.at[2 * px + py, c]
                copy(6 * i + q, s.at[c], slot, (px, py, c)).wait_recv()
                passed.append(copy(6 * i + 3 + q, slot, slot, sibling))
                passed[-1].start()
        for i, (s, o) in enumerate(zip(ins, outs)):
            for q, (px, py) in enumerate(chips):
                copy(6 * i + 3 + q, s.at[1 - c], o.at[2 * px + py, 1 - c], sibling).wait_recv()
        for cp in sends(ins, outs, send_sems, recv_sems) + passed:
            cp.wait_send()

    return _SideJob(shards, [jax.ShapeDtypeStruct((N_CHIPS,) + s.shape, s.dtype) for s in shards], 6 * len(shards),
                    start, finish)


def _swap_other_half(gs, *, name):
    def body(ins, outs, send_sems, recv_sems):
        x, y, c = lax.axis_index("x"), lax.axis_index("y"), lax.axis_index("c")
        copies = [_remote(send_sems, recv_sems, i, g.at[pl.ds(0, g.shape[0]), 1 - c], o, (x, y, 1 - c))
                  for i, (g, o) in enumerate(zip(ins, outs))]
        for cp in copies:
            cp.start()
        for cp in copies:
            cp.wait()

    return _comm_call(body, gs, [jax.ShapeDtypeStruct((g.shape[0],) + g.shape[2:], g.dtype) for g in gs], len(gs), name)


def _join_halves(halves, *, name):
    def body(ins, outs, send_sems, recv_sems):
        x, y, c = lax.axis_index("x"), lax.axis_index("y"), lax.axis_index("c")
        sibling = (x, y, 1 - c)
        sends = [_remote(send_sems, recv_sems, i, h, o.at[c], sibling) for i, (h, o) in enumerate(zip(ins, outs))]
        for cp in sends:
            cp.start()
        for i, (h, o) in enumerate(zip(ins, outs)):
            _remote(send_sems, recv_sems, i, h, o.at[1 - c], sibling).wait_recv()
        for cp in sends:
            cp.wait_send()

    return _comm_call(body, halves, [jax.ShapeDtypeStruct((2,) + h.shape, h.dtype) for h in halves], len(halves), name)


def _gather_all(v, *, name):
    def body(v_ref, o_ref, send_sems, recv_sems, local_sem):
        x, y, c = lax.axis_index("x"), lax.axis_index("y"), lax.axis_index("c")
        me = 4 * x + 2 * y + c
        mine = pltpu.make_async_copy(v_ref, o_ref.at[me], local_sem)
        mine.start()
        peers = [(x ^ (q >> 2 & 1), y ^ (q >> 1 & 1), c ^ (q & 1)) for q in range(1, 8)]
        sends = [pltpu.make_async_remote_copy(src_ref=v_ref, dst_ref=o_ref.at[me], send_sem=send_sems.at[q],
                                              recv_sem=recv_sems.at[q], device_id=peer, device_id_type=MESH)
                 for q, peer in enumerate(peers)]
        for cp in sends:
            cp.start()
        for q, (px, py, pc) in enumerate(peers):
            pltpu.make_async_remote_copy(src_ref=v_ref, dst_ref=o_ref.at[4 * px + 2 * py + pc], send_sem=send_sems.at[q],
                                         recv_sem=recv_sems.at[q], device_id=(px, py, pc), device_id_type=MESH).wait_recv()
        for cp in sends:
            cp.wait_send()
        mine.wait()

    return pl.pallas_call(
        body,
        in_specs=[ANY],
        out_specs=ANY,
        out_shape=jax.ShapeDtypeStruct((8,) + v.shape, v.dtype),
        scratch_shapes=[pltpu.SemaphoreType.DMA((7,)), pltpu.SemaphoreType.DMA((7,)), pltpu.SemaphoreType.DMA],
        compiler_params=pltpu.CompilerParams(has_side_effects=True),
        name=name,
    )(v)


WEIGHTS = ['ssm_norm_w', 'ssm_in_w', 'ssm_conv_w', 'ssm_conv_b', 'ssm_dt_bias', 'ssm_a_log', 'ssm_d',
           'ssm_gate_norm_w', 'ssm_out_w', 'kv_norm_w', 'w_k', 'w_v', 'attn_norm_w', 'w_q', 'w_o',
           'ffn_norm_w', 'ffn_up_w', 'ffn_conv_w', 'ffn_conv_b', 'ffn_down_w', 'final_norm_w']
SHARD_AXIS = {'ssm_norm_w': 1, 'ssm_in_w': 2, 'ssm_conv_w': 2, 'ssm_conv_b': 1, 'ssm_gate_norm_w': 1,
              'ssm_out_w': 1, 'w_k': 0, 'w_v': 0, 'w_q': 1, 'w_o': 1, 'ffn_up_w': 2, 'ffn_conv_w': 2,
              'ffn_down_w': 1}
BIG = ['ssm_in_w', 'ssm_out_w', 'w_k', 'w_v', 'w_q', 'w_o', 'ffn_up_w', 'ffn_down_w']
SMALL = [n for n in WEIGHTS if n in SHARD_AXIS and n not in BIG]
REPLICATED = [n for n in WEIGHTS if n not in SHARD_AXIS]
STACKED = ['ffn_up_w', 'ffn_down_w']
N_CHIPS = 4


PACK_ROWS = 16


def _piece_rows(n):
    return -(-n // (PACK_ROWS * LANES)) * PACK_ROWS


def _pack(arrs, dtype, row_mult):
    lead = arrs[0].shape[:-1]
    pieces, total = [], 0
    for a in arrs:
        n = a.shape[-1]
        rows = _piece_rows(n)
        a = a.astype(dtype)
        if rows * LANES != n:
            a = jnp.pad(a, [(0, 0)] * len(lead) + [(0, rows * LANES - n)])
        pieces.append(a.reshape(lead + (rows, LANES)))
        total += rows
    extra = -total % row_mult
    if extra:
        pieces.append(jnp.zeros(lead + (extra, LANES), dtype))
    return jnp.concatenate(pieces, axis=len(lead))


def _unpack(buf, shapes):
    lead = buf.shape[:-2]
    out, off = [], 0
    for shp in shapes:
        n = math.prod(shp)
        rows = _piece_rows(n)
        piece = lax.slice_in_dim(buf, off, off + rows, axis=len(lead)).reshape(lead + (rows * LANES,))
        out.append(piece[..., :n].reshape(lead + tuple(shp)))
        off += rows
    return out


def _set_slot(buf, piece, index):
    return lax.dynamic_update_slice_in_dim(buf, piece[None], index, axis=0)


def _from_shards(stacked, axis):
    return jnp.concatenate([stacked[j] for j in range(N_CHIPS)], axis=axis)


def _heads(a, h):
    t = a.shape[0]
    return a.reshape(t, h, a.shape[1] // h).transpose(1, 0, 2)


def _unheads(a):
    h, t, d = a.shape
    return a.transpose(1, 0, 2).reshape(t, h * d)


def _ffn_fwd(h, norm_w, w_up, conv_w, conv_b, w_down, tag, side=None):
    u = _rmsnorm_fwd(h, norm_w, name=f"ffn{tag}_norm")
    hid = _matmul(u, w_up, name=f"ffn{tag}_up")
    act, side_outs = _conv_glu_fwd(hid, conv_w, conv_b, side=side, name=f"ffn{tag}_glu")
    out = _matmul(act, w_down, add=h, name=f"ffn{tag}_down")
    return out, (u, hid, act), side_outs


def _ffn_bwd(h, saved, dout, norm_w, w_up, conv_w, conv_b, w_down, tag):
    u, hid, act = saved
    dact = _matmul(dout, w_down, tb=True, name=f"ffn{tag}_down_dx")
    dw_down = _matmul(act, dout, ta=True, name=f"ffn{tag}_down_dw")
    dhid, dwg, dwv, dbg, dbv = _conv_glu_bwd(hid, conv_w, conv_b, dact, name=f"ffn{tag}_glu_bwd")
    du = _matmul(dhid, w_up, tb=True, name=f"ffn{tag}_up_dx")
    dw_up = _matmul(u, dhid, ta=True, out_parts=N_CHIPS, name=f"ffn{tag}_up_dw")
    dh, (dnorm,) = _rmsnorm_bwd(h, [(du, norm_w)], dout, name=f"ffn{tag}_norm_bwd")
    return dh, dict(norm=dnorm[0], up=dw_up, conv_w=jnp.concatenate([dwg, dwv], axis=1),
                    conv_b=jnp.concatenate([dbg, dbv], axis=1)[0], down=dw_down)


class _Pieces:
    def __init__(self, local):
        self.c = lax.axis_index("c")
        self.chip = 2 * lax.axis_index("x") + lax.axis_index("y")
        self.shape, self.s16 = {}, {}
        for n in BIG:
            blk = local[n]
            layers = [(n, l, blk[l]) for l in range(blk.shape[0])] if n in STACKED else [(n, None, blk.reshape(blk.shape[-2:]))]
            for name, l, p in layers:
                self.shape[name, l] = p.shape
                self.s16[name, l] = p.astype(BF16).reshape(2, p.shape[0] // 2, p.shape[1])

    def gather_job(self, keys):
        return _gather_job([self.s16[k] for k in keys])

    def weights(self, keys, gathered):
        out = []
        for k, g in zip(keys, gathered):
            r, cc = self.shape[k]
            by_chip = _set_slot(g, self.s16[k], self.chip).reshape(N_CHIPS, r, cc)
            if k[0] == 'ssm_in_w':
                by_chip = by_chip.transpose(1, 0, 2).reshape(r, N_CHIPS * cc)
            elif k[0] != 'ffn_up_w':
                by_chip = by_chip.reshape(N_CHIPS * r, cc)
            out.append(by_chip)
        return out

    def pair_sums(self, keys, grads, tag):
        gs = []
        for k, g in zip(keys, grads):
            r, cc = self.shape[k]
            if k[0] == 'ssm_in_w':
                g = g.reshape(r, N_CHIPS, cc).transpose(1, 0, 2)
            gs.append(g.reshape(N_CHIPS, 2, r // 2, cc))
        recv = _swap_other_half(gs, name=f"rs_pair_swap_{tag}")
        return [_pair_add(g, rv, self.c, out_dtype=BF16, name=f"rs_pair_add_{tag}{i}") for i, (g, rv) in enumerate(zip(gs, recv))]

    def chip_sums(self, pairs, scattered, tag):
        return [_sum_leading(_set_slot(s, lax.dynamic_index_in_dim(p, self.chip, axis=0, keepdims=False), self.chip),
                             name=f"rs_chip_sum_{tag}{i}") for i, (s, p) in enumerate(zip(scattered, pairs))]

    def shards(self, keys, halves):
        joined = _join_halves(halves, name="rs_half_join")
        return {k: _set_slot(j, h, self.c).reshape(self.shape[k]) for k, h, j in zip(keys, halves, joined)}


def _step(x, target, w, pieces):
    t = x.shape[0]
    g_n, heads = SSM_GROUPS, SSM_HEADS
    r_h = heads // g_n
    di = D_INNER
    zx_cols = di + CONV_DIM
    k_in = [('ssm_in_w', None)]
    k_ffn0 = [('ssm_out_w', None), ('ffn_up_w', 0), ('ffn_down_w', 0)]
    k_qkv = [('w_k', None), ('w_v', None), ('w_q', None)]
    k_late = [('w_o', None), ('ffn_up_w', 1), ('ffn_down_w', 1)]
    (w_in,) = pieces.weights(k_in, _run_job(pieces.gather_job(k_in), "gather_ssm_in"))
    w_zx = w_in[:, :zx_cols]
    w_dt = jnp.pad(w_in[:, zx_cols:], ((0, 0), (0, LANES - heads)))
    conv_w, conv_b = w['ssm_conv_w'][0], w['ssm_conv_b'][0]
    hp = jnp.stack([w['ssm_dt_bias'][0], w['ssm_a_log'][0], w['ssm_d'][0]], axis=0).reshape(3, g_n, r_h)
    hpc, hpr = hp.transpose(1, 0, 2), hp.transpose(1, 2, 0)

    h0 = x
    u0 = _rmsnorm_fwd(h0, w['ssm_norm_w'][0], name="ssm_norm")
    zx = _matmul(u0, w_zx, name="ssm_in_zx")
    dt_raw = _matmul(u0, w_dt, name="ssm_in_dt")[:, :heads]
    dtg = dt_raw.reshape(t, g_n, r_h)
    dtc, dtr = dtg.transpose(1, 0, 2), dtg.transpose(1, 2, 0)
    xbc = _conv_silu_fwd(zx, conv_w, conv_b, x_off=di, name="ssm_conv")
    y, prev, got = _ssd_fwd(xbc, dtc, dtr, hpc, hpr, side=pieces.gather_job(k_ffn0), name="ssd_fwd")
    w_out, w_up0, w_down0 = pieces.weights(k_ffn0, got)
    yn = _gate_norm_fwd(y, zx, w['ssm_gate_norm_w'][0], name="ssm_gate_norm")
    h1 = _matmul(yn, w_out, add=h0, name="ssm_out")
    h2, ffn0, got = _ffn_fwd(h1, w['ffn_norm_w'][0], w_up0, w['ffn_conv_w'][0], w['ffn_conv_b'][0], w_down0, 0,
                             side=pieces.gather_job(k_qkv))
    w_k, w_v, w_q = pieces.weights(k_qkv, got)
    hk = _rmsnorm_fwd(h2, w['kv_norm_w'], name="kv_norm")
    qn = _rmsnorm_fwd(h2, w['attn_norm_w'][0], name="attn_norm")
    k2 = _matmul(hk, w_k, out_dtype=BF16, name="attn_k")
    v2 = _matmul(hk, w_v, out_dtype=BF16, name="attn_v")
    q2 = _matmul(qn, w_q, out_dtype=BF16, name="attn_q")
    qh, kh, vh = _heads(q2, SB_HEADS), _heads(k2, SB_HEADS), _heads(v2, SB_HEADS)
    oh, lt, first, got = _sb_fwd(qh, kh, vh, side=pieces.gather_job(k_late), name="sb_fwd")
    w_o, w_up1, w_down1 = pieces.weights(k_late, got)
    o2 = _unheads(oh)
    h3 = _matmul(o2, w_o, add=h2, name="attn_o")
    h4, ffn1, _ = _ffn_fwd(h3, w['ffn_norm_w'][1], w_up1, w['ffn_conv_w'][1], w['ffn_conv_b'][1], w_down1, 1)
    loss_p, dh4, d_final = _loss_head(h4, w['final_norm_w'], target, name="loss_head")

    dh3, g1 = _ffn_bwd(h3, ffn1, dh4, w['ffn_norm_w'][1], w_up1, w['ffn_conv_w'][1], w['ffn_conv_b'][1], w_down1, 1)
    do2 = _matmul(dh3, w_o, tb=True, name="attn_o_dx")
    dw_o = _matmul(o2, dh3, ta=True, name="attn_o_dw")
    dqh, dkh, dvh = _sb_bwd(qh, kh, vh, lt, first, _heads(do2, SB_HEADS), name="sb_bwd")
    dq2, dk2, dv2 = _unheads(dqh), _unheads(dkh), _unheads(dvh)
    dqn = _matmul(dq2, w_q, tb=True, name="attn_q_dx")
    dw_q = _matmul(qn, dq2, ta=True, name="attn_q_dw")
    dhk = _matmul(dk2, w_k, tb=True, name="attn_k_dx")
    dhk = _matmul(dv2, w_v, tb=True, add=dhk, name="attn_v_dx")
    dw_k = _matmul(hk, dk2, ta=True, name="attn_k_dw")
    dw_v = _matmul(hk, dv2, ta=True, name="attn_v_dw")
    dh2, (d_attn_norm, d_kv_norm) = _rmsnorm_bwd(h2, [(dqn, w['attn_norm_w'][0]), (dhk, w['kv_norm_w'])], dh3,
                                                 name="attn_norms_bwd")
    dh1, g0 = _ffn_bwd(h1, ffn0, dh2, w['ffn_norm_w'][0], w_up0, w['ffn_conv_w'][0], w['ffn_conv_b'][0], w_down0, 0)
    dyn = _matmul(dh1, w_out, tb=True, name="ssm_out_dx")
    dw_out = _matmul(yn, dh1, ta=True, name="ssm_out_dw")
    k_done = k_qkv + k_late + k_ffn0
    pairs_done = pieces.pair_sums(k_done, [dw_k, dw_v, dw_q, dw_o, g1['up'], g1['down'], dw_out, g0['up'], g0['down']], "a")
    dy, dz, d_gate = _gate_norm_bwd(y, zx, w['ssm_gate_norm_w'][0], dyn, name="ssm_gate_norm_bwd")
    dxs, dbm, dcm, ddt_g, hg, scattered_done = _ssd_bwd(xbc, dtc, dtr, hpc, hpr, prev, dy,
                                                        side=_scatter_job(pairs_done), name="ssd_bwd")
    dxbc = jnp.concatenate([dxs, dbm, dcm], axis=1)
    dxbc_pre, d_conv_w, d_conv_b = _conv_silu_bwd(zx, conv_w, conv_b, dxbc, x_off=di, name="ssm_conv_bwd")
    dzx = jnp.concatenate([dz, dxbc_pre], axis=1)
    ddt = jnp.pad(ddt_g.transpose(1, 0, 2).reshape(t, heads), ((0, 0), (0, LANES - heads)))
    du0 = _matmul(dzx, w_zx, tb=True, name="ssm_in_zx_dx")
    du0 = _matmul(ddt, w_dt, tb=True, add=du0, name="ssm_in_dt_dx")
    dw_in = jnp.concatenate([_matmul(u0, dzx, ta=True, name="ssm_in_zx_dw"),
                             _matmul(u0, ddt, ta=True, name="ssm_in_dt_dw")[:, :heads]], axis=1)
    dx, (d_ssm_norm,) = _rmsnorm_bwd(h0, [(du0, w['ssm_norm_w'][0])], dh1, name="ssm_norm_bwd")

    pairs_in = pieces.pair_sums(k_in, [dw_in], "b")
    halves = (pieces.chip_sums(pairs_done, scattered_done, "a")
              + pieces.chip_sums(pairs_in, _scatter_chips(pairs_in, name="rs_chip_scatter_b"), "b"))
    big_grads = pieces.shards(k_done + k_in, halves)

    hgr = hg.transpose(1, 0, 2).reshape(3, heads)
    grads = {
        'ssm_norm_w': d_ssm_norm, 'ssm_conv_w': d_conv_w[None], 'ssm_conv_b': d_conv_b,
        'ssm_dt_bias': hgr[0:1], 'ssm_a_log': hgr[1:2], 'ssm_d': hgr[2:3], 'ssm_gate_norm_w': d_gate,
        'kv_norm_w': d_kv_norm[0], 'attn_norm_w': d_attn_norm, 'ffn_norm_w': jnp.stack([g0['norm'], g1['norm']]),
        'ffn_conv_w': jnp.stack([g0['conv_w'], g1['conv_w']]), 'ffn_conv_b': jnp.stack([g0['conv_b'], g1['conv_b']]),
        'final_norm_w': d_final[0],
    }
    return loss_p, dx, grads, big_grads


def kernel(x, ssm_norm_w, ssm_in_w, ssm_conv_w, ssm_conv_b, ssm_dt_bias, ssm_a_log, ssm_d, ssm_gate_norm_w, ssm_out_w, kv_norm_w, w_k, w_v, attn_norm_w, w_q, w_o, ffn_norm_w, ffn_up_w, ffn_conv_w, ffn_conv_b, ffn_down_w, final_norm_w, loss_target, m_ssm_norm_w, m_ssm_in_w, m_ssm_conv_w, m_ssm_conv_b, m_ssm_dt_bias, m_ssm_a_log, m_ssm_d, m_ssm_gate_norm_w, m_ssm_out_w, m_kv_norm_w, m_w_k, m_w_v, m_attn_norm_w, m_w_q, m_w_o, m_ffn_norm_w, m_ffn_up_w, m_ffn_conv_w, m_ffn_conv_b, m_ffn_down_w, m_final_norm_w, v_ssm_norm_w, v_ssm_in_w, v_ssm_conv_w, v_ssm_conv_b, v_ssm_dt_bias, v_ssm_a_log, v_ssm_d, v_ssm_gate_norm_w, v_ssm_out_w, v_kv_norm_w, v_w_k, v_w_v, v_attn_norm_w, v_w_q, v_w_o, v_ffn_norm_w, v_ffn_up_w, v_ffn_conv_w, v_ffn_conv_b, v_ffn_down_w, v_final_norm_w):
    args = (ssm_norm_w, ssm_in_w, ssm_conv_w, ssm_conv_b, ssm_dt_bias, ssm_a_log, ssm_d, ssm_gate_norm_w, ssm_out_w, kv_norm_w, w_k, w_v, attn_norm_w, w_q, w_o, ffn_norm_w, ffn_up_w, ffn_conv_w, ffn_conv_b, ffn_down_w, final_norm_w)
    moms = (m_ssm_norm_w, m_ssm_in_w, m_ssm_conv_w, m_ssm_conv_b, m_ssm_dt_bias, m_ssm_a_log, m_ssm_d, m_ssm_gate_norm_w, m_ssm_out_w, m_kv_norm_w, m_w_k, m_w_v, m_attn_norm_w, m_w_q, m_w_o, m_ffn_norm_w, m_ffn_up_w, m_ffn_conv_w, m_ffn_conv_b, m_ffn_down_w, m_final_norm_w)
    vels = (v_ssm_norm_w, v_ssm_in_w, v_ssm_conv_w, v_ssm_conv_b, v_ssm_dt_bias, v_ssm_a_log, v_ssm_d, v_ssm_gate_norm_w, v_ssm_out_w, v_kv_norm_w, v_w_k, v_w_v, v_attn_norm_w, v_w_q, v_w_o, v_ffn_norm_w, v_ffn_up_w, v_ffn_conv_w, v_ffn_conv_b, v_ffn_down_w, v_final_norm_w)
    local = dict(zip(WEIGHTS, args))
    m_in = dict(zip(WEIGHTS, moms))
    v_in = dict(zip(WEIGHTS, vels))
    chip = 2 * lax.axis_index("x") + lax.axis_index("y")

    full = {n: local[n] for n in REPLICATED}
    small32 = _gather_chips(_pack([local[n].reshape(-1) for n in SMALL], F32, 8), name="gather_small")
    for n, st in zip(SMALL, _unpack(small32, [local[n].shape for n in SMALL])):
        full[n] = _from_shards(st, SHARD_AXIS[n])

    pieces = _Pieces(local)
    loss_p, dx, grads, big_grads = _step(x[0], loss_target[0], full, pieces)
    gshard = {}
    for n in BIG:
        if n in STACKED:
            gshard[n] = [big_grads[n, l] for l in range(local[n].shape[0])]
        else:
            gshard[n] = big_grads[n, None].reshape(local[n].shape)

    small = SMALL + REPLICATED
    rep = _pack([loss_p.reshape(-1)] + [grads[n].reshape(-1) for n in small], F32, 8)
    tot = _sum_leading(_gather_all(rep, name="ar_gather"), name="ar_sum")
    parts = _unpack(tot, [(LANES,)] + [grads[n].shape for n in small])
    loss = jnp.sum(parts[0])
    for n, g in zip(small, parts[1:]):
        if n in SHARD_AXIS:
            size = local[n].shape[SHARD_AXIS[n]]
            g = lax.dynamic_slice_in_dim(g, chip * size, size, axis=SHARD_AXIS[n])
        gshard[n] = g

    grads_out, deltas, new_m, new_v = [], [], [], []
    for n in WEIGHTS:
        if n in STACKED:
            g, d, nm, nv = _adamw_layers(local[n], gshard[n], m_in[n], v_in[n], name=f"adamw_{n}")
        else:
            g = gshard[n]
            d, nm, nv = _adamw(local[n], g, m_in[n], v_in[n], name=f"adamw_{n}")
        grads_out.append(g)
        deltas.append(d)
        new_m.append(nm)
        new_v.append(nv)
    return (loss, dx[None], *grads_out, *deltas, *new_m, *new_v)
```

```python
import functools
import math

import jax
import jax.numpy as jnp
from jax import lax
from jax.experimental import pallas as pl
from jax.experimental.pallas import tpu as pltpu

D_MODEL = 1024
D_INNER = 2048
SSM_HEAD_DIM = 64
SSM_HEADS = 32
SSM_GROUPS = 4
SSM_STATE = 128
SSM_CONV = 4
SSM_CHUNK = 128
GN = SSM_GROUPS * SSM_STATE
CONV_DIM = D_INNER + 2 * GN
SB_HEADS = 16
SB_HEAD_DIM = 64
D_FF = 2816
FFN_CONV = 3
EPS = 1e-6
ADAM_LR = 0.001
ADAM_B1 = 0.9
ADAM_B2 = 0.999
ADAM_EPS = 1e-08
ADAM_WD = 0.01
ADAM_STEP = 10

LANES = 128
SUBLANES = 8
VMEM_LIMIT = 48 * 1024 * 1024
ADAM_BLOCK_BYTES = 1 << 20
F32 = jnp.float32
BF16 = jnp.bfloat16
MESH = pl.DeviceIdType.MESH


def _cparams(sem=None):
    return pltpu.CompilerParams(dimension_semantics=sem, vmem_limit_bytes=VMEM_LIMIT)


class _SideJob:
    def __init__(self, ins, out_shapes, n_sems, start, finish):
        self.ins, self.out_shapes, self.n_sems, self.start, self.finish = ins, out_shapes, n_sems, start, finish


def _call(body, *, grid, in_specs, out_specs, out_shape, scratch_shapes=(), sem, name, args, side=None):
    in_specs, out_specs, out_shape, scratch_shapes = list(in_specs), list(out_specs), list(out_shape), list(scratch_shapes)
    n_in, n_out = len(in_specs), len(out_specs)
    if side is None:
        outs = pl.pallas_call(body, grid=grid, in_specs=in_specs, out_specs=out_specs, out_shape=out_shape,
                              scratch_shapes=scratch_shapes, compiler_params=_cparams(sem), name=name)(*args)
        return list(outs), []
    k_in, k_out = len(side.ins), len(side.out_shapes)

    def wrapped(*refs):
        ins, s_ins = refs[:n_in], refs[n_in:n_in + k_in]
        o0 = n_in + k_in
        outs, s_outs = refs[o0:o0 + n_out], refs[o0 + n_out:o0 + n_out + k_out]
        scratch, send_sems, recv_sems = refs[o0 + n_out + k_out:-2], refs[-2], refs[-1]
        ids = [pl.program_id(a) for a in range(len(grid))]
        first = functools.reduce(jnp.logical_and, [p == 0 for p in ids])
        last = functools.reduce(jnp.logical_and, [p == g - 1 for p, g in zip(ids, grid)])

        @pl.when(first)
        def _():
            side.start(s_ins, s_outs, send_sems, recv_sems)

        body(*ins, *outs, *scratch)

        @pl.when(last)
        def _():
            side.finish(s_ins, s_outs, send_sems, recv_sems)

    outs = pl.pallas_call(
        wrapped, grid=grid, in_specs=in_specs + [ANY] * k_in, out_specs=out_specs + [ANY] * k_out,
        out_shape=out_shape + list(side.out_shapes),
        scratch_shapes=scratch_shapes + [pltpu.SemaphoreType.DMA((side.n_sems,)), pltpu.SemaphoreType.DMA((side.n_sems,))],
        compiler_params=_cparams(tuple("arbitrary" for _ in grid)), name=name)(*args, *side.ins)
    return list(outs[:n_out]), list(outs[n_out:])


def _tile(n, cands):
    for c in cands:
        if n % c == 0:
            return c
    return n


def _nt(a, b):
    return lax.dot_general(a, b, (((1,), (1,)), ((), ())), preferred_element_type=F32)


def _tn(a, b):
    return lax.dot_general(a, b, (((0,), (0,)), ((), ())), preferred_element_type=F32)


def _nn(a, b):
    return jnp.dot(a, b, preferred_element_type=F32)


def _split(x, pieces):
    out = []
    for _ in range(pieces - 1):
        h = x.astype(BF16)
        out.append(h)
        x = x - h.astype(F32)
    out.append(x.astype(BF16))
    return out


def _ones_dot(ones, x, *, ones_left, pieces=3):
    o16 = ones.astype(BF16)
    acc = None
    for piece in _split(x, pieces):
        term = _nn(o16, piece) if ones_left else _nn(piece, o16)
        acc = term if acc is None else acc + term
    return acc


def _row_sums(x, pieces=3):
    return _ones_dot(jnp.ones((x.shape[1], LANES), F32), x, ones_left=False, pieces=pieces)


def _softplus(x):
    return jnp.maximum(x, 0.0) + jnp.log(1.0 + jnp.exp(-jnp.abs(x)))


def _sigmoid(x):
    return 0.5 * jnp.tanh(0.5 * x) + 0.5


MM_TILE_MAX = 1408
MM_VMEM_BUDGET = 40 * 1024 * 1024


def _divisors(n, cap):
    out = [d for d in range(min(cap, n) // LANES * LANES, 0, -LANES) if n % d == 0]
    return out or [n]


def _mm_tiles(m, n, k, a_bytes, b_bytes, o_bytes, add_bytes):
    best = None
    for tm in _divisors(m, MM_TILE_MAX):
        for tn in _divisors(n, MM_TILE_MAX):
            for tk in _divisors(k, MM_TILE_MAX):
                vmem = 2 * (tm * tk * a_bytes + tk * tn * b_bytes + tm * tn * (o_bytes + add_bytes)) + tm * tn * 4
                if vmem > MM_VMEM_BUDGET:
                    continue
                score = (tm * tn * tk, tm * tn)
                if best is None or score > best[0]:
                    best = (score, (tm, tn, tk))
    return best[1]


def _matmul(a, b, *, ta=False, tb=False, add=None, out_dtype=F32, out_parts=1, name):
    a_parts = a.shape[0] if a.ndim == 3 else 1
    b_parts = b.shape[0] if b.ndim == 3 else 1
    assert not (ta and a_parts > 1)
    a2, b2 = a.shape[-2:], b.shape[-2:]
    m, k = (a2[1], a2[0]) if ta else (a2[0], a2[1] * a_parts)
    n, kb = (b2[0], b2[1] * b_parts) if tb else (b2[1] * b_parts, b2[0])
    assert kb == k, (a.shape, b.shape)
    n_unit = math.gcd(n // out_parts, n if tb else b2[1])
    k_unit = math.gcd(k // a_parts, b2[1] if tb else k)
    tm, tn, tk = _mm_tiles(m, n_unit, k_unit, a.dtype.itemsize, b.dtype.itemsize, jnp.dtype(out_dtype).itemsize,
                           0 if add is None else add.dtype.itemsize)
    nk = k // tk
    ka, kbp = (k // a_parts) // tk, (k // b_parts) // tk
    nb, no = (n // b_parts) // tn, (n // out_parts) // tn

    def body(*refs):
        if add is None:
            a_ref, b_ref, o_ref = refs[:3]
            add_ref = None
        else:
            a_ref, b_ref, add_ref, o_ref = refs[:4]
        kk = pl.program_id(2)
        dn = (((0 if ta else 1,), (1 if tb else 0,)), ((), ()))
        prod = lax.dot_general(a_ref[...].astype(BF16), b_ref[...].astype(BF16), dn, preferred_element_type=F32)

        def finish(r):
            if add_ref is not None:
                r = r + add_ref[...].astype(F32)
            o_ref[...] = r.astype(o_ref.dtype)

        if nk == 1:
            finish(prod)
            return
        acc_ref = refs[-1]

        @pl.when(kk == 0)
        def _():
            acc_ref[...] = prod

        @pl.when(jnp.logical_and(kk > 0, kk < nk - 1))
        def _():
            acc_ref[...] += prod

        @pl.when(kk == nk - 1)
        def _():
            finish(acc_ref[...] + prod)

    if ta:
        a_spec = pl.BlockSpec((tk, tm), lambda i, j, kk: (kk, i))
    elif a_parts > 1:
        a_spec = pl.BlockSpec((None, tm, tk), lambda i, j, kk: (kk // ka, i, kk % ka))
    else:
        a_spec = pl.BlockSpec((tm, tk), lambda i, j, kk: (i, kk))
    if b_parts == 1:
        b_spec = pl.BlockSpec((tn, tk), lambda i, j, kk: (j, kk)) if tb else pl.BlockSpec((tk, tn), lambda i, j, kk: (kk, j))
    elif tb:
        b_spec = pl.BlockSpec((None, tn, tk), lambda i, j, kk: (kk // kbp, j, kk % kbp))
    else:
        b_spec = pl.BlockSpec((None, tk, tn), lambda i, j, kk: (j // nb, kk, j % nb))
    if out_parts > 1:
        o_spec = pl.BlockSpec((None, tm, tn), lambda i, j, kk: (j // no, i, j % no))
        o_shape = jax.ShapeDtypeStruct((out_parts, m, n // out_parts), out_dtype)
    else:
        o_spec = pl.BlockSpec((tm, tn), lambda i, j, kk: (i, j))
        o_shape = jax.ShapeDtypeStruct((m, n), out_dtype)
    in_specs = [a_spec, b_spec]
    args = [a, b]
    if add is not None:
        in_specs.append(pl.BlockSpec((tm, tn), lambda i, j, kk: (i, j)))
        args.append(add)
    return pl.pallas_call(
        body,
        grid=(m // tm, n // tn, nk),
        in_specs=in_specs,
        out_specs=o_spec,
        out_shape=o_shape,
        scratch_shapes=[pltpu.VMEM((tm, tn), F32)] if nk > 1 else [],
        compiler_params=_cparams(("parallel", "parallel", "arbitrary")),
        name=name,
    )(*args)


def _rmsnorm_fwd(x, w, *, name):
    t, d = x.shape
    tb = _tile(t, (512, 256, 128))

    def body(x_ref, w_ref, o_ref):
        xv = x_ref[...]
        r = lax.rsqrt(jnp.mean(xv * xv, axis=-1, keepdims=True) + EPS)
        o_ref[...] = (xv * r * w_ref[...]).astype(o_ref.dtype)

    return pl.pallas_call(
        body,
        grid=(t // tb,),
        in_specs=[pl.BlockSpec((tb, d), lambda i: (i, 0)), pl.BlockSpec((1, d), lambda i: (0, 0))],
        out_specs=pl.BlockSpec((tb, d), lambda i: (i, 0)),
        out_shape=jax.ShapeDtypeStruct((t, d), BF16),
        compiler_params=_cparams(("parallel",)),
        name=name,
    )(x, w.reshape(1, d))


def _rmsnorm_bwd(x, dys, dres, *, name):
    t, d = x.shape
    tb = _tile(t, (256, 128))
    nn = len(dys)
    has_res = dres is not None

    def body(*refs):
        x_ref = refs[0]
        dy_refs = refs[1:1 + nn]
        w_refs = refs[1 + nn:1 + 2 * nn]
        pos = 1 + 2 * nn
        res_ref = refs[pos] if has_res else None
        pos += 1 if has_res else 0
        dx_ref = refs[pos]
        dw_refs = refs[pos + 1:pos + 1 + nn]
        i = pl.program_id(0)
        xv = x_ref[...]
        r = lax.rsqrt(jnp.mean(xv * xv, axis=-1, keepdims=True) + EPS)
        xn = xv * r
        dx = res_ref[...] if has_res else jnp.zeros_like(xv)
        for q in range(nn):
            dy = dy_refs[q][...].astype(F32)
            g = dy * w_refs[q][...]
            dx = dx + r * (g - xn * jnp.mean(g * xn, axis=-1, keepdims=True))
            dwp = jnp.sum(dy * xn, axis=0, keepdims=True)

            @pl.when(i == 0)
            def _(q=q, dwp=dwp):
                dw_refs[q][...] = dwp

            @pl.when(i > 0)
            def _(q=q, dwp=dwp):
                dw_refs[q][...] += dwp
        dx_ref[...] = dx

    row = pl.BlockSpec((tb, d), lambda i: (i, 0))
    vec = pl.BlockSpec((1, d), lambda i: (0, 0))
    in_specs = [row] + [row] * nn + [vec] * nn + ([row] if has_res else [])
    args = [x] + [p[0] for p in dys] + [p[1].reshape(1, d) for p in dys] + ([dres] if has_res else [])
    outs = pl.pallas_call(
        body,
        grid=(t // tb,),
        in_specs=in_specs,
        out_specs=[row] + [vec] * nn,
        out_shape=[jax.ShapeDtypeStruct((t, d), F32)] + [jax.ShapeDtypeStruct((1, d), F32)] * nn,
        compiler_params=_cparams(("arbitrary",)),
        name=name,
    )(*args)
    return outs[0], list(outs[1:])


def _loss_head(x, w, target, *, name):
    t, d = x.shape
    tb = _tile(t, (256, 128))

    def body(x_ref, w_ref, t_ref, loss_ref, dx_ref, dw_ref):
        i = pl.program_id(0)
        xv = x_ref[...]
        wv = w_ref[...]
        r = lax.rsqrt(jnp.mean(xv * xv, axis=-1, keepdims=True) + EPS)
        xn = xv * r
        e = xn * wv - t_ref[...]
        lp = 0.5 * jnp.sum(jnp.mean(e * e, axis=-1, keepdims=True), axis=0, keepdims=True)
        dy = e * (1.0 / d)
        g = dy * wv
        dx_ref[...] = r * (g - xn * jnp.mean(g * xn, axis=-1, keepdims=True))
        dwp = jnp.sum(dy * xn, axis=0, keepdims=True)
        lpv = jnp.broadcast_to(lp, (1, LANES)) * (1.0 / LANES)

        @pl.when(i == 0)
        def _():
            dw_ref[...] = dwp
            loss_ref[...] = lpv

        @pl.when(i > 0)
        def _():
            dw_ref[...] += dwp
            loss_ref[...] += lpv

    row = pl.BlockSpec((tb, d), lambda i: (i, 0))
    vec = pl.BlockSpec((1, d), lambda i: (0, 0))
    return pl.pallas_call(
        body,
        grid=(t // tb,),
        in_specs=[row, vec, row],
        out_specs=[pl.BlockSpec((1, LANES), lambda i: (0, 0)), row, vec],
        out_shape=[jax.ShapeDtypeStruct((1, LANES), F32), jax.ShapeDtypeStruct((t, d), F32),
                   jax.ShapeDtypeStruct((1, d), F32)],
        compiler_params=_cparams(("arbitrary",)),
        name=name,
    )(x, w.reshape(1, d), target)


ROW_CHUNK = 64
PAD = SUBLANES


def _shifted(pad_ref, r0, rows, back):
    return pad_ref[pl.ds(PAD + r0 - back, rows), :]


def _conv_taps(pad_ref, w_ref, r0, rows, kw):
    acc = None
    for j in range(kw):
        term = _shifted(pad_ref, r0, rows, kw - 1 - j) * w_ref[j:j + 1, :]
        acc = term if acc is None else acc + term
    return acc


def _fill_pad(pad_ref, x_ref, t):
    pad_ref[0:PAD, :] = jnp.zeros((PAD, pad_ref.shape[1]), F32)
    pad_ref[pl.ds(PAD + t, PAD), :] = jnp.zeros((PAD, pad_ref.shape[1]), F32)
    pad_ref[pl.ds(PAD, t), :] = x_ref[...].astype(F32)


def _conv_silu_fwd(x, w, b, *, x_off=0, name):
    t = x.shape[0]
    kw, c = w.shape
    cw = _tile(math.gcd(c, x_off) if x_off else c, (256, 128))
    ob = x_off // cw
    rc = _tile(t, (ROW_CHUNK,))

    def body(x_ref, w_ref, b_ref, o_ref, pad_ref):
        _fill_pad(pad_ref, x_ref, t)
        for r0 in range(0, t, rc):
            pre = _conv_taps(pad_ref, w_ref, r0, rc, kw) + b_ref[...]
            o_ref[pl.ds(r0, rc), :] = pre * _sigmoid(pre)

    strip = pl.BlockSpec((t, cw), lambda i: (0, i))
    return pl.pallas_call(
        body,
        grid=(c // cw,),
        in_specs=[pl.BlockSpec((t, cw), lambda i: (0, i + ob)), pl.BlockSpec((kw, cw), lambda i: (0, i)),
                  pl.BlockSpec((1, cw), lambda i: (0, i))],
        out_specs=strip,
        out_shape=jax.ShapeDtypeStruct((t, c), F32),
        scratch_shapes=[pltpu.VMEM((t + 2 * PAD, cw), F32)],
        compiler_params=_cparams(("parallel",)),
        name=name,
    )(x, w, b.reshape(1, c))


def _conv_bwd_core(dpre_pad_ref, x_pad_ref, w_ref, dx_ref, dw_ref, db_ref, t, rc, kw):
    cw = dx_ref.shape[1]

    def fold(a):
        return jnp.sum(a.reshape(rc // SUBLANES, SUBLANES, cw), axis=0) if rc % SUBLANES == 0 else jnp.sum(a, axis=0, keepdims=True)

    dws = [None] * kw
    dbs = None
    for r0 in range(0, t, rc):
        dpre = dpre_pad_ref[pl.ds(PAD + r0, rc), :]
        dx = None
        for j in range(kw):
            s = kw - 1 - j
            term = dpre_pad_ref[pl.ds(PAD + r0 + s, rc), :] * w_ref[j:j + 1, :]
            dx = term if dx is None else dx + term
            part = fold(dpre * _shifted(x_pad_ref, r0, rc, s))
            dws[j] = part if dws[j] is None else dws[j] + part
        part = fold(dpre)
        dbs = part if dbs is None else dbs + part
        dx_ref[pl.ds(r0, rc), :] = dx
    for j in range(kw):
        dw_ref[j:j + 1, :] = jnp.sum(dws[j], axis=0, keepdims=True)
    db_ref[...] = jnp.sum(dbs, axis=0, keepdims=True)


def _conv_silu_bwd(x, w, b, dact, *, x_off=0, into=None, name):
    t = x.shape[0]
    kw, c = w.shape
    cw = _tile(math.gcd(c, x_off) if x_off else c, (256, 128))
    ob = x_off // cw
    rc = _tile(t, (ROW_CHUNK,))

    def body(x_ref, w_ref, b_ref, da_ref, *rest):
        dx_ref, dw_ref, db_ref, xpad_ref, dpad_ref = rest[-5:]
        _fill_pad(xpad_ref, x_ref, t)
        dpad_ref[0:PAD, :] = jnp.zeros((PAD, cw), F32)
        dpad_ref[pl.ds(PAD + t, PAD), :] = jnp.zeros((PAD, cw), F32)
        for r0 in range(0, t, rc):
            pre = _conv_taps(xpad_ref, w_ref, r0, rc, kw) + b_ref[...]
            sg = _sigmoid(pre)
            dpad_ref[pl.ds(PAD + r0, rc), :] = da_ref[pl.ds(r0, rc), :] * (sg * (1.0 + pre * (1.0 - sg)))
        _conv_bwd_core(dpad_ref, xpad_ref, w_ref, dx_ref, dw_ref, db_ref, t, rc, kw)

    strip = pl.BlockSpec((t, cw), lambda i: (0, i))
    wspec = pl.BlockSpec((kw, cw), lambda i: (0, i))
    bspec = pl.BlockSpec((1, cw), lambda i: (0, i))
    xspec = pl.BlockSpec((t, cw), lambda i: (0, i + ob))
    extra = {} if into is None else dict(input_output_aliases={4: 0})
    return pl.pallas_call(
        body,
        grid=(c // cw,),
        in_specs=[xspec, wspec, bspec, strip] + ([] if into is None else [ANY]),
        out_specs=[strip if into is None else xspec, wspec, bspec],
        out_shape=[jax.ShapeDtypeStruct((t, c) if into is None else into.shape, F32), jax.ShapeDtypeStruct((kw, c), F32),
                   jax.ShapeDtypeStruct((1, c), F32)],
        scratch_shapes=[pltpu.VMEM((t + 2 * PAD, cw), F32), pltpu.VMEM((t + 2 * PAD, cw), F32)],
        compiler_params=_cparams(("parallel",)),
        name=name,
        **extra,
    )(x, w, b.reshape(1, c), dact, *([] if into is None else [into]))


def _conv_glu_fwd(hid, w, b, *, side=None, name):
    t, c2 = hid.shape
    f = c2 // 2
    kw = w.shape[0]
    cw = _tile(f, (256, 128))
    nf = f // cw
    rc = _tile(t, (ROW_CHUNK,))

    def body(g_ref, v_ref, wg_ref, wv_ref, bg_ref, bv_ref, o_ref, gpad_ref, vpad_ref):
        _fill_pad(gpad_ref, g_ref, t)
        _fill_pad(vpad_ref, v_ref, t)
        for r0 in range(0, t, rc):
            gate = _conv_taps(gpad_ref, wg_ref, r0, rc, kw) + bg_ref[...]
            val = _conv_taps(vpad_ref, wv_ref, r0, rc, kw) + bv_ref[...]
            o_ref[pl.ds(r0, rc), :] = (gate * _sigmoid(gate) * val).astype(o_ref.dtype)

    gs = pl.BlockSpec((t, cw), lambda i: (0, i))
    vs = pl.BlockSpec((t, cw), lambda i: (0, i + nf))
    b2 = b.reshape(1, c2)
    (act,), side_outs = _call(
        body,
        grid=(nf,),
        in_specs=[gs, vs, pl.BlockSpec((kw, cw), lambda i: (0, i)), pl.BlockSpec((kw, cw), lambda i: (0, i + nf)),
                  pl.BlockSpec((1, cw), lambda i: (0, i)), pl.BlockSpec((1, cw), lambda i: (0, i + nf))],
        out_specs=[gs],
        out_shape=[jax.ShapeDtypeStruct((t, f), BF16)],
        scratch_shapes=[pltpu.VMEM((t + 2 * PAD, cw), F32), pltpu.VMEM((t + 2 * PAD, cw), F32)],
        sem=("parallel",),
        name=name,
        args=(hid, hid, w, w, b2, b2),
        side=side,
    )
    return act, side_outs


def _conv_glu_bwd(hid, w, b, dact, *, name):
    t, c2 = hid.shape
    f = c2 // 2
    kw = w.shape[0]
    cw = _tile(f, (128,))
    nf = f // cw
    rc = _tile(t, (ROW_CHUNK,))

    def body(g_ref, v_ref, wg_ref, wv_ref, bg_ref, bv_ref, da_ref,
             dgv_ref, dwg_ref, dwv_ref, dbg_ref, dbv_ref,
             gpad_ref, vpad_ref, dgpad_ref, dvpad_ref):
        _fill_pad(gpad_ref, g_ref, t)
        _fill_pad(vpad_ref, v_ref, t)
        for ref in (dgpad_ref, dvpad_ref):
            ref[0:PAD, :] = jnp.zeros((PAD, cw), F32)
            ref[pl.ds(PAD + t, PAD), :] = jnp.zeros((PAD, cw), F32)
        for r0 in range(0, t, rc):
            gate = _conv_taps(gpad_ref, wg_ref, r0, rc, kw) + bg_ref[...]
            val = _conv_taps(vpad_ref, wv_ref, r0, rc, kw) + bv_ref[...]
            sg = _sigmoid(gate)
            da = da_ref[pl.ds(r0, rc), :].astype(F32)
            dgpad_ref[pl.ds(PAD + r0, rc), :] = da * val * (sg * (1.0 + gate * (1.0 - sg)))
            dvpad_ref[pl.ds(PAD + r0, rc), :] = da * (gate * sg)
        _conv_bwd_core(dgpad_ref, gpad_ref, wg_ref, dgv_ref.at[0], dwg_ref, dbg_ref, t, rc, kw)
        _conv_bwd_core(dvpad_ref, vpad_ref, wv_ref, dgv_ref.at[1], dwv_ref, dbv_ref, t, rc, kw)

    gs = pl.BlockSpec((t, cw), lambda i: (0, i))
    vs = pl.BlockSpec((t, cw), lambda i: (0, i + nf))
    wg = pl.BlockSpec((kw, cw), lambda i: (0, i))
    wv = pl.BlockSpec((kw, cw), lambda i: (0, i + nf))
    bg = pl.BlockSpec((1, cw), lambda i: (0, i))
    bv = pl.BlockSpec((1, cw), lambda i: (0, i + nf))
    b2 = b.reshape(1, c2)
    pad = pltpu.VMEM((t + 2 * PAD, cw), F32)
    return pl.pallas_call(
        body,
        grid=(nf,),
        in_specs=[gs, vs, wg, wv, bg, bv, gs],
        out_specs=[pl.BlockSpec((2, t, cw), lambda i: (0, 0, i)), wg, wg, bg, bg],
        out_shape=[jax.ShapeDtypeStruct((2, t, f), F32),
                   jax.ShapeDtypeStruct((kw, f), F32), jax.ShapeDtypeStruct((kw, f), F32),
                   jax.ShapeDtypeStruct((1, f), F32), jax.ShapeDtypeStruct((1, f), F32)],
        scratch_shapes=[pad, pad, pad, pad],
        compiler_params=_cparams(("parallel",)),
        name=name,
    )(hid, hid, w, w, b2, b2, dact)


def _gate_norm_fwd(y, zx, w, *, name):
    t, di = y.shape
    gsz = di // SSM_GROUPS
    tb = _tile(t, (256, 128))

    def body(y_ref, z_ref, w_ref, o_ref):
        for g in range(SSM_GROUPS):
            sl = slice(g * gsz, (g + 1) * gsz)
            zv = z_ref[:, sl]
            gv = y_ref[:, sl] * (zv * _sigmoid(zv))
            r = lax.rsqrt(jnp.mean(gv * gv, axis=-1, keepdims=True) + EPS)
            o_ref[:, sl] = (gv * r * w_ref[:, sl]).astype(o_ref.dtype)

    row = pl.BlockSpec((tb, di), lambda i: (i, 0))
    return pl.pallas_call(
        body,
        grid=(t // tb,),
        in_specs=[row, row, pl.BlockSpec((1, di), lambda i: (0, 0))],
        out_specs=row,
        out_shape=jax.ShapeDtypeStruct((t, di), BF16),
        compiler_params=_cparams(("parallel",)),
        name=name,
    )(y, zx, w.reshape(1, di))


def _gate_norm_bwd(y, zx, w, dyn, *, name):
    t, di = y.shape
    gsz = di // SSM_GROUPS
    tb = _tile(t, (256, 128))

    def body(y_ref, z_ref, w_ref, d_ref, dy_ref, dz_ref, dw_ref):
        i = pl.program_id(0)
        for g in range(SSM_GROUPS):
            sl = slice(g * gsz, (g + 1) * gsz)
            zv = z_ref[:, sl]
            yv = y_ref[:, sl]
            sg = _sigmoid(zv)
            sz = zv * sg
            gv = yv * sz
            r = lax.rsqrt(jnp.mean(gv * gv, axis=-1, keepdims=True) + EPS)
            gn = gv * r
            dn = d_ref[:, sl].astype(F32)
            q = dn * w_ref[:, sl]
            dg = r * (q - gn * jnp.mean(q * gn, axis=-1, keepdims=True))
            dy_ref[:, sl] = dg * sz
            dz_ref[:, sl] = dg * yv * (sg * (1.0 + zv * (1.0 - sg)))
            dwp = jnp.sum(dn * gn, axis=0, keepdims=True)

            @pl.when(i == 0)
            def _(sl=sl, dwp=dwp):
                dw_ref[:, sl] = dwp

            @pl.when(i > 0)
            def _(sl=sl, dwp=dwp):
                dw_ref[:, sl] += dwp

    row = pl.BlockSpec((tb, di), lambda i: (i, 0))
    vec = pl.BlockSpec((1, di), lambda i: (0, 0))
    return pl.pallas_call(
        body,
        grid=(t // tb,),
        in_specs=[row, row, vec, row],
        out_specs=[row, row, vec],
        out_shape=[jax.ShapeDtypeStruct((t, di), F32), jax.ShapeDtypeStruct((t, zx.shape[1]), F32),
                   jax.ShapeDtypeStruct((1, di), F32)],
        compiler_params=_cparams(("arbitrary",)),
        name=name,
    )(y, zx, w.reshape(1, di), dyn)


def _adamw(w, g, m, v, *, name):
    shape = w.shape
    cols = shape[-1]
    rows = w.size // cols
    w2, g2, m2, v2 = (a.reshape(rows, cols) for a in (w, g, m, v))
    tr = rows
    if rows * cols * 4 > ADAM_BLOCK_BYTES:
        tr = _tile(rows, tuple(r for r in (512, 256, 128, 64, 32, 16, 8) if r * cols * 4 <= ADAM_BLOCK_BYTES))
    c1 = 1.0 - ADAM_B1 ** ADAM_STEP
    c2 = 1.0 - ADAM_B2 ** ADAM_STEP

    def body(w_ref, g_ref, m_ref, v_ref, d_ref, nm_ref, nv_ref):
        gv = g_ref[...]
        nm = ADAM_B1 * m_ref[...] + (1.0 - ADAM_B1) * gv
        nv = ADAM_B2 * v_ref[...] + (1.0 - ADAM_B2) * (gv * gv)
        d_ref[...] = -ADAM_LR * ((nm / c1) / (jnp.sqrt(nv / c2) + ADAM_EPS) + ADAM_WD * w_ref[...])
        nm_ref[...] = nm
        nv_ref[...] = nv

    blk = pl.BlockSpec((tr, cols), lambda i: (i, 0))
    outs = pl.pallas_call(
        body,
        grid=(rows // tr,),
        in_specs=[blk] * 4,
        out_specs=[blk] * 3,
        out_shape=[jax.ShapeDtypeStruct((rows, cols), F32)] * 3,
        compiler_params=_cparams(("parallel",)),
        name=name,
    )(w2, g2, m2, v2)
    return tuple(o.reshape(shape) for o in outs)


def _adamw_layers(w, gs, m, v, *, name):
    n_l, rows, cols = w.shape
    assert len(gs) == n_l
    tr = _tile(rows, tuple(r for r in (512, 256, 128, 64, 32, 16, 8) if r * cols * 4 <= ADAM_BLOCK_BYTES))
    c1 = 1.0 - ADAM_B1 ** ADAM_STEP
    c2 = 1.0 - ADAM_B2 ** ADAM_STEP

    def body(*refs):
        w_ref, m_ref, v_ref = refs[:3]
        g_refs = refs[3:3 + n_l]
        g_ref, d_ref, nm_ref, nv_ref = refs[3 + n_l:]
        layer = pl.program_id(0)
        gv = g_refs[0][...]
        for q in range(1, n_l):
            gv = jnp.where(layer == q, g_refs[q][...], gv)
        nm = ADAM_B1 * m_ref[...] + (1.0 - ADAM_B1) * gv
        nv = ADAM_B2 * v_ref[...] + (1.0 - ADAM_B2) * (gv * gv)
        g_ref[...] = gv
        d_ref[...] = -ADAM_LR * ((nm / c1) / (jnp.sqrt(nv / c2) + ADAM_EPS) + ADAM_WD * w_ref[...])
        nm_ref[...] = nm
        nv_ref[...] = nv

    stacked = pl.BlockSpec((None, tr, cols), lambda l, i: (l, i, 0))
    single = pl.BlockSpec((tr, cols), lambda l, i: (i, 0))
    return pl.pallas_call(
        body,
        grid=(n_l, rows // tr),
        in_specs=[stacked] * 3 + [single] * n_l,
        out_specs=[stacked] * 4,
        out_shape=[jax.ShapeDtypeStruct(w.shape, F32)] * 4,
        compiler_params=_cparams(("parallel", "parallel")),
        name=name,
    )(w, m, v, *gs)


def _ssd_scalars(dtc_ref, dtr_ref, hpc_ref, hpr_ref, ln):
    assert SSM_CHUNK == SSM_STATE == LANES, "the SSD kernels mix chunk, state and lane-wide tiles freely"
    bias_c, alog_c = hpc_ref[0, 0:1, :], hpc_ref[0, 1:2, :]
    bias_r, alog_r = hpr_ref[0, :, 0:1], hpr_ref[0, :, 1:2]
    a_c, a_r = -jnp.exp(alog_c), -jnp.exp(alog_r)
    raw_c = dtc_ref[0] + bias_c
    dt_c = _softplus(raw_c)
    dt_r = _softplus(dtr_ref[0] + bias_r)
    row = lax.broadcasted_iota(jnp.int32, (ln, ln), 0)
    col = lax.broadcasted_iota(jnp.int32, (ln, ln), 1)
    lower = (col <= row).astype(F32)
    upper = (row <= col).astype(F32)
    acs_c = _ones_dot(lower, dt_c * a_c, ones_left=True)
    acs_r = _ones_dot(upper, dt_r * a_r, ones_left=False)
    return raw_c, dt_c, a_c, acs_c, acs_r, row, col


def _ssd_specs(t, di, g_n, n_st, rp, ln, r_h, rev):
    nc = t // ln
    cidx = (lambda c: nc - 1 - c) if rev else (lambda c: c)
    xs = pl.BlockSpec((ln, rp), lambda g, c: (cidx(c), g))
    bm = pl.BlockSpec((ln, n_st), lambda g, c: (cidx(c), di // n_st + g))
    cm = pl.BlockSpec((ln, n_st), lambda g, c: (cidx(c), di // n_st + g_n + g))
    dtc = pl.BlockSpec((1, ln, r_h), lambda g, c: (g, cidx(c), 0))
    dtr = pl.BlockSpec((1, r_h, ln), lambda g, c: (g, 0, cidx(c)))
    hpc = pl.BlockSpec((1, 3, r_h), lambda g, c: (g, 0, 0))
    hpr = pl.BlockSpec((1, r_h, 3), lambda g, c: (g, 0, 0))
    prev = pl.BlockSpec((1, rp, n_st), lambda g, c: (cidx(c), g, 0))
    return xs, bm, cm, dtc, dtr, hpc, hpr, prev


def _ssd_fwd(xbc, dtc, dtr, hpc, hpr, *, side=None, name):
    t = xbc.shape[0]
    di, g_n, n_st, p_h, ln = D_INNER, SSM_GROUPS, SSM_STATE, SSM_HEAD_DIM, SSM_CHUNK
    r_h = SSM_HEADS // g_n
    rp = r_h * p_h
    nc = t // ln

    def body(xs_ref, b_ref, c_ref, dtc_ref, dtr_ref, hpc_ref, hpr_ref, y_ref, prev_ref, st_ref):
        @pl.when(pl.program_id(1) == 0)
        def _():
            st_ref[...] = jnp.zeros_like(st_ref)

        _, dt_c, _, acs_c, acs_r, row, col = _ssd_scalars(dtc_ref, dtr_ref, hpc_ref, hpr_ref, ln)
        bm = b_ref[...]
        cm = c_ref[...]
        cm16 = cm.astype(BF16)
        cb = _nt(cm16, bm.astype(BF16))
        causal = row >= col
        for r in range(r_h):
            sl = slice(r * p_h, (r + 1) * p_h)
            xs = xs_ref[:, sl]
            acs = jnp.broadcast_to(acs_c[:, r:r + 1], (ln, ln))
            last = acs[ln - 1:ln, :]
            lm = jnp.where(causal, jnp.exp(acs - acs_r[r:r + 1, :]), 0.0)
            xd = (xs * jnp.broadcast_to(dt_c[:, r:r + 1], (ln, p_h))).astype(BF16)
            prev = st_ref[sl, :]
            y = _nn((cb * lm).astype(BF16), xd)
            y = y + _nt(cm16, prev.astype(BF16)) * jnp.exp(acs[:, :p_h])
            y_ref[:, sl] = y + hpc_ref[0, 2:3, r:r + 1] * xs
            prev_ref[0, sl, :] = prev
            bd = (bm * jnp.exp(last - acs[:, :n_st])).astype(BF16)
            st_ref[sl, :] = prev * jnp.exp(last[:, :n_st]) + _tn(xd, bd)

    xs, bm, cm, dtcs, dtrs, hpcs, hprs, prev = _ssd_specs(t, di, g_n, n_st, rp, ln, r_h, False)
    (y, prev_out), side_outs = _call(
        body,
        grid=(g_n, nc),
        in_specs=[xs, bm, cm, dtcs, dtrs, hpcs, hprs],
        out_specs=[xs, prev],
        out_shape=[jax.ShapeDtypeStruct((t, di), F32), jax.ShapeDtypeStruct((nc, g_n * rp, n_st), F32)],
        scratch_shapes=[pltpu.VMEM((rp, n_st), F32)],
        sem=("parallel", "arbitrary"),
        name=name,
        args=(xbc, xbc, xbc, dtc, dtr, hpc, hpr),
        side=side,
    )
    return y, prev_out, side_outs


def _ssd_bwd(xbc, dtc, dtr, hpc, hpr, prev, dy, *, side=None, name):
    t = xbc.shape[0]
    di, g_n, n_st, p_h, ln = D_INNER, SSM_GROUPS, SSM_STATE, SSM_HEAD_DIM, SSM_CHUNK
    r_h = SSM_HEADS // g_n
    rp = r_h * p_h
    nc = t // ln

    def body(xs_ref, b_ref, c_ref, dtc_ref, dtr_ref, hpc_ref, hpr_ref, prev_ref, dy_ref,
             dxs_ref, db_ref, dc_ref, ddt_ref, hg_ref, ds_ref):
        step = pl.program_id(1)

        @pl.when(step == 0)
        def _():
            ds_ref[...] = jnp.zeros_like(ds_ref)

        raw_c, dt_c, a_c, acs_c, acs_r, row, col = _ssd_scalars(dtc_ref, dtr_ref, hpc_ref, hpr_ref, ln)
        bm = b_ref[...]
        cm = c_ref[...]
        bm16, cm16 = bm.astype(BF16), cm.astype(BF16)
        cb = _nt(cm16, bm16)
        cbt = _nt(bm16, cm16)
        lane_r = lax.broadcasted_iota(jnp.int32, (ln, r_h), 1)
        dacs_all = jnp.zeros((ln, r_h), F32)
        ddtx_all = jnp.zeros((ln, r_h), F32)
        dd_all = jnp.zeros((ln, r_h), F32)
        dcb = jnp.zeros((ln, ln), F32)
        dcbt = jnp.zeros((ln, ln), F32)
        dc_acc = jnp.zeros((ln, n_st), F32)
        db_acc = jnp.zeros((ln, n_st), F32)
        for r in range(r_h):
            sl = slice(r * p_h, (r + 1) * p_h)
            xs = xs_ref[:, sl]
            dyv = dy_ref[:, sl]
            dy16 = dyv.astype(BF16)
            acs = jnp.broadcast_to(acs_c[:, r:r + 1], (ln, ln))
            dtv = jnp.broadcast_to(dt_c[:, r:r + 1], (ln, p_h))
            acsr = acs_r[r:r + 1, :]
            last = acs[ln - 1:ln, :]
            xd = xs * dtv
            xd16 = xd.astype(BF16)
            lm = jnp.where(row >= col, jnp.exp(acs - acsr), 0.0)
            lmt = jnp.where(col >= row, jnp.exp(acsr - acs), 0.0)
            m_ls = cb * lm
            m_sl = cbt * lmt
            dm = _nt(dy16, xd16)
            dmt = _nt(xd16, dy16)
            dxd = _nn(m_sl.astype(BF16), dy16)
            dacs = _row_sums(dm * m_ls - dmt * m_sl)
            dcb = dcb + dm * lm
            dcbt = dcbt + dmt * lmt
            prev = prev_ref[0, sl, :]
            prev16 = prev.astype(BF16)
            e = jnp.exp(acs[:, :p_h])
            y_off = _nt(cm16, prev16) * e
            dacs = dacs + _row_sums(dyv * y_off)
            dyo16 = (dyv * e).astype(BF16)
            dc_acc = dc_acc + _nn(dyo16, prev16)
            dprev = _tn(dyo16, cm16)
            ds = ds_ref[sl, :]
            ds16 = ds.astype(BF16)
            decay = jnp.exp(last - acs)[:, :n_st]
            bd16 = (bm * decay).astype(BF16)
            dbd = _nn(xd16, ds16)
            dxd = dxd + _nt(bd16, ds16)
            db_acc = db_acc + dbd * decay
            tdec = _row_sums(dbd * bm, 2) * decay
            cd = jnp.exp(last)
            dlast = (jnp.sum(tdec, axis=0, keepdims=True)
                     + jnp.sum(_row_sums(prev * ds, 2), axis=0, keepdims=True) * cd)
            ds_ref[sl, :] = dprev + cd[:, :n_st] * ds
            dskip = hpc_ref[0, 2:3, r:r + 1]
            dxs_ref[:, sl] = dxd * dtv + dskip * dyv
            dacs = dacs - tdec + jnp.where(row == ln - 1, dlast, 0.0)
            dacs_all = jnp.where(lane_r == r, dacs[:, :r_h], dacs_all)
            ddtx_all = jnp.where(lane_r == r, _row_sums(dxd * xs, 2)[:, :r_h], ddtx_all)
            dd_all = jnp.where(lane_r == r, _row_sums(dyv * xs, 2)[:, :r_h], dd_all)
        dc_ref[...] = dc_acc + _nn(dcb.astype(BF16), bm16)
        db_ref[...] = db_acc + _nn(dcbt.astype(BF16), cm16)
        upper = (row <= col).astype(F32)
        dad = _ones_dot(upper, dacs_all, ones_left=True)
        ddt = dad * a_c + ddtx_all
        ddt_raw = ddt * _sigmoid(raw_c)
        ddt_ref[0] = ddt_raw
        d_bias = jnp.sum(ddt_raw, axis=0, keepdims=True)
        d_alog = jnp.sum(dad * dt_c, axis=0, keepdims=True) * a_c
        d_d = jnp.sum(dd_all, axis=0, keepdims=True)
        hg = jnp.concatenate([d_bias, d_alog, d_d], axis=0)

        @pl.when(step == 0)
        def _():
            hg_ref[0] = hg

        @pl.when(step > 0)
        def _():
            hg_ref[0] += hg

    xs, bms, cms, dtcs, dtrs, hpcs, hprs, prevs = _ssd_specs(t, di, g_n, n_st, rp, ln, r_h, True)
    bout = pl.BlockSpec((ln, n_st), lambda g, c: (nc - 1 - c, g))
    outs, side_outs = _call(
        body,
        grid=(g_n, nc),
        in_specs=[xs, bms, cms, dtcs, dtrs, hpcs, hprs, prevs, xs],
        out_specs=[xs, bout, bout, dtcs, hpcs],
        out_shape=[jax.ShapeDtypeStruct((t, di), F32), jax.ShapeDtypeStruct((t, g_n * n_st), F32),
                   jax.ShapeDtypeStruct((t, g_n * n_st), F32), jax.ShapeDtypeStruct((g_n, t, r_h), F32),
                   jax.ShapeDtypeStruct((g_n, 3, r_h), F32)],
        scratch_shapes=[pltpu.VMEM((rp, n_st), F32)],
        sem=("parallel", "arbitrary"),
        name=name,
        args=(xbc, xbc, xbc, dtc, dtr, hpc, hpr, prev, dy),
        side=side,
    )
    return (*outs, side_outs)


SB_KEYS = 256
SB_QUERIES = (512, 256)
SB_CUTOFF = 110.0
SB_PIECES = 2


def _sb_logits(qs, kv, valid):
    z = _nt(qs, kv)
    nz = -z
    lg = jnp.minimum(nz, 0.0) - jnp.log(1.0 + jnp.exp(jnp.minimum(z, nz)))
    return z + lg, (lg if valid is None else jnp.where(valid, lg, 0.0))


def _sb_iota(tq):
    diff = lax.broadcasted_iota(jnp.int32, (tq, SB_KEYS), 1) - lax.broadcasted_iota(jnp.int32, (tq, SB_KEYS), 0)
    krow = lax.broadcasted_iota(jnp.int32, (SB_KEYS, SB_KEYS), 0)
    kcol = lax.broadcasted_iota(jnp.int32, (SB_KEYS, SB_KEYS), 1)
    return diff, krow, kcol


def _sb_scale(d):
    scale = 1.0 / math.sqrt(d)
    assert math.frexp(scale)[0] == 0.5, "the scale is folded into bf16 queries: it must be a power of two"
    return scale


def _key_rows(j):
    return pl.ds(pl.multiple_of(j * SB_KEYS, SB_KEYS), SB_KEYS)


def _sb_fwd(q, k, v, *, side=None, name):
    h, t, d = q.shape
    tq = _tile(t, SB_QUERIES)
    nq = t // tq
    kpq = tq // SB_KEYS
    scale = _sb_scale(d)

    def body(q_ref, k_ref, v_ref, o_ref, lt_ref, first_ref):
        i = pl.program_id(1)
        qs = (q_ref[0].astype(F32) * scale).astype(BF16)
        diff, krow, kcol = _sb_iota(tq)
        later = (krow > kcol).astype(F32)

        def block(j, carry, valid):
            acc, cl = carry
            rows = _key_rows(j)
            ls, lg = _sb_logits(qs, k_ref[0, rows, :], valid)
            cs = _ones_dot(later, lg, ones_left=False, pieces=SB_PIECES)
            att = jnp.exp(ls + (cs + cl))
            if valid is not None:
                att = jnp.where(valid, att, 0.0)
            acc = acc + _nn(att.astype(BF16), v_ref[0, rows, :])
            return acc, cl + (cs[:, 0:1] + lg[:, 0:1])

        carry = (jnp.zeros((tq, d), F32), jnp.zeros((tq, 1), F32))
        for m in range(kpq - 1, -1, -1):
            carry = block(i * kpq + m, carry, diff < -m * SB_KEYS)
        nb = i * kpq

        def more(st):
            s, _, cl = st
            return jnp.logical_and(s < nb, jnp.max(cl) > -SB_CUTOFF)

        def step(st):
            s, acc, cl = st
            acc, cl = block(nb - 1 - s, (acc, cl), None)
            return s + 1, acc, cl

        walked, acc, cl = lax.while_loop(more, step, (jnp.int32(0),) + carry)
        o_ref[0] = acc.astype(o_ref.dtype)
        lt_ref[0] = cl
        first_ref[pl.program_id(0), i] = nb - walked

    qs = pl.BlockSpec((1, tq, d), lambda hh, i: (hh, i, 0))
    ls = pl.BlockSpec((1, tq, 1), lambda hh, i: (hh, i, 0))
    ks = pl.BlockSpec((1, t, d), lambda hh, i: (hh, 0, 0))
    outs, side_outs = _call(
        body,
        grid=(h, nq),
        in_specs=[qs, ks, ks],
        out_specs=[qs, ls, pl.BlockSpec(memory_space=pltpu.SMEM)],
        out_shape=[jax.ShapeDtypeStruct((h, t, d), BF16), jax.ShapeDtypeStruct((h, t, 1), F32),
                   jax.ShapeDtypeStruct((h, nq), jnp.int32)],
        sem=("arbitrary", "arbitrary"),
        name=name,
        args=(q, k, v),
        side=side,
    )
    return (*outs, side_outs)


def _sb_bwd(q, k, v, lt, first, do, *, name):
    h, t, d = q.shape
    tq = _tile(t, SB_QUERIES)
    nq = t // tq
    kpq = tq // SB_KEYS
    scale = _sb_scale(d)
    last = SB_KEYS - 1

    def body(q_ref, k_ref, v_ref, lt_ref, first_ref, do_ref, dq_ref, dk_ref, dv_ref, dk_acc, dv_acc):
        i = pl.program_id(1)

        @pl.when(i == 0)
        def _():
            dk_acc[...] = jnp.zeros_like(dk_acc)
            dv_acc[...] = jnp.zeros_like(dv_acc)

        qs = (q_ref[0].astype(F32) * scale).astype(BF16)
        do16 = do_ref[0].astype(BF16)
        ltot = lt_ref[0]
        diff, krow, kcol = _sb_iota(tq)
        upto = (krow <= kcol).astype(F32)
        before = (krow < kcol).astype(F32)

        def block(j, carry, valid):
            dq, pl_sum, pg_sum = carry
            rows = _key_rows(j)
            kv = k_ref[0, rows, :]
            vv = v_ref[0, rows, :]
            ls, lg = _sb_logits(qs, kv, valid)
            pre = _ones_dot(upto, lg, ones_left=False, pieces=SB_PIECES)
            att = jnp.exp(ls + (ltot - (pre + pl_sum)))
            if valid is not None:
                att = jnp.where(valid, att, 0.0)
            g = att * _nt(do16, vv)
            gpre = _ones_dot(before, g, ones_left=False, pieces=SB_PIECES)
            sig = jnp.exp(ls)
            dz16 = (g - sig * (g + (gpre + pg_sum))).astype(BF16)
            if valid is not None:
                dz16 = jnp.where(valid, dz16, jnp.zeros_like(dz16))
            dq = dq + _nn(dz16, kv)
            dk_acc[rows, :] += _tn(dz16, qs)
            dv_acc[rows, :] += _tn(att.astype(BF16), do16)
            return dq, pl_sum + pre[:, last:], pg_sum + (gpre[:, last:] + g[:, last:])

        zero = jnp.zeros((tq, 1), F32)
        nb = i * kpq
        start = jnp.clip(first_ref[pl.program_id(0), i], 0, nb)
        carry = lax.fori_loop(start, nb, lambda j, cr: block(j, cr, None), (jnp.zeros((tq, d), F32), zero, zero))
        for m in range(kpq):
            carry = block(nb + m, carry, diff < -m * SB_KEYS)
        dq_ref[0] = (carry[0] * scale).astype(dq_ref.dtype)

        @pl.when(i == nq - 1)
        def _():
            dk_ref[0] = dk_acc[...].astype(dk_ref.dtype)
            dv_ref[0] = dv_acc[...].astype(dv_ref.dtype)

    qs = pl.BlockSpec((1, tq, d), lambda hh, i: (hh, i, 0))
    ls = pl.BlockSpec((1, tq, 1), lambda hh, i: (hh, i, 0))
    ks = pl.BlockSpec((1, t, d), lambda hh, i: (hh, 0, 0))
    full = jax.ShapeDtypeStruct((h, t, d), BF16)
    return pl.pallas_call(
        body,
        grid=(h, nq),
        in_specs=[qs, ks, ks, ls, pl.BlockSpec(memory_space=pltpu.SMEM), qs],
        out_specs=[qs, ks, ks],
        out_shape=[full, full, full],
        scratch_shapes=[pltpu.VMEM((t, d), F32), pltpu.VMEM((t, d), F32)],
        compiler_params=_cparams(("arbitrary", "arbitrary")),
        name=name,
    )(q, k, v, lt, first, do)


def _row_tile(rows, cols):
    return _tile(rows, tuple(r for r in (2048, 1024, 512, 256, 128, 64, 32, 16, 8) if r * cols * 4 <= ADAM_BLOCK_BYTES))


def _sum_leading(x, *, name):
    n, rows, cols = x.shape
    tr = _row_tile(rows, cols)

    def body(x_ref, o_ref):
        acc = x_ref[0].astype(F32)
        for q in range(1, n):
            acc = acc + x_ref[q].astype(F32)
        o_ref[...] = acc

    return pl.pallas_call(
        body,
        grid=(rows // tr,),
        in_specs=[pl.BlockSpec((n, tr, cols), lambda i: (0, i, 0))],
        out_specs=pl.BlockSpec((tr, cols), lambda i: (i, 0)),
        out_shape=jax.ShapeDtypeStruct((rows, cols), F32),
        compiler_params=_cparams(("parallel",)),
        name=name,
    )(x)


def _pair_add(g4h, recv, c, *, out_dtype, name):
    n, _, rows, cols = g4h.shape
    tr = _row_tile(rows, cols)

    def body(c_ref, g_ref, r_ref, o_ref):
        o_ref[...] = (g_ref[...] + r_ref[...]).astype(o_ref.dtype)

    blk = pl.BlockSpec((1, tr, cols), lambda q, i, c_ref: (q, i, 0))
    return pl.pallas_call(
        body,
        grid_spec=pltpu.PrefetchScalarGridSpec(
            num_scalar_prefetch=1,
            grid=(n, rows // tr),
            in_specs=[pl.BlockSpec((1, None, tr, cols), lambda q, i, c_ref: (q, c_ref[0], i, 0)), blk],
            out_specs=blk),
        out_shape=jax.ShapeDtypeStruct((n, rows, cols), out_dtype),
        compiler_params=_cparams(("parallel", "parallel")),
        name=name,
    )(c.reshape(1).astype(jnp.int32), g4h, recv)


ANY = pl.BlockSpec(memory_space=pl.ANY)


def _other_chips(x, y):
    return [(1 - x, y), (x, 1 - y), (1 - x, 1 - y)]


def _gather_chips(shard, *, name):
    def body(x_ref, o_ref, send_sems, recv_sems, local_sem):
        x, y, c = lax.axis_index("x"), lax.axis_index("y"), lax.axis_index("c")
        me = 2 * x + y
        mine = pltpu.make_async_copy(x_ref, o_ref.at[me], local_sem)
        mine.start()
        chips = _other_chips(x, y)
        sends = [pltpu.make_async_remote_copy(src_ref=x_ref, dst_ref=o_ref.at[me], send_sem=send_sems.at[q],
                                              recv_sem=recv_sems.at[q], device_id=(px, py, c), device_id_type=MESH)
                 for q, (px, py) in enumerate(chips)]
        for cp in sends:
            cp.start()
        for q, (px, py) in enumerate(chips):
            pltpu.make_async_remote_copy(src_ref=x_ref, dst_ref=o_ref.at[2 * px + py], send_sem=send_sems.at[q],
                                         recv_sem=recv_sems.at[q], device_id=(px, py, c), device_id_type=MESH).wait_recv()
        for cp in sends:
            cp.wait_send()
        mine.wait()

    return pl.pallas_call(
        body,
        in_specs=[ANY],
        out_specs=ANY,
        out_shape=jax.ShapeDtypeStruct((4,) + shard.shape, shard.dtype),
        scratch_shapes=[pltpu.SemaphoreType.DMA((3,)), pltpu.SemaphoreType.DMA((3,)), pltpu.SemaphoreType.DMA],
        compiler_params=pltpu.CompilerParams(has_side_effects=True),
        name=name,
    )(shard)


def _comm_call(body, ins, out_shapes, n_sems, name):
    n = len(ins)

    def wrapped(*refs):
        body(refs[:n], refs[n:n + len(out_shapes)], refs[-2], refs[-1])

    return pl.pallas_call(
        wrapped,
        in_specs=[ANY] * n,
        out_specs=[ANY] * len(out_shapes),
        out_shape=out_shapes,
        scratch_shapes=[pltpu.SemaphoreType.DMA((n_sems,)), pltpu.SemaphoreType.DMA((n_sems,))],
        compiler_params=pltpu.CompilerParams(has_side_effects=True),
        name=name,
    )(*ins)


def _remote(send_sems, recv_sems, q, src, dst, to):
    return pltpu.make_async_remote_copy(src_ref=src, dst_ref=dst, send_sem=send_sems.at[q], recv_sem=recv_sems.at[q],
                                        device_id=to, device_id_type=MESH)


def _scatter_chips(parts, *, name):
    return _run_job(_scatter_job(parts), name)


def _scatter_job(parts):
    def sends(ins, outs, send_sems, recv_sems):
        x, y, c = lax.axis_index("x"), lax.axis_index("y"), lax.axis_index("c")
        return [_remote(send_sems, recv_sems, 3 * i + q, p.at[2 * px + py], o.at[2 * x + y], (px, py, c))
                for i, (p, o) in enumerate(zip(ins, outs)) for q, (px, py) in enumerate(_other_chips(x, y))]

    def start(ins, outs, send_sems, recv_sems):
        for cp in sends(ins, outs, send_sems, recv_sems):
            cp.start()

    def finish(ins, outs, send_sems, recv_sems):
        x, y, c = lax.axis_index("x"), lax.axis_index("y"), lax.axis_index("c")
        for i, (p, o) in enumerate(zip(ins, outs)):
            for q, (px, py) in enumerate(_other_chips(x, y)):
                _remote(send_sems, recv_sems, 3 * i + q, p.at[2 * x + y], o.at[2 * px + py], (px, py, c)).wait_recv()
        for cp in sends(ins, outs, send_sems, recv_sems):
            cp.wait_send()

    return _SideJob(parts, [jax.ShapeDtypeStruct(p.shape, p.dtype) for p in parts], 3 * len(parts), start, finish)


def _run_job(job, name):
    return _comm_call(lambda *refs: (job.start(*refs), job.finish(*refs)), job.ins, job.out_shapes, job.n_sems, name)


def _gather_job(shards):
    def sends(ins, outs, send_sems, recv_sems):
        x, y, c = lax.axis_index("x"), lax.axis_index("y"), lax.axis_index("c")
        return [_remote(send_sems, recv_sems, 6 * i + q, s.at[c], o.at[2 * x + y, c], (px, py, c))
                for i, (s, o) in enumerate(zip(ins, outs)) for q, (px, py) in enumerate(_other_chips(x, y))]

    def start(ins, outs, send_sems, recv_sems):
        for cp in sends(ins, outs, send_sems, recv_sems):
            cp.start()

    def finish(ins, outs, send_sems, recv_sems):
        x, y, c = lax.axis_index("x"), lax.axis_index("y"), lax.axis_index("c")
        sibling = (x, y, 1 - c)
        chips = _other_chips(x, y)
        copy = lambda q, src, dst, to: _remote(send_sems, recv_sems, q, src, dst, to)
        passed = []
        for i, (s, o) in enumerate(zip(ins, outs)):
            for q, (px, py) in enumerate(chips):
                slot = o.at[2 * px + py, c]
                copy(6 * i + q, s.at[c], slot, (px, py, c)).wait_recv()
                passed.append(copy(6 * i + 3 + q, slot, slot, sibling))
                passed[-1].start()
        for i, (s, o) in enumerate(zip(ins, outs)):
            for q, (px, py) in enumerate(chips):
                copy(6 * i + 3 + q, s.at[1 - c], o.at[2 * px + py, 1 - c], sibling).wait_recv()
        for cp in sends(ins, outs, send_sems, recv_sems) + passed:
            cp.wait_send()

    return _SideJob(shards, [jax.ShapeDtypeStruct((N_CHIPS,) + s.shape, s.dtype) for s in shards], 6 * len(shards),
                    start, finish)


def _swap_other_half(gs, *, name):
    def body(ins, outs, send_sems, recv_sems):
        x, y, c = lax.axis_index("x"), lax.axis_index("y"), lax.axis_index("c")
        copies = [_remote(send_sems, recv_sems, i, g.at[pl.ds(0, g.shape[0]), 1 - c], o, (x, y, 1 - c))
                  for i, (g, o) in enumerate(zip(ins, outs))]
        for cp in copies:
            cp.start()
        for cp in copies:
            cp.wait()

    return _comm_call(body, gs, [jax.ShapeDtypeStruct((g.shape[0],) + g.shape[2:], g.dtype) for g in gs], len(gs), name)


def _join_halves(halves, *, name):
    def body(ins, outs, send_sems, recv_sems):
        x, y, c = lax.axis_index("x"), lax.axis_index("y"), lax.axis_index("c")
        sibling = (x, y, 1 - c)
        sends = [_remote(send_sems, recv_sems, i, h, o.at[c], sibling) for i, (h, o) in enumerate(zip(ins, outs))]
        for cp in sends:
            cp.start()
        for i, (h, o) in enumerate(zip(ins, outs)):
            _remote(send_sems, recv_sems, i, h, o.at[1 - c], sibling).wait_recv()
        for cp in sends:
            cp.wait_send()

    return _comm_call(body, halves, [jax.ShapeDtypeStruct((2,) + h.shape, h.dtype) for h in halves], len(halves), name)


def _gather_all(v, *, name):
    def body(v_ref, o_ref, send_sems, recv_sems, local_sem):
        x, y, c = lax.axis_index("x"), lax.axis_index("y"), lax.axis_index("c")
        me = 4 * x + 2 * y + c
        mine = pltpu.make_async_copy(v_ref, o_ref.at[me], local_sem)
        mine.start()
        peers = [(x ^ (q >> 2 & 1), y ^ (q >> 1 & 1), c ^ (q & 1)) for q in range(1, 8)]
        sends = [pltpu.make_async_remote_copy(src_ref=v_ref, dst_ref=o_ref.at[me], send_sem=send_sems.at[q],
                                              recv_sem=recv_sems.at[q], device_id=peer, device_id_type=MESH)
                 for q, peer in enumerate(peers)]
        for cp in sends:
            cp.start()
        for q, (px, py, pc) in enumerate(peers):
            pltpu.make_async_remote_copy(src_ref=v_ref, dst_ref=o_ref.at[4 * px + 2 * py + pc], send_sem=send_sems.at[q],
                                         recv_sem=recv_sems.at[q], device_id=(px, py, pc), device_id_type=MESH).wait_recv()
        for cp in sends:
            cp.wait_send()
        mine.wait()

    return pl.pallas_call(
        body,
        in_specs=[ANY],
        out_specs=ANY,
        out_shape=jax.ShapeDtypeStruct((8,) + v.shape, v.dtype),
        scratch_shapes=[pltpu.SemaphoreType.DMA((7,)), pltpu.SemaphoreType.DMA((7,)), pltpu.SemaphoreType.DMA],
        compiler_params=pltpu.CompilerParams(has_side_effects=True),
        name=name,
    )(v)


WEIGHTS = ['ssm_norm_w', 'ssm_in_w', 'ssm_conv_w', 'ssm_conv_b', 'ssm_dt_bias', 'ssm_a_log', 'ssm_d',
           'ssm_gate_norm_w', 'ssm_out_w', 'kv_norm_w', 'w_k', 'w_v', 'attn_norm_w', 'w_q', 'w_o',
           'ffn_norm_w', 'ffn_up_w', 'ffn_conv_w', 'ffn_conv_b', 'ffn_down_w', 'final_norm_w']
SHARD_AXIS = {'ssm_norm_w': 1, 'ssm_in_w': 2, 'ssm_conv_w': 2, 'ssm_conv_b': 1, 'ssm_gate_norm_w': 1,
              'ssm_out_w': 1, 'w_k': 0, 'w_v': 0, 'w_q': 1, 'w_o': 1, 'ffn_up_w': 2, 'ffn_conv_w': 2,
              'ffn_down_w': 1}
BIG = ['ssm_in_w', 'ssm_out_w', 'w_k', 'w_v', 'w_q', 'w_o', 'ffn_up_w', 'ffn_down_w']
SMALL = [n for n in WEIGHTS if n in SHARD_AXIS and n not in BIG]
REPLICATED = [n for n in WEIGHTS if n not in SHARD_AXIS]
STACKED = ['ffn_up_w', 'ffn_down_w']
N_CHIPS = 4


PACK_ROWS = 16


def _piece_rows(n):
    return -(-n // (PACK_ROWS * LANES)) * PACK_ROWS


def _pack(arrs, dtype, row_mult):
    lead = arrs[0].shape[:-1]
    pieces, total = [], 0
    for a in arrs:
        n = a.shape[-1]
        rows = _piece_rows(n)
        a = a.astype(dtype)
        if rows * LANES != n:
            a = jnp.pad(a, [(0, 0)] * len(lead) + [(0, rows * LANES - n)])
        pieces.append(a.reshape(lead + (rows, LANES)))
        total += rows
    extra = -total % row_mult
    if extra:
        pieces.append(jnp.zeros(lead + (extra, LANES), dtype))
    return jnp.concatenate(pieces, axis=len(lead))


def _unpack(buf, shapes):
    lead = buf.shape[:-2]
    out, off = [], 0
    for shp in shapes:
        n = math.prod(shp)
        rows = _piece_rows(n)
        piece = lax.slice_in_dim(buf, off, off + rows, axis=len(lead)).reshape(lead + (rows * LANES,))
        out.append(piece[..., :n].reshape(lead + tuple(shp)))
        off += rows
    return out


def _set_slot(buf, piece, index):
    return lax.dynamic_update_slice_in_dim(buf, piece[None], index, axis=0)


def _from_shards(stacked, axis):
    return jnp.concatenate([stacked[j] for j in range(N_CHIPS)], axis=axis)


def _heads(a, h):
    t = a.shape[0]
    return a.reshape(t, h, a.shape[1] // h).transpose(1, 0, 2)


def _unheads(a):
    h, t, d = a.shape
    return a.transpose(1, 0, 2).reshape(t, h * d)


def _ffn_fwd(h, norm_w, w_up, conv_w, conv_b, w_down, tag, side=None):
    u = _rmsnorm_fwd(h, norm_w, name=f"ffn{tag}_norm")
    hid = _matmul(u, w_up, name=f"ffn{tag}_up")
    act, side_outs = _conv_glu_fwd(hid, conv_w, conv_b, side=side, name=f"ffn{tag}_glu")
    out = _matmul(act, w_down, add=h, name=f"ffn{tag}_down")
    return out, (u, hid, act), side_outs


def _ffn_bwd(h, saved, dout, norm_w, w_up, conv_w, conv_b, w_down, tag):
    u, hid, act = saved
    dact = _matmul(dout, w_down, tb=True, name=f"ffn{tag}_down_dx")
    dw_down = _matmul(act, dout, ta=True, name=f"ffn{tag}_down_dw")
    dhid, dwg, dwv, dbg, dbv = _conv_glu_bwd(hid, conv_w, conv_b, dact, name=f"ffn{tag}_glu_bwd")
    du = _matmul(dhid, w_up, tb=True, name=f"ffn{tag}_up_dx")
    dw_up = _matmul(u, dhid, ta=True, out_parts=N_CHIPS, name=f"ffn{tag}_up_dw")
    dh, (dnorm,) = _rmsnorm_bwd(h, [(du, norm_w)], dout, name=f"ffn{tag}_norm_bwd")
    return dh, dict(norm=dnorm[0], up=dw_up, conv_w=jnp.concatenate([dwg, dwv], axis=1),
                    conv_b=jnp.concatenate([dbg, dbv], axis=1)[0], down=dw_down)


class _Pieces:
    def __init__(self, local):
        self.c = lax.axis_index("c")
        self.chip = 2 * lax.axis_index("x") + lax.axis_index("y")
        self.shape, self.s16 = {}, {}
        for n in BIG:
            blk = local[n]
            layers = [(n, l, blk[l]) for l in range(blk.shape[0])] if n in STACKED else [(n, None, blk.reshape(blk.shape[-2:]))]
            for name, l, p in layers:
                self.shape[name, l] = p.shape
                self.s16[name, l] = p.astype(BF16).reshape(2, p.shape[0] // 2, p.shape[1])

    def gather_job(self, keys):
        return _gather_job([self.s16[k] for k in keys])

    def weights(self, keys, gathered):
        out = []
        for k, g in zip(keys, gathered):
            r, cc = self.shape[k]
            by_chip = _set_slot(g, self.s16[k], self.chip).reshape(N_CHIPS, r, cc)
            if k[0] == 'ssm_in_w':
                by_chip = by_chip.transpose(1, 0, 2).reshape(r, N_CHIPS * cc)
            elif k[0] != 'ffn_up_w':
                by_chip = by_chip.reshape(N_CHIPS * r, cc)
            out.append(by_chip)
        return out

    def pair_sums(self, keys, grads, tag):
        gs = []
        for k, g in zip(keys, grads):
            r, cc = self.shape[k]
            if k[0] == 'ssm_in_w':
                g = g.reshape(r, N_CHIPS, cc).transpose(1, 0, 2)
            gs.append(g.reshape(N_CHIPS, 2, r // 2, cc))
        recv = _swap_other_half(gs, name=f"rs_pair_swap_{tag}")
        return [_pair_add(g, rv, self.c, out_dtype=BF16, name=f"rs_pair_add_{tag}{i}") for i, (g, rv) in enumerate(zip(gs, recv))]

    def chip_sums(self, pairs, scattered, tag):
        return [_sum_leading(_set_slot(s, lax.dynamic_index_in_dim(p, self.chip, axis=0, keepdims=False), self.chip),
                             name=f"rs_chip_sum_{tag}{i}") for i, (s, p) in enumerate(zip(scattered, pairs))]

    def shards(self, keys, halves):
        joined = _join_halves(halves, name="rs_half_join")
        return {k: _set_slot(j, h, self.c).reshape(self.shape[k]) for k, h, j in zip(keys, halves, joined)}


def _step(x, target, w, pieces):
    t = x.shape[0]
    g_n, heads = SSM_GROUPS, SSM_HEADS
    r_h = heads // g_n
    di = D_INNER
    zx_cols = di + CONV_DIM
    k_in = [('ssm_in_w', None)]
    k_ffn0 = [('ssm_out_w', None), ('ffn_up_w', 0), ('ffn_down_w', 0)]
    k_qkv = [('w_k', None), ('w_v', None), ('w_q', None)]
    k_late = [('w_o', None), ('ffn_up_w', 1), ('ffn_down_w', 1)]
    (w_in,) = pieces.weights(k_in, _run_job(pieces.gather_job(k_in), "gather_ssm_in"))
    w_zx = w_in[:, :zx_cols]
    w_dt = jnp.pad(w_in[:, zx_cols:], ((0, 0), (0, LANES - heads)))
    conv_w, conv_b = w['ssm_conv_w'][0], w['ssm_conv_b'][0]
    hp = jnp.stack([w['ssm_dt_bias'][0], w['ssm_a_log'][0], w['ssm_d'][0]], axis=0).reshape(3, g_n, r_h)
    hpc, hpr = hp.transpose(1, 0, 2), hp.transpose(1, 2, 0)

    h0 = x
    u0 = _rmsnorm_fwd(h0, w['ssm_norm_w'][0], name="ssm_norm")
    zx = _matmul(u0, w_zx, name="ssm_in_zx")
    dt_raw = _matmul(u0, w_dt, name="ssm_in_dt")[:, :heads]
    dtg = dt_raw.reshape(t, g_n, r_h)
    dtc, dtr = dtg.transpose(1, 0, 2), dtg.transpose(1, 2, 0)
    xbc = _conv_silu_fwd(zx, conv_w, conv_b, x_off=di, name="ssm_conv")
    y, prev, got = _ssd_fwd(xbc, dtc, dtr, hpc, hpr, side=pieces.gather_job(k_ffn0), name="ssd_fwd")
    w_out, w_up0, w_down0 = pieces.weights(k_ffn0, got)
    yn = _gate_norm_fwd(y, zx, w['ssm_gate_norm_w'][0], name="ssm_gate_norm")
    h1 = _matmul(yn, w_out, add=h0, name="ssm_out")
    h2, ffn0, got = _ffn_fwd(h1, w['ffn_norm_w'][0], w_up0, w['ffn_conv_w'][0], w['ffn_conv_b'][0], w_down0, 0,
                             side=pieces.gather_job(k_qkv))
    w_k, w_v, w_q = pieces.weights(k_qkv, got)
    hk = _rmsnorm_fwd(h2, w['kv_norm_w'], name="kv_norm")
    qn = _rmsnorm_fwd(h2, w['attn_norm_w'][0], name="attn_norm")
    k2 = _matmul(hk, w_k, out_dtype=BF16, name="attn_k")
    v2 = _matmul(hk, w_v, out_dtype=BF16, name="attn_v")
    q2 = _matmul(qn, w_q, out_dtype=BF16, name="attn_q")
    qh, kh, vh = _heads(q2, SB_HEADS), _heads(k2, SB_HEADS), _heads(v2, SB_HEADS)
    oh, lt, first, got = _sb_fwd(qh, kh, vh, side=pieces.gather_job(k_late), name="sb_fwd")
    w_o, w_up1, w_down1 = pieces.weights(k_late, got)
    o2 = _unheads(oh)
    h3 = _matmul(o2, w_o, add=h2, name="attn_o")
    h4, ffn1, _ = _ffn_fwd(h3, w['ffn_norm_w'][1], w_up1, w['ffn_conv_w'][1], w['ffn_conv_b'][1], w_down1, 1)
    loss_p, dh4, d_final = _loss_head(h4, w['final_norm_w'], target, name="loss_head")

    dh3, g1 = _ffn_bwd(h3, ffn1, dh4, w['ffn_norm_w'][1], w_up1, w['ffn_conv_w'][1], w['ffn_conv_b'][1], w_down1, 1)
    do2 = _matmul(dh3, w_o, tb=True, out_dtype=BF16, name="attn_o_dx")
    dw_o = _matmul(o2, dh3, ta=True, name="attn_o_dw")
    dqh, dkh, dvh = _sb_bwd(qh, kh, vh, lt, first, _heads(do2, SB_HEADS), name="sb_bwd")
    dq2, dk2, dv2 = _unheads(dqh), _unheads(dkh), _unheads(dvh)
    dqn = _matmul(dq2, w_q, tb=True, name="attn_q_dx")
    dw_q = _matmul(qn, dq2, ta=True, name="attn_q_dw")
    dhk = _matmul(dk2, w_k, tb=True, name="attn_k_dx")
    dhk = _matmul(dv2, w_v, tb=True, add=dhk, name="attn_v_dx")
    dw_k = _matmul(hk, dk2, ta=True, name="attn_k_dw")
    dw_v = _matmul(hk, dv2, ta=True, name="attn_v_dw")
    dh2, (d_attn_norm, d_kv_norm) = _rmsnorm_bwd(h2, [(dqn, w['attn_norm_w'][0]), (dhk, w['kv_norm_w'])], dh3,
                                                 name="attn_norms_bwd")
    dh1, g0 = _ffn_bwd(h1, ffn0, dh2, w['ffn_norm_w'][0], w_up0, w['ffn_conv_w'][0], w['ffn_conv_b'][0], w_down0, 0)
    dyn = _matmul(dh1, w_out, tb=True, name="ssm_out_dx")
    dw_out = _matmul(yn, dh1, ta=True, name="ssm_out_dw")
    k_done = k_qkv + k_late + k_ffn0
    pairs_done = pieces.pair_sums(k_done, [dw_k, dw_v, dw_q, dw_o, g1['up'], g1['down'], dw_out, g0['up'], g0['down']], "a")
    dy, dz, d_gate = _gate_norm_bwd(y, zx, w['ssm_gate_norm_w'][0], dyn, name="ssm_gate_norm_bwd")
    dxs, dbm, dcm, ddt_g, hg, scattered_done = _ssd_bwd(xbc, dtc, dtr, hpc, hpr, prev, dy,
                                                        side=_scatter_job(pairs_done), name="ssd_bwd")
    dxbc = jnp.concatenate([dxs, dbm, dcm], axis=1)
    dzx, d_conv_w, d_conv_b = _conv_silu_bwd(zx, conv_w, conv_b, dxbc, x_off=di, into=dz, name="ssm_conv_bwd")
    ddt = jnp.pad(ddt_g.transpose(1, 0, 2).reshape(t, heads), ((0, 0), (0, LANES - heads)))
    du0 = _matmul(dzx, w_zx, tb=True, name="ssm_in_zx_dx")
    du0 = _matmul(ddt, w_dt, tb=True, add=du0, name="ssm_in_dt_dx")
    dw_in = jnp.concatenate([_matmul(u0, dzx, ta=True, name="ssm_in_zx_dw"),
                             _matmul(u0, ddt, ta=True, name="ssm_in_dt_dw")[:, :heads]], axis=1)
    dx, (d_ssm_norm,) = _rmsnorm_bwd(h0, [(du0, w['ssm_norm_w'][0])], dh1, name="ssm_norm_bwd")

    pairs_in = pieces.pair_sums(k_in, [dw_in], "b")
    halves = (pieces.chip_sums(pairs_done, scattered_done, "a")
              + pieces.chip_sums(pairs_in, _scatter_chips(pairs_in, name="rs_chip_scatter_b"), "b"))
    big_grads = pieces.shards(k_done + k_in, halves)

    hgr = hg.transpose(1, 0, 2).reshape(3, heads)
    grads = {
        'ssm_norm_w': d_ssm_norm, 'ssm_conv_w': d_conv_w[None], 'ssm_conv_b': d_conv_b,
        'ssm_dt_bias': hgr[0:1], 'ssm_a_log': hgr[1:2], 'ssm_d': hgr[2:3], 'ssm_gate_norm_w': d_gate,
        'kv_norm_w': d_kv_norm[0], 'attn_norm_w': d_attn_norm, 'ffn_norm_w': jnp.stack([g0['norm'], g1['norm']]),
        'ffn_conv_w': jnp.stack([g0['conv_w'], g1['conv_w']]), 'ffn_conv_b': jnp.stack([g0['conv_b'], g1['conv_b']]),
        'final_norm_w': d_final[0],
    }
    return loss_p, dx, grads, big_grads


def kernel(x, ssm_norm_w, ssm_in_w, ssm_conv_w, ssm_conv_b, ssm_dt_bias, ssm_a_log, ssm_d, ssm_gate_norm_w, ssm_out_w, kv_norm_w, w_k, w_v, attn_norm_w, w_q, w_o, ffn_norm_w, ffn_up_w, ffn_conv_w, ffn_conv_b, ffn_down_w, final_norm_w, loss_target, m_ssm_norm_w, m_ssm_in_w, m_ssm_conv_w, m_ssm_conv_b, m_ssm_dt_bias, m_ssm_a_log, m_ssm_d, m_ssm_gate_norm_w, m_ssm_out_w, m_kv_norm_w, m_w_k, m_w_v, m_attn_norm_w, m_w_q, m_w_o, m_ffn_norm_w, m_ffn_up_w, m_ffn_conv_w, m_ffn_conv_b, m_ffn_down_w, m_final_norm_w, v_ssm_norm_w, v_ssm_in_w, v_ssm_conv_w, v_ssm_conv_b, v_ssm_dt_bias, v_ssm_a_log, v_ssm_d, v_ssm_gate_norm_w, v_ssm_out_w, v_kv_norm_w, v_w_k, v_w_v, v_attn_norm_w, v_w_q, v_w_o, v_ffn_norm_w, v_ffn_up_w, v_ffn_conv_w, v_ffn_conv_b, v_ffn_down_w, v_final_norm_w):
    args = (ssm_norm_w, ssm_in_w, ssm_conv_w, ssm_conv_b, ssm_dt_bias, ssm_a_log, ssm_d, ssm_gate_norm_w, ssm_out_w, kv_norm_w, w_k, w_v, attn_norm_w, w_q, w_o, ffn_norm_w, ffn_up_w, ffn_conv_w, ffn_conv_b, ffn_down_w, final_norm_w)
    moms = (m_ssm_norm_w, m_ssm_in_w, m_ssm_conv_w, m_ssm_conv_b, m_ssm_dt_bias, m_ssm_a_log, m_ssm_d, m_ssm_gate_norm_w, m_ssm_out_w, m_kv_norm_w, m_w_k, m_w_v, m_attn_norm_w, m_w_q, m_w_o, m_ffn_norm_w, m_ffn_up_w, m_ffn_conv_w, m_ffn_conv_b, m_ffn_down_w, m_final_norm_w)
    vels = (v_ssm_norm_w, v_ssm_in_w, v_ssm_conv_w, v_ssm_conv_b, v_ssm_dt_bias, v_ssm_a_log, v_ssm_d, v_ssm_gate_norm_w, v_ssm_out_w, v_kv_norm_w, v_w_k, v_w_v, v_attn_norm_w, v_w_q, v_w_o, v_ffn_norm_w, v_ffn_up_w, v_ffn_conv_w, v_ffn_conv_b, v_ffn_down_w, v_final_norm_w)
    local = dict(zip(WEIGHTS, args))
    m_in = dict(zip(WEIGHTS, moms))
    v_in = dict(zip(WEIGHTS, vels))
    chip = 2 * lax.axis_index("x") + lax.axis_index("y")

    full = {n: local[n] for n in REPLICATED}
    small32 = _gather_chips(_pack([local[n].reshape(-1) for n in SMALL], F32, 8), name="gather_small")
    for n, st in zip(SMALL, _unpack(small32, [local[n].shape for n in SMALL])):
        full[n] = _from_shards(st, SHARD_AXIS[n])

    pieces = _Pieces(local)
    loss_p, dx, grads, big_grads = _step(x[0], loss_target[0], full, pieces)
    gshard = {}
    for n in BIG:
        if n in STACKED:
            gshard[n] = [big_grads[n, l] for l in range(local[n].shape[0])]
        else:
            gshard[n] = big_grads[n, None].reshape(local[n].shape)

    small = SMALL + REPLICATED
    rep = _pack([loss_p.reshape(-1)] + [grads[n].reshape(-1) for n in small], F32, 8)
    tot = _sum_leading(_gather_all(rep, name="ar_gather"), name="ar_sum")
    parts = _unpack(tot, [(LANES,)] + [grads[n].shape for n in small])
    loss = jnp.sum(parts[0])
    for n, g in zip(small, parts[1:]):
        if n in SHARD_AXIS:
            size = local[n].shape[SHARD_AXIS[n]]
            g = lax.dynamic_slice_in_dim(g, chip * size, size, axis=SHARD_AXIS[n])
        gshard[n] = g

    grads_out, deltas, new_m, new_v = [], [], [], []
    for n in WEIGHTS:
        if n in STACKED:
            g, d, nm, nv = _adamw_layers(local[n], gshard[n], m_in[n], v_in[n], name=f"adamw_{n}")
        else:
            g = gshard[n]
            d, nm, nv = _adamw(local[n], g, m_in[n], v_in[n], name=f"adamw_{n}")
        grads_out.append(g)
        deltas.append(d)
        new_m.append(nm)
        new_v.append(nv)
    return (loss, dx[None], *grads_out, *deltas, *new_m, *new_v)
```

```python
import functools
import math

import jax
import jax.numpy as jnp
from jax import lax
from jax.experimental import pallas as pl
from jax.experimental.pallas import tpu as pltpu

D_MODEL = 1024
D_INNER = 2048
SSM_HEAD_DIM = 64
SSM_HEADS = 32
SSM_GROUPS = 4
SSM_STATE = 128
SSM_CONV = 4
SSM_CHUNK = 128
GN = SSM_GROUPS * SSM_STATE
CONV_DIM = D_INNER + 2 * GN
SB_HEADS = 16
SB_HEAD_DIM = 64
D_FF = 2816
FFN_CONV = 3
EPS = 1e-6
ADAM_LR = 0.001
ADAM_B1 = 0.9
ADAM_B2 = 0.999
ADAM_EPS = 1e-08
ADAM_WD = 0.01
ADAM_STEP = 10

LANES = 128
SUBLANES = 8
VMEM_LIMIT = 48 * 1024 * 1024
ADAM_BLOCK_BYTES = 1 << 20
F32 = jnp.float32
BF16 = jnp.bfloat16
MESH = pl.DeviceIdType.MESH


def _cparams(sem=None):
    return pltpu.CompilerParams(dimension_semantics=sem, vmem_limit_bytes=VMEM_LIMIT)


class _SideJob:
    def __init__(self, ins, out_shapes, n_sems, start, finish):
        self.ins, self.out_shapes, self.n_sems, self.start, self.finish = ins, out_shapes, n_sems, start, finish


def _call(body, *, grid, in_specs, out_specs, out_shape, scratch_shapes=(), sem, name, args, side=None):
    in_specs, out_specs, out_shape, scratch_shapes = list(in_specs), list(out_specs), list(out_shape), list(scratch_shapes)
    n_in, n_out = len(in_specs), len(out_specs)
    if side is None:
        outs = pl.pallas_call(body, grid=grid, in_specs=in_specs, out_specs=out_specs, out_shape=out_shape,
                              scratch_shapes=scratch_shapes, compiler_params=_cparams(sem), name=name)(*args)
        return list(outs), []
    k_in, k_out = len(side.ins), len(side.out_shapes)

    def wrapped(*refs):
        ins, s_ins = refs[:n_in], refs[n_in:n_in + k_in]
        o0 = n_in + k_in
        outs, s_outs = refs[o0:o0 + n_out], refs[o0 + n_out:o0 + n_out + k_out]
        scratch, send_sems, recv_sems = refs[o0 + n_out + k_out:-2], refs[-2], refs[-1]
        ids = [pl.program_id(a) for a in range(len(grid))]
        first = functools.reduce(jnp.logical_and, [p == 0 for p in ids])
        last = functools.reduce(jnp.logical_and, [p == g - 1 for p, g in zip(ids, grid)])

        @pl.when(first)
        def _():
            side.start(s_ins, s_outs, send_sems, recv_sems)

        body(*ins, *outs, *scratch)

        @pl.when(last)
        def _():
            side.finish(s_ins, s_outs, send_sems, recv_sems)

    outs = pl.pallas_call(
        wrapped, grid=grid, in_specs=in_specs + [ANY] * k_in, out_specs=out_specs + [ANY] * k_out,
        out_shape=out_shape + list(side.out_shapes),
        scratch_shapes=scratch_shapes + [pltpu.SemaphoreType.DMA((side.n_sems,)), pltpu.SemaphoreType.DMA((side.n_sems,))],
        compiler_params=_cparams(tuple("arbitrary" for _ in grid)), name=name)(*args, *side.ins)
    return list(outs[:n_out]), list(outs[n_out:])


def _tile(n, cands):
    for c in cands:
        if n % c == 0:
            return c
    return n


def _nt(a, b):
    return lax.dot_general(a, b, (((1,), (1,)), ((), ())), preferred_element_type=F32)


def _tn(a, b):
    return lax.dot_general(a, b, (((0,), (0,)), ((), ())), preferred_element_type=F32)


def _nn(a, b):
    return jnp.dot(a, b, preferred_element_type=F32)


def _split(x, pieces):
    out = []
    for _ in range(pieces - 1):
        h = x.astype(BF16)
        out.append(h)
        x = x - h.astype(F32)
    out.append(x.astype(BF16))
    return out


def _ones_dot(ones, x, *, ones_left, pieces=3):
    o16 = ones.astype(BF16)
    acc = None
    for piece in _split(x, pieces):
        term = _nn(o16, piece) if ones_left else _nn(piece, o16)
        acc = term if acc is None else acc + term
    return acc


def _row_sums(x, pieces=3):
    return _ones_dot(jnp.ones((x.shape[1], LANES), F32), x, ones_left=False, pieces=pieces)


def _softplus(x):
    return jnp.maximum(x, 0.0) + jnp.log(1.0 + jnp.exp(-jnp.abs(x)))


def _sigmoid(x):
    return 0.5 * jnp.tanh(0.5 * x) + 0.5


MM_TILE_MAX = 1408
MM_VMEM_BUDGET = 40 * 1024 * 1024


def _divisors(n, cap):
    out = [d for d in range(min(cap, n) // LANES * LANES, 0, -LANES) if n % d == 0]
    return out or [n]


def _mm_tiles(m, n, k, a_bytes, b_bytes, o_bytes, add_bytes):
    best = None
    for tm in _divisors(m, MM_TILE_MAX):
        for tn in _divisors(n, MM_TILE_MAX):
            for tk in _divisors(k, MM_TILE_MAX):
                vmem = 2 * (tm * tk * a_bytes + tk * tn * b_bytes + tm * tn * (o_bytes + add_bytes)) + tm * tn * 4
                if vmem > MM_VMEM_BUDGET:
                    continue
                score = (tm * tn * tk, tm * tn)
                if best is None or score > best[0]:
                    best = (score, (tm, tn, tk))
    return best[1]


def _matmul(a, b, *, ta=False, tb=False, add=None, out_dtype=F32, out_parts=1, name):
    a_parts = a.shape[0] if a.ndim == 3 else 1
    b_parts = b.shape[0] if b.ndim == 3 else 1
    assert not (ta and a_parts > 1)
    a2, b2 = a.shape[-2:], b.shape[-2:]
    m, k = (a2[1], a2[0]) if ta else (a2[0], a2[1] * a_parts)
    n, kb = (b2[0], b2[1] * b_parts) if tb else (b2[1] * b_parts, b2[0])
    assert kb == k, (a.shape, b.shape)
    n_unit = math.gcd(n // out_parts, n if tb else b2[1])
    k_unit = math.gcd(k // a_parts, b2[1] if tb else k)
    tm, tn, tk = _mm_tiles(m, n_unit, k_unit, a.dtype.itemsize, b.dtype.itemsize, jnp.dtype(out_dtype).itemsize,
                           0 if add is None else add.dtype.itemsize)
    nk = k // tk
    ka, kbp = (k // a_parts) // tk, (k // b_parts) // tk
    nb, no = (n // b_parts) // tn, (n // out_parts) // tn

    def body(*refs):
        if add is None:
            a_ref, b_ref, o_ref = refs[:3]
            add_ref = None
        else:
            a_ref, b_ref, add_ref, o_ref = refs[:4]
        kk = pl.program_id(2)
        dn = (((0 if ta else 1,), (1 if tb else 0,)), ((), ()))
        prod = lax.dot_general(a_ref[...].astype(BF16), b_ref[...].astype(BF16), dn, preferred_element_type=F32)

        def finish(r):
            if add_ref is not None:
                r = r + add_ref[...].astype(F32)
            o_ref[...] = r.astype(o_ref.dtype)

        if nk == 1:
            finish(prod)
            return
        acc_ref = refs[-1]

        @pl.when(kk == 0)
        def _():
            acc_ref[...] = prod

        @pl.when(jnp.logical_and(kk > 0, kk < nk - 1))
        def _():
            acc_ref[...] += prod

        @pl.when(kk == nk - 1)
        def _():
            finish(acc_ref[...] + prod)

    if ta:
        a_spec = pl.BlockSpec((tk, tm), lambda i, j, kk: (kk, i))
    elif a_parts > 1:
        a_spec = pl.BlockSpec((None, tm, tk), lambda i, j, kk: (kk // ka, i, kk % ka))
    else:
        a_spec = pl.BlockSpec((tm, tk), lambda i, j, kk: (i, kk))
    if b_parts == 1:
        b_spec = pl.BlockSpec((tn, tk), lambda i, j, kk: (j, kk)) if tb else pl.BlockSpec((tk, tn), lambda i, j, kk: (kk, j))
    elif tb:
        b_spec = pl.BlockSpec((None, tn, tk), lambda i, j, kk: (kk // kbp, j, kk % kbp))
    else:
        b_spec = pl.BlockSpec((None, tk, tn), lambda i, j, kk: (j // nb, kk, j % nb))
    if out_parts > 1:
        o_spec = pl.BlockSpec((None, tm, tn), lambda i, j, kk: (j // no, i, j % no))
        o_shape = jax.ShapeDtypeStruct((out_parts, m, n // out_parts), out_dtype)
    else:
        o_spec = pl.BlockSpec((tm, tn), lambda i, j, kk: (i, j))
        o_shape = jax.ShapeDtypeStruct((m, n), out_dtype)
    in_specs = [a_spec, b_spec]
    args = [a, b]
    if add is not None:
        in_specs.append(pl.BlockSpec((tm, tn), lambda i, j, kk: (i, j)))
        args.append(add)
    return pl.pallas_call(
        body,
        grid=(m // tm, n // tn, nk),
        in_specs=in_specs,
        out_specs=o_spec,
        out_shape=o_shape,
        scratch_shapes=[pltpu.VMEM((tm, tn), F32)] if nk > 1 else [],
        compiler_params=_cparams(("parallel", "parallel", "arbitrary")),
        name=name,
    )(*args)


def _rmsnorm_fwd(x, w, *, name):
    t, d = x.shape
    tb = _tile(t, (512, 256, 128))

    def body(x_ref, w_ref, o_ref):
        xv = x_ref[...]
        r = lax.rsqrt(jnp.mean(xv * xv, axis=-1, keepdims=True) + EPS)
        o_ref[...] = (xv * r * w_ref[...]).astype(o_ref.dtype)

    return pl.pallas_call(
        body,
        grid=(t // tb,),
        in_specs=[pl.BlockSpec((tb, d), lambda i: (i, 0)), pl.BlockSpec((1, d), lambda i: (0, 0))],
        out_specs=pl.BlockSpec((tb, d), lambda i: (i, 0)),
        out_shape=jax.ShapeDtypeStruct((t, d), BF16),
        compiler_params=_cparams(("parallel",)),
        name=name,
    )(x, w.reshape(1, d))


def _rmsnorm_bwd(x, dys, dres, *, name):
    t, d = x.shape
    tb = _tile(t, (256, 128))
    nn = len(dys)
    has_res = dres is not None

    def body(*refs):
        x_ref = refs[0]
        dy_refs = refs[1:1 + nn]
        w_refs = refs[1 + nn:1 + 2 * nn]
        pos = 1 + 2 * nn
        res_ref = refs[pos] if has_res else None
        pos += 1 if has_res else 0
        dx_ref = refs[pos]
        dw_refs = refs[pos + 1:pos + 1 + nn]
        i = pl.program_id(0)
        xv = x_ref[...]
        r = lax.rsqrt(jnp.mean(xv * xv, axis=-1, keepdims=True) + EPS)
        xn = xv * r
        dx = res_ref[...] if has_res else jnp.zeros_like(xv)
        for q in range(nn):
            dy = dy_refs[q][...].astype(F32)
            g = dy * w_refs[q][...]
            dx = dx + r * (g - xn * jnp.mean(g * xn, axis=-1, keepdims=True))
            dwp = jnp.sum(dy * xn, axis=0, keepdims=True)

            @pl.when(i == 0)
            def _(q=q, dwp=dwp):
                dw_refs[q][...] = dwp

            @pl.when(i > 0)
            def _(q=q, dwp=dwp):
                dw_refs[q][...] += dwp
        dx_ref[...] = dx

    row = pl.BlockSpec((tb, d), lambda i: (i, 0))
    vec = pl.BlockSpec((1, d), lambda i: (0, 0))
    in_specs = [row] + [row] * nn + [vec] * nn + ([row] if has_res else [])
    args = [x] + [p[0] for p in dys] + [p[1].reshape(1, d) for p in dys] + ([dres] if has_res else [])
    outs = pl.pallas_call(
        body,
        grid=(t // tb,),
        in_specs=in_specs,
        out_specs=[row] + [vec] * nn,
        out_shape=[jax.ShapeDtypeStruct((t, d), F32)] + [jax.ShapeDtypeStruct((1, d), F32)] * nn,
        compiler_params=_cparams(("arbitrary",)),
        name=name,
    )(*args)
    return outs[0], list(outs[1:])


def _loss_head(x, w, target, *, name):
    t, d = x.shape
    tb = _tile(t, (256, 128))

    def body(x_ref, w_ref, t_ref, loss_ref, dx_ref, dw_ref):
        i = pl.program_id(0)
        xv = x_ref[...]
        wv = w_ref[...]
        r = lax.rsqrt(jnp.mean(xv * xv, axis=-1, keepdims=True) + EPS)
        xn = xv * r
        e = xn * wv - t_ref[...]
        lp = 0.5 * jnp.sum(jnp.mean(e * e, axis=-1, keepdims=True), axis=0, keepdims=True)
        dy = e * (1.0 / d)
        g = dy * wv
        dx_ref[...] = r * (g - xn * jnp.mean(g * xn, axis=-1, keepdims=True))
        dwp = jnp.sum(dy * xn, axis=0, keepdims=True)
        lpv = jnp.broadcast_to(lp, (1, LANES)) * (1.0 / LANES)

        @pl.when(i == 0)
        def _():
            dw_ref[...] = dwp
            loss_ref[...] = lpv

        @pl.when(i > 0)
        def _():
            dw_ref[...] += dwp
            loss_ref[...] += lpv

    row = pl.BlockSpec((tb, d), lambda i: (i, 0))
    vec = pl.BlockSpec((1, d), lambda i: (0, 0))
    return pl.pallas_call(
        body,
        grid=(t // tb,),
        in_specs=[row, vec, row],
        out_specs=[pl.BlockSpec((1, LANES), lambda i: (0, 0)), row, vec],
        out_shape=[jax.ShapeDtypeStruct((1, LANES), F32), jax.ShapeDtypeStruct((t, d), F32),
                   jax.ShapeDtypeStruct((1, d), F32)],
        compiler_params=_cparams(("arbitrary",)),
        name=name,
    )(x, w.reshape(1, d), target)


ROW_CHUNK = 64
PAD = SUBLANES


def _shifted(pad_ref, r0, rows, back):
    return pad_ref[pl.ds(PAD + r0 - back, rows), :]


def _conv_taps(pad_ref, w_ref, r0, rows, kw):
    acc = None
    for j in range(kw):
        term = _shifted(pad_ref, r0, rows, kw - 1 - j) * w_ref[j:j + 1, :]
        acc = term if acc is None else acc + term
    return acc


def _fill_pad(pad_ref, x_ref, t):
    pad_ref[0:PAD, :] = jnp.zeros((PAD, pad_ref.shape[1]), F32)
    pad_ref[pl.ds(PAD + t, PAD), :] = jnp.zeros((PAD, pad_ref.shape[1]), F32)
    pad_ref[pl.ds(PAD, t), :] = x_ref[...].astype(F32)


def _conv_silu_fwd(x, w, b, *, x_off=0, name):
    t = x.shape[0]
    kw, c = w.shape
    cw = _tile(math.gcd(c, x_off) if x_off else c, (256, 128))
    ob = x_off // cw
    rc = _tile(t, (ROW_CHUNK,))

    def body(x_ref, w_ref, b_ref, o_ref, pad_ref):
        _fill_pad(pad_ref, x_ref, t)
        for r0 in range(0, t, rc):
            pre = _conv_taps(pad_ref, w_ref, r0, rc, kw) + b_ref[...]
            o_ref[pl.ds(r0, rc), :] = pre * _sigmoid(pre)

    strip = pl.BlockSpec((t, cw), lambda i: (0, i))
    return pl.pallas_call(
        body,
        grid=(c // cw,),
        in_specs=[pl.BlockSpec((t, cw), lambda i: (0, i + ob)), pl.BlockSpec((kw, cw), lambda i: (0, i)),
                  pl.BlockSpec((1, cw), lambda i: (0, i))],
        out_specs=strip,
        out_shape=jax.ShapeDtypeStruct((t, c), F32),
        scratch_shapes=[pltpu.VMEM((t + 2 * PAD, cw), F32)],
        compiler_params=_cparams(("parallel",)),
        name=name,
    )(x, w, b.reshape(1, c))


def _conv_bwd_core(dpre_pad_ref, x_pad_ref, w_ref, dx_ref, dw_ref, db_ref, t, rc, kw):
    cw = dx_ref.shape[1]

    def fold(a):
        return jnp.sum(a.reshape(rc // SUBLANES, SUBLANES, cw), axis=0) if rc % SUBLANES == 0 else jnp.sum(a, axis=0, keepdims=True)

    dws = [None] * kw
    dbs = None
    for r0 in range(0, t, rc):
        dpre = dpre_pad_ref[pl.ds(PAD + r0, rc), :]
        dx = None
        for j in range(kw):
            s = kw - 1 - j
            term = dpre_pad_ref[pl.ds(PAD + r0 + s, rc), :] * w_ref[j:j + 1, :]
            dx = term if dx is None else dx + term
            part = fold(dpre * _shifted(x_pad_ref, r0, rc, s))
            dws[j] = part if dws[j] is None else dws[j] + part
        part = fold(dpre)
        dbs = part if dbs is None else dbs + part
        dx_ref[pl.ds(r0, rc), :] = dx
    for j in range(kw):
        dw_ref[j:j + 1, :] = jnp.sum(dws[j], axis=0, keepdims=True)
    db_ref[...] = jnp.sum(dbs, axis=0, keepdims=True)


def _conv_silu_bwd(x, w, b, dact, *, x_off=0, into=None, name):
    t = x.shape[0]
    kw, c = w.shape
    cw = _tile(math.gcd(c, x_off) if x_off else c, (256, 128))
    ob = x_off // cw
    rc = _tile(t, (ROW_CHUNK,))

    def body(x_ref, w_ref, b_ref, da_ref, *rest):
        dx_ref, dw_ref, db_ref, xpad_ref, dpad_ref = rest[-5:]
        _fill_pad(xpad_ref, x_ref, t)
        dpad_ref[0:PAD, :] = jnp.zeros((PAD, cw), F32)
        dpad_ref[pl.ds(PAD + t, PAD), :] = jnp.zeros((PAD, cw), F32)
        for r0 in range(0, t, rc):
            pre = _conv_taps(xpad_ref, w_ref, r0, rc, kw) + b_ref[...]
            sg = _sigmoid(pre)
            dpad_ref[pl.ds(PAD + r0, rc), :] = da_ref[pl.ds(r0, rc), :] * (sg * (1.0 + pre * (1.0 - sg)))
        _conv_bwd_core(dpad_ref, xpad_ref, w_ref, dx_ref, dw_ref, db_ref, t, rc, kw)

    strip = pl.BlockSpec((t, cw), lambda i: (0, i))
    wspec = pl.BlockSpec((kw, cw), lambda i: (0, i))
    bspec = pl.BlockSpec((1, cw), lambda i: (0, i))
    xspec = pl.BlockSpec((t, cw), lambda i: (0, i + ob))
    extra = {} if into is None else dict(input_output_aliases={4: 0})
    return pl.pallas_call(
        body,
        grid=(c // cw,),
        in_specs=[xspec, wspec, bspec, strip] + ([] if into is None else [ANY]),
        out_specs=[strip if into is None else xspec, wspec, bspec],
        out_shape=[jax.ShapeDtypeStruct((t, c) if into is None else into.shape, F32), jax.ShapeDtypeStruct((kw, c), F32),
                   jax.ShapeDtypeStruct((1, c), F32)],
        scratch_shapes=[pltpu.VMEM((t + 2 * PAD, cw), F32), pltpu.VMEM((t + 2 * PAD, cw), F32)],
        compiler_params=_cparams(("parallel",)),
        name=name,
        **extra,
    )(x, w, b.reshape(1, c), dact, *([] if into is None else [into]))


def _conv_glu_fwd(hid, w, b, *, side=None, name):
    t, c2 = hid.shape
    f = c2 // 2
    kw = w.shape[0]
    cw = _tile(f, (256, 128))
    nf = f // cw
    rc = _tile(t, (ROW_CHUNK,))

    def body(g_ref, v_ref, wg_ref, wv_ref, bg_ref, bv_ref, o_ref, gpad_ref, vpad_ref):
        _fill_pad(gpad_ref, g_ref, t)
        _fill_pad(vpad_ref, v_ref, t)
        for r0 in range(0, t, rc):
            gate = _conv_taps(gpad_ref, wg_ref, r0, rc, kw) + bg_ref[...]
            val = _conv_taps(vpad_ref, wv_ref, r0, rc, kw) + bv_ref[...]
            o_ref[pl.ds(r0, rc), :] = (gate * _sigmoid(gate) * val).astype(o_ref.dtype)

    gs = pl.BlockSpec((t, cw), lambda i: (0, i))
    vs = pl.BlockSpec((t, cw), lambda i: (0, i + nf))
    b2 = b.reshape(1, c2)
    (act,), side_outs = _call(
        body,
        grid=(nf,),
        in_specs=[gs, vs, pl.BlockSpec((kw, cw), lambda i: (0, i)), pl.BlockSpec((kw, cw), lambda i: (0, i + nf)),
                  pl.BlockSpec((1, cw), lambda i: (0, i)), pl.BlockSpec((1, cw), lambda i: (0, i + nf))],
        out_specs=[gs],
        out_shape=[jax.ShapeDtypeStruct((t, f), BF16)],
        scratch_shapes=[pltpu.VMEM((t + 2 * PAD, cw), F32), pltpu.VMEM((t + 2 * PAD, cw), F32)],
        sem=("parallel",),
        name=name,
        args=(hid, hid, w, w, b2, b2),
        side=side,
    )
    return act, side_outs


def _conv_glu_bwd(hid, w, b, dact, *, name):
    t, c2 = hid.shape
    f = c2 // 2
    kw = w.shape[0]
    cw = _tile(f, (128,))
    nf = f // cw
    rc = _tile(t, (ROW_CHUNK,))

    def body(g_ref, v_ref, wg_ref, wv_ref, bg_ref, bv_ref, da_ref,
             dgv_ref, dwg_ref, dwv_ref, dbg_ref, dbv_ref,
             gpad_ref, vpad_ref, dgpad_ref, dvpad_ref):
        _fill_pad(gpad_ref, g_ref, t)
        _fill_pad(vpad_ref, v_ref, t)
        for ref in (dgpad_ref, dvpad_ref):
            ref[0:PAD, :] = jnp.zeros((PAD, cw), F32)
            ref[pl.ds(PAD + t, PAD), :] = jnp.zeros((PAD, cw), F32)
        for r0 in range(0, t, rc):
            gate = _conv_taps(gpad_ref, wg_ref, r0, rc, kw) + bg_ref[...]
            val = _conv_taps(vpad_ref, wv_ref, r0, rc, kw) + bv_ref[...]
            sg = _sigmoid(gate)
            da = da_ref[pl.ds(r0, rc), :].astype(F32)
            dgpad_ref[pl.ds(PAD + r0, rc), :] = da * val * (sg * (1.0 + gate * (1.0 - sg)))
            dvpad_ref[pl.ds(PAD + r0, rc), :] = da * (gate * sg)
        _conv_bwd_core(dgpad_ref, gpad_ref, wg_ref, dgv_ref.at[0], dwg_ref, dbg_ref, t, rc, kw)
        _conv_bwd_core(dvpad_ref, vpad_ref, wv_ref, dgv_ref.at[1], dwv_ref, dbv_ref, t, rc, kw)

    gs = pl.BlockSpec((t, cw), lambda i: (0, i))
    vs = pl.BlockSpec((t, cw), lambda i: (0, i + nf))
    wg = pl.BlockSpec((kw, cw), lambda i: (0, i))
    wv = pl.BlockSpec((kw, cw), lambda i: (0, i + nf))
    bg = pl.BlockSpec((1, cw), lambda i: (0, i))
    bv = pl.BlockSpec((1, cw), lambda i: (0, i + nf))
    b2 = b.reshape(1, c2)
    pad = pltpu.VMEM((t + 2 * PAD, cw), F32)
    return pl.pallas_call(
        body,
        grid=(nf,),
        in_specs=[gs, vs, wg, wv, bg, bv, gs],
        out_specs=[pl.BlockSpec((2, t, cw), lambda i: (0, 0, i)), wg, wg, bg, bg],
        out_shape=[jax.ShapeDtypeStruct((2, t, f), F32),
                   jax.ShapeDtypeStruct((kw, f), F32), jax.ShapeDtypeStruct((kw, f), F32),
                   jax.ShapeDtypeStruct((1, f), F32), jax.ShapeDtypeStruct((1, f), F32)],
        scratch_shapes=[pad, pad, pad, pad],
        compiler_params=_cparams(("parallel",)),
        name=name,
    )(hid, hid, w, w, b2, b2, dact)


def _gate_norm_fwd(y, zx, w, *, name):
    t, di = y.shape
    gsz = di // SSM_GROUPS
    tb = _tile(t, (256, 128))

    def body(y_ref, z_ref, w_ref, o_ref):
        for g in range(SSM_GROUPS):
            sl = slice(g * gsz, (g + 1) * gsz)
            zv = z_ref[:, sl]
            gv = y_ref[:, sl] * (zv * _sigmoid(zv))
            r = lax.rsqrt(jnp.mean(gv * gv, axis=-1, keepdims=True) + EPS)
            o_ref[:, sl] = (gv * r * w_ref[:, sl]).astype(o_ref.dtype)

    row = pl.BlockSpec((tb, di), lambda i: (i, 0))
    return pl.pallas_call(
        body,
        grid=(t // tb,),
        in_specs=[row, row, pl.BlockSpec((1, di), lambda i: (0, 0))],
        out_specs=row,
        out_shape=jax.ShapeDtypeStruct((t, di), BF16),
        compiler_params=_cparams(("parallel",)),
        name=name,
    )(y, zx, w.reshape(1, di))


def _gate_norm_bwd(y, zx, w, dyn, *, side=None, name):
    t, di = y.shape
    gsz = di // SSM_GROUPS
    tb = _tile(t, (256, 128))

    def body(y_ref, z_ref, w_ref, d_ref, dy_ref, dz_ref, dw_ref):
        i = pl.program_id(0)
        for g in range(SSM_GROUPS):
            sl = slice(g * gsz, (g + 1) * gsz)
            zv = z_ref[:, sl]
            yv = y_ref[:, sl]
            sg = _sigmoid(zv)
            sz = zv * sg
            gv = yv * sz
            r = lax.rsqrt(jnp.mean(gv * gv, axis=-1, keepdims=True) + EPS)
            gn = gv * r
            dn = d_ref[:, sl].astype(F32)
            q = dn * w_ref[:, sl]
            dg = r * (q - gn * jnp.mean(q * gn, axis=-1, keepdims=True))
            dy_ref[:, sl] = dg * sz
            dz_ref[:, sl] = dg * yv * (sg * (1.0 + zv * (1.0 - sg)))
            dwp = jnp.sum(dn * gn, axis=0, keepdims=True)

            @pl.when(i == 0)
            def _(sl=sl, dwp=dwp):
                dw_ref[:, sl] = dwp

            @pl.when(i > 0)
            def _(sl=sl, dwp=dwp):
                dw_ref[:, sl] += dwp

    row = pl.BlockSpec((tb, di), lambda i: (i, 0))
    vec = pl.BlockSpec((1, di), lambda i: (0, 0))
    outs, side_outs = _call(
        body,
        grid=(t // tb,),
        in_specs=[row, row, vec, row],
        out_specs=[row, row, vec],
        out_shape=[jax.ShapeDtypeStruct((t, di), F32), jax.ShapeDtypeStruct((t, zx.shape[1]), F32),
                   jax.ShapeDtypeStruct((1, di), F32)],
        sem=("arbitrary",),
        name=name,
        args=(y, zx, w.reshape(1, di), dyn),
        side=side,
    )
    return (*outs, side_outs)


def _adamw(w, g, m, v, *, name):
    shape = w.shape
    cols = shape[-1]
    rows = w.size // cols
    w2, g2, m2, v2 = (a.reshape(rows, cols) for a in (w, g, m, v))
    tr = rows
    if rows * cols * 4 > ADAM_BLOCK_BYTES:
        tr = _tile(rows, tuple(r for r in (512, 256, 128, 64, 32, 16, 8) if r * cols * 4 <= ADAM_BLOCK_BYTES))
    c1 = 1.0 - ADAM_B1 ** ADAM_STEP
    c2 = 1.0 - ADAM_B2 ** ADAM_STEP

    def body(w_ref, g_ref, m_ref, v_ref, d_ref, nm_ref, nv_ref):
        gv = g_ref[...]
        nm = ADAM_B1 * m_ref[...] + (1.0 - ADAM_B1) * gv
        nv = ADAM_B2 * v_ref[...] + (1.0 - ADAM_B2) * (gv * gv)
        d_ref[...] = -ADAM_LR * ((nm / c1) / (jnp.sqrt(nv / c2) + ADAM_EPS) + ADAM_WD * w_ref[...])
        nm_ref[...] = nm
        nv_ref[...] = nv

    blk = pl.BlockSpec((tr, cols), lambda i: (i, 0))
    outs = pl.pallas_call(
        body,
        grid=(rows // tr,),
        in_specs=[blk] * 4,
        out_specs=[blk] * 3,
        out_shape=[jax.ShapeDtypeStruct((rows, cols), F32)] * 3,
        compiler_params=_cparams(("parallel",)),
        name=name,
    )(w2, g2, m2, v2)
    return tuple(o.reshape(shape) for o in outs)


def _adamw_layers(w, gs, m, v, *, name):
    n_l, rows, cols = w.shape
    assert len(gs) == n_l
    tr = _tile(rows, tuple(r for r in (512, 256, 128, 64, 32, 16, 8) if r * cols * 4 <= ADAM_BLOCK_BYTES))
    c1 = 1.0 - ADAM_B1 ** ADAM_STEP
    c2 = 1.0 - ADAM_B2 ** ADAM_STEP

    def body(*refs):
        w_ref, m_ref, v_ref = refs[:3]
        g_refs = refs[3:3 + n_l]
        g_ref, d_ref, nm_ref, nv_ref = refs[3 + n_l:]
        layer = pl.program_id(0)
        gv = g_refs[0][...]
        for q in range(1, n_l):
            gv = jnp.where(layer == q, g_refs[q][...], gv)
        nm = ADAM_B1 * m_ref[...] + (1.0 - ADAM_B1) * gv
        nv = ADAM_B2 * v_ref[...] + (1.0 - ADAM_B2) * (gv * gv)
        g_ref[...] = gv
        d_ref[...] = -ADAM_LR * ((nm / c1) / (jnp.sqrt(nv / c2) + ADAM_EPS) + ADAM_WD * w_ref[...])
        nm_ref[...] = nm
        nv_ref[...] = nv

    stacked = pl.BlockSpec((None, tr, cols), lambda l, i: (l, i, 0))
    single = pl.BlockSpec((tr, cols), lambda l, i: (i, 0))
    return pl.pallas_call(
        body,
        grid=(n_l, rows // tr),
        in_specs=[stacked] * 3 + [single] * n_l,
        out_specs=[stacked] * 4,
        out_shape=[jax.ShapeDtypeStruct(w.shape, F32)] * 4,
        compiler_params=_cparams(("parallel", "parallel")),
        name=name,
    )(w, m, v, *gs)


def _ssd_scalars(dtc_ref, dtr_ref, hpc_ref, hpr_ref, ln):
    assert SSM_CHUNK == SSM_STATE == LANES, "the SSD kernels mix chunk, state and lane-wide tiles freely"
    bias_c, alog_c = hpc_ref[0, 0:1, :], hpc_ref[0, 1:2, :]
    bias_r, alog_r = hpr_ref[0, :, 0:1], hpr_ref[0, :, 1:2]
    a_c, a_r = -jnp.exp(alog_c), -jnp.exp(alog_r)
    raw_c = dtc_ref[0] + bias_c
    dt_c = _softplus(raw_c)
    dt_r = _softplus(dtr_ref[0] + bias_r)
    row = lax.broadcasted_iota(jnp.int32, (ln, ln), 0)
    col = lax.broadcasted_iota(jnp.int32, (ln, ln), 1)
    lower = (col <= row).astype(F32)
    upper = (row <= col).astype(F32)
    acs_c = _ones_dot(lower, dt_c * a_c, ones_left=True)
    acs_r = _ones_dot(upper, dt_r * a_r, ones_left=False)
    return raw_c, dt_c, a_c, acs_c, acs_r, row, col


def _ssd_specs(t, di, g_n, n_st, rp, ln, r_h, rev):
    nc = t // ln
    cidx = (lambda c: nc - 1 - c) if rev else (lambda c: c)
    xs = pl.BlockSpec((ln, rp), lambda g, c: (cidx(c), g))
    bm = pl.BlockSpec((ln, n_st), lambda g, c: (cidx(c), di // n_st + g))
    cm = pl.BlockSpec((ln, n_st), lambda g, c: (cidx(c), di // n_st + g_n + g))
    dtc = pl.BlockSpec((1, ln, r_h), lambda g, c: (g, cidx(c), 0))
    dtr = pl.BlockSpec((1, r_h, ln), lambda g, c: (g, 0, cidx(c)))
    hpc = pl.BlockSpec((1, 3, r_h), lambda g, c: (g, 0, 0))
    hpr = pl.BlockSpec((1, r_h, 3), lambda g, c: (g, 0, 0))
    prev = pl.BlockSpec((1, rp, n_st), lambda g, c: (cidx(c), g, 0))
    return xs, bm, cm, dtc, dtr, hpc, hpr, prev


def _ssd_fwd(xbc, dtc, dtr, hpc, hpr, *, side=None, name):
    t = xbc.shape[0]
    di, g_n, n_st, p_h, ln = D_INNER, SSM_GROUPS, SSM_STATE, SSM_HEAD_DIM, SSM_CHUNK
    r_h = SSM_HEADS // g_n
    rp = r_h * p_h
    nc = t // ln

    def body(xs_ref, b_ref, c_ref, dtc_ref, dtr_ref, hpc_ref, hpr_ref, y_ref, prev_ref, st_ref):
        @pl.when(pl.program_id(1) == 0)
        def _():
            st_ref[...] = jnp.zeros_like(st_ref)

        _, dt_c, _, acs_c, acs_r, row, col = _ssd_scalars(dtc_ref, dtr_ref, hpc_ref, hpr_ref, ln)
        bm = b_ref[...]
        cm = c_ref[...]
        cm16 = cm.astype(BF16)
        cb = _nt(cm16, bm.astype(BF16))
        causal = row >= col
        for r in range(r_h):
            sl = slice(r * p_h, (r + 1) * p_h)
            xs = xs_ref[:, sl]
            acs = jnp.broadcast_to(acs_c[:, r:r + 1], (ln, ln))
            last = acs[ln - 1:ln, :]
            lm = jnp.where(causal, jnp.exp(acs - acs_r[r:r + 1, :]), 0.0)
            xd = (xs * jnp.broadcast_to(dt_c[:, r:r + 1], (ln, p_h))).astype(BF16)
            prev = st_ref[sl, :]
            y = _nn((cb * lm).astype(BF16), xd)
            y = y + _nt(cm16, prev.astype(BF16)) * jnp.exp(acs[:, :p_h])
            y_ref[:, sl] = y + hpc_ref[0, 2:3, r:r + 1] * xs
            prev_ref[0, sl, :] = prev
            bd = (bm * jnp.exp(last - acs[:, :n_st])).astype(BF16)
            st_ref[sl, :] = prev * jnp.exp(last[:, :n_st]) + _tn(xd, bd)

    xs, bm, cm, dtcs, dtrs, hpcs, hprs, prev = _ssd_specs(t, di, g_n, n_st, rp, ln, r_h, False)
    (y, prev_out), side_outs = _call(
        body,
        grid=(g_n, nc),
        in_specs=[xs, bm, cm, dtcs, dtrs, hpcs, hprs],
        out_specs=[xs, prev],
        out_shape=[jax.ShapeDtypeStruct((t, di), F32), jax.ShapeDtypeStruct((nc, g_n * rp, n_st), F32)],
        scratch_shapes=[pltpu.VMEM((rp, n_st), F32)],
        sem=("parallel", "arbitrary"),
        name=name,
        args=(xbc, xbc, xbc, dtc, dtr, hpc, hpr),
        side=side,
    )
    return y, prev_out, side_outs


def _ssd_bwd(xbc, dtc, dtr, hpc, hpr, prev, dy, *, side=None, name):
    t = xbc.shape[0]
    di, g_n, n_st, p_h, ln = D_INNER, SSM_GROUPS, SSM_STATE, SSM_HEAD_DIM, SSM_CHUNK
    r_h = SSM_HEADS // g_n
    rp = r_h * p_h
    nc = t // ln

    def body(xs_ref, b_ref, c_ref, dtc_ref, dtr_ref, hpc_ref, hpr_ref, prev_ref, dy_ref,
             dxs_ref, db_ref, dc_ref, ddt_ref, hg_ref, ds_ref):
        step = pl.program_id(1)

        @pl.when(step == 0)
        def _():
            ds_ref[...] = jnp.zeros_like(ds_ref)

        raw_c, dt_c, a_c, acs_c, acs_r, row, col = _ssd_scalars(dtc_ref, dtr_ref, hpc_ref, hpr_ref, ln)
        bm = b_ref[...]
        cm = c_ref[...]
        bm16, cm16 = bm.astype(BF16), cm.astype(BF16)
        cb = _nt(cm16, bm16)
        cbt = _nt(bm16, cm16)
        lane_r = lax.broadcasted_iota(jnp.int32, (ln, r_h), 1)
        dacs_all = jnp.zeros((ln, r_h), F32)
        ddtx_all = jnp.zeros((ln, r_h), F32)
        dd_all = jnp.zeros((ln, r_h), F32)
        dcb = jnp.zeros((ln, ln), F32)
        dcbt = jnp.zeros((ln, ln), F32)
        dc_acc = jnp.zeros((ln, n_st), F32)
        db_acc = jnp.zeros((ln, n_st), F32)
        for r in range(r_h):
            sl = slice(r * p_h, (r + 1) * p_h)
            xs = xs_ref[:, sl]
            dyv = dy_ref[:, sl]
            dy16 = dyv.astype(BF16)
            acs = jnp.broadcast_to(acs_c[:, r:r + 1], (ln, ln))
            dtv = jnp.broadcast_to(dt_c[:, r:r + 1], (ln, p_h))
            acsr = acs_r[r:r + 1, :]
            last = acs[ln - 1:ln, :]
            xd = xs * dtv
            xd16 = xd.astype(BF16)
            lm = jnp.where(row >= col, jnp.exp(acs - acsr), 0.0)
            lmt = jnp.where(col >= row, jnp.exp(acsr - acs), 0.0)
            m_ls = cb * lm
            m_sl = cbt * lmt
            dm = _nt(dy16, xd16)
            dmt = _nt(xd16, dy16)
            dxd = _nn(m_sl.astype(BF16), dy16)
            dacs = _row_sums(dm * m_ls - dmt * m_sl)
            dcb = dcb + dm * lm
            dcbt = dcbt + dmt * lmt
            prev = prev_ref[0, sl, :]
            prev16 = prev.astype(BF16)
            e = jnp.exp(acs[:, :p_h])
            y_off = _nt(cm16, prev16) * e
            dacs = dacs + _row_sums(dyv * y_off)
            dyo16 = (dyv * e).astype(BF16)
            dc_acc = dc_acc + _nn(dyo16, prev16)
            dprev = _tn(dyo16, cm16)
            ds = ds_ref[sl, :]
            ds16 = ds.astype(BF16)
            decay = jnp.exp(last - acs)[:, :n_st]
            bd16 = (bm * decay).astype(BF16)
            dbd = _nn(xd16, ds16)
            dxd = dxd + _nt(bd16, ds16)
            db_acc = db_acc + dbd * decay
            tdec = _row_sums(dbd * bm, 2) * decay
            cd = jnp.exp(last)
            dlast = (jnp.sum(tdec, axis=0, keepdims=True)
                     + jnp.sum(_row_sums(prev * ds, 2), axis=0, keepdims=True) * cd)
            ds_ref[sl, :] = dprev + cd[:, :n_st] * ds
            dskip = hpc_ref[0, 2:3, r:r + 1]
            dxs_ref[:, sl] = dxd * dtv + dskip * dyv
            dacs = dacs - tdec + jnp.where(row == ln - 1, dlast, 0.0)
            dacs_all = jnp.where(lane_r == r, dacs[:, :r_h], dacs_all)
            ddtx_all = jnp.where(lane_r == r, _row_sums(dxd * xs, 2)[:, :r_h], ddtx_all)
            dd_all = jnp.where(lane_r == r, _row_sums(dyv * xs, 2)[:, :r_h], dd_all)
        dc_ref[...] = dc_acc + _nn(dcb.astype(BF16), bm16)
        db_ref[...] = db_acc + _nn(dcbt.astype(BF16), cm16)
        upper = (row <= col).astype(F32)
        dad = _ones_dot(upper, dacs_all, ones_left=True)
        ddt = dad * a_c + ddtx_all
        ddt_raw = ddt * _sigmoid(raw_c)
        ddt_ref[0] = ddt_raw
        d_bias = jnp.sum(ddt_raw, axis=0, keepdims=True)
        d_alog = jnp.sum(dad * dt_c, axis=0, keepdims=True) * a_c
        d_d = jnp.sum(dd_all, axis=0, keepdims=True)
        hg = jnp.concatenate([d_bias, d_alog, d_d], axis=0)

        @pl.when(step == 0)
        def _():
            hg_ref[0] = hg

        @pl.when(step > 0)
        def _():
            hg_ref[0] += hg

    xs, bms, cms, dtcs, dtrs, hpcs, hprs, prevs = _ssd_specs(t, di, g_n, n_st, rp, ln, r_h, True)
    bout = pl.BlockSpec((ln, n_st), lambda g, c: (nc - 1 - c, g))
    outs, side_outs = _call(
        body,
        grid=(g_n, nc),
        in_specs=[xs, bms, cms, dtcs, dtrs, hpcs, hprs, prevs, xs],
        out_specs=[xs, bout, bout, dtcs, hpcs],
        out_shape=[jax.ShapeDtypeStruct((t, di), F32), jax.ShapeDtypeStruct((t, g_n * n_st), F32),
                   jax.ShapeDtypeStruct((t, g_n * n_st), F32), jax.ShapeDtypeStruct((g_n, t, r_h), F32),
                   jax.ShapeDtypeStruct((g_n, 3, r_h), F32)],
        scratch_shapes=[pltpu.VMEM((rp, n_st), F32)],
        sem=("parallel", "arbitrary"),
        name=name,
        args=(xbc, xbc, xbc, dtc, dtr, hpc, hpr, prev, dy),
        side=side,
    )
    return (*outs, side_outs)


SB_KEYS = 256
SB_QUERIES = (512, 256)
SB_CUTOFF = 110.0
SB_PIECES = 2


def _sb_logits(qs, kv, valid):
    z = _nt(qs, kv)
    nz = -z
    lg = jnp.minimum(nz, 0.0) - jnp.log(1.0 + jnp.exp(jnp.minimum(z, nz)))
    return z + lg, (lg if valid is None else jnp.where(valid, lg, 0.0))


def _sb_iota(tq):
    diff = lax.broadcasted_iota(jnp.int32, (tq, SB_KEYS), 1) - lax.broadcasted_iota(jnp.int32, (tq, SB_KEYS), 0)
    krow = lax.broadcasted_iota(jnp.int32, (SB_KEYS, SB_KEYS), 0)
    kcol = lax.broadcasted_iota(jnp.int32, (SB_KEYS, SB_KEYS), 1)
    return diff, krow, kcol


def _sb_scale(d):
    scale = 1.0 / math.sqrt(d)
    assert math.frexp(scale)[0] == 0.5, "the scale is folded into bf16 queries: it must be a power of two"
    return scale


def _key_rows(j):
    return pl.ds(pl.multiple_of(j * SB_KEYS, SB_KEYS), SB_KEYS)


def _sb_fwd(q, k, v, *, side=None, name):
    h, t, d = q.shape
    tq = _tile(t, SB_QUERIES)
    nq = t // tq
    kpq = tq // SB_KEYS
    scale = _sb_scale(d)

    def body(q_ref, k_ref, v_ref, o_ref, lt_ref, first_ref):
        i = pl.program_id(1)
        qs = (q_ref[0].astype(F32) * scale).astype(BF16)
        diff, krow, kcol = _sb_iota(tq)
        later = (krow > kcol).astype(F32)

        def block(j, carry, valid, r0=0):
            acc, cl = carry
            rows = _key_rows(j)
            ls, lg = _sb_logits(qs[r0:], k_ref[0, rows, :], valid)
            cs = _ones_dot(later, lg, ones_left=False, pieces=SB_PIECES)
            att = jnp.exp(ls + (cs + cl))
            if valid is not None:
                att = jnp.where(valid, att, 0.0)
            acc = acc + _nn(att.astype(BF16), v_ref[0, rows, :])
            return acc, cl + (cs[:, 0:1] + lg[:, 0:1])

        acc, cl = jnp.zeros((tq, d), F32), jnp.zeros((tq, 1), F32)
        for m in range(kpq - 1, -1, -1):
            r0 = m * SB_KEYS
            sub = block(i * kpq + m, (acc[r0:], cl[r0:]), diff[r0:] < -r0, r0)
            acc = jnp.concatenate([acc[:r0], sub[0]], axis=0) if r0 else sub[0]
            cl = jnp.concatenate([cl[:r0], sub[1]], axis=0) if r0 else sub[1]
        carry = (acc, cl)
        nb = i * kpq

        def more(st):
            s, _, cl = st
            return jnp.logical_and(s < nb, jnp.max(cl) > -SB_CUTOFF)

        def step(st):
            s, acc, cl = st
            acc, cl = block(nb - 1 - s, (acc, cl), None)
            return s + 1, acc, cl

        walked, acc, cl = lax.while_loop(more, step, (jnp.int32(0),) + carry)
        o_ref[0] = acc.astype(o_ref.dtype)
        lt_ref[0] = cl
        first_ref[pl.program_id(0), i] = nb - walked

    qs = pl.BlockSpec((1, tq, d), lambda hh, i: (hh, i, 0))
    ls = pl.BlockSpec((1, tq, 1), lambda hh, i: (hh, i, 0))
    ks = pl.BlockSpec((1, t, d), lambda hh, i: (hh, 0, 0))
    outs, side_outs = _call(
        body,
        grid=(h, nq),
        in_specs=[qs, ks, ks],
        out_specs=[qs, ls, pl.BlockSpec(memory_space=pltpu.SMEM)],
        out_shape=[jax.ShapeDtypeStruct((h, t, d), BF16), jax.ShapeDtypeStruct((h, t, 1), F32),
                   jax.ShapeDtypeStruct((h, nq), jnp.int32)],
        sem=("arbitrary", "arbitrary"),
        name=name,
        args=(q, k, v),
        side=side,
    )
    return (*outs, side_outs)


def _sb_bwd(q, k, v, lt, first, do, *, name):
    h, t, d = q.shape
    tq = _tile(t, SB_QUERIES)
    nq = t // tq
    kpq = tq // SB_KEYS
    scale = _sb_scale(d)
    last = SB_KEYS - 1

    def body(q_ref, k_ref, v_ref, lt_ref, first_ref, do_ref, dq_ref, dk_ref, dv_ref, dk_acc, dv_acc):
        i = pl.program_id(1)

        @pl.when(i == 0)
        def _():
            dk_acc[...] = jnp.zeros_like(dk_acc)
            dv_acc[...] = jnp.zeros_like(dv_acc)

        qs = (q_ref[0].astype(F32) * scale).astype(BF16)
        do16 = do_ref[0].astype(BF16)
        ltot = lt_ref[0]
        diff, krow, kcol = _sb_iota(tq)
        upto = (krow <= kcol).astype(F32)
        before = (krow < kcol).astype(F32)

        def block(j, carry, valid, r0=0):
            dq, pl_sum, pg_sum = carry
            rows = _key_rows(j)
            kv = k_ref[0, rows, :]
            vv = v_ref[0, rows, :]
            ls, lg = _sb_logits(qs[r0:], kv, valid)
            pre = _ones_dot(upto, lg, ones_left=False, pieces=SB_PIECES)
            att = jnp.exp(ls + (ltot[r0:] - (pre + pl_sum)))
            if valid is not None:
                att = jnp.where(valid, att, 0.0)
            g = att * _nt(do16[r0:], vv)
            gpre = _ones_dot(before, g, ones_left=False, pieces=SB_PIECES)
            sig = jnp.exp(ls)
            dz16 = (g - sig * (g + (gpre + pg_sum))).astype(BF16)
            if valid is not None:
                dz16 = jnp.where(valid, dz16, jnp.zeros_like(dz16))
            dq = dq + _nn(dz16, kv)
            dk_acc[rows, :] += _tn(dz16, qs[r0:])
            dv_acc[rows, :] += _tn(att.astype(BF16), do16[r0:])
            return dq, pl_sum + pre[:, last:], pg_sum + (gpre[:, last:] + g[:, last:])

        zero = jnp.zeros((tq, 1), F32)
        nb = i * kpq
        start = jnp.clip(first_ref[pl.program_id(0), i], 0, nb)
        carry = lax.fori_loop(start, nb, lambda j, cr: block(j, cr, None), (jnp.zeros((tq, d), F32), zero, zero))
        for m in range(kpq):
            r0 = m * SB_KEYS
            sub = block(nb + m, tuple(a[r0:] for a in carry), diff[r0:] < -r0, r0)
            carry = tuple(jnp.concatenate([a[:r0], s], axis=0) if r0 else s for a, s in zip(carry, sub))
        dq_ref[0] = (carry[0] * scale).astype(dq_ref.dtype)

        @pl.when(i == nq - 1)
        def _():
            dk_ref[0] = dk_acc[...].astype(dk_ref.dtype)
            dv_ref[0] = dv_acc[...].astype(dv_ref.dtype)

    qs = pl.BlockSpec((1, tq, d), lambda hh, i: (hh, i, 0))
    ls = pl.BlockSpec((1, tq, 1), lambda hh, i: (hh, i, 0))
    ks = pl.BlockSpec((1, t, d), lambda hh, i: (hh, 0, 0))
    full = jax.ShapeDtypeStruct((h, t, d), BF16)
    return pl.pallas_call(
        body,
        grid=(h, nq),
        in_specs=[qs, ks, ks, ls, pl.BlockSpec(memory_space=pltpu.SMEM), qs],
        out_specs=[qs, ks, ks],
        out_shape=[full, full, full],
        scratch_shapes=[pltpu.VMEM((t, d), F32), pltpu.VMEM((t, d), F32)],
        compiler_params=_cparams(("arbitrary", "arbitrary")),
        name=name,
    )(q, k, v, lt, first, do)


def _row_tile(rows, cols):
    return _tile(rows, tuple(r for r in (2048, 1024, 512, 256, 128, 64, 32, 16, 8) if r * cols * 4 <= ADAM_BLOCK_BYTES))


def _sum_leading(x, *, name):
    n, rows, cols = x.shape
    tr = _row_tile(rows, cols)

    def body(x_ref, o_ref):
        acc = x_ref[0].astype(F32)
        for q in range(1, n):
            acc = acc + x_ref[q].astype(F32)
        o_ref[...] = acc

    return pl.pallas_call(
        body,
        grid=(rows // tr,),
        in_specs=[pl.BlockSpec((n, tr, cols), lambda i: (0, i, 0))],
        out_specs=pl.BlockSpec((tr, cols), lambda i: (i, 0)),
        out_shape=jax.ShapeDtypeStruct((rows, cols), F32),
        compiler_params=_cparams(("parallel",)),
        name=name,
    )(x)


def _pair_add(g4h, recv, c, *, out_dtype, name):
    n, _, rows, cols = g4h.shape
    tr = _row_tile(rows, cols)

    def body(c_ref, g_ref, r_ref, o_ref):
        o_ref[...] = (g_ref[...] + r_ref[...]).astype(o_ref.dtype)

    blk = pl.BlockSpec((1, tr, cols), lambda q, i, c_ref: (q, i, 0))
    return pl.pallas_call(
        body,
        grid_spec=pltpu.PrefetchScalarGridSpec(
            num_scalar_prefetch=1,
            grid=(n, rows // tr),
            in_specs=[pl.BlockSpec((1, None, tr, cols), lambda q, i, c_ref: (q, c_ref[0], i, 0)), blk],
            out_specs=blk),
        out_shape=jax.ShapeDtypeStruct((n, rows, cols), out_dtype),
        compiler_params=_cparams(("parallel", "parallel")),
        name=name,
    )(c.reshape(1).astype(jnp.int32), g4h, recv)


ANY = pl.BlockSpec(memory_space=pl.ANY)


def _other_chips(x, y):
    return [(1 - x, y), (x, 1 - y), (1 - x, 1 - y)]


def _gather_chips(shard, *, name):
    def body(x_ref, o_ref, send_sems, recv_sems, local_sem):
        x, y, c = lax.axis_index("x"), lax.axis_index("y"), lax.axis_index("c")
        me = 2 * x + y
        mine = pltpu.make_async_copy(x_ref, o_ref.at[me], local_sem)
        mine.start()
        chips = _other_chips(x, y)
        sends = [pltpu.make_async_remote_copy(src_ref=x_ref, dst_ref=o_ref.at[me], send_sem=send_sems.at[q],
                                              recv_sem=recv_sems.at[q], device_id=(px, py, c), device_id_type=MESH)
                 for q, (px, py) in enumerate(chips)]
        for cp in sends:
            cp.start()
        for q, (px, py) in enumerate(chips):
            pltpu.make_async_remote_copy(src_ref=x_ref, dst_ref=o_ref.at[2 * px + py], send_sem=send_sems.at[q],
                                         recv_sem=recv_sems.at[q], device_id=(px, py, c), device_id_type=MESH).wait_recv()
        for cp in sends:
            cp.wait_send()
        mine.wait()

    return pl.pallas_call(
        body,
        in_specs=[ANY],
        out_specs=ANY,
        out_shape=jax.ShapeDtypeStruct((4,) + shard.shape, shard.dtype),
        scratch_shapes=[pltpu.SemaphoreType.DMA((3,)), pltpu.SemaphoreType.DMA((3,)), pltpu.SemaphoreType.DMA],
        compiler_params=pltpu.CompilerParams(has_side_effects=True),
        name=name,
    )(shard)


def _comm_call(body, ins, out_shapes, n_sems, name):
    n = len(ins)

    def wrapped(*refs):
        body(refs[:n], refs[n:n + len(out_shapes)], refs[-2], refs[-1])

    return pl.pallas_call(
        wrapped,
        in_specs=[ANY] * n,
        out_specs=[ANY] * len(out_shapes),
        out_shape=out_shapes,
        scratch_shapes=[pltpu.SemaphoreType.DMA((n_sems,)), pltpu.SemaphoreType.DMA((n_sems,))],
        compiler_params=pltpu.CompilerParams(has_side_effects=True),
        name=name,
    )(*ins)


def _remote(send_sems, recv_sems, q, src, dst, to):
    return pltpu.make_async_remote_copy(src_ref=src, dst_ref=dst, send_sem=send_sems.at[q], recv_sem=recv_sems.at[q],
                                        device_id=to, device_id_type=MESH)


def _scatter_chips(parts, *, name):
    return _run_job(_scatter_job(parts), name)


def _scatter_job(parts):
    def sends(ins, outs, send_sems, recv_sems):
        x, y, c = lax.axis_index("x"), lax.axis_index("y"), lax.axis_index("c")
        return [_remote(send_sems, recv_sems, 3 * i + q, p.at[2 * px + py], o.at[2 * x + y], (px, py, c))
                for i, (p, o) in enumerate(zip(ins, outs)) for q, (px, py) in enumerate(_other_chips(x, y))]

    def start(ins, outs, send_sems, recv_sems):
        for cp in sends(ins, outs, send_sems, recv_sems):
            cp.start()

    def finish(ins, outs, send_sems, recv_sems):
        x, y, c = lax.axis_index("x"), lax.axis_index("y"), lax.axis_index("c")
        for i, (p, o) in enumerate(zip(ins, outs)):
            for q, (px, py) in enumerate(_other_chips(x, y)):
                _remote(send_sems, recv_sems, 3 * i + q, p.at[2 * x + y], o.at[2 * px + py], (px, py, c)).wait_recv()
        for cp in sends(ins, outs, send_sems, recv_sems):
            cp.wait_send()

    return _SideJob(parts, [jax.ShapeDtypeStruct(p.shape, p.dtype) for p in parts], 3 * len(parts), start, finish)


def _run_job(job, name):
    return _comm_call(lambda *refs: (job.start(*refs), job.finish(*refs)), job.ins, job.out_shapes, job.n_sems, name)


def _gather_job(shards):
    def sends(ins, outs, send_sems, recv_sems):
        x, y, c = lax.axis_index("x"), lax.axis_index("y"), lax.axis_index("c")
        return [_remote(send_sems, recv_sems, 6 * i + q, s.at[c], o.at[2 * x + y, c], (px, py, c))
                for i, (s, o) in enumerate(zip(ins, outs)) for q, (px, py) in enumerate(_other_chips(x, y))]

    def start(ins, outs, send_sems, recv_sems):
        for cp in sends(ins, outs, send_sems, recv_sems):
            cp.start()

    def finish(ins, outs, send_sems, recv_sems):
        x, y, c = lax.axis_index("x"), lax.axis_index("y"), lax.axis_index("c")
        sibling = (x, y, 1 - c)
        chips = _other_chips(x, y)
        copy = lambda q, src, dst, to: _remote(send_sems, recv_sems, q, src, dst, to)
        passed = []
        for i, (s, o) in enumerate(zip(ins, outs)):
            for q, (px, py) in enumerate(chips):
                slot = o.at[2 * px + py, c]
                copy(6 * i + q, s.at[c], slot, (px, py, c)).wait_recv()
                passed.append(copy(6 * i + 3 + q, slot, slot, sibling))
                passed[-1].start()
        for i, (s, o) in enumerate(zip(ins, outs)):
            for q, (px, py) in enumerate(chips):
                copy(6 * i + 3 + q, s.at[1 - c], o.at[2 * px + py, 1 - c], sibling).wait_recv()
        for cp in sends(ins, outs, send_sems, recv_sems) + passed:
            cp.wait_send()

    return _SideJob(shards, [jax.ShapeDtypeStruct((N_CHIPS,) + s.shape, s.dtype) for s in shards], 6 * len(shards),
                    start, finish)


def _swap_job(gs):
    def copies(ins, outs, send_sems, recv_sems):
        x, y, c = lax.axis_index("x"), lax.axis_index("y"), lax.axis_index("c")
        return [_remote(send_sems, recv_sems, i, g.at[pl.ds(0, g.shape[0]), 1 - c], o, (x, y, 1 - c))
                for i, (g, o) in enumerate(zip(ins, outs))]

    def start(*refs):
        for cp in copies(*refs):
            cp.start()

    def finish(*refs):
        for cp in copies(*refs):
            cp.wait()

    return _SideJob(gs, [jax.ShapeDtypeStruct((g.shape[0],) + g.shape[2:], g.dtype) for g in gs], len(gs), start, finish)


def _join_halves(halves, *, name):
    def body(ins, outs, send_sems, recv_sems):
        x, y, c = lax.axis_index("x"), lax.axis_index("y"), lax.axis_index("c")
        sibling = (x, y, 1 - c)
        sends = [_remote(send_sems, recv_sems, i, h, o.at[c], sibling) for i, (h, o) in enumerate(zip(ins, outs))]
        for cp in sends:
            cp.start()
        for i, (h, o) in enumerate(zip(ins, outs)):
            _remote(send_sems, recv_sems, i, h, o.at[1 - c], sibling).wait_recv()
        for cp in sends:
            cp.wait_send()

    return _comm_call(body, halves, [jax.ShapeDtypeStruct((2,) + h.shape, h.dtype) for h in halves], len(halves), name)


def _gather_all(v, *, name):
    def body(v_ref, o_ref, send_sems, recv_sems, local_sem):
        x, y, c = lax.axis_index("x"), lax.axis_index("y"), lax.axis_index("c")
        me = 4 * x + 2 * y + c
        mine = pltpu.make_async_copy(v_ref, o_ref.at[me], local_sem)
        mine.start()
        peers = [(x ^ (q >> 2 & 1), y ^ (q >> 1 & 1), c ^ (q & 1)) for q in range(1, 8)]
        sends = [pltpu.make_async_remote_copy(src_ref=v_ref, dst_ref=o_ref.at[me], send_sem=send_sems.at[q],
                                              recv_sem=recv_sems.at[q], device_id=peer, device_id_type=MESH)
                 for q, peer in enumerate(peers)]
        for cp in sends:
            cp.start()
        for q, (px, py, pc) in enumerate(peers):
            pltpu.make_async_remote_copy(src_ref=v_ref, dst_ref=o_ref.at[4 * px + 2 * py + pc], send_sem=send_sems.at[q],
                                         recv_sem=recv_sems.at[q], device_id=(px, py, pc), device_id_type=MESH).wait_recv()
        for cp in sends:
            cp.wait_send()
        mine.wait()

    return pl.pallas_call(
        body,
        in_specs=[ANY],
        out_specs=ANY,
        out_shape=jax.ShapeDtypeStruct((8,) + v.shape, v.dtype),
        scratch_shapes=[pltpu.SemaphoreType.DMA((7,)), pltpu.SemaphoreType.DMA((7,)), pltpu.SemaphoreType.DMA],
        compiler_params=pltpu.CompilerParams(has_side_effects=True),
        name=name,
    )(v)


WEIGHTS = ['ssm_norm_w', 'ssm_in_w', 'ssm_conv_w', 'ssm_conv_b', 'ssm_dt_bias', 'ssm_a_log', 'ssm_d',
           'ssm_gate_norm_w', 'ssm_out_w', 'kv_norm_w', 'w_k', 'w_v', 'attn_norm_w', 'w_q', 'w_o',
           'ffn_norm_w', 'ffn_up_w', 'ffn_conv_w', 'ffn_conv_b', 'ffn_down_w', 'final_norm_w']
SHARD_AXIS = {'ssm_norm_w': 1, 'ssm_in_w': 2, 'ssm_conv_w': 2, 'ssm_conv_b': 1, 'ssm_gate_norm_w': 1,
              'ssm_out_w': 1, 'w_k': 0, 'w_v': 0, 'w_q': 1, 'w_o': 1, 'ffn_up_w': 2, 'ffn_conv_w': 2,
              'ffn_down_w': 1}
BIG = ['ssm_in_w', 'ssm_out_w', 'w_k', 'w_v', 'w_q', 'w_o', 'ffn_up_w', 'ffn_down_w']
SMALL = [n for n in WEIGHTS if n in SHARD_AXIS and n not in BIG]
REPLICATED = [n for n in WEIGHTS if n not in SHARD_AXIS]
STACKED = ['ffn_up_w', 'ffn_down_w']
N_CHIPS = 4


PACK_ROWS = 16


def _piece_rows(n):
    return -(-n // (PACK_ROWS * LANES)) * PACK_ROWS


def _pack(arrs, dtype, row_mult):
    lead = arrs[0].shape[:-1]
    pieces, total = [], 0
    for a in arrs:
        n = a.shape[-1]
        rows = _piece_rows(n)
        a = a.astype(dtype)
        if rows * LANES != n:
            a = jnp.pad(a, [(0, 0)] * len(lead) + [(0, rows * LANES - n)])
        pieces.append(a.reshape(lead + (rows, LANES)))
        total += rows
    extra = -total % row_mult
    if extra:
        pieces.append(jnp.zeros(lead + (extra, LANES), dtype))
    return jnp.concatenate(pieces, axis=len(lead))


def _unpack(buf, shapes):
    lead = buf.shape[:-2]
    out, off = [], 0
    for shp in shapes:
        n = math.prod(shp)
        rows = _piece_rows(n)
        piece = lax.slice_in_dim(buf, off, off + rows, axis=len(lead)).reshape(lead + (rows * LANES,))
        out.append(piece[..., :n].reshape(lead + tuple(shp)))
        off += rows
    return out


def _set_slot(buf, piece, index):
    return lax.dynamic_update_slice_in_dim(buf, piece[None], index, axis=0)


def _from_shards(stacked, axis):
    return jnp.concatenate([stacked[j] for j in range(N_CHIPS)], axis=axis)


def _heads(a, h):
    t = a.shape[0]
    return a.reshape(t, h, a.shape[1] // h).transpose(1, 0, 2)


def _unheads(a):
    h, t, d = a.shape
    return a.transpose(1, 0, 2).reshape(t, h * d)


def _ffn_fwd(h, norm_w, w_up, conv_w, conv_b, w_down, tag, side=None):
    u = _rmsnorm_fwd(h, norm_w, name=f"ffn{tag}_norm")
    hid = _matmul(u, w_up, name=f"ffn{tag}_up")
    act, side_outs = _conv_glu_fwd(hid, conv_w, conv_b, side=side, name=f"ffn{tag}_glu")
    out = _matmul(act, w_down, add=h, name=f"ffn{tag}_down")
    return out, (u, hid, act), side_outs


def _ffn_bwd(h, saved, dout, norm_w, w_up, conv_w, conv_b, w_down, tag):
    u, hid, act = saved
    dact = _matmul(dout, w_down, tb=True, name=f"ffn{tag}_down_dx")
    dw_down = _matmul(act, dout, ta=True, name=f"ffn{tag}_down_dw")
    dhid, dwg, dwv, dbg, dbv = _conv_glu_bwd(hid, conv_w, conv_b, dact, name=f"ffn{tag}_glu_bwd")
    du = _matmul(dhid, w_up, tb=True, name=f"ffn{tag}_up_dx")
    dw_up = _matmul(u, dhid, ta=True, out_parts=N_CHIPS, name=f"ffn{tag}_up_dw")
    dh, (dnorm,) = _rmsnorm_bwd(h, [(du, norm_w)], dout, name=f"ffn{tag}_norm_bwd")
    return dh, dict(norm=dnorm[0], up=dw_up, conv_w=jnp.concatenate([dwg, dwv], axis=1),
                    conv_b=jnp.concatenate([dbg, dbv], axis=1)[0], down=dw_down)


class _Pieces:
    def __init__(self, local):
        self.c = lax.axis_index("c")
        self.chip = 2 * lax.axis_index("x") + lax.axis_index("y")
        self.shape, self.s16 = {}, {}
        for n in BIG:
            blk = local[n]
            layers = [(n, l, blk[l]) for l in range(blk.shape[0])] if n in STACKED else [(n, None, blk.reshape(blk.shape[-2:]))]
            for name, l, p in layers:
                self.shape[name, l] = p.shape
                self.s16[name, l] = p.astype(BF16).reshape(2, p.shape[0] // 2, p.shape[1])

    def gather_job(self, keys):
        return _gather_job([self.s16[k] for k in keys])

    def weights(self, keys, gathered):
        out = []
        for k, g in zip(keys, gathered):
            r, cc = self.shape[k]
            by_chip = _set_slot(g, self.s16[k], self.chip).reshape(N_CHIPS, r, cc)
            if k[0] == 'ssm_in_w':
                by_chip = by_chip.transpose(1, 0, 2).reshape(r, N_CHIPS * cc)
            elif k[0] != 'ffn_up_w':
                by_chip = by_chip.reshape(N_CHIPS * r, cc)
            out.append(by_chip)
        return out

    def by_halves(self, keys, grads):
        gs = []
        for k, g in zip(keys, grads):
            r, cc = self.shape[k]
            if k[0] == 'ssm_in_w':
                g = g.reshape(r, N_CHIPS, cc).transpose(1, 0, 2)
            gs.append(g.reshape(N_CHIPS, 2, r // 2, cc))
        return gs

    def pair_sums(self, gs, recv, tag):
        return [_pair_add(g, rv, self.c, out_dtype=BF16, name=f"rs_pair_add_{tag}{i}") for i, (g, rv) in enumerate(zip(gs, recv))]

    def chip_sums(self, pairs, scattered, tag):
        return [_sum_leading(_set_slot(s, lax.dynamic_index_in_dim(p, self.chip, axis=0, keepdims=False), self.chip),
                             name=f"rs_chip_sum_{tag}{i}") for i, (s, p) in enumerate(zip(scattered, pairs))]

    def shards(self, keys, halves):
        joined = _join_halves(halves, name="rs_half_join")
        return {k: _set_slot(j, h, self.c).reshape(self.shape[k]) for k, h, j in zip(keys, halves, joined)}


def _step(x, target, w, pieces):
    t = x.shape[0]
    g_n, heads = SSM_GROUPS, SSM_HEADS
    r_h = heads // g_n
    di = D_INNER
    zx_cols = di + CONV_DIM
    k_in = [('ssm_in_w', None)]
    k_ffn0 = [('ssm_out_w', None), ('ffn_up_w', 0), ('ffn_down_w', 0)]
    k_qkv = [('w_k', None), ('w_v', None), ('w_q', None)]
    k_late = [('w_o', None), ('ffn_up_w', 1), ('ffn_down_w', 1)]
    (w_in,) = pieces.weights(k_in, _run_job(pieces.gather_job(k_in), "gather_ssm_in"))
    w_zx = w_in[:, :zx_cols]
    w_dt = jnp.pad(w_in[:, zx_cols:], ((0, 0), (0, LANES - heads)))
    conv_w, conv_b = w['ssm_conv_w'][0], w['ssm_conv_b'][0]
    hp = jnp.stack([w['ssm_dt_bias'][0], w['ssm_a_log'][0], w['ssm_d'][0]], axis=0).reshape(3, g_n, r_h)
    hpc, hpr = hp.transpose(1, 0, 2), hp.transpose(1, 2, 0)

    h0 = x
    u0 = _rmsnorm_fwd(h0, w['ssm_norm_w'][0], name="ssm_norm")
    zx = _matmul(u0, w_zx, name="ssm_in_zx")
    dt_raw = _matmul(u0, w_dt, name="ssm_in_dt")[:, :heads]
    dtg = dt_raw.reshape(t, g_n, r_h)
    dtc, dtr = dtg.transpose(1, 0, 2), dtg.transpose(1, 2, 0)
    xbc = _conv_silu_fwd(zx, conv_w, conv_b, x_off=di, name="ssm_conv")
    y, prev, got = _ssd_fwd(xbc, dtc, dtr, hpc, hpr, side=pieces.gather_job(k_ffn0), name="ssd_fwd")
    w_out, w_up0, w_down0 = pieces.weights(k_ffn0, got)
    yn = _gate_norm_fwd(y, zx, w['ssm_gate_norm_w'][0], name="ssm_gate_norm")
    h1 = _matmul(yn, w_out, add=h0, name="ssm_out")
    h2, ffn0, got = _ffn_fwd(h1, w['ffn_norm_w'][0], w_up0, w['ffn_conv_w'][0], w['ffn_conv_b'][0], w_down0, 0,
                             side=pieces.gather_job(k_qkv))
    w_k, w_v, w_q = pieces.weights(k_qkv, got)
    hk = _rmsnorm_fwd(h2, w['kv_norm_w'], name="kv_norm")
    qn = _rmsnorm_fwd(h2, w['attn_norm_w'][0], name="attn_norm")
    k2 = _matmul(hk, w_k, out_dtype=BF16, name="attn_k")
    v2 = _matmul(hk, w_v, out_dtype=BF16, name="attn_v")
    q2 = _matmul(qn, w_q, out_dtype=BF16, name="attn_q")
    qh, kh, vh = _heads(q2, SB_HEADS), _heads(k2, SB_HEADS), _heads(v2, SB_HEADS)
    oh, lt, first, got = _sb_fwd(qh, kh, vh, side=pieces.gather_job(k_late), name="sb_fwd")
    w_o, w_up1, w_down1 = pieces.weights(k_late, got)
    o2 = _unheads(oh)
    h3 = _matmul(o2, w_o, add=h2, name="attn_o")
    h4, ffn1, _ = _ffn_fwd(h3, w['ffn_norm_w'][1], w_up1, w['ffn_conv_w'][1], w['ffn_conv_b'][1], w_down1, 1)
    loss_p, dh4, d_final = _loss_head(h4, w['final_norm_w'], target, name="loss_head")

    dh3, g1 = _ffn_bwd(h3, ffn1, dh4, w['ffn_norm_w'][1], w_up1, w['ffn_conv_w'][1], w['ffn_conv_b'][1], w_down1, 1)
    do2 = _matmul(dh3, w_o, tb=True, out_dtype=BF16, name="attn_o_dx")
    dw_o = _matmul(o2, dh3, ta=True, name="attn_o_dw")
    dqh, dkh, dvh = _sb_bwd(qh, kh, vh, lt, first, _heads(do2, SB_HEADS), name="sb_bwd")
    dq2, dk2, dv2 = _unheads(dqh), _unheads(dkh), _unheads(dvh)
    dqn = _matmul(dq2, w_q, tb=True, name="attn_q_dx")
    dw_q = _matmul(qn, dq2, ta=True, name="attn_q_dw")
    dhk = _matmul(dk2, w_k, tb=True, name="attn_k_dx")
    dhk = _matmul(dv2, w_v, tb=True, add=dhk, name="attn_v_dx")
    dw_k = _matmul(hk, dk2, ta=True, name="attn_k_dw")
    dw_v = _matmul(hk, dv2, ta=True, name="attn_v_dw")
    dh2, (d_attn_norm, d_kv_norm) = _rmsnorm_bwd(h2, [(dqn, w['attn_norm_w'][0]), (dhk, w['kv_norm_w'])], dh3,
                                                 name="attn_norms_bwd")
    dh1, g0 = _ffn_bwd(h1, ffn0, dh2, w['ffn_norm_w'][0], w_up0, w['ffn_conv_w'][0], w['ffn_conv_b'][0], w_down0, 0)
    dyn = _matmul(dh1, w_out, tb=True, name="ssm_out_dx")
    dw_out = _matmul(yn, dh1, ta=True, name="ssm_out_dw")
    k_done = k_qkv + k_late + k_ffn0
    gs_done = pieces.by_halves(k_done, [dw_k, dw_v, dw_q, dw_o, g1['up'], g1['down'], dw_out, g0['up'], g0['down']])
    dy, dz, d_gate, recv = _gate_norm_bwd(y, zx, w['ssm_gate_norm_w'][0], dyn, side=_swap_job(gs_done),
                                          name="ssm_gate_norm_bwd")
    pairs_done = pieces.pair_sums(gs_done, recv, "a")
    dxs, dbm, dcm, ddt_g, hg, scattered_done = _ssd_bwd(xbc, dtc, dtr, hpc, hpr, prev, dy,
                                                        side=_scatter_job(pairs_done), name="ssd_bwd")
    dxbc = jnp.concatenate([dxs, dbm, dcm], axis=1)
    dzx, d_conv_w, d_conv_b = _conv_silu_bwd(zx, conv_w, conv_b, dxbc, x_off=di, into=dz, name="ssm_conv_bwd")
    ddt = jnp.pad(ddt_g.transpose(1, 0, 2).reshape(t, heads), ((0, 0), (0, LANES - heads)))
    du0 = _matmul(dzx, w_zx, tb=True, name="ssm_in_zx_dx")
    du0 = _matmul(ddt, w_dt, tb=True, add=du0, name="ssm_in_dt_dx")
    dw_in = jnp.concatenate([_matmul(u0, dzx, ta=True, name="ssm_in_zx_dw"),
                             _matmul(u0, ddt, ta=True, name="ssm_in_dt_dw")[:, :heads]], axis=1)
    dx, (d_ssm_norm,) = _rmsnorm_bwd(h0, [(du0, w['ssm_norm_w'][0])], dh1, name="ssm_norm_bwd")

    gs_in = pieces.by_halves(k_in, [dw_in])
    pairs_in = pieces.pair_sums(gs_in, _run_job(_swap_job(gs_in), "rs_pair_swap_b"), "b")
    halves = (pieces.chip_sums(pairs_done, scattered_done, "a")
              + pieces.chip_sums(pairs_in, _scatter_chips(pairs_in, name="rs_chip_scatter_b"), "b"))
    big_grads = pieces.shards(k_done + k_in, halves)

    hgr = hg.transpose(1, 0, 2).reshape(3, heads)
    grads = {
        'ssm_norm_w': d_ssm_norm, 'ssm_conv_w': d_conv_w[None], 'ssm_conv_b': d_conv_b,
        'ssm_dt_bias': hgr[0:1], 'ssm_a_log': hgr[1:2], 'ssm_d': hgr[2:3], 'ssm_gate_norm_w': d_gate,
        'kv_norm_w': d_kv_norm[0], 'attn_norm_w': d_attn_norm, 'ffn_norm_w': jnp.stack([g0['norm'], g1['norm']]),
        'ffn_conv_w': jnp.stack([g0['conv_w'], g1['conv_w']]), 'ffn_conv_b': jnp.stack([g0['conv_b'], g1['conv_b']]),
        'final_norm_w': d_final[0],
    }
    return loss_p, dx, grads, big_grads


def kernel(x, ssm_norm_w, ssm_in_w, ssm_conv_w, ssm_conv_b, ssm_dt_bias, ssm_a_log, ssm_d, ssm_gate_norm_w, ssm_out_w, kv_norm_w, w_k, w_v, attn_norm_w, w_q, w_o, ffn_norm_w, ffn_up_w, ffn_conv_w, ffn_conv_b, ffn_down_w, final_norm_w, loss_target, m_ssm_norm_w, m_ssm_in_w, m_ssm_conv_w, m_ssm_conv_b, m_ssm_dt_bias, m_ssm_a_log, m_ssm_d, m_ssm_gate_norm_w, m_ssm_out_w, m_kv_norm_w, m_w_k, m_w_v, m_attn_norm_w, m_w_q, m_w_o, m_ffn_norm_w, m_ffn_up_w, m_ffn_conv_w, m_ffn_conv_b, m_ffn_down_w, m_final_norm_w, v_ssm_norm_w, v_ssm_in_w, v_ssm_conv_w, v_ssm_conv_b, v_ssm_dt_bias, v_ssm_a_log, v_ssm_d, v_ssm_gate_norm_w, v_ssm_out_w, v_kv_norm_w, v_w_k, v_w_v, v_attn_norm_w, v_w_q, v_w_o, v_ffn_norm_w, v_ffn_up_w, v_ffn_conv_w, v_ffn_conv_b, v_ffn_down_w, v_final_norm_w):
    args = (ssm_norm_w, ssm_in_w, ssm_conv_w, ssm_conv_b, ssm_dt_bias, ssm_a_log, ssm_d, ssm_gate_norm_w, ssm_out_w, kv_norm_w, w_k, w_v, attn_norm_w, w_q, w_o, ffn_norm_w, ffn_up_w, ffn_conv_w, ffn_conv_b, ffn_down_w, final_norm_w)
    moms = (m_ssm_norm_w, m_ssm_in_w, m_ssm_conv_w, m_ssm_conv_b, m_ssm_dt_bias, m_ssm_a_log, m_ssm_d, m_ssm_gate_norm_w, m_ssm_out_w, m_kv_norm_w, m_w_k, m_w_v, m_attn_norm_w, m_w_q, m_w_o, m_ffn_norm_w, m_ffn_up_w, m_ffn_conv_w, m_ffn_conv_b, m_ffn_down_w, m_final_norm_w)
    vels = (v_ssm_norm_w, v_ssm_in_w, v_ssm_conv_w, v_ssm_conv_b, v_ssm_dt_bias, v_ssm_a_log, v_ssm_d, v_ssm_gate_norm_w, v_ssm_out_w, v_kv_norm_w, v_w_k, v_w_v, v_attn_norm_w, v_w_q, v_w_o, v_ffn_norm_w, v_ffn_up_w, v_ffn_conv_w, v_ffn_conv_b, v_ffn_down_w, v_final_norm_w)
    local = dict(zip(WEIGHTS, args))
    m_in = dict(zip(WEIGHTS, moms))
    v_in = dict(zip(WEIGHTS, vels))
    chip = 2 * lax.axis_index("x") + lax.axis_index("y")

    full = {n: local[n] for n in REPLICATED}
    small32 = _gather_chips(_pack([local[n].reshape(-1) for n in SMALL], F32, 8), name="gather_small")
    for n, st in zip(SMALL, _unpack(small32, [local[n].shape for n in SMALL])):
        full[n] = _from_shards(st, SHARD_AXIS[n])

    pieces = _Pieces(local)
    loss_p, dx, grads, big_grads = _step(x[0], loss_target[0], full, pieces)
    gshard = {}
    for n in BIG:
        if n in STACKED:
            gshard[n] = [big_grads[n, l] for l in range(local[n].shape[0])]
        else:
            gshard[n] = big_grads[n, None].reshape(local[n].shape)

    small = SMALL + REPLICATED
    rep = _pack([loss_p.reshape(-1)] + [grads[n].reshape(-1) for n in small], F32, 8)
    tot = _sum_leading(_gather_all(rep, name="ar_gather"), name="ar_sum")
    parts = _unpack(tot, [(LANES,)] + [grads[n].shape for n in small])
    loss = jnp.sum(parts[0])
    for n, g in zip(small, parts[1:]):
        if n in SHARD_AXIS:
            size = local[n].shape[SHARD_AXIS[n]]
            g = lax.dynamic_slice_in_dim(g, chip * size, size, axis=SHARD_AXIS[n])
        gshard[n] = g

    grads_out, deltas, new_m, new_v = [], [], [], []
    for n in WEIGHTS:
        if n in STACKED:
            g, d, nm, nv = _adamw_layers(local[n], gshard[n], m_in[n], v_in[n], name=f"adamw_{n}")
        else:
            g = gshard[n]
            d, nm, nv = _adamw(local[n], g, m_in[n], v_in[n], name=f"adamw_{n}")
        grads_out.append(g)
        deltas.append(d)
        new_m.append(nm)
        new_v.append(nv)
    return (loss, dx[None], *grads_out, *deltas, *new_m, *new_v)
```

```python
import functools
import math

import jax
import jax.numpy as jnp
from jax import lax
from jax.experimental import pallas as pl
from jax.experimental.pallas import tpu as pltpu

D_MODEL = 1024
D_INNER = 2048
SSM_HEAD_DIM = 64
SSM_HEADS = 32
SSM_GROUPS = 4
SSM_STATE = 128
SSM_CONV = 4
SSM_CHUNK = 128
GN = SSM_GROUPS * SSM_STATE
CONV_DIM = D_INNER + 2 * GN
SB_HEADS = 16
SB_HEAD_DIM = 64
D_FF = 2816
FFN_CONV = 3
EPS = 1e-6
ADAM_LR = 0.001
ADAM_B1 = 0.9
ADAM_B2 = 0.999
ADAM_EPS = 1e-08
ADAM_WD = 0.01
ADAM_STEP = 10

LANES = 128
SUBLANES = 8
VMEM_LIMIT = 48 * 1024 * 1024
ADAM_BLOCK_BYTES = 1 << 20
F32 = jnp.float32
BF16 = jnp.bfloat16
MESH = pl.DeviceIdType.MESH


def _cparams(sem=None):
    return pltpu.CompilerParams(dimension_semantics=sem, vmem_limit_bytes=VMEM_LIMIT)


class _SideJob:
    def __init__(self, ins, out_shapes, n_sems, start, finish):
        self.ins, self.out_shapes, self.n_sems, self.start, self.finish = ins, out_shapes, n_sems, start, finish


def _call(body, *, grid, in_specs, out_specs, out_shape, scratch_shapes=(), sem, name, args, side=None):
    in_specs, out_specs, out_shape, scratch_shapes = list(in_specs), list(out_specs), list(out_shape), list(scratch_shapes)
    n_in, n_out = len(in_specs), len(out_specs)
    if side is None:
        outs = pl.pallas_call(body, grid=grid, in_specs=in_specs, out_specs=out_specs, out_shape=out_shape,
                              scratch_shapes=scratch_shapes, compiler_params=_cparams(sem), name=name)(*args)
        return list(outs), []
    k_in, k_out = len(side.ins), len(side.out_shapes)

    def wrapped(*refs):
        ins, s_ins = refs[:n_in], refs[n_in:n_in + k_in]
        o0 = n_in + k_in
        outs, s_outs = refs[o0:o0 + n_out], refs[o0 + n_out:o0 + n_out + k_out]
        scratch, send_sems, recv_sems = refs[o0 + n_out + k_out:-2], refs[-2], refs[-1]
        ids = [pl.program_id(a) for a in range(len(grid))]
        first = functools.reduce(jnp.logical_and, [p == 0 for p in ids])
        last = functools.reduce(jnp.logical_and, [p == g - 1 for p, g in zip(ids, grid)])

        @pl.when(first)
        def _():
            side.start(s_ins, s_outs, send_sems, recv_sems)

        body(*ins, *outs, *scratch)

        @pl.when(last)
        def _():
            side.finish(s_ins, s_outs, send_sems, recv_sems)

    outs = pl.pallas_call(
        wrapped, grid=grid, in_specs=in_specs + [ANY] * k_in, out_specs=out_specs + [ANY] * k_out,
        out_shape=out_shape + list(side.out_shapes),
        scratch_shapes=scratch_shapes + [pltpu.SemaphoreType.DMA((side.n_sems,)), pltpu.SemaphoreType.DMA((side.n_sems,))],
        compiler_params=_cparams(tuple("arbitrary" for _ in grid)), name=name)(*args, *side.ins)
    return list(outs[:n_out]), list(outs[n_out:])


def _tile(n, cands):
    for c in cands:
        if n % c == 0:
            return c
    return n


def _nt(a, b):
    return lax.dot_general(a, b, (((1,), (1,)), ((), ())), preferred_element_type=F32)


def _tn(a, b):
    return lax.dot_general(a, b, (((0,), (0,)), ((), ())), preferred_element_type=F32)


def _nn(a, b):
    return jnp.dot(a, b, preferred_element_type=F32)


def _split(x, pieces):
    out = []
    for _ in range(pieces - 1):
        h = x.astype(BF16)
        out.append(h)
        x = x - h.astype(F32)
    out.append(x.astype(BF16))
    return out


def _ones_dot(ones, x, *, ones_left, pieces=3):
    o16 = ones.astype(BF16)
    acc = None
    for piece in _split(x, pieces):
        term = _nn(o16, piece) if ones_left else _nn(piece, o16)
        acc = term if acc is None else acc + term
    return acc


def _row_sums(x, pieces=3):
    return _ones_dot(jnp.ones((x.shape[1], LANES), F32), x, ones_left=False, pieces=pieces)


def _softplus(x):
    return jnp.maximum(x, 0.0) + jnp.log(1.0 + jnp.exp(-jnp.abs(x)))


def _sigmoid(x):
    return 0.5 * jnp.tanh(0.5 * x) + 0.5


MM_TILE_MAX = 1408
MM_VMEM_BUDGET = 40 * 1024 * 1024


def _divisors(n, cap):
    out = [d for d in range(min(cap, n) // LANES * LANES, 0, -LANES) if n % d == 0]
    return out or [n]


def _mm_tiles(m, n, k, a_bytes, b_bytes, o_bytes, add_bytes):
    best = None
    for tm in _divisors(m, MM_TILE_MAX):
        for tn in _divisors(n, MM_TILE_MAX):
            for tk in _divisors(k, MM_TILE_MAX):
                vmem = 2 * (tm * tk * a_bytes + tk * tn * b_bytes + tm * tn * (o_bytes + add_bytes)) + tm * tn * 4
                if vmem > MM_VMEM_BUDGET:
                    continue
                score = (tm * tn * tk, tm * tn)
                if best is None or score > best[0]:
                    best = (score, (tm, tn, tk))
    return best[1]


def _matmul(a, b, *, ta=False, tb=False, add=None, out_dtype=F32, out_parts=1, name):
    a_parts = a.shape[0] if a.ndim == 3 else 1
    b_parts = b.shape[0] if b.ndim == 3 else 1
    assert not (ta and a_parts > 1)
    a2, b2 = a.shape[-2:], b.shape[-2:]
    m, k = (a2[1], a2[0]) if ta else (a2[0], a2[1] * a_parts)
    n, kb = (b2[0], b2[1] * b_parts) if tb else (b2[1] * b_parts, b2[0])
    assert kb == k, (a.shape, b.shape)
    n_unit = math.gcd(n // out_parts, n if tb else b2[1])
    k_unit = math.gcd(k // a_parts, b2[1] if tb else k)
    tm, tn, tk = _mm_tiles(m, n_unit, k_unit, a.dtype.itemsize, b.dtype.itemsize, jnp.dtype(out_dtype).itemsize,
                           0 if add is None else add.dtype.itemsize)
    nk = k // tk
    ka, kbp = (k // a_parts) // tk, (k // b_parts) // tk
    nb, no = (n // b_parts) // tn, (n // out_parts) // tn

    def body(*refs):
        if add is None:
            a_ref, b_ref, o_ref = refs[:3]
            add_ref = None
        else:
            a_ref, b_ref, add_ref, o_ref = refs[:4]
        kk = pl.program_id(2)
        dn = (((0 if ta else 1,), (1 if tb else 0,)), ((), ()))
        prod = lax.dot_general(a_ref[...].astype(BF16), b_ref[...].astype(BF16), dn, preferred_element_type=F32)

        def finish(r):
            if add_ref is not None:
                r = r + add_ref[...].astype(F32)
            o_ref[...] = r.astype(o_ref.dtype)

        if nk == 1:
            finish(prod)
            return
        acc_ref = refs[-1]

        @pl.when(kk == 0)
        def _():
            acc_ref[...] = prod

        @pl.when(jnp.logical_and(kk > 0, kk < nk - 1))
        def _():
            acc_ref[...] += prod

        @pl.when(kk == nk - 1)
        def _():
            finish(acc_ref[...] + prod)

    if ta:
        a_spec = pl.BlockSpec((tk, tm), lambda i, j, kk: (kk, i))
    elif a_parts > 1:
        a_spec = pl.BlockSpec((None, tm, tk), lambda i, j, kk: (kk // ka, i, kk % ka))
    else:
        a_spec = pl.BlockSpec((tm, tk), lambda i, j, kk: (i, kk))
    if b_parts == 1:
        b_spec = pl.BlockSpec((tn, tk), lambda i, j, kk: (j, kk)) if tb else pl.BlockSpec((tk, tn), lambda i, j, kk: (kk, j))
    elif tb:
        b_spec = pl.BlockSpec((None, tn, tk), lambda i, j, kk: (kk // kbp, j, kk % kbp))
    else:
        b_spec = pl.BlockSpec((None, tk, tn), lambda i, j, kk: (j // nb, kk, j % nb))
    if out_parts > 1:
        o_spec = pl.BlockSpec((None, tm, tn), lambda i, j, kk: (j // no, i, j % no))
        o_shape = jax.ShapeDtypeStruct((out_parts, m, n // out_parts), out_dtype)
    else:
        o_spec = pl.BlockSpec((tm, tn), lambda i, j, kk: (i, j))
        o_shape = jax.ShapeDtypeStruct((m, n), out_dtype)
    in_specs = [a_spec, b_spec]
    args = [a, b]
    if add is not None:
        in_specs.append(pl.BlockSpec((tm, tn), lambda i, j, kk: (i, j)))
        args.append(add)
    return pl.pallas_call(
        body,
        grid=(m // tm, n // tn, nk),
        in_specs=in_specs,
        out_specs=o_spec,
        out_shape=o_shape,
        scratch_shapes=[pltpu.VMEM((tm, tn), F32)] if nk > 1 else [],
        compiler_params=_cparams(("parallel", "parallel", "arbitrary")),
        name=name,
    )(*args)


def _rmsnorm_fwd(x, w, *, name):
    t, d = x.shape
    tb = _tile(t, (512, 256, 128))

    def body(x_ref, w_ref, o_ref):
        xv = x_ref[...]
        r = lax.rsqrt(jnp.mean(xv * xv, axis=-1, keepdims=True) + EPS)
        o_ref[...] = (xv * r * w_ref[...]).astype(o_ref.dtype)

    return pl.pallas_call(
        body,
        grid=(t // tb,),
        in_specs=[pl.BlockSpec((tb, d), lambda i: (i, 0)), pl.BlockSpec((1, d), lambda i: (0, 0))],
        out_specs=pl.BlockSpec((tb, d), lambda i: (i, 0)),
        out_shape=jax.ShapeDtypeStruct((t, d), BF16),
        compiler_params=_cparams(("parallel",)),
        name=name,
    )(x, w.reshape(1, d))


def _rmsnorm_bwd(x, dys, dres, *, name):
    t, d = x.shape
    tb = _tile(t, (256, 128))
    nn = len(dys)
    has_res = dres is not None

    def body(*refs):
        x_ref = refs[0]
        dy_refs = refs[1:1 + nn]
        w_refs = refs[1 + nn:1 + 2 * nn]
        pos = 1 + 2 * nn
        res_ref = refs[pos] if has_res else None
        pos += 1 if has_res else 0
        dx_ref = refs[pos]
        dw_refs = refs[pos + 1:pos + 1 + nn]
        i = pl.program_id(0)
        xv = x_ref[...]
        r = lax.rsqrt(jnp.mean(xv * xv, axis=-1, keepdims=True) + EPS)
        xn = xv * r
        dx = res_ref[...] if has_res else jnp.zeros_like(xv)
        for q in range(nn):
            dy = dy_refs[q][...].astype(F32)
            g = dy * w_refs[q][...]
            dx = dx + r * (g - xn * jnp.mean(g * xn, axis=-1, keepdims=True))
            dwp = jnp.sum(dy * xn, axis=0, keepdims=True)

            @pl.when(i == 0)
            def _(q=q, dwp=dwp):
                dw_refs[q][...] = dwp

            @pl.when(i > 0)
            def _(q=q, dwp=dwp):
                dw_refs[q][...] += dwp
        dx_ref[...] = dx

    row = pl.BlockSpec((tb, d), lambda i: (i, 0))
    vec = pl.BlockSpec((1, d), lambda i: (0, 0))
    in_specs = [row] + [row] * nn + [vec] * nn + ([row] if has_res else [])
    args = [x] + [p[0] for p in dys] + [p[1].reshape(1, d) for p in dys] + ([dres] if has_res else [])
    outs = pl.pallas_call(
        body,
        grid=(t // tb,),
        in_specs=in_specs,
        out_specs=[row] + [vec] * nn,
        out_shape=[jax.ShapeDtypeStruct((t, d), F32)] + [jax.ShapeDtypeStruct((1, d), F32)] * nn,
        compiler_params=_cparams(("arbitrary",)),
        name=name,
    )(*args)
    return outs[0], list(outs[1:])


def _loss_head(x, w, target, *, name):
    t, d = x.shape
    tb = _tile(t, (256, 128))

    def body(x_ref, w_ref, t_ref, loss_ref, dx_ref, dw_ref):
        i = pl.program_id(0)
        xv = x_ref[...]
        wv = w_ref[...]
        r = lax.rsqrt(jnp.mean(xv * xv, axis=-1, keepdims=True) + EPS)
        xn = xv * r
        e = xn * wv - t_ref[...]
        lp = 0.5 * jnp.sum(jnp.mean(e * e, axis=-1, keepdims=True), axis=0, keepdims=True)
        dy = e * (1.0 / d)
        g = dy * wv
        dx_ref[...] = r * (g - xn * jnp.mean(g * xn, axis=-1, keepdims=True))
        dwp = jnp.sum(dy * xn, axis=0, keepdims=True)
        lpv = jnp.broadcast_to(lp, (1, LANES)) * (1.0 / LANES)

        @pl.when(i == 0)
        def _():
            dw_ref[...] = dwp
            loss_ref[...] = lpv

        @pl.when(i > 0)
        def _():
            dw_ref[...] += dwp
            loss_ref[...] += lpv

    row = pl.BlockSpec((tb, d), lambda i: (i, 0))
    vec = pl.BlockSpec((1, d), lambda i: (0, 0))
    return pl.pallas_call(
        body,
        grid=(t // tb,),
        in_specs=[row, vec, row],
        out_specs=[pl.BlockSpec((1, LANES), lambda i: (0, 0)), row, vec],
        out_shape=[jax.ShapeDtypeStruct((1, LANES), F32), jax.ShapeDtypeStruct((t, d), F32),
                   jax.ShapeDtypeStruct((1, d), F32)],
        compiler_params=_cparams(("arbitrary",)),
        name=name,
    )(x, w.reshape(1, d), target)


ROW_CHUNK = 64
PAD = SUBLANES


def _shifted(pad_ref, r0, rows, back):
    return pad_ref[pl.ds(PAD + r0 - back, rows), :]


def _conv_taps(pad_ref, w_ref, r0, rows, kw):
    acc = None
    for j in range(kw):
        term = _shifted(pad_ref, r0, rows, kw - 1 - j) * w_ref[j:j + 1, :]
        acc = term if acc is None else acc + term
    return acc


def _fill_pad(pad_ref, x_ref, t):
    pad_ref[0:PAD, :] = jnp.zeros((PAD, pad_ref.shape[1]), F32)
    pad_ref[pl.ds(PAD + t, PAD), :] = jnp.zeros((PAD, pad_ref.shape[1]), F32)
    pad_ref[pl.ds(PAD, t), :] = x_ref[...].astype(F32)


def _conv_silu_fwd(x, w, b, *, x_off=0, name):
    t = x.shape[0]
    kw, c = w.shape
    cw = _tile(math.gcd(c, x_off) if x_off else c, (256, 128))
    ob = x_off // cw
    rc = _tile(t, (ROW_CHUNK,))

    def body(x_ref, w_ref, b_ref, o_ref, pad_ref):
        _fill_pad(pad_ref, x_ref, t)
        for r0 in range(0, t, rc):
            pre = _conv_taps(pad_ref, w_ref, r0, rc, kw) + b_ref[...]
            o_ref[pl.ds(r0, rc), :] = pre * _sigmoid(pre)

    strip = pl.BlockSpec((t, cw), lambda i: (0, i))
    return pl.pallas_call(
        body,
        grid=(c // cw,),
        in_specs=[pl.BlockSpec((t, cw), lambda i: (0, i + ob)), pl.BlockSpec((kw, cw), lambda i: (0, i)),
                  pl.BlockSpec((1, cw), lambda i: (0, i))],
        out_specs=strip,
        out_shape=jax.ShapeDtypeStruct((t, c), F32),
        scratch_shapes=[pltpu.VMEM((t + 2 * PAD, cw), F32)],
        compiler_params=_cparams(("parallel",)),
        name=name,
    )(x, w, b.reshape(1, c))


def _conv_bwd_core(dpre_pad_ref, x_pad_ref, w_ref, dx_ref, dw_ref, db_ref, t, rc, kw):
    cw = dx_ref.shape[1]

    def fold(a):
        return jnp.sum(a.reshape(rc // SUBLANES, SUBLANES, cw), axis=0) if rc % SUBLANES == 0 else jnp.sum(a, axis=0, keepdims=True)

    dws = [None] * kw
    dbs = None
    for r0 in range(0, t, rc):
        dpre = dpre_pad_ref[pl.ds(PAD + r0, rc), :]
        dx = None
        for j in range(kw):
            s = kw - 1 - j
            term = dpre_pad_ref[pl.ds(PAD + r0 + s, rc), :] * w_ref[j:j + 1, :]
            dx = term if dx is None else dx + term
            part = fold(dpre * _shifted(x_pad_ref, r0, rc, s))
            dws[j] = part if dws[j] is None else dws[j] + part
        part = fold(dpre)
        dbs = part if dbs is None else dbs + part
        dx_ref[pl.ds(r0, rc), :] = dx
    for j in range(kw):
        dw_ref[j:j + 1, :] = jnp.sum(dws[j], axis=0, keepdims=True)
    db_ref[...] = jnp.sum(dbs, axis=0, keepdims=True)


def _conv_silu_bwd(x, w, b, dact, *, x_off=0, into=None, name):
    t = x.shape[0]
    kw, c = w.shape
    parts = dact if isinstance(dact, (list, tuple)) else [dact]
    widths = [p.shape[1] for p in parts]
    assert sum(widths) == c
    cw = _tile(functools.reduce(math.gcd, widths + [x_off or c]), (256, 128) if len(parts) == 1 else (128,))
    ob = x_off // cw
    rc = _tile(t, (ROW_CHUNK,))
    firsts = [sum(widths[:p]) // cw for p in range(len(parts))]
    counts = [wd // cw for wd in widths]
    n_p = len(parts)

    def body(x_ref, w_ref, b_ref, *rest):
        da_refs = rest[:n_p]
        dx_ref, dw_ref, db_ref, xpad_ref, dpad_ref = rest[-5 - (n_p > 1):][:5]
        if n_p > 1:
            da_ref = rest[-1]
            i = pl.program_id(0)
            for p in range(n_p):
                @pl.when(jnp.logical_and(i >= firsts[p], i < firsts[p] + counts[p]))
                def _(p=p):
                    da_ref[...] = da_refs[p][...]
        else:
            da_ref = da_refs[0]
        _fill_pad(xpad_ref, x_ref, t)
        dpad_ref[0:PAD, :] = jnp.zeros((PAD, cw), F32)
        dpad_ref[pl.ds(PAD + t, PAD), :] = jnp.zeros((PAD, cw), F32)
        for r0 in range(0, t, rc):
            pre = _conv_taps(xpad_ref, w_ref, r0, rc, kw) + b_ref[...]
            sg = _sigmoid(pre)
            dpad_ref[pl.ds(PAD + r0, rc), :] = da_ref[pl.ds(r0, rc), :] * (sg * (1.0 + pre * (1.0 - sg)))
        _conv_bwd_core(dpad_ref, xpad_ref, w_ref, dx_ref, dw_ref, db_ref, t, rc, kw)

    strip = pl.BlockSpec((t, cw), lambda i: (0, i))
    wspec = pl.BlockSpec((kw, cw), lambda i: (0, i))
    bspec = pl.BlockSpec((1, cw), lambda i: (0, i))
    xspec = pl.BlockSpec((t, cw), lambda i: (0, i + ob))
    dspecs = [pl.BlockSpec((t, cw), lambda i, f=f, n=n: (0, jnp.clip(i - f, 0, n - 1))) for f, n in zip(firsts, counts)]
    extra = {} if into is None else dict(input_output_aliases={3 + n_p: 0})
    pad = pltpu.VMEM((t + 2 * PAD, cw), F32)
    return pl.pallas_call(
        body,
        grid=(c // cw,),
        in_specs=[xspec, wspec, bspec] + dspecs + ([] if into is None else [ANY]),
        out_specs=[strip if into is None else xspec, wspec, bspec],
        out_shape=[jax.ShapeDtypeStruct((t, c) if into is None else into.shape, F32), jax.ShapeDtypeStruct((kw, c), F32),
                   jax.ShapeDtypeStruct((1, c), F32)],
        scratch_shapes=[pad, pad] + ([pltpu.VMEM((t, cw), F32)] if n_p > 1 else []),
        compiler_params=_cparams(("arbitrary",)),
        name=name,
        **extra,
    )(x, w, b.reshape(1, c), *parts, *([] if into is None else [into]))


def _conv_glu_fwd(hid, w, b, *, side=None, name):
    t, c2 = hid.shape
    f = c2 // 2
    kw = w.shape[0]
    cw = _tile(f, (256, 128))
    nf = f // cw
    rc = _tile(t, (ROW_CHUNK,))

    def body(g_ref, v_ref, wg_ref, wv_ref, bg_ref, bv_ref, o_ref, gpad_ref, vpad_ref):
        _fill_pad(gpad_ref, g_ref, t)
        _fill_pad(vpad_ref, v_ref, t)
        for r0 in range(0, t, rc):
            gate = _conv_taps(gpad_ref, wg_ref, r0, rc, kw) + bg_ref[...]
            val = _conv_taps(vpad_ref, wv_ref, r0, rc, kw) + bv_ref[...]
            o_ref[pl.ds(r0, rc), :] = (gate * _sigmoid(gate) * val).astype(o_ref.dtype)

    gs = pl.BlockSpec((t, cw), lambda i: (0, i))
    vs = pl.BlockSpec((t, cw), lambda i: (0, i + nf))
    b2 = b.reshape(1, c2)
    (act,), side_outs = _call(
        body,
        grid=(nf,),
        in_specs=[gs, vs, pl.BlockSpec((kw, cw), lambda i: (0, i)), pl.BlockSpec((kw, cw), lambda i: (0, i + nf)),
                  pl.BlockSpec((1, cw), lambda i: (0, i)), pl.BlockSpec((1, cw), lambda i: (0, i + nf))],
        out_specs=[gs],
        out_shape=[jax.ShapeDtypeStruct((t, f), BF16)],
        scratch_shapes=[pltpu.VMEM((t + 2 * PAD, cw), F32), pltpu.VMEM((t + 2 * PAD, cw), F32)],
        sem=("parallel",),
        name=name,
        args=(hid, hid, w, w, b2, b2),
        side=side,
    )
    return act, side_outs


def _conv_glu_bwd(hid, w, b, dact, *, name):
    t, c2 = hid.shape
    f = c2 // 2
    kw = w.shape[0]
    cw = _tile(f, (128,))
    nf = f // cw
    rc = _tile(t, (ROW_CHUNK,))

    def body(g_ref, v_ref, wg_ref, wv_ref, bg_ref, bv_ref, da_ref,
             dgv_ref, dwg_ref, dwv_ref, dbg_ref, dbv_ref,
             gpad_ref, vpad_ref, dgpad_ref, dvpad_ref):
        _fill_pad(gpad_ref, g_ref, t)
        _fill_pad(vpad_ref, v_ref, t)
        for ref in (dgpad_ref, dvpad_ref):
            ref[0:PAD, :] = jnp.zeros((PAD, cw), F32)
            ref[pl.ds(PAD + t, PAD), :] = jnp.zeros((PAD, cw), F32)
        for r0 in range(0, t, rc):
            gate = _conv_taps(gpad_ref, wg_ref, r0, rc, kw) + bg_ref[...]
            val = _conv_taps(vpad_ref, wv_ref, r0, rc, kw) + bv_ref[...]
            sg = _sigmoid(gate)
            da = da_ref[pl.ds(r0, rc), :].astype(F32)
            dgpad_ref[pl.ds(PAD + r0, rc), :] = da * val * (sg * (1.0 + gate * (1.0 - sg)))
            dvpad_ref[pl.ds(PAD + r0, rc), :] = da * (gate * sg)
        _conv_bwd_core(dgpad_ref, gpad_ref, wg_ref, dgv_ref.at[0], dwg_ref, dbg_ref, t, rc, kw)
        _conv_bwd_core(dvpad_ref, vpad_ref, wv_ref, dgv_ref.at[1], dwv_ref, dbv_ref, t, rc, kw)

    gs = pl.BlockSpec((t, cw), lambda i: (0, i))
    vs = pl.BlockSpec((t, cw), lambda i: (0, i + nf))
    wg = pl.BlockSpec((kw, cw), lambda i: (0, i))
    wv = pl.BlockSpec((kw, cw), lambda i: (0, i + nf))
    bg = pl.BlockSpec((1, cw), lambda i: (0, i))
    bv = pl.BlockSpec((1, cw), lambda i: (0, i + nf))
    b2 = b.reshape(1, c2)
    pad = pltpu.VMEM((t + 2 * PAD, cw), F32)
    return pl.pallas_call(
        body,
        grid=(nf,),
        in_specs=[gs, vs, wg, wv, bg, bv, gs],
        out_specs=[pl.BlockSpec((2, t, cw), lambda i: (0, 0, i)), wg, wg, bg, bg],
        out_shape=[jax.ShapeDtypeStruct((2, t, f), F32),
                   jax.ShapeDtypeStruct((kw, f), F32), jax.ShapeDtypeStruct((kw, f), F32),
                   jax.ShapeDtypeStruct((1, f), F32), jax.ShapeDtypeStruct((1, f), F32)],
        scratch_shapes=[pad, pad, pad, pad],
        compiler_params=_cparams(("parallel",)),
        name=name,
    )(hid, hid, w, w, b2, b2, dact)


def _gate_norm_fwd(y, zx, w, *, name):
    t, di = y.shape
    gsz = di // SSM_GROUPS
    tb = _tile(t, (256, 128))

    def body(y_ref, z_ref, w_ref, o_ref):
        for g in range(SSM_GROUPS):
            sl = slice(g * gsz, (g + 1) * gsz)
            zv = z_ref[:, sl]
            gv = y_ref[:, sl] * (zv * _sigmoid(zv))
            r = lax.rsqrt(jnp.mean(gv * gv, axis=-1, keepdims=True) + EPS)
            o_ref[:, sl] = (gv * r * w_ref[:, sl]).astype(o_ref.dtype)

    row = pl.BlockSpec((tb, di), lambda i: (i, 0))
    return pl.pallas_call(
        body,
        grid=(t // tb,),
        in_specs=[row, row, pl.BlockSpec((1, di), lambda i: (0, 0))],
        out_specs=row,
        out_shape=jax.ShapeDtypeStruct((t, di), BF16),
        compiler_params=_cparams(("parallel",)),
        name=name,
    )(y, zx, w.reshape(1, di))


def _gate_norm_bwd(y, zx, w, dyn, *, side=None, name):
    t, di = y.shape
    gsz = di // SSM_GROUPS
    tb = _tile(t, (256, 128))

    def body(y_ref, z_ref, w_ref, d_ref, dy_ref, dz_ref, dw_ref):
        i = pl.program_id(0)
        for g in range(SSM_GROUPS):
            sl = slice(g * gsz, (g + 1) * gsz)
            zv = z_ref[:, sl]
            yv = y_ref[:, sl]
            sg = _sigmoid(zv)
            sz = zv * sg
            gv = yv * sz
            r = lax.rsqrt(jnp.mean(gv * gv, axis=-1, keepdims=True) + EPS)
            gn = gv * r
            dn = d_ref[:, sl].astype(F32)
            q = dn * w_ref[:, sl]
            dg = r * (q - gn * jnp.mean(q * gn, axis=-1, keepdims=True))
            dy_ref[:, sl] = dg * sz
            dz_ref[:, sl] = dg * yv * (sg * (1.0 + zv * (1.0 - sg)))
            dwp = jnp.sum(dn * gn, axis=0, keepdims=True)

            @pl.when(i == 0)
            def _(sl=sl, dwp=dwp):
                dw_ref[:, sl] = dwp

            @pl.when(i > 0)
            def _(sl=sl, dwp=dwp):
                dw_ref[:, sl] += dwp

    row = pl.BlockSpec((tb, di), lambda i: (i, 0))
    vec = pl.BlockSpec((1, di), lambda i: (0, 0))
    outs, side_outs = _call(
        body,
        grid=(t // tb,),
        in_specs=[row, row, vec, row],
        out_specs=[row, row, vec],
        out_shape=[jax.ShapeDtypeStruct((t, di), F32), jax.ShapeDtypeStruct((t, zx.shape[1]), F32),
                   jax.ShapeDtypeStruct((1, di), F32)],
        sem=("arbitrary",),
        name=name,
        args=(y, zx, w.reshape(1, di), dyn),
        side=side,
    )
    return (*outs, side_outs)


def _adamw(w, g, m, v, *, name):
    shape = w.shape
    cols = shape[-1]
    rows = w.size // cols
    w2, g2, m2, v2 = (a.reshape(rows, cols) for a in (w, g, m, v))
    tr = rows
    if rows * cols * 4 > ADAM_BLOCK_BYTES:
        tr = _tile(rows, tuple(r for r in (512, 256, 128, 64, 32, 16, 8) if r * cols * 4 <= ADAM_BLOCK_BYTES))
    c1 = 1.0 - ADAM_B1 ** ADAM_STEP
    c2 = 1.0 - ADAM_B2 ** ADAM_STEP

    def body(w_ref, g_ref, m_ref, v_ref, d_ref, nm_ref, nv_ref):
        gv = g_ref[...]
        nm = ADAM_B1 * m_ref[...] + (1.0 - ADAM_B1) * gv
        nv = ADAM_B2 * v_ref[...] + (1.0 - ADAM_B2) * (gv * gv)
        d_ref[...] = -ADAM_LR * ((nm / c1) / (jnp.sqrt(nv / c2) + ADAM_EPS) + ADAM_WD * w_ref[...])
        nm_ref[...] = nm
        nv_ref[...] = nv

    blk = pl.BlockSpec((tr, cols), lambda i: (i, 0))
    outs = pl.pallas_call(
        body,
        grid=(rows // tr,),
        in_specs=[blk] * 4,
        out_specs=[blk] * 3,
        out_shape=[jax.ShapeDtypeStruct((rows, cols), F32)] * 3,
        compiler_params=_cparams(("parallel",)),
        name=name,
    )(w2, g2, m2, v2)
    return tuple(o.reshape(shape) for o in outs)


def _adamw_layers(w, gs, m, v, *, name):
    n_l, rows, cols = w.shape
    assert len(gs) == n_l
    tr = _tile(rows, tuple(r for r in (512, 256, 128, 64, 32, 16, 8) if r * cols * 4 <= ADAM_BLOCK_BYTES))
    c1 = 1.0 - ADAM_B1 ** ADAM_STEP
    c2 = 1.0 - ADAM_B2 ** ADAM_STEP

    def body(*refs):
        w_ref, m_ref, v_ref = refs[:3]
        g_refs = refs[3:3 + n_l]
        g_ref, d_ref, nm_ref, nv_ref = refs[3 + n_l:]
        layer = pl.program_id(0)
        gv = g_refs[0][...]
        for q in range(1, n_l):
            gv = jnp.where(layer == q, g_refs[q][...], gv)
        nm = ADAM_B1 * m_ref[...] + (1.0 - ADAM_B1) * gv
        nv = ADAM_B2 * v_ref[...] + (1.0 - ADAM_B2) * (gv * gv)
        g_ref[...] = gv
        d_ref[...] = -ADAM_LR * ((nm / c1) / (jnp.sqrt(nv / c2) + ADAM_EPS) + ADAM_WD * w_ref[...])
        nm_ref[...] = nm
        nv_ref[...] = nv

    stacked = pl.BlockSpec((None, tr, cols), lambda l, i: (l, i, 0))
    single = pl.BlockSpec((tr, cols), lambda l, i: (i, 0))
    return pl.pallas_call(
        body,
        grid=(n_l, rows // tr),
        in_specs=[stacked] * 3 + [single] * n_l,
        out_specs=[stacked] * 4,
        out_shape=[jax.ShapeDtypeStruct(w.shape, F32)] * 4,
        compiler_params=_cparams(("parallel", "parallel")),
        name=name,
    )(w, m, v, *gs)


def _ssd_scalars(dtc_ref, dtr_ref, hpc_ref, hpr_ref, ln):
    assert SSM_CHUNK == SSM_STATE == LANES, "the SSD kernels mix chunk, state and lane-wide tiles freely"
    bias_c, alog_c = hpc_ref[0, 0:1, :], hpc_ref[0, 1:2, :]
    bias_r, alog_r = hpr_ref[0, :, 0:1], hpr_ref[0, :, 1:2]
    a_c, a_r = -jnp.exp(alog_c), -jnp.exp(alog_r)
    raw_c = dtc_ref[0] + bias_c
    dt_c = _softplus(raw_c)
    dt_r = _softplus(dtr_ref[0] + bias_r)
    row = lax.broadcasted_iota(jnp.int32, (ln, ln), 0)
    col = lax.broadcasted_iota(jnp.int32, (ln, ln), 1)
    lower = (col <= row).astype(F32)
    upper = (row <= col).astype(F32)
    acs_c = _ones_dot(lower, dt_c * a_c, ones_left=True)
    acs_r = _ones_dot(upper, dt_r * a_r, ones_left=False)
    return raw_c, dt_c, a_c, acs_c, acs_r, row, col


def _ssd_specs(t, di, g_n, n_st, rp, ln, r_h, rev):
    nc = t // ln
    cidx = (lambda c: nc - 1 - c) if rev else (lambda c: c)
    xs = pl.BlockSpec((ln, rp), lambda g, c: (cidx(c), g))
    bm = pl.BlockSpec((ln, n_st), lambda g, c: (cidx(c), di // n_st + g))
    cm = pl.BlockSpec((ln, n_st), lambda g, c: (cidx(c), di // n_st + g_n + g))
    dtc = pl.BlockSpec((1, ln, r_h), lambda g, c: (g, cidx(c), 0))
    dtr = pl.BlockSpec((1, r_h, ln), lambda g, c: (g, 0, cidx(c)))
    hpc = pl.BlockSpec((1, 3, r_h), lambda g, c: (g, 0, 0))
    hpr = pl.BlockSpec((1, r_h, 3), lambda g, c: (g, 0, 0))
    prev = pl.BlockSpec((1, rp, n_st), lambda g, c: (cidx(c), g, 0))
    return xs, bm, cm, dtc, dtr, hpc, hpr, prev


def _ssd_fwd(xbc, dtc, dtr, hpc, hpr, *, side=None, name):
    t = xbc.shape[0]
    di, g_n, n_st, p_h, ln = D_INNER, SSM_GROUPS, SSM_STATE, SSM_HEAD_DIM, SSM_CHUNK
    r_h = SSM_HEADS // g_n
    rp = r_h * p_h
    nc = t // ln

    def body(xs_ref, b_ref, c_ref, dtc_ref, dtr_ref, hpc_ref, hpr_ref, y_ref, prev_ref, st_ref):
        @pl.when(pl.program_id(1) == 0)
        def _():
            st_ref[...] = jnp.zeros_like(st_ref)

        _, dt_c, _, acs_c, acs_r, row, col = _ssd_scalars(dtc_ref, dtr_ref, hpc_ref, hpr_ref, ln)
        bm = b_ref[...]
        cm = c_ref[...]
        cm16 = cm.astype(BF16)
        cb = _nt(cm16, bm.astype(BF16))
        causal = row >= col
        for r in range(r_h):
            sl = slice(r * p_h, (r + 1) * p_h)
            xs = xs_ref[:, sl]
            acs = jnp.broadcast_to(acs_c[:, r:r + 1], (ln, ln))
            last = acs[ln - 1:ln, :]
            lm = jnp.where(causal, jnp.exp(acs - acs_r[r:r + 1, :]), 0.0)
            xd = (xs * jnp.broadcast_to(dt_c[:, r:r + 1], (ln, p_h))).astype(BF16)
            prev = st_ref[sl, :]
            y = _nn((cb * lm).astype(BF16), xd)
            y = y + _nt(cm16, prev.astype(BF16)) * jnp.exp(acs[:, :p_h])
            y_ref[:, sl] = y + hpc_ref[0, 2:3, r:r + 1] * xs
            prev_ref[0, sl, :] = prev
            bd = (bm * jnp.exp(last - acs[:, :n_st])).astype(BF16)
            st_ref[sl, :] = prev * jnp.exp(last[:, :n_st]) + _tn(xd, bd)

    xs, bm, cm, dtcs, dtrs, hpcs, hprs, prev = _ssd_specs(t, di, g_n, n_st, rp, ln, r_h, False)
    (y, prev_out), side_outs = _call(
        body,
        grid=(g_n, nc),
        in_specs=[xs, bm, cm, dtcs, dtrs, hpcs, hprs],
        out_specs=[xs, prev],
        out_shape=[jax.ShapeDtypeStruct((t, di), F32), jax.ShapeDtypeStruct((nc, g_n * rp, n_st), F32)],
        scratch_shapes=[pltpu.VMEM((rp, n_st), F32)],
        sem=("parallel", "arbitrary"),
        name=name,
        args=(xbc, xbc, xbc, dtc, dtr, hpc, hpr),
        side=side,
    )
    return y, prev_out, side_outs


def _ssd_bwd(xbc, dtc, dtr, hpc, hpr, prev, dy, *, side=None, name):
    t = xbc.shape[0]
    di, g_n, n_st, p_h, ln = D_INNER, SSM_GROUPS, SSM_STATE, SSM_HEAD_DIM, SSM_CHUNK
    r_h = SSM_HEADS // g_n
    rp = r_h * p_h
    nc = t // ln

    def body(xs_ref, b_ref, c_ref, dtc_ref, dtr_ref, hpc_ref, hpr_ref, prev_ref, dy_ref,
             dxs_ref, db_ref, dc_ref, ddt_ref, hg_ref, ds_ref):
        step = pl.program_id(1)

        @pl.when(step == 0)
        def _():
            ds_ref[...] = jnp.zeros_like(ds_ref)

        raw_c, dt_c, a_c, acs_c, acs_r, row, col = _ssd_scalars(dtc_ref, dtr_ref, hpc_ref, hpr_ref, ln)
        bm = b_ref[...]
        cm = c_ref[...]
        bm16, cm16 = bm.astype(BF16), cm.astype(BF16)
        cb = _nt(cm16, bm16)
        cbt = _nt(bm16, cm16)
        lane_r = lax.broadcasted_iota(jnp.int32, (ln, r_h), 1)
        dacs_all = jnp.zeros((ln, r_h), F32)
        ddtx_all = jnp.zeros((ln, r_h), F32)
        dd_all = jnp.zeros((ln, r_h), F32)
        dcb = jnp.zeros((ln, ln), F32)
        dcbt = jnp.zeros((ln, ln), F32)
        dc_acc = jnp.zeros((ln, n_st), F32)
        db_acc = jnp.zeros((ln, n_st), F32)
        for r in range(r_h):
            sl = slice(r * p_h, (r + 1) * p_h)
            xs = xs_ref[:, sl]
            dyv = dy_ref[:, sl]
            dy16 = dyv.astype(BF16)
            acs = jnp.broadcast_to(acs_c[:, r:r + 1], (ln, ln))
            dtv = jnp.broadcast_to(dt_c[:, r:r + 1], (ln, p_h))
            acsr = acs_r[r:r + 1, :]
            last = acs[ln - 1:ln, :]
            xd = xs * dtv
            xd16 = xd.astype(BF16)
            lm = jnp.where(row >= col, jnp.exp(acs - acsr), 0.0)
            lmt = jnp.where(col >= row, jnp.exp(acsr - acs), 0.0)
            m_ls = cb * lm
            m_sl = cbt * lmt
            dm = _nt(dy16, xd16)
            dmt = _nt(xd16, dy16)
            dxd = _nn(m_sl.astype(BF16), dy16)
            dacs = _row_sums(dm * m_ls - dmt * m_sl)
            dcb = dcb + dm * lm
            dcbt = dcbt + dmt * lmt
            prev = prev_ref[0, sl, :]
            prev16 = prev.astype(BF16)
            e = jnp.exp(acs[:, :p_h])
            y_off = _nt(cm16, prev16) * e
            dacs = dacs + _row_sums(dyv * y_off)
            dyo16 = (dyv * e).astype(BF16)
            dc_acc = dc_acc + _nn(dyo16, prev16)
            dprev = _tn(dyo16, cm16)
            ds = ds_ref[sl, :]
            ds16 = ds.astype(BF16)
            decay = jnp.exp(last - acs)[:, :n_st]
            bd16 = (bm * decay).astype(BF16)
            dbd = _nn(xd16, ds16)
            dxd = dxd + _nt(bd16, ds16)
            db_acc = db_acc + dbd * decay
            tdec = _row_sums(dbd * bm, 2) * decay
            cd = jnp.exp(last)
            dlast = (jnp.sum(tdec, axis=0, keepdims=True)
                     + jnp.sum(_row_sums(prev * ds, 2), axis=0, keepdims=True) * cd)
            ds_ref[sl, :] = dprev + cd[:, :n_st] * ds
            dskip = hpc_ref[0, 2:3, r:r + 1]
            dxs_ref[:, sl] = dxd * dtv + dskip * dyv
            dacs = dacs - tdec + jnp.where(row == ln - 1, dlast, 0.0)
            dacs_all = jnp.where(lane_r == r, dacs[:, :r_h], dacs_all)
            ddtx_all = jnp.where(lane_r == r, _row_sums(dxd * xs, 2)[:, :r_h], ddtx_all)
            dd_all = jnp.where(lane_r == r, _row_sums(dyv * xs, 2)[:, :r_h], dd_all)
        dc_ref[...] = dc_acc + _nn(dcb.astype(BF16), bm16)
        db_ref[...] = db_acc + _nn(dcbt.astype(BF16), cm16)
        upper = (row <= col).astype(F32)
        dad = _ones_dot(upper, dacs_all, ones_left=True)
        ddt = dad * a_c + ddtx_all
        ddt_raw = ddt * _sigmoid(raw_c)
        ddt_ref[0] = ddt_raw
        d_bias = jnp.sum(ddt_raw, axis=0, keepdims=True)
        d_alog = jnp.sum(dad * dt_c, axis=0, keepdims=True) * a_c
        d_d = jnp.sum(dd_all, axis=0, keepdims=True)
        hg = jnp.concatenate([d_bias, d_alog, d_d], axis=0)

        @pl.when(step == 0)
        def _():
            hg_ref[0] = hg

        @pl.when(step > 0)
        def _():
            hg_ref[0] += hg

    xs, bms, cms, dtcs, dtrs, hpcs, hprs, prevs = _ssd_specs(t, di, g_n, n_st, rp, ln, r_h, True)
    bout = pl.BlockSpec((ln, n_st), lambda g, c: (nc - 1 - c, g))
    outs, side_outs = _call(
        body,
        grid=(g_n, nc),
        in_specs=[xs, bms, cms, dtcs, dtrs, hpcs, hprs, prevs, xs],
        out_specs=[xs, bout, bout, dtcs, hpcs],
        out_shape=[jax.ShapeDtypeStruct((t, di), F32), jax.ShapeDtypeStruct((t, g_n * n_st), F32),
                   jax.ShapeDtypeStruct((t, g_n * n_st), F32), jax.ShapeDtypeStruct((g_n, t, r_h), F32),
                   jax.ShapeDtypeStruct((g_n, 3, r_h), F32)],
        scratch_shapes=[pltpu.VMEM((rp, n_st), F32)],
        sem=("parallel", "arbitrary"),
        name=name,
        args=(xbc, xbc, xbc, dtc, dtr, hpc, hpr, prev, dy),
        side=side,
    )
    return (*outs, side_outs)


SB_KEYS = 256
SB_QUERIES = (512, 256)
SB_CUTOFF = 110.0
SB_PIECES = 2


def _sb_logits(qs, kv, valid):
    z = _nt(qs, kv)
    nz = -z
    lg = jnp.minimum(nz, 0.0) - jnp.log(1.0 + jnp.exp(jnp.minimum(z, nz)))
    return z + lg, (lg if valid is None else jnp.where(valid, lg, 0.0))


def _sb_iota(tq):
    diff = lax.broadcasted_iota(jnp.int32, (tq, SB_KEYS), 1) - lax.broadcasted_iota(jnp.int32, (tq, SB_KEYS), 0)
    krow = lax.broadcasted_iota(jnp.int32, (SB_KEYS, SB_KEYS), 0)
    kcol = lax.broadcasted_iota(jnp.int32, (SB_KEYS, SB_KEYS), 1)
    return diff, krow, kcol


def _sb_scale(d):
    scale = 1.0 / math.sqrt(d)
    assert math.frexp(scale)[0] == 0.5, "the scale is folded into bf16 queries: it must be a power of two"
    return scale


def _key_rows(j):
    return pl.ds(pl.multiple_of(j * SB_KEYS, SB_KEYS), SB_KEYS)


def _sb_fwd(q, k, v, *, side=None, name):
    h, t, d = q.shape
    tq = _tile(t, SB_QUERIES)
    nq = t // tq
    kpq = tq // SB_KEYS
    scale = _sb_scale(d)

    def body(q_ref, k_ref, v_ref, o_ref, lt_ref, first_ref):
        i = pl.program_id(1)
        qs = (q_ref[0].astype(F32) * scale).astype(BF16)
        diff, krow, kcol = _sb_iota(tq)
        later = (krow > kcol).astype(F32)

        def block(j, carry, valid):
            acc, cl = carry
            rows = _key_rows(j)
            ls, lg = _sb_logits(qs, k_ref[0, rows, :], valid)
            cs = _ones_dot(later, lg, ones_left=False, pieces=SB_PIECES)
            att = jnp.exp(ls + (cs + cl))
            if valid is not None:
                att = jnp.where(valid, att, 0.0)
            acc = acc + _nn(att.astype(BF16), v_ref[0, rows, :])
            return acc, cl + (cs[:, 0:1] + lg[:, 0:1])

        carry = (jnp.zeros((tq, d), F32), jnp.zeros((tq, 1), F32))
        for m in range(kpq - 1, -1, -1):
            carry = block(i * kpq + m, carry, diff < -m * SB_KEYS)
        nb = i * kpq

        def more(st):
            s, _, cl = st
            return jnp.logical_and(s < nb, jnp.max(cl) > -SB_CUTOFF)

        def step(st):
            s, acc, cl = st
            acc, cl = block(nb - 1 - s, (acc, cl), None)
            return s + 1, acc, cl

        walked, acc, cl = lax.while_loop(more, step, (jnp.int32(0),) + carry)
        o_ref[0] = acc.astype(o_ref.dtype)
        lt_ref[0] = cl
        first_ref[pl.program_id(0), i] = nb - walked

    qs = pl.BlockSpec((1, tq, d), lambda hh, i: (hh, i, 0))
    ls = pl.BlockSpec((1, tq, 1), lambda hh, i: (hh, i, 0))
    ks = pl.BlockSpec((1, t, d), lambda hh, i: (hh, 0, 0))
    outs, side_outs = _call(
        body,
        grid=(h, nq),
        in_specs=[qs, ks, ks],
        out_specs=[qs, ls, pl.BlockSpec(memory_space=pltpu.SMEM)],
        out_shape=[jax.ShapeDtypeStruct((h, t, d), BF16), jax.ShapeDtypeStruct((h, t, 1), F32),
                   jax.ShapeDtypeStruct((h, nq), jnp.int32)],
        sem=("arbitrary", "arbitrary"),
        name=name,
        args=(q, k, v),
        side=side,
    )
    return (*outs, side_outs)


def _sb_bwd(q, k, v, lt, first, do, *, name):
    h, t, d = q.shape
    tq = _tile(t, SB_QUERIES)
    nq = t // tq
    kpq = tq // SB_KEYS
    scale = _sb_scale(d)
    last = SB_KEYS - 1

    def body(q_ref, k_ref, v_ref, lt_ref, first_ref, do_ref, dq_ref, dk_ref, dv_ref, dk_acc, dv_acc):
        i = pl.program_id(1)

        @pl.when(i == 0)
        def _():
            dk_acc[...] = jnp.zeros_like(dk_acc)
            dv_acc[...] = jnp.zeros_like(dv_acc)

        qs = (q_ref[0].astype(F32) * scale).astype(BF16)
        do16 = do_ref[0].astype(BF16)
        ltot = lt_ref[0]
        diff, krow, kcol = _sb_iota(tq)
        upto = (krow <= kcol).astype(F32)
        before = (krow < kcol).astype(F32)

        def block(j, carry, valid, r0=0):
            dq, pl_sum, pg_sum = carry
            rows = _key_rows(j)
            kv = k_ref[0, rows, :]
            vv = v_ref[0, rows, :]
            ls, lg = _sb_logits(qs[r0:], kv, valid)
            pre = _ones_dot(upto, lg, ones_left=False, pieces=SB_PIECES)
            att = jnp.exp(ls + (ltot[r0:] - (pre + pl_sum)))
            if valid is not None:
                att = jnp.where(valid, att, 0.0)
            g = att * _nt(do16[r0:], vv)
            gpre = _ones_dot(before, g, ones_left=False, pieces=SB_PIECES)
            sig = jnp.exp(ls)
            dz16 = (g - sig * (g + (gpre + pg_sum))).astype(BF16)
            if valid is not None:
                dz16 = jnp.where(valid, dz16, jnp.zeros_like(dz16))
            dq = dq + _nn(dz16, kv)
            dk_acc[rows, :] += _tn(dz16, qs[r0:])
            dv_acc[rows, :] += _tn(att.astype(BF16), do16[r0:])
            return dq, pl_sum + pre[:, last:], pg_sum + (gpre[:, last:] + g[:, last:])

        zero = jnp.zeros((tq, 1), F32)
        nb = i * kpq
        start = jnp.clip(first_ref[pl.program_id(0), i], 0, nb)
        carry = lax.fori_loop(start, nb, lambda j, cr: block(j, cr, None), (jnp.zeros((tq, d), F32), zero, zero))
        for m in range(kpq):
            r0 = m * SB_KEYS
            sub = block(nb + m, tuple(a[r0:] for a in carry), diff[r0:] < -r0, r0)
            carry = tuple(jnp.concatenate([a[:r0], s], axis=0) if r0 else s for a, s in zip(carry, sub))
        dq_ref[0] = (carry[0] * scale).astype(dq_ref.dtype)

        @pl.when(i == nq - 1)
        def _():
            dk_ref[0] = dk_acc[...].astype(dk_ref.dtype)
            dv_ref[0] = dv_acc[...].astype(dv_ref.dtype)

    qs = pl.BlockSpec((1, tq, d), lambda hh, i: (hh, i, 0))
    ls = pl.BlockSpec((1, tq, 1), lambda hh, i: (hh, i, 0))
    ks = pl.BlockSpec((1, t, d), lambda hh, i: (hh, 0, 0))
    full = jax.ShapeDtypeStruct((h, t, d), BF16)
    return pl.pallas_call(
        body,
        grid=(h, nq),
        in_specs=[qs, ks, ks, ls, pl.BlockSpec(memory_space=pltpu.SMEM), qs],
        out_specs=[qs, ks, ks],
        out_shape=[full, full, full],
        scratch_shapes=[pltpu.VMEM((t, d), F32), pltpu.VMEM((t, d), F32)],
        compiler_params=_cparams(("arbitrary", "arbitrary")),
        name=name,
    )(q, k, v, lt, first, do)


def _row_tile(rows, cols):
    return _tile(rows, tuple(r for r in (2048, 1024, 512, 256, 128, 64, 32, 16, 8) if r * cols * 4 <= ADAM_BLOCK_BYTES))


def _sum_leading(x, *, name):
    n, rows, cols = x.shape
    tr = _row_tile(rows, cols)

    def body(x_ref, o_ref):
        acc = x_ref[0].astype(F32)
        for q in range(1, n):
            acc = acc + x_ref[q].astype(F32)
        o_ref[...] = acc

    return pl.pallas_call(
        body,
        grid=(rows // tr,),
        in_specs=[pl.BlockSpec((n, tr, cols), lambda i: (0, i, 0))],
        out_specs=pl.BlockSpec((tr, cols), lambda i: (i, 0)),
        out_shape=jax.ShapeDtypeStruct((rows, cols), F32),
        compiler_params=_cparams(("parallel",)),
        name=name,
    )(x)


def _pair_add(g4h, recv, c, *, out_dtype, name):
    n, _, rows, cols = g4h.shape
    tr = _row_tile(rows, cols)

    def body(c_ref, g_ref, r_ref, o_ref):
        o_ref[...] = (g_ref[...] + r_ref[...]).astype(o_ref.dtype)

    blk = pl.BlockSpec((1, tr, cols), lambda q, i, c_ref: (q, i, 0))
    return pl.pallas_call(
        body,
        grid_spec=pltpu.PrefetchScalarGridSpec(
            num_scalar_prefetch=1,
            grid=(n, rows // tr),
            in_specs=[pl.BlockSpec((1, None, tr, cols), lambda q, i, c_ref: (q, c_ref[0], i, 0)), blk],
            out_specs=blk),
        out_shape=jax.ShapeDtypeStruct((n, rows, cols), out_dtype),
        compiler_params=_cparams(("parallel", "parallel")),
        name=name,
    )(c.reshape(1).astype(jnp.int32), g4h, recv)


ANY = pl.BlockSpec(memory_space=pl.ANY)


def _other_chips(x, y):
    return [(1 - x, y), (x, 1 - y), (1 - x, 1 - y)]


def _gather_chips(shard, *, name):
    def body(x_ref, o_ref, send_sems, recv_sems, local_sem):
        x, y, c = lax.axis_index("x"), lax.axis_index("y"), lax.axis_index("c")
        me = 2 * x + y
        mine = pltpu.make_async_copy(x_ref, o_ref.at[me], local_sem)
        mine.start()
        chips = _other_chips(x, y)
        sends = [pltpu.make_async_remote_copy(src_ref=x_ref, dst_ref=o_ref.at[me], send_sem=send_sems.at[q],
                                              recv_sem=recv_sems.at[q], device_id=(px, py, c), device_id_type=MESH)
                 for q, (px, py) in enumerate(chips)]
        for cp in sends:
            cp.start()
        for q, (px, py) in enumerate(chips):
            pltpu.make_async_remote_copy(src_ref=x_ref, dst_ref=o_ref.at[2 * px + py], send_sem=send_sems.at[q],
                                         recv_sem=recv_sems.at[q], device_id=(px, py, c), device_id_type=MESH).wait_recv()
        for cp in sends:
            cp.wait_send()
        mine.wait()

    return pl.pallas_call(
        body,
        in_specs=[ANY],
        out_specs=ANY,
        out_shape=jax.ShapeDtypeStruct((4,) + shard.shape, shard.dtype),
        scratch_shapes=[pltpu.SemaphoreType.DMA((3,)), pltpu.SemaphoreType.DMA((3,)), pltpu.SemaphoreType.DMA],
        compiler_params=pltpu.CompilerParams(has_side_effects=True),
        name=name,
    )(shard)


def _comm_call(body, ins, out_shapes, n_sems, name):
    n = len(ins)

    def wrapped(*refs):
        body(refs[:n], refs[n:n + len(out_shapes)], refs[-2], refs[-1])

    return pl.pallas_call(
        wrapped,
        in_specs=[ANY] * n,
        out_specs=[ANY] * len(out_shapes),
        out_shape=out_shapes,
        scratch_shapes=[pltpu.SemaphoreType.DMA((n_sems,)), pltpu.SemaphoreType.DMA((n_sems,))],
        compiler_params=pltpu.CompilerParams(has_side_effects=True),
        name=name,
    )(*ins)


def _remote(send_sems, recv_sems, q, src, dst, to):
    return pltpu.make_async_remote_copy(src_ref=src, dst_ref=dst, send_sem=send_sems.at[q], recv_sem=recv_sems.at[q],
                                        device_id=to, device_id_type=MESH)


def _scatter_chips(parts, *, name):
    return _run_job(_scatter_job(parts), name)


def _scatter_job(parts):
    def sends(ins, outs, send_sems, recv_sems):
        x, y, c = lax.axis_index("x"), lax.axis_index("y"), lax.axis_index("c")
        return [_remote(send_sems, recv_sems, 3 * i + q, p.at[2 * px + py], o.at[2 * x + y], (px, py, c))
                for i, (p, o) in enumerate(zip(ins, outs)) for q, (px, py) in enumerate(_other_chips(x, y))]

    def start(ins, outs, send_sems, recv_sems):
        for cp in sends(ins, outs, send_sems, recv_sems):
            cp.start()

    def finish(ins, outs, send_sems, recv_sems):
        x, y, c = lax.axis_index("x"), lax.axis_index("y"), lax.axis_index("c")
        for i, (p, o) in enumerate(zip(ins, outs)):
            for q, (px, py) in enumerate(_other_chips(x, y)):
                _remote(send_sems, recv_sems, 3 * i + q, p.at[2 * x + y], o.at[2 * px + py], (px, py, c)).wait_recv()
        for cp in sends(ins, outs, send_sems, recv_sems):
            cp.wait_send()

    return _SideJob(parts, [jax.ShapeDtypeStruct(p.shape, p.dtype) for p in parts], 3 * len(parts), start, finish)


def _run_job(job, name):
    return _comm_call(lambda *refs: (job.start(*refs), job.finish(*refs)), job.ins, job.out_shapes, job.n_sems, name)


def _gather_job(shards):
    def sends(ins, outs, send_sems, recv_sems):
        x, y, c = lax.axis_index("x"), lax.axis_index("y"), lax.axis_index("c")
        return [_remote(send_sems, recv_sems, 6 * i + q, s.at[c], o.at[2 * x + y, c], (px, py, c))
                for i, (s, o) in enumerate(zip(ins, outs)) for q, (px, py) in enumerate(_other_chips(x, y))]

    def start(ins, outs, send_sems, recv_sems):
        for cp in sends(ins, outs, send_sems, recv_sems):
            cp.start()

    def finish(ins, outs, send_sems, recv_sems):
        x, y, c = lax.axis_index("x"), lax.axis_index("y"), lax.axis_index("c")
        sibling = (x, y, 1 - c)
        chips = _other_chips(x, y)
        copy = lambda q, src, dst, to: _remote(send_sems, recv_sems, q, src, dst, to)
        passed = []
        for i, (s, o) in enumerate(zip(ins, outs)):
            for q, (px, py) in enumerate(chips):
                slot = o.at[2 * px + py, c]
                copy(6 * i + q, s.at[c], slot, (px, py, c)).wait_recv()
                passed.append(copy(6 * i + 3 + q, slot, slot, sibling))
                passed[-1].start()
        for i, (s, o) in enumerate(zip(ins, outs)):
            for q, (px, py) in enumerate(chips):
                copy(6 * i + 3 + q, s.at[1 - c], o.at[2 * px + py, 1 - c], sibling).wait_recv()
        for cp in sends(ins, outs, send_sems, recv_sems) + passed:
            cp.wait_send()

    return _SideJob(shards, [jax.ShapeDtypeStruct((N_CHIPS,) + s.shape, s.dtype) for s in shards], 6 * len(shards),
                    start, finish)


def _swap_job(gs):
    def copies(ins, outs, send_sems, recv_sems):
        x, y, c = lax.axis_index("x"), lax.axis_index("y"), lax.axis_index("c")
        return [_remote(send_sems, recv_sems, i, g.at[pl.ds(0, g.shape[0]), 1 - c], o, (x, y, 1 - c))
                for i, (g, o) in enumerate(zip(ins, outs))]

    def start(*refs):
        for cp in copies(*refs):
            cp.start()

    def finish(*refs):
        for cp in copies(*refs):
            cp.wait()

    return _SideJob(gs, [jax.ShapeDtypeStruct((g.shape[0],) + g.shape[2:], g.dtype) for g in gs], len(gs), start, finish)


def _join_halves(halves, *, name):
    def body(ins, outs, send_sems, recv_sems):
        x, y, c = lax.axis_index("x"), lax.axis_index("y"), lax.axis_index("c")
        sibling = (x, y, 1 - c)
        sends = [_remote(send_sems, recv_sems, i, h, o.at[c], sibling) for i, (h, o) in enumerate(zip(ins, outs))]
        for cp in sends:
            cp.start()
        for i, (h, o) in enumerate(zip(ins, outs)):
            _remote(send_sems, recv_sems, i, h, o.at[1 - c], sibling).wait_recv()
        for cp in sends:
            cp.wait_send()

    return _comm_call(body, halves, [jax.ShapeDtypeStruct((2,) + h.shape, h.dtype) for h in halves], len(halves), name)


def _gather_all(v, *, name):
    def body(v_ref, o_ref, send_sems, recv_sems, local_sem):
        x, y, c = lax.axis_index("x"), lax.axis_index("y"), lax.axis_index("c")
        me = 4 * x + 2 * y + c
        mine = pltpu.make_async_copy(v_ref, o_ref.at[me], local_sem)
        mine.start()
        peers = [(x ^ (q >> 2 & 1), y ^ (q >> 1 & 1), c ^ (q & 1)) for q in range(1, 8)]
        sends = [pltpu.make_async_remote_copy(src_ref=v_ref, dst_ref=o_ref.at[me], send_sem=send_sems.at[q],
                                              recv_sem=recv_sems.at[q], device_id=peer, device_id_type=MESH)
                 for q, peer in enumerate(peers)]
        for cp in sends:
            cp.start()
        for q, (px, py, pc) in enumerate(peers):
            pltpu.make_async_remote_copy(src_ref=v_ref, dst_ref=o_ref.at[4 * px + 2 * py + pc], send_sem=send_sems.at[q],
                                         recv_sem=recv_sems.at[q], device_id=(px, py, pc), device_id_type=MESH).wait_recv()
        for cp in sends:
            cp.wait_send()
        mine.wait()

    return pl.pallas_call(
        body,
        in_specs=[ANY],
        out_specs=ANY,
        out_shape=jax.ShapeDtypeStruct((8,) + v.shape, v.dtype),
        scratch_shapes=[pltpu.SemaphoreType.DMA((7,)), pltpu.SemaphoreType.DMA((7,)), pltpu.SemaphoreType.DMA],
        compiler_params=pltpu.CompilerParams(has_side_effects=True),
        name=name,
    )(v)


WEIGHTS = ['ssm_norm_w', 'ssm_in_w', 'ssm_conv_w', 'ssm_conv_b', 'ssm_dt_bias', 'ssm_a_log', 'ssm_d',
           'ssm_gate_norm_w', 'ssm_out_w', 'kv_norm_w', 'w_k', 'w_v', 'attn_norm_w', 'w_q', 'w_o',
           'ffn_norm_w', 'ffn_up_w', 'ffn_conv_w', 'ffn_conv_b', 'ffn_down_w', 'final_norm_w']
SHARD_AXIS = {'ssm_norm_w': 1, 'ssm_in_w': 2, 'ssm_conv_w': 2, 'ssm_conv_b': 1, 'ssm_gate_norm_w': 1,
              'ssm_out_w': 1, 'w_k': 0, 'w_v': 0, 'w_q': 1, 'w_o': 1, 'ffn_up_w': 2, 'ffn_conv_w': 2,
              'ffn_down_w': 1}
BIG = ['ssm_in_w', 'ssm_out_w', 'w_k', 'w_v', 'w_q', 'w_o', 'ffn_up_w', 'ffn_down_w']
SMALL = [n for n in WEIGHTS if n in SHARD_AXIS and n not in BIG]
REPLICATED = [n for n in WEIGHTS if n not in SHARD_AXIS]
STACKED = ['ffn_up_w', 'ffn_down_w']
N_CHIPS = 4


PACK_ROWS = 16


def _piece_rows(n):
    return -(-n // (PACK_ROWS * LANES)) * PACK_ROWS


def _pack(arrs, dtype, row_mult):
    lead = arrs[0].shape[:-1]
    pieces, total = [], 0
    for a in arrs:
        n = a.shape[-1]
        rows = _piece_rows(n)
        a = a.astype(dtype)
        if rows * LANES != n:
            a = jnp.pad(a, [(0, 0)] * len(lead) + [(0, rows * LANES - n)])
        pieces.append(a.reshape(lead + (rows, LANES)))
        total += rows
    extra = -total % row_mult
    if extra:
        pieces.append(jnp.zeros(lead + (extra, LANES), dtype))
    return jnp.concatenate(pieces, axis=len(lead))


def _unpack(buf, shapes):
    lead = buf.shape[:-2]
    out, off = [], 0
    for shp in shapes:
        n = math.prod(shp)
        rows = _piece_rows(n)
        piece = lax.slice_in_dim(buf, off, off + rows, axis=len(lead)).reshape(lead + (rows * LANES,))
        out.append(piece[..., :n].reshape(lead + tuple(shp)))
        off += rows
    return out


def _set_slot(buf, piece, index):
    return lax.dynamic_update_slice_in_dim(buf, piece[None], index, axis=0)


def _from_shards(stacked, axis):
    return jnp.concatenate([stacked[j] for j in range(N_CHIPS)], axis=axis)


def _heads(a, h):
    t = a.shape[0]
    return a.reshape(t, h, a.shape[1] // h).transpose(1, 0, 2)


def _unheads(a):
    h, t, d = a.shape
    return a.transpose(1, 0, 2).reshape(t, h * d)


def _ffn_fwd(h, norm_w, w_up, conv_w, conv_b, w_down, tag, side=None):
    u = _rmsnorm_fwd(h, norm_w, name=f"ffn{tag}_norm")
    hid = _matmul(u, w_up, name=f"ffn{tag}_up")
    act, side_outs = _conv_glu_fwd(hid, conv_w, conv_b, side=side, name=f"ffn{tag}_glu")
    out = _matmul(act, w_down, add=h, name=f"ffn{tag}_down")
    return out, (u, hid, act), side_outs


def _ffn_bwd(h, saved, dout, norm_w, w_up, conv_w, conv_b, w_down, tag):
    u, hid, act = saved
    dact = _matmul(dout, w_down, tb=True, name=f"ffn{tag}_down_dx")
    dw_down = _matmul(act, dout, ta=True, name=f"ffn{tag}_down_dw")
    dhid, dwg, dwv, dbg, dbv = _conv_glu_bwd(hid, conv_w, conv_b, dact, name=f"ffn{tag}_glu_bwd")
    du = _matmul(dhid, w_up, tb=True, name=f"ffn{tag}_up_dx")
    dw_up = _matmul(u, dhid, ta=True, out_parts=N_CHIPS, name=f"ffn{tag}_up_dw")
    dh, (dnorm,) = _rmsnorm_bwd(h, [(du, norm_w)], dout, name=f"ffn{tag}_norm_bwd")
    return dh, dict(norm=dnorm[0], up=dw_up, conv_w=jnp.concatenate([dwg, dwv], axis=1),
                    conv_b=jnp.concatenate([dbg, dbv], axis=1)[0], down=dw_down)


class _Pieces:
    def __init__(self, local):
        self.c = lax.axis_index("c")
        self.chip = 2 * lax.axis_index("x") + lax.axis_index("y")
        self.shape, self.s16 = {}, {}
        for n in BIG:
            blk = local[n]
            layers = [(n, l, blk[l]) for l in range(blk.shape[0])] if n in STACKED else [(n, None, blk.reshape(blk.shape[-2:]))]
            for name, l, p in layers:
                self.shape[name, l] = p.shape
                self.s16[name, l] = p.astype(BF16).reshape(2, p.shape[0] // 2, p.shape[1])

    def gather_job(self, keys):
        return _gather_job([self.s16[k] for k in keys])

    def weights(self, keys, gathered):
        out = []
        for k, g in zip(keys, gathered):
            r, cc = self.shape[k]
            by_chip = _set_slot(g, self.s16[k], self.chip).reshape(N_CHIPS, r, cc)
            if k[0] == 'ssm_in_w':
                by_chip = by_chip.transpose(1, 0, 2).reshape(r, N_CHIPS * cc)
            elif k[0] != 'ffn_up_w':
                by_chip = by_chip.reshape(N_CHIPS * r, cc)
            out.append(by_chip)
        return out

    def by_halves(self, keys, grads):
        gs = []
        for k, g in zip(keys, grads):
            r, cc = self.shape[k]
            if k[0] == 'ssm_in_w':
                g = g.reshape(r, N_CHIPS, cc).transpose(1, 0, 2)
            gs.append(g.reshape(N_CHIPS, 2, r // 2, cc))
        return gs

    def pair_sums(self, gs, recv, tag):
        return [_pair_add(g, rv, self.c, out_dtype=BF16, name=f"rs_pair_add_{tag}{i}") for i, (g, rv) in enumerate(zip(gs, recv))]

    def chip_sums(self, pairs, scattered, tag):
        return [_sum_leading(_set_slot(s, lax.dynamic_index_in_dim(p, self.chip, axis=0, keepdims=False), self.chip),
                             name=f"rs_chip_sum_{tag}{i}") for i, (s, p) in enumerate(zip(scattered, pairs))]

    def shards(self, keys, halves):
        joined = _join_halves(halves, name="rs_half_join")
        return {k: _set_slot(j, h, self.c).reshape(self.shape[k]) for k, h, j in zip(keys, halves, joined)}


def _step(x, target, w, pieces):
    t = x.shape[0]
    g_n, heads = SSM_GROUPS, SSM_HEADS
    r_h = heads // g_n
    di = D_INNER
    zx_cols = di + CONV_DIM
    k_in = [('ssm_in_w', None)]
    k_ffn0 = [('ssm_out_w', None), ('ffn_up_w', 0), ('ffn_down_w', 0)]
    k_qkv = [('w_k', None), ('w_v', None), ('w_q', None)]
    k_late = [('w_o', None), ('ffn_up_w', 1), ('ffn_down_w', 1)]
    (w_in,) = pieces.weights(k_in, _run_job(pieces.gather_job(k_in), "gather_ssm_in"))
    w_zx = w_in[:, :zx_cols]
    w_dt = jnp.pad(w_in[:, zx_cols:], ((0, 0), (0, LANES - heads)))
    conv_w, conv_b = w['ssm_conv_w'][0], w['ssm_conv_b'][0]
    hp = jnp.stack([w['ssm_dt_bias'][0], w['ssm_a_log'][0], w['ssm_d'][0]], axis=0).reshape(3, g_n, r_h)
    hpc, hpr = hp.transpose(1, 0, 2), hp.transpose(1, 2, 0)

    h0 = x
    u0 = _rmsnorm_fwd(h0, w['ssm_norm_w'][0], name="ssm_norm")
    zx = _matmul(u0, w_zx, name="ssm_in_zx")
    dt_raw = _matmul(u0, w_dt, name="ssm_in_dt")[:, :heads]
    dtg = dt_raw.reshape(t, g_n, r_h)
    dtc, dtr = dtg.transpose(1, 0, 2), dtg.transpose(1, 2, 0)
    xbc = _conv_silu_fwd(zx, conv_w, conv_b, x_off=di, name="ssm_conv")
    y, prev, got = _ssd_fwd(xbc, dtc, dtr, hpc, hpr, side=pieces.gather_job(k_ffn0), name="ssd_fwd")
    w_out, w_up0, w_down0 = pieces.weights(k_ffn0, got)
    yn = _gate_norm_fwd(y, zx, w['ssm_gate_norm_w'][0], name="ssm_gate_norm")
    h1 = _matmul(yn, w_out, add=h0, name="ssm_out")
    h2, ffn0, got = _ffn_fwd(h1, w['ffn_norm_w'][0], w_up0, w['ffn_conv_w'][0], w['ffn_conv_b'][0], w_down0, 0,
                             side=pieces.gather_job(k_qkv))
    w_k, w_v, w_q = pieces.weights(k_qkv, got)
    hk = _rmsnorm_fwd(h2, w['kv_norm_w'], name="kv_norm")
    qn = _rmsnorm_fwd(h2, w['attn_norm_w'][0], name="attn_norm")
    k2 = _matmul(hk, w_k, out_dtype=BF16, name="attn_k")
    v2 = _matmul(hk, w_v, out_dtype=BF16, name="attn_v")
    q2 = _matmul(qn, w_q, out_dtype=BF16, name="attn_q")
    qh, kh, vh = _heads(q2, SB_HEADS), _heads(k2, SB_HEADS), _heads(v2, SB_HEADS)
    oh, lt, first, got = _sb_fwd(qh, kh, vh, side=pieces.gather_job(k_late), name="sb_fwd")
    w_o, w_up1, w_down1 = pieces.weights(k_late, got)
    o2 = _unheads(oh)
    h3 = _matmul(o2, w_o, add=h2, name="attn_o")
    h4, ffn1, _ = _ffn_fwd(h3, w['ffn_norm_w'][1], w_up1, w['ffn_conv_w'][1], w['ffn_conv_b'][1], w_down1, 1)
    loss_p, dh4, d_final = _loss_head(h4, w['final_norm_w'], target, name="loss_head")

    dh3, g1 = _ffn_bwd(h3, ffn1, dh4, w['ffn_norm_w'][1], w_up1, w['ffn_conv_w'][1], w['ffn_conv_b'][1], w_down1, 1)
    do2 = _matmul(dh3, w_o, tb=True, out_dtype=BF16, name="attn_o_dx")
    dw_o = _matmul(o2, dh3, ta=True, name="attn_o_dw")
    dqh, dkh, dvh = _sb_bwd(qh, kh, vh, lt, first, _heads(do2, SB_HEADS), name="sb_bwd")
    dq2, dk2, dv2 = _unheads(dqh), _unheads(dkh), _unheads(dvh)
    dqn = _matmul(dq2, w_q, tb=True, name="attn_q_dx")
    dw_q = _matmul(qn, dq2, ta=True, name="attn_q_dw")
    dhk = _matmul(dk2, w_k, tb=True, name="attn_k_dx")
    dhk = _matmul(dv2, w_v, tb=True, add=dhk, name="attn_v_dx")
    dw_k = _matmul(hk, dk2, ta=True, name="attn_k_dw")
    dw_v = _matmul(hk, dv2, ta=True, name="attn_v_dw")
    dh2, (d_attn_norm, d_kv_norm) = _rmsnorm_bwd(h2, [(dqn, w['attn_norm_w'][0]), (dhk, w['kv_norm_w'])], dh3,
                                                 name="attn_norms_bwd")
    dh1, g0 = _ffn_bwd(h1, ffn0, dh2, w['ffn_norm_w'][0], w_up0, w['ffn_conv_w'][0], w['ffn_conv_b'][0], w_down0, 0)
    dyn = _matmul(dh1, w_out, tb=True, name="ssm_out_dx")
    dw_out = _matmul(yn, dh1, ta=True, name="ssm_out_dw")
    k_done = k_qkv + k_late + k_ffn0
    gs_done = pieces.by_halves(k_done, [dw_k, dw_v, dw_q, dw_o, g1['up'], g1['down'], dw_out, g0['up'], g0['down']])
    dy, dz, d_gate, recv = _gate_norm_bwd(y, zx, w['ssm_gate_norm_w'][0], dyn, side=_swap_job(gs_done),
                                          name="ssm_gate_norm_bwd")
    pairs_done = pieces.pair_sums(gs_done, recv, "a")
    dxs, dbm, dcm, ddt_g, hg, scattered_done = _ssd_bwd(xbc, dtc, dtr, hpc, hpr, prev, dy,
                                                        side=_scatter_job(pairs_done), name="ssd_bwd")
    dzx, d_conv_w, d_conv_b = _conv_silu_bwd(zx, conv_w, conv_b, [dxs, dbm, dcm], x_off=di, into=dz, name="ssm_conv_bwd")
    ddt = jnp.pad(ddt_g.transpose(1, 0, 2).reshape(t, heads), ((0, 0), (0, LANES - heads)))
    du0 = _matmul(dzx, w_zx, tb=True, name="ssm_in_zx_dx")
    du0 = _matmul(ddt, w_dt, tb=True, add=du0, name="ssm_in_dt_dx")
    dw_in = jnp.concatenate([_matmul(u0, dzx, ta=True, name="ssm_in_zx_dw"),
                             _matmul(u0, ddt, ta=True, name="ssm_in_dt_dw")[:, :heads]], axis=1)
    dx, (d_ssm_norm,) = _rmsnorm_bwd(h0, [(du0, w['ssm_norm_w'][0])], dh1, name="ssm_norm_bwd")

    gs_in = pieces.by_halves(k_in, [dw_in])
    pairs_in = pieces.pair_sums(gs_in, _run_job(_swap_job(gs_in), "rs_pair_swap_b"), "b")
    halves = (pieces.chip_sums(pairs_done, scattered_done, "a")
              + pieces.chip_sums(pairs_in, _scatter_chips(pairs_in, name="rs_chip_scatter_b"), "b"))
    big_grads = pieces.shards(k_done + k_in, halves)

    hgr = hg.transpose(1, 0, 2).reshape(3, heads)
    grads = {
        'ssm_norm_w': d_ssm_norm, 'ssm_conv_w': d_conv_w[None], 'ssm_conv_b': d_conv_b,
        'ssm_dt_bias': hgr[0:1], 'ssm_a_log': hgr[1:2], 'ssm_d': hgr[2:3], 'ssm_gate_norm_w': d_gate,
        'kv_norm_w': d_kv_norm[0], 'attn_norm_w': d_attn_norm, 'ffn_norm_w': jnp.stack([g0['norm'], g1['norm']]),
        'ffn_conv_w': jnp.stack([g0['conv_w'], g1['conv_w']]), 'ffn_conv_b': jnp.stack([g0['conv_b'], g1['conv_b']]),
        'final_norm_w': d_final[0],
    }
    return loss_p, dx, grads, big_grads


def kernel(x, ssm_norm_w, ssm_in_w, ssm_conv_w, ssm_conv_b, ssm_dt_bias, ssm_a_log, ssm_d, ssm_gate_norm_w, ssm_out_w, kv_norm_w, w_k, w_v, attn_norm_w, w_q, w_o, ffn_norm_w, ffn_up_w, ffn_conv_w, ffn_conv_b, ffn_down_w, final_norm_w, loss_target, m_ssm_norm_w, m_ssm_in_w, m_ssm_conv_w, m_ssm_conv_b, m_ssm_dt_bias, m_ssm_a_log, m_ssm_d, m_ssm_gate_norm_w, m_ssm_out_w, m_kv_norm_w, m_w_k, m_w_v, m_attn_norm_w, m_w_q, m_w_o, m_ffn_norm_w, m_ffn_up_w, m_ffn_conv_w, m_ffn_conv_b, m_ffn_down_w, m_final_norm_w, v_ssm_norm_w, v_ssm_in_w, v_ssm_conv_w, v_ssm_conv_b, v_ssm_dt_bias, v_ssm_a_log, v_ssm_d, v_ssm_gate_norm_w, v_ssm_out_w, v_kv_norm_w, v_w_k, v_w_v, v_attn_norm_w, v_w_q, v_w_o, v_ffn_norm_w, v_ffn_up_w, v_ffn_conv_w, v_ffn_conv_b, v_ffn_down_w, v_final_norm_w):
    args = (ssm_norm_w, ssm_in_w, ssm_conv_w, ssm_conv_b, ssm_dt_bias, ssm_a_log, ssm_d, ssm_gate_norm_w, ssm_out_w, kv_norm_w, w_k, w_v, attn_norm_w, w_q, w_o, ffn_norm_w, ffn_up_w, ffn_conv_w, ffn_conv_b, ffn_down_w, final_norm_w)
    moms = (m_ssm_norm_w, m_ssm_in_w, m_ssm_conv_w, m_ssm_conv_b, m_ssm_dt_bias, m_ssm_a_log, m_ssm_d, m_ssm_gate_norm_w, m_ssm_out_w, m_kv_norm_w, m_w_k, m_w_v, m_attn_norm_w, m_w_q, m_w_o, m_ffn_norm_w, m_ffn_up_w, m_ffn_conv_w, m_ffn_conv_b, m_ffn_down_w, m_final_norm_w)
    vels = (v_ssm_norm_w, v_ssm_in_w, v_ssm_conv_w, v_ssm_conv_b, v_ssm_dt_bias, v_ssm_a_log, v_ssm_d, v_ssm_gate_norm_w, v_ssm_out_w, v_kv_norm_w, v_w_k, v_w_v, v_attn_norm_w, v_w_q, v_w_o, v_ffn_norm_w, v_ffn_up_w, v_ffn_conv_w, v_ffn_conv_b, v_ffn_down_w, v_final_norm_w)
    local = dict(zip(WEIGHTS, args))
    m_in = dict(zip(WEIGHTS, moms))
    v_in = dict(zip(WEIGHTS, vels))
    chip = 2 * lax.axis_index("x") + lax.axis_index("y")

    full = {n: local[n] for n in REPLICATED}
    small32 = _gather_chips(_pack([local[n].reshape(-1) for n in SMALL], F32, 8), name="gather_small")
    for n, st in zip(SMALL, _unpack(small32, [local[n].shape for n in SMALL])):
        full[n] = _from_shards(st, SHARD_AXIS[n])

    pieces = _Pieces(local)
    loss_p, dx, grads, big_grads = _step(x[0], loss_target[0], full, pieces)
    gshard = {}
    for n in BIG:
        if n in STACKED:
            gshard[n] = [big_grads[n, l] for l in range(local[n].shape[0])]
        else:
            gshard[n] = big_grads[n, None].reshape(local[n].shape)

    small = SMALL + REPLICATED
    rep = _pack([loss_p.reshape(-1)] + [grads[n].reshape(-1) for n in small], F32, 8)
    tot = _sum_leading(_gather_all(rep, name="ar_gather"), name="ar_sum")
    parts = _unpack(tot, [(LANES,)] + [grads[n].shape for n in small])
    loss = jnp.sum(parts[0])
    for n, g in zip(small, parts[1:]):
        if n in SHARD_AXIS:
            size = local[n].shape[SHARD_AXIS[n]]
            g = lax.dynamic_slice_in_dim(g, chip * size, size, axis=SHARD_AXIS[n])
        gshard[n] = g

    grads_out, deltas, new_m, new_v = [], [], [], []
    for n in WEIGHTS:
        if n in STACKED:
            g, d, nm, nv = _adamw_layers(local[n], gshard[n], m_in[n], v_in[n], name=f"adamw_{n}")
        else:
            g = gshard[n]
            d, nm, nv = _adamw(local[n], g, m_in[n], v_in[n], name=f"adamw_{n}")
        grads_out.append(g)
        deltas.append(d)
        new_m.append(nm)
        new_v.append(nv)
    return (loss, dx[None], *grads_out, *deltas, *new_m, *new_v)
```

```python
import functools
import math

import jax
import jax.numpy as jnp
from jax import lax
from jax.experimental import pallas as pl
from jax.experimental.pallas import tpu as pltpu

D_INNER = 2048
SSM_HEAD_DIM = 64
SSM_HEADS = 32
SSM_GROUPS = 4
SSM_STATE = 128
SSM_CHUNK = 128
GN = SSM_GROUPS * SSM_STATE
CONV_DIM = D_INNER + 2 * GN
SB_HEADS = 16
EPS = 1e-6
ADAM_LR = 0.001
ADAM_B1 = 0.9
ADAM_B2 = 0.999
ADAM_EPS = 1e-08
ADAM_WD = 0.01
ADAM_STEP = 10

LANES = 128
SUBLANES = 8
VMEM_LIMIT = 48 * 1024 * 1024
ADAM_BLOCK_BYTES = 1 << 20
F32 = jnp.float32
BF16 = jnp.bfloat16
MESH = pl.DeviceIdType.MESH


def _cparams(sem=None):
    return pltpu.CompilerParams(dimension_semantics=sem, vmem_limit_bytes=VMEM_LIMIT)


class _SideJob:
    def __init__(self, ins, out_shapes, n_sems, start, finish):
        self.ins, self.out_shapes, self.n_sems, self.start, self.finish = ins, out_shapes, n_sems, start, finish


def _call(body, *, grid, in_specs, out_specs, out_shape, scratch_shapes=(), sem, name, args, side=None):
    in_specs, out_specs, out_shape, scratch_shapes = list(in_specs), list(out_specs), list(out_shape), list(scratch_shapes)
    n_in, n_out = len(in_specs), len(out_specs)
    if side is None:
        outs = pl.pallas_call(body, grid=grid, in_specs=in_specs, out_specs=out_specs, out_shape=out_shape,
                              scratch_shapes=scratch_shapes, compiler_params=_cparams(sem), name=name)(*args)
        return list(outs), []
    k_in, k_out = len(side.ins), len(side.out_shapes)

    def wrapped(*refs):
        ins, s_ins = refs[:n_in], refs[n_in:n_in + k_in]
        o0 = n_in + k_in
        outs, s_outs = refs[o0:o0 + n_out], refs[o0 + n_out:o0 + n_out + k_out]
        scratch, send_sems, recv_sems = refs[o0 + n_out + k_out:-2], refs[-2], refs[-1]
        ids = [pl.program_id(a) for a in range(len(grid))]
        first = functools.reduce(jnp.logical_and, [p == 0 for p in ids])
        last = functools.reduce(jnp.logical_and, [p == g - 1 for p, g in zip(ids, grid)])

        @pl.when(first)
        def _():
            side.start(s_ins, s_outs, send_sems, recv_sems)

        body(*ins, *outs, *scratch)

        @pl.when(last)
        def _():
            side.finish(s_ins, s_outs, send_sems, recv_sems)

    outs = pl.pallas_call(
        wrapped, grid=grid, in_specs=in_specs + [ANY] * k_in, out_specs=out_specs + [ANY] * k_out,
        out_shape=out_shape + list(side.out_shapes),
        scratch_shapes=scratch_shapes + [pltpu.SemaphoreType.DMA((side.n_sems,)), pltpu.SemaphoreType.DMA((side.n_sems,))],
        compiler_params=_cparams(tuple("arbitrary" for _ in grid)), name=name)(*args, *side.ins)
    return list(outs[:n_out]), list(outs[n_out:])


def _tile(n, cands):
    for c in cands:
        if n % c == 0:
            return c
    return n


def _nt(a, b):
    return lax.dot_general(a, b, (((1,), (1,)), ((), ())), preferred_element_type=F32)


def _tn(a, b):
    return lax.dot_general(a, b, (((0,), (0,)), ((), ())), preferred_element_type=F32)


def _nn(a, b):
    return jnp.dot(a, b, preferred_element_type=F32)


def _split(x, pieces):
    out = []
    for _ in range(pieces - 1):
        h = x.astype(BF16)
        out.append(h)
        x = x - h.astype(F32)
    out.append(x.astype(BF16))
    return out


def _ones_dot(ones, x, *, ones_left, pieces=3):
    o16 = ones.astype(BF16)
    acc = None
    for piece in _split(x, pieces):
        term = _nn(o16, piece) if ones_left else _nn(piece, o16)
        acc = term if acc is None else acc + term
    return acc


def _row_sums(x, pieces=3):
    return _ones_dot(jnp.ones((x.shape[1], LANES), F32), x, ones_left=False, pieces=pieces)


def _softplus(x):
    return jnp.maximum(x, 0.0) + jnp.log(1.0 + jnp.exp(-jnp.abs(x)))


def _sigmoid(x):
    return 0.5 * jnp.tanh(0.5 * x) + 0.5


MM_TILE_MAX = 1408
MM_VMEM_BUDGET = 40 * 1024 * 1024


def _divisors(n, cap):
    out = [d for d in range(min(cap, n) // LANES * LANES, 0, -LANES) if n % d == 0]
    return out or [n]


def _mm_tiles(m, n, k, a_bytes, b_bytes, o_bytes, add_bytes):
    best = None
    for tm in _divisors(m, MM_TILE_MAX):
        for tn in _divisors(n, MM_TILE_MAX):
            for tk in _divisors(k, MM_TILE_MAX):
                vmem = 2 * (tm * tk * a_bytes + tk * tn * b_bytes + tm * tn * (o_bytes + add_bytes)) + tm * tn * 4
                if vmem > MM_VMEM_BUDGET:
                    continue
                score = (tm * tn * tk, tm * tn)
                if best is None or score > best[0]:
                    best = (score, (tm, tn, tk))
    return best[1]


def _matmul(a, b, *, ta=False, tb=False, add=None, out_dtype=F32, out_parts=1, name):
    a_parts = a.shape[0] if a.ndim == 3 else 1
    b_parts = b.shape[0] if b.ndim == 3 else 1
    assert not (ta and a_parts > 1)
    a2, b2 = a.shape[-2:], b.shape[-2:]
    m, k = (a2[1], a2[0]) if ta else (a2[0], a2[1] * a_parts)
    n, kb = (b2[0], b2[1] * b_parts) if tb else (b2[1] * b_parts, b2[0])
    assert kb == k, (a.shape, b.shape)
    n_unit = math.gcd(n // out_parts, n if tb else b2[1])
    k_unit = math.gcd(k // a_parts, b2[1] if tb else k)
    tm, tn, tk = _mm_tiles(m, n_unit, k_unit, a.dtype.itemsize, b.dtype.itemsize, jnp.dtype(out_dtype).itemsize,
                           0 if add is None else add.dtype.itemsize)
    nk = k // tk
    ka, kbp = (k // a_parts) // tk, (k // b_parts) // tk
    nb, no = (n // b_parts) // tn, (n // out_parts) // tn

    def body(*refs):
        if add is None:
            a_ref, b_ref, o_ref = refs[:3]
            add_ref = None
        else:
            a_ref, b_ref, add_ref, o_ref = refs[:4]
        kk = pl.program_id(2)
        dn = (((0 if ta else 1,), (1 if tb else 0,)), ((), ()))
        prod = lax.dot_general(a_ref[...].astype(BF16), b_ref[...].astype(BF16), dn, preferred_element_type=F32)

        def finish(r):
            if add_ref is not None:
                r = r + add_ref[...].astype(F32)
            o_ref[...] = r.astype(o_ref.dtype)

        if nk == 1:
            finish(prod)
            return
        acc_ref = refs[-1]

        @pl.when(kk == 0)
        def _():
            acc_ref[...] = prod

        @pl.when(jnp.logical_and(kk > 0, kk < nk - 1))
        def _():
            acc_ref[...] += prod

        @pl.when(kk == nk - 1)
        def _():
            finish(acc_ref[...] + prod)

    if ta:
        a_spec = pl.BlockSpec((tk, tm), lambda i, j, kk: (kk, i))
    elif a_parts > 1:
        a_spec = pl.BlockSpec((None, tm, tk), lambda i, j, kk: (kk // ka, i, kk % ka))
    else:
        a_spec = pl.BlockSpec((tm, tk), lambda i, j, kk: (i, kk))
    if b_parts == 1:
        b_spec = pl.BlockSpec((tn, tk), lambda i, j, kk: (j, kk)) if tb else pl.BlockSpec((tk, tn), lambda i, j, kk: (kk, j))
    elif tb:
        b_spec = pl.BlockSpec((None, tn, tk), lambda i, j, kk: (kk // kbp, j, kk % kbp))
    else:
        b_spec = pl.BlockSpec((None, tk, tn), lambda i, j, kk: (j // nb, kk, j % nb))
    if out_parts > 1:
        o_spec = pl.BlockSpec((None, tm, tn), lambda i, j, kk: (j // no, i, j % no))
        o_shape = jax.ShapeDtypeStruct((out_parts, m, n // out_parts), out_dtype)
    else:
        o_spec = pl.BlockSpec((tm, tn), lambda i, j, kk: (i, j))
        o_shape = jax.ShapeDtypeStruct((m, n), out_dtype)
    in_specs = [a_spec, b_spec]
    args = [a, b]
    if add is not None:
        in_specs.append(pl.BlockSpec((tm, tn), lambda i, j, kk: (i, j)))
        args.append(add)
    return pl.pallas_call(
        body,
        grid=(m // tm, n // tn, nk),
        in_specs=in_specs,
        out_specs=o_spec,
        out_shape=o_shape,
        scratch_shapes=[pltpu.VMEM((tm, tn), F32)] if nk > 1 else [],
        compiler_params=_cparams(("parallel", "parallel", "arbitrary")),
        name=name,
    )(*args)


def _rmsnorm_fwd(x, w, *, name):
    t, d = x.shape
    tb = _tile(t, (512, 256, 128))

    def body(x_ref, w_ref, o_ref):
        xv = x_ref[...]
        r = lax.rsqrt(jnp.mean(xv * xv, axis=-1, keepdims=True) + EPS)
        o_ref[...] = (xv * r * w_ref[...]).astype(o_ref.dtype)

    return pl.pallas_call(
        body,
        grid=(t // tb,),
        in_specs=[pl.BlockSpec((tb, d), lambda i: (i, 0)), pl.BlockSpec((1, d), lambda i: (0, 0))],
        out_specs=pl.BlockSpec((tb, d), lambda i: (i, 0)),
        out_shape=jax.ShapeDtypeStruct((t, d), BF16),
        compiler_params=_cparams(("parallel",)),
        name=name,
    )(x, w.reshape(1, d))


def _rmsnorm_bwd(x, dys, dres, *, name):
    t, d = x.shape
    tb = _tile(t, (256, 128))
    nn = len(dys)
    has_res = dres is not None

    def body(*refs):
        x_ref = refs[0]
        dy_refs = refs[1:1 + nn]
        w_refs = refs[1 + nn:1 + 2 * nn]
        pos = 1 + 2 * nn
        res_ref = refs[pos] if has_res else None
        pos += 1 if has_res else 0
        dx_ref = refs[pos]
        dw_refs = refs[pos + 1:pos + 1 + nn]
        i = pl.program_id(0)
        xv = x_ref[...]
        r = lax.rsqrt(jnp.mean(xv * xv, axis=-1, keepdims=True) + EPS)
        xn = xv * r
        dx = res_ref[...] if has_res else jnp.zeros_like(xv)
        for q in range(nn):
            dy = dy_refs[q][...].astype(F32)
            g = dy * w_refs[q][...]
            dx = dx + r * (g - xn * jnp.mean(g * xn, axis=-1, keepdims=True))
            dwp = jnp.sum(dy * xn, axis=0, keepdims=True)

            @pl.when(i == 0)
            def _(q=q, dwp=dwp):
                dw_refs[q][...] = dwp

            @pl.when(i > 0)
            def _(q=q, dwp=dwp):
                dw_refs[q][...] += dwp
        dx_ref[...] = dx

    row = pl.BlockSpec((tb, d), lambda i: (i, 0))
    vec = pl.BlockSpec((1, d), lambda i: (0, 0))
    in_specs = [row] + [row] * nn + [vec] * nn + ([row] if has_res else [])
    args = [x] + [p[0] for p in dys] + [p[1].reshape(1, d) for p in dys] + ([dres] if has_res else [])
    outs = pl.pallas_call(
        body,
        grid=(t // tb,),
        in_specs=in_specs,
        out_specs=[row] + [vec] * nn,
        out_shape=[jax.ShapeDtypeStruct((t, d), F32)] + [jax.ShapeDtypeStruct((1, d), F32)] * nn,
        compiler_params=_cparams(("arbitrary",)),
        name=name,
    )(*args)
    return outs[0], list(outs[1:])


def _loss_head(x, w, target, *, name):
    t, d = x.shape
    tb = _tile(t, (256, 128))

    def body(x_ref, w_ref, t_ref, loss_ref, dx_ref, dw_ref):
        i = pl.program_id(0)
        xv = x_ref[...]
        wv = w_ref[...]
        r = lax.rsqrt(jnp.mean(xv * xv, axis=-1, keepdims=True) + EPS)
        xn = xv * r
        e = xn * wv - t_ref[...]
        lp = 0.5 * jnp.sum(jnp.mean(e * e, axis=-1, keepdims=True), axis=0, keepdims=True)
        dy = e * (1.0 / d)
        g = dy * wv
        dx_ref[...] = r * (g - xn * jnp.mean(g * xn, axis=-1, keepdims=True))
        dwp = jnp.sum(dy * xn, axis=0, keepdims=True)
        lpv = jnp.broadcast_to(lp, (1, LANES)) * (1.0 / LANES)

        @pl.when(i == 0)
        def _():
            dw_ref[...] = dwp
            loss_ref[...] = lpv

        @pl.when(i > 0)
        def _():
            dw_ref[...] += dwp
            loss_ref[...] += lpv

    row = pl.BlockSpec((tb, d), lambda i: (i, 0))
    vec = pl.BlockSpec((1, d), lambda i: (0, 0))
    return pl.pallas_call(
        body,
        grid=(t // tb,),
        in_specs=[row, vec, row],
        out_specs=[pl.BlockSpec((1, LANES), lambda i: (0, 0)), row, vec],
        out_shape=[jax.ShapeDtypeStruct((1, LANES), F32), jax.ShapeDtypeStruct((t, d), F32),
                   jax.ShapeDtypeStruct((1, d), F32)],
        compiler_params=_cparams(("arbitrary",)),
        name=name,
    )(x, w.reshape(1, d), target)


ROW_CHUNK = 64
PAD = SUBLANES


def _shifted(pad_ref, r0, rows, back):
    return pad_ref[pl.ds(PAD + r0 - back, rows), :]


def _conv_taps(pad_ref, w_ref, r0, rows, kw):
    acc = None
    for j in range(kw):
        term = _shifted(pad_ref, r0, rows, kw - 1 - j) * w_ref[j:j + 1, :]
        acc = term if acc is None else acc + term
    return acc


def _fill_pad(pad_ref, x_ref, t):
    pad_ref[0:PAD, :] = jnp.zeros((PAD, pad_ref.shape[1]), F32)
    pad_ref[pl.ds(PAD + t, PAD), :] = jnp.zeros((PAD, pad_ref.shape[1]), F32)
    pad_ref[pl.ds(PAD, t), :] = x_ref[...].astype(F32)


def _conv_silu_fwd(x, w, b, *, x_off=0, name):
    t = x.shape[0]
    kw, c = w.shape
    cw = _tile(math.gcd(c, x_off) if x_off else c, (256, 128))
    ob = x_off // cw
    rc = _tile(t, (ROW_CHUNK,))

    def body(x_ref, w_ref, b_ref, o_ref, pad_ref):
        _fill_pad(pad_ref, x_ref, t)
        for r0 in range(0, t, rc):
            pre = _conv_taps(pad_ref, w_ref, r0, rc, kw) + b_ref[...]
            o_ref[pl.ds(r0, rc), :] = pre * _sigmoid(pre)

    strip = pl.BlockSpec((t, cw), lambda i: (0, i))
    return pl.pallas_call(
        body,
        grid=(c // cw,),
        in_specs=[pl.BlockSpec((t, cw), lambda i: (0, i + ob)), pl.BlockSpec((kw, cw), lambda i: (0, i)),
                  pl.BlockSpec((1, cw), lambda i: (0, i))],
        out_specs=strip,
        out_shape=jax.ShapeDtypeStruct((t, c), F32),
        scratch_shapes=[pltpu.VMEM((t + 2 * PAD, cw), F32)],
        compiler_params=_cparams(("parallel",)),
        name=name,
    )(x, w, b.reshape(1, c))


def _conv_bwd_core(dpre_pad_ref, x_pad_ref, w_ref, dx_ref, dw_ref, db_ref, t, rc, kw):
    cw = dx_ref.shape[1]

    def fold(a):
        return jnp.sum(a.reshape(rc // SUBLANES, SUBLANES, cw), axis=0) if rc % SUBLANES == 0 else jnp.sum(a, axis=0, keepdims=True)

    dws = [None] * kw
    dbs = None
    for r0 in range(0, t, rc):
        dpre = dpre_pad_ref[pl.ds(PAD + r0, rc), :]
        dx = None
        for j in range(kw):
            s = kw - 1 - j
            term = dpre_pad_ref[pl.ds(PAD + r0 + s, rc), :] * w_ref[j:j + 1, :]
            dx = term if dx is None else dx + term
            part = fold(dpre * _shifted(x_pad_ref, r0, rc, s))
            dws[j] = part if dws[j] is None else dws[j] + part
        part = fold(dpre)
        dbs = part if dbs is None else dbs + part
        dx_ref[pl.ds(r0, rc), :] = dx
    for j in range(kw):
        dw_ref[j:j + 1, :] = jnp.sum(dws[j], axis=0, keepdims=True)
    db_ref[...] = jnp.sum(dbs, axis=0, keepdims=True)


def _conv_silu_bwd(x, w, b, dact, *, x_off=0, into=None, name):
    t = x.shape[0]
    kw, c = w.shape
    parts = dact if isinstance(dact, (list, tuple)) else [dact]
    widths = [p.shape[1] for p in parts]
    assert sum(widths) == c
    cw = _tile(functools.reduce(math.gcd, widths + [x_off or c]), (256, 128) if len(parts) == 1 else (128,))
    ob = x_off // cw
    rc = _tile(t, (ROW_CHUNK,))
    firsts = [sum(widths[:p]) // cw for p in range(len(parts))]
    counts = [wd // cw for wd in widths]
    n_p = len(parts)

    def body(x_ref, w_ref, b_ref, *rest):
        da_refs = rest[:n_p]
        dx_ref, dw_ref, db_ref, xpad_ref, dpad_ref = rest[-5 - (n_p > 1):][:5]
        if n_p > 1:
            da_ref = rest[-1]
            i = pl.program_id(0)
            for p in range(n_p):
                @pl.when(jnp.logical_and(i >= firsts[p], i < firsts[p] + counts[p]))
                def _(p=p):
                    da_ref[...] = da_refs[p][...]
        else:
            da_ref = da_refs[0]
        _fill_pad(xpad_ref, x_ref, t)
        dpad_ref[0:PAD, :] = jnp.zeros((PAD, cw), F32)
        dpad_ref[pl.ds(PAD + t, PAD), :] = jnp.zeros((PAD, cw), F32)
        for r0 in range(0, t, rc):
            pre = _conv_taps(xpad_ref, w_ref, r0, rc, kw) + b_ref[...]
            sg = _sigmoid(pre)
            dpad_ref[pl.ds(PAD + r0, rc), :] = da_ref[pl.ds(r0, rc), :] * (sg * (1.0 + pre * (1.0 - sg)))
        _conv_bwd_core(dpad_ref, xpad_ref, w_ref, dx_ref, dw_ref, db_ref, t, rc, kw)

    strip = pl.BlockSpec((t, cw), lambda i: (0, i))
    wspec = pl.BlockSpec((kw, cw), lambda i: (0, i))
    bspec = pl.BlockSpec((1, cw), lambda i: (0, i))
    xspec = pl.BlockSpec((t, cw), lambda i: (0, i + ob))
    dspecs = [pl.BlockSpec((t, cw), lambda i, f=f, n=n: (0, jnp.clip(i - f, 0, n - 1))) for f, n in zip(firsts, counts)]
    extra = {} if into is None else dict(input_output_aliases={3 + n_p: 0})
    pad = pltpu.VMEM((t + 2 * PAD, cw), F32)
    return pl.pallas_call(
        body,
        grid=(c // cw,),
        in_specs=[xspec, wspec, bspec] + dspecs + ([] if into is None else [ANY]),
        out_specs=[strip if into is None else xspec, wspec, bspec],
        out_shape=[jax.ShapeDtypeStruct((t, c) if into is None else into.shape, F32), jax.ShapeDtypeStruct((kw, c), F32),
                   jax.ShapeDtypeStruct((1, c), F32)],
        scratch_shapes=[pad, pad] + ([pltpu.VMEM((t, cw), F32)] if n_p > 1 else []),
        compiler_params=_cparams(("arbitrary",)),
        name=name,
        **extra,
    )(x, w, b.reshape(1, c), *parts, *([] if into is None else [into]))


def _conv_glu_fwd(hid, w, b, *, side=None, name):
    t, c2 = hid.shape
    f = c2 // 2
    kw = w.shape[0]
    cw = _tile(f, (256, 128))
    nf = f // cw
    rc = _tile(t, (ROW_CHUNK,))

    def body(g_ref, v_ref, wg_ref, wv_ref, bg_ref, bv_ref, o_ref, gpad_ref, vpad_ref):
        _fill_pad(gpad_ref, g_ref, t)
        _fill_pad(vpad_ref, v_ref, t)
        for r0 in range(0, t, rc):
            gate = _conv_taps(gpad_ref, wg_ref, r0, rc, kw) + bg_ref[...]
            val = _conv_taps(vpad_ref, wv_ref, r0, rc, kw) + bv_ref[...]
            o_ref[pl.ds(r0, rc), :] = (gate * _sigmoid(gate) * val).astype(o_ref.dtype)

    gs = pl.BlockSpec((t, cw), lambda i: (0, i))
    vs = pl.BlockSpec((t, cw), lambda i: (0, i + nf))
    b2 = b.reshape(1, c2)
    (act,), side_outs = _call(
        body,
        grid=(nf,),
        in_specs=[gs, vs, pl.BlockSpec((kw, cw), lambda i: (0, i)), pl.BlockSpec((kw, cw), lambda i: (0, i + nf)),
                  pl.BlockSpec((1, cw), lambda i: (0, i)), pl.BlockSpec((1, cw), lambda i: (0, i + nf))],
        out_specs=[gs],
        out_shape=[jax.ShapeDtypeStruct((t, f), BF16)],
        scratch_shapes=[pltpu.VMEM((t + 2 * PAD, cw), F32), pltpu.VMEM((t + 2 * PAD, cw), F32)],
        sem=("parallel",),
        name=name,
        args=(hid, hid, w, w, b2, b2),
        side=side,
    )
    return act, side_outs


def _conv_glu_bwd(hid, w, b, dact, *, name):
    t, c2 = hid.shape
    f = c2 // 2
    kw = w.shape[0]
    cw = _tile(f, (128,))
    nf = f // cw
    rc = _tile(t, (ROW_CHUNK,))

    def body(g_ref, v_ref, wg_ref, wv_ref, bg_ref, bv_ref, da_ref,
             dgv_ref, dwg_ref, dwv_ref, dbg_ref, dbv_ref,
             gpad_ref, vpad_ref, dgpad_ref, dvpad_ref):
        _fill_pad(gpad_ref, g_ref, t)
        _fill_pad(vpad_ref, v_ref, t)
        for ref in (dgpad_ref, dvpad_ref):
            ref[0:PAD, :] = jnp.zeros((PAD, cw), F32)
            ref[pl.ds(PAD + t, PAD), :] = jnp.zeros((PAD, cw), F32)
        for r0 in range(0, t, rc):
            gate = _conv_taps(gpad_ref, wg_ref, r0, rc, kw) + bg_ref[...]
            val = _conv_taps(vpad_ref, wv_ref, r0, rc, kw) + bv_ref[...]
            sg = _sigmoid(gate)
            da = da_ref[pl.ds(r0, rc), :].astype(F32)
            dgpad_ref[pl.ds(PAD + r0, rc), :] = da * val * (sg * (1.0 + gate * (1.0 - sg)))
            dvpad_ref[pl.ds(PAD + r0, rc), :] = da * (gate * sg)
        _conv_bwd_core(dgpad_ref, gpad_ref, wg_ref, dgv_ref.at[0], dwg_ref, dbg_ref, t, rc, kw)
        _conv_bwd_core(dvpad_ref, vpad_ref, wv_ref, dgv_ref.at[1], dwv_ref, dbv_ref, t, rc, kw)

    gs = pl.BlockSpec((t, cw), lambda i: (0, i))
    vs = pl.BlockSpec((t, cw), lambda i: (0, i + nf))
    wg = pl.BlockSpec((kw, cw), lambda i: (0, i))
    wv = pl.BlockSpec((kw, cw), lambda i: (0, i + nf))
    bg = pl.BlockSpec((1, cw), lambda i: (0, i))
    bv = pl.BlockSpec((1, cw), lambda i: (0, i + nf))
    b2 = b.reshape(1, c2)
    pad = pltpu.VMEM((t + 2 * PAD, cw), F32)
    return pl.pallas_call(
        body,
        grid=(nf,),
        in_specs=[gs, vs, wg, wv, bg, bv, gs],
        out_specs=[pl.BlockSpec((2, t, cw), lambda i: (0, 0, i)), wg, wg, bg, bg],
        out_shape=[jax.ShapeDtypeStruct((2, t, f), F32),
                   jax.ShapeDtypeStruct((kw, f), F32), jax.ShapeDtypeStruct((kw, f), F32),
                   jax.ShapeDtypeStruct((1, f), F32), jax.ShapeDtypeStruct((1, f), F32)],
        scratch_shapes=[pad, pad, pad, pad],
        compiler_params=_cparams(("parallel",)),
        name=name,
    )(hid, hid, w, w, b2, b2, dact)


def _gate_norm_fwd(y, zx, w, *, name):
    t, di = y.shape
    gsz = di // SSM_GROUPS
    tb = _tile(t, (256, 128))

    def body(y_ref, z_ref, w_ref, o_ref):
        for g in range(SSM_GROUPS):
            sl = slice(g * gsz, (g + 1) * gsz)
            zv = z_ref[:, sl]
            gv = y_ref[:, sl] * (zv * _sigmoid(zv))
            r = lax.rsqrt(jnp.mean(gv * gv, axis=-1, keepdims=True) + EPS)
            o_ref[:, sl] = (gv * r * w_ref[:, sl]).astype(o_ref.dtype)

    row = pl.BlockSpec((tb, di), lambda i: (i, 0))
    return pl.pallas_call(
        body,
        grid=(t // tb,),
        in_specs=[row, row, pl.BlockSpec((1, di), lambda i: (0, 0))],
        out_specs=row,
        out_shape=jax.ShapeDtypeStruct((t, di), BF16),
        compiler_params=_cparams(("parallel",)),
        name=name,
    )(y, zx, w.reshape(1, di))


def _gate_norm_bwd(y, zx, w, dyn, *, side=None, name):
    t, di = y.shape
    gsz = di // SSM_GROUPS
    tb = _tile(t, (256, 128))

    def body(y_ref, z_ref, w_ref, d_ref, dy_ref, dz_ref, dw_ref):
        i = pl.program_id(0)
        for g in range(SSM_GROUPS):
            sl = slice(g * gsz, (g + 1) * gsz)
            zv = z_ref[:, sl]
            yv = y_ref[:, sl]
            sg = _sigmoid(zv)
            sz = zv * sg
            gv = yv * sz
            r = lax.rsqrt(jnp.mean(gv * gv, axis=-1, keepdims=True) + EPS)
            gn = gv * r
            dn = d_ref[:, sl].astype(F32)
            q = dn * w_ref[:, sl]
            dg = r * (q - gn * jnp.mean(q * gn, axis=-1, keepdims=True))
            dy_ref[:, sl] = dg * sz
            dz_ref[:, sl] = dg * yv * (sg * (1.0 + zv * (1.0 - sg)))
            dwp = jnp.sum(dn * gn, axis=0, keepdims=True)

            @pl.when(i == 0)
            def _(sl=sl, dwp=dwp):
                dw_ref[:, sl] = dwp

            @pl.when(i > 0)
            def _(sl=sl, dwp=dwp):
                dw_ref[:, sl] += dwp

    row = pl.BlockSpec((tb, di), lambda i: (i, 0))
    vec = pl.BlockSpec((1, di), lambda i: (0, 0))
    outs, side_outs = _call(
        body,
        grid=(t // tb,),
        in_specs=[row, row, vec, row],
        out_specs=[row, row, vec],
        out_shape=[jax.ShapeDtypeStruct((t, di), F32), jax.ShapeDtypeStruct((t, zx.shape[1]), F32),
                   jax.ShapeDtypeStruct((1, di), F32)],
        sem=("arbitrary",),
        name=name,
        args=(y, zx, w.reshape(1, di), dyn),
        side=side,
    )
    return (*outs, side_outs)


def _adamw(w, g, m, v, *, name):
    shape = w.shape
    cols = shape[-1]
    rows = w.size // cols
    w2, g2, m2, v2 = (a.reshape(rows, cols) for a in (w, g, m, v))
    tr = rows
    if rows * cols * 4 > ADAM_BLOCK_BYTES:
        tr = _tile(rows, tuple(r for r in (512, 256, 128, 64, 32, 16, 8) if r * cols * 4 <= ADAM_BLOCK_BYTES))
    c1 = 1.0 - ADAM_B1 ** ADAM_STEP
    c2 = 1.0 - ADAM_B2 ** ADAM_STEP

    def body(w_ref, g_ref, m_ref, v_ref, d_ref, nm_ref, nv_ref):
        gv = g_ref[...]
        nm = ADAM_B1 * m_ref[...] + (1.0 - ADAM_B1) * gv
        nv = ADAM_B2 * v_ref[...] + (1.0 - ADAM_B2) * (gv * gv)
        d_ref[...] = -ADAM_LR * ((nm / c1) / (jnp.sqrt(nv / c2) + ADAM_EPS) + ADAM_WD * w_ref[...])
        nm_ref[...] = nm
        nv_ref[...] = nv

    blk = pl.BlockSpec((tr, cols), lambda i: (i, 0))
    outs = pl.pallas_call(
        body,
        grid=(rows // tr,),
        in_specs=[blk] * 4,
        out_specs=[blk] * 3,
        out_shape=[jax.ShapeDtypeStruct((rows, cols), F32)] * 3,
        compiler_params=_cparams(("parallel",)),
        name=name,
    )(w2, g2, m2, v2)
    return tuple(o.reshape(shape) for o in outs)


def _adamw_layers(w, gs, m, v, *, name):
    n_l, rows, cols = w.shape
    assert len(gs) == n_l
    tr = _tile(rows, tuple(r for r in (512, 256, 128, 64, 32, 16, 8) if r * cols * 4 <= ADAM_BLOCK_BYTES))
    c1 = 1.0 - ADAM_B1 ** ADAM_STEP
    c2 = 1.0 - ADAM_B2 ** ADAM_STEP

    def body(*refs):
        w_ref, m_ref, v_ref = refs[:3]
        g_refs = refs[3:3 + n_l]
        g_ref, d_ref, nm_ref, nv_ref = refs[3 + n_l:]
        layer = pl.program_id(0)
        gv = g_refs[0][...]
        for q in range(1, n_l):
            gv = jnp.where(layer == q, g_refs[q][...], gv)
        nm = ADAM_B1 * m_ref[...] + (1.0 - ADAM_B1) * gv
        nv = ADAM_B2 * v_ref[...] + (1.0 - ADAM_B2) * (gv * gv)
        g_ref[...] = gv
        d_ref[...] = -ADAM_LR * ((nm / c1) / (jnp.sqrt(nv / c2) + ADAM_EPS) + ADAM_WD * w_ref[...])
        nm_ref[...] = nm
        nv_ref[...] = nv

    stacked = pl.BlockSpec((None, tr, cols), lambda l, i: (l, i, 0))
    single = pl.BlockSpec((tr, cols), lambda l, i: (i, 0))
    return pl.pallas_call(
        body,
        grid=(n_l, rows // tr),
        in_specs=[stacked] * 3 + [single] * n_l,
        out_specs=[stacked] * 4,
        out_shape=[jax.ShapeDtypeStruct(w.shape, F32)] * 4,
        compiler_params=_cparams(("parallel", "parallel")),
        name=name,
    )(w, m, v, *gs)


def _ssd_scalars(dtc_ref, dtr_ref, hpc_ref, hpr_ref, ln):
    assert SSM_CHUNK == SSM_STATE == LANES, "the SSD kernels mix chunk, state and lane-wide tiles freely"
    bias_c, alog_c = hpc_ref[0, 0:1, :], hpc_ref[0, 1:2, :]
    bias_r, alog_r = hpr_ref[0, :, 0:1], hpr_ref[0, :, 1:2]
    a_c, a_r = -jnp.exp(alog_c), -jnp.exp(alog_r)
    raw_c = dtc_ref[0] + bias_c
    dt_c = _softplus(raw_c)
    dt_r = _softplus(dtr_ref[0] + bias_r)
    row = lax.broadcasted_iota(jnp.int32, (ln, ln), 0)
    col = lax.broadcasted_iota(jnp.int32, (ln, ln), 1)
    lower = (col <= row).astype(F32)
    upper = (row <= col).astype(F32)
    acs_c = _ones_dot(lower, dt_c * a_c, ones_left=True)
    acs_r = _ones_dot(upper, dt_r * a_r, ones_left=False)
    return raw_c, dt_c, a_c, acs_c, acs_r, row, col


def _ssd_specs(t, di, g_n, n_st, rp, ln, r_h, rev):
    nc = t // ln
    cidx = (lambda c: nc - 1 - c) if rev else (lambda c: c)
    xs = pl.BlockSpec((ln, rp), lambda g, c: (cidx(c), g))
    bm = pl.BlockSpec((ln, n_st), lambda g, c: (cidx(c), di // n_st + g))
    cm = pl.BlockSpec((ln, n_st), lambda g, c: (cidx(c), di // n_st + g_n + g))
    dtc = pl.BlockSpec((1, ln, r_h), lambda g, c: (g, cidx(c), 0))
    dtr = pl.BlockSpec((1, r_h, ln), lambda g, c: (g, 0, cidx(c)))
    hpc = pl.BlockSpec((1, 3, r_h), lambda g, c: (g, 0, 0))
    hpr = pl.BlockSpec((1, r_h, 3), lambda g, c: (g, 0, 0))
    prev = pl.BlockSpec((1, rp, n_st), lambda g, c: (cidx(c), g, 0))
    return xs, bm, cm, dtc, dtr, hpc, hpr, prev


def _ssd_fwd(xbc, dtc, dtr, hpc, hpr, *, side=None, name):
    t = xbc.shape[0]
    di, g_n, n_st, p_h, ln = D_INNER, SSM_GROUPS, SSM_STATE, SSM_HEAD_DIM, SSM_CHUNK
    r_h = SSM_HEADS // g_n
    rp = r_h * p_h
    nc = t // ln

    def body(xs_ref, b_ref, c_ref, dtc_ref, dtr_ref, hpc_ref, hpr_ref, y_ref, prev_ref, st_ref):
        @pl.when(pl.program_id(1) == 0)
        def _():
            st_ref[...] = jnp.zeros_like(st_ref)

        _, dt_c, _, acs_c, acs_r, row, col = _ssd_scalars(dtc_ref, dtr_ref, hpc_ref, hpr_ref, ln)
        bm = b_ref[...]
        cm = c_ref[...]
        cm16 = cm.astype(BF16)
        cb = _nt(cm16, bm.astype(BF16))
        causal = row >= col
        for r in range(r_h):
            sl = slice(r * p_h, (r + 1) * p_h)
            xs = xs_ref[:, sl]
            acs = jnp.broadcast_to(acs_c[:, r:r + 1], (ln, ln))
            last = acs[ln - 1:ln, :]
            lm = jnp.where(causal, jnp.exp(acs - acs_r[r:r + 1, :]), 0.0)
            xd = (xs * jnp.broadcast_to(dt_c[:, r:r + 1], (ln, p_h))).astype(BF16)
            prev = st_ref[sl, :]
            y = _nn((cb * lm).astype(BF16), xd)
            y = y + _nt(cm16, prev.astype(BF16)) * jnp.exp(acs[:, :p_h])
            y_ref[:, sl] = y + hpc_ref[0, 2:3, r:r + 1] * xs
            prev_ref[0, sl, :] = prev
            bd = (bm * jnp.exp(last - acs[:, :n_st])).astype(BF16)
            st_ref[sl, :] = prev * jnp.exp(last[:, :n_st]) + _tn(xd, bd)

    xs, bm, cm, dtcs, dtrs, hpcs, hprs, prev = _ssd_specs(t, di, g_n, n_st, rp, ln, r_h, False)
    (y, prev_out), side_outs = _call(
        body,
        grid=(g_n, nc),
        in_specs=[xs, bm, cm, dtcs, dtrs, hpcs, hprs],
        out_specs=[xs, prev],
        out_shape=[jax.ShapeDtypeStruct((t, di), F32), jax.ShapeDtypeStruct((nc, g_n * rp, n_st), F32)],
        scratch_shapes=[pltpu.VMEM((rp, n_st), F32)],
        sem=("parallel", "arbitrary"),
        name=name,
        args=(xbc, xbc, xbc, dtc, dtr, hpc, hpr),
        side=side,
    )
    return y, prev_out, side_outs


def _ssd_bwd(xbc, dtc, dtr, hpc, hpr, prev, dy, *, side=None, name):
    t = xbc.shape[0]
    di, g_n, n_st, p_h, ln = D_INNER, SSM_GROUPS, SSM_STATE, SSM_HEAD_DIM, SSM_CHUNK
    r_h = SSM_HEADS // g_n
    rp = r_h * p_h
    nc = t // ln

    def body(xs_ref, b_ref, c_ref, dtc_ref, dtr_ref, hpc_ref, hpr_ref, prev_ref, dy_ref,
             dxs_ref, db_ref, dc_ref, ddt_ref, hg_ref, ds_ref):
        step = pl.program_id(1)

        @pl.when(step == 0)
        def _():
            ds_ref[...] = jnp.zeros_like(ds_ref)

        raw_c, dt_c, a_c, acs_c, acs_r, row, col = _ssd_scalars(dtc_ref, dtr_ref, hpc_ref, hpr_ref, ln)
        bm = b_ref[...]
        cm = c_ref[...]
        bm16, cm16 = bm.astype(BF16), cm.astype(BF16)
        cb = _nt(cm16, bm16)
        cbt = _nt(bm16, cm16)
        lane_r = lax.broadcasted_iota(jnp.int32, (ln, r_h), 1)
        dacs_all = jnp.zeros((ln, r_h), F32)
        ddtx_all = jnp.zeros((ln, r_h), F32)
        dd_all = jnp.zeros((ln, r_h), F32)
        dcb = jnp.zeros((ln, ln), F32)
        dcbt = jnp.zeros((ln, ln), F32)
        dc_acc = jnp.zeros((ln, n_st), F32)
        db_acc = jnp.zeros((ln, n_st), F32)
        for r in range(r_h):
            sl = slice(r * p_h, (r + 1) * p_h)
            xs = xs_ref[:, sl]
            dyv = dy_ref[:, sl]
            dy16 = dyv.astype(BF16)
            acs = jnp.broadcast_to(acs_c[:, r:r + 1], (ln, ln))
            dtv = jnp.broadcast_to(dt_c[:, r:r + 1], (ln, p_h))
            acsr = acs_r[r:r + 1, :]
            last = acs[ln - 1:ln, :]
            xd = xs * dtv
            xd16 = xd.astype(BF16)
            lm = jnp.where(row >= col, jnp.exp(acs - acsr), 0.0)
            lmt = jnp.where(col >= row, jnp.exp(acsr - acs), 0.0)
            m_ls = cb * lm
            m_sl = cbt * lmt
            dm = _nt(dy16, xd16)
            dmt = _nt(xd16, dy16)
            dxd = _nn(m_sl.astype(BF16), dy16)
            dacs = _row_sums(dm * m_ls - dmt * m_sl)
            dcb = dcb + dm * lm
            dcbt = dcbt + dmt * lmt
            prev = prev_ref[0, sl, :]
            prev16 = prev.astype(BF16)
            e = jnp.exp(acs[:, :p_h])
            y_off = _nt(cm16, prev16) * e
            dacs = dacs + _row_sums(dyv * y_off)
            dyo16 = (dyv * e).astype(BF16)
            dc_acc = dc_acc + _nn(dyo16, prev16)
            dprev = _tn(dyo16, cm16)
            ds = ds_ref[sl, :]
            ds16 = ds.astype(BF16)
            decay = jnp.exp(last - acs)[:, :n_st]
            bd16 = (bm * decay).astype(BF16)
            dbd = _nn(xd16, ds16)
            dxd = dxd + _nt(bd16, ds16)
            db_acc = db_acc + dbd * decay
            tdec = _row_sums(dbd * bm, 2) * decay
            cd = jnp.exp(last)
            dlast = (jnp.sum(tdec, axis=0, keepdims=True)
                     + jnp.sum(_row_sums(prev * ds, 2), axis=0, keepdims=True) * cd)
            ds_ref[sl, :] = dprev + cd[:, :n_st] * ds
            dskip = hpc_ref[0, 2:3, r:r + 1]
            dxs_ref[:, sl] = dxd * dtv + dskip * dyv
            dacs = dacs - tdec + jnp.where(row == ln - 1, dlast, 0.0)
            dacs_all = jnp.where(lane_r == r, dacs[:, :r_h], dacs_all)
            ddtx_all = jnp.where(lane_r == r, _row_sums(dxd * xs, 2)[:, :r_h], ddtx_all)
            dd_all = jnp.where(lane_r == r, _row_sums(dyv * xs, 2)[:, :r_h], dd_all)
        dc_ref[...] = dc_acc + _nn(dcb.astype(BF16), bm16)
        db_ref[...] = db_acc + _nn(dcbt.astype(BF16), cm16)
        upper = (row <= col).astype(F32)
        dad = _ones_dot(upper, dacs_all, ones_left=True)
        ddt = dad * a_c + ddtx_all
        ddt_raw = ddt * _sigmoid(raw_c)
        ddt_ref[0] = ddt_raw
        d_bias = jnp.sum(ddt_raw, axis=0, keepdims=True)
        d_alog = jnp.sum(dad * dt_c, axis=0, keepdims=True) * a_c
        d_d = jnp.sum(dd_all, axis=0, keepdims=True)
        hg = jnp.concatenate([d_bias, d_alog, d_d], axis=0)

        @pl.when(step == 0)
        def _():
            hg_ref[0] = hg

        @pl.when(step > 0)
        def _():
            hg_ref[0] += hg

    xs, bms, cms, dtcs, dtrs, hpcs, hprs, prevs = _ssd_specs(t, di, g_n, n_st, rp, ln, r_h, True)
    bout = pl.BlockSpec((ln, n_st), lambda g, c: (nc - 1 - c, g))
    outs, side_outs = _call(
        body,
        grid=(g_n, nc),
        in_specs=[xs, bms, cms, dtcs, dtrs, hpcs, hprs, prevs, xs],
        out_specs=[xs, bout, bout, dtcs, hpcs],
        out_shape=[jax.ShapeDtypeStruct((t, di), F32), jax.ShapeDtypeStruct((t, g_n * n_st), F32),
                   jax.ShapeDtypeStruct((t, g_n * n_st), F32), jax.ShapeDtypeStruct((g_n, t, r_h), F32),
                   jax.ShapeDtypeStruct((g_n, 3, r_h), F32)],
        scratch_shapes=[pltpu.VMEM((rp, n_st), F32)],
        sem=("parallel", "arbitrary"),
        name=name,
        args=(xbc, xbc, xbc, dtc, dtr, hpc, hpr, prev, dy),
        side=side,
    )
    return (*outs, side_outs)


SB_KEYS = 256
SB_QUERIES = (512, 256)
SB_CUTOFF = 110.0
SB_PIECES = 2


def _sb_logits(qs, kv, valid):
    z = _nt(qs, kv)
    nz = -z
    lg = jnp.minimum(nz, 0.0) - jnp.log(1.0 + jnp.exp(jnp.minimum(z, nz)))
    return z + lg, (lg if valid is None else jnp.where(valid, lg, 0.0))


def _sb_iota(tq):
    diff = lax.broadcasted_iota(jnp.int32, (tq, SB_KEYS), 1) - lax.broadcasted_iota(jnp.int32, (tq, SB_KEYS), 0)
    krow = lax.broadcasted_iota(jnp.int32, (SB_KEYS, SB_KEYS), 0)
    kcol = lax.broadcasted_iota(jnp.int32, (SB_KEYS, SB_KEYS), 1)
    return diff, krow, kcol


def _sb_scale(d):
    scale = 1.0 / math.sqrt(d)
    assert math.frexp(scale)[0] == 0.5, "the scale is folded into bf16 queries: it must be a power of two"
    return scale


def _key_rows(j):
    return pl.ds(pl.multiple_of(j * SB_KEYS, SB_KEYS), SB_KEYS)


def _sb_fwd(q, k, v, n_heads, *, side=None, name):
    t, hd = q.shape
    d = hd // n_heads
    hpt = LANES // d
    assert hpt * d == LANES and n_heads % hpt == 0
    tq = _tile(t, SB_QUERIES)
    nq = t // tq
    kpq = tq // SB_KEYS
    scale = _sb_scale(d)

    def body(q_ref, k_ref, v_ref, o_ref, lt_ref, first_ref):
        i = pl.program_id(1)
        diff, krow, kcol = _sb_iota(tq)
        later = (krow > kcol).astype(F32)
        nb = i * kpq
        for hh in range(hpt):
            sl = slice(hh * d, (hh + 1) * d)
            qs = (q_ref[:, sl].astype(F32) * scale).astype(BF16)

            def block(j, carry, valid, qs=qs, sl=sl):
                acc, cl = carry
                rows = _key_rows(j)
                ls, lg = _sb_logits(qs, k_ref[rows, sl], valid)
                cs = _ones_dot(later, lg, ones_left=False, pieces=SB_PIECES)
                att = jnp.exp(ls + (cs + cl))
                if valid is not None:
                    att = jnp.where(valid, att, 0.0)
                acc = acc + _nn(att.astype(BF16), v_ref[rows, sl])
                return acc, cl + (cs[:, 0:1] + lg[:, 0:1])

            carry = (jnp.zeros((tq, d), F32), jnp.zeros((tq, 1), F32))
            for m in range(kpq - 1, -1, -1):
                carry = block(i * kpq + m, carry, diff < -m * SB_KEYS)

            def more(st):
                s, _, cl = st
                return jnp.logical_and(s < nb, jnp.max(cl) > -SB_CUTOFF)

            def step(st, block=block):
                s, acc, cl = st
                acc, cl = block(nb - 1 - s, (acc, cl), None)
                return s + 1, acc, cl

            walked, acc, cl = lax.while_loop(more, step, (jnp.int32(0),) + carry)
            o_ref[:, sl] = acc.astype(o_ref.dtype)
            lt_ref[hh] = cl
            first_ref[pl.program_id(0) * hpt + hh, i] = nb - walked

    qs = pl.BlockSpec((tq, LANES), lambda p, i: (i, p))
    ls = pl.BlockSpec((hpt, tq, 1), lambda p, i: (p, i, 0))
    ks = pl.BlockSpec((t, LANES), lambda p, i: (0, p))
    outs, side_outs = _call(
        body,
        grid=(n_heads // hpt, nq),
        in_specs=[qs, ks, ks],
        out_specs=[qs, ls, pl.BlockSpec(memory_space=pltpu.SMEM)],
        out_shape=[jax.ShapeDtypeStruct((t, hd), BF16), jax.ShapeDtypeStruct((n_heads, t, 1), F32),
                   jax.ShapeDtypeStruct((n_heads, nq), jnp.int32)],
        sem=("arbitrary", "arbitrary"),
        name=name,
        args=(q, k, v),
        side=side,
    )
    return (*outs, side_outs)


def _sb_bwd(q, k, v, lt, first, do, n_heads, *, name):
    t, hd = q.shape
    d = hd // n_heads
    hpt = LANES // d
    tq = _tile(t, SB_QUERIES)
    nq = t // tq
    kpq = tq // SB_KEYS
    scale = _sb_scale(d)
    last = SB_KEYS - 1

    def body(q_ref, k_ref, v_ref, lt_ref, first_ref, do_ref, dq_ref, dk_ref, dv_ref, dk_acc, dv_acc):
        i = pl.program_id(1)

        @pl.when(i == 0)
        def _():
            dk_acc[...] = jnp.zeros_like(dk_acc)
            dv_acc[...] = jnp.zeros_like(dv_acc)

        diff, krow, kcol = _sb_iota(tq)
        upto = (krow <= kcol).astype(F32)
        before = (krow < kcol).astype(F32)
        zero = jnp.zeros((tq, 1), F32)
        nb = i * kpq
        for hh in range(hpt):
            sl = slice(hh * d, (hh + 1) * d)
            qs = (q_ref[:, sl].astype(F32) * scale).astype(BF16)
            do16 = do_ref[:, sl].astype(BF16)
            ltot = lt_ref[hh]

            def block(j, carry, valid, r0=0, qs=qs, do16=do16, ltot=ltot, sl=sl):
                dq, pl_sum, pg_sum = carry
                rows = _key_rows(j)
                kv = k_ref[rows, sl]
                vv = v_ref[rows, sl]
                ls, lg = _sb_logits(qs[r0:], kv, valid)
                pre = _ones_dot(upto, lg, ones_left=False, pieces=SB_PIECES)
                att = jnp.exp(ls + (ltot[r0:] - (pre + pl_sum)))
                if valid is not None:
                    att = jnp.where(valid, att, 0.0)
                g = att * _nt(do16[r0:], vv)
                gpre = _ones_dot(before, g, ones_left=False, pieces=SB_PIECES)
                sig = jnp.exp(ls)
                dz16 = (g - sig * (g + (gpre + pg_sum))).astype(BF16)
                if valid is not None:
                    dz16 = jnp.where(valid, dz16, jnp.zeros_like(dz16))
                dq = dq + _nn(dz16, kv)
                dk_acc[rows, sl] += _tn(dz16, qs[r0:])
                dv_acc[rows, sl] += _tn(att.astype(BF16), do16[r0:])
                return dq, pl_sum + pre[:, last:], pg_sum + (gpre[:, last:] + g[:, last:])

            start = jnp.clip(first_ref[pl.program_id(0) * hpt + hh, i], 0, nb)
            carry = lax.fori_loop(start, nb, lambda j, cr, block=block: block(j, cr, None),
                                  (jnp.zeros((tq, d), F32), zero, zero))
            for m in range(kpq):
                r0 = m * SB_KEYS
                sub = block(nb + m, tuple(a[r0:] for a in carry), diff[r0:] < -r0, r0)
                carry = tuple(jnp.concatenate([a[:r0], s], axis=0) if r0 else s for a, s in zip(carry, sub))
            dq_ref[:, sl] = (carry[0] * scale).astype(dq_ref.dtype)

        @pl.when(i == nq - 1)
        def _():
            dk_ref[...] = dk_acc[...].astype(dk_ref.dtype)
            dv_ref[...] = dv_acc[...].astype(dv_ref.dtype)

    qs = pl.BlockSpec((tq, LANES), lambda p, i: (i, p))
    ls = pl.BlockSpec((hpt, tq, 1), lambda p, i: (p, i, 0))
    ks = pl.BlockSpec((t, LANES), lambda p, i: (0, p))
    full = jax.ShapeDtypeStruct((t, hd), BF16)
    return pl.pallas_call(
        body,
        grid=(n_heads // hpt, nq),
        in_specs=[qs, ks, ks, ls, pl.BlockSpec(memory_space=pltpu.SMEM), qs],
        out_specs=[qs, ks, ks],
        out_shape=[full, full, full],
        scratch_shapes=[pltpu.VMEM((t, LANES), F32), pltpu.VMEM((t, LANES), F32)],
        compiler_params=_cparams(("arbitrary", "arbitrary")),
        name=name,
    )(q, k, v, lt, first, do)


def _row_tile(rows, cols):
    return _tile(rows, tuple(r for r in (2048, 1024, 512, 256, 128, 64, 32, 16, 8) if r * cols * 4 <= ADAM_BLOCK_BYTES))


def _sum_leading(x, *, name):
    n, rows, cols = x.shape
    tr = _row_tile(rows, cols)

    def body(x_ref, o_ref):
        acc = x_ref[0].astype(F32)
        for q in range(1, n):
            acc = acc + x_ref[q].astype(F32)
        o_ref[...] = acc

    return pl.pallas_call(
        body,
        grid=(rows // tr,),
        in_specs=[pl.BlockSpec((n, tr, cols), lambda i: (0, i, 0))],
        out_specs=pl.BlockSpec((tr, cols), lambda i: (i, 0)),
        out_shape=jax.ShapeDtypeStruct((rows, cols), F32),
        compiler_params=_cparams(("parallel",)),
        name=name,
    )(x)


def _pair_add(g4h, recv, c, *, out_dtype, name):
    n, _, rows, cols = g4h.shape
    tr = _row_tile(rows, cols)

    def body(c_ref, g_ref, r_ref, o_ref):
        o_ref[...] = (g_ref[...] + r_ref[...]).astype(o_ref.dtype)

    blk = pl.BlockSpec((1, tr, cols), lambda q, i, c_ref: (q, i, 0))
    return pl.pallas_call(
        body,
        grid_spec=pltpu.PrefetchScalarGridSpec(
            num_scalar_prefetch=1,
            grid=(n, rows // tr),
            in_specs=[pl.BlockSpec((1, None, tr, cols), lambda q, i, c_ref: (q, c_ref[0], i, 0)), blk],
            out_specs=blk),
        out_shape=jax.ShapeDtypeStruct((n, rows, cols), out_dtype),
        compiler_params=_cparams(("parallel", "parallel")),
        name=name,
    )(c.reshape(1).astype(jnp.int32), g4h, recv)


ANY = pl.BlockSpec(memory_space=pl.ANY)


def _other_chips(x, y):
    return [(1 - x, y), (x, 1 - y), (1 - x, 1 - y)]


def _gather_chips(shard, *, name):
    def body(x_ref, o_ref, send_sems, recv_sems, local_sem):
        x, y, c = lax.axis_index("x"), lax.axis_index("y"), lax.axis_index("c")
        me = 2 * x + y
        mine = pltpu.make_async_copy(x_ref, o_ref.at[me], local_sem)
        mine.start()
        chips = _other_chips(x, y)
        sends = [pltpu.make_async_remote_copy(src_ref=x_ref, dst_ref=o_ref.at[me], send_sem=send_sems.at[q],
                                              recv_sem=recv_sems.at[q], device_id=(px, py, c), device_id_type=MESH)
                 for q, (px, py) in enumerate(chips)]
        for cp in sends:
            cp.start()
        for q, (px, py) in enumerate(chips):
            pltpu.make_async_remote_copy(src_ref=x_ref, dst_ref=o_ref.at[2 * px + py], send_sem=send_sems.at[q],
                                         recv_sem=recv_sems.at[q], device_id=(px, py, c), device_id_type=MESH).wait_recv()
        for cp in sends:
            cp.wait_send()
        mine.wait()

    return pl.pallas_call(
        body,
        in_specs=[ANY],
        out_specs=ANY,
        out_shape=jax.ShapeDtypeStruct((4,) + shard.shape, shard.dtype),
        scratch_shapes=[pltpu.SemaphoreType.DMA((3,)), pltpu.SemaphoreType.DMA((3,)), pltpu.SemaphoreType.DMA],
        compiler_params=pltpu.CompilerParams(has_side_effects=True),
        name=name,
    )(shard)


def _comm_call(body, ins, out_shapes, n_sems, name):
    n = len(ins)

    def wrapped(*refs):
        body(refs[:n], refs[n:n + len(out_shapes)], refs[-2], refs[-1])

    return pl.pallas_call(
        wrapped,
        in_specs=[ANY] * n,
        out_specs=[ANY] * len(out_shapes),
        out_shape=out_shapes,
        scratch_shapes=[pltpu.SemaphoreType.DMA((n_sems,)), pltpu.SemaphoreType.DMA((n_sems,))],
        compiler_params=pltpu.CompilerParams(has_side_effects=True),
        name=name,
    )(*ins)


def _remote(send_sems, recv_sems, q, src, dst, to):
    return pltpu.make_async_remote_copy(src_ref=src, dst_ref=dst, send_sem=send_sems.at[q], recv_sem=recv_sems.at[q],
                                        device_id=to, device_id_type=MESH)


def _scatter_chips(parts, *, name):
    return _run_job(_scatter_job(parts), name)


def _scatter_job(parts):
    def sends(ins, outs, send_sems, recv_sems):
        x, y, c = lax.axis_index("x"), lax.axis_index("y"), lax.axis_index("c")
        return [_remote(send_sems, recv_sems, 3 * i + q, p.at[2 * px + py], o.at[2 * x + y], (px, py, c))
                for i, (p, o) in enumerate(zip(ins, outs)) for q, (px, py) in enumerate(_other_chips(x, y))]

    def start(ins, outs, send_sems, recv_sems):
        for cp in sends(ins, outs, send_sems, recv_sems):
            cp.start()

    def finish(ins, outs, send_sems, recv_sems):
        x, y, c = lax.axis_index("x"), lax.axis_index("y"), lax.axis_index("c")
        for i, (p, o) in enumerate(zip(ins, outs)):
            for q, (px, py) in enumerate(_other_chips(x, y)):
                _remote(send_sems, recv_sems, 3 * i + q, p.at[2 * x + y], o.at[2 * px + py], (px, py, c)).wait_recv()
        for cp in sends(ins, outs, send_sems, recv_sems):
            cp.wait_send()

    return _SideJob(parts, [jax.ShapeDtypeStruct(p.shape, p.dtype) for p in parts], 3 * len(parts), start, finish)


def _run_job(job, name):
    return _comm_call(lambda *refs: (job.start(*refs), job.finish(*refs)), job.ins, job.out_shapes, job.n_sems, name)


def _gather_job(shards):
    def sends(ins, outs, send_sems, recv_sems):
        x, y, c = lax.axis_index("x"), lax.axis_index("y"), lax.axis_index("c")
        return [_remote(send_sems, recv_sems, 6 * i + q, s.at[c], o.at[2 * x + y, c], (px, py, c))
                for i, (s, o) in enumerate(zip(ins, outs)) for q, (px, py) in enumerate(_other_chips(x, y))]

    def start(ins, outs, send_sems, recv_sems):
        for cp in sends(ins, outs, send_sems, recv_sems):
            cp.start()

    def finish(ins, outs, send_sems, recv_sems):
        x, y, c = lax.axis_index("x"), lax.axis_index("y"), lax.axis_index("c")
        sibling = (x, y, 1 - c)
        chips = _other_chips(x, y)
        copy = lambda q, src, dst, to: _remote(send_sems, recv_sems, q, src, dst, to)
        passed = []
        for i, (s, o) in enumerate(zip(ins, outs)):
            for q, (px, py) in enumerate(chips):
                slot = o.at[2 * px + py, c]
                copy(6 * i + q, s.at[c], slot, (px, py, c)).wait_recv()
                passed.append(copy(6 * i + 3 + q, slot, slot, sibling))
                passed[-1].start()
        for i, (s, o) in enumerate(zip(ins, outs)):
            for q, (px, py) in enumerate(chips):
                copy(6 * i + 3 + q, s.at[1 - c], o.at[2 * px + py, 1 - c], sibling).wait_recv()
        for cp in sends(ins, outs, send_sems, recv_sems) + passed:
            cp.wait_send()

    return _SideJob(shards, [jax.ShapeDtypeStruct((N_CHIPS,) + s.shape, s.dtype) for s in shards], 6 * len(shards),
                    start, finish)


def _swap_job(gs):
    def copies(ins, outs, send_sems, recv_sems):
        x, y, c = lax.axis_index("x"), lax.axis_index("y"), lax.axis_index("c")
        return [_remote(send_sems, recv_sems, i, g.at[pl.ds(0, g.shape[0]), 1 - c], o, (x, y, 1 - c))
                for i, (g, o) in enumerate(zip(ins, outs))]

    def start(*refs):
        for cp in copies(*refs):
            cp.start()

    def finish(*refs):
        for cp in copies(*refs):
            cp.wait()

    return _SideJob(gs, [jax.ShapeDtypeStruct((g.shape[0],) + g.shape[2:], g.dtype) for g in gs], len(gs), start, finish)


def _join_halves(halves, *, name):
    def body(ins, outs, send_sems, recv_sems):
        x, y, c = lax.axis_index("x"), lax.axis_index("y"), lax.axis_index("c")
        sibling = (x, y, 1 - c)
        sends = [_remote(send_sems, recv_sems, i, h, o.at[c], sibling) for i, (h, o) in enumerate(zip(ins, outs))]
        for cp in sends:
            cp.start()
        for i, (h, o) in enumerate(zip(ins, outs)):
            _remote(send_sems, recv_sems, i, h, o.at[1 - c], sibling).wait_recv()
        for cp in sends:
            cp.wait_send()

    return _comm_call(body, halves, [jax.ShapeDtypeStruct((2,) + h.shape, h.dtype) for h in halves], len(halves), name)


def _gather_all(v, *, name):
    def body(v_ref, o_ref, send_sems, recv_sems, local_sem):
        x, y, c = lax.axis_index("x"), lax.axis_index("y"), lax.axis_index("c")
        me = 4 * x + 2 * y + c
        mine = pltpu.make_async_copy(v_ref, o_ref.at[me], local_sem)
        mine.start()
        peers = [(x ^ (q >> 2 & 1), y ^ (q >> 1 & 1), c ^ (q & 1)) for q in range(1, 8)]
        sends = [pltpu.make_async_remote_copy(src_ref=v_ref, dst_ref=o_ref.at[me], send_sem=send_sems.at[q],
                                              recv_sem=recv_sems.at[q], device_id=peer, device_id_type=MESH)
                 for q, peer in enumerate(peers)]
        for cp in sends:
            cp.start()
        for q, (px, py, pc) in enumerate(peers):
            pltpu.make_async_remote_copy(src_ref=v_ref, dst_ref=o_ref.at[4 * px + 2 * py + pc], send_sem=send_sems.at[q],
                                         recv_sem=recv_sems.at[q], device_id=(px, py, pc), device_id_type=MESH).wait_recv()
        for cp in sends:
            cp.wait_send()
        mine.wait()

    return pl.pallas_call(
        body,
        in_specs=[ANY],
        out_specs=ANY,
        out_shape=jax.ShapeDtypeStruct((8,) + v.shape, v.dtype),
        scratch_shapes=[pltpu.SemaphoreType.DMA((7,)), pltpu.SemaphoreType.DMA((7,)), pltpu.SemaphoreType.DMA],
        compiler_params=pltpu.CompilerParams(has_side_effects=True),
        name=name,
    )(v)


WEIGHTS = ['ssm_norm_w', 'ssm_in_w', 'ssm_conv_w', 'ssm_conv_b', 'ssm_dt_bias', 'ssm_a_log', 'ssm_d',
           'ssm_gate_norm_w', 'ssm_out_w', 'kv_norm_w', 'w_k', 'w_v', 'attn_norm_w', 'w_q', 'w_o',
           'ffn_norm_w', 'ffn_up_w', 'ffn_conv_w', 'ffn_conv_b', 'ffn_down_w', 'final_norm_w']
SHARD_AXIS = {'ssm_norm_w': 1, 'ssm_in_w': 2, 'ssm_conv_w': 2, 'ssm_conv_b': 1, 'ssm_gate_norm_w': 1,
              'ssm_out_w': 1, 'w_k': 0, 'w_v': 0, 'w_q': 1, 'w_o': 1, 'ffn_up_w': 2, 'ffn_conv_w': 2,
              'ffn_down_w': 1}
BIG = ['ssm_in_w', 'ssm_out_w', 'w_k', 'w_v', 'w_q', 'w_o', 'ffn_up_w', 'ffn_down_w']
SMALL = [n for n in WEIGHTS if n in SHARD_AXIS and n not in BIG]
REPLICATED = [n for n in WEIGHTS if n not in SHARD_AXIS]
STACKED = ['ffn_up_w', 'ffn_down_w']
N_CHIPS = 4


PACK_ROWS = 16


def _piece_rows(n):
    return -(-n // (PACK_ROWS * LANES)) * PACK_ROWS


def _pack(arrs, dtype, row_mult):
    lead = arrs[0].shape[:-1]
    pieces, total = [], 0
    for a in arrs:
        n = a.shape[-1]
        rows = _piece_rows(n)
        a = a.astype(dtype)
        if rows * LANES != n:
            a = jnp.pad(a, [(0, 0)] * len(lead) + [(0, rows * LANES - n)])
        pieces.append(a.reshape(lead + (rows, LANES)))
        total += rows
    extra = -total % row_mult
    if extra:
        pieces.append(jnp.zeros(lead + (extra, LANES), dtype))
    return jnp.concatenate(pieces, axis=len(lead))


def _unpack(buf, shapes):
    lead = buf.shape[:-2]
    out, off = [], 0
    for shp in shapes:
        n = math.prod(shp)
        rows = _piece_rows(n)
        piece = lax.slice_in_dim(buf, off, off + rows, axis=len(lead)).reshape(lead + (rows * LANES,))
        out.append(piece[..., :n].reshape(lead + tuple(shp)))
        off += rows
    return out


def _set_slot(buf, piece, index):
    return lax.dynamic_update_slice_in_dim(buf, piece[None], index, axis=0)


def _from_shards(stacked, axis):
    return jnp.concatenate([stacked[j] for j in range(N_CHIPS)], axis=axis)


def _ffn_fwd(h, norm_w, w_up, conv_w, conv_b, w_down, tag, side=None):
    u = _rmsnorm_fwd(h, norm_w, name=f"ffn{tag}_norm")
    hid = _matmul(u, w_up, name=f"ffn{tag}_up")
    act, side_outs = _conv_glu_fwd(hid, conv_w, conv_b, side=side, name=f"ffn{tag}_glu")
    out = _matmul(act, w_down, add=h, name=f"ffn{tag}_down")
    return out, (u, hid, act), side_outs


def _ffn_bwd(h, saved, dout, norm_w, w_up, conv_w, conv_b, w_down, tag):
    u, hid, act = saved
    dact = _matmul(dout, w_down, tb=True, name=f"ffn{tag}_down_dx")
    dw_down = _matmul(act, dout, ta=True, name=f"ffn{tag}_down_dw")
    dhid, dwg, dwv, dbg, dbv = _conv_glu_bwd(hid, conv_w, conv_b, dact, name=f"ffn{tag}_glu_bwd")
    du = _matmul(dhid, w_up, tb=True, name=f"ffn{tag}_up_dx")
    dw_up = _matmul(u, dhid, ta=True, out_parts=N_CHIPS, name=f"ffn{tag}_up_dw")
    dh, (dnorm,) = _rmsnorm_bwd(h, [(du, norm_w)], dout, name=f"ffn{tag}_norm_bwd")
    return dh, dict(norm=dnorm[0], up=dw_up, conv_w=jnp.concatenate([dwg, dwv], axis=1),
                    conv_b=jnp.concatenate([dbg, dbv], axis=1)[0], down=dw_down)


class _Pieces:
    def __init__(self, local):
        self.c = lax.axis_index("c")
        self.chip = 2 * lax.axis_index("x") + lax.axis_index("y")
        self.shape, self.s16 = {}, {}
        for n in BIG:
            blk = local[n]
            layers = [(n, l, blk[l]) for l in range(blk.shape[0])] if n in STACKED else [(n, None, blk.reshape(blk.shape[-2:]))]
            for name, l, p in layers:
                self.shape[name, l] = p.shape
                self.s16[name, l] = p.astype(BF16).reshape(2, p.shape[0] // 2, p.shape[1])

    def gather_job(self, keys):
        return _gather_job([self.s16[k] for k in keys])

    def weights(self, keys, gathered):
        out = []
        for k, g in zip(keys, gathered):
            r, cc = self.shape[k]
            by_chip = _set_slot(g, self.s16[k], self.chip).reshape(N_CHIPS, r, cc)
            if k[0] == 'ssm_in_w':
                by_chip = by_chip.transpose(1, 0, 2).reshape(r, N_CHIPS * cc)
            elif k[0] != 'ffn_up_w':
                by_chip = by_chip.reshape(N_CHIPS * r, cc)
            out.append(by_chip)
        return out

    def by_halves(self, keys, grads):
        gs = []
        for k, g in zip(keys, grads):
            r, cc = self.shape[k]
            if k[0] == 'ssm_in_w':
                g = g.reshape(r, N_CHIPS, cc).transpose(1, 0, 2)
            gs.append(g.reshape(N_CHIPS, 2, r // 2, cc))
        return gs

    def pair_sums(self, gs, recv, tag):
        return [_pair_add(g, rv, self.c, out_dtype=BF16, name=f"rs_pair_add_{tag}{i}") for i, (g, rv) in enumerate(zip(gs, recv))]

    def chip_sums(self, pairs, scattered, tag):
        return [_sum_leading(_set_slot(s, lax.dynamic_index_in_dim(p, self.chip, axis=0, keepdims=False), self.chip),
                             name=f"rs_chip_sum_{tag}{i}") for i, (s, p) in enumerate(zip(scattered, pairs))]

    def shards(self, keys, halves):
        joined = _join_halves(halves, name="rs_half_join")
        return {k: _set_slot(j, h, self.c).reshape(self.shape[k]) for k, h, j in zip(keys, halves, joined)}


def _step(x, target, w, pieces):
    t = x.shape[0]
    g_n, heads = SSM_GROUPS, SSM_HEADS
    r_h = heads // g_n
    di = D_INNER
    zx_cols = di + CONV_DIM
    k_in = [('ssm_in_w', None)]
    k_ffn0 = [('ssm_out_w', None), ('ffn_up_w', 0), ('ffn_down_w', 0)]
    k_qkv = [('w_k', None), ('w_v', None), ('w_q', None)]
    k_late = [('w_o', None), ('ffn_up_w', 1), ('ffn_down_w', 1)]
    (w_in,) = pieces.weights(k_in, _run_job(pieces.gather_job(k_in), "gather_ssm_in"))
    w_zx = w_in[:, :zx_cols]
    w_dt = jnp.pad(w_in[:, zx_cols:], ((0, 0), (0, LANES - heads)))
    conv_w, conv_b = w['ssm_conv_w'][0], w['ssm_conv_b'][0]
    hp = jnp.stack([w['ssm_dt_bias'][0], w['ssm_a_log'][0], w['ssm_d'][0]], axis=0).reshape(3, g_n, r_h)
    hpc, hpr = hp.transpose(1, 0, 2), hp.transpose(1, 2, 0)

    h0 = x
    u0 = _rmsnorm_fwd(h0, w['ssm_norm_w'][0], name="ssm_norm")
    zx = _matmul(u0, w_zx, name="ssm_in_zx")
    dt_raw = _matmul(u0, w_dt, name="ssm_in_dt")[:, :heads]
    dtg = dt_raw.reshape(t, g_n, r_h)
    dtc, dtr = dtg.transpose(1, 0, 2), dtg.transpose(1, 2, 0)
    xbc = _conv_silu_fwd(zx, conv_w, conv_b, x_off=di, name="ssm_conv")
    y, prev, got = _ssd_fwd(xbc, dtc, dtr, hpc, hpr, side=pieces.gather_job(k_ffn0), name="ssd_fwd")
    w_out, w_up0, w_down0 = pieces.weights(k_ffn0, got)
    yn = _gate_norm_fwd(y, zx, w['ssm_gate_norm_w'][0], name="ssm_gate_norm")
    h1 = _matmul(yn, w_out, add=h0, name="ssm_out")
    h2, ffn0, got = _ffn_fwd(h1, w['ffn_norm_w'][0], w_up0, w['ffn_conv_w'][0], w['ffn_conv_b'][0], w_down0, 0,
                             side=pieces.gather_job(k_qkv))
    w_k, w_v, w_q = pieces.weights(k_qkv, got)
    hk = _rmsnorm_fwd(h2, w['kv_norm_w'], name="kv_norm")
    qn = _rmsnorm_fwd(h2, w['attn_norm_w'][0], name="attn_norm")
    k2 = _matmul(hk, w_k, out_dtype=BF16, name="attn_k")
    v2 = _matmul(hk, w_v, out_dtype=BF16, name="attn_v")
    q2 = _matmul(qn, w_q, out_dtype=BF16, name="attn_q")
    o2, lt, first, got = _sb_fwd(q2, k2, v2, SB_HEADS, side=pieces.gather_job(k_late), name="sb_fwd")
    w_o, w_up1, w_down1 = pieces.weights(k_late, got)
    h3 =_matmul(o2, w_o, add=h2, name="attn_o")
    h4, ffn1, _ = _ffn_fwd(h3, w['ffn_norm_w'][1], w_up1, w['ffn_conv_w'][1], w['ffn_conv_b'][1], w_down1, 1)
    loss_p, dh4, d_final = _loss_head(h4, w['final_norm_w'], target, name="loss_head")

    dh3, g1 = _ffn_bwd(h3, ffn1, dh4, w['ffn_norm_w'][1], w_up1, w['ffn_conv_w'][1], w['ffn_conv_b'][1], w_down1, 1)
    do2 = _matmul(dh3, w_o, tb=True, out_dtype=BF16, name="attn_o_dx")
    dw_o = _matmul(o2, dh3, ta=True, name="attn_o_dw")
    dq2, dk2, dv2 = _sb_bwd(q2, k2, v2, lt, first, do2, SB_HEADS, name="sb_bwd")
    dqn = _matmul(dq2, w_q, tb=True, name="attn_q_dx")
    dw_q = _matmul(qn, dq2, ta=True, name="attn_q_dw")
    dhk = _matmul(dk2, w_k, tb=True, name="attn_k_dx")
    dhk = _matmul(dv2, w_v, tb=True, add=dhk, name="attn_v_dx")
    dw_k = _matmul(hk, dk2, ta=True, name="attn_k_dw")
    dw_v = _matmul(hk, dv2, ta=True, name="attn_v_dw")
    dh2, (d_attn_norm, d_kv_norm) = _rmsnorm_bwd(h2, [(dqn, w['attn_norm_w'][0]), (dhk, w['kv_norm_w'])], dh3,
                                                 name="attn_norms_bwd")
    dh1, g0 = _ffn_bwd(h1, ffn0, dh2, w['ffn_norm_w'][0], w_up0, w['ffn_conv_w'][0], w['ffn_conv_b'][0], w_down0, 0)
    dyn = _matmul(dh1, w_out, tb=True, name="ssm_out_dx")
    dw_out = _matmul(yn, dh1, ta=True, name="ssm_out_dw")
    k_done = k_qkv + k_late + k_ffn0
    gs_done = pieces.by_halves(k_done, [dw_k, dw_v, dw_q, dw_o, g1['up'], g1['down'], dw_out, g0['up'], g0['down']])
    dy, dz, d_gate, recv = _gate_norm_bwd(y, zx, w['ssm_gate_norm_w'][0], dyn, side=_swap_job(gs_done),
                                          name="ssm_gate_norm_bwd")
    pairs_done = pieces.pair_sums(gs_done, recv, "a")
    dxs, dbm, dcm, ddt_g, hg, scattered_done = _ssd_bwd(xbc, dtc, dtr, hpc, hpr, prev, dy,
                                                        side=_scatter_job(pairs_done), name="ssd_bwd")
    dzx, d_conv_w, d_conv_b = _conv_silu_bwd(zx, conv_w, conv_b, [dxs, dbm, dcm], x_off=di, into=dz, name="ssm_conv_bwd")
    ddt = jnp.pad(ddt_g.transpose(1, 0, 2).reshape(t, heads), ((0, 0), (0, LANES - heads)))
    du0 = _matmul(dzx, w_zx, tb=True, name="ssm_in_zx_dx")
    du0 = _matmul(ddt, w_dt, tb=True, add=du0, name="ssm_in_dt_dx")
    dw_in = jnp.concatenate([_matmul(u0, dzx, ta=True, name="ssm_in_zx_dw"),
                             _matmul(u0, ddt, ta=True, name="ssm_in_dt_dw")[:, :heads]], axis=1)
    dx, (d_ssm_norm,) = _rmsnorm_bwd(h0, [(du0, w['ssm_norm_w'][0])], dh1, name="ssm_norm_bwd")

    gs_in = pieces.by_halves(k_in, [dw_in])
    pairs_in = pieces.pair_sums(gs_in, _run_job(_swap_job(gs_in), "rs_pair_swap_b"), "b")
    halves = (pieces.chip_sums(pairs_done, scattered_done, "a")
              + pieces.chip_sums(pairs_in, _scatter_chips(pairs_in, name="rs_chip_scatter_b"), "b"))
    big_grads = pieces.shards(k_done + k_in, halves)

    hgr = hg.transpose(1, 0, 2).reshape(3, heads)
    grads = {
        'ssm_norm_w': d_ssm_norm, 'ssm_conv_w': d_conv_w[None], 'ssm_conv_b': d_conv_b,
        'ssm_dt_bias': hgr[0:1], 'ssm_a_log': hgr[1:2], 'ssm_d': hgr[2:3], 'ssm_gate_norm_w': d_gate,
        'kv_norm_w': d_kv_norm[0], 'attn_norm_w': d_attn_norm, 'ffn_norm_w': jnp.stack([g0['norm'], g1['norm']]),
        'ffn_conv_w': jnp.stack([g0['conv_w'], g1['conv_w']]), 'ffn_conv_b': jnp.stack([g0['conv_b'], g1['conv_b']]),
        'final_norm_w': d_final[0],
    }
    return loss_p, dx, grads, big_grads


def kernel(x, ssm_norm_w, ssm_in_w, ssm_conv_w, ssm_conv_b, ssm_dt_bias, ssm_a_log, ssm_d, ssm_gate_norm_w, ssm_out_w, kv_norm_w, w_k, w_v, attn_norm_w, w_q, w_o, ffn_norm_w, ffn_up_w, ffn_conv_w, ffn_conv_b, ffn_down_w, final_norm_w, loss_target, m_ssm_norm_w, m_ssm_in_w, m_ssm_conv_w, m_ssm_conv_b, m_ssm_dt_bias, m_ssm_a_log, m_ssm_d, m_ssm_gate_norm_w, m_ssm_out_w, m_kv_norm_w, m_w_k, m_w_v, m_attn_norm_w, m_w_q, m_w_o, m_ffn_norm_w, m_ffn_up_w, m_ffn_conv_w, m_ffn_conv_b, m_ffn_down_w, m_final_norm_w, v_ssm_norm_w, v_ssm_in_w, v_ssm_conv_w, v_ssm_conv_b, v_ssm_dt_bias, v_ssm_a_log, v_ssm_d, v_ssm_gate_norm_w, v_ssm_out_w, v_kv_norm_w, v_w_k, v_w_v, v_attn_norm_w, v_w_q, v_w_o, v_ffn_norm_w, v_ffn_up_w, v_ffn_conv_w, v_ffn_conv_b, v_ffn_down_w, v_final_norm_w):
    args = (ssm_norm_w, ssm_in_w, ssm_conv_w, ssm_conv_b, ssm_dt_bias, ssm_a_log, ssm_d, ssm_gate_norm_w, ssm_out_w, kv_norm_w, w_k, w_v, attn_norm_w, w_q, w_o, ffn_norm_w, ffn_up_w, ffn_conv_w, ffn_conv_b, ffn_down_w, final_norm_w)
    moms = (m_ssm_norm_w, m_ssm_in_w, m_ssm_conv_w, m_ssm_conv_b, m_ssm_dt_bias, m_ssm_a_log, m_ssm_d, m_ssm_gate_norm_w, m_ssm_out_w, m_kv_norm_w, m_w_k, m_w_v, m_attn_norm_w, m_w_q, m_w_o, m_ffn_norm_w, m_ffn_up_w, m_ffn_conv_w, m_ffn_conv_b, m_ffn_down_w, m_final_norm_w)
    vels = (v_ssm_norm_w, v_ssm_in_w, v_ssm_conv_w, v_ssm_conv_b, v_ssm_dt_bias, v_ssm_a_log, v_ssm_d, v_ssm_gate_norm_w, v_ssm_out_w, v_kv_norm_w, v_w_k, v_w_v, v_attn_norm_w, v_w_q, v_w_o, v_ffn_norm_w, v_ffn_up_w, v_ffn_conv_w, v_ffn_conv_b, v_ffn_down_w, v_final_norm_w)
    local = dict(zip(WEIGHTS, args))
    m_in = dict(zip(WEIGHTS, moms))
    v_in = dict(zip(WEIGHTS, vels))
    chip = 2 * lax.axis_index("x") + lax.axis_index("y")

    full = {n: local[n] for n in REPLICATED}
    small32 = _gather_chips(_pack([local[n].reshape(-1) for n in SMALL], F32, 8), name="gather_small")
    for n, st in zip(SMALL, _unpack(small32, [local[n].shape for n in SMALL])):
        full[n] = _from_shards(st, SHARD_AXIS[n])

    pieces = _Pieces(local)
    loss_p, dx, grads, big_grads = _step(x[0], loss_target[0], full, pieces)
    gshard = {}
    for n in BIG:
        if n in STACKED:
            gshard[n] = [big_grads[n, l] for l in range(local[n].shape[0])]
        else:
            gshard[n] = big_grads[n, None].reshape(local[n].shape)

    small = SMALL + REPLICATED
    rep = _pack([loss_p.reshape(-1)] + [grads[n].reshape(-1) for n in small], F32, 8)
    tot = _sum_leading(_gather_all(rep, name="ar_gather"), name="ar_sum")
    parts = _unpack(tot, [(LANES,)] + [grads[n].shape for n in small])
    loss = jnp.sum(parts[0])
    for n, g in zip(small, parts[1:]):
        if n in SHARD_AXIS:
            size = local[n].shape[SHARD_AXIS[n]]
            g = lax.dynamic_slice_in_dim(g, chip * size, size, axis=SHARD_AXIS[n])
        gshard[n] = g

    grads_out, deltas, new_m, new_v = [], [], [], []
    for n in WEIGHTS:
        if n in STACKED:
            g, d, nm, nv = _adamw_layers(local[n], gshard[n], m_in[n], v_in[n], name=f"adamw_{n}")
        else:
            g = gshard[n]
            d, nm, nv = _adamw(local[n], g, m_in[n], v_in[n], name=f"adamw_{n}")
        grads_out.append(g)
        deltas.append(d)
        new_m.append(nm)
        new_v.append(nv)
    return (loss, dx[None], *grads_out, *deltas, *new_m, *new_v)
```

```python
import functools
import math

import jax
import jax.numpy as jnp
from jax import lax
from jax.experimental import pallas as pl
from jax.experimental.pallas import tpu as pltpu

D_INNER = 2048
SSM_HEAD_DIM = 64
SSM_HEADS = 32
SSM_GROUPS = 4
SSM_STATE = 128
SSM_CHUNK = 128
GN = SSM_GROUPS * SSM_STATE
CONV_DIM = D_INNER + 2 * GN
SB_HEADS = 16
EPS = 1e-6
ADAM_LR = 0.001
ADAM_B1 = 0.9
ADAM_B2 = 0.999
ADAM_EPS = 1e-08
ADAM_WD = 0.01
ADAM_STEP = 10

LANES = 128
SUBLANES = 8
VMEM_LIMIT = 48 * 1024 * 1024
ADAM_BLOCK_BYTES = 1 << 20
F32 = jnp.float32
BF16 = jnp.bfloat16
MESH = pl.DeviceIdType.MESH


def _cparams(sem=None):
    return pltpu.CompilerParams(dimension_semantics=sem, vmem_limit_bytes=VMEM_LIMIT)


class _SideJob:
    def __init__(self, ins, out_shapes, n_sems, start, finish):
        self.ins, self.out_shapes, self.n_sems, self.start, self.finish = ins, out_shapes, n_sems, start, finish


def _call(body, *, grid, in_specs, out_specs, out_shape, scratch_shapes=(), sem, name, args, side=None):
    in_specs, out_specs, out_shape, scratch_shapes = list(in_specs), list(out_specs), list(out_shape), list(scratch_shapes)
    n_in, n_out = len(in_specs), len(out_specs)
    if side is None:
        outs = pl.pallas_call(body, grid=grid, in_specs=in_specs, out_specs=out_specs, out_shape=out_shape,
                              scratch_shapes=scratch_shapes, compiler_params=_cparams(sem), name=name)(*args)
        return list(outs), []
    k_in, k_out = len(side.ins), len(side.out_shapes)

    def wrapped(*refs):
        ins, s_ins = refs[:n_in], refs[n_in:n_in + k_in]
        o0 = n_in + k_in
        outs, s_outs = refs[o0:o0 + n_out], refs[o0 + n_out:o0 + n_out + k_out]
        scratch, send_sems, recv_sems = refs[o0 + n_out + k_out:-2], refs[-2], refs[-1]
        ids = [pl.program_id(a) for a in range(len(grid))]
        first = functools.reduce(jnp.logical_and, [p == 0 for p in ids])
        last = functools.reduce(jnp.logical_and, [p == g - 1 for p, g in zip(ids, grid)])

        @pl.when(first)
        def _():
            side.start(s_ins, s_outs, send_sems, recv_sems)

        body(*ins, *outs, *scratch)

        @pl.when(last)
        def _():
            side.finish(s_ins, s_outs, send_sems, recv_sems)

    outs = pl.pallas_call(
        wrapped, grid=grid, in_specs=in_specs + [ANY] * k_in, out_specs=out_specs + [ANY] * k_out,
        out_shape=out_shape + list(side.out_shapes),
        scratch_shapes=scratch_shapes + [pltpu.SemaphoreType.DMA((side.n_sems,)), pltpu.SemaphoreType.DMA((side.n_sems,))],
        compiler_params=_cparams(tuple("arbitrary" for _ in grid)), name=name)(*args, *side.ins)
    return list(outs[:n_out]), list(outs[n_out:])


def _tile(n, cands):
    for c in cands:
        if n % c == 0:
            return c
    return n


def _nt(a, b):
    return lax.dot_general(a, b, (((1,), (1,)), ((), ())), preferred_element_type=F32)


def _tn(a, b):
    return lax.dot_general(a, b, (((0,), (0,)), ((), ())), preferred_element_type=F32)


def _nn(a, b):
    return jnp.dot(a, b, preferred_element_type=F32)


def _split(x, pieces):
    out = []
    for _ in range(pieces - 1):
        h = x.astype(BF16)
        out.append(h)
        x = x - h.astype(F32)
    out.append(x.astype(BF16))
    return out


def _ones_dot(ones, x, *, ones_left, pieces=3):
    o16 = ones.astype(BF16)
    acc = None
    for piece in _split(x, pieces):
        term = _nn(o16, piece) if ones_left else _nn(piece, o16)
        acc = term if acc is None else acc + term
    return acc


def _row_sums(x, pieces=2):
    return _ones_dot(jnp.ones((x.shape[1], LANES), F32), x, ones_left=False, pieces=pieces)


def _softplus(x):
    return jnp.maximum(x, 0.0) + jnp.log(1.0 + jnp.exp(-jnp.abs(x)))


def _sigmoid(x):
    return 0.5 * jnp.tanh(0.5 * x) + 0.5


MM_TILE_MAX = 1408
MM_VMEM_BUDGET = 40 * 1024 * 1024


def _divisors(n, cap):
    out = [d for d in range(min(cap, n) // LANES * LANES, 0, -LANES) if n % d == 0]
    return out or [n]


def _mm_tiles(m, n, k, a_bytes, b_bytes, o_bytes, add_bytes):
    best = None
    for tm in _divisors(m, MM_TILE_MAX):
        for tn in _divisors(n, MM_TILE_MAX):
            for tk in _divisors(k, MM_TILE_MAX):
                vmem = 2 * (tm * tk * a_bytes + tk * tn * b_bytes + tm * tn * (o_bytes + add_bytes)) + tm * tn * 4
                if vmem > MM_VMEM_BUDGET:
                    continue
                score = (tm * tn * tk, tm * tn)
                if best is None or score > best[0]:
                    best = (score, (tm, tn, tk))
    return best[1]


def _matmul(a, b, *, ta=False, tb=False, add=None, out_dtype=F32, out_parts=1, name):
    a_parts = a.shape[0] if a.ndim == 3 else 1
    b_parts = b.shape[0] if b.ndim == 3 else 1
    assert not (ta and a_parts > 1)
    a2, b2 = a.shape[-2:], b.shape[-2:]
    m, k = (a2[1], a2[0]) if ta else (a2[0], a2[1] * a_parts)
    n, kb = (b2[0], b2[1] * b_parts) if tb else (b2[1] * b_parts, b2[0])
    assert kb == k, (a.shape, b.shape)
    n_unit = math.gcd(n // out_parts, n if tb else b2[1])
    k_unit = math.gcd(k // a_parts, b2[1] if tb else k)
    tm, tn, tk = _mm_tiles(m, n_unit, k_unit, a.dtype.itemsize, b.dtype.itemsize, jnp.dtype(out_dtype).itemsize,
                           0 if add is None else add.dtype.itemsize)
    nk = k // tk
    ka, kbp = (k // a_parts) // tk, (k // b_parts) // tk
    nb, no = (n // b_parts) // tn, (n // out_parts) // tn

    def body(*refs):
        if add is None:
            a_ref, b_ref, o_ref = refs[:3]
            add_ref = None
        else:
            a_ref, b_ref, add_ref, o_ref = refs[:4]
        kk = pl.program_id(2)
        dn = (((0 if ta else 1,), (1 if tb else 0,)), ((), ()))
        prod = lax.dot_general(a_ref[...].astype(BF16), b_ref[...].astype(BF16), dn, preferred_element_type=F32)

        def finish(r):
            if add_ref is not None:
                r = r + add_ref[...].astype(F32)
            o_ref[...] = r.astype(o_ref.dtype)

        if nk == 1:
            finish(prod)
            return
        acc_ref = refs[-1]

        @pl.when(kk == 0)
        def _():
            acc_ref[...] = prod

        @pl.when(jnp.logical_and(kk > 0, kk < nk - 1))
        def _():
            acc_ref[...] += prod

        @pl.when(kk == nk - 1)
        def _():
            finish(acc_ref[...] + prod)

    if ta:
        a_spec = pl.BlockSpec((tk, tm), lambda i, j, kk: (kk, i))
    elif a_parts > 1:
        a_spec = pl.BlockSpec((None, tm, tk), lambda i, j, kk: (kk // ka, i, kk % ka))
    else:
        a_spec = pl.BlockSpec((tm, tk), lambda i, j, kk: (i, kk))
    if b_parts == 1:
        b_spec = pl.BlockSpec((tn, tk), lambda i, j, kk: (j, kk)) if tb else pl.BlockSpec((tk, tn), lambda i, j, kk: (kk, j))
    elif tb:
        b_spec = pl.BlockSpec((None, tn, tk), lambda i, j, kk: (kk // kbp, j, kk % kbp))
    else:
        b_spec = pl.BlockSpec((None, tk, tn), lambda i, j, kk: (j // nb, kk, j % nb))
    if out_parts > 1:
        o_spec = pl.BlockSpec((None, tm, tn), lambda i, j, kk: (j // no, i, j % no))
        o_shape = jax.ShapeDtypeStruct((out_parts, m, n // out_parts), out_dtype)
    else:
        o_spec = pl.BlockSpec((tm, tn), lambda i, j, kk: (i, j))
        o_shape = jax.ShapeDtypeStruct((m, n), out_dtype)
    in_specs = [a_spec, b_spec]
    args = [a, b]
    if add is not None:
        in_specs.append(pl.BlockSpec((tm, tn), lambda i, j, kk: (i, j)))
        args.append(add)
    return pl.pallas_call(
        body,
        grid=(m // tm, n // tn, nk),
        in_specs=in_specs,
        out_specs=o_spec,
        out_shape=o_shape,
        scratch_shapes=[pltpu.VMEM((tm, tn), F32)] if nk > 1 else [],
        compiler_params=_cparams(("parallel", "parallel", "arbitrary")),
        name=name,
    )(*args)


def _rmsnorm_fwd(x, w, *, name):
    t, d = x.shape
    tb = _tile(t, (512, 256, 128))

    def body(x_ref, w_ref, o_ref):
        xv = x_ref[...]
        r = lax.rsqrt(jnp.mean(xv * xv, axis=-1, keepdims=True) + EPS)
        o_ref[...] = (xv * r * w_ref[...]).astype(o_ref.dtype)

    return pl.pallas_call(
        body,
        grid=(t // tb,),
        in_specs=[pl.BlockSpec((tb, d), lambda i: (i, 0)), pl.BlockSpec((1, d), lambda i: (0, 0))],
        out_specs=pl.BlockSpec((tb, d), lambda i: (i, 0)),
        out_shape=jax.ShapeDtypeStruct((t, d), BF16),
        compiler_params=_cparams(("parallel",)),
        name=name,
    )(x, w.reshape(1, d))


def _rmsnorm_bwd(x, dys, dres, *, name):
    t, d = x.shape
    tb = _tile(t, (256, 128))
    nn = len(dys)
    has_res = dres is not None

    def body(*refs):
        x_ref = refs[0]
        dy_refs = refs[1:1 + nn]
        w_refs = refs[1 + nn:1 + 2 * nn]
        pos = 1 + 2 * nn
        res_ref = refs[pos] if has_res else None
        pos += 1 if has_res else 0
        dx_ref = refs[pos]
        dw_refs = refs[pos + 1:pos + 1 + nn]
        i = pl.program_id(0)
        xv = x_ref[...]
        r = lax.rsqrt(jnp.mean(xv * xv, axis=-1, keepdims=True) + EPS)
        xn = xv * r
        dx = res_ref[...] if has_res else jnp.zeros_like(xv)
        for q in range(nn):
            dy = dy_refs[q][...].astype(F32)
            g = dy * w_refs[q][...]
            dx = dx + r * (g - xn * jnp.mean(g * xn, axis=-1, keepdims=True))
            dwp = jnp.sum(dy * xn, axis=0, keepdims=True)

            @pl.when(i == 0)
            def _(q=q, dwp=dwp):
                dw_refs[q][...] = dwp

            @pl.when(i > 0)
            def _(q=q, dwp=dwp):
                dw_refs[q][...] += dwp
        dx_ref[...] = dx

    row = pl.BlockSpec((tb, d), lambda i: (i, 0))
    vec = pl.BlockSpec((1, d), lambda i: (0, 0))
    in_specs = [row] + [row] * nn + [vec] * nn + ([row] if has_res else [])
    args = [x] + [p[0] for p in dys] + [p[1].reshape(1, d) for p in dys] + ([dres] if has_res else [])
    outs = pl.pallas_call(
        body,
        grid=(t // tb,),
        in_specs=in_specs,
        out_specs=[row] + [vec] * nn,
        out_shape=[jax.ShapeDtypeStruct((t, d), F32)] + [jax.ShapeDtypeStruct((1, d), F32)] * nn,
        compiler_params=_cparams(("arbitrary",)),
        name=name,
    )(*args)
    return outs[0], list(outs[1:])


def _loss_head(x, w, target, *, name):
    t, d = x.shape
    tb = _tile(t, (256, 128))

    def body(x_ref, w_ref, t_ref, loss_ref, dx_ref, dw_ref):
        i = pl.program_id(0)
        xv = x_ref[...]
        wv = w_ref[...]
        r = lax.rsqrt(jnp.mean(xv * xv, axis=-1, keepdims=True) + EPS)
        xn = xv * r
        e = xn * wv - t_ref[...]
        lp = 0.5 * jnp.sum(jnp.mean(e * e, axis=-1, keepdims=True), axis=0, keepdims=True)
        dy = e * (1.0 / d)
        g = dy * wv
        dx_ref[...] = r * (g - xn * jnp.mean(g * xn, axis=-1, keepdims=True))
        dwp = jnp.sum(dy * xn, axis=0, keepdims=True)
        lpv = jnp.broadcast_to(lp, (1, LANES)) * (1.0 / LANES)

        @pl.when(i == 0)
        def _():
            dw_ref[...] = dwp
            loss_ref[...] = lpv

        @pl.when(i > 0)
        def _():
            dw_ref[...] += dwp
            loss_ref[...] += lpv

    row = pl.BlockSpec((tb, d), lambda i: (i, 0))
    vec = pl.BlockSpec((1, d), lambda i: (0, 0))
    return pl.pallas_call(
        body,
        grid=(t // tb,),
        in_specs=[row, vec, row],
        out_specs=[pl.BlockSpec((1, LANES), lambda i: (0, 0)), row, vec],
        out_shape=[jax.ShapeDtypeStruct((1, LANES), F32), jax.ShapeDtypeStruct((t, d), F32),
                   jax.ShapeDtypeStruct((1, d), F32)],
        compiler_params=_cparams(("arbitrary",)),
        name=name,
    )(x, w.reshape(1, d), target)


ROW_CHUNK = 64
PAD = SUBLANES


def _shifted(pad_ref, r0, rows, back):
    return pad_ref[pl.ds(PAD + r0 - back, rows), :]


def _conv_taps(pad_ref, w_ref, r0, rows, kw):
    acc = None
    for j in range(kw):
        term = _shifted(pad_ref, r0, rows, kw - 1 - j) * w_ref[j:j + 1, :]
        acc = term if acc is None else acc + term
    return acc


def _fill_pad(pad_ref, x_ref, t):
    pad_ref[0:PAD, :] = jnp.zeros((PAD, pad_ref.shape[1]), F32)
    pad_ref[pl.ds(PAD + t, PAD), :] = jnp.zeros((PAD, pad_ref.shape[1]), F32)
    pad_ref[pl.ds(PAD, t), :] = x_ref[...].astype(F32)


def _conv_silu_fwd(x, w, b, *, x_off=0, name):
    t = x.shape[0]
    kw, c = w.shape
    cw = _tile(math.gcd(c, x_off) if x_off else c, (256, 128))
    ob = x_off // cw
    rc = _tile(t, (ROW_CHUNK,))

    def body(x_ref, w_ref, b_ref, o_ref, pad_ref):
        _fill_pad(pad_ref, x_ref, t)
        for r0 in range(0, t, rc):
            pre = _conv_taps(pad_ref, w_ref, r0, rc, kw) + b_ref[...]
            o_ref[pl.ds(r0, rc), :] = pre * _sigmoid(pre)

    strip = pl.BlockSpec((t, cw), lambda i: (0, i))
    return pl.pallas_call(
        body,
        grid=(c // cw,),
        in_specs=[pl.BlockSpec((t, cw), lambda i: (0, i + ob)), pl.BlockSpec((kw, cw), lambda i: (0, i)),
                  pl.BlockSpec((1, cw), lambda i: (0, i))],
        out_specs=strip,
        out_shape=jax.ShapeDtypeStruct((t, c), F32),
        scratch_shapes=[pltpu.VMEM((t + 2 * PAD, cw), F32)],
        compiler_params=_cparams(("parallel",)),
        name=name,
    )(x, w, b.reshape(1, c))


def _conv_bwd_core(dpre_pad_ref, x_pad_ref, w_ref, dx_ref, dw_ref, db_ref, t, rc, kw):
    cw = dx_ref.shape[1]

    def fold(a):
        return jnp.sum(a.reshape(rc // SUBLANES, SUBLANES, cw), axis=0) if rc % SUBLANES == 0 else jnp.sum(a, axis=0, keepdims=True)

    dws = [None] * kw
    dbs = None
    for r0 in range(0, t, rc):
        dpre = dpre_pad_ref[pl.ds(PAD + r0, rc), :]
        dx = None
        for j in range(kw):
            s = kw - 1 - j
            term = dpre_pad_ref[pl.ds(PAD + r0 + s, rc), :] * w_ref[j:j + 1, :]
            dx = term if dx is None else dx + term
            part = fold(dpre * _shifted(x_pad_ref, r0, rc, s))
            dws[j] = part if dws[j] is None else dws[j] + part
        part = fold(dpre)
        dbs = part if dbs is None else dbs + part
        dx_ref[pl.ds(r0, rc), :] = dx
    for j in range(kw):
        dw_ref[j:j + 1, :] = jnp.sum(dws[j], axis=0, keepdims=True)
    db_ref[...] = jnp.sum(dbs, axis=0, keepdims=True)


def _conv_silu_bwd(x, w, b, dact, *, x_off=0, into=None, name):
    t = x.shape[0]
    kw, c = w.shape
    parts = dact if isinstance(dact, (list, tuple)) else [dact]
    widths = [p.shape[1] for p in parts]
    assert sum(widths) == c
    cw = _tile(functools.reduce(math.gcd, widths + [x_off or c]), (256, 128) if len(parts) == 1 else (128,))
    ob = x_off // cw
    rc = _tile(t, (ROW_CHUNK,))
    firsts = [sum(widths[:p]) // cw for p in range(len(parts))]
    counts = [wd // cw for wd in widths]
    n_p = len(parts)

    def body(x_ref, w_ref, b_ref, *rest):
        da_refs = rest[:n_p]
        dx_ref, dw_ref, db_ref, xpad_ref, dpad_ref = rest[-5 - (n_p > 1):][:5]
        if n_p > 1:
            da_ref = rest[-1]
            i = pl.program_id(0)
            for p in range(n_p):
                @pl.when(jnp.logical_and(i >= firsts[p], i < firsts[p] + counts[p]))
                def _(p=p):
                    da_ref[...] = da_refs[p][...]
        else:
            da_ref = da_refs[0]
        _fill_pad(xpad_ref, x_ref, t)
        dpad_ref[0:PAD, :] = jnp.zeros((PAD, cw), F32)
        dpad_ref[pl.ds(PAD + t, PAD), :] = jnp.zeros((PAD, cw), F32)
        for r0 in range(0, t, rc):
            pre = _conv_taps(xpad_ref, w_ref, r0, rc, kw) + b_ref[...]
            sg = _sigmoid(pre)
            dpad_ref[pl.ds(PAD + r0, rc), :] = da_ref[pl.ds(r0, rc), :] * (sg * (1.0 + pre * (1.0 - sg)))
        _conv_bwd_core(dpad_ref, xpad_ref, w_ref, dx_ref, dw_ref, db_ref, t, rc, kw)

    strip = pl.BlockSpec((t, cw), lambda i: (0, i))
    wspec = pl.BlockSpec((kw, cw), lambda i: (0, i))
    bspec = pl.BlockSpec((1, cw), lambda i: (0, i))
    xspec = pl.BlockSpec((t, cw), lambda i: (0, i + ob))
    dspecs = [pl.BlockSpec((t, cw), lambda i, f=f, n=n: (0, jnp.clip(i - f, 0, n - 1))) for f, n in zip(firsts, counts)]
    extra = {} if into is None else dict(input_output_aliases={3 + n_p: 0})
    pad = pltpu.VMEM((t + 2 * PAD, cw), F32)
    return pl.pallas_call(
        body,
        grid=(c // cw,),
        in_specs=[xspec, wspec, bspec] + dspecs + ([] if into is None else [ANY]),
        out_specs=[strip if into is None else xspec, wspec, bspec],
        out_shape=[jax.ShapeDtypeStruct((t, c) if into is None else into.shape, F32), jax.ShapeDtypeStruct((kw, c), F32),
                   jax.ShapeDtypeStruct((1, c), F32)],
        scratch_shapes=[pad, pad] + ([pltpu.VMEM((t, cw), F32)] if n_p > 1 else []),
        compiler_params=_cparams(("arbitrary",)),
        name=name,
        **extra,
    )(x, w, b.reshape(1, c), *parts, *([] if into is None else [into]))


def _conv_glu_fwd(hid, w, b, *, side=None, name):
    t, c2 = hid.shape
    f = c2 // 2
    kw = w.shape[0]
    cw = _tile(f, (256, 128))
    nf = f // cw
    rc = _tile(t, (ROW_CHUNK,))

    def body(g_ref, v_ref, wg_ref, wv_ref, bg_ref, bv_ref, o_ref, gpad_ref, vpad_ref):
        _fill_pad(gpad_ref, g_ref, t)
        _fill_pad(vpad_ref, v_ref, t)
        for r0 in range(0, t, rc):
            gate = _conv_taps(gpad_ref, wg_ref, r0, rc, kw) + bg_ref[...]
            val = _conv_taps(vpad_ref, wv_ref, r0, rc, kw) + bv_ref[...]
            o_ref[pl.ds(r0, rc), :] = (gate * _sigmoid(gate) * val).astype(o_ref.dtype)

    gs = pl.BlockSpec((t, cw), lambda i: (0, i))
    vs = pl.BlockSpec((t, cw), lambda i: (0, i + nf))
    b2 = b.reshape(1, c2)
    (act,), side_outs = _call(
        body,
        grid=(nf,),
        in_specs=[gs, vs, pl.BlockSpec((kw, cw), lambda i: (0, i)), pl.BlockSpec((kw, cw), lambda i: (0, i + nf)),
                  pl.BlockSpec((1, cw), lambda i: (0, i)), pl.BlockSpec((1, cw), lambda i: (0, i + nf))],
        out_specs=[gs],
        out_shape=[jax.ShapeDtypeStruct((t, f), BF16)],
        scratch_shapes=[pltpu.VMEM((t + 2 * PAD, cw), F32), pltpu.VMEM((t + 2 * PAD, cw), F32)],
        sem=("parallel",),
        name=name,
        args=(hid, hid, w, w, b2, b2),
        side=side,
    )
    return act, side_outs


def _conv_glu_bwd(hid, w, b, dact, *, name):
    t, c2 = hid.shape
    f = c2 // 2
    kw = w.shape[0]
    cw = _tile(f, (128,))
    nf = f // cw
    rc = _tile(t, (ROW_CHUNK,))

    def body(g_ref, v_ref, wg_ref, wv_ref, bg_ref, bv_ref, da_ref,
             dgv_ref, dwg_ref, dwv_ref, dbg_ref, dbv_ref,
             gpad_ref, vpad_ref, dgpad_ref, dvpad_ref):
        _fill_pad(gpad_ref, g_ref, t)
        _fill_pad(vpad_ref, v_ref, t)
        for ref in (dgpad_ref, dvpad_ref):
            ref[0:PAD, :] = jnp.zeros((PAD, cw), F32)
            ref[pl.ds(PAD + t, PAD), :] = jnp.zeros((PAD, cw), F32)
        for r0 in range(0, t, rc):
            gate = _conv_taps(gpad_ref, wg_ref, r0, rc, kw) + bg_ref[...]
            val = _conv_taps(vpad_ref, wv_ref, r0, rc, kw) + bv_ref[...]
            sg = _sigmoid(gate)
            da = da_ref[pl.ds(r0, rc), :].astype(F32)
            dgpad_ref[pl.ds(PAD + r0, rc), :] = da * val * (sg * (1.0 + gate * (1.0 - sg)))
            dvpad_ref[pl.ds(PAD + r0, rc), :] = da * (gate * sg)
        _conv_bwd_core(dgpad_ref, gpad_ref, wg_ref, dgv_ref.at[0], dwg_ref, dbg_ref, t, rc, kw)
        _conv_bwd_core(dvpad_ref, vpad_ref, wv_ref, dgv_ref.at[1], dwv_ref, dbv_ref, t, rc, kw)

    gs = pl.BlockSpec((t, cw), lambda i: (0, i))
    vs = pl.BlockSpec((t, cw), lambda i: (0, i + nf))
    wg = pl.BlockSpec((kw, cw), lambda i: (0, i))
    wv = pl.BlockSpec((kw, cw), lambda i: (0, i + nf))
    bg = pl.BlockSpec((1, cw), lambda i: (0, i))
    bv = pl.BlockSpec((1, cw), lambda i: (0, i + nf))
    b2 = b.reshape(1, c2)
    pad = pltpu.VMEM((t + 2 * PAD, cw), F32)
    return pl.pallas_call(
        body,
        grid=(nf,),
        in_specs=[gs, vs, wg, wv, bg, bv, gs],
        out_specs=[pl.BlockSpec((2, t, cw), lambda i: (0, 0, i)), wg, wg, bg, bg],
        out_shape=[jax.ShapeDtypeStruct((2, t, f), F32),
                   jax.ShapeDtypeStruct((kw, f), F32), jax.ShapeDtypeStruct((kw, f), F32),
                   jax.ShapeDtypeStruct((1, f), F32), jax.ShapeDtypeStruct((1, f), F32)],
        scratch_shapes=[pad, pad, pad, pad],
        compiler_params=_cparams(("parallel",)),
        name=name,
    )(hid, hid, w, w, b2, b2, dact)


def _gate_norm_fwd(y, zx, w, *, name):
    t, di = y.shape
    gsz = di // SSM_GROUPS
    tb = _tile(t, (256, 128))

    def body(y_ref, z_ref, w_ref, o_ref):
        for g in range(SSM_GROUPS):
            sl = slice(g * gsz, (g + 1) * gsz)
            zv = z_ref[:, sl]
            gv = y_ref[:, sl] * (zv * _sigmoid(zv))
            r = lax.rsqrt(jnp.mean(gv * gv, axis=-1, keepdims=True) + EPS)
            o_ref[:, sl] = (gv * r * w_ref[:, sl]).astype(o_ref.dtype)

    row = pl.BlockSpec((tb, di), lambda i: (i, 0))
    return pl.pallas_call(
        body,
        grid=(t // tb,),
        in_specs=[row, row, pl.BlockSpec((1, di), lambda i: (0, 0))],
        out_specs=row,
        out_shape=jax.ShapeDtypeStruct((t, di), BF16),
        compiler_params=_cparams(("parallel",)),
        name=name,
    )(y, zx, w.reshape(1, di))


def _gate_norm_bwd(y, zx, w, dyn, *, side=None, name):
    t, di = y.shape
    gsz = di // SSM_GROUPS
    tb = _tile(t, (256, 128))

    def body(y_ref, z_ref, w_ref, d_ref, dy_ref, dz_ref, dw_ref):
        i = pl.program_id(0)
        for g in range(SSM_GROUPS):
            sl = slice(g * gsz, (g + 1) * gsz)
            zv = z_ref[:, sl]
            yv = y_ref[:, sl]
            sg = _sigmoid(zv)
            sz = zv * sg
            gv = yv * sz
            r = lax.rsqrt(jnp.mean(gv * gv, axis=-1, keepdims=True) + EPS)
            gn = gv * r
            dn = d_ref[:, sl].astype(F32)
            q = dn * w_ref[:, sl]
            dg = r * (q - gn * jnp.mean(q * gn, axis=-1, keepdims=True))
            dy_ref[:, sl] = dg * sz
            dz_ref[:, sl] = dg * yv * (sg * (1.0 + zv * (1.0 - sg)))
            dwp = jnp.sum(dn * gn, axis=0, keepdims=True)

            @pl.when(i == 0)
            def _(sl=sl, dwp=dwp):
                dw_ref[:, sl] = dwp

            @pl.when(i > 0)
            def _(sl=sl, dwp=dwp):
                dw_ref[:, sl] += dwp

    row = pl.BlockSpec((tb, di), lambda i: (i, 0))
    vec = pl.BlockSpec((1, di), lambda i: (0, 0))
    outs, side_outs = _call(
        body,
        grid=(t // tb,),
        in_specs=[row, row, vec, row],
        out_specs=[row, row, vec],
        out_shape=[jax.ShapeDtypeStruct((t, di), F32), jax.ShapeDtypeStruct((t, zx.shape[1]), F32),
                   jax.ShapeDtypeStruct((1, di), F32)],
        sem=("arbitrary",),
        name=name,
        args=(y, zx, w.reshape(1, di), dyn),
        side=side,
    )
    return (*outs, side_outs)


def _adamw(w, g, m, v, *, name):
    shape = w.shape
    cols = shape[-1]
    rows = w.size // cols
    w2, g2, m2, v2 = (a.reshape(rows, cols) for a in (w, g, m, v))
    tr = rows if rows * cols * 4 <= ADAM_BLOCK_BYTES else _row_tile(rows, cols)
    c1 = 1.0 - ADAM_B1 ** ADAM_STEP
    c2 = 1.0 - ADAM_B2 ** ADAM_STEP

    def body(w_ref, g_ref, m_ref, v_ref, d_ref, nm_ref, nv_ref):
        gv = g_ref[...]
        nm = ADAM_B1 * m_ref[...] + (1.0 - ADAM_B1) * gv
        nv = ADAM_B2 * v_ref[...] + (1.0 - ADAM_B2) * (gv * gv)
        d_ref[...] = -ADAM_LR * ((nm / c1) / (jnp.sqrt(nv / c2) + ADAM_EPS) + ADAM_WD * w_ref[...])
        nm_ref[...] = nm
        nv_ref[...] = nv

    blk = pl.BlockSpec((tr, cols), lambda i: (i, 0))
    outs = pl.pallas_call(
        body,
        grid=(rows // tr,),
        in_specs=[blk] * 4,
        out_specs=[blk] * 3,
        out_shape=[jax.ShapeDtypeStruct((rows, cols), F32)] * 3,
        compiler_params=_cparams(("parallel",)),
        name=name,
    )(w2, g2, m2, v2)
    return tuple(o.reshape(shape) for o in outs)


def _adamw_layers(w, gs, m, v, *, name):
    n_l, rows, cols = w.shape
    assert len(gs) == n_l
    tr = _row_tile(rows, cols)
    c1 = 1.0 - ADAM_B1 ** ADAM_STEP
    c2 = 1.0 - ADAM_B2 ** ADAM_STEP

    def body(*refs):
        w_ref, m_ref, v_ref = refs[:3]
        g_refs = refs[3:3 + n_l]
        g_ref, d_ref, nm_ref, nv_ref = refs[3 + n_l:]
        layer = pl.program_id(0)
        gv = g_refs[0][...]
        for q in range(1, n_l):
            gv = jnp.where(layer == q, g_refs[q][...], gv)
        nm = ADAM_B1 * m_ref[...] + (1.0 - ADAM_B1) * gv
        nv = ADAM_B2 * v_ref[...] + (1.0 - ADAM_B2) * (gv * gv)
        g_ref[...] = gv
        d_ref[...] = -ADAM_LR * ((nm / c1) / (jnp.sqrt(nv / c2) + ADAM_EPS) + ADAM_WD * w_ref[...])
        nm_ref[...] = nm
        nv_ref[...] = nv

    stacked = pl.BlockSpec((None, tr, cols), lambda l, i: (l, i, 0))
    single = pl.BlockSpec((tr, cols), lambda l, i: (i, 0))
    return pl.pallas_call(
        body,
        grid=(n_l, rows // tr),
        in_specs=[stacked] * 3 + [single] * n_l,
        out_specs=[stacked] * 4,
        out_shape=[jax.ShapeDtypeStruct(w.shape, F32)] * 4,
        compiler_params=_cparams(("parallel", "parallel")),
        name=name,
    )(w, m, v, *gs)


def _ssd_scalars(dtc_ref, dtr_ref, hpc_ref, hpr_ref, ln):
    assert SSM_CHUNK == SSM_STATE == LANES, "the SSD kernels mix chunk, state and lane-wide tiles freely"
    bias_c, alog_c = hpc_ref[0, 0:1, :], hpc_ref[0, 1:2, :]
    bias_r, alog_r = hpr_ref[0, :, 0:1], hpr_ref[0, :, 1:2]
    a_c, a_r = -jnp.exp(alog_c), -jnp.exp(alog_r)
    raw_c = dtc_ref[0] + bias_c
    dt_c = _softplus(raw_c)
    dt_r = _softplus(dtr_ref[0] + bias_r)
    row = lax.broadcasted_iota(jnp.int32, (ln, ln), 0)
    col = lax.broadcasted_iota(jnp.int32, (ln, ln), 1)
    lower = (col <= row).astype(F32)
    upper = (row <= col).astype(F32)
    acs_c = _ones_dot(lower, dt_c * a_c, ones_left=True)
    acs_r = _ones_dot(upper, dt_r * a_r, ones_left=False)
    return raw_c, dt_c, a_c, acs_c, acs_r, row, col


def _ssd_specs(t, di, g_n, n_st, rp, ln, r_h, rev):
    nc = t // ln
    cidx = (lambda c: nc - 1 - c) if rev else (lambda c: c)
    xs = pl.BlockSpec((ln, rp), lambda g, c: (cidx(c), g))
    bm = pl.BlockSpec((ln, n_st), lambda g, c: (cidx(c), di // n_st + g))
    cm = pl.BlockSpec((ln, n_st), lambda g, c: (cidx(c), di // n_st + g_n + g))
    dtc = pl.BlockSpec((1, ln, r_h), lambda g, c: (g, cidx(c), 0))
    dtr = pl.BlockSpec((1, r_h, ln), lambda g, c: (g, 0, cidx(c)))
    hpc = pl.BlockSpec((1, 3, r_h), lambda g, c: (g, 0, 0))
    hpr = pl.BlockSpec((1, r_h, 3), lambda g, c: (g, 0, 0))
    prev = pl.BlockSpec((1, rp, n_st), lambda g, c: (cidx(c), g, 0))
    return xs, bm, cm, dtc, dtr, hpc, hpr, prev


def _ssd_fwd(xbc, dtc, dtr, hpc, hpr, *, side=None, name):
    t = xbc.shape[0]
    di, g_n, n_st, p_h, ln = D_INNER, SSM_GROUPS, SSM_STATE, SSM_HEAD_DIM, SSM_CHUNK
    r_h = SSM_HEADS // g_n
    rp = r_h * p_h
    nc = t // ln

    def body(xs_ref, b_ref, c_ref, dtc_ref, dtr_ref, hpc_ref, hpr_ref, y_ref, prev_ref, st_ref):
        @pl.when(pl.program_id(1) == 0)
        def _():
            st_ref[...] = jnp.zeros_like(st_ref)

        _, dt_c, _, acs_c, acs_r, row, col = _ssd_scalars(dtc_ref, dtr_ref, hpc_ref, hpr_ref, ln)
        bm = b_ref[...]
        cm = c_ref[...]
        cm16 = cm.astype(BF16)
        cb = _nt(cm16, bm.astype(BF16))
        causal = row >= col
        for r in range(r_h):
            sl = slice(r * p_h, (r + 1) * p_h)
            xs = xs_ref[:, sl]
            acs = jnp.broadcast_to(acs_c[:, r:r + 1], (ln, ln))
            last = acs[ln - 1:ln, :]
            lm = jnp.where(causal, jnp.exp(acs - acs_r[r:r + 1, :]), 0.0)
            xd = (xs * jnp.broadcast_to(dt_c[:, r:r + 1], (ln, p_h))).astype(BF16)
            prev = st_ref[sl, :]
            y = _nn((cb * lm).astype(BF16), xd)
            y = y + _nt(cm16, prev.astype(BF16)) * jnp.exp(acs[:, :p_h])
            y_ref[:, sl] = y + hpc_ref[0, 2:3, r:r + 1] * xs
            prev_ref[0, sl, :] = prev
            bd = (bm * jnp.exp(last - acs[:, :n_st])).astype(BF16)
            st_ref[sl, :] = prev * jnp.exp(last[:, :n_st]) + _tn(xd, bd)

    xs, bm, cm, dtcs, dtrs, hpcs, hprs, prev = _ssd_specs(t, di, g_n, n_st, rp, ln, r_h, False)
    (y, prev_out), side_outs = _call(
        body,
        grid=(g_n, nc),
        in_specs=[xs, bm, cm, dtcs, dtrs, hpcs, hprs],
        out_specs=[xs, prev],
        out_shape=[jax.ShapeDtypeStruct((t, di), F32), jax.ShapeDtypeStruct((nc, g_n * rp, n_st), F32)],
        scratch_shapes=[pltpu.VMEM((rp, n_st), F32)],
        sem=("parallel", "arbitrary"),
        name=name,
        args=(xbc, xbc, xbc, dtc, dtr, hpc, hpr),
        side=side,
    )
    return y, prev_out, side_outs


def _ssd_bwd(xbc, dtc, dtr, hpc, hpr, prev, dy, *, side=None, name):
    t = xbc.shape[0]
    di, g_n, n_st, p_h, ln = D_INNER, SSM_GROUPS, SSM_STATE, SSM_HEAD_DIM, SSM_CHUNK
    r_h = SSM_HEADS // g_n
    rp = r_h * p_h
    nc = t // ln

    def body(xs_ref, b_ref, c_ref, dtc_ref, dtr_ref, hpc_ref, hpr_ref, prev_ref, dy_ref,
             dxs_ref, db_ref, dc_ref, ddt_ref, hg_ref, ds_ref):
        step = pl.program_id(1)

        @pl.when(step == 0)
        def _():
            ds_ref[...] = jnp.zeros_like(ds_ref)

        raw_c, dt_c, a_c, acs_c, acs_r, row, col = _ssd_scalars(dtc_ref, dtr_ref, hpc_ref, hpr_ref, ln)
        bm = b_ref[...]
        cm = c_ref[...]
        bm16, cm16 = bm.astype(BF16), cm.astype(BF16)
        cb = _nt(cm16, bm16)
        cbt = _nt(bm16, cm16)
        lane_r = lax.broadcasted_iota(jnp.int32, (ln, r_h), 1)
        dacs_all = jnp.zeros((ln, r_h), F32)
        ddtx_all = jnp.zeros((ln, r_h), F32)
        dd_all = jnp.zeros((ln, r_h), F32)
        dcb = jnp.zeros((ln, ln), F32)
        dcbt = jnp.zeros((ln, ln), F32)
        dc_acc = jnp.zeros((ln, n_st), F32)
        db_acc = jnp.zeros((ln, n_st), F32)
        for r in range(r_h):
            sl = slice(r * p_h, (r + 1) * p_h)
            xs = xs_ref[:, sl]
            dyv = dy_ref[:, sl]
            dy16 = dyv.astype(BF16)
            acs = jnp.broadcast_to(acs_c[:, r:r + 1], (ln, ln))
            dtv = jnp.broadcast_to(dt_c[:, r:r + 1], (ln, p_h))
            acsr = acs_r[r:r + 1, :]
            last = acs[ln - 1:ln, :]
            xd = xs * dtv
            xd16 = xd.astype(BF16)
            lm = jnp.where(row >= col, jnp.exp(acs - acsr), 0.0)
            lmt = jnp.where(col >= row, jnp.exp(acsr - acs), 0.0)
            m_ls = cb * lm
            m_sl = cbt * lmt
            dm = _nt(dy16, xd16)
            dmt = _nt(xd16, dy16)
            dxd = _nn(m_sl.astype(BF16), dy16)
            dacs = _row_sums(dm * m_ls - dmt * m_sl)
            dcb = dcb + dm * lm
            dcbt = dcbt + dmt * lmt
            prev = prev_ref[0, sl, :]
            prev16 = prev.astype(BF16)
            e = jnp.exp(acs[:, :p_h])
            y_off = _nt(cm16, prev16) * e
            dacs = dacs + _row_sums(dyv * y_off)
            dyo16 = (dyv * e).astype(BF16)
            dc_acc = dc_acc + _nn(dyo16, prev16)
            dprev = _tn(dyo16, cm16)
            ds = ds_ref[sl, :]
            ds16 = ds.astype(BF16)
            decay = jnp.exp(last - acs)[:, :n_st]
            bd16 = (bm * decay).astype(BF16)
            dbd = _nn(xd16, ds16)
            dxd = dxd + _nt(bd16, ds16)
            db_acc = db_acc + dbd * decay
            tdec = _row_sums(dbd * bm) * decay
            cd = jnp.exp(last)
            dlast = (jnp.sum(tdec, axis=0, keepdims=True)
                     + jnp.sum(_row_sums(prev * ds), axis=0, keepdims=True) * cd)
            ds_ref[sl, :] = dprev + cd[:, :n_st] * ds
            dskip = hpc_ref[0, 2:3, r:r + 1]
            dxs_ref[:, sl] = dxd * dtv + dskip * dyv
            dacs = dacs - tdec + jnp.where(row == ln - 1, dlast, 0.0)
            dacs_all = jnp.where(lane_r == r, dacs[:, :r_h], dacs_all)
            ddtx_all = jnp.where(lane_r == r, _row_sums(dxd * xs)[:, :r_h], ddtx_all)
            dd_all = jnp.where(lane_r == r, _row_sums(dyv * xs)[:, :r_h], dd_all)
        dc_ref[...] = dc_acc + _nn(dcb.astype(BF16), bm16)
        db_ref[...] = db_acc + _nn(dcbt.astype(BF16), cm16)
        upper = (row <= col).astype(F32)
        dad = _ones_dot(upper, dacs_all, ones_left=True)
        ddt = dad * a_c + ddtx_all
        ddt_raw = ddt * _sigmoid(raw_c)
        ddt_ref[0] = ddt_raw
        d_bias = jnp.sum(ddt_raw, axis=0, keepdims=True)
        d_alog = jnp.sum(dad * dt_c, axis=0, keepdims=True) * a_c
        d_d = jnp.sum(dd_all, axis=0, keepdims=True)
        hg = jnp.concatenate([d_bias, d_alog, d_d], axis=0)

        @pl.when(step == 0)
        def _():
            hg_ref[0] = hg

        @pl.when(step > 0)
        def _():
            hg_ref[0] += hg

    xs, bms, cms, dtcs, dtrs, hpcs, hprs, prevs = _ssd_specs(t, di, g_n, n_st, rp, ln, r_h, True)
    bout = pl.BlockSpec((ln, n_st), lambda g, c: (nc - 1 - c, g))
    outs, side_outs = _call(
        body,
        grid=(g_n, nc),
        in_specs=[xs, bms, cms, dtcs, dtrs, hpcs, hprs, prevs, xs],
        out_specs=[xs, bout, bout, dtcs, hpcs],
        out_shape=[jax.ShapeDtypeStruct((t, di), F32), jax.ShapeDtypeStruct((t, g_n * n_st), F32),
                   jax.ShapeDtypeStruct((t, g_n * n_st), F32), jax.ShapeDtypeStruct((g_n, t, r_h), F32),
                   jax.ShapeDtypeStruct((g_n, 3, r_h), F32)],
        scratch_shapes=[pltpu.VMEM((rp, n_st), F32)],
        sem=("parallel", "arbitrary"),
        name=name,
        args=(xbc, xbc, xbc, dtc, dtr, hpc, hpr, prev, dy),
        side=side,
    )
    return (*outs, side_outs)


SB_KEYS = 256
SB_QUERIES = (512, 256)
SB_CUTOFF = 110.0
SB_PIECES = 2


def _sb_logits(qs, kv, valid):
    z = _nt(qs, kv)
    nz = -z
    lg = jnp.minimum(nz, 0.0) - jnp.log(1.0 + jnp.exp(jnp.minimum(z, nz)))
    return z + lg, (lg if valid is None else jnp.where(valid, lg, 0.0))


def _sb_iota(tq):
    diff = lax.broadcasted_iota(jnp.int32, (tq, SB_KEYS), 1) - lax.broadcasted_iota(jnp.int32, (tq, SB_KEYS), 0)
    krow = lax.broadcasted_iota(jnp.int32, (SB_KEYS, SB_KEYS), 0)
    kcol = lax.broadcasted_iota(jnp.int32, (SB_KEYS, SB_KEYS), 1)
    return diff, krow, kcol


def _sb_scale(d):
    scale = 1.0 / math.sqrt(d)
    assert math.frexp(scale)[0] == 0.5, "the scale is folded into bf16 queries: it must be a power of two"
    return scale


def _key_rows(j):
    return pl.ds(pl.multiple_of(j * SB_KEYS, SB_KEYS), SB_KEYS)


def _sb_fwd(q, k, v, n_heads, *, side=None, name):
    t, hd = q.shape
    d = hd // n_heads
    hpt = LANES // d
    assert hpt * d == LANES and n_heads % hpt == 0
    tq = _tile(t, SB_QUERIES)
    nq = t // tq
    kpq = tq // SB_KEYS
    scale = _sb_scale(d)

    def body(q_ref, k_ref, v_ref, o_ref, lt_ref, first_ref):
        i = pl.program_id(1)
        diff, krow, kcol = _sb_iota(tq)
        later = (krow > kcol).astype(F32)
        nb = i * kpq
        for hh in range(hpt):
            sl = slice(hh * d, (hh + 1) * d)
            qs = (q_ref[:, sl].astype(F32) * scale).astype(BF16)

            def block(j, carry, valid, qs=qs, sl=sl):
                acc, cl = carry
                rows = _key_rows(j)
                ls, lg = _sb_logits(qs, k_ref[rows, sl], valid)
                cs = _ones_dot(later, lg, ones_left=False, pieces=SB_PIECES)
                att = jnp.exp(ls + (cs + cl))
                if valid is not None:
                    att = jnp.where(valid, att, 0.0)
                acc = acc + _nn(att.astype(BF16), v_ref[rows, sl])
                return acc, cl + (cs[:, 0:1] + lg[:, 0:1])

            carry = (jnp.zeros((tq, d), F32), jnp.zeros((tq, 1), F32))
            for m in range(kpq - 1, -1, -1):
                carry = block(i * kpq + m, carry, diff < -m * SB_KEYS)

            def more(st):
                s, _, cl = st
                return jnp.logical_and(s < nb, jnp.max(cl) > -SB_CUTOFF)

            def step(st, block=block):
                s, acc, cl = st
                acc, cl = block(nb - 1 - s, (acc, cl), None)
                return s + 1, acc, cl

            walked, acc, cl = lax.while_loop(more, step, (jnp.int32(0),) + carry)
            o_ref[:, sl] = acc.astype(o_ref.dtype)
            lt_ref[hh] = cl
            first_ref[pl.program_id(0) * hpt + hh, i] = nb - walked

    qs = pl.BlockSpec((tq, LANES), lambda p, i: (i, p))
    ls = pl.BlockSpec((hpt, tq, 1), lambda p, i: (p, i, 0))
    ks = pl.BlockSpec((t, LANES), lambda p, i: (0, p))
    outs, side_outs = _call(
        body,
        grid=(n_heads // hpt, nq),
        in_specs=[qs, ks, ks],
        out_specs=[qs, ls, pl.BlockSpec(memory_space=pltpu.SMEM)],
        out_shape=[jax.ShapeDtypeStruct((t, hd), BF16), jax.ShapeDtypeStruct((n_heads, t, 1), F32),
                   jax.ShapeDtypeStruct((n_heads, nq), jnp.int32)],
        sem=("arbitrary", "arbitrary"),
        name=name,
        args=(q, k, v),
        side=side,
    )
    return (*outs, side_outs)


def _sb_bwd(q, k, v, lt, first, do, n_heads, *, name):
    t, hd = q.shape
    d = hd // n_heads
    hpt = LANES // d
    tq = _tile(t, SB_QUERIES)
    nq = t // tq
    kpq = tq // SB_KEYS
    scale = _sb_scale(d)
    last = SB_KEYS - 1

    def body(q_ref, k_ref, v_ref, lt_ref, first_ref, do_ref, dq_ref, dk_ref, dv_ref, dk_acc, dv_acc):
        i = pl.program_id(1)

        @pl.when(i == 0)
        def _():
            dk_acc[...] = jnp.zeros_like(dk_acc)
            dv_acc[...] = jnp.zeros_like(dv_acc)

        diff, krow, kcol = _sb_iota(tq)
        upto = (krow <= kcol).astype(F32)
        before = (krow < kcol).astype(F32)
        zero = jnp.zeros((tq, 1), F32)
        nb = i * kpq
        for hh in range(hpt):
            sl = slice(hh * d, (hh + 1) * d)
            qs = (q_ref[:, sl].astype(F32) * scale).astype(BF16)
            do16 = do_ref[:, sl].astype(BF16)
            ltot = lt_ref[hh]

            def block(j, carry, valid, r0=0, qs=qs, do16=do16, ltot=ltot, sl=sl):
                dq, pl_sum, pg_sum = carry
                rows = _key_rows(j)
                kv = k_ref[rows, sl]
                vv = v_ref[rows, sl]
                ls, lg = _sb_logits(qs[r0:], kv, valid)
                pre = _ones_dot(upto, lg, ones_left=False, pieces=SB_PIECES)
                att = jnp.exp(ls + (ltot[r0:] - (pre + pl_sum)))
                if valid is not None:
                    att = jnp.where(valid, att, 0.0)
                g = att * _nt(do16[r0:], vv)
                gpre = _ones_dot(before, g, ones_left=False, pieces=SB_PIECES)
                sig = jnp.exp(ls)
                dz16 = (g - sig * (g + (gpre + pg_sum))).astype(BF16)
                if valid is not None:
                    dz16 = jnp.where(valid, dz16, jnp.zeros_like(dz16))
                dq = dq + _nn(dz16, kv)
                dk_acc[rows, sl] += _tn(dz16, qs[r0:])
                dv_acc[rows, sl] += _tn(att.astype(BF16), do16[r0:])
                return dq, pl_sum + pre[:, last:], pg_sum + (gpre[:, last:] + g[:, last:])

            start = jnp.clip(first_ref[pl.program_id(0) * hpt + hh, i], 0, nb)
            carry = lax.fori_loop(start, nb, lambda j, cr, block=block: block(j, cr, None),
                                  (jnp.zeros((tq, d), F32), zero, zero))
            for m in range(kpq):
                r0 = m * SB_KEYS
                sub = block(nb + m, tuple(a[r0:] for a in carry), diff[r0:] < -r0, r0)
                carry = tuple(jnp.concatenate([a[:r0], s], axis=0) if r0 else s for a, s in zip(carry, sub))
            dq_ref[:, sl] = (carry[0] * scale).astype(dq_ref.dtype)

        @pl.when(i == nq - 1)
        def _():
            dk_ref[...] = dk_acc[...].astype(dk_ref.dtype)
            dv_ref[...] = dv_acc[...].astype(dv_ref.dtype)

    qs = pl.BlockSpec((tq, LANES), lambda p, i: (i, p))
    ls = pl.BlockSpec((hpt, tq, 1), lambda p, i: (p, i, 0))
    ks = pl.BlockSpec((t, LANES), lambda p, i: (0, p))
    full = jax.ShapeDtypeStruct((t, hd), BF16)
    return pl.pallas_call(
        body,
        grid=(n_heads // hpt, nq),
        in_specs=[qs, ks, ks, ls, pl.BlockSpec(memory_space=pltpu.SMEM), qs],
        out_specs=[qs, ks, ks],
        out_shape=[full, full, full],
        scratch_shapes=[pltpu.VMEM((t, LANES), F32), pltpu.VMEM((t, LANES), F32)],
        compiler_params=_cparams(("arbitrary", "arbitrary")),
        name=name,
    )(q, k, v, lt, first, do)


def _row_tile(rows, cols):
    fits = [r for r in range(16, rows + 1, 16) if rows % r == 0 and r * cols * 4 <= ADAM_BLOCK_BYTES]
    return max(fits) if fits else rows


def _sum_leading(x, *, name):
    n, rows, cols = x.shape
    tr = _row_tile(rows, cols)

    def body(x_ref, o_ref):
        acc = x_ref[0].astype(F32)
        for q in range(1, n):
            acc = acc + x_ref[q].astype(F32)
        o_ref[...] = acc

    return pl.pallas_call(
        body,
        grid=(rows // tr,),
        in_specs=[pl.BlockSpec((n, tr, cols), lambda i: (0, i, 0))],
        out_specs=pl.BlockSpec((tr, cols), lambda i: (i, 0)),
        out_shape=jax.ShapeDtypeStruct((rows, cols), F32),
        compiler_params=_cparams(("parallel",)),
        name=name,
    )(x)


def _pair_add(g4h, recv, c, *, out_dtype, name):
    n, _, rows, cols = g4h.shape
    tr = _row_tile(rows, cols)

    def body(c_ref, g_ref, r_ref, o_ref):
        o_ref[...] = (g_ref[...] + r_ref[...]).astype(o_ref.dtype)

    blk = pl.BlockSpec((1, tr, cols), lambda q, i, c_ref: (q, i, 0))
    return pl.pallas_call(
        body,
        grid_spec=pltpu.PrefetchScalarGridSpec(
            num_scalar_prefetch=1,
            grid=(n, rows // tr),
            in_specs=[pl.BlockSpec((1, None, tr, cols), lambda q, i, c_ref: (q, c_ref[0], i, 0)), blk],
            out_specs=blk),
        out_shape=jax.ShapeDtypeStruct((n, rows, cols), out_dtype),
        compiler_params=_cparams(("parallel", "parallel")),
        name=name,
    )(c.reshape(1).astype(jnp.int32), g4h, recv)


ANY = pl.BlockSpec(memory_space=pl.ANY)


def _other_chips(x, y):
    return [(1 - x, y), (x, 1 - y), (1 - x, 1 - y)]


def _gather_chips(shard, *, name):
    def body(x_ref, o_ref, send_sems, recv_sems, local_sem):
        x, y, c = lax.axis_index("x"), lax.axis_index("y"), lax.axis_index("c")
        me = 2 * x + y
        mine = pltpu.make_async_copy(x_ref, o_ref.at[me], local_sem)
        mine.start()
        chips = _other_chips(x, y)
        sends = [pltpu.make_async_remote_copy(src_ref=x_ref, dst_ref=o_ref.at[me], send_sem=send_sems.at[q],
                                              recv_sem=recv_sems.at[q], device_id=(px, py, c), device_id_type=MESH)
                 for q, (px, py) in enumerate(chips)]
        for cp in sends:
            cp.start()
        for q, (px, py) in enumerate(chips):
            pltpu.make_async_remote_copy(src_ref=x_ref, dst_ref=o_ref.at[2 * px + py], send_sem=send_sems.at[q],
                                         recv_sem=recv_sems.at[q], device_id=(px, py, c), device_id_type=MESH).wait_recv()
        for cp in sends:
            cp.wait_send()
        mine.wait()

    return pl.pallas_call(
        body,
        in_specs=[ANY],
        out_specs=ANY,
        out_shape=jax.ShapeDtypeStruct((4,) + shard.shape, shard.dtype),
        scratch_shapes=[pltpu.SemaphoreType.DMA((3,)), pltpu.SemaphoreType.DMA((3,)), pltpu.SemaphoreType.DMA],
        compiler_params=pltpu.CompilerParams(has_side_effects=True),
        name=name,
    )(shard)


def _comm_call(body, ins, out_shapes, n_sems, name):
    n = len(ins)

    def wrapped(*refs):
        body(refs[:n], refs[n:n + len(out_shapes)], refs[-2], refs[-1])

    return pl.pallas_call(
        wrapped,
        in_specs=[ANY] * n,
        out_specs=[ANY] * len(out_shapes),
        out_shape=out_shapes,
        scratch_shapes=[pltpu.SemaphoreType.DMA((n_sems,)), pltpu.SemaphoreType.DMA((n_sems,))],
        compiler_params=pltpu.CompilerParams(has_side_effects=True),
        name=name,
    )(*ins)


def _remote(send_sems, recv_sems, q, src, dst, to):
    return pltpu.make_async_remote_copy(src_ref=src, dst_ref=dst, send_sem=send_sems.at[q], recv_sem=recv_sems.at[q],
                                        device_id=to, device_id_type=MESH)


def _scatter_chips(parts, *, name):
    return _run_job(_scatter_job(parts), name)


def _scatter_job(parts):
    def sends(ins, outs, send_sems, recv_sems):
        x, y, c = lax.axis_index("x"), lax.axis_index("y"), lax.axis_index("c")
        return [_remote(send_sems, recv_sems, 3 * i + q, p.at[2 * px + py], o.at[2 * x + y], (px, py, c))
                for i, (p, o) in enumerate(zip(ins, outs)) for q, (px, py) in enumerate(_other_chips(x, y))]

    def start(ins, outs, send_sems, recv_sems):
        for cp in sends(ins, outs, send_sems, recv_sems):
            cp.start()

    def finish(ins, outs, send_sems, recv_sems):
        x, y, c = lax.axis_index("x"), lax.axis_index("y"), lax.axis_index("c")
        for i, (p, o) in enumerate(zip(ins, outs)):
            for q, (px, py) in enumerate(_other_chips(x, y)):
                _remote(send_sems, recv_sems, 3 * i + q, p.at[2 * x + y], o.at[2 * px + py], (px, py, c)).wait_recv()
        for cp in sends(ins, outs, send_sems, recv_sems):
            cp.wait_send()

    return _SideJob(parts, [jax.ShapeDtypeStruct(p.shape, p.dtype) for p in parts], 3 * len(parts), start, finish)


def _run_job(job, name):
    return _comm_call(lambda *refs: (job.start(*refs), job.finish(*refs)), job.ins, job.out_shapes, job.n_sems, name)


def _gather_job(shards):
    def sends(ins, outs, send_sems, recv_sems):
        x, y, c = lax.axis_index("x"), lax.axis_index("y"), lax.axis_index("c")
        return [_remote(send_sems, recv_sems, 6 * i + q, s.at[c], o.at[2 * x + y, c], (px, py, c))
                for i, (s, o) in enumerate(zip(ins, outs)) for q, (px, py) in enumerate(_other_chips(x, y))]

    def start(ins, outs, send_sems, recv_sems):
        for cp in sends(ins, outs, send_sems, recv_sems):
            cp.start()

    def finish(ins, outs, send_sems, recv_sems):
        x, y, c = lax.axis_index("x"), lax.axis_index("y"), lax.axis_index("c")
        sibling = (x, y, 1 - c)
        chips = _other_chips(x, y)
        copy = lambda q, src, dst, to: _remote(send_sems, recv_sems, q, src, dst, to)
        passed = []
        for i, (s, o) in enumerate(zip(ins, outs)):
            for q, (px, py) in enumerate(chips):
                slot = o.at[2 * px + py, c]
                copy(6 * i + q, s.at[c], slot, (px, py, c)).wait_recv()
                passed.append(copy(6 * i + 3 + q, slot, slot, sibling))
                passed[-1].start()
        for i, (s, o) in enumerate(zip(ins, outs)):
            for q, (px, py) in enumerate(chips):
                copy(6 * i + 3 + q, s.at[1 - c], o.at[2 * px + py, 1 - c], sibling).wait_recv()
        for cp in sends(ins, outs, send_sems, recv_sems) + passed:
            cp.wait_send()

    return _SideJob(shards, [jax.ShapeDtypeStruct((N_CHIPS,) + s.shape, s.dtype) for s in shards], 6 * len(shards),
                    start, finish)


def _swap_job(gs):
    def copies(ins, outs, send_sems, recv_sems):
        x, y, c = lax.axis_index("x"), lax.axis_index("y"), lax.axis_index("c")
        return [_remote(send_sems, recv_sems, i, g.at[pl.ds(0, g.shape[0]), 1 - c], o, (x, y, 1 - c))
                for i, (g, o) in enumerate(zip(ins, outs))]

    def start(*refs):
        for cp in copies(*refs):
            cp.start()

    def finish(*refs):
        for cp in copies(*refs):
            cp.wait()

    return _SideJob(gs, [jax.ShapeDtypeStruct((g.shape[0],) + g.shape[2:], g.dtype) for g in gs], len(gs), start, finish)


def _join_halves(halves, *, name):
    def body(ins, outs, send_sems, recv_sems):
        x, y, c = lax.axis_index("x"), lax.axis_index("y"), lax.axis_index("c")
        sibling = (x, y, 1 - c)
        sends = [_remote(send_sems, recv_sems, i, h, o.at[c], sibling) for i, (h, o) in enumerate(zip(ins, outs))]
        for cp in sends:
            cp.start()
        for i, (h, o) in enumerate(zip(ins, outs)):
            _remote(send_sems, recv_sems, i, h, o.at[1 - c], sibling).wait_recv()
        for cp in sends:
            cp.wait_send()

    return _comm_call(body, halves, [jax.ShapeDtypeStruct((2,) + h.shape, h.dtype) for h in halves], len(halves), name)


def _gather_all(v, *, name):
    def body(v_ref, o_ref, send_sems, recv_sems, local_sem):
        x, y, c = lax.axis_index("x"), lax.axis_index("y"), lax.axis_index("c")
        me = 4 * x + 2 * y + c
        mine = pltpu.make_async_copy(v_ref, o_ref.at[me], local_sem)
        mine.start()
        peers = [(x ^ (q >> 2 & 1), y ^ (q >> 1 & 1), c ^ (q & 1)) for q in range(1, 8)]
        sends = [pltpu.make_async_remote_copy(src_ref=v_ref, dst_ref=o_ref.at[me], send_sem=send_sems.at[q],
                                              recv_sem=recv_sems.at[q], device_id=peer, device_id_type=MESH)
                 for q, peer in enumerate(peers)]
        for cp in sends:
            cp.start()
        for q, (px, py, pc) in enumerate(peers):
            pltpu.make_async_remote_copy(src_ref=v_ref, dst_ref=o_ref.at[4 * px + 2 * py + pc], send_sem=send_sems.at[q],
                                         recv_sem=recv_sems.at[q], device_id=(px, py, pc), device_id_type=MESH).wait_recv()
        for cp in sends:
            cp.wait_send()
        mine.wait()

    return pl.pallas_call(
        body,
        in_specs=[ANY],
        out_specs=ANY,
        out_shape=jax.ShapeDtypeStruct((8,) + v.shape, v.dtype),
        scratch_shapes=[pltpu.SemaphoreType.DMA((7,)), pltpu.SemaphoreType.DMA((7,)), pltpu.SemaphoreType.DMA],
        compiler_params=pltpu.CompilerParams(has_side_effects=True),
        name=name,
    )(v)


WEIGHTS = ['ssm_norm_w', 'ssm_in_w', 'ssm_conv_w', 'ssm_conv_b', 'ssm_dt_bias', 'ssm_a_log', 'ssm_d',
           'ssm_gate_norm_w', 'ssm_out_w', 'kv_norm_w', 'w_k', 'w_v', 'attn_norm_w', 'w_q', 'w_o',
           'ffn_norm_w', 'ffn_up_w', 'ffn_conv_w', 'ffn_conv_b', 'ffn_down_w', 'final_norm_w']
SHARD_AXIS = {'ssm_norm_w': 1, 'ssm_in_w': 2, 'ssm_conv_w': 2, 'ssm_conv_b': 1, 'ssm_gate_norm_w': 1,
              'ssm_out_w': 1, 'w_k': 0, 'w_v': 0, 'w_q': 1, 'w_o': 1, 'ffn_up_w': 2, 'ffn_conv_w': 2,
              'ffn_down_w': 1}
BIG = ['ssm_in_w', 'ssm_out_w', 'w_k', 'w_v', 'w_q', 'w_o', 'ffn_up_w', 'ffn_down_w']
SMALL = [n for n in WEIGHTS if n in SHARD_AXIS and n not in BIG]
REPLICATED = [n for n in WEIGHTS if n not in SHARD_AXIS]
STACKED = ['ffn_up_w', 'ffn_down_w']
N_CHIPS = 4


PACK_ROWS = 16


def _piece_rows(n):
    return -(-n // (PACK_ROWS * LANES)) * PACK_ROWS


def _pack(arrs, dtype, row_mult):
    lead = arrs[0].shape[:-1]
    pieces, total = [], 0
    for a in arrs:
        n = a.shape[-1]
        rows = _piece_rows(n)
        a = a.astype(dtype)
        if rows * LANES != n:
            a = jnp.pad(a, [(0, 0)] * len(lead) + [(0, rows * LANES - n)])
        pieces.append(a.reshape(lead + (rows, LANES)))
        total += rows
    extra = -total % row_mult
    if extra:
        pieces.append(jnp.zeros(lead + (extra, LANES), dtype))
    return jnp.concatenate(pieces, axis=len(lead))


def _unpack(buf, shapes):
    lead = buf.shape[:-2]
    out, off = [], 0
    for shp in shapes:
        n = math.prod(shp)
        rows = _piece_rows(n)
        piece = lax.slice_in_dim(buf, off, off + rows, axis=len(lead)).reshape(lead + (rows * LANES,))
        out.append(piece[..., :n].reshape(lead + tuple(shp)))
        off += rows
    return out


def _set_slot(buf, piece, index):
    return lax.dynamic_update_slice_in_dim(buf, piece[None], index, axis=0)


def _from_shards(stacked, axis):
    return jnp.concatenate([stacked[j] for j in range(N_CHIPS)], axis=axis)


def _ffn_fwd(h, norm_w, w_up, conv_w, conv_b, w_down, tag, side=None):
    u = _rmsnorm_fwd(h, norm_w, name=f"ffn{tag}_norm")
    hid = _matmul(u, w_up, name=f"ffn{tag}_up")
    act, side_outs = _conv_glu_fwd(hid, conv_w, conv_b, side=side, name=f"ffn{tag}_glu")
    out = _matmul(act, w_down, add=h, name=f"ffn{tag}_down")
    return out, (u, hid, act), side_outs


def _ffn_bwd(h, saved, dout, norm_w, w_up, conv_w, conv_b, w_down, tag):
    u, hid, act = saved
    dact = _matmul(dout, w_down, tb=True, name=f"ffn{tag}_down_dx")
    dw_down = _matmul(act, dout, ta=True, name=f"ffn{tag}_down_dw")
    dhid, dwg, dwv, dbg, dbv = _conv_glu_bwd(hid, conv_w, conv_b, dact, name=f"ffn{tag}_glu_bwd")
    du = _matmul(dhid, w_up, tb=True, name=f"ffn{tag}_up_dx")
    dw_up = _matmul(u, dhid, ta=True, out_parts=N_CHIPS, name=f"ffn{tag}_up_dw")
    dh, (dnorm,) = _rmsnorm_bwd(h, [(du, norm_w)], dout, name=f"ffn{tag}_norm_bwd")
    return dh, dict(norm=dnorm[0], up=dw_up, conv_w=jnp.concatenate([dwg, dwv], axis=1),
                    conv_b=jnp.concatenate([dbg, dbv], axis=1)[0], down=dw_down)


class _Pieces:
    def __init__(self, local):
        self.c = lax.axis_index("c")
        self.chip = 2 * lax.axis_index("x") + lax.axis_index("y")
        self.shape, self.s16 = {}, {}
        for n in BIG:
            blk = local[n]
            layers = [(n, l, blk[l]) for l in range(blk.shape[0])] if n in STACKED else [(n, None, blk.reshape(blk.shape[-2:]))]
            for name, l, p in layers:
                self.shape[name, l] = p.shape
                self.s16[name, l] = p.astype(BF16).reshape(2, p.shape[0] // 2, p.shape[1])

    def gather_job(self, keys):
        return _gather_job([self.s16[k] for k in keys])

    def weights(self, keys, gathered):
        out = []
        for k, g in zip(keys, gathered):
            r, cc = self.shape[k]
            by_chip = _set_slot(g, self.s16[k], self.chip).reshape(N_CHIPS, r, cc)
            if k[0] == 'ssm_in_w':
                by_chip = by_chip.transpose(1, 0, 2).reshape(r, N_CHIPS * cc)
            elif k[0] != 'ffn_up_w':
                by_chip = by_chip.reshape(N_CHIPS * r, cc)
            out.append(by_chip)
        return out

    def by_halves(self, keys, grads):
        gs = []
        for k, g in zip(keys, grads):
            r, cc = self.shape[k]
            if k[0] == 'ssm_in_w':
                g = g.reshape(r, N_CHIPS, cc).transpose(1, 0, 2)
            gs.append(g.reshape(N_CHIPS, 2, r // 2, cc))
        return gs

    def pair_sums(self, gs, recv, tag):
        return [_pair_add(g, rv, self.c, out_dtype=BF16, name=f"rs_pair_add_{tag}{i}") for i, (g, rv) in enumerate(zip(gs, recv))]

    def chip_sums(self, pairs, scattered, tag):
        return [_sum_leading(_set_slot(s, lax.dynamic_index_in_dim(p, self.chip, axis=0, keepdims=False), self.chip),
                             name=f"rs_chip_sum_{tag}{i}") for i, (s, p) in enumerate(zip(scattered, pairs))]

    def shards(self, keys, halves):
        joined = _join_halves(halves, name="rs_half_join")
        return {k: _set_slot(j, h, self.c).reshape(self.shape[k]) for k, h, j in zip(keys, halves, joined)}


def _step(x, target, w, pieces):
    t = x.shape[0]
    g_n, heads = SSM_GROUPS, SSM_HEADS
    r_h = heads // g_n
    di = D_INNER
    zx_cols = di + CONV_DIM
    k_in = [('ssm_in_w', None)]
    k_ffn0 = [('ssm_out_w', None), ('ffn_up_w', 0), ('ffn_down_w', 0)]
    k_qkv = [('w_k', None), ('w_v', None), ('w_q', None)]
    k_late = [('w_o', None), ('ffn_up_w', 1), ('ffn_down_w', 1)]
    (w_in,) = pieces.weights(k_in, _run_job(pieces.gather_job(k_in), "gather_ssm_in"))
    w_zx = w_in[:, :zx_cols]
    w_dt = jnp.pad(w_in[:, zx_cols:], ((0, 0), (0, LANES - heads)))
    conv_w, conv_b = w['ssm_conv_w'][0], w['ssm_conv_b'][0]
    hp = jnp.stack([w['ssm_dt_bias'][0], w['ssm_a_log'][0], w['ssm_d'][0]], axis=0).reshape(3, g_n, r_h)
    hpc, hpr = hp.transpose(1, 0, 2), hp.transpose(1, 2, 0)

    h0 = x
    u0 = _rmsnorm_fwd(h0, w['ssm_norm_w'][0], name="ssm_norm")
    zx = _matmul(u0, w_zx, name="ssm_in_zx")
    dt_raw = _matmul(u0, w_dt, name="ssm_in_dt")[:, :heads]
    dtg = dt_raw.reshape(t, g_n, r_h)
    dtc, dtr = dtg.transpose(1, 0, 2), dtg.transpose(1, 2, 0)
    xbc = _conv_silu_fwd(zx, conv_w, conv_b, x_off=di, name="ssm_conv")
    y, prev, got = _ssd_fwd(xbc, dtc, dtr, hpc, hpr, side=pieces.gather_job(k_ffn0), name="ssd_fwd")
    w_out, w_up0, w_down0 = pieces.weights(k_ffn0, got)
    yn = _gate_norm_fwd(y, zx, w['ssm_gate_norm_w'][0], name="ssm_gate_norm")
    h1 = _matmul(yn, w_out, add=h0, name="ssm_out")
    h2, ffn0, got = _ffn_fwd(h1, w['ffn_norm_w'][0], w_up0, w['ffn_conv_w'][0], w['ffn_conv_b'][0], w_down0, 0,
                             side=pieces.gather_job(k_qkv))
    w_k, w_v, w_q = pieces.weights(k_qkv, got)
    hk = _rmsnorm_fwd(h2, w['kv_norm_w'], name="kv_norm")
    qn = _rmsnorm_fwd(h2, w['attn_norm_w'][0], name="attn_norm")
    k2 = _matmul(hk, w_k, out_dtype=BF16, name="attn_k")
    v2 = _matmul(hk, w_v, out_dtype=BF16, name="attn_v")
    q2 = _matmul(qn, w_q, out_dtype=BF16, name="attn_q")
    o2, lt, first, got = _sb_fwd(q2, k2, v2, SB_HEADS, side=pieces.gather_job(k_late), name="sb_fwd")
    w_o, w_up1, w_down1 = pieces.weights(k_late, got)
    h3 =_matmul(o2, w_o, add=h2, name="attn_o")
    h4, ffn1, _ = _ffn_fwd(h3, w['ffn_norm_w'][1], w_up1, w['ffn_conv_w'][1], w['ffn_conv_b'][1], w_down1, 1)
    loss_p, dh4, d_final = _loss_head(h4, w['final_norm_w'], target, name="loss_head")

    dh3, g1 = _ffn_bwd(h3, ffn1, dh4, w['ffn_norm_w'][1], w_up1, w['ffn_conv_w'][1], w['ffn_conv_b'][1], w_down1, 1)
    do2 = _matmul(dh3, w_o, tb=True, out_dtype=BF16, name="attn_o_dx")
    dw_o = _matmul(o2, dh3, ta=True, name="attn_o_dw")
    dq2, dk2, dv2 = _sb_bwd(q2, k2, v2, lt, first, do2, SB_HEADS, name="sb_bwd")
    dqn = _matmul(dq2, w_q, tb=True, name="attn_q_dx")
    dw_q = _matmul(qn, dq2, ta=True, name="attn_q_dw")
    dhk = _matmul(dk2, w_k, tb=True, name="attn_k_dx")
    dhk = _matmul(dv2, w_v, tb=True, add=dhk, name="attn_v_dx")
    dw_k = _matmul(hk, dk2, ta=True, name="attn_k_dw")
    dw_v = _matmul(hk, dv2, ta=True, name="attn_v_dw")
    dh2, (d_attn_norm, d_kv_norm) = _rmsnorm_bwd(h2, [(dqn, w['attn_norm_w'][0]), (dhk, w['kv_norm_w'])], dh3,
                                                 name="attn_norms_bwd")
    dh1, g0 = _ffn_bwd(h1, ffn0, dh2, w['ffn_norm_w'][0], w_up0, w['ffn_conv_w'][0], w['ffn_conv_b'][0], w_down0, 0)
    dyn = _matmul(dh1, w_out, tb=True, name="ssm_out_dx")
    dw_out = _matmul(yn, dh1, ta=True, name="ssm_out_dw")
    k_done = k_qkv + k_late + k_ffn0
    gs_done = pieces.by_halves(k_done, [dw_k, dw_v, dw_q, dw_o, g1['up'], g1['down'], dw_out, g0['up'], g0['down']])
    dy, dz, d_gate, recv = _gate_norm_bwd(y, zx, w['ssm_gate_norm_w'][0], dyn, side=_swap_job(gs_done),
                                          name="ssm_gate_norm_bwd")
    pairs_done = pieces.pair_sums(gs_done, recv, "a")
    dxs, dbm, dcm, ddt_g, hg, scattered_done = _ssd_bwd(xbc, dtc, dtr, hpc, hpr, prev, dy,
                                                        side=_scatter_job(pairs_done), name="ssd_bwd")
    dzx, d_conv_w, d_conv_b = _conv_silu_bwd(zx, conv_w, conv_b, [dxs, dbm, dcm], x_off=di, into=dz, name="ssm_conv_bwd")
    ddt = jnp.pad(ddt_g.transpose(1, 0, 2).reshape(t, heads), ((0, 0), (0, LANES - heads)))
    du0 = _matmul(dzx, w_zx, tb=True, name="ssm_in_zx_dx")
    du0 = _matmul(ddt, w_dt, tb=True, add=du0, name="ssm_in_dt_dx")
    dw_in = jnp.concatenate([_matmul(u0, dzx, ta=True, name="ssm_in_zx_dw"),
                             _matmul(u0, ddt, ta=True, name="ssm_in_dt_dw")[:, :heads]], axis=1)
    dx, (d_ssm_norm,) = _rmsnorm_bwd(h0, [(du0, w['ssm_norm_w'][0])], dh1, name="ssm_norm_bwd")

    gs_in = pieces.by_halves(k_in, [dw_in])
    pairs_in = pieces.pair_sums(gs_in, _run_job(_swap_job(gs_in), "rs_pair_swap_b"), "b")
    halves = (pieces.chip_sums(pairs_done, scattered_done, "a")
              + pieces.chip_sums(pairs_in, _scatter_chips(pairs_in, name="rs_chip_scatter_b"), "b"))
    big_grads = pieces.shards(k_done + k_in, halves)

    hgr = hg.transpose(1, 0, 2).reshape(3, heads)
    grads = {
        'ssm_norm_w': d_ssm_norm, 'ssm_conv_w': d_conv_w[None], 'ssm_conv_b': d_conv_b,
        'ssm_dt_bias': hgr[0:1], 'ssm_a_log': hgr[1:2], 'ssm_d': hgr[2:3], 'ssm_gate_norm_w': d_gate,
        'kv_norm_w': d_kv_norm[0], 'attn_norm_w': d_attn_norm, 'ffn_norm_w': jnp.stack([g0['norm'], g1['norm']]),
        'ffn_conv_w': jnp.stack([g0['conv_w'], g1['conv_w']]), 'ffn_conv_b': jnp.stack([g0['conv_b'], g1['conv_b']]),
        'final_norm_w': d_final[0],
    }
    return loss_p, dx, grads, big_grads


def kernel(x, ssm_norm_w, ssm_in_w, ssm_conv_w, ssm_conv_b, ssm_dt_bias, ssm_a_log, ssm_d, ssm_gate_norm_w, ssm_out_w, kv_norm_w, w_k, w_v, attn_norm_w, w_q, w_o, ffn_norm_w, ffn_up_w, ffn_conv_w, ffn_conv_b, ffn_down_w, final_norm_w, loss_target, m_ssm_norm_w, m_ssm_in_w, m_ssm_conv_w, m_ssm_conv_b, m_ssm_dt_bias, m_ssm_a_log, m_ssm_d, m_ssm_gate_norm_w, m_ssm_out_w, m_kv_norm_w, m_w_k, m_w_v, m_attn_norm_w, m_w_q, m_w_o, m_ffn_norm_w, m_ffn_up_w, m_ffn_conv_w, m_ffn_conv_b, m_ffn_down_w, m_final_norm_w, v_ssm_norm_w, v_ssm_in_w, v_ssm_conv_w, v_ssm_conv_b, v_ssm_dt_bias, v_ssm_a_log, v_ssm_d, v_ssm_gate_norm_w, v_ssm_out_w, v_kv_norm_w, v_w_k, v_w_v, v_attn_norm_w, v_w_q, v_w_o, v_ffn_norm_w, v_ffn_up_w, v_ffn_conv_w, v_ffn_conv_b, v_ffn_down_w, v_final_norm_w):
    args = (ssm_norm_w, ssm_in_w, ssm_conv_w, ssm_conv_b, ssm_dt_bias, ssm_a_log, ssm_d, ssm_gate_norm_w, ssm_out_w, kv_norm_w, w_k, w_v, attn_norm_w, w_q, w_o, ffn_norm_w, ffn_up_w, ffn_conv_w, ffn_conv_b, ffn_down_w, final_norm_w)
    moms = (m_ssm_norm_w, m_ssm_in_w, m_ssm_conv_w, m_ssm_conv_b, m_ssm_dt_bias, m_ssm_a_log, m_ssm_d, m_ssm_gate_norm_w, m_ssm_out_w, m_kv_norm_w, m_w_k, m_w_v, m_attn_norm_w, m_w_q, m_w_o, m_ffn_norm_w, m_ffn_up_w, m_ffn_conv_w, m_ffn_conv_b, m_ffn_down_w, m_final_norm_w)
    vels = (v_ssm_norm_w, v_ssm_in_w, v_ssm_conv_w, v_ssm_conv_b, v_ssm_dt_bias, v_ssm_a_log, v_ssm_d, v_ssm_gate_norm_w, v_ssm_out_w, v_kv_norm_w, v_w_k, v_w_v, v_attn_norm_w, v_w_q, v_w_o, v_ffn_norm_w, v_ffn_up_w, v_ffn_conv_w, v_ffn_conv_b, v_ffn_down_w, v_final_norm_w)
    local = dict(zip(WEIGHTS, args))
    m_in = dict(zip(WEIGHTS, moms))
    v_in = dict(zip(WEIGHTS, vels))
    chip = 2 * lax.axis_index("x") + lax.axis_index("y")

    full = {n: local[n] for n in REPLICATED}
    small32 = _gather_chips(_pack([local[n].reshape(-1) for n in SMALL], F32, 8), name="gather_small")
    for n, st in zip(SMALL, _unpack(small32, [local[n].shape for n in SMALL])):
        full[n] = _from_shards(st, SHARD_AXIS[n])

    pieces = _Pieces(local)
    loss_p, dx, grads, big_grads = _step(x[0], loss_target[0], full, pieces)
    gshard = {}
    for n in BIG:
        if n in STACKED:
            gshard[n] = [big_grads[n, l] for l in range(local[n].shape[0])]
        else:
            gshard[n] = big_grads[n, None].reshape(local[n].shape)

    small = SMALL + REPLICATED
    rep = _pack([loss_p.reshape(-1)] + [grads[n].reshape(-1) for n in small], F32, 8)
    tot = _sum_leading(_gather_all(rep, name="ar_gather"), name="ar_sum")
    parts = _unpack(tot, [(LANES,)] + [grads[n].shape for n in small])
    loss = jnp.sum(parts[0])
    for n, g in zip(small, parts[1:]):
        if n in SHARD_AXIS:
            size = local[n].shape[SHARD_AXIS[n]]
            g = lax.dynamic_slice_in_dim(g, chip * size, size, axis=SHARD_AXIS[n])
        gshard[n] = g

    grads_out, deltas, new_m, new_v = [], [], [], []
    for n in WEIGHTS:
        if n in STACKED:
            g, d, nm, nv = _adamw_layers(local[n], gshard[n], m_in[n], v_in[n], name=f"adamw_{n}")
        else:
            g = gshard[n]
            d, nm, nv = _adamw(local[n], g, m_in[n], v_in[n], name=f"adamw_{n}")
        grads_out.append(g)
        deltas.append(d)
        new_m.append(nm)
        new_v.append(nv)
    return (loss, dx[None], *grads_out, *deltas, *new_m, *new_v)
```

```python
import functools
import math

import jax
import jax.numpy as jnp
from jax import lax
from jax.experimental import pallas as pl
from jax.experimental.pallas import tpu as pltpu

D_INNER = 2048
SSM_HEAD_DIM = 64
SSM_HEADS = 32
SSM_GROUPS = 4
SSM_STATE = 128
SSM_CHUNK = 128
GN = SSM_GROUPS * SSM_STATE
CONV_DIM = D_INNER + 2 * GN
SB_HEADS = 16
EPS = 1e-6
ADAM_LR = 0.001
ADAM_B1 = 0.9
ADAM_B2 = 0.999
ADAM_EPS = 1e-08
ADAM_WD = 0.01
ADAM_STEP = 10

LANES = 128
SUBLANES = 8
VMEM_LIMIT = 48 * 1024 * 1024
ADAM_BLOCK_BYTES = 1 << 20
F32 = jnp.float32
BF16 = jnp.bfloat16
MESH = pl.DeviceIdType.MESH


def _cparams(sem=None):
    return pltpu.CompilerParams(dimension_semantics=sem, vmem_limit_bytes=VMEM_LIMIT)


class _SideJob:
    def __init__(self, ins, out_shapes, n_sems, start, finish):
        self.ins, self.out_shapes, self.n_sems, self.start, self.finish = ins, out_shapes, n_sems, start, finish


def _call(body, *, grid, in_specs, out_specs, out_shape, scratch_shapes=(), sem, name, args, side=None):
    in_specs, out_specs, out_shape, scratch_shapes = list(in_specs), list(out_specs), list(out_shape), list(scratch_shapes)
    n_in, n_out = len(in_specs), len(out_specs)
    if side is None:
        outs = pl.pallas_call(body, grid=grid, in_specs=in_specs, out_specs=out_specs, out_shape=out_shape,
                              scratch_shapes=scratch_shapes, compiler_params=_cparams(sem), name=name)(*args)
        return list(outs), []
    k_in, k_out = len(side.ins), len(side.out_shapes)

    def wrapped(*refs):
        ins, s_ins = refs[:n_in], refs[n_in:n_in + k_in]
        o0 = n_in + k_in
        outs, s_outs = refs[o0:o0 + n_out], refs[o0 + n_out:o0 + n_out + k_out]
        scratch, send_sems, recv_sems = refs[o0 + n_out + k_out:-2], refs[-2], refs[-1]
        ids = [pl.program_id(a) for a in range(len(grid))]
        first = functools.reduce(jnp.logical_and, [p == 0 for p in ids])
        last = functools.reduce(jnp.logical_and, [p == g - 1 for p, g in zip(ids, grid)])

        @pl.when(first)
        def _():
            side.start(s_ins, s_outs, send_sems, recv_sems)

        body(*ins, *outs, *scratch)

        @pl.when(last)
        def _():
            side.finish(s_ins, s_outs, send_sems, recv_sems)

    outs = pl.pallas_call(
        wrapped, grid=grid, in_specs=in_specs + [ANY] * k_in, out_specs=out_specs + [ANY] * k_out,
        out_shape=out_shape + list(side.out_shapes),
        scratch_shapes=scratch_shapes + [pltpu.SemaphoreType.DMA((side.n_sems,)), pltpu.SemaphoreType.DMA((side.n_sems,))],
        compiler_params=_cparams(tuple("arbitrary" for _ in grid)), name=name)(*args, *side.ins)
    return list(outs[:n_out]), list(outs[n_out:])


def _tile(n, cands):
    for c in cands:
        if n % c == 0:
            return c
    return n


def _nt(a, b):
    return lax.dot_general(a, b, (((1,), (1,)), ((), ())), preferred_element_type=F32)


def _tn(a, b):
    return lax.dot_general(a, b, (((0,), (0,)), ((), ())), preferred_element_type=F32)


def _nn(a, b):
    return jnp.dot(a, b, preferred_element_type=F32)


def _split(x, pieces):
    out = []
    for _ in range(pieces - 1):
        h = x.astype(BF16)
        out.append(h)
        x = x - h.astype(F32)
    out.append(x.astype(BF16))
    return out


def _ones_dot(ones, x, *, ones_left, pieces=3):
    o16 = ones.astype(BF16)
    acc = None
    for piece in _split(x, pieces):
        term = _nn(o16, piece) if ones_left else _nn(piece, o16)
        acc = term if acc is None else acc + term
    return acc


def _row_sums(x, pieces=2):
    return _ones_dot(jnp.ones((x.shape[1], LANES), F32), x, ones_left=False, pieces=pieces)


def _softplus(x):
    return jnp.maximum(x, 0.0) + jnp.log(1.0 + jnp.exp(-jnp.abs(x)))


def _sigmoid(x):
    return 0.5 * jnp.tanh(0.5 * x) + 0.5


MM_TILE_MAX = 1408
MM_VMEM_BUDGET = 40 * 1024 * 1024


def _divisors(n, cap):
    out = [d for d in range(min(cap, n) // LANES * LANES, 0, -LANES) if n % d == 0]
    return out or [n]


def _mm_tiles(m, n, k, a_bytes, b_bytes, o_bytes, add_bytes):
    best = None
    for tm in _divisors(m, MM_TILE_MAX):
        for tn in _divisors(n, MM_TILE_MAX):
            for tk in _divisors(k, MM_TILE_MAX):
                vmem = 2 * (tm * tk * a_bytes + tk * tn * b_bytes + tm * tn * (o_bytes + add_bytes)) + tm * tn * 4
                if vmem > MM_VMEM_BUDGET:
                    continue
                score = (tm * tn * tk, tm * tn)
                if best is None or score > best[0]:
                    best = (score, (tm, tn, tk))
    return best[1]


def _matmul(a, b, *, ta=False, tb=False, add=None, out_dtype=F32, out_parts=1, name):
    a_parts = a.shape[0] if a.ndim == 3 else 1
    b_parts = b.shape[0] if b.ndim == 3 else 1
    assert not (ta and a_parts > 1)
    a2, b2 = a.shape[-2:], b.shape[-2:]
    m, k = (a2[1], a2[0]) if ta else (a2[0], a2[1] * a_parts)
    n, kb = (b2[0], b2[1] * b_parts) if tb else (b2[1] * b_parts, b2[0])
    assert kb == k, (a.shape, b.shape)
    n_unit = math.gcd(n // out_parts, n if tb else b2[1])
    k_unit = math.gcd(k // a_parts, b2[1] if tb else k)
    tm, tn, tk = _mm_tiles(m, n_unit, k_unit, a.dtype.itemsize, b.dtype.itemsize, jnp.dtype(out_dtype).itemsize,
                           0 if add is None else add.dtype.itemsize)
    nk = k // tk
    ka, kbp = (k // a_parts) // tk, (k // b_parts) // tk
    nb, no = (n // b_parts) // tn, (n // out_parts) // tn

    def body(*refs):
        if add is None:
            a_ref, b_ref, o_ref = refs[:3]
            add_ref = None
        else:
            a_ref, b_ref, add_ref, o_ref = refs[:4]
        kk = pl.program_id(2)
        dn = (((0 if ta else 1,), (1 if tb else 0,)), ((), ()))
        prod = lax.dot_general(a_ref[...].astype(BF16), b_ref[...].astype(BF16), dn, preferred_element_type=F32)

        def finish(r):
            if add_ref is not None:
                r = r + add_ref[...].astype(F32)
            o_ref[...] = r.astype(o_ref.dtype)

        if nk == 1:
            finish(prod)
            return
        acc_ref = refs[-1]

        @pl.when(kk == 0)
        def _():
            acc_ref[...] = prod

        @pl.when(jnp.logical_and(kk > 0, kk < nk - 1))
        def _():
            acc_ref[...] += prod

        @pl.when(kk == nk - 1)
        def _():
            finish(acc_ref[...] + prod)

    if ta:
        a_spec = pl.BlockSpec((tk, tm), lambda i, j, kk: (kk, i))
    elif a_parts > 1:
        a_spec = pl.BlockSpec((None, tm, tk), lambda i, j, kk: (kk // ka, i, kk % ka))
    else:
        a_spec = pl.BlockSpec((tm, tk), lambda i, j, kk: (i, kk))
    if b_parts == 1:
        b_spec = pl.BlockSpec((tn, tk), lambda i, j, kk: (j, kk)) if tb else pl.BlockSpec((tk, tn), lambda i, j, kk: (kk, j))
    elif tb:
        b_spec = pl.BlockSpec((None, tn, tk), lambda i, j, kk: (kk // kbp, j, kk % kbp))
    else:
        b_spec = pl.BlockSpec((None, tk, tn), lambda i, j, kk: (j // nb, kk, j % nb))
    if out_parts > 1:
        o_spec = pl.BlockSpec((None, tm, tn), lambda i, j, kk: (j // no, i, j % no))
        o_shape = jax.ShapeDtypeStruct((out_parts, m, n // out_parts), out_dtype)
    else:
        o_spec = pl.BlockSpec((tm, tn), lambda i, j, kk: (i, j))
        o_shape = jax.ShapeDtypeStruct((m, n), out_dtype)
    in_specs = [a_spec, b_spec]
    args = [a, b]
    if add is not None:
        in_specs.append(pl.BlockSpec((tm, tn), lambda i, j, kk: (i, j)))
        args.append(add)
    return pl.pallas_call(
        body,
        grid=(m // tm, n // tn, nk),
        in_specs=in_specs,
        out_specs=o_spec,
        out_shape=o_shape,
        scratch_shapes=[pltpu.VMEM((tm, tn), F32)] if nk > 1 else [],
        compiler_params=_cparams(("parallel", "parallel", "arbitrary")),
        name=name,
    )(*args)


def _rmsnorm_fwd(x, w, *, name):
    t, d = x.shape
    tb = _tile(t, (512, 256, 128))

    def body(x_ref, w_ref, o_ref):
        xv = x_ref[...]
        r = lax.rsqrt(jnp.mean(xv * xv, axis=-1, keepdims=True) + EPS)
        o_ref[...] = (xv * r * w_ref[...]).astype(o_ref.dtype)

    return pl.pallas_call(
        body,
        grid=(t // tb,),
        in_specs=[pl.BlockSpec((tb, d), lambda i: (i, 0)), pl.BlockSpec((1, d), lambda i: (0, 0))],
        out_specs=pl.BlockSpec((tb, d), lambda i: (i, 0)),
        out_shape=jax.ShapeDtypeStruct((t, d), BF16),
        compiler_params=_cparams(("parallel",)),
        name=name,
    )(x, w.reshape(1, d))


def _rmsnorm_bwd(x, dys, dres, *, side=None, name):
    t, d = x.shape
    tb = _tile(t, (256, 128))
    nn = len(dys)
    has_res = dres is not None

    def body(*refs):
        x_ref = refs[0]
        dy_refs = refs[1:1 + nn]
        w_refs = refs[1 + nn:1 + 2 * nn]
        pos = 1 + 2 * nn
        res_ref = refs[pos] if has_res else None
        pos += 1 if has_res else 0
        dx_ref = refs[pos]
        dw_refs = refs[pos + 1:pos + 1 + nn]
        i = pl.program_id(0)
        xv = x_ref[...]
        r = lax.rsqrt(jnp.mean(xv * xv, axis=-1, keepdims=True) + EPS)
        xn = xv * r
        dx = res_ref[...] if has_res else jnp.zeros_like(xv)
        for q in range(nn):
            dy = dy_refs[q][...].astype(F32)
            g = dy * w_refs[q][...]
            dx = dx + r * (g - xn * jnp.mean(g * xn, axis=-1, keepdims=True))
            dwp = jnp.sum(dy * xn, axis=0, keepdims=True)

            @pl.when(i == 0)
            def _(q=q, dwp=dwp):
                dw_refs[q][...] = dwp

            @pl.when(i > 0)
            def _(q=q, dwp=dwp):
                dw_refs[q][...] += dwp
        dx_ref[...] = dx

    row = pl.BlockSpec((tb, d), lambda i: (i, 0))
    vec = pl.BlockSpec((1, d), lambda i: (0, 0))
    in_specs = [row] + [row] * nn + [vec] * nn + ([row] if has_res else [])
    args = [x] + [p[0] for p in dys] + [p[1].reshape(1, d) for p in dys] + ([dres] if has_res else [])
    outs, side_outs = _call(
        body,
        grid=(t // tb,),
        in_specs=in_specs,
        out_specs=[row] + [vec] * nn,
        out_shape=[jax.ShapeDtypeStruct((t, d), F32)] + [jax.ShapeDtypeStruct((1, d), F32)] * nn,
        sem=("arbitrary",),
        name=name,
        args=args,
        side=side,
    )
    return (outs[0], list(outs[1:])) if side is None else (outs[0], list(outs[1:]), side_outs)


def _loss_head(x, w, target, *, name):
    t, d = x.shape
    tb = _tile(t, (256, 128))

    def body(x_ref, w_ref, t_ref, loss_ref, dx_ref, dw_ref):
        i = pl.program_id(0)
        xv = x_ref[...]
        wv = w_ref[...]
        r = lax.rsqrt(jnp.mean(xv * xv, axis=-1, keepdims=True) + EPS)
        xn = xv * r
        e = xn * wv - t_ref[...]
        lp = 0.5 * jnp.sum(jnp.mean(e * e, axis=-1, keepdims=True), axis=0, keepdims=True)
        dy = e * (1.0 / d)
        g = dy * wv
        dx_ref[...] = r * (g - xn * jnp.mean(g * xn, axis=-1, keepdims=True))
        dwp = jnp.sum(dy * xn, axis=0, keepdims=True)
        lpv = jnp.broadcast_to(lp, (1, LANES)) * (1.0 / LANES)

        @pl.when(i == 0)
        def _():
            dw_ref[...] = dwp
            loss_ref[...] = lpv

        @pl.when(i > 0)
        def _():
            dw_ref[...] += dwp
            loss_ref[...] += lpv

    row = pl.BlockSpec((tb, d), lambda i: (i, 0))
    vec = pl.BlockSpec((1, d), lambda i: (0, 0))
    return pl.pallas_call(
        body,
        grid=(t // tb,),
        in_specs=[row, vec, row],
        out_specs=[pl.BlockSpec((1, LANES), lambda i: (0, 0)), row, vec],
        out_shape=[jax.ShapeDtypeStruct((1, LANES), F32), jax.ShapeDtypeStruct((t, d), F32),
                   jax.ShapeDtypeStruct((1, d), F32)],
        compiler_params=_cparams(("arbitrary",)),
        name=name,
    )(x, w.reshape(1, d), target)


ROW_CHUNK = 64
PAD = SUBLANES


def _shifted(pad_ref, r0, rows, back):
    return pad_ref[pl.ds(PAD + r0 - back, rows), :]


def _conv_taps(pad_ref, w_ref, r0, rows, kw):
    acc = None
    for j in range(kw):
        term = _shifted(pad_ref, r0, rows, kw - 1 - j) * w_ref[j:j + 1, :]
        acc = term if acc is None else acc + term
    return acc


def _fill_pad(pad_ref, x_ref, t):
    pad_ref[0:PAD, :] = jnp.zeros((PAD, pad_ref.shape[1]), F32)
    pad_ref[pl.ds(PAD + t, PAD), :] = jnp.zeros((PAD, pad_ref.shape[1]), F32)
    pad_ref[pl.ds(PAD, t), :] = x_ref[...].astype(F32)


def _conv_silu_fwd(x, w, b, *, x_off=0, name):
    t = x.shape[0]
    kw, c = w.shape
    cw = _tile(math.gcd(c, x_off) if x_off else c, (256, 128))
    ob = x_off // cw
    rc = _tile(t, (ROW_CHUNK,))

    def body(x_ref, w_ref, b_ref, o_ref, pad_ref):
        _fill_pad(pad_ref, x_ref, t)
        for r0 in range(0, t, rc):
            pre = _conv_taps(pad_ref, w_ref, r0, rc, kw) + b_ref[...]
            o_ref[pl.ds(r0, rc), :] = pre * _sigmoid(pre)

    strip = pl.BlockSpec((t, cw), lambda i: (0, i))
    return pl.pallas_call(
        body,
        grid=(c // cw,),
        in_specs=[pl.BlockSpec((t, cw), lambda i: (0, i + ob)), pl.BlockSpec((kw, cw), lambda i: (0, i)),
                  pl.BlockSpec((1, cw), lambda i: (0, i))],
        out_specs=strip,
        out_shape=jax.ShapeDtypeStruct((t, c), F32),
        scratch_shapes=[pltpu.VMEM((t + 2 * PAD, cw), F32)],
        compiler_params=_cparams(("parallel",)),
        name=name,
    )(x, w, b.reshape(1, c))


def _conv_bwd_core(dpre_pad_ref, x_pad_ref, w_ref, dx_ref, dw_ref, db_ref, t, rc, kw):
    cw = dx_ref.shape[1]

    def fold(a):
        return jnp.sum(a.reshape(rc // SUBLANES, SUBLANES, cw), axis=0) if rc % SUBLANES == 0 else jnp.sum(a, axis=0, keepdims=True)

    dws = [None] * kw
    dbs = None
    for r0 in range(0, t, rc):
        dpre = dpre_pad_ref[pl.ds(PAD + r0, rc), :]
        dx = None
        for j in range(kw):
            s = kw - 1 - j
            term = dpre_pad_ref[pl.ds(PAD + r0 + s, rc), :] * w_ref[j:j + 1, :]
            dx = term if dx is None else dx + term
            part = fold(dpre * _shifted(x_pad_ref, r0, rc, s))
            dws[j] = part if dws[j] is None else dws[j] + part
        part = fold(dpre)
        dbs = part if dbs is None else dbs + part
        dx_ref[pl.ds(r0, rc), :] = dx
    for j in range(kw):
        dw_ref[j:j + 1, :] = jnp.sum(dws[j], axis=0, keepdims=True)
    db_ref[...] = jnp.sum(dbs, axis=0, keepdims=True)


def _conv_silu_bwd(x, w, b, dact, *, x_off=0, into=None, name):
    t = x.shape[0]
    kw, c = w.shape
    parts = dact if isinstance(dact, (list, tuple)) else [dact]
    widths = [p.shape[1] for p in parts]
    assert sum(widths) == c
    cw = _tile(functools.reduce(math.gcd, widths + [x_off or c]), (256, 128) if len(parts) == 1 else (128,))
    ob = x_off // cw
    rc = _tile(t, (ROW_CHUNK,))
    firsts = [sum(widths[:p]) // cw for p in range(len(parts))]
    counts = [wd // cw for wd in widths]
    n_p = len(parts)

    def body(x_ref, w_ref, b_ref, *rest):
        da_refs = rest[:n_p]
        dx_ref, dw_ref, db_ref, xpad_ref, dpad_ref = rest[-5 - (n_p > 1):][:5]
        if n_p > 1:
            da_ref = rest[-1]
            i = pl.program_id(0)
            for p in range(n_p):
                @pl.when(jnp.logical_and(i >= firsts[p], i < firsts[p] + counts[p]))
                def _(p=p):
                    da_ref[...] = da_refs[p][...]
        else:
            da_ref = da_refs[0]
        _fill_pad(xpad_ref, x_ref, t)
        dpad_ref[0:PAD, :] = jnp.zeros((PAD, cw), F32)
        dpad_ref[pl.ds(PAD + t, PAD), :] = jnp.zeros((PAD, cw), F32)
        for r0 in range(0, t, rc):
            pre = _conv_taps(xpad_ref, w_ref, r0, rc, kw) + b_ref[...]
            sg = _sigmoid(pre)
            dpad_ref[pl.ds(PAD + r0, rc), :] = da_ref[pl.ds(r0, rc), :] * (sg * (1.0 + pre * (1.0 - sg)))
        _conv_bwd_core(dpad_ref, xpad_ref, w_ref, dx_ref, dw_ref, db_ref, t, rc, kw)

    strip = pl.BlockSpec((t, cw), lambda i: (0, i))
    wspec = pl.BlockSpec((kw, cw), lambda i: (0, i))
    bspec = pl.BlockSpec((1, cw), lambda i: (0, i))
    xspec = pl.BlockSpec((t, cw), lambda i: (0, i + ob))
    dspecs = [pl.BlockSpec((t, cw), lambda i, f=f, n=n: (0, jnp.clip(i - f, 0, n - 1))) for f, n in zip(firsts, counts)]
    extra = {} if into is None else dict(input_output_aliases={3 + n_p: 0})
    pad = pltpu.VMEM((t + 2 * PAD, cw), F32)
    return pl.pallas_call(
        body,
        grid=(c // cw,),
        in_specs=[xspec, wspec, bspec] + dspecs + ([] if into is None else [ANY]),
        out_specs=[strip if into is None else xspec, wspec, bspec],
        out_shape=[jax.ShapeDtypeStruct((t, c) if into is None else into.shape, F32), jax.ShapeDtypeStruct((kw, c), F32),
                   jax.ShapeDtypeStruct((1, c), F32)],
        scratch_shapes=[pad, pad] + ([pltpu.VMEM((t, cw), F32)] if n_p > 1 else []),
        compiler_params=_cparams(("arbitrary",)),
        name=name,
        **extra,
    )(x, w, b.reshape(1, c), *parts, *([] if into is None else [into]))


def _conv_glu_fwd(hid, w, b, *, side=None, name):
    t, c2 = hid.shape
    f = c2 // 2
    kw = w.shape[0]
    cw = _tile(f, (256, 128))
    nf = f // cw
    rc = _tile(t, (ROW_CHUNK,))

    def body(g_ref, v_ref, wg_ref, wv_ref, bg_ref, bv_ref, o_ref, gpad_ref, vpad_ref):
        _fill_pad(gpad_ref, g_ref, t)
        _fill_pad(vpad_ref, v_ref, t)
        for r0 in range(0, t, rc):
            gate = _conv_taps(gpad_ref, wg_ref, r0, rc, kw) + bg_ref[...]
            val = _conv_taps(vpad_ref, wv_ref, r0, rc, kw) + bv_ref[...]
            o_ref[pl.ds(r0, rc), :] = (gate * _sigmoid(gate) * val).astype(o_ref.dtype)

    gs = pl.BlockSpec((t, cw), lambda i: (0, i))
    vs = pl.BlockSpec((t, cw), lambda i: (0, i + nf))
    b2 = b.reshape(1, c2)
    (act,), side_outs = _call(
        body,
        grid=(nf,),
        in_specs=[gs, vs, pl.BlockSpec((kw, cw), lambda i: (0, i)), pl.BlockSpec((kw, cw), lambda i: (0, i + nf)),
                  pl.BlockSpec((1, cw), lambda i: (0, i)), pl.BlockSpec((1, cw), lambda i: (0, i + nf))],
        out_specs=[gs],
        out_shape=[jax.ShapeDtypeStruct((t, f), BF16)],
        scratch_shapes=[pltpu.VMEM((t + 2 * PAD, cw), F32), pltpu.VMEM((t + 2 * PAD, cw), F32)],
        sem=("parallel",),
        name=name,
        args=(hid, hid, w, w, b2, b2),
        side=side,
    )
    return act, side_outs


def _conv_glu_bwd(hid, w, b, dact, *, name):
    t, c2 = hid.shape
    f = c2 // 2
    kw = w.shape[0]
    cw = _tile(f, (128,))
    nf = f // cw
    rc = _tile(t, (ROW_CHUNK,))

    def body(g_ref, v_ref, wg_ref, wv_ref, bg_ref, bv_ref, da_ref,
             dgv_ref, dwg_ref, dwv_ref, dbg_ref, dbv_ref,
             gpad_ref, vpad_ref, dgpad_ref, dvpad_ref):
        _fill_pad(gpad_ref, g_ref, t)
        _fill_pad(vpad_ref, v_ref, t)
        for ref in (dgpad_ref, dvpad_ref):
            ref[0:PAD, :] = jnp.zeros((PAD, cw), F32)
            ref[pl.ds(PAD + t, PAD), :] = jnp.zeros((PAD, cw), F32)
        for r0 in range(0, t, rc):
            gate = _conv_taps(gpad_ref, wg_ref, r0, rc, kw) + bg_ref[...]
            val = _conv_taps(vpad_ref, wv_ref, r0, rc, kw) + bv_ref[...]
            sg = _sigmoid(gate)
            da = da_ref[pl.ds(r0, rc), :].astype(F32)
            dgpad_ref[pl.ds(PAD + r0, rc), :] = da * val * (sg * (1.0 + gate * (1.0 - sg)))
            dvpad_ref[pl.ds(PAD + r0, rc), :] = da * (gate * sg)
        _conv_bwd_core(dgpad_ref, gpad_ref, wg_ref, dgv_ref.at[0], dwg_ref, dbg_ref, t, rc, kw)
        _conv_bwd_core(dvpad_ref, vpad_ref, wv_ref, dgv_ref.at[1], dwv_ref, dbv_ref, t, rc, kw)

    gs = pl.BlockSpec((t, cw), lambda i: (0, i))
    vs = pl.BlockSpec((t, cw), lambda i: (0, i + nf))
    wg = pl.BlockSpec((kw, cw), lambda i: (0, i))
    wv = pl.BlockSpec((kw, cw), lambda i: (0, i + nf))
    bg = pl.BlockSpec((1, cw), lambda i: (0, i))
    bv = pl.BlockSpec((1, cw), lambda i: (0, i + nf))
    b2 = b.reshape(1, c2)
    pad = pltpu.VMEM((t + 2 * PAD, cw), F32)
    return pl.pallas_call(
        body,
        grid=(nf,),
        in_specs=[gs, vs, wg, wv, bg, bv, gs],
        out_specs=[pl.BlockSpec((2, t, cw), lambda i: (0, 0, i)), wg, wg, bg, bg],
        out_shape=[jax.ShapeDtypeStruct((2, t, f), F32),
                   jax.ShapeDtypeStruct((kw, f), F32), jax.ShapeDtypeStruct((kw, f), F32),
                   jax.ShapeDtypeStruct((1, f), F32), jax.ShapeDtypeStruct((1, f), F32)],
        scratch_shapes=[pad, pad, pad, pad],
        compiler_params=_cparams(("parallel",)),
        name=name,
    )(hid, hid, w, w, b2, b2, dact)


def _gate_norm_fwd(y, zx, w, *, name):
    t, di = y.shape
    gsz = di // SSM_GROUPS
    tb = _tile(t, (256, 128))

    def body(y_ref, z_ref, w_ref, o_ref):
        for g in range(SSM_GROUPS):
            sl = slice(g * gsz, (g + 1) * gsz)
            zv = z_ref[:, sl]
            gv = y_ref[:, sl] * (zv * _sigmoid(zv))
            r = lax.rsqrt(jnp.mean(gv * gv, axis=-1, keepdims=True) + EPS)
            o_ref[:, sl] = (gv * r * w_ref[:, sl]).astype(o_ref.dtype)

    row = pl.BlockSpec((tb, di), lambda i: (i, 0))
    return pl.pallas_call(
        body,
        grid=(t // tb,),
        in_specs=[row, row, pl.BlockSpec((1, di), lambda i: (0, 0))],
        out_specs=row,
        out_shape=jax.ShapeDtypeStruct((t, di), BF16),
        compiler_params=_cparams(("parallel",)),
        name=name,
    )(y, zx, w.reshape(1, di))


def _gate_norm_bwd(y, zx, w, dyn, *, side=None, name):
    t, di = y.shape
    gsz = di // SSM_GROUPS
    tb = _tile(t, (256, 128))

    def body(y_ref, z_ref, w_ref, d_ref, dy_ref, dz_ref, dw_ref):
        i = pl.program_id(0)
        for g in range(SSM_GROUPS):
            sl = slice(g * gsz, (g + 1) * gsz)
            zv = z_ref[:, sl]
            yv = y_ref[:, sl]
            sg = _sigmoid(zv)
            sz = zv * sg
            gv = yv * sz
            r = lax.rsqrt(jnp.mean(gv * gv, axis=-1, keepdims=True) + EPS)
            gn = gv * r
            dn = d_ref[:, sl].astype(F32)
            q = dn * w_ref[:, sl]
            dg = r * (q - gn * jnp.mean(q * gn, axis=-1, keepdims=True))
            dy_ref[:, sl] = dg * sz
            dz_ref[:, sl] = dg * yv * (sg * (1.0 + zv * (1.0 - sg)))
            dwp = jnp.sum(dn * gn, axis=0, keepdims=True)

            @pl.when(i == 0)
            def _(sl=sl, dwp=dwp):
                dw_ref[:, sl] = dwp

            @pl.when(i > 0)
            def _(sl=sl, dwp=dwp):
                dw_ref[:, sl] += dwp

    row = pl.BlockSpec((tb, di), lambda i: (i, 0))
    vec = pl.BlockSpec((1, di), lambda i: (0, 0))
    outs, side_outs = _call(
        body,
        grid=(t // tb,),
        in_specs=[row, row, vec, row],
        out_specs=[row, row, vec],
        out_shape=[jax.ShapeDtypeStruct((t, di), F32), jax.ShapeDtypeStruct((t, zx.shape[1]), F32),
                   jax.ShapeDtypeStruct((1, di), F32)],
        sem=("arbitrary",),
        name=name,
        args=(y, zx, w.reshape(1, di), dyn),
        side=side,
    )
    return (*outs, side_outs)


def _adamw(w, g, m, v, *, name):
    shape = w.shape
    cols = shape[-1]
    rows = w.size // cols
    w2, g2, m2, v2 = (a.reshape(rows, cols) for a in (w, g, m, v))
    tr = rows if rows * cols * 4 <= ADAM_BLOCK_BYTES else _row_tile(rows, cols)
    c1 = 1.0 - ADAM_B1 ** ADAM_STEP
    c2 = 1.0 - ADAM_B2 ** ADAM_STEP

    def body(w_ref, g_ref, m_ref, v_ref, d_ref, nm_ref, nv_ref):
        gv = g_ref[...]
        nm = ADAM_B1 * m_ref[...] + (1.0 - ADAM_B1) * gv
        nv = ADAM_B2 * v_ref[...] + (1.0 - ADAM_B2) * (gv * gv)
        d_ref[...] = -ADAM_LR * ((nm / c1) / (jnp.sqrt(nv / c2) + ADAM_EPS) + ADAM_WD * w_ref[...])
        nm_ref[...] = nm
        nv_ref[...] = nv

    blk = pl.BlockSpec((tr, cols), lambda i: (i, 0))
    outs = pl.pallas_call(
        body,
        grid=(rows // tr,),
        in_specs=[blk] * 4,
        out_specs=[blk] * 3,
        out_shape=[jax.ShapeDtypeStruct((rows, cols), F32)] * 3,
        compiler_params=_cparams(("parallel",)),
        name=name,
    )(w2, g2, m2, v2)
    return tuple(o.reshape(shape) for o in outs)


def _adamw_layers(w, gs, m, v, *, name):
    n_l, rows, cols = w.shape
    assert len(gs) == n_l
    tr = _row_tile(rows, cols)
    c1 = 1.0 - ADAM_B1 ** ADAM_STEP
    c2 = 1.0 - ADAM_B2 ** ADAM_STEP

    def body(*refs):
        w_ref, m_ref, v_ref = refs[:3]
        g_refs = refs[3:3 + n_l]
        g_ref, d_ref, nm_ref, nv_ref = refs[3 + n_l:]
        layer = pl.program_id(0)
        gv = g_refs[0][...]
        for q in range(1, n_l):
            gv = jnp.where(layer == q, g_refs[q][...], gv)
        nm = ADAM_B1 * m_ref[...] + (1.0 - ADAM_B1) * gv
        nv = ADAM_B2 * v_ref[...] + (1.0 - ADAM_B2) * (gv * gv)
        g_ref[...] = gv
        d_ref[...] = -ADAM_LR * ((nm / c1) / (jnp.sqrt(nv / c2) + ADAM_EPS) + ADAM_WD * w_ref[...])
        nm_ref[...] = nm
        nv_ref[...] = nv

    stacked = pl.BlockSpec((None, tr, cols), lambda l, i: (l, i, 0))
    single = pl.BlockSpec((tr, cols), lambda l, i: (i, 0))
    return pl.pallas_call(
        body,
        grid=(n_l, rows // tr),
        in_specs=[stacked] * 3 + [single] * n_l,
        out_specs=[stacked] * 4,
        out_shape=[jax.ShapeDtypeStruct(w.shape, F32)] * 4,
        compiler_params=_cparams(("parallel", "parallel")),
        name=name,
    )(w, m, v, *gs)


def _ssd_scalars(dtc_ref, dtr_ref, hpc_ref, hpr_ref, ln):
    assert SSM_CHUNK == SSM_STATE == LANES, "the SSD kernels mix chunk, state and lane-wide tiles freely"
    bias_c, alog_c = hpc_ref[0, 0:1, :], hpc_ref[0, 1:2, :]
    bias_r, alog_r = hpr_ref[0, :, 0:1], hpr_ref[0, :, 1:2]
    a_c, a_r = -jnp.exp(alog_c), -jnp.exp(alog_r)
    raw_c = dtc_ref[0] + bias_c
    dt_c = _softplus(raw_c)
    dt_r = _softplus(dtr_ref[0] + bias_r)
    row = lax.broadcasted_iota(jnp.int32, (ln, ln), 0)
    col = lax.broadcasted_iota(jnp.int32, (ln, ln), 1)
    lower = (col <= row).astype(F32)
    upper = (row <= col).astype(F32)
    acs_c = _ones_dot(lower, dt_c * a_c, ones_left=True)
    acs_r = _ones_dot(upper, dt_r * a_r, ones_left=False)
    return raw_c, dt_c, a_c, acs_c, acs_r, row, col


def _ssd_specs(t, di, g_n, n_st, rp, ln, r_h, rev):
    nc = t // ln
    cidx = (lambda c: nc - 1 - c) if rev else (lambda c: c)
    xs = pl.BlockSpec((ln, rp), lambda g, c: (cidx(c), g))
    bm = pl.BlockSpec((ln, n_st), lambda g, c: (cidx(c), di // n_st + g))
    cm = pl.BlockSpec((ln, n_st), lambda g, c: (cidx(c), di // n_st + g_n + g))
    dtc = pl.BlockSpec((1, ln, r_h), lambda g, c: (g, cidx(c), 0))
    dtr = pl.BlockSpec((1, r_h, ln), lambda g, c: (g, 0, cidx(c)))
    hpc = pl.BlockSpec((1, 3, r_h), lambda g, c: (g, 0, 0))
    hpr = pl.BlockSpec((1, r_h, 3), lambda g, c: (g, 0, 0))
    prev = pl.BlockSpec((1, rp, n_st), lambda g, c: (cidx(c), g, 0))
    return xs, bm, cm, dtc, dtr, hpc, hpr, prev


def _ssd_fwd(xbc, dtc, dtr, hpc, hpr, *, side=None, name):
    t = xbc.shape[0]
    di, g_n, n_st, p_h, ln = D_INNER, SSM_GROUPS, SSM_STATE, SSM_HEAD_DIM, SSM_CHUNK
    r_h = SSM_HEADS // g_n
    rp = r_h * p_h
    nc = t // ln

    def body(xs_ref, b_ref, c_ref, dtc_ref, dtr_ref, hpc_ref, hpr_ref, y_ref, prev_ref, st_ref):
        @pl.when(pl.program_id(1) == 0)
        def _():
            st_ref[...] = jnp.zeros_like(st_ref)

        _, dt_c, _, acs_c, acs_r, row, col = _ssd_scalars(dtc_ref, dtr_ref, hpc_ref, hpr_ref, ln)
        bm = b_ref[...]
        cm = c_ref[...]
        cm16 = cm.astype(BF16)
        cb = _nt(cm16, bm.astype(BF16))
        causal = row >= col
        for r in range(r_h):
            sl = slice(r * p_h, (r + 1) * p_h)
            xs = xs_ref[:, sl]
            acs = jnp.broadcast_to(acs_c[:, r:r + 1], (ln, ln))
            last = acs[ln - 1:ln, :]
            lm = jnp.where(causal, jnp.exp(acs - acs_r[r:r + 1, :]), 0.0)
            xd = (xs * jnp.broadcast_to(dt_c[:, r:r + 1], (ln, p_h))).astype(BF16)
            prev = st_ref[sl, :]
            y = _nn((cb * lm).astype(BF16), xd)
            y = y + _nt(cm16, prev.astype(BF16)) * jnp.exp(acs[:, :p_h])
            y_ref[:, sl] = y + hpc_ref[0, 2:3, r:r + 1] * xs
            prev_ref[0, sl, :] = prev
            bd = (bm * jnp.exp(last - acs[:, :n_st])).astype(BF16)
            st_ref[sl, :] = prev * jnp.exp(last[:, :n_st]) + _tn(xd, bd)

    xs, bm, cm, dtcs, dtrs, hpcs, hprs, prev = _ssd_specs(t, di, g_n, n_st, rp, ln, r_h, False)
    (y, prev_out), side_outs = _call(
        body,
        grid=(g_n, nc),
        in_specs=[xs, bm, cm, dtcs, dtrs, hpcs, hprs],
        out_specs=[xs, prev],
        out_shape=[jax.ShapeDtypeStruct((t, di), F32), jax.ShapeDtypeStruct((nc, g_n * rp, n_st), F32)],
        scratch_shapes=[pltpu.VMEM((rp, n_st), F32)],
        sem=("parallel", "arbitrary"),
        name=name,
        args=(xbc, xbc, xbc, dtc, dtr, hpc, hpr),
        side=side,
    )
    return y, prev_out, side_outs


def _ssd_bwd(xbc, dtc, dtr, hpc, hpr, prev, dy, *, side=None, name):
    t = xbc.shape[0]
    di, g_n, n_st, p_h, ln = D_INNER, SSM_GROUPS, SSM_STATE, SSM_HEAD_DIM, SSM_CHUNK
    r_h = SSM_HEADS // g_n
    rp = r_h * p_h
    nc = t // ln

    def body(xs_ref, b_ref, c_ref, dtc_ref, dtr_ref, hpc_ref, hpr_ref, prev_ref, dy_ref,
             dxs_ref, db_ref, dc_ref, ddt_ref, hg_ref, ds_ref):
        step = pl.program_id(1)

        @pl.when(step == 0)
        def _():
            ds_ref[...] = jnp.zeros_like(ds_ref)

        raw_c, dt_c, a_c, acs_c, acs_r, row, col = _ssd_scalars(dtc_ref, dtr_ref, hpc_ref, hpr_ref, ln)
        bm = b_ref[...]
        cm = c_ref[...]
        bm16, cm16 = bm.astype(BF16), cm.astype(BF16)
        cb = _nt(cm16, bm16)
        cbt = _nt(bm16, cm16)
        lane_r = lax.broadcasted_iota(jnp.int32, (ln, r_h), 1)
        dacs_all = jnp.zeros((ln, r_h), F32)
        ddtx_all = jnp.zeros((ln, r_h), F32)
        dd_all = jnp.zeros((ln, r_h), F32)
        dcb = jnp.zeros((ln, ln), F32)
        dcbt = jnp.zeros((ln, ln), F32)
        dc_acc = jnp.zeros((ln, n_st), F32)
        db_acc = jnp.zeros((ln, n_st), F32)
        for r in range(r_h):
            sl = slice(r * p_h, (r + 1) * p_h)
            xs = xs_ref[:, sl]
            dyv = dy_ref[:, sl]
            dy16 = dyv.astype(BF16)
            acs = jnp.broadcast_to(acs_c[:, r:r + 1], (ln, ln))
            dtv = jnp.broadcast_to(dt_c[:, r:r + 1], (ln, p_h))
            acsr = acs_r[r:r + 1, :]
            last = acs[ln - 1:ln, :]
            xd = xs * dtv
            xd16 = xd.astype(BF16)
            lm = jnp.where(row >= col, jnp.exp(acs - acsr), 0.0)
            lmt = jnp.where(col >= row, jnp.exp(acsr - acs), 0.0)
            m_ls = cb * lm
            m_sl = cbt * lmt
            dm = _nt(dy16, xd16)
            dmt = _nt(xd16, dy16)
            dxd = _nn(m_sl.astype(BF16), dy16)
            dacs = _row_sums(dm * m_ls - dmt * m_sl)
            dcb = dcb + dm * lm
            dcbt = dcbt + dmt * lmt
            prev = prev_ref[0, sl, :]
            prev16 = prev.astype(BF16)
            e = jnp.exp(acs[:, :p_h])
            y_off = _nt(cm16, prev16) * e
            dacs = dacs + _row_sums(dyv * y_off)
            dyo16 = (dyv * e).astype(BF16)
            dc_acc = dc_acc + _nn(dyo16, prev16)
            dprev = _tn(dyo16, cm16)
            ds = ds_ref[sl, :]
            ds16 = ds.astype(BF16)
            decay = jnp.exp(last - acs)[:, :n_st]
            bd16 = (bm * decay).astype(BF16)
            dbd = _nn(xd16, ds16)
            dxd = dxd + _nt(bd16, ds16)
            db_acc = db_acc + dbd * decay
            tdec = _row_sums(dbd * bm) * decay
            cd = jnp.exp(last)
            dlast = (jnp.sum(tdec, axis=0, keepdims=True)
                     + jnp.sum(_row_sums(prev * ds), axis=0, keepdims=True) * cd)
            ds_ref[sl, :] = dprev + cd[:, :n_st] * ds
            dskip = hpc_ref[0, 2:3, r:r + 1]
            dxs_ref[:, sl] = dxd * dtv + dskip * dyv
            dacs = dacs - tdec + jnp.where(row == ln - 1, dlast, 0.0)
            dacs_all = jnp.where(lane_r == r, dacs[:, :r_h], dacs_all)
            ddtx_all = jnp.where(lane_r == r, _row_sums(dxd * xs)[:, :r_h], ddtx_all)
            dd_all = jnp.where(lane_r == r, _row_sums(dyv * xs)[:, :r_h], dd_all)
        dc_ref[...] = dc_acc + _nn(dcb.astype(BF16), bm16)
        db_ref[...] = db_acc + _nn(dcbt.astype(BF16), cm16)
        upper = (row <= col).astype(F32)
        dad = _ones_dot(upper, dacs_all, ones_left=True)
        ddt = dad * a_c + ddtx_all
        ddt_raw = ddt * _sigmoid(raw_c)
        ddt_ref[0] = ddt_raw
        d_bias = jnp.sum(ddt_raw, axis=0, keepdims=True)
        d_alog = jnp.sum(dad * dt_c, axis=0, keepdims=True) * a_c
        d_d = jnp.sum(dd_all, axis=0, keepdims=True)
        hg = jnp.concatenate([d_bias, d_alog, d_d], axis=0)

        @pl.when(step == 0)
        def _():
            hg_ref[0] = hg

        @pl.when(step > 0)
        def _():
            hg_ref[0] += hg

    xs, bms, cms, dtcs, dtrs, hpcs, hprs, prevs = _ssd_specs(t, di, g_n, n_st, rp, ln, r_h, True)
    bout = pl.BlockSpec((ln, n_st), lambda g, c: (nc - 1 - c, g))
    outs, side_outs = _call(
        body,
        grid=(g_n, nc),
        in_specs=[xs, bms, cms, dtcs, dtrs, hpcs, hprs, prevs, xs],
        out_specs=[xs, bout, bout, dtcs, hpcs],
        out_shape=[jax.ShapeDtypeStruct((t, di), F32), jax.ShapeDtypeStruct((t, g_n * n_st), F32),
                   jax.ShapeDtypeStruct((t, g_n * n_st), F32), jax.ShapeDtypeStruct((g_n, t, r_h), F32),
                   jax.ShapeDtypeStruct((g_n, 3, r_h), F32)],
        scratch_shapes=[pltpu.VMEM((rp, n_st), F32)],
        sem=("parallel", "arbitrary"),
        name=name,
        args=(xbc, xbc, xbc, dtc, dtr, hpc, hpr, prev, dy),
        side=side,
    )
    return (*outs, side_outs)


SB_KEYS = 256
SB_QUERIES = (512, 256)
SB_CUTOFF = 110.0
SB_PIECES = 2


def _sb_logits(qs, kv, valid):
    z = _nt(qs, kv)
    nz = -z
    lg = jnp.minimum(nz, 0.0) - jnp.log(1.0 + jnp.exp(jnp.minimum(z, nz)))
    return z + lg, (lg if valid is None else jnp.where(valid, lg, 0.0))


def _sb_iota(tq):
    diff = lax.broadcasted_iota(jnp.int32, (tq, SB_KEYS), 1) - lax.broadcasted_iota(jnp.int32, (tq, SB_KEYS), 0)
    krow = lax.broadcasted_iota(jnp.int32, (SB_KEYS, SB_KEYS), 0)
    kcol = lax.broadcasted_iota(jnp.int32, (SB_KEYS, SB_KEYS), 1)
    return diff, krow, kcol


def _sb_scale(d):
    scale = 1.0 / math.sqrt(d)
    assert math.frexp(scale)[0] == 0.5, "the scale is folded into bf16 queries: it must be a power of two"
    return scale


def _key_rows(j):
    return pl.ds(pl.multiple_of(j * SB_KEYS, SB_KEYS), SB_KEYS)


def _sb_fwd(q, k, v, n_heads, *, side=None, name):
    t, hd = q.shape
    d = hd // n_heads
    hpt = LANES // d
    assert hpt * d == LANES and n_heads % hpt == 0
    tq = _tile(t, SB_QUERIES)
    nq = t // tq
    kpq = tq // SB_KEYS
    scale = _sb_scale(d)

    def body(q_ref, k_ref, v_ref, o_ref, lt_ref, first_ref):
        i = pl.program_id(1)
        diff, krow, kcol = _sb_iota(tq)
        later = (krow > kcol).astype(F32)
        nb = i * kpq
        for hh in range(hpt):
            sl = slice(hh * d, (hh + 1) * d)
            qs = (q_ref[:, sl].astype(F32) * scale).astype(BF16)

            def block(j, carry, valid, qs=qs, sl=sl):
                acc, cl = carry
                rows = _key_rows(j)
                ls, lg = _sb_logits(qs, k_ref[rows, sl], valid)
                cs = _ones_dot(later, lg, ones_left=False, pieces=SB_PIECES)
                att = jnp.exp(ls + (cs + cl))
                if valid is not None:
                    att = jnp.where(valid, att, 0.0)
                acc = acc + _nn(att.astype(BF16), v_ref[rows, sl])
                return acc, cl + (cs[:, 0:1] + lg[:, 0:1])

            carry = (jnp.zeros((tq, d), F32), jnp.zeros((tq, 1), F32))
            for m in range(kpq - 1, -1, -1):
                carry = block(i * kpq + m, carry, diff < -m * SB_KEYS)

            def more(st):
                s, _, cl = st
                return jnp.logical_and(s < nb, jnp.max(cl) > -SB_CUTOFF)

            def step(st, block=block):
                s, acc, cl = st
                acc, cl = block(nb - 1 - s, (acc, cl), None)
                return s + 1, acc, cl

            walked, acc, cl = lax.while_loop(more, step, (jnp.int32(0),) + carry)
            o_ref[:, sl] = acc.astype(o_ref.dtype)
            lt_ref[hh] = cl
            first_ref[pl.program_id(0) * hpt + hh, i] = nb - walked

    qs = pl.BlockSpec((tq, LANES), lambda p, i: (i, p))
    ls = pl.BlockSpec((hpt, tq, 1), lambda p, i: (p, i, 0))
    ks = pl.BlockSpec((t, LANES), lambda p, i: (0, p))
    outs, side_outs = _call(
        body,
        grid=(n_heads // hpt, nq),
        in_specs=[qs, ks, ks],
        out_specs=[qs, ls, pl.BlockSpec(memory_space=pltpu.SMEM)],
        out_shape=[jax.ShapeDtypeStruct((t, hd), BF16), jax.ShapeDtypeStruct((n_heads, t, 1), F32),
                   jax.ShapeDtypeStruct((n_heads, nq), jnp.int32)],
        sem=("arbitrary", "arbitrary"),
        name=name,
        args=(q, k, v),
        side=side,
    )
    return (*outs, side_outs)


def _sb_bwd(q, k, v, lt, first, do, n_heads, *, name):
    t, hd = q.shape
    d = hd // n_heads
    hpt = LANES // d
    tq = _tile(t, SB_QUERIES)
    nq = t // tq
    kpq = tq // SB_KEYS
    scale = _sb_scale(d)
    last = SB_KEYS - 1

    def body(q_ref, k_ref, v_ref, lt_ref, first_ref, do_ref, dq_ref, dk_ref, dv_ref, dk_acc, dv_acc):
        i = pl.program_id(1)

        @pl.when(i == 0)
        def _():
            dk_acc[...] = jnp.zeros_like(dk_acc)
            dv_acc[...] = jnp.zeros_like(dv_acc)

        diff, krow, kcol = _sb_iota(tq)
        upto = (krow <= kcol).astype(F32)
        before = (krow < kcol).astype(F32)
        zero = jnp.zeros((tq, 1), F32)
        nb = i * kpq
        for hh in range(hpt):
            sl = slice(hh * d, (hh + 1) * d)
            qs = (q_ref[:, sl].astype(F32) * scale).astype(BF16)
            do16 = do_ref[:, sl].astype(BF16)
            ltot = lt_ref[hh]

            def block(j, carry, valid, r0=0, qs=qs, do16=do16, ltot=ltot, sl=sl):
                dq, pl_sum, pg_sum = carry
                rows = _key_rows(j)
                kv = k_ref[rows, sl]
                vv = v_ref[rows, sl]
                ls, lg = _sb_logits(qs[r0:], kv, valid)
                pre = _ones_dot(upto, lg, ones_left=False, pieces=SB_PIECES)
                att = jnp.exp(ls + (ltot[r0:] - (pre + pl_sum)))
                if valid is not None:
                    att = jnp.where(valid, att, 0.0)
                g = att * _nt(do16[r0:], vv)
                gpre = _ones_dot(before, g, ones_left=False, pieces=SB_PIECES)
                sig = jnp.exp(ls)
                dz16 = (g - sig * (g + (gpre + pg_sum))).astype(BF16)
                if valid is not None:
                    dz16 = jnp.where(valid, dz16, jnp.zeros_like(dz16))
                dq = dq + _nn(dz16, kv)
                dk_acc[rows, sl] += _tn(dz16, qs[r0:])
                dv_acc[rows, sl] += _tn(att.astype(BF16), do16[r0:])
                return dq, pl_sum + pre[:, last:], pg_sum + (gpre[:, last:] + g[:, last:])

            start = jnp.clip(first_ref[pl.program_id(0) * hpt + hh, i], 0, nb)
            carry = lax.fori_loop(start, nb, lambda j, cr, block=block: block(j, cr, None),
                                  (jnp.zeros((tq, d), F32), zero, zero))
            for m in range(kpq):
                r0 = m * SB_KEYS
                sub = block(nb + m, tuple(a[r0:] for a in carry), diff[r0:] < -r0, r0)
                carry = tuple(jnp.concatenate([a[:r0], s], axis=0) if r0 else s for a, s in zip(carry, sub))
            dq_ref[:, sl] = (carry[0] * scale).astype(dq_ref.dtype)

        @pl.when(i == nq - 1)
        def _():
            dk_ref[...] = dk_acc[...].astype(dk_ref.dtype)
            dv_ref[...] = dv_acc[...].astype(dv_ref.dtype)

    qs = pl.BlockSpec((tq, LANES), lambda p, i: (i, p))
    ls = pl.BlockSpec((hpt, tq, 1), lambda p, i: (p, i, 0))
    ks = pl.BlockSpec((t, LANES), lambda p, i: (0, p))
    full = jax.ShapeDtypeStruct((t, hd), BF16)
    return pl.pallas_call(
        body,
        grid=(n_heads // hpt, nq),
        in_specs=[qs, ks, ks, ls, pl.BlockSpec(memory_space=pltpu.SMEM), qs],
        out_specs=[qs, ks, ks],
        out_shape=[full, full, full],
        scratch_shapes=[pltpu.VMEM((t, LANES), F32), pltpu.VMEM((t, LANES), F32)],
        compiler_params=_cparams(("arbitrary", "arbitrary")),
        name=name,
    )(q, k, v, lt, first, do)


def _row_tile(rows, cols):
    fits = [r for r in range(16, rows + 1, 16) if rows % r == 0 and r * cols * 4 <= ADAM_BLOCK_BYTES]
    return max(fits) if fits else rows


def _sum_leading(x, *, name):
    n, rows, cols = x.shape
    tr = _row_tile(rows, cols)

    def body(x_ref, o_ref):
        acc = x_ref[0].astype(F32)
        for q in range(1, n):
            acc = acc + x_ref[q].astype(F32)
        o_ref[...] = acc

    return pl.pallas_call(
        body,
        grid=(rows // tr,),
        in_specs=[pl.BlockSpec((n, tr, cols), lambda i: (0, i, 0))],
        out_specs=pl.BlockSpec((tr, cols), lambda i: (i, 0)),
        out_shape=jax.ShapeDtypeStruct((rows, cols), F32),
        compiler_params=_cparams(("parallel",)),
        name=name,
    )(x)


def _pair_add(g4h, recv, c, *, out_dtype, name):
    n, _, rows, cols = g4h.shape
    tr = _row_tile(rows, cols)

    def body(c_ref, g_ref, r_ref, o_ref):
        o_ref[...] = (g_ref[...] + r_ref[...]).astype(o_ref.dtype)

    blk = pl.BlockSpec((1, tr, cols), lambda q, i, c_ref: (q, i, 0))
    return pl.pallas_call(
        body,
        grid_spec=pltpu.PrefetchScalarGridSpec(
            num_scalar_prefetch=1,
            grid=(n, rows // tr),
            in_specs=[pl.BlockSpec((1, None, tr, cols), lambda q, i, c_ref: (q, c_ref[0], i, 0)), blk],
            out_specs=blk),
        out_shape=jax.ShapeDtypeStruct((n, rows, cols), out_dtype),
        compiler_params=_cparams(("parallel", "parallel")),
        name=name,
    )(c.reshape(1).astype(jnp.int32), g4h, recv)


ANY = pl.BlockSpec(memory_space=pl.ANY)


def _other_chips(x, y):
    return [(1 - x, y), (x, 1 - y), (1 - x, 1 - y)]


def _gather_chips(shard, *, name):
    def body(x_ref, o_ref, send_sems, recv_sems, local_sem):
        x, y, c = lax.axis_index("x"), lax.axis_index("y"), lax.axis_index("c")
        me = 2 * x + y
        mine = pltpu.make_async_copy(x_ref, o_ref.at[me], local_sem)
        mine.start()
        chips = _other_chips(x, y)
        sends = [pltpu.make_async_remote_copy(src_ref=x_ref, dst_ref=o_ref.at[me], send_sem=send_sems.at[q],
                                              recv_sem=recv_sems.at[q], device_id=(px, py, c), device_id_type=MESH)
                 for q, (px, py) in enumerate(chips)]
        for cp in sends:
            cp.start()
        for q, (px, py) in enumerate(chips):
            pltpu.make_async_remote_copy(src_ref=x_ref, dst_ref=o_ref.at[2 * px + py], send_sem=send_sems.at[q],
                                         recv_sem=recv_sems.at[q], device_id=(px, py, c), device_id_type=MESH).wait_recv()
        for cp in sends:
            cp.wait_send()
        mine.wait()

    return pl.pallas_call(
        body,
        in_specs=[ANY],
        out_specs=ANY,
        out_shape=jax.ShapeDtypeStruct((4,) + shard.shape, shard.dtype),
        scratch_shapes=[pltpu.SemaphoreType.DMA((3,)), pltpu.SemaphoreType.DMA((3,)), pltpu.SemaphoreType.DMA],
        compiler_params=pltpu.CompilerParams(has_side_effects=True),
        name=name,
    )(shard)


def _comm_call(body, ins, out_shapes, n_sems, name):
    n = len(ins)

    def wrapped(*refs):
        body(refs[:n], refs[n:n + len(out_shapes)], refs[-2], refs[-1])

    return pl.pallas_call(
        wrapped,
        in_specs=[ANY] * n,
        out_specs=[ANY] * len(out_shapes),
        out_shape=out_shapes,
        scratch_shapes=[pltpu.SemaphoreType.DMA((n_sems,)), pltpu.SemaphoreType.DMA((n_sems,))],
        compiler_params=pltpu.CompilerParams(has_side_effects=True),
        name=name,
    )(*ins)


def _remote(send_sems, recv_sems, q, src, dst, to):
    return pltpu.make_async_remote_copy(src_ref=src, dst_ref=dst, send_sem=send_sems.at[q], recv_sem=recv_sems.at[q],
                                        device_id=to, device_id_type=MESH)


def _scatter_job(parts):
    def sends(ins, outs, send_sems, recv_sems):
        x, y, c = lax.axis_index("x"), lax.axis_index("y"), lax.axis_index("c")
        return [_remote(send_sems, recv_sems, 3 * i + q, p.at[2 * px + py], o.at[2 * x + y], (px, py, c))
                for i, (p, o) in enumerate(zip(ins, outs)) for q, (px, py) in enumerate(_other_chips(x, y))]

    def start(ins, outs, send_sems, recv_sems):
        for cp in sends(ins, outs, send_sems, recv_sems):
            cp.start()

    def finish(ins, outs, send_sems, recv_sems):
        x, y, c = lax.axis_index("x"), lax.axis_index("y"), lax.axis_index("c")
        for i, (p, o) in enumerate(zip(ins, outs)):
            for q, (px, py) in enumerate(_other_chips(x, y)):
                _remote(send_sems, recv_sems, 3 * i + q, p.at[2 * x + y], o.at[2 * px + py], (px, py, c)).wait_recv()
        for cp in sends(ins, outs, send_sems, recv_sems):
            cp.wait_send()

    return _SideJob(parts, [jax.ShapeDtypeStruct(p.shape, p.dtype) for p in parts], 3 * len(parts), start, finish)


def _run_job(job, name):
    return _comm_call(lambda *refs: (job.start(*refs), job.finish(*refs)), job.ins, job.out_shapes, job.n_sems, name)


def _gather_job(shards):
    def sends(ins, outs, send_sems, recv_sems):
        x, y, c = lax.axis_index("x"), lax.axis_index("y"), lax.axis_index("c")
        return [_remote(send_sems, recv_sems, 6 * i + q, s.at[c], o.at[2 * x + y, c], (px, py, c))
                for i, (s, o) in enumerate(zip(ins, outs)) for q, (px, py) in enumerate(_other_chips(x, y))]

    def start(ins, outs, send_sems, recv_sems):
        for cp in sends(ins, outs, send_sems, recv_sems):
            cp.start()

    def finish(ins, outs, send_sems, recv_sems):
        x, y, c = lax.axis_index("x"), lax.axis_index("y"), lax.axis_index("c")
        sibling = (x, y, 1 - c)
        chips = _other_chips(x, y)
        copy = lambda q, src, dst, to: _remote(send_sems, recv_sems, q, src, dst, to)
        passed = []
        for i, (s, o) in enumerate(zip(ins, outs)):
            for q, (px, py) in enumerate(chips):
                slot = o.at[2 * px + py, c]
                copy(6 * i + q, s.at[c], slot, (px, py, c)).wait_recv()
                passed.append(copy(6 * i + 3 + q, slot, slot, sibling))
                passed[-1].start()
        for i, (s, o) in enumerate(zip(ins, outs)):
            for q, (px, py) in enumerate(chips):
                copy(6 * i + 3 + q, s.at[1 - c], o.at[2 * px + py, 1 - c], sibling).wait_recv()
        for cp in sends(ins, outs, send_sems, recv_sems) + passed:
            cp.wait_send()

    return _SideJob(shards, [jax.ShapeDtypeStruct((N_CHIPS,) + s.shape, s.dtype) for s in shards], 6 * len(shards),
                    start, finish)


def _swap_job(gs):
    def copies(ins, outs, send_sems, recv_sems):
        x, y, c = lax.axis_index("x"), lax.axis_index("y"), lax.axis_index("c")
        return [_remote(send_sems, recv_sems, i, g.at[pl.ds(0, g.shape[0]), 1 - c], o, (x, y, 1 - c))
                for i, (g, o) in enumerate(zip(ins, outs))]

    def start(*refs):
        for cp in copies(*refs):
            cp.start()

    def finish(*refs):
        for cp in copies(*refs):
            cp.wait()

    return _SideJob(gs, [jax.ShapeDtypeStruct((g.shape[0],) + g.shape[2:], g.dtype) for g in gs], len(gs), start, finish)


def _join_halves(halves, *, name):
    def body(ins, outs, send_sems, recv_sems):
        x, y, c = lax.axis_index("x"), lax.axis_index("y"), lax.axis_index("c")
        sibling = (x, y, 1 - c)
        sends = [_remote(send_sems, recv_sems, i, h, o.at[c], sibling) for i, (h, o) in enumerate(zip(ins, outs))]
        for cp in sends:
            cp.start()
        for i, (h, o) in enumerate(zip(ins, outs)):
            _remote(send_sems, recv_sems, i, h, o.at[1 - c], sibling).wait_recv()
        for cp in sends:
            cp.wait_send()

    return _comm_call(body, halves, [jax.ShapeDtypeStruct((2,) + h.shape, h.dtype) for h in halves], len(halves), name)


def _gather_all(v, *, name):
    def body(v_ref, o_ref, send_sems, recv_sems, local_sem):
        x, y, c = lax.axis_index("x"), lax.axis_index("y"), lax.axis_index("c")
        me = 4 * x + 2 * y + c
        mine = pltpu.make_async_copy(v_ref, o_ref.at[me], local_sem)
        mine.start()
        peers = [(x ^ (q >> 2 & 1), y ^ (q >> 1 & 1), c ^ (q & 1)) for q in range(1, 8)]
        sends = [pltpu.make_async_remote_copy(src_ref=v_ref, dst_ref=o_ref.at[me], send_sem=send_sems.at[q],
                                              recv_sem=recv_sems.at[q], device_id=peer, device_id_type=MESH)
                 for q, peer in enumerate(peers)]
        for cp in sends:
            cp.start()
        for q, (px, py, pc) in enumerate(peers):
            pltpu.make_async_remote_copy(src_ref=v_ref, dst_ref=o_ref.at[4 * px + 2 * py + pc], send_sem=send_sems.at[q],
                                         recv_sem=recv_sems.at[q], device_id=(px, py, pc), device_id_type=MESH).wait_recv()
        for cp in sends:
            cp.wait_send()
        mine.wait()

    return pl.pallas_call(
        body,
        in_specs=[ANY],
        out_specs=ANY,
        out_shape=jax.ShapeDtypeStruct((8,) + v.shape, v.dtype),
        scratch_shapes=[pltpu.SemaphoreType.DMA((7,)), pltpu.SemaphoreType.DMA((7,)), pltpu.SemaphoreType.DMA],
        compiler_params=pltpu.CompilerParams(has_side_effects=True),
        name=name,
    )(v)


WEIGHTS = ['ssm_norm_w', 'ssm_in_w', 'ssm_conv_w', 'ssm_conv_b', 'ssm_dt_bias', 'ssm_a_log', 'ssm_d',
           'ssm_gate_norm_w', 'ssm_out_w', 'kv_norm_w', 'w_k', 'w_v', 'attn_norm_w', 'w_q', 'w_o',
           'ffn_norm_w', 'ffn_up_w', 'ffn_conv_w', 'ffn_conv_b', 'ffn_down_w', 'final_norm_w']
SHARD_AXIS = {'ssm_norm_w': 1, 'ssm_in_w': 2, 'ssm_conv_w': 2, 'ssm_conv_b': 1, 'ssm_gate_norm_w': 1,
              'ssm_out_w': 1, 'w_k': 0, 'w_v': 0, 'w_q': 1, 'w_o': 1, 'ffn_up_w': 2, 'ffn_conv_w': 2,
              'ffn_down_w': 1}
BIG = ['ssm_in_w', 'ssm_out_w', 'w_k', 'w_v', 'w_q', 'w_o', 'ffn_up_w', 'ffn_down_w']
SMALL = [n for n in WEIGHTS if n in SHARD_AXIS and n not in BIG]
REPLICATED = [n for n in WEIGHTS if n not in SHARD_AXIS]
STACKED = ['ffn_up_w', 'ffn_down_w']
N_CHIPS = 4


PACK_ROWS = 16


def _piece_rows(n):
    return -(-n // (PACK_ROWS * LANES)) * PACK_ROWS


def _pack(arrs, dtype, row_mult):
    lead = arrs[0].shape[:-1]
    pieces, total = [], 0
    for a in arrs:
        n = a.shape[-1]
        rows = _piece_rows(n)
        a = a.astype(dtype)
        if rows * LANES != n:
            a = jnp.pad(a, [(0, 0)] * len(lead) + [(0, rows * LANES - n)])
        pieces.append(a.reshape(lead + (rows, LANES)))
        total += rows
    extra = -total % row_mult
    if extra:
        pieces.append(jnp.zeros(lead + (extra, LANES), dtype))
    return jnp.concatenate(pieces, axis=len(lead))


def _unpack(buf, shapes):
    lead = buf.shape[:-2]
    out, off = [], 0
    for shp in shapes:
        n = math.prod(shp)
        rows = _piece_rows(n)
        piece = lax.slice_in_dim(buf, off, off + rows, axis=len(lead)).reshape(lead + (rows * LANES,))
        out.append(piece[..., :n].reshape(lead + tuple(shp)))
        off += rows
    return out


def _set_slot(buf, piece, index):
    return lax.dynamic_update_slice_in_dim(buf, piece[None], index, axis=0)


def _from_shards(stacked, axis):
    return jnp.concatenate([stacked[j] for j in range(N_CHIPS)], axis=axis)


def _ffn_fwd(h, norm_w, w_up, conv_w, conv_b, w_down, tag, side=None):
    u = _rmsnorm_fwd(h, norm_w, name=f"ffn{tag}_norm")
    hid = _matmul(u, w_up, name=f"ffn{tag}_up")
    act, side_outs = _conv_glu_fwd(hid, conv_w, conv_b, side=side, name=f"ffn{tag}_glu")
    out = _matmul(act, w_down, add=h, name=f"ffn{tag}_down")
    return out, (u, hid, act), side_outs


def _ffn_bwd(h, saved, dout, norm_w, w_up, conv_w, conv_b, w_down, tag):
    u, hid, act = saved
    dact = _matmul(dout, w_down, tb=True, name=f"ffn{tag}_down_dx")
    dw_down = _matmul(act, dout, ta=True, name=f"ffn{tag}_down_dw")
    dhid, dwg, dwv, dbg, dbv = _conv_glu_bwd(hid, conv_w, conv_b, dact, name=f"ffn{tag}_glu_bwd")
    du = _matmul(dhid, w_up, tb=True, name=f"ffn{tag}_up_dx")
    dw_up = _matmul(u, dhid, ta=True, out_parts=N_CHIPS, name=f"ffn{tag}_up_dw")
    dh, (dnorm,) = _rmsnorm_bwd(h, [(du, norm_w)], dout, name=f"ffn{tag}_norm_bwd")
    return dh, dict(norm=dnorm[0], up=dw_up, conv_w=jnp.concatenate([dwg, dwv], axis=1),
                    conv_b=jnp.concatenate([dbg, dbv], axis=1)[0], down=dw_down)


class _Pieces:
    def __init__(self, local):
        self.c = lax.axis_index("c")
        self.chip = 2 * lax.axis_index("x") + lax.axis_index("y")
        self.shape, self.s16 = {}, {}
        for n in BIG:
            blk = local[n]
            layers = [(n, l, blk[l]) for l in range(blk.shape[0])] if n in STACKED else [(n, None, blk.reshape(blk.shape[-2:]))]
            for name, l, p in layers:
                self.shape[name, l] = p.shape
                self.s16[name, l] = p.astype(BF16).reshape(2, p.shape[0] // 2, p.shape[1])

    def gather_job(self, keys):
        return _gather_job([self.s16[k] for k in keys])

    def weights(self, keys, gathered):
        out = []
        for k, g in zip(keys, gathered):
            r, cc = self.shape[k]
            by_chip = _set_slot(g, self.s16[k], self.chip).reshape(N_CHIPS, r, cc)
            if k[0] == 'ssm_in_w':
                by_chip = by_chip.transpose(1, 0, 2).reshape(r, N_CHIPS * cc)
            elif k[0] != 'ffn_up_w':
                by_chip = by_chip.reshape(N_CHIPS * r, cc)
            out.append(by_chip)
        return out

    def by_halves(self, keys, grads):
        gs = []
        for k, g in zip(keys, grads):
            r, cc = self.shape[k]
            if k[0] == 'ssm_in_w':
                g = g.reshape(r, N_CHIPS, cc).transpose(1, 0, 2)
            gs.append(g.reshape(N_CHIPS, 2, r // 2, cc))
        return gs

    def pair_sums(self, gs, recv, tag):
        return [_pair_add(g, rv, self.c, out_dtype=BF16, name=f"rs_pair_add_{tag}{i}") for i, (g, rv) in enumerate(zip(gs, recv))]

    def chip_sums(self, pairs, scattered, tag):
        return [_sum_leading(_set_slot(s, lax.dynamic_index_in_dim(p, self.chip, axis=0, keepdims=False), self.chip),
                             name=f"rs_chip_sum_{tag}{i}") for i, (s, p) in enumerate(zip(scattered, pairs))]

    def shards(self, keys, halves):
        joined = _join_halves(halves, name="rs_half_join")
        return {k: _set_slot(j, h, self.c).reshape(self.shape[k]) for k, h, j in zip(keys, halves, joined)}


def _step(x, target, w, pieces):
    t = x.shape[0]
    g_n, heads = SSM_GROUPS, SSM_HEADS
    r_h = heads // g_n
    di = D_INNER
    zx_cols = di + CONV_DIM
    k_in = [('ssm_in_w', None)]
    k_ffn0 = [('ssm_out_w', None), ('ffn_up_w', 0), ('ffn_down_w', 0)]
    k_qkv = [('w_k', None), ('w_v', None), ('w_q', None)]
    k_late = [('w_o', None), ('ffn_up_w', 1), ('ffn_down_w', 1)]
    (w_in,) = pieces.weights(k_in, _run_job(pieces.gather_job(k_in), "gather_ssm_in"))
    w_zx = w_in[:, :zx_cols]
    w_dt = jnp.pad(w_in[:, zx_cols:], ((0, 0), (0, LANES - heads)))
    conv_w, conv_b = w['ssm_conv_w'][0], w['ssm_conv_b'][0]
    hp = jnp.stack([w['ssm_dt_bias'][0], w['ssm_a_log'][0], w['ssm_d'][0]], axis=0).reshape(3, g_n, r_h)
    hpc, hpr = hp.transpose(1, 0, 2), hp.transpose(1, 2, 0)

    h0 = x
    u0 = _rmsnorm_fwd(h0, w['ssm_norm_w'][0], name="ssm_norm")
    zx = _matmul(u0, w_zx, name="ssm_in_zx")
    dt_raw = _matmul(u0, w_dt, name="ssm_in_dt")[:, :heads]
    dtg = dt_raw.reshape(t, g_n, r_h)
    dtc, dtr = dtg.transpose(1, 0, 2), dtg.transpose(1, 2, 0)
    xbc = _conv_silu_fwd(zx, conv_w, conv_b, x_off=di, name="ssm_conv")
    y, prev, got = _ssd_fwd(xbc, dtc, dtr, hpc, hpr, side=pieces.gather_job(k_ffn0), name="ssd_fwd")
    w_out, w_up0, w_down0 = pieces.weights(k_ffn0, got)
    yn = _gate_norm_fwd(y, zx, w['ssm_gate_norm_w'][0], name="ssm_gate_norm")
    h1 = _matmul(yn, w_out, add=h0, name="ssm_out")
    h2, ffn0, got = _ffn_fwd(h1, w['ffn_norm_w'][0], w_up0, w['ffn_conv_w'][0], w['ffn_conv_b'][0], w_down0, 0,
                             side=pieces.gather_job(k_qkv))
    w_k, w_v, w_q = pieces.weights(k_qkv, got)
    hk = _rmsnorm_fwd(h2, w['kv_norm_w'], name="kv_norm")
    qn = _rmsnorm_fwd(h2, w['attn_norm_w'][0], name="attn_norm")
    k2 = _matmul(hk, w_k, out_dtype=BF16, name="attn_k")
    v2 = _matmul(hk, w_v, out_dtype=BF16, name="attn_v")
    q2 = _matmul(qn, w_q, out_dtype=BF16, name="attn_q")
    o2, lt, first, got = _sb_fwd(q2, k2, v2, SB_HEADS, side=pieces.gather_job(k_late), name="sb_fwd")
    w_o, w_up1, w_down1 = pieces.weights(k_late, got)
    h3 =_matmul(o2, w_o, add=h2, name="attn_o")
    h4, ffn1, _ = _ffn_fwd(h3, w['ffn_norm_w'][1], w_up1, w['ffn_conv_w'][1], w['ffn_conv_b'][1], w_down1, 1)
    loss_p, dh4, d_final = _loss_head(h4, w['final_norm_w'], target, name="loss_head")

    dh3, g1 = _ffn_bwd(h3, ffn1, dh4, w['ffn_norm_w'][1], w_up1, w['ffn_conv_w'][1], w['ffn_conv_b'][1], w_down1, 1)
    do2 = _matmul(dh3, w_o, tb=True, out_dtype=BF16, name="attn_o_dx")
    dw_o = _matmul(o2, dh3, ta=True, name="attn_o_dw")
    dq2, dk2, dv2 = _sb_bwd(q2, k2, v2, lt, first, do2, SB_HEADS, name="sb_bwd")
    dqn = _matmul(dq2, w_q, tb=True, name="attn_q_dx")
    dw_q = _matmul(qn, dq2, ta=True, name="attn_q_dw")
    dhk = _matmul(dk2, w_k, tb=True, name="attn_k_dx")
    dhk = _matmul(dv2, w_v, tb=True, add=dhk, name="attn_v_dx")
    dw_k = _matmul(hk, dk2, ta=True, name="attn_k_dw")
    dw_v = _matmul(hk, dv2, ta=True, name="attn_v_dw")
    dh2, (d_attn_norm, d_kv_norm) = _rmsnorm_bwd(h2, [(dqn, w['attn_norm_w'][0]), (dhk, w['kv_norm_w'])], dh3,
                                                 name="attn_norms_bwd")
    dh1, g0 = _ffn_bwd(h1, ffn0, dh2, w['ffn_norm_w'][0], w_up0, w['ffn_conv_w'][0], w['ffn_conv_b'][0], w_down0, 0)
    dyn = _matmul(dh1, w_out, tb=True, name="ssm_out_dx")
    dw_out = _matmul(yn, dh1, ta=True, name="ssm_out_dw")
    k_done = k_qkv + k_late + k_ffn0
    gs_done = pieces.by_halves(k_done, [dw_k, dw_v, dw_q, dw_o, g1['up'], g1['down'], dw_out, g0['up'], g0['down']])
    dy, dz, d_gate, recv = _gate_norm_bwd(y, zx, w['ssm_gate_norm_w'][0], dyn, side=_swap_job(gs_done),
                                          name="ssm_gate_norm_bwd")
    pairs_done = pieces.pair_sums(gs_done, recv, "a")
    dxs, dbm, dcm, ddt_g, hg, scattered_done = _ssd_bwd(xbc, dtc, dtr, hpc, hpr, prev, dy,
                                                        side=_scatter_job(pairs_done), name="ssd_bwd")
    dzx, d_conv_w, d_conv_b = _conv_silu_bwd(zx, conv_w, conv_b, [dxs, dbm, dcm], x_off=di, into=dz, name="ssm_conv_bwd")
    ddt = jnp.pad(ddt_g.transpose(1, 0, 2).reshape(t, heads), ((0, 0), (0, LANES - heads)))
    du0 = _matmul(dzx, w_zx, tb=True, name="ssm_in_zx_dx")
    du0 = _matmul(ddt, w_dt, tb=True, add=du0, name="ssm_in_dt_dx")
    dw_in = jnp.concatenate([_matmul(u0, dzx, ta=True, name="ssm_in_zx_dw"),
                             _matmul(u0, ddt, ta=True, name="ssm_in_dt_dw")[:, :heads]], axis=1)
    gs_in = pieces.by_halves(k_in, [dw_in])
    pairs_in = pieces.pair_sums(gs_in, _run_job(_swap_job(gs_in), "rs_pair_swap_b"), "b")
    dx, (d_ssm_norm,), scattered_in = _rmsnorm_bwd(h0, [(du0, w['ssm_norm_w'][0])], dh1, side=_scatter_job(pairs_in),
                                                   name="ssm_norm_bwd")
    halves = pieces.chip_sums(pairs_done, scattered_done, "a") + pieces.chip_sums(pairs_in, scattered_in, "b")
    big_grads = pieces.shards(k_done + k_in, halves)

    hgr = hg.transpose(1, 0, 2).reshape(3, heads)
    grads = {
        'ssm_norm_w': d_ssm_norm, 'ssm_conv_w': d_conv_w[None], 'ssm_conv_b': d_conv_b,
        'ssm_dt_bias': hgr[0:1], 'ssm_a_log': hgr[1:2], 'ssm_d': hgr[2:3], 'ssm_gate_norm_w': d_gate,
        'kv_norm_w': d_kv_norm[0], 'attn_norm_w': d_attn_norm, 'ffn_norm_w': jnp.stack([g0['norm'], g1['norm']]),
        'ffn_conv_w': jnp.stack([g0['conv_w'], g1['conv_w']]), 'ffn_conv_b': jnp.stack([g0['conv_b'], g1['conv_b']]),
        'final_norm_w': d_final[0],
    }
    return loss_p, dx, grads, big_grads


def kernel(x, ssm_norm_w, ssm_in_w, ssm_conv_w, ssm_conv_b, ssm_dt_bias, ssm_a_log, ssm_d, ssm_gate_norm_w, ssm_out_w, kv_norm_w, w_k, w_v, attn_norm_w, w_q, w_o, ffn_norm_w, ffn_up_w, ffn_conv_w, ffn_conv_b, ffn_down_w, final_norm_w, loss_target, m_ssm_norm_w, m_ssm_in_w, m_ssm_conv_w, m_ssm_conv_b, m_ssm_dt_bias, m_ssm_a_log, m_ssm_d, m_ssm_gate_norm_w, m_ssm_out_w, m_kv_norm_w, m_w_k, m_w_v, m_attn_norm_w, m_w_q, m_w_o, m_ffn_norm_w, m_ffn_up_w, m_ffn_conv_w, m_ffn_conv_b, m_ffn_down_w, m_final_norm_w, v_ssm_norm_w, v_ssm_in_w, v_ssm_conv_w, v_ssm_conv_b, v_ssm_dt_bias, v_ssm_a_log, v_ssm_d, v_ssm_gate_norm_w, v_ssm_out_w, v_kv_norm_w, v_w_k, v_w_v, v_attn_norm_w, v_w_q, v_w_o, v_ffn_norm_w, v_ffn_up_w, v_ffn_conv_w, v_ffn_conv_b, v_ffn_down_w, v_final_norm_w):
    args = (ssm_norm_w, ssm_in_w, ssm_conv_w, ssm_conv_b, ssm_dt_bias, ssm_a_log, ssm_d, ssm_gate_norm_w, ssm_out_w, kv_norm_w, w_k, w_v, attn_norm_w, w_q, w_o, ffn_norm_w, ffn_up_w, ffn_conv_w, ffn_conv_b, ffn_down_w, final_norm_w)
    moms = (m_ssm_norm_w, m_ssm_in_w, m_ssm_conv_w, m_ssm_conv_b, m_ssm_dt_bias, m_ssm_a_log, m_ssm_d, m_ssm_gate_norm_w, m_ssm_out_w, m_kv_norm_w, m_w_k, m_w_v, m_attn_norm_w, m_w_q, m_w_o, m_ffn_norm_w, m_ffn_up_w, m_ffn_conv_w, m_ffn_conv_b, m_ffn_down_w, m_final_norm_w)
    vels = (v_ssm_norm_w, v_ssm_in_w, v_ssm_conv_w, v_ssm_conv_b, v_ssm_dt_bias, v_ssm_a_log, v_ssm_d, v_ssm_gate_norm_w, v_ssm_out_w, v_kv_norm_w, v_w_k, v_w_v, v_attn_norm_w, v_w_q, v_w_o, v_ffn_norm_w, v_ffn_up_w, v_ffn_conv_w, v_ffn_conv_b, v_ffn_down_w, v_final_norm_w)
    local = dict(zip(WEIGHTS, args))
    m_in = dict(zip(WEIGHTS, moms))
    v_in = dict(zip(WEIGHTS, vels))
    chip = 2 * lax.axis_index("x") + lax.axis_index("y")

    full = {n: local[n] for n in REPLICATED}
    small32 = _gather_chips(_pack([local[n].reshape(-1) for n in SMALL], F32, 8), name="gather_small")
    for n, st in zip(SMALL, _unpack(small32, [local[n].shape for n in SMALL])):
        full[n] = _from_shards(st, SHARD_AXIS[n])

    pieces = _Pieces(local)
    loss_p, dx, grads, big_grads = _step(x[0], loss_target[0], full, pieces)
    gshard = {}
    for n in BIG:
        if n in STACKED:
            gshard[n] = [big_grads[n, l] for l in range(local[n].shape[0])]
        else:
            gshard[n] = big_grads[n, None].reshape(local[n].shape)

    small = SMALL + REPLICATED
    rep = _pack([loss_p.reshape(-1)] + [grads[n].reshape(-1) for n in small], F32, 8)
    tot = _sum_leading(_gather_all(rep, name="ar_gather"), name="ar_sum")
    parts = _unpack(tot, [(LANES,)] + [grads[n].shape for n in small])
    loss = jnp.sum(parts[0])
    for n, g in zip(small, parts[1:]):
        if n in SHARD_AXIS:
            size = local[n].shape[SHARD_AXIS[n]]
            g = lax.dynamic_slice_in_dim(g, chip * size, size, axis=SHARD_AXIS[n])
        gshard[n] = g

    grads_out, deltas, new_m, new_v = [], [], [], []
    for n in WEIGHTS:
        if n in STACKED:
            g, d, nm, nv = _adamw_layers(local[n], gshard[n], m_in[n], v_in[n], name=f"adamw_{n}")
        else:
            g = gshard[n]
            d, nm, nv = _adamw(local[n], g, m_in[n], v_in[n], name=f"adamw_{n}")
        grads_out.append(g)
        deltas.append(d)
        new_m.append(nm)
        new_v.append(nv)
    return (loss, dx[None], *grads_out, *deltas, *new_m, *new_v)
```

```python
import functools
import math

import jax
import jax.numpy as jnp
from jax import lax
from jax.experimental import pallas as pl
from jax.experimental.pallas import tpu as pltpu

D_INNER = 2048
SSM_HEAD_DIM = 64
SSM_HEADS = 32
SSM_GROUPS = 4
SSM_STATE = 128
SSM_CHUNK = 128
GN = SSM_GROUPS * SSM_STATE
CONV_DIM = D_INNER + 2 * GN
SB_HEADS = 16
EPS = 1e-6
ADAM_LR = 0.001
ADAM_B1 = 0.9
ADAM_B2 = 0.999
ADAM_EPS = 1e-08
ADAM_WD = 0.01
ADAM_STEP = 10

LANES = 128
SUBLANES = 8
VMEM_LIMIT = 48 * 1024 * 1024
ADAM_BLOCK_BYTES = 1 << 20
F32 = jnp.float32
BF16 = jnp.bfloat16
MESH = pl.DeviceIdType.MESH


def _cparams(sem=None):
    return pltpu.CompilerParams(dimension_semantics=sem, vmem_limit_bytes=VMEM_LIMIT)


class _SideJob:
    def __init__(self, ins, out_shapes, n_sems, start, finish):
        self.ins, self.out_shapes, self.n_sems, self.start, self.finish = ins, out_shapes, n_sems, start, finish


def _call(body, *, grid, in_specs, out_specs, out_shape, scratch_shapes=(), sem, name, args, side=None):
    in_specs, out_specs, out_shape, scratch_shapes = list(in_specs), list(out_specs), list(out_shape), list(scratch_shapes)
    n_in, n_out = len(in_specs), len(out_specs)
    if side is None:
        outs = pl.pallas_call(body, grid=grid, in_specs=in_specs, out_specs=out_specs, out_shape=out_shape,
                              scratch_shapes=scratch_shapes, compiler_params=_cparams(sem), name=name)(*args)
        return list(outs), []
    k_in, k_out = len(side.ins), len(side.out_shapes)

    def wrapped(*refs):
        ins, s_ins = refs[:n_in], refs[n_in:n_in + k_in]
        o0 = n_in + k_in
        outs, s_outs = refs[o0:o0 + n_out], refs[o0 + n_out:o0 + n_out + k_out]
        scratch, send_sems, recv_sems = refs[o0 + n_out + k_out:-2], refs[-2], refs[-1]
        ids = [pl.program_id(a) for a in range(len(grid))]
        first = functools.reduce(jnp.logical_and, [p == 0 for p in ids])
        last = functools.reduce(jnp.logical_and, [p == g - 1 for p, g in zip(ids, grid)])

        @pl.when(first)
        def _():
            side.start(s_ins, s_outs, send_sems, recv_sems)

        body(*ins, *outs, *scratch)

        @pl.when(last)
        def _():
            side.finish(s_ins, s_outs, send_sems, recv_sems)

    outs = pl.pallas_call(
        wrapped, grid=grid, in_specs=in_specs + [ANY] * k_in, out_specs=out_specs + [ANY] * k_out,
        out_shape=out_shape + list(side.out_shapes),
        scratch_shapes=scratch_shapes + [pltpu.SemaphoreType.DMA((side.n_sems,)), pltpu.SemaphoreType.DMA((side.n_sems,))],
        compiler_params=_cparams(tuple("arbitrary" for _ in grid)), name=name)(*args, *side.ins)
    return list(outs[:n_out]), list(outs[n_out:])


def _tile(n, cands):
    for c in cands:
        if n % c == 0:
            return c
    return n


def _nt(a, b):
    return lax.dot_general(a, b, (((1,), (1,)), ((), ())), preferred_element_type=F32)


def _tn(a, b):
    return lax.dot_general(a, b, (((0,), (0,)), ((), ())), preferred_element_type=F32)


def _nn(a, b):
    return jnp.dot(a, b, preferred_element_type=F32)


def _split(x, pieces):
    out = []
    for _ in range(pieces - 1):
        h = x.astype(BF16)
        out.append(h)
        x = x - h.astype(F32)
    out.append(x.astype(BF16))
    return out


def _ones_dot(ones, x, *, ones_left, pieces=3):
    o16 = ones.astype(BF16)
    acc = None
    for piece in _split(x, pieces):
        term = _nn(o16, piece) if ones_left else _nn(piece, o16)
        acc = term if acc is None else acc + term
    return acc


def _row_sums(x, pieces=2):
    return _ones_dot(jnp.ones((x.shape[1], LANES), F32), x, ones_left=False, pieces=pieces)


def _softplus(x):
    return jnp.maximum(x, 0.0) + jnp.log(1.0 + jnp.exp(-jnp.abs(x)))


def _sigmoid(x):
    return 0.5 * jnp.tanh(0.5 * x) + 0.5


MM_TILE_MAX = 1408
MM_VMEM_BUDGET = 40 * 1024 * 1024


def _divisors(n, cap):
    out = [d for d in range(min(cap, n) // LANES * LANES, 0, -LANES) if n % d == 0]
    return out or [n]


def _mm_tiles(m, n, k, a_bytes, b_bytes, o_bytes, add_bytes):
    best = None
    for tm in _divisors(m, MM_TILE_MAX):
        for tn in _divisors(n, MM_TILE_MAX):
            for tk in _divisors(k, MM_TILE_MAX):
                vmem = 2 * (tm * tk * a_bytes + tk * tn * b_bytes + tm * tn * (o_bytes + add_bytes)) + tm * tn * 4
                if vmem > MM_VMEM_BUDGET:
                    continue
                score = (tm * tn * tk, tm * tn)
                if best is None or score > best[0]:
                    best = (score, (tm, tn, tk))
    return best[1]


def _matmul(a, b, *, ta=False, tb=False, add=None, out_dtype=F32, out_parts=1, name):
    a_parts = a.shape[0] if a.ndim == 3 else 1
    b_parts = b.shape[0] if b.ndim == 3 else 1
    assert not (ta and a_parts > 1)
    a2, b2 = a.shape[-2:], b.shape[-2:]
    m, k = (a2[1], a2[0]) if ta else (a2[0], a2[1] * a_parts)
    n, kb = (b2[0], b2[1] * b_parts) if tb else (b2[1] * b_parts, b2[0])
    assert kb == k, (a.shape, b.shape)
    n_unit = math.gcd(n // out_parts, n if tb else b2[1])
    k_unit = math.gcd(k // a_parts, b2[1] if tb else k)
    tm, tn, tk = _mm_tiles(m, n_unit, k_unit, a.dtype.itemsize, b.dtype.itemsize, jnp.dtype(out_dtype).itemsize,
                           0 if add is None else add.dtype.itemsize)
    nk = k // tk
    ka, kbp = (k // a_parts) // tk, (k // b_parts) // tk
    nb, no = (n // b_parts) // tn, (n // out_parts) // tn

    def body(*refs):
        if add is None:
            a_ref, b_ref, o_ref = refs[:3]
            add_ref = None
        else:
            a_ref, b_ref, add_ref, o_ref = refs[:4]
        kk = pl.program_id(2)
        dn = (((0 if ta else 1,), (1 if tb else 0,)), ((), ()))
        prod = lax.dot_general(a_ref[...].astype(BF16), b_ref[...].astype(BF16), dn, preferred_element_type=F32)

        def finish(r):
            if add_ref is not None:
                r = r + add_ref[...].astype(F32)
            o_ref[...] = r.astype(o_ref.dtype)

        if nk == 1:
            finish(prod)
            return
        acc_ref = refs[-1]

        @pl.when(kk == 0)
        def _():
            acc_ref[...] = prod

        @pl.when(jnp.logical_and(kk > 0, kk < nk - 1))
        def _():
            acc_ref[...] += prod

        @pl.when(kk == nk - 1)
        def _():
            finish(acc_ref[...] + prod)

    if ta:
        a_spec = pl.BlockSpec((tk, tm), lambda i, j, kk: (kk, i))
    elif a_parts > 1:
        a_spec = pl.BlockSpec((None, tm, tk), lambda i, j, kk: (kk // ka, i, kk % ka))
    else:
        a_spec = pl.BlockSpec((tm, tk), lambda i, j, kk: (i, kk))
    if b_parts == 1:
        b_spec = pl.BlockSpec((tn, tk), lambda i, j, kk: (j, kk)) if tb else pl.BlockSpec((tk, tn), lambda i, j, kk: (kk, j))
    elif tb:
        b_spec = pl.BlockSpec((None, tn, tk), lambda i, j, kk: (kk // kbp, j, kk % kbp))
    else:
        b_spec = pl.BlockSpec((None, tk, tn), lambda i, j, kk: (j // nb, kk, j % nb))
    if out_parts > 1:
        o_spec = pl.BlockSpec((None, tm, tn), lambda i, j, kk: (j // no, i, j % no))
        o_shape = jax.ShapeDtypeStruct((out_parts, m, n // out_parts), out_dtype)
    else:
        o_spec = pl.BlockSpec((tm, tn), lambda i, j, kk: (i, j))
        o_shape = jax.ShapeDtypeStruct((m, n), out_dtype)
    in_specs = [a_spec, b_spec]
    args = [a, b]
    if add is not None:
        in_specs.append(pl.BlockSpec((tm, tn), lambda i, j, kk: (i, j)))
        args.append(add)
    return pl.pallas_call(
        body,
        grid=(m // tm, n // tn, nk),
        in_specs=in_specs,
        out_specs=o_spec,
        out_shape=o_shape,
        scratch_shapes=[pltpu.VMEM((tm, tn), F32)] if nk > 1 else [],
        compiler_params=_cparams(("parallel", "parallel", "arbitrary")),
        name=name,
    )(*args)


def _rmsnorm_fwd(x, w, *, name):
    t, d = x.shape
    tb = _tile(t, (512, 256, 128))

    def body(x_ref, w_ref, o_ref):
        xv = x_ref[...]
        r = lax.rsqrt(jnp.mean(xv * xv, axis=-1, keepdims=True) + EPS)
        o_ref[...] = (xv * r * w_ref[...]).astype(o_ref.dtype)

    return pl.pallas_call(
        body,
        grid=(t // tb,),
        in_specs=[pl.BlockSpec((tb, d), lambda i: (i, 0)), pl.BlockSpec((1, d), lambda i: (0, 0))],
        out_specs=pl.BlockSpec((tb, d), lambda i: (i, 0)),
        out_shape=jax.ShapeDtypeStruct((t, d), BF16),
        compiler_params=_cparams(("parallel",)),
        name=name,
    )(x, w.reshape(1, d))


def _rmsnorm_bwd(x, dys, dres, *, name):
    t, d = x.shape
    tb = _tile(t, (256, 128))
    nn = len(dys)
    has_res = dres is not None

    def body(*refs):
        x_ref = refs[0]
        dy_refs = refs[1:1 + nn]
        w_refs = refs[1 + nn:1 + 2 * nn]
        pos = 1 + 2 * nn
        res_ref = refs[pos] if has_res else None
        pos += 1 if has_res else 0
        dx_ref = refs[pos]
        dw_refs = refs[pos + 1:pos + 1 + nn]
        i = pl.program_id(0)
        xv = x_ref[...]
        r = lax.rsqrt(jnp.mean(xv * xv, axis=-1, keepdims=True) + EPS)
        xn = xv * r
        dx = res_ref[...] if has_res else jnp.zeros_like(xv)
        for q in range(nn):
            dy = dy_refs[q][...].astype(F32)
            g = dy * w_refs[q][...]
            dx = dx + r * (g - xn * jnp.mean(g * xn, axis=-1, keepdims=True))
            dwp = jnp.sum(dy * xn, axis=0, keepdims=True)

            @pl.when(i == 0)
            def _(q=q, dwp=dwp):
                dw_refs[q][...] = dwp

            @pl.when(i > 0)
            def _(q=q, dwp=dwp):
                dw_refs[q][...] += dwp
        dx_ref[...] = dx

    row = pl.BlockSpec((tb, d), lambda i: (i, 0))
    vec = pl.BlockSpec((1, d), lambda i: (0, 0))
    in_specs = [row] + [row] * nn + [vec] * nn + ([row] if has_res else [])
    args = [x] + [p[0] for p in dys] + [p[1].reshape(1, d) for p in dys] + ([dres] if has_res else [])
    outs = pl.pallas_call(
        body,
        grid=(t // tb,),
        in_specs=in_specs,
        out_specs=[row] + [vec] * nn,
        out_shape=[jax.ShapeDtypeStruct((t, d), F32)] + [jax.ShapeDtypeStruct((1, d), F32)] * nn,
        compiler_params=_cparams(("arbitrary",)),
        name=name,
    )(*args)
    return outs[0], list(outs[1:])


def _loss_head(x, w, target, *, name):
    t, d = x.shape
    tb = _tile(t, (256, 128))

    def body(x_ref, w_ref, t_ref, loss_ref, dx_ref, dw_ref):
        i = pl.program_id(0)
        xv = x_ref[...]
        wv = w_ref[...]
        r = lax.rsqrt(jnp.mean(xv * xv, axis=-1, keepdims=True) + EPS)
        xn = xv * r
        e = xn * wv - t_ref[...]
        lp = 0.5 * jnp.sum(jnp.mean(e * e, axis=-1, keepdims=True), axis=0, keepdims=True)
        dy = e * (1.0 / d)
        g = dy * wv
        dx_ref[...] = r * (g - xn * jnp.mean(g * xn, axis=-1, keepdims=True))
        dwp = jnp.sum(dy * xn, axis=0, keepdims=True)
        lpv = jnp.broadcast_to(lp, (1, LANES)) * (1.0 / LANES)

        @pl.when(i == 0)
        def _():
            dw_ref[...] = dwp
            loss_ref[...] = lpv

        @pl.when(i > 0)
        def _():
            dw_ref[...] += dwp
            loss_ref[...] += lpv

    row = pl.BlockSpec((tb, d), lambda i: (i, 0))
    vec = pl.BlockSpec((1, d), lambda i: (0, 0))
    return pl.pallas_call(
        body,
        grid=(t // tb,),
        in_specs=[row, vec, row],
        out_specs=[pl.BlockSpec((1, LANES), lambda i: (0, 0)), row, vec],
        out_shape=[jax.ShapeDtypeStruct((1, LANES), F32), jax.ShapeDtypeStruct((t, d), F32),
                   jax.ShapeDtypeStruct((1, d), F32)],
        compiler_params=_cparams(("arbitrary",)),
        name=name,
    )(x, w.reshape(1, d), target)


ROW_CHUNK = 64
PAD = SUBLANES


def _shifted(pad_ref, r0, rows, back):
    return pad_ref[pl.ds(PAD + r0 - back, rows), :]


def _conv_taps(pad_ref, w_ref, r0, rows, kw):
    acc = None
    for j in range(kw):
        term = _shifted(pad_ref, r0, rows, kw - 1 - j) * w_ref[j:j + 1, :]
        acc = term if acc is None else acc + term
    return acc


def _fill_pad(pad_ref, x_ref, t):
    pad_ref[0:PAD, :] = jnp.zeros((PAD, pad_ref.shape[1]), F32)
    pad_ref[pl.ds(PAD + t, PAD), :] = jnp.zeros((PAD, pad_ref.shape[1]), F32)
    pad_ref[pl.ds(PAD, t), :] = x_ref[...].astype(F32)


def _conv_silu_fwd(x, w, b, *, x_off=0, name):
    t = x.shape[0]
    kw, c = w.shape
    cw = _tile(math.gcd(c, x_off) if x_off else c, (256, 128))
    ob = x_off // cw
    rc = _tile(t, (ROW_CHUNK,))

    def body(x_ref, w_ref, b_ref, o_ref, pad_ref):
        _fill_pad(pad_ref, x_ref, t)
        for r0 in range(0, t, rc):
            pre = _conv_taps(pad_ref, w_ref, r0, rc, kw) + b_ref[...]
            o_ref[pl.ds(r0, rc), :] = pre * _sigmoid(pre)

    strip = pl.BlockSpec((t, cw), lambda i: (0, i))
    return pl.pallas_call(
        body,
        grid=(c // cw,),
        in_specs=[pl.BlockSpec((t, cw), lambda i: (0, i + ob)), pl.BlockSpec((kw, cw), lambda i: (0, i)),
                  pl.BlockSpec((1, cw), lambda i: (0, i))],
        out_specs=strip,
        out_shape=jax.ShapeDtypeStruct((t, c), F32),
        scratch_shapes=[pltpu.VMEM((t + 2 * PAD, cw), F32)],
        compiler_params=_cparams(("parallel",)),
        name=name,
    )(x, w, b.reshape(1, c))


def _conv_bwd_core(dpre_pad_ref, x_pad_ref, w_ref, dx_ref, dw_ref, db_ref, t, rc, kw):
    cw = dx_ref.shape[1]

    def fold(a):
        return jnp.sum(a.reshape(rc // SUBLANES, SUBLANES, cw), axis=0) if rc % SUBLANES == 0 else jnp.sum(a, axis=0, keepdims=True)

    dws = [None] * kw
    dbs = None
    for r0 in range(0, t, rc):
        dpre = dpre_pad_ref[pl.ds(PAD + r0, rc), :]
        dx = None
        for j in range(kw):
            s = kw - 1 - j
            term = dpre_pad_ref[pl.ds(PAD + r0 + s, rc), :] * w_ref[j:j + 1, :]
            dx = term if dx is None else dx + term
            part = fold(dpre * _shifted(x_pad_ref, r0, rc, s))
            dws[j] = part if dws[j] is None else dws[j] + part
        part = fold(dpre)
        dbs = part if dbs is None else dbs + part
        dx_ref[pl.ds(r0, rc), :] = dx
    for j in range(kw):
        dw_ref[j:j + 1, :] = jnp.sum(dws[j], axis=0, keepdims=True)
    db_ref[...] = jnp.sum(dbs, axis=0, keepdims=True)


def _conv_silu_bwd(x, w, b, dact, *, x_off=0, into=None, name):
    t = x.shape[0]
    kw, c = w.shape
    parts = dact if isinstance(dact, (list, tuple)) else [dact]
    widths = [p.shape[1] for p in parts]
    assert sum(widths) == c
    cw = _tile(functools.reduce(math.gcd, widths + [x_off or c]), (256, 128) if len(parts) == 1 else (128,))
    ob = x_off // cw
    rc = _tile(t, (ROW_CHUNK,))
    firsts = [sum(widths[:p]) // cw for p in range(len(parts))]
    counts = [wd // cw for wd in widths]
    n_p = len(parts)

    def body(x_ref, w_ref, b_ref, *rest):
        da_refs = rest[:n_p]
        dx_ref, dw_ref, db_ref, xpad_ref, dpad_ref = rest[-5 - (n_p > 1):][:5]
        if n_p > 1:
            da_ref = rest[-1]
            i = pl.program_id(0)
            for p in range(n_p):
                @pl.when(jnp.logical_and(i >= firsts[p], i < firsts[p] + counts[p]))
                def _(p=p):
                    da_ref[...] = da_refs[p][...]
        else:
            da_ref = da_refs[0]
        _fill_pad(xpad_ref, x_ref, t)
        dpad_ref[0:PAD, :] = jnp.zeros((PAD, cw), F32)
        dpad_ref[pl.ds(PAD + t, PAD), :] = jnp.zeros((PAD, cw), F32)
        for r0 in range(0, t, rc):
            pre = _conv_taps(xpad_ref, w_ref, r0, rc, kw) + b_ref[...]
            sg = _sigmoid(pre)
            dpad_ref[pl.ds(PAD + r0, rc), :] = da_ref[pl.ds(r0, rc), :] * (sg * (1.0 + pre * (1.0 - sg)))
        _conv_bwd_core(dpad_ref, xpad_ref, w_ref, dx_ref, dw_ref, db_ref, t, rc, kw)

    strip = pl.BlockSpec((t, cw), lambda i: (0, i))
    wspec = pl.BlockSpec((kw, cw), lambda i: (0, i))
    bspec = pl.BlockSpec((1, cw), lambda i: (0, i))
    xspec = pl.BlockSpec((t, cw), lambda i: (0, i + ob))
    dspecs = [pl.BlockSpec((t, cw), lambda i, f=f, n=n: (0, jnp.clip(i - f, 0, n - 1))) for f, n in zip(firsts, counts)]
    extra = {} if into is None else dict(input_output_aliases={3 + n_p: 0})
    pad = pltpu.VMEM((t + 2 * PAD, cw), F32)
    return pl.pallas_call(
        body,
        grid=(c // cw,),
        in_specs=[xspec, wspec, bspec] + dspecs + ([] if into is None else [ANY]),
        out_specs=[strip if into is None else xspec, wspec, bspec],
        out_shape=[jax.ShapeDtypeStruct((t, c) if into is None else into.shape, F32), jax.ShapeDtypeStruct((kw, c), F32),
                   jax.ShapeDtypeStruct((1, c), F32)],
        scratch_shapes=[pad, pad] + ([pltpu.VMEM((t, cw), F32)] if n_p > 1 else []),
        compiler_params=_cparams(("arbitrary",)),
        name=name,
        **extra,
    )(x, w, b.reshape(1, c), *parts, *([] if into is None else [into]))


def _conv_glu_fwd(hid, w, b, *, side=None, name):
    t, c2 = hid.shape
    f = c2 // 2
    kw = w.shape[0]
    cw = _tile(f, (256, 128))
    nf = f // cw
    rc = _tile(t, (ROW_CHUNK,))

    def body(g_ref, v_ref, wg_ref, wv_ref, bg_ref, bv_ref, o_ref, gpad_ref, vpad_ref):
        _fill_pad(gpad_ref, g_ref, t)
        _fill_pad(vpad_ref, v_ref, t)
        for r0 in range(0, t, rc):
            gate = _conv_taps(gpad_ref, wg_ref, r0, rc, kw) + bg_ref[...]
            val = _conv_taps(vpad_ref, wv_ref, r0, rc, kw) + bv_ref[...]
            o_ref[pl.ds(r0, rc), :] = (gate * _sigmoid(gate) * val).astype(o_ref.dtype)

    gs = pl.BlockSpec((t, cw), lambda i: (0, i))
    vs = pl.BlockSpec((t, cw), lambda i: (0, i + nf))
    b2 = b.reshape(1, c2)
    (act,), side_outs = _call(
        body,
        grid=(nf,),
        in_specs=[gs, vs, pl.BlockSpec((kw, cw), lambda i: (0, i)), pl.BlockSpec((kw, cw), lambda i: (0, i + nf)),
                  pl.BlockSpec((1, cw), lambda i: (0, i)), pl.BlockSpec((1, cw), lambda i: (0, i + nf))],
        out_specs=[gs],
        out_shape=[jax.ShapeDtypeStruct((t, f), BF16)],
        scratch_shapes=[pltpu.VMEM((t + 2 * PAD, cw), F32), pltpu.VMEM((t + 2 * PAD, cw), F32)],
        sem=("parallel",),
        name=name,
        args=(hid, hid, w, w, b2, b2),
        side=side,
    )
    return act, side_outs


def _conv_glu_bwd(hid, w, b, dact, *, side=None, name):
    t, c2 = hid.shape
    f = c2 // 2
    kw = w.shape[0]
    cw = _tile(f, (128,))
    nf = f // cw
    rc = _tile(t, (ROW_CHUNK,))

    def body(g_ref, v_ref, wg_ref, wv_ref, bg_ref, bv_ref, da_ref,
             dgv_ref, dwg_ref, dwv_ref, dbg_ref, dbv_ref,
             gpad_ref, vpad_ref, dgpad_ref, dvpad_ref):
        _fill_pad(gpad_ref, g_ref, t)
        _fill_pad(vpad_ref, v_ref, t)
        for ref in (dgpad_ref, dvpad_ref):
            ref[0:PAD, :] = jnp.zeros((PAD, cw), F32)
            ref[pl.ds(PAD + t, PAD), :] = jnp.zeros((PAD, cw), F32)
        for r0 in range(0, t, rc):
            gate = _conv_taps(gpad_ref, wg_ref, r0, rc, kw) + bg_ref[...]
            val = _conv_taps(vpad_ref, wv_ref, r0, rc, kw) + bv_ref[...]
            sg = _sigmoid(gate)
            da = da_ref[pl.ds(r0, rc), :].astype(F32)
            dgpad_ref[pl.ds(PAD + r0, rc), :] = da * val * (sg * (1.0 + gate * (1.0 - sg)))
            dvpad_ref[pl.ds(PAD + r0, rc), :] = da * (gate * sg)
        _conv_bwd_core(dgpad_ref, gpad_ref, wg_ref, dgv_ref.at[0], dwg_ref, dbg_ref, t, rc, kw)
        _conv_bwd_core(dvpad_ref, vpad_ref, wv_ref, dgv_ref.at[1], dwv_ref, dbv_ref, t, rc, kw)

    gs = pl.BlockSpec((t, cw), lambda i: (0, i))
    vs = pl.BlockSpec((t, cw), lambda i: (0, i + nf))
    wg = pl.BlockSpec((kw, cw), lambda i: (0, i))
    wv = pl.BlockSpec((kw, cw), lambda i: (0, i + nf))
    bg = pl.BlockSpec((1, cw), lambda i: (0, i))
    bv = pl.BlockSpec((1, cw), lambda i: (0, i + nf))
    b2 = b.reshape(1, c2)
    pad = pltpu.VMEM((t + 2 * PAD, cw), F32)
    outs, side_outs = _call(
        body,
        grid=(nf,),
        in_specs=[gs, vs, wg, wv, bg, bv, gs],
        out_specs=[pl.BlockSpec((2, t, cw), lambda i: (0, 0, i)), wg, wg, bg, bg],
        out_shape=[jax.ShapeDtypeStruct((2, t, f), F32),
                   jax.ShapeDtypeStruct((kw, f), F32), jax.ShapeDtypeStruct((kw, f), F32),
                   jax.ShapeDtypeStruct((1, f), F32), jax.ShapeDtypeStruct((1, f), F32)],
        scratch_shapes=[pad, pad, pad, pad],
        sem=("parallel",),
        name=name,
        args=(hid, hid, w, w, b2, b2, dact),
        side=side,
    )
    return (*outs, side_outs)


def _gate_norm_fwd(y, zx, w, *, name):
    t, di = y.shape
    gsz = di // SSM_GROUPS
    tb = _tile(t, (256, 128))

    def body(y_ref, z_ref, w_ref, o_ref):
        for g in range(SSM_GROUPS):
            sl = slice(g * gsz, (g + 1) * gsz)
            zv = z_ref[:, sl]
            gv = y_ref[:, sl] * (zv * _sigmoid(zv))
            r = lax.rsqrt(jnp.mean(gv * gv, axis=-1, keepdims=True) + EPS)
            o_ref[:, sl] = (gv * r * w_ref[:, sl]).astype(o_ref.dtype)

    row = pl.BlockSpec((tb, di), lambda i: (i, 0))
    return pl.pallas_call(
        body,
        grid=(t // tb,),
        in_specs=[row, row, pl.BlockSpec((1, di), lambda i: (0, 0))],
        out_specs=row,
        out_shape=jax.ShapeDtypeStruct((t, di), BF16),
        compiler_params=_cparams(("parallel",)),
        name=name,
    )(y, zx, w.reshape(1, di))


def _gate_norm_bwd(y, zx, w, dyn, *, side=None, name):
    t, di = y.shape
    gsz = di // SSM_GROUPS
    tb = _tile(t, (256, 128))

    def body(y_ref, z_ref, w_ref, d_ref, dy_ref, dz_ref, dw_ref):
        i = pl.program_id(0)
        for g in range(SSM_GROUPS):
            sl = slice(g * gsz, (g + 1) * gsz)
            zv = z_ref[:, sl]
            yv = y_ref[:, sl]
            sg = _sigmoid(zv)
            sz = zv * sg
            gv = yv * sz
            r = lax.rsqrt(jnp.mean(gv * gv, axis=-1, keepdims=True) + EPS)
            gn = gv * r
            dn = d_ref[:, sl].astype(F32)
            q = dn * w_ref[:, sl]
            dg = r * (q - gn * jnp.mean(q * gn, axis=-1, keepdims=True))
            dy_ref[:, sl] = dg * sz
            dz_ref[:, sl] = dg * yv * (sg * (1.0 + zv * (1.0 - sg)))
            dwp = jnp.sum(dn * gn, axis=0, keepdims=True)

            @pl.when(i == 0)
            def _(sl=sl, dwp=dwp):
                dw_ref[:, sl] = dwp

            @pl.when(i > 0)
            def _(sl=sl, dwp=dwp):
                dw_ref[:, sl] += dwp

    row = pl.BlockSpec((tb, di), lambda i: (i, 0))
    vec = pl.BlockSpec((1, di), lambda i: (0, 0))
    outs, side_outs = _call(
        body,
        grid=(t // tb,),
        in_specs=[row, row, vec, row],
        out_specs=[row, row, vec],
        out_shape=[jax.ShapeDtypeStruct((t, di), F32), jax.ShapeDtypeStruct((t, zx.shape[1]), F32),
                   jax.ShapeDtypeStruct((1, di), F32)],
        sem=("arbitrary",),
        name=name,
        args=(y, zx, w.reshape(1, di), dyn),
        side=side,
    )
    return (*outs, side_outs)


def _adamw(w, g, m, v, *, name):
    shape = w.shape
    cols = shape[-1]
    rows = w.size // cols
    w2, g2, m2, v2 = (a.reshape(rows, cols) for a in (w, g, m, v))
    tr = rows if rows * cols * 4 <= ADAM_BLOCK_BYTES else _row_tile(rows, cols)
    c1 = 1.0 - ADAM_B1 ** ADAM_STEP
    c2 = 1.0 - ADAM_B2 ** ADAM_STEP

    def body(w_ref, g_ref, m_ref, v_ref, d_ref, nm_ref, nv_ref):
        gv = g_ref[...]
        nm = ADAM_B1 * m_ref[...] + (1.0 - ADAM_B1) * gv
        nv = ADAM_B2 * v_ref[...] + (1.0 - ADAM_B2) * (gv * gv)
        d_ref[...] = -ADAM_LR * ((nm / c1) / (jnp.sqrt(nv / c2) + ADAM_EPS) + ADAM_WD * w_ref[...])
        nm_ref[...] = nm
        nv_ref[...] = nv

    blk = pl.BlockSpec((tr, cols), lambda i: (i, 0))
    outs = pl.pallas_call(
        body,
        grid=(rows // tr,),
        in_specs=[blk] * 4,
        out_specs=[blk] * 3,
        out_shape=[jax.ShapeDtypeStruct((rows, cols), F32)] * 3,
        compiler_params=_cparams(("parallel",)),
        name=name,
    )(w2, g2, m2, v2)
    return tuple(o.reshape(shape) for o in outs)


def _adamw_layers(w, gs, m, v, *, name):
    n_l, rows, cols = w.shape
    assert len(gs) == n_l
    tr = _row_tile(rows, cols)
    c1 = 1.0 - ADAM_B1 ** ADAM_STEP
    c2 = 1.0 - ADAM_B2 ** ADAM_STEP

    def body(*refs):
        w_ref, m_ref, v_ref = refs[:3]
        g_refs = refs[3:3 + n_l]
        g_ref, d_ref, nm_ref, nv_ref = refs[3 + n_l:]
        layer = pl.program_id(0)
        gv = g_refs[0][...]
        for q in range(1, n_l):
            gv = jnp.where(layer == q, g_refs[q][...], gv)
        nm = ADAM_B1 * m_ref[...] + (1.0 - ADAM_B1) * gv
        nv = ADAM_B2 * v_ref[...] + (1.0 - ADAM_B2) * (gv * gv)
        g_ref[...] = gv
        d_ref[...] = -ADAM_LR * ((nm / c1) / (jnp.sqrt(nv / c2) + ADAM_EPS) + ADAM_WD * w_ref[...])
        nm_ref[...] = nm
        nv_ref[...] = nv

    stacked = pl.BlockSpec((None, tr, cols), lambda l, i: (l, i, 0))
    single = pl.BlockSpec((tr, cols), lambda l, i: (i, 0))
    return pl.pallas_call(
        body,
        grid=(n_l, rows // tr),
        in_specs=[stacked] * 3 + [single] * n_l,
        out_specs=[stacked] * 4,
        out_shape=[jax.ShapeDtypeStruct(w.shape, F32)] * 4,
        compiler_params=_cparams(("parallel", "parallel")),
        name=name,
    )(w, m, v, *gs)


def _ssd_scalars(dtc_ref, dtr_ref, hpc_ref, hpr_ref, ln):
    assert SSM_CHUNK == SSM_STATE == LANES, "the SSD kernels mix chunk, state and lane-wide tiles freely"
    bias_c, alog_c = hpc_ref[0, 0:1, :], hpc_ref[0, 1:2, :]
    bias_r, alog_r = hpr_ref[0, :, 0:1], hpr_ref[0, :, 1:2]
    a_c, a_r = -jnp.exp(alog_c), -jnp.exp(alog_r)
    raw_c = dtc_ref[0] + bias_c
    dt_c = _softplus(raw_c)
    dt_r = _softplus(dtr_ref[0] + bias_r)
    row = lax.broadcasted_iota(jnp.int32, (ln, ln), 0)
    col = lax.broadcasted_iota(jnp.int32, (ln, ln), 1)
    lower = (col <= row).astype(F32)
    upper = (row <= col).astype(F32)
    acs_c = _ones_dot(lower, dt_c * a_c, ones_left=True)
    acs_r = _ones_dot(upper, dt_r * a_r, ones_left=False)
    return raw_c, dt_c, a_c, acs_c, acs_r, row, col


def _ssd_specs(t, di, g_n, n_st, rp, ln, r_h, rev):
    nc = t // ln
    cidx = (lambda c: nc - 1 - c) if rev else (lambda c: c)
    xs = pl.BlockSpec((ln, rp), lambda g, c: (cidx(c), g))
    bm = pl.BlockSpec((ln, n_st), lambda g, c: (cidx(c), di // n_st + g))
    cm = pl.BlockSpec((ln, n_st), lambda g, c: (cidx(c), di // n_st + g_n + g))
    dtc = pl.BlockSpec((1, ln, r_h), lambda g, c: (g, cidx(c), 0))
    dtr = pl.BlockSpec((1, r_h, ln), lambda g, c: (g, 0, cidx(c)))
    hpc = pl.BlockSpec((1, 3, r_h), lambda g, c: (g, 0, 0))
    hpr = pl.BlockSpec((1, r_h, 3), lambda g, c: (g, 0, 0))
    prev = pl.BlockSpec((1, rp, n_st), lambda g, c: (cidx(c), g, 0))
    return xs, bm, cm, dtc, dtr, hpc, hpr, prev


def _ssd_fwd(xbc, dtc, dtr, hpc, hpr, *, side=None, name):
    t = xbc.shape[0]
    di, g_n, n_st, p_h, ln = D_INNER, SSM_GROUPS, SSM_STATE, SSM_HEAD_DIM, SSM_CHUNK
    r_h = SSM_HEADS // g_n
    rp = r_h * p_h
    nc = t // ln

    def body(xs_ref, b_ref, c_ref, dtc_ref, dtr_ref, hpc_ref, hpr_ref, y_ref, prev_ref, st_ref):
        @pl.when(pl.program_id(1) == 0)
        def _():
            st_ref[...] = jnp.zeros_like(st_ref)

        _, dt_c, _, acs_c, acs_r, row, col = _ssd_scalars(dtc_ref, dtr_ref, hpc_ref, hpr_ref, ln)
        bm = b_ref[...]
        cm = c_ref[...]
        cm16 = cm.astype(BF16)
        cb = _nt(cm16, bm.astype(BF16))
        causal = row >= col
        for r in range(r_h):
            sl = slice(r * p_h, (r + 1) * p_h)
            xs = xs_ref[:, sl]
            acs = jnp.broadcast_to(acs_c[:, r:r + 1], (ln, ln))
            last = acs[ln - 1:ln, :]
            lm = jnp.where(causal, jnp.exp(acs - acs_r[r:r + 1, :]), 0.0)
            xd = (xs * jnp.broadcast_to(dt_c[:, r:r + 1], (ln, p_h))).astype(BF16)
            prev = st_ref[sl, :]
            y = _nn((cb * lm).astype(BF16), xd)
            y = y + _nt(cm16, prev.astype(BF16)) * jnp.exp(acs[:, :p_h])
            y_ref[:, sl] = y + hpc_ref[0, 2:3, r:r + 1] * xs
            prev_ref[0, sl, :] = prev
            bd = (bm * jnp.exp(last - acs[:, :n_st])).astype(BF16)
            st_ref[sl, :] = prev * jnp.exp(last[:, :n_st]) + _tn(xd, bd)

    xs, bm, cm, dtcs, dtrs, hpcs, hprs, prev = _ssd_specs(t, di, g_n, n_st, rp, ln, r_h, False)
    (y, prev_out), side_outs = _call(
        body,
        grid=(g_n, nc),
        in_specs=[xs, bm, cm, dtcs, dtrs, hpcs, hprs],
        out_specs=[xs, prev],
        out_shape=[jax.ShapeDtypeStruct((t, di), F32), jax.ShapeDtypeStruct((nc, g_n * rp, n_st), F32)],
        scratch_shapes=[pltpu.VMEM((rp, n_st), F32)],
        sem=("parallel", "arbitrary"),
        name=name,
        args=(xbc, xbc, xbc, dtc, dtr, hpc, hpr),
        side=side,
    )
    return y, prev_out, side_outs


def _ssd_bwd(xbc, dtc, dtr, hpc, hpr, prev, dy, *, side=None, name):
    t = xbc.shape[0]
    di, g_n, n_st, p_h, ln = D_INNER, SSM_GROUPS, SSM_STATE, SSM_HEAD_DIM, SSM_CHUNK
    r_h = SSM_HEADS // g_n
    rp = r_h * p_h
    nc = t // ln

    def body(xs_ref, b_ref, c_ref, dtc_ref, dtr_ref, hpc_ref, hpr_ref, prev_ref, dy_ref,
             dxs_ref, db_ref, dc_ref, ddt_ref, hg_ref, ds_ref):
        step = pl.program_id(1)

        @pl.when(step == 0)
        def _():
            ds_ref[...] = jnp.zeros_like(ds_ref)

        raw_c, dt_c, a_c, acs_c, acs_r, row, col = _ssd_scalars(dtc_ref, dtr_ref, hpc_ref, hpr_ref, ln)
        bm = b_ref[...]
        cm = c_ref[...]
        bm16, cm16 = bm.astype(BF16), cm.astype(BF16)
        cb = _nt(cm16, bm16)
        cbt = _nt(bm16, cm16)
        lane_r = lax.broadcasted_iota(jnp.int32, (ln, r_h), 1)
        dacs_all = jnp.zeros((ln, r_h), F32)
        ddtx_all = jnp.zeros((ln, r_h), F32)
        dd_all = jnp.zeros((ln, r_h), F32)
        dcb = jnp.zeros((ln, ln), F32)
        dcbt = jnp.zeros((ln, ln), F32)
        dc_acc = jnp.zeros((ln, n_st), F32)
        db_acc = jnp.zeros((ln, n_st), F32)
        for r in range(r_h):
            sl = slice(r * p_h, (r + 1) * p_h)
            xs = xs_ref[:, sl]
            dyv = dy_ref[:, sl]
            dy16 = dyv.astype(BF16)
            acs = jnp.broadcast_to(acs_c[:, r:r + 1], (ln, ln))
            dtv = jnp.broadcast_to(dt_c[:, r:r + 1], (ln, p_h))
            acsr = acs_r[r:r + 1, :]
            last = acs[ln - 1:ln, :]
            xd = xs * dtv
            xd16 = xd.astype(BF16)
            lm = jnp.where(row >= col, jnp.exp(acs - acsr), 0.0)
            lmt = jnp.where(col >= row, jnp.exp(acsr - acs), 0.0)
            m_ls = cb * lm
            m_sl = cbt * lmt
            dm = _nt(dy16, xd16)
            dmt = _nt(xd16, dy16)
            dxd = _nn(m_sl.astype(BF16), dy16)
            dacs = _row_sums(dm * m_ls - dmt * m_sl)
            dcb = dcb + dm * lm
            dcbt = dcbt + dmt * lmt
            prev = prev_ref[0, sl, :]
            prev16 = prev.astype(BF16)
            e = jnp.exp(acs[:, :p_h])
            y_off = _nt(cm16, prev16) * e
            dacs = dacs + _row_sums(dyv * y_off)
            dyo16 = (dyv * e).astype(BF16)
            dc_acc = dc_acc + _nn(dyo16, prev16)
            dprev = _tn(dyo16, cm16)
            ds = ds_ref[sl, :]
            ds16 = ds.astype(BF16)
            decay = jnp.exp(last - acs)[:, :n_st]
            bd16 = (bm * decay).astype(BF16)
            dbd = _nn(xd16, ds16)
            dxd = dxd + _nt(bd16, ds16)
            db_acc = db_acc + dbd * decay
            tdec = _row_sums(dbd * bm) * decay
            cd = jnp.exp(last)
            dlast = (jnp.sum(tdec, axis=0, keepdims=True)
                     + jnp.sum(_row_sums(prev * ds), axis=0, keepdims=True) * cd)
            ds_ref[sl, :] = dprev + cd[:, :n_st] * ds
            dskip = hpc_ref[0, 2:3, r:r + 1]
            dxs_ref[:, sl] = dxd * dtv + dskip * dyv
            dacs = dacs - tdec + jnp.where(row == ln - 1, dlast, 0.0)
            dacs_all = jnp.where(lane_r == r, dacs[:, :r_h], dacs_all)
            ddtx_all = jnp.where(lane_r == r, _row_sums(dxd * xs)[:, :r_h], ddtx_all)
            dd_all = jnp.where(lane_r == r, _row_sums(dyv * xs)[:, :r_h], dd_all)
        dc_ref[...] = dc_acc + _nn(dcb.astype(BF16), bm16)
        db_ref[...] = db_acc + _nn(dcbt.astype(BF16), cm16)
        upper = (row <= col).astype(F32)
        dad = _ones_dot(upper, dacs_all, ones_left=True)
        ddt = dad * a_c + ddtx_all
        ddt_raw = ddt * _sigmoid(raw_c)
        ddt_ref[0] = ddt_raw
        d_bias = jnp.sum(ddt_raw, axis=0, keepdims=True)
        d_alog = jnp.sum(dad * dt_c, axis=0, keepdims=True) * a_c
        d_d = jnp.sum(dd_all, axis=0, keepdims=True)
        hg = jnp.concatenate([d_bias, d_alog, d_d], axis=0)

        @pl.when(step == 0)
        def _():
            hg_ref[0] = hg

        @pl.when(step > 0)
        def _():
            hg_ref[0] += hg

    xs, bms, cms, dtcs, dtrs, hpcs, hprs, prevs = _ssd_specs(t, di, g_n, n_st, rp, ln, r_h, True)
    bout = pl.BlockSpec((ln, n_st), lambda g, c: (nc - 1 - c, g))
    outs, side_outs = _call(
        body,
        grid=(g_n, nc),
        in_specs=[xs, bms, cms, dtcs, dtrs, hpcs, hprs, prevs, xs],
        out_specs=[xs, bout, bout, dtcs, hpcs],
        out_shape=[jax.ShapeDtypeStruct((t, di), F32), jax.ShapeDtypeStruct((t, g_n * n_st), F32),
                   jax.ShapeDtypeStruct((t, g_n * n_st), F32), jax.ShapeDtypeStruct((g_n, t, r_h), F32),
                   jax.ShapeDtypeStruct((g_n, 3, r_h), F32)],
        scratch_shapes=[pltpu.VMEM((rp, n_st), F32)],
        sem=("parallel", "arbitrary"),
        name=name,
        args=(xbc, xbc, xbc, dtc, dtr, hpc, hpr, prev, dy),
        side=side,
    )
    return (*outs, side_outs)


SB_KEYS = 256
SB_QUERIES = (512, 256)
SB_CUTOFF = 110.0
SB_PIECES = 2


def _sb_logits(qs, kv, valid):
    z = _nt(qs, kv)
    nz = -z
    lg = jnp.minimum(nz, 0.0) - jnp.log(1.0 + jnp.exp(jnp.minimum(z, nz)))
    return z + lg, (lg if valid is None else jnp.where(valid, lg, 0.0))


def _sb_iota(tq):
    diff = lax.broadcasted_iota(jnp.int32, (tq, SB_KEYS), 1) - lax.broadcasted_iota(jnp.int32, (tq, SB_KEYS), 0)
    krow = lax.broadcasted_iota(jnp.int32, (SB_KEYS, SB_KEYS), 0)
    kcol = lax.broadcasted_iota(jnp.int32, (SB_KEYS, SB_KEYS), 1)
    return diff, krow, kcol


def _sb_scale(d):
    scale = 1.0 / math.sqrt(d)
    assert math.frexp(scale)[0] == 0.5, "the scale is folded into bf16 queries: it must be a power of two"
    return scale


def _key_rows(j):
    return pl.ds(pl.multiple_of(j * SB_KEYS, SB_KEYS), SB_KEYS)


def _sb_fwd(q, k, v, n_heads, *, side=None, name):
    t, hd = q.shape
    d = hd // n_heads
    hpt = LANES // d
    assert hpt * d == LANES and n_heads % hpt == 0
    tq = _tile(t, SB_QUERIES)
    nq = t // tq
    kpq = tq // SB_KEYS
    scale = _sb_scale(d)

    def body(q_ref, k_ref, v_ref, o_ref, lt_ref, first_ref):
        i = pl.program_id(1)
        diff, krow, kcol = _sb_iota(tq)
        later = (krow > kcol).astype(F32)
        nb = i * kpq
        for hh in range(hpt):
            sl = slice(hh * d, (hh + 1) * d)
            qs = (q_ref[:, sl].astype(F32) * scale).astype(BF16)

            def block(j, carry, valid, qs=qs, sl=sl):
                acc, cl = carry
                rows = _key_rows(j)
                ls, lg = _sb_logits(qs, k_ref[rows, sl], valid)
                cs = _ones_dot(later, lg, ones_left=False, pieces=SB_PIECES)
                att = jnp.exp(ls + (cs + cl))
                if valid is not None:
                    att = jnp.where(valid, att, 0.0)
                acc = acc + _nn(att.astype(BF16), v_ref[rows, sl])
                return acc, cl + (cs[:, 0:1] + lg[:, 0:1])

            carry = (jnp.zeros((tq, d), F32), jnp.zeros((tq, 1), F32))
            for m in range(kpq - 1, -1, -1):
                carry = block(i * kpq + m, carry, diff < -m * SB_KEYS)

            def more(st):
                s, _, cl = st
                return jnp.logical_and(s < nb, jnp.max(cl) > -SB_CUTOFF)

            def step(st, block=block):
                s, acc, cl = st
                acc, cl = block(nb - 1 - s, (acc, cl), None)
                return s + 1, acc, cl

            walked, acc, cl = lax.while_loop(more, step, (jnp.int32(0),) + carry)
            o_ref[:, sl] = acc.astype(o_ref.dtype)
            lt_ref[hh] = cl
            first_ref[pl.program_id(0) * hpt + hh, i] = nb - walked

    qs = pl.BlockSpec((tq, LANES), lambda p, i: (i, p))
    ls = pl.BlockSpec((hpt, tq, 1), lambda p, i: (p, i, 0))
    ks = pl.BlockSpec((t, LANES), lambda p, i: (0, p))
    outs, side_outs = _call(
        body,
        grid=(n_heads // hpt, nq),
        in_specs=[qs, ks, ks],
        out_specs=[qs, ls, pl.BlockSpec(memory_space=pltpu.SMEM)],
        out_shape=[jax.ShapeDtypeStruct((t, hd), BF16), jax.ShapeDtypeStruct((n_heads, t, 1), F32),
                   jax.ShapeDtypeStruct((n_heads, nq), jnp.int32)],
        sem=("arbitrary", "arbitrary"),
        name=name,
        args=(q, k, v),
        side=side,
    )
    return (*outs, side_outs)


def _sb_bwd(q, k, v, lt, first, do, n_heads, *, name):
    t, hd = q.shape
    d = hd // n_heads
    hpt = LANES // d
    tq = _tile(t, SB_QUERIES)
    nq = t // tq
    kpq = tq // SB_KEYS
    scale = _sb_scale(d)
    last = SB_KEYS - 1

    def body(q_ref, k_ref, v_ref, lt_ref, first_ref, do_ref, dq_ref, dk_ref, dv_ref, dk_acc, dv_acc):
        i = pl.program_id(1)

        @pl.when(i == 0)
        def _():
            dk_acc[...] = jnp.zeros_like(dk_acc)
            dv_acc[...] = jnp.zeros_like(dv_acc)

        diff, krow, kcol = _sb_iota(tq)
        upto = (krow <= kcol).astype(F32)
        before = (krow < kcol).astype(F32)
        zero = jnp.zeros((tq, 1), F32)
        nb = i * kpq
        for hh in range(hpt):
            sl = slice(hh * d, (hh + 1) * d)
            qs = (q_ref[:, sl].astype(F32) * scale).astype(BF16)
            do16 = do_ref[:, sl].astype(BF16)
            ltot = lt_ref[hh]

            def block(j, carry, valid, r0=0, qs=qs, do16=do16, ltot=ltot, sl=sl):
                dq, pl_sum, pg_sum = carry
                rows = _key_rows(j)
                kv = k_ref[rows, sl]
                vv = v_ref[rows, sl]
                ls, lg = _sb_logits(qs[r0:], kv, valid)
                pre = _ones_dot(upto, lg, ones_left=False, pieces=SB_PIECES)
                att = jnp.exp(ls + (ltot[r0:] - (pre + pl_sum)))
                if valid is not None:
                    att = jnp.where(valid, att, 0.0)
                g = att * _nt(do16[r0:], vv)
                gpre = _ones_dot(before, g, ones_left=False, pieces=SB_PIECES)
                sig = jnp.exp(ls)
                dz16 = (g - sig * (g + (gpre + pg_sum))).astype(BF16)
                if valid is not None:
                    dz16 = jnp.where(valid, dz16, jnp.zeros_like(dz16))
                dq = dq + _nn(dz16, kv)
                dk_acc[rows, sl] += _tn(dz16, qs[r0:])
                dv_acc[rows, sl] += _tn(att.astype(BF16), do16[r0:])
                return dq, pl_sum + pre[:, last:], pg_sum + (gpre[:, last:] + g[:, last:])

            start = jnp.clip(first_ref[pl.program_id(0) * hpt + hh, i], 0, nb)
            carry = lax.fori_loop(start, nb, lambda j, cr, block=block: block(j, cr, None),
                                  (jnp.zeros((tq, d), F32), zero, zero))
            for m in range(kpq):
                r0 = m * SB_KEYS
                sub = block(nb + m, tuple(a[r0:] for a in carry), diff[r0:] < -r0, r0)
                carry = tuple(jnp.concatenate([a[:r0], s], axis=0) if r0 else s for a, s in zip(carry, sub))
            dq_ref[:, sl] = (carry[0] * scale).astype(dq_ref.dtype)

        @pl.when(i == nq - 1)
        def _():
            dk_ref[...] = dk_acc[...].astype(dk_ref.dtype)
            dv_ref[...] = dv_acc[...].astype(dv_ref.dtype)

    qs = pl.BlockSpec((tq, LANES), lambda p, i: (i, p))
    ls = pl.BlockSpec((hpt, tq, 1), lambda p, i: (p, i, 0))
    ks = pl.BlockSpec((t, LANES), lambda p, i: (0, p))
    full = jax.ShapeDtypeStruct((t, hd), BF16)
    return pl.pallas_call(
        body,
        grid=(n_heads // hpt, nq),
        in_specs=[qs, ks, ks, ls, pl.BlockSpec(memory_space=pltpu.SMEM), qs],
        out_specs=[qs, ks, ks],
        out_shape=[full, full, full],
        scratch_shapes=[pltpu.VMEM((t, LANES), F32), pltpu.VMEM((t, LANES), F32)],
        compiler_params=_cparams(("arbitrary", "arbitrary")),
        name=name,
    )(q, k, v, lt, first, do)


def _row_tile(rows, cols):
    fits = [r for r in range(16, rows + 1, 16) if rows % r == 0 and r * cols * 4 <= ADAM_BLOCK_BYTES]
    return max(fits) if fits else rows


def _sum_leading(x, *, name):
    n, rows, cols = x.shape
    tr = _row_tile(rows, cols)

    def body(x_ref, o_ref):
        acc = x_ref[0].astype(F32)
        for q in range(1, n):
            acc = acc + x_ref[q].astype(F32)
        o_ref[...] = acc

    return pl.pallas_call(
        body,
        grid=(rows // tr,),
        in_specs=[pl.BlockSpec((n, tr, cols), lambda i: (0, i, 0))],
        out_specs=pl.BlockSpec((tr, cols), lambda i: (i, 0)),
        out_shape=jax.ShapeDtypeStruct((rows, cols), F32),
        compiler_params=_cparams(("parallel",)),
        name=name,
    )(x)


def _pair_add(g4h, recv, c, *, out_dtype, name):
    n, _, rows, cols = g4h.shape
    tr = _row_tile(rows, cols)

    def body(c_ref, g_ref, r_ref, o_ref):
        o_ref[...] = (g_ref[...] + r_ref[...]).astype(o_ref.dtype)

    blk = pl.BlockSpec((1, tr, cols), lambda q, i, c_ref: (q, i, 0))
    return pl.pallas_call(
        body,
        grid_spec=pltpu.PrefetchScalarGridSpec(
            num_scalar_prefetch=1,
            grid=(n, rows // tr),
            in_specs=[pl.BlockSpec((1, None, tr, cols), lambda q, i, c_ref: (q, c_ref[0], i, 0)), blk],
            out_specs=blk),
        out_shape=jax.ShapeDtypeStruct((n, rows, cols), out_dtype),
        compiler_params=_cparams(("parallel", "parallel")),
        name=name,
    )(c.reshape(1).astype(jnp.int32), g4h, recv)


ANY = pl.BlockSpec(memory_space=pl.ANY)


def _other_chips(x, y):
    return [(1 - x, y), (x, 1 - y), (1 - x, 1 - y)]


def _gather_chips(shard, *, name):
    def body(x_ref, o_ref, send_sems, recv_sems, local_sem):
        x, y, c = lax.axis_index("x"), lax.axis_index("y"), lax.axis_index("c")
        me = 2 * x + y
        mine = pltpu.make_async_copy(x_ref, o_ref.at[me], local_sem)
        mine.start()
        chips = _other_chips(x, y)
        sends = [pltpu.make_async_remote_copy(src_ref=x_ref, dst_ref=o_ref.at[me], send_sem=send_sems.at[q],
                                              recv_sem=recv_sems.at[q], device_id=(px, py, c), device_id_type=MESH)
                 for q, (px, py) in enumerate(chips)]
        for cp in sends:
            cp.start()
        for q, (px, py) in enumerate(chips):
            pltpu.make_async_remote_copy(src_ref=x_ref, dst_ref=o_ref.at[2 * px + py], send_sem=send_sems.at[q],
                                         recv_sem=recv_sems.at[q], device_id=(px, py, c), device_id_type=MESH).wait_recv()
        for cp in sends:
            cp.wait_send()
        mine.wait()

    return pl.pallas_call(
        body,
        in_specs=[ANY],
        out_specs=ANY,
        out_shape=jax.ShapeDtypeStruct((4,) + shard.shape, shard.dtype),
        scratch_shapes=[pltpu.SemaphoreType.DMA((3,)), pltpu.SemaphoreType.DMA((3,)), pltpu.SemaphoreType.DMA],
        compiler_params=pltpu.CompilerParams(has_side_effects=True),
        name=name,
    )(shard)


def _comm_call(body, ins, out_shapes, n_sems, name):
    n = len(ins)

    def wrapped(*refs):
        body(refs[:n], refs[n:n + len(out_shapes)], refs[-2], refs[-1])

    return pl.pallas_call(
        wrapped,
        in_specs=[ANY] * n,
        out_specs=[ANY] * len(out_shapes),
        out_shape=out_shapes,
        scratch_shapes=[pltpu.SemaphoreType.DMA((n_sems,)), pltpu.SemaphoreType.DMA((n_sems,))],
        compiler_params=pltpu.CompilerParams(has_side_effects=True),
        name=name,
    )(*ins)


def _remote(send_sems, recv_sems, q, src, dst, to):
    return pltpu.make_async_remote_copy(src_ref=src, dst_ref=dst, send_sem=send_sems.at[q], recv_sem=recv_sems.at[q],
                                        device_id=to, device_id_type=MESH)


def _scatter_job(parts):
    def sends(ins, outs, send_sems, recv_sems):
        x, y, c = lax.axis_index("x"), lax.axis_index("y"), lax.axis_index("c")
        return [_remote(send_sems, recv_sems, 3 * i + q, p.at[2 * px + py], o.at[2 * x + y], (px, py, c))
                for i, (p, o) in enumerate(zip(ins, outs)) for q, (px, py) in enumerate(_other_chips(x, y))]

    def start(ins, outs, send_sems, recv_sems):
        for cp in sends(ins, outs, send_sems, recv_sems):
            cp.start()

    def finish(ins, outs, send_sems, recv_sems):
        x, y, c = lax.axis_index("x"), lax.axis_index("y"), lax.axis_index("c")
        for i, (p, o) in enumerate(zip(ins, outs)):
            for q, (px, py) in enumerate(_other_chips(x, y)):
                _remote(send_sems, recv_sems, 3 * i + q, p.at[2 * x + y], o.at[2 * px + py], (px, py, c)).wait_recv()
        for cp in sends(ins, outs, send_sems, recv_sems):
            cp.wait_send()

    return _SideJob(parts, [jax.ShapeDtypeStruct(p.shape, p.dtype) for p in parts], 3 * len(parts), start, finish)


def _run_job(job, name):
    return _comm_call(lambda *refs: (job.start(*refs), job.finish(*refs)), job.ins, job.out_shapes, job.n_sems, name)


def _gather_job(shards):
    def sends(ins, outs, send_sems, recv_sems):
        x, y, c = lax.axis_index("x"), lax.axis_index("y"), lax.axis_index("c")
        return [_remote(send_sems, recv_sems, 6 * i + q, s.at[c], o.at[2 * x + y, c], (px, py, c))
                for i, (s, o) in enumerate(zip(ins, outs)) for q, (px, py) in enumerate(_other_chips(x, y))]

    def start(ins, outs, send_sems, recv_sems):
        for cp in sends(ins, outs, send_sems, recv_sems):
            cp.start()

    def finish(ins, outs, send_sems, recv_sems):
        x, y, c = lax.axis_index("x"), lax.axis_index("y"), lax.axis_index("c")
        sibling = (x, y, 1 - c)
        chips = _other_chips(x, y)
        copy = lambda q, src, dst, to: _remote(send_sems, recv_sems, q, src, dst, to)
        passed = []
        for i, (s, o) in enumerate(zip(ins, outs)):
            for q, (px, py) in enumerate(chips):
                slot = o.at[2 * px + py, c]
                copy(6 * i + q, s.at[c], slot, (px, py, c)).wait_recv()
                passed.append(copy(6 * i + 3 + q, slot, slot, sibling))
                passed[-1].start()
        for i, (s, o) in enumerate(zip(ins, outs)):
            for q, (px, py) in enumerate(chips):
                copy(6 * i + 3 + q, s.at[1 - c], o.at[2 * px + py, 1 - c], sibling).wait_recv()
        for cp in sends(ins, outs, send_sems, recv_sems) + passed:
            cp.wait_send()

    return _SideJob(shards, [jax.ShapeDtypeStruct((N_CHIPS,) + s.shape, s.dtype) for s in shards], 6 * len(shards),
                    start, finish)


def _swap_job(gs):
    def copies(ins, outs, send_sems, recv_sems):
        x, y, c = lax.axis_index("x"), lax.axis_index("y"), lax.axis_index("c")
        return [_remote(send_sems, recv_sems, i, g.at[pl.ds(0, g.shape[0]), 1 - c], o, (x, y, 1 - c))
                for i, (g, o) in enumerate(zip(ins, outs))]

    def start(*refs):
        for cp in copies(*refs):
            cp.start()

    def finish(*refs):
        for cp in copies(*refs):
            cp.wait()

    return _SideJob(gs, [jax.ShapeDtypeStruct((g.shape[0],) + g.shape[2:], g.dtype) for g in gs], len(gs), start, finish)


def _join_halves(halves, *, name):
    def body(ins, outs, send_sems, recv_sems):
        x, y, c = lax.axis_index("x"), lax.axis_index("y"), lax.axis_index("c")
        sibling = (x, y, 1 - c)
        sends = [_remote(send_sems, recv_sems, i, h, o.at[c], sibling) for i, (h, o) in enumerate(zip(ins, outs))]
        for cp in sends:
            cp.start()
        for i, (h, o) in enumerate(zip(ins, outs)):
            _remote(send_sems, recv_sems, i, h, o.at[1 - c], sibling).wait_recv()
        for cp in sends:
            cp.wait_send()

    return _comm_call(body, halves, [jax.ShapeDtypeStruct((2,) + h.shape, h.dtype) for h in halves], len(halves), name)


def _gather_all(v, *, name):
    def body(v_ref, o_ref, send_sems, recv_sems, local_sem):
        x, y, c = lax.axis_index("x"), lax.axis_index("y"), lax.axis_index("c")
        me = 4 * x + 2 * y + c
        mine = pltpu.make_async_copy(v_ref, o_ref.at[me], local_sem)
        mine.start()
        peers = [(x ^ (q >> 2 & 1), y ^ (q >> 1 & 1), c ^ (q & 1)) for q in range(1, 8)]
        sends = [pltpu.make_async_remote_copy(src_ref=v_ref, dst_ref=o_ref.at[me], send_sem=send_sems.at[q],
                                              recv_sem=recv_sems.at[q], device_id=peer, device_id_type=MESH)
                 for q, peer in enumerate(peers)]
        for cp in sends:
            cp.start()
        for q, (px, py, pc) in enumerate(peers):
            pltpu.make_async_remote_copy(src_ref=v_ref, dst_ref=o_ref.at[4 * px + 2 * py + pc], send_sem=send_sems.at[q],
                                         recv_sem=recv_sems.at[q], device_id=(px, py, pc), device_id_type=MESH).wait_recv()
        for cp in sends:
            cp.wait_send()
        mine.wait()

    return pl.pallas_call(
        body,
        in_specs=[ANY],
        out_specs=ANY,
        out_shape=jax.ShapeDtypeStruct((8,) + v.shape, v.dtype),
        scratch_shapes=[pltpu.SemaphoreType.DMA((7,)), pltpu.SemaphoreType.DMA((7,)), pltpu.SemaphoreType.DMA],
        compiler_params=pltpu.CompilerParams(has_side_effects=True),
        name=name,
    )(v)


WEIGHTS = ['ssm_norm_w', 'ssm_in_w', 'ssm_conv_w', 'ssm_conv_b', 'ssm_dt_bias', 'ssm_a_log', 'ssm_d',
           'ssm_gate_norm_w', 'ssm_out_w', 'kv_norm_w', 'w_k', 'w_v', 'attn_norm_w', 'w_q', 'w_o',
           'ffn_norm_w', 'ffn_up_w', 'ffn_conv_w', 'ffn_conv_b', 'ffn_down_w', 'final_norm_w']
SHARD_AXIS = {'ssm_norm_w': 1, 'ssm_in_w': 2, 'ssm_conv_w': 2, 'ssm_conv_b': 1, 'ssm_gate_norm_w': 1,
              'ssm_out_w': 1, 'w_k': 0, 'w_v': 0, 'w_q': 1, 'w_o': 1, 'ffn_up_w': 2, 'ffn_conv_w': 2,
              'ffn_down_w': 1}
BIG = ['ssm_in_w', 'ssm_out_w', 'w_k', 'w_v', 'w_q', 'w_o', 'ffn_up_w', 'ffn_down_w']
SMALL = [n for n in WEIGHTS if n in SHARD_AXIS and n not in BIG]
REPLICATED = [n for n in WEIGHTS if n not in SHARD_AXIS]
STACKED = ['ffn_up_w', 'ffn_down_w']
N_CHIPS = 4


PACK_ROWS = 16


def _piece_rows(n):
    return -(-n // (PACK_ROWS * LANES)) * PACK_ROWS


def _pack(arrs, dtype, row_mult):
    lead = arrs[0].shape[:-1]
    pieces, total = [], 0
    for a in arrs:
        n = a.shape[-1]
        rows = _piece_rows(n)
        a = a.astype(dtype)
        if rows * LANES != n:
            a = jnp.pad(a, [(0, 0)] * len(lead) + [(0, rows * LANES - n)])
        pieces.append(a.reshape(lead + (rows, LANES)))
        total += rows
    extra = -total % row_mult
    if extra:
        pieces.append(jnp.zeros(lead + (extra, LANES), dtype))
    return jnp.concatenate(pieces, axis=len(lead))


def _unpack(buf, shapes):
    lead = buf.shape[:-2]
    out, off = [], 0
    for shp in shapes:
        n = math.prod(shp)
        rows = _piece_rows(n)
        piece = lax.slice_in_dim(buf, off, off + rows, axis=len(lead)).reshape(lead + (rows * LANES,))
        out.append(piece[..., :n].reshape(lead + tuple(shp)))
        off += rows
    return out


def _set_slot(buf, piece, index):
    return lax.dynamic_update_slice_in_dim(buf, piece[None], index, axis=0)


def _from_shards(stacked, axis):
    return jnp.concatenate([stacked[j] for j in range(N_CHIPS)], axis=axis)


def _ffn_fwd(h, norm_w, w_up, conv_w, conv_b, w_down, tag, side=None):
    u = _rmsnorm_fwd(h, norm_w, name=f"ffn{tag}_norm")
    hid = _matmul(u, w_up, name=f"ffn{tag}_up")
    act, side_outs = _conv_glu_fwd(hid, conv_w, conv_b, side=side, name=f"ffn{tag}_glu")
    out = _matmul(act, w_down, add=h, name=f"ffn{tag}_down")
    return out, (u, hid, act), side_outs


def _ffn_bwd(h, saved, dout, norm_w, w_up, conv_w, conv_b, w_down, tag, side=None):
    u, hid, act = saved
    dact = _matmul(dout, w_down, tb=True, name=f"ffn{tag}_down_dx")
    dw_down = _matmul(act, dout, ta=True, name=f"ffn{tag}_down_dw")
    dhid, dwg, dwv, dbg, dbv, side_outs = _conv_glu_bwd(hid, conv_w, conv_b, dact, side=side, name=f"ffn{tag}_glu_bwd")
    du = _matmul(dhid, w_up, tb=True, name=f"ffn{tag}_up_dx")
    dw_up = _matmul(u, dhid, ta=True, out_parts=N_CHIPS, name=f"ffn{tag}_up_dw")
    dh, (dnorm,) = _rmsnorm_bwd(h, [(du, norm_w)], dout, name=f"ffn{tag}_norm_bwd")
    return dh, dict(norm=dnorm[0], up=dw_up, conv_w=jnp.concatenate([dwg, dwv], axis=1),
                    conv_b=jnp.concatenate([dbg, dbv], axis=1)[0], down=dw_down), side_outs


class _Pieces:
    def __init__(self, local):
        self.c = lax.axis_index("c")
        self.chip = 2 * lax.axis_index("x") + lax.axis_index("y")
        self.shape, self.s16 = {}, {}
        for n in BIG:
            blk = local[n]
            layers = [(n, l, blk[l]) for l in range(blk.shape[0])] if n in STACKED else [(n, None, blk.reshape(blk.shape[-2:]))]
            for name, l, p in layers:
                self.shape[name, l] = p.shape
                self.s16[name, l] = p.astype(BF16).reshape(2, p.shape[0] // 2, p.shape[1])

    def gather_job(self, keys):
        return _gather_job([self.s16[k] for k in keys])

    def weights(self, keys, gathered):
        out = []
        for k, g in zip(keys, gathered):
            r, cc = self.shape[k]
            by_chip = _set_slot(g, self.s16[k], self.chip).reshape(N_CHIPS, r, cc)
            if k[0] == 'ssm_in_w':
                by_chip = by_chip.transpose(1, 0, 2).reshape(r, N_CHIPS * cc)
            elif k[0] != 'ffn_up_w':
                by_chip = by_chip.reshape(N_CHIPS * r, cc)
            out.append(by_chip)
        return out

    def by_halves(self, keys, grads):
        gs = []
        for k, g in zip(keys, grads):
            r, cc = self.shape[k]
            if k[0] == 'ssm_in_w':
                g = g.reshape(r, N_CHIPS, cc).transpose(1, 0, 2)
            gs.append(g.reshape(N_CHIPS, 2, r // 2, cc))
        return gs

    def pair_sums(self, gs, recv, tag):
        return [_pair_add(g, rv, self.c, out_dtype=BF16, name=f"rs_pair_add_{tag}{i}") for i, (g, rv) in enumerate(zip(gs, recv))]

    def chip_sums(self, pairs, scattered, tag):
        return [_sum_leading(_set_slot(s, lax.dynamic_index_in_dim(p, self.chip, axis=0, keepdims=False), self.chip),
                             name=f"rs_chip_sum_{tag}{i}") for i, (s, p) in enumerate(zip(scattered, pairs))]

    def shards(self, keys, halves):
        joined = _join_halves(halves, name="rs_half_join")
        return {k: _set_slot(j, h, self.c).reshape(self.shape[k]) for k, h, j in zip(keys, halves, joined)}


def _step(x, target, w, pieces):
    t = x.shape[0]
    g_n, heads = SSM_GROUPS, SSM_HEADS
    r_h = heads // g_n
    di = D_INNER
    zx_cols = di + CONV_DIM
    k_in = [('ssm_in_w', None)]
    k_ffn0 = [('ssm_out_w', None), ('ffn_up_w', 0), ('ffn_down_w', 0)]
    k_qkv = [('w_k', None), ('w_v', None), ('w_q', None)]
    k_late = [('w_o', None), ('ffn_up_w', 1), ('ffn_down_w', 1)]
    (w_in,) = pieces.weights(k_in, _run_job(pieces.gather_job(k_in), "gather_ssm_in"))
    w_zx = w_in[:, :zx_cols]
    w_dt = jnp.pad(w_in[:, zx_cols:], ((0, 0), (0, LANES - heads)))
    conv_w, conv_b = w['ssm_conv_w'][0], w['ssm_conv_b'][0]
    hp = jnp.stack([w['ssm_dt_bias'][0], w['ssm_a_log'][0], w['ssm_d'][0]], axis=0).reshape(3, g_n, r_h)
    hpc, hpr = hp.transpose(1, 0, 2), hp.transpose(1, 2, 0)

    h0 = x
    u0 = _rmsnorm_fwd(h0, w['ssm_norm_w'][0], name="ssm_norm")
    zx = _matmul(u0, w_zx, name="ssm_in_zx")
    dt_raw = _matmul(u0, w_dt, name="ssm_in_dt")[:, :heads]
    dtg = dt_raw.reshape(t, g_n, r_h)
    dtc, dtr = dtg.transpose(1, 0, 2), dtg.transpose(1, 2, 0)
    xbc = _conv_silu_fwd(zx, conv_w, conv_b, x_off=di, name="ssm_conv")
    y, prev, got = _ssd_fwd(xbc, dtc, dtr, hpc, hpr, side=pieces.gather_job(k_ffn0), name="ssd_fwd")
    w_out, w_up0, w_down0 = pieces.weights(k_ffn0, got)
    yn = _gate_norm_fwd(y, zx, w['ssm_gate_norm_w'][0], name="ssm_gate_norm")
    h1 = _matmul(yn, w_out, add=h0, name="ssm_out")
    h2, ffn0, got = _ffn_fwd(h1, w['ffn_norm_w'][0], w_up0, w['ffn_conv_w'][0], w['ffn_conv_b'][0], w_down0, 0,
                             side=pieces.gather_job(k_qkv))
    w_k, w_v, w_q = pieces.weights(k_qkv, got)
    hk = _rmsnorm_fwd(h2, w['kv_norm_w'], name="kv_norm")
    qn = _rmsnorm_fwd(h2, w['attn_norm_w'][0], name="attn_norm")
    k2 = _matmul(hk, w_k, out_dtype=BF16, name="attn_k")
    v2 = _matmul(hk, w_v, out_dtype=BF16, name="attn_v")
    q2 = _matmul(qn, w_q, out_dtype=BF16, name="attn_q")
    o2, lt, first, got = _sb_fwd(q2, k2, v2, SB_HEADS, side=pieces.gather_job(k_late), name="sb_fwd")
    w_o, w_up1, w_down1 = pieces.weights(k_late, got)
    h3 =_matmul(o2, w_o, add=h2, name="attn_o")
    h4, ffn1, _ = _ffn_fwd(h3, w['ffn_norm_w'][1], w_up1, w['ffn_conv_w'][1], w['ffn_conv_b'][1], w_down1, 1)
    loss_p, dh4, d_final = _loss_head(h4, w['final_norm_w'], target, name="loss_head")

    dh3, g1, _ = _ffn_bwd(h3, ffn1, dh4, w['ffn_norm_w'][1], w_up1, w['ffn_conv_w'][1], w['ffn_conv_b'][1], w_down1, 1)
    do2 = _matmul(dh3, w_o, tb=True, out_dtype=BF16, name="attn_o_dx")
    dw_o = _matmul(o2, dh3, ta=True, name="attn_o_dw")
    dq2, dk2, dv2 = _sb_bwd(q2, k2, v2, lt, first, do2, SB_HEADS, name="sb_bwd")
    dqn = _matmul(dq2, w_q, tb=True, name="attn_q_dx")
    dw_q = _matmul(qn, dq2, ta=True, name="attn_q_dw")
    dhk = _matmul(dk2, w_k, tb=True, name="attn_k_dx")
    dhk = _matmul(dv2, w_v, tb=True, add=dhk, name="attn_v_dx")
    dw_k = _matmul(hk, dk2, ta=True, name="attn_k_dw")
    dw_v = _matmul(hk, dv2, ta=True, name="attn_v_dw")
    dh2, (d_attn_norm, d_kv_norm) = _rmsnorm_bwd(h2, [(dqn, w['attn_norm_w'][0]), (dhk, w['kv_norm_w'])], dh3,
                                                 name="attn_norms_bwd")
    gs_late = pieces.by_halves(k_qkv + k_late, [dw_k, dw_v, dw_q, dw_o, g1['up'], g1['down']])
    dh1, g0, recv = _ffn_bwd(h1, ffn0, dh2, w['ffn_norm_w'][0], w_up0, w['ffn_conv_w'][0], w['ffn_conv_b'][0], w_down0, 0,
                             side=_swap_job(gs_late))
    pairs_late = pieces.pair_sums(gs_late, recv, "a")
    dyn = _matmul(dh1, w_out, tb=True, name="ssm_out_dx")
    dw_out = _matmul(yn, dh1, ta=True, name="ssm_out_dw")
    k_done = k_qkv + k_late + k_ffn0
    gs_ffn0 = pieces.by_halves(k_ffn0, [dw_out, g0['up'], g0['down']])
    dy, dz, d_gate, recv = _gate_norm_bwd(y, zx, w['ssm_gate_norm_w'][0], dyn, side=_swap_job(gs_ffn0),
                                          name="ssm_gate_norm_bwd")
    pairs_done = pairs_late + pieces.pair_sums(gs_ffn0, recv, "c")
    dxs, dbm, dcm, ddt_g, hg, scattered_done = _ssd_bwd(xbc, dtc, dtr, hpc, hpr, prev, dy,
                                                        side=_scatter_job(pairs_done), name="ssd_bwd")
    dzx, d_conv_w, d_conv_b = _conv_silu_bwd(zx, conv_w, conv_b, [dxs, dbm, dcm], x_off=di, into=dz, name="ssm_conv_bwd")
    ddt = jnp.pad(ddt_g.transpose(1, 0, 2).reshape(t, heads), ((0, 0), (0, LANES - heads)))
    du0 = _matmul(dzx, w_zx, tb=True, name="ssm_in_zx_dx")
    du0 = _matmul(ddt, w_dt, tb=True, add=du0, name="ssm_in_dt_dx")
    dw_in = jnp.concatenate([_matmul(u0, dzx, ta=True, name="ssm_in_zx_dw"),
                             _matmul(u0, ddt, ta=True, name="ssm_in_dt_dw")[:, :heads]], axis=1)
    dx, (d_ssm_norm,) = _rmsnorm_bwd(h0, [(du0, w['ssm_norm_w'][0])], dh1, name="ssm_norm_bwd")

    gs_in = pieces.by_halves(k_in, [dw_in])
    pairs_in = pieces.pair_sums(gs_in, _run_job(_swap_job(gs_in), "rs_pair_swap_b"), "b")
    scattered_in = _run_job(_scatter_job(pairs_in), "rs_chip_scatter_b")
    halves = pieces.chip_sums(pairs_done, scattered_done, "a") + pieces.chip_sums(pairs_in, scattered_in, "b")
    big_grads = pieces.shards(k_done + k_in, halves)

    hgr = hg.transpose(1, 0, 2).reshape(3, heads)
    grads = {
        'ssm_norm_w': d_ssm_norm, 'ssm_conv_w': d_conv_w[None], 'ssm_conv_b': d_conv_b,
        'ssm_dt_bias': hgr[0:1], 'ssm_a_log': hgr[1:2], 'ssm_d': hgr[2:3], 'ssm_gate_norm_w': d_gate,
        'kv_norm_w': d_kv_norm[0], 'attn_norm_w': d_attn_norm, 'ffn_norm_w': jnp.stack([g0['norm'], g1['norm']]),
        'ffn_conv_w': jnp.stack([g0['conv_w'], g1['conv_w']]), 'ffn_conv_b': jnp.stack([g0['conv_b'], g1['conv_b']]),
        'final_norm_w': d_final[0],
    }
    return loss_p, dx, grads, big_grads


def kernel(x, ssm_norm_w, ssm_in_w, ssm_conv_w, ssm_conv_b, ssm_dt_bias, ssm_a_log, ssm_d, ssm_gate_norm_w, ssm_out_w, kv_norm_w, w_k, w_v, attn_norm_w, w_q, w_o, ffn_norm_w, ffn_up_w, ffn_conv_w, ffn_conv_b, ffn_down_w, final_norm_w, loss_target, m_ssm_norm_w, m_ssm_in_w, m_ssm_conv_w, m_ssm_conv_b, m_ssm_dt_bias, m_ssm_a_log, m_ssm_d, m_ssm_gate_norm_w, m_ssm_out_w, m_kv_norm_w, m_w_k, m_w_v, m_attn_norm_w, m_w_q, m_w_o, m_ffn_norm_w, m_ffn_up_w, m_ffn_conv_w, m_ffn_conv_b, m_ffn_down_w, m_final_norm_w, v_ssm_norm_w, v_ssm_in_w, v_ssm_conv_w, v_ssm_conv_b, v_ssm_dt_bias, v_ssm_a_log, v_ssm_d, v_ssm_gate_norm_w, v_ssm_out_w, v_kv_norm_w, v_w_k, v_w_v, v_attn_norm_w, v_w_q, v_w_o, v_ffn_norm_w, v_ffn_up_w, v_ffn_conv_w, v_ffn_conv_b, v_ffn_down_w, v_final_norm_w):
    args = (ssm_norm_w, ssm_in_w, ssm_conv_w, ssm_conv_b, ssm_dt_bias, ssm_a_log, ssm_d, ssm_gate_norm_w, ssm_out_w, kv_norm_w, w_k, w_v, attn_norm_w, w_q, w_o, ffn_norm_w, ffn_up_w, ffn_conv_w, ffn_conv_b, ffn_down_w, final_norm_w)
    moms = (m_ssm_norm_w, m_ssm_in_w, m_ssm_conv_w, m_ssm_conv_b, m_ssm_dt_bias, m_ssm_a_log, m_ssm_d, m_ssm_gate_norm_w, m_ssm_out_w, m_kv_norm_w, m_w_k, m_w_v, m_attn_norm_w, m_w_q, m_w_o, m_ffn_norm_w, m_ffn_up_w, m_ffn_conv_w, m_ffn_conv_b, m_ffn_down_w, m_final_norm_w)
    vels = (v_ssm_norm_w, v_ssm_in_w, v_ssm_conv_w, v_ssm_conv_b, v_ssm_dt_bias, v_ssm_a_log, v_ssm_d, v_ssm_gate_norm_w, v_ssm_out_w, v_kv_norm_w, v_w_k, v_w_v, v_attn_norm_w, v_w_q, v_w_o, v_ffn_norm_w, v_ffn_up_w, v_ffn_conv_w, v_ffn_conv_b, v_ffn_down_w, v_final_norm_w)
    local = dict(zip(WEIGHTS, args))
    m_in = dict(zip(WEIGHTS, moms))
    v_in = dict(zip(WEIGHTS, vels))
    chip = 2 * lax.axis_index("x") + lax.axis_index("y")

    full = {n: local[n] for n in REPLICATED}
    small32 = _gather_chips(_pack([local[n].reshape(-1) for n in SMALL], F32, 8), name="gather_small")
    for n, st in zip(SMALL, _unpack(small32, [local[n].shape for n in SMALL])):
        full[n] = _from_shards(st, SHARD_AXIS[n])

    pieces = _Pieces(local)
    loss_p, dx, grads, big_grads = _step(x[0], loss_target[0], full, pieces)
    gshard = {}
    for n in BIG:
        if n in STACKED:
            gshard[n] = [big_grads[n, l] for l in range(local[n].shape[0])]
        else:
            gshard[n] = big_grads[n, None].reshape(local[n].shape)

    small = SMALL + REPLICATED
    rep = _pack([loss_p.reshape(-1)] + [grads[n].reshape(-1) for n in small], F32, 8)
    tot = _sum_leading(_gather_all(rep, name="ar_gather"), name="ar_sum")
    parts = _unpack(tot, [(LANES,)] + [grads[n].shape for n in small])
    loss = jnp.sum(parts[0])
    for n, g in zip(small, parts[1:]):
        if n in SHARD_AXIS:
            size = local[n].shape[SHARD_AXIS[n]]
            g = lax.dynamic_slice_in_dim(g, chip * size, size, axis=SHARD_AXIS[n])
        gshard[n] = g

    grads_out, deltas, new_m, new_v = [], [], [], []
    for n in WEIGHTS:
        if n in STACKED:
            g, d, nm, nv = _adamw_layers(local[n], gshard[n], m_in[n], v_in[n], name=f"adamw_{n}")
        else:
            g = gshard[n]
            d, nm, nv = _adamw(local[n], g, m_in[n], v_in[n], name=f"adamw_{n}")
        grads_out.append(g)
        deltas.append(d)
        new_m.append(nm)
        new_v.append(nv)
    return (loss, dx[None], *grads_out, *deltas, *new_m, *new_v)
```

```python
import functools
import math

import jax
import jax.numpy as jnp
from jax import lax
from jax.experimental import pallas as pl
from jax.experimental.pallas import tpu as pltpu

D_INNER = 2048
SSM_HEAD_DIM = 64
SSM_HEADS = 32
SSM_GROUPS = 4
SSM_STATE = 128
SSM_CHUNK = 128
GN = SSM_GROUPS * SSM_STATE
CONV_DIM = D_INNER + 2 * GN
SB_HEADS = 16
EPS = 1e-6
ADAM_LR = 0.001
ADAM_B1 = 0.9
ADAM_B2 = 0.999
ADAM_EPS = 1e-08
ADAM_WD = 0.01
ADAM_STEP = 10

LANES = 128
SUBLANES = 8
VMEM_LIMIT = 48 * 1024 * 1024
ADAM_BLOCK_BYTES = 1 << 20
F32 = jnp.float32
BF16 = jnp.bfloat16
MESH = pl.DeviceIdType.MESH


def _cparams(sem=None):
    return pltpu.CompilerParams(dimension_semantics=sem, vmem_limit_bytes=VMEM_LIMIT)


class _SideJob:
    def __init__(self, ins, out_shapes, n_sems, start, finish):
        self.ins, self.out_shapes, self.n_sems, self.start, self.finish = ins, out_shapes, n_sems, start, finish


def _call(body, *, grid, in_specs, out_specs, out_shape, scratch_shapes=(), sem, name, args, side=None):
    in_specs, out_specs, out_shape, scratch_shapes = list(in_specs), list(out_specs), list(out_shape), list(scratch_shapes)
    n_in, n_out = len(in_specs), len(out_specs)
    if side is None:
        outs = pl.pallas_call(body, grid=grid, in_specs=in_specs, out_specs=out_specs, out_shape=out_shape,
                              scratch_shapes=scratch_shapes, compiler_params=_cparams(sem), name=name)(*args)
        return list(outs), []
    k_in, k_out = len(side.ins), len(side.out_shapes)

    def wrapped(*refs):
        ins, s_ins = refs[:n_in], refs[n_in:n_in + k_in]
        o0 = n_in + k_in
        outs, s_outs = refs[o0:o0 + n_out], refs[o0 + n_out:o0 + n_out + k_out]
        scratch, send_sems, recv_sems = refs[o0 + n_out + k_out:-2], refs[-2], refs[-1]
        ids = [pl.program_id(a) for a in range(len(grid))]
        first = functools.reduce(jnp.logical_and, [p == 0 for p in ids])
        last = functools.reduce(jnp.logical_and, [p == g - 1 for p, g in zip(ids, grid)])

        @pl.when(first)
        def _():
            side.start(s_ins, s_outs, send_sems, recv_sems)

        body(*ins, *outs, *scratch)

        @pl.when(last)
        def _():
            side.finish(s_ins, s_outs, send_sems, recv_sems)

    outs = pl.pallas_call(
        wrapped, grid=grid, in_specs=in_specs + [ANY] * k_in, out_specs=out_specs + [ANY] * k_out,
        out_shape=out_shape + list(side.out_shapes),
        scratch_shapes=scratch_shapes + [pltpu.SemaphoreType.DMA((side.n_sems,)), pltpu.SemaphoreType.DMA((side.n_sems,))],
        compiler_params=_cparams(tuple("arbitrary" for _ in grid)), name=name)(*args, *side.ins)
    return list(outs[:n_out]), list(outs[n_out:])


def _tile(n, cands):
    for c in cands:
        if n % c == 0:
            return c
    return n


def _nt(a, b):
    return lax.dot_general(a, b, (((1,), (1,)), ((), ())), preferred_element_type=F32)


def _tn(a, b):
    return lax.dot_general(a, b, (((0,), (0,)), ((), ())), preferred_element_type=F32)


def _nn(a, b):
    return jnp.dot(a, b, preferred_element_type=F32)


def _split(x, pieces):
    out = []
    for _ in range(pieces - 1):
        h = x.astype(BF16)
        out.append(h)
        x = x - h.astype(F32)
    out.append(x.astype(BF16))
    return out


def _ones_dot(ones, x, *, ones_left, pieces=3):
    o16 = ones.astype(BF16)
    acc = None
    for piece in _split(x, pieces):
        term = _nn(o16, piece) if ones_left else _nn(piece, o16)
        acc = term if acc is None else acc + term
    return acc


def _row_sums(x, pieces=2):
    return _ones_dot(jnp.ones((x.shape[1], LANES), F32), x, ones_left=False, pieces=pieces)


def _softplus(x):
    return jnp.maximum(x, 0.0) + jnp.log(1.0 + jnp.exp(-jnp.abs(x)))


def _sigmoid(x):
    return 0.5 * jnp.tanh(0.5 * x) + 0.5


MM_TILE_MAX = 1408
MM_VMEM_BUDGET = 40 * 1024 * 1024


def _divisors(n, cap):
    out = [d for d in range(min(cap, n) // LANES * LANES, 0, -LANES) if n % d == 0]
    return out or [n]


def _mm_tiles(m, n, k, a_bytes, b_bytes, o_bytes, add_bytes):
    best = None
    for tm in _divisors(m, MM_TILE_MAX):
        for tn in _divisors(n, MM_TILE_MAX):
            for tk in _divisors(k, MM_TILE_MAX):
                vmem = 2 * (tm * tk * a_bytes + tk * tn * b_bytes + tm * tn * (o_bytes + add_bytes)) + tm * tn * 4
                if vmem > MM_VMEM_BUDGET:
                    continue
                score = (tm * tn * tk, tm * tn)
                if best is None or score > best[0]:
                    best = (score, (tm, tn, tk))
    return best[1]


def _matmul(a, b, *, ta=False, tb=False, add=None, out_dtype=F32, out_parts=1, name):
    a_parts = a.shape[0] if a.ndim == 3 else 1
    b_parts = b.shape[0] if b.ndim == 3 else 1
    assert not (ta and a_parts > 1)
    a2, b2 = a.shape[-2:], b.shape[-2:]
    m, k = (a2[1], a2[0]) if ta else (a2[0], a2[1] * a_parts)
    n, kb = (b2[0], b2[1] * b_parts) if tb else (b2[1] * b_parts, b2[0])
    assert kb == k, (a.shape, b.shape)
    n_unit = math.gcd(n // out_parts, n if tb else b2[1])
    k_unit = math.gcd(k // a_parts, b2[1] if tb else k)
    tm, tn, tk = _mm_tiles(m, n_unit, k_unit, a.dtype.itemsize, b.dtype.itemsize, jnp.dtype(out_dtype).itemsize,
                           0 if add is None else add.dtype.itemsize)
    nk = k // tk
    ka, kbp = (k // a_parts) // tk, (k // b_parts) // tk
    nb, no = (n // b_parts) // tn, (n // out_parts) // tn

    def body(*refs):
        if add is None:
            a_ref, b_ref, o_ref = refs[:3]
            add_ref = None
        else:
            a_ref, b_ref, add_ref, o_ref = refs[:4]
        kk = pl.program_id(2)
        dn = (((0 if ta else 1,), (1 if tb else 0,)), ((), ()))
        prod = lax.dot_general(a_ref[...].astype(BF16), b_ref[...].astype(BF16), dn, preferred_element_type=F32)

        def finish(r):
            if add_ref is not None:
                r = r + add_ref[...].astype(F32)
            o_ref[...] = r.astype(o_ref.dtype)

        if nk == 1:
            finish(prod)
            return
        acc_ref = refs[-1]

        @pl.when(kk == 0)
        def _():
            acc_ref[...] = prod

        @pl.when(jnp.logical_and(kk > 0, kk < nk - 1))
        def _():
            acc_ref[...] += prod

        @pl.when(kk == nk - 1)
        def _():
            finish(acc_ref[...] + prod)

    if ta:
        a_spec = pl.BlockSpec((tk, tm), lambda i, j, kk: (kk, i))
    elif a_parts > 1:
        a_spec = pl.BlockSpec((None, tm, tk), lambda i, j, kk: (kk // ka, i, kk % ka))
    else:
        a_spec = pl.BlockSpec((tm, tk), lambda i, j, kk: (i, kk))
    if b_parts == 1:
        b_spec = pl.BlockSpec((tn, tk), lambda i, j, kk: (j, kk)) if tb else pl.BlockSpec((tk, tn), lambda i, j, kk: (kk, j))
    elif tb:
        b_spec = pl.BlockSpec((None, tn, tk), lambda i, j, kk: (kk // kbp, j, kk % kbp))
    else:
        b_spec = pl.BlockSpec((None, tk, tn), lambda i, j, kk: (j // nb, kk, j % nb))
    if out_parts > 1:
        o_spec = pl.BlockSpec((None, tm, tn), lambda i, j, kk: (j // no, i, j % no))
        o_shape = jax.ShapeDtypeStruct((out_parts, m, n // out_parts), out_dtype)
    else:
        o_spec = pl.BlockSpec((tm, tn), lambda i, j, kk: (i, j))
        o_shape = jax.ShapeDtypeStruct((m, n), out_dtype)
    in_specs = [a_spec, b_spec]
    args = [a, b]
    if add is not None:
        in_specs.append(pl.BlockSpec((tm, tn), lambda i, j, kk: (i, j)))
        args.append(add)
    return pl.pallas_call(
        body,
        grid=(m // tm, n // tn, nk),
        in_specs=in_specs,
        out_specs=o_spec,
        out_shape=o_shape,
        scratch_shapes=[pltpu.VMEM((tm, tn), F32)] if nk > 1 else [],
        compiler_params=_cparams(("parallel", "parallel", "arbitrary")),
        name=name,
    )(*args)


def _rmsnorm_fwd(x, w, *, name):
    t, d = x.shape
    tb = _tile(t, (512, 256, 128))

    def body(x_ref, w_ref, o_ref):
        xv = x_ref[...]
        r = lax.rsqrt(jnp.mean(xv * xv, axis=-1, keepdims=True) + EPS)
        o_ref[...] = (xv * r * w_ref[...]).astype(o_ref.dtype)

    return pl.pallas_call(
        body,
        grid=(t // tb,),
        in_specs=[pl.BlockSpec((tb, d), lambda i: (i, 0)), pl.BlockSpec((1, d), lambda i: (0, 0))],
        out_specs=pl.BlockSpec((tb, d), lambda i: (i, 0)),
        out_shape=jax.ShapeDtypeStruct((t, d), BF16),
        compiler_params=_cparams(("parallel",)),
        name=name,
    )(x, w.reshape(1, d))


def _rmsnorm_bwd(x, dys, dres, *, name):
    t, d = x.shape
    tb = _tile(t, (256, 128))
    nn = len(dys)
    has_res = dres is not None

    def body(*refs):
        x_ref = refs[0]
        dy_refs = refs[1:1 + nn]
        w_refs = refs[1 + nn:1 + 2 * nn]
        pos = 1 + 2 * nn
        res_ref = refs[pos] if has_res else None
        pos += 1 if has_res else 0
        dx_ref = refs[pos]
        dw_refs = refs[pos + 1:pos + 1 + nn]
        i = pl.program_id(0)
        xv = x_ref[...]
        r = lax.rsqrt(jnp.mean(xv * xv, axis=-1, keepdims=True) + EPS)
        xn = xv * r
        dx = res_ref[...] if has_res else jnp.zeros_like(xv)
        for q in range(nn):
            dy = dy_refs[q][...].astype(F32)
            g = dy * w_refs[q][...]
            dx = dx + r * (g - xn * jnp.mean(g * xn, axis=-1, keepdims=True))
            dwp = jnp.sum(dy * xn, axis=0, keepdims=True)

            @pl.when(i == 0)
            def _(q=q, dwp=dwp):
                dw_refs[q][...] = dwp

            @pl.when(i > 0)
            def _(q=q, dwp=dwp):
                dw_refs[q][...] += dwp
        dx_ref[...] = dx

    row = pl.BlockSpec((tb, d), lambda i: (i, 0))
    vec = pl.BlockSpec((1, d), lambda i: (0, 0))
    in_specs = [row] + [row] * nn + [vec] * nn + ([row] if has_res else [])
    args = [x] + [p[0] for p in dys] + [p[1].reshape(1, d) for p in dys] + ([dres] if has_res else [])
    outs = pl.pallas_call(
        body,
        grid=(t // tb,),
        in_specs=in_specs,
        out_specs=[row] + [vec] * nn,
        out_shape=[jax.ShapeDtypeStruct((t, d), F32)] + [jax.ShapeDtypeStruct((1, d), F32)] * nn,
        compiler_params=_cparams(("arbitrary",)),
        name=name,
    )(*args)
    return outs[0], list(outs[1:])


def _loss_head(x, w, target, *, name):
    t, d = x.shape
    tb = _tile(t, (256, 128))

    def body(x_ref, w_ref, t_ref, loss_ref, dx_ref, dw_ref):
        i = pl.program_id(0)
        xv = x_ref[...]
        wv = w_ref[...]
        r = lax.rsqrt(jnp.mean(xv * xv, axis=-1, keepdims=True) + EPS)
        xn = xv * r
        e = xn * wv - t_ref[...]
        lp = 0.5 * jnp.sum(jnp.mean(e * e, axis=-1, keepdims=True), axis=0, keepdims=True)
        dy = e * (1.0 / d)
        g = dy * wv
        dx_ref[...] = r * (g - xn * jnp.mean(g * xn, axis=-1, keepdims=True))
        dwp = jnp.sum(dy * xn, axis=0, keepdims=True)
        lpv = jnp.broadcast_to(lp, (1, LANES)) * (1.0 / LANES)

        @pl.when(i == 0)
        def _():
            dw_ref[...] = dwp
            loss_ref[...] = lpv

        @pl.when(i > 0)
        def _():
            dw_ref[...] += dwp
            loss_ref[...] += lpv

    row = pl.BlockSpec((tb, d), lambda i: (i, 0))
    vec = pl.BlockSpec((1, d), lambda i: (0, 0))
    return pl.pallas_call(
        body,
        grid=(t // tb,),
        in_specs=[row, vec, row],
        out_specs=[pl.BlockSpec((1, LANES), lambda i: (0, 0)), row, vec],
        out_shape=[jax.ShapeDtypeStruct((1, LANES), F32), jax.ShapeDtypeStruct((t, d), F32),
                   jax.ShapeDtypeStruct((1, d), F32)],
        compiler_params=_cparams(("arbitrary",)),
        name=name,
    )(x, w.reshape(1, d), target)


ROW_CHUNK = 64
PAD = SUBLANES


class _Strip:
    def __init__(self, head_ref, x_ref, rows):
        self.head_ref, self.x_ref = head_ref, x_ref
        head_ref[0:PAD, :] = jnp.zeros((PAD, head_ref.shape[1]), F32)
        head_ref[pl.ds(PAD, rows), :] = x_ref[pl.ds(0, rows), :]

    def rows(self, r0, rows, back):
        if r0 == 0:
            return self.head_ref[pl.ds(PAD - back, rows), :]
        return self.x_ref[pl.ds(r0 - back, rows), :]


def _shifted(strip, r0, rows, back):
    return strip.rows(r0, rows, back)


def _conv_taps(strip, w_ref, r0, rows, kw):
    acc = None
    for j in range(kw):
        term = _shifted(strip, r0, rows, kw - 1 - j) * w_ref[j:j + 1, :]
        acc = term if acc is None else acc + term
    return acc


def _fill_pad(pad_ref, x_ref, rows):
    return _Strip(pad_ref, x_ref, rows)


def _conv_silu_fwd(x, w, b, *, x_off=0, name):
    t = x.shape[0]
    kw, c = w.shape
    cw = _tile(math.gcd(c, x_off) if x_off else c, (256, 128))
    ob = x_off // cw
    rc = _tile(t, (ROW_CHUNK,))

    def body(x_ref, w_ref, b_ref, o_ref, pad_ref):
        xs = _fill_pad(pad_ref, x_ref, rc)
        for r0 in range(0, t, rc):
            pre = _conv_taps(xs, w_ref, r0, rc, kw) + b_ref[...]
            o_ref[pl.ds(r0, rc), :] = pre * _sigmoid(pre)

    strip = pl.BlockSpec((t, cw), lambda i: (0, i))
    return pl.pallas_call(
        body,
        grid=(c // cw,),
        in_specs=[pl.BlockSpec((t, cw), lambda i: (0, i + ob)), pl.BlockSpec((kw, cw), lambda i: (0, i)),
                  pl.BlockSpec((1, cw), lambda i: (0, i))],
        out_specs=strip,
        out_shape=jax.ShapeDtypeStruct((t, c), F32),
        scratch_shapes=[pltpu.VMEM((PAD + rc, cw), F32)],
        compiler_params=_cparams(("parallel",)),
        name=name,
    )(x, w, b.reshape(1, c))


def _conv_bwd_core(dpre_pad_ref, x_pad_ref, w_ref, dx_ref, dw_ref, db_ref, t, rc, kw):
    cw = dx_ref.shape[1]

    def fold(a):
        return jnp.sum(a.reshape(rc // SUBLANES, SUBLANES, cw), axis=0) if rc % SUBLANES == 0 else jnp.sum(a, axis=0, keepdims=True)

    dws = [None] * kw
    dbs = None
    for r0 in range(0, t, rc):
        dpre = dpre_pad_ref[pl.ds(PAD + r0, rc), :]
        dx = None
        for j in range(kw):
            s = kw - 1 - j
            term = dpre_pad_ref[pl.ds(PAD + r0 + s, rc), :] * w_ref[j:j + 1, :]
            dx = term if dx is None else dx + term
            part = fold(dpre * _shifted(x_pad_ref, r0, rc, s))
            dws[j] = part if dws[j] is None else dws[j] + part
        part = fold(dpre)
        dbs = part if dbs is None else dbs + part
        dx_ref[pl.ds(r0, rc), :] = dx
    for j in range(kw):
        dw_ref[j:j + 1, :] = jnp.sum(dws[j], axis=0, keepdims=True)
    db_ref[...] = jnp.sum(dbs, axis=0, keepdims=True)


def _conv_silu_bwd(x, w, b, dact, *, x_off=0, into=None, name):
    t = x.shape[0]
    kw, c = w.shape
    parts = dact if isinstance(dact, (list, tuple)) else [dact]
    widths = [p.shape[1] for p in parts]
    assert sum(widths) == c
    cw = _tile(functools.reduce(math.gcd, widths + [x_off or c]), (256, 128) if len(parts) == 1 else (128,))
    ob = x_off // cw
    rc = _tile(t, (ROW_CHUNK,))
    firsts = [sum(widths[:p]) // cw for p in range(len(parts))]
    counts = [wd // cw for wd in widths]
    n_p = len(parts)

    def body(x_ref, w_ref, b_ref, *rest):
        da_refs = rest[:n_p]
        dx_ref, dw_ref, db_ref, xpad_ref, dpad_ref = rest[-5 - (n_p > 1):][:5]
        if n_p > 1:
            da_ref = rest[-1]
            i = pl.program_id(0)
            for p in range(n_p):
                @pl.when(jnp.logical_and(i >= firsts[p], i < firsts[p] + counts[p]))
                def _(p=p):
                    da_ref[...] = da_refs[p][...]
        else:
            da_ref = da_refs[0]
        xs = _fill_pad(xpad_ref, x_ref, rc)
        dpad_ref[0:PAD, :] = jnp.zeros((PAD, cw), F32)
        dpad_ref[pl.ds(PAD + t, PAD), :] = jnp.zeros((PAD, cw), F32)
        for r0 in range(0, t, rc):
            pre = _conv_taps(xs, w_ref, r0, rc, kw) + b_ref[...]
            sg = _sigmoid(pre)
            dpad_ref[pl.ds(PAD + r0, rc), :] = da_ref[pl.ds(r0, rc), :] * (sg * (1.0 + pre * (1.0 - sg)))
        _conv_bwd_core(dpad_ref, xs, w_ref, dx_ref, dw_ref, db_ref, t, rc, kw)

    strip = pl.BlockSpec((t, cw), lambda i: (0, i))
    wspec = pl.BlockSpec((kw, cw), lambda i: (0, i))
    bspec = pl.BlockSpec((1, cw), lambda i: (0, i))
    xspec = pl.BlockSpec((t, cw), lambda i: (0, i + ob))
    dspecs = [pl.BlockSpec((t, cw), lambda i, f=f, n=n: (0, jnp.clip(i - f, 0, n - 1))) for f, n in zip(firsts, counts)]
    extra = {} if into is None else dict(input_output_aliases={3 + n_p: 0})
    pad = pltpu.VMEM((t + 2 * PAD, cw), F32)
    return pl.pallas_call(
        body,
        grid=(c // cw,),
        in_specs=[xspec, wspec, bspec] + dspecs + ([] if into is None else [ANY]),
        out_specs=[strip if into is None else xspec, wspec, bspec],
        out_shape=[jax.ShapeDtypeStruct((t, c) if into is None else into.shape, F32), jax.ShapeDtypeStruct((kw, c), F32),
                   jax.ShapeDtypeStruct((1, c), F32)],
        scratch_shapes=[pad, pad] + ([pltpu.VMEM((t, cw), F32)] if n_p > 1 else []),
        compiler_params=_cparams(("arbitrary",)),
        name=name,
        **extra,
    )(x, w, b.reshape(1, c), *parts, *([] if into is None else [into]))


def _conv_glu_fwd(hid, w, b, *, side=None, name):
    t, c2 = hid.shape
    f = c2 // 2
    kw = w.shape[0]
    cw = _tile(f, (256, 128))
    nf = f // cw
    rc = _tile(t, (ROW_CHUNK,))

    def body(g_ref, v_ref, wg_ref, wv_ref, bg_ref, bv_ref, o_ref, gpad_ref, vpad_ref):
        gs_, vs_ = _fill_pad(gpad_ref, g_ref, rc), _fill_pad(vpad_ref, v_ref, rc)
        for r0 in range(0, t, rc):
            gate = _conv_taps(gs_, wg_ref, r0, rc, kw) + bg_ref[...]
            val = _conv_taps(vs_, wv_ref, r0, rc, kw) + bv_ref[...]
            o_ref[pl.ds(r0, rc), :] = (gate * _sigmoid(gate) * val).astype(o_ref.dtype)

    gs = pl.BlockSpec((t, cw), lambda i: (0, i))
    vs = pl.BlockSpec((t, cw), lambda i: (0, i + nf))
    b2 = b.reshape(1, c2)
    (act,), side_outs = _call(
        body,
        grid=(nf,),
        in_specs=[gs, vs, pl.BlockSpec((kw, cw), lambda i: (0, i)), pl.BlockSpec((kw, cw), lambda i: (0, i + nf)),
                  pl.BlockSpec((1, cw), lambda i: (0, i)), pl.BlockSpec((1, cw), lambda i: (0, i + nf))],
        out_specs=[gs],
        out_shape=[jax.ShapeDtypeStruct((t, f), BF16)],
        scratch_shapes=[pltpu.VMEM((PAD + rc, cw), F32), pltpu.VMEM((PAD + rc, cw), F32)],
        sem=("parallel",),
        name=name,
        args=(hid, hid, w, w, b2, b2),
        side=side,
    )
    return act, side_outs


def _conv_glu_bwd(hid, w, b, dact, *, side=None, name):
    t, c2 = hid.shape
    f = c2 // 2
    kw = w.shape[0]
    cw = _tile(f, (128,))
    nf = f // cw
    rc = _tile(t, (ROW_CHUNK,))

    def body(g_ref, v_ref, wg_ref, wv_ref, bg_ref, bv_ref, da_ref,
             dgv_ref, dwg_ref, dwv_ref, dbg_ref, dbv_ref,
             gpad_ref, vpad_ref, dgpad_ref, dvpad_ref):
        gs_, vs_ = _fill_pad(gpad_ref, g_ref, rc), _fill_pad(vpad_ref, v_ref, rc)
        for ref in (dgpad_ref, dvpad_ref):
            ref[0:PAD, :] = jnp.zeros((PAD, cw), F32)
            ref[pl.ds(PAD + t, PAD), :] = jnp.zeros((PAD, cw), F32)
        for r0 in range(0, t, rc):
            gate = _conv_taps(gs_, wg_ref, r0, rc, kw) + bg_ref[...]
            val = _conv_taps(vs_, wv_ref, r0, rc, kw) + bv_ref[...]
            sg = _sigmoid(gate)
            da = da_ref[pl.ds(r0, rc), :].astype(F32)
            dgpad_ref[pl.ds(PAD + r0, rc), :] = da * val * (sg * (1.0 + gate * (1.0 - sg)))
            dvpad_ref[pl.ds(PAD + r0, rc), :] = da * (gate * sg)
        _conv_bwd_core(dgpad_ref, gs_, wg_ref, dgv_ref.at[0], dwg_ref, dbg_ref, t, rc, kw)
        _conv_bwd_core(dvpad_ref, vs_, wv_ref, dgv_ref.at[1], dwv_ref, dbv_ref, t, rc, kw)

    gs = pl.BlockSpec((t, cw), lambda i: (0, i))
    vs = pl.BlockSpec((t, cw), lambda i: (0, i + nf))
    wg = pl.BlockSpec((kw, cw), lambda i: (0, i))
    wv = pl.BlockSpec((kw, cw), lambda i: (0, i + nf))
    bg = pl.BlockSpec((1, cw), lambda i: (0, i))
    bv = pl.BlockSpec((1, cw), lambda i: (0, i + nf))
    b2 = b.reshape(1, c2)
    pad = pltpu.VMEM((t + 2 * PAD, cw), F32)
    outs, side_outs = _call(
        body,
        grid=(nf,),
        in_specs=[gs, vs, wg, wv, bg, bv, gs],
        out_specs=[pl.BlockSpec((2, t, cw), lambda i: (0, 0, i)), wg, wg, bg, bg],
        out_shape=[jax.ShapeDtypeStruct((2, t, f), F32),
                   jax.ShapeDtypeStruct((kw, f), F32), jax.ShapeDtypeStruct((kw, f), F32),
                   jax.ShapeDtypeStruct((1, f), F32), jax.ShapeDtypeStruct((1, f), F32)],
        scratch_shapes=[pad, pad, pad, pad],
        sem=("parallel",),
        name=name,
        args=(hid, hid, w, w, b2, b2, dact),
        side=side,
    )
    return (*outs, side_outs)


def _gate_norm_fwd(y, zx, w, *, name):
    t, di = y.shape
    gsz = di // SSM_GROUPS
    tb = _tile(t, (256, 128))

    def body(y_ref, z_ref, w_ref, o_ref):
        for g in range(SSM_GROUPS):
            sl = slice(g * gsz, (g + 1) * gsz)
            zv = z_ref[:, sl]
            gv = y_ref[:, sl] * (zv * _sigmoid(zv))
            r = lax.rsqrt(jnp.mean(gv * gv, axis=-1, keepdims=True) + EPS)
            o_ref[:, sl] = (gv * r * w_ref[:, sl]).astype(o_ref.dtype)

    row = pl.BlockSpec((tb, di), lambda i: (i, 0))
    return pl.pallas_call(
        body,
        grid=(t // tb,),
        in_specs=[row, row, pl.BlockSpec((1, di), lambda i: (0, 0))],
        out_specs=row,
        out_shape=jax.ShapeDtypeStruct((t, di), BF16),
        compiler_params=_cparams(("parallel",)),
        name=name,
    )(y, zx, w.reshape(1, di))


def _gate_norm_bwd(y, zx, w, dyn, *, side=None, name):
    t, di = y.shape
    gsz = di // SSM_GROUPS
    tb = _tile(t, (256, 128))

    def body(y_ref, z_ref, w_ref, d_ref, dy_ref, dz_ref, dw_ref):
        i = pl.program_id(0)
        for g in range(SSM_GROUPS):
            sl = slice(g * gsz, (g + 1) * gsz)
            zv = z_ref[:, sl]
            yv = y_ref[:, sl]
            sg = _sigmoid(zv)
            sz = zv * sg
            gv = yv * sz
            r = lax.rsqrt(jnp.mean(gv * gv, axis=-1, keepdims=True) + EPS)
            gn = gv * r
            dn = d_ref[:, sl].astype(F32)
            q = dn * w_ref[:, sl]
            dg = r * (q - gn * jnp.mean(q * gn, axis=-1, keepdims=True))
            dy_ref[:, sl] = dg * sz
            dz_ref[:, sl] = dg * yv * (sg * (1.0 + zv * (1.0 - sg)))
            dwp = jnp.sum(dn * gn, axis=0, keepdims=True)

            @pl.when(i == 0)
            def _(sl=sl, dwp=dwp):
                dw_ref[:, sl] = dwp

            @pl.when(i > 0)
            def _(sl=sl, dwp=dwp):
                dw_ref[:, sl] += dwp

    row = pl.BlockSpec((tb, di), lambda i: (i, 0))
    vec = pl.BlockSpec((1, di), lambda i: (0, 0))
    outs, side_outs = _call(
        body,
        grid=(t // tb,),
        in_specs=[row, row, vec, row],
        out_specs=[row, row, vec],
        out_shape=[jax.ShapeDtypeStruct((t, di), F32), jax.ShapeDtypeStruct((t, zx.shape[1]), F32),
                   jax.ShapeDtypeStruct((1, di), F32)],
        sem=("arbitrary",),
        name=name,
        args=(y, zx, w.reshape(1, di), dyn),
        side=side,
    )
    return (*outs, side_outs)


def _adamw(w, g, m, v, *, name):
    shape = w.shape
    cols = shape[-1]
    rows = w.size // cols
    w2, g2, m2, v2 = (a.reshape(rows, cols) for a in (w, g, m, v))
    tr = rows if rows * cols * 4 <= ADAM_BLOCK_BYTES else _row_tile(rows, cols)
    c1 = 1.0 - ADAM_B1 ** ADAM_STEP
    c2 = 1.0 - ADAM_B2 ** ADAM_STEP

    def body(w_ref, g_ref, m_ref, v_ref, d_ref, nm_ref, nv_ref):
        gv = g_ref[...]
        nm = ADAM_B1 * m_ref[...] + (1.0 - ADAM_B1) * gv
        nv = ADAM_B2 * v_ref[...] + (1.0 - ADAM_B2) * (gv * gv)
        d_ref[...] = -ADAM_LR * ((nm / c1) / (jnp.sqrt(nv / c2) + ADAM_EPS) + ADAM_WD * w_ref[...])
        nm_ref[...] = nm
        nv_ref[...] = nv

    blk = pl.BlockSpec((tr, cols), lambda i: (i, 0))
    outs = pl.pallas_call(
        body,
        grid=(rows // tr,),
        in_specs=[blk] * 4,
        out_specs=[blk] * 3,
        out_shape=[jax.ShapeDtypeStruct((rows, cols), F32)] * 3,
        compiler_params=_cparams(("parallel",)),
        name=name,
    )(w2, g2, m2, v2)
    return tuple(o.reshape(shape) for o in outs)


def _adamw_layers(w, gs, m, v, *, name):
    n_l, rows, cols = w.shape
    assert len(gs) == n_l
    tr = _row_tile(rows, cols)
    c1 = 1.0 - ADAM_B1 ** ADAM_STEP
    c2 = 1.0 - ADAM_B2 ** ADAM_STEP

    def body(*refs):
        w_ref, m_ref, v_ref = refs[:3]
        g_refs = refs[3:3 + n_l]
        g_ref, d_ref, nm_ref, nv_ref = refs[3 + n_l:]
        layer = pl.program_id(0)
        gv = g_refs[0][...]
        for q in range(1, n_l):
            gv = jnp.where(layer == q, g_refs[q][...], gv)
        nm = ADAM_B1 * m_ref[...] + (1.0 - ADAM_B1) * gv
        nv = ADAM_B2 * v_ref[...] + (1.0 - ADAM_B2) * (gv * gv)
        g_ref[...] = gv
        d_ref[...] = -ADAM_LR * ((nm / c1) / (jnp.sqrt(nv / c2) + ADAM_EPS) + ADAM_WD * w_ref[...])
        nm_ref[...] = nm
        nv_ref[...] = nv

    stacked = pl.BlockSpec((None, tr, cols), lambda l, i: (l, i, 0))
    single = pl.BlockSpec((tr, cols), lambda l, i: (i, 0))
    return pl.pallas_call(
        body,
        grid=(n_l, rows // tr),
        in_specs=[stacked] * 3 + [single] * n_l,
        out_specs=[stacked] * 4,
        out_shape=[jax.ShapeDtypeStruct(w.shape, F32)] * 4,
        compiler_params=_cparams(("parallel", "parallel")),
        name=name,
    )(w, m, v, *gs)


def _ssd_scalars(dtc_ref, dtr_ref, hpc_ref, hpr_ref, ln):
    assert SSM_CHUNK == SSM_STATE == LANES, "the SSD kernels mix chunk, state and lane-wide tiles freely"
    bias_c, alog_c = hpc_ref[0, 0:1, :], hpc_ref[0, 1:2, :]
    bias_r, alog_r = hpr_ref[0, :, 0:1], hpr_ref[0, :, 1:2]
    a_c, a_r = -jnp.exp(alog_c), -jnp.exp(alog_r)
    raw_c = dtc_ref[0] + bias_c
    dt_c = _softplus(raw_c)
    dt_r = _softplus(dtr_ref[0] + bias_r)
    row = lax.broadcasted_iota(jnp.int32, (ln, ln), 0)
    col = lax.broadcasted_iota(jnp.int32, (ln, ln), 1)
    lower = (col <= row).astype(F32)
    upper = (row <= col).astype(F32)
    acs_c = _ones_dot(lower, dt_c * a_c, ones_left=True)
    acs_r = _ones_dot(upper, dt_r * a_r, ones_left=False)
    return raw_c, dt_c, a_c, acs_c, acs_r, row, col


def _ssd_specs(t, di, g_n, n_st, rp, ln, r_h, rev):
    nc = t // ln
    cidx = (lambda c: nc - 1 - c) if rev else (lambda c: c)
    xs = pl.BlockSpec((ln, rp), lambda g, c: (cidx(c), g))
    bm = pl.BlockSpec((ln, n_st), lambda g, c: (cidx(c), di // n_st + g))
    cm = pl.BlockSpec((ln, n_st), lambda g, c: (cidx(c), di // n_st + g_n + g))
    dtc = pl.BlockSpec((1, ln, r_h), lambda g, c: (g, cidx(c), 0))
    dtr = pl.BlockSpec((1, r_h, ln), lambda g, c: (g, 0, cidx(c)))
    hpc = pl.BlockSpec((1, 3, r_h), lambda g, c: (g, 0, 0))
    hpr = pl.BlockSpec((1, r_h, 3), lambda g, c: (g, 0, 0))
    prev = pl.BlockSpec((1, rp, n_st), lambda g, c: (cidx(c), g, 0))
    return xs, bm, cm, dtc, dtr, hpc, hpr, prev


def _ssd_fwd(xbc, dtc, dtr, hpc, hpr, *, side=None, name):
    t = xbc.shape[0]
    di, g_n, n_st, p_h, ln = D_INNER, SSM_GROUPS, SSM_STATE, SSM_HEAD_DIM, SSM_CHUNK
    r_h = SSM_HEADS // g_n
    rp = r_h * p_h
    nc = t // ln

    def body(xs_ref, b_ref, c_ref, dtc_ref, dtr_ref, hpc_ref, hpr_ref, y_ref, prev_ref, st_ref):
        @pl.when(pl.program_id(1) == 0)
        def _():
            st_ref[...] = jnp.zeros_like(st_ref)

        _, dt_c, _, acs_c, acs_r, row, col = _ssd_scalars(dtc_ref, dtr_ref, hpc_ref, hpr_ref, ln)
        bm = b_ref[...]
        cm = c_ref[...]
        cm16 = cm.astype(BF16)
        cb = _nt(cm16, bm.astype(BF16))
        causal = row >= col
        for r in range(r_h):
            sl = slice(r * p_h, (r + 1) * p_h)
            xs = xs_ref[:, sl]
            acs = jnp.broadcast_to(acs_c[:, r:r + 1], (ln, ln))
            last = acs[ln - 1:ln, :]
            lm = jnp.where(causal, jnp.exp(acs - acs_r[r:r + 1, :]), 0.0)
            xd = (xs * jnp.broadcast_to(dt_c[:, r:r + 1], (ln, p_h))).astype(BF16)
            prev = st_ref[sl, :]
            y = _nn((cb * lm).astype(BF16), xd)
            y = y + _nt(cm16, prev.astype(BF16)) * jnp.exp(acs[:, :p_h])
            y_ref[:, sl] = y + hpc_ref[0, 2:3, r:r + 1] * xs
            prev_ref[0, sl, :] = prev
            bd = (bm * jnp.exp(last - acs[:, :n_st])).astype(BF16)
            st_ref[sl, :] = prev * jnp.exp(last[:, :n_st]) + _tn(xd, bd)

    xs, bm, cm, dtcs, dtrs, hpcs, hprs, prev = _ssd_specs(t, di, g_n, n_st, rp, ln, r_h, False)
    (y, prev_out), side_outs = _call(
        body,
        grid=(g_n, nc),
        in_specs=[xs, bm, cm, dtcs, dtrs, hpcs, hprs],
        out_specs=[xs, prev],
        out_shape=[jax.ShapeDtypeStruct((t, di), F32), jax.ShapeDtypeStruct((nc, g_n * rp, n_st), F32)],
        scratch_shapes=[pltpu.VMEM((rp, n_st), F32)],
        sem=("parallel", "arbitrary"),
        name=name,
        args=(xbc, xbc, xbc, dtc, dtr, hpc, hpr),
        side=side,
    )
    return y, prev_out, side_outs


def _ssd_bwd(xbc, dtc, dtr, hpc, hpr, prev, dy, *, side=None, name):
    t = xbc.shape[0]
    di, g_n, n_st, p_h, ln = D_INNER, SSM_GROUPS, SSM_STATE, SSM_HEAD_DIM, SSM_CHUNK
    r_h = SSM_HEADS // g_n
    rp = r_h * p_h
    nc = t // ln

    def body(xs_ref, b_ref, c_ref, dtc_ref, dtr_ref, hpc_ref, hpr_ref, prev_ref, dy_ref,
             dxs_ref, db_ref, dc_ref, ddt_ref, hg_ref, ds_ref):
        step = pl.program_id(1)

        @pl.when(step == 0)
        def _():
            ds_ref[...] = jnp.zeros_like(ds_ref)

        raw_c, dt_c, a_c, acs_c, acs_r, row, col = _ssd_scalars(dtc_ref, dtr_ref, hpc_ref, hpr_ref, ln)
        bm = b_ref[...]
        cm = c_ref[...]
        bm16, cm16 = bm.astype(BF16), cm.astype(BF16)
        cb = _nt(cm16, bm16)
        cbt = _nt(bm16, cm16)
        lane_r = lax.broadcasted_iota(jnp.int32, (ln, r_h), 1)
        dacs_all = jnp.zeros((ln, r_h), F32)
        ddtx_all = jnp.zeros((ln, r_h), F32)
        dd_all = jnp.zeros((ln, r_h), F32)
        dcb = jnp.zeros((ln, ln), F32)
        dcbt = jnp.zeros((ln, ln), F32)
        dc_acc = jnp.zeros((ln, n_st), F32)
        db_acc = jnp.zeros((ln, n_st), F32)
        for r in range(r_h):
            sl = slice(r * p_h, (r + 1) * p_h)
            xs = xs_ref[:, sl]
            dyv = dy_ref[:, sl]
            dy16 = dyv.astype(BF16)
            acs = jnp.broadcast_to(acs_c[:, r:r + 1], (ln, ln))
            dtv = jnp.broadcast_to(dt_c[:, r:r + 1], (ln, p_h))
            acsr = acs_r[r:r + 1, :]
            last = acs[ln - 1:ln, :]
            xd = xs * dtv
            xd16 = xd.astype(BF16)
            lm = jnp.where(row >= col, jnp.exp(acs - acsr), 0.0)
            lmt = jnp.where(col >= row, jnp.exp(acsr - acs), 0.0)
            m_ls = cb * lm
            m_sl = cbt * lmt
            dm = _nt(dy16, xd16)
            dmt = _nt(xd16, dy16)
            dxd = _nn(m_sl.astype(BF16), dy16)
            dacs = _row_sums(dm * m_ls - dmt * m_sl)
            dcb = dcb + dm * lm
            dcbt = dcbt + dmt * lmt
            prev = prev_ref[0, sl, :]
            prev16 = prev.astype(BF16)
            e = jnp.exp(acs[:, :p_h])
            y_off = _nt(cm16, prev16) * e
            dacs = dacs + _row_sums(dyv * y_off)
            dyo16 = (dyv * e).astype(BF16)
            dc_acc = dc_acc + _nn(dyo16, prev16)
            dprev = _tn(dyo16, cm16)
            ds = ds_ref[sl, :]
            ds16 = ds.astype(BF16)
            decay = jnp.exp(last - acs)[:, :n_st]
            bd16 = (bm * decay).astype(BF16)
            dbd = _nn(xd16, ds16)
            dxd = dxd + _nt(bd16, ds16)
            db_acc = db_acc + dbd * decay
            tdec = _row_sums(dbd * bm) * decay
            cd = jnp.exp(last)
            dlast = (jnp.sum(tdec, axis=0, keepdims=True)
                     + jnp.sum(_row_sums(prev * ds), axis=0, keepdims=True) * cd)
            ds_ref[sl, :] = dprev + cd[:, :n_st] * ds
            dskip = hpc_ref[0, 2:3, r:r + 1]
            dxs_ref[:, sl] = dxd * dtv + dskip * dyv
            dacs = dacs - tdec + jnp.where(row == ln - 1, dlast, 0.0)
            dacs_all = jnp.where(lane_r == r, dacs[:, :r_h], dacs_all)
            ddtx_all = jnp.where(lane_r == r, _row_sums(dxd * xs)[:, :r_h], ddtx_all)
            dd_all = jnp.where(lane_r == r, _row_sums(dyv * xs)[:, :r_h], dd_all)
        dc_ref[...] = dc_acc + _nn(dcb.astype(BF16), bm16)
        db_ref[...] = db_acc + _nn(dcbt.astype(BF16), cm16)
        upper = (row <= col).astype(F32)
        dad = _ones_dot(upper, dacs_all, ones_left=True)
        ddt = dad * a_c + ddtx_all
        ddt_raw = ddt * _sigmoid(raw_c)
        ddt_ref[0] = ddt_raw
        d_bias = jnp.sum(ddt_raw, axis=0, keepdims=True)
        d_alog = jnp.sum(dad * dt_c, axis=0, keepdims=True) * a_c
        d_d = jnp.sum(dd_all, axis=0, keepdims=True)
        hg = jnp.concatenate([d_bias, d_alog, d_d], axis=0)

        @pl.when(step == 0)
        def _():
            hg_ref[0] = hg

        @pl.when(step > 0)
        def _():
            hg_ref[0] += hg

    xs, bms, cms, dtcs, dtrs, hpcs, hprs, prevs = _ssd_specs(t, di, g_n, n_st, rp, ln, r_h, True)
    bout = pl.BlockSpec((ln, n_st), lambda g, c: (nc - 1 - c, g))
    outs, side_outs = _call(
        body,
        grid=(g_n, nc),
        in_specs=[xs, bms, cms, dtcs, dtrs, hpcs, hprs, prevs, xs],
        out_specs=[xs, bout, bout, dtcs, hpcs],
        out_shape=[jax.ShapeDtypeStruct((t, di), F32), jax.ShapeDtypeStruct((t, g_n * n_st), F32),
                   jax.ShapeDtypeStruct((t, g_n * n_st), F32), jax.ShapeDtypeStruct((g_n, t, r_h), F32),
                   jax.ShapeDtypeStruct((g_n, 3, r_h), F32)],
        scratch_shapes=[pltpu.VMEM((rp, n_st), F32)],
        sem=("parallel", "arbitrary"),
        name=name,
        args=(xbc, xbc, xbc, dtc, dtr, hpc, hpr, prev, dy),
        side=side,
    )
    return (*outs, side_outs)


SB_KEYS = 256
SB_QUERIES = (512, 256)
SB_CUTOFF = 110.0
SB_PIECES = 2


def _sb_logits(qs, kv, valid):
    z = _nt(qs, kv)
    nz = -z
    lg = jnp.minimum(nz, 0.0) - jnp.log(1.0 + jnp.exp(jnp.minimum(z, nz)))
    return z + lg, (lg if valid is None else jnp.where(valid, lg, 0.0))


def _sb_iota(tq):
    diff = lax.broadcasted_iota(jnp.int32, (tq, SB_KEYS), 1) - lax.broadcasted_iota(jnp.int32, (tq, SB_KEYS), 0)
    krow = lax.broadcasted_iota(jnp.int32, (SB_KEYS, SB_KEYS), 0)
    kcol = lax.broadcasted_iota(jnp.int32, (SB_KEYS, SB_KEYS), 1)
    return diff, krow, kcol


def _sb_scale(d):
    scale = 1.0 / math.sqrt(d)
    assert math.frexp(scale)[0] == 0.5, "the scale is folded into bf16 queries: it must be a power of two"
    return scale


def _key_rows(j):
    return pl.ds(pl.multiple_of(j * SB_KEYS, SB_KEYS), SB_KEYS)


def _sb_fwd(q, k, v, n_heads, *, side=None, name):
    t, hd = q.shape
    d = hd // n_heads
    hpt = LANES // d
    assert hpt * d == LANES and n_heads % hpt == 0
    tq = _tile(t, SB_QUERIES)
    nq = t // tq
    kpq = tq // SB_KEYS
    scale = _sb_scale(d)

    def body(q_ref, k_ref, v_ref, o_ref, lt_ref, first_ref):
        i = pl.program_id(1)
        diff, krow, kcol = _sb_iota(tq)
        later = (krow > kcol).astype(F32)
        nb = i * kpq
        for hh in range(hpt):
            sl = slice(hh * d, (hh + 1) * d)
            qs = (q_ref[:, sl].astype(F32) * scale).astype(BF16)

            def block(j, carry, valid, qs=qs, sl=sl):
                acc, cl = carry
                rows = _key_rows(j)
                ls, lg = _sb_logits(qs, k_ref[rows, sl], valid)
                cs = _ones_dot(later, lg, ones_left=False, pieces=SB_PIECES)
                att = jnp.exp(ls + (cs + cl))
                if valid is not None:
                    att = jnp.where(valid, att, 0.0)
                acc = acc + _nn(att.astype(BF16), v_ref[rows, sl])
                return acc, cl + (cs[:, 0:1] + lg[:, 0:1])

            carry = (jnp.zeros((tq, d), F32), jnp.zeros((tq, 1), F32))
            for m in range(kpq - 1, -1, -1):
                carry = block(i * kpq + m, carry, diff < -m * SB_KEYS)

            def more(st):
                s, _, cl = st
                return jnp.logical_and(s < nb, jnp.max(cl) > -SB_CUTOFF)

            def step(st, block=block):
                s, acc, cl = st
                acc, cl = block(nb - 1 - s, (acc, cl), None)
                return s + 1, acc, cl

            walked, acc, cl = lax.while_loop(more, step, (jnp.int32(0),) + carry)
            o_ref[:, sl] = acc.astype(o_ref.dtype)
            lt_ref[hh] = cl
            first_ref[pl.program_id(0) * hpt + hh, i] = nb - walked

    qs = pl.BlockSpec((tq, LANES), lambda p, i: (i, p))
    ls = pl.BlockSpec((hpt, tq, 1), lambda p, i: (p, i, 0))
    ks = pl.BlockSpec((t, LANES), lambda p, i: (0, p))
    outs, side_outs = _call(
        body,
        grid=(n_heads // hpt, nq),
        in_specs=[qs, ks, ks],
        out_specs=[qs, ls, pl.BlockSpec(memory_space=pltpu.SMEM)],
        out_shape=[jax.ShapeDtypeStruct((t, hd), BF16), jax.ShapeDtypeStruct((n_heads, t, 1), F32),
                   jax.ShapeDtypeStruct((n_heads, nq), jnp.int32)],
        sem=("arbitrary", "arbitrary"),
        name=name,
        args=(q, k, v),
        side=side,
    )
    return (*outs, side_outs)


def _sb_bwd(q, k, v, lt, first, do, n_heads, *, name):
    t, hd = q.shape
    d = hd // n_heads
    hpt = LANES // d
    tq = _tile(t, SB_QUERIES)
    nq = t // tq
    kpq = tq // SB_KEYS
    scale = _sb_scale(d)
    last = SB_KEYS - 1

    def body(q_ref, k_ref, v_ref, lt_ref, first_ref, do_ref, dq_ref, dk_ref, dv_ref, dk_acc, dv_acc):
        i = pl.program_id(1)

        @pl.when(i == 0)
        def _():
            dk_acc[...] = jnp.zeros_like(dk_acc)
            dv_acc[...] = jnp.zeros_like(dv_acc)

        diff, krow, kcol = _sb_iota(tq)
        upto = (krow <= kcol).astype(F32)
        before = (krow < kcol).astype(F32)
        zero = jnp.zeros((tq, 1), F32)
        nb = i * kpq
        for hh in range(hpt):
            sl = slice(hh * d, (hh + 1) * d)
            qs = (q_ref[:, sl].astype(F32) * scale).astype(BF16)
            do16 = do_ref[:, sl].astype(BF16)
            ltot = lt_ref[hh]

            def block(j, carry, valid, r0=0, qs=qs, do16=do16, ltot=ltot, sl=sl):
                dq, pl_sum, pg_sum = carry
                rows = _key_rows(j)
                kv = k_ref[rows, sl]
                vv = v_ref[rows, sl]
                ls, lg = _sb_logits(qs[r0:], kv, valid)
                pre = _ones_dot(upto, lg, ones_left=False, pieces=SB_PIECES)
                att = jnp.exp(ls + (ltot[r0:] - (pre + pl_sum)))
                if valid is not None:
                    att = jnp.where(valid, att, 0.0)
                g = att * _nt(do16[r0:], vv)
                gpre = _ones_dot(before, g, ones_left=False, pieces=SB_PIECES)
                sig = jnp.exp(ls)
                dz16 = (g - sig * (g + (gpre + pg_sum))).astype(BF16)
                if valid is not None:
                    dz16 = jnp.where(valid, dz16, jnp.zeros_like(dz16))
                dq = dq + _nn(dz16, kv)
                dk_acc[rows, sl] += _tn(dz16, qs[r0:])
                dv_acc[rows, sl] += _tn(att.astype(BF16), do16[r0:])
                return dq, pl_sum + pre[:, last:], pg_sum + (gpre[:, last:] + g[:, last:])

            start = jnp.clip(first_ref[pl.program_id(0) * hpt + hh, i], 0, nb)
            carry = lax.fori_loop(start, nb, lambda j, cr, block=block: block(j, cr, None),
                                  (jnp.zeros((tq, d), F32), zero, zero))
            for m in range(kpq):
                r0 = m * SB_KEYS
                sub = block(nb + m, tuple(a[r0:] for a in carry), diff[r0:] < -r0, r0)
                carry = tuple(jnp.concatenate([a[:r0], s], axis=0) if r0 else s for a, s in zip(carry, sub))
            dq_ref[:, sl] = (carry[0] * scale).astype(dq_ref.dtype)

        @pl.when(i == nq - 1)
        def _():
            dk_ref[...] = dk_acc[...].astype(dk_ref.dtype)
            dv_ref[...] = dv_acc[...].astype(dv_ref.dtype)

    qs = pl.BlockSpec((tq, LANES), lambda p, i: (i, p))
    ls = pl.BlockSpec((hpt, tq, 1), lambda p, i: (p, i, 0))
    ks = pl.BlockSpec((t, LANES), lambda p, i: (0, p))
    full = jax.ShapeDtypeStruct((t, hd), BF16)
    return pl.pallas_call(
        body,
        grid=(n_heads // hpt, nq),
        in_specs=[qs, ks, ks, ls, pl.BlockSpec(memory_space=pltpu.SMEM), qs],
        out_specs=[qs, ks, ks],
        out_shape=[full, full, full],
        scratch_shapes=[pltpu.VMEM((t, LANES), F32), pltpu.VMEM((t, LANES), F32)],
        compiler_params=_cparams(("arbitrary", "arbitrary")),
        name=name,
    )(q, k, v, lt, first, do)


def _row_tile(rows, cols):
    fits = [r for r in range(16, rows + 1, 16) if rows % r == 0 and r * cols * 4 <= ADAM_BLOCK_BYTES]
    return max(fits) if fits else rows


def _sum_leading(x, *, name):
    n, rows, cols = x.shape
    tr = _row_tile(rows, cols)

    def body(x_ref, o_ref):
        acc = x_ref[0].astype(F32)
        for q in range(1, n):
            acc = acc + x_ref[q].astype(F32)
        o_ref[...] = acc

    return pl.pallas_call(
        body,
        grid=(rows // tr,),
        in_specs=[pl.BlockSpec((n, tr, cols), lambda i: (0, i, 0))],
        out_specs=pl.BlockSpec((tr, cols), lambda i: (i, 0)),
        out_shape=jax.ShapeDtypeStruct((rows, cols), F32),
        compiler_params=_cparams(("parallel",)),
        name=name,
    )(x)


def _pair_add(g4h, recv, c, *, out_dtype, name):
    n, _, rows, cols = g4h.shape
    tr = _row_tile(rows, cols)

    def body(c_ref, g_ref, r_ref, o_ref):
        o_ref[...] = (g_ref[...] + r_ref[...]).astype(o_ref.dtype)

    blk = pl.BlockSpec((1, tr, cols), lambda q, i, c_ref: (q, i, 0))
    return pl.pallas_call(
        body,
        grid_spec=pltpu.PrefetchScalarGridSpec(
            num_scalar_prefetch=1,
            grid=(n, rows // tr),
            in_specs=[pl.BlockSpec((1, None, tr, cols), lambda q, i, c_ref: (q, c_ref[0], i, 0)), blk],
            out_specs=blk),
        out_shape=jax.ShapeDtypeStruct((n, rows, cols), out_dtype),
        compiler_params=_cparams(("parallel", "parallel")),
        name=name,
    )(c.reshape(1).astype(jnp.int32), g4h, recv)


ANY = pl.BlockSpec(memory_space=pl.ANY)


def _other_chips(x, y):
    return [(1 - x, y), (x, 1 - y), (1 - x, 1 - y)]


def _gather_chips(shard, *, name):
    def body(x_ref, o_ref, send_sems, recv_sems, local_sem):
        x, y, c = lax.axis_index("x"), lax.axis_index("y"), lax.axis_index("c")
        me = 2 * x + y
        mine = pltpu.make_async_copy(x_ref, o_ref.at[me], local_sem)
        mine.start()
        chips = _other_chips(x, y)
        sends = [pltpu.make_async_remote_copy(src_ref=x_ref, dst_ref=o_ref.at[me], send_sem=send_sems.at[q],
                                              recv_sem=recv_sems.at[q], device_id=(px, py, c), device_id_type=MESH)
                 for q, (px, py) in enumerate(chips)]
        for cp in sends:
            cp.start()
        for q, (px, py) in enumerate(chips):
            pltpu.make_async_remote_copy(src_ref=x_ref, dst_ref=o_ref.at[2 * px + py], send_sem=send_sems.at[q],
                                         recv_sem=recv_sems.at[q], device_id=(px, py, c), device_id_type=MESH).wait_recv()
        for cp in sends:
            cp.wait_send()
        mine.wait()

    return pl.pallas_call(
        body,
        in_specs=[ANY],
        out_specs=ANY,
        out_shape=jax.ShapeDtypeStruct((4,) + shard.shape, shard.dtype),
        scratch_shapes=[pltpu.SemaphoreType.DMA((3,)), pltpu.SemaphoreType.DMA((3,)), pltpu.SemaphoreType.DMA],
        compiler_params=pltpu.CompilerParams(has_side_effects=True),
        name=name,
    )(shard)


def _comm_call(body, ins, out_shapes, n_sems, name):
    n = len(ins)

    def wrapped(*refs):
        body(refs[:n], refs[n:n + len(out_shapes)], refs[-2], refs[-1])

    return pl.pallas_call(
        wrapped,
        in_specs=[ANY] * n,
        out_specs=[ANY] * len(out_shapes),
        out_shape=out_shapes,
        scratch_shapes=[pltpu.SemaphoreType.DMA((n_sems,)), pltpu.SemaphoreType.DMA((n_sems,))],
        compiler_params=pltpu.CompilerParams(has_side_effects=True),
        name=name,
    )(*ins)


def _remote(send_sems, recv_sems, q, src, dst, to):
    return pltpu.make_async_remote_copy(src_ref=src, dst_ref=dst, send_sem=send_sems.at[q], recv_sem=recv_sems.at[q],
                                        device_id=to, device_id_type=MESH)


def _scatter_job(parts):
    def sends(ins, outs, send_sems, recv_sems):
        x, y, c = lax.axis_index("x"), lax.axis_index("y"), lax.axis_index("c")
        return [_remote(send_sems, recv_sems, 3 * i + q, p.at[2 * px + py], o.at[2 * x + y], (px, py, c))
                for i, (p, o) in enumerate(zip(ins, outs)) for q, (px, py) in enumerate(_other_chips(x, y))]

    def start(ins, outs, send_sems, recv_sems):
        for cp in sends(ins, outs, send_sems, recv_sems):
            cp.start()

    def finish(ins, outs, send_sems, recv_sems):
        x, y, c = lax.axis_index("x"), lax.axis_index("y"), lax.axis_index("c")
        for i, (p, o) in enumerate(zip(ins, outs)):
            for q, (px, py) in enumerate(_other_chips(x, y)):
                _remote(send_sems, recv_sems, 3 * i + q, p.at[2 * x + y], o.at[2 * px + py], (px, py, c)).wait_recv()
        for cp in sends(ins, outs, send_sems, recv_sems):
            cp.wait_send()

    return _SideJob(parts, [jax.ShapeDtypeStruct(p.shape, p.dtype) for p in parts], 3 * len(parts), start, finish)


def _run_job(job, name):
    return _comm_call(lambda *refs: (job.start(*refs), job.finish(*refs)), job.ins, job.out_shapes, job.n_sems, name)


def _gather_job(shards):
    def sends(ins, outs, send_sems, recv_sems):
        x, y, c = lax.axis_index("x"), lax.axis_index("y"), lax.axis_index("c")
        return [_remote(send_sems, recv_sems, 6 * i + q, s.at[c], o.at[2 * x + y, c], (px, py, c))
                for i, (s, o) in enumerate(zip(ins, outs)) for q, (px, py) in enumerate(_other_chips(x, y))]

    def start(ins, outs, send_sems, recv_sems):
        for cp in sends(ins, outs, send_sems, recv_sems):
            cp.start()

    def finish(ins, outs, send_sems, recv_sems):
        x, y, c = lax.axis_index("x"), lax.axis_index("y"), lax.axis_index("c")
        sibling = (x, y, 1 - c)
        chips = _other_chips(x, y)
        copy = lambda q, src, dst, to: _remote(send_sems, recv_sems, q, src, dst, to)
        passed = []
        for i, (s, o) in enumerate(zip(ins, outs)):
            for q, (px, py) in enumerate(chips):
                slot = o.at[2 * px + py, c]
                copy(6 * i + q, s.at[c], slot, (px, py, c)).wait_recv()
                passed.append(copy(6 * i + 3 + q, slot, slot, sibling))
                passed[-1].start()
        for i, (s, o) in enumerate(zip(ins, outs)):
            for q, (px, py) in enumerate(chips):
                copy(6 * i + 3 + q, s.at[1 - c], o.at[2 * px + py, 1 - c], sibling).wait_recv()
        for cp in sends(ins, outs, send_sems, recv_sems) + passed:
            cp.wait_send()

    return _SideJob(shards, [jax.ShapeDtypeStruct((N_CHIPS,) + s.shape, s.dtype) for s in shards], 6 * len(shards),
                    start, finish)


def _swap_job(gs):
    def copies(ins, outs, send_sems, recv_sems):
        x, y, c = lax.axis_index("x"), lax.axis_index("y"), lax.axis_index("c")
        return [_remote(send_sems, recv_sems, i, g.at[pl.ds(0, g.shape[0]), 1 - c], o, (x, y, 1 - c))
                for i, (g, o) in enumerate(zip(ins, outs))]

    def start(*refs):
        for cp in copies(*refs):
            cp.start()

    def finish(*refs):
        for cp in copies(*refs):
            cp.wait()

    return _SideJob(gs, [jax.ShapeDtypeStruct((g.shape[0],) + g.shape[2:], g.dtype) for g in gs], len(gs), start, finish)


def _join_halves(halves, *, name):
    def body(ins, outs, send_sems, recv_sems):
        x, y, c = lax.axis_index("x"), lax.axis_index("y"), lax.axis_index("c")
        sibling = (x, y, 1 - c)
        sends = [_remote(send_sems, recv_sems, i, h, o.at[c], sibling) for i, (h, o) in enumerate(zip(ins, outs))]
        for cp in sends:
            cp.start()
        for i, (h, o) in enumerate(zip(ins, outs)):
            _remote(send_sems, recv_sems, i, h, o.at[1 - c], sibling).wait_recv()
        for cp in sends:
            cp.wait_send()

    return _comm_call(body, halves, [jax.ShapeDtypeStruct((2,) + h.shape, h.dtype) for h in halves], len(halves), name)


def _gather_all(v, *, name):
    def body(v_ref, o_ref, send_sems, recv_sems, local_sem):
        x, y, c = lax.axis_index("x"), lax.axis_index("y"), lax.axis_index("c")
        me = 4 * x + 2 * y + c
        mine = pltpu.make_async_copy(v_ref, o_ref.at[me], local_sem)
        mine.start()
        peers = [(x ^ (q >> 2 & 1), y ^ (q >> 1 & 1), c ^ (q & 1)) for q in range(1, 8)]
        sends = [pltpu.make_async_remote_copy(src_ref=v_ref, dst_ref=o_ref.at[me], send_sem=send_sems.at[q],
                                              recv_sem=recv_sems.at[q], device_id=peer, device_id_type=MESH)
                 for q, peer in enumerate(peers)]
        for cp in sends:
            cp.start()
        for q, (px, py, pc) in enumerate(peers):
            pltpu.make_async_remote_copy(src_ref=v_ref, dst_ref=o_ref.at[4 * px + 2 * py + pc], send_sem=send_sems.at[q],
                                         recv_sem=recv_sems.at[q], device_id=(px, py, pc), device_id_type=MESH).wait_recv()
        for cp in sends:
            cp.wait_send()
        mine.wait()

    return pl.pallas_call(
        body,
        in_specs=[ANY],
        out_specs=ANY,
        out_shape=jax.ShapeDtypeStruct((8,) + v.shape, v.dtype),
        scratch_shapes=[pltpu.SemaphoreType.DMA((7,)), pltpu.SemaphoreType.DMA((7,)), pltpu.SemaphoreType.DMA],
        compiler_params=pltpu.CompilerParams(has_side_effects=True),
        name=name,
    )(v)


WEIGHTS = ['ssm_norm_w', 'ssm_in_w', 'ssm_conv_w', 'ssm_conv_b', 'ssm_dt_bias', 'ssm_a_log', 'ssm_d',
           'ssm_gate_norm_w', 'ssm_out_w', 'kv_norm_w', 'w_k', 'w_v', 'attn_norm_w', 'w_q', 'w_o',
           'ffn_norm_w', 'ffn_up_w', 'ffn_conv_w', 'ffn_conv_b', 'ffn_down_w', 'final_norm_w']
SHARD_AXIS = {'ssm_norm_w': 1, 'ssm_in_w': 2, 'ssm_conv_w': 2, 'ssm_conv_b': 1, 'ssm_gate_norm_w': 1,
              'ssm_out_w': 1, 'w_k': 0, 'w_v': 0, 'w_q': 1, 'w_o': 1, 'ffn_up_w': 2, 'ffn_conv_w': 2,
              'ffn_down_w': 1}
BIG = ['ssm_in_w', 'ssm_out_w', 'w_k', 'w_v', 'w_q', 'w_o', 'ffn_up_w', 'ffn_down_w']
SMALL = [n for n in WEIGHTS if n in SHARD_AXIS and n not in BIG]
REPLICATED = [n for n in WEIGHTS if n not in SHARD_AXIS]
STACKED = ['ffn_up_w', 'ffn_down_w']
N_CHIPS = 4


PACK_ROWS = 16


def _piece_rows(n):
    return -(-n // (PACK_ROWS * LANES)) * PACK_ROWS


def _pack(arrs, dtype, row_mult):
    lead = arrs[0].shape[:-1]
    pieces, total = [], 0
    for a in arrs:
        n = a.shape[-1]
        rows = _piece_rows(n)
        a = a.astype(dtype)
        if rows * LANES != n:
            a = jnp.pad(a, [(0, 0)] * len(lead) + [(0, rows * LANES - n)])
        pieces.append(a.reshape(lead + (rows, LANES)))
        total += rows
    extra = -total % row_mult
    if extra:
        pieces.append(jnp.zeros(lead + (extra, LANES), dtype))
    return jnp.concatenate(pieces, axis=len(lead))


def _unpack(buf, shapes):
    lead = buf.shape[:-2]
    out, off = [], 0
    for shp in shapes:
        n = math.prod(shp)
        rows = _piece_rows(n)
        piece = lax.slice_in_dim(buf, off, off + rows, axis=len(lead)).reshape(lead + (rows * LANES,))
        out.append(piece[..., :n].reshape(lead + tuple(shp)))
        off += rows
    return out


def _set_slot(buf, piece, index):
    return lax.dynamic_update_slice_in_dim(buf, piece[None], index, axis=0)


def _from_shards(stacked, axis):
    return jnp.concatenate([stacked[j] for j in range(N_CHIPS)], axis=axis)


def _ffn_fwd(h, norm_w, w_up, conv_w, conv_b, w_down, tag, side=None):
    u = _rmsnorm_fwd(h, norm_w, name=f"ffn{tag}_norm")
    hid = _matmul(u, w_up, name=f"ffn{tag}_up")
    act, side_outs = _conv_glu_fwd(hid, conv_w, conv_b, side=side, name=f"ffn{tag}_glu")
    out = _matmul(act, w_down, add=h, name=f"ffn{tag}_down")
    return out, (u, hid, act), side_outs


def _ffn_bwd(h, saved, dout, norm_w, w_up, conv_w, conv_b, w_down, tag, side=None):
    u, hid, act = saved
    dact = _matmul(dout, w_down, tb=True, name=f"ffn{tag}_down_dx")
    dw_down = _matmul(act, dout, ta=True, name=f"ffn{tag}_down_dw")
    dhid, dwg, dwv, dbg, dbv, side_outs = _conv_glu_bwd(hid, conv_w, conv_b, dact, side=side, name=f"ffn{tag}_glu_bwd")
    du = _matmul(dhid, w_up, tb=True, name=f"ffn{tag}_up_dx")
    dw_up = _matmul(u, dhid, ta=True, out_parts=N_CHIPS, name=f"ffn{tag}_up_dw")
    dh, (dnorm,) = _rmsnorm_bwd(h, [(du, norm_w)], dout, name=f"ffn{tag}_norm_bwd")
    return dh, dict(norm=dnorm[0], up=dw_up, conv_w=jnp.concatenate([dwg, dwv], axis=1),
                    conv_b=jnp.concatenate([dbg, dbv], axis=1)[0], down=dw_down), side_outs


class _Pieces:
    def __init__(self, local):
        self.c = lax.axis_index("c")
        self.chip = 2 * lax.axis_index("x") + lax.axis_index("y")
        self.shape, self.s16 = {}, {}
        for n in BIG:
            blk = local[n]
            layers = [(n, l, blk[l]) for l in range(blk.shape[0])] if n in STACKED else [(n, None, blk.reshape(blk.shape[-2:]))]
            for name, l, p in layers:
                self.shape[name, l] = p.shape
                self.s16[name, l] = p.astype(BF16).reshape(2, p.shape[0] // 2, p.shape[1])

    def gather_job(self, keys):
        return _gather_job([self.s16[k] for k in keys])

    def weights(self, keys, gathered):
        out = []
        for k, g in zip(keys, gathered):
            r, cc = self.shape[k]
            by_chip = _set_slot(g, self.s16[k], self.chip).reshape(N_CHIPS, r, cc)
            if k[0] == 'ssm_in_w':
                by_chip = by_chip.transpose(1, 0, 2).reshape(r, N_CHIPS * cc)
            elif k[0] != 'ffn_up_w':
                by_chip = by_chip.reshape(N_CHIPS * r, cc)
            out.append(by_chip)
        return out

    def by_halves(self, keys, grads):
        gs = []
        for k, g in zip(keys, grads):
            r, cc = self.shape[k]
            if k[0] == 'ssm_in_w':
                g = g.reshape(r, N_CHIPS, cc).transpose(1, 0, 2)
            gs.append(g.reshape(N_CHIPS, 2, r // 2, cc))
        return gs

    def pair_sums(self, gs, recv, tag):
        return [_pair_add(g, rv, self.c, out_dtype=BF16, name=f"rs_pair_add_{tag}{i}") for i, (g, rv) in enumerate(zip(gs, recv))]

    def chip_sums(self, pairs, scattered, tag):
        return [_sum_leading(_set_slot(s, lax.dynamic_index_in_dim(p, self.chip, axis=0, keepdims=False), self.chip),
                             name=f"rs_chip_sum_{tag}{i}") for i, (s, p) in enumerate(zip(scattered, pairs))]

    def shards(self, keys, halves):
        joined = _join_halves(halves, name="rs_half_join")
        return {k: _set_slot(j, h, self.c).reshape(self.shape[k]) for k, h, j in zip(keys, halves, joined)}


def _step(x, target, w, pieces):
    t = x.shape[0]
    g_n, heads = SSM_GROUPS, SSM_HEADS
    r_h = heads // g_n
    di = D_INNER
    zx_cols = di + CONV_DIM
    k_in = [('ssm_in_w', None)]
    k_ffn0 = [('ssm_out_w', None), ('ffn_up_w', 0), ('ffn_down_w', 0)]
    k_qkv = [('w_k', None), ('w_v', None), ('w_q', None)]
    k_late = [('w_o', None), ('ffn_up_w', 1), ('ffn_down_w', 1)]
    (w_in,) = pieces.weights(k_in, _run_job(pieces.gather_job(k_in), "gather_ssm_in"))
    w_zx = w_in[:, :zx_cols]
    w_dt = jnp.pad(w_in[:, zx_cols:], ((0, 0), (0, LANES - heads)))
    conv_w, conv_b = w['ssm_conv_w'][0], w['ssm_conv_b'][0]
    hp = jnp.stack([w['ssm_dt_bias'][0], w['ssm_a_log'][0], w['ssm_d'][0]], axis=0).reshape(3, g_n, r_h)
    hpc, hpr = hp.transpose(1, 0, 2), hp.transpose(1, 2, 0)

    h0 = x
    u0 = _rmsnorm_fwd(h0, w['ssm_norm_w'][0], name="ssm_norm")
    zx = _matmul(u0, w_zx, name="ssm_in_zx")
    dt_raw = _matmul(u0, w_dt, name="ssm_in_dt")[:, :heads]
    dtg = dt_raw.reshape(t, g_n, r_h)
    dtc, dtr = dtg.transpose(1, 0, 2), dtg.transpose(1, 2, 0)
    xbc = _conv_silu_fwd(zx, conv_w, conv_b, x_off=di, name="ssm_conv")
    y, prev, got = _ssd_fwd(xbc, dtc, dtr, hpc, hpr, side=pieces.gather_job(k_ffn0), name="ssd_fwd")
    w_out, w_up0, w_down0 = pieces.weights(k_ffn0, got)
    yn = _gate_norm_fwd(y, zx, w['ssm_gate_norm_w'][0], name="ssm_gate_norm")
    h1 = _matmul(yn, w_out, add=h0, name="ssm_out")
    h2, ffn0, got = _ffn_fwd(h1, w['ffn_norm_w'][0], w_up0, w['ffn_conv_w'][0], w['ffn_conv_b'][0], w_down0, 0,
                             side=pieces.gather_job(k_qkv))
    w_k, w_v, w_q = pieces.weights(k_qkv, got)
    hk = _rmsnorm_fwd(h2, w['kv_norm_w'], name="kv_norm")
    qn = _rmsnorm_fwd(h2, w['attn_norm_w'][0], name="attn_norm")
    k2 = _matmul(hk, w_k, out_dtype=BF16, name="attn_k")
    v2 = _matmul(hk, w_v, out_dtype=BF16, name="attn_v")
    q2 = _matmul(qn, w_q, out_dtype=BF16, name="attn_q")
    o2, lt, first, got = _sb_fwd(q2, k2, v2, SB_HEADS, side=pieces.gather_job(k_late), name="sb_fwd")
    w_o, w_up1, w_down1 = pieces.weights(k_late, got)
    h3 =_matmul(o2, w_o, add=h2, name="attn_o")
    h4, ffn1, _ = _ffn_fwd(h3, w['ffn_norm_w'][1], w_up1, w['ffn_conv_w'][1], w['ffn_conv_b'][1], w_down1, 1)
    loss_p, dh4, d_final = _loss_head(h4, w['final_norm_w'], target, name="loss_head")

    dh3, g1, _ = _ffn_bwd(h3, ffn1, dh4, w['ffn_norm_w'][1], w_up1, w['ffn_conv_w'][1], w['ffn_conv_b'][1], w_down1, 1)
    do2 = _matmul(dh3, w_o, tb=True, out_dtype=BF16, name="attn_o_dx")
    dw_o = _matmul(o2, dh3, ta=True, name="attn_o_dw")
    dq2, dk2, dv2 = _sb_bwd(q2, k2, v2, lt, first, do2, SB_HEADS, name="sb_bwd")
    dqn = _matmul(dq2, w_q, tb=True, name="attn_q_dx")
    dw_q = _matmul(qn, dq2, ta=True, name="attn_q_dw")
    dhk = _matmul(dk2, w_k, tb=True, name="attn_k_dx")
    dhk = _matmul(dv2, w_v, tb=True, add=dhk, name="attn_v_dx")
    dw_k = _matmul(hk, dk2, ta=True, name="attn_k_dw")
    dw_v = _matmul(hk, dv2, ta=True, name="attn_v_dw")
    dh2, (d_attn_norm, d_kv_norm) = _rmsnorm_bwd(h2, [(dqn, w['attn_norm_w'][0]), (dhk, w['kv_norm_w'])], dh3,
                                                 name="attn_norms_bwd")
    gs_late = pieces.by_halves(k_qkv + k_late, [dw_k, dw_v, dw_q, dw_o, g1['up'], g1['down']])
    dh1, g0, recv = _ffn_bwd(h1, ffn0, dh2, w['ffn_norm_w'][0], w_up0, w['ffn_conv_w'][0], w['ffn_conv_b'][0], w_down0, 0,
                             side=_swap_job(gs_late))
    pairs_late = pieces.pair_sums(gs_late, recv, "a")
    dyn = _matmul(dh1, w_out, tb=True, name="ssm_out_dx")
    dw_out = _matmul(yn, dh1, ta=True, name="ssm_out_dw")
    k_done = k_qkv + k_late + k_ffn0
    gs_ffn0 = pieces.by_halves(k_ffn0, [dw_out, g0['up'], g0['down']])
    dy, dz, d_gate, recv = _gate_norm_bwd(y, zx, w['ssm_gate_norm_w'][0], dyn, side=_swap_job(gs_ffn0),
                                          name="ssm_gate_norm_bwd")
    pairs_done = pairs_late + pieces.pair_sums(gs_ffn0, recv, "c")
    dxs, dbm, dcm, ddt_g, hg, scattered_done = _ssd_bwd(xbc, dtc, dtr, hpc, hpr, prev, dy,
                                                        side=_scatter_job(pairs_done), name="ssd_bwd")
    dzx, d_conv_w, d_conv_b = _conv_silu_bwd(zx, conv_w, conv_b, [dxs, dbm, dcm], x_off=di, into=dz, name="ssm_conv_bwd")
    ddt = jnp.pad(ddt_g.transpose(1, 0, 2).reshape(t, heads), ((0, 0), (0, LANES - heads)))
    du0 = _matmul(dzx, w_zx, tb=True, name="ssm_in_zx_dx")
    du0 = _matmul(ddt, w_dt, tb=True, add=du0, name="ssm_in_dt_dx")
    dw_in = jnp.concatenate([_matmul(u0, dzx, ta=True, name="ssm_in_zx_dw"),
                             _matmul(u0, ddt, ta=True, name="ssm_in_dt_dw")[:, :heads]], axis=1)
    dx, (d_ssm_norm,) = _rmsnorm_bwd(h0, [(du0, w['ssm_norm_w'][0])], dh1, name="ssm_norm_bwd")

    gs_in = pieces.by_halves(k_in, [dw_in])
    pairs_in = pieces.pair_sums(gs_in, _run_job(_swap_job(gs_in), "rs_pair_swap_b"), "b")
    scattered_in = _run_job(_scatter_job(pairs_in), "rs_chip_scatter_b")
    halves = pieces.chip_sums(pairs_done, scattered_done, "a") + pieces.chip_sums(pairs_in, scattered_in, "b")
    big_grads = pieces.shards(k_done + k_in, halves)

    hgr = hg.transpose(1, 0, 2).reshape(3, heads)
    grads = {
        'ssm_norm_w': d_ssm_norm, 'ssm_conv_w': d_conv_w[None], 'ssm_conv_b': d_conv_b,
        'ssm_dt_bias': hgr[0:1], 'ssm_a_log': hgr[1:2], 'ssm_d': hgr[2:3], 'ssm_gate_norm_w': d_gate,
        'kv_norm_w': d_kv_norm[0], 'attn_norm_w': d_attn_norm, 'ffn_norm_w': jnp.stack([g0['norm'], g1['norm']]),
        'ffn_conv_w': jnp.stack([g0['conv_w'], g1['conv_w']]), 'ffn_conv_b': jnp.stack([g0['conv_b'], g1['conv_b']]),
        'final_norm_w': d_final[0],
    }
    return loss_p, dx, grads, big_grads


def kernel(x, ssm_norm_w, ssm_in_w, ssm_conv_w, ssm_conv_b, ssm_dt_bias, ssm_a_log, ssm_d, ssm_gate_norm_w, ssm_out_w, kv_norm_w, w_k, w_v, attn_norm_w, w_q, w_o, ffn_norm_w, ffn_up_w, ffn_conv_w, ffn_conv_b, ffn_down_w, final_norm_w, loss_target, m_ssm_norm_w, m_ssm_in_w, m_ssm_conv_w, m_ssm_conv_b, m_ssm_dt_bias, m_ssm_a_log, m_ssm_d, m_ssm_gate_norm_w, m_ssm_out_w, m_kv_norm_w, m_w_k, m_w_v, m_attn_norm_w, m_w_q, m_w_o, m_ffn_norm_w, m_ffn_up_w, m_ffn_conv_w, m_ffn_conv_b, m_ffn_down_w, m_final_norm_w, v_ssm_norm_w, v_ssm_in_w, v_ssm_conv_w, v_ssm_conv_b, v_ssm_dt_bias, v_ssm_a_log, v_ssm_d, v_ssm_gate_norm_w, v_ssm_out_w, v_kv_norm_w, v_w_k, v_w_v, v_attn_norm_w, v_w_q, v_w_o, v_ffn_norm_w, v_ffn_up_w, v_ffn_conv_w, v_ffn_conv_b, v_ffn_down_w, v_final_norm_w):
    args = (ssm_norm_w, ssm_in_w, ssm_conv_w, ssm_conv_b, ssm_dt_bias, ssm_a_log, ssm_d, ssm_gate_norm_w, ssm_out_w, kv_norm_w, w_k, w_v, attn_norm_w, w_q, w_o, ffn_norm_w, ffn_up_w, ffn_conv_w, ffn_conv_b, ffn_down_w, final_norm_w)
    moms = (m_ssm_norm_w, m_ssm_in_w, m_ssm_conv_w, m_ssm_conv_b, m_ssm_dt_bias, m_ssm_a_log, m_ssm_d, m_ssm_gate_norm_w, m_ssm_out_w, m_kv_norm_w, m_w_k, m_w_v, m_attn_norm_w, m_w_q, m_w_o, m_ffn_norm_w, m_ffn_up_w, m_ffn_conv_w, m_ffn_conv_b, m_ffn_down_w, m_final_norm_w)
    vels = (v_ssm_norm_w, v_ssm_in_w, v_ssm_conv_w, v_ssm_conv_b, v_ssm_dt_bias, v_ssm_a_log, v_ssm_d, v_ssm_gate_norm_w, v_ssm_out_w, v_kv_norm_w, v_w_k, v_w_v, v_attn_norm_w, v_w_q, v_w_o, v_ffn_norm_w, v_ffn_up_w, v_ffn_conv_w, v_ffn_conv_b, v_ffn_down_w, v_final_norm_w)
    local = dict(zip(WEIGHTS, args))
    m_in = dict(zip(WEIGHTS, moms))
    v_in = dict(zip(WEIGHTS, vels))
    chip = 2 * lax.axis_index("x") + lax.axis_index("y")

    full = {n: local[n] for n in REPLICATED}
    small32 = _gather_chips(_pack([local[n].reshape(-1) for n in SMALL], F32, 8), name="gather_small")
    for n, st in zip(SMALL, _unpack(small32, [local[n].shape for n in SMALL])):
        full[n] = _from_shards(st, SHARD_AXIS[n])

    pieces = _Pieces(local)
    loss_p, dx, grads, big_grads = _step(x[0], loss_target[0], full, pieces)
    gshard = {}
    for n in BIG:
        if n in STACKED:
            gshard[n] = [big_grads[n, l] for l in range(local[n].shape[0])]
        else:
            gshard[n] = big_grads[n, None].reshape(local[n].shape)

    small = SMALL + REPLICATED
    rep = _pack([loss_p.reshape(-1)] + [grads[n].reshape(-1) for n in small], F32, 8)
    tot = _sum_leading(_gather_all(rep, name="ar_gather"), name="ar_sum")
    parts = _unpack(tot, [(LANES,)] + [grads[n].shape for n in small])
    loss = jnp.sum(parts[0])
    for n, g in zip(small, parts[1:]):
        if n in SHARD_AXIS:
            size = local[n].shape[SHARD_AXIS[n]]
            g = lax.dynamic_slice_in_dim(g, chip * size, size, axis=SHARD_AXIS[n])
        gshard[n] = g

    grads_out, deltas, new_m, new_v = [], [], [], []
    for n in WEIGHTS:
        if n in STACKED:
            g, d, nm, nv = _adamw_layers(local[n], gshard[n], m_in[n], v_in[n], name=f"adamw_{n}")
        else:
            g = gshard[n]
            d, nm, nv = _adamw(local[n], g, m_in[n], v_in[n], name=f"adamw_{n}")
        grads_out.append(g)
        deltas.append(d)
        new_m.append(nm)
        new_v.append(nv)
    return (loss, dx[None], *grads_out, *deltas, *new_m, *new_v)
```

```python
import functools
import math

import jax
import jax.numpy as jnp
from jax import lax
from jax.experimental import pallas as pl
from jax.experimental.pallas import tpu as pltpu

D_INNER = 2048
SSM_HEAD_DIM = 64
SSM_HEADS = 32
SSM_GROUPS = 4
SSM_STATE = 128
SSM_CHUNK = 128
GN = SSM_GROUPS * SSM_STATE
CONV_DIM = D_INNER + 2 * GN
SB_HEADS = 16
EPS = 1e-6
ADAM_LR = 0.001
ADAM_B1 = 0.9
ADAM_B2 = 0.999
ADAM_EPS = 1e-08
ADAM_WD = 0.01
ADAM_STEP = 10

LANES = 128
SUBLANES = 8
VMEM_LIMIT = 48 * 1024 * 1024
ADAM_BLOCK_BYTES = 2 << 20
F32 = jnp.float32
BF16 = jnp.bfloat16
MESH = pl.DeviceIdType.MESH


def _cparams(sem=None):
    return pltpu.CompilerParams(dimension_semantics=sem, vmem_limit_bytes=VMEM_LIMIT)


class _SideJob:
    def __init__(self, ins, out_shapes, n_sems, start, finish):
        self.ins, self.out_shapes, self.n_sems, self.start, self.finish = ins, out_shapes, n_sems, start, finish


def _call(body, *, grid, in_specs, out_specs, out_shape, scratch_shapes=(), sem, name, args, side=None):
    in_specs, out_specs, out_shape, scratch_shapes = list(in_specs), list(out_specs), list(out_shape), list(scratch_shapes)
    n_in, n_out = len(in_specs), len(out_specs)
    if side is None:
        outs = pl.pallas_call(body, grid=grid, in_specs=in_specs, out_specs=out_specs, out_shape=out_shape,
                              scratch_shapes=scratch_shapes, compiler_params=_cparams(sem), name=name)(*args)
        return list(outs), []
    k_in, k_out = len(side.ins), len(side.out_shapes)

    def wrapped(*refs):
        ins, s_ins = refs[:n_in], refs[n_in:n_in + k_in]
        o0 = n_in + k_in
        outs, s_outs = refs[o0:o0 + n_out], refs[o0 + n_out:o0 + n_out + k_out]
        scratch, send_sems, recv_sems = refs[o0 + n_out + k_out:-2], refs[-2], refs[-1]
        ids = [pl.program_id(a) for a in range(len(grid))]
        first = functools.reduce(jnp.logical_and, [p == 0 for p in ids])
        last = functools.reduce(jnp.logical_and, [p == g - 1 for p, g in zip(ids, grid)])

        @pl.when(first)
        def _():
            side.start(s_ins, s_outs, send_sems, recv_sems)

        body(*ins, *outs, *scratch)

        @pl.when(last)
        def _():
            side.finish(s_ins, s_outs, send_sems, recv_sems)

    outs = pl.pallas_call(
        wrapped, grid=grid, in_specs=in_specs + [ANY] * k_in, out_specs=out_specs + [ANY] * k_out,
        out_shape=out_shape + list(side.out_shapes),
        scratch_shapes=scratch_shapes + [pltpu.SemaphoreType.DMA((side.n_sems,)), pltpu.SemaphoreType.DMA((side.n_sems,))],
        compiler_params=_cparams(tuple("arbitrary" for _ in grid)), name=name)(*args, *side.ins)
    return list(outs[:n_out]), list(outs[n_out:])


def _tile(n, cands):
    for c in cands:
        if n % c == 0:
            return c
    return n


def _nt(a, b):
    return lax.dot_general(a, b, (((1,), (1,)), ((), ())), preferred_element_type=F32)


def _tn(a, b):
    return lax.dot_general(a, b, (((0,), (0,)), ((), ())), preferred_element_type=F32)


def _nn(a, b):
    return jnp.dot(a, b, preferred_element_type=F32)


def _split(x, pieces):
    out = []
    for _ in range(pieces - 1):
        h = x.astype(BF16)
        out.append(h)
        x = x - h.astype(F32)
    out.append(x.astype(BF16))
    return out


def _ones_dot(ones, x, *, ones_left, pieces=3):
    o16 = ones.astype(BF16)
    acc = None
    for piece in _split(x, pieces):
        term = _nn(o16, piece) if ones_left else _nn(piece, o16)
        acc = term if acc is None else acc + term
    return acc


def _row_sums(x, pieces=2):
    return _ones_dot(jnp.ones((x.shape[1], LANES), F32), x, ones_left=False, pieces=pieces)


def _softplus(x):
    return jnp.maximum(x, 0.0) + jnp.log(1.0 + jnp.exp(-jnp.abs(x)))


def _sigmoid(x):
    return 0.5 * jnp.tanh(0.5 * x) + 0.5


MM_TILE_MAX = 1408
MM_VMEM_BUDGET = 40 * 1024 * 1024


def _divisors(n, cap):
    out = [d for d in range(min(cap, n) // LANES * LANES, 0, -LANES) if n % d == 0]
    return out or [n]


def _mm_tiles(m, n, k, a_bytes, b_bytes, o_bytes, add_bytes):
    best = None
    for tm in _divisors(m, MM_TILE_MAX):
        for tn in _divisors(n, MM_TILE_MAX):
            for tk in _divisors(k, MM_TILE_MAX):
                vmem = 2 * (tm * tk * a_bytes + tk * tn * b_bytes + tm * tn * (o_bytes + add_bytes)) + tm * tn * 4
                if vmem > MM_VMEM_BUDGET:
                    continue
                score = (tm * tn * tk, tm * tn)
                if best is None or score > best[0]:
                    best = (score, (tm, tn, tk))
    return best[1]


def _matmul(a, b, *, ta=False, tb=False, add=None, out_dtype=F32, out_parts=1, name):
    a_parts = a.shape[0] if a.ndim == 3 else 1
    b_parts = b.shape[0] if b.ndim == 3 else 1
    assert not (ta and a_parts > 1)
    a2, b2 = a.shape[-2:], b.shape[-2:]
    m, k = (a2[1], a2[0]) if ta else (a2[0], a2[1] * a_parts)
    n, kb = (b2[0], b2[1] * b_parts) if tb else (b2[1] * b_parts, b2[0])
    assert kb == k, (a.shape, b.shape)
    n_unit = math.gcd(n // out_parts, n if tb else b2[1])
    k_unit = math.gcd(k // a_parts, b2[1] if tb else k)
    tm, tn, tk = _mm_tiles(m, n_unit, k_unit, a.dtype.itemsize, b.dtype.itemsize, jnp.dtype(out_dtype).itemsize,
                           0 if add is None else add.dtype.itemsize)
    nk = k // tk
    ka, kbp = (k // a_parts) // tk, (k // b_parts) // tk
    nb, no = (n // b_parts) // tn, (n // out_parts) // tn

    def body(*refs):
        if add is None:
            a_ref, b_ref, o_ref = refs[:3]
            add_ref = None
        else:
            a_ref, b_ref, add_ref, o_ref = refs[:4]
        kk = pl.program_id(2)
        dn = (((0 if ta else 1,), (1 if tb else 0,)), ((), ()))
        prod = lax.dot_general(a_ref[...].astype(BF16), b_ref[...].astype(BF16), dn, preferred_element_type=F32)

        def finish(r):
            if add_ref is not None:
                r = r + add_ref[...].astype(F32)
            o_ref[...] = r.astype(o_ref.dtype)

        if nk == 1:
            finish(prod)
            return
        acc_ref = refs[-1]

        @pl.when(kk == 0)
        def _():
            acc_ref[...] = prod

        @pl.when(jnp.logical_and(kk > 0, kk < nk - 1))
        def _():
            acc_ref[...] += prod

        @pl.when(kk == nk - 1)
        def _():
            finish(acc_ref[...] + prod)

    if ta:
        a_spec = pl.BlockSpec((tk, tm), lambda i, j, kk: (kk, i))
    elif a_parts > 1:
        a_spec = pl.BlockSpec((None, tm, tk), lambda i, j, kk: (kk // ka, i, kk % ka))
    else:
        a_spec = pl.BlockSpec((tm, tk), lambda i, j, kk: (i, kk))
    if b_parts == 1:
        b_spec = pl.BlockSpec((tn, tk), lambda i, j, kk: (j, kk)) if tb else pl.BlockSpec((tk, tn), lambda i, j, kk: (kk, j))
    elif tb:
        b_spec = pl.BlockSpec((None, tn, tk), lambda i, j, kk: (kk // kbp, j, kk % kbp))
    else:
        b_spec = pl.BlockSpec((None, tk, tn), lambda i, j, kk: (j // nb, kk, j % nb))
    if out_parts > 1:
        o_spec = pl.BlockSpec((None, tm, tn), lambda i, j, kk: (j // no, i, j % no))
        o_shape = jax.ShapeDtypeStruct((out_parts, m, n // out_parts), out_dtype)
    else:
        o_spec = pl.BlockSpec((tm, tn), lambda i, j, kk: (i, j))
        o_shape = jax.ShapeDtypeStruct((m, n), out_dtype)
    in_specs = [a_spec, b_spec]
    args = [a, b]
    if add is not None:
        in_specs.append(pl.BlockSpec((tm, tn), lambda i, j, kk: (i, j)))
        args.append(add)
    return pl.pallas_call(
        body,
        grid=(m // tm, n // tn, nk),
        in_specs=in_specs,
        out_specs=o_spec,
        out_shape=o_shape,
        scratch_shapes=[pltpu.VMEM((tm, tn), F32)] if nk > 1 else [],
        compiler_params=_cparams(("parallel", "parallel", "arbitrary")),
        name=name,
    )(*args)


def _rmsnorm_fwd(x, w, *, name):
    t, d = x.shape
    tb = _tile(t, (512, 256, 128))

    def body(x_ref, w_ref, o_ref):
        xv = x_ref[...]
        r = lax.rsqrt(jnp.mean(xv * xv, axis=-1, keepdims=True) + EPS)
        o_ref[...] = (xv * r * w_ref[...]).astype(o_ref.dtype)

    return pl.pallas_call(
        body,
        grid=(t // tb,),
        in_specs=[pl.BlockSpec((tb, d), lambda i: (i, 0)), pl.BlockSpec((1, d), lambda i: (0, 0))],
        out_specs=pl.BlockSpec((tb, d), lambda i: (i, 0)),
        out_shape=jax.ShapeDtypeStruct((t, d), BF16),
        compiler_params=_cparams(("parallel",)),
        name=name,
    )(x, w.reshape(1, d))


def _rmsnorm_bwd(x, dys, dres, *, name):
    t, d = x.shape
    tb = _tile(t, (256, 128))
    nn = len(dys)
    has_res = dres is not None

    def body(*refs):
        x_ref = refs[0]
        dy_refs = refs[1:1 + nn]
        w_refs = refs[1 + nn:1 + 2 * nn]
        pos = 1 + 2 * nn
        res_ref = refs[pos] if has_res else None
        pos += 1 if has_res else 0
        dx_ref = refs[pos]
        dw_refs = refs[pos + 1:pos + 1 + nn]
        i = pl.program_id(0)
        xv = x_ref[...]
        r = lax.rsqrt(jnp.mean(xv * xv, axis=-1, keepdims=True) + EPS)
        xn = xv * r
        dx = res_ref[...] if has_res else jnp.zeros_like(xv)
        for q in range(nn):
            dy = dy_refs[q][...].astype(F32)
            g = dy * w_refs[q][...]
            dx = dx + r * (g - xn * jnp.mean(g * xn, axis=-1, keepdims=True))
            dwp = jnp.sum(dy * xn, axis=0, keepdims=True)

            @pl.when(i == 0)
            def _(q=q, dwp=dwp):
                dw_refs[q][...] = dwp

            @pl.when(i > 0)
            def _(q=q, dwp=dwp):
                dw_refs[q][...] += dwp
        dx_ref[...] = dx

    row = pl.BlockSpec((tb, d), lambda i: (i, 0))
    vec = pl.BlockSpec((1, d), lambda i: (0, 0))
    in_specs = [row] + [row] * nn + [vec] * nn + ([row] if has_res else [])
    args = [x] + [p[0] for p in dys] + [p[1].reshape(1, d) for p in dys] + ([dres] if has_res else [])
    outs = pl.pallas_call(
        body,
        grid=(t // tb,),
        in_specs=in_specs,
        out_specs=[row] + [vec] * nn,
        out_shape=[jax.ShapeDtypeStruct((t, d), F32)] + [jax.ShapeDtypeStruct((1, d), F32)] * nn,
        compiler_params=_cparams(("arbitrary",)),
        name=name,
    )(*args)
    return outs[0], list(outs[1:])


def _loss_head(x, w, target, *, name):
    t, d = x.shape
    tb = _tile(t, (256, 128))

    def body(x_ref, w_ref, t_ref, loss_ref, dx_ref, dw_ref):
        i = pl.program_id(0)
        xv = x_ref[...]
        wv = w_ref[...]
        r = lax.rsqrt(jnp.mean(xv * xv, axis=-1, keepdims=True) + EPS)
        xn = xv * r
        e = xn * wv - t_ref[...]
        lp = 0.5 * jnp.sum(jnp.mean(e * e, axis=-1, keepdims=True), axis=0, keepdims=True)
        dy = e * (1.0 / d)
        g = dy * wv
        dx_ref[...] = r * (g - xn * jnp.mean(g * xn, axis=-1, keepdims=True))
        dwp = jnp.sum(dy * xn, axis=0, keepdims=True)
        lpv = jnp.broadcast_to(lp, (1, LANES)) * (1.0 / LANES)

        @pl.when(i == 0)
        def _():
            dw_ref[...] = dwp
            loss_ref[...] = lpv

        @pl.when(i > 0)
        def _():
            dw_ref[...] += dwp
            loss_ref[...] += lpv

    row = pl.BlockSpec((tb, d), lambda i: (i, 0))
    vec = pl.BlockSpec((1, d), lambda i: (0, 0))
    return pl.pallas_call(
        body,
        grid=(t // tb,),
        in_specs=[row, vec, row],
        out_specs=[pl.BlockSpec((1, LANES), lambda i: (0, 0)), row, vec],
        out_shape=[jax.ShapeDtypeStruct((1, LANES), F32), jax.ShapeDtypeStruct((t, d), F32),
                   jax.ShapeDtypeStruct((1, d), F32)],
        compiler_params=_cparams(("arbitrary",)),
        name=name,
    )(x, w.reshape(1, d), target)


ROW_CHUNK = 64
PAD = SUBLANES


class _Strip:
    def __init__(self, head_ref, x_ref, rows):
        self.head_ref, self.x_ref = head_ref, x_ref
        head_ref[0:PAD, :] = jnp.zeros((PAD, head_ref.shape[1]), F32)
        head_ref[pl.ds(PAD, rows), :] = x_ref[pl.ds(0, rows), :]

    def rows(self, r0, rows, back):
        if r0 == 0:
            return self.head_ref[pl.ds(PAD - back, rows), :]
        return self.x_ref[pl.ds(r0 - back, rows), :]


def _shifted(strip, r0, rows, back):
    return strip.rows(r0, rows, back)


def _conv_taps(strip, w_ref, r0, rows, kw):
    acc = None
    for j in range(kw):
        term = _shifted(strip, r0, rows, kw - 1 - j) * w_ref[j:j + 1, :]
        acc = term if acc is None else acc + term
    return acc


def _fill_pad(pad_ref, x_ref, rows):
    return _Strip(pad_ref, x_ref, rows)


def _conv_silu_fwd(x, w, b, *, x_off=0, name):
    t = x.shape[0]
    kw, c = w.shape
    cw = _tile(math.gcd(c, x_off) if x_off else c, (256, 128))
    ob = x_off // cw
    rc = _tile(t, (ROW_CHUNK,))

    def body(x_ref, w_ref, b_ref, o_ref, pad_ref):
        xs = _fill_pad(pad_ref, x_ref, rc)
        for r0 in range(0, t, rc):
            pre = _conv_taps(xs, w_ref, r0, rc, kw) + b_ref[...]
            o_ref[pl.ds(r0, rc), :] = pre * _sigmoid(pre)

    strip = pl.BlockSpec((t, cw), lambda i: (0, i))
    return pl.pallas_call(
        body,
        grid=(c // cw,),
        in_specs=[pl.BlockSpec((t, cw), lambda i: (0, i + ob)), pl.BlockSpec((kw, cw), lambda i: (0, i)),
                  pl.BlockSpec((1, cw), lambda i: (0, i))],
        out_specs=strip,
        out_shape=jax.ShapeDtypeStruct((t, c), F32),
        scratch_shapes=[pltpu.VMEM((PAD + rc, cw), F32)],
        compiler_params=_cparams(("parallel",)),
        name=name,
    )(x, w, b.reshape(1, c))


def _conv_bwd_core(dpre_pad_ref, x_pad_ref, w_ref, dx_ref, dw_ref, db_ref, t, rc, kw):
    cw = dx_ref.shape[1]

    def fold(a):
        return jnp.sum(a.reshape(rc // SUBLANES, SUBLANES, cw), axis=0) if rc % SUBLANES == 0 else jnp.sum(a, axis=0, keepdims=True)

    dws = [None] * kw
    dbs = None
    for r0 in range(0, t, rc):
        dpre = dpre_pad_ref[pl.ds(PAD + r0, rc), :]
        dx = None
        for j in range(kw):
            s = kw - 1 - j
            term = dpre_pad_ref[pl.ds(PAD + r0 + s, rc), :] * w_ref[j:j + 1, :]
            dx = term if dx is None else dx + term
            part = fold(dpre * _shifted(x_pad_ref, r0, rc, s))
            dws[j] = part if dws[j] is None else dws[j] + part
        part = fold(dpre)
        dbs = part if dbs is None else dbs + part
        dx_ref[pl.ds(r0, rc), :] = dx
    for j in range(kw):
        dw_ref[j:j + 1, :] = jnp.sum(dws[j], axis=0, keepdims=True)
    db_ref[...] = jnp.sum(dbs, axis=0, keepdims=True)


def _conv_silu_bwd(x, w, b, dact, *, x_off=0, into=None, name):
    t = x.shape[0]
    kw, c = w.shape
    parts = dact if isinstance(dact, (list, tuple)) else [dact]
    widths = [p.shape[1] for p in parts]
    assert sum(widths) == c
    cw = _tile(functools.reduce(math.gcd, widths + [x_off or c]), (256, 128) if len(parts) == 1 else (128,))
    ob = x_off // cw
    rc = _tile(t, (ROW_CHUNK,))
    firsts = [sum(widths[:p]) // cw for p in range(len(parts))]
    counts = [wd // cw for wd in widths]
    n_p = len(parts)

    def body(x_ref, w_ref, b_ref, *rest):
        da_refs = rest[:n_p]
        dx_ref, dw_ref, db_ref, xpad_ref, dpad_ref = rest[-5 - (n_p > 1):][:5]
        if n_p > 1:
            da_ref = rest[-1]
            i = pl.program_id(0)
            for p in range(n_p):
                @pl.when(jnp.logical_and(i >= firsts[p], i < firsts[p] + counts[p]))
                def _(p=p):
                    da_ref[...] = da_refs[p][...]
        else:
            da_ref = da_refs[0]
        xs = _fill_pad(xpad_ref, x_ref, rc)
        dpad_ref[0:PAD, :] = jnp.zeros((PAD, cw), F32)
        dpad_ref[pl.ds(PAD + t, PAD), :] = jnp.zeros((PAD, cw), F32)
        for r0 in range(0, t, rc):
            pre = _conv_taps(xs, w_ref, r0, rc, kw) + b_ref[...]
            sg = _sigmoid(pre)
            dpad_ref[pl.ds(PAD + r0, rc), :] = da_ref[pl.ds(r0, rc), :] * (sg * (1.0 + pre * (1.0 - sg)))
        _conv_bwd_core(dpad_ref, xs, w_ref, dx_ref, dw_ref, db_ref, t, rc, kw)

    strip = pl.BlockSpec((t, cw), lambda i: (0, i))
    wspec = pl.BlockSpec((kw, cw), lambda i: (0, i))
    bspec = pl.BlockSpec((1, cw), lambda i: (0, i))
    xspec = pl.BlockSpec((t, cw), lambda i: (0, i + ob))
    dspecs = [pl.BlockSpec((t, cw), lambda i, f=f, n=n: (0, jnp.clip(i - f, 0, n - 1))) for f, n in zip(firsts, counts)]
    extra = {} if into is None else dict(input_output_aliases={3 + n_p: 0})
    pad = pltpu.VMEM((t + 2 * PAD, cw), F32)
    return pl.pallas_call(
        body,
        grid=(c // cw,),
        in_specs=[xspec, wspec, bspec] + dspecs + ([] if into is None else [ANY]),
        out_specs=[strip if into is None else xspec, wspec, bspec],
        out_shape=[jax.ShapeDtypeStruct((t, c) if into is None else into.shape, F32), jax.ShapeDtypeStruct((kw, c), F32),
                   jax.ShapeDtypeStruct((1, c), F32)],
        scratch_shapes=[pad, pad] + ([pltpu.VMEM((t, cw), F32)] if n_p > 1 else []),
        compiler_params=_cparams(("arbitrary",)),
        name=name,
        **extra,
    )(x, w, b.reshape(1, c), *parts, *([] if into is None else [into]))


def _conv_glu_fwd(hid, w, b, *, side=None, name):
    t, c2 = hid.shape
    f = c2 // 2
    kw = w.shape[0]
    cw = _tile(f, (256, 128))
    nf = f // cw
    rc = _tile(t, (ROW_CHUNK,))

    def body(g_ref, v_ref, wg_ref, wv_ref, bg_ref, bv_ref, o_ref, gpad_ref, vpad_ref):
        gs_, vs_ = _fill_pad(gpad_ref, g_ref, rc), _fill_pad(vpad_ref, v_ref, rc)
        for r0 in range(0, t, rc):
            gate = _conv_taps(gs_, wg_ref, r0, rc, kw) + bg_ref[...]
            val = _conv_taps(vs_, wv_ref, r0, rc, kw) + bv_ref[...]
            o_ref[pl.ds(r0, rc), :] = (gate * _sigmoid(gate) * val).astype(o_ref.dtype)

    gs = pl.BlockSpec((t, cw), lambda i: (0, i))
    vs = pl.BlockSpec((t, cw), lambda i: (0, i + nf))
    b2 = b.reshape(1, c2)
    (act,), side_outs = _call(
        body,
        grid=(nf,),
        in_specs=[gs, vs, pl.BlockSpec((kw, cw), lambda i: (0, i)), pl.BlockSpec((kw, cw), lambda i: (0, i + nf)),
                  pl.BlockSpec((1, cw), lambda i: (0, i)), pl.BlockSpec((1, cw), lambda i: (0, i + nf))],
        out_specs=[gs],
        out_shape=[jax.ShapeDtypeStruct((t, f), BF16)],
        scratch_shapes=[pltpu.VMEM((PAD + rc, cw), F32), pltpu.VMEM((PAD + rc, cw), F32)],
        sem=("parallel",),
        name=name,
        args=(hid, hid, w, w, b2, b2),
        side=side,
    )
    return act, side_outs


def _conv_glu_bwd(hid, w, b, dact, *, side=None, name):
    t, c2 = hid.shape
    f = c2 // 2
    kw = w.shape[0]
    cw = _tile(f, (128,))
    nf = f // cw
    rc = _tile(t, (ROW_CHUNK,))

    def body(g_ref, v_ref, wg_ref, wv_ref, bg_ref, bv_ref, da_ref,
             dgv_ref, dwg_ref, dwv_ref, dbg_ref, dbv_ref,
             gpad_ref, vpad_ref, dgpad_ref, dvpad_ref):
        gs_, vs_ = _fill_pad(gpad_ref, g_ref, rc), _fill_pad(vpad_ref, v_ref, rc)
        for ref in (dgpad_ref, dvpad_ref):
            ref[0:PAD, :] = jnp.zeros((PAD, cw), F32)
            ref[pl.ds(PAD + t, PAD), :] = jnp.zeros((PAD, cw), F32)
        for r0 in range(0, t, rc):
            gate = _conv_taps(gs_, wg_ref, r0, rc, kw) + bg_ref[...]
            val = _conv_taps(vs_, wv_ref, r0, rc, kw) + bv_ref[...]
            sg = _sigmoid(gate)
            da = da_ref[pl.ds(r0, rc), :].astype(F32)
            dgpad_ref[pl.ds(PAD + r0, rc), :] = da * val * (sg * (1.0 + gate * (1.0 - sg)))
            dvpad_ref[pl.ds(PAD + r0, rc), :] = da * (gate * sg)
        _conv_bwd_core(dgpad_ref, gs_, wg_ref, dgv_ref.at[0], dwg_ref, dbg_ref, t, rc, kw)
        _conv_bwd_core(dvpad_ref, vs_, wv_ref, dgv_ref.at[1], dwv_ref, dbv_ref, t, rc, kw)

    gs = pl.BlockSpec((t, cw), lambda i: (0, i))
    vs = pl.BlockSpec((t, cw), lambda i: (0, i + nf))
    wg = pl.BlockSpec((kw, cw), lambda i: (0, i))
    wv = pl.BlockSpec((kw, cw), lambda i: (0, i + nf))
    bg = pl.BlockSpec((1, cw), lambda i: (0, i))
    bv = pl.BlockSpec((1, cw), lambda i: (0, i + nf))
    b2 = b.reshape(1, c2)
    pad = pltpu.VMEM((t + 2 * PAD, cw), F32)
    outs, side_outs = _call(
        body,
        grid=(nf,),
        in_specs=[gs, vs, wg, wv, bg, bv, gs],
        out_specs=[pl.BlockSpec((2, t, cw), lambda i: (0, 0, i)), wg, wg, bg, bg],
        out_shape=[jax.ShapeDtypeStruct((2, t, f), F32),
                   jax.ShapeDtypeStruct((kw, f), F32), jax.ShapeDtypeStruct((kw, f), F32),
                   jax.ShapeDtypeStruct((1, f), F32), jax.ShapeDtypeStruct((1, f), F32)],
        scratch_shapes=[pad, pad, pad, pad],
        sem=("parallel",),
        name=name,
        args=(hid, hid, w, w, b2, b2, dact),
        side=side,
    )
    return (*outs, side_outs)


def _gate_norm_fwd(y, zx, w, *, name):
    t, di = y.shape
    gsz = di // SSM_GROUPS
    tb = _tile(t, (256, 128))

    def body(y_ref, z_ref, w_ref, o_ref):
        for g in range(SSM_GROUPS):
            sl = slice(g * gsz, (g + 1) * gsz)
            zv = z_ref[:, sl]
            gv = y_ref[:, sl] * (zv * _sigmoid(zv))
            r = lax.rsqrt(jnp.mean(gv * gv, axis=-1, keepdims=True) + EPS)
            o_ref[:, sl] = (gv * r * w_ref[:, sl]).astype(o_ref.dtype)

    row = pl.BlockSpec((tb, di), lambda i: (i, 0))
    return pl.pallas_call(
        body,
        grid=(t // tb,),
        in_specs=[row, row, pl.BlockSpec((1, di), lambda i: (0, 0))],
        out_specs=row,
        out_shape=jax.ShapeDtypeStruct((t, di), BF16),
        compiler_params=_cparams(("parallel",)),
        name=name,
    )(y, zx, w.reshape(1, di))


def _gate_norm_bwd(y, zx, w, dyn, *, side=None, name):
    t, di = y.shape
    gsz = di // SSM_GROUPS
    tb = _tile(t, (256, 128))

    def body(y_ref, z_ref, w_ref, d_ref, dy_ref, dz_ref, dw_ref):
        i = pl.program_id(0)
        for g in range(SSM_GROUPS):
            sl = slice(g * gsz, (g + 1) * gsz)
            zv = z_ref[:, sl]
            yv = y_ref[:, sl]
            sg = _sigmoid(zv)
            sz = zv * sg
            gv = yv * sz
            r = lax.rsqrt(jnp.mean(gv * gv, axis=-1, keepdims=True) + EPS)
            gn = gv * r
            dn = d_ref[:, sl].astype(F32)
            q = dn * w_ref[:, sl]
            dg = r * (q - gn * jnp.mean(q * gn, axis=-1, keepdims=True))
            dy_ref[:, sl] = dg * sz
            dz_ref[:, sl] = dg * yv * (sg * (1.0 + zv * (1.0 - sg)))
            dwp = jnp.sum(dn * gn, axis=0, keepdims=True)

            @pl.when(i == 0)
            def _(sl=sl, dwp=dwp):
                dw_ref[:, sl] = dwp

            @pl.when(i > 0)
            def _(sl=sl, dwp=dwp):
                dw_ref[:, sl] += dwp

    row = pl.BlockSpec((tb, di), lambda i: (i, 0))
    vec = pl.BlockSpec((1, di), lambda i: (0, 0))
    outs, side_outs = _call(
        body,
        grid=(t // tb,),
        in_specs=[row, row, vec, row],
        out_specs=[row, row, vec],
        out_shape=[jax.ShapeDtypeStruct((t, di), F32), jax.ShapeDtypeStruct((t, zx.shape[1]), F32),
                   jax.ShapeDtypeStruct((1, di), F32)],
        sem=("arbitrary",),
        name=name,
        args=(y, zx, w.reshape(1, di), dyn),
        side=side,
    )
    return (*outs, side_outs)


def _adamw(w, g, m, v, *, name):
    shape = w.shape
    cols = shape[-1]
    rows = w.size // cols
    w2, g2, m2, v2 = (a.reshape(rows, cols) for a in (w, g, m, v))
    tr = rows if rows * cols * 4 <= ADAM_BLOCK_BYTES else _row_tile(rows, cols)
    c1 = 1.0 - ADAM_B1 ** ADAM_STEP
    c2 = 1.0 - ADAM_B2 ** ADAM_STEP

    def body(w_ref, g_ref, m_ref, v_ref, d_ref, nm_ref, nv_ref):
        gv = g_ref[...]
        nm = ADAM_B1 * m_ref[...] + (1.0 - ADAM_B1) * gv
        nv = ADAM_B2 * v_ref[...] + (1.0 - ADAM_B2) * (gv * gv)
        d_ref[...] = -ADAM_LR * ((nm / c1) / (jnp.sqrt(nv / c2) + ADAM_EPS) + ADAM_WD * w_ref[...])
        nm_ref[...] = nm
        nv_ref[...] = nv

    blk = pl.BlockSpec((tr, cols), lambda i: (i, 0))
    outs = pl.pallas_call(
        body,
        grid=(rows // tr,),
        in_specs=[blk] * 4,
        out_specs=[blk] * 3,
        out_shape=[jax.ShapeDtypeStruct((rows, cols), F32)] * 3,
        compiler_params=_cparams(("parallel",)),
        name=name,
    )(w2, g2, m2, v2)
    return tuple(o.reshape(shape) for o in outs)


def _adamw_layers(w, gs, m, v, *, name):
    n_l, rows, cols = w.shape
    assert len(gs) == n_l
    tr = _row_tile(rows, cols)
    c1 = 1.0 - ADAM_B1 ** ADAM_STEP
    c2 = 1.0 - ADAM_B2 ** ADAM_STEP

    def body(*refs):
        w_ref, m_ref, v_ref = refs[:3]
        g_refs = refs[3:3 + n_l]
        g_ref, d_ref, nm_ref, nv_ref = refs[3 + n_l:]
        layer = pl.program_id(0)
        gv = g_refs[0][...]
        for q in range(1, n_l):
            gv = jnp.where(layer == q, g_refs[q][...], gv)
        nm = ADAM_B1 * m_ref[...] + (1.0 - ADAM_B1) * gv
        nv = ADAM_B2 * v_ref[...] + (1.0 - ADAM_B2) * (gv * gv)
        g_ref[...] = gv
        d_ref[...] = -ADAM_LR * ((nm / c1) / (jnp.sqrt(nv / c2) + ADAM_EPS) + ADAM_WD * w_ref[...])
        nm_ref[...] = nm
        nv_ref[...] = nv

    stacked = pl.BlockSpec((None, tr, cols), lambda l, i: (l, i, 0))
    single = pl.BlockSpec((tr, cols), lambda l, i: (i, 0))
    return pl.pallas_call(
        body,
        grid=(n_l, rows // tr),
        in_specs=[stacked] * 3 + [single] * n_l,
        out_specs=[stacked] * 4,
        out_shape=[jax.ShapeDtypeStruct(w.shape, F32)] * 4,
        compiler_params=_cparams(("parallel", "parallel")),
        name=name,
    )(w, m, v, *gs)


def _ssd_scalars(dtc_ref, dtr_ref, hpc_ref, hpr_ref, ln):
    assert SSM_CHUNK == SSM_STATE == LANES, "the SSD kernels mix chunk, state and lane-wide tiles freely"
    bias_c, alog_c = hpc_ref[0, 0:1, :], hpc_ref[0, 1:2, :]
    bias_r, alog_r = hpr_ref[0, :, 0:1], hpr_ref[0, :, 1:2]
    a_c, a_r = -jnp.exp(alog_c), -jnp.exp(alog_r)
    raw_c = dtc_ref[0] + bias_c
    dt_c = _softplus(raw_c)
    dt_r = _softplus(dtr_ref[0] + bias_r)
    row = lax.broadcasted_iota(jnp.int32, (ln, ln), 0)
    col = lax.broadcasted_iota(jnp.int32, (ln, ln), 1)
    lower = (col <= row).astype(F32)
    upper = (row <= col).astype(F32)
    acs_c = _ones_dot(lower, dt_c * a_c, ones_left=True)
    acs_r = _ones_dot(upper, dt_r * a_r, ones_left=False)
    return raw_c, dt_c, a_c, acs_c, acs_r, row, col


def _ssd_specs(t, di, g_n, n_st, rp, ln, r_h, rev):
    nc = t // ln
    cidx = (lambda c: nc - 1 - c) if rev else (lambda c: c)
    xs = pl.BlockSpec((ln, rp), lambda g, c: (cidx(c), g))
    bm = pl.BlockSpec((ln, n_st), lambda g, c: (cidx(c), di // n_st + g))
    cm = pl.BlockSpec((ln, n_st), lambda g, c: (cidx(c), di // n_st + g_n + g))
    dtc = pl.BlockSpec((1, ln, r_h), lambda g, c: (g, cidx(c), 0))
    dtr = pl.BlockSpec((1, r_h, ln), lambda g, c: (g, 0, cidx(c)))
    hpc = pl.BlockSpec((1, 3, r_h), lambda g, c: (g, 0, 0))
    hpr = pl.BlockSpec((1, r_h, 3), lambda g, c: (g, 0, 0))
    prev = pl.BlockSpec((1, rp, n_st), lambda g, c: (cidx(c), g, 0))
    return xs, bm, cm, dtc, dtr, hpc, hpr, prev


def _ssd_fwd(xbc, dtc, dtr, hpc, hpr, *, side=None, name):
    t = xbc.shape[0]
    di, g_n, n_st, p_h, ln = D_INNER, SSM_GROUPS, SSM_STATE, SSM_HEAD_DIM, SSM_CHUNK
    r_h = SSM_HEADS // g_n
    rp = r_h * p_h
    nc = t // ln

    def body(xs_ref, b_ref, c_ref, dtc_ref, dtr_ref, hpc_ref, hpr_ref, y_ref, prev_ref, st_ref):
        @pl.when(pl.program_id(1) == 0)
        def _():
            st_ref[...] = jnp.zeros_like(st_ref)

        _, dt_c, _, acs_c, acs_r, row, col = _ssd_scalars(dtc_ref, dtr_ref, hpc_ref, hpr_ref, ln)
        bm = b_ref[...]
        cm = c_ref[...]
        cm16 = cm.astype(BF16)
        cb = _nt(cm16, bm.astype(BF16))
        causal = row >= col
        for r in range(r_h):
            sl = slice(r * p_h, (r + 1) * p_h)
            xs = xs_ref[:, sl]
            acs = jnp.broadcast_to(acs_c[:, r:r + 1], (ln, ln))
            last = acs[ln - 1:ln, :]
            lm = jnp.where(causal, jnp.exp(acs - acs_r[r:r + 1, :]), 0.0)
            xd = (xs * jnp.broadcast_to(dt_c[:, r:r + 1], (ln, p_h))).astype(BF16)
            prev = st_ref[sl, :]
            y = _nn((cb * lm).astype(BF16), xd)
            y = y + _nt(cm16, prev.astype(BF16)) * jnp.exp(acs[:, :p_h])
            y_ref[:, sl] = y + hpc_ref[0, 2:3, r:r + 1] * xs
            prev_ref[0, sl, :] = prev
            bd = (bm * jnp.exp(last - acs[:, :n_st])).astype(BF16)
            st_ref[sl, :] = prev * jnp.exp(last[:, :n_st]) + _tn(xd, bd)

    xs, bm, cm, dtcs, dtrs, hpcs, hprs, prev = _ssd_specs(t, di, g_n, n_st, rp, ln, r_h, False)
    (y, prev_out), side_outs = _call(
        body,
        grid=(g_n, nc),
        in_specs=[xs, bm, cm, dtcs, dtrs, hpcs, hprs],
        out_specs=[xs, prev],
        out_shape=[jax.ShapeDtypeStruct((t, di), F32), jax.ShapeDtypeStruct((nc, g_n * rp, n_st), F32)],
        scratch_shapes=[pltpu.VMEM((rp, n_st), F32)],
        sem=("parallel", "arbitrary"),
        name=name,
        args=(xbc, xbc, xbc, dtc, dtr, hpc, hpr),
        side=side,
    )
    return y, prev_out, side_outs


def _ssd_bwd(xbc, dtc, dtr, hpc, hpr, prev, dy, *, side=None, name):
    t = xbc.shape[0]
    di, g_n, n_st, p_h, ln = D_INNER, SSM_GROUPS, SSM_STATE, SSM_HEAD_DIM, SSM_CHUNK
    r_h = SSM_HEADS // g_n
    rp = r_h * p_h
    nc = t // ln

    def body(xs_ref, b_ref, c_ref, dtc_ref, dtr_ref, hpc_ref, hpr_ref, prev_ref, dy_ref,
             dxs_ref, db_ref, dc_ref, ddt_ref, hg_ref, ds_ref):
        step = pl.program_id(1)

        @pl.when(step == 0)
        def _():
            ds_ref[...] = jnp.zeros_like(ds_ref)

        raw_c, dt_c, a_c, acs_c, acs_r, row, col = _ssd_scalars(dtc_ref, dtr_ref, hpc_ref, hpr_ref, ln)
        bm = b_ref[...]
        cm = c_ref[...]
        bm16, cm16 = bm.astype(BF16), cm.astype(BF16)
        cb = _nt(cm16, bm16)
        cbt = _nt(bm16, cm16)
        lane_r = lax.broadcasted_iota(jnp.int32, (ln, r_h), 1)
        dacs_all = jnp.zeros((ln, r_h), F32)
        ddtx_all = jnp.zeros((ln, r_h), F32)
        dd_all = jnp.zeros((ln, r_h), F32)
        dcb = jnp.zeros((ln, ln), F32)
        dcbt = jnp.zeros((ln, ln), F32)
        dc_acc = jnp.zeros((ln, n_st), F32)
        db_acc = jnp.zeros((ln, n_st), F32)
        for r in range(r_h):
            sl = slice(r * p_h, (r + 1) * p_h)
            xs = xs_ref[:, sl]
            dyv = dy_ref[:, sl]
            dy16 = dyv.astype(BF16)
            acs = jnp.broadcast_to(acs_c[:, r:r + 1], (ln, ln))
            dtv = jnp.broadcast_to(dt_c[:, r:r + 1], (ln, p_h))
            acsr = acs_r[r:r + 1, :]
            last = acs[ln - 1:ln, :]
            xd = xs * dtv
            xd16 = xd.astype(BF16)
            lm = jnp.where(row >= col, jnp.exp(acs - acsr), 0.0)
            lmt = jnp.where(col >= row, jnp.exp(acsr - acs), 0.0)
            m_ls = cb * lm
            m_sl = cbt * lmt
            dm = _nt(dy16, xd16)
            dmt = _nt(xd16, dy16)
            dxd = _nn(m_sl.astype(BF16), dy16)
            dacs = _row_sums(dm * m_ls - dmt * m_sl)
            dcb = dcb + dm * lm
            dcbt = dcbt + dmt * lmt
            prev = prev_ref[0, sl, :]
            prev16 = prev.astype(BF16)
            e = jnp.exp(acs[:, :p_h])
            y_off = _nt(cm16, prev16) * e
            dacs = dacs + _row_sums(dyv * y_off)
            dyo16 = (dyv * e).astype(BF16)
            dc_acc = dc_acc + _nn(dyo16, prev16)
            dprev = _tn(dyo16, cm16)
            ds = ds_ref[sl, :]
            ds16 = ds.astype(BF16)
            decay = jnp.exp(last - acs)[:, :n_st]
            bd16 = (bm * decay).astype(BF16)
            dbd = _nn(xd16, ds16)
            dxd = dxd + _nt(bd16, ds16)
            db_acc = db_acc + dbd * decay
            tdec = _row_sums(dbd * bm) * decay
            cd = jnp.exp(last)
            dlast = (jnp.sum(tdec, axis=0, keepdims=True)
                     + jnp.sum(_row_sums(prev * ds), axis=0, keepdims=True) * cd)
            ds_ref[sl, :] = dprev + cd[:, :n_st] * ds
            dskip = hpc_ref[0, 2:3, r:r + 1]
            dxs_ref[:, sl] = dxd * dtv + dskip * dyv
            dacs = dacs - tdec + jnp.where(row == ln - 1, dlast, 0.0)
            dacs_all = jnp.where(lane_r == r, dacs[:, :r_h], dacs_all)
            ddtx_all = jnp.where(lane_r == r, _row_sums(dxd * xs)[:, :r_h], ddtx_all)
            dd_all = jnp.where(lane_r == r, _row_sums(dyv * xs)[:, :r_h], dd_all)
        dc_ref[...] = dc_acc + _nn(dcb.astype(BF16), bm16)
        db_ref[...] = db_acc + _nn(dcbt.astype(BF16), cm16)
        upper = (row <= col).astype(F32)
        dad = _ones_dot(upper, dacs_all, ones_left=True)
        ddt = dad * a_c + ddtx_all
        ddt_raw = ddt * _sigmoid(raw_c)
        ddt_ref[0] = ddt_raw
        d_bias = jnp.sum(ddt_raw, axis=0, keepdims=True)
        d_alog = jnp.sum(dad * dt_c, axis=0, keepdims=True) * a_c
        d_d = jnp.sum(dd_all, axis=0, keepdims=True)
        hg = jnp.concatenate([d_bias, d_alog, d_d], axis=0)

        @pl.when(step == 0)
        def _():
            hg_ref[0] = hg

        @pl.when(step > 0)
        def _():
            hg_ref[0] += hg

    xs, bms, cms, dtcs, dtrs, hpcs, hprs, prevs = _ssd_specs(t, di, g_n, n_st, rp, ln, r_h, True)
    bout = pl.BlockSpec((ln, n_st), lambda g, c: (nc - 1 - c, g))
    outs, side_outs = _call(
        body,
        grid=(g_n, nc),
        in_specs=[xs, bms, cms, dtcs, dtrs, hpcs, hprs, prevs, xs],
        out_specs=[xs, bout, bout, dtcs, hpcs],
        out_shape=[jax.ShapeDtypeStruct((t, di), F32), jax.ShapeDtypeStruct((t, g_n * n_st), F32),
                   jax.ShapeDtypeStruct((t, g_n * n_st), F32), jax.ShapeDtypeStruct((g_n, t, r_h), F32),
                   jax.ShapeDtypeStruct((g_n, 3, r_h), F32)],
        scratch_shapes=[pltpu.VMEM((rp, n_st), F32)],
        sem=("parallel", "arbitrary"),
        name=name,
        args=(xbc, xbc, xbc, dtc, dtr, hpc, hpr, prev, dy),
        side=side,
    )
    return (*outs, side_outs)


SB_KEYS = 256
SB_QUERIES = (512, 256)
SB_CUTOFF = 110.0
SB_PIECES = 2


def _sb_logits(qs, kv, valid):
    z = _nt(qs, kv)
    nz = -z
    lg = jnp.minimum(nz, 0.0) - jnp.log(1.0 + jnp.exp(jnp.minimum(z, nz)))
    return z + lg, (lg if valid is None else jnp.where(valid, lg, 0.0))


def _sb_iota(tq):
    diff = lax.broadcasted_iota(jnp.int32, (tq, SB_KEYS), 1) - lax.broadcasted_iota(jnp.int32, (tq, SB_KEYS), 0)
    krow = lax.broadcasted_iota(jnp.int32, (SB_KEYS, SB_KEYS), 0)
    kcol = lax.broadcasted_iota(jnp.int32, (SB_KEYS, SB_KEYS), 1)
    return diff, krow, kcol


def _sb_scale(d):
    scale = 1.0 / math.sqrt(d)
    assert math.frexp(scale)[0] == 0.5, "the scale is folded into bf16 queries: it must be a power of two"
    return scale


def _key_rows(j):
    return pl.ds(pl.multiple_of(j * SB_KEYS, SB_KEYS), SB_KEYS)


def _sb_fwd(q, k, v, n_heads, *, side=None, name):
    t, hd = q.shape
    d = hd // n_heads
    hpt = LANES // d
    assert hpt * d == LANES and n_heads % hpt == 0
    tq = _tile(t, SB_QUERIES)
    nq = t // tq
    kpq = tq // SB_KEYS
    scale = _sb_scale(d)

    def body(q_ref, k_ref, v_ref, o_ref, lt_ref, first_ref):
        i = pl.program_id(1)
        diff, krow, kcol = _sb_iota(tq)
        later = (krow > kcol).astype(F32)
        nb = i * kpq
        for hh in range(hpt):
            sl = slice(hh * d, (hh + 1) * d)
            qs = (q_ref[:, sl].astype(F32) * scale).astype(BF16)

            def block(j, carry, valid, qs=qs, sl=sl):
                acc, cl = carry
                rows = _key_rows(j)
                ls, lg = _sb_logits(qs, k_ref[rows, sl], valid)
                cs = _ones_dot(later, lg, ones_left=False, pieces=SB_PIECES)
                att = jnp.exp(ls + (cs + cl))
                if valid is not None:
                    att = jnp.where(valid, att, 0.0)
                acc = acc + _nn(att.astype(BF16), v_ref[rows, sl])
                return acc, cl + (cs[:, 0:1] + lg[:, 0:1])

            carry = (jnp.zeros((tq, d), F32), jnp.zeros((tq, 1), F32))
            for m in range(kpq - 1, -1, -1):
                carry = block(i * kpq + m, carry, diff < -m * SB_KEYS)

            def more(st):
                s, _, cl = st
                return jnp.logical_and(s < nb, jnp.max(cl) > -SB_CUTOFF)

            def step(st, block=block):
                s, acc, cl = st
                acc, cl = block(nb - 1 - s, (acc, cl), None)
                return s + 1, acc, cl

            walked, acc, cl = lax.while_loop(more, step, (jnp.int32(0),) + carry)
            o_ref[:, sl] = acc.astype(o_ref.dtype)
            lt_ref[hh] = cl
            first_ref[pl.program_id(0) * hpt + hh, i] = nb - walked

    qs = pl.BlockSpec((tq, LANES), lambda p, i: (i, p))
    ls = pl.BlockSpec((hpt, tq, 1), lambda p, i: (p, i, 0))
    ks = pl.BlockSpec((t, LANES), lambda p, i: (0, p))
    outs, side_outs = _call(
        body,
        grid=(n_heads // hpt, nq),
        in_specs=[qs, ks, ks],
        out_specs=[qs, ls, pl.BlockSpec(memory_space=pltpu.SMEM)],
        out_shape=[jax.ShapeDtypeStruct((t, hd), BF16), jax.ShapeDtypeStruct((n_heads, t, 1), F32),
                   jax.ShapeDtypeStruct((n_heads, nq), jnp.int32)],
        sem=("arbitrary", "arbitrary"),
        name=name,
        args=(q, k, v),
        side=side,
    )
    return (*outs, side_outs)


def _sb_bwd(q, k, v, lt, first, do, n_heads, *, name):
    t, hd = q.shape
    d = hd // n_heads
    hpt = LANES // d
    tq = _tile(t, SB_QUERIES)
    nq = t // tq
    kpq = tq // SB_KEYS
    scale = _sb_scale(d)
    last = SB_KEYS - 1

    def body(q_ref, k_ref, v_ref, lt_ref, first_ref, do_ref, dq_ref, dk_ref, dv_ref, dk_acc, dv_acc):
        i = pl.program_id(1)

        @pl.when(i == 0)
        def _():
            dk_acc[...] = jnp.zeros_like(dk_acc)
            dv_acc[...] = jnp.zeros_like(dv_acc)

        diff, krow, kcol = _sb_iota(tq)
        upto = (krow <= kcol).astype(F32)
        before = (krow < kcol).astype(F32)
        zero = jnp.zeros((tq, 1), F32)
        nb = i * kpq
        for hh in range(hpt):
            sl = slice(hh * d, (hh + 1) * d)
            qs = (q_ref[:, sl].astype(F32) * scale).astype(BF16)
            do16 = do_ref[:, sl].astype(BF16)
            ltot = lt_ref[hh]

            def block(j, carry, valid, r0=0, qs=qs, do16=do16, ltot=ltot, sl=sl):
                dq, pl_sum, pg_sum = carry
                rows = _key_rows(j)
                kv = k_ref[rows, sl]
                vv = v_ref[rows, sl]
                ls, lg = _sb_logits(qs[r0:], kv, valid)
                pre = _ones_dot(upto, lg, ones_left=False, pieces=SB_PIECES)
                att = jnp.exp(ls + (ltot[r0:] - (pre + pl_sum)))
                if valid is not None:
                    att = jnp.where(valid, att, 0.0)
                g = att * _nt(do16[r0:], vv)
                gpre = _ones_dot(before, g, ones_left=False, pieces=SB_PIECES)
                sig = jnp.exp(ls)
                dz16 = (g - sig * (g + (gpre + pg_sum))).astype(BF16)
                if valid is not None:
                    dz16 = jnp.where(valid, dz16, jnp.zeros_like(dz16))
                dq = dq + _nn(dz16, kv)
                dk_acc[rows, sl] += _tn(dz16, qs[r0:])
                dv_acc[rows, sl] += _tn(att.astype(BF16), do16[r0:])
                return dq, pl_sum + pre[:, last:], pg_sum + (gpre[:, last:] + g[:, last:])

            start = jnp.clip(first_ref[pl.program_id(0) * hpt + hh, i], 0, nb)
            carry = lax.fori_loop(start, nb, lambda j, cr, block=block: block(j, cr, None),
                                  (jnp.zeros((tq, d), F32), zero, zero))
            for m in range(kpq):
                r0 = m * SB_KEYS
                sub = block(nb + m, tuple(a[r0:] for a in carry), diff[r0:] < -r0, r0)
                carry = tuple(jnp.concatenate([a[:r0], s], axis=0) if r0 else s for a, s in zip(carry, sub))
            dq_ref[:, sl] = (carry[0] * scale).astype(dq_ref.dtype)

        @pl.when(i == nq - 1)
        def _():
            dk_ref[...] = dk_acc[...].astype(dk_ref.dtype)
            dv_ref[...] = dv_acc[...].astype(dv_ref.dtype)

    qs = pl.BlockSpec((tq, LANES), lambda p, i: (i, p))
    ls = pl.BlockSpec((hpt, tq, 1), lambda p, i: (p, i, 0))
    ks = pl.BlockSpec((t, LANES), lambda p, i: (0, p))
    full = jax.ShapeDtypeStruct((t, hd), BF16)
    return pl.pallas_call(
        body,
        grid=(n_heads // hpt, nq),
        in_specs=[qs, ks, ks, ls, pl.BlockSpec(memory_space=pltpu.SMEM), qs],
        out_specs=[qs, ks, ks],
        out_shape=[full, full, full],
        scratch_shapes=[pltpu.VMEM((t, LANES), F32), pltpu.VMEM((t, LANES), F32)],
        compiler_params=_cparams(("arbitrary", "arbitrary")),
        name=name,
    )(q, k, v, lt, first, do)


def _row_tile(rows, cols):
    fits = [r for r in range(16, rows + 1, 16) if rows % r == 0 and r * cols * 4 <= ADAM_BLOCK_BYTES]
    return max(fits) if fits else rows


def _sum_leading(x, *, name):
    n, rows, cols = x.shape
    tr = _row_tile(rows, cols)

    def body(x_ref, o_ref):
        acc = x_ref[0].astype(F32)
        for q in range(1, n):
            acc = acc + x_ref[q].astype(F32)
        o_ref[...] = acc

    return pl.pallas_call(
        body,
        grid=(rows // tr,),
        in_specs=[pl.BlockSpec((n, tr, cols), lambda i: (0, i, 0))],
        out_specs=pl.BlockSpec((tr, cols), lambda i: (i, 0)),
        out_shape=jax.ShapeDtypeStruct((rows, cols), F32),
        compiler_params=_cparams(("parallel",)),
        name=name,
    )(x)


def _pair_add(g4h, recv, c, *, out_dtype, name):
    n, _, rows, cols = g4h.shape
    tr = _row_tile(rows, cols)

    def body(c_ref, g_ref, r_ref, o_ref):
        o_ref[...] = (g_ref[...] + r_ref[...]).astype(o_ref.dtype)

    blk = pl.BlockSpec((1, tr, cols), lambda q, i, c_ref: (q, i, 0))
    return pl.pallas_call(
        body,
        grid_spec=pltpu.PrefetchScalarGridSpec(
            num_scalar_prefetch=1,
            grid=(n, rows // tr),
            in_specs=[pl.BlockSpec((1, None, tr, cols), lambda q, i, c_ref: (q, c_ref[0], i, 0)), blk],
            out_specs=blk),
        out_shape=jax.ShapeDtypeStruct((n, rows, cols), out_dtype),
        compiler_params=_cparams(("parallel", "parallel")),
        name=name,
    )(c.reshape(1).astype(jnp.int32), g4h, recv)


ANY = pl.BlockSpec(memory_space=pl.ANY)


def _other_chips(x, y):
    return [(1 - x, y), (x, 1 - y), (1 - x, 1 - y)]


def _gather_chips(shard, *, name):
    def body(x_ref, o_ref, send_sems, recv_sems, local_sem):
        x, y, c = lax.axis_index("x"), lax.axis_index("y"), lax.axis_index("c")
        me = 2 * x + y
        mine = pltpu.make_async_copy(x_ref, o_ref.at[me], local_sem)
        mine.start()
        chips = _other_chips(x, y)
        sends = [pltpu.make_async_remote_copy(src_ref=x_ref, dst_ref=o_ref.at[me], send_sem=send_sems.at[q],
                                              recv_sem=recv_sems.at[q], device_id=(px, py, c), device_id_type=MESH)
                 for q, (px, py) in enumerate(chips)]
        for cp in sends:
            cp.start()
        for q, (px, py) in enumerate(chips):
            pltpu.make_async_remote_copy(src_ref=x_ref, dst_ref=o_ref.at[2 * px + py], send_sem=send_sems.at[q],
                                         recv_sem=recv_sems.at[q], device_id=(px, py, c), device_id_type=MESH).wait_recv()
        for cp in sends:
            cp.wait_send()
        mine.wait()

    return pl.pallas_call(
        body,
        in_specs=[ANY],
        out_specs=ANY,
        out_shape=jax.ShapeDtypeStruct((4,) + shard.shape, shard.dtype),
        scratch_shapes=[pltpu.SemaphoreType.DMA((3,)), pltpu.SemaphoreType.DMA((3,)), pltpu.SemaphoreType.DMA],
        compiler_params=pltpu.CompilerParams(has_side_effects=True),
        name=name,
    )(shard)


def _comm_call(body, ins, out_shapes, n_sems, name):
    n = len(ins)

    def wrapped(*refs):
        body(refs[:n], refs[n:n + len(out_shapes)], refs[-2], refs[-1])

    return pl.pallas_call(
        wrapped,
        in_specs=[ANY] * n,
        out_specs=[ANY] * len(out_shapes),
        out_shape=out_shapes,
        scratch_shapes=[pltpu.SemaphoreType.DMA((n_sems,)), pltpu.SemaphoreType.DMA((n_sems,))],
        compiler_params=pltpu.CompilerParams(has_side_effects=True),
        name=name,
    )(*ins)


def _remote(send_sems, recv_sems, q, src, dst, to):
    return pltpu.make_async_remote_copy(src_ref=src, dst_ref=dst, send_sem=send_sems.at[q], recv_sem=recv_sems.at[q],
                                        device_id=to, device_id_type=MESH)


def _scatter_job(parts):
    def sends(ins, outs, send_sems, recv_sems):
        x, y, c = lax.axis_index("x"), lax.axis_index("y"), lax.axis_index("c")
        return [_remote(send_sems, recv_sems, 3 * i + q, p.at[2 * px + py], o.at[2 * x + y], (px, py, c))
                for i, (p, o) in enumerate(zip(ins, outs)) for q, (px, py) in enumerate(_other_chips(x, y))]

    def start(ins, outs, send_sems, recv_sems):
        for cp in sends(ins, outs, send_sems, recv_sems):
            cp.start()

    def finish(ins, outs, send_sems, recv_sems):
        x, y, c = lax.axis_index("x"), lax.axis_index("y"), lax.axis_index("c")
        for i, (p, o) in enumerate(zip(ins, outs)):
            for q, (px, py) in enumerate(_other_chips(x, y)):
                _remote(send_sems, recv_sems, 3 * i + q, p.at[2 * x + y], o.at[2 * px + py], (px, py, c)).wait_recv()
        for cp in sends(ins, outs, send_sems, recv_sems):
            cp.wait_send()

    return _SideJob(parts, [jax.ShapeDtypeStruct(p.shape, p.dtype) for p in parts], 3 * len(parts), start, finish)


def _run_job(job, name):
    return _comm_call(lambda *refs: (job.start(*refs), job.finish(*refs)), job.ins, job.out_shapes, job.n_sems, name)


def _gather_job(shards):
    def sends(ins, outs, send_sems, recv_sems):
        x, y, c = lax.axis_index("x"), lax.axis_index("y"), lax.axis_index("c")
        return [_remote(send_sems, recv_sems, 6 * i + q, s.at[c], o.at[2 * x + y, c], (px, py, c))
                for i, (s, o) in enumerate(zip(ins, outs)) for q, (px, py) in enumerate(_other_chips(x, y))]

    def start(ins, outs, send_sems, recv_sems):
        for cp in sends(ins, outs, send_sems, recv_sems):
            cp.start()

    def finish(ins, outs, send_sems, recv_sems):
        x, y, c = lax.axis_index("x"), lax.axis_index("y"), lax.axis_index("c")
        sibling = (x, y, 1 - c)
        chips = _other_chips(x, y)
        copy = lambda q, src, dst, to: _remote(send_sems, recv_sems, q, src, dst, to)
        passed = []
        for i, (s, o) in enumerate(zip(ins, outs)):
            for q, (px, py) in enumerate(chips):
                slot = o.at[2 * px + py, c]
                copy(6 * i + q, s.at[c], slot, (px, py, c)).wait_recv()
                passed.append(copy(6 * i + 3 + q, slot, slot, sibling))
                passed[-1].start()
        for i, (s, o) in enumerate(zip(ins, outs)):
            for q, (px, py) in enumerate(chips):
                copy(6 * i + 3 + q, s.at[1 - c], o.at[2 * px + py, 1 - c], sibling).wait_recv()
        for cp in sends(ins, outs, send_sems, recv_sems) + passed:
            cp.wait_send()

    return _SideJob(shards, [jax.ShapeDtypeStruct((N_CHIPS,) + s.shape, s.dtype) for s in shards], 6 * len(shards),
                    start, finish)


def _swap_job(gs):
    def copies(ins, outs, send_sems, recv_sems):
        x, y, c = lax.axis_index("x"), lax.axis_index("y"), lax.axis_index("c")
        return [_remote(send_sems, recv_sems, i, g.at[pl.ds(0, g.shape[0]), 1 - c], o, (x, y, 1 - c))
                for i, (g, o) in enumerate(zip(ins, outs))]

    def start(*refs):
        for cp in copies(*refs):
            cp.start()

    def finish(*refs):
        for cp in copies(*refs):
            cp.wait()

    return _SideJob(gs, [jax.ShapeDtypeStruct((g.shape[0],) + g.shape[2:], g.dtype) for g in gs], len(gs), start, finish)


def _join_halves(halves, *, name):
    def body(ins, outs, send_sems, recv_sems):
        x, y, c = lax.axis_index("x"), lax.axis_index("y"), lax.axis_index("c")
        sibling = (x, y, 1 - c)
        sends = [_remote(send_sems, recv_sems, i, h, o.at[c], sibling) for i, (h, o) in enumerate(zip(ins, outs))]
        for cp in sends:
            cp.start()
        for i, (h, o) in enumerate(zip(ins, outs)):
            _remote(send_sems, recv_sems, i, h, o.at[1 - c], sibling).wait_recv()
        for cp in sends:
            cp.wait_send()

    return _comm_call(body, halves, [jax.ShapeDtypeStruct((2,) + h.shape, h.dtype) for h in halves], len(halves), name)


def _gather_all(v, *, name):
    def body(v_ref, o_ref, send_sems, recv_sems, local_sem):
        x, y, c = lax.axis_index("x"), lax.axis_index("y"), lax.axis_index("c")
        me = 4 * x + 2 * y + c
        mine = pltpu.make_async_copy(v_ref, o_ref.at[me], local_sem)
        mine.start()
        peers = [(x ^ (q >> 2 & 1), y ^ (q >> 1 & 1), c ^ (q & 1)) for q in range(1, 8)]
        sends = [pltpu.make_async_remote_copy(src_ref=v_ref, dst_ref=o_ref.at[me], send_sem=send_sems.at[q],
                                              recv_sem=recv_sems.at[q], device_id=peer, device_id_type=MESH)
                 for q, peer in enumerate(peers)]
        for cp in sends:
            cp.start()
        for q, (px, py, pc) in enumerate(peers):
            pltpu.make_async_remote_copy(src_ref=v_ref, dst_ref=o_ref.at[4 * px + 2 * py + pc], send_sem=send_sems.at[q],
                                         recv_sem=recv_sems.at[q], device_id=(px, py, pc), device_id_type=MESH).wait_recv()
        for cp in sends:
            cp.wait_send()
        mine.wait()

    return pl.pallas_call(
        body,
        in_specs=[ANY],
        out_specs=ANY,
        out_shape=jax.ShapeDtypeStruct((8,) + v.shape, v.dtype),
        scratch_shapes=[pltpu.SemaphoreType.DMA((7,)), pltpu.SemaphoreType.DMA((7,)), pltpu.SemaphoreType.DMA],
        compiler_params=pltpu.CompilerParams(has_side_effects=True),
        name=name,
    )(v)


WEIGHTS = ['ssm_norm_w', 'ssm_in_w', 'ssm_conv_w', 'ssm_conv_b', 'ssm_dt_bias', 'ssm_a_log', 'ssm_d',
           'ssm_gate_norm_w', 'ssm_out_w', 'kv_norm_w', 'w_k', 'w_v', 'attn_norm_w', 'w_q', 'w_o',
           'ffn_norm_w', 'ffn_up_w', 'ffn_conv_w', 'ffn_conv_b', 'ffn_down_w', 'final_norm_w']
SHARD_AXIS = {'ssm_norm_w': 1, 'ssm_in_w': 2, 'ssm_conv_w': 2, 'ssm_conv_b': 1, 'ssm_gate_norm_w': 1,
              'ssm_out_w': 1, 'w_k': 0, 'w_v': 0, 'w_q': 1, 'w_o': 1, 'ffn_up_w': 2, 'ffn_conv_w': 2,
              'ffn_down_w': 1}
BIG = ['ssm_in_w', 'ssm_out_w', 'w_k', 'w_v', 'w_q', 'w_o', 'ffn_up_w', 'ffn_down_w']
SMALL = [n for n in WEIGHTS if n in SHARD_AXIS and n not in BIG]
REPLICATED = [n for n in WEIGHTS if n not in SHARD_AXIS]
STACKED = ['ffn_up_w', 'ffn_down_w']
N_CHIPS = 4


PACK_ROWS = 16


def _piece_rows(n):
    return -(-n // (PACK_ROWS * LANES)) * PACK_ROWS


def _pack(arrs, dtype, row_mult):
    lead = arrs[0].shape[:-1]
    pieces, total = [], 0
    for a in arrs:
        n = a.shape[-1]
        rows = _piece_rows(n)
        a = a.astype(dtype)
        if rows * LANES != n:
            a = jnp.pad(a, [(0, 0)] * len(lead) + [(0, rows * LANES - n)])
        pieces.append(a.reshape(lead + (rows, LANES)))
        total += rows
    extra = -total % row_mult
    if extra:
        pieces.append(jnp.zeros(lead + (extra, LANES), dtype))
    return jnp.concatenate(pieces, axis=len(lead))


def _unpack(buf, shapes):
    lead = buf.shape[:-2]
    out, off = [], 0
    for shp in shapes:
        n = math.prod(shp)
        rows = _piece_rows(n)
        piece = lax.slice_in_dim(buf, off, off + rows, axis=len(lead)).reshape(lead + (rows * LANES,))
        out.append(piece[..., :n].reshape(lead + tuple(shp)))
        off += rows
    return out


def _set_slot(buf, piece, index):
    return lax.dynamic_update_slice_in_dim(buf, piece[None], index, axis=0)


def _from_shards(stacked, axis):
    return jnp.concatenate([stacked[j] for j in range(N_CHIPS)], axis=axis)


def _ffn_fwd(h, norm_w, w_up, conv_w, conv_b, w_down, tag, side=None):
    u = _rmsnorm_fwd(h, norm_w, name=f"ffn{tag}_norm")
    hid = _matmul(u, w_up, name=f"ffn{tag}_up")
    act, side_outs = _conv_glu_fwd(hid, conv_w, conv_b, side=side, name=f"ffn{tag}_glu")
    out = _matmul(act, w_down, add=h, name=f"ffn{tag}_down")
    return out, (u, hid, act), side_outs


def _ffn_bwd(h, saved, dout, norm_w, w_up, conv_w, conv_b, w_down, tag, side=None):
    u, hid, act = saved
    dact = _matmul(dout, w_down, tb=True, name=f"ffn{tag}_down_dx")
    dw_down = _matmul(act, dout, ta=True, name=f"ffn{tag}_down_dw")
    dhid, dwg, dwv, dbg, dbv, side_outs = _conv_glu_bwd(hid, conv_w, conv_b, dact, side=side, name=f"ffn{tag}_glu_bwd")
    du = _matmul(dhid, w_up, tb=True, name=f"ffn{tag}_up_dx")
    dw_up = _matmul(u, dhid, ta=True, out_parts=N_CHIPS, name=f"ffn{tag}_up_dw")
    dh, (dnorm,) = _rmsnorm_bwd(h, [(du, norm_w)], dout, name=f"ffn{tag}_norm_bwd")
    return dh, dict(norm=dnorm[0], up=dw_up, conv_w=jnp.concatenate([dwg, dwv], axis=1),
                    conv_b=jnp.concatenate([dbg, dbv], axis=1)[0], down=dw_down), side_outs


class _Pieces:
    def __init__(self, local):
        self.c = lax.axis_index("c")
        self.chip = 2 * lax.axis_index("x") + lax.axis_index("y")
        self.shape, self.s16 = {}, {}
        for n in BIG:
            blk = local[n]
            layers = [(n, l, blk[l]) for l in range(blk.shape[0])] if n in STACKED else [(n, None, blk.reshape(blk.shape[-2:]))]
            for name, l, p in layers:
                self.shape[name, l] = p.shape
                self.s16[name, l] = p.astype(BF16).reshape(2, p.shape[0] // 2, p.shape[1])

    def gather_job(self, keys):
        return _gather_job([self.s16[k] for k in keys])

    def weights(self, keys, gathered):
        out = []
        for k, g in zip(keys, gathered):
            r, cc = self.shape[k]
            by_chip = _set_slot(g, self.s16[k], self.chip).reshape(N_CHIPS, r, cc)
            if k[0] == 'ssm_in_w':
                by_chip = by_chip.transpose(1, 0, 2).reshape(r, N_CHIPS * cc)
            elif k[0] != 'ffn_up_w':
                by_chip = by_chip.reshape(N_CHIPS * r, cc)
            out.append(by_chip)
        return out

    def by_halves(self, keys, grads):
        gs = []
        for k, g in zip(keys, grads):
            r, cc = self.shape[k]
            if k[0] == 'ssm_in_w':
                g = g.reshape(r, N_CHIPS, cc).transpose(1, 0, 2)
            gs.append(g.reshape(N_CHIPS, 2, r // 2, cc))
        return gs

    def pair_sums(self, gs, recv, tag):
        return [_pair_add(g, rv, self.c, out_dtype=BF16, name=f"rs_pair_add_{tag}{i}") for i, (g, rv) in enumerate(zip(gs, recv))]

    def chip_sums(self, pairs, scattered, tag):
        return [_sum_leading(_set_slot(s, lax.dynamic_index_in_dim(p, self.chip, axis=0, keepdims=False), self.chip),
                             name=f"rs_chip_sum_{tag}{i}") for i, (s, p) in enumerate(zip(scattered, pairs))]

    def shards(self, keys, halves):
        joined = _join_halves(halves, name="rs_half_join")
        return {k: _set_slot(j, h, self.c).reshape(self.shape[k]) for k, h, j in zip(keys, halves, joined)}


def _step(x, target, w, pieces):
    t = x.shape[0]
    g_n, heads = SSM_GROUPS, SSM_HEADS
    r_h = heads // g_n
    di = D_INNER
    zx_cols = di + CONV_DIM
    k_in = [('ssm_in_w', None)]
    k_ffn0 = [('ssm_out_w', None), ('ffn_up_w', 0), ('ffn_down_w', 0)]
    k_qkv = [('w_k', None), ('w_v', None), ('w_q', None)]
    k_late = [('w_o', None), ('ffn_up_w', 1), ('ffn_down_w', 1)]
    (w_in,) = pieces.weights(k_in, _run_job(pieces.gather_job(k_in), "gather_ssm_in"))
    w_zx = w_in[:, :zx_cols]
    w_dt = jnp.pad(w_in[:, zx_cols:], ((0, 0), (0, LANES - heads)))
    conv_w, conv_b = w['ssm_conv_w'][0], w['ssm_conv_b'][0]
    hp = jnp.stack([w['ssm_dt_bias'][0], w['ssm_a_log'][0], w['ssm_d'][0]], axis=0).reshape(3, g_n, r_h)
    hpc, hpr = hp.transpose(1, 0, 2), hp.transpose(1, 2, 0)

    h0 = x
    u0 = _rmsnorm_fwd(h0, w['ssm_norm_w'][0], name="ssm_norm")
    zx = _matmul(u0, w_zx, name="ssm_in_zx")
    dt_raw = _matmul(u0, w_dt, name="ssm_in_dt")[:, :heads]
    dtg = dt_raw.reshape(t, g_n, r_h)
    dtc, dtr = dtg.transpose(1, 0, 2), dtg.transpose(1, 2, 0)
    xbc = _conv_silu_fwd(zx, conv_w, conv_b, x_off=di, name="ssm_conv")
    y, prev, got = _ssd_fwd(xbc, dtc, dtr, hpc, hpr, side=pieces.gather_job(k_ffn0), name="ssd_fwd")
    w_out, w_up0, w_down0 = pieces.weights(k_ffn0, got)
    yn = _gate_norm_fwd(y, zx, w['ssm_gate_norm_w'][0], name="ssm_gate_norm")
    h1 = _matmul(yn, w_out, add=h0, name="ssm_out")
    h2, ffn0, got = _ffn_fwd(h1, w['ffn_norm_w'][0], w_up0, w['ffn_conv_w'][0], w['ffn_conv_b'][0], w_down0, 0,
                             side=pieces.gather_job(k_qkv))
    w_k, w_v, w_q = pieces.weights(k_qkv, got)
    hk = _rmsnorm_fwd(h2, w['kv_norm_w'], name="kv_norm")
    qn = _rmsnorm_fwd(h2, w['attn_norm_w'][0], name="attn_norm")
    k2 = _matmul(hk, w_k, out_dtype=BF16, name="attn_k")
    v2 = _matmul(hk, w_v, out_dtype=BF16, name="attn_v")
    q2 = _matmul(qn, w_q, out_dtype=BF16, name="attn_q")
    o2, lt, first, got = _sb_fwd(q2, k2, v2, SB_HEADS, side=pieces.gather_job(k_late), name="sb_fwd")
    w_o, w_up1, w_down1 = pieces.weights(k_late, got)
    h3 =_matmul(o2, w_o, add=h2, name="attn_o")
    h4, ffn1, _ = _ffn_fwd(h3, w['ffn_norm_w'][1], w_up1, w['ffn_conv_w'][1], w['ffn_conv_b'][1], w_down1, 1)
    loss_p, dh4, d_final = _loss_head(h4, w['final_norm_w'], target, name="loss_head")

    dh3, g1, _ = _ffn_bwd(h3, ffn1, dh4, w['ffn_norm_w'][1], w_up1, w['ffn_conv_w'][1], w['ffn_conv_b'][1], w_down1, 1)
    do2 = _matmul(dh3, w_o, tb=True, out_dtype=BF16, name="attn_o_dx")
    dw_o = _matmul(o2, dh3, ta=True, name="attn_o_dw")
    dq2, dk2, dv2 = _sb_bwd(q2, k2, v2, lt, first, do2, SB_HEADS, name="sb_bwd")
    dqn = _matmul(dq2, w_q, tb=True, name="attn_q_dx")
    dw_q = _matmul(qn, dq2, ta=True, name="attn_q_dw")
    dhk = _matmul(dk2, w_k, tb=True, name="attn_k_dx")
    dhk = _matmul(dv2, w_v, tb=True, add=dhk, name="attn_v_dx")
    dw_k = _matmul(hk, dk2, ta=True, name="attn_k_dw")
    dw_v = _matmul(hk, dv2, ta=True, name="attn_v_dw")
    dh2, (d_attn_norm, d_kv_norm) = _rmsnorm_bwd(h2, [(dqn, w['attn_norm_w'][0]), (dhk, w['kv_norm_w'])], dh3,
                                                 name="attn_norms_bwd")
    gs_late = pieces.by_halves(k_qkv + k_late, [dw_k, dw_v, dw_q, dw_o, g1['up'], g1['down']])
    dh1, g0, recv = _ffn_bwd(h1, ffn0, dh2, w['ffn_norm_w'][0], w_up0, w['ffn_conv_w'][0], w['ffn_conv_b'][0], w_down0, 0,
                             side=_swap_job(gs_late))
    pairs_late = pieces.pair_sums(gs_late, recv, "a")
    dyn = _matmul(dh1, w_out, tb=True, name="ssm_out_dx")
    dw_out = _matmul(yn, dh1, ta=True, name="ssm_out_dw")
    k_done = k_qkv + k_late + k_ffn0
    gs_ffn0 = pieces.by_halves(k_ffn0, [dw_out, g0['up'], g0['down']])
    dy, dz, d_gate, recv = _gate_norm_bwd(y, zx, w['ssm_gate_norm_w'][0], dyn, side=_swap_job(gs_ffn0),
                                          name="ssm_gate_norm_bwd")
    pairs_done = pairs_late + pieces.pair_sums(gs_ffn0, recv, "c")
    dxs, dbm, dcm, ddt_g, hg, scattered_done = _ssd_bwd(xbc, dtc, dtr, hpc, hpr, prev, dy,
                                                        side=_scatter_job(pairs_done), name="ssd_bwd")
    dzx, d_conv_w, d_conv_b = _conv_silu_bwd(zx, conv_w, conv_b, [dxs, dbm, dcm], x_off=di, into=dz, name="ssm_conv_bwd")
    ddt = jnp.pad(ddt_g.transpose(1, 0, 2).reshape(t, heads), ((0, 0), (0, LANES - heads)))
    du0 = _matmul(dzx, w_zx, tb=True, name="ssm_in_zx_dx")
    du0 = _matmul(ddt, w_dt, tb=True, add=du0, name="ssm_in_dt_dx")
    dw_in = jnp.concatenate([_matmul(u0, dzx, ta=True, name="ssm_in_zx_dw"),
                             _matmul(u0, ddt, ta=True, name="ssm_in_dt_dw")[:, :heads]], axis=1)
    dx, (d_ssm_norm,) = _rmsnorm_bwd(h0, [(du0, w['ssm_norm_w'][0])], dh1, name="ssm_norm_bwd")

    gs_in = pieces.by_halves(k_in, [dw_in])
    pairs_in = pieces.pair_sums(gs_in, _run_job(_swap_job(gs_in), "rs_pair_swap_b"), "b")
    scattered_in = _run_job(_scatter_job(pairs_in), "rs_chip_scatter_b")
    halves = pieces.chip_sums(pairs_done, scattered_done, "a") + pieces.chip_sums(pairs_in, scattered_in, "b")
    big_grads = pieces.shards(k_done + k_in, halves)

    hgr = hg.transpose(1, 0, 2).reshape(3, heads)
    grads = {
        'ssm_norm_w': d_ssm_norm, 'ssm_conv_w': d_conv_w[None], 'ssm_conv_b': d_conv_b,
        'ssm_dt_bias': hgr[0:1], 'ssm_a_log': hgr[1:2], 'ssm_d': hgr[2:3], 'ssm_gate_norm_w': d_gate,
        'kv_norm_w': d_kv_norm[0], 'attn_norm_w': d_attn_norm, 'ffn_norm_w': jnp.stack([g0['norm'], g1['norm']]),
        'ffn_conv_w': jnp.stack([g0['conv_w'], g1['conv_w']]), 'ffn_conv_b': jnp.stack([g0['conv_b'], g1['conv_b']]),
        'final_norm_w': d_final[0],
    }
    return loss_p, dx, grads, big_grads


def kernel(x, ssm_norm_w, ssm_in_w, ssm_conv_w, ssm_conv_b, ssm_dt_bias, ssm_a_log, ssm_d, ssm_gate_norm_w, ssm_out_w, kv_norm_w, w_k, w_v, attn_norm_w, w_q, w_o, ffn_norm_w, ffn_up_w, ffn_conv_w, ffn_conv_b, ffn_down_w, final_norm_w, loss_target, m_ssm_norm_w, m_ssm_in_w, m_ssm_conv_w, m_ssm_conv_b, m_ssm_dt_bias, m_ssm_a_log, m_ssm_d, m_ssm_gate_norm_w, m_ssm_out_w, m_kv_norm_w, m_w_k, m_w_v, m_attn_norm_w, m_w_q, m_w_o, m_ffn_norm_w, m_ffn_up_w, m_ffn_conv_w, m_ffn_conv_b, m_ffn_down_w, m_final_norm_w, v_ssm_norm_w, v_ssm_in_w, v_ssm_conv_w, v_ssm_conv_b, v_ssm_dt_bias, v_ssm_a_log, v_ssm_d, v_ssm_gate_norm_w, v_ssm_out_w, v_kv_norm_w, v_w_k, v_w_v, v_attn_norm_w, v_w_q, v_w_o, v_ffn_norm_w, v_ffn_up_w, v_ffn_conv_w, v_ffn_conv_b, v_ffn_down_w, v_final_norm_w):
    args = (ssm_norm_w, ssm_in_w, ssm_conv_w, ssm_conv_b, ssm_dt_bias, ssm_a_log, ssm_d, ssm_gate_norm_w, ssm_out_w, kv_norm_w, w_k, w_v, attn_norm_w, w_q, w_o, ffn_norm_w, ffn_up_w, ffn_conv_w, ffn_conv_b, ffn_down_w, final_norm_w)
    moms = (m_ssm_norm_w, m_ssm_in_w, m_ssm_conv_w, m_ssm_conv_b, m_ssm_dt_bias, m_ssm_a_log, m_ssm_d, m_ssm_gate_norm_w, m_ssm_out_w, m_kv_norm_w, m_w_k, m_w_v, m_attn_norm_w, m_w_q, m_w_o, m_ffn_norm_w, m_ffn_up_w, m_ffn_conv_w, m_ffn_conv_b, m_ffn_down_w, m_final_norm_w)
    vels = (v_ssm_norm_w, v_ssm_in_w, v_ssm_conv_w, v_ssm_conv_b, v_ssm_dt_bias, v_ssm_a_log, v_ssm_d, v_ssm_gate_norm_w, v_ssm_out_w, v_kv_norm_w, v_w_k, v_w_v, v_attn_norm_w, v_w_q, v_w_o, v_ffn_norm_w, v_ffn_up_w, v_ffn_conv_w, v_ffn_conv_b, v_ffn_down_w, v_final_norm_w)
    local = dict(zip(WEIGHTS, args))
    m_in = dict(zip(WEIGHTS, moms))
    v_in = dict(zip(WEIGHTS, vels))
    chip = 2 * lax.axis_index("x") + lax.axis_index("y")

    full = {n: local[n] for n in REPLICATED}
    small32 = _gather_chips(_pack([local[n].reshape(-1) for n in SMALL], F32, 8), name="gather_small")
    for n, st in zip(SMALL, _unpack(small32, [local[n].shape for n in SMALL])):
        full[n] = _from_shards(st, SHARD_AXIS[n])

    pieces = _Pieces(local)
    loss_p, dx, grads, big_grads = _step(x[0], loss_target[0], full, pieces)
    gshard = {}
    for n in BIG:
        if n in STACKED:
            gshard[n] = [big_grads[n, l] for l in range(local[n].shape[0])]
        else:
            gshard[n] = big_grads[n, None].reshape(local[n].shape)

    small = SMALL + REPLICATED
    rep = _pack([loss_p.reshape(-1)] + [grads[n].reshape(-1) for n in small], F32, 8)
    tot = _sum_leading(_gather_all(rep, name="ar_gather"), name="ar_sum")
    parts = _unpack(tot, [(LANES,)] + [grads[n].shape for n in small])
    loss = jnp.sum(parts[0])
    for n, g in zip(small, parts[1:]):
        if n in SHARD_AXIS:
            size = local[n].shape[SHARD_AXIS[n]]
            g = lax.dynamic_slice_in_dim(g, chip * size, size, axis=SHARD_AXIS[n])
        gshard[n] = g

    grads_out, deltas, new_m, new_v = [], [], [], []
    for n in WEIGHTS:
        if n in STACKED:
            g, d, nm, nv = _adamw_layers(local[n], gshard[n], m_in[n], v_in[n], name=f"adamw_{n}")
        else:
            g = gshard[n]
            d, nm, nv = _adamw(local[n], g, m_in[n], v_in[n], name=f"adamw_{n}")
        grads_out.append(g)
        deltas.append(d)
        new_m.append(nm)
        new_v.append(nv)
    return (loss, dx[None], *grads_out, *deltas, *new_m, *new_v)
```

```python
import functools
import math

import jax
import jax.numpy as jnp
from jax import lax
from jax.experimental import pallas as pl
from jax.experimental.pallas import tpu as pltpu

D_INNER = 2048
SSM_HEAD_DIM = 64
SSM_HEADS = 32
SSM_GROUPS = 4
SSM_STATE = 128
SSM_CHUNK = 128
GN = SSM_GROUPS * SSM_STATE
CONV_DIM = D_INNER + 2 * GN
SB_HEADS = 16
EPS = 1e-6
ADAM_LR = 0.001
ADAM_B1 = 0.9
ADAM_B2 = 0.999
ADAM_EPS = 1e-08
ADAM_WD = 0.01
ADAM_STEP = 10

LANES = 128
SUBLANES = 8
VMEM_LIMIT = 48 * 1024 * 1024
ADAM_BLOCK_BYTES = 2 << 20
F32 = jnp.float32
BF16 = jnp.bfloat16
MESH = pl.DeviceIdType.MESH


def _cparams(sem=None):
    return pltpu.CompilerParams(dimension_semantics=sem, vmem_limit_bytes=VMEM_LIMIT)


class _SideJob:
    def __init__(self, ins, out_shapes, n_sems, start, finish):
        self.ins, self.out_shapes, self.n_sems, self.start, self.finish = ins, out_shapes, n_sems, start, finish


def _call(body, *, grid, in_specs, out_specs, out_shape, scratch_shapes=(), sem, name, args, side=None):
    in_specs, out_specs, out_shape, scratch_shapes = list(in_specs), list(out_specs), list(out_shape), list(scratch_shapes)
    n_in, n_out = len(in_specs), len(out_specs)
    if side is None:
        outs = pl.pallas_call(body, grid=grid, in_specs=in_specs, out_specs=out_specs, out_shape=out_shape,
                              scratch_shapes=scratch_shapes, compiler_params=_cparams(sem), name=name)(*args)
        return list(outs), []
    k_in, k_out = len(side.ins), len(side.out_shapes)

    def wrapped(*refs):
        ins, s_ins = refs[:n_in], refs[n_in:n_in + k_in]
        o0 = n_in + k_in
        outs, s_outs = refs[o0:o0 + n_out], refs[o0 + n_out:o0 + n_out + k_out]
        scratch, send_sems, recv_sems = refs[o0 + n_out + k_out:-2], refs[-2], refs[-1]
        ids = [pl.program_id(a) for a in range(len(grid))]
        first = functools.reduce(jnp.logical_and, [p == 0 for p in ids])
        last = functools.reduce(jnp.logical_and, [p == g - 1 for p, g in zip(ids, grid)])

        @pl.when(first)
        def _():
            side.start(s_ins, s_outs, send_sems, recv_sems)

        body(*ins, *outs, *scratch)

        @pl.when(last)
        def _():
            side.finish(s_ins, s_outs, send_sems, recv_sems)

    outs = pl.pallas_call(
        wrapped, grid=grid, in_specs=in_specs + [ANY] * k_in, out_specs=out_specs + [ANY] * k_out,
        out_shape=out_shape + list(side.out_shapes),
        scratch_shapes=scratch_shapes + [pltpu.SemaphoreType.DMA((side.n_sems,)), pltpu.SemaphoreType.DMA((side.n_sems,))],
        compiler_params=_cparams(tuple("arbitrary" for _ in grid)), name=name)(*args, *side.ins)
    return list(outs[:n_out]), list(outs[n_out:])


def _tile(n, cands):
    for c in cands:
        if n % c == 0:
            return c
    return n


def _nt(a, b):
    return lax.dot_general(a, b, (((1,), (1,)), ((), ())), preferred_element_type=F32)


def _tn(a, b):
    return lax.dot_general(a, b, (((0,), (0,)), ((), ())), preferred_element_type=F32)


def _nn(a, b):
    return jnp.dot(a, b, preferred_element_type=F32)


def _split(x, pieces):
    out = []
    for _ in range(pieces - 1):
        h = x.astype(BF16)
        out.append(h)
        x = x - h.astype(F32)
    out.append(x.astype(BF16))
    return out


def _ones_dot(ones, x, *, ones_left, pieces=3):
    o16 = ones.astype(BF16)
    acc = None
    for piece in _split(x, pieces):
        term = _nn(o16, piece) if ones_left else _nn(piece, o16)
        acc = term if acc is None else acc + term
    return acc


def _row_sums(x, pieces=2):
    return _ones_dot(jnp.ones((x.shape[1], LANES), F32), x, ones_left=False, pieces=pieces)


def _softplus(x):
    return jnp.maximum(x, 0.0) + jnp.log(1.0 + jnp.exp(-jnp.abs(x)))


def _sigmoid(x):
    return 0.5 * jnp.tanh(0.5 * x) + 0.5


MM_TILE_MAX = 1408
MM_VMEM_BUDGET = 40 * 1024 * 1024


def _divisors(n, cap):
    out = [d for d in range(min(cap, n) // LANES * LANES, 0, -LANES) if n % d == 0]
    return out or [n]


def _mm_tiles(m, n, k, a_bytes, b_bytes, o_bytes, add_bytes):
    best = None
    for tm in _divisors(m, MM_TILE_MAX):
        for tn in _divisors(n, MM_TILE_MAX):
            for tk in _divisors(k, MM_TILE_MAX):
                vmem = 2 * (tm * tk * a_bytes + tk * tn * b_bytes + tm * tn * (o_bytes + add_bytes)) + tm * tn * 4
                if vmem > MM_VMEM_BUDGET:
                    continue
                score = (tm * tn * tk, tm * tn)
                if best is None or score > best[0]:
                    best = (score, (tm, tn, tk))
    return best[1]


def _matmul(a, b, *, ta=False, tb=False, add=None, out_dtype=F32, out_parts=1, name):
    a_parts = a.shape[0] if a.ndim == 3 else 1
    b_parts = b.shape[0] if b.ndim == 3 else 1
    assert not (ta and a_parts > 1)
    a2, b2 = a.shape[-2:], b.shape[-2:]
    m, k = (a2[1], a2[0]) if ta else (a2[0], a2[1] * a_parts)
    n, kb = (b2[0], b2[1] * b_parts) if tb else (b2[1] * b_parts, b2[0])
    assert kb == k, (a.shape, b.shape)
    n_unit = math.gcd(n // out_parts, n if tb else b2[1])
    k_unit = math.gcd(k // a_parts, b2[1] if tb else k)
    tm, tn, tk = _mm_tiles(m, n_unit, k_unit, a.dtype.itemsize, b.dtype.itemsize, jnp.dtype(out_dtype).itemsize,
                           0 if add is None else add.dtype.itemsize)
    nk = k // tk
    ka, kbp = (k // a_parts) // tk, (k // b_parts) // tk
    nb, no = (n // b_parts) // tn, (n // out_parts) // tn

    def body(*refs):
        if add is None:
            a_ref, b_ref, o_ref = refs[:3]
            add_ref = None
        else:
            a_ref, b_ref, add_ref, o_ref = refs[:4]
        kk = pl.program_id(2)
        dn = (((0 if ta else 1,), (1 if tb else 0,)), ((), ()))
        prod = lax.dot_general(a_ref[...].astype(BF16), b_ref[...].astype(BF16), dn, preferred_element_type=F32)

        def finish(r):
            if add_ref is not None:
                r = r + add_ref[...].astype(F32)
            o_ref[...] = r.astype(o_ref.dtype)

        if nk == 1:
            finish(prod)
            return
        acc_ref = refs[-1]

        @pl.when(kk == 0)
        def _():
            acc_ref[...] = prod

        @pl.when(jnp.logical_and(kk > 0, kk < nk - 1))
        def _():
            acc_ref[...] += prod

        @pl.when(kk == nk - 1)
        def _():
            finish(acc_ref[...] + prod)

    if ta:
        a_spec = pl.BlockSpec((tk, tm), lambda i, j, kk: (kk, i))
    elif a_parts > 1:
        a_spec = pl.BlockSpec((None, tm, tk), lambda i, j, kk: (kk // ka, i, kk % ka))
    else:
        a_spec = pl.BlockSpec((tm, tk), lambda i, j, kk: (i, kk))
    if b_parts == 1:
        b_spec = pl.BlockSpec((tn, tk), lambda i, j, kk: (j, kk)) if tb else pl.BlockSpec((tk, tn), lambda i, j, kk: (kk, j))
    elif tb:
        b_spec = pl.BlockSpec((None, tn, tk), lambda i, j, kk: (kk // kbp, j, kk % kbp))
    else:
        b_spec = pl.BlockSpec((None, tk, tn), lambda i, j, kk: (j // nb, kk, j % nb))
    if out_parts > 1:
        o_spec = pl.BlockSpec((None, tm, tn), lambda i, j, kk: (j // no, i, j % no))
        o_shape = jax.ShapeDtypeStruct((out_parts, m, n // out_parts), out_dtype)
    else:
        o_spec = pl.BlockSpec((tm, tn), lambda i, j, kk: (i, j))
        o_shape = jax.ShapeDtypeStruct((m, n), out_dtype)
    in_specs = [a_spec, b_spec]
    args = [a, b]
    if add is not None:
        in_specs.append(pl.BlockSpec((tm, tn), lambda i, j, kk: (i, j)))
        args.append(add)
    return pl.pallas_call(
        body,
        grid=(m // tm, n // tn, nk),
        in_specs=in_specs,
        out_specs=o_spec,
        out_shape=o_shape,
        scratch_shapes=[pltpu.VMEM((tm, tn), F32)] if nk > 1 else [],
        compiler_params=_cparams(("parallel", "parallel", "arbitrary")),
        name=name,
    )(*args)


def _rmsnorm_fwd(x, w, *, name):
    t, d = x.shape
    tb = _tile(t, (512, 256, 128))

    def body(x_ref, w_ref, o_ref):
        xv = x_ref[...]
        r = lax.rsqrt(jnp.mean(xv * xv, axis=-1, keepdims=True) + EPS)
        o_ref[...] = (xv * r * w_ref[...]).astype(o_ref.dtype)

    return pl.pallas_call(
        body,
        grid=(t // tb,),
        in_specs=[pl.BlockSpec((tb, d), lambda i: (i, 0)), pl.BlockSpec((1, d), lambda i: (0, 0))],
        out_specs=pl.BlockSpec((tb, d), lambda i: (i, 0)),
        out_shape=jax.ShapeDtypeStruct((t, d), BF16),
        compiler_params=_cparams(("parallel",)),
        name=name,
    )(x, w.reshape(1, d))


def _rmsnorm_bwd(x, dys, dres, *, name):
    t, d = x.shape
    tb = _tile(t, (512, 256, 128))
    nn = len(dys)
    has_res = dres is not None

    def body(*refs):
        x_ref = refs[0]
        dy_refs = refs[1:1 + nn]
        w_refs = refs[1 + nn:1 + 2 * nn]
        pos = 1 + 2 * nn
        res_ref = refs[pos] if has_res else None
        pos += 1 if has_res else 0
        dx_ref = refs[pos]
        dw_refs = refs[pos + 1:pos + 1 + nn]
        i = pl.program_id(0)
        xv = x_ref[...]
        r = lax.rsqrt(jnp.mean(xv * xv, axis=-1, keepdims=True) + EPS)
        xn = xv * r
        dx = res_ref[...] if has_res else jnp.zeros_like(xv)
        for q in range(nn):
            dy = dy_refs[q][...].astype(F32)
            g = dy * w_refs[q][...]
            dx = dx + r * (g - xn * jnp.mean(g * xn, axis=-1, keepdims=True))
            dwp = jnp.sum(dy * xn, axis=0, keepdims=True)

            @pl.when(i == 0)
            def _(q=q, dwp=dwp):
                dw_refs[q][...] = dwp

            @pl.when(i > 0)
            def _(q=q, dwp=dwp):
                dw_refs[q][...] += dwp
        dx_ref[...] = dx

    row = pl.BlockSpec((tb, d), lambda i: (i, 0))
    vec = pl.BlockSpec((1, d), lambda i: (0, 0))
    in_specs = [row] + [row] * nn + [vec] * nn + ([row] if has_res else [])
    args = [x] + [p[0] for p in dys] + [p[1].reshape(1, d) for p in dys] + ([dres] if has_res else [])
    outs = pl.pallas_call(
        body,
        grid=(t // tb,),
        in_specs=in_specs,
        out_specs=[row] + [vec] * nn,
        out_shape=[jax.ShapeDtypeStruct((t, d), F32)] + [jax.ShapeDtypeStruct((1, d), F32)] * nn,
        compiler_params=_cparams(("arbitrary",)),
        name=name,
    )(*args)
    return outs[0], list(outs[1:])


def _loss_head(x, w, target, *, name):
    t, d = x.shape
    tb = _tile(t, (512, 256, 128))

    def body(x_ref, w_ref, t_ref, loss_ref, dx_ref, dw_ref):
        i = pl.program_id(0)
        xv = x_ref[...]
        wv = w_ref[...]
        r = lax.rsqrt(jnp.mean(xv * xv, axis=-1, keepdims=True) + EPS)
        xn = xv * r
        e = xn * wv - t_ref[...]
        lp = 0.5 * jnp.sum(jnp.mean(e * e, axis=-1, keepdims=True), axis=0, keepdims=True)
        dy = e * (1.0 / d)
        g = dy * wv
        dx_ref[...] = r * (g - xn * jnp.mean(g * xn, axis=-1, keepdims=True))
        dwp = jnp.sum(dy * xn, axis=0, keepdims=True)
        lpv = jnp.broadcast_to(lp, (1, LANES)) * (1.0 / LANES)

        @pl.when(i == 0)
        def _():
            dw_ref[...] = dwp
            loss_ref[...] = lpv

        @pl.when(i > 0)
        def _():
            dw_ref[...] += dwp
            loss_ref[...] += lpv

    row = pl.BlockSpec((tb, d), lambda i: (i, 0))
    vec = pl.BlockSpec((1, d), lambda i: (0, 0))
    return pl.pallas_call(
        body,
        grid=(t // tb,),
        in_specs=[row, vec, row],
        out_specs=[pl.BlockSpec((1, LANES), lambda i: (0, 0)), row, vec],
        out_shape=[jax.ShapeDtypeStruct((1, LANES), F32), jax.ShapeDtypeStruct((t, d), F32),
                   jax.ShapeDtypeStruct((1, d), F32)],
        compiler_params=_cparams(("arbitrary",)),
        name=name,
    )(x, w.reshape(1, d), target)


ROW_CHUNK = 64
PAD = SUBLANES


class _Strip:
    def __init__(self, head_ref, x_ref, rows):
        self.head_ref, self.x_ref = head_ref, x_ref
        head_ref[0:PAD, :] = jnp.zeros((PAD, head_ref.shape[1]), F32)
        head_ref[pl.ds(PAD, rows), :] = x_ref[pl.ds(0, rows), :]

    def rows(self, r0, rows, back):
        if r0 == 0:
            return self.head_ref[pl.ds(PAD - back, rows), :]
        return self.x_ref[pl.ds(r0 - back, rows), :]


def _shifted(strip, r0, rows, back):
    return strip.rows(r0, rows, back)


def _conv_taps(strip, w_ref, r0, rows, kw):
    acc = None
    for j in range(kw):
        term = _shifted(strip, r0, rows, kw - 1 - j) * w_ref[j:j + 1, :]
        acc = term if acc is None else acc + term
    return acc


def _fill_pad(pad_ref, x_ref, rows):
    return _Strip(pad_ref, x_ref, rows)


def _conv_silu_fwd(x, w, b, *, x_off=0, name):
    t = x.shape[0]
    kw, c = w.shape
    cw = _tile(math.gcd(c, x_off) if x_off else c, (256, 128))
    ob = x_off // cw
    rc = _tile(t, (ROW_CHUNK,))

    def body(x_ref, w_ref, b_ref, o_ref, pad_ref):
        xs = _fill_pad(pad_ref, x_ref, rc)
        for r0 in range(0, t, rc):
            pre = _conv_taps(xs, w_ref, r0, rc, kw) + b_ref[...]
            o_ref[pl.ds(r0, rc), :] = pre * _sigmoid(pre)

    strip = pl.BlockSpec((t, cw), lambda i: (0, i))
    return pl.pallas_call(
        body,
        grid=(c // cw,),
        in_specs=[pl.BlockSpec((t, cw), lambda i: (0, i + ob)), pl.BlockSpec((kw, cw), lambda i: (0, i)),
                  pl.BlockSpec((1, cw), lambda i: (0, i))],
        out_specs=strip,
        out_shape=jax.ShapeDtypeStruct((t, c), F32),
        scratch_shapes=[pltpu.VMEM((PAD + rc, cw), F32)],
        compiler_params=_cparams(("parallel",)),
        name=name,
    )(x, w, b.reshape(1, c))


def _conv_bwd_core(dpre_pad_ref, x_pad_ref, w_ref, dx_ref, dw_ref, db_ref, t, rc, kw):
    cw = dx_ref.shape[1]

    def fold(a):
        return jnp.sum(a.reshape(rc // SUBLANES, SUBLANES, cw), axis=0) if rc % SUBLANES == 0 else jnp.sum(a, axis=0, keepdims=True)

    dws = [None] * kw
    dbs = None
    for r0 in range(0, t, rc):
        dpre = dpre_pad_ref[pl.ds(PAD + r0, rc), :]
        dx = None
        for j in range(kw):
            s = kw - 1 - j
            term = dpre_pad_ref[pl.ds(PAD + r0 + s, rc), :] * w_ref[j:j + 1, :]
            dx = term if dx is None else dx + term
            part = fold(dpre * _shifted(x_pad_ref, r0, rc, s))
            dws[j] = part if dws[j] is None else dws[j] + part
        part = fold(dpre)
        dbs = part if dbs is None else dbs + part
        dx_ref[pl.ds(r0, rc), :] = dx
    for j in range(kw):
        dw_ref[j:j + 1, :] = jnp.sum(dws[j], axis=0, keepdims=True)
    db_ref[...] = jnp.sum(dbs, axis=0, keepdims=True)


def _conv_silu_bwd(x, w, b, dact, *, x_off=0, into=None, name):
    t = x.shape[0]
    kw, c = w.shape
    parts = dact if isinstance(dact, (list, tuple)) else [dact]
    widths = [p.shape[1] for p in parts]
    assert sum(widths) == c
    cw = _tile(functools.reduce(math.gcd, widths + [x_off or c]), (256, 128) if len(parts) == 1 else (128,))
    ob = x_off // cw
    rc = _tile(t, (ROW_CHUNK,))
    firsts = [sum(widths[:p]) // cw for p in range(len(parts))]
    counts = [wd // cw for wd in widths]
    n_p = len(parts)

    def body(x_ref, w_ref, b_ref, *rest):
        da_refs = rest[:n_p]
        dx_ref, dw_ref, db_ref, xpad_ref, dpad_ref = rest[-5 - (n_p > 1):][:5]
        if n_p > 1:
            da_ref = rest[-1]
            i = pl.program_id(0)
            for p in range(n_p):
                @pl.when(jnp.logical_and(i >= firsts[p], i < firsts[p] + counts[p]))
                def _(p=p):
                    da_ref[...] = da_refs[p][...]
        else:
            da_ref = da_refs[0]
        xs = _fill_pad(xpad_ref, x_ref, rc)
        dpad_ref[0:PAD, :] = jnp.zeros((PAD, cw), F32)
        dpad_ref[pl.ds(PAD + t, PAD), :] = jnp.zeros((PAD, cw), F32)
        for r0 in range(0, t, rc):
            pre = _conv_taps(xs, w_ref, r0, rc, kw) + b_ref[...]
            sg = _sigmoid(pre)
            dpad_ref[pl.ds(PAD + r0, rc), :] = da_ref[pl.ds(r0, rc), :] * (sg * (1.0 + pre * (1.0 - sg)))
        _conv_bwd_core(dpad_ref, xs, w_ref, dx_ref, dw_ref, db_ref, t, rc, kw)

    strip = pl.BlockSpec((t, cw), lambda i: (0, i))
    wspec = pl.BlockSpec((kw, cw), lambda i: (0, i))
    bspec = pl.BlockSpec((1, cw), lambda i: (0, i))
    xspec = pl.BlockSpec((t, cw), lambda i: (0, i + ob))
    dspecs = [pl.BlockSpec((t, cw), lambda i, f=f, n=n: (0, jnp.clip(i - f, 0, n - 1))) for f, n in zip(firsts, counts)]
    extra = {} if into is None else dict(input_output_aliases={3 + n_p: 0})
    pad = pltpu.VMEM((t + 2 * PAD, cw), F32)
    return pl.pallas_call(
        body,
        grid=(c // cw,),
        in_specs=[xspec, wspec, bspec] + dspecs + ([] if into is None else [ANY]),
        out_specs=[strip if into is None else xspec, wspec, bspec],
        out_shape=[jax.ShapeDtypeStruct((t, c) if into is None else into.shape, F32), jax.ShapeDtypeStruct((kw, c), F32),
                   jax.ShapeDtypeStruct((1, c), F32)],
        scratch_shapes=[pad, pad] + ([pltpu.VMEM((t, cw), F32)] if n_p > 1 else []),
        compiler_params=_cparams(("arbitrary",)),
        name=name,
        **extra,
    )(x, w, b.reshape(1, c), *parts, *([] if into is None else [into]))


def _conv_glu_fwd(hid, w, b, *, side=None, name):
    t, c2 = hid.shape
    f = c2 // 2
    kw = w.shape[0]
    cw = _tile(f, (256, 128))
    nf = f // cw
    rc = _tile(t, (ROW_CHUNK,))

    def body(g_ref, v_ref, wg_ref, wv_ref, bg_ref, bv_ref, o_ref, gpad_ref, vpad_ref):
        gs_, vs_ = _fill_pad(gpad_ref, g_ref, rc), _fill_pad(vpad_ref, v_ref, rc)
        for r0 in range(0, t, rc):
            gate = _conv_taps(gs_, wg_ref, r0, rc, kw) + bg_ref[...]
            val = _conv_taps(vs_, wv_ref, r0, rc, kw) + bv_ref[...]
            o_ref[pl.ds(r0, rc), :] = (gate * _sigmoid(gate) * val).astype(o_ref.dtype)

    gs = pl.BlockSpec((t, cw), lambda i: (0, i))
    vs = pl.BlockSpec((t, cw), lambda i: (0, i + nf))
    b2 = b.reshape(1, c2)
    (act,), side_outs = _call(
        body,
        grid=(nf,),
        in_specs=[gs, vs, pl.BlockSpec((kw, cw), lambda i: (0, i)), pl.BlockSpec((kw, cw), lambda i: (0, i + nf)),
                  pl.BlockSpec((1, cw), lambda i: (0, i)), pl.BlockSpec((1, cw), lambda i: (0, i + nf))],
        out_specs=[gs],
        out_shape=[jax.ShapeDtypeStruct((t, f), BF16)],
        scratch_shapes=[pltpu.VMEM((PAD + rc, cw), F32), pltpu.VMEM((PAD + rc, cw), F32)],
        sem=("parallel",),
        name=name,
        args=(hid, hid, w, w, b2, b2),
        side=side,
    )
    return act, side_outs


def _conv_glu_bwd(hid, w, b, dact, *, side=None, name):
    t, c2 = hid.shape
    f = c2 // 2
    kw = w.shape[0]
    cw = _tile(f, (128,))
    nf = f // cw
    rc = _tile(t, (ROW_CHUNK,))

    def body(g_ref, v_ref, wg_ref, wv_ref, bg_ref, bv_ref, da_ref,
             dgv_ref, dwg_ref, dwv_ref, dbg_ref, dbv_ref,
             gpad_ref, vpad_ref, dgpad_ref, dvpad_ref):
        gs_, vs_ = _fill_pad(gpad_ref, g_ref, rc), _fill_pad(vpad_ref, v_ref, rc)
        for ref in (dgpad_ref, dvpad_ref):
            ref[0:PAD, :] = jnp.zeros((PAD, cw), F32)
            ref[pl.ds(PAD + t, PAD), :] = jnp.zeros((PAD, cw), F32)
        for r0 in range(0, t, rc):
            gate = _conv_taps(gs_, wg_ref, r0, rc, kw) + bg_ref[...]
            val = _conv_taps(vs_, wv_ref, r0, rc, kw) + bv_ref[...]
            sg = _sigmoid(gate)
            da = da_ref[pl.ds(r0, rc), :].astype(F32)
            dgpad_ref[pl.ds(PAD + r0, rc), :] = da * val * (sg * (1.0 + gate * (1.0 - sg)))
            dvpad_ref[pl.ds(PAD + r0, rc), :] = da * (gate * sg)
        _conv_bwd_core(dgpad_ref, gs_, wg_ref, dgv_ref.at[0], dwg_ref, dbg_ref, t, rc, kw)
        _conv_bwd_core(dvpad_ref, vs_, wv_ref, dgv_ref.at[1], dwv_ref, dbv_ref, t, rc, kw)

    gs = pl.BlockSpec((t, cw), lambda i: (0, i))
    vs = pl.BlockSpec((t, cw), lambda i: (0, i + nf))
    wg = pl.BlockSpec((kw, cw), lambda i: (0, i))
    wv = pl.BlockSpec((kw, cw), lambda i: (0, i + nf))
    bg = pl.BlockSpec((1, cw), lambda i: (0, i))
    bv = pl.BlockSpec((1, cw), lambda i: (0, i + nf))
    b2 = b.reshape(1, c2)
    pad = pltpu.VMEM((t + 2 * PAD, cw), F32)
    outs, side_outs = _call(
        body,
        grid=(nf,),
        in_specs=[gs, vs, wg, wv, bg, bv, gs],
        out_specs=[pl.BlockSpec((2, t, cw), lambda i: (0, 0, i)), wg, wg, bg, bg],
        out_shape=[jax.ShapeDtypeStruct((2, t, f), F32),
                   jax.ShapeDtypeStruct((kw, f), F32), jax.ShapeDtypeStruct((kw, f), F32),
                   jax.ShapeDtypeStruct((1, f), F32), jax.ShapeDtypeStruct((1, f), F32)],
        scratch_shapes=[pad, pad, pad, pad],
        sem=("parallel",),
        name=name,
        args=(hid, hid, w, w, b2, b2, dact),
        side=side,
    )
    return (*outs, side_outs)


def _gate_norm_fwd(y, zx, w, *, name):
    t, di = y.shape
    gsz = di // SSM_GROUPS
    tb = _tile(t, (256, 128))

    def body(y_ref, z_ref, w_ref, o_ref):
        for g in range(SSM_GROUPS):
            sl = slice(g * gsz, (g + 1) * gsz)
            zv = z_ref[:, sl]
            gv = y_ref[:, sl] * (zv * _sigmoid(zv))
            r = lax.rsqrt(jnp.mean(gv * gv, axis=-1, keepdims=True) + EPS)
            o_ref[:, sl] = (gv * r * w_ref[:, sl]).astype(o_ref.dtype)

    row = pl.BlockSpec((tb, di), lambda i: (i, 0))
    return pl.pallas_call(
        body,
        grid=(t // tb,),
        in_specs=[row, row, pl.BlockSpec((1, di), lambda i: (0, 0))],
        out_specs=row,
        out_shape=jax.ShapeDtypeStruct((t, di), BF16),
        compiler_params=_cparams(("parallel",)),
        name=name,
    )(y, zx, w.reshape(1, di))


def _gate_norm_bwd(y, zx, w, dyn, *, side=None, name):
    t, di = y.shape
    gsz = di // SSM_GROUPS
    tb = _tile(t, (256, 128))

    def body(y_ref, z_ref, w_ref, d_ref, dy_ref, dz_ref, dw_ref):
        i = pl.program_id(0)
        for g in range(SSM_GROUPS):
            sl = slice(g * gsz, (g + 1) * gsz)
            zv = z_ref[:, sl]
            yv = y_ref[:, sl]
            sg = _sigmoid(zv)
            sz = zv * sg
            gv = yv * sz
            r = lax.rsqrt(jnp.mean(gv * gv, axis=-1, keepdims=True) + EPS)
            gn = gv * r
            dn = d_ref[:, sl].astype(F32)
            q = dn * w_ref[:, sl]
            dg = r * (q - gn * jnp.mean(q * gn, axis=-1, keepdims=True))
            dy_ref[:, sl] = dg * sz
            dz_ref[:, sl] = dg * yv * (sg * (1.0 + zv * (1.0 - sg)))
            dwp = jnp.sum(dn * gn, axis=0, keepdims=True)

            @pl.when(i == 0)
            def _(sl=sl, dwp=dwp):
                dw_ref[:, sl] = dwp

            @pl.when(i > 0)
            def _(sl=sl, dwp=dwp):
                dw_ref[:, sl] += dwp

    row = pl.BlockSpec((tb, di), lambda i: (i, 0))
    vec = pl.BlockSpec((1, di), lambda i: (0, 0))
    outs, side_outs = _call(
        body,
        grid=(t // tb,),
        in_specs=[row, row, vec, row],
        out_specs=[row, row, vec],
        out_shape=[jax.ShapeDtypeStruct((t, di), F32), jax.ShapeDtypeStruct((t, zx.shape[1]), F32),
                   jax.ShapeDtypeStruct((1, di), F32)],
        sem=("arbitrary",),
        name=name,
        args=(y, zx, w.reshape(1, di), dyn),
        side=side,
    )
    return (*outs, side_outs)


def _adamw(w, g, m, v, *, name):
    shape = w.shape
    cols = shape[-1]
    rows = w.size // cols
    w2, g2, m2, v2 = (a.reshape(rows, cols) for a in (w, g, m, v))
    tr = rows if rows * cols * 4 <= ADAM_BLOCK_BYTES else _row_tile(rows, cols)
    c1 = 1.0 - ADAM_B1 ** ADAM_STEP
    c2 = 1.0 - ADAM_B2 ** ADAM_STEP

    def body(w_ref, g_ref, m_ref, v_ref, d_ref, nm_ref, nv_ref):
        gv = g_ref[...]
        nm = ADAM_B1 * m_ref[...] + (1.0 - ADAM_B1) * gv
        nv = ADAM_B2 * v_ref[...] + (1.0 - ADAM_B2) * (gv * gv)
        d_ref[...] = -ADAM_LR * ((nm / c1) / (jnp.sqrt(nv / c2) + ADAM_EPS) + ADAM_WD * w_ref[...])
        nm_ref[...] = nm
        nv_ref[...] = nv

    blk = pl.BlockSpec((tr, cols), lambda i: (i, 0))
    outs = pl.pallas_call(
        body,
        grid=(rows // tr,),
        in_specs=[blk] * 4,
        out_specs=[blk] * 3,
        out_shape=[jax.ShapeDtypeStruct((rows, cols), F32)] * 3,
        compiler_params=_cparams(("parallel",)),
        name=name,
    )(w2, g2, m2, v2)
    return tuple(o.reshape(shape) for o in outs)


def _adamw_layers(w, gs, m, v, *, name):
    n_l, rows, cols = w.shape
    assert len(gs) == n_l
    tr = _row_tile(rows, cols)
    c1 = 1.0 - ADAM_B1 ** ADAM_STEP
    c2 = 1.0 - ADAM_B2 ** ADAM_STEP

    def body(*refs):
        w_ref, m_ref, v_ref = refs[:3]
        g_refs = refs[3:3 + n_l]
        g_ref, d_ref, nm_ref, nv_ref = refs[3 + n_l:]
        layer = pl.program_id(0)
        gv = g_refs[0][...]
        for q in range(1, n_l):
            gv = jnp.where(layer == q, g_refs[q][...], gv)
        nm = ADAM_B1 * m_ref[...] + (1.0 - ADAM_B1) * gv
        nv = ADAM_B2 * v_ref[...] + (1.0 - ADAM_B2) * (gv * gv)
        g_ref[...] = gv
        d_ref[...] = -ADAM_LR * ((nm / c1) / (jnp.sqrt(nv / c2) + ADAM_EPS) + ADAM_WD * w_ref[...])
        nm_ref[...] = nm
        nv_ref[...] = nv

    stacked = pl.BlockSpec((None, tr, cols), lambda l, i: (l, i, 0))
    single = pl.BlockSpec((tr, cols), lambda l, i: (i, 0))
    return pl.pallas_call(
        body,
        grid=(n_l, rows // tr),
        in_specs=[stacked] * 3 + [single] * n_l,
        out_specs=[stacked] * 4,
        out_shape=[jax.ShapeDtypeStruct(w.shape, F32)] * 4,
        compiler_params=_cparams(("parallel", "parallel")),
        name=name,
    )(w, m, v, *gs)


def _ssd_scalars(dtc_ref, dtr_ref, hpc_ref, hpr_ref, ln):
    assert SSM_CHUNK == SSM_STATE == LANES, "the SSD kernels mix chunk, state and lane-wide tiles freely"
    bias_c, alog_c = hpc_ref[0, 0:1, :], hpc_ref[0, 1:2, :]
    bias_r, alog_r = hpr_ref[0, :, 0:1], hpr_ref[0, :, 1:2]
    a_c, a_r = -jnp.exp(alog_c), -jnp.exp(alog_r)
    raw_c = dtc_ref[0] + bias_c
    dt_c = _softplus(raw_c)
    dt_r = _softplus(dtr_ref[0] + bias_r)
    row = lax.broadcasted_iota(jnp.int32, (ln, ln), 0)
    col = lax.broadcasted_iota(jnp.int32, (ln, ln), 1)
    lower = (col <= row).astype(F32)
    upper = (row <= col).astype(F32)
    acs_c = _ones_dot(lower, dt_c * a_c, ones_left=True)
    acs_r = _ones_dot(upper, dt_r * a_r, ones_left=False)
    return raw_c, dt_c, a_c, acs_c, acs_r, row, col


def _ssd_specs(t, di, g_n, n_st, rp, ln, r_h, rev):
    nc = t // ln
    cidx = (lambda c: nc - 1 - c) if rev else (lambda c: c)
    xs = pl.BlockSpec((ln, rp), lambda g, c: (cidx(c), g))
    bm = pl.BlockSpec((ln, n_st), lambda g, c: (cidx(c), di // n_st + g))
    cm = pl.BlockSpec((ln, n_st), lambda g, c: (cidx(c), di // n_st + g_n + g))
    dtc = pl.BlockSpec((1, ln, r_h), lambda g, c: (g, cidx(c), 0))
    dtr = pl.BlockSpec((1, r_h, ln), lambda g, c: (g, 0, cidx(c)))
    hpc = pl.BlockSpec((1, 3, r_h), lambda g, c: (g, 0, 0))
    hpr = pl.BlockSpec((1, r_h, 3), lambda g, c: (g, 0, 0))
    prev = pl.BlockSpec((1, rp, n_st), lambda g, c: (cidx(c), g, 0))
    return xs, bm, cm, dtc, dtr, hpc, hpr, prev


def _ssd_fwd(xbc, dtc, dtr, hpc, hpr, *, side=None, name):
    t = xbc.shape[0]
    di, g_n, n_st, p_h, ln = D_INNER, SSM_GROUPS, SSM_STATE, SSM_HEAD_DIM, SSM_CHUNK
    r_h = SSM_HEADS // g_n
    rp = r_h * p_h
    nc = t // ln

    def body(xs_ref, b_ref, c_ref, dtc_ref, dtr_ref, hpc_ref, hpr_ref, y_ref, prev_ref, st_ref):
        @pl.when(pl.program_id(1) == 0)
        def _():
            st_ref[...] = jnp.zeros_like(st_ref)

        _, dt_c, _, acs_c, acs_r, row, col = _ssd_scalars(dtc_ref, dtr_ref, hpc_ref, hpr_ref, ln)
        bm = b_ref[...]
        cm = c_ref[...]
        cm16 = cm.astype(BF16)
        cb = _nt(cm16, bm.astype(BF16))
        causal = row >= col
        for r in range(r_h):
            sl = slice(r * p_h, (r + 1) * p_h)
            xs = xs_ref[:, sl]
            acs = jnp.broadcast_to(acs_c[:, r:r + 1], (ln, ln))
            last = acs[ln - 1:ln, :]
            lm = jnp.where(causal, jnp.exp(acs - acs_r[r:r + 1, :]), 0.0)
            xd = (xs * jnp.broadcast_to(dt_c[:, r:r + 1], (ln, p_h))).astype(BF16)
            prev = st_ref[sl, :]
            y = _nn((cb * lm).astype(BF16), xd)
            y = y + _nt(cm16, prev.astype(BF16)) * jnp.exp(acs[:, :p_h])
            y_ref[:, sl] = y + hpc_ref[0, 2:3, r:r + 1] * xs
            prev_ref[0, sl, :] = prev
            bd = (bm * jnp.exp(last - acs[:, :n_st])).astype(BF16)
            st_ref[sl, :] = prev * jnp.exp(last[:, :n_st]) + _tn(xd, bd)

    xs, bm, cm, dtcs, dtrs, hpcs, hprs, prev = _ssd_specs(t, di, g_n, n_st, rp, ln, r_h, False)
    (y, prev_out), side_outs = _call(
        body,
        grid=(g_n, nc),
        in_specs=[xs, bm, cm, dtcs, dtrs, hpcs, hprs],
        out_specs=[xs, prev],
        out_shape=[jax.ShapeDtypeStruct((t, di), F32), jax.ShapeDtypeStruct((nc, g_n * rp, n_st), F32)],
        scratch_shapes=[pltpu.VMEM((rp, n_st), F32)],
        sem=("parallel", "arbitrary"),
        name=name,
        args=(xbc, xbc, xbc, dtc, dtr, hpc, hpr),
        side=side,
    )
    return y, prev_out, side_outs


def _ssd_bwd(xbc, dtc, dtr, hpc, hpr, prev, dy, *, side=None, name):
    t = xbc.shape[0]
    di, g_n, n_st, p_h, ln = D_INNER, SSM_GROUPS, SSM_STATE, SSM_HEAD_DIM, SSM_CHUNK
    r_h = SSM_HEADS // g_n
    rp = r_h * p_h
    nc = t // ln

    def body(xs_ref, b_ref, c_ref, dtc_ref, dtr_ref, hpc_ref, hpr_ref, prev_ref, dy_ref,
             dxs_ref, db_ref, dc_ref, ddt_ref, hg_ref, ds_ref):
        step = pl.program_id(1)

        @pl.when(step == 0)
        def _():
            ds_ref[...] = jnp.zeros_like(ds_ref)

        raw_c, dt_c, a_c, acs_c, acs_r, row, col = _ssd_scalars(dtc_ref, dtr_ref, hpc_ref, hpr_ref, ln)
        bm = b_ref[...]
        cm = c_ref[...]
        bm16, cm16 = bm.astype(BF16), cm.astype(BF16)
        cb = _nt(cm16, bm16)
        cbt = _nt(bm16, cm16)
        lane_r = lax.broadcasted_iota(jnp.int32, (ln, r_h), 1)
        dacs_all = jnp.zeros((ln, r_h), F32)
        ddtx_all = jnp.zeros((ln, r_h), F32)
        dd_all = jnp.zeros((ln, r_h), F32)
        dcb = jnp.zeros((ln, ln), F32)
        dcbt = jnp.zeros((ln, ln), F32)
        dc_acc = jnp.zeros((ln, n_st), F32)
        db_acc = jnp.zeros((ln, n_st), F32)
        for r in range(r_h):
            sl = slice(r * p_h, (r + 1) * p_h)
            xs = xs_ref[:, sl]
            dyv = dy_ref[:, sl]
            dy16 = dyv.astype(BF16)
            acs = jnp.broadcast_to(acs_c[:, r:r + 1], (ln, ln))
            dtv = jnp.broadcast_to(dt_c[:, r:r + 1], (ln, p_h))
            acsr = acs_r[r:r + 1, :]
            last = acs[ln - 1:ln, :]
            xd = xs * dtv
            xd16 = xd.astype(BF16)
            lm = jnp.where(row >= col, jnp.exp(acs - acsr), 0.0)
            lmt = jnp.where(col >= row, jnp.exp(acsr - acs), 0.0)
            m_ls = cb * lm
            m_sl = cbt * lmt
            dm = _nt(dy16, xd16)
            dmt = _nt(xd16, dy16)
            dxd = _nn(m_sl.astype(BF16), dy16)
            dacs = _row_sums(dm * m_ls - dmt * m_sl)
            dcb = dcb + dm * lm
            dcbt = dcbt + dmt * lmt
            prev = prev_ref[0, sl, :]
            prev16 = prev.astype(BF16)
            e = jnp.exp(acs[:, :p_h])
            y_off = _nt(cm16, prev16) * e
            dacs = dacs + _row_sums(dyv * y_off)
            dyo16 = (dyv * e).astype(BF16)
            dc_acc = dc_acc + _nn(dyo16, prev16)
            dprev = _tn(dyo16, cm16)
            ds = ds_ref[sl, :]
            ds16 = ds.astype(BF16)
            decay = jnp.exp(last - acs)[:, :n_st]
            bd16 = (bm * decay).astype(BF16)
            dbd = _nn(xd16, ds16)
            dxd = dxd + _nt(bd16, ds16)
            db_acc = db_acc + dbd * decay
            tdec = _row_sums(dbd * bm) * decay
            cd = jnp.exp(last)
            dlast = (jnp.sum(tdec, axis=0, keepdims=True)
                     + jnp.sum(_row_sums(prev * ds), axis=0, keepdims=True) * cd)
            ds_ref[sl, :] = dprev + cd[:, :n_st] * ds
            dskip = hpc_ref[0, 2:3, r:r + 1]
            dxs_ref[:, sl] = dxd * dtv + dskip * dyv
            dacs = dacs - tdec + jnp.where(row == ln - 1, dlast, 0.0)
            dacs_all = jnp.where(lane_r == r, dacs[:, :r_h], dacs_all)
            ddtx_all = jnp.where(lane_r == r, _row_sums(dxd * xs)[:, :r_h], ddtx_all)
            dd_all = jnp.where(lane_r == r, _row_sums(dyv * xs)[:, :r_h], dd_all)
        dc_ref[...] = dc_acc + _nn(dcb.astype(BF16), bm16)
        db_ref[...] = db_acc + _nn(dcbt.astype(BF16), cm16)
        upper = (row <= col).astype(F32)
        dad = _ones_dot(upper, dacs_all, ones_left=True)
        ddt = dad * a_c + ddtx_all
        ddt_raw = ddt * _sigmoid(raw_c)
        ddt_ref[0] = ddt_raw
        d_bias = jnp.sum(ddt_raw, axis=0, keepdims=True)
        d_alog = jnp.sum(dad * dt_c, axis=0, keepdims=True) * a_c
        d_d = jnp.sum(dd_all, axis=0, keepdims=True)
        hg = jnp.concatenate([d_bias, d_alog, d_d], axis=0)

        @pl.when(step == 0)
        def _():
            hg_ref[0] = hg

        @pl.when(step > 0)
        def _():
            hg_ref[0] += hg

    xs, bms, cms, dtcs, dtrs, hpcs, hprs, prevs = _ssd_specs(t, di, g_n, n_st, rp, ln, r_h, True)
    bout = pl.BlockSpec((ln, n_st), lambda g, c: (nc - 1 - c, g))
    outs, side_outs = _call(
        body,
        grid=(g_n, nc),
        in_specs=[xs, bms, cms, dtcs, dtrs, hpcs, hprs, prevs, xs],
        out_specs=[xs, bout, bout, dtcs, hpcs],
        out_shape=[jax.ShapeDtypeStruct((t, di), F32), jax.ShapeDtypeStruct((t, g_n * n_st), F32),
                   jax.ShapeDtypeStruct((t, g_n * n_st), F32), jax.ShapeDtypeStruct((g_n, t, r_h), F32),
                   jax.ShapeDtypeStruct((g_n, 3, r_h), F32)],
        scratch_shapes=[pltpu.VMEM((rp, n_st), F32)],
        sem=("parallel", "arbitrary"),
        name=name,
        args=(xbc, xbc, xbc, dtc, dtr, hpc, hpr, prev, dy),
        side=side,
    )
    return (*outs, side_outs)


SB_KEYS = 256
SB_QUERIES = (512, 256)
SB_CUTOFF = 110.0
SB_PIECES = 2


def _sb_logits(qs, kv, valid):
    z = _nt(qs, kv)
    nz = -z
    lg = jnp.minimum(nz, 0.0) - jnp.log(1.0 + jnp.exp(jnp.minimum(z, nz)))
    return z + lg, (lg if valid is None else jnp.where(valid, lg, 0.0))


def _sb_iota(tq):
    diff = lax.broadcasted_iota(jnp.int32, (tq, SB_KEYS), 1) - lax.broadcasted_iota(jnp.int32, (tq, SB_KEYS), 0)
    krow = lax.broadcasted_iota(jnp.int32, (SB_KEYS, SB_KEYS), 0)
    kcol = lax.broadcasted_iota(jnp.int32, (SB_KEYS, SB_KEYS), 1)
    return diff, krow, kcol


def _sb_scale(d):
    scale = 1.0 / math.sqrt(d)
    assert math.frexp(scale)[0] == 0.5, "the scale is folded into bf16 queries: it must be a power of two"
    return scale


def _key_rows(j):
    return pl.ds(pl.multiple_of(j * SB_KEYS, SB_KEYS), SB_KEYS)


def _sb_fwd(q, k, v, n_heads, *, side=None, name):
    t, hd = q.shape
    d = hd // n_heads
    hpt = LANES // d
    assert hpt * d == LANES and n_heads % hpt == 0
    tq = _tile(t, SB_QUERIES)
    nq = t // tq
    kpq = tq // SB_KEYS
    scale = _sb_scale(d)

    def body(q_ref, k_ref, v_ref, o_ref, lt_ref, first_ref):
        i = pl.program_id(1)
        diff, krow, kcol = _sb_iota(tq)
        later = (krow > kcol).astype(F32)
        nb = i * kpq
        for hh in range(hpt):
            sl = slice(hh * d, (hh + 1) * d)
            qs = (q_ref[:, sl].astype(F32) * scale).astype(BF16)

            def block(j, carry, valid, qs=qs, sl=sl):
                acc, cl = carry
                rows = _key_rows(j)
                ls, lg = _sb_logits(qs, k_ref[rows, sl], valid)
                cs = _ones_dot(later, lg, ones_left=False, pieces=SB_PIECES)
                att = jnp.exp(ls + (cs + cl))
                if valid is not None:
                    att = jnp.where(valid, att, 0.0)
                acc = acc + _nn(att.astype(BF16), v_ref[rows, sl])
                return acc, cl + (cs[:, 0:1] + lg[:, 0:1])

            carry = (jnp.zeros((tq, d), F32), jnp.zeros((tq, 1), F32))
            for m in range(kpq - 1, -1, -1):
                carry = block(i * kpq + m, carry, diff < -m * SB_KEYS)

            def more(st):
                s, _, cl = st
                return jnp.logical_and(s < nb, jnp.max(cl) > -SB_CUTOFF)

            def step(st, block=block):
                s, acc, cl = st
                acc, cl = block(nb - 1 - s, (acc, cl), None)
                return s + 1, acc, cl

            walked, acc, cl = lax.while_loop(more, step, (jnp.int32(0),) + carry)
            o_ref[:, sl] = acc.astype(o_ref.dtype)
            lt_ref[hh] = cl
            first_ref[pl.program_id(0) * hpt + hh, i] = nb - walked

    qs = pl.BlockSpec((tq, LANES), lambda p, i: (i, p))
    ls = pl.BlockSpec((hpt, tq, 1), lambda p, i: (p, i, 0))
    ks = pl.BlockSpec((t, LANES), lambda p, i: (0, p))
    outs, side_outs = _call(
        body,
        grid=(n_heads // hpt, nq),
        in_specs=[qs, ks, ks],
        out_specs=[qs, ls, pl.BlockSpec(memory_space=pltpu.SMEM)],
        out_shape=[jax.ShapeDtypeStruct((t, hd), BF16), jax.ShapeDtypeStruct((n_heads, t, 1), F32),
                   jax.ShapeDtypeStruct((n_heads, nq), jnp.int32)],
        sem=("arbitrary", "arbitrary"),
        name=name,
        args=(q, k, v),
        side=side,
    )
    return (*outs, side_outs)


def _sb_bwd(q, k, v, lt, first, do, n_heads, *, name):
    t, hd = q.shape
    d = hd // n_heads
    hpt = LANES // d
    tq = _tile(t, SB_QUERIES)
    nq = t // tq
    kpq = tq // SB_KEYS
    scale = _sb_scale(d)
    last = SB_KEYS - 1

    def body(q_ref, k_ref, v_ref, lt_ref, first_ref, do_ref, dq_ref, dk_ref, dv_ref, dk_acc, dv_acc):
        i = pl.program_id(1)

        @pl.when(i == 0)
        def _():
            dk_acc[...] = jnp.zeros_like(dk_acc)
            dv_acc[...] = jnp.zeros_like(dv_acc)

        diff, krow, kcol = _sb_iota(tq)
        upto = (krow <= kcol).astype(F32)
        before = (krow < kcol).astype(F32)
        zero = jnp.zeros((tq, 1), F32)
        nb = i * kpq
        for hh in range(hpt):
            sl = slice(hh * d, (hh + 1) * d)
            qs = (q_ref[:, sl].astype(F32) * scale).astype(BF16)
            do16 = do_ref[:, sl].astype(BF16)
            ltot = lt_ref[hh]

            def block(j, carry, valid, r0=0, qs=qs, do16=do16, ltot=ltot, sl=sl):
                dq, pl_sum, pg_sum = carry
                rows = _key_rows(j)
                kv = k_ref[rows, sl]
                vv = v_ref[rows, sl]
                ls, lg = _sb_logits(qs[r0:], kv, valid)
                pre = _ones_dot(upto, lg, ones_left=False, pieces=SB_PIECES)
                att = jnp.exp(ls + (ltot[r0:] - (pre + pl_sum)))
                if valid is not None:
                    att = jnp.where(valid, att, 0.0)
                g = att * _nt(do16[r0:], vv)
                gpre = _ones_dot(before, g, ones_left=False, pieces=SB_PIECES)
                sig = jnp.exp(ls)
                dz16 = (g - sig * (g + (gpre + pg_sum))).astype(BF16)
                if valid is not None:
                    dz16 = jnp.where(valid, dz16, jnp.zeros_like(dz16))
                dq = dq + _nn(dz16, kv)
                dk_acc[rows, sl] += _tn(dz16, qs[r0:])
                dv_acc[rows, sl] += _tn(att.astype(BF16), do16[r0:])
                return dq, pl_sum + pre[:, last:], pg_sum + (gpre[:, last:] + g[:, last:])

            start = jnp.clip(first_ref[pl.program_id(0) * hpt + hh, i], 0, nb)
            carry = lax.fori_loop(start, nb, lambda j, cr, block=block: block(j, cr, None),
                                  (jnp.zeros((tq, d), F32), zero, zero))
            for m in range(kpq):
                r0 = m * SB_KEYS
                sub = block(nb + m, tuple(a[r0:] for a in carry), diff[r0:] < -r0, r0)
                carry = tuple(jnp.concatenate([a[:r0], s], axis=0) if r0 else s for a, s in zip(carry, sub))
            dq_ref[:, sl] = (carry[0] * scale).astype(dq_ref.dtype)

        @pl.when(i == nq - 1)
        def _():
            dk_ref[...] = dk_acc[...].astype(dk_ref.dtype)
            dv_ref[...] = dv_acc[...].astype(dv_ref.dtype)

    qs = pl.BlockSpec((tq, LANES), lambda p, i: (i, p))
    ls = pl.BlockSpec((hpt, tq, 1), lambda p, i: (p, i, 0))
    ks = pl.BlockSpec((t, LANES), lambda p, i: (0, p))
    full = jax.ShapeDtypeStruct((t, hd), BF16)
    return pl.pallas_call(
        body,
        grid=(n_heads // hpt, nq),
        in_specs=[qs, ks, ks, ls, pl.BlockSpec(memory_space=pltpu.SMEM), qs],
        out_specs=[qs, ks, ks],
        out_shape=[full, full, full],
        scratch_shapes=[pltpu.VMEM((t, LANES), F32), pltpu.VMEM((t, LANES), F32)],
        compiler_params=_cparams(("arbitrary", "arbitrary")),
        name=name,
    )(q, k, v, lt, first, do)


def _row_tile(rows, cols):
    fits = [r for r in range(16, rows + 1, 16) if rows % r == 0 and r * cols * 4 <= ADAM_BLOCK_BYTES]
    return max(fits) if fits else rows


def _sum_leading(x, *, name):
    n, rows, cols = x.shape
    tr = _row_tile(rows, cols)

    def body(x_ref, o_ref):
        acc = x_ref[0].astype(F32)
        for q in range(1, n):
            acc = acc + x_ref[q].astype(F32)
        o_ref[...] = acc

    return pl.pallas_call(
        body,
        grid=(rows // tr,),
        in_specs=[pl.BlockSpec((n, tr, cols), lambda i: (0, i, 0))],
        out_specs=pl.BlockSpec((tr, cols), lambda i: (i, 0)),
        out_shape=jax.ShapeDtypeStruct((rows, cols), F32),
        compiler_params=_cparams(("parallel",)),
        name=name,
    )(x)


def _pair_add(g4h, recv, c, *, out_dtype, name):
    n, _, rows, cols = g4h.shape
    tr = _row_tile(rows, cols)

    def body(c_ref, g_ref, r_ref, o_ref):
        o_ref[...] = (g_ref[...] + r_ref[...]).astype(o_ref.dtype)

    blk = pl.BlockSpec((1, tr, cols), lambda q, i, c_ref: (q, i, 0))
    return pl.pallas_call(
        body,
        grid_spec=pltpu.PrefetchScalarGridSpec(
            num_scalar_prefetch=1,
            grid=(n, rows // tr),
            in_specs=[pl.BlockSpec((1, None, tr, cols), lambda q, i, c_ref: (q, c_ref[0], i, 0)), blk],
            out_specs=blk),
        out_shape=jax.ShapeDtypeStruct((n, rows, cols), out_dtype),
        compiler_params=_cparams(("parallel", "parallel")),
        name=name,
    )(c.reshape(1).astype(jnp.int32), g4h, recv)


ANY = pl.BlockSpec(memory_space=pl.ANY)


def _other_chips(x, y):
    return [(1 - x, y), (x, 1 - y), (1 - x, 1 - y)]


def _gather_chips(shard, *, name):
    def body(x_ref, o_ref, send_sems, recv_sems, local_sem):
        x, y, c = lax.axis_index("x"), lax.axis_index("y"), lax.axis_index("c")
        me = 2 * x + y
        mine = pltpu.make_async_copy(x_ref, o_ref.at[me], local_sem)
        mine.start()
        chips = _other_chips(x, y)
        sends = [pltpu.make_async_remote_copy(src_ref=x_ref, dst_ref=o_ref.at[me], send_sem=send_sems.at[q],
                                              recv_sem=recv_sems.at[q], device_id=(px, py, c), device_id_type=MESH)
                 for q, (px, py) in enumerate(chips)]
        for cp in sends:
            cp.start()
        for q, (px, py) in enumerate(chips):
            pltpu.make_async_remote_copy(src_ref=x_ref, dst_ref=o_ref.at[2 * px + py], send_sem=send_sems.at[q],
                                         recv_sem=recv_sems.at[q], device_id=(px, py, c), device_id_type=MESH).wait_recv()
        for cp in sends:
            cp.wait_send()
        mine.wait()

    return pl.pallas_call(
        body,
        in_specs=[ANY],
        out_specs=ANY,
        out_shape=jax.ShapeDtypeStruct((4,) + shard.shape, shard.dtype),
        scratch_shapes=[pltpu.SemaphoreType.DMA((3,)), pltpu.SemaphoreType.DMA((3,)), pltpu.SemaphoreType.DMA],
        compiler_params=pltpu.CompilerParams(has_side_effects=True),
        name=name,
    )(shard)


def _comm_call(body, ins, out_shapes, n_sems, name):
    n = len(ins)

    def wrapped(*refs):
        body(refs[:n], refs[n:n + len(out_shapes)], refs[-2], refs[-1])

    return pl.pallas_call(
        wrapped,
        in_specs=[ANY] * n,
        out_specs=[ANY] * len(out_shapes),
        out_shape=out_shapes,
        scratch_shapes=[pltpu.SemaphoreType.DMA((n_sems,)), pltpu.SemaphoreType.DMA((n_sems,))],
        compiler_params=pltpu.CompilerParams(has_side_effects=True),
        name=name,
    )(*ins)


def _remote(send_sems, recv_sems, q, src, dst, to):
    return pltpu.make_async_remote_copy(src_ref=src, dst_ref=dst, send_sem=send_sems.at[q], recv_sem=recv_sems.at[q],
                                        device_id=to, device_id_type=MESH)


def _scatter_job(parts):
    def sends(ins, outs, send_sems, recv_sems):
        x, y, c = lax.axis_index("x"), lax.axis_index("y"), lax.axis_index("c")
        return [_remote(send_sems, recv_sems, 3 * i + q, p.at[2 * px + py], o.at[2 * x + y], (px, py, c))
                for i, (p, o) in enumerate(zip(ins, outs)) for q, (px, py) in enumerate(_other_chips(x, y))]

    def start(ins, outs, send_sems, recv_sems):
        for cp in sends(ins, outs, send_sems, recv_sems):
            cp.start()

    def finish(ins, outs, send_sems, recv_sems):
        x, y, c = lax.axis_index("x"), lax.axis_index("y"), lax.axis_index("c")
        for i, (p, o) in enumerate(zip(ins, outs)):
            for q, (px, py) in enumerate(_other_chips(x, y)):
                _remote(send_sems, recv_sems, 3 * i + q, p.at[2 * x + y], o.at[2 * px + py], (px, py, c)).wait_recv()
        for cp in sends(ins, outs, send_sems, recv_sems):
            cp.wait_send()

    return _SideJob(parts, [jax.ShapeDtypeStruct(p.shape, p.dtype) for p in parts], 3 * len(parts), start, finish)


def _run_job(job, name):
    return _comm_call(lambda *refs: (job.start(*refs), job.finish(*refs)), job.ins, job.out_shapes, job.n_sems, name)


def _gather_job(shards):
    def sends(ins, outs, send_sems, recv_sems):
        x, y, c = lax.axis_index("x"), lax.axis_index("y"), lax.axis_index("c")
        return [_remote(send_sems, recv_sems, 6 * i + q, s.at[c], o.at[2 * x + y, c], (px, py, c))
                for i, (s, o) in enumerate(zip(ins, outs)) for q, (px, py) in enumerate(_other_chips(x, y))]

    def start(ins, outs, send_sems, recv_sems):
        for cp in sends(ins, outs, send_sems, recv_sems):
            cp.start()

    def finish(ins, outs, send_sems, recv_sems):
        x, y, c = lax.axis_index("x"), lax.axis_index("y"), lax.axis_index("c")
        sibling = (x, y, 1 - c)
        chips = _other_chips(x, y)
        copy = lambda q, src, dst, to: _remote(send_sems, recv_sems, q, src, dst, to)
        passed = []
        for i, (s, o) in enumerate(zip(ins, outs)):
            for q, (px, py) in enumerate(chips):
                slot = o.at[2 * px + py, c]
                copy(6 * i + q, s.at[c], slot, (px, py, c)).wait_recv()
                passed.append(copy(6 * i + 3 + q, slot, slot, sibling))
                passed[-1].start()
        for i, (s, o) in enumerate(zip(ins, outs)):
            for q, (px, py) in enumerate(chips):
                copy(6 * i + 3 + q, s.at[1 - c], o.at[2 * px + py, 1 - c], sibling).wait_recv()
        for cp in sends(ins, outs, send_sems, recv_sems) + passed:
            cp.wait_send()

    return _SideJob(shards, [jax.ShapeDtypeStruct((N_CHIPS,) + s.shape, s.dtype) for s in shards], 6 * len(shards),
                    start, finish)


def _swap_job(gs):
    def copies(ins, outs, send_sems, recv_sems):
        x, y, c = lax.axis_index("x"), lax.axis_index("y"), lax.axis_index("c")
        return [_remote(send_sems, recv_sems, i, g.at[pl.ds(0, g.shape[0]), 1 - c], o, (x, y, 1 - c))
                for i, (g, o) in enumerate(zip(ins, outs))]

    def start(*refs):
        for cp in copies(*refs):
            cp.start()

    def finish(*refs):
        for cp in copies(*refs):
            cp.wait()

    return _SideJob(gs, [jax.ShapeDtypeStruct((g.shape[0],) + g.shape[2:], g.dtype) for g in gs], len(gs), start, finish)


def _join_halves(halves, *, name):
    def body(ins, outs, send_sems, recv_sems):
        x, y, c = lax.axis_index("x"), lax.axis_index("y"), lax.axis_index("c")
        sibling = (x, y, 1 - c)
        sends = [_remote(send_sems, recv_sems, i, h, o.at[c], sibling) for i, (h, o) in enumerate(zip(ins, outs))]
        for cp in sends:
            cp.start()
        for i, (h, o) in enumerate(zip(ins, outs)):
            _remote(send_sems, recv_sems, i, h, o.at[1 - c], sibling).wait_recv()
        for cp in sends:
            cp.wait_send()

    return _comm_call(body, halves, [jax.ShapeDtypeStruct((2,) + h.shape, h.dtype) for h in halves], len(halves), name)


def _gather_all(v, *, name):
    def body(v_ref, o_ref, send_sems, recv_sems, local_sem):
        x, y, c = lax.axis_index("x"), lax.axis_index("y"), lax.axis_index("c")
        me = 4 * x + 2 * y + c
        mine = pltpu.make_async_copy(v_ref, o_ref.at[me], local_sem)
        mine.start()
        peers = [(x ^ (q >> 2 & 1), y ^ (q >> 1 & 1), c ^ (q & 1)) for q in range(1, 8)]
        sends = [pltpu.make_async_remote_copy(src_ref=v_ref, dst_ref=o_ref.at[me], send_sem=send_sems.at[q],
                                              recv_sem=recv_sems.at[q], device_id=peer, device_id_type=MESH)
                 for q, peer in enumerate(peers)]
        for cp in sends:
            cp.start()
        for q, (px, py, pc) in enumerate(peers):
            pltpu.make_async_remote_copy(src_ref=v_ref, dst_ref=o_ref.at[4 * px + 2 * py + pc], send_sem=send_sems.at[q],
                                         recv_sem=recv_sems.at[q], device_id=(px, py, pc), device_id_type=MESH).wait_recv()
        for cp in sends:
            cp.wait_send()
        mine.wait()

    return pl.pallas_call(
        body,
        in_specs=[ANY],
        out_specs=ANY,
        out_shape=jax.ShapeDtypeStruct((8,) + v.shape, v.dtype),
        scratch_shapes=[pltpu.SemaphoreType.DMA((7,)), pltpu.SemaphoreType.DMA((7,)), pltpu.SemaphoreType.DMA],
        compiler_params=pltpu.CompilerParams(has_side_effects=True),
        name=name,
    )(v)


WEIGHTS = ['ssm_norm_w', 'ssm_in_w', 'ssm_conv_w', 'ssm_conv_b', 'ssm_dt_bias', 'ssm_a_log', 'ssm_d',
           'ssm_gate_norm_w', 'ssm_out_w', 'kv_norm_w', 'w_k', 'w_v', 'attn_norm_w', 'w_q', 'w_o',
           'ffn_norm_w', 'ffn_up_w', 'ffn_conv_w', 'ffn_conv_b', 'ffn_down_w', 'final_norm_w']
SHARD_AXIS = {'ssm_norm_w': 1, 'ssm_in_w': 2, 'ssm_conv_w': 2, 'ssm_conv_b': 1, 'ssm_gate_norm_w': 1,
              'ssm_out_w': 1, 'w_k': 0, 'w_v': 0, 'w_q': 1, 'w_o': 1, 'ffn_up_w': 2, 'ffn_conv_w': 2,
              'ffn_down_w': 1}
BIG = ['ssm_in_w', 'ssm_out_w', 'w_k', 'w_v', 'w_q', 'w_o', 'ffn_up_w', 'ffn_down_w']
SMALL = [n for n in WEIGHTS if n in SHARD_AXIS and n not in BIG]
REPLICATED = [n for n in WEIGHTS if n not in SHARD_AXIS]
STACKED = ['ffn_up_w', 'ffn_down_w']
N_CHIPS = 4


PACK_ROWS = 16


def _piece_rows(n):
    return -(-n // (PACK_ROWS * LANES)) * PACK_ROWS


def _pack(arrs, dtype, row_mult):
    lead = arrs[0].shape[:-1]
    pieces, total = [], 0
    for a in arrs:
        n = a.shape[-1]
        rows = _piece_rows(n)
        a = a.astype(dtype)
        if rows * LANES != n:
            a = jnp.pad(a, [(0, 0)] * len(lead) + [(0, rows * LANES - n)])
        pieces.append(a.reshape(lead + (rows, LANES)))
        total += rows
    extra = -total % row_mult
    if extra:
        pieces.append(jnp.zeros(lead + (extra, LANES), dtype))
    return jnp.concatenate(pieces, axis=len(lead))


def _unpack(buf, shapes):
    lead = buf.shape[:-2]
    out, off = [], 0
    for shp in shapes:
        n = math.prod(shp)
        rows = _piece_rows(n)
        piece = lax.slice_in_dim(buf, off, off + rows, axis=len(lead)).reshape(lead + (rows * LANES,))
        out.append(piece[..., :n].reshape(lead + tuple(shp)))
        off += rows
    return out


def _set_slot(buf, piece, index):
    return lax.dynamic_update_slice_in_dim(buf, piece[None], index, axis=0)


def _from_shards(stacked, axis):
    return jnp.concatenate([stacked[j] for j in range(N_CHIPS)], axis=axis)


def _ffn_fwd(h, norm_w, w_up, conv_w, conv_b, w_down, tag, side=None):
    u = _rmsnorm_fwd(h, norm_w, name=f"ffn{tag}_norm")
    hid = _matmul(u, w_up, name=f"ffn{tag}_up")
    act, side_outs = _conv_glu_fwd(hid, conv_w, conv_b, side=side, name=f"ffn{tag}_glu")
    out = _matmul(act, w_down, add=h, name=f"ffn{tag}_down")
    return out, (u, hid, act), side_outs


def _ffn_bwd(h, saved, dout, norm_w, w_up, conv_w, conv_b, w_down, tag, side=None):
    u, hid, act = saved
    dact = _matmul(dout, w_down, tb=True, name=f"ffn{tag}_down_dx")
    dw_down = _matmul(act, dout, ta=True, name=f"ffn{tag}_down_dw")
    dhid, dwg, dwv, dbg, dbv, side_outs = _conv_glu_bwd(hid, conv_w, conv_b, dact, side=side, name=f"ffn{tag}_glu_bwd")
    du = _matmul(dhid, w_up, tb=True, name=f"ffn{tag}_up_dx")
    dw_up = _matmul(u, dhid, ta=True, out_parts=N_CHIPS, name=f"ffn{tag}_up_dw")
    dh, (dnorm,) = _rmsnorm_bwd(h, [(du, norm_w)], dout, name=f"ffn{tag}_norm_bwd")
    return dh, dict(norm=dnorm[0], up=dw_up, conv_w=jnp.concatenate([dwg, dwv], axis=1),
                    conv_b=jnp.concatenate([dbg, dbv], axis=1)[0], down=dw_down), side_outs


class _Pieces:
    def __init__(self, local):
        self.c = lax.axis_index("c")
        self.chip = 2 * lax.axis_index("x") + lax.axis_index("y")
        self.shape, self.s16 = {}, {}
        for n in BIG:
            blk = local[n]
            layers = [(n, l, blk[l]) for l in range(blk.shape[0])] if n in STACKED else [(n, None, blk.reshape(blk.shape[-2:]))]
            for name, l, p in layers:
                self.shape[name, l] = p.shape
                self.s16[name, l] = p.astype(BF16).reshape(2, p.shape[0] // 2, p.shape[1])

    def gather_job(self, keys):
        return _gather_job([self.s16[k] for k in keys])

    def weights(self, keys, gathered):
        out = []
        for k, g in zip(keys, gathered):
            r, cc = self.shape[k]
            by_chip = _set_slot(g, self.s16[k], self.chip).reshape(N_CHIPS, r, cc)
            if k[0] == 'ssm_in_w':
                by_chip = by_chip.transpose(1, 0, 2).reshape(r, N_CHIPS * cc)
            elif k[0] != 'ffn_up_w':
                by_chip = by_chip.reshape(N_CHIPS * r, cc)
            out.append(by_chip)
        return out

    def by_halves(self, keys, grads):
        gs = []
        for k, g in zip(keys, grads):
            r, cc = self.shape[k]
            if k[0] == 'ssm_in_w':
                g = g.reshape(r, N_CHIPS, cc).transpose(1, 0, 2)
            gs.append(g.reshape(N_CHIPS, 2, r // 2, cc))
        return gs

    def pair_sums(self, gs, recv, tag):
        return [_pair_add(g, rv, self.c, out_dtype=BF16, name=f"rs_pair_add_{tag}{i}") for i, (g, rv) in enumerate(zip(gs, recv))]

    def chip_sums(self, pairs, scattered, tag):
        return [_sum_leading(_set_slot(s, lax.dynamic_index_in_dim(p, self.chip, axis=0, keepdims=False), self.chip),
                             name=f"rs_chip_sum_{tag}{i}") for i, (s, p) in enumerate(zip(scattered, pairs))]

    def shards(self, keys, halves):
        joined = _join_halves(halves, name="rs_half_join")
        return {k: _set_slot(j, h, self.c).reshape(self.shape[k]) for k, h, j in zip(keys, halves, joined)}


def _step(x, target, w, pieces):
    t = x.shape[0]
    g_n, heads = SSM_GROUPS, SSM_HEADS
    r_h = heads // g_n
    di = D_INNER
    zx_cols = di + CONV_DIM
    k_in = [('ssm_in_w', None)]
    k_ffn0 = [('ssm_out_w', None), ('ffn_up_w', 0), ('ffn_down_w', 0)]
    k_qkv = [('w_k', None), ('w_v', None), ('w_q', None)]
    k_late = [('w_o', None), ('ffn_up_w', 1), ('ffn_down_w', 1)]
    (w_in,) = pieces.weights(k_in, _run_job(pieces.gather_job(k_in), "gather_ssm_in"))
    w_zx = w_in[:, :zx_cols]
    w_dt = jnp.pad(w_in[:, zx_cols:], ((0, 0), (0, LANES - heads)))
    conv_w, conv_b = w['ssm_conv_w'][0], w['ssm_conv_b'][0]
    hp = jnp.stack([w['ssm_dt_bias'][0], w['ssm_a_log'][0], w['ssm_d'][0]], axis=0).reshape(3, g_n, r_h)
    hpc, hpr = hp.transpose(1, 0, 2), hp.transpose(1, 2, 0)

    h0 = x
    u0 = _rmsnorm_fwd(h0, w['ssm_norm_w'][0], name="ssm_norm")
    zx = _matmul(u0, w_zx, name="ssm_in_zx")
    dt_raw = _matmul(u0, w_dt, name="ssm_in_dt")[:, :heads]
    dtg = dt_raw.reshape(t, g_n, r_h)
    dtc, dtr = dtg.transpose(1, 0, 2), dtg.transpose(1, 2, 0)
    xbc = _conv_silu_fwd(zx, conv_w, conv_b, x_off=di, name="ssm_conv")
    y, prev, got = _ssd_fwd(xbc, dtc, dtr, hpc, hpr, side=pieces.gather_job(k_ffn0), name="ssd_fwd")
    w_out, w_up0, w_down0 = pieces.weights(k_ffn0, got)
    yn = _gate_norm_fwd(y, zx, w['ssm_gate_norm_w'][0], name="ssm_gate_norm")
    h1 = _matmul(yn, w_out, add=h0, name="ssm_out")
    h2, ffn0, got = _ffn_fwd(h1, w['ffn_norm_w'][0], w_up0, w['ffn_conv_w'][0], w['ffn_conv_b'][0], w_down0, 0,
                             side=pieces.gather_job(k_qkv))
    w_k, w_v, w_q = pieces.weights(k_qkv, got)
    hk = _rmsnorm_fwd(h2, w['kv_norm_w'], name="kv_norm")
    qn = _rmsnorm_fwd(h2, w['attn_norm_w'][0], name="attn_norm")
    k2 = _matmul(hk, w_k, out_dtype=BF16, name="attn_k")
    v2 = _matmul(hk, w_v, out_dtype=BF16, name="attn_v")
    q2 = _matmul(qn, w_q, out_dtype=BF16, name="attn_q")
    o2, lt, first, got = _sb_fwd(q2, k2, v2, SB_HEADS, side=pieces.gather_job(k_late), name="sb_fwd")
    w_o, w_up1, w_down1 = pieces.weights(k_late, got)
    h3 =_matmul(o2, w_o, add=h2, name="attn_o")
    h4, ffn1, _ = _ffn_fwd(h3, w['ffn_norm_w'][1], w_up1, w['ffn_conv_w'][1], w['ffn_conv_b'][1], w_down1, 1)
    loss_p, dh4, d_final = _loss_head(h4, w['final_norm_w'], target, name="loss_head")

    dh3, g1, _ = _ffn_bwd(h3, ffn1, dh4, w['ffn_norm_w'][1], w_up1, w['ffn_conv_w'][1], w['ffn_conv_b'][1], w_down1, 1)
    do2 = _matmul(dh3, w_o, tb=True, out_dtype=BF16, name="attn_o_dx")
    dw_o = _matmul(o2, dh3, ta=True, name="attn_o_dw")
    dq2, dk2, dv2 = _sb_bwd(q2, k2, v2, lt, first, do2, SB_HEADS, name="sb_bwd")
    dqn = _matmul(dq2, w_q, tb=True, name="attn_q_dx")
    dw_q = _matmul(qn, dq2, ta=True, name="attn_q_dw")
    dhk = _matmul(dk2, w_k, tb=True, name="attn_k_dx")
    dhk = _matmul(dv2, w_v, tb=True, add=dhk, name="attn_v_dx")
    dw_k = _matmul(hk, dk2, ta=True, name="attn_k_dw")
    dw_v = _matmul(hk, dv2, ta=True, name="attn_v_dw")
    dh2, (d_attn_norm, d_kv_norm) = _rmsnorm_bwd(h2, [(dqn, w['attn_norm_w'][0]), (dhk, w['kv_norm_w'])], dh3,
                                                 name="attn_norms_bwd")
    gs_late = pieces.by_halves(k_qkv + k_late, [dw_k, dw_v, dw_q, dw_o, g1['up'], g1['down']])
    dh1, g0, recv = _ffn_bwd(h1, ffn0, dh2, w['ffn_norm_w'][0], w_up0, w['ffn_conv_w'][0], w['ffn_conv_b'][0], w_down0, 0,
                             side=_swap_job(gs_late))
    pairs_late = pieces.pair_sums(gs_late, recv, "a")
    dyn = _matmul(dh1, w_out, tb=True, name="ssm_out_dx")
    dw_out = _matmul(yn, dh1, ta=True, name="ssm_out_dw")
    k_done = k_qkv + k_late + k_ffn0
    gs_ffn0 = pieces.by_halves(k_ffn0, [dw_out, g0['up'], g0['down']])
    dy, dz, d_gate, recv = _gate_norm_bwd(y, zx, w['ssm_gate_norm_w'][0], dyn, side=_swap_job(gs_ffn0),
                                          name="ssm_gate_norm_bwd")
    pairs_done = pairs_late + pieces.pair_sums(gs_ffn0, recv, "c")
    dxs, dbm, dcm, ddt_g, hg, scattered_done = _ssd_bwd(xbc, dtc, dtr, hpc, hpr, prev, dy,
                                                        side=_scatter_job(pairs_done), name="ssd_bwd")
    dzx, d_conv_w, d_conv_b = _conv_silu_bwd(zx, conv_w, conv_b, [dxs, dbm, dcm], x_off=di, into=dz, name="ssm_conv_bwd")
    ddt = jnp.pad(ddt_g.transpose(1, 0, 2).reshape(t, heads), ((0, 0), (0, LANES - heads)))
    du0 = _matmul(dzx, w_zx, tb=True, name="ssm_in_zx_dx")
    du0 = _matmul(ddt, w_dt, tb=True, add=du0, name="ssm_in_dt_dx")
    dw_in = jnp.concatenate([_matmul(u0, dzx, ta=True, name="ssm_in_zx_dw"),
                             _matmul(u0, ddt, ta=True, name="ssm_in_dt_dw")[:, :heads]], axis=1)
    dx, (d_ssm_norm,) = _rmsnorm_bwd(h0, [(du0, w['ssm_norm_w'][0])], dh1, name="ssm_norm_bwd")

    gs_in = pieces.by_halves(k_in, [dw_in])
    pairs_in = pieces.pair_sums(gs_in, _run_job(_swap_job(gs_in), "rs_pair_swap_b"), "b")
    scattered_in = _run_job(_scatter_job(pairs_in), "rs_chip_scatter_b")
    halves = pieces.chip_sums(pairs_done, scattered_done, "a") + pieces.chip_sums(pairs_in, scattered_in, "b")
    big_grads = pieces.shards(k_done + k_in, halves)

    hgr = hg.transpose(1, 0, 2).reshape(3, heads)
    grads = {
        'ssm_norm_w': d_ssm_norm, 'ssm_conv_w': d_conv_w[None], 'ssm_conv_b': d_conv_b,
        'ssm_dt_bias': hgr[0:1], 'ssm_a_log': hgr[1:2], 'ssm_d': hgr[2:3], 'ssm_gate_norm_w': d_gate,
        'kv_norm_w': d_kv_norm[0], 'attn_norm_w': d_attn_norm, 'ffn_norm_w': jnp.stack([g0['norm'], g1['norm']]),
        'ffn_conv_w': jnp.stack([g0['conv_w'], g1['conv_w']]), 'ffn_conv_b': jnp.stack([g0['conv_b'], g1['conv_b']]),
        'final_norm_w': d_final[0],
    }
    return loss_p, dx, grads, big_grads


def kernel(x, ssm_norm_w, ssm_in_w, ssm_conv_w, ssm_conv_b, ssm_dt_bias, ssm_a_log, ssm_d, ssm_gate_norm_w, ssm_out_w, kv_norm_w, w_k, w_v, attn_norm_w, w_q, w_o, ffn_norm_w, ffn_up_w, ffn_conv_w, ffn_conv_b, ffn_down_w, final_norm_w, loss_target, m_ssm_norm_w, m_ssm_in_w, m_ssm_conv_w, m_ssm_conv_b, m_ssm_dt_bias, m_ssm_a_log, m_ssm_d, m_ssm_gate_norm_w, m_ssm_out_w, m_kv_norm_w, m_w_k, m_w_v, m_attn_norm_w, m_w_q, m_w_o, m_ffn_norm_w, m_ffn_up_w, m_ffn_conv_w, m_ffn_conv_b, m_ffn_down_w, m_final_norm_w, v_ssm_norm_w, v_ssm_in_w, v_ssm_conv_w, v_ssm_conv_b, v_ssm_dt_bias, v_ssm_a_log, v_ssm_d, v_ssm_gate_norm_w, v_ssm_out_w, v_kv_norm_w, v_w_k, v_w_v, v_attn_norm_w, v_w_q, v_w_o, v_ffn_norm_w, v_ffn_up_w, v_ffn_conv_w, v_ffn_conv_b, v_ffn_down_w, v_final_norm_w):
    args = (ssm_norm_w, ssm_in_w, ssm_conv_w, ssm_conv_b, ssm_dt_bias, ssm_a_log, ssm_d, ssm_gate_norm_w, ssm_out_w, kv_norm_w, w_k, w_v, attn_norm_w, w_q, w_o, ffn_norm_w, ffn_up_w, ffn_conv_w, ffn_conv_b, ffn_down_w, final_norm_w)
    moms = (m_ssm_norm_w, m_ssm_in_w, m_ssm_conv_w, m_ssm_conv_b, m_ssm_dt_bias, m_ssm_a_log, m_ssm_d, m_ssm_gate_norm_w, m_ssm_out_w, m_kv_norm_w, m_w_k, m_w_v, m_attn_norm_w, m_w_q, m_w_o, m_ffn_norm_w, m_ffn_up_w, m_ffn_conv_w, m_ffn_conv_b, m_ffn_down_w, m_final_norm_w)
    vels = (v_ssm_norm_w, v_ssm_in_w, v_ssm_conv_w, v_ssm_conv_b, v_ssm_dt_bias, v_ssm_a_log, v_ssm_d, v_ssm_gate_norm_w, v_ssm_out_w, v_kv_norm_w, v_w_k, v_w_v, v_attn_norm_w, v_w_q, v_w_o, v_ffn_norm_w, v_ffn_up_w, v_ffn_conv_w, v_ffn_conv_b, v_ffn_down_w, v_final_norm_w)
    local = dict(zip(WEIGHTS, args))
    m_in = dict(zip(WEIGHTS, moms))
    v_in = dict(zip(WEIGHTS, vels))
    chip = 2 * lax.axis_index("x") + lax.axis_index("y")

    full = {n: local[n] for n in REPLICATED}
    small32 = _gather_chips(_pack([local[n].reshape(-1) for n in SMALL], F32, 8), name="gather_small")
    for n, st in zip(SMALL, _unpack(small32, [local[n].shape for n in SMALL])):
        full[n] = _from_shards(st, SHARD_AXIS[n])

    pieces = _Pieces(local)
    loss_p, dx, grads, big_grads = _step(x[0], loss_target[0], full, pieces)
    gshard = {}
    for n in BIG:
        if n in STACKED:
            gshard[n] = [big_grads[n, l] for l in range(local[n].shape[0])]
        else:
            gshard[n] = big_grads[n, None].reshape(local[n].shape)

    small = SMALL + REPLICATED
    rep = _pack([loss_p.reshape(-1)] + [grads[n].reshape(-1) for n in small], F32, 8)
    tot = _sum_leading(_gather_all(rep, name="ar_gather"), name="ar_sum")
    parts = _unpack(tot, [(LANES,)] + [grads[n].shape for n in small])
    loss = jnp.sum(parts[0])
    for n, g in zip(small, parts[1:]):
        if n in SHARD_AXIS:
            size = local[n].shape[SHARD_AXIS[n]]
            g = lax.dynamic_slice_in_dim(g, chip * size, size, axis=SHARD_AXIS[n])
        gshard[n] = g

    grads_out, deltas, new_m, new_v = [], [], [], []
    for n in WEIGHTS:
        if n in STACKED:
            g, d, nm, nv = _adamw_layers(local[n], gshard[n], m_in[n], v_in[n], name=f"adamw_{n}")
        else:
            g = gshard[n]
            d, nm, nv = _adamw(local[n], g, m_in[n], v_in[n], name=f"adamw_{n}")
        grads_out.append(g)
        deltas.append(d)
        new_m.append(nm)
        new_v.append(nv)
    return (loss, dx[None], *grads_out, *deltas, *new_m, *new_v)
```

```python
import functools
import math

import jax
import jax.numpy as jnp
from jax import lax
from jax.experimental import pallas as pl
from jax.experimental.pallas import tpu as pltpu

D_INNER = 2048
SSM_HEAD_DIM = 64
SSM_HEADS = 32
SSM_GROUPS = 4
SSM_STATE = 128
SSM_CHUNK = 128
GN = SSM_GROUPS * SSM_STATE
CONV_DIM = D_INNER + 2 * GN
SB_HEADS = 16
EPS = 1e-6
ADAM_LR = 0.001
ADAM_B1 = 0.9
ADAM_B2 = 0.999
ADAM_EPS = 1e-08
ADAM_WD = 0.01
ADAM_STEP = 10

LANES = 128
SUBLANES = 8
VMEM_LIMIT = 48 * 1024 * 1024
ADAM_BLOCK_BYTES = 2 << 20
F32 = jnp.float32
BF16 = jnp.bfloat16
MESH = pl.DeviceIdType.MESH


def _cparams(sem=None):
    return pltpu.CompilerParams(dimension_semantics=sem, vmem_limit_bytes=VMEM_LIMIT)


class _SideJob:
    def __init__(self, ins, out_shapes, n_sems, start, finish):
        self.ins, self.out_shapes, self.n_sems, self.start, self.finish = ins, out_shapes, n_sems, start, finish


def _call(body, *, grid, in_specs, out_specs, out_shape, scratch_shapes=(), sem, name, args, side=None):
    in_specs, out_specs, out_shape, scratch_shapes = list(in_specs), list(out_specs), list(out_shape), list(scratch_shapes)
    n_in, n_out = len(in_specs), len(out_specs)
    if side is None:
        outs = pl.pallas_call(body, grid=grid, in_specs=in_specs, out_specs=out_specs, out_shape=out_shape,
                              scratch_shapes=scratch_shapes, compiler_params=_cparams(sem), name=name)(*args)
        return list(outs), []
    k_in, k_out = len(side.ins), len(side.out_shapes)

    def wrapped(*refs):
        ins, s_ins = refs[:n_in], refs[n_in:n_in + k_in]
        o0 = n_in + k_in
        outs, s_outs = refs[o0:o0 + n_out], refs[o0 + n_out:o0 + n_out + k_out]
        scratch, send_sems, recv_sems = refs[o0 + n_out + k_out:-2], refs[-2], refs[-1]
        ids = [pl.program_id(a) for a in range(len(grid))]
        first = functools.reduce(jnp.logical_and, [p == 0 for p in ids])
        last = functools.reduce(jnp.logical_and, [p == g - 1 for p, g in zip(ids, grid)])

        @pl.when(first)
        def _():
            side.start(s_ins, s_outs, send_sems, recv_sems)

        body(*ins, *outs, *scratch)

        @pl.when(last)
        def _():
            side.finish(s_ins, s_outs, send_sems, recv_sems)

    outs = pl.pallas_call(
        wrapped, grid=grid, in_specs=in_specs + [ANY] * k_in, out_specs=out_specs + [ANY] * k_out,
        out_shape=out_shape + list(side.out_shapes),
        scratch_shapes=scratch_shapes + [pltpu.SemaphoreType.DMA((side.n_sems,)), pltpu.SemaphoreType.DMA((side.n_sems,))],
        compiler_params=_cparams(tuple("arbitrary" for _ in grid)), name=name)(*args, *side.ins)
    return list(outs[:n_out]), list(outs[n_out:])


def _tile(n, cands):
    for c in cands:
        if n % c == 0:
            return c
    return n


def _nt(a, b):
    return lax.dot_general(a, b, (((1,), (1,)), ((), ())), preferred_element_type=F32)


def _tn(a, b):
    return lax.dot_general(a, b, (((0,), (0,)), ((), ())), preferred_element_type=F32)


def _nn(a, b):
    return jnp.dot(a, b, preferred_element_type=F32)


def _split(x, pieces):
    out = []
    for _ in range(pieces - 1):
        h = x.astype(BF16)
        out.append(h)
        x = x - h.astype(F32)
    out.append(x.astype(BF16))
    return out


def _ones_dot(ones, x, *, ones_left, pieces=3):
    o16 = ones.astype(BF16)
    acc = None
    for piece in _split(x, pieces):
        term = _nn(o16, piece) if ones_left else _nn(piece, o16)
        acc = term if acc is None else acc + term
    return acc


def _row_sums(x, pieces=2):
    return _ones_dot(jnp.ones((x.shape[1], LANES), F32), x, ones_left=False, pieces=pieces)


def _softplus(x):
    return jnp.maximum(x, 0.0) + jnp.log(1.0 + jnp.exp(-jnp.abs(x)))


def _sigmoid(x):
    return 0.5 * jnp.tanh(0.5 * x) + 0.5


MM_TILE_MAX = 1408
MM_VMEM_BUDGET = 40 * 1024 * 1024


def _divisors(n, cap):
    out = [d for d in range(min(cap, n) // LANES * LANES, 0, -LANES) if n % d == 0]
    return out or [n]


def _mm_tiles(m, n, k, a_bytes, b_bytes, o_bytes, add_bytes):
    best = None
    for tm in _divisors(m, MM_TILE_MAX):
        for tn in _divisors(n, MM_TILE_MAX):
            for tk in _divisors(k, MM_TILE_MAX):
                vmem = 2 * (tm * tk * a_bytes + tk * tn * b_bytes + tm * tn * (o_bytes + add_bytes)) + tm * tn * 4
                if vmem > MM_VMEM_BUDGET:
                    continue
                score = (tm * tn * tk, tm * tn)
                if best is None or score > best[0]:
                    best = (score, (tm, tn, tk))
    return best[1]


def _matmul(a, b, *, ta=False, tb=False, add=None, out_dtype=F32, out_parts=1, name):
    a_parts = a.shape[0] if a.ndim == 3 else 1
    b_parts = b.shape[0] if b.ndim == 3 else 1
    assert not (ta and a_parts > 1)
    a2, b2 = a.shape[-2:], b.shape[-2:]
    m, k = (a2[1], a2[0]) if ta else (a2[0], a2[1] * a_parts)
    n, kb = (b2[0], b2[1] * b_parts) if tb else (b2[1] * b_parts, b2[0])
    assert kb == k, (a.shape, b.shape)
    n_unit = math.gcd(n // out_parts, n if tb else b2[1])
    k_unit = math.gcd(k // a_parts, b2[1] if tb else k)
    tm, tn, tk = _mm_tiles(m, n_unit, k_unit, a.dtype.itemsize, b.dtype.itemsize, jnp.dtype(out_dtype).itemsize,
                           0 if add is None else add.dtype.itemsize)
    nk = k // tk
    ka, kbp = (k // a_parts) // tk, (k // b_parts) // tk
    nb, no = (n // b_parts) // tn, (n // out_parts) // tn

    def body(*refs):
        if add is None:
            a_ref, b_ref, o_ref = refs[:3]
            add_ref = None
        else:
            a_ref, b_ref, add_ref, o_ref = refs[:4]
        kk = pl.program_id(2)
        dn = (((0 if ta else 1,), (1 if tb else 0,)), ((), ()))
        prod = lax.dot_general(a_ref[...].astype(BF16), b_ref[...].astype(BF16), dn, preferred_element_type=F32)

        def finish(r):
            if add_ref is not None:
                r = r + add_ref[...].astype(F32)
            o_ref[...] = r.astype(o_ref.dtype)

        if nk == 1:
            finish(prod)
            return
        acc_ref = refs[-1]

        @pl.when(kk == 0)
        def _():
            acc_ref[...] = prod

        @pl.when(jnp.logical_and(kk > 0, kk < nk - 1))
        def _():
            acc_ref[...] += prod

        @pl.when(kk == nk - 1)
        def _():
            finish(acc_ref[...] + prod)

    if ta:
        a_spec = pl.BlockSpec((tk, tm), lambda i, j, kk: (kk, i))
    elif a_parts > 1:
        a_spec = pl.BlockSpec((None, tm, tk), lambda i, j, kk: (kk // ka, i, kk % ka))
    else:
        a_spec = pl.BlockSpec((tm, tk), lambda i, j, kk: (i, kk))
    if b_parts == 1:
        b_spec = pl.BlockSpec((tn, tk), lambda i, j, kk: (j, kk)) if tb else pl.BlockSpec((tk, tn), lambda i, j, kk: (kk, j))
    elif tb:
        b_spec = pl.BlockSpec((None, tn, tk), lambda i, j, kk: (kk // kbp, j, kk % kbp))
    else:
        b_spec = pl.BlockSpec((None, tk, tn), lambda i, j, kk: (j // nb, kk, j % nb))
    if out_parts > 1:
        o_spec = pl.BlockSpec((None, tm, tn), lambda i, j, kk: (j // no, i, j % no))
        o_shape = jax.ShapeDtypeStruct((out_parts, m, n // out_parts), out_dtype)
    else:
        o_spec = pl.BlockSpec((tm, tn), lambda i, j, kk: (i, j))
        o_shape = jax.ShapeDtypeStruct((m, n), out_dtype)
    in_specs = [a_spec, b_spec]
    args = [a, b]
    if add is not None:
        in_specs.append(pl.BlockSpec((tm, tn), lambda i, j, kk: (i, j)))
        args.append(add)
    return pl.pallas_call(
        body,
        grid=(m // tm, n // tn, nk),
        in_specs=in_specs,
        out_specs=o_spec,
        out_shape=o_shape,
        scratch_shapes=[pltpu.VMEM((tm, tn), F32)] if nk > 1 else [],
        compiler_params=_cparams(("parallel", "parallel", "arbitrary")),
        name=name,
    )(*args)


def _rmsnorm_fwd(x, w, *, name):
    t, d = x.shape
    tb = _tile(t, (512, 256, 128))

    def body(x_ref, w_ref, o_ref):
        xv = x_ref[...]
        r = lax.rsqrt(jnp.mean(xv * xv, axis=-1, keepdims=True) + EPS)
        o_ref[...] = (xv * r * w_ref[...]).astype(o_ref.dtype)

    return pl.pallas_call(
        body,
        grid=(t // tb,),
        in_specs=[pl.BlockSpec((tb, d), lambda i: (i, 0)), pl.BlockSpec((1, d), lambda i: (0, 0))],
        out_specs=pl.BlockSpec((tb, d), lambda i: (i, 0)),
        out_shape=jax.ShapeDtypeStruct((t, d), BF16),
        compiler_params=_cparams(("parallel",)),
        name=name,
    )(x, w.reshape(1, d))


def _rmsnorm_bwd(x, dys, dres, *, name):
    t, d = x.shape
    tb = _tile(t, (512, 256, 128))
    nn = len(dys)
    has_res = dres is not None

    def body(*refs):
        x_ref = refs[0]
        dy_refs = refs[1:1 + nn]
        w_refs = refs[1 + nn:1 + 2 * nn]
        pos = 1 + 2 * nn
        res_ref = refs[pos] if has_res else None
        pos += 1 if has_res else 0
        dx_ref = refs[pos]
        dw_refs = refs[pos + 1:pos + 1 + nn]
        i = pl.program_id(0)
        xv = x_ref[...]
        r = lax.rsqrt(jnp.mean(xv * xv, axis=-1, keepdims=True) + EPS)
        xn = xv * r
        dx = res_ref[...] if has_res else jnp.zeros_like(xv)
        for q in range(nn):
            dy = dy_refs[q][...].astype(F32)
            g = dy * w_refs[q][...]
            dx = dx + r * (g - xn * jnp.mean(g * xn, axis=-1, keepdims=True))
            dwp = jnp.sum(dy * xn, axis=0, keepdims=True)

            @pl.when(i == 0)
            def _(q=q, dwp=dwp):
                dw_refs[q][...] = dwp

            @pl.when(i > 0)
            def _(q=q, dwp=dwp):
                dw_refs[q][...] += dwp
        dx_ref[...] = dx

    row = pl.BlockSpec((tb, d), lambda i: (i, 0))
    vec = pl.BlockSpec((1, d), lambda i: (0, 0))
    in_specs = [row] + [row] * nn + [vec] * nn + ([row] if has_res else [])
    args = [x] + [p[0] for p in dys] + [p[1].reshape(1, d) for p in dys] + ([dres] if has_res else [])
    outs = pl.pallas_call(
        body,
        grid=(t // tb,),
        in_specs=in_specs,
        out_specs=[row] + [vec] * nn,
        out_shape=[jax.ShapeDtypeStruct((t, d), F32)] + [jax.ShapeDtypeStruct((1, d), F32)] * nn,
        compiler_params=_cparams(("arbitrary",)),
        name=name,
    )(*args)
    return outs[0], list(outs[1:])


def _loss_head(x, w, target, *, name):
    t, d = x.shape
    tb = _tile(t, (512, 256, 128))

    def body(x_ref, w_ref, t_ref, loss_ref, dx_ref, dw_ref):
        i = pl.program_id(0)
        xv = x_ref[...]
        wv = w_ref[...]
        r = lax.rsqrt(jnp.mean(xv * xv, axis=-1, keepdims=True) + EPS)
        xn = xv * r
        e = xn * wv - t_ref[...]
        lp = 0.5 * jnp.sum(jnp.mean(e * e, axis=-1, keepdims=True), axis=0, keepdims=True)
        dy = e * (1.0 / d)
        g = dy * wv
        dx_ref[...] = r * (g - xn * jnp.mean(g * xn, axis=-1, keepdims=True))
        dwp = jnp.sum(dy * xn, axis=0, keepdims=True)
        lpv = jnp.broadcast_to(lp, (1, LANES)) * (1.0 / LANES)

        @pl.when(i == 0)
        def _():
            dw_ref[...] = dwp
            loss_ref[...] = lpv

        @pl.when(i > 0)
        def _():
            dw_ref[...] += dwp
            loss_ref[...] += lpv

    row = pl.BlockSpec((tb, d), lambda i: (i, 0))
    vec = pl.BlockSpec((1, d), lambda i: (0, 0))
    return pl.pallas_call(
        body,
        grid=(t // tb,),
        in_specs=[row, vec, row],
        out_specs=[pl.BlockSpec((1, LANES), lambda i: (0, 0)), row, vec],
        out_shape=[jax.ShapeDtypeStruct((1, LANES), F32), jax.ShapeDtypeStruct((t, d), F32),
                   jax.ShapeDtypeStruct((1, d), F32)],
        compiler_params=_cparams(("arbitrary",)),
        name=name,
    )(x, w.reshape(1, d), target)


ROW_CHUNK = 64
PAD = SUBLANES


class _Strip:
    def __init__(self, head_ref, x_ref, rows):
        self.head_ref, self.x_ref = head_ref, x_ref
        head_ref[0:PAD, :] = jnp.zeros((PAD, head_ref.shape[1]), F32)
        head_ref[pl.ds(PAD, rows), :] = x_ref[pl.ds(0, rows), :]

    def rows(self, r0, rows, back):
        if r0 == 0:
            return self.head_ref[pl.ds(PAD - back, rows), :]
        return self.x_ref[pl.ds(r0 - back, rows), :]


def _shifted(strip, r0, rows, back):
    return strip.rows(r0, rows, back)


def _conv_taps(strip, w_ref, r0, rows, kw):
    acc = None
    for j in range(kw):
        term = _shifted(strip, r0, rows, kw - 1 - j) * w_ref[j:j + 1, :]
        acc = term if acc is None else acc + term
    return acc


def _fill_pad(pad_ref, x_ref, rows):
    return _Strip(pad_ref, x_ref, rows)


def _conv_silu_fwd(x, w, b, *, x_off=0, name):
    t = x.shape[0]
    kw, c = w.shape
    cw = _tile(math.gcd(c, x_off) if x_off else c, (256, 128))
    ob = x_off // cw
    rc = _tile(t, (ROW_CHUNK,))

    def body(x_ref, w_ref, b_ref, o_ref, pad_ref):
        xs = _fill_pad(pad_ref, x_ref, rc)
        for r0 in range(0, t, rc):
            pre = _conv_taps(xs, w_ref, r0, rc, kw) + b_ref[...]
            o_ref[pl.ds(r0, rc), :] = pre * _sigmoid(pre)

    strip = pl.BlockSpec((t, cw), lambda i: (0, i))
    return pl.pallas_call(
        body,
        grid=(c // cw,),
        in_specs=[pl.BlockSpec((t, cw), lambda i: (0, i + ob)), pl.BlockSpec((kw, cw), lambda i: (0, i)),
                  pl.BlockSpec((1, cw), lambda i: (0, i))],
        out_specs=strip,
        out_shape=jax.ShapeDtypeStruct((t, c), F32),
        scratch_shapes=[pltpu.VMEM((PAD + rc, cw), F32)],
        compiler_params=_cparams(("parallel",)),
        name=name,
    )(x, w, b.reshape(1, c))


def _conv_bwd_core(dpre_pad_ref, x_pad_ref, w_ref, dx_ref, dw_ref, db_ref, t, rc, kw):
    cw = dx_ref.shape[1]

    def fold(a):
        return jnp.sum(a.reshape(rc // SUBLANES, SUBLANES, cw), axis=0) if rc % SUBLANES == 0 else jnp.sum(a, axis=0, keepdims=True)

    dws = [None] * kw
    dbs = None
    for r0 in range(0, t, rc):
        dpre = dpre_pad_ref[pl.ds(PAD + r0, rc), :]
        dx = None
        for j in range(kw):
            s = kw - 1 - j
            term = dpre_pad_ref[pl.ds(PAD + r0 + s, rc), :] * w_ref[j:j + 1, :]
            dx = term if dx is None else dx + term
            part = fold(dpre * _shifted(x_pad_ref, r0, rc, s))
            dws[j] = part if dws[j] is None else dws[j] + part
        part = fold(dpre)
        dbs = part if dbs is None else dbs + part
        dx_ref[pl.ds(r0, rc), :] = dx
    for j in range(kw):
        dw_ref[j:j + 1, :] = jnp.sum(dws[j], axis=0, keepdims=True)
    db_ref[...] = jnp.sum(dbs, axis=0, keepdims=True)


def _conv_silu_bwd(x, w, b, dact, *, x_off=0, into=None, name):
    t = x.shape[0]
    kw, c = w.shape
    parts = dact if isinstance(dact, (list, tuple)) else [dact]
    widths = [p.shape[1] for p in parts]
    assert sum(widths) == c
    cw = _tile(functools.reduce(math.gcd, widths + [x_off or c]), (256, 128) if len(parts) == 1 else (128,))
    ob = x_off // cw
    rc = _tile(t, (ROW_CHUNK * 256 // cw, ROW_CHUNK))
    firsts =[sum(widths[:p]) // cw for p in range(len(parts))]
    counts = [wd // cw for wd in widths]
    n_p = len(parts)

    def body(x_ref, w_ref, b_ref, *rest):
        da_refs = rest[:n_p]
        dx_ref, dw_ref, db_ref, xpad_ref, dpad_ref = rest[-5 - (n_p > 1):][:5]
        if n_p > 1:
            da_ref = rest[-1]
            i = pl.program_id(0)
            for p in range(n_p):
                @pl.when(jnp.logical_and(i >= firsts[p], i < firsts[p] + counts[p]))
                def _(p=p):
                    da_ref[...] = da_refs[p][...]
        else:
            da_ref = da_refs[0]
        xs = _fill_pad(xpad_ref, x_ref, rc)
        dpad_ref[0:PAD, :] = jnp.zeros((PAD, cw), F32)
        dpad_ref[pl.ds(PAD + t, PAD), :] = jnp.zeros((PAD, cw), F32)
        for r0 in range(0, t, rc):
            pre = _conv_taps(xs, w_ref, r0, rc, kw) + b_ref[...]
            sg = _sigmoid(pre)
            dpad_ref[pl.ds(PAD + r0, rc), :] = da_ref[pl.ds(r0, rc), :] * (sg * (1.0 + pre * (1.0 - sg)))
        _conv_bwd_core(dpad_ref, xs, w_ref, dx_ref, dw_ref, db_ref, t, rc, kw)

    strip = pl.BlockSpec((t, cw), lambda i: (0, i))
    wspec = pl.BlockSpec((kw, cw), lambda i: (0, i))
    bspec = pl.BlockSpec((1, cw), lambda i: (0, i))
    xspec = pl.BlockSpec((t, cw), lambda i: (0, i + ob))
    dspecs = [pl.BlockSpec((t, cw), lambda i, f=f, n=n: (0, jnp.clip(i - f, 0, n - 1))) for f, n in zip(firsts, counts)]
    extra = {} if into is None else dict(input_output_aliases={3 + n_p: 0})
    pad = pltpu.VMEM((t + 2 * PAD, cw), F32)
    return pl.pallas_call(
        body,
        grid=(c // cw,),
        in_specs=[xspec, wspec, bspec] + dspecs + ([] if into is None else [ANY]),
        out_specs=[strip if into is None else xspec, wspec, bspec],
        out_shape=[jax.ShapeDtypeStruct((t, c) if into is None else into.shape, F32), jax.ShapeDtypeStruct((kw, c), F32),
                   jax.ShapeDtypeStruct((1, c), F32)],
        scratch_shapes=[pad, pad] + ([pltpu.VMEM((t, cw), F32)] if n_p > 1 else []),
        compiler_params=_cparams(("arbitrary",)),
        name=name,
        **extra,
    )(x, w, b.reshape(1, c), *parts, *([] if into is None else [into]))


def _conv_glu_fwd(hid, w, b, *, side=None, name):
    t, c2 = hid.shape
    f = c2 // 2
    kw = w.shape[0]
    cw = _tile(f, (256, 128))
    nf = f // cw
    rc = _tile(t, (ROW_CHUNK,))

    def body(g_ref, v_ref, wg_ref, wv_ref, bg_ref, bv_ref, o_ref, gpad_ref, vpad_ref):
        gs_, vs_ = _fill_pad(gpad_ref, g_ref, rc), _fill_pad(vpad_ref, v_ref, rc)
        for r0 in range(0, t, rc):
            gate = _conv_taps(gs_, wg_ref, r0, rc, kw) + bg_ref[...]
            val = _conv_taps(vs_, wv_ref, r0, rc, kw) + bv_ref[...]
            o_ref[pl.ds(r0, rc), :] = (gate * _sigmoid(gate) * val).astype(o_ref.dtype)

    gs = pl.BlockSpec((t, cw), lambda i: (0, i))
    vs = pl.BlockSpec((t, cw), lambda i: (0, i + nf))
    b2 = b.reshape(1, c2)
    (act,), side_outs = _call(
        body,
        grid=(nf,),
        in_specs=[gs, vs, pl.BlockSpec((kw, cw), lambda i: (0, i)), pl.BlockSpec((kw, cw), lambda i: (0, i + nf)),
                  pl.BlockSpec((1, cw), lambda i: (0, i)), pl.BlockSpec((1, cw), lambda i: (0, i + nf))],
        out_specs=[gs],
        out_shape=[jax.ShapeDtypeStruct((t, f), BF16)],
        scratch_shapes=[pltpu.VMEM((PAD + rc, cw), F32), pltpu.VMEM((PAD + rc, cw), F32)],
        sem=("parallel",),
        name=name,
        args=(hid, hid, w, w, b2, b2),
        side=side,
    )
    return act, side_outs


def _conv_glu_bwd(hid, w, b, dact, *, side=None, name):
    t, c2 = hid.shape
    f = c2 // 2
    kw = w.shape[0]
    cw = _tile(f, (128,))
    nf = f // cw
    rc = _tile(t, (ROW_CHUNK * 256 // cw, ROW_CHUNK))

    def body(g_ref, v_ref, wg_ref, wv_ref, bg_ref, bv_ref, da_ref,
             dgv_ref, dwg_ref, dwv_ref, dbg_ref, dbv_ref,
             gpad_ref, vpad_ref, dgpad_ref, dvpad_ref):
        gs_, vs_ = _fill_pad(gpad_ref, g_ref, rc), _fill_pad(vpad_ref, v_ref, rc)
        for ref in (dgpad_ref, dvpad_ref):
            ref[0:PAD, :] = jnp.zeros((PAD, cw), F32)
            ref[pl.ds(PAD + t, PAD), :] = jnp.zeros((PAD, cw), F32)
        for r0 in range(0, t, rc):
            gate = _conv_taps(gs_, wg_ref, r0, rc, kw) + bg_ref[...]
            val = _conv_taps(vs_, wv_ref, r0, rc, kw) + bv_ref[...]
            sg = _sigmoid(gate)
            da = da_ref[pl.ds(r0, rc), :].astype(F32)
            dgpad_ref[pl.ds(PAD + r0, rc), :] = da * val * (sg * (1.0 + gate * (1.0 - sg)))
            dvpad_ref[pl.ds(PAD + r0, rc), :] = da * (gate * sg)
        _conv_bwd_core(dgpad_ref, gs_, wg_ref, dgv_ref.at[0], dwg_ref, dbg_ref, t, rc, kw)
        _conv_bwd_core(dvpad_ref, vs_, wv_ref, dgv_ref.at[1], dwv_ref, dbv_ref, t, rc, kw)

    gs = pl.BlockSpec((t, cw), lambda i: (0, i))
    vs = pl.BlockSpec((t, cw), lambda i: (0, i + nf))
    wg = pl.BlockSpec((kw, cw), lambda i: (0, i))
    wv = pl.BlockSpec((kw, cw), lambda i: (0, i + nf))
    bg = pl.BlockSpec((1, cw), lambda i: (0, i))
    bv = pl.BlockSpec((1, cw), lambda i: (0, i + nf))
    b2 = b.reshape(1, c2)
    pad = pltpu.VMEM((t + 2 * PAD, cw), F32)
    outs, side_outs = _call(
        body,
        grid=(nf,),
        in_specs=[gs, vs, wg, wv, bg, bv, gs],
        out_specs=[pl.BlockSpec((2, t, cw), lambda i: (0, 0, i)), wg, wg, bg, bg],
        out_shape=[jax.ShapeDtypeStruct((2, t, f), F32),
                   jax.ShapeDtypeStruct((kw, f), F32), jax.ShapeDtypeStruct((kw, f), F32),
                   jax.ShapeDtypeStruct((1, f), F32), jax.ShapeDtypeStruct((1, f), F32)],
        scratch_shapes=[pad, pad, pad, pad],
        sem=("parallel",),
        name=name,
        args=(hid, hid, w, w, b2, b2, dact),
        side=side,
    )
    return (*outs, side_outs)


def _gate_norm_fwd(y, zx, w, *, name):
    t, di = y.shape
    gsz = di // SSM_GROUPS
    tb = _tile(t, (256, 128))

    def body(y_ref, z_ref, w_ref, o_ref):
        for g in range(SSM_GROUPS):
            sl = slice(g * gsz, (g + 1) * gsz)
            zv = z_ref[:, sl]
            gv = y_ref[:, sl] * (zv * _sigmoid(zv))
            r = lax.rsqrt(jnp.mean(gv * gv, axis=-1, keepdims=True) + EPS)
            o_ref[:, sl] = (gv * r * w_ref[:, sl]).astype(o_ref.dtype)

    row = pl.BlockSpec((tb, di), lambda i: (i, 0))
    return pl.pallas_call(
        body,
        grid=(t // tb,),
        in_specs=[row, row, pl.BlockSpec((1, di), lambda i: (0, 0))],
        out_specs=row,
        out_shape=jax.ShapeDtypeStruct((t, di), BF16),
        compiler_params=_cparams(("parallel",)),
        name=name,
    )(y, zx, w.reshape(1, di))


def _gate_norm_bwd(y, zx, w, dyn, *, side=None, name):
    t, di = y.shape
    gsz = di // SSM_GROUPS
    tb = _tile(t, (256, 128))

    def body(y_ref, z_ref, w_ref, d_ref, dy_ref, dz_ref, dw_ref):
        i = pl.program_id(0)
        for g in range(SSM_GROUPS):
            sl = slice(g * gsz, (g + 1) * gsz)
            zv = z_ref[:, sl]
            yv = y_ref[:, sl]
            sg = _sigmoid(zv)
            sz = zv * sg
            gv = yv * sz
            r = lax.rsqrt(jnp.mean(gv * gv, axis=-1, keepdims=True) + EPS)
            gn = gv * r
            dn = d_ref[:, sl].astype(F32)
            q = dn * w_ref[:, sl]
            dg = r * (q - gn * jnp.mean(q * gn, axis=-1, keepdims=True))
            dy_ref[:, sl] = dg * sz
            dz_ref[:, sl] = dg * yv * (sg * (1.0 + zv * (1.0 - sg)))
            dwp = jnp.sum(dn * gn, axis=0, keepdims=True)

            @pl.when(i == 0)
            def _(sl=sl, dwp=dwp):
                dw_ref[:, sl] = dwp

            @pl.when(i > 0)
            def _(sl=sl, dwp=dwp):
                dw_ref[:, sl] += dwp

    row = pl.BlockSpec((tb, di), lambda i: (i, 0))
    vec = pl.BlockSpec((1, di), lambda i: (0, 0))
    outs, side_outs = _call(
        body,
        grid=(t // tb,),
        in_specs=[row, row, vec, row],
        out_specs=[row, row, vec],
        out_shape=[jax.ShapeDtypeStruct((t, di), F32), jax.ShapeDtypeStruct((t, zx.shape[1]), F32),
                   jax.ShapeDtypeStruct((1, di), F32)],
        sem=("arbitrary",),
        name=name,
        args=(y, zx, w.reshape(1, di), dyn),
        side=side,
    )
    return (*outs, side_outs)


def _adamw(w, g, m, v, *, name):
    shape = w.shape
    cols = shape[-1]
    rows = w.size // cols
    w2, g2, m2, v2 = (a.reshape(rows, cols) for a in (w, g, m, v))
    tr = rows if rows * cols * 4 <= ADAM_BLOCK_BYTES else _row_tile(rows, cols)
    c1 = 1.0 - ADAM_B1 ** ADAM_STEP
    c2 = 1.0 - ADAM_B2 ** ADAM_STEP

    def body(w_ref, g_ref, m_ref, v_ref, d_ref, nm_ref, nv_ref):
        gv = g_ref[...]
        nm = ADAM_B1 * m_ref[...] + (1.0 - ADAM_B1) * gv
        nv = ADAM_B2 * v_ref[...] + (1.0 - ADAM_B2) * (gv * gv)
        d_ref[...] = -ADAM_LR * ((nm / c1) / (jnp.sqrt(nv / c2) + ADAM_EPS) + ADAM_WD * w_ref[...])
        nm_ref[...] = nm
        nv_ref[...] = nv

    blk = pl.BlockSpec((tr, cols), lambda i: (i, 0))
    outs = pl.pallas_call(
        body,
        grid=(rows // tr,),
        in_specs=[blk] * 4,
        out_specs=[blk] * 3,
        out_shape=[jax.ShapeDtypeStruct((rows, cols), F32)] * 3,
        compiler_params=_cparams(("parallel",)),
        name=name,
    )(w2, g2, m2, v2)
    return tuple(o.reshape(shape) for o in outs)


def _adamw_layers(w, gs, m, v, *, name):
    n_l, rows, cols = w.shape
    assert len(gs) == n_l
    tr = _row_tile(rows, cols)
    c1 = 1.0 - ADAM_B1 ** ADAM_STEP
    c2 = 1.0 - ADAM_B2 ** ADAM_STEP

    def body(*refs):
        w_ref, m_ref, v_ref = refs[:3]
        g_refs = refs[3:3 + n_l]
        g_ref, d_ref, nm_ref, nv_ref = refs[3 + n_l:]
        layer = pl.program_id(0)
        gv = g_refs[0][...]
        for q in range(1, n_l):
            gv = jnp.where(layer == q, g_refs[q][...], gv)
        nm = ADAM_B1 * m_ref[...] + (1.0 - ADAM_B1) * gv
        nv = ADAM_B2 * v_ref[...] + (1.0 - ADAM_B2) * (gv * gv)
        g_ref[...] = gv
        d_ref[...] = -ADAM_LR * ((nm / c1) / (jnp.sqrt(nv / c2) + ADAM_EPS) + ADAM_WD * w_ref[...])
        nm_ref[...] = nm
        nv_ref[...] = nv

    stacked = pl.BlockSpec((None, tr, cols), lambda l, i: (l, i, 0))
    single = pl.BlockSpec((tr, cols), lambda l, i: (i, 0))
    return pl.pallas_call(
        body,
        grid=(n_l, rows // tr),
        in_specs=[stacked] * 3 + [single] * n_l,
        out_specs=[stacked] * 4,
        out_shape=[jax.ShapeDtypeStruct(w.shape, F32)] * 4,
        compiler_params=_cparams(("parallel", "parallel")),
        name=name,
    )(w, m, v, *gs)


def _ssd_scalars(dtc_ref, dtr_ref, hpc_ref, hpr_ref, ln):
    assert SSM_CHUNK == SSM_STATE == LANES, "the SSD kernels mix chunk, state and lane-wide tiles freely"
    bias_c, alog_c = hpc_ref[0, 0:1, :], hpc_ref[0, 1:2, :]
    bias_r, alog_r = hpr_ref[0, :, 0:1], hpr_ref[0, :, 1:2]
    a_c, a_r = -jnp.exp(alog_c), -jnp.exp(alog_r)
    raw_c = dtc_ref[0] + bias_c
    dt_c = _softplus(raw_c)
    dt_r = _softplus(dtr_ref[0] + bias_r)
    row = lax.broadcasted_iota(jnp.int32, (ln, ln), 0)
    col = lax.broadcasted_iota(jnp.int32, (ln, ln), 1)
    lower = (col <= row).astype(F32)
    upper = (row <= col).astype(F32)
    acs_c = _ones_dot(lower, dt_c * a_c, ones_left=True)
    acs_r = _ones_dot(upper, dt_r * a_r, ones_left=False)
    return raw_c, dt_c, a_c, acs_c, acs_r, row, col


def _ssd_specs(t, di, g_n, n_st, rp, ln, r_h, rev):
    nc = t // ln
    cidx = (lambda c: nc - 1 - c) if rev else (lambda c: c)
    xs = pl.BlockSpec((ln, rp), lambda g, c: (cidx(c), g))
    bm = pl.BlockSpec((ln, n_st), lambda g, c: (cidx(c), di // n_st + g))
    cm = pl.BlockSpec((ln, n_st), lambda g, c: (cidx(c), di // n_st + g_n + g))
    dtc = pl.BlockSpec((1, ln, r_h), lambda g, c: (g, cidx(c), 0))
    dtr = pl.BlockSpec((1, r_h, ln), lambda g, c: (g, 0, cidx(c)))
    hpc = pl.BlockSpec((1, 3, r_h), lambda g, c: (g, 0, 0))
    hpr = pl.BlockSpec((1, r_h, 3), lambda g, c: (g, 0, 0))
    prev = pl.BlockSpec((1, rp, n_st), lambda g, c: (cidx(c), g, 0))
    return xs, bm, cm, dtc, dtr, hpc, hpr, prev


def _ssd_fwd(xbc, dtc, dtr, hpc, hpr, *, side=None, name):
    t = xbc.shape[0]
    di, g_n, n_st, p_h, ln = D_INNER, SSM_GROUPS, SSM_STATE, SSM_HEAD_DIM, SSM_CHUNK
    r_h = SSM_HEADS // g_n
    rp = r_h * p_h
    nc = t // ln

    def body(xs_ref, b_ref, c_ref, dtc_ref, dtr_ref, hpc_ref, hpr_ref, y_ref, prev_ref, st_ref):
        @pl.when(pl.program_id(1) == 0)
        def _():
            st_ref[...] = jnp.zeros_like(st_ref)

        _, dt_c, _, acs_c, acs_r, row, col = _ssd_scalars(dtc_ref, dtr_ref, hpc_ref, hpr_ref, ln)
        bm = b_ref[...]
        cm = c_ref[...]
        cm16 = cm.astype(BF16)
        cb = _nt(cm16, bm.astype(BF16))
        causal = row >= col
        for r in range(r_h):
            sl = slice(r * p_h, (r + 1) * p_h)
            xs = xs_ref[:, sl]
            acs = jnp.broadcast_to(acs_c[:, r:r + 1], (ln, ln))
            last = acs[ln - 1:ln, :]
            lm = jnp.where(causal, jnp.exp(acs - acs_r[r:r + 1, :]), 0.0)
            xd = (xs * jnp.broadcast_to(dt_c[:, r:r + 1], (ln, p_h))).astype(BF16)
            prev = st_ref[sl, :]
            y = _nn((cb * lm).astype(BF16), xd)
            y = y + _nt(cm16, prev.astype(BF16)) * jnp.exp(acs[:, :p_h])
            y_ref[:, sl] = y + hpc_ref[0, 2:3, r:r + 1] * xs
            prev_ref[0, sl, :] = prev
            bd = (bm * jnp.exp(last - acs[:, :n_st])).astype(BF16)
            st_ref[sl, :] = prev * jnp.exp(last[:, :n_st]) + _tn(xd, bd)

    xs, bm, cm, dtcs, dtrs, hpcs, hprs, prev = _ssd_specs(t, di, g_n, n_st, rp, ln, r_h, False)
    (y, prev_out), side_outs = _call(
        body,
        grid=(g_n, nc),
        in_specs=[xs, bm, cm, dtcs, dtrs, hpcs, hprs],
        out_specs=[xs, prev],
        out_shape=[jax.ShapeDtypeStruct((t, di), F32), jax.ShapeDtypeStruct((nc, g_n * rp, n_st), F32)],
        scratch_shapes=[pltpu.VMEM((rp, n_st), F32)],
        sem=("parallel", "arbitrary"),
        name=name,
        args=(xbc, xbc, xbc, dtc, dtr, hpc, hpr),
        side=side,
    )
    return y, prev_out, side_outs


def _ssd_bwd(xbc, dtc, dtr, hpc, hpr, prev, dy, *, side=None, name):
    t = xbc.shape[0]
    di, g_n, n_st, p_h, ln = D_INNER, SSM_GROUPS, SSM_STATE, SSM_HEAD_DIM, SSM_CHUNK
    r_h = SSM_HEADS // g_n
    rp = r_h * p_h
    nc = t // ln

    def body(xs_ref, b_ref, c_ref, dtc_ref, dtr_ref, hpc_ref, hpr_ref, prev_ref, dy_ref,
             dxs_ref, db_ref, dc_ref, ddt_ref, hg_ref, ds_ref):
        step = pl.program_id(1)

        @pl.when(step == 0)
        def _():
            ds_ref[...] = jnp.zeros_like(ds_ref)

        raw_c, dt_c, a_c, acs_c, acs_r, row, col = _ssd_scalars(dtc_ref, dtr_ref, hpc_ref, hpr_ref, ln)
        bm = b_ref[...]
        cm = c_ref[...]
        bm16, cm16 = bm.astype(BF16), cm.astype(BF16)
        cb = _nt(cm16, bm16)
        cbt = _nt(bm16, cm16)
        lane_r = lax.broadcasted_iota(jnp.int32, (ln, r_h), 1)
        dacs_all = jnp.zeros((ln, r_h), F32)
        ddtx_all = jnp.zeros((ln, r_h), F32)
        dd_all = jnp.zeros((ln, r_h), F32)
        dcb = jnp.zeros((ln, ln), F32)
        dcbt = jnp.zeros((ln, ln), F32)
        dc_acc = jnp.zeros((ln, n_st), F32)
        db_acc = jnp.zeros((ln, n_st), F32)
        for r in range(r_h):
            sl = slice(r * p_h, (r + 1) * p_h)
            xs = xs_ref[:, sl]
            dyv = dy_ref[:, sl]
            dy16 = dyv.astype(BF16)
            acs = jnp.broadcast_to(acs_c[:, r:r + 1], (ln, ln))
            dtv = jnp.broadcast_to(dt_c[:, r:r + 1], (ln, p_h))
            acsr = acs_r[r:r + 1, :]
            last = acs[ln - 1:ln, :]
            xd = xs * dtv
            xd16 = xd.astype(BF16)
            lm = jnp.where(row >= col, jnp.exp(acs - acsr), 0.0)
            lmt = jnp.where(col >= row, jnp.exp(acsr - acs), 0.0)
            m_ls = cb * lm
            m_sl = cbt * lmt
            dm = _nt(dy16, xd16)
            dmt = _nt(xd16, dy16)
            dxd = _nn(m_sl.astype(BF16), dy16)
            dacs = _row_sums(dm * m_ls - dmt * m_sl)
            dcb = dcb + dm * lm
            dcbt = dcbt + dmt * lmt
            prev = prev_ref[0, sl, :]
            prev16 = prev.astype(BF16)
            e = jnp.exp(acs[:, :p_h])
            y_off = _nt(cm16, prev16) * e
            dacs = dacs + _row_sums(dyv * y_off)
            dyo16 = (dyv * e).astype(BF16)
            dc_acc = dc_acc + _nn(dyo16, prev16)
            dprev = _tn(dyo16, cm16)
            ds = ds_ref[sl, :]
            ds16 = ds.astype(BF16)
            decay = jnp.exp(last - acs)[:, :n_st]
            bd16 = (bm * decay).astype(BF16)
            dbd = _nn(xd16, ds16)
            dxd = dxd + _nt(bd16, ds16)
            db_acc = db_acc + dbd * decay
            tdec = _row_sums(dbd * bm) * decay
            cd = jnp.exp(last)
            dlast = (jnp.sum(tdec, axis=0, keepdims=True)
                     + jnp.sum(_row_sums(prev * ds), axis=0, keepdims=True) * cd)
            ds_ref[sl, :] = dprev + cd[:, :n_st] * ds
            dskip = hpc_ref[0, 2:3, r:r + 1]
            dxs_ref[:, sl] = dxd * dtv + dskip * dyv
            dacs = dacs - tdec + jnp.where(row == ln - 1, dlast, 0.0)
            dacs_all = jnp.where(lane_r == r, dacs[:, :r_h], dacs_all)
            ddtx_all = jnp.where(lane_r == r, _row_sums(dxd * xs)[:, :r_h], ddtx_all)
            dd_all = jnp.where(lane_r == r, _row_sums(dyv * xs)[:, :r_h], dd_all)
        dc_ref[...] = dc_acc + _nn(dcb.astype(BF16), bm16)
        db_ref[...] = db_acc + _nn(dcbt.astype(BF16), cm16)
        upper = (row <= col).astype(F32)
        dad = _ones_dot(upper, dacs_all, ones_left=True)
        ddt = dad * a_c + ddtx_all
        ddt_raw = ddt * _sigmoid(raw_c)
        ddt_ref[0] = ddt_raw
        d_bias = jnp.sum(ddt_raw, axis=0, keepdims=True)
        d_alog = jnp.sum(dad * dt_c, axis=0, keepdims=True) * a_c
        d_d = jnp.sum(dd_all, axis=0, keepdims=True)
        hg = jnp.concatenate([d_bias, d_alog, d_d], axis=0)

        @pl.when(step == 0)
        def _():
            hg_ref[0] = hg

        @pl.when(step > 0)
        def _():
            hg_ref[0] += hg

    xs, bms, cms, dtcs, dtrs, hpcs, hprs, prevs = _ssd_specs(t, di, g_n, n_st, rp, ln, r_h, True)
    bout = pl.BlockSpec((ln, n_st), lambda g, c: (nc - 1 - c, g))
    outs, side_outs = _call(
        body,
        grid=(g_n, nc),
        in_specs=[xs, bms, cms, dtcs, dtrs, hpcs, hprs, prevs, xs],
        out_specs=[xs, bout, bout, dtcs, hpcs],
        out_shape=[jax.ShapeDtypeStruct((t, di), F32), jax.ShapeDtypeStruct((t, g_n * n_st), F32),
                   jax.ShapeDtypeStruct((t, g_n * n_st), F32), jax.ShapeDtypeStruct((g_n, t, r_h), F32),
                   jax.ShapeDtypeStruct((g_n, 3, r_h), F32)],
        scratch_shapes=[pltpu.VMEM((rp, n_st), F32)],
        sem=("parallel", "arbitrary"),
        name=name,
        args=(xbc, xbc, xbc, dtc, dtr, hpc, hpr, prev, dy),
        side=side,
    )
    return (*outs, side_outs)


SB_KEYS = 256
SB_QUERIES = (512, 256)
SB_CUTOFF = 110.0
SB_PIECES = 2


def _sb_logits(qs, kv, valid):
    z = _nt(qs, kv)
    nz = -z
    lg = jnp.minimum(nz, 0.0) - jnp.log(1.0 + jnp.exp(jnp.minimum(z, nz)))
    return z + lg, (lg if valid is None else jnp.where(valid, lg, 0.0))


def _sb_iota(tq):
    diff = lax.broadcasted_iota(jnp.int32, (tq, SB_KEYS), 1) - lax.broadcasted_iota(jnp.int32, (tq, SB_KEYS), 0)
    krow = lax.broadcasted_iota(jnp.int32, (SB_KEYS, SB_KEYS), 0)
    kcol = lax.broadcasted_iota(jnp.int32, (SB_KEYS, SB_KEYS), 1)
    return diff, krow, kcol


def _sb_scale(d):
    scale = 1.0 / math.sqrt(d)
    assert math.frexp(scale)[0] == 0.5, "the scale is folded into bf16 queries: it must be a power of two"
    return scale


def _key_rows(j):
    return pl.ds(pl.multiple_of(j * SB_KEYS, SB_KEYS), SB_KEYS)


def _sb_fwd(q, k, v, n_heads, *, side=None, name):
    t, hd = q.shape
    d = hd // n_heads
    hpt = LANES // d
    assert hpt * d == LANES and n_heads % hpt == 0
    tq = _tile(t, SB_QUERIES)
    nq = t // tq
    kpq = tq // SB_KEYS
    scale = _sb_scale(d)

    def body(q_ref, k_ref, v_ref, o_ref, lt_ref, first_ref):
        i = pl.program_id(1)
        diff, krow, kcol = _sb_iota(tq)
        later = (krow > kcol).astype(F32)
        nb = i * kpq
        for hh in range(hpt):
            sl = slice(hh * d, (hh + 1) * d)
            qs = (q_ref[:, sl].astype(F32) * scale).astype(BF16)

            def block(j, carry, valid, qs=qs, sl=sl):
                acc, cl = carry
                rows = _key_rows(j)
                ls, lg = _sb_logits(qs, k_ref[rows, sl], valid)
                cs = _ones_dot(later, lg, ones_left=False, pieces=SB_PIECES)
                att = jnp.exp(ls + (cs + cl))
                if valid is not None:
                    att = jnp.where(valid, att, 0.0)
                acc = acc + _nn(att.astype(BF16), v_ref[rows, sl])
                return acc, cl + (cs[:, 0:1] + lg[:, 0:1])

            carry = (jnp.zeros((tq, d), F32), jnp.zeros((tq, 1), F32))
            for m in range(kpq - 1, -1, -1):
                carry = block(i * kpq + m, carry, diff < -m * SB_KEYS)

            def more(st):
                s, _, cl = st
                return jnp.logical_and(s < nb, jnp.max(cl) > -SB_CUTOFF)

            def step(st, block=block):
                s, acc, cl = st
                acc, cl = block(nb - 1 - s, (acc, cl), None)
                return s + 1, acc, cl

            walked, acc, cl = lax.while_loop(more, step, (jnp.int32(0),) + carry)
            o_ref[:, sl] = acc.astype(o_ref.dtype)
            lt_ref[hh] = cl
            first_ref[pl.program_id(0) * hpt + hh, i] = nb - walked

    qs = pl.BlockSpec((tq, LANES), lambda p, i: (i, p))
    ls = pl.BlockSpec((hpt, tq, 1), lambda p, i: (p, i, 0))
    ks = pl.BlockSpec((t, LANES), lambda p, i: (0, p))
    outs, side_outs = _call(
        body,
        grid=(n_heads // hpt, nq),
        in_specs=[qs, ks, ks],
        out_specs=[qs, ls, pl.BlockSpec(memory_space=pltpu.SMEM)],
        out_shape=[jax.ShapeDtypeStruct((t, hd), BF16), jax.ShapeDtypeStruct((n_heads, t, 1), F32),
                   jax.ShapeDtypeStruct((n_heads, nq), jnp.int32)],
        sem=("arbitrary", "arbitrary"),
        name=name,
        args=(q, k, v),
        side=side,
    )
    return (*outs, side_outs)


def _sb_bwd(q, k, v, lt, first, do, n_heads, *, name):
    t, hd = q.shape
    d = hd // n_heads
    hpt = LANES // d
    tq = _tile(t, SB_QUERIES)
    nq = t // tq
    kpq = tq // SB_KEYS
    scale = _sb_scale(d)
    last = SB_KEYS - 1

    def body(q_ref, k_ref, v_ref, lt_ref, first_ref, do_ref, dq_ref, dk_ref, dv_ref, dk_acc, dv_acc):
        i = pl.program_id(1)

        @pl.when(i == 0)
        def _():
            dk_acc[...] = jnp.zeros_like(dk_acc)
            dv_acc[...] = jnp.zeros_like(dv_acc)

        diff, krow, kcol = _sb_iota(tq)
        upto = (krow <= kcol).astype(F32)
        before = (krow < kcol).astype(F32)
        zero = jnp.zeros((tq, 1), F32)
        nb = i * kpq
        for hh in range(hpt):
            sl = slice(hh * d, (hh + 1) * d)
            qs = (q_ref[:, sl].astype(F32) * scale).astype(BF16)
            do16 = do_ref[:, sl].astype(BF16)
            ltot = lt_ref[hh]

            def block(j, carry, valid, r0=0, qs=qs, do16=do16, ltot=ltot, sl=sl):
                dq, pl_sum, pg_sum = carry
                rows = _key_rows(j)
                kv = k_ref[rows, sl]
                vv = v_ref[rows, sl]
                ls, lg = _sb_logits(qs[r0:], kv, valid)
                pre = _ones_dot(upto, lg, ones_left=False, pieces=SB_PIECES)
                att = jnp.exp(ls + (ltot[r0:] - (pre + pl_sum)))
                if valid is not None:
                    att = jnp.where(valid, att, 0.0)
                g = att * _nt(do16[r0:], vv)
                gpre = _ones_dot(before, g, ones_left=False, pieces=SB_PIECES)
                sig = jnp.exp(ls)
                dz16 = (g - sig * (g + (gpre + pg_sum))).astype(BF16)
                if valid is not None:
                    dz16 = jnp.where(valid, dz16, jnp.zeros_like(dz16))
                dq = dq + _nn(dz16, kv)
                dk_acc[rows, sl] += _tn(dz16, qs[r0:])
                dv_acc[rows, sl] += _tn(att.astype(BF16), do16[r0:])
                return dq, pl_sum + pre[:, last:], pg_sum + (gpre[:, last:] + g[:, last:])

            start = jnp.clip(first_ref[pl.program_id(0) * hpt + hh, i], 0, nb)
            carry = lax.fori_loop(start, nb, lambda j, cr, block=block: block(j, cr, None),
                                  (jnp.zeros((tq, d), F32), zero, zero))
            for m in range(kpq):
                r0 = m * SB_KEYS
                sub = block(nb + m, tuple(a[r0:] for a in carry), diff[r0:] < -r0, r0)
                carry = tuple(jnp.concatenate([a[:r0], s], axis=0) if r0 else s for a, s in zip(carry, sub))
            dq_ref[:, sl] = (carry[0] * scale).astype(dq_ref.dtype)

        @pl.when(i == nq - 1)
        def _():
            dk_ref[...] = dk_acc[...].astype(dk_ref.dtype)
            dv_ref[...] = dv_acc[...].astype(dv_ref.dtype)

    qs = pl.BlockSpec((tq, LANES), lambda p, i: (i, p))
    ls = pl.BlockSpec((hpt, tq, 1), lambda p, i: (p, i, 0))
    ks = pl.BlockSpec((t, LANES), lambda p, i: (0, p))
    full = jax.ShapeDtypeStruct((t, hd), BF16)
    return pl.pallas_call(
        body,
        grid=(n_heads // hpt, nq),
        in_specs=[qs, ks, ks, ls, pl.BlockSpec(memory_space=pltpu.SMEM), qs],
        out_specs=[qs, ks, ks],
        out_shape=[full, full, full],
        scratch_shapes=[pltpu.VMEM((t, LANES), F32), pltpu.VMEM((t, LANES), F32)],
        compiler_params=_cparams(("arbitrary", "arbitrary")),
        name=name,
    )(q, k, v, lt, first, do)


def _row_tile(rows, cols):
    fits = [r for r in range(16, rows + 1, 16) if rows % r == 0 and r * cols * 4 <= ADAM_BLOCK_BYTES]
    return max(fits) if fits else rows


def _sum_leading(x, *, name):
    n, rows, cols = x.shape
    tr = _row_tile(rows, cols)

    def body(x_ref, o_ref):
        acc = x_ref[0].astype(F32)
        for q in range(1, n):
            acc = acc + x_ref[q].astype(F32)
        o_ref[...] = acc

    return pl.pallas_call(
        body,
        grid=(rows // tr,),
        in_specs=[pl.BlockSpec((n, tr, cols), lambda i: (0, i, 0))],
        out_specs=pl.BlockSpec((tr, cols), lambda i: (i, 0)),
        out_shape=jax.ShapeDtypeStruct((rows, cols), F32),
        compiler_params=_cparams(("parallel",)),
        name=name,
    )(x)


def _pair_add(g4h, recv, c, *, out_dtype, name):
    n, _, rows, cols = g4h.shape
    tr = _row_tile(rows, cols)

    def body(c_ref, g_ref, r_ref, o_ref):
        o_ref[...] = (g_ref[...] + r_ref[...]).astype(o_ref.dtype)

    blk = pl.BlockSpec((1, tr, cols), lambda q, i, c_ref: (q, i, 0))
    return pl.pallas_call(
        body,
        grid_spec=pltpu.PrefetchScalarGridSpec(
            num_scalar_prefetch=1,
            grid=(n, rows // tr),
            in_specs=[pl.BlockSpec((1, None, tr, cols), lambda q, i, c_ref: (q, c_ref[0], i, 0)), blk],
            out_specs=blk),
        out_shape=jax.ShapeDtypeStruct((n, rows, cols), out_dtype),
        compiler_params=_cparams(("parallel", "parallel")),
        name=name,
    )(c.reshape(1).astype(jnp.int32), g4h, recv)


ANY = pl.BlockSpec(memory_space=pl.ANY)


def _other_chips(x, y):
    return [(1 - x, y), (x, 1 - y), (1 - x, 1 - y)]


def _gather_chips(shard, *, name):
    def body(x_ref, o_ref, send_sems, recv_sems, local_sem):
        x, y, c = lax.axis_index("x"), lax.axis_index("y"), lax.axis_index("c")
        me = 2 * x + y
        mine = pltpu.make_async_copy(x_ref, o_ref.at[me], local_sem)
        mine.start()
        chips = _other_chips(x, y)
        sends = [pltpu.make_async_remote_copy(src_ref=x_ref, dst_ref=o_ref.at[me], send_sem=send_sems.at[q],
                                              recv_sem=recv_sems.at[q], device_id=(px, py, c), device_id_type=MESH)
                 for q, (px, py) in enumerate(chips)]
        for cp in sends:
            cp.start()
        for q, (px, py) in enumerate(chips):
            pltpu.make_async_remote_copy(src_ref=x_ref, dst_ref=o_ref.at[2 * px + py], send_sem=send_sems.at[q],
                                         recv_sem=recv_sems.at[q], device_id=(px, py, c), device_id_type=MESH).wait_recv()
        for cp in sends:
            cp.wait_send()
        mine.wait()

    return pl.pallas_call(
        body,
        in_specs=[ANY],
        out_specs=ANY,
        out_shape=jax.ShapeDtypeStruct((4,) + shard.shape, shard.dtype),
        scratch_shapes=[pltpu.SemaphoreType.DMA((3,)), pltpu.SemaphoreType.DMA((3,)), pltpu.SemaphoreType.DMA],
        compiler_params=pltpu.CompilerParams(has_side_effects=True),
        name=name,
    )(shard)


def _comm_call(body, ins, out_shapes, n_sems, name):
    n = len(ins)

    def wrapped(*refs):
        body(refs[:n], refs[n:n + len(out_shapes)], refs[-2], refs[-1])

    return pl.pallas_call(
        wrapped,
        in_specs=[ANY] * n,
        out_specs=[ANY] * len(out_shapes),
        out_shape=out_shapes,
        scratch_shapes=[pltpu.SemaphoreType.DMA((n_sems,)), pltpu.SemaphoreType.DMA((n_sems,))],
        compiler_params=pltpu.CompilerParams(has_side_effects=True),
        name=name,
    )(*ins)


def _remote(send_sems, recv_sems, q, src, dst, to):
    return pltpu.make_async_remote_copy(src_ref=src, dst_ref=dst, send_sem=send_sems.at[q], recv_sem=recv_sems.at[q],
                                        device_id=to, device_id_type=MESH)


def _scatter_job(parts):
    def sends(ins, outs, send_sems, recv_sems):
        x, y, c = lax.axis_index("x"), lax.axis_index("y"), lax.axis_index("c")
        return [_remote(send_sems, recv_sems, 3 * i + q, p.at[2 * px + py], o.at[2 * x + y], (px, py, c))
                for i, (p, o) in enumerate(zip(ins, outs)) for q, (px, py) in enumerate(_other_chips(x, y))]

    def start(ins, outs, send_sems, recv_sems):
        for cp in sends(ins, outs, send_sems, recv_sems):
            cp.start()

    def finish(ins, outs, send_sems, recv_sems):
        x, y, c = lax.axis_index("x"), lax.axis_index("y"), lax.axis_index("c")
        for i, (p, o) in enumerate(zip(ins, outs)):
            for q, (px, py) in enumerate(_other_chips(x, y)):
                _remote(send_sems, recv_sems, 3 * i + q, p.at[2 * x + y], o.at[2 * px + py], (px, py, c)).wait_recv()
        for cp in sends(ins, outs, send_sems, recv_sems):
            cp.wait_send()

    return _SideJob(parts, [jax.ShapeDtypeStruct(p.shape, p.dtype) for p in parts], 3 * len(parts), start, finish)


def _run_job(job, name):
    return _comm_call(lambda *refs: (job.start(*refs), job.finish(*refs)), job.ins, job.out_shapes, job.n_sems, name)


def _gather_job(shards):
    def sends(ins, outs, send_sems, recv_sems):
        x, y, c = lax.axis_index("x"), lax.axis_index("y"), lax.axis_index("c")
        return [_remote(send_sems, recv_sems, 6 * i + q, s.at[c], o.at[2 * x + y, c], (px, py, c))
                for i, (s, o) in enumerate(zip(ins, outs)) for q, (px, py) in enumerate(_other_chips(x, y))]

    def start(ins, outs, send_sems, recv_sems):
        for cp in sends(ins, outs, send_sems, recv_sems):
            cp.start()

    def finish(ins, outs, send_sems, recv_sems):
        x, y, c = lax.axis_index("x"), lax.axis_index("y"), lax.axis_index("c")
        sibling = (x, y, 1 - c)
        chips = _other_chips(x, y)
        copy = lambda q, src, dst, to: _remote(send_sems, recv_sems, q, src, dst, to)
        passed = []
        for i, (s, o) in enumerate(zip(ins, outs)):
            for q, (px, py) in enumerate(chips):
                slot = o.at[2 * px + py, c]
                copy(6 * i + q, s.at[c], slot, (px, py, c)).wait_recv()
                passed.append(copy(6 * i + 3 + q, slot, slot, sibling))
                passed[-1].start()
        for i, (s, o) in enumerate(zip(ins, outs)):
            for q, (px, py) in enumerate(chips):
                copy(6 * i + 3 + q, s.at[1 - c], o.at[2 * px + py, 1 - c], sibling).wait_recv()
        for cp in sends(ins, outs, send_sems, recv_sems) + passed:
            cp.wait_send()

    return _SideJob(shards, [jax.ShapeDtypeStruct((N_CHIPS,) + s.shape, s.dtype) for s in shards], 6 * len(shards),
                    start, finish)


def _swap_job(gs):
    def copies(ins, outs, send_sems, recv_sems):
        x, y, c = lax.axis_index("x"), lax.axis_index("y"), lax.axis_index("c")
        return [_remote(send_sems, recv_sems, i, g.at[pl.ds(0, g.shape[0]), 1 - c], o, (x, y, 1 - c))
                for i, (g, o) in enumerate(zip(ins, outs))]

    def start(*refs):
        for cp in copies(*refs):
            cp.start()

    def finish(*refs):
        for cp in copies(*refs):
            cp.wait()

    return _SideJob(gs, [jax.ShapeDtypeStruct((g.shape[0],) + g.shape[2:], g.dtype) for g in gs], len(gs), start, finish)


def _join_halves(halves, *, name):
    def body(ins, outs, send_sems, recv_sems):
        x, y, c = lax.axis_index("x"), lax.axis_index("y"), lax.axis_index("c")
        sibling = (x, y, 1 - c)
        sends = [_remote(send_sems, recv_sems, i, h, o.at[c], sibling) for i, (h, o) in enumerate(zip(ins, outs))]
        for cp in sends:
            cp.start()
        for i, (h, o) in enumerate(zip(ins, outs)):
            _remote(send_sems, recv_sems, i, h, o.at[1 - c], sibling).wait_recv()
        for cp in sends:
            cp.wait_send()

    return _comm_call(body, halves, [jax.ShapeDtypeStruct((2,) + h.shape, h.dtype) for h in halves], len(halves), name)


def _gather_all(v, *, name):
    def body(v_ref, o_ref, send_sems, recv_sems, local_sem):
        x, y, c = lax.axis_index("x"), lax.axis_index("y"), lax.axis_index("c")
        me = 4 * x + 2 * y + c
        mine = pltpu.make_async_copy(v_ref, o_ref.at[me], local_sem)
        mine.start()
        peers = [(x ^ (q >> 2 & 1), y ^ (q >> 1 & 1), c ^ (q & 1)) for q in range(1, 8)]
        sends = [pltpu.make_async_remote_copy(src_ref=v_ref, dst_ref=o_ref.at[me], send_sem=send_sems.at[q],
                                              recv_sem=recv_sems.at[q], device_id=peer, device_id_type=MESH)
                 for q, peer in enumerate(peers)]
        for cp in sends:
            cp.start()
        for q, (px, py, pc) in enumerate(peers):
            pltpu.make_async_remote_copy(src_ref=v_ref, dst_ref=o_ref.at[4 * px + 2 * py + pc], send_sem=send_sems.at[q],
                                         recv_sem=recv_sems.at[q], device_id=(px, py, pc), device_id_type=MESH).wait_recv()
        for cp in sends:
            cp.wait_send()
        mine.wait()

    return pl.pallas_call(
        body,
        in_specs=[ANY],
        out_specs=ANY,
        out_shape=jax.ShapeDtypeStruct((8,) + v.shape, v.dtype),
        scratch_shapes=[pltpu.SemaphoreType.DMA((7,)), pltpu.SemaphoreType.DMA((7,)), pltpu.SemaphoreType.DMA],
        compiler_params=pltpu.CompilerParams(has_side_effects=True),
        name=name,
    )(v)


WEIGHTS = ['ssm_norm_w', 'ssm_in_w', 'ssm_conv_w', 'ssm_conv_b', 'ssm_dt_bias', 'ssm_a_log', 'ssm_d',
           'ssm_gate_norm_w', 'ssm_out_w', 'kv_norm_w', 'w_k', 'w_v', 'attn_norm_w', 'w_q', 'w_o',
           'ffn_norm_w', 'ffn_up_w', 'ffn_conv_w', 'ffn_conv_b', 'ffn_down_w', 'final_norm_w']
SHARD_AXIS = {'ssm_norm_w': 1, 'ssm_in_w': 2, 'ssm_conv_w': 2, 'ssm_conv_b': 1, 'ssm_gate_norm_w': 1,
              'ssm_out_w': 1, 'w_k': 0, 'w_v': 0, 'w_q': 1, 'w_o': 1, 'ffn_up_w': 2, 'ffn_conv_w': 2,
              'ffn_down_w': 1}
BIG = ['ssm_in_w', 'ssm_out_w', 'w_k', 'w_v', 'w_q', 'w_o', 'ffn_up_w', 'ffn_down_w']
SMALL = [n for n in WEIGHTS if n in SHARD_AXIS and n not in BIG]
REPLICATED = [n for n in WEIGHTS if n not in SHARD_AXIS]
STACKED = ['ffn_up_w', 'ffn_down_w']
N_CHIPS = 4


PACK_ROWS = 16


def _piece_rows(n):
    return -(-n // (PACK_ROWS * LANES)) * PACK_ROWS


def _pack(arrs, dtype, row_mult):
    lead = arrs[0].shape[:-1]
    pieces, total = [], 0
    for a in arrs:
        n = a.shape[-1]
        rows = _piece_rows(n)
        a = a.astype(dtype)
        if rows * LANES != n:
            a = jnp.pad(a, [(0, 0)] * len(lead) + [(0, rows * LANES - n)])
        pieces.append(a.reshape(lead + (rows, LANES)))
        total += rows
    extra = -total % row_mult
    if extra:
        pieces.append(jnp.zeros(lead + (extra, LANES), dtype))
    return jnp.concatenate(pieces, axis=len(lead))


def _unpack(buf, shapes):
    lead = buf.shape[:-2]
    out, off = [], 0
    for shp in shapes:
        n = math.prod(shp)
        rows = _piece_rows(n)
        piece = lax.slice_in_dim(buf, off, off + rows, axis=len(lead)).reshape(lead + (rows * LANES,))
        out.append(piece[..., :n].reshape(lead + tuple(shp)))
        off += rows
    return out


def _set_slot(buf, piece, index):
    return lax.dynamic_update_slice_in_dim(buf, piece[None], index, axis=0)


def _from_shards(stacked, axis):
    return jnp.concatenate([stacked[j] for j in range(N_CHIPS)], axis=axis)


def _ffn_fwd(h, norm_w, w_up, conv_w, conv_b, w_down, tag, side=None):
    u = _rmsnorm_fwd(h, norm_w, name=f"ffn{tag}_norm")
    hid = _matmul(u, w_up, name=f"ffn{tag}_up")
    act, side_outs = _conv_glu_fwd(hid, conv_w, conv_b, side=side, name=f"ffn{tag}_glu")
    out = _matmul(act, w_down, add=h, name=f"ffn{tag}_down")
    return out, (u, hid, act), side_outs


def _ffn_bwd(h, saved, dout, norm_w, w_up, conv_w, conv_b, w_down, tag, side=None):
    u, hid, act = saved
    dact = _matmul(dout, w_down, tb=True, name=f"ffn{tag}_down_dx")
    dw_down = _matmul(act, dout, ta=True, name=f"ffn{tag}_down_dw")
    dhid, dwg, dwv, dbg, dbv, side_outs = _conv_glu_bwd(hid, conv_w, conv_b, dact, side=side, name=f"ffn{tag}_glu_bwd")
    du = _matmul(dhid, w_up, tb=True, name=f"ffn{tag}_up_dx")
    dw_up = _matmul(u, dhid, ta=True, out_parts=N_CHIPS, name=f"ffn{tag}_up_dw")
    dh, (dnorm,) = _rmsnorm_bwd(h, [(du, norm_w)], dout, name=f"ffn{tag}_norm_bwd")
    return dh, dict(norm=dnorm[0], up=dw_up, conv_w=jnp.concatenate([dwg, dwv], axis=1),
                    conv_b=jnp.concatenate([dbg, dbv], axis=1)[0], down=dw_down), side_outs


class _Pieces:
    def __init__(self, local):
        self.c = lax.axis_index("c")
        self.chip = 2 * lax.axis_index("x") + lax.axis_index("y")
        self.shape, self.s16 = {}, {}
        for n in BIG:
            blk = local[n]
            layers = [(n, l, blk[l]) for l in range(blk.shape[0])] if n in STACKED else [(n, None, blk.reshape(blk.shape[-2:]))]
            for name, l, p in layers:
                self.shape[name, l] = p.shape
                self.s16[name, l] = p.astype(BF16).reshape(2, p.shape[0] // 2, p.shape[1])

    def gather_job(self, keys):
        return _gather_job([self.s16[k] for k in keys])

    def weights(self, keys, gathered):
        out = []
        for k, g in zip(keys, gathered):
            r, cc = self.shape[k]
            by_chip = _set_slot(g, self.s16[k], self.chip).reshape(N_CHIPS, r, cc)
            if k[0] == 'ssm_in_w':
                by_chip = by_chip.transpose(1, 0, 2).reshape(r, N_CHIPS * cc)
            elif k[0] != 'ffn_up_w':
                by_chip = by_chip.reshape(N_CHIPS * r, cc)
            out.append(by_chip)
        return out

    def by_halves(self, keys, grads):
        gs = []
        for k, g in zip(keys, grads):
            r, cc = self.shape[k]
            if k[0] == 'ssm_in_w':
                g = g.reshape(r, N_CHIPS, cc).transpose(1, 0, 2)
            gs.append(g.reshape(N_CHIPS, 2, r // 2, cc))
        return gs

    def pair_sums(self, gs, recv, tag):
        return [_pair_add(g, rv, self.c, out_dtype=BF16, name=f"rs_pair_add_{tag}{i}") for i, (g, rv) in enumerate(zip(gs, recv))]

    def chip_sums(self, pairs, scattered, tag):
        return [_sum_leading(_set_slot(s, lax.dynamic_index_in_dim(p, self.chip, axis=0, keepdims=False), self.chip),
                             name=f"rs_chip_sum_{tag}{i}") for i, (s, p) in enumerate(zip(scattered, pairs))]

    def shards(self, keys, halves):
        joined = _join_halves(halves, name="rs_half_join")
        return {k: _set_slot(j, h, self.c).reshape(self.shape[k]) for k, h, j in zip(keys, halves, joined)}


def _step(x, target, w, pieces):
    t = x.shape[0]
    g_n, heads = SSM_GROUPS, SSM_HEADS
    r_h = heads // g_n
    di = D_INNER
    zx_cols = di + CONV_DIM
    k_in = [('ssm_in_w', None)]
    k_ffn0 = [('ssm_out_w', None), ('ffn_up_w', 0), ('ffn_down_w', 0)]
    k_qkv = [('w_k', None), ('w_v', None), ('w_q', None)]
    k_late = [('w_o', None), ('ffn_up_w', 1), ('ffn_down_w', 1)]
    (w_in,) = pieces.weights(k_in, _run_job(pieces.gather_job(k_in), "gather_ssm_in"))
    w_zx = w_in[:, :zx_cols]
    w_dt = jnp.pad(w_in[:, zx_cols:], ((0, 0), (0, LANES - heads)))
    conv_w, conv_b = w['ssm_conv_w'][0], w['ssm_conv_b'][0]
    hp = jnp.stack([w['ssm_dt_bias'][0], w['ssm_a_log'][0], w['ssm_d'][0]], axis=0).reshape(3, g_n, r_h)
    hpc, hpr = hp.transpose(1, 0, 2), hp.transpose(1, 2, 0)

    h0 = x
    u0 = _rmsnorm_fwd(h0, w['ssm_norm_w'][0], name="ssm_norm")
    zx = _matmul(u0, w_zx, name="ssm_in_zx")
    dt_raw = _matmul(u0, w_dt, name="ssm_in_dt")[:, :heads]
    dtg = dt_raw.reshape(t, g_n, r_h)
    dtc, dtr = dtg.transpose(1, 0, 2), dtg.transpose(1, 2, 0)
    xbc = _conv_silu_fwd(zx, conv_w, conv_b, x_off=di, name="ssm_conv")
    y, prev, got = _ssd_fwd(xbc, dtc, dtr, hpc, hpr, side=pieces.gather_job(k_ffn0), name="ssd_fwd")
    w_out, w_up0, w_down0 = pieces.weights(k_ffn0, got)
    yn = _gate_norm_fwd(y, zx, w['ssm_gate_norm_w'][0], name="ssm_gate_norm")
    h1 = _matmul(yn, w_out, add=h0, name="ssm_out")
    h2, ffn0, got = _ffn_fwd(h1, w['ffn_norm_w'][0], w_up0, w['ffn_conv_w'][0], w['ffn_conv_b'][0], w_down0, 0,
                             side=pieces.gather_job(k_qkv))
    w_k, w_v, w_q = pieces.weights(k_qkv, got)
    hk = _rmsnorm_fwd(h2, w['kv_norm_w'], name="kv_norm")
    qn = _rmsnorm_fwd(h2, w['attn_norm_w'][0], name="attn_norm")
    k2 = _matmul(hk, w_k, out_dtype=BF16, name="attn_k")
    v2 = _matmul(hk, w_v, out_dtype=BF16, name="attn_v")
    q2 = _matmul(qn, w_q, out_dtype=BF16, name="attn_q")
    o2, lt, first, got = _sb_fwd(q2, k2, v2, SB_HEADS, side=pieces.gather_job(k_late), name="sb_fwd")
    w_o, w_up1, w_down1 = pieces.weights(k_late, got)
    h3 =_matmul(o2, w_o, add=h2, name="attn_o")
    h4, ffn1, _ = _ffn_fwd(h3, w['ffn_norm_w'][1], w_up1, w['ffn_conv_w'][1], w['ffn_conv_b'][1], w_down1, 1)
    loss_p, dh4, d_final = _loss_head(h4, w['final_norm_w'], target, name="loss_head")

    dh3, g1, _ = _ffn_bwd(h3, ffn1, dh4, w['ffn_norm_w'][1], w_up1, w['ffn_conv_w'][1], w['ffn_conv_b'][1], w_down1, 1)
    do2 = _matmul(dh3, w_o, tb=True, out_dtype=BF16, name="attn_o_dx")
    dw_o = _matmul(o2, dh3, ta=True, name="attn_o_dw")
    dq2, dk2, dv2 = _sb_bwd(q2, k2, v2, lt, first, do2, SB_HEADS, name="sb_bwd")
    dqn = _matmul(dq2, w_q, tb=True, name="attn_q_dx")
    dw_q = _matmul(qn, dq2, ta=True, name="attn_q_dw")
    dhk = _matmul(dk2, w_k, tb=True, name="attn_k_dx")
    dhk = _matmul(dv2, w_v, tb=True, add=dhk, name="attn_v_dx")
    dw_k = _matmul(hk, dk2, ta=True, name="attn_k_dw")
    dw_v = _matmul(hk, dv2, ta=True, name="attn_v_dw")
    dh2, (d_attn_norm, d_kv_norm) = _rmsnorm_bwd(h2, [(dqn, w['attn_norm_w'][0]), (dhk, w['kv_norm_w'])], dh3,
                                                 name="attn_norms_bwd")
    gs_late = pieces.by_halves(k_qkv + k_late, [dw_k, dw_v, dw_q, dw_o, g1['up'], g1['down']])
    dh1, g0, recv = _ffn_bwd(h1, ffn0, dh2, w['ffn_norm_w'][0], w_up0, w['ffn_conv_w'][0], w['ffn_conv_b'][0], w_down0, 0,
                             side=_swap_job(gs_late))
    pairs_late = pieces.pair_sums(gs_late, recv, "a")
    dyn = _matmul(dh1, w_out, tb=True, name="ssm_out_dx")
    dw_out = _matmul(yn, dh1, ta=True, name="ssm_out_dw")
    k_done = k_qkv + k_late + k_ffn0
    gs_ffn0 = pieces.by_halves(k_ffn0, [dw_out, g0['up'], g0['down']])
    dy, dz, d_gate, recv = _gate_norm_bwd(y, zx, w['ssm_gate_norm_w'][0], dyn, side=_swap_job(gs_ffn0),
                                          name="ssm_gate_norm_bwd")
    pairs_done = pairs_late + pieces.pair_sums(gs_ffn0, recv, "c")
    dxs, dbm, dcm, ddt_g, hg, scattered_done = _ssd_bwd(xbc, dtc, dtr, hpc, hpr, prev, dy,
                                                        side=_scatter_job(pairs_done), name="ssd_bwd")
    dzx, d_conv_w, d_conv_b = _conv_silu_bwd(zx, conv_w, conv_b, [dxs, dbm, dcm], x_off=di, into=dz, name="ssm_conv_bwd")
    ddt = jnp.pad(ddt_g.transpose(1, 0, 2).reshape(t, heads), ((0, 0), (0, LANES - heads)))
    du0 = _matmul(dzx, w_zx, tb=True, name="ssm_in_zx_dx")
    du0 = _matmul(ddt, w_dt, tb=True, add=du0, name="ssm_in_dt_dx")
    dw_in = jnp.concatenate([_matmul(u0, dzx, ta=True, name="ssm_in_zx_dw"),
                             _matmul(u0, ddt, ta=True, name="ssm_in_dt_dw")[:, :heads]], axis=1)
    dx, (d_ssm_norm,) = _rmsnorm_bwd(h0, [(du0, w['ssm_norm_w'][0])], dh1, name="ssm_norm_bwd")

    gs_in = pieces.by_halves(k_in, [dw_in])
    pairs_in = pieces.pair_sums(gs_in, _run_job(_swap_job(gs_in), "rs_pair_swap_b"), "b")
    scattered_in = _run_job(_scatter_job(pairs_in), "rs_chip_scatter_b")
    halves = pieces.chip_sums(pairs_done, scattered_done, "a") + pieces.chip_sums(pairs_in, scattered_in, "b")
    big_grads = pieces.shards(k_done + k_in, halves)

    hgr = hg.transpose(1, 0, 2).reshape(3, heads)
    grads = {
        'ssm_norm_w': d_ssm_norm, 'ssm_conv_w': d_conv_w[None], 'ssm_conv_b': d_conv_b,
        'ssm_dt_bias': hgr[0:1], 'ssm_a_log': hgr[1:2], 'ssm_d': hgr[2:3], 'ssm_gate_norm_w': d_gate,
        'kv_norm_w': d_kv_norm[0], 'attn_norm_w': d_attn_norm, 'ffn_norm_w': jnp.stack([g0['norm'], g1['norm']]),
        'ffn_conv_w': jnp.stack([g0['conv_w'], g1['conv_w']]), 'ffn_conv_b': jnp.stack([g0['conv_b'], g1['conv_b']]),
        'final_norm_w': d_final[0],
    }
    return loss_p, dx, grads, big_grads


def kernel(x, ssm_norm_w, ssm_in_w, ssm_conv_w, ssm_conv_b, ssm_dt_bias, ssm_a_log, ssm_d, ssm_gate_norm_w, ssm_out_w, kv_norm_w, w_k, w_v, attn_norm_w, w_q, w_o, ffn_norm_w, ffn_up_w, ffn_conv_w, ffn_conv_b, ffn_down_w, final_norm_w, loss_target, m_ssm_norm_w, m_ssm_in_w, m_ssm_conv_w, m_ssm_conv_b, m_ssm_dt_bias, m_ssm_a_log, m_ssm_d, m_ssm_gate_norm_w, m_ssm_out_w, m_kv_norm_w, m_w_k, m_w_v, m_attn_norm_w, m_w_q, m_w_o, m_ffn_norm_w, m_ffn_up_w, m_ffn_conv_w, m_ffn_conv_b, m_ffn_down_w, m_final_norm_w, v_ssm_norm_w, v_ssm_in_w, v_ssm_conv_w, v_ssm_conv_b, v_ssm_dt_bias, v_ssm_a_log, v_ssm_d, v_ssm_gate_norm_w, v_ssm_out_w, v_kv_norm_w, v_w_k, v_w_v, v_attn_norm_w, v_w_q, v_w_o, v_ffn_norm_w, v_ffn_up_w, v_ffn_conv_w, v_ffn_conv_b, v_ffn_down_w, v_final_norm_w):
    args = (ssm_norm_w, ssm_in_w, ssm_conv_w, ssm_conv_b, ssm_dt_bias, ssm_a_log, ssm_d, ssm_gate_norm_w, ssm_out_w, kv_norm_w, w_k, w_v, attn_norm_w, w_q, w_o, ffn_norm_w, ffn_up_w, ffn_conv_w, ffn_conv_b, ffn_down_w, final_norm_w)
    moms = (m_ssm_norm_w, m_ssm_in_w, m_ssm_conv_w, m_ssm_conv_b, m_ssm_dt_bias, m_ssm_a_log, m_ssm_d, m_ssm_gate_norm_w, m_ssm_out_w, m_kv_norm_w, m_w_k, m_w_v, m_attn_norm_w, m_w_q, m_w_o, m_ffn_norm_w, m_ffn_up_w, m_ffn_conv_w, m_ffn_conv_b, m_ffn_down_w, m_final_norm_w)
    vels = (v_ssm_norm_w, v_ssm_in_w, v_ssm_conv_w, v_ssm_conv_b, v_ssm_dt_bias, v_ssm_a_log, v_ssm_d, v_ssm_gate_norm_w, v_ssm_out_w, v_kv_norm_w, v_w_k, v_w_v, v_attn_norm_w, v_w_q, v_w_o, v_ffn_norm_w, v_ffn_up_w, v_ffn_conv_w, v_ffn_conv_b, v_ffn_down_w, v_final_norm_w)
    local = dict(zip(WEIGHTS, args))
    m_in = dict(zip(WEIGHTS, moms))
    v_in = dict(zip(WEIGHTS, vels))
    chip = 2 * lax.axis_index("x") + lax.axis_index("y")

    full = {n: local[n] for n in REPLICATED}
    small32 = _gather_chips(_pack([local[n].reshape(-1) for n in SMALL], F32, 8), name="gather_small")
    for n, st in zip(SMALL, _unpack(small32, [local[n].shape for n in SMALL])):
        full[n] = _from_shards(st, SHARD_AXIS[n])

    pieces = _Pieces(local)
    loss_p, dx, grads, big_grads = _step(x[0], loss_target[0], full, pieces)
    gshard = {}
    for n in BIG:
        if n in STACKED:
            gshard[n] = [big_grads[n, l] for l in range(local[n].shape[0])]
        else:
            gshard[n] = big_grads[n, None].reshape(local[n].shape)

    small = SMALL + REPLICATED
    rep = _pack([loss_p.reshape(-1)] + [grads[n].reshape(-1) for n in small], F32, 8)
    tot = _sum_leading(_gather_all(rep, name="ar_gather"), name="ar_sum")
    parts = _unpack(tot, [(LANES,)] + [grads[n].shape for n in small])
    loss = jnp.sum(parts[0])
    for n, g in zip(small, parts[1:]):
        if n in SHARD_AXIS:
            size = local[n].shape[SHARD_AXIS[n]]
            g = lax.dynamic_slice_in_dim(g, chip * size, size, axis=SHARD_AXIS[n])
        gshard[n] = g

    grads_out, deltas, new_m, new_v = [], [], [], []
    for n in WEIGHTS:
        if n in STACKED:
            g, d, nm, nv = _adamw_layers(local[n], gshard[n], m_in[n], v_in[n], name=f"adamw_{n}")
        else:
            g = gshard[n]
            d, nm, nv = _adamw(local[n], g, m_in[n], v_in[n], name=f"adamw_{n}")
        grads_out.append(g)
        deltas.append(d)
        new_m.append(nm)
        new_v.append(nv)
    return (loss, dx[None], *grads_out, *deltas, *new_m, *new_v)
```
